```python
import math
import jax, jax.numpy as jnp
from jax import lax
import numpy as np

D_MODEL = 1024
BATCH = 8
SEQ = 8192
DEPTH = 2

MEM_LEN = 256
N_MIXERS = 4
D_GROUP = D_MODEL // N_MIXERS
N_IN_COLS = 6 * D_GROUP
S5_CH_PER_GROUP = 16
S5_GROUPS = D_GROUP // S5_CH_PER_GROUP
S5_STATE = 64
S5_DT_MIN = 1e-3
S5_DT_MAX = 1e-1
CONV_WIDTH = 31
CONV_GN_GROUPS = 4
LRU_HEADS = 4
LRU_HEAD_DIM = D_GROUP // LRU_HEADS
LRU_CONV_WIDTH = 4
LRU_C = 8.0
ATTN_HEADS = 4
ATTN_HEAD_DIM = D_GROUP // ATTN_HEADS
D_FF = ((8 * D_MODEL // 3 + 127) // 128) * 128
FFN_CONV_WIDTH = 3
DEEPNORM_ALPHA = (2 * DEPTH) ** 0.25
DEEPNORM_BETA = (8 * DEPTH) ** -0.25
LN_EPS = 1e-5

kernel_name = "hymba_style_s5_conformer_rglru_memxattn_deepnorm"

F32 = jnp.float32


def layer_norm(x, g, b):
    xf = x.astype(F32)
    mu = jnp.mean(xf, -1, keepdims=True)
    var = jnp.mean(jnp.square(xf - mu), -1, keepdims=True)
    return ((xf - mu) * lax.rsqrt(var + LN_EPS) * g.astype(F32) + b.astype(F32)).astype(x.dtype)


def group_norm(x, g, b, groups):
    lead = x.shape[:-1]
    c = x.shape[-1]
    xf = x.astype(F32).reshape(lead + (groups, c // groups))
    mu = jnp.mean(xf, -1, keepdims=True)
    var = jnp.mean(jnp.square(xf - mu), -1, keepdims=True)
    xn = ((xf - mu) * lax.rsqrt(var + LN_EPS)).reshape(lead + (c,))
    return xn * g.astype(F32) + b.astype(F32)


def causal_dwconv(x, w, b):
    k, c = w.shape
    y = lax.conv_general_dilated(
        x, w.astype(x.dtype)[:, None, :], window_strides=(1,), padding=[(k - 1, 0)],
        dimension_numbers=("NWC", "WIO", "NWC"), feature_group_count=c)
    return y + b.astype(x.dtype)


def linear_scan(a, b):
    def combine(l, r):
        a1, b1 = l
        a2, b2 = r
        return a1 * a2, a2 * b1 + b2
    _, h = lax.associative_scan(combine, (a, b), axis=1)
    return h


def s5_mixer(u, lam_re, lam_im, log_dt, b_re, b_im, c_re, c_im, d_skip, w_glu, b_glu):
    lead = u.shape[:-1]
    uf = u.astype(F32).reshape(lead + (S5_GROUPS, S5_CH_PER_GROUP))
    lam = lax.complex(lam_re.astype(F32), lam_im.astype(F32))
    dt = jnp.exp(log_dt.astype(F32))[:, None]
    lam_bar = jnp.exp(lam * dt)
    bmat = lax.complex(b_re.astype(F32), b_im.astype(F32))
    b_bar = ((lam_bar - 1.0) / lam)[..., None] * bmat
    bu = jnp.einsum("bsgc,gpc->bsgp", uf, b_bar)
    states = linear_scan(jnp.broadcast_to(lam_bar, bu.shape), bu)
    cmat = lax.complex(c_re.astype(F32), c_im.astype(F32))
    y = jnp.einsum("bsgp,gcp->bsgc", states, cmat).real
    y = y + d_skip.astype(F32).reshape(S5_GROUPS, S5_CH_PER_GROUP) * uf
    y = jax.nn.gelu(y.reshape(lead + (D_GROUP,)))
    return y * jax.nn.sigmoid(y @ w_glu.astype(F32) + b_glu.astype(F32))


def conformer_conv(v, g, conv_w, conv_b, gn_g, gn_b, w_pw, b_pw):
    h = v * jax.nn.sigmoid(g)
    h = causal_dwconv(h, conv_w, conv_b)
    h = jax.nn.silu(group_norm(h, gn_g, gn_b, CONV_GN_GROUPS))
    return h @ w_pw.astype(F32) + b_pw.astype(F32)


def rglru_branch(xg, xr, conv_w, conv_b, w_r, b_r, w_i, b_i, lam):
    gate = jax.nn.gelu(xg.astype(F32))
    xc = causal_dwconv(xr, conv_w, conv_b).astype(F32)
    lead = xc.shape[:-1]
    xh = xc.reshape(lead + (LRU_HEADS, LRU_HEAD_DIM))
    r = jax.nn.sigmoid(jnp.einsum("bshi,hij->bshj", xh, w_r.astype(F32)).reshape(lead + (D_GROUP,)) + b_r.astype(F32))
    i = jax.nn.sigmoid(jnp.einsum("bshi,hij->bshj", xh, w_i.astype(F32)).reshape(lead + (D_GROUP,)) + b_i.astype(F32))
    log_a = -LRU_C * r * jax.nn.softplus(-lam.astype(F32))
    a = jnp.exp(log_a)
    bvals = jnp.sqrt(-jnp.expm1(2.0 * log_a)) * (i * xc)
    h = linear_scan(a, bvals)
    return h * gate


def memory_cross_attention(q, mem, w_kv):
    lead = q.shape[:-1]
    qh = q.astype(F32).reshape(lead + (ATTN_HEADS, ATTN_HEAD_DIM))
    kv = mem.astype(F32) @ w_kv.astype(F32)
    k, v = jnp.split(kv, 2, axis=-1)
    k = k.reshape(k.shape[:-1] + (ATTN_HEADS, ATTN_HEAD_DIM))
    v = v.reshape(v.shape[:-1] + (ATTN_HEADS, ATTN_HEAD_DIM))
    s = jnp.einsum("bshd,bmhd->bhsm", qh, k) * (ATTN_HEAD_DIM ** -0.5)
    p = jax.nn.softmax(s, axis=-1)
    o = jnp.einsum("bhsm,bmhd->bshd", p, v)
    return o.reshape(lead + (D_GROUP,))


def conv_ffn(x, w_up, conv_w, conv_b, w_down):
    u = x @ w_up
    u = causal_dwconv(u, conv_w, conv_b)
    val, gt = jnp.split(u, 2, axis=-1)
    h = val.astype(F32) * jax.nn.gelu(gt.astype(F32))
    return h @ w_down.astype(F32)


def _fwd_setup_inputs(seed: int = 0) -> dict:
    key = jax.random.key(seed)
    keys = iter(jax.random.split(key, 64))
    L = DEPTH

    def nrm(shape, scale):
        return jax.random.normal(next(keys), shape, F32) * scale

    def gain(shape):
        return 1.0 + nrm(shape, 0.02)

    d = {}
    d["x"] = nrm((BATCH, SEQ, D_MODEL), 1.0)
    d["mem"] = nrm((BATCH, MEM_LEN, D_MODEL), 1.0)
    d["ln_in_g"] = gain((D_MODEL,))
    d["ln_in_b"] = nrm((D_MODEL,), 0.02)
    d["w_in"] = nrm((L, D_MODEL, N_IN_COLS), D_MODEL ** -0.5)
    d["b_in"] = nrm((L, N_IN_COLS), 0.01)
    d["s5_lam_re"] = -0.5 + nrm((L, S5_GROUPS, S5_STATE), 0.01)
    d["s5_lam_im"] = math.pi * jnp.arange(S5_STATE, dtype=F32) + nrm((L, S5_GROUPS, S5_STATE), 0.01)
    d["s5_log_dt"] = jax.random.uniform(next(keys), (L, S5_GROUPS), F32, math.log(S5_DT_MIN), math.log(S5_DT_MAX))
    d["s5_b_re"] = nrm((L, S5_GROUPS, S5_STATE, S5_CH_PER_GROUP), (2 * S5_CH_PER_GROUP) ** -0.5)
    d["s5_b_im"] = nrm((L, S5_GROUPS, S5_STATE, S5_CH_PER_GROUP), (2 * S5_CH_PER_GROUP) ** -0.5)
    d["s5_c_re"] = nrm((L, S5_GROUPS, S5_CH_PER_GROUP, S5_STATE), (2 * S5_STATE) ** -0.5)
    d["s5_c_im"] = nrm((L, S5_GROUPS, S5_CH_PER_GROUP, S5_STATE), (2 * S5_STATE) ** -0.5)
    d["s5_d"] = nrm((L, D_GROUP), 1.0)
    d["s5_w_glu"] = nrm((L, D_GROUP, D_GROUP), D_GROUP ** -0.5)
    d["s5_b_glu"] = nrm((L, D_GROUP), 0.01)
    d["cv_w"] = nrm((L, CONV_WIDTH, D_GROUP), CONV_WIDTH ** -0.5)
    d["cv_b"] = nrm((L, D_GROUP), 0.01)
    d["cv_gn_g"] = gain((L, D_GROUP))
    d["cv_gn_b"] = nrm((L, D_GROUP), 0.02)
    d["cv_w_pw"] = nrm((L, D_GROUP, D_GROUP), D_GROUP ** -0.5)
    d["cv_b_pw"] = nrm((L, D_GROUP), 0.01)
    d["lru_conv_w"] = nrm((L, LRU_CONV_WIDTH, D_GROUP), LRU_CONV_WIDTH ** -0.5)
    d["lru_conv_b"] = nrm((L, D_GROUP), 0.01)
    d["lru_w_r"] = nrm((L, LRU_HEADS, LRU_HEAD_DIM, LRU_HEAD_DIM), LRU_HEAD_DIM ** -0.5)
    d["lru_b_r"] = nrm((L, D_GROUP), 0.01)
    d["lru_w_i"] = nrm((L, LRU_HEADS, LRU_HEAD_DIM, LRU_HEAD_DIM), LRU_HEAD_DIM ** -0.5)
    d["lru_b_i"] = nrm((L, D_GROUP), 0.01)
    a_c = jax.random.uniform(next(keys), (L, D_GROUP), F32, 0.9, 0.999)
    a0 = a_c ** (1.0 / LRU_C)
    d["lru_lam"] = jnp.log(a0) - jnp.log1p(-a0)
    d["attn_w_kv"] = nrm((L, D_MODEL, 2 * D_GROUP), D_MODEL ** -0.5)
    d["w_out"] = nrm((L, D_MODEL, D_MODEL), D_MODEL ** -0.5 * DEEPNORM_BETA)
    d["b_out"] = nrm((L, D_MODEL), 0.01)
    d["ln1_g"] = gain((L, D_MODEL))
    d["ln1_b"] = nrm((L, D_MODEL), 0.02)
    d["ffn_w_up"] = nrm((L, D_MODEL, 2 * D_FF), D_MODEL ** -0.5)
    d["ffn_conv_w"] = nrm((L, FFN_CONV_WIDTH, 2 * D_FF), FFN_CONV_WIDTH ** -0.5)
    d["ffn_conv_b"] = nrm((L, 2 * D_FF), 0.01)
    d["ffn_w_down"] = nrm((L, D_FF, D_MODEL), D_FF ** -0.5 * DEEPNORM_BETA)
    d["ln2_g"] = gain((L, D_MODEL))
    d["ln2_b"] = nrm((L, D_MODEL), 0.02)
    return d


def _fwd_reference(x, mem, ln_in_g, ln_in_b, w_in, b_in,
              s5_lam_re, s5_lam_im, s5_log_dt, s5_b_re, s5_b_im, s5_c_re, s5_c_im, s5_d, s5_w_glu, s5_b_glu,
              cv_w, cv_b, cv_gn_g, cv_gn_b, cv_w_pw, cv_b_pw,
              lru_conv_w, lru_conv_b, lru_w_r, lru_b_r, lru_w_i, lru_b_i, lru_lam,
              attn_w_kv, w_out, b_out, ln1_g, ln1_b,
              ffn_w_up, ffn_conv_w, ffn_conv_b, ffn_w_down, ln2_g, ln2_b):
    x = layer_norm(x, ln_in_g, ln_in_b)
    for l in range(DEPTH):
        h = x @ w_in[l] + b_in[l]
        s5_u, cv_v, cv_g, lru_g, lru_x, q = jnp.split(h, 6, axis=-1)
        y_s5 = s5_mixer(s5_u, s5_lam_re[l], s5_lam_im[l], s5_log_dt[l], s5_b_re[l], s5_b_im[l],
                        s5_c_re[l], s5_c_im[l], s5_d[l], s5_w_glu[l], s5_b_glu[l])
        y_cv = conformer_conv(cv_v, cv_g, cv_w[l], cv_b[l], cv_gn_g[l], cv_gn_b[l], cv_w_pw[l], cv_b_pw[l])
        y_lru = rglru_branch(lru_g, lru_x, lru_conv_w[l], lru_conv_b[l], lru_w_r[l], lru_b_r[l],
                             lru_w_i[l], lru_b_i[l], lru_lam[l])
        y_mem = memory_cross_attention(q, mem, attn_w_kv[l])
        mix = jnp.concatenate([y_s5, y_cv, y_lru, y_mem], axis=-1)
        mix = mix @ w_out[l].astype(F32) + b_out[l].astype(F32)
        x = layer_norm(DEEPNORM_ALPHA * x + mix, ln1_g[l], ln1_b[l])
        f = conv_ffn(x, ffn_w_up[l], ffn_conv_w[l], ffn_conv_b[l], ffn_w_down[l])
        x = layer_norm(DEEPNORM_ALPHA * x + f, ln2_g[l], ln2_b[l])
    return x


import jax as _jax
import jax.numpy as _jnp

TWIN_FORMAT = 'train_step'
FWD_PARAMS = ['x', 'mem', 'ln_in_g', 'ln_in_b', 'w_in', 'b_in', 's5_lam_re', 's5_lam_im', 's5_log_dt', 's5_b_re', 's5_b_im', 's5_c_re', 's5_c_im', 's5_d', 's5_w_glu', 's5_b_glu', 'cv_w', 'cv_b', 'cv_gn_g', 'cv_gn_b', 'cv_w_pw', 'cv_b_pw', 'lru_conv_w', 'lru_conv_b', 'lru_w_r', 'lru_b_r', 'lru_w_i', 'lru_b_i', 'lru_lam', 'attn_w_kv', 'w_out', 'b_out', 'ln1_g', 'ln1_b', 'ffn_w_up', 'ffn_conv_w', 'ffn_conv_b', 'ffn_w_down', 'ln2_g', 'ln2_b']
TWIN_WEIGHTS = ['ln_in_g', 'ln_in_b', 'w_in', 'b_in', 's5_lam_re', 's5_lam_im', 's5_log_dt', 's5_b_re', 's5_b_im', 's5_c_re', 's5_c_im', 's5_d', 's5_w_glu', 's5_b_glu', 'cv_w', 'cv_b', 'cv_gn_g', 'cv_gn_b', 'cv_w_pw', 'cv_b_pw', 'lru_conv_w', 'lru_conv_b', 'lru_w_r', 'lru_b_r', 'lru_w_i', 'lru_b_i', 'lru_lam', 'attn_w_kv', 'w_out', 'b_out', 'ln1_g', 'ln1_b', 'ffn_w_up', 'ffn_conv_w', 'ffn_conv_b', 'ffn_w_down', 'ln2_g', 'ln2_b']
TWIN_DIFF_INPUT = 'x'
TWIN_INPUTS = ['x', 'mem', 'ln_in_g', 'ln_in_b', 'w_in', 'b_in', 's5_lam_re', 's5_lam_im', 's5_log_dt', 's5_b_re', 's5_b_im', 's5_c_re', 's5_c_im', 's5_d', 's5_w_glu', 's5_b_glu', 'cv_w', 'cv_b', 'cv_gn_g', 'cv_gn_b', 'cv_w_pw', 'cv_b_pw', 'lru_conv_w', 'lru_conv_b', 'lru_w_r', 'lru_b_r', 'lru_w_i', 'lru_b_i', 'lru_lam', 'attn_w_kv', 'w_out', 'b_out', 'ln1_g', 'ln1_b', 'ffn_w_up', 'ffn_conv_w', 'ffn_conv_b', 'ffn_w_down', 'ln2_g', 'ln2_b', 'loss_target', 'm_ln_in_g', 'm_ln_in_b', 'm_w_in', 'm_b_in', 'm_s5_lam_re', 'm_s5_lam_im', 'm_s5_log_dt', 'm_s5_b_re', 'm_s5_b_im', 'm_s5_c_re', 'm_s5_c_im', 'm_s5_d', 'm_s5_w_glu', 'm_s5_b_glu', 'm_cv_w', 'm_cv_b', 'm_cv_gn_g', 'm_cv_gn_b', 'm_cv_w_pw', 'm_cv_b_pw', 'm_lru_conv_w', 'm_lru_conv_b', 'm_lru_w_r', 'm_lru_b_r', 'm_lru_w_i', 'm_lru_b_i', 'm_lru_lam', 'm_attn_w_kv', 'm_w_out', 'm_b_out', 'm_ln1_g', 'm_ln1_b', 'm_ffn_w_up', 'm_ffn_conv_w', 'm_ffn_conv_b', 'm_ffn_w_down', 'm_ln2_g', 'm_ln2_b', 'v_ln_in_g', 'v_ln_in_b', 'v_w_in', 'v_b_in', 'v_s5_lam_re', 'v_s5_lam_im', 'v_s5_log_dt', 'v_s5_b_re', 'v_s5_b_im', 'v_s5_c_re', 'v_s5_c_im', 'v_s5_d', 'v_s5_w_glu', 'v_s5_b_glu', 'v_cv_w', 'v_cv_b', 'v_cv_gn_g', 'v_cv_gn_b', 'v_cv_w_pw', 'v_cv_b_pw', 'v_lru_conv_w', 'v_lru_conv_b', 'v_lru_w_r', 'v_lru_b_r', 'v_lru_w_i', 'v_lru_b_i', 'v_lru_lam', 'v_attn_w_kv', 'v_w_out', 'v_b_out', 'v_ln1_g', 'v_ln1_b', 'v_ffn_w_up', 'v_ffn_conv_w', 'v_ffn_conv_b', 'v_ffn_w_down', 'v_ln2_g', 'v_ln2_b']
TWIN_OUTPUTS = ['loss', 'grad_x', 'grad_ln_in_g', 'grad_ln_in_b', 'grad_w_in', 'grad_b_in', 'grad_s5_lam_re', 'grad_s5_lam_im', 'grad_s5_log_dt', 'grad_s5_b_re', 'grad_s5_b_im', 'grad_s5_c_re', 'grad_s5_c_im', 'grad_s5_d', 'grad_s5_w_glu', 'grad_s5_b_glu', 'grad_cv_w', 'grad_cv_b', 'grad_cv_gn_g', 'grad_cv_gn_b', 'grad_cv_w_pw', 'grad_cv_b_pw', 'grad_lru_conv_w', 'grad_lru_conv_b', 'grad_lru_w_r', 'grad_lru_b_r', 'grad_lru_w_i', 'grad_lru_b_i', 'grad_lru_lam', 'grad_attn_w_kv', 'grad_w_out', 'grad_b_out', 'grad_ln1_g', 'grad_ln1_b', 'grad_ffn_w_up', 'grad_ffn_conv_w', 'grad_ffn_conv_b', 'grad_ffn_w_down', 'grad_ln2_g', 'grad_ln2_b', 'delta_ln_in_g', 'delta_ln_in_b', 'delta_w_in', 'delta_b_in', 'delta_s5_lam_re', 'delta_s5_lam_im', 'delta_s5_log_dt', 'delta_s5_b_re', 'delta_s5_b_im', 'delta_s5_c_re', 'delta_s5_c_im', 'delta_s5_d', 'delta_s5_w_glu', 'delta_s5_b_glu', 'delta_cv_w', 'delta_cv_b', 'delta_cv_gn_g', 'delta_cv_gn_b', 'delta_cv_w_pw', 'delta_cv_b_pw', 'delta_lru_conv_w', 'delta_lru_conv_b', 'delta_lru_w_r', 'delta_lru_b_r', 'delta_lru_w_i', 'delta_lru_b_i', 'delta_lru_lam', 'delta_attn_w_kv', 'delta_w_out', 'delta_b_out', 'delta_ln1_g', 'delta_ln1_b', 'delta_ffn_w_up', 'delta_ffn_conv_w', 'delta_ffn_conv_b', 'delta_ffn_w_down', 'delta_ln2_g', 'delta_ln2_b', 'new_m_ln_in_g', 'new_m_ln_in_b', 'new_m_w_in', 'new_m_b_in', 'new_m_s5_lam_re', 'new_m_s5_lam_im', 'new_m_s5_log_dt', 'new_m_s5_b_re', 'new_m_s5_b_im', 'new_m_s5_c_re', 'new_m_s5_c_im', 'new_m_s5_d', 'new_m_s5_w_glu', 'new_m_s5_b_glu', 'new_m_cv_w', 'new_m_cv_b', 'new_m_cv_gn_g', 'new_m_cv_gn_b', 'new_m_cv_w_pw', 'new_m_cv_b_pw', 'new_m_lru_conv_w', 'new_m_lru_conv_b', 'new_m_lru_w_r', 'new_m_lru_b_r', 'new_m_lru_w_i', 'new_m_lru_b_i', 'new_m_lru_lam', 'new_m_attn_w_kv', 'new_m_w_out', 'new_m_b_out', 'new_m_ln1_g', 'new_m_ln1_b', 'new_m_ffn_w_up', 'new_m_ffn_conv_w', 'new_m_ffn_conv_b', 'new_m_ffn_w_down', 'new_m_ln2_g', 'new_m_ln2_b', 'new_v_ln_in_g', 'new_v_ln_in_b', 'new_v_w_in', 'new_v_b_in', 'new_v_s5_lam_re', 'new_v_s5_lam_im', 'new_v_s5_log_dt', 'new_v_s5_b_re', 'new_v_s5_b_im', 'new_v_s5_c_re', 'new_v_s5_c_im', 'new_v_s5_d', 'new_v_s5_w_glu', 'new_v_s5_b_glu', 'new_v_cv_w', 'new_v_cv_b', 'new_v_cv_gn_g', 'new_v_cv_gn_b', 'new_v_cv_w_pw', 'new_v_cv_b_pw', 'new_v_lru_conv_w', 'new_v_lru_conv_b', 'new_v_lru_w_r', 'new_v_lru_b_r', 'new_v_lru_w_i', 'new_v_lru_b_i', 'new_v_lru_lam', 'new_v_attn_w_kv', 'new_v_w_out', 'new_v_b_out', 'new_v_ln1_g', 'new_v_ln1_b', 'new_v_ffn_w_up', 'new_v_ffn_conv_w', 'new_v_ffn_conv_b', 'new_v_ffn_w_down', 'new_v_ln2_g', 'new_v_ln2_b']
TWIN_LEAF_KINDS = {'loss': 'loss', 'grad_x': 'grad_x', 'grad_ln_in_g': 'grad_w', 'grad_ln_in_b': 'grad_w', 'grad_w_in': 'grad_w', 'grad_b_in': 'grad_w', 'grad_s5_lam_re': 'grad_w', 'grad_s5_lam_im': 'grad_w', 'grad_s5_log_dt': 'grad_w', 'grad_s5_b_re': 'grad_w', 'grad_s5_b_im': 'grad_w', 'grad_s5_c_re': 'grad_w', 'grad_s5_c_im': 'grad_w', 'grad_s5_d': 'grad_w', 'grad_s5_w_glu': 'grad_w', 'grad_s5_b_glu': 'grad_w', 'grad_cv_w': 'grad_w', 'grad_cv_b': 'grad_w', 'grad_cv_gn_g': 'grad_w', 'grad_cv_gn_b': 'grad_w', 'grad_cv_w_pw': 'grad_w', 'grad_cv_b_pw': 'grad_w', 'grad_lru_conv_w': 'grad_w', 'grad_lru_conv_b': 'grad_w', 'grad_lru_w_r': 'grad_w', 'grad_lru_b_r': 'grad_w', 'grad_lru_w_i': 'grad_w', 'grad_lru_b_i': 'grad_w', 'grad_lru_lam': 'grad_w', 'grad_attn_w_kv': 'grad_w', 'grad_w_out': 'grad_w', 'grad_b_out': 'grad_w', 'grad_ln1_g': 'grad_w', 'grad_ln1_b': 'grad_w', 'grad_ffn_w_up': 'grad_w', 'grad_ffn_conv_w': 'grad_w', 'grad_ffn_conv_b': 'grad_w', 'grad_ffn_w_down': 'grad_w', 'grad_ln2_g': 'grad_w', 'grad_ln2_b': 'grad_w', 'delta_ln_in_g': 'delta_w', 'delta_ln_in_b': 'delta_w', 'delta_w_in': 'delta_w', 'delta_b_in': 'delta_w', 'delta_s5_lam_re': 'delta_w', 'delta_s5_lam_im': 'delta_w', 'delta_s5_log_dt': 'delta_w', 'delta_s5_b_re': 'delta_w', 'delta_s5_b_im': 'delta_w', 'delta_s5_c_re': 'delta_w', 'delta_s5_c_im': 'delta_w', 'delta_s5_d': 'delta_w', 'delta_s5_w_glu': 'delta_w', 'delta_s5_b_glu': 'delta_w', 'delta_cv_w': 'delta_w', 'delta_cv_b': 'delta_w', 'delta_cv_gn_g': 'delta_w', 'delta_cv_gn_b': 'delta_w', 'delta_cv_w_pw': 'delta_w', 'delta_cv_b_pw': 'delta_w', 'delta_lru_conv_w': 'delta_w', 'delta_lru_conv_b': 'delta_w', 'delta_lru_w_r': 'delta_w', 'delta_lru_b_r': 'delta_w', 'delta_lru_w_i': 'delta_w', 'delta_lru_b_i': 'delta_w', 'delta_lru_lam': 'delta_w', 'delta_attn_w_kv': 'delta_w', 'delta_w_out': 'delta_w', 'delta_b_out': 'delta_w', 'delta_ln1_g': 'delta_w', 'delta_ln1_b': 'delta_w', 'delta_ffn_w_up': 'delta_w', 'delta_ffn_conv_w': 'delta_w', 'delta_ffn_conv_b': 'delta_w', 'delta_ffn_w_down': 'delta_w', 'delta_ln2_g': 'delta_w', 'delta_ln2_b': 'delta_w', 'new_m_ln_in_g': 'new_m', 'new_m_ln_in_b': 'new_m', 'new_m_w_in': 'new_m', 'new_m_b_in': 'new_m', 'new_m_s5_lam_re': 'new_m', 'new_m_s5_lam_im': 'new_m', 'new_m_s5_log_dt': 'new_m', 'new_m_s5_b_re': 'new_m', 'new_m_s5_b_im': 'new_m', 'new_m_s5_c_re': 'new_m', 'new_m_s5_c_im': 'new_m', 'new_m_s5_d': 'new_m', 'new_m_s5_w_glu': 'new_m', 'new_m_s5_b_glu': 'new_m', 'new_m_cv_w': 'new_m', 'new_m_cv_b': 'new_m', 'new_m_cv_gn_g': 'new_m', 'new_m_cv_gn_b': 'new_m', 'new_m_cv_w_pw': 'new_m', 'new_m_cv_b_pw': 'new_m', 'new_m_lru_conv_w': 'new_m', 'new_m_lru_conv_b': 'new_m', 'new_m_lru_w_r': 'new_m', 'new_m_lru_b_r': 'new_m', 'new_m_lru_w_i': 'new_m', 'new_m_lru_b_i': 'new_m', 'new_m_lru_lam': 'new_m', 'new_m_attn_w_kv': 'new_m', 'new_m_w_out': 'new_m', 'new_m_b_out': 'new_m', 'new_m_ln1_g': 'new_m', 'new_m_ln1_b': 'new_m', 'new_m_ffn_w_up': 'new_m', 'new_m_ffn_conv_w': 'new_m', 'new_m_ffn_conv_b': 'new_m', 'new_m_ffn_w_down': 'new_m', 'new_m_ln2_g': 'new_m', 'new_m_ln2_b': 'new_m', 'new_v_ln_in_g': 'new_v', 'new_v_ln_in_b': 'new_v', 'new_v_w_in': 'new_v', 'new_v_b_in': 'new_v', 'new_v_s5_lam_re': 'new_v', 'new_v_s5_lam_im': 'new_v', 'new_v_s5_log_dt': 'new_v', 'new_v_s5_b_re': 'new_v', 'new_v_s5_b_im': 'new_v', 'new_v_s5_c_re': 'new_v', 'new_v_s5_c_im': 'new_v', 'new_v_s5_d': 'new_v', 'new_v_s5_w_glu': 'new_v', 'new_v_s5_b_glu': 'new_v', 'new_v_cv_w': 'new_v', 'new_v_cv_b': 'new_v', 'new_v_cv_gn_g': 'new_v', 'new_v_cv_gn_b': 'new_v', 'new_v_cv_w_pw': 'new_v', 'new_v_cv_b_pw': 'new_v', 'new_v_lru_conv_w': 'new_v', 'new_v_lru_conv_b': 'new_v', 'new_v_lru_w_r': 'new_v', 'new_v_lru_b_r': 'new_v', 'new_v_lru_w_i': 'new_v', 'new_v_lru_b_i': 'new_v', 'new_v_lru_lam': 'new_v', 'new_v_attn_w_kv': 'new_v', 'new_v_w_out': 'new_v', 'new_v_b_out': 'new_v', 'new_v_ln1_g': 'new_v', 'new_v_ln1_b': 'new_v', 'new_v_ffn_w_up': 'new_v', 'new_v_ffn_conv_w': 'new_v', 'new_v_ffn_conv_b': 'new_v', 'new_v_ffn_w_down': 'new_v', 'new_v_ln2_g': 'new_v', 'new_v_ln2_b': 'new_v'}


def _forward(args):
    return _fwd_reference(*[args[k] for k in FWD_PARAMS])


def _output_shape():
    def fwd():
        inp = _fwd_setup_inputs(0)
        return _fwd_reference(*[inp[k] for k in FWD_PARAMS])
    out = _jax.eval_shape(fwd)
    return out.shape, out.dtype

N_MICROBATCH = 1
ADAM_LR = 0.001
ADAM_B1 = 0.9
ADAM_B2 = 0.999
ADAM_EPS = 1e-08
ADAM_WD = 0.01
ADAM_STEP = 10
PER_EXAMPLE_BATCH_AXIS = {'x': 0, 'mem': 0, 'loss_target': 0}
SHARED_INPUTS = []
_WEIGHT_DTYPES = {'ln_in_g': _jnp.float32, 'ln_in_b': _jnp.float32, 'w_in': _jnp.float32, 'b_in': _jnp.float32, 's5_lam_re': _jnp.float32, 's5_lam_im': _jnp.float32, 's5_log_dt': _jnp.float32, 's5_b_re': _jnp.float32, 's5_b_im': _jnp.float32, 's5_c_re': _jnp.float32, 's5_c_im': _jnp.float32, 's5_d': _jnp.float32, 's5_w_glu': _jnp.float32, 's5_b_glu': _jnp.float32, 'cv_w': _jnp.float32, 'cv_b': _jnp.float32, 'cv_gn_g': _jnp.float32, 'cv_gn_b': _jnp.float32, 'cv_w_pw': _jnp.float32, 'cv_b_pw': _jnp.float32, 'lru_conv_w': _jnp.float32, 'lru_conv_b': _jnp.float32, 'lru_w_r': _jnp.float32, 'lru_b_r': _jnp.float32, 'lru_w_i': _jnp.float32, 'lru_b_i': _jnp.float32, 'lru_lam': _jnp.float32, 'attn_w_kv': _jnp.float32, 'w_out': _jnp.float32, 'b_out': _jnp.float32, 'ln1_g': _jnp.float32, 'ln1_b': _jnp.float32, 'ffn_w_up': _jnp.float32, 'ffn_conv_w': _jnp.float32, 'ffn_conv_b': _jnp.float32, 'ffn_w_down': _jnp.float32, 'ln2_g': _jnp.float32, 'ln2_b': _jnp.float32}
MOMENT_SCALE = {'ln_in_g': 1.920074e+00, 'ln_in_b': 1.059121e+00, 'w_in': 3.976989e-02, 'b_in': 2.891261e-01, 's5_lam_re': 1.637942e-03, 's5_lam_im': 2.021671e-03, 's5_log_dt': 7.782682e-01, 's5_b_re': 1.139071e-03, 's5_b_im': 1.100510e-03, 's5_c_re': 2.213805e-03, 's5_c_im': 2.192344e-03, 's5_d': 6.510063e-02, 's5_w_glu': 1.149018e-02, 's5_b_glu': 2.313895e-02, 'cv_w': 6.000643e-02, 'cv_b': 3.058440e-01, 'cv_gn_g': 1.362937e-01, 'cv_gn_b': 1.869904e-01, 'cv_w_pw': 7.707696e-02, 'cv_b_pw': 3.497615e-01, 'lru_conv_w': 5.433979e-02, 'lru_conv_b': 7.590579e-01, 'lru_w_r': 2.692948e-02, 'lru_b_r': 2.000360e-02, 'lru_w_i': 5.033531e-02, 'lru_b_i': 1.863570e-02, 'lru_lam': 3.493095e-02, 'attn_w_kv': 1.062562e-02, 'w_out': 1.143838e-01, 'b_out': 7.002783e-01, 'ln1_g': 2.070616e+00, 'ln1_b': 1.003067e+00, 'ffn_w_up': 3.512826e-02, 'ffn_conv_w': 3.518715e-02, 'ffn_conv_b': 4.776717e-02, 'ffn_w_down': 1.153116e-01, 'ln2_g': 4.534528e+01, 'ln2_b': 3.528737e+00}


def _to_microbatches(a, axis):
    t = _jnp.moveaxis(a, axis, 0)
    t = t.reshape((N_MICROBATCH, t.shape[0] // N_MICROBATCH) + t.shape[1:])
    return _jnp.moveaxis(t, 1, axis + 1)


def setup_inputs(seed: int = 0) -> dict:
    inp = _fwd_setup_inputs(seed)
    key = _jax.random.fold_in(_jax.random.key(seed), 7919)
    shape, _ = _output_shape()
    out = dict(inp)
    out["loss_target"] = _jax.random.normal(_jax.random.fold_in(key, 0), shape, _jnp.float32)
    for i, name in enumerate(TWIN_WEIGHTS):
        w = inp[name].astype(_jnp.float32)
        if MOMENT_SCALE is None:
            s = _jnp.sqrt(_jnp.mean(_jnp.square(w)) + 1e-30)
        else:
            s = MOMENT_SCALE[name]
        km, kv = _jax.random.split(_jax.random.fold_in(key, i + 1))
        out[name] = w
        out["m_" + name] = s * _jax.random.normal(km, w.shape, _jnp.float32)
        out["v_" + name] = (s * s) * _jax.random.uniform(kv, w.shape, _jnp.float32, 0.5, 1.5)
    if N_MICROBATCH > 1:
        for name, axis in PER_EXAMPLE_BATCH_AXIS.items():
            out[name] = _to_microbatches(out[name], axis)
    return {'x': out['x'], 'mem': out['mem'], 'ln_in_g': out['ln_in_g'], 'ln_in_b': out['ln_in_b'], 'w_in': out['w_in'], 'b_in': out['b_in'], 's5_lam_re': out['s5_lam_re'], 's5_lam_im': out['s5_lam_im'], 's5_log_dt': out['s5_log_dt'], 's5_b_re': out['s5_b_re'], 's5_b_im': out['s5_b_im'], 's5_c_re': out['s5_c_re'], 's5_c_im': out['s5_c_im'], 's5_d': out['s5_d'], 's5_w_glu': out['s5_w_glu'], 's5_b_glu': out['s5_b_glu'], 'cv_w': out['cv_w'], 'cv_b': out['cv_b'], 'cv_gn_g': out['cv_gn_g'], 'cv_gn_b': out['cv_gn_b'], 'cv_w_pw': out['cv_w_pw'], 'cv_b_pw': out['cv_b_pw'], 'lru_conv_w': out['lru_conv_w'], 'lru_conv_b': out['lru_conv_b'], 'lru_w_r': out['lru_w_r'], 'lru_b_r': out['lru_b_r'], 'lru_w_i': out['lru_w_i'], 'lru_b_i': out['lru_b_i'], 'lru_lam': out['lru_lam'], 'attn_w_kv': out['attn_w_kv'], 'w_out': out['w_out'], 'b_out': out['b_out'], 'ln1_g': out['ln1_g'], 'ln1_b': out['ln1_b'], 'ffn_w_up': out['ffn_w_up'], 'ffn_conv_w': out['ffn_conv_w'], 'ffn_conv_b': out['ffn_conv_b'], 'ffn_w_down': out['ffn_w_down'], 'ln2_g': out['ln2_g'], 'ln2_b': out['ln2_b'], 'loss_target': out['loss_target'], 'm_ln_in_g': out['m_ln_in_g'], 'm_ln_in_b': out['m_ln_in_b'], 'm_w_in': out['m_w_in'], 'm_b_in': out['m_b_in'], 'm_s5_lam_re': out['m_s5_lam_re'], 'm_s5_lam_im': out['m_s5_lam_im'], 'm_s5_log_dt': out['m_s5_log_dt'], 'm_s5_b_re': out['m_s5_b_re'], 'm_s5_b_im': out['m_s5_b_im'], 'm_s5_c_re': out['m_s5_c_re'], 'm_s5_c_im': out['m_s5_c_im'], 'm_s5_d': out['m_s5_d'], 'm_s5_w_glu': out['m_s5_w_glu'], 'm_s5_b_glu': out['m_s5_b_glu'], 'm_cv_w': out['m_cv_w'], 'm_cv_b': out['m_cv_b'], 'm_cv_gn_g': out['m_cv_gn_g'], 'm_cv_gn_b': out['m_cv_gn_b'], 'm_cv_w_pw': out['m_cv_w_pw'], 'm_cv_b_pw': out['m_cv_b_pw'], 'm_lru_conv_w': out['m_lru_conv_w'], 'm_lru_conv_b': out['m_lru_conv_b'], 'm_lru_w_r': out['m_lru_w_r'], 'm_lru_b_r': out['m_lru_b_r'], 'm_lru_w_i': out['m_lru_w_i'], 'm_lru_b_i': out['m_lru_b_i'], 'm_lru_lam': out['m_lru_lam'], 'm_attn_w_kv': out['m_attn_w_kv'], 'm_w_out': out['m_w_out'], 'm_b_out': out['m_b_out'], 'm_ln1_g': out['m_ln1_g'], 'm_ln1_b': out['m_ln1_b'], 'm_ffn_w_up': out['m_ffn_w_up'], 'm_ffn_conv_w': out['m_ffn_conv_w'], 'm_ffn_conv_b': out['m_ffn_conv_b'], 'm_ffn_w_down': out['m_ffn_w_down'], 'm_ln2_g': out['m_ln2_g'], 'm_ln2_b': out['m_ln2_b'], 'v_ln_in_g': out['v_ln_in_g'], 'v_ln_in_b': out['v_ln_in_b'], 'v_w_in': out['v_w_in'], 'v_b_in': out['v_b_in'], 'v_s5_lam_re': out['v_s5_lam_re'], 'v_s5_lam_im': out['v_s5_lam_im'], 'v_s5_log_dt': out['v_s5_log_dt'], 'v_s5_b_re': out['v_s5_b_re'], 'v_s5_b_im': out['v_s5_b_im'], 'v_s5_c_re': out['v_s5_c_re'], 'v_s5_c_im': out['v_s5_c_im'], 'v_s5_d': out['v_s5_d'], 'v_s5_w_glu': out['v_s5_w_glu'], 'v_s5_b_glu': out['v_s5_b_glu'], 'v_cv_w': out['v_cv_w'], 'v_cv_b': out['v_cv_b'], 'v_cv_gn_g': out['v_cv_gn_g'], 'v_cv_gn_b': out['v_cv_gn_b'], 'v_cv_w_pw': out['v_cv_w_pw'], 'v_cv_b_pw': out['v_cv_b_pw'], 'v_lru_conv_w': out['v_lru_conv_w'], 'v_lru_conv_b': out['v_lru_conv_b'], 'v_lru_w_r': out['v_lru_w_r'], 'v_lru_b_r': out['v_lru_b_r'], 'v_lru_w_i': out['v_lru_w_i'], 'v_lru_b_i': out['v_lru_b_i'], 'v_lru_lam': out['v_lru_lam'], 'v_attn_w_kv': out['v_attn_w_kv'], 'v_w_out': out['v_w_out'], 'v_b_out': out['v_b_out'], 'v_ln1_g': out['v_ln1_g'], 'v_ln1_b': out['v_ln1_b'], 'v_ffn_w_up': out['v_ffn_w_up'], 'v_ffn_conv_w': out['v_ffn_conv_w'], 'v_ffn_conv_b': out['v_ffn_conv_b'], 'v_ffn_w_down': out['v_ffn_w_down'], 'v_ln2_g': out['v_ln2_g'], 'v_ln2_b': out['v_ln2_b']}


def _loss(weights, diff, rest, loss_target):
    with _jax.named_scope("forward"):
        args = {**rest, TWIN_DIFF_INPUT: diff, **{k: w.astype(_WEIGHT_DTYPES[k]) for k, w in weights.items()}}
        y = _forward(args)
    with _jax.named_scope("loss_head"):
        err = _jnp.square(y.astype(_jnp.float32) - loss_target)
        return 0.5 * _jnp.sum(_jnp.mean(err, axis=-1)) if err.ndim else 0.5 * err


def _adamw(w, g, m, v):
    m = ADAM_B1 * m + (1.0 - ADAM_B1) * g
    v = ADAM_B2 * v + (1.0 - ADAM_B2) * _jnp.square(g)
    m_hat = m / (1.0 - ADAM_B1 ** ADAM_STEP)
    v_hat = v / (1.0 - ADAM_B2 ** ADAM_STEP)
    delta = -ADAM_LR * (m_hat / (_jnp.sqrt(v_hat) + ADAM_EPS) + ADAM_WD * w)
    return delta, m, v


def reference(x, mem, ln_in_g, ln_in_b, w_in, b_in, s5_lam_re, s5_lam_im, s5_log_dt, s5_b_re, s5_b_im, s5_c_re, s5_c_im, s5_d, s5_w_glu, s5_b_glu, cv_w, cv_b, cv_gn_g, cv_gn_b, cv_w_pw, cv_b_pw, lru_conv_w, lru_conv_b, lru_w_r, lru_b_r, lru_w_i, lru_b_i, lru_lam, attn_w_kv, w_out, b_out, ln1_g, ln1_b, ffn_w_up, ffn_conv_w, ffn_conv_b, ffn_w_down, ln2_g, ln2_b, loss_target, m_ln_in_g, m_ln_in_b, m_w_in, m_b_in, m_s5_lam_re, m_s5_lam_im, m_s5_log_dt, m_s5_b_re, m_s5_b_im, m_s5_c_re, m_s5_c_im, m_s5_d, m_s5_w_glu, m_s5_b_glu, m_cv_w, m_cv_b, m_cv_gn_g, m_cv_gn_b, m_cv_w_pw, m_cv_b_pw, m_lru_conv_w, m_lru_conv_b, m_lru_w_r, m_lru_b_r, m_lru_w_i, m_lru_b_i, m_lru_lam, m_attn_w_kv, m_w_out, m_b_out, m_ln1_g, m_ln1_b, m_ffn_w_up, m_ffn_conv_w, m_ffn_conv_b, m_ffn_w_down, m_ln2_g, m_ln2_b, v_ln_in_g, v_ln_in_b, v_w_in, v_b_in, v_s5_lam_re, v_s5_lam_im, v_s5_log_dt, v_s5_b_re, v_s5_b_im, v_s5_c_re, v_s5_c_im, v_s5_d, v_s5_w_glu, v_s5_b_glu, v_cv_w, v_cv_b, v_cv_gn_g, v_cv_gn_b, v_cv_w_pw, v_cv_b_pw, v_lru_conv_w, v_lru_conv_b, v_lru_w_r, v_lru_b_r, v_lru_w_i, v_lru_b_i, v_lru_lam, v_attn_w_kv, v_w_out, v_b_out, v_ln1_g, v_ln1_b, v_ffn_w_up, v_ffn_conv_w, v_ffn_conv_b, v_ffn_w_down, v_ln2_g, v_ln2_b):
    given = dict(x=x, mem=mem, ln_in_g=ln_in_g, ln_in_b=ln_in_b, w_in=w_in, b_in=b_in, s5_lam_re=s5_lam_re, s5_lam_im=s5_lam_im, s5_log_dt=s5_log_dt, s5_b_re=s5_b_re, s5_b_im=s5_b_im, s5_c_re=s5_c_re, s5_c_im=s5_c_im, s5_d=s5_d, s5_w_glu=s5_w_glu, s5_b_glu=s5_b_glu, cv_w=cv_w, cv_b=cv_b, cv_gn_g=cv_gn_g, cv_gn_b=cv_gn_b, cv_w_pw=cv_w_pw, cv_b_pw=cv_b_pw, lru_conv_w=lru_conv_w, lru_conv_b=lru_conv_b, lru_w_r=lru_w_r, lru_b_r=lru_b_r, lru_w_i=lru_w_i, lru_b_i=lru_b_i, lru_lam=lru_lam, attn_w_kv=attn_w_kv, w_out=w_out, b_out=b_out, ln1_g=ln1_g, ln1_b=ln1_b, ffn_w_up=ffn_w_up, ffn_conv_w=ffn_conv_w, ffn_conv_b=ffn_conv_b, ffn_w_down=ffn_w_down, ln2_g=ln2_g, ln2_b=ln2_b, loss_target=loss_target, m_ln_in_g=m_ln_in_g, m_ln_in_b=m_ln_in_b, m_w_in=m_w_in, m_b_in=m_b_in, m_s5_lam_re=m_s5_lam_re, m_s5_lam_im=m_s5_lam_im, m_s5_log_dt=m_s5_log_dt, m_s5_b_re=m_s5_b_re, m_s5_b_im=m_s5_b_im, m_s5_c_re=m_s5_c_re, m_s5_c_im=m_s5_c_im, m_s5_d=m_s5_d, m_s5_w_glu=m_s5_w_glu, m_s5_b_glu=m_s5_b_glu, m_cv_w=m_cv_w, m_cv_b=m_cv_b, m_cv_gn_g=m_cv_gn_g, m_cv_gn_b=m_cv_gn_b, m_cv_w_pw=m_cv_w_pw, m_cv_b_pw=m_cv_b_pw, m_lru_conv_w=m_lru_conv_w, m_lru_conv_b=m_lru_conv_b, m_lru_w_r=m_lru_w_r, m_lru_b_r=m_lru_b_r, m_lru_w_i=m_lru_w_i, m_lru_b_i=m_lru_b_i, m_lru_lam=m_lru_lam, m_attn_w_kv=m_attn_w_kv, m_w_out=m_w_out, m_b_out=m_b_out, m_ln1_g=m_ln1_g, m_ln1_b=m_ln1_b, m_ffn_w_up=m_ffn_w_up, m_ffn_conv_w=m_ffn_conv_w, m_ffn_conv_b=m_ffn_conv_b, m_ffn_w_down=m_ffn_w_down, m_ln2_g=m_ln2_g, m_ln2_b=m_ln2_b, v_ln_in_g=v_ln_in_g, v_ln_in_b=v_ln_in_b, v_w_in=v_w_in, v_b_in=v_b_in, v_s5_lam_re=v_s5_lam_re, v_s5_lam_im=v_s5_lam_im, v_s5_log_dt=v_s5_log_dt, v_s5_b_re=v_s5_b_re, v_s5_b_im=v_s5_b_im, v_s5_c_re=v_s5_c_re, v_s5_c_im=v_s5_c_im, v_s5_d=v_s5_d, v_s5_w_glu=v_s5_w_glu, v_s5_b_glu=v_s5_b_glu, v_cv_w=v_cv_w, v_cv_b=v_cv_b, v_cv_gn_g=v_cv_gn_g, v_cv_gn_b=v_cv_gn_b, v_cv_w_pw=v_cv_w_pw, v_cv_b_pw=v_cv_b_pw, v_lru_conv_w=v_lru_conv_w, v_lru_conv_b=v_lru_conv_b, v_lru_w_r=v_lru_w_r, v_lru_b_r=v_lru_b_r, v_lru_w_i=v_lru_w_i, v_lru_b_i=v_lru_b_i, v_lru_lam=v_lru_lam, v_attn_w_kv=v_attn_w_kv, v_w_out=v_w_out, v_b_out=v_b_out, v_ln1_g=v_ln1_g, v_ln1_b=v_ln1_b, v_ffn_w_up=v_ffn_w_up, v_ffn_conv_w=v_ffn_conv_w, v_ffn_conv_b=v_ffn_conv_b, v_ffn_w_down=v_ffn_w_down, v_ln2_g=v_ln2_g, v_ln2_b=v_ln2_b)
    weights = {n: given[n] for n in TWIN_WEIGHTS}
    shared = {n: given[n] for n in SHARED_INPUTS}
    per_example = {n: given[n] for n in ['x', 'mem']}
    grad_fn = _jax.value_and_grad(_loss, argnums=(0, 1))

    def one_microbatch(ex, loss_target):
        ex = dict(ex)
        diff = ex.pop(TWIN_DIFF_INPUT)
        return grad_fn(weights, diff, {**shared, **ex}, loss_target)

    if N_MICROBATCH == 1:
        loss, (grad_w, grad_x) = one_microbatch(per_example, given["loss_target"])
    else:
        def body(carry, xs):
            loss_sum, grad_sum = carry
            l_k, (gw_k, gx_k) = one_microbatch(xs[0], xs[1])
            with _jax.named_scope("update"):
                return (loss_sum + l_k, _jax.tree.map(_jnp.add, grad_sum, gw_k)), gx_k

        init = (_jnp.zeros((), _jnp.float32), _jax.tree.map(_jnp.zeros_like, weights))
        (loss, grad_w), grad_x = _jax.lax.scan(body, init, (per_example, given["loss_target"]))
    with _jax.named_scope("update"):
        delta_w, new_m, new_v = {}, {}, {}
        for n in TWIN_WEIGHTS:
            delta_w[n], new_m[n], new_v[n] = _adamw(weights[n], grad_w[n], given["m_" + n], given["v_" + n])
    return (loss, grad_x, *[grad_w[n] for n in TWIN_WEIGHTS], *[delta_w[n] for n in TWIN_WEIGHTS],
            *[new_m[n] for n in TWIN_WEIGHTS], *[new_v[n] for n in TWIN_WEIGHTS])
```

```python
import functools
import math

import jax
import jax.numpy as jnp
from jax import lax
from jax.experimental import pallas as pl
from jax.experimental.pallas import tpu as pltpu

F32 = jnp.float32
BF16 = jnp.bfloat16

D_MODEL = 1024
DEPTH = 2
D_GROUP = 256
N_IN_COLS = 6 * D_GROUP
S5_GROUPS = 16
S5_CH = 16
S5_STATE = 64
S5_LANES = S5_GROUPS * S5_STATE
CONV_WIDTH = 31
GN_GROUPS = 4
LRU_HEADS = 4
LRU_CONV_WIDTH = 4
LRU_C = 8.0
ATTN_HEADS = 4
ATTN_HEAD_DIM = 64
MEM_LEN = 256
D_FF = 2816
FFN_CONV_WIDTH = 3
ALPHA = (2 * DEPTH) ** 0.25
LN_EPS = 1e-5
ADAM_LR, ADAM_B1, ADAM_B2, ADAM_EPS, ADAM_WD, ADAM_STEP = 0.001, 0.9, 0.999, 1e-08, 0.01, 10

N_DEV = 8
LANE = 128
SUBLANE = 8
VMEM_LIMIT = 56 * 1024 * 1024
PACK_COLS = 1024
PACK_ROW_BLOCK = 256

SHARDED = {
    "w_in": 2, "s5_w_glu": 1, "cv_w": 2, "cv_w_pw": 1, "lru_conv_w": 2, "attn_w_kv": 1,
    "w_out": 1, "ffn_w_up": 2, "ffn_conv_w": 2, "ffn_w_down": 1,
}
MATMUL_WEIGHTS = ("w_in", "s5_w_glu", "cv_w_pw", "attn_w_kv", "w_out", "ffn_w_up", "ffn_w_down")
TAP_WEIGHTS = ("cv_w", "lru_conv_w", "ffn_conv_w")
WEIGHTS = ['ln_in_g', 'ln_in_b', 'w_in', 'b_in', 's5_lam_re', 's5_lam_im', 's5_log_dt', 's5_b_re', 's5_b_im',
           's5_c_re', 's5_c_im', 's5_d', 's5_w_glu', 's5_b_glu', 'cv_w', 'cv_b', 'cv_gn_g', 'cv_gn_b', 'cv_w_pw',
           'cv_b_pw', 'lru_conv_w', 'lru_conv_b', 'lru_w_r', 'lru_b_r', 'lru_w_i', 'lru_b_i', 'lru_lam',
           'attn_w_kv', 'w_out', 'b_out', 'ln1_g', 'ln1_b', 'ffn_w_up', 'ffn_conv_w', 'ffn_conv_b', 'ffn_w_down',
           'ln2_g', 'ln2_b']
REPLICATED = [n for n in WEIGHTS if n not in SHARDED]
SHARDED_ORDER = [n for n in WEIGHTS if n in SHARDED]


def _cparams(n_axes):
    return pltpu.CompilerParams(dimension_semantics=("arbitrary",) * n_axes, vmem_limit_bytes=VMEM_LIMIT)


def _pick(n, cap):
    if n <= cap:
        return n
    best = None
    for t in range(LANE, cap + 1, LANE):
        if n % t == 0:
            best = t
    assert best is not None, (n, cap)
    return best


def _full_spec(arr):
    nd = arr.ndim
    return pl.BlockSpec(arr.shape, lambda *_: (0,) * nd)


def _dot(a, b):
    return lax.dot_general(a.astype(BF16), b.astype(BF16), (((1,), (0,)), ((), ())), preferred_element_type=F32)


def _dot_nt(a, b):
    return lax.dot_general(a.astype(BF16), b.astype(BF16), (((1,), (1,)), ((), ())), preferred_element_type=F32)


def _dot_tn(a, b):
    return lax.dot_general(a.astype(BF16), b.astype(BF16), (((0,), (0,)), ((), ())), preferred_element_type=F32)


def _dot_hi(a, b):
    return jnp.dot(a, b, precision=lax.Precision.HIGHEST, preferred_element_type=F32)


def _colsum(x):
    return jnp.sum(x, axis=0, keepdims=True)


def _sigmoid(x):
    return 1.0 / (1.0 + jnp.exp(-x))


_GELU_K = math.sqrt(2.0 / math.pi)
_GELU_C = 0.044715


def _gelu(x):
    t = jnp.tanh(_GELU_K * (x + _GELU_C * x * x * x))
    return 0.5 * x * (1.0 + t)


def _gelu_and_grad(x):
    x2 = x * x
    t = jnp.tanh(_GELU_K * (x + _GELU_C * x2 * x))
    g = 0.5 * x * (1.0 + t)
    dg = 0.5 * (1.0 + t) + 0.5 * x * (1.0 - t * t) * (_GELU_K * (1.0 + 3.0 * _GELU_C * x2))
    return g, dg


def _neg_expm1(x):
    series = x * (1.0 + x * (0.5 + x * (1.0 / 6.0 + x * (1.0 / 24.0 + x * (1.0 / 120.0)))))
    return -jnp.where(jnp.abs(x) < 0.1, series, jnp.exp(x) - 1.0)


def _seq_tile(s, want):
    t = min(s, want)
    assert s % t == 0
    return t


def _mm(a, b, *, bias=None, res=None, res_scale=1.0, trans_b=False, out_dtype=F32, name):
    m, kdim = a.shape
    n = b.shape[0] if trans_b else b.shape[1]
    tm = _seq_tile(m, 512)
    tn = _pick(n, 1408)
    tk = _pick(kdim, 1536)
    nk = kdim // tk
    has_bias, has_res = bias is not None, res is not None

    def body(*refs):
        a_ref, b_ref = refs[0], refs[1]
        pos = 2
        bias_ref = res_ref = None
        if has_bias:
            bias_ref = refs[pos]
            pos += 1
        if has_res:
            res_ref = refs[pos]
            pos += 1
        o_ref, acc_ref = refs[pos], refs[pos + 1]
        k = pl.program_id(2)

        @pl.when(k == 0)
        def _():
            acc_ref[...] = jnp.zeros_like(acc_ref)

        if trans_b:
            acc_ref[...] += _dot_nt(a_ref[...], b_ref[...])
        else:
            acc_ref[...] += _dot(a_ref[...], b_ref[...])

        @pl.when(k == nk - 1)
        def _():
            r = acc_ref[...]
            if has_bias:
                r = r + bias_ref[...]
            if has_res:
                r = r + res_scale * res_ref[...]
            o_ref[...] = r.astype(out_dtype)

    ins = [a, b]
    in_specs = [pl.BlockSpec((tm, tk), lambda i, j, k: (i, k)),
                pl.BlockSpec((tn, tk), lambda i, j, k: (j, k)) if trans_b
                else pl.BlockSpec((tk, tn), lambda i, j, k: (k, j))]
    if has_bias:
        ins.append(bias)
        in_specs.append(pl.BlockSpec((1, tn), lambda i, j, k: (0, j)))
    if has_res:
        ins.append(res)
        in_specs.append(pl.BlockSpec((tm, tn), lambda i, j, k: (i, j)))
    return pl.pallas_call(
        body, grid=(m // tm, n // tn, nk), in_specs=in_specs,
        out_specs=pl.BlockSpec((tm, tn), lambda i, j, k: (i, j)),
        out_shape=jax.ShapeDtypeStruct((m, n), out_dtype),
        scratch_shapes=[pltpu.VMEM((tm, tn), F32)],
        compiler_params=_cparams(3), name=name)(*ins)


def _mm_tn(a, b, *, name):
    s, ka = a.shape
    nb = b.shape[1]
    ts = _seq_tile(s, 512)
    tka = _pick(ka, 1024)
    tnb = _pick(nb, 1408)

    def body(a_ref, b_ref, o_ref):
        @pl.when(pl.program_id(2) == 0)
        def _():
            o_ref[...] = jnp.zeros_like(o_ref)

        o_ref[...] += _dot_tn(a_ref[...], b_ref[...])

    return pl.pallas_call(
        body, grid=(ka // tka, nb // tnb, s // ts),
        in_specs=[pl.BlockSpec((ts, tka), lambda i, j, k: (k, i)), pl.BlockSpec((ts, tnb), lambda i, j, k: (k, j))],
        out_specs=pl.BlockSpec((tka, tnb), lambda i, j, k: (i, j)),
        out_shape=jax.ShapeDtypeStruct((ka, nb), F32),
        compiler_params=_cparams(3), name=name)(a, b)


def _colsum_call(x, *, name):
    s, n = x.shape
    ts = _seq_tile(s, 512)

    def body(x_ref, o_ref):
        @pl.when(pl.program_id(0) == 0)
        def _():
            o_ref[...] = jnp.zeros_like(o_ref)

        o_ref[...] += _colsum(x_ref[...])

    return pl.pallas_call(
        body, grid=(s // ts,), in_specs=[pl.BlockSpec((ts, n), lambda i: (i, 0))],
        out_specs=pl.BlockSpec((1, n), lambda i: (0, 0)), out_shape=jax.ShapeDtypeStruct((1, n), F32),
        compiler_params=_cparams(1), name=name)(x)


def _ln_fwd(r, g, b, *, name):
    s, d = r.shape
    ts = _seq_tile(s, 512)

    def body(r_ref, g_ref, b_ref, o_ref):
        x = r_ref[...]
        mu = jnp.mean(x, axis=1, keepdims=True)
        xc = x - mu
        var = jnp.mean(xc * xc, axis=1, keepdims=True)
        o_ref[...] = xc * lax.rsqrt(var + LN_EPS) * g_ref[...] + b_ref[...]

    return pl.pallas_call(
        body, grid=(s // ts,),
        in_specs=[pl.BlockSpec((ts, d), lambda i: (i, 0)), _full_spec(g), _full_spec(b)],
        out_specs=pl.BlockSpec((ts, d), lambda i: (i, 0)), out_shape=jax.ShapeDtypeStruct((s, d), F32),
        compiler_params=_cparams(1), name=name)(r, g, b)


def _ln_bwd(r, dy, g, *, name):
    s, d = r.shape
    ts = _seq_tile(s, 512)

    def body(r_ref, dy_ref, g_ref, dr_ref, dg_ref, db_ref, ds_ref):
        @pl.when(pl.program_id(0) == 0)
        def _():
            dg_ref[...] = jnp.zeros_like(dg_ref)
            db_ref[...] = jnp.zeros_like(db_ref)
            ds_ref[...] = jnp.zeros_like(ds_ref)

        x = r_ref[...]
        dy = dy_ref[...]
        mu = jnp.mean(x, axis=1, keepdims=True)
        xc = x - mu
        var = jnp.mean(xc * xc, axis=1, keepdims=True)
        rstd = lax.rsqrt(var + LN_EPS)
        xh = xc * rstd
        dxh = dy * g_ref[...]
        m1 = jnp.mean(dxh, axis=1, keepdims=True)
        m2 = jnp.mean(dxh * xh, axis=1, keepdims=True)
        dr = rstd * (dxh - m1 - xh * m2)
        dr_ref[...] = dr
        dg_ref[...] += _colsum(dy * xh)
        db_ref[...] += _colsum(dy)
        ds_ref[...] += _colsum(dr)

    vec = jax.ShapeDtypeStruct((1, d), F32)
    vspec = pl.BlockSpec((1, d), lambda i: (0, 0))
    return pl.pallas_call(
        body, grid=(s // ts,),
        in_specs=[pl.BlockSpec((ts, d), lambda i: (i, 0)), pl.BlockSpec((ts, d), lambda i: (i, 0)), _full_spec(g)],
        out_specs=[pl.BlockSpec((ts, d), lambda i: (i, 0)), vspec, vspec, vspec],
        out_shape=[jax.ShapeDtypeStruct((s, d), F32), vec, vec, vec],
        compiler_params=_cparams(1), name=name)(r, dy, g)


def _loss_grad(y, target, *, name):
    s, d = y.shape
    ts = _seq_tile(s, 512)

    def body(y_ref, t_ref, dy_ref, l_ref):
        @pl.when(pl.program_id(0) == 0)
        def _():
            l_ref[...] = jnp.zeros_like(l_ref)

        e = y_ref[...] - t_ref[...]
        dy_ref[...] = e * (1.0 / d)
        part = jnp.sum(jnp.sum(e * e, axis=1, keepdims=True), axis=0, keepdims=True) * (0.5 / d)
        l_ref[...] += jnp.broadcast_to(part, l_ref.shape)

    return pl.pallas_call(
        body, grid=(s // ts,),
        in_specs=[pl.BlockSpec((ts, d), lambda i: (i, 0)), pl.BlockSpec((ts, d), lambda i: (i, 0))],
        out_specs=[pl.BlockSpec((ts, d), lambda i: (i, 0)), pl.BlockSpec((SUBLANE, LANE), lambda i: (0, 0))],
        out_shape=[jax.ShapeDtypeStruct((s, d), F32), jax.ShapeDtypeStruct((SUBLANE, LANE), F32)],
        compiler_params=_cparams(1), name=name)(y, target)


SCAN_CHUNK = 32


def _cscan_levels(bufs, apow_ref, t, pad, *, reverse):
    half = bufs[0].shape[1] // 2
    ch = min(SCAN_CHUNK, t)
    nlev = t.bit_length() - 1
    assert (1 << nlev) == t
    for k in range(nlev):
        d = 1 << k
        src, dst = bufs[k % 2], bufs[(k + 1) % 2]

        def chunk(c, carry, src=src, dst=dst, d=d, k=k):
            ar = apow_ref[k:k + 1, :half]
            ai = apow_ref[k:k + 1, half:]
            if reverse:
                ai = -ai
            r0 = pl.multiple_of(c * ch, ch)
            cur = src[pl.ds(pad + r0, ch), :]
            if d >= SUBLANE:
                off = pad + d if reverse else pad - d
                sh = src[pl.ds(off + r0, ch), :]
            elif reverse:
                blk = src[pl.ds(pad + r0, ch + SUBLANE), :]
                sh = pltpu.roll(blk, ch + SUBLANE - d, axis=0)[:ch, :]
            else:
                blk = src[pl.ds(pad - SUBLANE + r0, ch + SUBLANE), :]
                sh = pltpu.roll(blk, d, axis=0)[SUBLANE:, :]
            sre, sim = sh[:, :half], sh[:, half:]
            dst[pl.ds(pad + r0, ch), :half] = cur[:, :half] + ar * sre - ai * sim
            dst[pl.ds(pad + r0, ch), half:] = cur[:, half:] + ar * sim + ai * sre
            return carry

        lax.fori_loop(0, t // ch, chunk, 0)
    return nlev % 2


def _rscan_levels(abufs, bbufs, t, pad, *, reverse):
    nlev = t.bit_length() - 1
    assert (1 << nlev) == t
    for k in range(nlev):
        d = 1 << k
        asrc, adst = abufs[k % 2], abufs[(k + 1) % 2]
        bsrc, bdst = bbufs[k % 2], bbufs[(k + 1) % 2]
        off = pad + d if reverse else pad - d
        a = asrc[pad:pad + t, :]
        bdst[pad:pad + t, :] = a * bsrc[off:off + t, :] + bsrc[pad:pad + t, :]
        if k < nlev - 1:
            adst[pad:pad + t, :] = a * asrc[off:off + t, :]
    return nlev % 2


S5_TILE = 256


def _s5_scan_forward(u_bf, wb_ref, apow_ref, state, bufs, t, pad):
    half = S5_LANES
    bufs[0][pad:pad + t, :] = _dot(u_bf, wb_ref[...])
    ar, ai = apow_ref[0:1, :half], apow_ref[0:1, half:]
    sr, si = state[:, :half], state[:, half:]
    bufs[0][pad:pad + 1, :half] += ar * sr - ai * si
    bufs[0][pad:pad + 1, half:] += ar * si + ai * sr
    return _cscan_levels(bufs, apow_ref, t, pad, reverse=False)


def _s5_fwd(h_in, wb, apow, wc, dvec, wglu, bglu, *, name):
    s = h_in.shape[0]
    t = _seq_tile(s, S5_TILE)
    pad = t // 2
    nt = s // t
    lanes2 = 2 * S5_LANES

    def body(u_ref, wb_ref, apow_ref, wc_ref, d_ref, wglu_ref, bglu_ref, out_ref, y1_ref, hb_ref, buf0, buf1, carry):
        bufs = (buf0, buf1)

        @pl.when(pl.program_id(0) == 0)
        def _():
            buf0[0:pad, :] = jnp.zeros((pad, lanes2), F32)
            buf1[0:pad, :] = jnp.zeros((pad, lanes2), F32)
            carry[...] = jnp.zeros_like(carry)

        u = u_ref[...]
        state = carry[0:1, :]
        hb_ref[0] = state
        fin = _s5_scan_forward(u.astype(BF16), wb_ref, apow_ref, state, bufs, t, pad)
        hbuf = bufs[fin]
        carry[0:1, :] = hbuf[pad + t - 1:pad + t, :]
        y1 = _dot(hbuf[pad:pad + t, :], wc_ref[...]) + d_ref[...] * u
        y1_ref[...] = y1
        y2 = _gelu(y1)
        z = _dot(y2, wglu_ref[...]) + bglu_ref[...]
        out_ref[...] = y2 * _sigmoid(z)

    ins = [h_in, wb, apow, wc, dvec, wglu, bglu]
    in_specs = [pl.BlockSpec((t, D_GROUP), lambda i: (i, 0))] + [_full_spec(a) for a in ins[1:]]
    row = pl.BlockSpec((t, D_GROUP), lambda i: (i, 0))
    return pl.pallas_call(
        body, grid=(nt,), in_specs=in_specs,
        out_specs=[row, row, pl.BlockSpec((1, 1, lanes2), lambda i: (i, 0, 0))],
        out_shape=[jax.ShapeDtypeStruct((s, D_GROUP), F32), jax.ShapeDtypeStruct((s, D_GROUP), F32),
                   jax.ShapeDtypeStruct((nt, 1, lanes2), F32)],
        scratch_shapes=[pltpu.VMEM((pad + t, lanes2), F32), pltpu.VMEM((pad + t, lanes2), F32),
                        pltpu.VMEM((SUBLANE, lanes2), F32)],
        compiler_params=_cparams(1), name=name)(*ins)


def _s5_bwd(h_in, y1, dmix, hb, wb, apow, wc, dvec, wglu, bglu, *, name):
    s = h_in.shape[0]
    t = _seq_tile(s, S5_TILE)
    pad = t // 2
    nt = s // t
    half = S5_LANES
    lanes2 = 2 * half
    rows = pad + t + pad

    def body(u_ref, y1_ref, do_ref, hb_ref, wb_ref, apow_ref, wc_ref, d_ref, wglu_ref, bglu_ref,
             du_ref, dwglu_ref, dwc_ref, dwb_ref, dbglu_ref, dd_ref, da_ref, buf0, buf1, buf2, buf3, carry):
        @pl.when(pl.program_id(0) == 0)
        def _():
            for bf in (buf0, buf1, buf2, buf3):
                bf[0:pad, :] = jnp.zeros((pad, lanes2), F32)
                bf[pad + t:rows, :] = jnp.zeros((pad, lanes2), F32)
            carry[...] = jnp.zeros_like(carry)
            for r in (dwglu_ref, dwc_ref, dwb_ref, dbglu_ref, dd_ref, da_ref):
                r[...] = jnp.zeros_like(r)

        u = u_ref[...]
        u_bf = u.astype(BF16)
        state = hb_ref[0]
        hfin = _s5_scan_forward(u_bf, wb_ref, apow_ref, state, (buf0, buf1), t, pad)
        hbuf = (buf0, buf1)[hfin]
        h_bf = hbuf[pad:pad + t, :].astype(BF16)

        y1 = y1_ref[...]
        dout = do_ref[...]
        y2, dgelu = _gelu_and_grad(y1)
        sg = _sigmoid(_dot(y2, wglu_ref[...]) + bglu_ref[...])
        dz = dout * y2 * sg * (1.0 - sg)
        dy2 = dout * sg + _dot_nt(dz, wglu_ref[...])
        dwglu_ref[...] += _dot_tn(y2, dz)
        dbglu_ref[...] += _colsum(dz)
        dy1 = dy2 * dgelu
        dd_ref[...] += _colsum(dy1 * u)
        dy1_bf = dy1.astype(BF16)
        dwc_ref[...] += _dot_tn(h_bf, dy1_bf)

        buf2[pad:pad + t, :] = _dot_nt(dy1_bf, wc_ref[...])
        ar, ai = apow_ref[0:1, :half], apow_ref[0:1, half:]
        cr, ci = carry[0:1, :half], carry[0:1, half:]
        buf2[pad + t - 1:pad + t, :half] += ar * cr + ai * ci
        buf2[pad + t - 1:pad + t, half:] += ar * ci - ai * cr
        lfin = _cscan_levels((buf2, buf3), apow_ref, t, pad, reverse=True)
        lbuf = (buf2, buf3)[lfin]
        lam = lbuf[pad:pad + t, :]
        carry[0:1, :] = lbuf[pad:pad + 1, :]
        lam_bf = lam.astype(BF16)
        du_ref[...] = dy1 * d_ref[...] + _dot_nt(lam_bf, wb_ref[...])
        dwb_ref[...] += _dot_tn(u_bf, lam_bf)

        hbuf[pad - 1:pad, :] = state
        hp = hbuf[pad - 1:pad - 1 + t, :]
        hbuf[pad - 1:pad, :] = jnp.zeros((1, lanes2), F32)
        lre, lim = lam[:, :half], lam[:, half:]
        hre, him = hp[:, :half], hp[:, half:]
        da_ref[:, :half] += _colsum(lre * hre + lim * him)
        da_ref[:, half:] += _colsum(lim * hre - lre * him)

    def rev(i):
        return (nt - 1 - i, 0)

    ins = [h_in, y1, dmix, hb, wb, apow, wc, dvec, wglu, bglu]
    row = pl.BlockSpec((t, D_GROUP), rev)
    in_specs = [row, row, row, pl.BlockSpec((1, 1, lanes2), lambda i: (nt - 1 - i, 0, 0))] + [_full_spec(a) for a in ins[4:]]
    outs = [jax.ShapeDtypeStruct((s, D_GROUP), F32), jax.ShapeDtypeStruct((D_GROUP, D_GROUP), F32),
            jax.ShapeDtypeStruct((lanes2, D_GROUP), F32), jax.ShapeDtypeStruct((D_GROUP, lanes2), F32),
            jax.ShapeDtypeStruct((1, D_GROUP), F32), jax.ShapeDtypeStruct((1, D_GROUP), F32),
            jax.ShapeDtypeStruct((1, lanes2), F32)]
    out_specs = [row] + [_full_spec(o) for o in outs[1:]]
    return pl.pallas_call(
        body, grid=(nt,), in_specs=in_specs, out_specs=out_specs, out_shape=outs,
        scratch_shapes=[pltpu.VMEM((rows, lanes2), F32) for _ in range(4)] + [pltpu.VMEM((SUBLANE, lanes2), F32)],
        compiler_params=_cparams(1), name=name)(*ins)


def _s5_param_map(lam_re, lam_im, log_dt, b_re, b_im, c_re, c_im):
    dt = jnp.exp(log_dt)[:, None]
    er = jnp.exp(lam_re * dt)
    a_re, a_im = er * jnp.cos(lam_im * dt), er * jnp.sin(lam_im * dt)
    den = lam_re * lam_re + lam_im * lam_im
    n_re = a_re - 1.0
    k_re = (n_re * lam_re + a_im * lam_im) / den
    k_im = (a_im * lam_re - n_re * lam_im) / den
    bb_re = k_re[..., None] * b_re - k_im[..., None] * b_im
    bb_im = k_re[..., None] * b_im + k_im[..., None] * b_re
    eye = jnp.eye(S5_GROUPS, dtype=F32)

    def blockdiag_in(m):
        return jnp.einsum("gpc,gh->gchp", m, eye).reshape(S5_GROUPS * S5_CH, S5_LANES)

    def blockdiag_out(m):
        return jnp.einsum("gcp,gh->gphc", m, eye).reshape(S5_LANES, S5_GROUPS * S5_CH)

    a = jnp.concatenate([a_re.reshape(1, -1), a_im.reshape(1, -1)], axis=1)
    wb = jnp.concatenate([blockdiag_in(bb_re), blockdiag_in(bb_im)], axis=1)
    wc = jnp.concatenate([blockdiag_out(c_re), -blockdiag_out(c_im)], axis=0)
    return a, wb, wc


def _s5_apow(a, nlev):
    half = S5_LANES
    re, im = a[:, :half], a[:, half:]
    rows = []
    for _ in range(nlev):
        rows.append(jnp.concatenate([re, im], axis=1))
        re, im = re * re - im * im, 2.0 * re * im
    n_rows = -(-nlev // SUBLANE) * SUBLANE
    rows += [jnp.zeros_like(rows[0])] * (n_rows - nlev)
    return lax.stop_gradient(jnp.concatenate(rows, axis=0))


CV_TILE = 256
CV_PAD = 32
CV_CHUNK = 64


def _gn_stats(c, mavg):
    mu = _dot_hi(c, mavg)
    cen = c - mu
    var = _dot_hi(cen * cen, mavg)
    rstd = lax.rsqrt(var + LN_EPS)
    return cen * rstd, rstd


def _cv_fwd(h_in, cw, cb, gng, gnb, mavg, wpw, bpw, *, name):
    s = h_in.shape[0]
    t = _seq_tile(s, CV_TILE)
    ch = min(CV_CHUNK, t)

    def body(v_ref, g_ref, cw_ref, cb_ref, gng_ref, gnb_ref, mavg_ref, wpw_ref, bpw_ref, out_ref, c_ref, xpad):
        @pl.when(pl.program_id(0) == 0)
        def _():
            xpad[0:CV_PAD, :] = jnp.zeros((CV_PAD, D_GROUP), F32)

        xpad[CV_PAD:CV_PAD + t, :] = v_ref[...] * _sigmoid(g_ref[...])
        for r0 in range(0, t, ch):
            acc = jnp.broadcast_to(cb_ref[...], (ch, D_GROUP))
            for k in range(CONV_WIDTH):
                o = CV_PAD - (CONV_WIDTH - 1) + k + r0
                acc = acc + cw_ref[k:k + 1, :] * xpad[o:o + ch, :]
            c_ref[r0:r0 + ch, :] = acc
        xpad[0:CV_PAD, :] = xpad[t:t + CV_PAD, :]
        xn, _ = _gn_stats(c_ref[...], mavg_ref[...])
        gn = xn * gng_ref[...] + gnb_ref[...]
        out_ref[...] = _dot(gn * _sigmoid(gn), wpw_ref[...]) + bpw_ref[...]

    ins = [h_in, h_in, cw, cb, gng, gnb, mavg, wpw, bpw]
    row = pl.BlockSpec((t, D_GROUP), lambda i: (i, 0))
    in_specs = [pl.BlockSpec((t, D_GROUP), lambda i: (i, 1)), pl.BlockSpec((t, D_GROUP), lambda i: (i, 2))] + \
               [_full_spec(a) for a in ins[2:]]
    return pl.pallas_call(
        body, grid=(s // t,), in_specs=in_specs, out_specs=[row, row],
        out_shape=[jax.ShapeDtypeStruct((s, D_GROUP), F32)] * 2,
        scratch_shapes=[pltpu.VMEM((CV_PAD + t, D_GROUP), F32)],
        compiler_params=_cparams(1), name=name)(*ins)


def _cv_bwd(h_in, c, dmix, cw, gng, gnb, mavg, wpw, *, name):
    s = h_in.shape[0]
    t = _seq_tile(s, CV_TILE)
    nt = s // t
    ch = min(CV_CHUNK, t)

    def body(v_ref, g_ref, c_ref, do_ref, cw_ref, gng_ref, gnb_ref, mavg_ref, wpw_ref,
             dv_ref, dg_ref, dwpw_ref, dcw_ref, dbpw_ref, dgg_ref, dgb_ref, dcb_ref, dcpad, hgbuf):
        @pl.when(pl.program_id(0) == 0)
        def _():
            dcpad[t:t + CV_PAD, :] = jnp.zeros((CV_PAD, D_GROUP), F32)
            for r in (dwpw_ref, dcw_ref, dbpw_ref, dgg_ref, dgb_ref, dcb_ref):
                r[...] = jnp.zeros_like(r)

        mavg = mavg_ref[...]
        xn, rstd = _gn_stats(c_ref[...], mavg)
        gg = gng_ref[...]
        gn = xn * gg + gnb_ref[...]
        sg = _sigmoid(gn)
        dout = do_ref[...]
        dwpw_ref[...] += _dot_tn(gn * sg, dout)
        dbpw_ref[...] += _colsum(dout)
        dgn = _dot_nt(dout, wpw_ref[...]) * (sg * (1.0 + gn * (1.0 - sg)))
        dgg_ref[...] += _colsum(dgn * xn)
        dgb_ref[...] += _colsum(dgn)
        dxn = dgn * gg
        dc = rstd * (dxn - _dot_hi(dxn, mavg) - xn * _dot_hi(dxn * xn, mavg))
        dcb_ref[...] += _colsum(dc)
        dcpad[0:t, :] = dc

        v = v_ref[...]
        sgm = _sigmoid(g_ref[...])
        hgbuf[...] = v * sgm
        for r0 in range(0, t, ch):
            hg = hgbuf[r0:r0 + ch, :]
            acc = jnp.zeros((ch, D_GROUP), F32)
            for k in range(CONV_WIDTH):
                o = (CONV_WIDTH - 1) - k + r0
                sh = dcpad[o:o + ch, :]
                acc = acc + cw_ref[k:k + 1, :] * sh
                dcw_ref[k:k + 1, :] += _colsum(hg * sh)
            hgbuf[r0:r0 + ch, :] = acc
        dcpad[t:t + CV_PAD, :] = dcpad[0:CV_PAD, :]
        dhg = hgbuf[...]
        dv_ref[...] = dhg * sgm
        dg_ref[...] = dhg * v * sgm * (1.0 - sgm)

    def rev(col):
        return lambda i: (nt - 1 - i, col)

    ins = [h_in, h_in, c, dmix, cw, gng, gnb, mavg, wpw]
    in_specs = [pl.BlockSpec((t, D_GROUP), rev(1)), pl.BlockSpec((t, D_GROUP), rev(2)),
                pl.BlockSpec((t, D_GROUP), rev(0)), pl.BlockSpec((t, D_GROUP), rev(1))] + [_full_spec(a) for a in ins[4:]]
    vec = jax.ShapeDtypeStruct((1, D_GROUP), F32)
    outs = [jax.ShapeDtypeStruct((s, D_GROUP), F32)] * 2 + \
           [jax.ShapeDtypeStruct((D_GROUP, D_GROUP), F32), jax.ShapeDtypeStruct((CV_PAD, D_GROUP), F32), vec, vec, vec, vec]
    out_specs = [pl.BlockSpec((t, D_GROUP), rev(0))] * 2 + [_full_spec(o) for o in outs[2:]]
    return pl.pallas_call(
        body, grid=(nt,), in_specs=in_specs, out_specs=out_specs, out_shape=outs,
        scratch_shapes=[pltpu.VMEM((t + CV_PAD, D_GROUP), F32), pltpu.VMEM((t, D_GROUP), F32)],
        compiler_params=_cparams(1), name=name)(*ins)


LRU_TILE = 256


def _lru_gates(xc, wr_ref, br_ref, wi_ref, bi_ref, sp_ref):
    r = _sigmoid(_dot(xc, wr_ref[...]) + br_ref[...])
    i = _sigmoid(_dot(xc, wi_ref[...]) + bi_ref[...])
    log_a = -LRU_C * r * sp_ref[...]
    a = jnp.exp(log_a)
    m = jnp.sqrt(_neg_expm1(2.0 * log_a))
    return r, i, a, m


def _lru_fwd(h_in, lcw, lcb, wr, br, wi, bi, sp, *, name):
    s = h_in.shape[0]
    t = _seq_tile(s, LRU_TILE)
    pad = max(t // 2, SUBLANE)

    def body(xg_ref, xr_ref, lcw_ref, lcb_ref, wr_ref, br_ref, wi_ref, bi_ref, sp_ref,
             out_ref, xc_ref, h_ref, xpad, a0, a1, b0, b1, carry):
        @pl.when(pl.program_id(0) == 0)
        def _():
            xpad[0:SUBLANE, :] = jnp.zeros((SUBLANE, D_GROUP), F32)
            for bf in (a0, a1, b0, b1):
                bf[0:pad, :] = jnp.zeros((pad, D_GROUP), F32)
            carry[...] = jnp.zeros_like(carry)

        xpad[SUBLANE:SUBLANE + t, :] = xr_ref[...]
        xc = jnp.broadcast_to(lcb_ref[...], (t, D_GROUP))
        for k in range(LRU_CONV_WIDTH):
            o = SUBLANE - (LRU_CONV_WIDTH - 1) + k
            xc = xc + lcw_ref[k:k + 1, :] * xpad[o:o + t, :]
        xpad[0:SUBLANE, :] = xpad[t:t + SUBLANE, :]
        xc_ref[...] = xc
        _, i, a, m = _lru_gates(xc, wr_ref, br_ref, wi_ref, bi_ref, sp_ref)
        a0[pad:pad + t, :] = a
        b0[pad:pad + t, :] = m * (i * xc)
        b0[pad:pad + 1, :] += a0[pad:pad + 1, :] * carry[0:1, :]
        fin = _rscan_levels((a0, a1), (b0, b1), t, pad, reverse=False)
        hbuf = (b0, b1)[fin]
        carry[0:1, :] = hbuf[pad + t - 1:pad + t, :]
        h = hbuf[pad:pad + t, :]
        h_ref[...] = h
        out_ref[...] = h * _gelu(xg_ref[...])

    ins = [h_in, h_in, lcw, lcb, wr, br, wi, bi, sp]
    row = pl.BlockSpec((t, D_GROUP), lambda i: (i, 0))
    in_specs = [pl.BlockSpec((t, D_GROUP), lambda i: (i, 3)), pl.BlockSpec((t, D_GROUP), lambda i: (i, 4))] + \
               [_full_spec(a) for a in ins[2:]]
    return pl.pallas_call(
        body, grid=(s // t,), in_specs=in_specs, out_specs=[row, row, row],
        out_shape=[jax.ShapeDtypeStruct((s, D_GROUP), F32)] * 3,
        scratch_shapes=[pltpu.VMEM((SUBLANE + t, D_GROUP), F32)] + [pltpu.VMEM((pad + t, D_GROUP), F32)] * 4 +
                       [pltpu.VMEM((SUBLANE, D_GROUP), F32)],
        compiler_params=_cparams(1), name=name)(*ins)


def _lru_bwd(h_in, xc_all, h_all, dmix, lcw, wr, br, wi, bi, sp, *, name):
    s = h_in.shape[0]
    t = _seq_tile(s, LRU_TILE)
    nt = s // t
    pad = max(t // 2, SUBLANE)
    tb = t // SUBLANE

    def body(xg_ref, xr_ref, xc_ref, h_ref, hprev_ref, do_ref, lcw_ref, wr_ref, br_ref, wi_ref, bi_ref, sp_ref,
             dxg_ref, dxr_ref, dwr_ref, dwi_ref, dlcw_ref, dbr_ref, dbi_ref, dsp_ref, dlcb_ref,
             a0, a1, b0, b1, hp, dxpad, carry):
        pid = pl.program_id(0)

        @pl.when(pid == 0)
        def _():
            for bf in (a0, a1, b0, b1):
                bf[pad + t:pad + t + pad, :] = jnp.zeros((pad, D_GROUP), F32)
            dxpad[t:t + SUBLANE, :] = jnp.zeros((SUBLANE, D_GROUP), F32)
            carry[...] = jnp.zeros_like(carry)
            for r in (dwr_ref, dwi_ref, dlcw_ref, dbr_ref, dbi_ref, dsp_ref, dlcb_ref):
                r[...] = jnp.zeros_like(r)

        xc = xc_ref[...]
        h = h_ref[...]
        dout = do_ref[...]
        gate, dgate = _gelu_and_grad(xg_ref[...])
        dxg_ref[...] = dout * h * dgate
        r, i, a, m = _lru_gates(xc, wr_ref, br_ref, wi_ref, bi_ref, sp_ref)

        a0[pad:pad + t, :] = a
        b0[pad:pad + t, :] = dout * gate
        b0[pad + t - 1:pad + t, :] += carry[0:1, :]
        a1[pad:pad + t, :] = a0[pad + 1:pad + 1 + t, :]
        fin = _rscan_levels((a1, a0), (b0, b1), t, pad, reverse=True)
        lam = (b0, b1)[fin][pad:pad + t, :]
        carry[0:1, :] = a[0:1, :] * lam[0:1, :]

        is_first = pid == nt - 1
        hp[0:SUBLANE, :] = jnp.where(is_first, 0.0, hprev_ref[...])
        hp[SUBLANE:SUBLANE + t, :] = h
        hprev = hp[SUBLANE - 1:SUBLANE - 1 + t, :]

        ix = i * xc
        dmm = lam * ix
        dix = lam * m
        da = lam * hprev - dmm * (a / m)
        dlog_a = da * a
        dr = dlog_a * (-LRU_C * sp_ref[...])
        dsp_ref[...] += _colsum(dlog_a * (-LRU_C * r))
        dpr = dr * r * (1.0 - r)
        dpi = dix * xc * i * (1.0 - i)
        dbr_ref[...] += _colsum(dpr)
        dbi_ref[...] += _colsum(dpi)
        dwr_ref[...] += _dot_tn(xc, dpr)
        dwi_ref[...] += _dot_tn(xc, dpi)
        dxc = dix * i + _dot_nt(dpr, wr_ref[...]) + _dot_nt(dpi, wi_ref[...])
        dlcb_ref[...] += _colsum(dxc)

        dxpad[0:t, :] = dxc
        xr = xr_ref[...]
        dxr = jnp.zeros((t, D_GROUP), F32)
        for k in range(LRU_CONV_WIDTH):
            o = (LRU_CONV_WIDTH - 1) - k
            sh = dxpad[o:o + t, :]
            dxr = dxr + lcw_ref[k:k + 1, :] * sh
            dlcw_ref[k:k + 1, :] += _colsum(xr * sh)
        dxpad[t:t + SUBLANE, :] = dxpad[0:SUBLANE, :]
        dxr_ref[...] = dxr

    def rev(col):
        return lambda i: (nt - 1 - i, col)

    ins = [h_in, h_in, xc_all, h_all, h_all, dmix, lcw, wr, br, wi, bi, sp]
    in_specs = [pl.BlockSpec((t, D_GROUP), rev(3)), pl.BlockSpec((t, D_GROUP), rev(4)),
                pl.BlockSpec((t, D_GROUP), rev(0)), pl.BlockSpec((t, D_GROUP), rev(0)),
                pl.BlockSpec((SUBLANE, D_GROUP), lambda i: (jnp.maximum((nt - 1 - i) * tb - 1, 0), 0)),
                pl.BlockSpec((t, D_GROUP), rev(2))] + [_full_spec(a) for a in ins[6:]]
    vec = jax.ShapeDtypeStruct((1, D_GROUP), F32)
    mat = jax.ShapeDtypeStruct((D_GROUP, D_GROUP), F32)
    outs = [jax.ShapeDtypeStruct((s, D_GROUP), F32)] * 2 + [mat, mat, jax.ShapeDtypeStruct((SUBLANE, D_GROUP), F32),
                                                              vec, vec, vec, vec]
    out_specs = [pl.BlockSpec((t, D_GROUP), rev(0))] * 2 + [_full_spec(o) for o in outs[2:]]
    return pl.pallas_call(
        body, grid=(nt,), in_specs=in_specs, out_specs=out_specs, out_shape=outs,
        scratch_shapes=[pltpu.VMEM((pad + t + pad, D_GROUP), F32)] * 4 +
                       [pltpu.VMEM((SUBLANE + t, D_GROUP), F32), pltpu.VMEM((t + SUBLANE, D_GROUP), F32),
                        pltpu.VMEM((SUBLANE, D_GROUP), F32)],
        compiler_params=_cparams(1), name=name)(*ins)


def _blockdiag(w):
    h, d, _ = w.shape
    return jnp.einsum("hij,hk->hikj", w, jnp.eye(h, dtype=w.dtype)).reshape(h * d, h * d)


ATTN_TILE = 512
ATTN_SCALE = ATTN_HEAD_DIM ** -0.5


def _attn_big(kv):
    m = kv.shape[0]
    k = kv[:, :D_GROUP].reshape(m, ATTN_HEADS, ATTN_HEAD_DIM)
    v = kv[:, D_GROUP:].reshape(m, ATTN_HEADS, ATTN_HEAD_DIM)
    eye = jnp.eye(ATTN_HEADS, dtype=kv.dtype)
    kbig = jnp.einsum("mhd,hg->hdgm", k, eye).reshape(D_GROUP, ATTN_HEADS * m)
    vbig = jnp.einsum("mhd,hg->gmhd", v, eye).reshape(ATTN_HEADS * m, D_GROUP)
    return kbig, vbig


def _attn_probs(q, kbig_ref, m):
    sc = _dot(q, kbig_ref[...]) * ATTN_SCALE
    ps = []
    for h in range(ATTN_HEADS):
        sh = sc[:, h * m:(h + 1) * m]
        e = jnp.exp(sh - jnp.max(sh, axis=1, keepdims=True))
        ps.append(e / jnp.sum(e, axis=1, keepdims=True))
    return ps


def _attn_fwd(h_in, kbig, vbig, *, name):
    s = h_in.shape[0]
    t = _seq_tile(s, ATTN_TILE)
    m = kbig.shape[1] // ATTN_HEADS

    def body(q_ref, kbig_ref, vbig_ref, o_ref):
        ps = _attn_probs(q_ref[...], kbig_ref, m)
        o_ref[...] = _dot(jnp.concatenate(ps, axis=1), vbig_ref[...])

    return pl.pallas_call(
        body, grid=(s // t,),
        in_specs=[pl.BlockSpec((t, D_GROUP), lambda i: (i, 5)), _full_spec(kbig), _full_spec(vbig)],
        out_specs=pl.BlockSpec((t, D_GROUP), lambda i: (i, 0)),
        out_shape=jax.ShapeDtypeStruct((s, D_GROUP), F32),
        compiler_params=_cparams(1), name=name)(h_in, kbig, vbig)


def _attn_bwd(h_in, dmix, kbig, vbig, *, name):
    s = h_in.shape[0]
    t = _seq_tile(s, ATTN_TILE)
    m = kbig.shape[1] // ATTN_HEADS

    def body(q_ref, do_ref, kbig_ref, vbig_ref, dq_ref, dk_ref, dv_ref):
        @pl.when(pl.program_id(0) == 0)
        def _():
            dk_ref[...] = jnp.zeros_like(dk_ref)
            dv_ref[...] = jnp.zeros_like(dv_ref)

        q = q_ref[...]
        dout = do_ref[...]
        ps = _attn_probs(q, kbig_ref, m)
        dp = _dot_nt(dout, vbig_ref[...])
        dss = []
        for h in range(ATTN_HEADS):
            dph = dp[:, h * m:(h + 1) * m]
            dss.append(ps[h] * (dph - jnp.sum(dph * ps[h], axis=1, keepdims=True)))
        ds = (jnp.concatenate(dss, axis=1) * ATTN_SCALE).astype(BF16)
        dv_ref[...] += _dot_tn(jnp.concatenate(ps, axis=1), dout)
        dq_ref[...] = _dot_nt(ds, kbig_ref[...])
        dk_ref[...] += _dot_tn(q, ds)

    outs = [jax.ShapeDtypeStruct((s, D_GROUP), F32), jax.ShapeDtypeStruct(kbig.shape, F32),
            jax.ShapeDtypeStruct(vbig.shape, F32)]
    return pl.pallas_call(
        body, grid=(s // t,),
        in_specs=[pl.BlockSpec((t, D_GROUP), lambda i: (i, 5)), pl.BlockSpec((t, D_GROUP), lambda i: (i, 3)),
                  _full_spec(kbig), _full_spec(vbig)],
        out_specs=[pl.BlockSpec((t, D_GROUP), lambda i: (i, 0)), _full_spec(outs[1]), _full_spec(outs[2])],
        out_shape=outs, compiler_params=_cparams(1), name=name)(h_in, dmix, kbig, vbig)


FFN_TILE = 256
FFN_COLS = 1408
FFN_CHUNK = 64


def _ffn_conv(pad_ref, w_ref, b_ref, r0, ch):
    acc = jnp.broadcast_to(b_ref[...], (ch, b_ref.shape[1]))
    for k in range(FFN_CONV_WIDTH):
        o = SUBLANE - (FFN_CONV_WIDTH - 1) + k + r0
        acc = acc + w_ref[k:k + 1, :] * pad_ref[o:o + ch, :]
    return acc


def _ffn_gate_fwd(u, fcw, fcb, *, name):
    s = u.shape[0]
    t = _seq_tile(s, FFN_TILE)
    cw = FFN_COLS
    ncol = D_FF // cw
    ch = min(FFN_CHUNK, t)

    def body(uv_ref, ug_ref, wv_ref, wg_ref, bv_ref, bg_ref, o_ref, vpad, gpad):
        @pl.when(pl.program_id(1) == 0)
        def _():
            vpad[0:SUBLANE, :] = jnp.zeros((SUBLANE, cw), F32)
            gpad[0:SUBLANE, :] = jnp.zeros((SUBLANE, cw), F32)

        vpad[SUBLANE:SUBLANE + t, :] = uv_ref[...]
        gpad[SUBLANE:SUBLANE + t, :] = ug_ref[...]
        for r0 in range(0, t, ch):
            val = _ffn_conv(vpad, wv_ref, bv_ref, r0, ch)
            gt = _ffn_conv(gpad, wg_ref, bg_ref, r0, ch)
            o_ref[r0:r0 + ch, :] = (val * _gelu(gt)).astype(BF16)
        vpad[0:SUBLANE, :] = vpad[t:t + SUBLANE, :]
        gpad[0:SUBLANE, :] = gpad[t:t + SUBLANE, :]

    in_specs = [pl.BlockSpec((t, cw), lambda j, i: (i, j)), pl.BlockSpec((t, cw), lambda j, i: (i, j + ncol)),
                pl.BlockSpec((SUBLANE, cw), lambda j, i: (0, j)), pl.BlockSpec((SUBLANE, cw), lambda j, i: (0, j + ncol)),
                pl.BlockSpec((1, cw), lambda j, i: (0, j)), pl.BlockSpec((1, cw), lambda j, i: (0, j + ncol))]
    return pl.pallas_call(
        body, grid=(ncol, s // t), in_specs=in_specs,
        out_specs=pl.BlockSpec((t, cw), lambda j, i: (i, j)),
        out_shape=jax.ShapeDtypeStruct((s, D_FF), BF16),
        scratch_shapes=[pltpu.VMEM((SUBLANE + t, cw), F32)] * 2,
        compiler_params=_cparams(2), name=name)(u, u, fcw, fcw, fcb, fcb)


def _ffn_gate_bwd(u, dh, fcw, fcb, *, name):
    s = u.shape[0]
    t = _seq_tile(s, FFN_TILE)
    nt = s // t
    cw = FFN_COLS
    ncol = D_FF // cw
    ch = min(FFN_CHUNK, t)
    tb = t // SUBLANE

    def body(uv_ref, ug_ref, hv_ref, hg_ref, dh_ref, wv_ref, wg_ref, bv_ref, bg_ref,
             duv_ref, dug_ref, dwv_ref, dwg_ref, dbv_ref, dbg_ref, vpad, gpad, dvpad, dgpad):
        pid = pl.program_id(1)

        @pl.when(pid == 0)
        def _():
            dvpad[t:t + SUBLANE, :] = jnp.zeros((SUBLANE, cw), F32)
            dgpad[t:t + SUBLANE, :] = jnp.zeros((SUBLANE, cw), F32)
            for r in (dwv_ref, dwg_ref, dbv_ref, dbg_ref):
                r[...] = jnp.zeros_like(r)

        is_first = pid == nt - 1
        vpad[0:SUBLANE, :] = jnp.where(is_first, 0.0, hv_ref[...])
        gpad[0:SUBLANE, :] = jnp.where(is_first, 0.0, hg_ref[...])
        vpad[SUBLANE:SUBLANE + t, :] = uv_ref[...]
        gpad[SUBLANE:SUBLANE + t, :] = ug_ref[...]
        for r0 in range(0, t, ch):
            val = _ffn_conv(vpad, wv_ref, bv_ref, r0, ch)
            gt = _ffn_conv(gpad, wg_ref, bg_ref, r0, ch)
            gl, dgl = _gelu_and_grad(gt)
            d = dh_ref[r0:r0 + ch, :]
            dvpad[r0:r0 + ch, :] = d * gl
            dgpad[r0:r0 + ch, :] = d * val * dgl
        dbv_ref[...] += _colsum(dvpad[0:t, :])
        dbg_ref[...] += _colsum(dgpad[0:t, :])
        for r0 in range(0, t, ch):
            for (dpad, w_ref, x_ref, du_ref, dw_ref) in ((dvpad, wv_ref, uv_ref, duv_ref, dwv_ref),
                                                         (dgpad, wg_ref, ug_ref, dug_ref, dwg_ref)):
                x = x_ref[r0:r0 + ch, :]
                acc = jnp.zeros((ch, cw), F32)
                for k in range(FFN_CONV_WIDTH):
                    o = (FFN_CONV_WIDTH - 1) - k + r0
                    sh = dpad[o:o + ch, :]
                    acc = acc + w_ref[k:k + 1, :] * sh
                    dw_ref[k:k + 1, :] += _colsum(x * sh)
                du_ref[r0:r0 + ch, :] = acc
        dvpad[t:t + SUBLANE, :] = dvpad[0:SUBLANE, :]
        dgpad[t:t + SUBLANE, :] = dgpad[0:SUBLANE, :]

    def halo(col_off):
        return lambda j, i: (jnp.maximum((nt - 1 - i) * tb - 1, 0), j + col_off)

    in_specs = [pl.BlockSpec((t, cw), lambda j, i: (nt - 1 - i, j)), pl.BlockSpec((t, cw), lambda j, i: (nt - 1 - i, j + ncol)),
                pl.BlockSpec((SUBLANE, cw), halo(0)), pl.BlockSpec((SUBLANE, cw), halo(ncol)),
                pl.BlockSpec((t, cw), lambda j, i: (nt - 1 - i, j)),
                pl.BlockSpec((SUBLANE, cw), lambda j, i: (0, j)), pl.BlockSpec((SUBLANE, cw), lambda j, i: (0, j + ncol)),
                pl.BlockSpec((1, cw), lambda j, i: (0, j)), pl.BlockSpec((1, cw), lambda j, i: (0, j + ncol))]
    big = jax.ShapeDtypeStruct((s, D_FF), F32)
    wsh = jax.ShapeDtypeStruct((SUBLANE, D_FF), F32)
    bsh = jax.ShapeDtypeStruct((1, D_FF), F32)
    out_specs = [pl.BlockSpec((t, cw), lambda j, i: (nt - 1 - i, j))] * 2 + \
                [pl.BlockSpec((SUBLANE, cw), lambda j, i: (0, j))] * 2 + [pl.BlockSpec((1, cw), lambda j, i: (0, j))] * 2
    return pl.pallas_call(
        body, grid=(ncol, nt), in_specs=in_specs, out_specs=out_specs,
        out_shape=[big, big, wsh, wsh, bsh, bsh],
        scratch_shapes=[pltpu.VMEM((SUBLANE + t, cw), F32)] * 2 + [pltpu.VMEM((t + SUBLANE, cw), F32)] * 2,
        compiler_params=_cparams(2), name=name)(u, u, u, u, dh, fcw, fcw, fcb, fcb)


def _adamw(gstack, w, m, v, *, name):
    _, r, c = gstack.shape
    tr = min(r, PACK_ROW_BLOCK)
    assert r % tr == 0
    inv_b1 = 1.0 - ADAM_B1 ** ADAM_STEP
    inv_b2 = 1.0 - ADAM_B2 ** ADAM_STEP

    def body(g_ref, w_ref, m_ref, v_ref, go_ref, d_ref, mo_ref, vo_ref):
        g = g_ref[0]
        for dev in range(1, N_DEV):
            g = g + g_ref[dev]
        go_ref[...] = g
        mn = ADAM_B1 * m_ref[...] + (1.0 - ADAM_B1) * g
        vn = ADAM_B2 * v_ref[...] + (1.0 - ADAM_B2) * (g * g)
        mo_ref[...] = mn
        vo_ref[...] = vn
        d_ref[...] = -ADAM_LR * ((mn / inv_b1) / (jnp.sqrt(vn / inv_b2) + ADAM_EPS) + ADAM_WD * w_ref[...])

    blk = pl.BlockSpec((tr, c), lambda i: (i, 0))
    sh = jax.ShapeDtypeStruct((r, c), F32)
    return pl.pallas_call(
        body, grid=(r // tr,),
        in_specs=[pl.BlockSpec((N_DEV, tr, c), lambda i: (0, i, 0)), blk, blk, blk],
        out_specs=[blk] * 4, out_shape=[sh] * 4,
        compiler_params=_cparams(1), name=name)(gstack, w, m, v)


_ANY = pl.BlockSpec(memory_space=pl.ANY)
_MESH = pl.DeviceIdType.MESH


def _allgather(x, *, name):
    def body(x_ref, out_ref, send_sems, recv_sems, local_sem):
        mx, my, mc = lax.axis_index("x"), lax.axis_index("y"), lax.axis_index("c")
        me, sibling = (mx, my, mc), (mx, my, 1 - mc)
        chips = [(1 - mx, my), (mx, 1 - my), (1 - mx, 1 - my)]

        def slot(px, py, pc):
            return out_ref.at[4 * px + 2 * py + pc]

        def copy(k, block, to, src=None):
            return pltpu.make_async_remote_copy(
                src_ref=slot(*block) if src is None else src, dst_ref=slot(*block),
                send_sem=send_sems.at[k], recv_sem=recv_sems.at[k], device_id=to, device_id_type=_MESH)

        mine = pltpu.make_async_copy(x_ref, slot(*me), local_sem)
        mine.start()
        first = [copy(0, me, sibling, src=x_ref)]
        first += [copy(1 + j, me, (*chip, mc), src=x_ref) for j, chip in enumerate(chips)]
        for cp in first:
            cp.start()
        passed = [copy(4 + j, (*chip, mc), sibling) for j, chip in enumerate(chips)]
        for j, chip in enumerate(chips):
            copy(1 + j, (*chip, mc), me).wait_recv()
            passed[j].start()
        copy(0, sibling, me).wait_recv()
        for j, chip in enumerate(chips):
            copy(4 + j, (*chip, 1 - mc), me).wait_recv()
        for cp in first + passed:
            cp.wait_send()
        mine.wait()

    return pl.pallas_call(
        body, in_specs=[_ANY], out_specs=_ANY,
        out_shape=jax.ShapeDtypeStruct((N_DEV,) + x.shape, x.dtype),
        scratch_shapes=[pltpu.SemaphoreType.DMA((7,)), pltpu.SemaphoreType.DMA((7,)), pltpu.SemaphoreType.DMA(())],
        name=name)(x)


def _alltoall(x, *, name):
    def body(x_ref, out_ref, send_sems, recv_sems, local_sem):
        mx, my, mc = lax.axis_index("x"), lax.axis_index("y"), lax.axis_index("c")
        my_id = 4 * mx + 2 * my + mc
        mine = pltpu.make_async_copy(x_ref.at[my_id], out_ref.at[my_id], local_sem)
        mine.start()
        copies = []
        for k in range(1, N_DEV):
            px = mx ^ ((k >> 2) & 1)
            py = my ^ ((k >> 1) & 1)
            pc = mc ^ (k & 1)
            copies.append(pltpu.make_async_remote_copy(
                src_ref=x_ref.at[4 * px + 2 * py + pc], dst_ref=out_ref.at[my_id],
                send_sem=send_sems.at[k - 1], recv_sem=recv_sems.at[k - 1],
                device_id=(px, py, pc), device_id_type=_MESH))
        for cp in copies:
            cp.start()
        for cp in copies:
            cp.wait_recv()
        for cp in copies:
            cp.wait_send()
        mine.wait()

    return pl.pallas_call(
        body, in_specs=[_ANY], out_specs=_ANY, out_shape=jax.ShapeDtypeStruct(x.shape, x.dtype),
        scratch_shapes=[pltpu.SemaphoreType.DMA((7,)), pltpu.SemaphoreType.DMA((7,)), pltpu.SemaphoreType.DMA(())],
        name=name)(x)


def _pack_rows(n):
    rows = -(-n // PACK_COLS)
    return -(-rows // PACK_ROW_BLOCK) * PACK_ROW_BLOCK


def _pack(arrs, dtype):
    flat = jnp.concatenate([a.reshape(-1).astype(dtype) for a in arrs])
    rows = _pack_rows(flat.shape[0])
    flat = jnp.pad(flat, (0, rows * PACK_COLS - flat.shape[0]))
    return flat.reshape(rows, PACK_COLS)


def _pack_lead(arrs, dtype):
    flat = jnp.concatenate([a.reshape(N_DEV, -1).astype(dtype) for a in arrs], axis=1)
    rows = _pack_rows(flat.shape[1])
    flat = jnp.pad(flat, ((0, 0), (0, rows * PACK_COLS - flat.shape[1])))
    return flat.reshape(N_DEV, rows, PACK_COLS)


def _unpack(packed, shapes, lead=False):
    flat = packed.reshape(N_DEV, -1) if lead else packed.reshape(-1)
    out, pos = [], 0
    for sh in shapes:
        n = math.prod(sh)
        out.append(flat[:, pos:pos + n].reshape((N_DEV,) + tuple(sh)) if lead else flat[pos:pos + n].reshape(sh))
        pos += n
    return out


def _join_shards(stacked, axis):
    return jnp.concatenate([stacked[d] for d in range(N_DEV)], axis=axis)


def _split_shards(full, axis):
    return jnp.stack(jnp.split(full, N_DEV, axis=axis), axis=0)


def _row(v):
    return v.reshape(1, -1)


def _pad_rows(w, rows):
    return jnp.pad(w, ((0, rows - w.shape[0]), (0, 0)))


def _gn_avg_matrix():
    gsz = D_GROUP // GN_GROUPS
    grp = jnp.arange(D_GROUP) // gsz
    return (grp[:, None] == grp[None, :]).astype(F32) / gsz


def _layer_params(p, l):
    q = {}
    (a, wb, wc), q["s5_vjp"] = jax.vjp(_s5_param_map, p["s5_lam_re"][l], p["s5_lam_im"][l], p["s5_log_dt"][l],
                                       p["s5_b_re"][l], p["s5_b_im"][l], p["s5_c_re"][l], p["s5_c_im"][l])
    q["wb"], q["wc"] = wb.astype(BF16), wc.astype(BF16)
    q["apow"] = _s5_apow(a, max(S5_TILE.bit_length() - 1, 1))
    (q["wr"], q["wi"]), q["lru_w_vjp"] = jax.vjp(lambda r, i: (_blockdiag(r), _blockdiag(i)), p["lru_w_r"][l], p["lru_w_i"][l])
    q["wr"], q["wi"] = q["wr"].astype(BF16), q["wi"].astype(BF16)
    q["sp"], q["sp_vjp"] = jax.vjp(lambda lam: _row(jax.nn.softplus(-lam)), p["lru_lam"][l])
    return q


def _local_step(x, mem, target, p):
    grads = {}
    mavg = _gn_avg_matrix()
    saved = []

    xs = _ln_fwd(x, _row(p["ln_in_g"]), _row(p["ln_in_b"]), name="ln_in_fwd")
    for l in range(DEPTH):
        q = _layer_params(p, l)
        n = f"l{l}_"
        hin = _mm(xs, p["w_in"][l], bias=_row(p["b_in"][l]), name=n + "inproj")
        s5_out, s5_y1, s5_hb = _s5_fwd(hin, q["wb"], q["apow"], q["wc"], _row(p["s5_d"][l]), p["s5_w_glu"][l],
                                       _row(p["s5_b_glu"][l]), name=n + "s5_fwd")
        cvw = _pad_rows(p["cv_w"][l], CV_PAD)
        cv_out, cv_c = _cv_fwd(hin, cvw, _row(p["cv_b"][l]), _row(p["cv_gn_g"][l]), _row(p["cv_gn_b"][l]), mavg,
                               p["cv_w_pw"][l], _row(p["cv_b_pw"][l]), name=n + "cv_fwd")
        lcw = _pad_rows(p["lru_conv_w"][l], SUBLANE)
        lru_out, lru_xc, lru_h = _lru_fwd(hin, lcw, _row(p["lru_conv_b"][l]), q["wr"], _row(p["lru_b_r"][l]), q["wi"],
                                          _row(p["lru_b_i"][l]), q["sp"], name=n + "lru_fwd")
        kv = _mm(mem, p["attn_w_kv"][l], name=n + "kv")
        (kbig, vbig), kv_vjp = jax.vjp(_attn_big, kv)
        kbig, vbig = kbig.astype(BF16), vbig.astype(BF16)
        at_out = _attn_fwd(hin, kbig, vbig, name=n + "attn_fwd")
        mix = jnp.concatenate([s5_out, cv_out, lru_out, at_out], axis=1)
        r1 = _mm(mix, p["w_out"][l], bias=_row(p["b_out"][l]), res=xs, res_scale=ALPHA, name=n + "outproj")
        x1 = _ln_fwd(r1, _row(p["ln1_g"][l]), _row(p["ln1_b"][l]), name=n + "ln1_fwd")
        u = _mm(x1, p["ffn_w_up"][l], name=n + "ffn_up")
        fcw = _pad_rows(p["ffn_conv_w"][l], SUBLANE)
        fcb = _row(p["ffn_conv_b"][l])
        hff = _ffn_gate_fwd(u, fcw, fcb, name=n + "ffn_gate_fwd")
        r2 = _mm(hff, p["ffn_w_down"][l], res=x1, res_scale=ALPHA, name=n + "ffn_down")
        x2 = _ln_fwd(r2, _row(p["ln2_g"][l]), _row(p["ln2_b"][l]), name=n + "ln2_fwd")
        saved.append(dict(q=q, xs=xs, hin=hin, s5_y1=s5_y1, s5_hb=s5_hb, cvw=cvw, cv_c=cv_c, lcw=lcw, lru_xc=lru_xc,
                          lru_h=lru_h, kbig=kbig, vbig=vbig, kv_vjp=kv_vjp, mix=mix, r1=r1, x1=x1, u=u, fcw=fcw,
                          fcb=fcb, hff=hff, r2=r2))
        xs = x2

    dx, loss_blk = _loss_grad(xs, target, name="loss_grad")
    loss = loss_blk[0, 0]

    for l in reversed(range(DEPTH)):
        sv = saved[l]
        q = sv["q"]
        n = f"l{l}_"
        g = {}
        dr2, g["ln2_g"], g["ln2_b"], _ = _ln_bwd(sv["r2"], dx, _row(p["ln2_g"][l]), name=n + "ln2_bwd")
        g["ffn_w_down"] = _mm_tn(sv["hff"], dr2, name=n + "dw_down")
        dhff = _mm(dr2, p["ffn_w_down"][l], trans_b=True, name=n + "dhff")
        duv, dug, dfwv, dfwg, dfbv, dfbg = _ffn_gate_bwd(sv["u"], dhff, sv["fcw"], sv["fcb"], name=n + "ffn_gate_bwd")
        du = jnp.concatenate([duv, dug], axis=1)
        g["ffn_conv_w"] = jnp.concatenate([dfwv, dfwg], axis=1)[:FFN_CONV_WIDTH]
        g["ffn_conv_b"] = jnp.concatenate([dfbv, dfbg], axis=1)
        g["ffn_w_up"] = _mm_tn(sv["x1"], du, name=n + "dw_up")
        dx1 = _mm(du, p["ffn_w_up"][l], trans_b=True, res=dr2, res_scale=ALPHA, name=n + "dx1")
        dr1, g["ln1_g"], g["ln1_b"], g["b_out"] = _ln_bwd(sv["r1"], dx1, _row(p["ln1_g"][l]), name=n + "ln1_bwd")
        g["w_out"] = _mm_tn(sv["mix"], dr1, name=n + "dw_out")
        dmix = _mm(dr1, p["w_out"][l], trans_b=True, name=n + "dmix")

        hin = sv["hin"]
        du_s5, g["s5_w_glu"], dwc, dwb, g["s5_b_glu"], g["s5_d"], da = _s5_bwd(
            hin, sv["s5_y1"], dmix, sv["s5_hb"], q["wb"], q["apow"], q["wc"], _row(p["s5_d"][l]), p["s5_w_glu"][l],
            _row(p["s5_b_glu"][l]), name=n + "s5_bwd")
        (g["s5_lam_re"], g["s5_lam_im"], g["s5_log_dt"], g["s5_b_re"], g["s5_b_im"], g["s5_c_re"],
         g["s5_c_im"]) = q["s5_vjp"]((da, dwb, dwc))
        dv, dg, g["cv_w_pw"], dcw, g["cv_b_pw"], g["cv_gn_g"], g["cv_gn_b"], g["cv_b"] = _cv_bwd(
            hin, sv["cv_c"], dmix, sv["cvw"], _row(p["cv_gn_g"][l]), _row(p["cv_gn_b"][l]), mavg, p["cv_w_pw"][l],
            name=n + "cv_bwd")
        g["cv_w"] = dcw[:CONV_WIDTH]
        dxg, dxr, dwr, dwi, dlcw, g["lru_b_r"], g["lru_b_i"], dsp, g["lru_conv_b"] = _lru_bwd(
            hin, sv["lru_xc"], sv["lru_h"], dmix, sv["lcw"], q["wr"], _row(p["lru_b_r"][l]), q["wi"],
            _row(p["lru_b_i"][l]), q["sp"], name=n + "lru_bwd")
        g["lru_conv_w"] = dlcw[:LRU_CONV_WIDTH]
        g["lru_w_r"], g["lru_w_i"] = q["lru_w_vjp"]((dwr, dwi))
        (g["lru_lam"],) = q["sp_vjp"](dsp)
        dq, dkbig, dvbig = _attn_bwd(hin, dmix, sv["kbig"], sv["vbig"], name=n + "attn_bwd")
        (dkv,) = sv["kv_vjp"]((dkbig, dvbig))
        g["attn_w_kv"] = _mm_tn(mem, dkv, name=n + "dw_kv")

        dh = jnp.concatenate([du_s5, dv, dg, dxg, dxr, dq], axis=1)
        g["b_in"] = _colsum_call(dh, name=n + "db_in")
        g["w_in"] = _mm_tn(sv["xs"], dh, name=n + "dw_in")
        dx = _mm(dh, p["w_in"][l], trans_b=True, res=dr1, res_scale=ALPHA, name=n + "dxs")
        for k, v in g.items():
            grads.setdefault(k, [None] * DEPTH)[l] = v.reshape(p[k].shape[1:])

    grad_x, dgi, dbi, _ = _ln_bwd(x, dx, _row(p["ln_in_g"]), name="ln_in_bwd")
    out = {k: jnp.stack(v, axis=0) for k, v in grads.items()}
    out["ln_in_g"], out["ln_in_b"] = dgi.reshape(-1), dbi.reshape(-1)
    return loss, grad_x, out


def kernel(x, mem, ln_in_g, ln_in_b, w_in, b_in, s5_lam_re, s5_lam_im, s5_log_dt, s5_b_re, s5_b_im, s5_c_re, s5_c_im, s5_d, s5_w_glu, s5_b_glu, cv_w, cv_b, cv_gn_g, cv_gn_b, cv_w_pw, cv_b_pw, lru_conv_w, lru_conv_b, lru_w_r, lru_b_r, lru_w_i, lru_b_i, lru_lam, attn_w_kv, w_out, b_out, ln1_g, ln1_b, ffn_w_up, ffn_conv_w, ffn_conv_b, ffn_w_down, ln2_g, ln2_b, loss_target, m_ln_in_g, m_ln_in_b, m_w_in, m_b_in, m_s5_lam_re, m_s5_lam_im, m_s5_log_dt, m_s5_b_re, m_s5_b_im, m_s5_c_re, m_s5_c_im, m_s5_d, m_s5_w_glu, m_s5_b_glu, m_cv_w, m_cv_b, m_cv_gn_g, m_cv_gn_b, m_cv_w_pw, m_cv_b_pw, m_lru_conv_w, m_lru_conv_b, m_lru_w_r, m_lru_b_r, m_lru_w_i, m_lru_b_i, m_lru_lam, m_attn_w_kv, m_w_out, m_b_out, m_ln1_g, m_ln1_b, m_ffn_w_up, m_ffn_conv_w, m_ffn_conv_b, m_ffn_w_down, m_ln2_g, m_ln2_b, v_ln_in_g, v_ln_in_b, v_w_in, v_b_in, v_s5_lam_re, v_s5_lam_im, v_s5_log_dt, v_s5_b_re, v_s5_b_im, v_s5_c_re, v_s5_c_im, v_s5_d, v_s5_w_glu, v_s5_b_glu, v_cv_w, v_cv_b, v_cv_gn_g, v_cv_gn_b, v_cv_w_pw, v_cv_b_pw, v_lru_conv_w, v_lru_conv_b, v_lru_w_r, v_lru_b_r, v_lru_w_i, v_lru_b_i, v_lru_lam, v_attn_w_kv, v_w_out, v_b_out, v_ln1_g, v_ln1_b, v_ffn_w_up, v_ffn_conv_w, v_ffn_conv_b, v_ffn_w_down, v_ln2_g, v_ln2_b):
    args = locals()
    w = {n: args[n] for n in WEIGHTS}
    mom = {n: args["m_" + n] for n in WEIGHTS}
    var = {n: args["v_" + n] for n in WEIGHTS}

    mm_names = [n for n in SHARDED_ORDER if n in MATMUL_WEIGHTS]
    tap_names = [n for n in SHARDED_ORDER if n in TAP_WEIGHTS]
    full = {n: w[n] for n in REPLICATED}
    for names, dtype, tag in ((mm_names, BF16, "ag_matmul_w"), (tap_names, F32, "ag_tap_w")):
        gathered = _allgather(_pack([w[n] for n in names], dtype), name=tag)
        for n, st in zip(names, _unpack(gathered, [w[n].shape for n in names], lead=True)):
            full[n] = _join_shards(st, SHARDED[n])

    loss, grad_x, g_full = _local_step(x[0], mem[0], loss_target[0], full)
    loss = lax.psum(loss, ("x", "y", "c"))

    sh_names = SHARDED_ORDER
    sent = _pack_lead([_split_shards(g_full[n], SHARDED[n]) for n in sh_names], F32)
    recv = _alltoall(sent, name="a2a_grads")
    sh_out = _adamw(recv, _pack([w[n] for n in sh_names], F32), _pack([mom[n] for n in sh_names], F32),
                    _pack([var[n] for n in sh_names], F32), name="adamw_sharded")
    rp_names = REPLICATED
    parts = _allgather(_pack([g_full[n] for n in rp_names], F32), name="ag_small_grads")
    rp_out = _adamw(parts, _pack([w[n] for n in rp_names], F32), _pack([mom[n] for n in rp_names], F32),
                    _pack([var[n] for n in rp_names], F32), name="adamw_replicated")

    res = [dict(), dict(), dict(), dict()]
    for names, packed in ((sh_names, sh_out), (rp_names, rp_out)):
        for kind in range(4):
            for n, a in zip(names, _unpack(packed[kind], [w[n].shape for n in names])):
                res[kind][n] = a
    return (loss, grad_x[None], *[res[0][n] for n in WEIGHTS], *[res[1][n] for n in WEIGHTS],
            *[res[2][n] for n in WEIGHTS], *[res[3][n] for n in WEIGHTS])
```

```python
import math

import jax
import jax.numpy as jnp
from jax import lax
from jax.experimental import pallas as pl
from jax.experimental.pallas import tpu as pltpu

F32 = jnp.float32
BF16 = jnp.bfloat16

D_MODEL = 1024
DEPTH = 2
D_GROUP = 256
N_IN_COLS = 6 * D_GROUP
S5_GROUPS = 16
S5_CH = 16
S5_STATE = 64
S5_LANES = S5_GROUPS * S5_STATE
CONV_WIDTH = 31
GN_GROUPS = 4
LRU_HEADS = 4
LRU_CONV_WIDTH = 4
LRU_C = 8.0
ATTN_HEADS = 4
ATTN_HEAD_DIM = 64
D_FF = 2816
FFN_CONV_WIDTH = 3
ALPHA = (2 * DEPTH) ** 0.25
LN_EPS = 1e-5
ADAM_LR, ADAM_B1, ADAM_B2, ADAM_EPS, ADAM_WD, ADAM_STEP = 0.001, 0.9, 0.999, 1e-08, 0.01, 10

N_DEV = 8
N_PEERS = N_DEV - 1
LANE = 128
SUBLANE = 8
VMEM_LIMIT = 56 * 1024 * 1024
PACK_COLS = 1024
PACK_ROW_BLOCK = 256

SHARDED = {
    "w_in": 2, "s5_w_glu": 1, "cv_w": 2, "cv_w_pw": 1, "lru_conv_w": 2, "attn_w_kv": 1,
    "w_out": 1, "ffn_w_up": 2, "ffn_conv_w": 2, "ffn_w_down": 1,
}
BIG = ("w_in", "attn_w_kv", "w_out", "ffn_w_up", "ffn_w_down")
SMALL_SHARDED = ("s5_w_glu", "cv_w", "cv_w_pw", "lru_conv_w", "ffn_conv_w")
MATMUL_WEIGHTS = ("w_in", "s5_w_glu", "cv_w_pw", "attn_w_kv", "w_out", "ffn_w_up", "ffn_w_down")
WEIGHTS = ['ln_in_g', 'ln_in_b', 'w_in', 'b_in', 's5_lam_re', 's5_lam_im', 's5_log_dt', 's5_b_re', 's5_b_im',
           's5_c_re', 's5_c_im', 's5_d', 's5_w_glu', 's5_b_glu', 'cv_w', 'cv_b', 'cv_gn_g', 'cv_gn_b', 'cv_w_pw',
           'cv_b_pw', 'lru_conv_w', 'lru_conv_b', 'lru_w_r', 'lru_b_r', 'lru_w_i', 'lru_b_i', 'lru_lam',
           'attn_w_kv', 'w_out', 'b_out', 'ln1_g', 'ln1_b', 'ffn_w_up', 'ffn_conv_w', 'ffn_conv_b', 'ffn_w_down',
           'ln2_g', 'ln2_b']
REPLICATED = [n for n in WEIGHTS if n not in SHARDED]

COL_CV_V, COL_CV_G, COL_LRU_G, COL_LRU_X, COL_S5, COL_Q = range(6)
IN_PERM = (1, 2, 3, 4, 0, 5)
MIX_S5, MIX_CV, MIX_LRU, MIX_ATTN = range(4)


_ANY = pl.BlockSpec(memory_space=pl.ANY)
_MESH = pl.DeviceIdType.MESH


def _cparams(n_axes):
    return pltpu.CompilerParams(dimension_semantics=("arbitrary",) * n_axes, vmem_limit_bytes=VMEM_LIMIT)


def _pick(n, cap):
    if n <= cap:
        return n
    best = None
    for t in range(LANE, cap + 1, LANE):
        if n % t == 0:
            best = t
    assert best is not None, (n, cap)
    return best


def _pick_rows(n, cap):
    best = None
    for t in range(SUBLANE, min(n, cap) + 1, SUBLANE):
        if n % t == 0:
            best = t
    assert best is not None, (n, cap)
    return best


def _full_spec(arr):
    nd = arr.ndim
    return pl.BlockSpec(arr.shape, lambda *_: (0,) * nd)


def _dot(a, b):
    return lax.dot_general(a.astype(BF16), b.astype(BF16), (((1,), (0,)), ((), ())), preferred_element_type=F32)


def _dot_nt(a, b):
    return lax.dot_general(a.astype(BF16), b.astype(BF16), (((1,), (1,)), ((), ())), preferred_element_type=F32)


def _dot_tn(a, b):
    return lax.dot_general(a.astype(BF16), b.astype(BF16), (((0,), (0,)), ((), ())), preferred_element_type=F32)


def _dot_hi(a, b):
    return jnp.dot(a, b, precision=lax.Precision.HIGHEST, preferred_element_type=F32)


def _colsum(x):
    return jnp.sum(x, axis=0, keepdims=True)


def _sigmoid(x):
    return 1.0 / (1.0 + jnp.exp(-x))


_GELU_K = math.sqrt(2.0 / math.pi)
_GELU_C = 0.044715


def _gelu(x):
    t = jnp.tanh(_GELU_K * (x + _GELU_C * x * x * x))
    return 0.5 * x * (1.0 + t)


def _gelu_and_grad(x):
    x2 = x * x
    t = jnp.tanh(_GELU_K * (x + _GELU_C * x2 * x))
    g = 0.5 * x * (1.0 + t)
    dg = 0.5 * (1.0 + t) + 0.5 * x * (1.0 - t * t) * (_GELU_K * (1.0 + 3.0 * _GELU_C * x2))
    return g, dg


def _neg_expm1(x):
    series = x * (1.0 + x * (0.5 + x * (1.0 / 6.0 + x * (1.0 / 24.0 + x * (1.0 / 120.0)))))
    return -jnp.where(jnp.abs(x) < 0.1, series, jnp.exp(x) - 1.0)


def _seq_tile(s, want):
    t = min(s, want)
    assert s % t == 0
    return t


def _block_mask(n_blocks, block_rows, block_cols):
    r = jnp.arange(n_blocks * block_rows) // block_rows
    c = jnp.arange(n_blocks * block_cols) // block_cols
    return (r[:, None] == c[None, :]).astype(F32)


def _mm(a, b, *, bias=None, res=None, res_scale=1.0, trans_b=False, out_dtype=F32, name):
    m, kdim = a.shape
    n = b.shape[0] if trans_b else b.shape[1]
    tm = _seq_tile(m, 512)
    tn = _pick(n, 1408)
    tk = _pick(kdim, 1536)
    nk = kdim // tk
    has_bias, has_res = bias is not None, res is not None

    def body(*refs):
        a_ref, b_ref = refs[0], refs[1]
        pos = 2
        bias_ref = res_ref = None
        if has_bias:
            bias_ref = refs[pos]
            pos += 1
        if has_res:
            res_ref = refs[pos]
            pos += 1
        o_ref, acc_ref = refs[pos], refs[pos + 1]
        k = pl.program_id(2)

        @pl.when(k == 0)
        def _():
            acc_ref[...] = jnp.zeros_like(acc_ref)

        if trans_b:
            acc_ref[...] += _dot_nt(a_ref[...], b_ref[...])
        else:
            acc_ref[...] += _dot(a_ref[...], b_ref[...])

        @pl.when(k == nk - 1)
        def _():
            r = acc_ref[...]
            if has_bias:
                r = r + bias_ref[...]
            if has_res:
                r = r + res_scale * res_ref[...]
            o_ref[...] = r.astype(out_dtype)

    ins = [a, b]
    in_specs = [pl.BlockSpec((tm, tk), lambda i, j, k: (i, k)),
                pl.BlockSpec((tn, tk), lambda i, j, k: (j, k)) if trans_b
                else pl.BlockSpec((tk, tn), lambda i, j, k: (k, j))]
    if has_bias:
        ins.append(bias)
        in_specs.append(pl.BlockSpec((1, tn), lambda i, j, k: (0, j)))
    if has_res:
        ins.append(res)
        in_specs.append(pl.BlockSpec((tm, tn), lambda i, j, k: (i, j)))
    return pl.pallas_call(
        body, grid=(m // tm, n // tn, nk), in_specs=in_specs,
        out_specs=pl.BlockSpec((tm, tn), lambda i, j, k: (i, j)),
        out_shape=jax.ShapeDtypeStruct((m, n), out_dtype),
        scratch_shapes=[pltpu.VMEM((tm, tn), F32)],
        compiler_params=_cparams(3), name=name)(*ins)


def _mm_tn(a, b, *, name, layer=0, n_layers=1, into=None):
    s, ka = a.shape
    nb = b.shape[1]
    ts = _seq_tile(s, 512)
    tka = _pick(ka, 1024)
    tnb = _pick(nb, 1408)

    def body(*refs):
        a_ref, b_ref, o_ref = refs[0], refs[1], refs[-1]

        @pl.when(pl.program_id(2) == 0)
        def _():
            o_ref[...] = jnp.zeros_like(o_ref)

        o_ref[...] += _dot_tn(a_ref[...], b_ref[...])

    ins = [a, b]
    in_specs = [pl.BlockSpec((ts, tka), lambda i, j, k: (k, i)), pl.BlockSpec((ts, tnb), lambda i, j, k: (k, j))]
    aliases = {}
    if into is not None:
        ins.append(into)
        in_specs.append(_ANY)
        aliases = {2: 0}
    return pl.pallas_call(
        body, grid=(ka // tka, nb // tnb, s // ts), in_specs=in_specs,
        out_specs=pl.BlockSpec((None, tka, tnb), lambda i, j, k: (layer, i, j)),
        out_shape=jax.ShapeDtypeStruct((n_layers, ka, nb), F32), input_output_aliases=aliases,
        compiler_params=_cparams(3), name=name)(*ins)


def _colsum_call(x, *, name):
    s, n = x.shape
    ts = _seq_tile(s, 512)

    def body(x_ref, o_ref):
        @pl.when(pl.program_id(0) == 0)
        def _():
            o_ref[...] = jnp.zeros_like(o_ref)

        o_ref[...] += _colsum(x_ref[...])

    return pl.pallas_call(
        body, grid=(s // ts,), in_specs=[pl.BlockSpec((ts, n), lambda i: (i, 0))],
        out_specs=pl.BlockSpec((1, n), lambda i: (0, 0)), out_shape=jax.ShapeDtypeStruct((1, n), F32),
        compiler_params=_cparams(1), name=name)(x)


def _ln_fwd(r, g, b, *, name):
    s, d = r.shape
    ts = _seq_tile(s, 512)

    def body(r_ref, g_ref, b_ref, o_ref):
        x = r_ref[...]
        mu = jnp.mean(x, axis=1, keepdims=True)
        xc = x - mu
        var = jnp.mean(xc * xc, axis=1, keepdims=True)
        o_ref[...] = xc * lax.rsqrt(var + LN_EPS) * g_ref[...] + b_ref[...]

    return pl.pallas_call(
        body, grid=(s // ts,),
        in_specs=[pl.BlockSpec((ts, d), lambda i: (i, 0)), _full_spec(g), _full_spec(b)],
        out_specs=pl.BlockSpec((ts, d), lambda i: (i, 0)), out_shape=jax.ShapeDtypeStruct((s, d), F32),
        compiler_params=_cparams(1), name=name)(r, g, b)


def _ln_bwd(r, dy, g, *, name):
    s, d = r.shape
    ts = _seq_tile(s, 512)

    def body(r_ref, dy_ref, g_ref, dr_ref, dg_ref, db_ref, ds_ref):
        @pl.when(pl.program_id(0) == 0)
        def _():
            dg_ref[...] = jnp.zeros_like(dg_ref)
            db_ref[...] = jnp.zeros_like(db_ref)
            ds_ref[...] = jnp.zeros_like(ds_ref)

        x = r_ref[...]
        dy = dy_ref[...]
        mu = jnp.mean(x, axis=1, keepdims=True)
        xc = x - mu
        var = jnp.mean(xc * xc, axis=1, keepdims=True)
        rstd = lax.rsqrt(var + LN_EPS)
        xh = xc * rstd
        dxh = dy * g_ref[...]
        m1 = jnp.mean(dxh, axis=1, keepdims=True)
        m2 = jnp.mean(dxh * xh, axis=1, keepdims=True)
        dr = rstd * (dxh - m1 - xh * m2)
        dr_ref[...] = dr
        dg_ref[...] += _colsum(dy * xh)
        db_ref[...] += _colsum(dy)
        ds_ref[...] += _colsum(dr)

    vec = jax.ShapeDtypeStruct((1, d), F32)
    vspec = pl.BlockSpec((1, d), lambda i: (0, 0))
    return pl.pallas_call(
        body, grid=(s // ts,),
        in_specs=[pl.BlockSpec((ts, d), lambda i: (i, 0)), pl.BlockSpec((ts, d), lambda i: (i, 0)), _full_spec(g)],
        out_specs=[pl.BlockSpec((ts, d), lambda i: (i, 0)), vspec, vspec, vspec],
        out_shape=[jax.ShapeDtypeStruct((s, d), F32), vec, vec, vec],
        compiler_params=_cparams(1), name=name)(r, dy, g)


def _loss_grad(y, target, *, name):
    s, d = y.shape
    ts = _seq_tile(s, 512)

    def body(y_ref, t_ref, dy_ref, l_ref):
        @pl.when(pl.program_id(0) == 0)
        def _():
            l_ref[...] = jnp.zeros_like(l_ref)

        e = y_ref[...] - t_ref[...]
        dy_ref[...] = e * (1.0 / d)
        part = jnp.sum(jnp.sum(e * e, axis=1, keepdims=True), axis=0, keepdims=True) * (0.5 / d)
        l_ref[...] += jnp.broadcast_to(part, l_ref.shape)

    return pl.pallas_call(
        body, grid=(s // ts,),
        in_specs=[pl.BlockSpec((ts, d), lambda i: (i, 0)), pl.BlockSpec((ts, d), lambda i: (i, 0))],
        out_specs=[pl.BlockSpec((ts, d), lambda i: (i, 0)), pl.BlockSpec((SUBLANE, LANE), lambda i: (0, 0))],
        out_shape=[jax.ShapeDtypeStruct((s, d), F32), jax.ShapeDtypeStruct((SUBLANE, LANE), F32)],
        compiler_params=_cparams(1), name=name)(y, target)


SCAN_CHUNK = 32


def _cscan_levels(bufs, apow_ref, t, pad, *, reverse):
    half = bufs[0].shape[1] // 2
    ch = min(SCAN_CHUNK, t)
    nlev = t.bit_length() - 1
    assert (1 << nlev) == t
    for k in range(nlev):
        d = 1 << k
        src, dst = bufs[k % 2], bufs[(k + 1) % 2]

        def chunk(c, carry, src=src, dst=dst, d=d, k=k):
            ar = apow_ref[k:k + 1, :half]
            ai = apow_ref[k:k + 1, half:]
            if reverse:
                ai = -ai
            r0 = pl.multiple_of(c * ch, ch)
            cur = src[pl.ds(pad + r0, ch), :]
            if d >= SUBLANE:
                off = pad + d if reverse else pad - d
                sh = src[pl.ds(off + r0, ch), :]
            elif reverse:
                blk = src[pl.ds(pad + r0, ch + SUBLANE), :]
                sh = pltpu.roll(blk, ch + SUBLANE - d, axis=0)[:ch, :]
            else:
                blk = src[pl.ds(pad - SUBLANE + r0, ch + SUBLANE), :]
                sh = pltpu.roll(blk, d, axis=0)[SUBLANE:, :]
            sre, sim = sh[:, :half], sh[:, half:]
            dst[pl.ds(pad + r0, ch), :half] = cur[:, :half] + ar * sre - ai * sim
            dst[pl.ds(pad + r0, ch), half:] = cur[:, half:] + ar * sim + ai * sre
            return carry

        lax.fori_loop(0, t // ch, chunk, 0)
    return nlev % 2


def _rscan_levels(abufs, bbufs, t, pad, *, reverse):
    nlev = t.bit_length() - 1
    assert (1 << nlev) == t
    for k in range(nlev):
        d = 1 << k
        asrc, adst = abufs[k % 2], abufs[(k + 1) % 2]
        bsrc, bdst = bbufs[k % 2], bbufs[(k + 1) % 2]
        off = pad + d if reverse else pad - d
        a = asrc[pad:pad + t, :]
        bdst[pad:pad + t, :] = a * bsrc[off:off + t, :] + bsrc[pad:pad + t, :]
        if k < nlev - 1:
            adst[pad:pad + t, :] = a * asrc[off:off + t, :]
    return nlev % 2


S5_TILE = 256


def _s5_scan_forward(u_bf, wb_ref, apow_ref, state, bufs, t, pad):
    half = S5_LANES
    bufs[0][pad:pad + t, :] = _dot(u_bf, wb_ref[...])
    ar, ai = apow_ref[0:1, :half], apow_ref[0:1, half:]
    sr, si = state[:, :half], state[:, half:]
    bufs[0][pad:pad + 1, :half] += ar * sr - ai * si
    bufs[0][pad:pad + 1, half:] += ar * si + ai * sr
    return _cscan_levels(bufs, apow_ref, t, pad, reverse=False)


def _s5_fwd(h_in, wb, apow, wc, dvec, wglu, bglu, *, name):
    s = h_in.shape[0]
    t = _seq_tile(s, S5_TILE)
    pad = t // 2
    nt = s // t
    lanes2 = 2 * S5_LANES

    def body(u_ref, wb_ref, apow_ref, wc_ref, d_ref, wglu_ref, bglu_ref, out_ref, y1_ref, hb_ref, buf0, buf1, carry):
        bufs = (buf0, buf1)

        @pl.when(pl.program_id(0) == 0)
        def _():
            buf0[0:pad, :] = jnp.zeros((pad, lanes2), F32)
            buf1[0:pad, :] = jnp.zeros((pad, lanes2), F32)
            carry[...] = jnp.zeros_like(carry)

        u = u_ref[...]
        state = carry[0:1, :]
        hb_ref[0] = state
        fin = _s5_scan_forward(u.astype(BF16), wb_ref, apow_ref, state, bufs, t, pad)
        hbuf = bufs[fin]
        carry[0:1, :] = hbuf[pad + t - 1:pad + t, :]
        y1 = _dot(hbuf[pad:pad + t, :], wc_ref[...]) + d_ref[...] * u
        y1_ref[...] = y1
        y2 = _gelu(y1)
        z = _dot(y2, wglu_ref[...]) + bglu_ref[...]
        out_ref[...] = y2 * _sigmoid(z)

    ins = [h_in, wb, apow, wc, dvec, wglu, bglu]
    in_specs = [pl.BlockSpec((t, D_GROUP), lambda i: (i, COL_S5))] + [_full_spec(a) for a in ins[1:]]
    return pl.pallas_call(
        body, grid=(nt,), in_specs=in_specs,
        out_specs=[pl.BlockSpec((t, D_GROUP), lambda i: (i, MIX_S5)), pl.BlockSpec((t, D_GROUP), lambda i: (i, 0)),
                   pl.BlockSpec((1, 1, lanes2), lambda i: (i, 0, 0))],
        out_shape=[jax.ShapeDtypeStruct((s, D_MODEL), F32), jax.ShapeDtypeStruct((s, D_GROUP), F32),
                   jax.ShapeDtypeStruct((nt, 1, lanes2), F32)],
        scratch_shapes=[pltpu.VMEM((pad + t, lanes2), F32), pltpu.VMEM((pad + t, lanes2), F32),
                        pltpu.VMEM((SUBLANE, lanes2), F32)],
        compiler_params=_cparams(1), name=name)(*ins)


def _s5_bwd(h_in, y1, dmix, hb, wb, apow, wc, dvec, wglu, bglu, dh_all, *, name):
    s = h_in.shape[0]
    t = _seq_tile(s, S5_TILE)
    pad = t // 2
    nt = s // t
    half = S5_LANES
    lanes2 = 2 * half
    rows = pad + t + pad

    def body(u_ref, y1_ref, do_ref, hb_ref, wb_ref, apow_ref, wc_ref, d_ref, wglu_ref, bglu_ref, _dh_in,
             du_ref, dwglu_ref, dwc_ref, dwb_ref, dbglu_ref, dd_ref, da_ref, buf0, buf1, buf2, buf3, carry):
        @pl.when(pl.program_id(0) == 0)
        def _():
            for bf in (buf0, buf1, buf2, buf3):
                bf[0:pad, :] = jnp.zeros((pad, lanes2), F32)
                bf[pad + t:rows, :] = jnp.zeros((pad, lanes2), F32)
            carry[...] = jnp.zeros_like(carry)
            for r in (dwglu_ref, dwc_ref, dwb_ref, dbglu_ref, dd_ref, da_ref):
                r[...] = jnp.zeros_like(r)

        u = u_ref[...]
        u_bf = u.astype(BF16)
        state = hb_ref[0]
        hfin = _s5_scan_forward(u_bf, wb_ref, apow_ref, state, (buf0, buf1), t, pad)
        hbuf = (buf0, buf1)[hfin]
        h_bf = hbuf[pad:pad + t, :].astype(BF16)

        y1 = y1_ref[...]
        dout = do_ref[...]
        y2, dgelu = _gelu_and_grad(y1)
        sg = _sigmoid(_dot(y2, wglu_ref[...]) + bglu_ref[...])
        dz = dout * y2 * sg * (1.0 - sg)
        dy2 = dout * sg + _dot_nt(dz, wglu_ref[...])
        dwglu_ref[...] += _dot_tn(y2, dz)
        dbglu_ref[...] += _colsum(dz)
        dy1 = dy2 * dgelu
        dd_ref[...] += _colsum(dy1 * u)
        dy1_bf = dy1.astype(BF16)
        dwc_ref[...] += _dot_tn(h_bf, dy1_bf)

        buf2[pad:pad + t, :] = _dot_nt(dy1_bf, wc_ref[...])
        ar, ai = apow_ref[0:1, :half], apow_ref[0:1, half:]
        cr, ci = carry[0:1, :half], carry[0:1, half:]
        buf2[pad + t - 1:pad + t, :half] += ar * cr + ai * ci
        buf2[pad + t - 1:pad + t, half:] += ar * ci - ai * cr
        lfin = _cscan_levels((buf2, buf3), apow_ref, t, pad, reverse=True)
        lbuf = (buf2, buf3)[lfin]
        lam = lbuf[pad:pad + t, :]
        carry[0:1, :] = lbuf[pad:pad + 1, :]
        lam_bf = lam.astype(BF16)
        du_ref[...] = dy1 * d_ref[...] + _dot_nt(lam_bf, wb_ref[...])
        dwb_ref[...] += _dot_tn(u_bf, lam_bf)

        hbuf[pad - 1:pad, :] = state
        hp = hbuf[pad - 1:pad - 1 + t, :]
        hbuf[pad - 1:pad, :] = jnp.zeros((1, lanes2), F32)
        lre, lim = lam[:, :half], lam[:, half:]
        hre, him = hp[:, :half], hp[:, half:]
        da_ref[:, :half] += _colsum(lre * hre + lim * him)
        da_ref[:, half:] += _colsum(lim * hre - lre * him)

    def rev(col):
        return lambda i: (nt - 1 - i, col)

    ins = [h_in, y1, dmix, hb, wb, apow, wc, dvec, wglu, bglu, dh_all]
    in_specs = [pl.BlockSpec((t, D_GROUP), rev(COL_S5)), pl.BlockSpec((t, D_GROUP), rev(0)),
                pl.BlockSpec((t, D_GROUP), rev(MIX_S5)), pl.BlockSpec((1, 1, lanes2), lambda i: (nt - 1 - i, 0, 0))] + \
               [_full_spec(a) for a in ins[4:10]] + [_ANY]
    outs = [jax.ShapeDtypeStruct((s, N_IN_COLS), F32), jax.ShapeDtypeStruct((D_GROUP, D_GROUP), F32),
            jax.ShapeDtypeStruct((lanes2, D_GROUP), F32), jax.ShapeDtypeStruct((D_GROUP, lanes2), F32),
            jax.ShapeDtypeStruct((1, D_GROUP), F32), jax.ShapeDtypeStruct((1, D_GROUP), F32),
            jax.ShapeDtypeStruct((1, lanes2), F32)]
    out_specs = [pl.BlockSpec((t, D_GROUP), rev(COL_S5))] + [_full_spec(o) for o in outs[1:]]
    return pl.pallas_call(
        body, grid=(nt,), in_specs=in_specs, out_specs=out_specs, out_shape=outs, input_output_aliases={10: 0},
        scratch_shapes=[pltpu.VMEM((rows, lanes2), F32) for _ in range(4)] + [pltpu.VMEM((SUBLANE, lanes2), F32)],
        compiler_params=_cparams(1), name=name)(*ins)


def _s5_param_map(lam_re, lam_im, log_dt, b_re, b_im, c_re, c_im):
    dt = jnp.exp(log_dt)[:, None]
    er = jnp.exp(lam_re * dt)
    a_re, a_im = er * jnp.cos(lam_im * dt), er * jnp.sin(lam_im * dt)
    den = lam_re * lam_re + lam_im * lam_im
    n_re = a_re - 1.0
    k_re = (n_re * lam_re + a_im * lam_im) / den
    k_im = (a_im * lam_re - n_re * lam_im) / den
    bb_re = k_re[..., None] * b_re - k_im[..., None] * b_im
    bb_im = k_re[..., None] * b_im + k_im[..., None] * b_re
    mask_in = _block_mask(S5_GROUPS, S5_CH, S5_STATE)
    mask_out = _block_mask(S5_GROUPS, S5_STATE, S5_CH)

    def blockdiag_in(m):
        return jnp.tile(jnp.transpose(m, (0, 2, 1)).reshape(S5_GROUPS * S5_CH, S5_STATE), (1, S5_GROUPS)) * mask_in

    def blockdiag_out(m):
        return jnp.tile(jnp.transpose(m, (0, 2, 1)).reshape(S5_LANES, S5_CH), (1, S5_GROUPS)) * mask_out

    a = jnp.concatenate([a_re.reshape(1, -1), a_im.reshape(1, -1)], axis=1)
    wb = jnp.concatenate([blockdiag_in(bb_re), blockdiag_in(bb_im)], axis=1)
    wc = jnp.concatenate([blockdiag_out(c_re), -blockdiag_out(c_im)], axis=0)
    return a, wb, wc


def _s5_apow(a, nlev):
    half = S5_LANES
    re, im = a[:, :half], a[:, half:]
    rows = []
    for _ in range(nlev):
        rows.append(jnp.concatenate([re, im], axis=1))
        re, im = re * re - im * im, 2.0 * re * im
    n_rows = -(-nlev // SUBLANE) * SUBLANE
    rows += [jnp.zeros_like(rows[0])] * (n_rows - nlev)
    return lax.stop_gradient(jnp.concatenate(rows, axis=0))


CV_TILE = 256
CV_PAD = 32
CV_CHUNK = 64


def _gn_stats(c, mavg):
    mu = _dot_hi(c, mavg)
    cen = c - mu
    var = _dot_hi(cen * cen, mavg)
    rstd = lax.rsqrt(var + LN_EPS)
    return cen * rstd, rstd


def _cv_fwd(h_in, cw, cb, gng, gnb, mavg, wpw, bpw, mix, *, name):
    s = h_in.shape[0]
    t = _seq_tile(s, CV_TILE)
    ch = min(CV_CHUNK, t)

    def body(v_ref, g_ref, cw_ref, cb_ref, gng_ref, gnb_ref, mavg_ref, wpw_ref, bpw_ref, _mix_in, out_ref, c_ref, xpad):
        @pl.when(pl.program_id(0) == 0)
        def _():
            xpad[0:CV_PAD, :] = jnp.zeros((CV_PAD, D_GROUP), F32)

        xpad[CV_PAD:CV_PAD + t, :] = v_ref[...] * _sigmoid(g_ref[...])
        for r0 in range(0, t, ch):
            acc = jnp.broadcast_to(cb_ref[...], (ch, D_GROUP))
            for k in range(CONV_WIDTH):
                o = CV_PAD - (CONV_WIDTH - 1) + k + r0
                acc = acc + cw_ref[k:k + 1, :] * xpad[o:o + ch, :]
            c_ref[r0:r0 + ch, :] = acc
        xpad[0:CV_PAD, :] = xpad[t:t + CV_PAD, :]
        xn, _ = _gn_stats(c_ref[...], mavg_ref[...])
        gn = xn * gng_ref[...] + gnb_ref[...]
        out_ref[...] = _dot(gn * _sigmoid(gn), wpw_ref[...]) + bpw_ref[...]

    ins = [h_in, h_in, cw, cb, gng, gnb, mavg, wpw, bpw, mix]
    in_specs = [pl.BlockSpec((t, D_GROUP), lambda i: (i, COL_CV_V)), pl.BlockSpec((t, D_GROUP), lambda i: (i, COL_CV_G))] + \
               [_full_spec(a) for a in ins[2:9]] + [_ANY]
    return pl.pallas_call(
        body, grid=(s // t,), in_specs=in_specs,
        out_specs=[pl.BlockSpec((t, D_GROUP), lambda i: (i, MIX_CV)), pl.BlockSpec((t, D_GROUP), lambda i: (i, 0))],
        out_shape=[jax.ShapeDtypeStruct((s, D_MODEL), F32), jax.ShapeDtypeStruct((s, D_GROUP), F32)],
        input_output_aliases={9: 0},
        scratch_shapes=[pltpu.VMEM((CV_PAD + t, D_GROUP), F32)],
        compiler_params=_cparams(1), name=name)(*ins)


def _cv_bwd(h_in, c, dmix, cw, gng, gnb, mavg, wpw, *, name):
    s = h_in.shape[0]
    t = _seq_tile(s, CV_TILE)
    nt = s // t
    ch = min(CV_CHUNK, t)

    def body(v_ref, g_ref, c_ref, do_ref, cw_ref, gng_ref, gnb_ref, mavg_ref, wpw_ref,
             dvg_ref, dwpw_ref, dcw_ref, dbpw_ref, dgg_ref, dgb_ref, dcb_ref, dcpad, hgbuf):
        @pl.when(pl.program_id(0) == 0)
        def _():
            dcpad[t:t + CV_PAD, :] = jnp.zeros((CV_PAD, D_GROUP), F32)
            for r in (dwpw_ref, dcw_ref, dbpw_ref, dgg_ref, dgb_ref, dcb_ref):
                r[...] = jnp.zeros_like(r)

        mavg = mavg_ref[...]
        xn, rstd = _gn_stats(c_ref[...], mavg)
        gg = gng_ref[...]
        gn = xn * gg + gnb_ref[...]
        sg = _sigmoid(gn)
        dout = do_ref[...]
        dwpw_ref[...] += _dot_tn(gn * sg, dout)
        dbpw_ref[...] += _colsum(dout)
        dgn = _dot_nt(dout, wpw_ref[...]) * (sg * (1.0 + gn * (1.0 - sg)))
        dgg_ref[...] += _colsum(dgn * xn)
        dgb_ref[...] += _colsum(dgn)
        dxn = dgn * gg
        dc = rstd * (dxn - _dot_hi(dxn, mavg) - xn * _dot_hi(dxn * xn, mavg))
        dcb_ref[...] += _colsum(dc)
        dcpad[0:t, :] = dc

        v = v_ref[...]
        sgm = _sigmoid(g_ref[...])
        hgbuf[...] = v * sgm
        for r0 in range(0, t, ch):
            hg = hgbuf[r0:r0 + ch, :]
            acc = jnp.zeros((ch, D_GROUP), F32)
            for k in range(CONV_WIDTH):
                o = (CONV_WIDTH - 1) - k + r0
                sh = dcpad[o:o + ch, :]
                acc = acc + cw_ref[k:k + 1, :] * sh
                dcw_ref[k:k + 1, :] += _colsum(hg * sh)
            hgbuf[r0:r0 + ch, :] = acc
        dcpad[t:t + CV_PAD, :] = dcpad[0:CV_PAD, :]
        dhg = hgbuf[...]
        dvg_ref[:, :D_GROUP] = dhg * sgm
        dvg_ref[:, D_GROUP:] = dhg * v * sgm * (1.0 - sgm)

    def rev(col):
        return lambda i: (nt - 1 - i, col)

    ins = [h_in, h_in, c, dmix, cw, gng, gnb, mavg, wpw]
    in_specs = [pl.BlockSpec((t, D_GROUP), rev(COL_CV_V)), pl.BlockSpec((t, D_GROUP), rev(COL_CV_G)),
                pl.BlockSpec((t, D_GROUP), rev(0)), pl.BlockSpec((t, D_GROUP), rev(MIX_CV))] + [_full_spec(a) for a in ins[4:]]
    vec = jax.ShapeDtypeStruct((1, D_GROUP), F32)
    outs = [jax.ShapeDtypeStruct((s, N_IN_COLS), F32),
            jax.ShapeDtypeStruct((D_GROUP, D_GROUP), F32), jax.ShapeDtypeStruct((CV_PAD, D_GROUP), F32), vec, vec, vec, vec]
    out_specs = [pl.BlockSpec((t, 2 * D_GROUP), rev(COL_CV_V // 2))] + [_full_spec(o) for o in outs[1:]]
    return pl.pallas_call(
        body, grid=(nt,), in_specs=in_specs, out_specs=out_specs, out_shape=outs,
        scratch_shapes=[pltpu.VMEM((t + CV_PAD, D_GROUP), F32), pltpu.VMEM((t, D_GROUP), F32)],
        compiler_params=_cparams(1), name=name)(*ins)


LRU_TILE = 256


def _lru_gates(xc, wr_ref, br_ref, wi_ref, bi_ref, sp_ref):
    r = _sigmoid(_dot(xc, wr_ref[...]) + br_ref[...])
    i = _sigmoid(_dot(xc, wi_ref[...]) + bi_ref[...])
    log_a = -LRU_C * r * sp_ref[...]
    a = jnp.exp(log_a)
    m = jnp.sqrt(_neg_expm1(2.0 * log_a))
    return r, i, a, m


def _lru_fwd(h_in, lcw, lcb, wr, br, wi, bi, sp, mix, *, name):
    s = h_in.shape[0]
    t = _seq_tile(s, LRU_TILE)
    pad = max(t // 2, SUBLANE)

    def body(xg_ref, xr_ref, lcw_ref, lcb_ref, wr_ref, br_ref, wi_ref, bi_ref, sp_ref, _mix_in,
             out_ref, xc_ref, h_ref, xpad, a0, a1, b0, b1, carry):
        @pl.when(pl.program_id(0) == 0)
        def _():
            xpad[0:SUBLANE, :] = jnp.zeros((SUBLANE, D_GROUP), F32)
            for bf in (a0, a1, b0, b1):
                bf[0:pad, :] = jnp.zeros((pad, D_GROUP), F32)
            carry[...] = jnp.zeros_like(carry)

        xpad[SUBLANE:SUBLANE + t, :] = xr_ref[...]
        xc = jnp.broadcast_to(lcb_ref[...], (t, D_GROUP))
        for k in range(LRU_CONV_WIDTH):
            o = SUBLANE - (LRU_CONV_WIDTH - 1) + k
            xc = xc + lcw_ref[k:k + 1, :] * xpad[o:o + t, :]
        xpad[0:SUBLANE, :] = xpad[t:t + SUBLANE, :]
        xc_ref[...] = xc
        _, i, a, m = _lru_gates(xc, wr_ref, br_ref, wi_ref, bi_ref, sp_ref)
        a0[pad:pad + t, :] = a
        b0[pad:pad + t, :] = m * (i * xc)
        b0[pad:pad + 1, :] += a0[pad:pad + 1, :] * carry[0:1, :]
        fin = _rscan_levels((a0, a1), (b0, b1), t, pad, reverse=False)
        hbuf = (b0, b1)[fin]
        carry[0:1, :] = hbuf[pad + t - 1:pad + t, :]
        h = hbuf[pad:pad + t, :]
        h_ref[...] = h
        out_ref[...] = h * _gelu(xg_ref[...])

    ins = [h_in, h_in, lcw, lcb, wr, br, wi, bi, sp, mix]
    row = pl.BlockSpec((t, D_GROUP), lambda i: (i, 0))
    in_specs = [pl.BlockSpec((t, D_GROUP), lambda i: (i, COL_LRU_G)), pl.BlockSpec((t, D_GROUP), lambda i: (i, COL_LRU_X))] + \
               [_full_spec(a) for a in ins[2:9]] + [_ANY]
    return pl.pallas_call(
        body, grid=(s // t,), in_specs=in_specs,
        out_specs=[pl.BlockSpec((t, D_GROUP), lambda i: (i, MIX_LRU)), row, row],
        out_shape=[jax.ShapeDtypeStruct((s, D_MODEL), F32)] + [jax.ShapeDtypeStruct((s, D_GROUP), F32)] * 2,
        input_output_aliases={9: 0},
        scratch_shapes=[pltpu.VMEM((SUBLANE + t, D_GROUP), F32)] + [pltpu.VMEM((pad + t, D_GROUP), F32)] * 4 +
                       [pltpu.VMEM((SUBLANE, D_GROUP), F32)],
        compiler_params=_cparams(1), name=name)(*ins)


def _lru_bwd(h_in, xc_all, h_all, dmix, lcw, wr, br, wi, bi, sp, dh_all, *, name):
    s = h_in.shape[0]
    t = _seq_tile(s, LRU_TILE)
    nt = s // t
    pad = max(t // 2, SUBLANE)
    tb = t // SUBLANE

    def body(xg_ref, xr_ref, xc_ref, h_ref, hprev_ref, do_ref, lcw_ref, wr_ref, br_ref, wi_ref, bi_ref, sp_ref, _dh_in,
             dgr_ref, dwr_ref, dwi_ref, dlcw_ref, dbr_ref, dbi_ref, dsp_ref, dlcb_ref,
             a0, a1, b0, b1, hp, dxpad, carry):
        pid = pl.program_id(0)

        @pl.when(pid == 0)
        def _():
            for bf in (a0, a1, b0, b1):
                bf[pad + t:pad + t + pad, :] = jnp.zeros((pad, D_GROUP), F32)
            dxpad[t:t + SUBLANE, :] = jnp.zeros((SUBLANE, D_GROUP), F32)
            carry[...] = jnp.zeros_like(carry)
            for r in (dwr_ref, dwi_ref, dlcw_ref, dbr_ref, dbi_ref, dsp_ref, dlcb_ref):
                r[...] = jnp.zeros_like(r)

        xc = xc_ref[...]
        h = h_ref[...]
        dout = do_ref[...]
        gate, dgate = _gelu_and_grad(xg_ref[...])
        dgr_ref[:, :D_GROUP] = dout * h * dgate
        r, i, a, m = _lru_gates(xc, wr_ref, br_ref, wi_ref, bi_ref, sp_ref)

        a0[pad:pad + t, :] = a
        b0[pad:pad + t, :] = dout * gate
        b0[pad + t - 1:pad + t, :] += carry[0:1, :]
        a1[pad:pad + t, :] = a0[pad + 1:pad + 1 + t, :]
        fin = _rscan_levels((a1, a0), (b0, b1), t, pad, reverse=True)
        lam = (b0, b1)[fin][pad:pad + t, :]
        carry[0:1, :] = a[0:1, :] * lam[0:1, :]

        is_first = pid == nt - 1
        hp[0:SUBLANE, :] = jnp.where(is_first, 0.0, hprev_ref[...])
        hp[SUBLANE:SUBLANE + t, :] = h
        hprev = hp[SUBLANE - 1:SUBLANE - 1 + t, :]

        ix = i * xc
        dmm = lam * ix
        dix = lam * m
        da = lam * hprev - dmm * (a / m)
        dlog_a = da * a
        dr = dlog_a * (-LRU_C * sp_ref[...])
        dsp_ref[...] += _colsum(dlog_a * (-LRU_C * r))
        dpr = dr * r * (1.0 - r)
        dpi = dix * xc * i * (1.0 - i)
        dbr_ref[...] += _colsum(dpr)
        dbi_ref[...] += _colsum(dpi)
        dwr_ref[...] += _dot_tn(xc, dpr)
        dwi_ref[...] += _dot_tn(xc, dpi)
        dxc = dix * i + _dot_nt(dpr, wr_ref[...]) + _dot_nt(dpi, wi_ref[...])
        dlcb_ref[...] += _colsum(dxc)

        dxpad[0:t, :] = dxc
        xr = xr_ref[...]
        dxr = jnp.zeros((t, D_GROUP), F32)
        for k in range(LRU_CONV_WIDTH):
            o = (LRU_CONV_WIDTH - 1) - k
            sh = dxpad[o:o + t, :]
            dxr = dxr + lcw_ref[k:k + 1, :] * sh
            dlcw_ref[k:k + 1, :] += _colsum(xr * sh)
        dxpad[t:t + SUBLANE, :] = dxpad[0:SUBLANE, :]
        dgr_ref[:, D_GROUP:] = dxr

    def rev(col):
        return lambda i: (nt - 1 - i, col)

    ins = [h_in, h_in, xc_all, h_all, h_all, dmix, lcw, wr, br, wi, bi, sp, dh_all]
    in_specs = [pl.BlockSpec((t, D_GROUP), rev(COL_LRU_G)), pl.BlockSpec((t, D_GROUP), rev(COL_LRU_X)),
                pl.BlockSpec((t, D_GROUP), rev(0)), pl.BlockSpec((t, D_GROUP), rev(0)),
                pl.BlockSpec((SUBLANE, D_GROUP), lambda i: (jnp.maximum((nt - 1 - i) * tb - 1, 0), 0)),
                pl.BlockSpec((t, D_GROUP), rev(MIX_LRU))] + [_full_spec(a) for a in ins[6:12]] + [_ANY]
    vec = jax.ShapeDtypeStruct((1, D_GROUP), F32)
    mat = jax.ShapeDtypeStruct((D_GROUP, D_GROUP), F32)
    outs = [jax.ShapeDtypeStruct((s, N_IN_COLS), F32), mat, mat, jax.ShapeDtypeStruct((SUBLANE, D_GROUP), F32),
            vec, vec, vec, vec]
    out_specs = [pl.BlockSpec((t, 2 * D_GROUP), rev(COL_LRU_G // 2))] + [_full_spec(o) for o in outs[1:]]
    return pl.pallas_call(
        body, grid=(nt,), in_specs=in_specs, out_specs=out_specs, out_shape=outs, input_output_aliases={12: 0},
        scratch_shapes=[pltpu.VMEM((pad + t + pad, D_GROUP), F32)] * 4 +
                       [pltpu.VMEM((SUBLANE + t, D_GROUP), F32), pltpu.VMEM((t + SUBLANE, D_GROUP), F32),
                        pltpu.VMEM((SUBLANE, D_GROUP), F32)],
        compiler_params=_cparams(1), name=name)(*ins)


def _blockdiag(w):
    h, d, _ = w.shape
    return jnp.tile(w.reshape(h * d, d), (1, h)) * _block_mask(h, d, d)


ATTN_TILE = 512
ATTN_SCALE = ATTN_HEAD_DIM ** -0.5


def _attn_big(kv):
    m = kv.shape[0]
    kbig = jnp.tile(kv[:, :D_GROUP].T, (1, ATTN_HEADS)) * _block_mask(ATTN_HEADS, ATTN_HEAD_DIM, m)
    vbig = jnp.tile(kv[:, D_GROUP:], (ATTN_HEADS, 1)) * _block_mask(ATTN_HEADS, m, ATTN_HEAD_DIM)
    return kbig, vbig


def _attn_probs(q, kbig_ref, m):
    sc = _dot(q, kbig_ref[...]) * ATTN_SCALE
    ps = []
    for h in range(ATTN_HEADS):
        sh = sc[:, h * m:(h + 1) * m]
        e = jnp.exp(sh - jnp.max(sh, axis=1, keepdims=True))
        ps.append(e / jnp.sum(e, axis=1, keepdims=True))
    return ps


def _attn_fwd(h_in, kbig, vbig, mix, *, name):
    s = h_in.shape[0]
    t = _seq_tile(s, ATTN_TILE)
    m = kbig.shape[1] // ATTN_HEADS

    def body(q_ref, kbig_ref, vbig_ref, _mix_in, o_ref):
        ps = _attn_probs(q_ref[...], kbig_ref, m)
        o_ref[...] = _dot(jnp.concatenate(ps, axis=1), vbig_ref[...])

    return pl.pallas_call(
        body, grid=(s // t,),
        in_specs=[pl.BlockSpec((t, D_GROUP), lambda i: (i, COL_Q)), _full_spec(kbig), _full_spec(vbig), _ANY],
        out_specs=pl.BlockSpec((t, D_GROUP), lambda i: (i, MIX_ATTN)),
        out_shape=jax.ShapeDtypeStruct((s, D_MODEL), F32), input_output_aliases={3: 0},
        compiler_params=_cparams(1), name=name)(h_in, kbig, vbig, mix)


def _attn_bwd(h_in, dmix, kbig, vbig, dh_all, *, name):
    s = h_in.shape[0]
    t = _seq_tile(s, ATTN_TILE)
    m = kbig.shape[1] // ATTN_HEADS

    def body(q_ref, do_ref, kbig_ref, vbig_ref, _dh_in, dq_ref, dk_ref, dv_ref):
        @pl.when(pl.program_id(0) == 0)
        def _():
            dk_ref[...] = jnp.zeros_like(dk_ref)
            dv_ref[...] = jnp.zeros_like(dv_ref)

        q = q_ref[...]
        dout = do_ref[...]
        ps = _attn_probs(q, kbig_ref, m)
        dp = _dot_nt(dout, vbig_ref[...])
        dss = []
        for h in range(ATTN_HEADS):
            dph = dp[:, h * m:(h + 1) * m]
            dss.append(ps[h] * (dph - jnp.sum(dph * ps[h], axis=1, keepdims=True)))
        ds = (jnp.concatenate(dss, axis=1) * ATTN_SCALE).astype(BF16)
        dv_ref[...] += _dot_tn(jnp.concatenate(ps, axis=1), dout)
        dq_ref[...] = _dot_nt(ds, kbig_ref[...])
        dk_ref[...] += _dot_tn(q, ds)

    outs = [jax.ShapeDtypeStruct((s, N_IN_COLS), F32), jax.ShapeDtypeStruct(kbig.shape, F32),
            jax.ShapeDtypeStruct(vbig.shape, F32)]
    return pl.pallas_call(
        body, grid=(s // t,),
        in_specs=[pl.BlockSpec((t, D_GROUP), lambda i: (i, COL_Q)), pl.BlockSpec((t, D_GROUP), lambda i: (i, MIX_ATTN)),
                  _full_spec(kbig), _full_spec(vbig), _ANY],
        out_specs=[pl.BlockSpec((t, D_GROUP), lambda i: (i, COL_Q)), _full_spec(outs[1]), _full_spec(outs[2])],
        out_shape=outs, input_output_aliases={4: 0},
        compiler_params=_cparams(1), name=name)(h_in, dmix, kbig, vbig, dh_all)


FFN_TILE = 128
FFN_COL_CHUNK = 256
FFN_ROW_CHUNK = 64


def _ffn_conv(pad_ref, w_ref, b_ref, r0, ch, c0):
    cc = FFN_COL_CHUNK
    acc = jnp.broadcast_to(b_ref[:, c0:c0 + cc], (ch, cc))
    for k in range(FFN_CONV_WIDTH):
        o = SUBLANE - (FFN_CONV_WIDTH - 1) + k + r0
        acc = acc + w_ref[k:k + 1, c0:c0 + cc] * pad_ref[o:o + ch, c0:c0 + cc]
    return acc


def _ffn_gate_fwd(u, fcw, fcb, *, name):
    s = u.shape[0]
    t = _seq_tile(s, FFN_TILE)
    ch = min(FFN_ROW_CHUNK, t)
    cc = FFN_COL_CHUNK

    def body(u_ref, w_ref, b_ref, o_ref, upad):
        @pl.when(pl.program_id(0) == 0)
        def _():
            upad[0:SUBLANE, :] = jnp.zeros((SUBLANE, 2 * D_FF), F32)

        upad[SUBLANE:SUBLANE + t, :] = u_ref[...]
        for c0 in range(0, D_FF, cc):
            for r0 in range(0, t, ch):
                val = _ffn_conv(upad, w_ref, b_ref, r0, ch, c0)
                gt = _ffn_conv(upad, w_ref, b_ref, r0, ch, c0 + D_FF)
                o_ref[r0:r0 + ch, c0:c0 + cc] = (val * _gelu(gt)).astype(BF16)
        upad[0:SUBLANE, :] = upad[t:t + SUBLANE, :]

    return pl.pallas_call(
        body, grid=(s // t,),
        in_specs=[pl.BlockSpec((t, 2 * D_FF), lambda i: (i, 0)), _full_spec(fcw), _full_spec(fcb)],
        out_specs=pl.BlockSpec((t, D_FF), lambda i: (i, 0)),
        out_shape=jax.ShapeDtypeStruct((s, D_FF), BF16),
        scratch_shapes=[pltpu.VMEM((SUBLANE + t, 2 * D_FF), F32)],
        compiler_params=_cparams(1), name=name)(u, fcw, fcb)


def _ffn_gate_bwd(u, dh, fcw, fcb, *, name):
    s = u.shape[0]
    t = _seq_tile(s, FFN_TILE)
    nt = s // t
    ch = min(FFN_ROW_CHUNK, t)
    cc = FFN_COL_CHUNK
    tb = t // SUBLANE

    def body(u_ref, halo_ref, dh_ref, w_ref, b_ref, du_ref, dw_ref, db_ref, upad, dpad):
        pid = pl.program_id(0)

        @pl.when(pid == 0)
        def _():
            dpad[t:t + SUBLANE, :] = jnp.zeros((SUBLANE, 2 * D_FF), F32)
            dw_ref[...] = jnp.zeros_like(dw_ref)
            db_ref[...] = jnp.zeros_like(db_ref)

        upad[0:SUBLANE, :] = jnp.where(pid == nt - 1, 0.0, halo_ref[...])
        upad[SUBLANE:SUBLANE + t, :] = u_ref[...]
        for c0 in range(0, D_FF, cc):
            for r0 in range(0, t, ch):
                val = _ffn_conv(upad, w_ref, b_ref, r0, ch, c0)
                gt = _ffn_conv(upad, w_ref, b_ref, r0, ch, c0 + D_FF)
                gl, dgl = _gelu_and_grad(gt)
                d = dh_ref[r0:r0 + ch, c0:c0 + cc]
                dpad[r0:r0 + ch, c0:c0 + cc] = d * gl
                dpad[r0:r0 + ch, c0 + D_FF:c0 + D_FF + cc] = d * val * dgl
        for c0 in range(0, 2 * D_FF, cc):
            dbs = jnp.zeros((1, cc), F32)
            dws = [jnp.zeros((1, cc), F32) for _ in range(FFN_CONV_WIDTH)]
            for r0 in range(0, t, ch):
                x = u_ref[r0:r0 + ch, c0:c0 + cc]
                acc = jnp.zeros((ch, cc), F32)
                for k in range(FFN_CONV_WIDTH):
                    o = (FFN_CONV_WIDTH - 1) - k + r0
                    sh = dpad[o:o + ch, c0:c0 + cc]
                    acc = acc + w_ref[k:k + 1, c0:c0 + cc] * sh
                    dws[k] = dws[k] + _colsum(x * sh)
                    if k == FFN_CONV_WIDTH - 1:
                        dbs = dbs + _colsum(sh)
                du_ref[r0:r0 + ch, c0:c0 + cc] = acc
            db_ref[:, c0:c0 + cc] += dbs
            for k in range(FFN_CONV_WIDTH):
                dw_ref[k:k + 1, c0:c0 + cc] += dws[k]
        dpad[t:t + SUBLANE, :] = dpad[0:SUBLANE, :]

    outs = [jax.ShapeDtypeStruct((s, 2 * D_FF), F32), jax.ShapeDtypeStruct((SUBLANE, 2 * D_FF), F32),
            jax.ShapeDtypeStruct((1, 2 * D_FF), F32)]
    return pl.pallas_call(
        body, grid=(nt,),
        in_specs=[pl.BlockSpec((t, 2 * D_FF), lambda i: (nt - 1 - i, 0)),
                  pl.BlockSpec((SUBLANE, 2 * D_FF), lambda i: (jnp.maximum((nt - 1 - i) * tb - 1, 0), 0)),
                  pl.BlockSpec((t, D_FF), lambda i: (nt - 1 - i, 0)), _full_spec(fcw), _full_spec(fcb)],
        out_specs=[pl.BlockSpec((t, 2 * D_FF), lambda i: (nt - 1 - i, 0)), _full_spec(outs[1]), _full_spec(outs[2])],
        out_shape=outs,
        scratch_shapes=[pltpu.VMEM((SUBLANE + t, 2 * D_FF), F32), pltpu.VMEM((t + SUBLANE, 2 * D_FF), F32)],
        compiler_params=_cparams(1), name=name)(u, u, dh, fcw, fcb)


def _adamw(gstack, w, m, v, *, name):
    _, r, c = gstack.shape
    tr = _pick_rows(r, PACK_ROW_BLOCK)
    inv_b1 = 1.0 - ADAM_B1 ** ADAM_STEP
    inv_b2 = 1.0 - ADAM_B2 ** ADAM_STEP

    def body(g_ref, w_ref, m_ref, v_ref, go_ref, d_ref, mo_ref, vo_ref):
        g = g_ref[0]
        for dev in range(1, N_DEV):
            g = g + g_ref[dev]
        go_ref[...] = g
        mn = ADAM_B1 * m_ref[...] + (1.0 - ADAM_B1) * g
        vn = ADAM_B2 * v_ref[...] + (1.0 - ADAM_B2) * (g * g)
        mo_ref[...] = mn
        vo_ref[...] = vn
        d_ref[...] = -ADAM_LR * ((mn / inv_b1) / (jnp.sqrt(vn / inv_b2) + ADAM_EPS) + ADAM_WD * w_ref[...])

    blk = pl.BlockSpec((tr, c), lambda i: (i, 0))
    sh = jax.ShapeDtypeStruct((r, c), F32)
    return pl.pallas_call(
        body, grid=(r // tr,),
        in_specs=[pl.BlockSpec((N_DEV, tr, c), lambda i: (0, i, 0)), blk, blk, blk],
        out_specs=[blk] * 4, out_shape=[sh] * 4,
        compiler_params=_cparams(1), name=name)(gstack, w, m, v)


def _allgather(xs, slot_axes, *, name):
    n = len(xs)
    out_shapes = []
    for x, ax in zip(xs, slot_axes):
        shp = (N_DEV,) + x.shape if ax == 0 else (x.shape[0], N_DEV) + x.shape[1:]
        out_shapes.append(jax.ShapeDtypeStruct(shp, x.dtype))

    def body(*refs):
        x_refs, out_refs = refs[:n], refs[n:2 * n]
        send_sems, recv_sems, local_sems = refs[2 * n:]
        mx, my, mc = lax.axis_index("x"), lax.axis_index("y"), lax.axis_index("c")
        me, sibling = (mx, my, mc), (mx, my, 1 - mc)
        chips = [(1 - mx, my), (mx, 1 - my), (1 - mx, 1 - my)]

        def slot(i, px, py, pc):
            idx = 4 * px + 2 * py + pc
            if slot_axes[i] == 0:
                return out_refs[i].at[idx]
            return out_refs[i].at[pl.ds(0, out_refs[i].shape[0]), idx]

        def copy(i, k, block, to, src=None):
            return pltpu.make_async_remote_copy(
                src_ref=slot(i, *block) if src is None else src, dst_ref=slot(i, *block),
                send_sem=send_sems.at[i * N_PEERS + k], recv_sem=recv_sems.at[i * N_PEERS + k],
                device_id=to, device_id_type=_MESH)

        mine = [pltpu.make_async_copy(x_refs[i], slot(i, *me), local_sems.at[i]) for i in range(n)]
        for cp in mine:
            cp.start()
        first = []
        for i in range(n):
            first.append(copy(i, 0, me, sibling, src=x_refs[i]))
            first += [copy(i, 1 + j, me, (*chip, mc), src=x_refs[i]) for j, chip in enumerate(chips)]
        for cp in first:
            cp.start()
        passed = []
        for j, chip in enumerate(chips):
            for i in range(n):
                copy(i, 1 + j, (*chip, mc), me).wait_recv()
                fwd = copy(i, 4 + j, (*chip, mc), sibling)
                fwd.start()
                passed.append(fwd)
        for i in range(n):
            copy(i, 0, sibling, me).wait_recv()
            for j, chip in enumerate(chips):
                copy(i, 4 + j, (*chip, 1 - mc), me).wait_recv()
        for cp in first + passed:
            cp.wait_send()
        for cp in mine:
            cp.wait()

    return pl.pallas_call(
        body, in_specs=[_ANY] * n, out_specs=[_ANY] * n, out_shape=out_shapes,
        scratch_shapes=[pltpu.SemaphoreType.DMA((n * N_PEERS,)), pltpu.SemaphoreType.DMA((n * N_PEERS,)),
                        pltpu.SemaphoreType.DMA((n,))],
        name=name)(*xs)


def _exchange_grads(srcs, kinds, *, name):
    n = len(srcs)
    out_shapes = []
    for x, kind in zip(srcs, kinds):
        if kind == "lead":
            shp = x.shape
        elif kind == "rows":
            shp = (N_DEV, x.shape[0], x.shape[1] // N_DEV, x.shape[2])
        else:
            shp = (N_DEV,) + x.shape
        out_shapes.append(jax.ShapeDtypeStruct(shp, x.dtype))

    def body(*refs):
        x_refs, out_refs = refs[:n], refs[n:2 * n]
        send_sems, recv_sems, local_sems = refs[2 * n:]
        mx, my, mc = lax.axis_index("x"), lax.axis_index("y"), lax.axis_index("c")
        my_id = 4 * mx + 2 * my + mc

        def piece(i, dev):
            if kinds[i] == "lead":
                return x_refs[i].at[dev]
            if kinds[i] == "rows":
                r = x_refs[i].shape[1] // N_DEV
                return x_refs[i].at[pl.ds(0, x_refs[i].shape[0]), pl.ds(pl.multiple_of(dev * r, SUBLANE), r)]
            return x_refs[i]

        mine = [pltpu.make_async_copy(piece(i, my_id), out_refs[i].at[my_id], local_sems.at[i]) for i in range(n)]
        for cp in mine:
            cp.start()
        copies = []
        for k in range(1, N_DEV):
            px, py, pc = mx ^ ((k >> 2) & 1), my ^ ((k >> 1) & 1), mc ^ (k & 1)
            for i in range(n):
                copies.append(pltpu.make_async_remote_copy(
                    src_ref=piece(i, 4 * px + 2 * py + pc), dst_ref=out_refs[i].at[my_id],
                    send_sem=send_sems.at[i * N_PEERS + k - 1], recv_sem=recv_sems.at[i * N_PEERS + k - 1],
                    device_id=(px, py, pc), device_id_type=_MESH))
        for cp in copies:
            cp.start()
        for cp in copies:
            cp.wait_recv()
        for cp in copies:
            cp.wait_send()
        for cp in mine:
            cp.wait()

    return pl.pallas_call(
        body, in_specs=[_ANY] * n, out_specs=[_ANY] * n, out_shape=out_shapes,
        scratch_shapes=[pltpu.SemaphoreType.DMA((n * N_PEERS,)), pltpu.SemaphoreType.DMA((n * N_PEERS,)),
                        pltpu.SemaphoreType.DMA((n,))],
        name=name)(*srcs)


def _pack_rows(n):
    rows = -(-n // PACK_COLS)
    return -(-rows // PACK_ROW_BLOCK) * PACK_ROW_BLOCK


def _pack(arrs, dtype):
    flat = jnp.concatenate([a.reshape(-1).astype(dtype) for a in arrs])
    rows = _pack_rows(flat.shape[0])
    flat = jnp.pad(flat, (0, rows * PACK_COLS - flat.shape[0]))
    return flat.reshape(rows, PACK_COLS)


def _pack_lead(arrs, dtype):
    flat = jnp.concatenate([a.reshape(N_DEV, -1).astype(dtype) for a in arrs], axis=1)
    rows = _pack_rows(flat.shape[1])
    flat = jnp.pad(flat, ((0, 0), (0, rows * PACK_COLS - flat.shape[1])))
    return flat.reshape(N_DEV, rows, PACK_COLS)


def _unpack(packed, shapes, lead=False):
    flat = packed.reshape(N_DEV, -1) if lead else packed.reshape(-1)
    out, pos = [], 0
    for sh in shapes:
        n = math.prod(sh)
        out.append(flat[:, pos:pos + n].reshape((N_DEV,) + tuple(sh)) if lead else flat[pos:pos + n].reshape(sh))
        pos += n
    return out


def _join_shards(stacked, axis):
    return jnp.concatenate([stacked[d] for d in range(N_DEV)], axis=axis)


def _split_shards(full, axis):
    return jnp.stack(jnp.split(full, N_DEV, axis=axis), axis=0)


def _join_cols(stacked):
    _, l, k, c = stacked.shape
    return jnp.transpose(stacked, (1, 2, 0, 3)).reshape(l, k, N_DEV * c)


def _split_cols(full):
    l, k, n = full.shape
    return jnp.transpose(full.reshape(l, k, N_DEV, n // N_DEV), (2, 0, 1, 3))


def _perm_in_cols(a, inverse=False):
    blocks = jnp.split(a, 6, axis=-1)
    if inverse:
        order = [IN_PERM.index(j) for j in range(6)]
    else:
        order = list(IN_PERM)
    return jnp.concatenate([blocks[j] for j in order], axis=-1)


def _row(v):
    return v.reshape(1, -1)


def _pad_rows(w, rows):
    return jnp.pad(w, ((0, rows - w.shape[0]), (0, 0)))


def _gn_avg_matrix():
    return _block_mask(GN_GROUPS, D_GROUP // GN_GROUPS, D_GROUP // GN_GROUPS) / (D_GROUP // GN_GROUPS)


def _layer_params(p, l):
    q = {}
    (a, wb, wc), q["s5_vjp"] = jax.vjp(_s5_param_map, p["s5_lam_re"][l], p["s5_lam_im"][l], p["s5_log_dt"][l],
                                       p["s5_b_re"][l], p["s5_b_im"][l], p["s5_c_re"][l], p["s5_c_im"][l])
    q["wb"], q["wc"] = wb.astype(BF16), wc.astype(BF16)
    q["apow"] = _s5_apow(a, max(S5_TILE.bit_length() - 1, 1))
    (q["wr"], q["wi"]), q["lru_w_vjp"] = jax.vjp(lambda r, i: (_blockdiag(r), _blockdiag(i)), p["lru_w_r"][l], p["lru_w_i"][l])
    q["wr"], q["wi"] = q["wr"].astype(BF16), q["wi"].astype(BF16)
    q["sp"], q["sp_vjp"] = jax.vjp(lambda lam: _row(jax.nn.softplus(-lam)), p["lru_lam"][l])
    return q


def _local_step(x, mem, target, p):
    small = {}
    big = {n: None for n in BIG}
    mavg = _gn_avg_matrix()
    saved = []

    xs = _ln_fwd(x, _row(p["ln_in_g"]), _row(p["ln_in_b"]), name="ln_in_fwd")
    for l in range(DEPTH):
        q = _layer_params(p, l)
        n = f"l{l}_"
        hin = _mm(xs, p["w_in"][l], bias=_row(p["b_in"][l]), name=n + "inproj")
        mix, s5_y1, s5_hb = _s5_fwd(hin, q["wb"], q["apow"], q["wc"], _row(p["s5_d"][l]), p["s5_w_glu"][l],
                                    _row(p["s5_b_glu"][l]), name=n + "s5_fwd")
        cvw = _pad_rows(p["cv_w"][l], CV_PAD)
        mix, cv_c = _cv_fwd(hin, cvw, _row(p["cv_b"][l]), _row(p["cv_gn_g"][l]), _row(p["cv_gn_b"][l]), mavg,
                            p["cv_w_pw"][l], _row(p["cv_b_pw"][l]), mix, name=n + "cv_fwd")
        lcw = _pad_rows(p["lru_conv_w"][l], SUBLANE)
        mix, lru_xc, lru_h = _lru_fwd(hin, lcw, _row(p["lru_conv_b"][l]), q["wr"], _row(p["lru_b_r"][l]), q["wi"],
                                      _row(p["lru_b_i"][l]), q["sp"], mix, name=n + "lru_fwd")
        kv = _mm(mem, p["attn_w_kv"][l], name=n + "kv")
        (kbig, vbig), kv_vjp = jax.vjp(_attn_big, kv)
        kbig, vbig = kbig.astype(BF16), vbig.astype(BF16)
        mix = _attn_fwd(hin, kbig, vbig, mix, name=n + "attn_fwd")
        r1 = _mm(mix, p["w_out"][l], bias=_row(p["b_out"][l]), res=xs, res_scale=ALPHA, name=n + "outproj")
        x1 = _ln_fwd(r1, _row(p["ln1_g"][l]), _row(p["ln1_b"][l]), name=n + "ln1_fwd")
        u = _mm(x1, p["ffn_w_up"][l], name=n + "ffn_up")
        fcw = _pad_rows(p["ffn_conv_w"][l], SUBLANE)
        fcb = _row(p["ffn_conv_b"][l])
        hff = _ffn_gate_fwd(u, fcw, fcb, name=n + "ffn_gate_fwd")
        r2 = _mm(hff, p["ffn_w_down"][l], res=x1, res_scale=ALPHA, name=n + "ffn_down")
        x2 = _ln_fwd(r2, _row(p["ln2_g"][l]), _row(p["ln2_b"][l]), name=n + "ln2_fwd")
        saved.append(dict(q=q, xs=xs, hin=hin, s5_y1=s5_y1, s5_hb=s5_hb, cvw=cvw, cv_c=cv_c, lcw=lcw, lru_xc=lru_xc,
                          lru_h=lru_h, kbig=kbig, vbig=vbig, kv_vjp=kv_vjp, mix=mix, r1=r1, x1=x1, u=u, fcw=fcw,
                          fcb=fcb, hff=hff, r2=r2))
        xs = x2

    dx, loss_blk = _loss_grad(xs, target, name="loss_grad")
    loss = loss_blk[0, 0]

    def big_grad(key, a, b, l, name):
        big[key] = _mm_tn(a, b, layer=l, n_layers=DEPTH, into=big[key], name=name)

    for l in reversed(range(DEPTH)):
        sv = saved[l]
        q = sv["q"]
        n = f"l{l}_"
        g = {}
        dr2, g["ln2_g"], g["ln2_b"], _ = _ln_bwd(sv["r2"], dx, _row(p["ln2_g"][l]), name=n + "ln2_bwd")
        big_grad("ffn_w_down", sv["hff"], dr2, l, n + "dw_down")
        dhff = _mm(dr2, p["ffn_w_down"][l], trans_b=True, name=n + "dhff")
        du, dfw, g["ffn_conv_b"] = _ffn_gate_bwd(sv["u"], dhff, sv["fcw"], sv["fcb"], name=n + "ffn_gate_bwd")
        g["ffn_conv_w"] = dfw[:FFN_CONV_WIDTH]
        big_grad("ffn_w_up", sv["x1"], du, l, n + "dw_up")
        dx1 = _mm(du, p["ffn_w_up"][l], trans_b=True, res=dr2, res_scale=ALPHA, name=n + "dx1")
        dr1, g["ln1_g"], g["ln1_b"], g["b_out"] = _ln_bwd(sv["r1"], dx1, _row(p["ln1_g"][l]), name=n + "ln1_bwd")
        big_grad("w_out", sv["mix"], dr1, l, n + "dw_out")
        dmix = _mm(dr1, p["w_out"][l], trans_b=True, name=n + "dmix")

        hin = sv["hin"]
        dh, g["cv_w_pw"], dcw, g["cv_b_pw"], g["cv_gn_g"], g["cv_gn_b"], g["cv_b"] = _cv_bwd(
            hin, sv["cv_c"], dmix, sv["cvw"], _row(p["cv_gn_g"][l]), _row(p["cv_gn_b"][l]), mavg, p["cv_w_pw"][l],
            name=n + "cv_bwd")
        g["cv_w"] = dcw[:CONV_WIDTH]
        dh, dwr, dwi, dlcw, g["lru_b_r"], g["lru_b_i"], dsp, g["lru_conv_b"] = _lru_bwd(
            hin, sv["lru_xc"], sv["lru_h"], dmix, sv["lcw"], q["wr"], _row(p["lru_b_r"][l]), q["wi"],
            _row(p["lru_b_i"][l]), q["sp"], dh, name=n + "lru_bwd")
        g["lru_conv_w"] = dlcw[:LRU_CONV_WIDTH]
        g["lru_w_r"], g["lru_w_i"] = q["lru_w_vjp"]((dwr, dwi))
        (g["lru_lam"],) = q["sp_vjp"](dsp)
        dh, g["s5_w_glu"], dwc, dwb, g["s5_b_glu"], g["s5_d"], da = _s5_bwd(
            hin, sv["s5_y1"], dmix, sv["s5_hb"], q["wb"], q["apow"], q["wc"], _row(p["s5_d"][l]), p["s5_w_glu"][l],
            _row(p["s5_b_glu"][l]), dh, name=n + "s5_bwd")
        (g["s5_lam_re"], g["s5_lam_im"], g["s5_log_dt"], g["s5_b_re"], g["s5_b_im"], g["s5_c_re"],
         g["s5_c_im"]) = q["s5_vjp"]((da, dwb, dwc))
        dh, dkbig, dvbig = _attn_bwd(hin, dmix, sv["kbig"], sv["vbig"], dh, name=n + "attn_bwd")
        (dkv,) = sv["kv_vjp"]((dkbig, dvbig))
        big_grad("attn_w_kv", mem, dkv, l, n + "dw_kv")

        g["b_in"] = _colsum_call(dh, name=n + "db_in")
        big_grad("w_in", sv["xs"], dh, l, n + "dw_in")
        dx = _mm(dh, p["w_in"][l], trans_b=True, res=dr1, res_scale=ALPHA, name=n + "dxs")
        for k, v in g.items():
            small.setdefault(k, [None] * DEPTH)[l] = v.reshape(p[k].shape[1:])

    grad_x, dgi, dbi, _ = _ln_bwd(x, dx, _row(p["ln_in_g"]), name="ln_in_bwd")
    out = {k: jnp.stack(v, axis=0) for k, v in small.items()}
    out.update(big)
    out["ln_in_g"], out["ln_in_b"] = dgi.reshape(-1), dbi.reshape(-1)
    return loss, grad_x, out


def kernel(x, mem, ln_in_g, ln_in_b, w_in, b_in, s5_lam_re, s5_lam_im, s5_log_dt, s5_b_re, s5_b_im, s5_c_re, s5_c_im, s5_d, s5_w_glu, s5_b_glu, cv_w, cv_b, cv_gn_g, cv_gn_b, cv_w_pw, cv_b_pw, lru_conv_w, lru_conv_b, lru_w_r, lru_b_r, lru_w_i, lru_b_i, lru_lam, attn_w_kv, w_out, b_out, ln1_g, ln1_b, ffn_w_up, ffn_conv_w, ffn_conv_b, ffn_w_down, ln2_g, ln2_b, loss_target, m_ln_in_g, m_ln_in_b, m_w_in, m_b_in, m_s5_lam_re, m_s5_lam_im, m_s5_log_dt, m_s5_b_re, m_s5_b_im, m_s5_c_re, m_s5_c_im, m_s5_d, m_s5_w_glu, m_s5_b_glu, m_cv_w, m_cv_b, m_cv_gn_g, m_cv_gn_b, m_cv_w_pw, m_cv_b_pw, m_lru_conv_w, m_lru_conv_b, m_lru_w_r, m_lru_b_r, m_lru_w_i, m_lru_b_i, m_lru_lam, m_attn_w_kv, m_w_out, m_b_out, m_ln1_g, m_ln1_b, m_ffn_w_up, m_ffn_conv_w, m_ffn_conv_b, m_ffn_w_down, m_ln2_g, m_ln2_b, v_ln_in_g, v_ln_in_b, v_w_in, v_b_in, v_s5_lam_re, v_s5_lam_im, v_s5_log_dt, v_s5_b_re, v_s5_b_im, v_s5_c_re, v_s5_c_im, v_s5_d, v_s5_w_glu, v_s5_b_glu, v_cv_w, v_cv_b, v_cv_gn_g, v_cv_gn_b, v_cv_w_pw, v_cv_b_pw, v_lru_conv_w, v_lru_conv_b, v_lru_w_r, v_lru_b_r, v_lru_w_i, v_lru_b_i, v_lru_lam, v_attn_w_kv, v_w_out, v_b_out, v_ln1_g, v_ln1_b, v_ffn_w_up, v_ffn_conv_w, v_ffn_conv_b, v_ffn_w_down, v_ln2_g, v_ln2_b):
    args = locals()
    w = {n: args[n] for n in WEIGHTS}
    mom = {n: args["m_" + n] for n in WEIGHTS}
    var = {n: args["v_" + n] for n in WEIGHTS}
    col_sharded = {n: SHARDED[n] == 2 for n in BIG}

    small_shapes = [w[n].shape for n in SMALL_SHARDED]
    gathered = _allgather([w[n].astype(BF16) for n in BIG] + [_pack([w[n] for n in SMALL_SHARDED], F32)],
                          [0 if col_sharded[n] else 1 for n in BIG] + [0], name="ag_weights")
    full = {n: w[n] for n in REPLICATED}
    for n, got in zip(BIG, gathered):
        if col_sharded[n]:
            full[n] = _join_cols(got)
        else:
            l, _, r, c = got.shape
            full[n] = got.reshape(l, N_DEV * r, c)
    for n, st in zip(SMALL_SHARDED, _unpack(gathered[-1], small_shapes, lead=True)):
        full[n] = _join_shards(st, SHARDED[n])
    for n in ("s5_w_glu", "cv_w_pw"):
        full[n] = full[n].astype(BF16)
    full["w_in"] = _perm_in_cols(full["w_in"])
    full["b_in"] = _perm_in_cols(full["b_in"])

    loss, grad_x, g_full = _local_step(x[0], mem[0], loss_target[0], full)
    loss = lax.psum(loss, ("x", "y", "c"))
    g_full["w_in"] = _perm_in_cols(g_full["w_in"], inverse=True)
    g_full["b_in"] = _perm_in_cols(g_full["b_in"], inverse=True)

    srcs = [_split_cols(g_full[n]) if col_sharded[n] else g_full[n] for n in BIG]
    kinds = ["lead" if col_sharded[n] else "rows" for n in BIG]
    srcs.append(_pack_lead([_split_shards(g_full[n], SHARDED[n]) for n in SMALL_SHARDED], F32))
    kinds.append("lead")
    srcs.append(_pack([g_full[n] for n in REPLICATED], F32))
    kinds.append("all")
    recv = _exchange_grads(srcs, kinds, name="exchange_grads")

    res = [dict(), dict(), dict(), dict()]

    def flat2(a):
        return a.reshape(-1, a.shape[-1])

    for n, got in zip(BIG, recv):
        outs = _adamw(got.reshape(N_DEV, -1, got.shape[-1]), flat2(w[n]), flat2(mom[n]), flat2(var[n]), name="adamw_" + n)
        for kind in range(4):
            res[kind][n] = outs[kind].reshape(w[n].shape)
    for names, got, tag in ((SMALL_SHARDED, recv[len(BIG)], "adamw_small_sharded"), (REPLICATED, recv[len(BIG) + 1], "adamw_replicated")):
        outs = _adamw(got, _pack([w[n] for n in names], F32), _pack([mom[n] for n in names], F32),
                      _pack([var[n] for n in names], F32), name=tag)
        for kind in range(4):
            for n, a in zip(names, _unpack(outs[kind], [w[n].shape for n in names])):
                res[kind][n] = a
    return (loss, grad_x[None], *[res[0][n] for n in WEIGHTS], *[res[1][n] for n in WEIGHTS],
            *[res[2][n] for n in WEIGHTS], *[res[3][n] for n in WEIGHTS])
```

```python
import math

import jax
import jax.numpy as jnp
from jax import lax
from jax.experimental import pallas as pl
from jax.experimental.pallas import tpu as pltpu

F32 = jnp.float32
BF16 = jnp.bfloat16

D_MODEL = 1024
DEPTH = 2
D_GROUP = 256
N_IN_COLS = 6 * D_GROUP
S5_GROUPS = 16
S5_CH = 16
S5_STATE = 64
S5_LANES = S5_GROUPS * S5_STATE
CONV_WIDTH = 31
GN_GROUPS = 4
LRU_HEADS = 4
LRU_CONV_WIDTH = 4
LRU_C = 8.0
ATTN_HEADS = 4
ATTN_HEAD_DIM = 64
D_FF = 2816
FFN_CONV_WIDTH = 3
ALPHA = (2 * DEPTH) ** 0.25
LN_EPS = 1e-5
ADAM_LR, ADAM_B1, ADAM_B2, ADAM_EPS, ADAM_WD, ADAM_STEP = 0.001, 0.9, 0.999, 1e-08, 0.01, 10

N_DEV = 8
N_PEERS = N_DEV - 1
LANE = 128
SUBLANE = 8
VMEM_LIMIT = 56 * 1024 * 1024
PACK_COLS = 1024
PACK_ROW_BLOCK = 256

SHARDED = {
    "w_in": 2, "s5_w_glu": 1, "cv_w": 2, "cv_w_pw": 1, "lru_conv_w": 2, "attn_w_kv": 1,
    "w_out": 1, "ffn_w_up": 2, "ffn_conv_w": 2, "ffn_w_down": 1,
}
BIG = ("w_in", "attn_w_kv", "w_out", "ffn_w_up", "ffn_w_down")
SMALL_SHARDED = ("s5_w_glu", "cv_w", "cv_w_pw", "lru_conv_w", "ffn_conv_w")
MATMUL_WEIGHTS = ("w_in", "s5_w_glu", "cv_w_pw", "attn_w_kv", "w_out", "ffn_w_up", "ffn_w_down")
WEIGHTS = ['ln_in_g', 'ln_in_b', 'w_in', 'b_in', 's5_lam_re', 's5_lam_im', 's5_log_dt', 's5_b_re', 's5_b_im',
           's5_c_re', 's5_c_im', 's5_d', 's5_w_glu', 's5_b_glu', 'cv_w', 'cv_b', 'cv_gn_g', 'cv_gn_b', 'cv_w_pw',
           'cv_b_pw', 'lru_conv_w', 'lru_conv_b', 'lru_w_r', 'lru_b_r', 'lru_w_i', 'lru_b_i', 'lru_lam',
           'attn_w_kv', 'w_out', 'b_out', 'ln1_g', 'ln1_b', 'ffn_w_up', 'ffn_conv_w', 'ffn_conv_b', 'ffn_w_down',
           'ln2_g', 'ln2_b']
REPLICATED = [n for n in WEIGHTS if n not in SHARDED]

COL_CV_V, COL_CV_G, COL_LRU_G, COL_LRU_X, COL_S5, COL_Q = range(6)
IN_PERM = (1, 2, 3, 4, 0, 5)
MIX_S5, MIX_CV, MIX_LRU, MIX_ATTN = range(4)


_ANY = pl.BlockSpec(memory_space=pl.ANY)
_MESH = pl.DeviceIdType.MESH


def _cparams(n_axes):
    return pltpu.CompilerParams(dimension_semantics=("arbitrary",) * n_axes, vmem_limit_bytes=VMEM_LIMIT)


def _pick(n, cap):
    if n <= cap:
        return n
    best = None
    for t in range(LANE, cap + 1, LANE):
        if n % t == 0:
            best = t
    assert best is not None, (n, cap)
    return best


def _pick_rows(n, cap):
    best = None
    for t in range(SUBLANE, min(n, cap) + 1, SUBLANE):
        if n % t == 0:
            best = t
    assert best is not None, (n, cap)
    return best


def _full_spec(arr):
    nd = arr.ndim
    return pl.BlockSpec(arr.shape, lambda *_: (0,) * nd)


def _dot(a, b):
    return lax.dot_general(a.astype(BF16), b.astype(BF16), (((1,), (0,)), ((), ())), preferred_element_type=F32)


def _dot_nt(a, b):
    return lax.dot_general(a.astype(BF16), b.astype(BF16), (((1,), (1,)), ((), ())), preferred_element_type=F32)


def _dot_tn(a, b):
    return lax.dot_general(a.astype(BF16), b.astype(BF16), (((0,), (0,)), ((), ())), preferred_element_type=F32)


def _dot_hi(a, b):
    return jnp.dot(a, b, precision=lax.Precision.HIGHEST, preferred_element_type=F32)


def _colsum(x):
    return jnp.sum(x, axis=0, keepdims=True)


def _sigmoid(x):
    return 1.0 / (1.0 + jnp.exp(-x))


_GELU_K = math.sqrt(2.0 / math.pi)
_GELU_C = 0.044715


def _gelu(x):
    t = jnp.tanh(_GELU_K * (x + _GELU_C * x * x * x))
    return 0.5 * x * (1.0 + t)


def _gelu_and_grad(x):
    x2 = x * x
    t = jnp.tanh(_GELU_K * (x + _GELU_C * x2 * x))
    g = 0.5 * x * (1.0 + t)
    dg = 0.5 * (1.0 + t) + 0.5 * x * (1.0 - t * t) * (_GELU_K * (1.0 + 3.0 * _GELU_C * x2))
    return g, dg


def _neg_expm1(x):
    series = x * (1.0 + x * (0.5 + x * (1.0 / 6.0 + x * (1.0 / 24.0 + x * (1.0 / 120.0)))))
    return -jnp.where(jnp.abs(x) < 0.1, series, jnp.exp(x) - 1.0)


def _seq_tile(s, want):
    t = min(s, want)
    assert s % t == 0
    return t


class _Rider:
    def __init__(self, srcs, kinds):
        self.srcs, self.kinds = list(srcs), list(kinds)
        self.n = len(self.srcs)

    def out_shapes(self):
        shapes = []
        for x, kind in zip(self.srcs, self.kinds):
            if kind == "lead":
                shp = x.shape
            elif kind == "rows":
                shp = (N_DEV, x.shape[0] // N_DEV) + x.shape[1:]
            else:
                shp = (N_DEV,) + x.shape
            shapes.append(jax.ShapeDtypeStruct(shp, x.dtype))
        return shapes

    def scratch(self):
        return [pltpu.SemaphoreType.DMA((self.n * N_PEERS,)), pltpu.SemaphoreType.DMA((self.n * N_PEERS,)),
                pltpu.SemaphoreType.DMA((self.n,))]

    def _copies(self, x_refs, out_refs, sems):
        send_sems, recv_sems, local_sems = sems
        mx, my, mc = lax.axis_index("x"), lax.axis_index("y"), lax.axis_index("c")
        my_id = 4 * mx + 2 * my + mc

        def piece(i, dev):
            if self.kinds[i] == "lead":
                return x_refs[i].at[dev]
            if self.kinds[i] == "rows":
                r = x_refs[i].shape[0] // N_DEV
                return x_refs[i].at[pl.ds(pl.multiple_of(dev * r, SUBLANE), r)]
            return x_refs[i]

        mine = [pltpu.make_async_copy(piece(i, my_id), out_refs[i].at[my_id], local_sems.at[i]) for i in range(self.n)]
        copies = []
        for k in range(1, N_DEV):
            px, py, pc = mx ^ ((k >> 2) & 1), my ^ ((k >> 1) & 1), mc ^ (k & 1)
            for i in range(self.n):
                copies.append(pltpu.make_async_remote_copy(
                    src_ref=piece(i, 4 * px + 2 * py + pc), dst_ref=out_refs[i].at[my_id],
                    send_sem=send_sems.at[i * N_PEERS + k - 1], recv_sem=recv_sems.at[i * N_PEERS + k - 1],
                    device_id=(px, py, pc), device_id_type=_MESH))
        return mine, copies

    def start(self, x_refs, out_refs, sems):
        mine, copies = self._copies(x_refs, out_refs, sems)
        for cp in mine + copies:
            cp.start()

    def wait(self, x_refs, out_refs, sems):
        mine, copies = self._copies(x_refs, out_refs, sems)
        for cp in copies:
            cp.wait_recv()
        for cp in copies:
            cp.wait_send()
        for cp in mine:
            cp.wait()


def _call(body, *, grid, ins, in_specs, outs, out_specs, scratch=(), aliases=None, name, rider=None):
    n_axes = len(grid)
    common = dict(grid=grid, input_output_aliases=aliases or {}, compiler_params=_cparams(n_axes), name=name)
    if rider is None:
        res = pl.pallas_call(body, in_specs=list(in_specs), out_specs=list(out_specs), out_shape=list(outs),
                             scratch_shapes=list(scratch), **common)(*ins)
        return list(res), []
    n_in, n_out, n_scr, nr = len(ins), len(outs), len(scratch), rider.n

    def wrapped(*refs):
        pos = [0]

        def take(k):
            part = refs[pos[0]:pos[0] + k]
            pos[0] += k
            return part

        a_in, r_in, a_out, r_out, a_scr, sems = take(n_in), take(nr), take(n_out), take(nr), take(n_scr), take(3)
        first = last = None
        for ax in range(n_axes):
            pid = pl.program_id(ax)
            f, l = pid == 0, pid == grid[ax] - 1
            first = f if first is None else jnp.logical_and(first, f)
            last = l if last is None else jnp.logical_and(last, l)

        @pl.when(first)
        def _():
            rider.start(r_in, r_out, sems)

        body(*a_in, *a_out, *a_scr)

        @pl.when(last)
        def _():
            rider.wait(r_in, r_out, sems)

    res = pl.pallas_call(
        wrapped, in_specs=list(in_specs) + [_ANY] * nr, out_specs=list(out_specs) + [_ANY] * nr,
        out_shape=list(outs) + rider.out_shapes(), scratch_shapes=list(scratch) + rider.scratch(), **common)(*ins, *rider.srcs)
    return list(res[:n_out]), list(res[n_out:])


def _block_mask(n_blocks, block_rows, block_cols):
    r = jnp.arange(n_blocks * block_rows) // block_rows
    c = jnp.arange(n_blocks * block_cols) // block_cols
    return (r[:, None] == c[None, :]).astype(F32)


def _mm(a, b, *, bias=None, res=None, res_scale=1.0, trans_b=False, out_dtype=F32, name, rider=None):
    m, kdim = a.shape
    n = b.shape[0] if trans_b else b.shape[1]
    tm = _seq_tile(m, 512)
    tn = _pick(n, 1408)
    tk = _pick(kdim, 1536)
    nk = kdim // tk
    has_bias, has_res = bias is not None, res is not None

    def body(*refs):
        a_ref, b_ref = refs[0], refs[1]
        pos = 2
        bias_ref = res_ref = None
        if has_bias:
            bias_ref = refs[pos]
            pos += 1
        if has_res:
            res_ref = refs[pos]
            pos += 1
        o_ref, acc_ref = refs[pos], refs[pos + 1]
        k = pl.program_id(2)

        @pl.when(k == 0)
        def _():
            acc_ref[...] = jnp.zeros_like(acc_ref)

        if trans_b:
            acc_ref[...] += _dot_nt(a_ref[...], b_ref[...])
        else:
            acc_ref[...] += _dot(a_ref[...], b_ref[...])

        @pl.when(k == nk - 1)
        def _():
            r = acc_ref[...]
            if has_bias:
                r = r + bias_ref[...]
            if has_res:
                r = r + res_scale * res_ref[...]
            o_ref[...] = r.astype(out_dtype)

    ins = [a, b]
    in_specs = [pl.BlockSpec((tm, tk), lambda i, j, k: (i, k)),
                pl.BlockSpec((tn, tk), lambda i, j, k: (j, k)) if trans_b
                else pl.BlockSpec((tk, tn), lambda i, j, k: (k, j))]
    if has_bias:
        ins.append(bias)
        in_specs.append(pl.BlockSpec((1, tn), lambda i, j, k: (0, j)))
    if has_res:
        ins.append(res)
        in_specs.append(pl.BlockSpec((tm, tn), lambda i, j, k: (i, j)))
    (out,), routs = _call(
        body, grid=(m // tm, n // tn, nk), ins=ins, in_specs=in_specs,
        outs=[jax.ShapeDtypeStruct((m, n), out_dtype)], out_specs=[pl.BlockSpec((tm, tn), lambda i, j, k: (i, j))],
        scratch=[pltpu.VMEM((tm, tn), F32)], name=name, rider=rider)
    return out if rider is None else (out, routs)


def _mm_tn(a, b, *, name, rider=None):
    s, ka = a.shape
    nb = b.shape[1]
    ts = _seq_tile(s, 512)
    tka = _pick(ka, 1408)
    tnb = _pick(nb, 1408)

    def body(a_ref, b_ref, o_ref):
        @pl.when(pl.program_id(2) == 0)
        def _():
            o_ref[...] = jnp.zeros_like(o_ref)

        o_ref[...] += _dot_tn(a_ref[...], b_ref[...])

    (out,), routs = _call(
        body, grid=(ka // tka, nb // tnb, s // ts), ins=[a, b],
        in_specs=[pl.BlockSpec((ts, tka), lambda i, j, k: (k, i)), pl.BlockSpec((ts, tnb), lambda i, j, k: (k, j))],
        outs=[jax.ShapeDtypeStruct((ka, nb), F32)], out_specs=[pl.BlockSpec((tka, tnb), lambda i, j, k: (i, j))],
        name=name, rider=rider)
    return out if rider is None else (out, routs)


def _colsum_call(x, *, name):
    s, n = x.shape
    ts = _seq_tile(s, 512)

    def body(x_ref, o_ref):
        @pl.when(pl.program_id(0) == 0)
        def _():
            o_ref[...] = jnp.zeros_like(o_ref)

        o_ref[...] += _colsum(x_ref[...])

    return pl.pallas_call(
        body, grid=(s // ts,), in_specs=[pl.BlockSpec((ts, n), lambda i: (i, 0))],
        out_specs=pl.BlockSpec((1, n), lambda i: (0, 0)), out_shape=jax.ShapeDtypeStruct((1, n), F32),
        compiler_params=_cparams(1), name=name)(x)


def _ln_fwd(r, g, b, *, name):
    s, d = r.shape
    ts = _seq_tile(s, 512)

    def body(r_ref, g_ref, b_ref, o_ref):
        x = r_ref[...]
        mu = jnp.mean(x, axis=1, keepdims=True)
        xc = x - mu
        var = jnp.mean(xc * xc, axis=1, keepdims=True)
        o_ref[...] = xc * lax.rsqrt(var + LN_EPS) * g_ref[...] + b_ref[...]

    return pl.pallas_call(
        body, grid=(s // ts,),
        in_specs=[pl.BlockSpec((ts, d), lambda i: (i, 0)), _full_spec(g), _full_spec(b)],
        out_specs=pl.BlockSpec((ts, d), lambda i: (i, 0)), out_shape=jax.ShapeDtypeStruct((s, d), F32),
        compiler_params=_cparams(1), name=name)(r, g, b)


def _ln_bwd(r, dy, g, *, name):
    s, d = r.shape
    ts = _seq_tile(s, 512)

    def body(r_ref, dy_ref, g_ref, dr_ref, dg_ref, db_ref, ds_ref):
        @pl.when(pl.program_id(0) == 0)
        def _():
            dg_ref[...] = jnp.zeros_like(dg_ref)
            db_ref[...] = jnp.zeros_like(db_ref)
            ds_ref[...] = jnp.zeros_like(ds_ref)

        x = r_ref[...]
        dy = dy_ref[...]
        mu = jnp.mean(x, axis=1, keepdims=True)
        xc = x - mu
        var = jnp.mean(xc * xc, axis=1, keepdims=True)
        rstd = lax.rsqrt(var + LN_EPS)
        xh = xc * rstd
        dxh = dy * g_ref[...]
        m1 = jnp.mean(dxh, axis=1, keepdims=True)
        m2 = jnp.mean(dxh * xh, axis=1, keepdims=True)
        dr = rstd * (dxh - m1 - xh * m2)
        dr_ref[...] = dr
        dg_ref[...] += _colsum(dy * xh)
        db_ref[...] += _colsum(dy)
        ds_ref[...] += _colsum(dr)

    vec = jax.ShapeDtypeStruct((1, d), F32)
    vspec = pl.BlockSpec((1, d), lambda i: (0, 0))
    return pl.pallas_call(
        body, grid=(s // ts,),
        in_specs=[pl.BlockSpec((ts, d), lambda i: (i, 0)), pl.BlockSpec((ts, d), lambda i: (i, 0)), _full_spec(g)],
        out_specs=[pl.BlockSpec((ts, d), lambda i: (i, 0)), vspec, vspec, vspec],
        out_shape=[jax.ShapeDtypeStruct((s, d), F32), vec, vec, vec],
        compiler_params=_cparams(1), name=name)(r, dy, g)


def _loss_grad(y, target, *, name):
    s, d = y.shape
    ts = _seq_tile(s, 512)

    def body(y_ref, t_ref, dy_ref, l_ref):
        @pl.when(pl.program_id(0) == 0)
        def _():
            l_ref[...] = jnp.zeros_like(l_ref)

        e = y_ref[...] - t_ref[...]
        dy_ref[...] = e * (1.0 / d)
        part = jnp.sum(jnp.sum(e * e, axis=1, keepdims=True), axis=0, keepdims=True) * (0.5 / d)
        l_ref[...] += jnp.broadcast_to(part, l_ref.shape)

    return pl.pallas_call(
        body, grid=(s // ts,),
        in_specs=[pl.BlockSpec((ts, d), lambda i: (i, 0)), pl.BlockSpec((ts, d), lambda i: (i, 0))],
        out_specs=[pl.BlockSpec((ts, d), lambda i: (i, 0)), pl.BlockSpec((SUBLANE, LANE), lambda i: (0, 0))],
        out_shape=[jax.ShapeDtypeStruct((s, d), F32), jax.ShapeDtypeStruct((SUBLANE, LANE), F32)],
        compiler_params=_cparams(1), name=name)(y, target)


SCAN_CHUNK = 32


def _cscan_levels(bufs, apow_ref, t, pad, *, reverse):
    half = bufs[0].shape[1] // 2
    ch = min(SCAN_CHUNK, t)
    nlev = t.bit_length() - 1
    assert (1 << nlev) == t
    for k in range(nlev):
        d = 1 << k
        src, dst = bufs[k % 2], bufs[(k + 1) % 2]

        def chunk(c, carry, src=src, dst=dst, d=d, k=k):
            ar = apow_ref[k:k + 1, :half]
            ai = apow_ref[k:k + 1, half:]
            if reverse:
                ai = -ai
            r0 = pl.multiple_of(c * ch, ch)
            cur = src[pl.ds(pad + r0, ch), :]
            if d >= SUBLANE:
                off = pad + d if reverse else pad - d
                sh = src[pl.ds(off + r0, ch), :]
            elif reverse:
                blk = src[pl.ds(pad + r0, ch + SUBLANE), :]
                sh = pltpu.roll(blk, ch + SUBLANE - d, axis=0)[:ch, :]
            else:
                blk = src[pl.ds(pad - SUBLANE + r0, ch + SUBLANE), :]
                sh = pltpu.roll(blk, d, axis=0)[SUBLANE:, :]
            sre, sim = sh[:, :half], sh[:, half:]
            dst[pl.ds(pad + r0, ch), :half] = cur[:, :half] + ar * sre - ai * sim
            dst[pl.ds(pad + r0, ch), half:] = cur[:, half:] + ar * sim + ai * sre
            return carry

        lax.fori_loop(0, t // ch, chunk, 0)
    return nlev % 2


def _rscan_levels(abufs, bbufs, t, pad, *, reverse):
    nlev = t.bit_length() - 1
    assert (1 << nlev) == t
    for k in range(nlev):
        d = 1 << k
        asrc, adst = abufs[k % 2], abufs[(k + 1) % 2]
        bsrc, bdst = bbufs[k % 2], bbufs[(k + 1) % 2]
        off = pad + d if reverse else pad - d
        a = asrc[pad:pad + t, :]
        bdst[pad:pad + t, :] = a * bsrc[off:off + t, :] + bsrc[pad:pad + t, :]
        if k < nlev - 1:
            adst[pad:pad + t, :] = a * asrc[off:off + t, :]
    return nlev % 2


S5_TILE = 256


def _s5_scan_forward(u_bf, wb_ref, apow_ref, state, bufs, t, pad):
    half = S5_LANES
    bufs[0][pad:pad + t, :] = _dot(u_bf, wb_ref[...])
    ar, ai = apow_ref[0:1, :half], apow_ref[0:1, half:]
    sr, si = state[:, :half], state[:, half:]
    bufs[0][pad:pad + 1, :half] += ar * sr - ai * si
    bufs[0][pad:pad + 1, half:] += ar * si + ai * sr
    return _cscan_levels(bufs, apow_ref, t, pad, reverse=False)


def _s5_fwd(h_in, wb, apow, wc, dvec, wglu, bglu, *, name, rider=None):
    s = h_in.shape[0]
    t = _seq_tile(s, S5_TILE)
    pad = t // 2
    nt = s // t
    lanes2 = 2 * S5_LANES

    def body(u_ref, wb_ref, apow_ref, wc_ref, d_ref, wglu_ref, bglu_ref, out_ref, y1_ref, hb_ref, buf0, buf1, carry):
        bufs = (buf0, buf1)

        @pl.when(pl.program_id(0) == 0)
        def _():
            buf0[0:pad, :] = jnp.zeros((pad, lanes2), F32)
            buf1[0:pad, :] = jnp.zeros((pad, lanes2), F32)
            carry[...] = jnp.zeros_like(carry)

        u = u_ref[...]
        state = carry[0:1, :]
        hb_ref[0] = state
        fin = _s5_scan_forward(u.astype(BF16), wb_ref, apow_ref, state, bufs, t, pad)
        hbuf = bufs[fin]
        carry[0:1, :] = hbuf[pad + t - 1:pad + t, :]
        y1 = _dot(hbuf[pad:pad + t, :], wc_ref[...]) + d_ref[...] * u
        y1_ref[...] = y1
        y2 = _gelu(y1)
        z = _dot(y2, wglu_ref[...]) + bglu_ref[...]
        out_ref[...] = y2 * _sigmoid(z)

    ins = [h_in, wb, apow, wc, dvec, wglu, bglu]
    in_specs = [pl.BlockSpec((t, D_GROUP), lambda i: (i, COL_S5))] + [_full_spec(a) for a in ins[1:]]
    return _call(
        body, grid=(nt,), ins=ins, in_specs=in_specs,
        out_specs=[pl.BlockSpec((t, D_GROUP), lambda i: (i, MIX_S5)), pl.BlockSpec((t, D_GROUP), lambda i: (i, 0)),
                   pl.BlockSpec((1, 1, lanes2), lambda i: (i, 0, 0))],
        outs=[jax.ShapeDtypeStruct((s, D_MODEL), F32), jax.ShapeDtypeStruct((s, D_GROUP), F32),
              jax.ShapeDtypeStruct((nt, 1, lanes2), F32)],
        scratch=[pltpu.VMEM((pad + t, lanes2), F32), pltpu.VMEM((pad + t, lanes2), F32),
                 pltpu.VMEM((SUBLANE, lanes2), F32)],
        name=name, rider=rider)


def _s5_bwd(h_in, y1, dmix, hb, wb, apow, wc, dvec, wglu, bglu, dh_all, *, name, rider=None):
    s = h_in.shape[0]
    t = _seq_tile(s, S5_TILE)
    pad = t // 2
    nt = s // t
    half = S5_LANES
    lanes2 = 2 * half
    rows = pad + t + pad

    def body(u_ref, y1_ref, do_ref, hb_ref, wb_ref, apow_ref, wc_ref, d_ref, wglu_ref, bglu_ref, _dh_in,
             du_ref, dwglu_ref, dwc_ref, dwb_ref, dbglu_ref, dd_ref, da_ref, buf0, buf1, buf2, buf3, carry):
        @pl.when(pl.program_id(0) == 0)
        def _():
            for bf in (buf0, buf1, buf2, buf3):
                bf[0:pad, :] = jnp.zeros((pad, lanes2), F32)
                bf[pad + t:rows, :] = jnp.zeros((pad, lanes2), F32)
            carry[...] = jnp.zeros_like(carry)
            for r in (dwglu_ref, dwc_ref, dwb_ref, dbglu_ref, dd_ref, da_ref):
                r[...] = jnp.zeros_like(r)

        u = u_ref[...]
        u_bf = u.astype(BF16)
        state = hb_ref[0]
        hfin = _s5_scan_forward(u_bf, wb_ref, apow_ref, state, (buf0, buf1), t, pad)
        hbuf = (buf0, buf1)[hfin]
        h_bf = hbuf[pad:pad + t, :].astype(BF16)

        y1 = y1_ref[...]
        dout = do_ref[...]
        y2, dgelu = _gelu_and_grad(y1)
        sg = _sigmoid(_dot(y2, wglu_ref[...]) + bglu_ref[...])
        dz = dout * y2 * sg * (1.0 - sg)
        dy2 = dout * sg + _dot_nt(dz, wglu_ref[...])
        dwglu_ref[...] += _dot_tn(y2, dz)
        dbglu_ref[...] += _colsum(dz)
        dy1 = dy2 * dgelu
        dd_ref[...] += _colsum(dy1 * u)
        dy1_bf = dy1.astype(BF16)
        dwc_ref[...] += _dot_tn(h_bf, dy1_bf)

        buf2[pad:pad + t, :] = _dot_nt(dy1_bf, wc_ref[...])
        ar, ai = apow_ref[0:1, :half], apow_ref[0:1, half:]
        cr, ci = carry[0:1, :half], carry[0:1, half:]
        buf2[pad + t - 1:pad + t, :half] += ar * cr + ai * ci
        buf2[pad + t - 1:pad + t, half:] += ar * ci - ai * cr
        lfin = _cscan_levels((buf2, buf3), apow_ref, t, pad, reverse=True)
        lbuf = (buf2, buf3)[lfin]
        lam = lbuf[pad:pad + t, :]
        carry[0:1, :] = lbuf[pad:pad + 1, :]
        lam_bf = lam.astype(BF16)
        du_ref[...] = dy1 * d_ref[...] + _dot_nt(lam_bf, wb_ref[...])
        dwb_ref[...] += _dot_tn(u_bf, lam_bf)

        hbuf[pad - 1:pad, :] = state
        hp = hbuf[pad - 1:pad - 1 + t, :]
        hbuf[pad - 1:pad, :] = jnp.zeros((1, lanes2), F32)
        lre, lim = lam[:, :half], lam[:, half:]
        hre, him = hp[:, :half], hp[:, half:]
        da_ref[:, :half] += _colsum(lre * hre + lim * him)
        da_ref[:, half:] += _colsum(lim * hre - lre * him)

    def rev(col):
        return lambda i: (nt - 1 - i, col)

    ins = [h_in, y1, dmix, hb, wb, apow, wc, dvec, wglu, bglu, dh_all]
    in_specs = [pl.BlockSpec((t, D_GROUP), rev(COL_S5)), pl.BlockSpec((t, D_GROUP), rev(0)),
                pl.BlockSpec((t, D_GROUP), rev(MIX_S5)), pl.BlockSpec((1, 1, lanes2), lambda i: (nt - 1 - i, 0, 0))] + \
               [_full_spec(a) for a in ins[4:10]] + [_ANY]
    outs = [jax.ShapeDtypeStruct((s, N_IN_COLS), F32), jax.ShapeDtypeStruct((D_GROUP, D_GROUP), F32),
            jax.ShapeDtypeStruct((lanes2, D_GROUP), F32), jax.ShapeDtypeStruct((D_GROUP, lanes2), F32),
            jax.ShapeDtypeStruct((1, D_GROUP), F32), jax.ShapeDtypeStruct((1, D_GROUP), F32),
            jax.ShapeDtypeStruct((1, lanes2), F32)]
    out_specs = [pl.BlockSpec((t, D_GROUP), rev(COL_S5))] + [_full_spec(o) for o in outs[1:]]
    return _call(
        body, grid=(nt,), ins=ins, in_specs=in_specs, out_specs=out_specs, outs=outs, aliases={10: 0},
        scratch=[pltpu.VMEM((rows, lanes2), F32) for _ in range(4)] + [pltpu.VMEM((SUBLANE, lanes2), F32)],
        name=name, rider=rider)


def _s5_param_map(lam_re, lam_im, log_dt, b_re, b_im, c_re, c_im):
    dt = jnp.exp(log_dt)[:, None]
    er = jnp.exp(lam_re * dt)
    a_re, a_im = er * jnp.cos(lam_im * dt), er * jnp.sin(lam_im * dt)
    den = lam_re * lam_re + lam_im * lam_im
    n_re = a_re - 1.0
    k_re = (n_re * lam_re + a_im * lam_im) / den
    k_im = (a_im * lam_re - n_re * lam_im) / den
    bb_re = k_re[..., None] * b_re - k_im[..., None] * b_im
    bb_im = k_re[..., None] * b_im + k_im[..., None] * b_re
    mask_in = _block_mask(S5_GROUPS, S5_CH, S5_STATE)
    mask_out = _block_mask(S5_GROUPS, S5_STATE, S5_CH)

    def blockdiag_in(m):
        return jnp.tile(jnp.transpose(m, (0, 2, 1)).reshape(S5_GROUPS * S5_CH, S5_STATE), (1, S5_GROUPS)) * mask_in

    def blockdiag_out(m):
        return jnp.tile(jnp.transpose(m, (0, 2, 1)).reshape(S5_LANES, S5_CH), (1, S5_GROUPS)) * mask_out

    a = jnp.concatenate([a_re.reshape(1, -1), a_im.reshape(1, -1)], axis=1)
    wb = jnp.concatenate([blockdiag_in(bb_re), blockdiag_in(bb_im)], axis=1)
    wc = jnp.concatenate([blockdiag_out(c_re), -blockdiag_out(c_im)], axis=0)
    return a, wb, wc


def _s5_apow(a, nlev):
    half = S5_LANES
    re, im = a[:, :half], a[:, half:]
    rows = []
    for _ in range(nlev):
        rows.append(jnp.concatenate([re, im], axis=1))
        re, im = re * re - im * im, 2.0 * re * im
    n_rows = -(-nlev // SUBLANE) * SUBLANE
    rows += [jnp.zeros_like(rows[0])] * (n_rows - nlev)
    return lax.stop_gradient(jnp.concatenate(rows, axis=0))


CV_TILE = 256
CV_PAD = 32
CV_CHUNK = 64


def _gn_stats(c, mavg):
    mu = _dot_hi(c, mavg)
    cen = c - mu
    var = _dot_hi(cen * cen, mavg)
    rstd = lax.rsqrt(var + LN_EPS)
    return cen * rstd, rstd


def _cv_fwd(h_in, cw, cb, gng, gnb, mavg, wpw, bpw, mix, *, name, rider=None):
    s = h_in.shape[0]
    t = _seq_tile(s, CV_TILE)
    ch = min(CV_CHUNK, t)

    def body(v_ref, g_ref, cw_ref, cb_ref, gng_ref, gnb_ref, mavg_ref, wpw_ref, bpw_ref, _mix_in, out_ref, c_ref, xpad):
        @pl.when(pl.program_id(0) == 0)
        def _():
            xpad[0:CV_PAD, :] = jnp.zeros((CV_PAD, D_GROUP), F32)

        xpad[CV_PAD:CV_PAD + t, :] = v_ref[...] * _sigmoid(g_ref[...])
        for r0 in range(0, t, ch):
            acc = jnp.broadcast_to(cb_ref[...], (ch, D_GROUP))
            for k in range(CONV_WIDTH):
                o = CV_PAD - (CONV_WIDTH - 1) + k + r0
                acc = acc + cw_ref[k:k + 1, :] * xpad[o:o + ch, :]
            c_ref[r0:r0 + ch, :] = acc
        xpad[0:CV_PAD, :] = xpad[t:t + CV_PAD, :]
        xn, _ = _gn_stats(c_ref[...], mavg_ref[...])
        gn = xn * gng_ref[...] + gnb_ref[...]
        out_ref[...] = _dot(gn * _sigmoid(gn), wpw_ref[...]) + bpw_ref[...]

    ins = [h_in, h_in, cw, cb, gng, gnb, mavg, wpw, bpw, mix]
    in_specs = [pl.BlockSpec((t, D_GROUP), lambda i: (i, COL_CV_V)), pl.BlockSpec((t, D_GROUP), lambda i: (i, COL_CV_G))] + \
               [_full_spec(a) for a in ins[2:9]] + [_ANY]
    return _call(
        body, grid=(s // t,), ins=ins, in_specs=in_specs,
        out_specs=[pl.BlockSpec((t, D_GROUP), lambda i: (i, MIX_CV)), pl.BlockSpec((t, D_GROUP), lambda i: (i, 0))],
        outs=[jax.ShapeDtypeStruct((s, D_MODEL), F32), jax.ShapeDtypeStruct((s, D_GROUP), F32)],
        aliases={9: 0}, scratch=[pltpu.VMEM((CV_PAD + t, D_GROUP), F32)], name=name, rider=rider)


def _cv_bwd(h_in, c, dmix, cw, gng, gnb, mavg, wpw, *, name, rider=None):
    s = h_in.shape[0]
    t = _seq_tile(s, CV_TILE)
    nt = s // t
    ch = min(CV_CHUNK, t)

    def body(v_ref, g_ref, c_ref, do_ref, cw_ref, gng_ref, gnb_ref, mavg_ref, wpw_ref,
             dvg_ref, dwpw_ref, dcw_ref, dbpw_ref, dgg_ref, dgb_ref, dcb_ref, dcpad, hgbuf):
        @pl.when(pl.program_id(0) == 0)
        def _():
            dcpad[t:t + CV_PAD, :] = jnp.zeros((CV_PAD, D_GROUP), F32)
            for r in (dwpw_ref, dcw_ref, dbpw_ref, dgg_ref, dgb_ref, dcb_ref):
                r[...] = jnp.zeros_like(r)

        mavg = mavg_ref[...]
        xn, rstd = _gn_stats(c_ref[...], mavg)
        gg = gng_ref[...]
        gn = xn * gg + gnb_ref[...]
        sg = _sigmoid(gn)
        dout = do_ref[...]
        dwpw_ref[...] += _dot_tn(gn * sg, dout)
        dbpw_ref[...] += _colsum(dout)
        dgn = _dot_nt(dout, wpw_ref[...]) * (sg * (1.0 + gn * (1.0 - sg)))
        dgg_ref[...] += _colsum(dgn * xn)
        dgb_ref[...] += _colsum(dgn)
        dxn = dgn * gg
        dc = rstd * (dxn - _dot_hi(dxn, mavg) - xn * _dot_hi(dxn * xn, mavg))
        dcb_ref[...] += _colsum(dc)
        dcpad[0:t, :] = dc

        v = v_ref[...]
        sgm = _sigmoid(g_ref[...])
        hgbuf[...] = v * sgm
        for r0 in range(0, t, ch):
            hg = hgbuf[r0:r0 + ch, :]
            acc = jnp.zeros((ch, D_GROUP), F32)
            for k in range(CONV_WIDTH):
                o = (CONV_WIDTH - 1) - k + r0
                sh = dcpad[o:o + ch, :]
                acc = acc + cw_ref[k:k + 1, :] * sh
                dcw_ref[k:k + 1, :] += _colsum(hg * sh)
            hgbuf[r0:r0 + ch, :] = acc
        dcpad[t:t + CV_PAD, :] = dcpad[0:CV_PAD, :]
        dhg = hgbuf[...]
        dvg_ref[:, :D_GROUP] = dhg * sgm
        dvg_ref[:, D_GROUP:] = dhg * v * sgm * (1.0 - sgm)

    def rev(col):
        return lambda i: (nt - 1 - i, col)

    ins = [h_in, h_in, c, dmix, cw, gng, gnb, mavg, wpw]
    in_specs = [pl.BlockSpec((t, D_GROUP), rev(COL_CV_V)), pl.BlockSpec((t, D_GROUP), rev(COL_CV_G)),
                pl.BlockSpec((t, D_GROUP), rev(0)), pl.BlockSpec((t, D_GROUP), rev(MIX_CV))] + [_full_spec(a) for a in ins[4:]]
    vec = jax.ShapeDtypeStruct((1, D_GROUP), F32)
    outs = [jax.ShapeDtypeStruct((s, N_IN_COLS), F32),
            jax.ShapeDtypeStruct((D_GROUP, D_GROUP), F32), jax.ShapeDtypeStruct((CV_PAD, D_GROUP), F32), vec, vec, vec, vec]
    out_specs = [pl.BlockSpec((t, 2 * D_GROUP), rev(COL_CV_V // 2))] + [_full_spec(o) for o in outs[1:]]
    return _call(
        body, grid=(nt,), ins=ins, in_specs=in_specs, out_specs=out_specs, outs=outs,
        scratch=[pltpu.VMEM((t + CV_PAD, D_GROUP), F32), pltpu.VMEM((t, D_GROUP), F32)], name=name, rider=rider)


LRU_TILE = 256


def _lru_gates(xc, wr_ref, br_ref, wi_ref, bi_ref, sp_ref):
    r = _sigmoid(_dot(xc, wr_ref[...]) + br_ref[...])
    i = _sigmoid(_dot(xc, wi_ref[...]) + bi_ref[...])
    log_a = -LRU_C * r * sp_ref[...]
    a = jnp.exp(log_a)
    m = jnp.sqrt(_neg_expm1(2.0 * log_a))
    return r, i, a, m


def _lru_fwd(h_in, lcw, lcb, wr, br, wi, bi, sp, mix, *, name):
    s = h_in.shape[0]
    t = _seq_tile(s, LRU_TILE)
    pad = max(t // 2, SUBLANE)

    def body(xg_ref, xr_ref, lcw_ref, lcb_ref, wr_ref, br_ref, wi_ref, bi_ref, sp_ref, _mix_in,
             out_ref, xc_ref, h_ref, xpad, a0, a1, b0, b1, carry):
        @pl.when(pl.program_id(0) == 0)
        def _():
            xpad[0:SUBLANE, :] = jnp.zeros((SUBLANE, D_GROUP), F32)
            for bf in (a0, a1, b0, b1):
                bf[0:pad, :] = jnp.zeros((pad, D_GROUP), F32)
            carry[...] = jnp.zeros_like(carry)

        xpad[SUBLANE:SUBLANE + t, :] = xr_ref[...]
        xc = jnp.broadcast_to(lcb_ref[...], (t, D_GROUP))
        for k in range(LRU_CONV_WIDTH):
            o = SUBLANE - (LRU_CONV_WIDTH - 1) + k
            xc = xc + lcw_ref[k:k + 1, :] * xpad[o:o + t, :]
        xpad[0:SUBLANE, :] = xpad[t:t + SUBLANE, :]
        xc_ref[...] = xc
        _, i, a, m = _lru_gates(xc, wr_ref, br_ref, wi_ref, bi_ref, sp_ref)
        a0[pad:pad + t, :] = a
        b0[pad:pad + t, :] = m * (i * xc)
        b0[pad:pad + 1, :] += a0[pad:pad + 1, :] * carry[0:1, :]
        fin = _rscan_levels((a0, a1), (b0, b1), t, pad, reverse=False)
        hbuf = (b0, b1)[fin]
        carry[0:1, :] = hbuf[pad + t - 1:pad + t, :]
        h = hbuf[pad:pad + t, :]
        h_ref[...] = h
        out_ref[...] = h * _gelu(xg_ref[...])

    ins = [h_in, h_in, lcw, lcb, wr, br, wi, bi, sp, mix]
    row = pl.BlockSpec((t, D_GROUP), lambda i: (i, 0))
    in_specs = [pl.BlockSpec((t, D_GROUP), lambda i: (i, COL_LRU_G)), pl.BlockSpec((t, D_GROUP), lambda i: (i, COL_LRU_X))] + \
               [_full_spec(a) for a in ins[2:9]] + [_ANY]
    return pl.pallas_call(
        body, grid=(s // t,), in_specs=in_specs,
        out_specs=[pl.BlockSpec((t, D_GROUP), lambda i: (i, MIX_LRU)), row, row],
        out_shape=[jax.ShapeDtypeStruct((s, D_MODEL), F32)] + [jax.ShapeDtypeStruct((s, D_GROUP), F32)] * 2,
        input_output_aliases={9: 0},
        scratch_shapes=[pltpu.VMEM((SUBLANE + t, D_GROUP), F32)] + [pltpu.VMEM((pad + t, D_GROUP), F32)] * 4 +
                       [pltpu.VMEM((SUBLANE, D_GROUP), F32)],
        compiler_params=_cparams(1), name=name)(*ins)


def _lru_bwd(h_in, xc_all, h_all, dmix, lcw, wr, br, wi, bi, sp, dh_all, *, name):
    s = h_in.shape[0]
    t = _seq_tile(s, LRU_TILE)
    nt = s // t
    pad = max(t // 2, SUBLANE)
    tb = t // SUBLANE

    def body(xg_ref, xr_ref, xc_ref, h_ref, hprev_ref, do_ref, lcw_ref, wr_ref, br_ref, wi_ref, bi_ref, sp_ref, _dh_in,
             dgr_ref, dwr_ref, dwi_ref, dlcw_ref, dbr_ref, dbi_ref, dsp_ref, dlcb_ref,
             a0, a1, b0, b1, hp, dxpad, carry):
        pid = pl.program_id(0)

        @pl.when(pid == 0)
        def _():
            for bf in (a0, a1, b0, b1):
                bf[pad + t:pad + t + pad, :] = jnp.zeros((pad, D_GROUP), F32)
            dxpad[t:t + SUBLANE, :] = jnp.zeros((SUBLANE, D_GROUP), F32)
            carry[...] = jnp.zeros_like(carry)
            for r in (dwr_ref, dwi_ref, dlcw_ref, dbr_ref, dbi_ref, dsp_ref, dlcb_ref):
                r[...] = jnp.zeros_like(r)

        xc = xc_ref[...]
        h = h_ref[...]
        dout = do_ref[...]
        gate, dgate = _gelu_and_grad(xg_ref[...])
        dgr_ref[:, :D_GROUP] = dout * h * dgate
        r, i, a, m = _lru_gates(xc, wr_ref, br_ref, wi_ref, bi_ref, sp_ref)

        a0[pad:pad + t, :] = a
        b0[pad:pad + t, :] = dout * gate
        b0[pad + t - 1:pad + t, :] += carry[0:1, :]
        a1[pad:pad + t, :] = a0[pad + 1:pad + 1 + t, :]
        fin = _rscan_levels((a1, a0), (b0, b1), t, pad, reverse=True)
        lam = (b0, b1)[fin][pad:pad + t, :]
        carry[0:1, :] = a[0:1, :] * lam[0:1, :]

        is_first = pid == nt - 1
        hp[0:SUBLANE, :] = jnp.where(is_first, 0.0, hprev_ref[...])
        hp[SUBLANE:SUBLANE + t, :] = h
        hprev = hp[SUBLANE - 1:SUBLANE - 1 + t, :]

        ix = i * xc
        dmm = lam * ix
        dix = lam * m
        da = lam * hprev - dmm * (a / m)
        dlog_a = da * a
        dr = dlog_a * (-LRU_C * sp_ref[...])
        dsp_ref[...] += _colsum(dlog_a * (-LRU_C * r))
        dpr = dr * r * (1.0 - r)
        dpi = dix * xc * i * (1.0 - i)
        dbr_ref[...] += _colsum(dpr)
        dbi_ref[...] += _colsum(dpi)
        dwr_ref[...] += _dot_tn(xc, dpr)
        dwi_ref[...] += _dot_tn(xc, dpi)
        dxc = dix * i + _dot_nt(dpr, wr_ref[...]) + _dot_nt(dpi, wi_ref[...])
        dlcb_ref[...] += _colsum(dxc)

        dxpad[0:t, :] = dxc
        xr = xr_ref[...]
        dxr = jnp.zeros((t, D_GROUP), F32)
        for k in range(LRU_CONV_WIDTH):
            o = (LRU_CONV_WIDTH - 1) - k
            sh = dxpad[o:o + t, :]
            dxr = dxr + lcw_ref[k:k + 1, :] * sh
            dlcw_ref[k:k + 1, :] += _colsum(xr * sh)
        dxpad[t:t + SUBLANE, :] = dxpad[0:SUBLANE, :]
        dgr_ref[:, D_GROUP:] = dxr

    def rev(col):
        return lambda i: (nt - 1 - i, col)

    ins = [h_in, h_in, xc_all, h_all, h_all, dmix, lcw, wr, br, wi, bi, sp, dh_all]
    in_specs = [pl.BlockSpec((t, D_GROUP), rev(COL_LRU_G)), pl.BlockSpec((t, D_GROUP), rev(COL_LRU_X)),
                pl.BlockSpec((t, D_GROUP), rev(0)), pl.BlockSpec((t, D_GROUP), rev(0)),
                pl.BlockSpec((SUBLANE, D_GROUP), lambda i: (jnp.maximum((nt - 1 - i) * tb - 1, 0), 0)),
                pl.BlockSpec((t, D_GROUP), rev(MIX_LRU))] + [_full_spec(a) for a in ins[6:12]] + [_ANY]
    vec = jax.ShapeDtypeStruct((1, D_GROUP), F32)
    mat = jax.ShapeDtypeStruct((D_GROUP, D_GROUP), F32)
    outs = [jax.ShapeDtypeStruct((s, N_IN_COLS), F32), mat, mat, jax.ShapeDtypeStruct((SUBLANE, D_GROUP), F32),
            vec, vec, vec, vec]
    out_specs = [pl.BlockSpec((t, 2 * D_GROUP), rev(COL_LRU_G // 2))] + [_full_spec(o) for o in outs[1:]]
    return pl.pallas_call(
        body, grid=(nt,), in_specs=in_specs, out_specs=out_specs, out_shape=outs, input_output_aliases={12: 0},
        scratch_shapes=[pltpu.VMEM((pad + t + pad, D_GROUP), F32)] * 4 +
                       [pltpu.VMEM((SUBLANE + t, D_GROUP), F32), pltpu.VMEM((t + SUBLANE, D_GROUP), F32),
                        pltpu.VMEM((SUBLANE, D_GROUP), F32)],
        compiler_params=_cparams(1), name=name)(*ins)


def _blockdiag(w):
    h, d, _ = w.shape
    return jnp.tile(w.reshape(h * d, d), (1, h)) * _block_mask(h, d, d)


ATTN_TILE = 512
ATTN_SCALE = ATTN_HEAD_DIM ** -0.5


def _attn_big(kv):
    m = kv.shape[0]
    kbig = jnp.tile(kv[:, :D_GROUP].T, (1, ATTN_HEADS)) * _block_mask(ATTN_HEADS, ATTN_HEAD_DIM, m)
    vbig = jnp.tile(kv[:, D_GROUP:], (ATTN_HEADS, 1)) * _block_mask(ATTN_HEADS, m, ATTN_HEAD_DIM)
    return kbig, vbig


def _attn_probs(q, kbig_ref, m):
    sc = _dot(q, kbig_ref[...]) * ATTN_SCALE
    ps = []
    for h in range(ATTN_HEADS):
        sh = sc[:, h * m:(h + 1) * m]
        e = jnp.exp(sh - jnp.max(sh, axis=1, keepdims=True))
        ps.append(e / jnp.sum(e, axis=1, keepdims=True))
    return ps


def _attn_fwd(h_in, kbig, vbig, mix, *, name):
    s = h_in.shape[0]
    t = _seq_tile(s, ATTN_TILE)
    m = kbig.shape[1] // ATTN_HEADS

    def body(q_ref, kbig_ref, vbig_ref, _mix_in, o_ref):
        ps = _attn_probs(q_ref[...], kbig_ref, m)
        o_ref[...] = _dot(jnp.concatenate(ps, axis=1), vbig_ref[...])

    return pl.pallas_call(
        body, grid=(s // t,),
        in_specs=[pl.BlockSpec((t, D_GROUP), lambda i: (i, COL_Q)), _full_spec(kbig), _full_spec(vbig), _ANY],
        out_specs=pl.BlockSpec((t, D_GROUP), lambda i: (i, MIX_ATTN)),
        out_shape=jax.ShapeDtypeStruct((s, D_MODEL), F32), input_output_aliases={3: 0},
        compiler_params=_cparams(1), name=name)(h_in, kbig, vbig, mix)


def _attn_bwd(h_in, dmix, kbig, vbig, dh_all, *, name):
    s = h_in.shape[0]
    t = _seq_tile(s, ATTN_TILE)
    m = kbig.shape[1] // ATTN_HEADS

    def body(q_ref, do_ref, kbig_ref, vbig_ref, _dh_in, dq_ref, dk_ref, dv_ref):
        @pl.when(pl.program_id(0) == 0)
        def _():
            dk_ref[...] = jnp.zeros_like(dk_ref)
            dv_ref[...] = jnp.zeros_like(dv_ref)

        q = q_ref[...]
        dout = do_ref[...]
        ps = _attn_probs(q, kbig_ref, m)
        dp = _dot_nt(dout, vbig_ref[...])
        dss = []
        for h in range(ATTN_HEADS):
            dph = dp[:, h * m:(h + 1) * m]
            dss.append(ps[h] * (dph - jnp.sum(dph * ps[h], axis=1, keepdims=True)))
        ds = (jnp.concatenate(dss, axis=1) * ATTN_SCALE).astype(BF16)
        dv_ref[...] += _dot_tn(jnp.concatenate(ps, axis=1), dout)
        dq_ref[...] = _dot_nt(ds, kbig_ref[...])
        dk_ref[...] += _dot_tn(q, ds)

    outs = [jax.ShapeDtypeStruct((s, N_IN_COLS), F32), jax.ShapeDtypeStruct(kbig.shape, F32),
            jax.ShapeDtypeStruct(vbig.shape, F32)]
    return pl.pallas_call(
        body, grid=(s // t,),
        in_specs=[pl.BlockSpec((t, D_GROUP), lambda i: (i, COL_Q)), pl.BlockSpec((t, D_GROUP), lambda i: (i, MIX_ATTN)),
                  _full_spec(kbig), _full_spec(vbig), _ANY],
        out_specs=[pl.BlockSpec((t, D_GROUP), lambda i: (i, COL_Q)), _full_spec(outs[1]), _full_spec(outs[2])],
        out_shape=outs, input_output_aliases={4: 0},
        compiler_params=_cparams(1), name=name)(h_in, dmix, kbig, vbig, dh_all)


FFN_TILE = 128
FFN_COL_CHUNK = 256
FFN_ROW_CHUNK = 64


def _ffn_conv(pad_ref, w_ref, b_ref, r0, ch, c0):
    cc = FFN_COL_CHUNK
    acc = jnp.broadcast_to(b_ref[:, c0:c0 + cc], (ch, cc))
    for k in range(FFN_CONV_WIDTH):
        o = SUBLANE - (FFN_CONV_WIDTH - 1) + k + r0
        acc = acc + w_ref[k:k + 1, c0:c0 + cc] * pad_ref[o:o + ch, c0:c0 + cc]
    return acc


def _ffn_gate_fwd(u, fcw, fcb, *, name):
    s = u.shape[0]
    t = _seq_tile(s, FFN_TILE)
    ch = min(FFN_ROW_CHUNK, t)
    cc = FFN_COL_CHUNK

    def body(u_ref, w_ref, b_ref, o_ref, upad):
        @pl.when(pl.program_id(0) == 0)
        def _():
            upad[0:SUBLANE, :] = jnp.zeros((SUBLANE, 2 * D_FF), F32)

        upad[SUBLANE:SUBLANE + t, :] = u_ref[...]
        for c0 in range(0, D_FF, cc):
            for r0 in range(0, t, ch):
                val = _ffn_conv(upad, w_ref, b_ref, r0, ch, c0)
                gt = _ffn_conv(upad, w_ref, b_ref, r0, ch, c0 + D_FF)
                o_ref[r0:r0 + ch, c0:c0 + cc] = (val * _gelu(gt)).astype(BF16)
        upad[0:SUBLANE, :] = upad[t:t + SUBLANE, :]

    return pl.pallas_call(
        body, grid=(s // t,),
        in_specs=[pl.BlockSpec((t, 2 * D_FF), lambda i: (i, 0)), _full_spec(fcw), _full_spec(fcb)],
        out_specs=pl.BlockSpec((t, D_FF), lambda i: (i, 0)),
        out_shape=jax.ShapeDtypeStruct((s, D_FF), BF16),
        scratch_shapes=[pltpu.VMEM((SUBLANE + t, 2 * D_FF), F32)],
        compiler_params=_cparams(1), name=name)(u, fcw, fcb)


def _ffn_gate_bwd(u, dh, fcw, fcb, *, name, rider=None):
    s = u.shape[0]
    t = _seq_tile(s, FFN_TILE)
    nt = s // t
    ch = min(FFN_ROW_CHUNK, t)
    cc = FFN_COL_CHUNK
    tb = t // SUBLANE

    def body(u_ref, halo_ref, dh_ref, w_ref, b_ref, du_ref, dw_ref, db_ref, upad, dpad):
        pid = pl.program_id(0)

        @pl.when(pid == 0)
        def _():
            dpad[t:t + SUBLANE, :] = jnp.zeros((SUBLANE, 2 * D_FF), F32)
            dw_ref[...] = jnp.zeros_like(dw_ref)
            db_ref[...] = jnp.zeros_like(db_ref)

        upad[0:SUBLANE, :] = jnp.where(pid == nt - 1, 0.0, halo_ref[...])
        upad[SUBLANE:SUBLANE + t, :] = u_ref[...]
        for c0 in range(0, D_FF, cc):
            for r0 in range(0, t, ch):
                val = _ffn_conv(upad, w_ref, b_ref, r0, ch, c0)
                gt = _ffn_conv(upad, w_ref, b_ref, r0, ch, c0 + D_FF)
                gl, dgl = _gelu_and_grad(gt)
                d = dh_ref[r0:r0 + ch, c0:c0 + cc]
                dpad[r0:r0 + ch, c0:c0 + cc] = d * gl
                dpad[r0:r0 + ch, c0 + D_FF:c0 + D_FF + cc] = d * val * dgl
        for c0 in range(0, 2 * D_FF, cc):
            dbs = jnp.zeros((1, cc), F32)
            dws = [jnp.zeros((1, cc), F32) for _ in range(FFN_CONV_WIDTH)]
            for r0 in range(0, t, ch):
                x = u_ref[r0:r0 + ch, c0:c0 + cc]
                acc = jnp.zeros((ch, cc), F32)
                for k in range(FFN_CONV_WIDTH):
                    o = (FFN_CONV_WIDTH - 1) - k + r0
                    sh = dpad[o:o + ch, c0:c0 + cc]
                    acc = acc + w_ref[k:k + 1, c0:c0 + cc] * sh
                    dws[k] = dws[k] + _colsum(x * sh)
                    if k == FFN_CONV_WIDTH - 1:
                        dbs = dbs + _colsum(sh)
                du_ref[r0:r0 + ch, c0:c0 + cc] = acc
            db_ref[:, c0:c0 + cc] += dbs
            for k in range(FFN_CONV_WIDTH):
                dw_ref[k:k + 1, c0:c0 + cc] += dws[k]
        dpad[t:t + SUBLANE, :] = dpad[0:SUBLANE, :]

    outs = [jax.ShapeDtypeStruct((s, 2 * D_FF), F32), jax.ShapeDtypeStruct((SUBLANE, 2 * D_FF), F32),
            jax.ShapeDtypeStruct((1, 2 * D_FF), F32)]
    return _call(
        body, grid=(nt,), ins=[u, u, dh, fcw, fcb],
        in_specs=[pl.BlockSpec((t, 2 * D_FF), lambda i: (nt - 1 - i, 0)),
                  pl.BlockSpec((SUBLANE, 2 * D_FF), lambda i: (jnp.maximum((nt - 1 - i) * tb - 1, 0), 0)),
                  pl.BlockSpec((t, D_FF), lambda i: (nt - 1 - i, 0)), _full_spec(fcw), _full_spec(fcb)],
        out_specs=[pl.BlockSpec((t, 2 * D_FF), lambda i: (nt - 1 - i, 0)), _full_spec(outs[1]), _full_spec(outs[2])],
        outs=outs,
        scratch=[pltpu.VMEM((SUBLANE + t, 2 * D_FF), F32), pltpu.VMEM((t + SUBLANE, 2 * D_FF), F32)],
        name=name, rider=rider)


def _adamw_body(g_ref, w_ref, m_ref, v_ref, go_ref, d_ref, mo_ref, vo_ref):
    inv_b1 = 1.0 - ADAM_B1 ** ADAM_STEP
    inv_b2 = 1.0 - ADAM_B2 ** ADAM_STEP
    g = g_ref[0]
    for dev in range(1, N_DEV):
        g = g + g_ref[dev]
    go_ref[...] = g
    mn = ADAM_B1 * m_ref[...] + (1.0 - ADAM_B1) * g
    vn = ADAM_B2 * v_ref[...] + (1.0 - ADAM_B2) * (g * g)
    mo_ref[...] = mn
    vo_ref[...] = vn
    d_ref[...] = -ADAM_LR * ((mn / inv_b1) / (jnp.sqrt(vn / inv_b2) + ADAM_EPS) + ADAM_WD * w_ref[...])


def _adamw(gstack, w, m, v, *, name):
    _, r, c = gstack.shape
    tr = _pick_rows(r, PACK_ROW_BLOCK)

    def body(*refs):
        _adamw_body(*refs)

    blk = pl.BlockSpec((tr, c), lambda i: (i, 0))
    sh = jax.ShapeDtypeStruct((r, c), F32)
    return pl.pallas_call(
        body, grid=(r // tr,),
        in_specs=[pl.BlockSpec((N_DEV, tr, c), lambda i: (0, i, 0)), blk, blk, blk],
        out_specs=[blk] * 4, out_shape=[sh] * 4,
        compiler_params=_cparams(1), name=name)(gstack, w, m, v)


def _adamw_layer(gstack, w, m, v, layer, into, *, name):
    n_layers, r, c = w.shape
    tr = _pick_rows(r, PACK_ROW_BLOCK)

    def body(g_ref, w_ref, m_ref, v_ref, *rest):
        _adamw_body(g_ref, w_ref, m_ref, v_ref, *rest[-4:])

    blk = pl.BlockSpec((None, tr, c), lambda i: (layer, i, 0))
    sh = jax.ShapeDtypeStruct((n_layers, r, c), F32)
    into = list(into or [])
    return pl.pallas_call(
        body, grid=(r // tr,),
        in_specs=[pl.BlockSpec((N_DEV, tr, c), lambda i: (0, i, 0)), blk, blk, blk] + [_ANY] * len(into),
        out_specs=[blk] * 4, out_shape=[sh] * 4, input_output_aliases={4 + k: k for k in range(len(into))},
        compiler_params=_cparams(1), name=name)(gstack, w, m, v, *into)


def _allgather(xs, slot_axes, *, name):
    n = len(xs)
    out_shapes = []
    for x, ax in zip(xs, slot_axes):
        shp = (N_DEV,) + x.shape if ax == 0 else (x.shape[0], N_DEV) + x.shape[1:]
        out_shapes.append(jax.ShapeDtypeStruct(shp, x.dtype))

    def body(*refs):
        x_refs, out_refs = refs[:n], refs[n:2 * n]
        send_sems, recv_sems, local_sems = refs[2 * n:]
        mx, my, mc = lax.axis_index("x"), lax.axis_index("y"), lax.axis_index("c")
        me, sibling = (mx, my, mc), (mx, my, 1 - mc)
        chips = [(1 - mx, my), (mx, 1 - my), (1 - mx, 1 - my)]

        def slot(i, px, py, pc):
            idx = 4 * px + 2 * py + pc
            if slot_axes[i] == 0:
                return out_refs[i].at[idx]
            return out_refs[i].at[pl.ds(0, out_refs[i].shape[0]), idx]

        def copy(i, k, block, to, src=None):
            return pltpu.make_async_remote_copy(
                src_ref=slot(i, *block) if src is None else src, dst_ref=slot(i, *block),
                send_sem=send_sems.at[i * N_PEERS + k], recv_sem=recv_sems.at[i * N_PEERS + k],
                device_id=to, device_id_type=_MESH)

        mine = [pltpu.make_async_copy(x_refs[i], slot(i, *me), local_sems.at[i]) for i in range(n)]
        for cp in mine:
            cp.start()
        first = []
        for i in range(n):
            first.append(copy(i, 0, me, sibling, src=x_refs[i]))
            first += [copy(i, 1 + j, me, (*chip, mc), src=x_refs[i]) for j, chip in enumerate(chips)]
        for cp in first:
            cp.start()
        passed = []
        for j, chip in enumerate(chips):
            for i in range(n):
                copy(i, 1 + j, (*chip, mc), me).wait_recv()
                fwd = copy(i, 4 + j, (*chip, mc), sibling)
                fwd.start()
                passed.append(fwd)
        for i in range(n):
            copy(i, 0, sibling, me).wait_recv()
            for j, chip in enumerate(chips):
                copy(i, 4 + j, (*chip, 1 - mc), me).wait_recv()
        for cp in first + passed:
            cp.wait_send()
        for cp in mine:
            cp.wait()

    return pl.pallas_call(
        body, in_specs=[_ANY] * n, out_specs=[_ANY] * n, out_shape=out_shapes,
        scratch_shapes=[pltpu.SemaphoreType.DMA((n * N_PEERS,)), pltpu.SemaphoreType.DMA((n * N_PEERS,)),
                        pltpu.SemaphoreType.DMA((n,))],
        name=name)(*xs)


def _exchange(rider, *, name):
    n = rider.n

    def body(*refs):
        x_refs, out_refs, sems = refs[:n], refs[n:2 * n], refs[2 * n:]
        rider.start(x_refs, out_refs, sems)
        rider.wait(x_refs, out_refs, sems)

    return pl.pallas_call(
        body, in_specs=[_ANY] * n, out_specs=[_ANY] * n, out_shape=rider.out_shapes(),
        scratch_shapes=rider.scratch(), name=name)(*rider.srcs)


def _pack_rows(n):
    rows = -(-n // PACK_COLS)
    return -(-rows // PACK_ROW_BLOCK) * PACK_ROW_BLOCK


def _pack(arrs, dtype):
    flat = jnp.concatenate([a.reshape(-1).astype(dtype) for a in arrs])
    rows = _pack_rows(flat.shape[0])
    flat = jnp.pad(flat, (0, rows * PACK_COLS - flat.shape[0]))
    return flat.reshape(rows, PACK_COLS)


def _pack_lead(arrs, dtype):
    flat = jnp.concatenate([a.reshape(N_DEV, -1).astype(dtype) for a in arrs], axis=1)
    rows = _pack_rows(flat.shape[1])
    flat = jnp.pad(flat, ((0, 0), (0, rows * PACK_COLS - flat.shape[1])))
    return flat.reshape(N_DEV, rows, PACK_COLS)


def _unpack(packed, shapes, lead=False):
    flat = packed.reshape(N_DEV, -1) if lead else packed.reshape(-1)
    out, pos = [], 0
    for sh in shapes:
        n = math.prod(sh)
        out.append(flat[:, pos:pos + n].reshape((N_DEV,) + tuple(sh)) if lead else flat[pos:pos + n].reshape(sh))
        pos += n
    return out


def _join_shards(stacked, axis):
    return jnp.concatenate([stacked[d] for d in range(N_DEV)], axis=axis)


def _split_shards(full, axis):
    return jnp.stack(jnp.split(full, N_DEV, axis=axis), axis=0)


def _join_cols(stacked):
    _, l, k, c = stacked.shape
    return jnp.transpose(stacked, (1, 2, 0, 3)).reshape(l, k, N_DEV * c)


def _split_cols(full):
    l, k, n = full.shape
    return jnp.transpose(full.reshape(l, k, N_DEV, n // N_DEV), (2, 0, 1, 3))


def _perm_in_cols(a, inverse=False):
    blocks = jnp.split(a, 6, axis=-1)
    if inverse:
        order = [IN_PERM.index(j) for j in range(6)]
    else:
        order = list(IN_PERM)
    return jnp.concatenate([blocks[j] for j in order], axis=-1)


def _row(v):
    return v.reshape(1, -1)


def _pad_rows(w, rows):
    return jnp.pad(w, ((0, rows - w.shape[0]), (0, 0)))


def _gn_avg_matrix():
    return _block_mask(GN_GROUPS, D_GROUP // GN_GROUPS, D_GROUP // GN_GROUPS) / (D_GROUP // GN_GROUPS)


def _layer_params(p, l):
    q = {}
    (a, wb, wc), q["s5_vjp"] = jax.vjp(_s5_param_map, p["s5_lam_re"][l], p["s5_lam_im"][l], p["s5_log_dt"][l],
                                       p["s5_b_re"][l], p["s5_b_im"][l], p["s5_c_re"][l], p["s5_c_im"][l])
    q["wb"], q["wc"] = wb.astype(BF16), wc.astype(BF16)
    q["apow"] = _s5_apow(a, max(S5_TILE.bit_length() - 1, 1))
    (q["wr"], q["wi"]), q["lru_w_vjp"] = jax.vjp(lambda r, i: (_blockdiag(r), _blockdiag(i)), p["lru_w_r"][l], p["lru_w_i"][l])
    q["wr"], q["wi"] = q["wr"].astype(BF16), q["wi"].astype(BF16)
    q["sp"], q["sp_vjp"] = jax.vjp(lambda lam: _row(jax.nn.softplus(-lam)), p["lru_lam"][l])
    return q


WEIGHT_RIDES = {(0, "s5_fwd"): [("ffn_w_up", 1)], (0, "cv_fwd"): [("ffn_w_down", 1)],
                (0, "ffn_up"): [("w_in", 1), ("attn_w_kv", 1), ("w_out", 1)]}
GRAD_RIDES = {(1, "ffn_gate_bwd"): [("ffn_w_down", 1)],
              (0, "dw_down"): [("w_out", 1), ("attn_w_kv", 1), ("w_in", 1)],
              (0, "ffn_gate_bwd"): [("ffn_w_up", 1)],
              (0, "dx1"): [("ffn_w_down", 0)],
              (0, "cv_bwd"): [("w_out", 0)],
              (0, "s5_bwd"): [("ffn_w_up", 0)]}


def _assemble_weight(n, gathered):
    if SHARDED[n] == 2:
        full = jnp.transpose(gathered, (1, 0, 2)).reshape(gathered.shape[1], -1)
        return _perm_in_cols(full) if n == "w_in" else full
    return gathered.reshape(-1, gathered.shape[-1])


def _grad_source(n, g):
    if SHARDED[n] == 2:
        if n == "w_in":
            g = _perm_in_cols(g, inverse=True)
        k, nn = g.shape
        return jnp.transpose(g.reshape(k, N_DEV, nn // N_DEV), (1, 0, 2)), "lead"
    return g, "rows"


def _hosted(fn, keys_rider, land, *args, **kw):
    keys, rider = keys_rider
    if rider is None:
        return fn(*args, **kw)
    out, routs = fn(*args, rider=rider, **kw)
    land(keys, routs)
    return out


def _local_step(x, mem, target, p, big_w, shards=None):
    dist = shards is not None
    small, saved = {}, []
    big_g, ready, recv = {}, {}, {}
    mavg = _gn_avg_matrix()

    def weight_rider(l, host):
        keys = WEIGHT_RIDES.get((l, host), []) if dist else []
        return keys, (_Rider([shards[n][ll] for n, ll in keys], ["all"] * len(keys)) if keys else None)

    def land_weights(keys, routs):
        for (n, ll), r in zip(keys, routs):
            big_w[n][ll] = _assemble_weight(n, r)

    def grad_rider(l, host):
        keys = [k for k in GRAD_RIDES.get((l, host), []) if k in ready] if dist else []
        return keys, (_Rider([ready[k][0] for k in keys], [ready[k][1] for k in keys]) if keys else None)

    def land_grads(keys, routs):
        for k, r in zip(keys, routs):
            recv[k] = r
            del ready[k]

    def big_grad(n, l, g):
        if dist:
            ready[(n, l)] = _grad_source(n, g)
        else:
            big_g[(n, l)] = g

    xs = _ln_fwd(x, _row(p["ln_in_g"]), _row(p["ln_in_b"]), name="ln_in_fwd")
    for l in range(DEPTH):
        q = _layer_params(p, l)
        n = f"l{l}_"
        hin = _mm(xs, big_w["w_in"][l], bias=_row(p["b_in"][l]), name=n + "inproj")
        keys, rd = weight_rider(l, "s5_fwd")
        (mix, s5_y1, s5_hb), routs = _s5_fwd(hin, q["wb"], q["apow"], q["wc"], _row(p["s5_d"][l]), p["s5_w_glu"][l],
                                             _row(p["s5_b_glu"][l]), name=n + "s5_fwd", rider=rd)
        land_weights(keys, routs)
        cvw = _pad_rows(p["cv_w"][l], CV_PAD)
        keys, rd = weight_rider(l, "cv_fwd")
        (mix, cv_c), routs = _cv_fwd(hin, cvw, _row(p["cv_b"][l]), _row(p["cv_gn_g"][l]), _row(p["cv_gn_b"][l]), mavg,
                                     p["cv_w_pw"][l], _row(p["cv_b_pw"][l]), mix, name=n + "cv_fwd", rider=rd)
        land_weights(keys, routs)
        lcw = _pad_rows(p["lru_conv_w"][l], SUBLANE)
        mix, lru_xc, lru_h = _lru_fwd(hin, lcw, _row(p["lru_conv_b"][l]), q["wr"], _row(p["lru_b_r"][l]), q["wi"],
                                      _row(p["lru_b_i"][l]), q["sp"], mix, name=n + "lru_fwd")
        kv = _mm(mem, big_w["attn_w_kv"][l], name=n + "kv")
        (kbig, vbig), kv_vjp = jax.vjp(_attn_big, kv)
        kbig, vbig = kbig.astype(BF16), vbig.astype(BF16)
        mix = _attn_fwd(hin, kbig, vbig, mix, name=n + "attn_fwd")
        r1 = _mm(mix, big_w["w_out"][l], bias=_row(p["b_out"][l]), res=xs, res_scale=ALPHA, name=n + "outproj")
        x1 = _ln_fwd(r1, _row(p["ln1_g"][l]), _row(p["ln1_b"][l]), name=n + "ln1_fwd")
        u = _hosted(_mm, weight_rider(l, "ffn_up"), land_weights, x1, big_w["ffn_w_up"][l], name=n + "ffn_up")
        fcw = _pad_rows(p["ffn_conv_w"][l], SUBLANE)
        fcb = _row(p["ffn_conv_b"][l])
        hff = _ffn_gate_fwd(u, fcw, fcb, name=n + "ffn_gate_fwd")
        r2 = _mm(hff, big_w["ffn_w_down"][l], res=x1, res_scale=ALPHA, name=n + "ffn_down")
        x2 = _ln_fwd(r2, _row(p["ln2_g"][l]), _row(p["ln2_b"][l]), name=n + "ln2_fwd")
        saved.append(dict(q=q, xs=xs, hin=hin, s5_y1=s5_y1, s5_hb=s5_hb, cvw=cvw, cv_c=cv_c, lcw=lcw, lru_xc=lru_xc,
                          lru_h=lru_h, kbig=kbig, vbig=vbig, kv_vjp=kv_vjp, mix=mix, r1=r1, x1=x1, u=u, fcw=fcw,
                          fcb=fcb, hff=hff, r2=r2))
        xs = x2

    dx, loss_blk = _loss_grad(xs, target, name="loss_grad")
    loss = loss_blk[0, 0]

    for l in reversed(range(DEPTH)):
        sv = saved[l]
        q = sv["q"]
        n = f"l{l}_"
        g = {}
        dr2, g["ln2_g"], g["ln2_b"], _ = _ln_bwd(sv["r2"], dx, _row(p["ln2_g"][l]), name=n + "ln2_bwd")
        big_grad("ffn_w_down", l, _hosted(_mm_tn, grad_rider(l, "dw_down"), land_grads, sv["hff"], dr2, name=n + "dw_down"))
        dhff = _mm(dr2, big_w["ffn_w_down"][l], trans_b=True, name=n + "dhff")
        keys, rd = grad_rider(l, "ffn_gate_bwd")
        (du, dfw, g["ffn_conv_b"]), routs = _ffn_gate_bwd(sv["u"], dhff, sv["fcw"], sv["fcb"], name=n + "ffn_gate_bwd",
                                                          rider=rd)
        land_grads(keys, routs)
        g["ffn_conv_w"] = dfw[:FFN_CONV_WIDTH]
        big_grad("ffn_w_up", l, _mm_tn(sv["x1"], du, name=n + "dw_up"))
        dx1 = _hosted(_mm, grad_rider(l, "dx1"), land_grads, du, big_w["ffn_w_up"][l], trans_b=True, res=dr2,
                      res_scale=ALPHA, name=n + "dx1")
        dr1, g["ln1_g"], g["ln1_b"], g["b_out"] = _ln_bwd(sv["r1"], dx1, _row(p["ln1_g"][l]), name=n + "ln1_bwd")
        big_grad("w_out", l, _mm_tn(sv["mix"], dr1, name=n + "dw_out"))
        dmix = _mm(dr1, big_w["w_out"][l], trans_b=True, name=n + "dmix")

        hin = sv["hin"]
        keys, rd = grad_rider(l, "cv_bwd")
        (dh, g["cv_w_pw"], dcw, g["cv_b_pw"], g["cv_gn_g"], g["cv_gn_b"], g["cv_b"]), routs = _cv_bwd(
            hin, sv["cv_c"], dmix, sv["cvw"], _row(p["cv_gn_g"][l]), _row(p["cv_gn_b"][l]), mavg, p["cv_w_pw"][l],
            name=n + "cv_bwd", rider=rd)
        land_grads(keys, routs)
        g["cv_w"] = dcw[:CONV_WIDTH]
        dh, dwr, dwi, dlcw, g["lru_b_r"], g["lru_b_i"], dsp, g["lru_conv_b"] = _lru_bwd(
            hin, sv["lru_xc"], sv["lru_h"], dmix, sv["lcw"], q["wr"], _row(p["lru_b_r"][l]), q["wi"],
            _row(p["lru_b_i"][l]), q["sp"], dh, name=n + "lru_bwd")
        g["lru_conv_w"] = dlcw[:LRU_CONV_WIDTH]
        g["lru_w_r"], g["lru_w_i"] = q["lru_w_vjp"]((dwr, dwi))
        (g["lru_lam"],) = q["sp_vjp"](dsp)
        keys, rd = grad_rider(l, "s5_bwd")
        (dh, g["s5_w_glu"], dwc, dwb, g["s5_b_glu"], g["s5_d"], da), routs = _s5_bwd(
            hin, sv["s5_y1"], dmix, sv["s5_hb"], q["wb"], q["apow"], q["wc"], _row(p["s5_d"][l]), p["s5_w_glu"][l],
            _row(p["s5_b_glu"][l]), dh, name=n + "s5_bwd", rider=rd)
        land_grads(keys, routs)
        (g["s5_lam_re"], g["s5_lam_im"], g["s5_log_dt"], g["s5_b_re"], g["s5_b_im"], g["s5_c_re"],
         g["s5_c_im"]) = q["s5_vjp"]((da, dwb, dwc))
        dh, dkbig, dvbig = _attn_bwd(hin, dmix, sv["kbig"], sv["vbig"], dh, name=n + "attn_bwd")
        (dkv,) = sv["kv_vjp"]((dkbig, dvbig))
        big_grad("attn_w_kv", l, _mm_tn(mem, dkv, name=n + "dw_kv"))

        g["b_in"] = _colsum_call(dh, name=n + "db_in")
        big_grad("w_in", l, _mm_tn(sv["xs"], dh, name=n + "dw_in"))
        dx = _mm(dh, big_w["w_in"][l], trans_b=True, res=dr1, res_scale=ALPHA, name=n + "dxs")
        for k, v in g.items():
            small.setdefault(k, [None] * DEPTH)[l] = v.reshape(p[k].shape[1:])

    grad_x, dgi, dbi, _ = _ln_bwd(x, dx, _row(p["ln_in_g"]), name="ln_in_bwd")
    out = {k: jnp.stack(v, axis=0) for k, v in small.items()}
    out["ln_in_g"], out["ln_in_b"] = dgi.reshape(-1), dbi.reshape(-1)
    return loss, grad_x, out, ((recv, ready) if dist else big_g)


def kernel(x, mem, ln_in_g, ln_in_b, w_in, b_in, s5_lam_re, s5_lam_im, s5_log_dt, s5_b_re, s5_b_im, s5_c_re, s5_c_im, s5_d, s5_w_glu, s5_b_glu, cv_w, cv_b, cv_gn_g, cv_gn_b, cv_w_pw, cv_b_pw, lru_conv_w, lru_conv_b, lru_w_r, lru_b_r, lru_w_i, lru_b_i, lru_lam, attn_w_kv, w_out, b_out, ln1_g, ln1_b, ffn_w_up, ffn_conv_w, ffn_conv_b, ffn_w_down, ln2_g, ln2_b, loss_target, m_ln_in_g, m_ln_in_b, m_w_in, m_b_in, m_s5_lam_re, m_s5_lam_im, m_s5_log_dt, m_s5_b_re, m_s5_b_im, m_s5_c_re, m_s5_c_im, m_s5_d, m_s5_w_glu, m_s5_b_glu, m_cv_w, m_cv_b, m_cv_gn_g, m_cv_gn_b, m_cv_w_pw, m_cv_b_pw, m_lru_conv_w, m_lru_conv_b, m_lru_w_r, m_lru_b_r, m_lru_w_i, m_lru_b_i, m_lru_lam, m_attn_w_kv, m_w_out, m_b_out, m_ln1_g, m_ln1_b, m_ffn_w_up, m_ffn_conv_w, m_ffn_conv_b, m_ffn_w_down, m_ln2_g, m_ln2_b, v_ln_in_g, v_ln_in_b, v_w_in, v_b_in, v_s5_lam_re, v_s5_lam_im, v_s5_log_dt, v_s5_b_re, v_s5_b_im, v_s5_c_re, v_s5_c_im, v_s5_d, v_s5_w_glu, v_s5_b_glu, v_cv_w, v_cv_b, v_cv_gn_g, v_cv_gn_b, v_cv_w_pw, v_cv_b_pw, v_lru_conv_w, v_lru_conv_b, v_lru_w_r, v_lru_b_r, v_lru_w_i, v_lru_b_i, v_lru_lam, v_attn_w_kv, v_w_out, v_b_out, v_ln1_g, v_ln1_b, v_ffn_w_up, v_ffn_conv_w, v_ffn_conv_b, v_ffn_w_down, v_ln2_g, v_ln2_b):
    args = locals()
    w = {n: args[n] for n in WEIGHTS}
    mom = {n: args["m_" + n] for n in WEIGHTS}
    var = {n: args["v_" + n] for n in WEIGHTS}

    shards = {n: w[n].astype(BF16) for n in BIG}
    small_shapes = [w[n].shape for n in SMALL_SHARDED]
    gathered = _allgather([shards[n][0] for n in BIG] + [_pack([w[n] for n in SMALL_SHARDED], F32)],
                          [0] * (len(BIG) + 1), name="ag_weights")
    big_w = {n: [_assemble_weight(n, got)] + [None] * (DEPTH - 1) for n, got in zip(BIG, gathered)}
    p = {n: w[n] for n in REPLICATED}
    for n, st in zip(SMALL_SHARDED, _unpack(gathered[-1], small_shapes, lead=True)):
        p[n] = _join_shards(st, SHARDED[n])
    for n in ("s5_w_glu", "cv_w_pw"):
        p[n] = p[n].astype(BF16)
    p["b_in"] = _perm_in_cols(p["b_in"])

    loss, grad_x, g_small, (recv, ready) = _local_step(x[0], mem[0], loss_target[0], p, big_w, shards)
    loss = lax.psum(loss, ("x", "y", "c"))
    g_small["b_in"] = _perm_in_cols(g_small["b_in"], inverse=True)

    left = list(ready)
    rider = _Rider(
        [ready[k][0] for k in left] + [_pack_lead([_split_shards(g_small[n], SHARDED[n]) for n in SMALL_SHARDED], F32),
                                       _pack([g_small[n] for n in REPLICATED], F32)],
        [ready[k][1] for k in left] + ["lead", "all"])
    got = _exchange(rider, name="exchange_grads")
    for k, r in zip(left, got):
        recv[k] = r

    res = [dict(), dict(), dict(), dict()]
    for n in BIG:
        outs = None
        for l in range(DEPTH):
            outs = _adamw_layer(recv[(n, l)], w[n], mom[n], var[n], l, outs, name=f"adamw_{n}_l{l}")
        for kind in range(4):
            res[kind][n] = outs[kind]
    for names, got_pack, tag in ((SMALL_SHARDED, got[len(left)], "adamw_small_sharded"),
                                 (REPLICATED, got[len(left) + 1], "adamw_replicated")):
        outs = _adamw(got_pack, _pack([w[n] for n in names], F32), _pack([mom[n] for n in names], F32),
                      _pack([var[n] for n in names], F32), name=tag)
        for kind in range(4):
            for n, a in zip(names, _unpack(outs[kind], [w[n].shape for n in names])):
                res[kind][n] = a
    return (loss, grad_x[None], *[res[0][n] for n in WEIGHTS], *[res[1][n] for n in WEIGHTS],
            *[res[2][n] for n in WEIGHTS], *[res[3][n] for n in WEIGHTS])
```

```python
import math

import jax
import jax.numpy as jnp
from jax import lax
from jax.experimental import pallas as pl
from jax.experimental.pallas import tpu as pltpu

F32 = jnp.float32
BF16 = jnp.bfloat16

D_MODEL = 1024
DEPTH = 2
D_GROUP = 256
N_IN_COLS = 6 * D_GROUP
S5_GROUPS = 16
S5_CH = 16
S5_STATE = 64
S5_LANES = S5_GROUPS * S5_STATE
CONV_WIDTH = 31
GN_GROUPS = 4
LRU_HEADS = 4
LRU_CONV_WIDTH = 4
LRU_C = 8.0
ATTN_HEADS = 4
ATTN_HEAD_DIM = 64
D_FF = 2816
FFN_CONV_WIDTH = 3
ALPHA = (2 * DEPTH) ** 0.25
LN_EPS = 1e-5
ADAM_LR, ADAM_B1, ADAM_B2, ADAM_EPS, ADAM_WD, ADAM_STEP = 0.001, 0.9, 0.999, 1e-08, 0.01, 10

N_DEV = 8
N_PEERS = N_DEV - 1
LANE = 128
SUBLANE = 8
VMEM_LIMIT = 56 * 1024 * 1024
PACK_COLS = 1024
PACK_ROW_BLOCK = 256

SHARDED = {
    "w_in": 2, "s5_w_glu": 1, "cv_w": 2, "cv_w_pw": 1, "lru_conv_w": 2, "attn_w_kv": 1,
    "w_out": 1, "ffn_w_up": 2, "ffn_conv_w": 2, "ffn_w_down": 1,
}
BIG = ("w_in", "attn_w_kv", "w_out", "ffn_w_up", "ffn_w_down")
SMALL_SHARDED = ("s5_w_glu", "cv_w", "cv_w_pw", "lru_conv_w", "ffn_conv_w")
MATMUL_WEIGHTS = ("w_in", "s5_w_glu", "cv_w_pw", "attn_w_kv", "w_out", "ffn_w_up", "ffn_w_down")
WEIGHTS = ['ln_in_g', 'ln_in_b', 'w_in', 'b_in', 's5_lam_re', 's5_lam_im', 's5_log_dt', 's5_b_re', 's5_b_im',
           's5_c_re', 's5_c_im', 's5_d', 's5_w_glu', 's5_b_glu', 'cv_w', 'cv_b', 'cv_gn_g', 'cv_gn_b', 'cv_w_pw',
           'cv_b_pw', 'lru_conv_w', 'lru_conv_b', 'lru_w_r', 'lru_b_r', 'lru_w_i', 'lru_b_i', 'lru_lam',
           'attn_w_kv', 'w_out', 'b_out', 'ln1_g', 'ln1_b', 'ffn_w_up', 'ffn_conv_w', 'ffn_conv_b', 'ffn_w_down',
           'ln2_g', 'ln2_b']
REPLICATED = [n for n in WEIGHTS if n not in SHARDED]

COL_CV_V, COL_CV_G, COL_LRU_G, COL_LRU_X, COL_S5, COL_Q = range(6)
IN_PERM = (1, 2, 3, 4, 0, 5)
MIX_S5, MIX_CV, MIX_LRU, MIX_ATTN = range(4)


_ANY = pl.BlockSpec(memory_space=pl.ANY)
_MESH = pl.DeviceIdType.MESH


def _cparams(n_axes):
    return pltpu.CompilerParams(dimension_semantics=("arbitrary",) * n_axes, vmem_limit_bytes=VMEM_LIMIT)


def _pick(n, cap):
    if n <= cap:
        return n
    best = None
    for t in range(LANE, cap + 1, LANE):
        if n % t == 0:
            best = t
    assert best is not None, (n, cap)
    return best


def _pick_rows(n, cap):
    best = None
    for t in range(SUBLANE, min(n, cap) + 1, SUBLANE):
        if n % t == 0:
            best = t
    assert best is not None, (n, cap)
    return best


def _full_spec(arr):
    nd = arr.ndim
    return pl.BlockSpec(arr.shape, lambda *_: (0,) * nd)


def _dot(a, b):
    return lax.dot_general(a.astype(BF16), b.astype(BF16), (((1,), (0,)), ((), ())), preferred_element_type=F32)


def _dot_nt(a, b):
    return lax.dot_general(a.astype(BF16), b.astype(BF16), (((1,), (1,)), ((), ())), preferred_element_type=F32)


def _dot_tn(a, b):
    return lax.dot_general(a.astype(BF16), b.astype(BF16), (((0,), (0,)), ((), ())), preferred_element_type=F32)


def _dot_hi(a, b):
    return jnp.dot(a, b, precision=lax.Precision.HIGHEST, preferred_element_type=F32)


def _colsum(x):
    return jnp.sum(x, axis=0, keepdims=True)


def _sigmoid(x):
    return 1.0 / (1.0 + jnp.exp(-x))


_GELU_K = math.sqrt(2.0 / math.pi)
_GELU_C = 0.044715


def _gelu(x):
    t = jnp.tanh(_GELU_K * (x + _GELU_C * x * x * x))
    return 0.5 * x * (1.0 + t)


def _gelu_and_grad(x):
    x2 = x * x
    t = jnp.tanh(_GELU_K * (x + _GELU_C * x2 * x))
    g = 0.5 * x * (1.0 + t)
    dg = 0.5 * (1.0 + t) + 0.5 * x * (1.0 - t * t) * (_GELU_K * (1.0 + 3.0 * _GELU_C * x2))
    return g, dg


def _neg_expm1(x):
    series = x * (1.0 + x * (0.5 + x * (1.0 / 6.0 + x * (1.0 / 24.0 + x * (1.0 / 120.0)))))
    return -jnp.where(jnp.abs(x) < 0.1, series, jnp.exp(x) - 1.0)


def _seq_tile(s, want):
    t = min(s, want)
    assert s % t == 0
    return t


class _Rider:
    def __init__(self, srcs, kinds):
        self.srcs, self.kinds = list(srcs), list(kinds)
        self.n = len(self.srcs)

    def out_shapes(self):
        shapes = []
        for x, kind in zip(self.srcs, self.kinds):
            if kind == "lead":
                shp = x.shape
            elif kind == "rows":
                shp = (N_DEV, x.shape[0] // N_DEV) + x.shape[1:]
            else:
                shp = (N_DEV,) + x.shape
            shapes.append(jax.ShapeDtypeStruct(shp, x.dtype))
        return shapes

    def scratch(self):
        return [pltpu.SemaphoreType.DMA((self.n * N_PEERS,)), pltpu.SemaphoreType.DMA((self.n * N_PEERS,)),
                pltpu.SemaphoreType.DMA((self.n,))]

    def _copies(self, x_refs, out_refs, sems):
        send_sems, recv_sems, local_sems = sems
        mx, my, mc = lax.axis_index("x"), lax.axis_index("y"), lax.axis_index("c")
        my_id = 4 * mx + 2 * my + mc

        def piece(i, dev):
            if self.kinds[i] == "lead":
                return x_refs[i].at[dev]
            if self.kinds[i] == "rows":
                r = x_refs[i].shape[0] // N_DEV
                return x_refs[i].at[pl.ds(pl.multiple_of(dev * r, SUBLANE), r)]
            return x_refs[i]

        mine = [pltpu.make_async_copy(piece(i, my_id), out_refs[i].at[my_id], local_sems.at[i]) for i in range(self.n)]
        copies = []
        for k in range(1, N_DEV):
            px, py, pc = mx ^ ((k >> 2) & 1), my ^ ((k >> 1) & 1), mc ^ (k & 1)
            for i in range(self.n):
                copies.append(pltpu.make_async_remote_copy(
                    src_ref=piece(i, 4 * px + 2 * py + pc), dst_ref=out_refs[i].at[my_id],
                    send_sem=send_sems.at[i * N_PEERS + k - 1], recv_sem=recv_sems.at[i * N_PEERS + k - 1],
                    device_id=(px, py, pc), device_id_type=_MESH))
        return mine, copies

    def start(self, x_refs, out_refs, sems):
        mine, copies = self._copies(x_refs, out_refs, sems)
        for cp in mine + copies:
            cp.start()

    def wait(self, x_refs, out_refs, sems):
        mine, copies = self._copies(x_refs, out_refs, sems)
        for cp in copies:
            cp.wait_recv()
        for cp in copies:
            cp.wait_send()
        for cp in mine:
            cp.wait()


def _call(body, *, grid, ins, in_specs, outs, out_specs, scratch=(), aliases=None, name, rider=None):
    n_axes = len(grid)
    common = dict(grid=grid, input_output_aliases=aliases or {}, compiler_params=_cparams(n_axes), name=name)
    if rider is None:
        res = pl.pallas_call(body, in_specs=list(in_specs), out_specs=list(out_specs), out_shape=list(outs),
                             scratch_shapes=list(scratch), **common)(*ins)
        return list(res), []
    n_in, n_out, n_scr, nr = len(ins), len(outs), len(scratch), rider.n

    def wrapped(*refs):
        pos = [0]

        def take(k):
            part = refs[pos[0]:pos[0] + k]
            pos[0] += k
            return part

        a_in, r_in, a_out, r_out, a_scr, sems = take(n_in), take(nr), take(n_out), take(nr), take(n_scr), take(3)
        first = last = None
        for ax in range(n_axes):
            pid = pl.program_id(ax)
            f, l = pid == 0, pid == grid[ax] - 1
            first = f if first is None else jnp.logical_and(first, f)
            last = l if last is None else jnp.logical_and(last, l)

        @pl.when(first)
        def _():
            rider.start(r_in, r_out, sems)

        body(*a_in, *a_out, *a_scr)

        @pl.when(last)
        def _():
            rider.wait(r_in, r_out, sems)

    res = pl.pallas_call(
        wrapped, in_specs=list(in_specs) + [_ANY] * nr, out_specs=list(out_specs) + [_ANY] * nr,
        out_shape=list(outs) + rider.out_shapes(), scratch_shapes=list(scratch) + rider.scratch(), **common)(*ins, *rider.srcs)
    return list(res[:n_out]), list(res[n_out:])


def _block_mask(n_blocks, block_rows, block_cols):
    r = jnp.arange(n_blocks * block_rows) // block_rows
    c = jnp.arange(n_blocks * block_cols) // block_cols
    return (r[:, None] == c[None, :]).astype(F32)


def _mm(a, b, *, bias=None, res=None, res_scale=1.0, trans_b=False, out_dtype=F32, name, rider=None):
    m, kdim = a.shape
    n = b.shape[0] if trans_b else b.shape[1]
    tm = _seq_tile(m, 1024)
    tn = _pick(n, 1408)
    tk = _pick(kdim, 1536)
    nk = kdim // tk
    has_bias, has_res = bias is not None, res is not None

    def body(*refs):
        a_ref, b_ref = refs[0], refs[1]
        pos = 2
        bias_ref = res_ref = None
        if has_bias:
            bias_ref = refs[pos]
            pos += 1
        if has_res:
            res_ref = refs[pos]
            pos += 1
        o_ref, acc_ref = refs[pos], refs[pos + 1]
        k = pl.program_id(2)

        @pl.when(k == 0)
        def _():
            acc_ref[...] = jnp.zeros_like(acc_ref)

        if trans_b:
            acc_ref[...] += _dot_nt(a_ref[...], b_ref[...])
        else:
            acc_ref[...] += _dot(a_ref[...], b_ref[...])

        @pl.when(k == nk - 1)
        def _():
            r = acc_ref[...]
            if has_bias:
                r = r + bias_ref[...]
            if has_res:
                r = r + res_scale * res_ref[...]
            o_ref[...] = r.astype(out_dtype)

    ins = [a, b]
    in_specs = [pl.BlockSpec((tm, tk), lambda i, j, k: (i, k)),
                pl.BlockSpec((tn, tk), lambda i, j, k: (j, k)) if trans_b
                else pl.BlockSpec((tk, tn), lambda i, j, k: (k, j))]
    if has_bias:
        ins.append(bias)
        in_specs.append(pl.BlockSpec((1, tn), lambda i, j, k: (0, j)))
    if has_res:
        ins.append(res)
        in_specs.append(pl.BlockSpec((tm, tn), lambda i, j, k: (i, j)))
    (out,), routs = _call(
        body, grid=(m // tm, n // tn, nk), ins=ins, in_specs=in_specs,
        outs=[jax.ShapeDtypeStruct((m, n), out_dtype)], out_specs=[pl.BlockSpec((tm, tn), lambda i, j, k: (i, j))],
        scratch=[pltpu.VMEM((tm, tn), F32)], name=name, rider=rider)
    return out if rider is None else (out, routs)


def _mm_tn(a, b, *, name, rider=None):
    s, ka = a.shape
    nb = b.shape[1]
    ts = _seq_tile(s, 512)
    tka = _pick(ka, 1408)
    tnb = _pick(nb, 1408)

    def body(a_ref, b_ref, o_ref):
        @pl.when(pl.program_id(2) == 0)
        def _():
            o_ref[...] = jnp.zeros_like(o_ref)

        o_ref[...] += _dot_tn(a_ref[...], b_ref[...])

    (out,), routs = _call(
        body, grid=(ka // tka, nb // tnb, s // ts), ins=[a, b],
        in_specs=[pl.BlockSpec((ts, tka), lambda i, j, k: (k, i)), pl.BlockSpec((ts, tnb), lambda i, j, k: (k, j))],
        outs=[jax.ShapeDtypeStruct((ka, nb), F32)], out_specs=[pl.BlockSpec((tka, tnb), lambda i, j, k: (i, j))],
        name=name, rider=rider)
    return out if rider is None else (out, routs)


def _colsum_call(x, *, name):
    s, n = x.shape
    ts = _seq_tile(s, 512)

    def body(x_ref, o_ref):
        @pl.when(pl.program_id(0) == 0)
        def _():
            o_ref[...] = jnp.zeros_like(o_ref)

        o_ref[...] += _colsum(x_ref[...])

    return pl.pallas_call(
        body, grid=(s // ts,), in_specs=[pl.BlockSpec((ts, n), lambda i: (i, 0))],
        out_specs=pl.BlockSpec((1, n), lambda i: (0, 0)), out_shape=jax.ShapeDtypeStruct((1, n), F32),
        compiler_params=_cparams(1), name=name)(x)


def _ln_fwd(r, g, b, *, name):
    s, d = r.shape
    ts = _seq_tile(s, 512)

    def body(r_ref, g_ref, b_ref, o_ref):
        x = r_ref[...]
        mu = jnp.mean(x, axis=1, keepdims=True)
        xc = x - mu
        var = jnp.mean(xc * xc, axis=1, keepdims=True)
        o_ref[...] = xc * lax.rsqrt(var + LN_EPS) * g_ref[...] + b_ref[...]

    return pl.pallas_call(
        body, grid=(s // ts,),
        in_specs=[pl.BlockSpec((ts, d), lambda i: (i, 0)), _full_spec(g), _full_spec(b)],
        out_specs=pl.BlockSpec((ts, d), lambda i: (i, 0)), out_shape=jax.ShapeDtypeStruct((s, d), F32),
        compiler_params=_cparams(1), name=name)(r, g, b)


def _ln_bwd(r, dy, g, *, name):
    s, d = r.shape
    ts = _seq_tile(s, 512)

    def body(r_ref, dy_ref, g_ref, dr_ref, dg_ref, db_ref, ds_ref):
        @pl.when(pl.program_id(0) == 0)
        def _():
            dg_ref[...] = jnp.zeros_like(dg_ref)
            db_ref[...] = jnp.zeros_like(db_ref)
            ds_ref[...] = jnp.zeros_like(ds_ref)

        x = r_ref[...]
        dy = dy_ref[...]
        mu = jnp.mean(x, axis=1, keepdims=True)
        xc = x - mu
        var = jnp.mean(xc * xc, axis=1, keepdims=True)
        rstd = lax.rsqrt(var + LN_EPS)
        xh = xc * rstd
        dxh = dy * g_ref[...]
        m1 = jnp.mean(dxh, axis=1, keepdims=True)
        m2 = jnp.mean(dxh * xh, axis=1, keepdims=True)
        dr = rstd * (dxh - m1 - xh * m2)
        dr_ref[...] = dr
        dg_ref[...] += _colsum(dy * xh)
        db_ref[...] += _colsum(dy)
        ds_ref[...] += _colsum(dr)

    vec = jax.ShapeDtypeStruct((1, d), F32)
    vspec = pl.BlockSpec((1, d), lambda i: (0, 0))
    return pl.pallas_call(
        body, grid=(s // ts,),
        in_specs=[pl.BlockSpec((ts, d), lambda i: (i, 0)), pl.BlockSpec((ts, d), lambda i: (i, 0)), _full_spec(g)],
        out_specs=[pl.BlockSpec((ts, d), lambda i: (i, 0)), vspec, vspec, vspec],
        out_shape=[jax.ShapeDtypeStruct((s, d), F32), vec, vec, vec],
        compiler_params=_cparams(1), name=name)(r, dy, g)


def _loss_grad(y, target, *, name):
    s, d = y.shape
    ts = _seq_tile(s, 512)

    def body(y_ref, t_ref, dy_ref, l_ref):
        @pl.when(pl.program_id(0) == 0)
        def _():
            l_ref[...] = jnp.zeros_like(l_ref)

        e = y_ref[...] - t_ref[...]
        dy_ref[...] = e * (1.0 / d)
        part = jnp.sum(jnp.sum(e * e, axis=1, keepdims=True), axis=0, keepdims=True) * (0.5 / d)
        l_ref[...] += jnp.broadcast_to(part, l_ref.shape)

    return pl.pallas_call(
        body, grid=(s // ts,),
        in_specs=[pl.BlockSpec((ts, d), lambda i: (i, 0)), pl.BlockSpec((ts, d), lambda i: (i, 0))],
        out_specs=[pl.BlockSpec((ts, d), lambda i: (i, 0)), pl.BlockSpec((SUBLANE, LANE), lambda i: (0, 0))],
        out_shape=[jax.ShapeDtypeStruct((s, d), F32), jax.ShapeDtypeStruct((SUBLANE, LANE), F32)],
        compiler_params=_cparams(1), name=name)(y, target)


SCAN_CHUNK = 32


def _cscan_levels(bufs, apow_ref, t, pad, *, reverse):
    half = bufs[0].shape[1] // 2
    ch = min(SCAN_CHUNK, t)
    nlev = t.bit_length() - 1
    assert (1 << nlev) == t
    for k in range(nlev):
        d = 1 << k
        src, dst = bufs[k % 2], bufs[(k + 1) % 2]

        def chunk(c, carry, src=src, dst=dst, d=d, k=k):
            ar = apow_ref[k:k + 1, :half]
            ai = apow_ref[k:k + 1, half:]
            if reverse:
                ai = -ai
            r0 = pl.multiple_of(c * ch, ch)
            cur = src[pl.ds(pad + r0, ch), :]
            if d >= SUBLANE:
                off = pad + d if reverse else pad - d
                sh = src[pl.ds(off + r0, ch), :]
            elif reverse:
                blk = src[pl.ds(pad + r0, ch + SUBLANE), :]
                sh = pltpu.roll(blk, ch + SUBLANE - d, axis=0)[:ch, :]
            else:
                blk = src[pl.ds(pad - SUBLANE + r0, ch + SUBLANE), :]
                sh = pltpu.roll(blk, d, axis=0)[SUBLANE:, :]
            sre, sim = sh[:, :half], sh[:, half:]
            dst[pl.ds(pad + r0, ch), :half] = cur[:, :half] + ar * sre - ai * sim
            dst[pl.ds(pad + r0, ch), half:] = cur[:, half:] + ar * sim + ai * sre
            return carry

        lax.fori_loop(0, t // ch, chunk, 0)
    return nlev % 2


def _rscan_levels(abufs, bbufs, t, pad, *, reverse):
    nlev = t.bit_length() - 1
    assert (1 << nlev) == t
    for k in range(nlev):
        d = 1 << k
        asrc, adst = abufs[k % 2], abufs[(k + 1) % 2]
        bsrc, bdst = bbufs[k % 2], bbufs[(k + 1) % 2]
        off = pad + d if reverse else pad - d
        a = asrc[pad:pad + t, :]
        bdst[pad:pad + t, :] = a * bsrc[off:off + t, :] + bsrc[pad:pad + t, :]
        if k < nlev - 1:
            adst[pad:pad + t, :] = a * asrc[off:off + t, :]
    return nlev % 2


S5_TILE = 256


def _s5_scan_forward(u_bf, wb_ref, apow_ref, state, bufs, t, pad):
    half = S5_LANES
    bufs[0][pad:pad + t, :] = _dot(u_bf, wb_ref[...])
    ar, ai = apow_ref[0:1, :half], apow_ref[0:1, half:]
    sr, si = state[:, :half], state[:, half:]
    bufs[0][pad:pad + 1, :half] += ar * sr - ai * si
    bufs[0][pad:pad + 1, half:] += ar * si + ai * sr
    return _cscan_levels(bufs, apow_ref, t, pad, reverse=False)


def _s5_fwd(h_in, wb, apow, wc, dvec, wglu, bglu, *, name, rider=None):
    s = h_in.shape[0]
    t = _seq_tile(s, S5_TILE)
    pad = t // 2
    nt = s // t
    lanes2 = 2 * S5_LANES

    def body(u_ref, wb_ref, apow_ref, wc_ref, d_ref, wglu_ref, bglu_ref, out_ref, y1_ref, hb_ref, buf0, buf1, carry):
        bufs = (buf0, buf1)

        @pl.when(pl.program_id(0) == 0)
        def _():
            buf0[0:pad, :] = jnp.zeros((pad, lanes2), F32)
            buf1[0:pad, :] = jnp.zeros((pad, lanes2), F32)
            carry[...] = jnp.zeros_like(carry)

        u = u_ref[...]
        state = carry[0:1, :]
        hb_ref[0] = state
        fin = _s5_scan_forward(u.astype(BF16), wb_ref, apow_ref, state, bufs, t, pad)
        hbuf = bufs[fin]
        carry[0:1, :] = hbuf[pad + t - 1:pad + t, :]
        y1 = _dot(hbuf[pad:pad + t, :], wc_ref[...]) + d_ref[...] * u
        y1_ref[...] = y1
        y2 = _gelu(y1)
        z = _dot(y2, wglu_ref[...]) + bglu_ref[...]
        out_ref[...] = (y2 * _sigmoid(z)).astype(BF16)

    ins = [h_in, wb, apow, wc, dvec, wglu, bglu]
    in_specs = [pl.BlockSpec((t, D_GROUP), lambda i: (i, COL_S5))] + [_full_spec(a) for a in ins[1:]]
    return _call(
        body, grid=(nt,), ins=ins, in_specs=in_specs,
        out_specs=[pl.BlockSpec((t, D_GROUP), lambda i: (i, MIX_S5)), pl.BlockSpec((t, D_GROUP), lambda i: (i, 0)),
                   pl.BlockSpec((1, 1, lanes2), lambda i: (i, 0, 0))],
        outs=[jax.ShapeDtypeStruct((s, D_MODEL), BF16), jax.ShapeDtypeStruct((s, D_GROUP), F32),
              jax.ShapeDtypeStruct((nt, 1, lanes2), F32)],
        scratch=[pltpu.VMEM((pad + t, lanes2), F32), pltpu.VMEM((pad + t, lanes2), F32),
                 pltpu.VMEM((SUBLANE, lanes2), F32)],
        name=name, rider=rider)


def _s5_bwd(h_in, y1, dmix, hb, wb, apow, wc, dvec, wglu, bglu, dh_all, *, name, rider=None):
    s = h_in.shape[0]
    t = _seq_tile(s, S5_TILE)
    pad = t // 2
    nt = s // t
    half = S5_LANES
    lanes2 = 2 * half
    rows = pad + t + pad

    def body(u_ref, y1_ref, do_ref, hb_ref, wb_ref, apow_ref, wc_ref, d_ref, wglu_ref, bglu_ref, _dh_in,
             du_ref, dwglu_ref, dwc_ref, dwb_ref, dbglu_ref, dd_ref, da_ref, buf0, buf1, buf2, buf3, carry):
        @pl.when(pl.program_id(0) == 0)
        def _():
            for bf in (buf0, buf1, buf2, buf3):
                bf[0:pad, :] = jnp.zeros((pad, lanes2), F32)
                bf[pad + t:rows, :] = jnp.zeros((pad, lanes2), F32)
            carry[...] = jnp.zeros_like(carry)
            for r in (dwglu_ref, dwc_ref, dwb_ref, dbglu_ref, dd_ref, da_ref):
                r[...] = jnp.zeros_like(r)

        u = u_ref[...]
        u_bf = u.astype(BF16)
        state = hb_ref[0]
        hfin = _s5_scan_forward(u_bf, wb_ref, apow_ref, state, (buf0, buf1), t, pad)
        hbuf = (buf0, buf1)[hfin]
        h_bf = hbuf[pad:pad + t, :].astype(BF16)

        y1 = y1_ref[...]
        dout = do_ref[...]
        y2, dgelu = _gelu_and_grad(y1)
        sg = _sigmoid(_dot(y2, wglu_ref[...]) + bglu_ref[...])
        dz = dout * y2 * sg * (1.0 - sg)
        dy2 = dout * sg + _dot_nt(dz, wglu_ref[...])
        dwglu_ref[...] += _dot_tn(y2, dz)
        dbglu_ref[...] += _colsum(dz)
        dy1 = dy2 * dgelu
        dd_ref[...] += _colsum(dy1 * u)
        dy1_bf = dy1.astype(BF16)
        dwc_ref[...] += _dot_tn(h_bf, dy1_bf)

        buf2[pad:pad + t, :] = _dot_nt(dy1_bf, wc_ref[...])
        ar, ai = apow_ref[0:1, :half], apow_ref[0:1, half:]
        cr, ci = carry[0:1, :half], carry[0:1, half:]
        buf2[pad + t - 1:pad + t, :half] += ar * cr + ai * ci
        buf2[pad + t - 1:pad + t, half:] += ar * ci - ai * cr
        lfin = _cscan_levels((buf2, buf3), apow_ref, t, pad, reverse=True)
        lbuf = (buf2, buf3)[lfin]
        lam = lbuf[pad:pad + t, :]
        carry[0:1, :] = lbuf[pad:pad + 1, :]
        lam_bf = lam.astype(BF16)
        du_ref[...] = dy1 * d_ref[...] + _dot_nt(lam_bf, wb_ref[...])
        dwb_ref[...] += _dot_tn(u_bf, lam_bf)

        hbuf[pad - 1:pad, :] = state
        hp = hbuf[pad - 1:pad - 1 + t, :]
        hbuf[pad - 1:pad, :] = jnp.zeros((1, lanes2), F32)
        lre, lim = lam[:, :half], lam[:, half:]
        hre, him = hp[:, :half], hp[:, half:]
        da_ref[:, :half] += _colsum(lre * hre + lim * him)
        da_ref[:, half:] += _colsum(lim * hre - lre * him)

    def rev(col):
        return lambda i: (nt - 1 - i, col)

    ins = [h_in, y1, dmix, hb, wb, apow, wc, dvec, wglu, bglu, dh_all]
    in_specs = [pl.BlockSpec((t, D_GROUP), rev(COL_S5)), pl.BlockSpec((t, D_GROUP), rev(0)),
                pl.BlockSpec((t, D_GROUP), rev(MIX_S5)), pl.BlockSpec((1, 1, lanes2), lambda i: (nt - 1 - i, 0, 0))] + \
               [_full_spec(a) for a in ins[4:10]] + [_ANY]
    outs = [jax.ShapeDtypeStruct((s, N_IN_COLS), F32), jax.ShapeDtypeStruct((D_GROUP, D_GROUP), F32),
            jax.ShapeDtypeStruct((lanes2, D_GROUP), F32), jax.ShapeDtypeStruct((D_GROUP, lanes2), F32),
            jax.ShapeDtypeStruct((1, D_GROUP), F32), jax.ShapeDtypeStruct((1, D_GROUP), F32),
            jax.ShapeDtypeStruct((1, lanes2), F32)]
    out_specs = [pl.BlockSpec((t, D_GROUP), rev(COL_S5))] + [_full_spec(o) for o in outs[1:]]
    return _call(
        body, grid=(nt,), ins=ins, in_specs=in_specs, out_specs=out_specs, outs=outs, aliases={10: 0},
        scratch=[pltpu.VMEM((rows, lanes2), F32) for _ in range(4)] + [pltpu.VMEM((SUBLANE, lanes2), F32)],
        name=name, rider=rider)


def _s5_param_map(lam_re, lam_im, log_dt, b_re, b_im, c_re, c_im):
    dt = jnp.exp(log_dt)[:, None]
    er = jnp.exp(lam_re * dt)
    a_re, a_im = er * jnp.cos(lam_im * dt), er * jnp.sin(lam_im * dt)
    den = lam_re * lam_re + lam_im * lam_im
    n_re = a_re - 1.0
    k_re = (n_re * lam_re + a_im * lam_im) / den
    k_im = (a_im * lam_re - n_re * lam_im) / den
    bb_re = k_re[..., None] * b_re - k_im[..., None] * b_im
    bb_im = k_re[..., None] * b_im + k_im[..., None] * b_re
    mask_in = _block_mask(S5_GROUPS, S5_CH, S5_STATE)
    mask_out = _block_mask(S5_GROUPS, S5_STATE, S5_CH)

    def blockdiag_in(m):
        return jnp.tile(jnp.transpose(m, (0, 2, 1)).reshape(S5_GROUPS * S5_CH, S5_STATE), (1, S5_GROUPS)) * mask_in

    def blockdiag_out(m):
        return jnp.tile(jnp.transpose(m, (0, 2, 1)).reshape(S5_LANES, S5_CH), (1, S5_GROUPS)) * mask_out

    a = jnp.concatenate([a_re.reshape(1, -1), a_im.reshape(1, -1)], axis=1)
    wb = jnp.concatenate([blockdiag_in(bb_re), blockdiag_in(bb_im)], axis=1)
    wc = jnp.concatenate([blockdiag_out(c_re), -blockdiag_out(c_im)], axis=0)
    return a, wb, wc


def _s5_apow(a, nlev):
    half = S5_LANES
    re, im = a[:, :half], a[:, half:]
    rows = []
    for _ in range(nlev):
        rows.append(jnp.concatenate([re, im], axis=1))
        re, im = re * re - im * im, 2.0 * re * im
    n_rows = -(-nlev // SUBLANE) * SUBLANE
    rows += [jnp.zeros_like(rows[0])] * (n_rows - nlev)
    return lax.stop_gradient(jnp.concatenate(rows, axis=0))


CV_TILE = 256
CV_PAD = 32
CV_CHUNK = 64


def _gn_stats(c, mavg):
    mu = _dot_hi(c, mavg)
    cen = c - mu
    var = _dot_hi(cen * cen, mavg)
    rstd = lax.rsqrt(var + LN_EPS)
    return cen * rstd, rstd


def _cv_fwd(h_in, cw, cb, gng, gnb, mavg, wpw, bpw, mix, *, name, rider=None):
    s = h_in.shape[0]
    t = _seq_tile(s, CV_TILE)
    ch = min(CV_CHUNK, t)

    def body(v_ref, g_ref, cw_ref, cb_ref, gng_ref, gnb_ref, mavg_ref, wpw_ref, bpw_ref, _mix_in, out_ref, c_ref, xpad):
        @pl.when(pl.program_id(0) == 0)
        def _():
            xpad[0:CV_PAD, :] = jnp.zeros((CV_PAD, D_GROUP), F32)

        xpad[CV_PAD:CV_PAD + t, :] = v_ref[...] * _sigmoid(g_ref[...])
        for r0 in range(0, t, ch):
            acc = jnp.broadcast_to(cb_ref[...], (ch, D_GROUP))
            for k in range(CONV_WIDTH):
                o = CV_PAD - (CONV_WIDTH - 1) + k + r0
                acc = acc + cw_ref[k:k + 1, :] * xpad[o:o + ch, :]
            c_ref[r0:r0 + ch, :] = acc
        xpad[0:CV_PAD, :] = xpad[t:t + CV_PAD, :]
        xn, _ = _gn_stats(c_ref[...], mavg_ref[...])
        gn = xn * gng_ref[...] + gnb_ref[...]
        out_ref[...] = (_dot(gn * _sigmoid(gn), wpw_ref[...]) + bpw_ref[...]).astype(BF16)

    ins = [h_in, h_in, cw, cb, gng, gnb, mavg, wpw, bpw, mix]
    in_specs = [pl.BlockSpec((t, D_GROUP), lambda i: (i, COL_CV_V)), pl.BlockSpec((t, D_GROUP), lambda i: (i, COL_CV_G))] + \
               [_full_spec(a) for a in ins[2:9]] + [_ANY]
    return _call(
        body, grid=(s // t,), ins=ins, in_specs=in_specs,
        out_specs=[pl.BlockSpec((t, D_GROUP), lambda i: (i, MIX_CV)), pl.BlockSpec((t, D_GROUP), lambda i: (i, 0))],
        outs=[jax.ShapeDtypeStruct((s, D_MODEL), BF16), jax.ShapeDtypeStruct((s, D_GROUP), F32)],
        aliases={9: 0}, scratch=[pltpu.VMEM((CV_PAD + t, D_GROUP), F32)], name=name, rider=rider)


def _cv_bwd(h_in, c, dmix, cw, gng, gnb, mavg, wpw, *, name, rider=None):
    s = h_in.shape[0]
    t = _seq_tile(s, CV_TILE)
    nt = s // t
    ch = min(CV_CHUNK, t)

    def body(v_ref, g_ref, c_ref, do_ref, cw_ref, gng_ref, gnb_ref, mavg_ref, wpw_ref,
             dvg_ref, dwpw_ref, dcw_ref, dbpw_ref, dgg_ref, dgb_ref, dcb_ref, dcpad, hgbuf):
        @pl.when(pl.program_id(0) == 0)
        def _():
            dcpad[t:t + CV_PAD, :] = jnp.zeros((CV_PAD, D_GROUP), F32)
            for r in (dwpw_ref, dcw_ref, dbpw_ref, dgg_ref, dgb_ref, dcb_ref):
                r[...] = jnp.zeros_like(r)

        mavg = mavg_ref[...]
        xn, rstd = _gn_stats(c_ref[...], mavg)
        gg = gng_ref[...]
        gn = xn * gg + gnb_ref[...]
        sg = _sigmoid(gn)
        dout = do_ref[...]
        dwpw_ref[...] += _dot_tn(gn * sg, dout)
        dbpw_ref[...] += _colsum(dout)
        dgn = _dot_nt(dout, wpw_ref[...]) * (sg * (1.0 + gn * (1.0 - sg)))
        dgg_ref[...] += _colsum(dgn * xn)
        dgb_ref[...] += _colsum(dgn)
        dxn = dgn * gg
        dc = rstd * (dxn - _dot_hi(dxn, mavg) - xn * _dot_hi(dxn * xn, mavg))
        dcb_ref[...] += _colsum(dc)
        dcpad[0:t, :] = dc

        v = v_ref[...]
        sgm = _sigmoid(g_ref[...])
        hgbuf[...] = v * sgm
        for r0 in range(0, t, ch):
            hg = hgbuf[r0:r0 + ch, :]
            acc = jnp.zeros((ch, D_GROUP), F32)
            for k in range(CONV_WIDTH):
                o = (CONV_WIDTH - 1) - k + r0
                sh = dcpad[o:o + ch, :]
                acc = acc + cw_ref[k:k + 1, :] * sh
                dcw_ref[k:k + 1, :] += _colsum(hg * sh)
            hgbuf[r0:r0 + ch, :] = acc
        dcpad[t:t + CV_PAD, :] = dcpad[0:CV_PAD, :]
        dhg = hgbuf[...]
        dvg_ref[:, :D_GROUP] = dhg * sgm
        dvg_ref[:, D_GROUP:] = dhg * v * sgm * (1.0 - sgm)

    def rev(col):
        return lambda i: (nt - 1 - i, col)

    ins = [h_in, h_in, c, dmix, cw, gng, gnb, mavg, wpw]
    in_specs = [pl.BlockSpec((t, D_GROUP), rev(COL_CV_V)), pl.BlockSpec((t, D_GROUP), rev(COL_CV_G)),
                pl.BlockSpec((t, D_GROUP), rev(0)), pl.BlockSpec((t, D_GROUP), rev(MIX_CV))] + [_full_spec(a) for a in ins[4:]]
    vec = jax.ShapeDtypeStruct((1, D_GROUP), F32)
    outs = [jax.ShapeDtypeStruct((s, N_IN_COLS), F32),
            jax.ShapeDtypeStruct((D_GROUP, D_GROUP), F32), jax.ShapeDtypeStruct((CV_PAD, D_GROUP), F32), vec, vec, vec, vec]
    out_specs = [pl.BlockSpec((t, 2 * D_GROUP), rev(COL_CV_V // 2))] + [_full_spec(o) for o in outs[1:]]
    return _call(
        body, grid=(nt,), ins=ins, in_specs=in_specs, out_specs=out_specs, outs=outs,
        scratch=[pltpu.VMEM((t + CV_PAD, D_GROUP), F32), pltpu.VMEM((t, D_GROUP), F32)], name=name, rider=rider)


LRU_TILE = 256


def _lru_gates(xc, wr_ref, br_ref, wi_ref, bi_ref, sp_ref):
    r = _sigmoid(_dot(xc, wr_ref[...]) + br_ref[...])
    i = _sigmoid(_dot(xc, wi_ref[...]) + bi_ref[...])
    log_a = -LRU_C * r * sp_ref[...]
    a = jnp.exp(log_a)
    m = jnp.sqrt(_neg_expm1(2.0 * log_a))
    return r, i, a, m


def _lru_fwd(h_in, lcw, lcb, wr, br, wi, bi, sp, mix, *, name):
    s = h_in.shape[0]
    t = _seq_tile(s, LRU_TILE)
    pad = max(t // 2, SUBLANE)

    def body(xg_ref, xr_ref, lcw_ref, lcb_ref, wr_ref, br_ref, wi_ref, bi_ref, sp_ref, _mix_in,
             out_ref, xc_ref, h_ref, xpad, a0, a1, b0, b1, carry):
        @pl.when(pl.program_id(0) == 0)
        def _():
            xpad[0:SUBLANE, :] = jnp.zeros((SUBLANE, D_GROUP), F32)
            for bf in (a0, a1, b0, b1):
                bf[0:pad, :] = jnp.zeros((pad, D_GROUP), F32)
            carry[...] = jnp.zeros_like(carry)

        xpad[SUBLANE:SUBLANE + t, :] = xr_ref[...]
        xc = jnp.broadcast_to(lcb_ref[...], (t, D_GROUP))
        for k in range(LRU_CONV_WIDTH):
            o = SUBLANE - (LRU_CONV_WIDTH - 1) + k
            xc = xc + lcw_ref[k:k + 1, :] * xpad[o:o + t, :]
        xpad[0:SUBLANE, :] = xpad[t:t + SUBLANE, :]
        xc_ref[...] = xc
        _, i, a, m = _lru_gates(xc, wr_ref, br_ref, wi_ref, bi_ref, sp_ref)
        a0[pad:pad + t, :] = a
        b0[pad:pad + t, :] = m * (i * xc)
        b0[pad:pad + 1, :] += a0[pad:pad + 1, :] * carry[0:1, :]
        fin = _rscan_levels((a0, a1), (b0, b1), t, pad, reverse=False)
        hbuf = (b0, b1)[fin]
        carry[0:1, :] = hbuf[pad + t - 1:pad + t, :]
        h = hbuf[pad:pad + t, :]
        h_ref[...] = h
        out_ref[...] = (h * _gelu(xg_ref[...])).astype(BF16)

    ins = [h_in, h_in, lcw, lcb, wr, br, wi, bi, sp, mix]
    row = pl.BlockSpec((t, D_GROUP), lambda i: (i, 0))
    in_specs = [pl.BlockSpec((t, D_GROUP), lambda i: (i, COL_LRU_G)), pl.BlockSpec((t, D_GROUP), lambda i: (i, COL_LRU_X))] + \
               [_full_spec(a) for a in ins[2:9]] + [_ANY]
    return pl.pallas_call(
        body, grid=(s // t,), in_specs=in_specs,
        out_specs=[pl.BlockSpec((t, D_GROUP), lambda i: (i, MIX_LRU)), row, row],
        out_shape=[jax.ShapeDtypeStruct((s, D_MODEL), BF16)] + [jax.ShapeDtypeStruct((s, D_GROUP), F32)] * 2,
        input_output_aliases={9: 0},
        scratch_shapes=[pltpu.VMEM((SUBLANE + t, D_GROUP), F32)] + [pltpu.VMEM((pad + t, D_GROUP), F32)] * 4 +
                       [pltpu.VMEM((SUBLANE, D_GROUP), F32)],
        compiler_params=_cparams(1), name=name)(*ins)


def _lru_bwd(h_in, xc_all, h_all, dmix, lcw, wr, br, wi, bi, sp, dh_all, *, name):
    s = h_in.shape[0]
    t = _seq_tile(s, LRU_TILE)
    nt = s // t
    pad = max(t // 2, SUBLANE)
    tb = t // SUBLANE

    def body(xg_ref, xr_ref, xc_ref, h_ref, hprev_ref, do_ref, lcw_ref, wr_ref, br_ref, wi_ref, bi_ref, sp_ref, _dh_in,
             dgr_ref, dwr_ref, dwi_ref, dlcw_ref, dbr_ref, dbi_ref, dsp_ref, dlcb_ref,
             a0, a1, b0, b1, hp, dxpad, carry):
        pid = pl.program_id(0)

        @pl.when(pid == 0)
        def _():
            for bf in (a0, a1, b0, b1):
                bf[pad + t:pad + t + pad, :] = jnp.zeros((pad, D_GROUP), F32)
            dxpad[t:t + SUBLANE, :] = jnp.zeros((SUBLANE, D_GROUP), F32)
            carry[...] = jnp.zeros_like(carry)
            for r in (dwr_ref, dwi_ref, dlcw_ref, dbr_ref, dbi_ref, dsp_ref, dlcb_ref):
                r[...] = jnp.zeros_like(r)

        xc = xc_ref[...]
        h = h_ref[...]
        dout = do_ref[...]
        gate, dgate = _gelu_and_grad(xg_ref[...])
        dgr_ref[:, :D_GROUP] = dout * h * dgate
        r, i, a, m = _lru_gates(xc, wr_ref, br_ref, wi_ref, bi_ref, sp_ref)

        a0[pad:pad + t, :] = a
        b0[pad:pad + t, :] = dout * gate
        b0[pad + t - 1:pad + t, :] += carry[0:1, :]
        a1[pad:pad + t, :] = a0[pad + 1:pad + 1 + t, :]
        fin = _rscan_levels((a1, a0), (b0, b1), t, pad, reverse=True)
        lam = (b0, b1)[fin][pad:pad + t, :]
        carry[0:1, :] = a[0:1, :] * lam[0:1, :]

        is_first = pid == nt - 1
        hp[0:SUBLANE, :] = jnp.where(is_first, 0.0, hprev_ref[...])
        hp[SUBLANE:SUBLANE + t, :] = h
        hprev = hp[SUBLANE - 1:SUBLANE - 1 + t, :]

        ix = i * xc
        dmm = lam * ix
        dix = lam * m
        da = lam * hprev - dmm * (a / m)
        dlog_a = da * a
        dr = dlog_a * (-LRU_C * sp_ref[...])
        dsp_ref[...] += _colsum(dlog_a * (-LRU_C * r))
        dpr = dr * r * (1.0 - r)
        dpi = dix * xc * i * (1.0 - i)
        dbr_ref[...] += _colsum(dpr)
        dbi_ref[...] += _colsum(dpi)
        dwr_ref[...] += _dot_tn(xc, dpr)
        dwi_ref[...] += _dot_tn(xc, dpi)
        dxc = dix * i + _dot_nt(dpr, wr_ref[...]) + _dot_nt(dpi, wi_ref[...])
        dlcb_ref[...] += _colsum(dxc)

        dxpad[0:t, :] = dxc
        xr = xr_ref[...]
        dxr = jnp.zeros((t, D_GROUP), F32)
        for k in range(LRU_CONV_WIDTH):
            o = (LRU_CONV_WIDTH - 1) - k
            sh = dxpad[o:o + t, :]
            dxr = dxr + lcw_ref[k:k + 1, :] * sh
            dlcw_ref[k:k + 1, :] += _colsum(xr * sh)
        dxpad[t:t + SUBLANE, :] = dxpad[0:SUBLANE, :]
        dgr_ref[:, D_GROUP:] = dxr

    def rev(col):
        return lambda i: (nt - 1 - i, col)

    ins = [h_in, h_in, xc_all, h_all, h_all, dmix, lcw, wr, br, wi, bi, sp, dh_all]
    in_specs = [pl.BlockSpec((t, D_GROUP), rev(COL_LRU_G)), pl.BlockSpec((t, D_GROUP), rev(COL_LRU_X)),
                pl.BlockSpec((t, D_GROUP), rev(0)), pl.BlockSpec((t, D_GROUP), rev(0)),
                pl.BlockSpec((SUBLANE, D_GROUP), lambda i: (jnp.maximum((nt - 1 - i) * tb - 1, 0), 0)),
                pl.BlockSpec((t, D_GROUP), rev(MIX_LRU))] + [_full_spec(a) for a in ins[6:12]] + [_ANY]
    vec = jax.ShapeDtypeStruct((1, D_GROUP), F32)
    mat = jax.ShapeDtypeStruct((D_GROUP, D_GROUP), F32)
    outs = [jax.ShapeDtypeStruct((s, N_IN_COLS), F32), mat, mat, jax.ShapeDtypeStruct((SUBLANE, D_GROUP), F32),
            vec, vec, vec, vec]
    out_specs = [pl.BlockSpec((t, 2 * D_GROUP), rev(COL_LRU_G // 2))] + [_full_spec(o) for o in outs[1:]]
    return pl.pallas_call(
        body, grid=(nt,), in_specs=in_specs, out_specs=out_specs, out_shape=outs, input_output_aliases={12: 0},
        scratch_shapes=[pltpu.VMEM((pad + t + pad, D_GROUP), F32)] * 4 +
                       [pltpu.VMEM((SUBLANE + t, D_GROUP), F32), pltpu.VMEM((t + SUBLANE, D_GROUP), F32),
                        pltpu.VMEM((SUBLANE, D_GROUP), F32)],
        compiler_params=_cparams(1), name=name)(*ins)


def _blockdiag(w):
    h, d, _ = w.shape
    return jnp.tile(w.reshape(h * d, d), (1, h)) * _block_mask(h, d, d)


ATTN_TILE = 512
ATTN_SCALE = ATTN_HEAD_DIM ** -0.5


def _attn_big(kv):
    m = kv.shape[0]
    kbig = jnp.tile(kv[:, :D_GROUP].T, (1, ATTN_HEADS)) * _block_mask(ATTN_HEADS, ATTN_HEAD_DIM, m)
    vbig = jnp.tile(kv[:, D_GROUP:], (ATTN_HEADS, 1)) * _block_mask(ATTN_HEADS, m, ATTN_HEAD_DIM)
    return kbig, vbig


def _attn_probs(q, kbig_ref, m):
    sc = _dot(q, kbig_ref[...]) * ATTN_SCALE
    ps = []
    for h in range(ATTN_HEADS):
        sh = sc[:, h * m:(h + 1) * m]
        e = jnp.exp(sh - jnp.max(sh, axis=1, keepdims=True))
        ps.append(e / jnp.sum(e, axis=1, keepdims=True))
    return ps


def _attn_fwd(h_in, kbig, vbig, mix, *, name):
    s = h_in.shape[0]
    t = _seq_tile(s, ATTN_TILE)
    m = kbig.shape[1] // ATTN_HEADS

    def body(q_ref, kbig_ref, vbig_ref, _mix_in, o_ref):
        ps = _attn_probs(q_ref[...], kbig_ref, m)
        o_ref[...] = _dot(jnp.concatenate(ps, axis=1), vbig_ref[...]).astype(BF16)

    return pl.pallas_call(
        body, grid=(s // t,),
        in_specs=[pl.BlockSpec((t, D_GROUP), lambda i: (i, COL_Q)), _full_spec(kbig), _full_spec(vbig), _ANY],
        out_specs=pl.BlockSpec((t, D_GROUP), lambda i: (i, MIX_ATTN)),
        out_shape=jax.ShapeDtypeStruct((s, D_MODEL), BF16), input_output_aliases={3: 0},
        compiler_params=_cparams(1), name=name)(h_in, kbig, vbig, mix)


def _attn_bwd(h_in, dmix, kbig, vbig, dh_all, *, name):
    s = h_in.shape[0]
    t = _seq_tile(s, ATTN_TILE)
    m = kbig.shape[1] // ATTN_HEADS

    def body(q_ref, do_ref, kbig_ref, vbig_ref, _dh_in, dq_ref, dk_ref, dv_ref):
        @pl.when(pl.program_id(0) == 0)
        def _():
            dk_ref[...] = jnp.zeros_like(dk_ref)
            dv_ref[...] = jnp.zeros_like(dv_ref)

        q = q_ref[...]
        dout = do_ref[...]
        ps = _attn_probs(q, kbig_ref, m)
        dp = _dot_nt(dout, vbig_ref[...])
        dss = []
        for h in range(ATTN_HEADS):
            dph = dp[:, h * m:(h + 1) * m]
            dss.append(ps[h] * (dph - jnp.sum(dph * ps[h], axis=1, keepdims=True)))
        ds = (jnp.concatenate(dss, axis=1) * ATTN_SCALE).astype(BF16)
        dv_ref[...] += _dot_tn(jnp.concatenate(ps, axis=1), dout)
        dq_ref[...] = _dot_nt(ds, kbig_ref[...])
        dk_ref[...] += _dot_tn(q, ds)

    outs = [jax.ShapeDtypeStruct((s, N_IN_COLS), F32), jax.ShapeDtypeStruct(kbig.shape, F32),
            jax.ShapeDtypeStruct(vbig.shape, F32)]
    return pl.pallas_call(
        body, grid=(s // t,),
        in_specs=[pl.BlockSpec((t, D_GROUP), lambda i: (i, COL_Q)), pl.BlockSpec((t, D_GROUP), lambda i: (i, MIX_ATTN)),
                  _full_spec(kbig), _full_spec(vbig), _ANY],
        out_specs=[pl.BlockSpec((t, D_GROUP), lambda i: (i, COL_Q)), _full_spec(outs[1]), _full_spec(outs[2])],
        out_shape=outs, input_output_aliases={4: 0},
        compiler_params=_cparams(1), name=name)(h_in, dmix, kbig, vbig, dh_all)


FFN_TILE = 128
FFN_COL_CHUNK = 256
FFN_ROW_CHUNK = 64


def _ffn_conv(pad_ref, w_ref, b_ref, r0, ch, c0):
    cc = FFN_COL_CHUNK
    acc = jnp.broadcast_to(b_ref[:, c0:c0 + cc], (ch, cc))
    for k in range(FFN_CONV_WIDTH):
        o = SUBLANE - (FFN_CONV_WIDTH - 1) + k + r0
        acc = acc + w_ref[k:k + 1, c0:c0 + cc] * pad_ref[o:o + ch, c0:c0 + cc]
    return acc


def _ffn_gate_fwd(u, fcw, fcb, *, name, rider=None):
    s = u.shape[0]
    t = _seq_tile(s, FFN_TILE)
    ch = min(FFN_ROW_CHUNK, t)
    cc = FFN_COL_CHUNK

    def body(u_ref, w_ref, b_ref, o_ref, upad):
        @pl.when(pl.program_id(0) == 0)
        def _():
            upad[0:SUBLANE, :] = jnp.zeros((SUBLANE, 2 * D_FF), F32)

        upad[SUBLANE:SUBLANE + t, :] = u_ref[...].astype(F32)
        for c0 in range(0, D_FF, cc):
            for r0 in range(0, t, ch):
                val = _ffn_conv(upad, w_ref, b_ref, r0, ch, c0)
                gt = _ffn_conv(upad, w_ref, b_ref, r0, ch, c0 + D_FF)
                o_ref[r0:r0 + ch, c0:c0 + cc] = (val * _gelu(gt)).astype(BF16)
        upad[0:SUBLANE, :] = upad[t:t + SUBLANE, :]

    (out,), routs = _call(
        body, grid=(s // t,), ins=[u, fcw, fcb],
        in_specs=[pl.BlockSpec((t, 2 * D_FF), lambda i: (i, 0)), _full_spec(fcw), _full_spec(fcb)],
        out_specs=[pl.BlockSpec((t, D_FF), lambda i: (i, 0))], outs=[jax.ShapeDtypeStruct((s, D_FF), BF16)],
        scratch=[pltpu.VMEM((SUBLANE + t, 2 * D_FF), F32)], name=name, rider=rider)
    return out if rider is None else (out, routs)


def _ffn_gate_bwd(u, dh, fcw, fcb, *, name, rider=None):
    s = u.shape[0]
    t = _seq_tile(s, FFN_TILE)
    nt = s // t
    ch = min(FFN_ROW_CHUNK, t)
    cc = FFN_COL_CHUNK
    halo = 2 * SUBLANE
    tb = t // halo

    def body(u_ref, halo_ref, dh_ref, w_ref, b_ref, du_ref, dw_ref, db_ref, upad, dpad):
        pid = pl.program_id(0)

        @pl.when(pid == 0)
        def _():
            dpad[t:t + SUBLANE, :] = jnp.zeros((SUBLANE, 2 * D_FF), F32)
            dw_ref[...] = jnp.zeros_like(dw_ref)
            db_ref[...] = jnp.zeros_like(db_ref)

        upad[0:SUBLANE, :] = jnp.where(pid == nt - 1, 0.0, halo_ref[...].astype(F32)[SUBLANE:, :])
        upad[SUBLANE:SUBLANE + t, :] = u_ref[...].astype(F32)
        for c0 in range(0, D_FF, cc):
            for r0 in range(0, t, ch):
                val = _ffn_conv(upad, w_ref, b_ref, r0, ch, c0)
                gt = _ffn_conv(upad, w_ref, b_ref, r0, ch, c0 + D_FF)
                gl, dgl = _gelu_and_grad(gt)
                d = dh_ref[r0:r0 + ch, c0:c0 + cc].astype(F32)
                dpad[r0:r0 + ch, c0:c0 + cc] = d * gl
                dpad[r0:r0 + ch, c0 + D_FF:c0 + D_FF + cc] = d * val * dgl
        for c0 in range(0, 2 * D_FF, cc):
            dbs = jnp.zeros((1, cc), F32)
            dws = [jnp.zeros((1, cc), F32) for _ in range(FFN_CONV_WIDTH)]
            for r0 in range(0, t, ch):
                x = upad[SUBLANE + r0:SUBLANE + r0 + ch, c0:c0 + cc]
                acc = jnp.zeros((ch, cc), F32)
                for k in range(FFN_CONV_WIDTH):
                    o = (FFN_CONV_WIDTH - 1) - k + r0
                    sh = dpad[o:o + ch, c0:c0 + cc]
                    acc = acc + w_ref[k:k + 1, c0:c0 + cc] * sh
                    dws[k] = dws[k] + _colsum(x * sh)
                    if k == FFN_CONV_WIDTH - 1:
                        dbs = dbs + _colsum(sh)
                du_ref[r0:r0 + ch, c0:c0 + cc] = acc.astype(BF16)
            db_ref[:, c0:c0 + cc] += dbs
            for k in range(FFN_CONV_WIDTH):
                dw_ref[k:k + 1, c0:c0 + cc] += dws[k]
        dpad[t:t + SUBLANE, :] = dpad[0:SUBLANE, :]

    outs = [jax.ShapeDtypeStruct((s, 2 * D_FF), BF16), jax.ShapeDtypeStruct((SUBLANE, 2 * D_FF), F32),
            jax.ShapeDtypeStruct((1, 2 * D_FF), F32)]
    return _call(
        body, grid=(nt,), ins=[u, u, dh, fcw, fcb],
        in_specs=[pl.BlockSpec((t, 2 * D_FF), lambda i: (nt - 1 - i, 0)),
                  pl.BlockSpec((halo, 2 * D_FF), lambda i: (jnp.maximum((nt - 1 - i) * tb - 1, 0), 0)),
                  pl.BlockSpec((t, D_FF), lambda i: (nt - 1 - i, 0)), _full_spec(fcw), _full_spec(fcb)],
        out_specs=[pl.BlockSpec((t, 2 * D_FF), lambda i: (nt - 1 - i, 0)), _full_spec(outs[1]), _full_spec(outs[2])],
        outs=outs,
        scratch=[pltpu.VMEM((SUBLANE + t, 2 * D_FF), F32), pltpu.VMEM((t + SUBLANE, 2 * D_FF), F32)],
        name=name, rider=rider)


def _adamw_body(g_ref, w_ref, m_ref, v_ref, go_ref, d_ref, mo_ref, vo_ref):
    inv_b1 = 1.0 - ADAM_B1 ** ADAM_STEP
    inv_b2 = 1.0 - ADAM_B2 ** ADAM_STEP
    g = g_ref[0]
    for dev in range(1, N_DEV):
        g = g + g_ref[dev]
    go_ref[...] = g
    mn = ADAM_B1 * m_ref[...] + (1.0 - ADAM_B1) * g
    vn = ADAM_B2 * v_ref[...] + (1.0 - ADAM_B2) * (g * g)
    mo_ref[...] = mn
    vo_ref[...] = vn
    d_ref[...] = -ADAM_LR * ((mn / inv_b1) / (jnp.sqrt(vn / inv_b2) + ADAM_EPS) + ADAM_WD * w_ref[...])


def _adamw(gstack, w, m, v, *, name):
    _, r, c = gstack.shape
    tr = _pick_rows(r, PACK_ROW_BLOCK)

    def body(*refs):
        _adamw_body(*refs)

    blk = pl.BlockSpec((tr, c), lambda i: (i, 0))
    sh = jax.ShapeDtypeStruct((r, c), F32)
    return pl.pallas_call(
        body, grid=(r // tr,),
        in_specs=[pl.BlockSpec((N_DEV, tr, c), lambda i: (0, i, 0)), blk, blk, blk],
        out_specs=[blk] * 4, out_shape=[sh] * 4,
        compiler_params=_cparams(1), name=name)(gstack, w, m, v)


def _adamw_layer(gstack, w, m, v, layer, into, *, name):
    n_layers, r, c = w.shape
    tr = _pick_rows(r, PACK_ROW_BLOCK)

    def body(g_ref, w_ref, m_ref, v_ref, *rest):
        _adamw_body(g_ref, w_ref, m_ref, v_ref, *rest[-4:])

    blk = pl.BlockSpec((None, tr, c), lambda i: (layer, i, 0))
    sh = jax.ShapeDtypeStruct((n_layers, r, c), F32)
    into = list(into or [])
    return pl.pallas_call(
        body, grid=(r // tr,),
        in_specs=[pl.BlockSpec((N_DEV, tr, c), lambda i: (0, i, 0)), blk, blk, blk] + [_ANY] * len(into),
        out_specs=[blk] * 4, out_shape=[sh] * 4, input_output_aliases={4 + k: k for k in range(len(into))},
        compiler_params=_cparams(1), name=name)(gstack, w, m, v, *into)


def _allgather(xs, slot_axes, *, name):
    n = len(xs)
    out_shapes = []
    for x, ax in zip(xs, slot_axes):
        shp = (N_DEV,) + x.shape if ax == 0 else (x.shape[0], N_DEV) + x.shape[1:]
        out_shapes.append(jax.ShapeDtypeStruct(shp, x.dtype))

    def body(*refs):
        x_refs, out_refs = refs[:n], refs[n:2 * n]
        send_sems, recv_sems, local_sems = refs[2 * n:]
        mx, my, mc = lax.axis_index("x"), lax.axis_index("y"), lax.axis_index("c")
        me, sibling = (mx, my, mc), (mx, my, 1 - mc)
        chips = [(1 - mx, my), (mx, 1 - my), (1 - mx, 1 - my)]

        def slot(i, px, py, pc):
            idx = 4 * px + 2 * py + pc
            if slot_axes[i] == 0:
                return out_refs[i].at[idx]
            return out_refs[i].at[pl.ds(0, out_refs[i].shape[0]), idx]

        def copy(i, k, block, to, src=None):
            return pltpu.make_async_remote_copy(
                src_ref=slot(i, *block) if src is None else src, dst_ref=slot(i, *block),
                send_sem=send_sems.at[i * N_PEERS + k], recv_sem=recv_sems.at[i * N_PEERS + k],
                device_id=to, device_id_type=_MESH)

        mine = [pltpu.make_async_copy(x_refs[i], slot(i, *me), local_sems.at[i]) for i in range(n)]
        for cp in mine:
            cp.start()
        first = []
        for i in range(n):
            first.append(copy(i, 0, me, sibling, src=x_refs[i]))
            first += [copy(i, 1 + j, me, (*chip, mc), src=x_refs[i]) for j, chip in enumerate(chips)]
        for cp in first:
            cp.start()
        passed = []
        for j, chip in enumerate(chips):
            for i in range(n):
                copy(i, 1 + j, (*chip, mc), me).wait_recv()
                fwd = copy(i, 4 + j, (*chip, mc), sibling)
                fwd.start()
                passed.append(fwd)
        for i in range(n):
            copy(i, 0, sibling, me).wait_recv()
            for j, chip in enumerate(chips):
                copy(i, 4 + j, (*chip, 1 - mc), me).wait_recv()
        for cp in first + passed:
            cp.wait_send()
        for cp in mine:
            cp.wait()

    return pl.pallas_call(
        body, in_specs=[_ANY] * n, out_specs=[_ANY] * n, out_shape=out_shapes,
        scratch_shapes=[pltpu.SemaphoreType.DMA((n * N_PEERS,)), pltpu.SemaphoreType.DMA((n * N_PEERS,)),
                        pltpu.SemaphoreType.DMA((n,))],
        name=name)(*xs)


def _exchange(rider, *, name):
    n = rider.n

    def body(*refs):
        x_refs, out_refs, sems = refs[:n], refs[n:2 * n], refs[2 * n:]
        rider.start(x_refs, out_refs, sems)
        rider.wait(x_refs, out_refs, sems)

    return pl.pallas_call(
        body, in_specs=[_ANY] * n, out_specs=[_ANY] * n, out_shape=rider.out_shapes(),
        scratch_shapes=rider.scratch(), name=name)(*rider.srcs)


def _pack_rows(n):
    rows = -(-n // PACK_COLS)
    return -(-rows // PACK_ROW_BLOCK) * PACK_ROW_BLOCK


def _pack(arrs, dtype):
    flat = jnp.concatenate([a.reshape(-1).astype(dtype) for a in arrs])
    rows = _pack_rows(flat.shape[0])
    flat = jnp.pad(flat, (0, rows * PACK_COLS - flat.shape[0]))
    return flat.reshape(rows, PACK_COLS)


def _pack_lead(arrs, dtype):
    flat = jnp.concatenate([a.reshape(N_DEV, -1).astype(dtype) for a in arrs], axis=1)
    rows = _pack_rows(flat.shape[1])
    flat = jnp.pad(flat, ((0, 0), (0, rows * PACK_COLS - flat.shape[1])))
    return flat.reshape(N_DEV, rows, PACK_COLS)


def _unpack(packed, shapes, lead=False):
    flat = packed.reshape(N_DEV, -1) if lead else packed.reshape(-1)
    out, pos = [], 0
    for sh in shapes:
        n = math.prod(sh)
        out.append(flat[:, pos:pos + n].reshape((N_DEV,) + tuple(sh)) if lead else flat[pos:pos + n].reshape(sh))
        pos += n
    return out


def _join_shards(stacked, axis):
    return jnp.concatenate([stacked[d] for d in range(N_DEV)], axis=axis)


def _split_shards(full, axis):
    return jnp.stack(jnp.split(full, N_DEV, axis=axis), axis=0)


def _join_cols(stacked):
    _, l, k, c = stacked.shape
    return jnp.transpose(stacked, (1, 2, 0, 3)).reshape(l, k, N_DEV * c)


def _split_cols(full):
    l, k, n = full.shape
    return jnp.transpose(full.reshape(l, k, N_DEV, n // N_DEV), (2, 0, 1, 3))


def _perm_in_cols(a, inverse=False):
    blocks = jnp.split(a, 6, axis=-1)
    if inverse:
        order = [IN_PERM.index(j) for j in range(6)]
    else:
        order = list(IN_PERM)
    return jnp.concatenate([blocks[j] for j in order], axis=-1)


def _row(v):
    return v.reshape(1, -1)


def _pad_rows(w, rows):
    return jnp.pad(w, ((0, rows - w.shape[0]), (0, 0)))


def _gn_avg_matrix():
    return _block_mask(GN_GROUPS, D_GROUP // GN_GROUPS, D_GROUP // GN_GROUPS) / (D_GROUP // GN_GROUPS)


def _layer_params(p, l):
    q = {}
    (a, wb, wc), q["s5_vjp"] = jax.vjp(_s5_param_map, p["s5_lam_re"][l], p["s5_lam_im"][l], p["s5_log_dt"][l],
                                       p["s5_b_re"][l], p["s5_b_im"][l], p["s5_c_re"][l], p["s5_c_im"][l])
    q["wb"], q["wc"] = wb.astype(BF16), wc.astype(BF16)
    q["apow"] = _s5_apow(a, max(S5_TILE.bit_length() - 1, 1))
    (q["wr"], q["wi"]), q["lru_w_vjp"] = jax.vjp(lambda r, i: (_blockdiag(r), _blockdiag(i)), p["lru_w_r"][l], p["lru_w_i"][l])
    q["wr"], q["wi"] = q["wr"].astype(BF16), q["wi"].astype(BF16)
    q["sp"], q["sp_vjp"] = jax.vjp(lambda lam: _row(jax.nn.softplus(-lam)), p["lru_lam"][l])
    return q


WEIGHT_RIDES = {(0, "inproj"): [("attn_w_kv", 0), ("w_out", 0)], (0, "s5_fwd"): [("ffn_w_up", 0)],
                (0, "cv_fwd"): [("ffn_w_down", 0)],
                (0, "ffn_up"): [("w_in", 1), ("attn_w_kv", 1), ("w_out", 1), ("ffn_w_down", 1)],
                (0, "ffn_gate_fwd"): [("ffn_w_up", 1)]}
GRAD_RIDES = {(1, "ffn_gate_bwd"): [("ffn_w_down", 1)],
              (0, "dw_down"): [("w_out", 1), ("attn_w_kv", 1), ("w_in", 1)],
              (0, "ffn_gate_bwd"): [("ffn_w_up", 1)],
              (0, "dx1"): [("ffn_w_down", 0)],
              (0, "cv_bwd"): [("w_out", 0)],
              (0, "s5_bwd"): [("ffn_w_up", 0)],
              (0, "dxs"): [("attn_w_kv", 0), ("w_in", 0)]}


def _assemble_weight(n, gathered):
    if SHARDED[n] == 2:
        full = jnp.transpose(gathered, (1, 0, 2)).reshape(gathered.shape[1], -1)
        return _perm_in_cols(full) if n == "w_in" else full
    return gathered.reshape(-1, gathered.shape[-1])


def _grad_source(n, g):
    if SHARDED[n] == 2:
        if n == "w_in":
            g = _perm_in_cols(g, inverse=True)
        k, nn = g.shape
        return jnp.transpose(g.reshape(k, N_DEV, nn // N_DEV), (1, 0, 2)), "lead"
    return g, "rows"


def _hosted(fn, keys_rider, land, *args, **kw):
    keys, rider = keys_rider
    if rider is None:
        return fn(*args, **kw)
    out, routs = fn(*args, rider=rider, **kw)
    land(keys, routs)
    return out


def _local_step(x, mem, target, p, big_w, shards=None):
    dist = shards is not None
    small, saved = {}, []
    big_g, ready, recv = {}, {}, {}
    mavg = _gn_avg_matrix()

    def weight_rider(l, host):
        keys = WEIGHT_RIDES.get((l, host), []) if dist else []
        return keys, (_Rider([shards[n][ll] for n, ll in keys], ["all"] * len(keys)) if keys else None)

    def land_weights(keys, routs):
        for (n, ll), r in zip(keys, routs):
            big_w[n][ll] = _assemble_weight(n, r)

    def grad_rider(l, host):
        keys = [k for k in GRAD_RIDES.get((l, host), []) if k in ready] if dist else []
        return keys, (_Rider([ready[k][0] for k in keys], [ready[k][1] for k in keys]) if keys else None)

    def land_grads(keys, routs):
        for k, r in zip(keys, routs):
            recv[k] = r
            del ready[k]

    def big_grad(n, l, g):
        if dist:
            ready[(n, l)] = _grad_source(n, g)
        else:
            big_g[(n, l)] = g

    xs = _ln_fwd(x, _row(p["ln_in_g"]), _row(p["ln_in_b"]), name="ln_in_fwd")
    for l in range(DEPTH):
        q = _layer_params(p, l)
        n = f"l{l}_"
        hin = _hosted(_mm, weight_rider(l, "inproj"), land_weights, xs, big_w["w_in"][l], bias=_row(p["b_in"][l]),
                      name=n + "inproj")
        keys, rd = weight_rider(l, "s5_fwd")
        (mix, s5_y1, s5_hb), routs = _s5_fwd(hin, q["wb"], q["apow"], q["wc"], _row(p["s5_d"][l]), p["s5_w_glu"][l],
                                             _row(p["s5_b_glu"][l]), name=n + "s5_fwd", rider=rd)
        land_weights(keys, routs)
        cvw = _pad_rows(p["cv_w"][l], CV_PAD)
        keys, rd = weight_rider(l, "cv_fwd")
        (mix, cv_c), routs = _cv_fwd(hin, cvw, _row(p["cv_b"][l]), _row(p["cv_gn_g"][l]), _row(p["cv_gn_b"][l]), mavg,
                                     p["cv_w_pw"][l], _row(p["cv_b_pw"][l]), mix, name=n + "cv_fwd", rider=rd)
        land_weights(keys, routs)
        lcw = _pad_rows(p["lru_conv_w"][l], SUBLANE)
        mix, lru_xc, lru_h = _lru_fwd(hin, lcw, _row(p["lru_conv_b"][l]), q["wr"], _row(p["lru_b_r"][l]), q["wi"],
                                      _row(p["lru_b_i"][l]), q["sp"], mix, name=n + "lru_fwd")
        kv = _mm(mem, big_w["attn_w_kv"][l], name=n + "kv")
        (kbig, vbig), kv_vjp = jax.vjp(_attn_big, kv)
        kbig, vbig = kbig.astype(BF16), vbig.astype(BF16)
        mix = _attn_fwd(hin, kbig, vbig, mix, name=n + "attn_fwd")
        r1 = _mm(mix, big_w["w_out"][l], bias=_row(p["b_out"][l]), res=xs, res_scale=ALPHA, name=n + "outproj")
        x1 = _ln_fwd(r1, _row(p["ln1_g"][l]), _row(p["ln1_b"][l]), name=n + "ln1_fwd")
        u = _hosted(_mm, weight_rider(l, "ffn_up"), land_weights, x1, big_w["ffn_w_up"][l], out_dtype=BF16,
                    name=n + "ffn_up")
        fcw = _pad_rows(p["ffn_conv_w"][l], SUBLANE)
        fcb = _row(p["ffn_conv_b"][l])
        hff = _hosted(_ffn_gate_fwd, weight_rider(l, "ffn_gate_fwd"), land_weights, u, fcw, fcb, name=n + "ffn_gate_fwd")
        r2 = _mm(hff, big_w["ffn_w_down"][l], res=x1, res_scale=ALPHA, name=n + "ffn_down")
        x2 = _ln_fwd(r2, _row(p["ln2_g"][l]), _row(p["ln2_b"][l]), name=n + "ln2_fwd")
        saved.append(dict(q=q, xs=xs, hin=hin, s5_y1=s5_y1, s5_hb=s5_hb, cvw=cvw, cv_c=cv_c, lcw=lcw, lru_xc=lru_xc,
                          lru_h=lru_h, kbig=kbig, vbig=vbig, kv_vjp=kv_vjp, mix=mix, r1=r1, x1=x1, u=u, fcw=fcw,
                          fcb=fcb, hff=hff, r2=r2))
        xs = x2

    dx, loss_blk = _loss_grad(xs, target, name="loss_grad")
    loss = loss_blk[0, 0]

    for l in reversed(range(DEPTH)):
        sv = saved[l]
        q = sv["q"]
        n = f"l{l}_"
        g = {}
        dr2, g["ln2_g"], g["ln2_b"], _ = _ln_bwd(sv["r2"], dx, _row(p["ln2_g"][l]), name=n + "ln2_bwd")
        big_grad("ffn_w_down", l, _hosted(_mm_tn, grad_rider(l, "dw_down"), land_grads, sv["hff"], dr2, name=n + "dw_down"))
        dhff = _mm(dr2, big_w["ffn_w_down"][l], trans_b=True, out_dtype=BF16, name=n + "dhff")
        keys, rd = grad_rider(l, "ffn_gate_bwd")
        (du, dfw, g["ffn_conv_b"]), routs = _ffn_gate_bwd(sv["u"], dhff, sv["fcw"], sv["fcb"], name=n + "ffn_gate_bwd",
                                                          rider=rd)
        land_grads(keys, routs)
        g["ffn_conv_w"] = dfw[:FFN_CONV_WIDTH]
        big_grad("ffn_w_up", l, _mm_tn(sv["x1"], du, name=n + "dw_up"))
        dx1 = _hosted(_mm, grad_rider(l, "dx1"), land_grads, du, big_w["ffn_w_up"][l], trans_b=True, res=dr2,
                      res_scale=ALPHA, name=n + "dx1")
        dr1, g["ln1_g"], g["ln1_b"], g["b_out"] = _ln_bwd(sv["r1"], dx1, _row(p["ln1_g"][l]), name=n + "ln1_bwd")
        big_grad("w_out", l, _mm_tn(sv["mix"], dr1, name=n + "dw_out"))
        dmix = _mm(dr1, big_w["w_out"][l], trans_b=True, name=n + "dmix")

        hin = sv["hin"]
        keys, rd = grad_rider(l, "cv_bwd")
        (dh, g["cv_w_pw"], dcw, g["cv_b_pw"], g["cv_gn_g"], g["cv_gn_b"], g["cv_b"]), routs = _cv_bwd(
            hin, sv["cv_c"], dmix, sv["cvw"], _row(p["cv_gn_g"][l]), _row(p["cv_gn_b"][l]), mavg, p["cv_w_pw"][l],
            name=n + "cv_bwd", rider=rd)
        land_grads(keys, routs)
        g["cv_w"] = dcw[:CONV_WIDTH]
        dh, dwr, dwi, dlcw, g["lru_b_r"], g["lru_b_i"], dsp, g["lru_conv_b"] = _lru_bwd(
            hin, sv["lru_xc"], sv["lru_h"], dmix, sv["lcw"], q["wr"], _row(p["lru_b_r"][l]), q["wi"],
            _row(p["lru_b_i"][l]), q["sp"], dh, name=n + "lru_bwd")
        g["lru_conv_w"] = dlcw[:LRU_CONV_WIDTH]
        g["lru_w_r"], g["lru_w_i"] = q["lru_w_vjp"]((dwr, dwi))
        (g["lru_lam"],) = q["sp_vjp"](dsp)
        keys, rd = grad_rider(l, "s5_bwd")
        (dh, g["s5_w_glu"], dwc, dwb, g["s5_b_glu"], g["s5_d"], da), routs = _s5_bwd(
            hin, sv["s5_y1"], dmix, sv["s5_hb"], q["wb"], q["apow"], q["wc"], _row(p["s5_d"][l]), p["s5_w_glu"][l],
            _row(p["s5_b_glu"][l]), dh, name=n + "s5_bwd", rider=rd)
        land_grads(keys, routs)
        (g["s5_lam_re"], g["s5_lam_im"], g["s5_log_dt"], g["s5_b_re"], g["s5_b_im"], g["s5_c_re"],
         g["s5_c_im"]) = q["s5_vjp"]((da, dwb, dwc))
        dh, dkbig, dvbig = _attn_bwd(hin, dmix, sv["kbig"], sv["vbig"], dh, name=n + "attn_bwd")
        (dkv,) = sv["kv_vjp"]((dkbig, dvbig))
        big_grad("attn_w_kv", l, _mm_tn(mem, dkv, name=n + "dw_kv"))

        g["b_in"] = _colsum_call(dh, name=n + "db_in")
        big_grad("w_in", l, _mm_tn(sv["xs"], dh, name=n + "dw_in"))
        dx = _hosted(_mm, grad_rider(l, "dxs"), land_grads, dh, big_w["w_in"][l], trans_b=True, res=dr1,
                     res_scale=ALPHA, name=n + "dxs")
        for k, v in g.items():
            small.setdefault(k, [None] * DEPTH)[l] = v.reshape(p[k].shape[1:])

    grad_x, dgi, dbi, _ = _ln_bwd(x, dx, _row(p["ln_in_g"]), name="ln_in_bwd")
    out = {k: jnp.stack(v, axis=0) for k, v in small.items()}
    out["ln_in_g"], out["ln_in_b"] = dgi.reshape(-1), dbi.reshape(-1)
    return loss, grad_x, out, ((recv, ready) if dist else big_g)


def kernel(x, mem, ln_in_g, ln_in_b, w_in, b_in, s5_lam_re, s5_lam_im, s5_log_dt, s5_b_re, s5_b_im, s5_c_re, s5_c_im, s5_d, s5_w_glu, s5_b_glu, cv_w, cv_b, cv_gn_g, cv_gn_b, cv_w_pw, cv_b_pw, lru_conv_w, lru_conv_b, lru_w_r, lru_b_r, lru_w_i, lru_b_i, lru_lam, attn_w_kv, w_out, b_out, ln1_g, ln1_b, ffn_w_up, ffn_conv_w, ffn_conv_b, ffn_w_down, ln2_g, ln2_b, loss_target, m_ln_in_g, m_ln_in_b, m_w_in, m_b_in, m_s5_lam_re, m_s5_lam_im, m_s5_log_dt, m_s5_b_re, m_s5_b_im, m_s5_c_re, m_s5_c_im, m_s5_d, m_s5_w_glu, m_s5_b_glu, m_cv_w, m_cv_b, m_cv_gn_g, m_cv_gn_b, m_cv_w_pw, m_cv_b_pw, m_lru_conv_w, m_lru_conv_b, m_lru_w_r, m_lru_b_r, m_lru_w_i, m_lru_b_i, m_lru_lam, m_attn_w_kv, m_w_out, m_b_out, m_ln1_g, m_ln1_b, m_ffn_w_up, m_ffn_conv_w, m_ffn_conv_b, m_ffn_w_down, m_ln2_g, m_ln2_b, v_ln_in_g, v_ln_in_b, v_w_in, v_b_in, v_s5_lam_re, v_s5_lam_im, v_s5_log_dt, v_s5_b_re, v_s5_b_im, v_s5_c_re, v_s5_c_im, v_s5_d, v_s5_w_glu, v_s5_b_glu, v_cv_w, v_cv_b, v_cv_gn_g, v_cv_gn_b, v_cv_w_pw, v_cv_b_pw, v_lru_conv_w, v_lru_conv_b, v_lru_w_r, v_lru_b_r, v_lru_w_i, v_lru_b_i, v_lru_lam, v_attn_w_kv, v_w_out, v_b_out, v_ln1_g, v_ln1_b, v_ffn_w_up, v_ffn_conv_w, v_ffn_conv_b, v_ffn_w_down, v_ln2_g, v_ln2_b):
    args = locals()
    w = {n: args[n] for n in WEIGHTS}
    mom = {n: args["m_" + n] for n in WEIGHTS}
    var = {n: args["v_" + n] for n in WEIGHTS}

    shards = {n: w[n].astype(BF16) for n in BIG}
    small_shapes = [w[n].shape for n in SMALL_SHARDED]
    gathered = _allgather([shards["w_in"][0], _pack([w[n] for n in SMALL_SHARDED], F32)], [0, 0], name="ag_weights")
    big_w = {n: [None] * DEPTH for n in BIG}
    big_w["w_in"][0] = _assemble_weight("w_in", gathered[0])
    p = {n: w[n] for n in REPLICATED}
    for n, st in zip(SMALL_SHARDED, _unpack(gathered[-1], small_shapes, lead=True)):
        p[n] = _join_shards(st, SHARDED[n])
    for n in ("s5_w_glu", "cv_w_pw"):
        p[n] = p[n].astype(BF16)
    p["b_in"] = _perm_in_cols(p["b_in"])

    loss, grad_x, g_small, (recv, ready) = _local_step(x[0], mem[0], loss_target[0], p, big_w, shards)
    loss = lax.psum(loss, ("x", "y", "c"))
    g_small["b_in"] = _perm_in_cols(g_small["b_in"], inverse=True)

    left = list(ready)
    rider = _Rider(
        [ready[k][0] for k in left] + [_pack_lead([_split_shards(g_small[n], SHARDED[n]) for n in SMALL_SHARDED], F32),
                                       _pack([g_small[n] for n in REPLICATED], F32)],
        [ready[k][1] for k in left] + ["lead", "all"])
    got = _exchange(rider, name="exchange_grads")
    for k, r in zip(left, got):
        recv[k] = r

    res = [dict(), dict(), dict(), dict()]
    for n in BIG:
        outs = None
        for l in range(DEPTH):
            outs = _adamw_layer(recv[(n, l)], w[n], mom[n], var[n], l, outs, name=f"adamw_{n}_l{l}")
        for kind in range(4):
            res[kind][n] = outs[kind]
    for names, got_pack, tag in ((SMALL_SHARDED, got[len(left)], "adamw_small_sharded"),
                                 (REPLICATED, got[len(left) + 1], "adamw_replicated")):
        outs = _adamw(got_pack, _pack([w[n] for n in names], F32), _pack([mom[n] for n in names], F32),
                      _pack([var[n] for n in names], F32), name=tag)
        for kind in range(4):
            for n, a in zip(names, _unpack(outs[kind], [w[n].shape for n in names])):
                res[kind][n] = a
    return (loss, grad_x[None], *[res[0][n] for n in WEIGHTS], *[res[1][n] for n in WEIGHTS],
            *[res[2][n] for n in WEIGHTS], *[res[3][n] for n in WEIGHTS])
```

```python
import math

import jax
import jax.numpy as jnp
from jax import lax
from jax.experimental import pallas as pl
from jax.experimental.pallas import tpu as pltpu

F32 = jnp.float32
BF16 = jnp.bfloat16

D_MODEL = 1024
DEPTH = 2
D_GROUP = 256
N_IN_COLS = 6 * D_GROUP
S5_GROUPS = 16
S5_CH = 16
S5_STATE = 64
S5_LANES = S5_GROUPS * S5_STATE
CONV_WIDTH = 31
GN_GROUPS = 4
LRU_HEADS = 4
LRU_CONV_WIDTH = 4
LRU_C = 8.0
ATTN_HEADS = 4
ATTN_HEAD_DIM = 64
D_FF = 2816
FFN_CONV_WIDTH = 3
ALPHA = (2 * DEPTH) ** 0.25
LN_EPS = 1e-5
ADAM_LR, ADAM_B1, ADAM_B2, ADAM_EPS, ADAM_WD, ADAM_STEP = 0.001, 0.9, 0.999, 1e-08, 0.01, 10

N_DEV = 8
N_PEERS = N_DEV - 1
LANE = 128
SUBLANE = 8
VMEM_LIMIT = 56 * 1024 * 1024
PACK_COLS = 1024
PACK_ROW_BLOCK = 256

SHARDED = {
    "w_in": 2, "s5_w_glu": 1, "cv_w": 2, "cv_w_pw": 1, "lru_conv_w": 2, "attn_w_kv": 1,
    "w_out": 1, "ffn_w_up": 2, "ffn_conv_w": 2, "ffn_w_down": 1,
}
BIG = ("w_in", "attn_w_kv", "w_out", "ffn_w_up", "ffn_w_down")
SMALL_SHARDED = ("s5_w_glu", "cv_w", "cv_w_pw", "lru_conv_w", "ffn_conv_w")
MATMUL_WEIGHTS = ("w_in", "s5_w_glu", "cv_w_pw", "attn_w_kv", "w_out", "ffn_w_up", "ffn_w_down")
WEIGHTS = ['ln_in_g', 'ln_in_b', 'w_in', 'b_in', 's5_lam_re', 's5_lam_im', 's5_log_dt', 's5_b_re', 's5_b_im',
           's5_c_re', 's5_c_im', 's5_d', 's5_w_glu', 's5_b_glu', 'cv_w', 'cv_b', 'cv_gn_g', 'cv_gn_b', 'cv_w_pw',
           'cv_b_pw', 'lru_conv_w', 'lru_conv_b', 'lru_w_r', 'lru_b_r', 'lru_w_i', 'lru_b_i', 'lru_lam',
           'attn_w_kv', 'w_out', 'b_out', 'ln1_g', 'ln1_b', 'ffn_w_up', 'ffn_conv_w', 'ffn_conv_b', 'ffn_w_down',
           'ln2_g', 'ln2_b']
REPLICATED = [n for n in WEIGHTS if n not in SHARDED]
REP_LAYERED = [n for n in REPLICATED if n not in ("ln_in_g", "ln_in_b")]

COL_CV_V, COL_CV_G, COL_LRU_G, COL_LRU_X, COL_S5, COL_Q = range(6)
IN_PERM = (1, 2, 3, 4, 0, 5)
MIX_S5, MIX_CV, MIX_LRU, MIX_ATTN = range(4)


_ANY = pl.BlockSpec(memory_space=pl.ANY)
_MESH = pl.DeviceIdType.MESH


def _cparams(n_axes):
    return pltpu.CompilerParams(dimension_semantics=("arbitrary",) * n_axes, vmem_limit_bytes=VMEM_LIMIT)


def _pick(n, cap):
    if n <= cap:
        return n
    best = None
    for t in range(LANE, cap + 1, LANE):
        if n % t == 0:
            best = t
    assert best is not None, (n, cap)
    return best


def _pick_rows(n, cap):
    best = None
    for t in range(SUBLANE, min(n, cap) + 1, SUBLANE):
        if n % t == 0:
            best = t
    assert best is not None, (n, cap)
    return best


def _full_spec(arr):
    nd = arr.ndim
    return pl.BlockSpec(arr.shape, lambda *_: (0,) * nd)


def _dot(a, b):
    return lax.dot_general(a.astype(BF16), b.astype(BF16), (((1,), (0,)), ((), ())), preferred_element_type=F32)


def _dot_nt(a, b):
    return lax.dot_general(a.astype(BF16), b.astype(BF16), (((1,), (1,)), ((), ())), preferred_element_type=F32)


def _dot_tn(a, b):
    return lax.dot_general(a.astype(BF16), b.astype(BF16), (((0,), (0,)), ((), ())), preferred_element_type=F32)


def _dot_hi(a, b):
    return jnp.dot(a, b, precision=lax.Precision.HIGHEST, preferred_element_type=F32)


def _colsum(x):
    return jnp.sum(x, axis=0, keepdims=True)


def _sigmoid(x):
    return 1.0 / (1.0 + jnp.exp(-x))


_GELU_K = math.sqrt(2.0 / math.pi)
_GELU_C = 0.044715


def _gelu(x):
    t = jnp.tanh(_GELU_K * (x + _GELU_C * x * x * x))
    return 0.5 * x * (1.0 + t)


def _gelu_and_grad(x):
    x2 = x * x
    t = jnp.tanh(_GELU_K * (x + _GELU_C * x2 * x))
    g = 0.5 * x * (1.0 + t)
    dg = 0.5 * (1.0 + t) + 0.5 * x * (1.0 - t * t) * (_GELU_K * (1.0 + 3.0 * _GELU_C * x2))
    return g, dg


def _neg_expm1(x):
    series = x * (1.0 + x * (0.5 + x * (1.0 / 6.0 + x * (1.0 / 24.0 + x * (1.0 / 120.0)))))
    return -jnp.where(jnp.abs(x) < 0.1, series, jnp.exp(x) - 1.0)


def _seq_tile(s, want):
    t = min(s, want)
    assert s % t == 0
    return t


class _Rider:
    def __init__(self, srcs, kinds):
        self.srcs, self.kinds = list(srcs), list(kinds)
        self.n = len(self.srcs)

    def out_shapes(self):
        shapes = []
        for x, kind in zip(self.srcs, self.kinds):
            if kind == "lead":
                shp = x.shape
            elif kind == "rows":
                shp = (N_DEV, x.shape[0] // N_DEV) + x.shape[1:]
            else:
                shp = (N_DEV,) + x.shape
            shapes.append(jax.ShapeDtypeStruct(shp, x.dtype))
        return shapes

    def scratch(self):
        return [pltpu.SemaphoreType.DMA((self.n * N_PEERS,)), pltpu.SemaphoreType.DMA((self.n * N_PEERS,)),
                pltpu.SemaphoreType.DMA((self.n,))]

    def _copies(self, x_refs, out_refs, sems):
        send_sems, recv_sems, local_sems = sems
        mx, my, mc = lax.axis_index("x"), lax.axis_index("y"), lax.axis_index("c")
        my_id = 4 * mx + 2 * my + mc

        def piece(i, dev):
            if self.kinds[i] == "lead":
                return x_refs[i].at[dev]
            if self.kinds[i] == "rows":
                r = x_refs[i].shape[0] // N_DEV
                return x_refs[i].at[pl.ds(pl.multiple_of(dev * r, SUBLANE), r)]
            return x_refs[i]

        mine = [pltpu.make_async_copy(piece(i, my_id), out_refs[i].at[my_id], local_sems.at[i]) for i in range(self.n)]
        copies = []
        for k in range(1, N_DEV):
            px, py, pc = mx ^ ((k >> 2) & 1), my ^ ((k >> 1) & 1), mc ^ (k & 1)
            for i in range(self.n):
                copies.append(pltpu.make_async_remote_copy(
                    src_ref=piece(i, 4 * px + 2 * py + pc), dst_ref=out_refs[i].at[my_id],
                    send_sem=send_sems.at[i * N_PEERS + k - 1], recv_sem=recv_sems.at[i * N_PEERS + k - 1],
                    device_id=(px, py, pc), device_id_type=_MESH))
        return mine, copies

    def start(self, x_refs, out_refs, sems):
        mine, copies = self._copies(x_refs, out_refs, sems)
        for cp in mine + copies:
            cp.start()

    def wait(self, x_refs, out_refs, sems):
        mine, copies = self._copies(x_refs, out_refs, sems)
        for cp in copies:
            cp.wait_recv()
        for cp in copies:
            cp.wait_send()
        for cp in mine:
            cp.wait()


def _call(body, *, grid, ins, in_specs, outs, out_specs, scratch=(), aliases=None, name, rider=None):
    n_axes = len(grid)
    common = dict(grid=grid, input_output_aliases=aliases or {}, compiler_params=_cparams(n_axes), name=name)
    if rider is None:
        res = pl.pallas_call(body, in_specs=list(in_specs), out_specs=list(out_specs), out_shape=list(outs),
                             scratch_shapes=list(scratch), **common)(*ins)
        return list(res), []
    n_in, n_out, n_scr, nr = len(ins), len(outs), len(scratch), rider.n

    def wrapped(*refs):
        pos = [0]

        def take(k):
            part = refs[pos[0]:pos[0] + k]
            pos[0] += k
            return part

        a_in, r_in, a_out, r_out, a_scr, sems = take(n_in), take(nr), take(n_out), take(nr), take(n_scr), take(3)
        first = last = None
        for ax in range(n_axes):
            pid = pl.program_id(ax)
            f, l = pid == 0, pid == grid[ax] - 1
            first = f if first is None else jnp.logical_and(first, f)
            last = l if last is None else jnp.logical_and(last, l)

        @pl.when(first)
        def _():
            rider.start(r_in, r_out, sems)

        body(*a_in, *a_out, *a_scr)

        @pl.when(last)
        def _():
            rider.wait(r_in, r_out, sems)

    res = pl.pallas_call(
        wrapped, in_specs=list(in_specs) + [_ANY] * nr, out_specs=list(out_specs) + [_ANY] * nr,
        out_shape=list(outs) + rider.out_shapes(), scratch_shapes=list(scratch) + rider.scratch(), **common)(*ins, *rider.srcs)
    return list(res[:n_out]), list(res[n_out:])


def _block_mask(n_blocks, block_rows, block_cols):
    r = jnp.arange(n_blocks * block_rows) // block_rows
    c = jnp.arange(n_blocks * block_cols) // block_cols
    return (r[:, None] == c[None, :]).astype(F32)


def _mm(a, b, *, bias=None, res=None, res_scale=1.0, trans_b=False, out_dtype=F32, name, rider=None):
    m, kdim = a.shape
    n = b.shape[0] if trans_b else b.shape[1]
    tm = _seq_tile(m, 1024)
    tn = _pick(n, 1408)
    tk = _pick(kdim, 1536)
    nk = kdim // tk
    has_bias, has_res = bias is not None, res is not None

    def body(*refs):
        a_ref, b_ref = refs[0], refs[1]
        pos = 2
        bias_ref = res_ref = None
        if has_bias:
            bias_ref = refs[pos]
            pos += 1
        if has_res:
            res_ref = refs[pos]
            pos += 1
        o_ref, acc_ref = refs[pos], refs[pos + 1]
        k = pl.program_id(2)

        @pl.when(k == 0)
        def _():
            acc_ref[...] = jnp.zeros_like(acc_ref)

        if trans_b:
            acc_ref[...] += _dot_nt(a_ref[...], b_ref[...])
        else:
            acc_ref[...] += _dot(a_ref[...], b_ref[...])

        @pl.when(k == nk - 1)
        def _():
            r = acc_ref[...]
            if has_bias:
                r = r + bias_ref[...]
            if has_res:
                r = r + res_scale * res_ref[...]
            o_ref[...] = r.astype(out_dtype)

    ins = [a, b]
    in_specs = [pl.BlockSpec((tm, tk), lambda i, j, k: (i, k)),
                pl.BlockSpec((tn, tk), lambda i, j, k: (j, k)) if trans_b
                else pl.BlockSpec((tk, tn), lambda i, j, k: (k, j))]
    if has_bias:
        ins.append(bias)
        in_specs.append(pl.BlockSpec((1, tn), lambda i, j, k: (0, j)))
    if has_res:
        ins.append(res)
        in_specs.append(pl.BlockSpec((tm, tn), lambda i, j, k: (i, j)))
    (out,), routs = _call(
        body, grid=(m // tm, n // tn, nk), ins=ins, in_specs=in_specs,
        outs=[jax.ShapeDtypeStruct((m, n), out_dtype)], out_specs=[pl.BlockSpec((tm, tn), lambda i, j, k: (i, j))],
        scratch=[pltpu.VMEM((tm, tn), F32)], name=name, rider=rider)
    return out if rider is None else (out, routs)


def _mm_tn(a, b, *, name, rider=None):
    s, ka = a.shape
    nb = b.shape[1]
    ts = _seq_tile(s, 512)
    tka = _pick(ka, 1408)
    tnb = _pick(nb, 1408)

    def body(a_ref, b_ref, o_ref):
        @pl.when(pl.program_id(2) == 0)
        def _():
            o_ref[...] = jnp.zeros_like(o_ref)

        o_ref[...] += _dot_tn(a_ref[...], b_ref[...])

    (out,), routs = _call(
        body, grid=(ka // tka, nb // tnb, s // ts), ins=[a, b],
        in_specs=[pl.BlockSpec((ts, tka), lambda i, j, k: (k, i)), pl.BlockSpec((ts, tnb), lambda i, j, k: (k, j))],
        outs=[jax.ShapeDtypeStruct((ka, nb), F32)], out_specs=[pl.BlockSpec((tka, tnb), lambda i, j, k: (i, j))],
        name=name, rider=rider)
    return out if rider is None else (out, routs)


def _colsum_call(x, *, name):
    s, n = x.shape
    ts = _seq_tile(s, 512)

    def body(x_ref, o_ref):
        @pl.when(pl.program_id(0) == 0)
        def _():
            o_ref[...] = jnp.zeros_like(o_ref)

        o_ref[...] += _colsum(x_ref[...])

    return pl.pallas_call(
        body, grid=(s // ts,), in_specs=[pl.BlockSpec((ts, n), lambda i: (i, 0))],
        out_specs=pl.BlockSpec((1, n), lambda i: (0, 0)), out_shape=jax.ShapeDtypeStruct((1, n), F32),
        compiler_params=_cparams(1), name=name)(x)


def _ln_fwd(r, g, b, *, name, rider=None):
    s, d = r.shape
    ts = _seq_tile(s, 512)

    def body(r_ref, g_ref, b_ref, o_ref):
        x = r_ref[...]
        mu = jnp.mean(x, axis=1, keepdims=True)
        xc = x - mu
        var = jnp.mean(xc * xc, axis=1, keepdims=True)
        o_ref[...] = xc * lax.rsqrt(var + LN_EPS) * g_ref[...] + b_ref[...]

    (out,), routs = _call(
        body, grid=(s // ts,), ins=[r, g, b],
        in_specs=[pl.BlockSpec((ts, d), lambda i: (i, 0)), _full_spec(g), _full_spec(b)],
        out_specs=[pl.BlockSpec((ts, d), lambda i: (i, 0))], outs=[jax.ShapeDtypeStruct((s, d), F32)],
        name=name, rider=rider)
    return out if rider is None else (out, routs)


def _ln_bwd(r, dy, g, *, name, rider=None):
    s, d = r.shape
    ts = _seq_tile(s, 512)

    def body(r_ref, dy_ref, g_ref, dr_ref, dg_ref, db_ref, ds_ref):
        @pl.when(pl.program_id(0) == 0)
        def _():
            dg_ref[...] = jnp.zeros_like(dg_ref)
            db_ref[...] = jnp.zeros_like(db_ref)
            ds_ref[...] = jnp.zeros_like(ds_ref)

        x = r_ref[...]
        dy = dy_ref[...]
        mu = jnp.mean(x, axis=1, keepdims=True)
        xc = x - mu
        var = jnp.mean(xc * xc, axis=1, keepdims=True)
        rstd = lax.rsqrt(var + LN_EPS)
        xh = xc * rstd
        dxh = dy * g_ref[...]
        m1 = jnp.mean(dxh, axis=1, keepdims=True)
        m2 = jnp.mean(dxh * xh, axis=1, keepdims=True)
        dr = rstd * (dxh - m1 - xh * m2)
        dr_ref[...] = dr
        dg_ref[...] += _colsum(dy * xh)
        db_ref[...] += _colsum(dy)
        ds_ref[...] += _colsum(dr)

    vec = jax.ShapeDtypeStruct((1, d), F32)
    vspec = pl.BlockSpec((1, d), lambda i: (0, 0))
    outs, routs = _call(
        body, grid=(s // ts,), ins=[r, dy, g],
        in_specs=[pl.BlockSpec((ts, d), lambda i: (i, 0)), pl.BlockSpec((ts, d), lambda i: (i, 0)), _full_spec(g)],
        out_specs=[pl.BlockSpec((ts, d), lambda i: (i, 0)), vspec, vspec, vspec],
        outs=[jax.ShapeDtypeStruct((s, d), F32), vec, vec, vec], name=name, rider=rider)
    return outs if rider is None else (outs, routs)


def _loss_grad(y, target, *, name):
    s, d = y.shape
    ts = _seq_tile(s, 512)

    def body(y_ref, t_ref, dy_ref, l_ref):
        @pl.when(pl.program_id(0) == 0)
        def _():
            l_ref[...] = jnp.zeros_like(l_ref)

        e = y_ref[...] - t_ref[...]
        dy_ref[...] = e * (1.0 / d)
        part = jnp.sum(jnp.sum(e * e, axis=1, keepdims=True), axis=0, keepdims=True) * (0.5 / d)
        l_ref[...] += jnp.broadcast_to(part, l_ref.shape)

    return pl.pallas_call(
        body, grid=(s // ts,),
        in_specs=[pl.BlockSpec((ts, d), lambda i: (i, 0)), pl.BlockSpec((ts, d), lambda i: (i, 0))],
        out_specs=[pl.BlockSpec((ts, d), lambda i: (i, 0)), pl.BlockSpec((SUBLANE, LANE), lambda i: (0, 0))],
        out_shape=[jax.ShapeDtypeStruct((s, d), F32), jax.ShapeDtypeStruct((SUBLANE, LANE), F32)],
        compiler_params=_cparams(1), name=name)(y, target)


SCAN_CHUNK = 32


def _cscan_levels(bufs, apow_ref, t, pad, *, reverse):
    half = bufs[0].shape[1] // 2
    ch = min(SCAN_CHUNK, t)
    nlev = t.bit_length() - 1
    assert (1 << nlev) == t
    for k in range(nlev):
        d = 1 << k
        src, dst = bufs[k % 2], bufs[(k + 1) % 2]

        def chunk(c, carry, src=src, dst=dst, d=d, k=k):
            ar = apow_ref[k:k + 1, :half]
            ai = apow_ref[k:k + 1, half:]
            if reverse:
                ai = -ai
            r0 = pl.multiple_of(c * ch, ch)
            cur = src[pl.ds(pad + r0, ch), :]
            if d >= SUBLANE:
                off = pad + d if reverse else pad - d
                sh = src[pl.ds(off + r0, ch), :]
            elif reverse:
                blk = src[pl.ds(pad + r0, ch + SUBLANE), :]
                sh = pltpu.roll(blk, ch + SUBLANE - d, axis=0)[:ch, :]
            else:
                blk = src[pl.ds(pad - SUBLANE + r0, ch + SUBLANE), :]
                sh = pltpu.roll(blk, d, axis=0)[SUBLANE:, :]
            sre, sim = sh[:, :half], sh[:, half:]
            dst[pl.ds(pad + r0, ch), :half] = cur[:, :half] + ar * sre - ai * sim
            dst[pl.ds(pad + r0, ch), half:] = cur[:, half:] + ar * sim + ai * sre
            return carry

        lax.fori_loop(0, t // ch, chunk, 0)
    return nlev % 2


def _rscan_levels(abufs, bbufs, t, pad, *, reverse):
    nlev = t.bit_length() - 1
    assert (1 << nlev) == t
    for k in range(nlev):
        d = 1 << k
        asrc, adst = abufs[k % 2], abufs[(k + 1) % 2]
        bsrc, bdst = bbufs[k % 2], bbufs[(k + 1) % 2]
        off = pad + d if reverse else pad - d
        a = asrc[pad:pad + t, :]
        bdst[pad:pad + t, :] = a * bsrc[off:off + t, :] + bsrc[pad:pad + t, :]
        if k < nlev - 1:
            adst[pad:pad + t, :] = a * asrc[off:off + t, :]
    return nlev % 2


S5_TILE = 256


def _s5_scan_forward(u_bf, wb_ref, apow_ref, state, bufs, t, pad):
    half = S5_LANES
    bufs[0][pad:pad + t, :] = _dot(u_bf, wb_ref[...])
    ar, ai = apow_ref[0:1, :half], apow_ref[0:1, half:]
    sr, si = state[:, :half], state[:, half:]
    bufs[0][pad:pad + 1, :half] += ar * sr - ai * si
    bufs[0][pad:pad + 1, half:] += ar * si + ai * sr
    return _cscan_levels(bufs, apow_ref, t, pad, reverse=False)


def _s5_fwd(h_in, wb, apow, wc, dvec, wglu, bglu, *, name, rider=None):
    s = h_in.shape[0]
    t = _seq_tile(s, S5_TILE)
    pad = t // 2
    nt = s // t
    lanes2 = 2 * S5_LANES

    def body(u_ref, wb_ref, apow_ref, wc_ref, d_ref, wglu_ref, bglu_ref, out_ref, y1_ref, hb_ref, buf0, buf1, carry):
        bufs = (buf0, buf1)

        @pl.when(pl.program_id(0) == 0)
        def _():
            buf0[0:pad, :] = jnp.zeros((pad, lanes2), F32)
            buf1[0:pad, :] = jnp.zeros((pad, lanes2), F32)
            carry[...] = jnp.zeros_like(carry)

        u = u_ref[...]
        state = carry[0:1, :]
        hb_ref[0] = state
        fin = _s5_scan_forward(u.astype(BF16), wb_ref, apow_ref, state, bufs, t, pad)
        hbuf = bufs[fin]
        carry[0:1, :] = hbuf[pad + t - 1:pad + t, :]
        y1 = _dot(hbuf[pad:pad + t, :], wc_ref[...]) + d_ref[...] * u
        y1_ref[...] = y1
        y2 = _gelu(y1)
        z = _dot(y2, wglu_ref[...]) + bglu_ref[...]
        out_ref[...] = (y2 * _sigmoid(z)).astype(BF16)

    ins = [h_in, wb, apow, wc, dvec, wglu, bglu]
    in_specs = [pl.BlockSpec((t, D_GROUP), lambda i: (i, COL_S5))] + [_full_spec(a) for a in ins[1:]]
    return _call(
        body, grid=(nt,), ins=ins, in_specs=in_specs,
        out_specs=[pl.BlockSpec((t, D_GROUP), lambda i: (i, MIX_S5)), pl.BlockSpec((t, D_GROUP), lambda i: (i, 0)),
                   pl.BlockSpec((1, 1, lanes2), lambda i: (i, 0, 0))],
        outs=[jax.ShapeDtypeStruct((s, D_MODEL), BF16), jax.ShapeDtypeStruct((s, D_GROUP), F32),
              jax.ShapeDtypeStruct((nt, 1, lanes2), F32)],
        scratch=[pltpu.VMEM((pad + t, lanes2), F32), pltpu.VMEM((pad + t, lanes2), F32),
                 pltpu.VMEM((SUBLANE, lanes2), F32)],
        name=name, rider=rider)


def _s5_bwd(h_in, y1, dmix, hb, wb, apow, wc, dvec, wglu, bglu, dh_all, *, name, rider=None):
    s = h_in.shape[0]
    t = _seq_tile(s, S5_TILE)
    pad = t // 2
    nt = s // t
    half = S5_LANES
    lanes2 = 2 * half
    rows = pad + t + pad

    def body(u_ref, y1_ref, do_ref, hb_ref, wb_ref, apow_ref, wc_ref, d_ref, wglu_ref, bglu_ref, _dh_in,
             du_ref, dwglu_ref, dwc_ref, dwb_ref, dbglu_ref, dd_ref, da_ref, buf0, buf1, buf2, buf3, carry):
        @pl.when(pl.program_id(0) == 0)
        def _():
            for bf in (buf0, buf1, buf2, buf3):
                bf[0:pad, :] = jnp.zeros((pad, lanes2), F32)
                bf[pad + t:rows, :] = jnp.zeros((pad, lanes2), F32)
            carry[...] = jnp.zeros_like(carry)
            for r in (dwglu_ref, dwc_ref, dwb_ref, dbglu_ref, dd_ref, da_ref):
                r[...] = jnp.zeros_like(r)

        u = u_ref[...]
        u_bf = u.astype(BF16)
        state = hb_ref[0]
        hfin = _s5_scan_forward(u_bf, wb_ref, apow_ref, state, (buf0, buf1), t, pad)
        hbuf = (buf0, buf1)[hfin]
        h_bf = hbuf[pad:pad + t, :].astype(BF16)

        y1 = y1_ref[...]
        dout = do_ref[...]
        y2, dgelu = _gelu_and_grad(y1)
        sg = _sigmoid(_dot(y2, wglu_ref[...]) + bglu_ref[...])
        dz = dout * y2 * sg * (1.0 - sg)
        dy2 = dout * sg + _dot_nt(dz, wglu_ref[...])
        dwglu_ref[...] += _dot_tn(y2, dz)
        dbglu_ref[...] += _colsum(dz)
        dy1 = dy2 * dgelu
        dd_ref[...] += _colsum(dy1 * u)
        dy1_bf = dy1.astype(BF16)
        dwc_ref[...] += _dot_tn(h_bf, dy1_bf)

        buf2[pad:pad + t, :] = _dot_nt(dy1_bf, wc_ref[...])
        ar, ai = apow_ref[0:1, :half], apow_ref[0:1, half:]
        cr, ci = carry[0:1, :half], carry[0:1, half:]
        buf2[pad + t - 1:pad + t, :half] += ar * cr + ai * ci
        buf2[pad + t - 1:pad + t, half:] += ar * ci - ai * cr
        lfin = _cscan_levels((buf2, buf3), apow_ref, t, pad, reverse=True)
        lbuf = (buf2, buf3)[lfin]
        lam = lbuf[pad:pad + t, :]
        carry[0:1, :] = lbuf[pad:pad + 1, :]
        lam_bf = lam.astype(BF16)
        du_ref[...] = dy1 * d_ref[...] + _dot_nt(lam_bf, wb_ref[...])
        dwb_ref[...] += _dot_tn(u_bf, lam_bf)

        hbuf[pad - 1:pad, :] = state
        hp = hbuf[pad - 1:pad - 1 + t, :]
        hbuf[pad - 1:pad, :] = jnp.zeros((1, lanes2), F32)
        lre, lim = lam[:, :half], lam[:, half:]
        hre, him = hp[:, :half], hp[:, half:]
        da_ref[:, :half] += _colsum(lre * hre + lim * him)
        da_ref[:, half:] += _colsum(lim * hre - lre * him)

    def rev(col):
        return lambda i: (nt - 1 - i, col)

    ins = [h_in, y1, dmix, hb, wb, apow, wc, dvec, wglu, bglu, dh_all]
    in_specs = [pl.BlockSpec((t, D_GROUP), rev(COL_S5)), pl.BlockSpec((t, D_GROUP), rev(0)),
                pl.BlockSpec((t, D_GROUP), rev(MIX_S5)), pl.BlockSpec((1, 1, lanes2), lambda i: (nt - 1 - i, 0, 0))] + \
               [_full_spec(a) for a in ins[4:10]] + [_ANY]
    outs = [jax.ShapeDtypeStruct((s, N_IN_COLS), F32), jax.ShapeDtypeStruct((D_GROUP, D_GROUP), F32),
            jax.ShapeDtypeStruct((lanes2, D_GROUP), F32), jax.ShapeDtypeStruct((D_GROUP, lanes2), F32),
            jax.ShapeDtypeStruct((1, D_GROUP), F32), jax.ShapeDtypeStruct((1, D_GROUP), F32),
            jax.ShapeDtypeStruct((1, lanes2), F32)]
    out_specs = [pl.BlockSpec((t, D_GROUP), rev(COL_S5))] + [_full_spec(o) for o in outs[1:]]
    return _call(
        body, grid=(nt,), ins=ins, in_specs=in_specs, out_specs=out_specs, outs=outs, aliases={10: 0},
        scratch=[pltpu.VMEM((rows, lanes2), F32) for _ in range(4)] + [pltpu.VMEM((SUBLANE, lanes2), F32)],
        name=name, rider=rider)


def _s5_param_map(lam_re, lam_im, log_dt, b_re, b_im, c_re, c_im):
    dt = jnp.exp(log_dt)[:, None]
    er = jnp.exp(lam_re * dt)
    a_re, a_im = er * jnp.cos(lam_im * dt), er * jnp.sin(lam_im * dt)
    den = lam_re * lam_re + lam_im * lam_im
    n_re = a_re - 1.0
    k_re = (n_re * lam_re + a_im * lam_im) / den
    k_im = (a_im * lam_re - n_re * lam_im) / den
    bb_re = k_re[..., None] * b_re - k_im[..., None] * b_im
    bb_im = k_re[..., None] * b_im + k_im[..., None] * b_re
    mask_in = _block_mask(S5_GROUPS, S5_CH, S5_STATE)
    mask_out = _block_mask(S5_GROUPS, S5_STATE, S5_CH)

    def blockdiag_in(m):
        return jnp.tile(jnp.transpose(m, (0, 2, 1)).reshape(S5_GROUPS * S5_CH, S5_STATE), (1, S5_GROUPS)) * mask_in

    def blockdiag_out(m):
        return jnp.tile(jnp.transpose(m, (0, 2, 1)).reshape(S5_LANES, S5_CH), (1, S5_GROUPS)) * mask_out

    a = jnp.concatenate([a_re.reshape(1, -1), a_im.reshape(1, -1)], axis=1)
    wb = jnp.concatenate([blockdiag_in(bb_re), blockdiag_in(bb_im)], axis=1)
    wc = jnp.concatenate([blockdiag_out(c_re), -blockdiag_out(c_im)], axis=0)
    return a, wb, wc


def _s5_apow(a, nlev):
    half = S5_LANES
    re, im = a[:, :half], a[:, half:]
    rows = []
    for _ in range(nlev):
        rows.append(jnp.concatenate([re, im], axis=1))
        re, im = re * re - im * im, 2.0 * re * im
    n_rows = -(-nlev // SUBLANE) * SUBLANE
    rows += [jnp.zeros_like(rows[0])] * (n_rows - nlev)
    return lax.stop_gradient(jnp.concatenate(rows, axis=0))


CV_TILE = 256
CV_PAD = 32
CV_CHUNK = 64


def _shifted_copies(buf, shifted, rows):
    n = rows - SUBLANE
    for s in range(1, SUBLANE):
        shifted[s - 1, 0:n, :] = buf[s:s + n, :]


def _window(buf, shifted, o, ch):
    q, s = divmod(o, SUBLANE)
    if s == 0:
        return buf[o:o + ch, :]
    return shifted[s - 1, q * SUBLANE:q * SUBLANE + ch, :]


def _gn_stats(c, mavg):
    mu = _dot_hi(c, mavg)
    cen = c - mu
    var = _dot_hi(cen * cen, mavg)
    rstd = lax.rsqrt(var + LN_EPS)
    return cen * rstd, rstd


def _cv_fwd(h_in, cw, cb, gng, gnb, mavg, wpw, bpw, mix, *, name, rider=None):
    s = h_in.shape[0]
    t = _seq_tile(s, CV_TILE)
    ch = min(CV_CHUNK, t)

    def body(v_ref, g_ref, cw_ref, cb_ref, gng_ref, gnb_ref, mavg_ref, wpw_ref, bpw_ref, _mix_in, out_ref, c_ref, xpad,
             shifted):
        @pl.when(pl.program_id(0) == 0)
        def _():
            xpad[0:CV_PAD, :] = jnp.zeros((CV_PAD, D_GROUP), F32)

        xpad[CV_PAD:CV_PAD + t, :] = v_ref[...] * _sigmoid(g_ref[...])
        _shifted_copies(xpad, shifted, t + CV_PAD)
        for r0 in range(0, t, ch):
            acc = jnp.broadcast_to(cb_ref[...], (ch, D_GROUP))
            for k in range(CONV_WIDTH):
                o = CV_PAD - (CONV_WIDTH - 1) + k + r0
                acc = acc + cw_ref[k:k + 1, :] * _window(xpad, shifted, o, ch)
            c_ref[r0:r0 + ch, :] = acc
        xpad[0:CV_PAD, :] = xpad[t:t + CV_PAD, :]
        xn, _ = _gn_stats(c_ref[...], mavg_ref[...])
        gn = xn * gng_ref[...] + gnb_ref[...]
        out_ref[...] = (_dot(gn * _sigmoid(gn), wpw_ref[...]) + bpw_ref[...]).astype(BF16)

    ins = [h_in, h_in, cw, cb, gng, gnb, mavg, wpw, bpw, mix]
    in_specs = [pl.BlockSpec((t, D_GROUP), lambda i: (i, COL_CV_V)), pl.BlockSpec((t, D_GROUP), lambda i: (i, COL_CV_G))] + \
               [_full_spec(a) for a in ins[2:9]] + [_ANY]
    return _call(
        body, grid=(s // t,), ins=ins, in_specs=in_specs,
        out_specs=[pl.BlockSpec((t, D_GROUP), lambda i: (i, MIX_CV)), pl.BlockSpec((t, D_GROUP), lambda i: (i, 0))],
        outs=[jax.ShapeDtypeStruct((s, D_MODEL), BF16), jax.ShapeDtypeStruct((s, D_GROUP), F32)],
        aliases={9: 0},
        scratch=[pltpu.VMEM((CV_PAD + t, D_GROUP), F32), pltpu.VMEM((SUBLANE - 1, CV_PAD + t, D_GROUP), F32)],
        name=name, rider=rider)


def _cv_bwd(h_in, c, dmix, cw, gng, gnb, mavg, wpw, *, name, rider=None):
    s = h_in.shape[0]
    t = _seq_tile(s, CV_TILE)
    nt = s // t
    ch = min(CV_CHUNK, t)

    def body(v_ref, g_ref, c_ref, do_ref, cw_ref, gng_ref, gnb_ref, mavg_ref, wpw_ref,
             dvg_ref, dwpw_ref, dcw_ref, dbpw_ref, dgg_ref, dgb_ref, dcb_ref, dcpad, hgbuf, shifted):
        @pl.when(pl.program_id(0) == 0)
        def _():
            dcpad[t:t + CV_PAD, :] = jnp.zeros((CV_PAD, D_GROUP), F32)
            for r in (dwpw_ref, dcw_ref, dbpw_ref, dgg_ref, dgb_ref, dcb_ref):
                r[...] = jnp.zeros_like(r)

        mavg = mavg_ref[...]
        xn, rstd = _gn_stats(c_ref[...], mavg)
        gg = gng_ref[...]
        gn = xn * gg + gnb_ref[...]
        sg = _sigmoid(gn)
        dout = do_ref[...]
        dwpw_ref[...] += _dot_tn(gn * sg, dout)
        dbpw_ref[...] += _colsum(dout)
        dgn = _dot_nt(dout, wpw_ref[...]) * (sg * (1.0 + gn * (1.0 - sg)))
        dgg_ref[...] += _colsum(dgn * xn)
        dgb_ref[...] += _colsum(dgn)
        dxn = dgn * gg
        dc = rstd * (dxn - _dot_hi(dxn, mavg) - xn * _dot_hi(dxn * xn, mavg))
        dcb_ref[...] += _colsum(dc)
        dcpad[0:t, :] = dc

        v = v_ref[...]
        sgm = _sigmoid(g_ref[...])
        hgbuf[...] = v * sgm
        _shifted_copies(dcpad, shifted, t + CV_PAD)
        for r0 in range(0, t, ch):
            hg = hgbuf[r0:r0 + ch, :]
            acc = jnp.zeros((ch, D_GROUP), F32)
            for k in range(CONV_WIDTH):
                o = (CONV_WIDTH - 1) - k + r0
                sh = _window(dcpad, shifted, o, ch)
                acc = acc + cw_ref[k:k + 1, :] * sh
                dcw_ref[k:k + 1, :] += _colsum(hg * sh)
            hgbuf[r0:r0 + ch, :] = acc
        dcpad[t:t + CV_PAD, :] = dcpad[0:CV_PAD, :]
        dhg = hgbuf[...]
        dvg_ref[:, :D_GROUP] = dhg * sgm
        dvg_ref[:, D_GROUP:] = dhg * v * sgm * (1.0 - sgm)

    def rev(col):
        return lambda i: (nt - 1 - i, col)

    ins = [h_in, h_in, c, dmix, cw, gng, gnb, mavg, wpw]
    in_specs = [pl.BlockSpec((t, D_GROUP), rev(COL_CV_V)), pl.BlockSpec((t, D_GROUP), rev(COL_CV_G)),
                pl.BlockSpec((t, D_GROUP), rev(0)), pl.BlockSpec((t, D_GROUP), rev(MIX_CV))] + [_full_spec(a) for a in ins[4:]]
    vec = jax.ShapeDtypeStruct((1, D_GROUP), F32)
    outs = [jax.ShapeDtypeStruct((s, N_IN_COLS), F32),
            jax.ShapeDtypeStruct((D_GROUP, D_GROUP), F32), jax.ShapeDtypeStruct((CV_PAD, D_GROUP), F32), vec, vec, vec, vec]
    out_specs = [pl.BlockSpec((t, 2 * D_GROUP), rev(COL_CV_V // 2))] + [_full_spec(o) for o in outs[1:]]
    return _call(
        body, grid=(nt,), ins=ins, in_specs=in_specs, out_specs=out_specs, outs=outs,
        scratch=[pltpu.VMEM((t + CV_PAD, D_GROUP), F32), pltpu.VMEM((t, D_GROUP), F32),
                 pltpu.VMEM((SUBLANE - 1, t + CV_PAD, D_GROUP), F32)], name=name, rider=rider)


LRU_TILE = 256


def _lru_gates(xc, wr_ref, br_ref, wi_ref, bi_ref, sp_ref):
    r = _sigmoid(_dot(xc, wr_ref[...]) + br_ref[...])
    i = _sigmoid(_dot(xc, wi_ref[...]) + bi_ref[...])
    log_a = -LRU_C * r * sp_ref[...]
    a = jnp.exp(log_a)
    m = jnp.sqrt(_neg_expm1(2.0 * log_a))
    return r, i, a, m


def _lru_fwd(h_in, lcw, lcb, wr, br, wi, bi, sp, mix, *, name):
    s = h_in.shape[0]
    t = _seq_tile(s, LRU_TILE)
    pad = max(t // 2, SUBLANE)

    def body(xg_ref, xr_ref, lcw_ref, lcb_ref, wr_ref, br_ref, wi_ref, bi_ref, sp_ref, _mix_in,
             out_ref, xc_ref, h_ref, xpad, a0, a1, b0, b1, carry):
        @pl.when(pl.program_id(0) == 0)
        def _():
            xpad[0:SUBLANE, :] = jnp.zeros((SUBLANE, D_GROUP), F32)
            for bf in (a0, a1, b0, b1):
                bf[0:pad, :] = jnp.zeros((pad, D_GROUP), F32)
            carry[...] = jnp.zeros_like(carry)

        xpad[SUBLANE:SUBLANE + t, :] = xr_ref[...]
        xc = jnp.broadcast_to(lcb_ref[...], (t, D_GROUP))
        for k in range(LRU_CONV_WIDTH):
            o = SUBLANE - (LRU_CONV_WIDTH - 1) + k
            xc = xc + lcw_ref[k:k + 1, :] * xpad[o:o + t, :]
        xpad[0:SUBLANE, :] = xpad[t:t + SUBLANE, :]
        xc_ref[...] = xc
        _, i, a, m = _lru_gates(xc, wr_ref, br_ref, wi_ref, bi_ref, sp_ref)
        a0[pad:pad + t, :] = a
        b0[pad:pad + t, :] = m * (i * xc)
        b0[pad:pad + 1, :] += a0[pad:pad + 1, :] * carry[0:1, :]
        fin = _rscan_levels((a0, a1), (b0, b1), t, pad, reverse=False)
        hbuf = (b0, b1)[fin]
        carry[0:1, :] = hbuf[pad + t - 1:pad + t, :]
        h = hbuf[pad:pad + t, :]
        h_ref[...] = h
        out_ref[...] = (h * _gelu(xg_ref[...])).astype(BF16)

    ins = [h_in, h_in, lcw, lcb, wr, br, wi, bi, sp, mix]
    row = pl.BlockSpec((t, D_GROUP), lambda i: (i, 0))
    in_specs = [pl.BlockSpec((t, D_GROUP), lambda i: (i, COL_LRU_G)), pl.BlockSpec((t, D_GROUP), lambda i: (i, COL_LRU_X))] + \
               [_full_spec(a) for a in ins[2:9]] + [_ANY]
    return pl.pallas_call(
        body, grid=(s // t,), in_specs=in_specs,
        out_specs=[pl.BlockSpec((t, D_GROUP), lambda i: (i, MIX_LRU)), row, row],
        out_shape=[jax.ShapeDtypeStruct((s, D_MODEL), BF16)] + [jax.ShapeDtypeStruct((s, D_GROUP), F32)] * 2,
        input_output_aliases={9: 0},
        scratch_shapes=[pltpu.VMEM((SUBLANE + t, D_GROUP), F32)] + [pltpu.VMEM((pad + t, D_GROUP), F32)] * 4 +
                       [pltpu.VMEM((SUBLANE, D_GROUP), F32)],
        compiler_params=_cparams(1), name=name)(*ins)


def _lru_bwd(h_in, xc_all, h_all, dmix, lcw, wr, br, wi, bi, sp, dh_all, *, name):
    s = h_in.shape[0]
    t = _seq_tile(s, LRU_TILE)
    nt = s // t
    pad = max(t // 2, SUBLANE)
    tb = t // SUBLANE

    def body(xg_ref, xr_ref, xc_ref, h_ref, hprev_ref, do_ref, lcw_ref, wr_ref, br_ref, wi_ref, bi_ref, sp_ref, _dh_in,
             dgr_ref, dwr_ref, dwi_ref, dlcw_ref, dbr_ref, dbi_ref, dsp_ref, dlcb_ref,
             a0, a1, b0, b1, hp, dxpad, carry):
        pid = pl.program_id(0)

        @pl.when(pid == 0)
        def _():
            for bf in (a0, a1, b0, b1):
                bf[pad + t:pad + t + pad, :] = jnp.zeros((pad, D_GROUP), F32)
            dxpad[t:t + SUBLANE, :] = jnp.zeros((SUBLANE, D_GROUP), F32)
            carry[...] = jnp.zeros_like(carry)
            for r in (dwr_ref, dwi_ref, dlcw_ref, dbr_ref, dbi_ref, dsp_ref, dlcb_ref):
                r[...] = jnp.zeros_like(r)

        xc = xc_ref[...]
        h = h_ref[...]
        dout = do_ref[...]
        gate, dgate = _gelu_and_grad(xg_ref[...])
        dgr_ref[:, :D_GROUP] = dout * h * dgate
        r, i, a, m = _lru_gates(xc, wr_ref, br_ref, wi_ref, bi_ref, sp_ref)

        a0[pad:pad + t, :] = a
        b0[pad:pad + t, :] = dout * gate
        b0[pad + t - 1:pad + t, :] += carry[0:1, :]
        a1[pad:pad + t, :] = a0[pad + 1:pad + 1 + t, :]
        fin = _rscan_levels((a1, a0), (b0, b1), t, pad, reverse=True)
        lam = (b0, b1)[fin][pad:pad + t, :]
        carry[0:1, :] = a[0:1, :] * lam[0:1, :]

        is_first = pid == nt - 1
        hp[0:SUBLANE, :] = jnp.where(is_first, 0.0, hprev_ref[...])
        hp[SUBLANE:SUBLANE + t, :] = h
        hprev = hp[SUBLANE - 1:SUBLANE - 1 + t, :]

        ix = i * xc
        dmm = lam * ix
        dix = lam * m
        da = lam * hprev - dmm * (a / m)
        dlog_a = da * a
        dr = dlog_a * (-LRU_C * sp_ref[...])
        dsp_ref[...] += _colsum(dlog_a * (-LRU_C * r))
        dpr = dr * r * (1.0 - r)
        dpi = dix * xc * i * (1.0 - i)
        dbr_ref[...] += _colsum(dpr)
        dbi_ref[...] += _colsum(dpi)
        dwr_ref[...] += _dot_tn(xc, dpr)
        dwi_ref[...] += _dot_tn(xc, dpi)
        dxc = dix * i + _dot_nt(dpr, wr_ref[...]) + _dot_nt(dpi, wi_ref[...])
        dlcb_ref[...] += _colsum(dxc)

        dxpad[0:t, :] = dxc
        xr = xr_ref[...]
        dxr = jnp.zeros((t, D_GROUP), F32)
        for k in range(LRU_CONV_WIDTH):
            o = (LRU_CONV_WIDTH - 1) - k
            sh = dxpad[o:o + t, :]
            dxr = dxr + lcw_ref[k:k + 1, :] * sh
            dlcw_ref[k:k + 1, :] += _colsum(xr * sh)
        dxpad[t:t + SUBLANE, :] = dxpad[0:SUBLANE, :]
        dgr_ref[:, D_GROUP:] = dxr

    def rev(col):
        return lambda i: (nt - 1 - i, col)

    ins = [h_in, h_in, xc_all, h_all, h_all, dmix, lcw, wr, br, wi, bi, sp, dh_all]
    in_specs = [pl.BlockSpec((t, D_GROUP), rev(COL_LRU_G)), pl.BlockSpec((t, D_GROUP), rev(COL_LRU_X)),
                pl.BlockSpec((t, D_GROUP), rev(0)), pl.BlockSpec((t, D_GROUP), rev(0)),
                pl.BlockSpec((SUBLANE, D_GROUP), lambda i: (jnp.maximum((nt - 1 - i) * tb - 1, 0), 0)),
                pl.BlockSpec((t, D_GROUP), rev(MIX_LRU))] + [_full_spec(a) for a in ins[6:12]] + [_ANY]
    vec = jax.ShapeDtypeStruct((1, D_GROUP), F32)
    mat = jax.ShapeDtypeStruct((D_GROUP, D_GROUP), F32)
    outs = [jax.ShapeDtypeStruct((s, N_IN_COLS), F32), mat, mat, jax.ShapeDtypeStruct((SUBLANE, D_GROUP), F32),
            vec, vec, vec, vec]
    out_specs = [pl.BlockSpec((t, 2 * D_GROUP), rev(COL_LRU_G // 2))] + [_full_spec(o) for o in outs[1:]]
    return pl.pallas_call(
        body, grid=(nt,), in_specs=in_specs, out_specs=out_specs, out_shape=outs, input_output_aliases={12: 0},
        scratch_shapes=[pltpu.VMEM((pad + t + pad, D_GROUP), F32)] * 4 +
                       [pltpu.VMEM((SUBLANE + t, D_GROUP), F32), pltpu.VMEM((t + SUBLANE, D_GROUP), F32),
                        pltpu.VMEM((SUBLANE, D_GROUP), F32)],
        compiler_params=_cparams(1), name=name)(*ins)


def _blockdiag(w):
    h, d, _ = w.shape
    return jnp.tile(w.reshape(h * d, d), (1, h)) * _block_mask(h, d, d)


ATTN_TILE = 512
ATTN_SCALE = ATTN_HEAD_DIM ** -0.5


def _attn_big(kv):
    m = kv.shape[0]
    kbig = jnp.tile(kv[:, :D_GROUP].T, (1, ATTN_HEADS)) * _block_mask(ATTN_HEADS, ATTN_HEAD_DIM, m)
    vbig = jnp.tile(kv[:, D_GROUP:], (ATTN_HEADS, 1)) * _block_mask(ATTN_HEADS, m, ATTN_HEAD_DIM)
    return kbig, vbig


def _attn_probs(q, kbig_ref, m):
    sc = _dot(q, kbig_ref[...]) * ATTN_SCALE
    ps = []
    for h in range(ATTN_HEADS):
        sh = sc[:, h * m:(h + 1) * m]
        e = jnp.exp(sh - jnp.max(sh, axis=1, keepdims=True))
        ps.append(e / jnp.sum(e, axis=1, keepdims=True))
    return ps


def _attn_fwd(h_in, kbig, vbig, mix, *, name):
    s = h_in.shape[0]
    t = _seq_tile(s, ATTN_TILE)
    m = kbig.shape[1] // ATTN_HEADS

    def body(q_ref, kbig_ref, vbig_ref, _mix_in, o_ref):
        ps = _attn_probs(q_ref[...], kbig_ref, m)
        o_ref[...] = _dot(jnp.concatenate(ps, axis=1), vbig_ref[...]).astype(BF16)

    return pl.pallas_call(
        body, grid=(s // t,),
        in_specs=[pl.BlockSpec((t, D_GROUP), lambda i: (i, COL_Q)), _full_spec(kbig), _full_spec(vbig), _ANY],
        out_specs=pl.BlockSpec((t, D_GROUP), lambda i: (i, MIX_ATTN)),
        out_shape=jax.ShapeDtypeStruct((s, D_MODEL), BF16), input_output_aliases={3: 0},
        compiler_params=_cparams(1), name=name)(h_in, kbig, vbig, mix)


def _attn_bwd(h_in, dmix, kbig, vbig, dh_all, *, name):
    s = h_in.shape[0]
    t = _seq_tile(s, ATTN_TILE)
    m = kbig.shape[1] // ATTN_HEADS

    def body(q_ref, do_ref, kbig_ref, vbig_ref, _dh_in, dq_ref, dk_ref, dv_ref):
        @pl.when(pl.program_id(0) == 0)
        def _():
            dk_ref[...] = jnp.zeros_like(dk_ref)
            dv_ref[...] = jnp.zeros_like(dv_ref)

        q = q_ref[...]
        dout = do_ref[...]
        ps = _attn_probs(q, kbig_ref, m)
        dp = _dot_nt(dout, vbig_ref[...])
        dss = []
        for h in range(ATTN_HEADS):
            dph = dp[:, h * m:(h + 1) * m]
            dss.append(ps[h] * (dph - jnp.sum(dph * ps[h], axis=1, keepdims=True)))
        ds = (jnp.concatenate(dss, axis=1) * ATTN_SCALE).astype(BF16)
        dv_ref[...] += _dot_tn(jnp.concatenate(ps, axis=1), dout)
        dq_ref[...] = _dot_nt(ds, kbig_ref[...])
        dk_ref[...] += _dot_tn(q, ds)

    outs = [jax.ShapeDtypeStruct((s, N_IN_COLS), F32), jax.ShapeDtypeStruct(kbig.shape, F32),
            jax.ShapeDtypeStruct(vbig.shape, F32)]
    return pl.pallas_call(
        body, grid=(s // t,),
        in_specs=[pl.BlockSpec((t, D_GROUP), lambda i: (i, COL_Q)), pl.BlockSpec((t, D_GROUP), lambda i: (i, MIX_ATTN)),
                  _full_spec(kbig), _full_spec(vbig), _ANY],
        out_specs=[pl.BlockSpec((t, D_GROUP), lambda i: (i, COL_Q)), _full_spec(outs[1]), _full_spec(outs[2])],
        out_shape=outs, input_output_aliases={4: 0},
        compiler_params=_cparams(1), name=name)(h_in, dmix, kbig, vbig, dh_all)


FFN_TILE = 128
FFN_COL_CHUNK = 256
FFN_ROW_CHUNK = 64


def _ffn_conv(pad_ref, w_ref, b_ref, r0, ch, c0):
    cc = FFN_COL_CHUNK
    acc = jnp.broadcast_to(b_ref[:, c0:c0 + cc], (ch, cc))
    for k in range(FFN_CONV_WIDTH):
        o = SUBLANE - (FFN_CONV_WIDTH - 1) + k + r0
        acc = acc + w_ref[k:k + 1, c0:c0 + cc] * pad_ref[o:o + ch, c0:c0 + cc]
    return acc


def _ffn_gate_fwd(u, fcw, fcb, *, name, rider=None):
    s = u.shape[0]
    t = _seq_tile(s, FFN_TILE)
    ch = min(FFN_ROW_CHUNK, t)
    cc = FFN_COL_CHUNK

    def body(u_ref, w_ref, b_ref, o_ref, upad):
        @pl.when(pl.program_id(0) == 0)
        def _():
            upad[0:SUBLANE, :] = jnp.zeros((SUBLANE, 2 * D_FF), F32)

        upad[SUBLANE:SUBLANE + t, :] = u_ref[...].astype(F32)
        for c0 in range(0, D_FF, cc):
            for r0 in range(0, t, ch):
                val = _ffn_conv(upad, w_ref, b_ref, r0, ch, c0)
                gt = _ffn_conv(upad, w_ref, b_ref, r0, ch, c0 + D_FF)
                o_ref[r0:r0 + ch, c0:c0 + cc] = (val * _gelu(gt)).astype(BF16)
        upad[0:SUBLANE, :] = upad[t:t + SUBLANE, :]

    (out,), routs = _call(
        body, grid=(s // t,), ins=[u, fcw, fcb],
        in_specs=[pl.BlockSpec((t, 2 * D_FF), lambda i: (i, 0)), _full_spec(fcw), _full_spec(fcb)],
        out_specs=[pl.BlockSpec((t, D_FF), lambda i: (i, 0))], outs=[jax.ShapeDtypeStruct((s, D_FF), BF16)],
        scratch=[pltpu.VMEM((SUBLANE + t, 2 * D_FF), F32)], name=name, rider=rider)
    return out if rider is None else (out, routs)


def _ffn_gate_bwd(u, dh, fcw, fcb, *, name, rider=None):
    s = u.shape[0]
    t = _seq_tile(s, FFN_TILE)
    nt = s // t
    ch = min(FFN_ROW_CHUNK, t)
    cc = FFN_COL_CHUNK
    halo = 2 * SUBLANE
    tb = t // halo

    def body(u_ref, halo_ref, dh_ref, w_ref, b_ref, du_ref, dw_ref, db_ref, upad, dpad):
        pid = pl.program_id(0)

        @pl.when(pid == 0)
        def _():
            dpad[t:t + SUBLANE, :] = jnp.zeros((SUBLANE, 2 * D_FF), F32)
            dw_ref[...] = jnp.zeros_like(dw_ref)
            db_ref[...] = jnp.zeros_like(db_ref)

        upad[0:SUBLANE, :] = jnp.where(pid == nt - 1, 0.0, halo_ref[...].astype(F32)[SUBLANE:, :])
        upad[SUBLANE:SUBLANE + t, :] = u_ref[...].astype(F32)
        for c0 in range(0, D_FF, cc):
            for r0 in range(0, t, ch):
                val = _ffn_conv(upad, w_ref, b_ref, r0, ch, c0)
                gt = _ffn_conv(upad, w_ref, b_ref, r0, ch, c0 + D_FF)
                gl, dgl = _gelu_and_grad(gt)
                d = dh_ref[r0:r0 + ch, c0:c0 + cc].astype(F32)
                dpad[r0:r0 + ch, c0:c0 + cc] = d * gl
                dpad[r0:r0 + ch, c0 + D_FF:c0 + D_FF + cc] = d * val * dgl
        for c0 in range(0, 2 * D_FF, cc):
            dbs = jnp.zeros((1, cc), F32)
            dws = [jnp.zeros((1, cc), F32) for _ in range(FFN_CONV_WIDTH)]
            for r0 in range(0, t, ch):
                x = upad[SUBLANE + r0:SUBLANE + r0 + ch, c0:c0 + cc]
                acc = jnp.zeros((ch, cc), F32)
                for k in range(FFN_CONV_WIDTH):
                    o = (FFN_CONV_WIDTH - 1) - k + r0
                    sh = dpad[o:o + ch, c0:c0 + cc]
                    acc = acc + w_ref[k:k + 1, c0:c0 + cc] * sh
                    dws[k] = dws[k] + _colsum(x * sh)
                    if k == FFN_CONV_WIDTH - 1:
                        dbs = dbs + _colsum(sh)
                du_ref[r0:r0 + ch, c0:c0 + cc] = acc.astype(BF16)
            db_ref[:, c0:c0 + cc] += dbs
            for k in range(FFN_CONV_WIDTH):
                dw_ref[k:k + 1, c0:c0 + cc] += dws[k]
        dpad[t:t + SUBLANE, :] = dpad[0:SUBLANE, :]

    outs = [jax.ShapeDtypeStruct((s, 2 * D_FF), BF16), jax.ShapeDtypeStruct((SUBLANE, 2 * D_FF), F32),
            jax.ShapeDtypeStruct((1, 2 * D_FF), F32)]
    return _call(
        body, grid=(nt,), ins=[u, u, dh, fcw, fcb],
        in_specs=[pl.BlockSpec((t, 2 * D_FF), lambda i: (nt - 1 - i, 0)),
                  pl.BlockSpec((halo, 2 * D_FF), lambda i: (jnp.maximum((nt - 1 - i) * tb - 1, 0), 0)),
                  pl.BlockSpec((t, D_FF), lambda i: (nt - 1 - i, 0)), _full_spec(fcw), _full_spec(fcb)],
        out_specs=[pl.BlockSpec((t, 2 * D_FF), lambda i: (nt - 1 - i, 0)), _full_spec(outs[1]), _full_spec(outs[2])],
        outs=outs,
        scratch=[pltpu.VMEM((SUBLANE + t, 2 * D_FF), F32), pltpu.VMEM((t + SUBLANE, 2 * D_FF), F32)],
        name=name, rider=rider)


def _adamw_body(g_ref, w_ref, m_ref, v_ref, go_ref, d_ref, mo_ref, vo_ref):
    inv_b1 = 1.0 - ADAM_B1 ** ADAM_STEP
    inv_b2 = 1.0 - ADAM_B2 ** ADAM_STEP
    g = g_ref[0]
    for dev in range(1, N_DEV):
        g = g + g_ref[dev]
    go_ref[...] = g
    mn = ADAM_B1 * m_ref[...] + (1.0 - ADAM_B1) * g
    vn = ADAM_B2 * v_ref[...] + (1.0 - ADAM_B2) * (g * g)
    mo_ref[...] = mn
    vo_ref[...] = vn
    d_ref[...] = -ADAM_LR * ((mn / inv_b1) / (jnp.sqrt(vn / inv_b2) + ADAM_EPS) + ADAM_WD * w_ref[...])


def _adamw(gstack, w, m, v, *, name):
    _, r, c = gstack.shape
    tr = _pick_rows(r, PACK_ROW_BLOCK)

    def body(*refs):
        _adamw_body(*refs)

    blk = pl.BlockSpec((tr, c), lambda i: (i, 0))
    sh = jax.ShapeDtypeStruct((r, c), F32)
    return pl.pallas_call(
        body, grid=(r // tr,),
        in_specs=[pl.BlockSpec((N_DEV, tr, c), lambda i: (0, i, 0)), blk, blk, blk],
        out_specs=[blk] * 4, out_shape=[sh] * 4,
        compiler_params=_cparams(1), name=name)(gstack, w, m, v)


def _adamw_layer(gstack, w, m, v, layer, into, *, name):
    n_layers, r, c = w.shape
    tr = _pick_rows(r, PACK_ROW_BLOCK)

    def body(g_ref, w_ref, m_ref, v_ref, *rest):
        _adamw_body(g_ref, w_ref, m_ref, v_ref, *rest[-4:])

    blk = pl.BlockSpec((None, tr, c), lambda i: (layer, i, 0))
    sh = jax.ShapeDtypeStruct((n_layers, r, c), F32)
    into = list(into or [])
    return pl.pallas_call(
        body, grid=(r // tr,),
        in_specs=[pl.BlockSpec((N_DEV, tr, c), lambda i: (0, i, 0)), blk, blk, blk] + [_ANY] * len(into),
        out_specs=[blk] * 4, out_shape=[sh] * 4, input_output_aliases={4 + k: k for k in range(len(into))},
        compiler_params=_cparams(1), name=name)(gstack, w, m, v, *into)


def _exchange(rider, *, name):
    n = rider.n

    def body(*refs):
        x_refs, out_refs, sems = refs[:n], refs[n:2 * n], refs[2 * n:]
        rider.start(x_refs, out_refs, sems)
        rider.wait(x_refs, out_refs, sems)

    return pl.pallas_call(
        body, in_specs=[_ANY] * n, out_specs=[_ANY] * n, out_shape=rider.out_shapes(),
        scratch_shapes=rider.scratch(), name=name)(*rider.srcs)


def _pack_rows(n):
    rows = -(-n // PACK_COLS)
    return -(-rows // SUBLANE) * SUBLANE


def _pack(arrs, dtype):
    flat = jnp.concatenate([a.reshape(-1).astype(dtype) for a in arrs])
    rows = _pack_rows(flat.shape[0])
    flat = jnp.pad(flat, (0, rows * PACK_COLS - flat.shape[0]))
    return flat.reshape(rows, PACK_COLS)


def _pack_lead(arrs, dtype):
    flat = jnp.concatenate([a.reshape(N_DEV, -1).astype(dtype) for a in arrs], axis=1)
    rows = _pack_rows(flat.shape[1])
    flat = jnp.pad(flat, ((0, 0), (0, rows * PACK_COLS - flat.shape[1])))
    return flat.reshape(N_DEV, rows, PACK_COLS)


def _unpack(packed, shapes, lead=False):
    flat = packed.reshape(N_DEV, -1) if lead else packed.reshape(-1)
    out, pos = [], 0
    for sh in shapes:
        n = math.prod(sh)
        out.append(flat[:, pos:pos + n].reshape((N_DEV,) + tuple(sh)) if lead else flat[pos:pos + n].reshape(sh))
        pos += n
    return out


def _join_shards(stacked, axis):
    return jnp.concatenate([stacked[d] for d in range(N_DEV)], axis=axis)


def _split_shards(full, axis):
    return jnp.stack(jnp.split(full, N_DEV, axis=axis), axis=0)


def _perm_in_cols(a, inverse=False):
    blocks = jnp.split(a, 6, axis=-1)
    if inverse:
        order = [IN_PERM.index(j) for j in range(6)]
    else:
        order = list(IN_PERM)
    return jnp.concatenate([blocks[j] for j in order], axis=-1)


def _row(v):
    return v.reshape(1, -1)


def _pad_rows(w, rows):
    return jnp.pad(w, ((0, rows - w.shape[0]), (0, 0)))


def _gn_avg_matrix():
    return _block_mask(GN_GROUPS, D_GROUP // GN_GROUPS, D_GROUP // GN_GROUPS) / (D_GROUP // GN_GROUPS)


def _layer_params(p, l):
    q = {}
    (a, wb, wc), q["s5_vjp"] = jax.vjp(_s5_param_map, p["s5_lam_re"][l], p["s5_lam_im"][l], p["s5_log_dt"][l],
                                       p["s5_b_re"][l], p["s5_b_im"][l], p["s5_c_re"][l], p["s5_c_im"][l])
    q["wb"], q["wc"] = wb.astype(BF16), wc.astype(BF16)
    q["apow"] = _s5_apow(a, max(S5_TILE.bit_length() - 1, 1))
    (q["wr"], q["wi"]), q["lru_w_vjp"] = jax.vjp(lambda r, i: (_blockdiag(r), _blockdiag(i)), p["lru_w_r"][l], p["lru_w_i"][l])
    q["wr"], q["wi"] = q["wr"].astype(BF16), q["wi"].astype(BF16)
    q["sp"], q["sp_vjp"] = jax.vjp(lambda lam: _row(jax.nn.softplus(-lam)), p["lru_lam"][l])
    return q


WEIGHT_RIDES = {(0, "ln_in_fwd"): [("w_in", 0), ("small_pack", 0)],
                (0, "inproj"): [("attn_w_kv", 0), ("w_out", 0)], (0, "s5_fwd"): [("ffn_w_up", 0)],
                (0, "cv_fwd"): [("ffn_w_down", 0)],
                (0, "ffn_up"): [("w_in", 1), ("attn_w_kv", 1), ("w_out", 1), ("ffn_w_down", 1)],
                (0, "ffn_gate_fwd"): [("ffn_w_up", 1)]}
GRAD_RIDES = {(1, "ffn_gate_bwd"): [("ffn_w_down", 1)],
              (0, "dw_down"): [("w_out", 1), ("attn_w_kv", 1), ("w_in", 1)],
              (0, "dhff"): [("rep", 1), ("ssh", 1)],
              (0, "ln_in_bwd"): [("rep", 0), ("ssh", 0)],
              (0, "ffn_gate_bwd"): [("ffn_w_up", 1)],
              (0, "dx1"): [("ffn_w_down", 0)],
              (0, "cv_bwd"): [("w_out", 0)],
              (0, "s5_bwd"): [("ffn_w_up", 0)],
              (0, "dxs"): [("attn_w_kv", 0), ("w_in", 0)]}


def _assemble_weight(n, gathered):
    if SHARDED[n] == 2:
        full = jnp.transpose(gathered, (1, 0, 2)).reshape(gathered.shape[1], -1)
        return _perm_in_cols(full) if n == "w_in" else full
    return gathered.reshape(-1, gathered.shape[-1])


def _grad_source(n, g):
    if SHARDED[n] == 2:
        if n == "w_in":
            g = _perm_in_cols(g, inverse=True)
        k, nn = g.shape
        return jnp.transpose(g.reshape(k, N_DEV, nn // N_DEV), (1, 0, 2)), "lead"
    return g, "rows"


def _hosted(fn, keys_rider, land, *args, **kw):
    keys, rider = keys_rider
    if rider is None:
        return fn(*args, **kw)
    out, routs = fn(*args, rider=rider, **kw)
    land(keys, routs)
    return out


def _local_step(x, mem, target, p, big_w, shards=None, unpack_small=None):
    dist = shards is not None
    small, saved = {}, []
    big_g, ready, recv = {}, {}, {}
    mavg = _gn_avg_matrix()

    def weight_rider(l, host):
        keys = WEIGHT_RIDES.get((l, host), []) if dist else []
        return keys, (_Rider([shards[n][ll] for n, ll in keys], ["all"] * len(keys)) if keys else None)

    def land_weights(keys, routs):
        for (n, ll), r in zip(keys, routs):
            if n == "small_pack":
                p.update(unpack_small(r))
            else:
                big_w[n][ll] = _assemble_weight(n, r)

    def grad_rider(l, host):
        keys = [k for k in GRAD_RIDES.get((l, host), []) if k in ready] if dist else []
        return keys, (_Rider([ready[k][0] for k in keys], [ready[k][1] for k in keys]) if keys else None)

    def land_grads(keys, routs):
        for k, r in zip(keys, routs):
            recv[k] = r
            del ready[k]

    def big_grad(n, l, g):
        if dist:
            ready[(n, l)] = _grad_source(n, g)
        else:
            big_g[(n, l)] = g

    xs = _hosted(_ln_fwd, weight_rider(0, "ln_in_fwd"), land_weights, x, _row(p["ln_in_g"]), _row(p["ln_in_b"]),
                 name="ln_in_fwd")
    for l in range(DEPTH):
        q = _layer_params(p, l)
        n = f"l{l}_"
        hin = _hosted(_mm, weight_rider(l, "inproj"), land_weights, xs, big_w["w_in"][l], bias=_row(p["b_in"][l]),
                      name=n + "inproj")
        keys, rd = weight_rider(l, "s5_fwd")
        (mix, s5_y1, s5_hb), routs = _s5_fwd(hin, q["wb"], q["apow"], q["wc"], _row(p["s5_d"][l]), p["s5_w_glu"][l],
                                             _row(p["s5_b_glu"][l]), name=n + "s5_fwd", rider=rd)
        land_weights(keys, routs)
        cvw = _pad_rows(p["cv_w"][l], CV_PAD)
        keys, rd = weight_rider(l, "cv_fwd")
        (mix, cv_c), routs = _cv_fwd(hin, cvw, _row(p["cv_b"][l]), _row(p["cv_gn_g"][l]), _row(p["cv_gn_b"][l]), mavg,
                                     p["cv_w_pw"][l], _row(p["cv_b_pw"][l]), mix, name=n + "cv_fwd", rider=rd)
        land_weights(keys, routs)
        lcw = _pad_rows(p["lru_conv_w"][l], SUBLANE)
        mix, lru_xc, lru_h = _lru_fwd(hin, lcw, _row(p["lru_conv_b"][l]), q["wr"], _row(p["lru_b_r"][l]), q["wi"],
                                      _row(p["lru_b_i"][l]), q["sp"], mix, name=n + "lru_fwd")
        kv = _mm(mem, big_w["attn_w_kv"][l], name=n + "kv")
        (kbig, vbig), kv_vjp = jax.vjp(_attn_big, kv)
        kbig, vbig = kbig.astype(BF16), vbig.astype(BF16)
        mix = _attn_fwd(hin, kbig, vbig, mix, name=n + "attn_fwd")
        r1 = _mm(mix, big_w["w_out"][l], bias=_row(p["b_out"][l]), res=xs, res_scale=ALPHA, name=n + "outproj")
        x1 = _ln_fwd(r1, _row(p["ln1_g"][l]), _row(p["ln1_b"][l]), name=n + "ln1_fwd")
        u = _hosted(_mm, weight_rider(l, "ffn_up"), land_weights, x1, big_w["ffn_w_up"][l], out_dtype=BF16,
                    name=n + "ffn_up")
        fcw = _pad_rows(p["ffn_conv_w"][l], SUBLANE)
        fcb = _row(p["ffn_conv_b"][l])
        hff = _hosted(_ffn_gate_fwd, weight_rider(l, "ffn_gate_fwd"), land_weights, u, fcw, fcb, name=n + "ffn_gate_fwd")
        r2 = _mm(hff, big_w["ffn_w_down"][l], res=x1, res_scale=ALPHA, name=n + "ffn_down")
        x2 = _ln_fwd(r2, _row(p["ln2_g"][l]), _row(p["ln2_b"][l]), name=n + "ln2_fwd")
        saved.append(dict(q=q, xs=xs, hin=hin, s5_y1=s5_y1, s5_hb=s5_hb, cvw=cvw, cv_c=cv_c, lcw=lcw, lru_xc=lru_xc,
                          lru_h=lru_h, kbig=kbig, vbig=vbig, kv_vjp=kv_vjp, mix=mix, r1=r1, x1=x1, u=u, fcw=fcw,
                          fcb=fcb, hff=hff, r2=r2))
        xs = x2

    dx, loss_blk = _loss_grad(xs, target, name="loss_grad")
    loss = loss_blk[0, 0]

    for l in reversed(range(DEPTH)):
        sv = saved[l]
        q = sv["q"]
        n = f"l{l}_"
        g = {}
        dr2, g["ln2_g"], g["ln2_b"], _ = _ln_bwd(sv["r2"], dx, _row(p["ln2_g"][l]), name=n + "ln2_bwd")
        big_grad("ffn_w_down", l, _hosted(_mm_tn, grad_rider(l, "dw_down"), land_grads, sv["hff"], dr2, name=n + "dw_down"))
        dhff = _hosted(_mm, grad_rider(l, "dhff"), land_grads, dr2, big_w["ffn_w_down"][l], trans_b=True,
                       out_dtype=BF16, name=n + "dhff")
        keys, rd = grad_rider(l, "ffn_gate_bwd")
        (du, dfw, g["ffn_conv_b"]), routs = _ffn_gate_bwd(sv["u"], dhff, sv["fcw"], sv["fcb"], name=n + "ffn_gate_bwd",
                                                          rider=rd)
        land_grads(keys, routs)
        g["ffn_conv_w"] = dfw[:FFN_CONV_WIDTH]
        big_grad("ffn_w_up", l, _mm_tn(sv["x1"], du, name=n + "dw_up"))
        dx1 = _hosted(_mm, grad_rider(l, "dx1"), land_grads, du, big_w["ffn_w_up"][l], trans_b=True, res=dr2,
                      res_scale=ALPHA, name=n + "dx1")
        dr1, g["ln1_g"], g["ln1_b"], g["b_out"] = _ln_bwd(sv["r1"], dx1, _row(p["ln1_g"][l]), name=n + "ln1_bwd")
        big_grad("w_out", l, _mm_tn(sv["mix"], dr1, name=n + "dw_out"))
        dmix = _mm(dr1, big_w["w_out"][l], trans_b=True, name=n + "dmix")

        hin = sv["hin"]
        keys, rd = grad_rider(l, "cv_bwd")
        (dh, g["cv_w_pw"], dcw, g["cv_b_pw"], g["cv_gn_g"], g["cv_gn_b"], g["cv_b"]), routs = _cv_bwd(
            hin, sv["cv_c"], dmix, sv["cvw"], _row(p["cv_gn_g"][l]), _row(p["cv_gn_b"][l]), mavg, p["cv_w_pw"][l],
            name=n + "cv_bwd", rider=rd)
        land_grads(keys, routs)
        g["cv_w"] = dcw[:CONV_WIDTH]
        dh, dwr, dwi, dlcw, g["lru_b_r"], g["lru_b_i"], dsp, g["lru_conv_b"] = _lru_bwd(
            hin, sv["lru_xc"], sv["lru_h"], dmix, sv["lcw"], q["wr"], _row(p["lru_b_r"][l]), q["wi"],
            _row(p["lru_b_i"][l]), q["sp"], dh, name=n + "lru_bwd")
        g["lru_conv_w"] = dlcw[:LRU_CONV_WIDTH]
        g["lru_w_r"], g["lru_w_i"] = q["lru_w_vjp"]((dwr, dwi))
        (g["lru_lam"],) = q["sp_vjp"](dsp)
        keys, rd = grad_rider(l, "s5_bwd")
        (dh, g["s5_w_glu"], dwc, dwb, g["s5_b_glu"], g["s5_d"], da), routs = _s5_bwd(
            hin, sv["s5_y1"], dmix, sv["s5_hb"], q["wb"], q["apow"], q["wc"], _row(p["s5_d"][l]), p["s5_w_glu"][l],
            _row(p["s5_b_glu"][l]), dh, name=n + "s5_bwd", rider=rd)
        land_grads(keys, routs)
        (g["s5_lam_re"], g["s5_lam_im"], g["s5_log_dt"], g["s5_b_re"], g["s5_b_im"], g["s5_c_re"],
         g["s5_c_im"]) = q["s5_vjp"]((da, dwb, dwc))
        dh, dkbig, dvbig = _attn_bwd(hin, dmix, sv["kbig"], sv["vbig"], dh, name=n + "attn_bwd")
        (dkv,) = sv["kv_vjp"]((dkbig, dvbig))
        big_grad("attn_w_kv", l, _mm_tn(mem, dkv, name=n + "dw_kv"))

        g["b_in"] = _colsum_call(dh, name=n + "db_in")
        big_grad("w_in", l, _mm_tn(sv["xs"], dh, name=n + "dw_in"))
        dx = _hosted(_mm, grad_rider(l, "dxs"), land_grads, dh, big_w["w_in"][l], trans_b=True, res=dr1,
                     res_scale=ALPHA, name=n + "dxs")
        for k, v in g.items():
            small.setdefault(k, [None] * DEPTH)[l] = v.reshape(p[k].shape[1:])
        if dist:
            gl = {k: small[k][l] for k in g}
            gl["b_in"] = _perm_in_cols(gl["b_in"], inverse=True)
            ready[("rep", l)] = (_pack([gl[k] for k in REP_LAYERED], F32), "all")
            ready[("ssh", l)] = (_pack_lead([_split_shards(gl[k], SHARDED[k] - 1) for k in SMALL_SHARDED], F32), "lead")

    keys, rd = grad_rider(0, "ln_in_bwd")
    if rd is None:
        grad_x, dgi, dbi, _ = _ln_bwd(x, dx, _row(p["ln_in_g"]), name="ln_in_bwd")
    else:
        (grad_x, dgi, dbi, _), routs = _ln_bwd(x, dx, _row(p["ln_in_g"]), name="ln_in_bwd", rider=rd)
        land_grads(keys, routs)
    out = {k: jnp.stack(v, axis=0) for k, v in small.items()}
    out["ln_in_g"], out["ln_in_b"] = dgi.reshape(-1), dbi.reshape(-1)
    return loss, grad_x, out, ((recv, ready) if dist else big_g)


def kernel(x, mem, ln_in_g, ln_in_b, w_in, b_in, s5_lam_re, s5_lam_im, s5_log_dt, s5_b_re, s5_b_im, s5_c_re, s5_c_im, s5_d, s5_w_glu, s5_b_glu, cv_w, cv_b, cv_gn_g, cv_gn_b, cv_w_pw, cv_b_pw, lru_conv_w, lru_conv_b, lru_w_r, lru_b_r, lru_w_i, lru_b_i, lru_lam, attn_w_kv, w_out, b_out, ln1_g, ln1_b, ffn_w_up, ffn_conv_w, ffn_conv_b, ffn_w_down, ln2_g, ln2_b, loss_target, m_ln_in_g, m_ln_in_b, m_w_in, m_b_in, m_s5_lam_re, m_s5_lam_im, m_s5_log_dt, m_s5_b_re, m_s5_b_im, m_s5_c_re, m_s5_c_im, m_s5_d, m_s5_w_glu, m_s5_b_glu, m_cv_w, m_cv_b, m_cv_gn_g, m_cv_gn_b, m_cv_w_pw, m_cv_b_pw, m_lru_conv_w, m_lru_conv_b, m_lru_w_r, m_lru_b_r, m_lru_w_i, m_lru_b_i, m_lru_lam, m_attn_w_kv, m_w_out, m_b_out, m_ln1_g, m_ln1_b, m_ffn_w_up, m_ffn_conv_w, m_ffn_conv_b, m_ffn_w_down, m_ln2_g, m_ln2_b, v_ln_in_g, v_ln_in_b, v_w_in, v_b_in, v_s5_lam_re, v_s5_lam_im, v_s5_log_dt, v_s5_b_re, v_s5_b_im, v_s5_c_re, v_s5_c_im, v_s5_d, v_s5_w_glu, v_s5_b_glu, v_cv_w, v_cv_b, v_cv_gn_g, v_cv_gn_b, v_cv_w_pw, v_cv_b_pw, v_lru_conv_w, v_lru_conv_b, v_lru_w_r, v_lru_b_r, v_lru_w_i, v_lru_b_i, v_lru_lam, v_attn_w_kv, v_w_out, v_b_out, v_ln1_g, v_ln1_b, v_ffn_w_up, v_ffn_conv_w, v_ffn_conv_b, v_ffn_w_down, v_ln2_g, v_ln2_b):
    args = locals()
    w = {n: args[n] for n in WEIGHTS}
    mom = {n: args["m_" + n] for n in WEIGHTS}
    var = {n: args["v_" + n] for n in WEIGHTS}

    shards = {n: w[n].astype(BF16) for n in BIG}
    shards["small_pack"] = [_pack([w[n] for n in SMALL_SHARDED], F32)]
    small_shapes = [w[n].shape for n in SMALL_SHARDED]

    def unpack_small(gathered):
        out = {n: _join_shards(st, SHARDED[n]) for n, st in zip(SMALL_SHARDED, _unpack(gathered, small_shapes, lead=True))}
        for n in ("s5_w_glu", "cv_w_pw"):
            out[n] = out[n].astype(BF16)
        return out

    big_w = {n: [None] * DEPTH for n in BIG}
    p = {n: w[n] for n in REPLICATED}
    p["b_in"] = _perm_in_cols(p["b_in"])

    loss, grad_x, g_small, (recv, ready) = _local_step(x[0], mem[0], loss_target[0], p, big_w, shards, unpack_small)
    loss = lax.psum(loss, ("x", "y", "c"))

    left = list(ready)
    rider = _Rider([ready[k][0] for k in left] + [_pack([g_small["ln_in_g"], g_small["ln_in_b"]], F32)],
                   [ready[k][1] for k in left] + ["all"])
    got = _exchange(rider, name="exchange_grads")
    for k, r in zip(left, got):
        recv[k] = r

    res = [dict(), dict(), dict(), dict()]
    for n in BIG:
        outs = None
        for l in range(DEPTH):
            outs = _adamw_layer(recv[(n, l)], w[n], mom[n], var[n], l, outs, name=f"adamw_{n}_l{l}")
        for kind in range(4):
            res[kind][n] = outs[kind]
    per_layer = {}
    for names, key, tag in ((SMALL_SHARDED, "ssh", "adamw_small_sharded"), (REP_LAYERED, "rep", "adamw_replicated")):
        for l in range(DEPTH):
            outs = _adamw(recv[(key, l)], _pack([w[n][l] for n in names], F32), _pack([mom[n][l] for n in names], F32),
                          _pack([var[n][l] for n in names], F32), name=f"{tag}_l{l}")
            for kind in range(4):
                for n, a in zip(names, _unpack(outs[kind], [w[n].shape[1:] for n in names])):
                    per_layer.setdefault((kind, n), []).append(a)
    for (kind, n), parts in per_layer.items():
        res[kind][n] = jnp.stack(parts, axis=0)
    ln_names = ("ln_in_g", "ln_in_b")
    outs = _adamw(got[len(left)], _pack([w[n] for n in ln_names], F32), _pack([mom[n] for n in ln_names], F32),
                  _pack([var[n] for n in ln_names], F32), name="adamw_ln_in")
    for kind in range(4):
        for n, a in zip(ln_names, _unpack(outs[kind], [w[n].shape for n in ln_names])):
            res[kind][n] = a
    return (loss, grad_x[None], *[res[0][n] for n in WEIGHTS], *[res[1][n] for n in WEIGHTS],
            *[res[2][n] for n in WEIGHTS], *[res[3][n] for n in WEIGHTS])
```

```python
import math

import jax
import jax.numpy as jnp
from jax import lax
from jax.experimental import pallas as pl
from jax.experimental.pallas import tpu as pltpu

F32 = jnp.float32
BF16 = jnp.bfloat16

D_MODEL = 1024
DEPTH = 2
D_GROUP = 256
N_IN_COLS = 6 * D_GROUP
S5_GROUPS = 16
S5_CH = 16
S5_STATE = 64
S5_LANES = S5_GROUPS * S5_STATE
CONV_WIDTH = 31
GN_GROUPS = 4
LRU_HEADS = 4
LRU_CONV_WIDTH = 4
LRU_C = 8.0
ATTN_HEADS = 4
ATTN_HEAD_DIM = 64
D_FF = 2816
FFN_CONV_WIDTH = 3
ALPHA = (2 * DEPTH) ** 0.25
LN_EPS = 1e-5
ADAM_LR, ADAM_B1, ADAM_B2, ADAM_EPS, ADAM_WD, ADAM_STEP = 0.001, 0.9, 0.999, 1e-08, 0.01, 10

N_DEV = 8
N_PEERS = N_DEV - 1
LANE = 128
SUBLANE = 8
VMEM_LIMIT = 56 * 1024 * 1024
PACK_COLS = 1024
PACK_ROW_BLOCK = 256

SHARDED = {
    "w_in": 2, "s5_w_glu": 1, "cv_w": 2, "cv_w_pw": 1, "lru_conv_w": 2, "attn_w_kv": 1,
    "w_out": 1, "ffn_w_up": 2, "ffn_conv_w": 2, "ffn_w_down": 1,
}
BIG = ("w_in", "attn_w_kv", "w_out", "ffn_w_up", "ffn_w_down")
SMALL_SHARDED = ("s5_w_glu", "cv_w", "cv_w_pw", "lru_conv_w", "ffn_conv_w")
MATMUL_WEIGHTS = ("w_in", "s5_w_glu", "cv_w_pw", "attn_w_kv", "w_out", "ffn_w_up", "ffn_w_down")
WEIGHTS = ['ln_in_g', 'ln_in_b', 'w_in', 'b_in', 's5_lam_re', 's5_lam_im', 's5_log_dt', 's5_b_re', 's5_b_im',
           's5_c_re', 's5_c_im', 's5_d', 's5_w_glu', 's5_b_glu', 'cv_w', 'cv_b', 'cv_gn_g', 'cv_gn_b', 'cv_w_pw',
           'cv_b_pw', 'lru_conv_w', 'lru_conv_b', 'lru_w_r', 'lru_b_r', 'lru_w_i', 'lru_b_i', 'lru_lam',
           'attn_w_kv', 'w_out', 'b_out', 'ln1_g', 'ln1_b', 'ffn_w_up', 'ffn_conv_w', 'ffn_conv_b', 'ffn_w_down',
           'ln2_g', 'ln2_b']
REPLICATED = [n for n in WEIGHTS if n not in SHARDED]
REP_LAYERED = [n for n in REPLICATED if n not in ("ln_in_g", "ln_in_b")]

COL_CV_V, COL_CV_G, COL_LRU_G, COL_LRU_X, COL_S5, COL_Q = range(6)
IN_PERM = (1, 2, 3, 4, 0, 5)
MIX_S5, MIX_CV, MIX_LRU, MIX_ATTN = range(4)


_ANY = pl.BlockSpec(memory_space=pl.ANY)
_MESH = pl.DeviceIdType.MESH


def _cparams(n_axes):
    return pltpu.CompilerParams(dimension_semantics=("arbitrary",) * n_axes, vmem_limit_bytes=VMEM_LIMIT)


def _pick(n, cap):
    if n <= cap:
        return n
    best = None
    for t in range(LANE, cap + 1, LANE):
        if n % t == 0:
            best = t
    assert best is not None, (n, cap)
    return best


def _pick_rows(n, cap):
    best = None
    for t in range(SUBLANE, min(n, cap) + 1, SUBLANE):
        if n % t == 0:
            best = t
    assert best is not None, (n, cap)
    return best


def _full_spec(arr):
    nd = arr.ndim
    return pl.BlockSpec(arr.shape, lambda *_: (0,) * nd)


def _dot(a, b):
    return lax.dot_general(a.astype(BF16), b.astype(BF16), (((1,), (0,)), ((), ())), preferred_element_type=F32)


def _dot_nt(a, b):
    return lax.dot_general(a.astype(BF16), b.astype(BF16), (((1,), (1,)), ((), ())), preferred_element_type=F32)


def _dot_tn(a, b):
    return lax.dot_general(a.astype(BF16), b.astype(BF16), (((0,), (0,)), ((), ())), preferred_element_type=F32)


def _dot_hi(a, b):
    return jnp.dot(a, b, precision=lax.Precision.HIGHEST, preferred_element_type=F32)


def _colsum(x):
    return jnp.sum(x, axis=0, keepdims=True)


def _sigmoid(x):
    return 1.0 / (1.0 + jnp.exp(-x))


_GELU_K = math.sqrt(2.0 / math.pi)
_GELU_C = 0.044715


def _gelu(x):
    t = jnp.tanh(_GELU_K * (x + _GELU_C * x * x * x))
    return 0.5 * x * (1.0 + t)


def _gelu_and_grad(x):
    x2 = x * x
    t = jnp.tanh(_GELU_K * (x + _GELU_C * x2 * x))
    g = 0.5 * x * (1.0 + t)
    dg = 0.5 * (1.0 + t) + 0.5 * x * (1.0 - t * t) * (_GELU_K * (1.0 + 3.0 * _GELU_C * x2))
    return g, dg


def _neg_expm1(x):
    series = x * (1.0 + x * (0.5 + x * (1.0 / 6.0 + x * (1.0 / 24.0 + x * (1.0 / 120.0)))))
    return -jnp.where(jnp.abs(x) < 0.1, series, jnp.exp(x) - 1.0)


def _seq_tile(s, want):
    t = min(s, want)
    assert s % t == 0
    return t


class _Rider:
    def __init__(self, srcs, kinds):
        self.srcs, self.kinds = list(srcs), list(kinds)
        self.n = len(self.srcs)

    def out_shapes(self):
        shapes = []
        for x, kind in zip(self.srcs, self.kinds):
            if kind == "lead":
                shp = x.shape
            elif kind == "rows":
                shp = (N_DEV, x.shape[0] // N_DEV) + x.shape[1:]
            else:
                shp = (N_DEV,) + x.shape
            shapes.append(jax.ShapeDtypeStruct(shp, x.dtype))
        return shapes

    def scratch(self):
        return [pltpu.SemaphoreType.DMA((self.n * N_PEERS,)), pltpu.SemaphoreType.DMA((self.n * N_PEERS,)),
                pltpu.SemaphoreType.DMA((self.n,))]

    def _copies(self, x_refs, out_refs, sems):
        send_sems, recv_sems, local_sems = sems
        mx, my, mc = lax.axis_index("x"), lax.axis_index("y"), lax.axis_index("c")
        my_id = 4 * mx + 2 * my + mc

        def piece(i, dev):
            if self.kinds[i] == "lead":
                return x_refs[i].at[dev]
            if self.kinds[i] == "rows":
                r = x_refs[i].shape[0] // N_DEV
                return x_refs[i].at[pl.ds(pl.multiple_of(dev * r, SUBLANE), r)]
            return x_refs[i]

        mine = [pltpu.make_async_copy(piece(i, my_id), out_refs[i].at[my_id], local_sems.at[i]) for i in range(self.n)]
        copies = []
        for k in range(1, N_DEV):
            px, py, pc = mx ^ ((k >> 2) & 1), my ^ ((k >> 1) & 1), mc ^ (k & 1)
            for i in range(self.n):
                copies.append(pltpu.make_async_remote_copy(
                    src_ref=piece(i, 4 * px + 2 * py + pc), dst_ref=out_refs[i].at[my_id],
                    send_sem=send_sems.at[i * N_PEERS + k - 1], recv_sem=recv_sems.at[i * N_PEERS + k - 1],
                    device_id=(px, py, pc), device_id_type=_MESH))
        return mine, copies

    def start(self, x_refs, out_refs, sems):
        mine, copies = self._copies(x_refs, out_refs, sems)
        for cp in mine + copies:
            cp.start()

    def wait(self, x_refs, out_refs, sems):
        mine, copies = self._copies(x_refs, out_refs, sems)
        for cp in copies:
            cp.wait_recv()
        for cp in copies:
            cp.wait_send()
        for cp in mine:
            cp.wait()


def _call(body, *, grid, ins, in_specs, outs, out_specs, scratch=(), aliases=None, name, rider=None):
    n_axes = len(grid)
    common = dict(grid=grid, input_output_aliases=aliases or {}, compiler_params=_cparams(n_axes), name=name)
    if rider is None:
        res = pl.pallas_call(body, in_specs=list(in_specs), out_specs=list(out_specs), out_shape=list(outs),
                             scratch_shapes=list(scratch), **common)(*ins)
        return list(res), []
    n_in, n_out, n_scr, nr = len(ins), len(outs), len(scratch), rider.n

    def wrapped(*refs):
        pos = [0]

        def take(k):
            part = refs[pos[0]:pos[0] + k]
            pos[0] += k
            return part

        a_in, r_in, a_out, r_out, a_scr, sems = take(n_in), take(nr), take(n_out), take(nr), take(n_scr), take(3)
        first = last = None
        for ax in range(n_axes):
            pid = pl.program_id(ax)
            f, l = pid == 0, pid == grid[ax] - 1
            first = f if first is None else jnp.logical_and(first, f)
            last = l if last is None else jnp.logical_and(last, l)

        @pl.when(first)
        def _():
            rider.start(r_in, r_out, sems)

        body(*a_in, *a_out, *a_scr)

        @pl.when(last)
        def _():
            rider.wait(r_in, r_out, sems)

    res = pl.pallas_call(
        wrapped, in_specs=list(in_specs) + [_ANY] * nr, out_specs=list(out_specs) + [_ANY] * nr,
        out_shape=list(outs) + rider.out_shapes(), scratch_shapes=list(scratch) + rider.scratch(), **common)(*ins, *rider.srcs)
    return list(res[:n_out]), list(res[n_out:])


def _block_mask(n_blocks, block_rows, block_cols):
    r = jnp.arange(n_blocks * block_rows) // block_rows
    c = jnp.arange(n_blocks * block_cols) // block_cols
    return (r[:, None] == c[None, :]).astype(F32)


def _mm(a, b, *, bias=None, res=None, res_scale=1.0, trans_b=False, out_dtype=F32, name, rider=None):
    m, kdim = a.shape
    n = b.shape[0] if trans_b else b.shape[1]
    tm = _seq_tile(m, 1024)
    tn = _pick(n, 1408)
    tk = _pick(kdim, 1536)
    nk = kdim // tk
    has_bias, has_res = bias is not None, res is not None

    def body(*refs):
        a_ref, b_ref = refs[0], refs[1]
        pos = 2
        bias_ref = res_ref = None
        if has_bias:
            bias_ref = refs[pos]
            pos += 1
        if has_res:
            res_ref = refs[pos]
            pos += 1
        o_ref, acc_ref = refs[pos], refs[pos + 1]
        k = pl.program_id(2)

        @pl.when(k == 0)
        def _():
            acc_ref[...] = jnp.zeros_like(acc_ref)

        if trans_b:
            acc_ref[...] += _dot_nt(a_ref[...], b_ref[...])
        else:
            acc_ref[...] += _dot(a_ref[...], b_ref[...])

        @pl.when(k == nk - 1)
        def _():
            r = acc_ref[...]
            if has_bias:
                r = r + bias_ref[...]
            if has_res:
                r = r + res_scale * res_ref[...]
            o_ref[...] = r.astype(out_dtype)

    ins = [a, b]
    in_specs = [pl.BlockSpec((tm, tk), lambda i, j, k: (i, k)),
                pl.BlockSpec((tn, tk), lambda i, j, k: (j, k)) if trans_b
                else pl.BlockSpec((tk, tn), lambda i, j, k: (k, j))]
    if has_bias:
        ins.append(bias)
        in_specs.append(pl.BlockSpec((1, tn), lambda i, j, k: (0, j)))
    if has_res:
        ins.append(res)
        in_specs.append(pl.BlockSpec((tm, tn), lambda i, j, k: (i, j)))
    (out,), routs = _call(
        body, grid=(m // tm, n // tn, nk), ins=ins, in_specs=in_specs,
        outs=[jax.ShapeDtypeStruct((m, n), out_dtype)], out_specs=[pl.BlockSpec((tm, tn), lambda i, j, k: (i, j))],
        scratch=[pltpu.VMEM((tm, tn), F32)], name=name, rider=rider)
    return out if rider is None else (out, routs)


def _mm_tn(a, b, *, name, rider=None):
    s, ka = a.shape
    nb = b.shape[1]
    ts = _seq_tile(s, 512)
    tka = _pick(ka, 1408)
    tnb = _pick(nb, 1408)

    def body(a_ref, b_ref, o_ref):
        @pl.when(pl.program_id(2) == 0)
        def _():
            o_ref[...] = jnp.zeros_like(o_ref)

        o_ref[...] += _dot_tn(a_ref[...], b_ref[...])

    (out,), routs = _call(
        body, grid=(ka // tka, nb // tnb, s // ts), ins=[a, b],
        in_specs=[pl.BlockSpec((ts, tka), lambda i, j, k: (k, i)), pl.BlockSpec((ts, tnb), lambda i, j, k: (k, j))],
        outs=[jax.ShapeDtypeStruct((ka, nb), F32)], out_specs=[pl.BlockSpec((tka, tnb), lambda i, j, k: (i, j))],
        name=name, rider=rider)
    return out if rider is None else (out, routs)


def _colsum_call(x, *, name):
    s, n = x.shape
    ts = _seq_tile(s, 512)

    def body(x_ref, o_ref):
        @pl.when(pl.program_id(0) == 0)
        def _():
            o_ref[...] = jnp.zeros_like(o_ref)

        o_ref[...] += _colsum(x_ref[...])

    return pl.pallas_call(
        body, grid=(s // ts,), in_specs=[pl.BlockSpec((ts, n), lambda i: (i, 0))],
        out_specs=pl.BlockSpec((1, n), lambda i: (0, 0)), out_shape=jax.ShapeDtypeStruct((1, n), F32),
        compiler_params=_cparams(1), name=name)(x)


def _ln_fwd(r, g, b, *, name, rider=None):
    s, d = r.shape
    ts = _seq_tile(s, 512)

    def body(r_ref, g_ref, b_ref, o_ref):
        x = r_ref[...]
        mu = jnp.mean(x, axis=1, keepdims=True)
        xc = x - mu
        var = jnp.mean(xc * xc, axis=1, keepdims=True)
        o_ref[...] = xc * lax.rsqrt(var + LN_EPS) * g_ref[...] + b_ref[...]

    (out,), routs = _call(
        body, grid=(s // ts,), ins=[r, g, b],
        in_specs=[pl.BlockSpec((ts, d), lambda i: (i, 0)), _full_spec(g), _full_spec(b)],
        out_specs=[pl.BlockSpec((ts, d), lambda i: (i, 0))], outs=[jax.ShapeDtypeStruct((s, d), F32)],
        name=name, rider=rider)
    return out if rider is None else (out, routs)


def _ln_bwd(r, dy, g, *, name, rider=None):
    s, d = r.shape
    ts = _seq_tile(s, 512)

    def body(r_ref, dy_ref, g_ref, dr_ref, dg_ref, db_ref, ds_ref):
        @pl.when(pl.program_id(0) == 0)
        def _():
            dg_ref[...] = jnp.zeros_like(dg_ref)
            db_ref[...] = jnp.zeros_like(db_ref)
            ds_ref[...] = jnp.zeros_like(ds_ref)

        x = r_ref[...]
        dy = dy_ref[...]
        mu = jnp.mean(x, axis=1, keepdims=True)
        xc = x - mu
        var = jnp.mean(xc * xc, axis=1, keepdims=True)
        rstd = lax.rsqrt(var + LN_EPS)
        xh = xc * rstd
        dxh = dy * g_ref[...]
        m1 = jnp.mean(dxh, axis=1, keepdims=True)
        m2 = jnp.mean(dxh * xh, axis=1, keepdims=True)
        dr = rstd * (dxh - m1 - xh * m2)
        dr_ref[...] = dr
        dg_ref[...] += _colsum(dy * xh)
        db_ref[...] += _colsum(dy)
        ds_ref[...] += _colsum(dr)

    vec = jax.ShapeDtypeStruct((1, d), F32)
    vspec = pl.BlockSpec((1, d), lambda i: (0, 0))
    outs, routs = _call(
        body, grid=(s // ts,), ins=[r, dy, g],
        in_specs=[pl.BlockSpec((ts, d), lambda i: (i, 0)), pl.BlockSpec((ts, d), lambda i: (i, 0)), _full_spec(g)],
        out_specs=[pl.BlockSpec((ts, d), lambda i: (i, 0)), vspec, vspec, vspec],
        outs=[jax.ShapeDtypeStruct((s, d), F32), vec, vec, vec], name=name, rider=rider)
    return outs if rider is None else (outs, routs)


def _loss_grad(y, target, *, name):
    s, d = y.shape
    ts = _seq_tile(s, 512)

    def body(y_ref, t_ref, dy_ref, l_ref):
        @pl.when(pl.program_id(0) == 0)
        def _():
            l_ref[...] = jnp.zeros_like(l_ref)

        e = y_ref[...] - t_ref[...]
        dy_ref[...] = e * (1.0 / d)
        part = jnp.sum(jnp.sum(e * e, axis=1, keepdims=True), axis=0, keepdims=True) * (0.5 / d)
        l_ref[...] += jnp.broadcast_to(part, l_ref.shape)

    return pl.pallas_call(
        body, grid=(s // ts,),
        in_specs=[pl.BlockSpec((ts, d), lambda i: (i, 0)), pl.BlockSpec((ts, d), lambda i: (i, 0))],
        out_specs=[pl.BlockSpec((ts, d), lambda i: (i, 0)), pl.BlockSpec((SUBLANE, LANE), lambda i: (0, 0))],
        out_shape=[jax.ShapeDtypeStruct((s, d), F32), jax.ShapeDtypeStruct((SUBLANE, LANE), F32)],
        compiler_params=_cparams(1), name=name)(y, target)


SCAN_CHUNK = 32


def _cscan_levels(bufs, apow_ref, t, pad, *, reverse):
    half = bufs[0].shape[1] // 2
    ch = min(SCAN_CHUNK, t)
    nlev = t.bit_length() - 1
    assert (1 << nlev) == t
    for k in range(nlev):
        d = 1 << k
        src, dst = bufs[k % 2], bufs[(k + 1) % 2]

        def chunk(c, carry, src=src, dst=dst, d=d, k=k):
            ar = apow_ref[k:k + 1, :half]
            ai = apow_ref[k:k + 1, half:]
            if reverse:
                ai = -ai
            r0 = pl.multiple_of(c * ch, ch)
            cur = src[pl.ds(pad + r0, ch), :]
            if d >= SUBLANE:
                off = pad + d if reverse else pad - d
                sh = src[pl.ds(off + r0, ch), :]
            elif reverse:
                blk = src[pl.ds(pad + r0, ch + SUBLANE), :]
                sh = pltpu.roll(blk, ch + SUBLANE - d, axis=0)[:ch, :]
            else:
                blk = src[pl.ds(pad - SUBLANE + r0, ch + SUBLANE), :]
                sh = pltpu.roll(blk, d, axis=0)[SUBLANE:, :]
            sre, sim = sh[:, :half], sh[:, half:]
            dst[pl.ds(pad + r0, ch), :half] = cur[:, :half] + ar * sre - ai * sim
            dst[pl.ds(pad + r0, ch), half:] = cur[:, half:] + ar * sim + ai * sre
            return carry

        lax.fori_loop(0, t // ch, chunk, 0)
    return nlev % 2


def _rscan_levels(abufs, bbufs, t, pad, *, reverse):
    nlev = t.bit_length() - 1
    assert (1 << nlev) == t
    for k in range(nlev):
        d = 1 << k
        asrc, adst = abufs[k % 2], abufs[(k + 1) % 2]
        bsrc, bdst = bbufs[k % 2], bbufs[(k + 1) % 2]
        off = pad + d if reverse else pad - d
        a = asrc[pad:pad + t, :]
        bdst[pad:pad + t, :] = a * bsrc[off:off + t, :] + bsrc[pad:pad + t, :]
        if k < nlev - 1:
            adst[pad:pad + t, :] = a * asrc[off:off + t, :]
    return nlev % 2


S5_TILE = 256


def _s5_scan_forward(u_bf, wb_ref, apow_ref, state, bufs, t, pad):
    half = S5_LANES
    bufs[0][pad:pad + t, :] = _dot(u_bf, wb_ref[...])
    ar, ai = apow_ref[0:1, :half], apow_ref[0:1, half:]
    sr, si = state[:, :half], state[:, half:]
    bufs[0][pad:pad + 1, :half] += ar * sr - ai * si
    bufs[0][pad:pad + 1, half:] += ar * si + ai * sr
    return _cscan_levels(bufs, apow_ref, t, pad, reverse=False)


def _s5_fwd(h_in, wb, apow, wc, dvec, wglu, bglu, *, name, rider=None):
    s = h_in.shape[0]
    t = _seq_tile(s, S5_TILE)
    pad = t // 2
    nt = s // t
    lanes2 = 2 * S5_LANES

    def body(u_ref, wb_ref, apow_ref, wc_ref, d_ref, wglu_ref, bglu_ref, out_ref, y1_ref, hb_ref, buf0, buf1, carry):
        bufs = (buf0, buf1)

        @pl.when(pl.program_id(0) == 0)
        def _():
            buf0[0:pad, :] = jnp.zeros((pad, lanes2), F32)
            buf1[0:pad, :] = jnp.zeros((pad, lanes2), F32)
            carry[...] = jnp.zeros_like(carry)

        u = u_ref[...]
        state = carry[0:1, :]
        hb_ref[0] = state
        fin = _s5_scan_forward(u.astype(BF16), wb_ref, apow_ref, state, bufs, t, pad)
        hbuf = bufs[fin]
        carry[0:1, :] = hbuf[pad + t - 1:pad + t, :]
        y1 = _dot(hbuf[pad:pad + t, :], wc_ref[...]) + d_ref[...] * u
        y1_ref[...] = y1
        y2 = _gelu(y1)
        z = _dot(y2, wglu_ref[...]) + bglu_ref[...]
        out_ref[...] = (y2 * _sigmoid(z)).astype(BF16)

    ins = [h_in, wb, apow, wc, dvec, wglu, bglu]
    in_specs = [pl.BlockSpec((t, D_GROUP), lambda i: (i, COL_S5))] + [_full_spec(a) for a in ins[1:]]
    return _call(
        body, grid=(nt,), ins=ins, in_specs=in_specs,
        out_specs=[pl.BlockSpec((t, D_GROUP), lambda i: (i, MIX_S5)), pl.BlockSpec((t, D_GROUP), lambda i: (i, 0)),
                   pl.BlockSpec((1, 1, lanes2), lambda i: (i, 0, 0))],
        outs=[jax.ShapeDtypeStruct((s, D_MODEL), BF16), jax.ShapeDtypeStruct((s, D_GROUP), F32),
              jax.ShapeDtypeStruct((nt, 1, lanes2), F32)],
        scratch=[pltpu.VMEM((pad + t, lanes2), F32), pltpu.VMEM((pad + t, lanes2), F32),
                 pltpu.VMEM((SUBLANE, lanes2), F32)],
        name=name, rider=rider)


def _s5_bwd(h_in, y1, dmix, hb, wb, apow, wc, dvec, wglu, bglu, dh_all, *, name, rider=None):
    s = h_in.shape[0]
    t = _seq_tile(s, S5_TILE)
    pad = t // 2
    nt = s // t
    half = S5_LANES
    lanes2 = 2 * half
    rows = pad + t + pad

    def body(u_ref, y1_ref, do_ref, hb_ref, wb_ref, apow_ref, wc_ref, d_ref, wglu_ref, bglu_ref, _dh_in,
             du_ref, dwglu_ref, dwc_ref, dwb_ref, dbglu_ref, dd_ref, da_ref, buf0, buf1, buf2, buf3, carry):
        @pl.when(pl.program_id(0) == 0)
        def _():
            for bf in (buf0, buf1, buf2, buf3):
                bf[0:pad, :] = jnp.zeros((pad, lanes2), F32)
                bf[pad + t:rows, :] = jnp.zeros((pad, lanes2), F32)
            carry[...] = jnp.zeros_like(carry)
            for r in (dwglu_ref, dwc_ref, dwb_ref, dbglu_ref, dd_ref, da_ref):
                r[...] = jnp.zeros_like(r)

        u = u_ref[...]
        u_bf = u.astype(BF16)
        state = hb_ref[0]
        hfin = _s5_scan_forward(u_bf, wb_ref, apow_ref, state, (buf0, buf1), t, pad)
        hbuf = (buf0, buf1)[hfin]
        h_bf = hbuf[pad:pad + t, :].astype(BF16)

        y1 = y1_ref[...]
        dout = do_ref[...]
        y2, dgelu = _gelu_and_grad(y1)
        sg = _sigmoid(_dot(y2, wglu_ref[...]) + bglu_ref[...])
        dz = dout * y2 * sg * (1.0 - sg)
        dy2 = dout * sg + _dot_nt(dz, wglu_ref[...])
        dwglu_ref[...] += _dot_tn(y2, dz)
        dbglu_ref[...] += _colsum(dz)
        dy1 = dy2 * dgelu
        dd_ref[...] += _colsum(dy1 * u)
        dy1_bf = dy1.astype(BF16)
        dwc_ref[...] += _dot_tn(h_bf, dy1_bf)

        buf2[pad:pad + t, :] = _dot_nt(dy1_bf, wc_ref[...])
        ar, ai = apow_ref[0:1, :half], apow_ref[0:1, half:]
        cr, ci = carry[0:1, :half], carry[0:1, half:]
        buf2[pad + t - 1:pad + t, :half] += ar * cr + ai * ci
        buf2[pad + t - 1:pad + t, half:] += ar * ci - ai * cr
        lfin = _cscan_levels((buf2, buf3), apow_ref, t, pad, reverse=True)
        lbuf = (buf2, buf3)[lfin]
        lam = lbuf[pad:pad + t, :]
        carry[0:1, :] = lbuf[pad:pad + 1, :]
        lam_bf = lam.astype(BF16)
        du_ref[...] = dy1 * d_ref[...] + _dot_nt(lam_bf, wb_ref[...])
        dwb_ref[...] += _dot_tn(u_bf, lam_bf)

        hbuf[pad - 1:pad, :] = state
        hp = hbuf[pad - 1:pad - 1 + t, :]
        hbuf[pad - 1:pad, :] = jnp.zeros((1, lanes2), F32)
        lre, lim = lam[:, :half], lam[:, half:]
        hre, him = hp[:, :half], hp[:, half:]
        da_ref[:, :half] += _colsum(lre * hre + lim * him)
        da_ref[:, half:] += _colsum(lim * hre - lre * him)

    def rev(col):
        return lambda i: (nt - 1 - i, col)

    ins = [h_in, y1, dmix, hb, wb, apow, wc, dvec, wglu, bglu, dh_all]
    in_specs = [pl.BlockSpec((t, D_GROUP), rev(COL_S5)), pl.BlockSpec((t, D_GROUP), rev(0)),
                pl.BlockSpec((t, D_GROUP), rev(MIX_S5)), pl.BlockSpec((1, 1, lanes2), lambda i: (nt - 1 - i, 0, 0))] + \
               [_full_spec(a) for a in ins[4:10]] + [_ANY]
    outs = [jax.ShapeDtypeStruct((s, N_IN_COLS), F32), jax.ShapeDtypeStruct((D_GROUP, D_GROUP), F32),
            jax.ShapeDtypeStruct((lanes2, D_GROUP), F32), jax.ShapeDtypeStruct((D_GROUP, lanes2), F32),
            jax.ShapeDtypeStruct((1, D_GROUP), F32), jax.ShapeDtypeStruct((1, D_GROUP), F32),
            jax.ShapeDtypeStruct((1, lanes2), F32)]
    out_specs = [pl.BlockSpec((t, D_GROUP), rev(COL_S5))] + [_full_spec(o) for o in outs[1:]]
    return _call(
        body, grid=(nt,), ins=ins, in_specs=in_specs, out_specs=out_specs, outs=outs, aliases={10: 0},
        scratch=[pltpu.VMEM((rows, lanes2), F32) for _ in range(4)] + [pltpu.VMEM((SUBLANE, lanes2), F32)],
        name=name, rider=rider)


def _s5_param_map(lam_re, lam_im, log_dt, b_re, b_im, c_re, c_im):
    dt = jnp.exp(log_dt)[:, None]
    er = jnp.exp(lam_re * dt)
    a_re, a_im = er * jnp.cos(lam_im * dt), er * jnp.sin(lam_im * dt)
    den = lam_re * lam_re + lam_im * lam_im
    n_re = a_re - 1.0
    k_re = (n_re * lam_re + a_im * lam_im) / den
    k_im = (a_im * lam_re - n_re * lam_im) / den
    bb_re = k_re[..., None] * b_re - k_im[..., None] * b_im
    bb_im = k_re[..., None] * b_im + k_im[..., None] * b_re
    mask_in = _block_mask(S5_GROUPS, S5_CH, S5_STATE)
    mask_out = _block_mask(S5_GROUPS, S5_STATE, S5_CH)

    def blockdiag_in(m):
        return jnp.tile(jnp.transpose(m, (0, 2, 1)).reshape(S5_GROUPS * S5_CH, S5_STATE), (1, S5_GROUPS)) * mask_in

    def blockdiag_out(m):
        return jnp.tile(jnp.transpose(m, (0, 2, 1)).reshape(S5_LANES, S5_CH), (1, S5_GROUPS)) * mask_out

    a = jnp.concatenate([a_re.reshape(1, -1), a_im.reshape(1, -1)], axis=1)
    wb = jnp.concatenate([blockdiag_in(bb_re), blockdiag_in(bb_im)], axis=1)
    wc = jnp.concatenate([blockdiag_out(c_re), -blockdiag_out(c_im)], axis=0)
    return a, wb, wc


def _s5_apow(a, nlev):
    half = S5_LANES
    re, im = a[:, :half], a[:, half:]
    rows = []
    for _ in range(nlev):
        rows.append(jnp.concatenate([re, im], axis=1))
        re, im = re * re - im * im, 2.0 * re * im
    n_rows = -(-nlev // SUBLANE) * SUBLANE
    rows += [jnp.zeros_like(rows[0])] * (n_rows - nlev)
    return lax.stop_gradient(jnp.concatenate(rows, axis=0))


CV_TILE = 256
CV_PAD = 32
CV_CHUNK = 64


def _shifted_copies(buf, shifted, rows):
    n = rows - SUBLANE
    for s in range(1, SUBLANE):
        shifted[s - 1, 0:n, :] = buf[s:s + n, :]


def _window(buf, shifted, o, ch):
    q, s = divmod(o, SUBLANE)
    if s == 0:
        return buf[o:o + ch, :]
    return shifted[s - 1, q * SUBLANE:q * SUBLANE + ch, :]


def _gn_stats(c, mavg):
    mu = _dot_hi(c, mavg)
    cen = c - mu
    var = _dot_hi(cen * cen, mavg)
    rstd = lax.rsqrt(var + LN_EPS)
    return cen * rstd, rstd


def _cv_fwd(h_in, cw, cb, gng, gnb, mavg, wpw, bpw, mix, *, name, rider=None):
    s = h_in.shape[0]
    t = _seq_tile(s, CV_TILE)
    ch = min(CV_CHUNK, t)

    def body(v_ref, g_ref, cw_ref, cb_ref, gng_ref, gnb_ref, mavg_ref, wpw_ref, bpw_ref, _mix_in, out_ref, c_ref, xpad,
             shifted):
        @pl.when(pl.program_id(0) == 0)
        def _():
            xpad[0:CV_PAD, :] = jnp.zeros((CV_PAD, D_GROUP), F32)

        xpad[CV_PAD:CV_PAD + t, :] = v_ref[...] * _sigmoid(g_ref[...])
        _shifted_copies(xpad, shifted, t + CV_PAD)
        for r0 in range(0, t, ch):
            acc = jnp.broadcast_to(cb_ref[...], (ch, D_GROUP))
            for k in range(CONV_WIDTH):
                o = CV_PAD - (CONV_WIDTH - 1) + k + r0
                acc = acc + cw_ref[k:k + 1, :] * _window(xpad, shifted, o, ch)
            c_ref[r0:r0 + ch, :] = acc
        xpad[0:CV_PAD, :] = xpad[t:t + CV_PAD, :]
        xn, _ = _gn_stats(c_ref[...], mavg_ref[...])
        gn = xn * gng_ref[...] + gnb_ref[...]
        out_ref[...] = (_dot(gn * _sigmoid(gn), wpw_ref[...]) + bpw_ref[...]).astype(BF16)

    ins = [h_in, h_in, cw, cb, gng, gnb, mavg, wpw, bpw, mix]
    in_specs = [pl.BlockSpec((t, D_GROUP), lambda i: (i, COL_CV_V)), pl.BlockSpec((t, D_GROUP), lambda i: (i, COL_CV_G))] + \
               [_full_spec(a) for a in ins[2:9]] + [_ANY]
    return _call(
        body, grid=(s // t,), ins=ins, in_specs=in_specs,
        out_specs=[pl.BlockSpec((t, D_GROUP), lambda i: (i, MIX_CV)), pl.BlockSpec((t, D_GROUP), lambda i: (i, 0))],
        outs=[jax.ShapeDtypeStruct((s, D_MODEL), BF16), jax.ShapeDtypeStruct((s, D_GROUP), F32)],
        aliases={9: 0},
        scratch=[pltpu.VMEM((CV_PAD + t, D_GROUP), F32), pltpu.VMEM((SUBLANE - 1, CV_PAD + t, D_GROUP), F32)],
        name=name, rider=rider)


def _cv_bwd(h_in, c, dmix, cw, gng, gnb, mavg, wpw, *, name, rider=None):
    s = h_in.shape[0]
    t = _seq_tile(s, CV_TILE)
    nt = s // t
    ch = min(CV_CHUNK, t)

    def body(v_ref, g_ref, c_ref, do_ref, cw_ref, gng_ref, gnb_ref, mavg_ref, wpw_ref,
             dvg_ref, dwpw_ref, dcw_ref, dbpw_ref, dgg_ref, dgb_ref, dcb_ref, dcpad, hgbuf, shifted):
        @pl.when(pl.program_id(0) == 0)
        def _():
            dcpad[t:t + CV_PAD, :] = jnp.zeros((CV_PAD, D_GROUP), F32)
            for r in (dwpw_ref, dcw_ref, dbpw_ref, dgg_ref, dgb_ref, dcb_ref):
                r[...] = jnp.zeros_like(r)

        mavg = mavg_ref[...]
        xn, rstd = _gn_stats(c_ref[...], mavg)
        gg = gng_ref[...]
        gn = xn * gg + gnb_ref[...]
        sg = _sigmoid(gn)
        dout = do_ref[...]
        dwpw_ref[...] += _dot_tn(gn * sg, dout)
        dbpw_ref[...] += _colsum(dout)
        dgn = _dot_nt(dout, wpw_ref[...]) * (sg * (1.0 + gn * (1.0 - sg)))
        dgg_ref[...] += _colsum(dgn * xn)
        dgb_ref[...] += _colsum(dgn)
        dxn = dgn * gg
        dc = rstd * (dxn - _dot_hi(dxn, mavg) - xn * _dot_hi(dxn * xn, mavg))
        dcb_ref[...] += _colsum(dc)
        dcpad[0:t, :] = dc

        v = v_ref[...]
        sgm = _sigmoid(g_ref[...])
        hgbuf[...] = v * sgm
        _shifted_copies(dcpad, shifted, t + CV_PAD)
        for r0 in range(0, t, ch):
            hg = hgbuf[r0:r0 + ch, :]
            acc = jnp.zeros((ch, D_GROUP), F32)
            for k in range(CONV_WIDTH):
                o = (CONV_WIDTH - 1) - k + r0
                sh = _window(dcpad, shifted, o, ch)
                acc = acc + cw_ref[k:k + 1, :] * sh
                dcw_ref[k:k + 1, :] += _colsum(hg * sh)
            hgbuf[r0:r0 + ch, :] = acc
        dcpad[t:t + CV_PAD, :] = dcpad[0:CV_PAD, :]
        dhg = hgbuf[...]
        dvg_ref[:, :D_GROUP] = dhg * sgm
        dvg_ref[:, D_GROUP:] = dhg * v * sgm * (1.0 - sgm)

    def rev(col):
        return lambda i: (nt - 1 - i, col)

    ins = [h_in, h_in, c, dmix, cw, gng, gnb, mavg, wpw]
    in_specs = [pl.BlockSpec((t, D_GROUP), rev(COL_CV_V)), pl.BlockSpec((t, D_GROUP), rev(COL_CV_G)),
                pl.BlockSpec((t, D_GROUP), rev(0)), pl.BlockSpec((t, D_GROUP), rev(MIX_CV))] + [_full_spec(a) for a in ins[4:]]
    vec = jax.ShapeDtypeStruct((1, D_GROUP), F32)
    outs = [jax.ShapeDtypeStruct((s, N_IN_COLS), F32),
            jax.ShapeDtypeStruct((D_GROUP, D_GROUP), F32), jax.ShapeDtypeStruct((CV_PAD, D_GROUP), F32), vec, vec, vec, vec]
    out_specs = [pl.BlockSpec((t, 2 * D_GROUP), rev(COL_CV_V // 2))] + [_full_spec(o) for o in outs[1:]]
    return _call(
        body, grid=(nt,), ins=ins, in_specs=in_specs, out_specs=out_specs, outs=outs,
        scratch=[pltpu.VMEM((t + CV_PAD, D_GROUP), F32), pltpu.VMEM((t, D_GROUP), F32),
                 pltpu.VMEM((SUBLANE - 1, t + CV_PAD, D_GROUP), F32)], name=name, rider=rider)


LRU_TILE = 256


def _lru_gates(xc, wr_ref, br_ref, wi_ref, bi_ref, sp_ref):
    r = _sigmoid(_dot(xc, wr_ref[...]) + br_ref[...])
    i = _sigmoid(_dot(xc, wi_ref[...]) + bi_ref[...])
    log_a = -LRU_C * r * sp_ref[...]
    a = jnp.exp(log_a)
    m = jnp.sqrt(_neg_expm1(2.0 * log_a))
    return r, i, a, m


def _lru_fwd(h_in, lcw, lcb, wr, br, wi, bi, sp, mix, *, name):
    s = h_in.shape[0]
    t = _seq_tile(s, LRU_TILE)
    pad = max(t // 2, SUBLANE)

    def body(xg_ref, xr_ref, lcw_ref, lcb_ref, wr_ref, br_ref, wi_ref, bi_ref, sp_ref, _mix_in,
             out_ref, xc_ref, h_ref, xpad, a0, a1, b0, b1, carry):
        @pl.when(pl.program_id(0) == 0)
        def _():
            xpad[0:SUBLANE, :] = jnp.zeros((SUBLANE, D_GROUP), F32)
            for bf in (a0, a1, b0, b1):
                bf[0:pad, :] = jnp.zeros((pad, D_GROUP), F32)
            carry[...] = jnp.zeros_like(carry)

        xpad[SUBLANE:SUBLANE + t, :] = xr_ref[...]
        xc = jnp.broadcast_to(lcb_ref[...], (t, D_GROUP))
        for k in range(LRU_CONV_WIDTH):
            o = SUBLANE - (LRU_CONV_WIDTH - 1) + k
            xc = xc + lcw_ref[k:k + 1, :] * xpad[o:o + t, :]
        xpad[0:SUBLANE, :] = xpad[t:t + SUBLANE, :]
        xc_ref[...] = xc
        _, i, a, m = _lru_gates(xc, wr_ref, br_ref, wi_ref, bi_ref, sp_ref)
        a0[pad:pad + t, :] = a
        b0[pad:pad + t, :] = m * (i * xc)
        b0[pad:pad + 1, :] += a0[pad:pad + 1, :] * carry[0:1, :]
        fin = _rscan_levels((a0, a1), (b0, b1), t, pad, reverse=False)
        hbuf = (b0, b1)[fin]
        carry[0:1, :] = hbuf[pad + t - 1:pad + t, :]
        h = hbuf[pad:pad + t, :]
        h_ref[...] = h
        out_ref[...] = (h * _gelu(xg_ref[...])).astype(BF16)

    ins = [h_in, h_in, lcw, lcb, wr, br, wi, bi, sp, mix]
    row = pl.BlockSpec((t, D_GROUP), lambda i: (i, 0))
    in_specs = [pl.BlockSpec((t, D_GROUP), lambda i: (i, COL_LRU_G)), pl.BlockSpec((t, D_GROUP), lambda i: (i, COL_LRU_X))] + \
               [_full_spec(a) for a in ins[2:9]] + [_ANY]
    return pl.pallas_call(
        body, grid=(s // t,), in_specs=in_specs,
        out_specs=[pl.BlockSpec((t, D_GROUP), lambda i: (i, MIX_LRU)), row, row],
        out_shape=[jax.ShapeDtypeStruct((s, D_MODEL), BF16)] + [jax.ShapeDtypeStruct((s, D_GROUP), F32)] * 2,
        input_output_aliases={9: 0},
        scratch_shapes=[pltpu.VMEM((SUBLANE + t, D_GROUP), F32)] + [pltpu.VMEM((pad + t, D_GROUP), F32)] * 4 +
                       [pltpu.VMEM((SUBLANE, D_GROUP), F32)],
        compiler_params=_cparams(1), name=name)(*ins)


def _lru_bwd(h_in, xc_all, h_all, dmix, lcw, wr, br, wi, bi, sp, dh_all, *, name):
    s = h_in.shape[0]
    t = _seq_tile(s, LRU_TILE)
    nt = s // t
    pad = max(t // 2, SUBLANE)
    tb = t // SUBLANE

    def body(xg_ref, xr_ref, xc_ref, h_ref, hprev_ref, do_ref, lcw_ref, wr_ref, br_ref, wi_ref, bi_ref, sp_ref, _dh_in,
             dgr_ref, dwr_ref, dwi_ref, dlcw_ref, dbr_ref, dbi_ref, dsp_ref, dlcb_ref,
             a0, a1, b0, b1, hp, dxpad, carry):
        pid = pl.program_id(0)

        @pl.when(pid == 0)
        def _():
            for bf in (a0, a1, b0, b1):
                bf[pad + t:pad + t + pad, :] = jnp.zeros((pad, D_GROUP), F32)
            dxpad[t:t + SUBLANE, :] = jnp.zeros((SUBLANE, D_GROUP), F32)
            carry[...] = jnp.zeros_like(carry)
            for r in (dwr_ref, dwi_ref, dlcw_ref, dbr_ref, dbi_ref, dsp_ref, dlcb_ref):
                r[...] = jnp.zeros_like(r)

        xc = xc_ref[...]
        h = h_ref[...]
        dout = do_ref[...]
        gate, dgate = _gelu_and_grad(xg_ref[...])
        dgr_ref[:, :D_GROUP] = dout * h * dgate
        r, i, a, m = _lru_gates(xc, wr_ref, br_ref, wi_ref, bi_ref, sp_ref)

        a0[pad:pad + t, :] = a
        b0[pad:pad + t, :] = dout * gate
        b0[pad + t - 1:pad + t, :] += carry[0:1, :]
        a1[pad:pad + t, :] = a0[pad + 1:pad + 1 + t, :]
        fin = _rscan_levels((a1, a0), (b0, b1), t, pad, reverse=True)
        lam = (b0, b1)[fin][pad:pad + t, :]
        carry[0:1, :] = a[0:1, :] * lam[0:1, :]

        is_first = pid == nt - 1
        hp[0:SUBLANE, :] = jnp.where(is_first, 0.0, hprev_ref[...])
        hp[SUBLANE:SUBLANE + t, :] = h
        hprev = hp[SUBLANE - 1:SUBLANE - 1 + t, :]

        ix = i * xc
        dmm = lam * ix
        dix = lam * m
        da = lam * hprev - dmm * (a / m)
        dlog_a = da * a
        dr = dlog_a * (-LRU_C * sp_ref[...])
        dsp_ref[...] += _colsum(dlog_a * (-LRU_C * r))
        dpr = dr * r * (1.0 - r)
        dpi = dix * xc * i * (1.0 - i)
        dbr_ref[...] += _colsum(dpr)
        dbi_ref[...] += _colsum(dpi)
        dwr_ref[...] += _dot_tn(xc, dpr)
        dwi_ref[...] += _dot_tn(xc, dpi)
        dxc = dix * i + _dot_nt(dpr, wr_ref[...]) + _dot_nt(dpi, wi_ref[...])
        dlcb_ref[...] += _colsum(dxc)

        dxpad[0:t, :] = dxc
        xr = xr_ref[...]
        dxr = jnp.zeros((t, D_GROUP), F32)
        for k in range(LRU_CONV_WIDTH):
            o = (LRU_CONV_WIDTH - 1) - k
            sh = dxpad[o:o + t, :]
            dxr = dxr + lcw_ref[k:k + 1, :] * sh
            dlcw_ref[k:k + 1, :] += _colsum(xr * sh)
        dxpad[t:t + SUBLANE, :] = dxpad[0:SUBLANE, :]
        dgr_ref[:, D_GROUP:] = dxr

    def rev(col):
        return lambda i: (nt - 1 - i, col)

    ins = [h_in, h_in, xc_all, h_all, h_all, dmix, lcw, wr, br, wi, bi, sp, dh_all]
    in_specs = [pl.BlockSpec((t, D_GROUP), rev(COL_LRU_G)), pl.BlockSpec((t, D_GROUP), rev(COL_LRU_X)),
                pl.BlockSpec((t, D_GROUP), rev(0)), pl.BlockSpec((t, D_GROUP), rev(0)),
                pl.BlockSpec((SUBLANE, D_GROUP), lambda i: (jnp.maximum((nt - 1 - i) * tb - 1, 0), 0)),
                pl.BlockSpec((t, D_GROUP), rev(MIX_LRU))] + [_full_spec(a) for a in ins[6:12]] + [_ANY]
    vec = jax.ShapeDtypeStruct((1, D_GROUP), F32)
    mat = jax.ShapeDtypeStruct((D_GROUP, D_GROUP), F32)
    outs = [jax.ShapeDtypeStruct((s, N_IN_COLS), F32), mat, mat, jax.ShapeDtypeStruct((SUBLANE, D_GROUP), F32),
            vec, vec, vec, vec]
    out_specs = [pl.BlockSpec((t, 2 * D_GROUP), rev(COL_LRU_G // 2))] + [_full_spec(o) for o in outs[1:]]
    return pl.pallas_call(
        body, grid=(nt,), in_specs=in_specs, out_specs=out_specs, out_shape=outs, input_output_aliases={12: 0},
        scratch_shapes=[pltpu.VMEM((pad + t + pad, D_GROUP), F32)] * 4 +
                       [pltpu.VMEM((SUBLANE + t, D_GROUP), F32), pltpu.VMEM((t + SUBLANE, D_GROUP), F32),
                        pltpu.VMEM((SUBLANE, D_GROUP), F32)],
        compiler_params=_cparams(1), name=name)(*ins)


def _blockdiag(w):
    h, d, _ = w.shape
    return jnp.tile(w.reshape(h * d, d), (1, h)) * _block_mask(h, d, d)


ATTN_TILE = 512
ATTN_SCALE = ATTN_HEAD_DIM ** -0.5


def _attn_big(kv):
    m = kv.shape[0]
    kbig = jnp.tile(kv[:, :D_GROUP].T, (1, ATTN_HEADS)) * _block_mask(ATTN_HEADS, ATTN_HEAD_DIM, m)
    vbig = jnp.tile(kv[:, D_GROUP:], (ATTN_HEADS, 1)) * _block_mask(ATTN_HEADS, m, ATTN_HEAD_DIM)
    return kbig, vbig


def _attn_probs(q, kbig_ref, m):
    sc = _dot(q, kbig_ref[...]) * ATTN_SCALE
    ps = []
    for h in range(ATTN_HEADS):
        sh = sc[:, h * m:(h + 1) * m]
        e = jnp.exp(sh - jnp.max(sh, axis=1, keepdims=True))
        ps.append(e / jnp.sum(e, axis=1, keepdims=True))
    return ps


def _attn_fwd(h_in, kbig, vbig, mix, *, name):
    s = h_in.shape[0]
    t = _seq_tile(s, ATTN_TILE)
    m = kbig.shape[1] // ATTN_HEADS

    def body(q_ref, kbig_ref, vbig_ref, _mix_in, o_ref):
        ps = _attn_probs(q_ref[...], kbig_ref, m)
        o_ref[...] = _dot(jnp.concatenate(ps, axis=1), vbig_ref[...]).astype(BF16)

    return pl.pallas_call(
        body, grid=(s // t,),
        in_specs=[pl.BlockSpec((t, D_GROUP), lambda i: (i, COL_Q)), _full_spec(kbig), _full_spec(vbig), _ANY],
        out_specs=pl.BlockSpec((t, D_GROUP), lambda i: (i, MIX_ATTN)),
        out_shape=jax.ShapeDtypeStruct((s, D_MODEL), BF16), input_output_aliases={3: 0},
        compiler_params=_cparams(1), name=name)(h_in, kbig, vbig, mix)


def _attn_bwd(h_in, dmix, kbig, vbig, dh_all, *, name):
    s = h_in.shape[0]
    t = _seq_tile(s, ATTN_TILE)
    m = kbig.shape[1] // ATTN_HEADS

    def body(q_ref, do_ref, kbig_ref, vbig_ref, _dh_in, dq_ref, dk_ref, dv_ref):
        @pl.when(pl.program_id(0) == 0)
        def _():
            dk_ref[...] = jnp.zeros_like(dk_ref)
            dv_ref[...] = jnp.zeros_like(dv_ref)

        q = q_ref[...]
        dout = do_ref[...]
        ps = _attn_probs(q, kbig_ref, m)
        dp = _dot_nt(dout, vbig_ref[...])
        dss = []
        for h in range(ATTN_HEADS):
            dph = dp[:, h * m:(h + 1) * m]
            dss.append(ps[h] * (dph - jnp.sum(dph * ps[h], axis=1, keepdims=True)))
        ds = (jnp.concatenate(dss, axis=1) * ATTN_SCALE).astype(BF16)
        dv_ref[...] += _dot_tn(jnp.concatenate(ps, axis=1), dout)
        dq_ref[...] = _dot_nt(ds, kbig_ref[...])
        dk_ref[...] += _dot_tn(q, ds)

    outs = [jax.ShapeDtypeStruct((s, N_IN_COLS), F32), jax.ShapeDtypeStruct(kbig.shape, F32),
            jax.ShapeDtypeStruct(vbig.shape, F32)]
    return pl.pallas_call(
        body, grid=(s // t,),
        in_specs=[pl.BlockSpec((t, D_GROUP), lambda i: (i, COL_Q)), pl.BlockSpec((t, D_GROUP), lambda i: (i, MIX_ATTN)),
                  _full_spec(kbig), _full_spec(vbig), _ANY],
        out_specs=[pl.BlockSpec((t, D_GROUP), lambda i: (i, COL_Q)), _full_spec(outs[1]), _full_spec(outs[2])],
        out_shape=outs, input_output_aliases={4: 0},
        compiler_params=_cparams(1), name=name)(h_in, dmix, kbig, vbig, dh_all)


FFN_TILE = 128
FFN_COL_CHUNK = 256
FFN_ROW_CHUNK = 64


def _ffn_conv(pad_ref, w_ref, b_ref, r0, ch, c0):
    cc = FFN_COL_CHUNK
    acc = jnp.broadcast_to(b_ref[:, c0:c0 + cc], (ch, cc))
    for k in range(FFN_CONV_WIDTH):
        o = SUBLANE - (FFN_CONV_WIDTH - 1) + k + r0
        acc = acc + w_ref[k:k + 1, c0:c0 + cc] * pad_ref[o:o + ch, c0:c0 + cc]
    return acc


def _ffn_gate_fwd(u, fcw, fcb, *, name, rider=None):
    s = u.shape[0]
    t = _seq_tile(s, FFN_TILE)
    ch = min(FFN_ROW_CHUNK, t)
    cc = FFN_COL_CHUNK

    def body(u_ref, w_ref, b_ref, o_ref, upad):
        @pl.when(pl.program_id(0) == 0)
        def _():
            upad[0:SUBLANE, :] = jnp.zeros((SUBLANE, 2 * D_FF), F32)

        upad[SUBLANE:SUBLANE + t, :] = u_ref[...].astype(F32)
        for c0 in range(0, D_FF, cc):
            for r0 in range(0, t, ch):
                val = _ffn_conv(upad, w_ref, b_ref, r0, ch, c0)
                gt = _ffn_conv(upad, w_ref, b_ref, r0, ch, c0 + D_FF)
                o_ref[r0:r0 + ch, c0:c0 + cc] = (val * _gelu(gt)).astype(BF16)
        upad[0:SUBLANE, :] = upad[t:t + SUBLANE, :]

    (out,), routs = _call(
        body, grid=(s // t,), ins=[u, fcw, fcb],
        in_specs=[pl.BlockSpec((t, 2 * D_FF), lambda i: (i, 0)), _full_spec(fcw), _full_spec(fcb)],
        out_specs=[pl.BlockSpec((t, D_FF), lambda i: (i, 0))], outs=[jax.ShapeDtypeStruct((s, D_FF), BF16)],
        scratch=[pltpu.VMEM((SUBLANE + t, 2 * D_FF), F32)], name=name, rider=rider)
    return out if rider is None else (out, routs)


def _ffn_gate_bwd(u, dh, fcw, fcb, *, name, rider=None):
    s = u.shape[0]
    t = _seq_tile(s, FFN_TILE)
    nt = s // t
    ch = min(FFN_ROW_CHUNK, t)
    cc = FFN_COL_CHUNK
    halo = 2 * SUBLANE
    tb = t // halo

    def body(u_ref, halo_ref, dh_ref, w_ref, b_ref, du_ref, dw_ref, db_ref, upad, dpad):
        pid = pl.program_id(0)

        @pl.when(pid == 0)
        def _():
            dpad[t:t + SUBLANE, :] = jnp.zeros((SUBLANE, 2 * D_FF), F32)
            dw_ref[...] = jnp.zeros_like(dw_ref)
            db_ref[...] = jnp.zeros_like(db_ref)

        upad[0:SUBLANE, :] = jnp.where(pid == nt - 1, 0.0, halo_ref[...].astype(F32)[SUBLANE:, :])
        upad[SUBLANE:SUBLANE + t, :] = u_ref[...].astype(F32)
        for c0 in range(0, D_FF, cc):
            for r0 in range(0, t, ch):
                val = _ffn_conv(upad, w_ref, b_ref, r0, ch, c0)
                gt = _ffn_conv(upad, w_ref, b_ref, r0, ch, c0 + D_FF)
                gl, dgl = _gelu_and_grad(gt)
                d = dh_ref[r0:r0 + ch, c0:c0 + cc].astype(F32)
                dpad[r0:r0 + ch, c0:c0 + cc] = d * gl
                dpad[r0:r0 + ch, c0 + D_FF:c0 + D_FF + cc] = d * val * dgl
        for c0 in range(0, 2 * D_FF, cc):
            dbs = jnp.zeros((1, cc), F32)
            dws = [jnp.zeros((1, cc), F32) for _ in range(FFN_CONV_WIDTH)]
            for r0 in range(0, t, ch):
                x = upad[SUBLANE + r0:SUBLANE + r0 + ch, c0:c0 + cc]
                acc = jnp.zeros((ch, cc), F32)
                for k in range(FFN_CONV_WIDTH):
                    o = (FFN_CONV_WIDTH - 1) - k + r0
                    sh = dpad[o:o + ch, c0:c0 + cc]
                    acc = acc + w_ref[k:k + 1, c0:c0 + cc] * sh
                    dws[k] = dws[k] + _colsum(x * sh)
                    if k == FFN_CONV_WIDTH - 1:
                        dbs = dbs + _colsum(sh)
                du_ref[r0:r0 + ch, c0:c0 + cc] = acc.astype(BF16)
            db_ref[:, c0:c0 + cc] += dbs
            for k in range(FFN_CONV_WIDTH):
                dw_ref[k:k + 1, c0:c0 + cc] += dws[k]
        dpad[t:t + SUBLANE, :] = dpad[0:SUBLANE, :]

    outs = [jax.ShapeDtypeStruct((s, 2 * D_FF), BF16), jax.ShapeDtypeStruct((SUBLANE, 2 * D_FF), F32),
            jax.ShapeDtypeStruct((1, 2 * D_FF), F32)]
    return _call(
        body, grid=(nt,), ins=[u, u, dh, fcw, fcb],
        in_specs=[pl.BlockSpec((t, 2 * D_FF), lambda i: (nt - 1 - i, 0)),
                  pl.BlockSpec((halo, 2 * D_FF), lambda i: (jnp.maximum((nt - 1 - i) * tb - 1, 0), 0)),
                  pl.BlockSpec((t, D_FF), lambda i: (nt - 1 - i, 0)), _full_spec(fcw), _full_spec(fcb)],
        out_specs=[pl.BlockSpec((t, 2 * D_FF), lambda i: (nt - 1 - i, 0)), _full_spec(outs[1]), _full_spec(outs[2])],
        outs=outs,
        scratch=[pltpu.VMEM((SUBLANE + t, 2 * D_FF), F32), pltpu.VMEM((t + SUBLANE, 2 * D_FF), F32)],
        name=name, rider=rider)


def _adamw_body(g_ref, w_ref, m_ref, v_ref, go_ref, d_ref, mo_ref, vo_ref):
    inv_b1 = 1.0 - ADAM_B1 ** ADAM_STEP
    inv_b2 = 1.0 - ADAM_B2 ** ADAM_STEP
    g = g_ref[0]
    for dev in range(1, N_DEV):
        g = g + g_ref[dev]
    go_ref[...] = g
    mn = ADAM_B1 * m_ref[...] + (1.0 - ADAM_B1) * g
    vn = ADAM_B2 * v_ref[...] + (1.0 - ADAM_B2) * (g * g)
    mo_ref[...] = mn
    vo_ref[...] = vn
    d_ref[...] = -ADAM_LR * ((mn / inv_b1) / (jnp.sqrt(vn / inv_b2) + ADAM_EPS) + ADAM_WD * w_ref[...])


def _adamw(gstack, w, m, v, *, name):
    _, r, c = gstack.shape
    tr = _pick_rows(r, PACK_ROW_BLOCK)

    def body(*refs):
        _adamw_body(*refs)

    blk = pl.BlockSpec((tr, c), lambda i: (i, 0))
    sh = jax.ShapeDtypeStruct((r, c), F32)
    return pl.pallas_call(
        body, grid=(r // tr,),
        in_specs=[pl.BlockSpec((N_DEV, tr, c), lambda i: (0, i, 0)), blk, blk, blk],
        out_specs=[blk] * 4, out_shape=[sh] * 4,
        compiler_params=_cparams(1), name=name)(gstack, w, m, v)


def _adamw_layer(gstack, w, m, v, layer, into, *, name):
    n_layers, r, c = w.shape
    tr = _pick_rows(r, PACK_ROW_BLOCK)

    def body(g_ref, w_ref, m_ref, v_ref, *rest):
        _adamw_body(g_ref, w_ref, m_ref, v_ref, *rest[-4:])

    blk = pl.BlockSpec((None, tr, c), lambda i: (layer, i, 0))
    sh = jax.ShapeDtypeStruct((n_layers, r, c), F32)
    into = list(into or [])
    return pl.pallas_call(
        body, grid=(r // tr,),
        in_specs=[pl.BlockSpec((N_DEV, tr, c), lambda i: (0, i, 0)), blk, blk, blk] + [_ANY] * len(into),
        out_specs=[blk] * 4, out_shape=[sh] * 4, input_output_aliases={4 + k: k for k in range(len(into))},
        compiler_params=_cparams(1), name=name)(gstack, w, m, v, *into)


def _exchange(rider, *, name):
    n = rider.n

    def body(*refs):
        x_refs, out_refs, sems = refs[:n], refs[n:2 * n], refs[2 * n:]
        rider.start(x_refs, out_refs, sems)
        rider.wait(x_refs, out_refs, sems)

    return pl.pallas_call(
        body, in_specs=[_ANY] * n, out_specs=[_ANY] * n, out_shape=rider.out_shapes(),
        scratch_shapes=rider.scratch(), name=name)(*rider.srcs)


def _pack_rows(n):
    rows = -(-n // PACK_COLS)
    return -(-rows // SUBLANE) * SUBLANE


def _pack(arrs, dtype):
    flat = jnp.concatenate([a.reshape(-1).astype(dtype) for a in arrs])
    rows = _pack_rows(flat.shape[0])
    flat = jnp.pad(flat, (0, rows * PACK_COLS - flat.shape[0]))
    return flat.reshape(rows, PACK_COLS)


def _pack_lead(arrs, dtype):
    flat = jnp.concatenate([a.reshape(N_DEV, -1).astype(dtype) for a in arrs], axis=1)
    rows = _pack_rows(flat.shape[1])
    flat = jnp.pad(flat, ((0, 0), (0, rows * PACK_COLS - flat.shape[1])))
    return flat.reshape(N_DEV, rows, PACK_COLS)


def _pack_layers(arrs, dtype):
    n_layers = arrs[0].shape[0]
    flat = jnp.concatenate([a.reshape(n_layers, -1).astype(dtype) for a in arrs], axis=1)
    rows = _pack_rows(flat.shape[1])
    flat = jnp.pad(flat, ((0, 0), (0, rows * PACK_COLS - flat.shape[1])))
    return flat.reshape(n_layers, rows, PACK_COLS)


def _unpack_layers(packed, shapes):
    flat = packed.reshape(packed.shape[0], -1)
    out, pos = [], 0
    for sh in shapes:
        n = math.prod(sh[1:])
        out.append(flat[:, pos:pos + n].reshape(sh))
        pos += n
    return out


def _unpack(packed, shapes, lead=False):
    flat = packed.reshape(N_DEV, -1) if lead else packed.reshape(-1)
    out, pos = [], 0
    for sh in shapes:
        n = math.prod(sh)
        out.append(flat[:, pos:pos + n].reshape((N_DEV,) + tuple(sh)) if lead else flat[pos:pos + n].reshape(sh))
        pos += n
    return out


def _join_shards(stacked, axis):
    return jnp.concatenate([stacked[d] for d in range(N_DEV)], axis=axis)


def _split_shards(full, axis):
    return jnp.stack(jnp.split(full, N_DEV, axis=axis), axis=0)


def _perm_in_cols(a, inverse=False):
    blocks = jnp.split(a, 6, axis=-1)
    if inverse:
        order = [IN_PERM.index(j) for j in range(6)]
    else:
        order = list(IN_PERM)
    return jnp.concatenate([blocks[j] for j in order], axis=-1)


def _row(v):
    return v.reshape(1, -1)


def _pad_rows(w, rows):
    return jnp.pad(w, ((0, rows - w.shape[0]), (0, 0)))


def _gn_avg_matrix():
    return _block_mask(GN_GROUPS, D_GROUP // GN_GROUPS, D_GROUP // GN_GROUPS) / (D_GROUP // GN_GROUPS)


def _layer_params(p, l):
    q = {}
    (a, wb, wc), q["s5_vjp"] = jax.vjp(_s5_param_map, p["s5_lam_re"][l], p["s5_lam_im"][l], p["s5_log_dt"][l],
                                       p["s5_b_re"][l], p["s5_b_im"][l], p["s5_c_re"][l], p["s5_c_im"][l])
    q["wb"], q["wc"] = wb.astype(BF16), wc.astype(BF16)
    q["apow"] = _s5_apow(a, max(S5_TILE.bit_length() - 1, 1))
    (q["wr"], q["wi"]), q["lru_w_vjp"] = jax.vjp(lambda r, i: (_blockdiag(r), _blockdiag(i)), p["lru_w_r"][l], p["lru_w_i"][l])
    q["wr"], q["wi"] = q["wr"].astype(BF16), q["wi"].astype(BF16)
    q["sp"], q["sp_vjp"] = jax.vjp(lambda lam: _row(jax.nn.softplus(-lam)), p["lru_lam"][l])
    return q


WEIGHT_RIDES = {(0, "ln_in_fwd"): [("w_in", 0), ("small_pack", 0)],
                (0, "inproj"): [("attn_w_kv", 0), ("w_out", 0)], (0, "s5_fwd"): [("ffn_w_up", 0)],
                (0, "cv_fwd"): [("ffn_w_down", 0)],
                (0, "ffn_up"): [("w_in", 1), ("attn_w_kv", 1), ("w_out", 1), ("ffn_w_down", 1)],
                (0, "ffn_gate_fwd"): [("ffn_w_up", 1)]}
GRAD_RIDES = {(1, "ffn_gate_bwd"): [("ffn_w_down", 1)],
              (0, "dw_down"): [("w_out", 1), ("attn_w_kv", 1), ("w_in", 1)],
              (0, "dhff"): [("rep", 1), ("ssh", 1)],
              (0, "dw_in"): [("attn_w_kv", 0), ("ssh", 0)],
              (0, "ln_in_bwd"): [("rep", 0)],
              (0, "ffn_gate_bwd"): [("ffn_w_up", 1)],
              (0, "dx1"): [("ffn_w_down", 0)],
              (0, "cv_bwd"): [("w_out", 0)],
              (0, "s5_bwd"): [("ffn_w_up", 0)],
              (0, "dxs"): [("w_in", 0)]}


def _assemble_weight(n, gathered):
    if SHARDED[n] == 2:
        full = jnp.transpose(gathered, (1, 0, 2)).reshape(gathered.shape[1], -1)
        return _perm_in_cols(full) if n == "w_in" else full
    return gathered.reshape(-1, gathered.shape[-1])


def _grad_source(n, g):
    if SHARDED[n] == 2:
        if n == "w_in":
            g = _perm_in_cols(g, inverse=True)
        k, nn = g.shape
        return jnp.transpose(g.reshape(k, N_DEV, nn // N_DEV), (1, 0, 2)), "lead"
    return g, "rows"


def _hosted(fn, keys_rider, land, *args, **kw):
    keys, rider = keys_rider
    if rider is None:
        return fn(*args, **kw)
    out, routs = fn(*args, rider=rider, **kw)
    land(keys, routs)
    return out


def _local_step(x, mem, target, p, big_w, shards=None, unpack_small=None):
    dist = shards is not None
    small, saved = {}, []
    big_g, ready, recv = {}, {}, {}
    mavg = _gn_avg_matrix()

    def weight_rider(l, host):
        keys = WEIGHT_RIDES.get((l, host), []) if dist else []
        return keys, (_Rider([shards[n][ll] for n, ll in keys], ["all"] * len(keys)) if keys else None)

    def land_weights(keys, routs):
        for (n, ll), r in zip(keys, routs):
            if n == "small_pack":
                p.update(unpack_small(r))
            else:
                big_w[n][ll] = _assemble_weight(n, r)

    def grad_rider(l, host):
        keys = [k for k in GRAD_RIDES.get((l, host), []) if k in ready] if dist else []
        return keys, (_Rider([ready[k][0] for k in keys], [ready[k][1] for k in keys]) if keys else None)

    def land_grads(keys, routs):
        for k, r in zip(keys, routs):
            recv[k] = r
            del ready[k]

    def big_grad(n, l, g):
        if dist:
            ready[(n, l)] = _grad_source(n, g)
        else:
            big_g[(n, l)] = g

    xs = _hosted(_ln_fwd, weight_rider(0, "ln_in_fwd"), land_weights, x, _row(p["ln_in_g"]), _row(p["ln_in_b"]),
                 name="ln_in_fwd")
    for l in range(DEPTH):
        q = _layer_params(p, l)
        n = f"l{l}_"
        hin = _hosted(_mm, weight_rider(l, "inproj"), land_weights, xs, big_w["w_in"][l], bias=_row(p["b_in"][l]),
                      name=n + "inproj")
        keys, rd = weight_rider(l, "s5_fwd")
        (mix, s5_y1, s5_hb), routs = _s5_fwd(hin, q["wb"], q["apow"], q["wc"], _row(p["s5_d"][l]), p["s5_w_glu"][l],
                                             _row(p["s5_b_glu"][l]), name=n + "s5_fwd", rider=rd)
        land_weights(keys, routs)
        cvw = _pad_rows(p["cv_w"][l], CV_PAD)
        keys, rd = weight_rider(l, "cv_fwd")
        (mix, cv_c), routs = _cv_fwd(hin, cvw, _row(p["cv_b"][l]), _row(p["cv_gn_g"][l]), _row(p["cv_gn_b"][l]), mavg,
                                     p["cv_w_pw"][l], _row(p["cv_b_pw"][l]), mix, name=n + "cv_fwd", rider=rd)
        land_weights(keys, routs)
        lcw = _pad_rows(p["lru_conv_w"][l], SUBLANE)
        mix, lru_xc, lru_h = _lru_fwd(hin, lcw, _row(p["lru_conv_b"][l]), q["wr"], _row(p["lru_b_r"][l]), q["wi"],
                                      _row(p["lru_b_i"][l]), q["sp"], mix, name=n + "lru_fwd")
        kv = _mm(mem, big_w["attn_w_kv"][l], name=n + "kv")
        (kbig, vbig), kv_vjp = jax.vjp(_attn_big, kv)
        kbig, vbig = kbig.astype(BF16), vbig.astype(BF16)
        mix = _attn_fwd(hin, kbig, vbig, mix, name=n + "attn_fwd")
        r1 = _mm(mix, big_w["w_out"][l], bias=_row(p["b_out"][l]), res=xs, res_scale=ALPHA, name=n + "outproj")
        x1 = _ln_fwd(r1, _row(p["ln1_g"][l]), _row(p["ln1_b"][l]), name=n + "ln1_fwd")
        u = _hosted(_mm, weight_rider(l, "ffn_up"), land_weights, x1, big_w["ffn_w_up"][l], out_dtype=BF16,
                    name=n + "ffn_up")
        fcw = _pad_rows(p["ffn_conv_w"][l], SUBLANE)
        fcb = _row(p["ffn_conv_b"][l])
        hff = _hosted(_ffn_gate_fwd, weight_rider(l, "ffn_gate_fwd"), land_weights, u, fcw, fcb, name=n + "ffn_gate_fwd")
        r2 = _mm(hff, big_w["ffn_w_down"][l], res=x1, res_scale=ALPHA, name=n + "ffn_down")
        x2 = _ln_fwd(r2, _row(p["ln2_g"][l]), _row(p["ln2_b"][l]), name=n + "ln2_fwd")
        saved.append(dict(q=q, xs=xs, hin=hin, s5_y1=s5_y1, s5_hb=s5_hb, cvw=cvw, cv_c=cv_c, lcw=lcw, lru_xc=lru_xc,
                          lru_h=lru_h, kbig=kbig, vbig=vbig, kv_vjp=kv_vjp, mix=mix, r1=r1, x1=x1, u=u, fcw=fcw,
                          fcb=fcb, hff=hff, r2=r2))
        xs = x2

    dx, loss_blk = _loss_grad(xs, target, name="loss_grad")
    loss = loss_blk[0, 0]

    for l in reversed(range(DEPTH)):
        sv = saved[l]
        q = sv["q"]
        n = f"l{l}_"
        g = {}
        dr2, g["ln2_g"], g["ln2_b"], _ = _ln_bwd(sv["r2"], dx, _row(p["ln2_g"][l]), name=n + "ln2_bwd")
        big_grad("ffn_w_down", l, _hosted(_mm_tn, grad_rider(l, "dw_down"), land_grads, sv["hff"], dr2, name=n + "dw_down"))
        dhff = _hosted(_mm, grad_rider(l, "dhff"), land_grads, dr2, big_w["ffn_w_down"][l], trans_b=True,
                       out_dtype=BF16, name=n + "dhff")
        keys, rd = grad_rider(l, "ffn_gate_bwd")
        (du, dfw, g["ffn_conv_b"]), routs = _ffn_gate_bwd(sv["u"], dhff, sv["fcw"], sv["fcb"], name=n + "ffn_gate_bwd",
                                                          rider=rd)
        land_grads(keys, routs)
        g["ffn_conv_w"] = dfw[:FFN_CONV_WIDTH]
        big_grad("ffn_w_up", l, _mm_tn(sv["x1"], du, name=n + "dw_up"))
        dx1 = _hosted(_mm, grad_rider(l, "dx1"), land_grads, du, big_w["ffn_w_up"][l], trans_b=True, res=dr2,
                      res_scale=ALPHA, name=n + "dx1")
        dr1, g["ln1_g"], g["ln1_b"], g["b_out"] = _ln_bwd(sv["r1"], dx1, _row(p["ln1_g"][l]), name=n + "ln1_bwd")
        big_grad("w_out", l, _mm_tn(sv["mix"], dr1, name=n + "dw_out"))
        dmix = _mm(dr1, big_w["w_out"][l], trans_b=True, name=n + "dmix")

        hin = sv["hin"]
        keys, rd = grad_rider(l, "cv_bwd")
        (dh, g["cv_w_pw"], dcw, g["cv_b_pw"], g["cv_gn_g"], g["cv_gn_b"], g["cv_b"]), routs = _cv_bwd(
            hin, sv["cv_c"], dmix, sv["cvw"], _row(p["cv_gn_g"][l]), _row(p["cv_gn_b"][l]), mavg, p["cv_w_pw"][l],
            name=n + "cv_bwd", rider=rd)
        land_grads(keys, routs)
        g["cv_w"] = dcw[:CONV_WIDTH]
        dh, dwr, dwi, dlcw, g["lru_b_r"], g["lru_b_i"], dsp, g["lru_conv_b"] = _lru_bwd(
            hin, sv["lru_xc"], sv["lru_h"], dmix, sv["lcw"], q["wr"], _row(p["lru_b_r"][l]), q["wi"],
            _row(p["lru_b_i"][l]), q["sp"], dh, name=n + "lru_bwd")
        g["lru_conv_w"] = dlcw[:LRU_CONV_WIDTH]
        g["lru_w_r"], g["lru_w_i"] = q["lru_w_vjp"]((dwr, dwi))
        (g["lru_lam"],) = q["sp_vjp"](dsp)
        keys, rd = grad_rider(l, "s5_bwd")
        (dh, g["s5_w_glu"], dwc, dwb, g["s5_b_glu"], g["s5_d"], da), routs = _s5_bwd(
            hin, sv["s5_y1"], dmix, sv["s5_hb"], q["wb"], q["apow"], q["wc"], _row(p["s5_d"][l]), p["s5_w_glu"][l],
            _row(p["s5_b_glu"][l]), dh, name=n + "s5_bwd", rider=rd)
        land_grads(keys, routs)
        (g["s5_lam_re"], g["s5_lam_im"], g["s5_log_dt"], g["s5_b_re"], g["s5_b_im"], g["s5_c_re"],
         g["s5_c_im"]) = q["s5_vjp"]((da, dwb, dwc))
        dh, dkbig, dvbig = _attn_bwd(hin, dmix, sv["kbig"], sv["vbig"], dh, name=n + "attn_bwd")
        (dkv,) = sv["kv_vjp"]((dkbig, dvbig))
        big_grad("attn_w_kv", l, _mm_tn(mem, dkv, name=n + "dw_kv"))

        g["b_in"] = _colsum_call(dh, name=n + "db_in")
        if dist:
            g["b_in"] = _perm_in_cols(g["b_in"], inverse=True)
            ready[("rep", l)] = (_pack([g[k] for k in REP_LAYERED], F32), "all")
            ready[("ssh", l)] = (_pack_lead([_split_shards(g[k], SHARDED[k] - 1) for k in SMALL_SHARDED], F32), "lead")
        else:
            for k, v in g.items():
                small.setdefault(k, [None] * DEPTH)[l] = v.reshape(p[k].shape[1:])
        big_grad("w_in", l, _hosted(_mm_tn, grad_rider(l, "dw_in"), land_grads, sv["xs"], dh, name=n + "dw_in"))
        dx = _hosted(_mm, grad_rider(l, "dxs"), land_grads, dh, big_w["w_in"][l], trans_b=True, res=dr1,
                     res_scale=ALPHA, name=n + "dxs")

    keys, rd = grad_rider(0, "ln_in_bwd")
    if rd is None:
        grad_x, dgi, dbi, _ = _ln_bwd(x, dx, _row(p["ln_in_g"]), name="ln_in_bwd")
    else:
        (grad_x, dgi, dbi, _), routs = _ln_bwd(x, dx, _row(p["ln_in_g"]), name="ln_in_bwd", rider=rd)
        land_grads(keys, routs)
    out = {k: jnp.stack(v, axis=0) for k, v in small.items()}
    out["ln_in_g"], out["ln_in_b"] = dgi.reshape(-1), dbi.reshape(-1)
    return loss, grad_x, out, ((recv, ready) if dist else big_g)


def kernel(x, mem, ln_in_g, ln_in_b, w_in, b_in, s5_lam_re, s5_lam_im, s5_log_dt, s5_b_re, s5_b_im, s5_c_re, s5_c_im, s5_d, s5_w_glu, s5_b_glu, cv_w, cv_b, cv_gn_g, cv_gn_b, cv_w_pw, cv_b_pw, lru_conv_w, lru_conv_b, lru_w_r, lru_b_r, lru_w_i, lru_b_i, lru_lam, attn_w_kv, w_out, b_out, ln1_g, ln1_b, ffn_w_up, ffn_conv_w, ffn_conv_b, ffn_w_down, ln2_g, ln2_b, loss_target, m_ln_in_g, m_ln_in_b, m_w_in, m_b_in, m_s5_lam_re, m_s5_lam_im, m_s5_log_dt, m_s5_b_re, m_s5_b_im, m_s5_c_re, m_s5_c_im, m_s5_d, m_s5_w_glu, m_s5_b_glu, m_cv_w, m_cv_b, m_cv_gn_g, m_cv_gn_b, m_cv_w_pw, m_cv_b_pw, m_lru_conv_w, m_lru_conv_b, m_lru_w_r, m_lru_b_r, m_lru_w_i, m_lru_b_i, m_lru_lam, m_attn_w_kv, m_w_out, m_b_out, m_ln1_g, m_ln1_b, m_ffn_w_up, m_ffn_conv_w, m_ffn_conv_b, m_ffn_w_down, m_ln2_g, m_ln2_b, v_ln_in_g, v_ln_in_b, v_w_in, v_b_in, v_s5_lam_re, v_s5_lam_im, v_s5_log_dt, v_s5_b_re, v_s5_b_im, v_s5_c_re, v_s5_c_im, v_s5_d, v_s5_w_glu, v_s5_b_glu, v_cv_w, v_cv_b, v_cv_gn_g, v_cv_gn_b, v_cv_w_pw, v_cv_b_pw, v_lru_conv_w, v_lru_conv_b, v_lru_w_r, v_lru_b_r, v_lru_w_i, v_lru_b_i, v_lru_lam, v_attn_w_kv, v_w_out, v_b_out, v_ln1_g, v_ln1_b, v_ffn_w_up, v_ffn_conv_w, v_ffn_conv_b, v_ffn_w_down, v_ln2_g, v_ln2_b):
    args = locals()
    w = {n: args[n] for n in WEIGHTS}
    mom = {n: args["m_" + n] for n in WEIGHTS}
    var = {n: args["v_" + n] for n in WEIGHTS}

    shards = {n: w[n].astype(BF16) for n in BIG}
    shards["small_pack"] = [_pack([w[n] for n in SMALL_SHARDED], F32)]
    small_shapes = [w[n].shape for n in SMALL_SHARDED]

    def unpack_small(gathered):
        out = {n: _join_shards(st, SHARDED[n]) for n, st in zip(SMALL_SHARDED, _unpack(gathered, small_shapes, lead=True))}
        for n in ("s5_w_glu", "cv_w_pw"):
            out[n] = out[n].astype(BF16)
        return out

    big_w = {n: [None] * DEPTH for n in BIG}
    p = {n: w[n] for n in REPLICATED}
    p["b_in"] = _perm_in_cols(p["b_in"])

    loss, grad_x, g_small, (recv, ready) = _local_step(x[0], mem[0], loss_target[0], p, big_w, shards, unpack_small)
    loss = lax.psum(loss, ("x", "y", "c"))

    left = list(ready)
    rider = _Rider([ready[k][0] for k in left] + [_pack([g_small["ln_in_g"], g_small["ln_in_b"]], F32)],
                   [ready[k][1] for k in left] + ["all"])
    got = _exchange(rider, name="exchange_grads")
    for k, r in zip(left, got):
        recv[k] = r

    res = [dict(), dict(), dict(), dict()]
    for n in BIG:
        outs = None
        for l in range(DEPTH):
            outs = _adamw_layer(recv[(n, l)], w[n], mom[n], var[n], l, outs, name=f"adamw_{n}_l{l}")
        for kind in range(4):
            res[kind][n] = outs[kind]
    for names, key, tag in ((SMALL_SHARDED, "ssh", "adamw_small_sharded"), (REP_LAYERED, "rep", "adamw_replicated")):
        gstack = jnp.concatenate([recv[(key, l)] for l in range(DEPTH)], axis=1)
        packs = [_pack_layers([t[n] for n in names], F32) for t in (w, mom, var)]
        rows = packs[0].shape[1]
        outs = _adamw(gstack, *[pk.reshape(DEPTH * rows, PACK_COLS) for pk in packs], name=tag)
        for kind in range(4):
            for n, a in zip(names, _unpack_layers(outs[kind].reshape(DEPTH, rows, PACK_COLS), [w[n].shape for n in names])):
                res[kind][n] = a
    ln_names = ("ln_in_g", "ln_in_b")
    outs = _adamw(got[len(left)], _pack([w[n] for n in ln_names], F32), _pack([mom[n] for n in ln_names], F32),
                  _pack([var[n] for n in ln_names], F32), name="adamw_ln_in")
    for kind in range(4):
        for n, a in zip(ln_names, _unpack(outs[kind], [w[n].shape for n in ln_names])):
            res[kind][n] = a
    return (loss, grad_x[None], *[res[0][n] for n in WEIGHTS], *[res[1][n] for n in WEIGHTS],
            *[res[2][n] for n in WEIGHTS], *[res[3][n] for n in WEIGHTS])
```

```python
import math

import jax
import jax.numpy as jnp
from jax import lax
from jax.experimental import pallas as pl
from jax.experimental.pallas import tpu as pltpu

F32 = jnp.float32
BF16 = jnp.bfloat16

D_MODEL = 1024
DEPTH = 2
D_GROUP = 256
N_IN_COLS = 6 * D_GROUP
S5_GROUPS = 16
S5_CH = 16
S5_STATE = 64
S5_LANES = S5_GROUPS * S5_STATE
CONV_WIDTH = 31
GN_GROUPS = 4
LRU_HEADS = 4
LRU_CONV_WIDTH = 4
LRU_C = 8.0
ATTN_HEADS = 4
ATTN_HEAD_DIM = 64
D_FF = 2816
FFN_CONV_WIDTH = 3
ALPHA = (2 * DEPTH) ** 0.25
LN_EPS = 1e-5
ADAM_LR, ADAM_B1, ADAM_B2, ADAM_EPS, ADAM_WD, ADAM_STEP = 0.001, 0.9, 0.999, 1e-08, 0.01, 10

N_DEV = 8
N_PEERS = N_DEV - 1
LANE = 128
SUBLANE = 8
VMEM_LIMIT = 56 * 1024 * 1024
PACK_COLS = 1024
PACK_ROW_BLOCK = 256

SHARDED = {
    "w_in": 2, "s5_w_glu": 1, "cv_w": 2, "cv_w_pw": 1, "lru_conv_w": 2, "attn_w_kv": 1,
    "w_out": 1, "ffn_w_up": 2, "ffn_conv_w": 2, "ffn_w_down": 1,
}
BIG = ("w_in", "attn_w_kv", "w_out", "ffn_w_up", "ffn_w_down")
SMALL_SHARDED = ("s5_w_glu", "cv_w", "cv_w_pw", "lru_conv_w", "ffn_conv_w")
MATMUL_WEIGHTS = ("w_in", "s5_w_glu", "cv_w_pw", "attn_w_kv", "w_out", "ffn_w_up", "ffn_w_down")
WEIGHTS = ['ln_in_g', 'ln_in_b', 'w_in', 'b_in', 's5_lam_re', 's5_lam_im', 's5_log_dt', 's5_b_re', 's5_b_im',
           's5_c_re', 's5_c_im', 's5_d', 's5_w_glu', 's5_b_glu', 'cv_w', 'cv_b', 'cv_gn_g', 'cv_gn_b', 'cv_w_pw',
           'cv_b_pw', 'lru_conv_w', 'lru_conv_b', 'lru_w_r', 'lru_b_r', 'lru_w_i', 'lru_b_i', 'lru_lam',
           'attn_w_kv', 'w_out', 'b_out', 'ln1_g', 'ln1_b', 'ffn_w_up', 'ffn_conv_w', 'ffn_conv_b', 'ffn_w_down',
           'ln2_g', 'ln2_b']
REPLICATED = [n for n in WEIGHTS if n not in SHARDED]
REP_LAYERED = [n for n in REPLICATED if n not in ("ln_in_g", "ln_in_b")]

COL_CV_V, COL_CV_G, COL_LRU_G, COL_LRU_X, COL_S5, COL_Q = range(6)
IN_PERM = (1, 2, 3, 4, 0, 5)
MIX_S5, MIX_CV, MIX_LRU, MIX_ATTN = range(4)


_ANY = pl.BlockSpec(memory_space=pl.ANY)
_MESH = pl.DeviceIdType.MESH


def _cparams(n_axes):
    return pltpu.CompilerParams(dimension_semantics=("arbitrary",) * n_axes, vmem_limit_bytes=VMEM_LIMIT)


def _pick(n, cap):
    if n <= cap:
        return n
    best = None
    for t in range(LANE, cap + 1, LANE):
        if n % t == 0:
            best = t
    assert best is not None, (n, cap)
    return best


def _pick_rows(n, cap):
    best = None
    for t in range(SUBLANE, min(n, cap) + 1, SUBLANE):
        if n % t == 0:
            best = t
    assert best is not None, (n, cap)
    return best


def _full_spec(arr):
    nd = arr.ndim
    return pl.BlockSpec(arr.shape, lambda *_: (0,) * nd)


def _dot(a, b):
    return lax.dot_general(a.astype(BF16), b.astype(BF16), (((1,), (0,)), ((), ())), preferred_element_type=F32)


def _dot_nt(a, b):
    return lax.dot_general(a.astype(BF16), b.astype(BF16), (((1,), (1,)), ((), ())), preferred_element_type=F32)


def _dot_tn(a, b):
    return lax.dot_general(a.astype(BF16), b.astype(BF16), (((0,), (0,)), ((), ())), preferred_element_type=F32)


def _dot_hi(a, b):
    return jnp.dot(a, b, precision=lax.Precision.HIGHEST, preferred_element_type=F32)


def _colsum(x):
    return jnp.sum(x, axis=0, keepdims=True)


def _sigmoid(x):
    return 1.0 / (1.0 + jnp.exp(-x))


_GELU_K = math.sqrt(2.0 / math.pi)
_GELU_C = 0.044715


def _gelu(x):
    t = jnp.tanh(_GELU_K * (x + _GELU_C * x * x * x))
    return 0.5 * x * (1.0 + t)


def _gelu_and_grad(x):
    x2 = x * x
    t = jnp.tanh(_GELU_K * (x + _GELU_C * x2 * x))
    g = 0.5 * x * (1.0 + t)
    dg = 0.5 * (1.0 + t) + 0.5 * x * (1.0 - t * t) * (_GELU_K * (1.0 + 3.0 * _GELU_C * x2))
    return g, dg


def _neg_expm1(x):
    series = x * (1.0 + x * (0.5 + x * (1.0 / 6.0 + x * (1.0 / 24.0 + x * (1.0 / 120.0)))))
    return -jnp.where(jnp.abs(x) < 0.1, series, jnp.exp(x) - 1.0)


def _seq_tile(s, want):
    t = min(s, want)
    assert s % t == 0
    return t


class _Rider:
    def __init__(self, srcs, kinds):
        self.srcs, self.kinds = list(srcs), list(kinds)
        self.n = len(self.srcs)

    def out_shapes(self):
        shapes = []
        for x, kind in zip(self.srcs, self.kinds):
            if kind == "lead":
                shp = x.shape
            elif kind == "rows":
                shp = (N_DEV, x.shape[0] // N_DEV) + x.shape[1:]
            else:
                shp = (N_DEV,) + x.shape
            shapes.append(jax.ShapeDtypeStruct(shp, x.dtype))
        return shapes

    def scratch(self):
        return [pltpu.SemaphoreType.DMA((self.n * N_PEERS,)), pltpu.SemaphoreType.DMA((self.n * N_PEERS,)),
                pltpu.SemaphoreType.DMA((self.n,))]

    def _copies(self, x_refs, out_refs, sems):
        send_sems, recv_sems, local_sems = sems
        mx, my, mc = lax.axis_index("x"), lax.axis_index("y"), lax.axis_index("c")
        my_id = 4 * mx + 2 * my + mc

        def piece(i, dev):
            if self.kinds[i] == "lead":
                return x_refs[i].at[dev]
            if self.kinds[i] == "rows":
                r = x_refs[i].shape[0] // N_DEV
                return x_refs[i].at[pl.ds(pl.multiple_of(dev * r, SUBLANE), r)]
            return x_refs[i]

        mine = [pltpu.make_async_copy(piece(i, my_id), out_refs[i].at[my_id], local_sems.at[i]) for i in range(self.n)]
        copies = []
        for k in range(1, N_DEV):
            px, py, pc = mx ^ ((k >> 2) & 1), my ^ ((k >> 1) & 1), mc ^ (k & 1)
            for i in range(self.n):
                copies.append(pltpu.make_async_remote_copy(
                    src_ref=piece(i, 4 * px + 2 * py + pc), dst_ref=out_refs[i].at[my_id],
                    send_sem=send_sems.at[i * N_PEERS + k - 1], recv_sem=recv_sems.at[i * N_PEERS + k - 1],
                    device_id=(px, py, pc), device_id_type=_MESH))
        return mine, copies

    def start(self, x_refs, out_refs, sems):
        mine, copies = self._copies(x_refs, out_refs, sems)
        for cp in mine + copies:
            cp.start()

    def wait(self, x_refs, out_refs, sems):
        mine, copies = self._copies(x_refs, out_refs, sems)
        for cp in copies:
            cp.wait_recv()
        for cp in copies:
            cp.wait_send()
        for cp in mine:
            cp.wait()


def _call(body, *, grid, ins, in_specs, outs, out_specs, scratch=(), aliases=None, name, rider=None):
    n_axes = len(grid)
    common = dict(grid=grid, input_output_aliases=aliases or {}, compiler_params=_cparams(n_axes), name=name)
    if rider is None:
        res = pl.pallas_call(body, in_specs=list(in_specs), out_specs=list(out_specs), out_shape=list(outs),
                             scratch_shapes=list(scratch), **common)(*ins)
        return list(res), []
    n_in, n_out, n_scr, nr = len(ins), len(outs), len(scratch), rider.n

    def wrapped(*refs):
        pos = [0]

        def take(k):
            part = refs[pos[0]:pos[0] + k]
            pos[0] += k
            return part

        a_in, r_in, a_out, r_out, a_scr, sems = take(n_in), take(nr), take(n_out), take(nr), take(n_scr), take(3)
        first = last = None
        for ax in range(n_axes):
            pid = pl.program_id(ax)
            f, l = pid == 0, pid == grid[ax] - 1
            first = f if first is None else jnp.logical_and(first, f)
            last = l if last is None else jnp.logical_and(last, l)

        @pl.when(first)
        def _():
            rider.start(r_in, r_out, sems)

        body(*a_in, *a_out, *a_scr)

        @pl.when(last)
        def _():
            rider.wait(r_in, r_out, sems)

    res = pl.pallas_call(
        wrapped, in_specs=list(in_specs) + [_ANY] * nr, out_specs=list(out_specs) + [_ANY] * nr,
        out_shape=list(outs) + rider.out_shapes(), scratch_shapes=list(scratch) + rider.scratch(), **common)(*ins, *rider.srcs)
    return list(res[:n_out]), list(res[n_out:])


def _block_mask(n_blocks, block_rows, block_cols):
    r = jnp.arange(n_blocks * block_rows) // block_rows
    c = jnp.arange(n_blocks * block_cols) // block_cols
    return (r[:, None] == c[None, :]).astype(F32)


def _mm(a, b, *, bias=None, res=None, res_scale=1.0, trans_b=False, out_dtype=F32, name, rider=None):
    m, kdim = a.shape
    n = b.shape[0] if trans_b else b.shape[1]
    tm = _seq_tile(m, 1024)
    tn = _pick(n, 1408)
    tk = _pick(kdim, 1536)
    nk = kdim // tk
    has_bias, has_res = bias is not None, res is not None

    def body(*refs):
        a_ref, b_ref = refs[0], refs[1]
        pos = 2
        bias_ref = res_ref = None
        if has_bias:
            bias_ref = refs[pos]
            pos += 1
        if has_res:
            res_ref = refs[pos]
            pos += 1
        o_ref, acc_ref = refs[pos], refs[pos + 1]
        k = pl.program_id(2)

        @pl.when(k == 0)
        def _():
            acc_ref[...] = jnp.zeros_like(acc_ref)

        if trans_b:
            acc_ref[...] += _dot_nt(a_ref[...], b_ref[...])
        else:
            acc_ref[...] += _dot(a_ref[...], b_ref[...])

        @pl.when(k == nk - 1)
        def _():
            r = acc_ref[...]
            if has_bias:
                r = r + bias_ref[...]
            if has_res:
                r = r + res_scale * res_ref[...]
            o_ref[...] = r.astype(out_dtype)

    ins = [a, b]
    in_specs = [pl.BlockSpec((tm, tk), lambda i, j, k: (i, k)),
                pl.BlockSpec((tn, tk), lambda i, j, k: (j, k)) if trans_b
                else pl.BlockSpec((tk, tn), lambda i, j, k: (k, j))]
    if has_bias:
        ins.append(bias)
        in_specs.append(pl.BlockSpec((1, tn), lambda i, j, k: (0, j)))
    if has_res:
        ins.append(res)
        in_specs.append(pl.BlockSpec((tm, tn), lambda i, j, k: (i, j)))
    (out,), routs = _call(
        body, grid=(m // tm, n // tn, nk), ins=ins, in_specs=in_specs,
        outs=[jax.ShapeDtypeStruct((m, n), out_dtype)], out_specs=[pl.BlockSpec((tm, tn), lambda i, j, k: (i, j))],
        scratch=[pltpu.VMEM((tm, tn), F32)], name=name, rider=rider)
    return out if rider is None else (out, routs)


def _mm_tn(a, b, *, name, rider=None):
    s, ka = a.shape
    nb = b.shape[1]
    ts = _seq_tile(s, 512)
    tka = _pick(ka, 1408)
    tnb = _pick(nb, 1408)

    def body(a_ref, b_ref, o_ref):
        @pl.when(pl.program_id(2) == 0)
        def _():
            o_ref[...] = jnp.zeros_like(o_ref)

        o_ref[...] += _dot_tn(a_ref[...], b_ref[...])

    (out,), routs = _call(
        body, grid=(ka // tka, nb // tnb, s // ts), ins=[a, b],
        in_specs=[pl.BlockSpec((ts, tka), lambda i, j, k: (k, i)), pl.BlockSpec((ts, tnb), lambda i, j, k: (k, j))],
        outs=[jax.ShapeDtypeStruct((ka, nb), F32)], out_specs=[pl.BlockSpec((tka, tnb), lambda i, j, k: (i, j))],
        name=name, rider=rider)
    return out if rider is None else (out, routs)


def _colsum_call(x, *, name):
    s, n = x.shape
    ts = _seq_tile(s, 512)

    def body(x_ref, o_ref):
        @pl.when(pl.program_id(0) == 0)
        def _():
            o_ref[...] = jnp.zeros_like(o_ref)

        o_ref[...] += _colsum(x_ref[...])

    return pl.pallas_call(
        body, grid=(s // ts,), in_specs=[pl.BlockSpec((ts, n), lambda i: (i, 0))],
        out_specs=pl.BlockSpec((1, n), lambda i: (0, 0)), out_shape=jax.ShapeDtypeStruct((1, n), F32),
        compiler_params=_cparams(1), name=name)(x)


def _ln_fwd(r, g, b, *, name, rider=None):
    s, d = r.shape
    ts = _seq_tile(s, 512)

    def body(r_ref, g_ref, b_ref, o_ref):
        x = r_ref[...]
        mu = jnp.mean(x, axis=1, keepdims=True)
        xc = x - mu
        var = jnp.mean(xc * xc, axis=1, keepdims=True)
        o_ref[...] = xc * lax.rsqrt(var + LN_EPS) * g_ref[...] + b_ref[...]

    (out,), routs = _call(
        body, grid=(s // ts,), ins=[r, g, b],
        in_specs=[pl.BlockSpec((ts, d), lambda i: (i, 0)), _full_spec(g), _full_spec(b)],
        out_specs=[pl.BlockSpec((ts, d), lambda i: (i, 0))], outs=[jax.ShapeDtypeStruct((s, d), F32)],
        name=name, rider=rider)
    return out if rider is None else (out, routs)


def _ln_bwd(r, dy, g, *, name, rider=None):
    s, d = r.shape
    ts = _seq_tile(s, 512)

    def body(r_ref, dy_ref, g_ref, dr_ref, dg_ref, db_ref, ds_ref):
        @pl.when(pl.program_id(0) == 0)
        def _():
            dg_ref[...] = jnp.zeros_like(dg_ref)
            db_ref[...] = jnp.zeros_like(db_ref)
            ds_ref[...] = jnp.zeros_like(ds_ref)

        x = r_ref[...]
        dy = dy_ref[...]
        mu = jnp.mean(x, axis=1, keepdims=True)
        xc = x - mu
        var = jnp.mean(xc * xc, axis=1, keepdims=True)
        rstd = lax.rsqrt(var + LN_EPS)
        xh = xc * rstd
        dxh = dy * g_ref[...]
        m1 = jnp.mean(dxh, axis=1, keepdims=True)
        m2 = jnp.mean(dxh * xh, axis=1, keepdims=True)
        dr = rstd * (dxh - m1 - xh * m2)
        dr_ref[...] = dr
        dg_ref[...] += _colsum(dy * xh)
        db_ref[...] += _colsum(dy)
        ds_ref[...] += _colsum(dr)

    vec = jax.ShapeDtypeStruct((1, d), F32)
    vspec = pl.BlockSpec((1, d), lambda i: (0, 0))
    outs, routs = _call(
        body, grid=(s // ts,), ins=[r, dy, g],
        in_specs=[pl.BlockSpec((ts, d), lambda i: (i, 0)), pl.BlockSpec((ts, d), lambda i: (i, 0)), _full_spec(g)],
        out_specs=[pl.BlockSpec((ts, d), lambda i: (i, 0)), vspec, vspec, vspec],
        outs=[jax.ShapeDtypeStruct((s, d), F32), vec, vec, vec], name=name, rider=rider)
    return outs if rider is None else (outs, routs)


def _loss_grad(y, target, *, name):
    s, d = y.shape
    ts = _seq_tile(s, 512)

    def body(y_ref, t_ref, dy_ref, l_ref):
        @pl.when(pl.program_id(0) == 0)
        def _():
            l_ref[...] = jnp.zeros_like(l_ref)

        e = y_ref[...] - t_ref[...]
        dy_ref[...] = e * (1.0 / d)
        part = jnp.sum(jnp.sum(e * e, axis=1, keepdims=True), axis=0, keepdims=True) * (0.5 / d)
        l_ref[...] += jnp.broadcast_to(part, l_ref.shape)

    return pl.pallas_call(
        body, grid=(s // ts,),
        in_specs=[pl.BlockSpec((ts, d), lambda i: (i, 0)), pl.BlockSpec((ts, d), lambda i: (i, 0))],
        out_specs=[pl.BlockSpec((ts, d), lambda i: (i, 0)), pl.BlockSpec((SUBLANE, LANE), lambda i: (0, 0))],
        out_shape=[jax.ShapeDtypeStruct((s, d), F32), jax.ShapeDtypeStruct((SUBLANE, LANE), F32)],
        compiler_params=_cparams(1), name=name)(y, target)


SCAN_CHUNK = 32


def _cscan_levels(bufs, apow_ref, t, pad, *, reverse):
    half = bufs[0].shape[1] // 2
    ch = min(SCAN_CHUNK, t)
    nlev = t.bit_length() - 1
    assert (1 << nlev) == t
    for k in range(nlev):
        d = 1 << k
        src, dst = bufs[k % 2], bufs[(k + 1) % 2]

        def chunk(c, carry, src=src, dst=dst, d=d, k=k):
            ar = apow_ref[k:k + 1, :half]
            ai = apow_ref[k:k + 1, half:]
            if reverse:
                ai = -ai
            r0 = pl.multiple_of(c * ch, ch)
            cur = src[pl.ds(pad + r0, ch), :]
            if d >= SUBLANE:
                off = pad + d if reverse else pad - d
                sh = src[pl.ds(off + r0, ch), :]
            elif reverse:
                blk = src[pl.ds(pad + r0, ch + SUBLANE), :]
                sh = pltpu.roll(blk, ch + SUBLANE - d, axis=0)[:ch, :]
            else:
                blk = src[pl.ds(pad - SUBLANE + r0, ch + SUBLANE), :]
                sh = pltpu.roll(blk, d, axis=0)[SUBLANE:, :]
            sre, sim = sh[:, :half], sh[:, half:]
            dst[pl.ds(pad + r0, ch), :half] = cur[:, :half] + ar * sre - ai * sim
            dst[pl.ds(pad + r0, ch), half:] = cur[:, half:] + ar * sim + ai * sre
            return carry

        lax.fori_loop(0, t // ch, chunk, 0)
    return nlev % 2


def _rscan_levels(abufs, bbufs, t, pad, *, reverse):
    nlev = t.bit_length() - 1
    assert (1 << nlev) == t
    for k in range(nlev):
        d = 1 << k
        asrc, adst = abufs[k % 2], abufs[(k + 1) % 2]
        bsrc, bdst = bbufs[k % 2], bbufs[(k + 1) % 2]
        off = pad + d if reverse else pad - d
        a = asrc[pad:pad + t, :]
        bdst[pad:pad + t, :] = a * bsrc[off:off + t, :] + bsrc[pad:pad + t, :]
        if k < nlev - 1:
            adst[pad:pad + t, :] = a * asrc[off:off + t, :]
    return nlev % 2


S5_TILE = 256


def _s5_scan_forward(u_bf, wb_ref, apow_ref, state, bufs, t, pad):
    half = S5_LANES
    bufs[0][pad:pad + t, :] = _dot(u_bf, wb_ref[...])
    ar, ai = apow_ref[0:1, :half], apow_ref[0:1, half:]
    sr, si = state[:, :half], state[:, half:]
    bufs[0][pad:pad + 1, :half] += ar * sr - ai * si
    bufs[0][pad:pad + 1, half:] += ar * si + ai * sr
    return _cscan_levels(bufs, apow_ref, t, pad, reverse=False)


def _s5_fwd(h_in, wb, apow, wc, dvec, wglu, bglu, *, name, rider=None):
    s = h_in.shape[0]
    t = _seq_tile(s, S5_TILE)
    pad = t // 2
    nt = s // t
    lanes2 = 2 * S5_LANES

    def body(u_ref, wb_ref, apow_ref, wc_ref, d_ref, wglu_ref, bglu_ref, out_ref, y1_ref, hb_ref, h_ref, buf0, buf1,
             carry):
        bufs = (buf0, buf1)

        @pl.when(pl.program_id(0) == 0)
        def _():
            buf0[0:pad, :] = jnp.zeros((pad, lanes2), F32)
            buf1[0:pad, :] = jnp.zeros((pad, lanes2), F32)
            carry[...] = jnp.zeros_like(carry)

        u = u_ref[...]
        state = carry[0:1, :]
        hb_ref[0] = state
        fin = _s5_scan_forward(u.astype(BF16), wb_ref, apow_ref, state, bufs, t, pad)
        hbuf = bufs[fin]
        carry[0:1, :] = hbuf[pad + t - 1:pad + t, :]
        h_bf = hbuf[pad:pad + t, :].astype(BF16)
        h_ref[...] = h_bf
        y1 = _dot(h_bf, wc_ref[...]) + d_ref[...] * u
        y1_ref[...] = y1
        y2 = _gelu(y1)
        z = _dot(y2, wglu_ref[...]) + bglu_ref[...]
        out_ref[...] = (y2 * _sigmoid(z)).astype(BF16)

    ins = [h_in, wb, apow, wc, dvec, wglu, bglu]
    in_specs = [pl.BlockSpec((t, D_GROUP), lambda i: (i, COL_S5))] + [_full_spec(a) for a in ins[1:]]
    return _call(
        body, grid=(nt,), ins=ins, in_specs=in_specs,
        out_specs=[pl.BlockSpec((t, D_GROUP), lambda i: (i, MIX_S5)), pl.BlockSpec((t, D_GROUP), lambda i: (i, 0)),
                   pl.BlockSpec((1, 1, lanes2), lambda i: (i, 0, 0)), pl.BlockSpec((t, lanes2), lambda i: (i, 0))],
        outs=[jax.ShapeDtypeStruct((s, D_MODEL), BF16), jax.ShapeDtypeStruct((s, D_GROUP), F32),
              jax.ShapeDtypeStruct((nt, 1, lanes2), F32), jax.ShapeDtypeStruct((s, lanes2), BF16)],
        scratch=[pltpu.VMEM((pad + t, lanes2), F32), pltpu.VMEM((pad + t, lanes2), F32),
                 pltpu.VMEM((SUBLANE, lanes2), F32)],
        name=name, rider=rider)


def _s5_bwd(h_in, y1, dmix, hb, h_all, wb, apow, wc, dvec, wglu, bglu, dh_all, *, name, rider=None):
    s = h_in.shape[0]
    t = _seq_tile(s, S5_TILE)
    pad = t // 2
    nt = s // t
    half = S5_LANES
    lanes2 = 2 * half
    rows = t + pad

    def body(u_ref, y1_ref, do_ref, hb_ref, h_ref, wb_ref, apow_ref, wc_ref, d_ref, wglu_ref, bglu_ref, _dh_in,
             du_ref, dwglu_ref, dwc_ref, dwb_ref, dbglu_ref, dd_ref, da_ref, buf2, buf3, hpbuf, carry):
        @pl.when(pl.program_id(0) == 0)
        def _():
            for bf in (buf2, buf3):
                bf[t:rows, :] = jnp.zeros((pad, lanes2), F32)
            carry[...] = jnp.zeros_like(carry)
            for r in (dwglu_ref, dwc_ref, dwb_ref, dbglu_ref, dd_ref, da_ref):
                r[...] = jnp.zeros_like(r)

        u = u_ref[...]
        u_bf = u.astype(BF16)
        h_bf = h_ref[...]

        y1 = y1_ref[...]
        dout = do_ref[...]
        y2, dgelu = _gelu_and_grad(y1)
        sg = _sigmoid(_dot(y2, wglu_ref[...]) + bglu_ref[...])
        dz = dout * y2 * sg * (1.0 - sg)
        dy2 = dout * sg + _dot_nt(dz, wglu_ref[...])
        dwglu_ref[...] += _dot_tn(y2, dz)
        dbglu_ref[...] += _colsum(dz)
        dy1 = dy2 * dgelu
        dd_ref[...] += _colsum(dy1 * u)
        dy1_bf = dy1.astype(BF16)
        dwc_ref[...] += _dot_tn(h_bf, dy1_bf)

        buf2[0:t, :] = _dot_nt(dy1_bf, wc_ref[...])
        ar, ai = apow_ref[0:1, :half], apow_ref[0:1, half:]
        cr, ci = carry[0:1, :half], carry[0:1, half:]
        buf2[t - 1:t, :half] += ar * cr + ai * ci
        buf2[t - 1:t, half:] += ar * ci - ai * cr
        lfin = _cscan_levels((buf2, buf3), apow_ref, t, 0, reverse=True)
        lbuf = (buf2, buf3)[lfin]
        lam = lbuf[0:t, :]
        carry[0:1, :] = lbuf[0:1, :]
        lam_bf = lam.astype(BF16)
        du_ref[...] = dy1 * d_ref[...] + _dot_nt(lam_bf, wb_ref[...])
        dwb_ref[...] += _dot_tn(u_bf, lam_bf)

        hpbuf[SUBLANE - 1:SUBLANE, :] = hb_ref[0]
        hpbuf[SUBLANE:SUBLANE + t, :] = h_bf.astype(F32)
        hp = hpbuf[SUBLANE - 1:SUBLANE - 1 + t, :]
        lre, lim = lam[:, :half], lam[:, half:]
        hre, him = hp[:, :half], hp[:, half:]
        da_ref[:, :half] += _colsum(lre * hre + lim * him)
        da_ref[:, half:] += _colsum(lim * hre - lre * him)

    def rev(col):
        return lambda i: (nt - 1 - i, col)

    ins = [h_in, y1, dmix, hb, h_all, wb, apow, wc, dvec, wglu, bglu, dh_all]
    in_specs = [pl.BlockSpec((t, D_GROUP), rev(COL_S5)), pl.BlockSpec((t, D_GROUP), rev(0)),
                pl.BlockSpec((t, D_GROUP), rev(MIX_S5)), pl.BlockSpec((1, 1, lanes2), lambda i: (nt - 1 - i, 0, 0)),
                pl.BlockSpec((t, lanes2), rev(0))] + [_full_spec(a) for a in ins[5:11]] + [_ANY]
    outs = [jax.ShapeDtypeStruct((s, N_IN_COLS), F32), jax.ShapeDtypeStruct((D_GROUP, D_GROUP), F32),
            jax.ShapeDtypeStruct((lanes2, D_GROUP), F32), jax.ShapeDtypeStruct((D_GROUP, lanes2), F32),
            jax.ShapeDtypeStruct((1, D_GROUP), F32), jax.ShapeDtypeStruct((1, D_GROUP), F32),
            jax.ShapeDtypeStruct((1, lanes2), F32)]
    out_specs = [pl.BlockSpec((t, D_GROUP), rev(COL_S5))] + [_full_spec(o) for o in outs[1:]]
    return _call(
        body, grid=(nt,), ins=ins, in_specs=in_specs, out_specs=out_specs, outs=outs, aliases={11: 0},
        scratch=[pltpu.VMEM((rows, lanes2), F32), pltpu.VMEM((rows, lanes2), F32),
                 pltpu.VMEM((SUBLANE + t, lanes2), F32), pltpu.VMEM((SUBLANE, lanes2), F32)],
        name=name, rider=rider)


def _s5_param_map(lam_re, lam_im, log_dt, b_re, b_im, c_re, c_im):
    dt = jnp.exp(log_dt)[:, None]
    er = jnp.exp(lam_re * dt)
    a_re, a_im = er * jnp.cos(lam_im * dt), er * jnp.sin(lam_im * dt)
    den = lam_re * lam_re + lam_im * lam_im
    n_re = a_re - 1.0
    k_re = (n_re * lam_re + a_im * lam_im) / den
    k_im = (a_im * lam_re - n_re * lam_im) / den
    bb_re = k_re[..., None] * b_re - k_im[..., None] * b_im
    bb_im = k_re[..., None] * b_im + k_im[..., None] * b_re
    mask_in = _block_mask(S5_GROUPS, S5_CH, S5_STATE)
    mask_out = _block_mask(S5_GROUPS, S5_STATE, S5_CH)

    def blockdiag_in(m):
        return jnp.tile(jnp.transpose(m, (0, 2, 1)).reshape(S5_GROUPS * S5_CH, S5_STATE), (1, S5_GROUPS)) * mask_in

    def blockdiag_out(m):
        return jnp.tile(jnp.transpose(m, (0, 2, 1)).reshape(S5_LANES, S5_CH), (1, S5_GROUPS)) * mask_out

    a = jnp.concatenate([a_re.reshape(1, -1), a_im.reshape(1, -1)], axis=1)
    wb = jnp.concatenate([blockdiag_in(bb_re), blockdiag_in(bb_im)], axis=1)
    wc = jnp.concatenate([blockdiag_out(c_re), -blockdiag_out(c_im)], axis=0)
    return a, wb, wc


def _s5_apow(a, nlev):
    half = S5_LANES
    re, im = a[:, :half], a[:, half:]
    rows = []
    for _ in range(nlev):
        rows.append(jnp.concatenate([re, im], axis=1))
        re, im = re * re - im * im, 2.0 * re * im
    n_rows = -(-nlev // SUBLANE) * SUBLANE
    rows += [jnp.zeros_like(rows[0])] * (n_rows - nlev)
    return lax.stop_gradient(jnp.concatenate(rows, axis=0))


CV_TILE = 256
CV_PAD = 32
CV_CHUNK = 64


def _shifted_copies(buf, shifted, rows):
    n = rows - SUBLANE
    for s in range(1, SUBLANE):
        shifted[s - 1, 0:n, :] = buf[s:s + n, :]


def _window(buf, shifted, o, ch):
    q, s = divmod(o, SUBLANE)
    if s == 0:
        return buf[o:o + ch, :]
    return shifted[s - 1, q * SUBLANE:q * SUBLANE + ch, :]


def _gn_stats(c, mavg):
    mu = _dot_hi(c, mavg)
    cen = c - mu
    var = _dot_hi(cen * cen, mavg)
    rstd = lax.rsqrt(var + LN_EPS)
    return cen * rstd, rstd


def _cv_fwd(h_in, cw, cb, gng, gnb, mavg, wpw, bpw, mix, *, name, rider=None):
    s = h_in.shape[0]
    t = _seq_tile(s, CV_TILE)
    ch = min(CV_CHUNK, t)

    def body(v_ref, g_ref, cw_ref, cb_ref, gng_ref, gnb_ref, mavg_ref, wpw_ref, bpw_ref, _mix_in, out_ref, c_ref, xpad,
             shifted):
        @pl.when(pl.program_id(0) == 0)
        def _():
            xpad[0:CV_PAD, :] = jnp.zeros((CV_PAD, D_GROUP), F32)

        xpad[CV_PAD:CV_PAD + t, :] = v_ref[...] * _sigmoid(g_ref[...])
        _shifted_copies(xpad, shifted, t + CV_PAD)
        for r0 in range(0, t, ch):
            acc = jnp.broadcast_to(cb_ref[...], (ch, D_GROUP))
            for k in range(CONV_WIDTH):
                o = CV_PAD - (CONV_WIDTH - 1) + k + r0
                acc = acc + cw_ref[k:k + 1, :] * _window(xpad, shifted, o, ch)
            c_ref[r0:r0 + ch, :] = acc
        xpad[0:CV_PAD, :] = xpad[t:t + CV_PAD, :]
        xn, _ = _gn_stats(c_ref[...], mavg_ref[...])
        gn = xn * gng_ref[...] + gnb_ref[...]
        out_ref[...] = (_dot(gn * _sigmoid(gn), wpw_ref[...]) + bpw_ref[...]).astype(BF16)

    ins = [h_in, h_in, cw, cb, gng, gnb, mavg, wpw, bpw, mix]
    in_specs = [pl.BlockSpec((t, D_GROUP), lambda i: (i, COL_CV_V)), pl.BlockSpec((t, D_GROUP), lambda i: (i, COL_CV_G))] + \
               [_full_spec(a) for a in ins[2:9]] + [_ANY]
    return _call(
        body, grid=(s // t,), ins=ins, in_specs=in_specs,
        out_specs=[pl.BlockSpec((t, D_GROUP), lambda i: (i, MIX_CV)), pl.BlockSpec((t, D_GROUP), lambda i: (i, 0))],
        outs=[jax.ShapeDtypeStruct((s, D_MODEL), BF16), jax.ShapeDtypeStruct((s, D_GROUP), F32)],
        aliases={9: 0},
        scratch=[pltpu.VMEM((CV_PAD + t, D_GROUP), F32), pltpu.VMEM((SUBLANE - 1, CV_PAD + t, D_GROUP), F32)],
        name=name, rider=rider)


def _cv_bwd(h_in, c, dmix, cw, gng, gnb, mavg, wpw, *, name, rider=None):
    s = h_in.shape[0]
    t = _seq_tile(s, CV_TILE)
    nt = s // t
    ch = min(CV_CHUNK, t)

    def body(v_ref, g_ref, c_ref, do_ref, cw_ref, gng_ref, gnb_ref, mavg_ref, wpw_ref,
             dvg_ref, dwpw_ref, dcw_ref, dbpw_ref, dgg_ref, dgb_ref, dcb_ref, dcpad, hgbuf, shifted):
        @pl.when(pl.program_id(0) == 0)
        def _():
            dcpad[t:t + CV_PAD, :] = jnp.zeros((CV_PAD, D_GROUP), F32)
            for r in (dwpw_ref, dcw_ref, dbpw_ref, dgg_ref, dgb_ref, dcb_ref):
                r[...] = jnp.zeros_like(r)

        mavg = mavg_ref[...]
        xn, rstd = _gn_stats(c_ref[...], mavg)
        gg = gng_ref[...]
        gn = xn * gg + gnb_ref[...]
        sg = _sigmoid(gn)
        dout = do_ref[...]
        dwpw_ref[...] += _dot_tn(gn * sg, dout)
        dbpw_ref[...] += _colsum(dout)
        dgn = _dot_nt(dout, wpw_ref[...]) * (sg * (1.0 + gn * (1.0 - sg)))
        dgg_ref[...] += _colsum(dgn * xn)
        dgb_ref[...] += _colsum(dgn)
        dxn = dgn * gg
        dc = rstd * (dxn - _dot_hi(dxn, mavg) - xn * _dot_hi(dxn * xn, mavg))
        dcb_ref[...] += _colsum(dc)
        dcpad[0:t, :] = dc

        v = v_ref[...]
        sgm = _sigmoid(g_ref[...])
        hgbuf[...] = v * sgm
        _shifted_copies(dcpad, shifted, t + CV_PAD)
        for r0 in range(0, t, ch):
            hg = hgbuf[r0:r0 + ch, :]
            acc = jnp.zeros((ch, D_GROUP), F32)
            for k in range(CONV_WIDTH):
                o = (CONV_WIDTH - 1) - k + r0
                sh = _window(dcpad, shifted, o, ch)
                acc = acc + cw_ref[k:k + 1, :] * sh
                dcw_ref[k:k + 1, :] += _colsum(hg * sh)
            hgbuf[r0:r0 + ch, :] = acc
        dcpad[t:t + CV_PAD, :] = dcpad[0:CV_PAD, :]
        dhg = hgbuf[...]
        dvg_ref[:, :D_GROUP] = dhg * sgm
        dvg_ref[:, D_GROUP:] = dhg * v * sgm * (1.0 - sgm)

    def rev(col):
        return lambda i: (nt - 1 - i, col)

    ins = [h_in, h_in, c, dmix, cw, gng, gnb, mavg, wpw]
    in_specs = [pl.BlockSpec((t, D_GROUP), rev(COL_CV_V)), pl.BlockSpec((t, D_GROUP), rev(COL_CV_G)),
                pl.BlockSpec((t, D_GROUP), rev(0)), pl.BlockSpec((t, D_GROUP), rev(MIX_CV))] + [_full_spec(a) for a in ins[4:]]
    vec = jax.ShapeDtypeStruct((1, D_GROUP), F32)
    outs = [jax.ShapeDtypeStruct((s, N_IN_COLS), F32),
            jax.ShapeDtypeStruct((D_GROUP, D_GROUP), F32), jax.ShapeDtypeStruct((CV_PAD, D_GROUP), F32), vec, vec, vec, vec]
    out_specs = [pl.BlockSpec((t, 2 * D_GROUP), rev(COL_CV_V // 2))] + [_full_spec(o) for o in outs[1:]]
    return _call(
        body, grid=(nt,), ins=ins, in_specs=in_specs, out_specs=out_specs, outs=outs,
        scratch=[pltpu.VMEM((t + CV_PAD, D_GROUP), F32), pltpu.VMEM((t, D_GROUP), F32),
                 pltpu.VMEM((SUBLANE - 1, t + CV_PAD, D_GROUP), F32)], name=name, rider=rider)


LRU_TILE = 256


def _lru_gates(xc, wr_ref, br_ref, wi_ref, bi_ref, sp_ref):
    r = _sigmoid(_dot(xc, wr_ref[...]) + br_ref[...])
    i = _sigmoid(_dot(xc, wi_ref[...]) + bi_ref[...])
    log_a = -LRU_C * r * sp_ref[...]
    a = jnp.exp(log_a)
    m = jnp.sqrt(_neg_expm1(2.0 * log_a))
    return r, i, a, m


def _lru_fwd(h_in, lcw, lcb, wr, br, wi, bi, sp, mix, *, name):
    s = h_in.shape[0]
    t = _seq_tile(s, LRU_TILE)
    pad = max(t // 2, SUBLANE)

    def body(xg_ref, xr_ref, lcw_ref, lcb_ref, wr_ref, br_ref, wi_ref, bi_ref, sp_ref, _mix_in,
             out_ref, xc_ref, h_ref, xpad, a0, a1, b0, b1, carry):
        @pl.when(pl.program_id(0) == 0)
        def _():
            xpad[0:SUBLANE, :] = jnp.zeros((SUBLANE, D_GROUP), F32)
            for bf in (a0, a1, b0, b1):
                bf[0:pad, :] = jnp.zeros((pad, D_GROUP), F32)
            carry[...] = jnp.zeros_like(carry)

        xpad[SUBLANE:SUBLANE + t, :] = xr_ref[...]
        xc = jnp.broadcast_to(lcb_ref[...], (t, D_GROUP))
        for k in range(LRU_CONV_WIDTH):
            o = SUBLANE - (LRU_CONV_WIDTH - 1) + k
            xc = xc + lcw_ref[k:k + 1, :] * xpad[o:o + t, :]
        xpad[0:SUBLANE, :] = xpad[t:t + SUBLANE, :]
        xc_ref[...] = xc
        _, i, a, m = _lru_gates(xc, wr_ref, br_ref, wi_ref, bi_ref, sp_ref)
        a0[pad:pad + t, :] = a
        b0[pad:pad + t, :] = m * (i * xc)
        b0[pad:pad + 1, :] += a0[pad:pad + 1, :] * carry[0:1, :]
        fin = _rscan_levels((a0, a1), (b0, b1), t, pad, reverse=False)
        hbuf = (b0, b1)[fin]
        carry[0:1, :] = hbuf[pad + t - 1:pad + t, :]
        h = hbuf[pad:pad + t, :]
        h_ref[...] = h
        out_ref[...] = (h * _gelu(xg_ref[...])).astype(BF16)

    ins = [h_in, h_in, lcw, lcb, wr, br, wi, bi, sp, mix]
    row = pl.BlockSpec((t, D_GROUP), lambda i: (i, 0))
    in_specs = [pl.BlockSpec((t, D_GROUP), lambda i: (i, COL_LRU_G)), pl.BlockSpec((t, D_GROUP), lambda i: (i, COL_LRU_X))] + \
               [_full_spec(a) for a in ins[2:9]] + [_ANY]
    return pl.pallas_call(
        body, grid=(s // t,), in_specs=in_specs,
        out_specs=[pl.BlockSpec((t, D_GROUP), lambda i: (i, MIX_LRU)), row, row],
        out_shape=[jax.ShapeDtypeStruct((s, D_MODEL), BF16)] + [jax.ShapeDtypeStruct((s, D_GROUP), F32)] * 2,
        input_output_aliases={9: 0},
        scratch_shapes=[pltpu.VMEM((SUBLANE + t, D_GROUP), F32)] + [pltpu.VMEM((pad + t, D_GROUP), F32)] * 4 +
                       [pltpu.VMEM((SUBLANE, D_GROUP), F32)],
        compiler_params=_cparams(1), name=name)(*ins)


def _lru_bwd(h_in, xc_all, h_all, dmix, lcw, wr, br, wi, bi, sp, dh_all, *, name):
    s = h_in.shape[0]
    t = _seq_tile(s, LRU_TILE)
    nt = s // t
    pad = max(t // 2, SUBLANE)
    tb = t // SUBLANE

    def body(xg_ref, xr_ref, xc_ref, h_ref, hprev_ref, do_ref, lcw_ref, wr_ref, br_ref, wi_ref, bi_ref, sp_ref, _dh_in,
             dgr_ref, dwr_ref, dwi_ref, dlcw_ref, dbr_ref, dbi_ref, dsp_ref, dlcb_ref,
             a0, a1, b0, b1, hp, dxpad, carry):
        pid = pl.program_id(0)

        @pl.when(pid == 0)
        def _():
            for bf in (a0, a1, b0, b1):
                bf[pad + t:pad + t + pad, :] = jnp.zeros((pad, D_GROUP), F32)
            dxpad[t:t + SUBLANE, :] = jnp.zeros((SUBLANE, D_GROUP), F32)
            carry[...] = jnp.zeros_like(carry)
            for r in (dwr_ref, dwi_ref, dlcw_ref, dbr_ref, dbi_ref, dsp_ref, dlcb_ref):
                r[...] = jnp.zeros_like(r)

        xc = xc_ref[...]
        h = h_ref[...]
        dout = do_ref[...]
        gate, dgate = _gelu_and_grad(xg_ref[...])
        dgr_ref[:, :D_GROUP] = dout * h * dgate
        r, i, a, m = _lru_gates(xc, wr_ref, br_ref, wi_ref, bi_ref, sp_ref)

        a0[pad:pad + t, :] = a
        b0[pad:pad + t, :] = dout * gate
        b0[pad + t - 1:pad + t, :] += carry[0:1, :]
        a1[pad:pad + t, :] = a0[pad + 1:pad + 1 + t, :]
        fin = _rscan_levels((a1, a0), (b0, b1), t, pad, reverse=True)
        lam = (b0, b1)[fin][pad:pad + t, :]
        carry[0:1, :] = a[0:1, :] * lam[0:1, :]

        is_first = pid == nt - 1
        hp[0:SUBLANE, :] = jnp.where(is_first, 0.0, hprev_ref[...])
        hp[SUBLANE:SUBLANE + t, :] = h
        hprev = hp[SUBLANE - 1:SUBLANE - 1 + t, :]

        ix = i * xc
        dmm = lam * ix
        dix = lam * m
        da = lam * hprev - dmm * (a / m)
        dlog_a = da * a
        dr = dlog_a * (-LRU_C * sp_ref[...])
        dsp_ref[...] += _colsum(dlog_a * (-LRU_C * r))
        dpr = dr * r * (1.0 - r)
        dpi = dix * xc * i * (1.0 - i)
        dbr_ref[...] += _colsum(dpr)
        dbi_ref[...] += _colsum(dpi)
        dwr_ref[...] += _dot_tn(xc, dpr)
        dwi_ref[...] += _dot_tn(xc, dpi)
        dxc = dix * i + _dot_nt(dpr, wr_ref[...]) + _dot_nt(dpi, wi_ref[...])
        dlcb_ref[...] += _colsum(dxc)

        dxpad[0:t, :] = dxc
        xr = xr_ref[...]
        dxr = jnp.zeros((t, D_GROUP), F32)
        for k in range(LRU_CONV_WIDTH):
            o = (LRU_CONV_WIDTH - 1) - k
            sh = dxpad[o:o + t, :]
            dxr = dxr + lcw_ref[k:k + 1, :] * sh
            dlcw_ref[k:k + 1, :] += _colsum(xr * sh)
        dxpad[t:t + SUBLANE, :] = dxpad[0:SUBLANE, :]
        dgr_ref[:, D_GROUP:] = dxr

    def rev(col):
        return lambda i: (nt - 1 - i, col)

    ins = [h_in, h_in, xc_all, h_all, h_all, dmix, lcw, wr, br, wi, bi, sp, dh_all]
    in_specs = [pl.BlockSpec((t, D_GROUP), rev(COL_LRU_G)), pl.BlockSpec((t, D_GROUP), rev(COL_LRU_X)),
                pl.BlockSpec((t, D_GROUP), rev(0)), pl.BlockSpec((t, D_GROUP), rev(0)),
                pl.BlockSpec((SUBLANE, D_GROUP), lambda i: (jnp.maximum((nt - 1 - i) * tb - 1, 0), 0)),
                pl.BlockSpec((t, D_GROUP), rev(MIX_LRU))] + [_full_spec(a) for a in ins[6:12]] + [_ANY]
    vec = jax.ShapeDtypeStruct((1, D_GROUP), F32)
    mat = jax.ShapeDtypeStruct((D_GROUP, D_GROUP), F32)
    outs = [jax.ShapeDtypeStruct((s, N_IN_COLS), F32), mat, mat, jax.ShapeDtypeStruct((SUBLANE, D_GROUP), F32),
            vec, vec, vec, vec]
    out_specs = [pl.BlockSpec((t, 2 * D_GROUP), rev(COL_LRU_G // 2))] + [_full_spec(o) for o in outs[1:]]
    return pl.pallas_call(
        body, grid=(nt,), in_specs=in_specs, out_specs=out_specs, out_shape=outs, input_output_aliases={12: 0},
        scratch_shapes=[pltpu.VMEM((pad + t + pad, D_GROUP), F32)] * 4 +
                       [pltpu.VMEM((SUBLANE + t, D_GROUP), F32), pltpu.VMEM((t + SUBLANE, D_GROUP), F32),
                        pltpu.VMEM((SUBLANE, D_GROUP), F32)],
        compiler_params=_cparams(1), name=name)(*ins)


def _blockdiag(w):
    h, d, _ = w.shape
    return jnp.tile(w.reshape(h * d, d), (1, h)) * _block_mask(h, d, d)


ATTN_TILE = 512
ATTN_SCALE = ATTN_HEAD_DIM ** -0.5


def _attn_big(kv):
    m = kv.shape[0]
    kbig = jnp.tile(kv[:, :D_GROUP].T, (1, ATTN_HEADS)) * _block_mask(ATTN_HEADS, ATTN_HEAD_DIM, m)
    vbig = jnp.tile(kv[:, D_GROUP:], (ATTN_HEADS, 1)) * _block_mask(ATTN_HEADS, m, ATTN_HEAD_DIM)
    return kbig, vbig


def _attn_probs(q, kbig_ref, m):
    sc = _dot(q, kbig_ref[...]) * ATTN_SCALE
    ps = []
    for h in range(ATTN_HEADS):
        sh = sc[:, h * m:(h + 1) * m]
        e = jnp.exp(sh - jnp.max(sh, axis=1, keepdims=True))
        ps.append(e / jnp.sum(e, axis=1, keepdims=True))
    return ps


def _attn_fwd(h_in, kbig, vbig, mix, *, name):
    s = h_in.shape[0]
    t = _seq_tile(s, ATTN_TILE)
    m = kbig.shape[1] // ATTN_HEADS

    def body(q_ref, kbig_ref, vbig_ref, _mix_in, o_ref):
        ps = _attn_probs(q_ref[...], kbig_ref, m)
        o_ref[...] = _dot(jnp.concatenate(ps, axis=1), vbig_ref[...]).astype(BF16)

    return pl.pallas_call(
        body, grid=(s // t,),
        in_specs=[pl.BlockSpec((t, D_GROUP), lambda i: (i, COL_Q)), _full_spec(kbig), _full_spec(vbig), _ANY],
        out_specs=pl.BlockSpec((t, D_GROUP), lambda i: (i, MIX_ATTN)),
        out_shape=jax.ShapeDtypeStruct((s, D_MODEL), BF16), input_output_aliases={3: 0},
        compiler_params=_cparams(1), name=name)(h_in, kbig, vbig, mix)


def _attn_bwd(h_in, dmix, kbig, vbig, dh_all, *, name):
    s = h_in.shape[0]
    t = _seq_tile(s, ATTN_TILE)
    m = kbig.shape[1] // ATTN_HEADS

    def body(q_ref, do_ref, kbig_ref, vbig_ref, _dh_in, dq_ref, dk_ref, dv_ref):
        @pl.when(pl.program_id(0) == 0)
        def _():
            dk_ref[...] = jnp.zeros_like(dk_ref)
            dv_ref[...] = jnp.zeros_like(dv_ref)

        q = q_ref[...]
        dout = do_ref[...]
        ps = _attn_probs(q, kbig_ref, m)
        dp = _dot_nt(dout, vbig_ref[...])
        dss = []
        for h in range(ATTN_HEADS):
            dph = dp[:, h * m:(h + 1) * m]
            dss.append(ps[h] * (dph - jnp.sum(dph * ps[h], axis=1, keepdims=True)))
        ds = (jnp.concatenate(dss, axis=1) * ATTN_SCALE).astype(BF16)
        dv_ref[...] += _dot_tn(jnp.concatenate(ps, axis=1), dout)
        dq_ref[...] = _dot_nt(ds, kbig_ref[...])
        dk_ref[...] += _dot_tn(q, ds)

    outs = [jax.ShapeDtypeStruct((s, N_IN_COLS), F32), jax.ShapeDtypeStruct(kbig.shape, F32),
            jax.ShapeDtypeStruct(vbig.shape, F32)]
    return pl.pallas_call(
        body, grid=(s // t,),
        in_specs=[pl.BlockSpec((t, D_GROUP), lambda i: (i, COL_Q)), pl.BlockSpec((t, D_GROUP), lambda i: (i, MIX_ATTN)),
                  _full_spec(kbig), _full_spec(vbig), _ANY],
        out_specs=[pl.BlockSpec((t, D_GROUP), lambda i: (i, COL_Q)), _full_spec(outs[1]), _full_spec(outs[2])],
        out_shape=outs, input_output_aliases={4: 0},
        compiler_params=_cparams(1), name=name)(h_in, dmix, kbig, vbig, dh_all)


FFN_TILE = 128
FFN_COL_CHUNK = 256
FFN_ROW_CHUNK = 64


def _ffn_conv(pad_ref, w_ref, b_ref, r0, ch, c0):
    cc = FFN_COL_CHUNK
    acc = jnp.broadcast_to(b_ref[:, c0:c0 + cc], (ch, cc))
    for k in range(FFN_CONV_WIDTH):
        o = SUBLANE - (FFN_CONV_WIDTH - 1) + k + r0
        acc = acc + w_ref[k:k + 1, c0:c0 + cc] * pad_ref[o:o + ch, c0:c0 + cc]
    return acc


def _ffn_gate_fwd(u, fcw, fcb, *, name, rider=None):
    s = u.shape[0]
    t = _seq_tile(s, FFN_TILE)
    ch = min(FFN_ROW_CHUNK, t)
    cc = FFN_COL_CHUNK

    def body(u_ref, w_ref, b_ref, o_ref, uc_ref, upad):
        @pl.when(pl.program_id(0) == 0)
        def _():
            upad[0:SUBLANE, :] = jnp.zeros((SUBLANE, 2 * D_FF), F32)

        upad[SUBLANE:SUBLANE + t, :] = u_ref[...].astype(F32)
        for c0 in range(0, D_FF, cc):
            for r0 in range(0, t, ch):
                val = _ffn_conv(upad, w_ref, b_ref, r0, ch, c0)
                gt = _ffn_conv(upad, w_ref, b_ref, r0, ch, c0 + D_FF)
                o_ref[r0:r0 + ch, c0:c0 + cc] = (val * _gelu(gt)).astype(BF16)
                uc_ref[r0:r0 + ch, c0:c0 + cc] = val.astype(BF16)
                uc_ref[r0:r0 + ch, c0 + D_FF:c0 + D_FF + cc] = gt.astype(BF16)
        upad[0:SUBLANE, :] = upad[t:t + SUBLANE, :]

    return _call(
        body, grid=(s // t,), ins=[u, fcw, fcb],
        in_specs=[pl.BlockSpec((t, 2 * D_FF), lambda i: (i, 0)), _full_spec(fcw), _full_spec(fcb)],
        out_specs=[pl.BlockSpec((t, D_FF), lambda i: (i, 0)), pl.BlockSpec((t, 2 * D_FF), lambda i: (i, 0))],
        outs=[jax.ShapeDtypeStruct((s, D_FF), BF16), jax.ShapeDtypeStruct((s, 2 * D_FF), BF16)],
        scratch=[pltpu.VMEM((SUBLANE + t, 2 * D_FF), F32)], name=name, rider=rider)


def _ffn_gate_bwd(u, uc, dh, fcw, *, name, rider=None):
    s = u.shape[0]
    t = _seq_tile(s, FFN_TILE)
    nt = s // t
    ch = min(FFN_ROW_CHUNK, t)
    cc = FFN_COL_CHUNK

    def body(u_ref, uc_ref, dh_ref, w_ref, du_ref, dw_ref, db_ref, dpad):
        @pl.when(pl.program_id(0) == 0)
        def _():
            dpad[t:t + SUBLANE, :] = jnp.zeros((SUBLANE, 2 * D_FF), F32)
            dw_ref[...] = jnp.zeros_like(dw_ref)
            db_ref[...] = jnp.zeros_like(db_ref)

        for c0 in range(0, D_FF, cc):
            for r0 in range(0, t, ch):
                val = uc_ref[r0:r0 + ch, c0:c0 + cc].astype(F32)
                gt = uc_ref[r0:r0 + ch, c0 + D_FF:c0 + D_FF + cc].astype(F32)
                gl, dgl = _gelu_and_grad(gt)
                d = dh_ref[r0:r0 + ch, c0:c0 + cc].astype(F32)
                dpad[r0:r0 + ch, c0:c0 + cc] = d * gl
                dpad[r0:r0 + ch, c0 + D_FF:c0 + D_FF + cc] = d * val * dgl
        for c0 in range(0, 2 * D_FF, cc):
            dbs = jnp.zeros((1, cc), F32)
            dws = [jnp.zeros((1, cc), F32) for _ in range(FFN_CONV_WIDTH)]
            for r0 in range(0, t, ch):
                x = u_ref[r0:r0 + ch, c0:c0 + cc].astype(F32)
                acc = jnp.zeros((ch, cc), F32)
                for k in range(FFN_CONV_WIDTH):
                    o = (FFN_CONV_WIDTH - 1) - k + r0
                    sh = dpad[o:o + ch, c0:c0 + cc]
                    acc = acc + w_ref[k:k + 1, c0:c0 + cc] * sh
                    dws[k] = dws[k] + _colsum(x * sh)
                    if k == FFN_CONV_WIDTH - 1:
                        dbs = dbs + _colsum(sh)
                du_ref[r0:r0 + ch, c0:c0 + cc] = acc.astype(BF16)
            db_ref[:, c0:c0 + cc] += dbs
            for k in range(FFN_CONV_WIDTH):
                dw_ref[k:k + 1, c0:c0 + cc] += dws[k]
        dpad[t:t + SUBLANE, :] = dpad[0:SUBLANE, :]

    outs = [jax.ShapeDtypeStruct((s, 2 * D_FF), BF16), jax.ShapeDtypeStruct((SUBLANE, 2 * D_FF), F32),
            jax.ShapeDtypeStruct((1, 2 * D_FF), F32)]
    return _call(
        body, grid=(nt,), ins=[u, uc, dh, fcw],
        in_specs=[pl.BlockSpec((t, 2 * D_FF), lambda i: (nt - 1 - i, 0)),
                  pl.BlockSpec((t, 2 * D_FF), lambda i: (nt - 1 - i, 0)),
                  pl.BlockSpec((t, D_FF), lambda i: (nt - 1 - i, 0)), _full_spec(fcw)],
        out_specs=[pl.BlockSpec((t, 2 * D_FF), lambda i: (nt - 1 - i, 0)), _full_spec(outs[1]), _full_spec(outs[2])],
        outs=outs, scratch=[pltpu.VMEM((t + SUBLANE, 2 * D_FF), F32)], name=name, rider=rider)


def _adamw_body(g_ref, w_ref, m_ref, v_ref, go_ref, d_ref, mo_ref, vo_ref):
    inv_b1 = 1.0 - ADAM_B1 ** ADAM_STEP
    inv_b2 = 1.0 - ADAM_B2 ** ADAM_STEP
    g = g_ref[0]
    for dev in range(1, N_DEV):
        g = g + g_ref[dev]
    go_ref[...] = g
    mn = ADAM_B1 * m_ref[...] + (1.0 - ADAM_B1) * g
    vn = ADAM_B2 * v_ref[...] + (1.0 - ADAM_B2) * (g * g)
    mo_ref[...] = mn
    vo_ref[...] = vn
    d_ref[...] = -ADAM_LR * ((mn / inv_b1) / (jnp.sqrt(vn / inv_b2) + ADAM_EPS) + ADAM_WD * w_ref[...])


def _adamw(gstack, w, m, v, *, name):
    _, r, c = gstack.shape
    tr = _pick_rows(r, PACK_ROW_BLOCK)

    def body(*refs):
        _adamw_body(*refs)

    blk = pl.BlockSpec((tr, c), lambda i: (i, 0))
    sh = jax.ShapeDtypeStruct((r, c), F32)
    return pl.pallas_call(
        body, grid=(r // tr,),
        in_specs=[pl.BlockSpec((N_DEV, tr, c), lambda i: (0, i, 0)), blk, blk, blk],
        out_specs=[blk] * 4, out_shape=[sh] * 4,
        compiler_params=_cparams(1), name=name)(gstack, w, m, v)


def _adamw_layer(gstack, w, m, v, layer, into, *, name):
    n_layers, r, c = w.shape
    tr = _pick_rows(r, PACK_ROW_BLOCK)

    def body(g_ref, w_ref, m_ref, v_ref, *rest):
        _adamw_body(g_ref, w_ref, m_ref, v_ref, *rest[-4:])

    blk = pl.BlockSpec((None, tr, c), lambda i: (layer, i, 0))
    sh = jax.ShapeDtypeStruct((n_layers, r, c), F32)
    into = list(into or [])
    return pl.pallas_call(
        body, grid=(r // tr,),
        in_specs=[pl.BlockSpec((N_DEV, tr, c), lambda i: (0, i, 0)), blk, blk, blk] + [_ANY] * len(into),
        out_specs=[blk] * 4, out_shape=[sh] * 4, input_output_aliases={4 + k: k for k in range(len(into))},
        compiler_params=_cparams(1), name=name)(gstack, w, m, v, *into)


def _exchange(rider, *, name):
    n = rider.n

    def body(*refs):
        x_refs, out_refs, sems = refs[:n], refs[n:2 * n], refs[2 * n:]
        rider.start(x_refs, out_refs, sems)
        rider.wait(x_refs, out_refs, sems)

    return pl.pallas_call(
        body, in_specs=[_ANY] * n, out_specs=[_ANY] * n, out_shape=rider.out_shapes(),
        scratch_shapes=rider.scratch(), name=name)(*rider.srcs)


def _pack_rows(n):
    rows = -(-n // PACK_COLS)
    return -(-rows // SUBLANE) * SUBLANE


def _pack(arrs, dtype):
    flat = jnp.concatenate([a.reshape(-1).astype(dtype) for a in arrs])
    rows = _pack_rows(flat.shape[0])
    flat = jnp.pad(flat, (0, rows * PACK_COLS - flat.shape[0]))
    return flat.reshape(rows, PACK_COLS)


def _pack_lead(arrs, dtype):
    flat = jnp.concatenate([a.reshape(N_DEV, -1).astype(dtype) for a in arrs], axis=1)
    rows = _pack_rows(flat.shape[1])
    flat = jnp.pad(flat, ((0, 0), (0, rows * PACK_COLS - flat.shape[1])))
    return flat.reshape(N_DEV, rows, PACK_COLS)


def _pack_layers(arrs, dtype):
    n_layers = arrs[0].shape[0]
    flat = jnp.concatenate([a.reshape(n_layers, -1).astype(dtype) for a in arrs], axis=1)
    rows = _pack_rows(flat.shape[1])
    flat = jnp.pad(flat, ((0, 0), (0, rows * PACK_COLS - flat.shape[1])))
    return flat.reshape(n_layers, rows, PACK_COLS)


def _unpack_layers(packed, shapes):
    flat = packed.reshape(packed.shape[0], -1)
    out, pos = [], 0
    for sh in shapes:
        n = math.prod(sh[1:])
        out.append(flat[:, pos:pos + n].reshape(sh))
        pos += n
    return out


def _unpack(packed, shapes, lead=False):
    flat = packed.reshape(N_DEV, -1) if lead else packed.reshape(-1)
    out, pos = [], 0
    for sh in shapes:
        n = math.prod(sh)
        out.append(flat[:, pos:pos + n].reshape((N_DEV,) + tuple(sh)) if lead else flat[pos:pos + n].reshape(sh))
        pos += n
    return out


def _join_shards(stacked, axis):
    return jnp.concatenate([stacked[d] for d in range(N_DEV)], axis=axis)


def _split_shards(full, axis):
    return jnp.stack(jnp.split(full, N_DEV, axis=axis), axis=0)


def _perm_in_cols(a, inverse=False):
    blocks = jnp.split(a, 6, axis=-1)
    if inverse:
        order = [IN_PERM.index(j) for j in range(6)]
    else:
        order = list(IN_PERM)
    return jnp.concatenate([blocks[j] for j in order], axis=-1)


def _row(v):
    return v.reshape(1, -1)


def _pad_rows(w, rows):
    return jnp.pad(w, ((0, rows - w.shape[0]), (0, 0)))


def _gn_avg_matrix():
    return _block_mask(GN_GROUPS, D_GROUP // GN_GROUPS, D_GROUP // GN_GROUPS) / (D_GROUP // GN_GROUPS)


def _layer_params(p, l):
    q = {}
    (a, wb, wc), q["s5_vjp"] = jax.vjp(_s5_param_map, p["s5_lam_re"][l], p["s5_lam_im"][l], p["s5_log_dt"][l],
                                       p["s5_b_re"][l], p["s5_b_im"][l], p["s5_c_re"][l], p["s5_c_im"][l])
    q["wb"], q["wc"] = wb.astype(BF16), wc.astype(BF16)
    q["apow"] = _s5_apow(a, max(S5_TILE.bit_length() - 1, 1))
    (q["wr"], q["wi"]), q["lru_w_vjp"] = jax.vjp(lambda r, i: (_blockdiag(r), _blockdiag(i)), p["lru_w_r"][l], p["lru_w_i"][l])
    q["wr"], q["wi"] = q["wr"].astype(BF16), q["wi"].astype(BF16)
    q["sp"], q["sp_vjp"] = jax.vjp(lambda lam: _row(jax.nn.softplus(-lam)), p["lru_lam"][l])
    return q


WEIGHT_RIDES = {(0, "ln_in_fwd"): [("w_in", 0), ("small_pack", 0)],
                (0, "inproj"): [("attn_w_kv", 0), ("w_out", 0)], (0, "s5_fwd"): [("ffn_w_up", 0)],
                (0, "cv_fwd"): [("ffn_w_down", 0)],
                (0, "ffn_up"): [("w_in", 1), ("attn_w_kv", 1), ("w_out", 1), ("ffn_w_down", 1)],
                (0, "ffn_gate_fwd"): [("ffn_w_up", 1)]}
GRAD_RIDES = {(1, "ffn_gate_bwd"): [("ffn_w_down", 1)],
              (0, "dw_down"): [("w_out", 1), ("attn_w_kv", 1), ("w_in", 1)],
              (0, "dhff"): [("rep", 1), ("ssh", 1)],
              (0, "dw_in"): [("attn_w_kv", 0), ("ssh", 0)],
              (0, "ln_in_bwd"): [("rep", 0)],
              (0, "ffn_gate_bwd"): [("ffn_w_up", 1)],
              (0, "dx1"): [("ffn_w_down", 0)],
              (0, "cv_bwd"): [("w_out", 0)],
              (0, "s5_bwd"): [("ffn_w_up", 0)],
              (0, "dxs"): [("w_in", 0)]}


def _assemble_weight(n, gathered):
    if SHARDED[n] == 2:
        full = jnp.transpose(gathered, (1, 0, 2)).reshape(gathered.shape[1], -1)
        return _perm_in_cols(full) if n == "w_in" else full
    return gathered.reshape(-1, gathered.shape[-1])


def _grad_source(n, g):
    if SHARDED[n] == 2:
        if n == "w_in":
            g = _perm_in_cols(g, inverse=True)
        k, nn = g.shape
        return jnp.transpose(g.reshape(k, N_DEV, nn // N_DEV), (1, 0, 2)), "lead"
    return g, "rows"


def _hosted(fn, keys_rider, land, *args, **kw):
    keys, rider = keys_rider
    if rider is None:
        return fn(*args, **kw)
    out, routs = fn(*args, rider=rider, **kw)
    land(keys, routs)
    return out


def _local_step(x, mem, target, p, big_w, shards=None, unpack_small=None):
    dist = shards is not None
    small, saved = {}, []
    big_g, ready, recv = {}, {}, {}
    mavg = _gn_avg_matrix()

    def weight_rider(l, host):
        keys = WEIGHT_RIDES.get((l, host), []) if dist else []
        return keys, (_Rider([shards[n][ll] for n, ll in keys], ["all"] * len(keys)) if keys else None)

    def land_weights(keys, routs):
        for (n, ll), r in zip(keys, routs):
            if n == "small_pack":
                p.update(unpack_small(r))
            else:
                big_w[n][ll] = _assemble_weight(n, r)

    def grad_rider(l, host):
        keys = [k for k in GRAD_RIDES.get((l, host), []) if k in ready] if dist else []
        return keys, (_Rider([ready[k][0] for k in keys], [ready[k][1] for k in keys]) if keys else None)

    def land_grads(keys, routs):
        for k, r in zip(keys, routs):
            recv[k] = r
            del ready[k]

    def big_grad(n, l, g):
        if dist:
            ready[(n, l)] = _grad_source(n, g)
        else:
            big_g[(n, l)] = g

    xs = _hosted(_ln_fwd, weight_rider(0, "ln_in_fwd"), land_weights, x, _row(p["ln_in_g"]), _row(p["ln_in_b"]),
                 name="ln_in_fwd")
    for l in range(DEPTH):
        q = _layer_params(p, l)
        n = f"l{l}_"
        hin = _hosted(_mm, weight_rider(l, "inproj"), land_weights, xs, big_w["w_in"][l], bias=_row(p["b_in"][l]),
                      name=n + "inproj")
        keys, rd = weight_rider(l, "s5_fwd")
        (mix, s5_y1, s5_hb, s5_h), routs = _s5_fwd(hin, q["wb"], q["apow"], q["wc"], _row(p["s5_d"][l]),
                                                   p["s5_w_glu"][l], _row(p["s5_b_glu"][l]), name=n + "s5_fwd", rider=rd)
        land_weights(keys, routs)
        cvw = _pad_rows(p["cv_w"][l], CV_PAD)
        keys, rd = weight_rider(l, "cv_fwd")
        (mix, cv_c), routs = _cv_fwd(hin, cvw, _row(p["cv_b"][l]), _row(p["cv_gn_g"][l]), _row(p["cv_gn_b"][l]), mavg,
                                     p["cv_w_pw"][l], _row(p["cv_b_pw"][l]), mix, name=n + "cv_fwd", rider=rd)
        land_weights(keys, routs)
        lcw = _pad_rows(p["lru_conv_w"][l], SUBLANE)
        mix, lru_xc, lru_h = _lru_fwd(hin, lcw, _row(p["lru_conv_b"][l]), q["wr"], _row(p["lru_b_r"][l]), q["wi"],
                                      _row(p["lru_b_i"][l]), q["sp"], mix, name=n + "lru_fwd")
        kv = _mm(mem, big_w["attn_w_kv"][l], name=n + "kv")
        (kbig, vbig), kv_vjp = jax.vjp(_attn_big, kv)
        kbig, vbig = kbig.astype(BF16), vbig.astype(BF16)
        mix = _attn_fwd(hin, kbig, vbig, mix, name=n + "attn_fwd")
        r1 = _mm(mix, big_w["w_out"][l], bias=_row(p["b_out"][l]), res=xs, res_scale=ALPHA, name=n + "outproj")
        x1 = _ln_fwd(r1, _row(p["ln1_g"][l]), _row(p["ln1_b"][l]), name=n + "ln1_fwd")
        u = _hosted(_mm, weight_rider(l, "ffn_up"), land_weights, x1, big_w["ffn_w_up"][l], out_dtype=BF16,
                    name=n + "ffn_up")
        fcw = _pad_rows(p["ffn_conv_w"][l], SUBLANE)
        fcb = _row(p["ffn_conv_b"][l])
        keys, rd = weight_rider(l, "ffn_gate_fwd")
        (hff, uc), routs = _ffn_gate_fwd(u, fcw, fcb, name=n + "ffn_gate_fwd", rider=rd)
        land_weights(keys, routs)
        r2 = _mm(hff, big_w["ffn_w_down"][l], res=x1, res_scale=ALPHA, name=n + "ffn_down")
        x2 = _ln_fwd(r2, _row(p["ln2_g"][l]), _row(p["ln2_b"][l]), name=n + "ln2_fwd")
        saved.append(dict(q=q, xs=xs, hin=hin, s5_y1=s5_y1, s5_hb=s5_hb, s5_h=s5_h, cvw=cvw, cv_c=cv_c, lcw=lcw, lru_xc=lru_xc,
                          lru_h=lru_h, kbig=kbig, vbig=vbig, kv_vjp=kv_vjp, mix=mix, r1=r1, x1=x1, u=u, uc=uc, fcw=fcw,
                          hff=hff, r2=r2))
        xs = x2

    dx, loss_blk = _loss_grad(xs, target, name="loss_grad")
    loss = loss_blk[0, 0]

    for l in reversed(range(DEPTH)):
        sv = saved[l]
        q = sv["q"]
        n = f"l{l}_"
        g = {}
        dr2, g["ln2_g"], g["ln2_b"], _ = _ln_bwd(sv["r2"], dx, _row(p["ln2_g"][l]), name=n + "ln2_bwd")
        big_grad("ffn_w_down", l, _hosted(_mm_tn, grad_rider(l, "dw_down"), land_grads, sv["hff"], dr2, name=n + "dw_down"))
        dhff = _hosted(_mm, grad_rider(l, "dhff"), land_grads, dr2, big_w["ffn_w_down"][l], trans_b=True,
                       out_dtype=BF16, name=n + "dhff")
        keys, rd = grad_rider(l, "ffn_gate_bwd")
        (du, dfw, g["ffn_conv_b"]), routs = _ffn_gate_bwd(sv["u"], sv["uc"], dhff, sv["fcw"], name=n + "ffn_gate_bwd",
                                                          rider=rd)
        land_grads(keys, routs)
        g["ffn_conv_w"] = dfw[:FFN_CONV_WIDTH]
        big_grad("ffn_w_up", l, _mm_tn(sv["x1"], du, name=n + "dw_up"))
        dx1 = _hosted(_mm, grad_rider(l, "dx1"), land_grads, du, big_w["ffn_w_up"][l], trans_b=True, res=dr2,
                      res_scale=ALPHA, name=n + "dx1")
        dr1, g["ln1_g"], g["ln1_b"], g["b_out"] = _ln_bwd(sv["r1"], dx1, _row(p["ln1_g"][l]), name=n + "ln1_bwd")
        big_grad("w_out", l, _mm_tn(sv["mix"], dr1, name=n + "dw_out"))
        dmix = _mm(dr1, big_w["w_out"][l], trans_b=True, name=n + "dmix")

        hin = sv["hin"]
        keys, rd = grad_rider(l, "cv_bwd")
        (dh, g["cv_w_pw"], dcw, g["cv_b_pw"], g["cv_gn_g"], g["cv_gn_b"], g["cv_b"]), routs = _cv_bwd(
            hin, sv["cv_c"], dmix, sv["cvw"], _row(p["cv_gn_g"][l]), _row(p["cv_gn_b"][l]), mavg, p["cv_w_pw"][l],
            name=n + "cv_bwd", rider=rd)
        land_grads(keys, routs)
        g["cv_w"] = dcw[:CONV_WIDTH]
        dh, dwr, dwi, dlcw, g["lru_b_r"], g["lru_b_i"], dsp, g["lru_conv_b"] = _lru_bwd(
            hin, sv["lru_xc"], sv["lru_h"], dmix, sv["lcw"], q["wr"], _row(p["lru_b_r"][l]), q["wi"],
            _row(p["lru_b_i"][l]), q["sp"], dh, name=n + "lru_bwd")
        g["lru_conv_w"] = dlcw[:LRU_CONV_WIDTH]
        g["lru_w_r"], g["lru_w_i"] = q["lru_w_vjp"]((dwr, dwi))
        (g["lru_lam"],) = q["sp_vjp"](dsp)
        keys, rd = grad_rider(l, "s5_bwd")
        (dh, g["s5_w_glu"], dwc, dwb, g["s5_b_glu"], g["s5_d"], da), routs = _s5_bwd(
            hin, sv["s5_y1"], dmix, sv["s5_hb"], sv["s5_h"], q["wb"], q["apow"], q["wc"], _row(p["s5_d"][l]), p["s5_w_glu"][l],
            _row(p["s5_b_glu"][l]), dh, name=n + "s5_bwd", rider=rd)
        land_grads(keys, routs)
        (g["s5_lam_re"], g["s5_lam_im"], g["s5_log_dt"], g["s5_b_re"], g["s5_b_im"], g["s5_c_re"],
         g["s5_c_im"]) = q["s5_vjp"]((da, dwb, dwc))
        dh, dkbig, dvbig = _attn_bwd(hin, dmix, sv["kbig"], sv["vbig"], dh, name=n + "attn_bwd")
        (dkv,) = sv["kv_vjp"]((dkbig, dvbig))
        big_grad("attn_w_kv", l, _mm_tn(mem, dkv, name=n + "dw_kv"))

        g["b_in"] = _colsum_call(dh, name=n + "db_in")
        if dist:
            g["b_in"] = _perm_in_cols(g["b_in"], inverse=True)
            ready[("rep", l)] = (_pack([g[k] for k in REP_LAYERED], F32), "all")
            ready[("ssh", l)] = (_pack_lead([_split_shards(g[k], SHARDED[k] - 1) for k in SMALL_SHARDED], F32), "lead")
        else:
            for k, v in g.items():
                small.setdefault(k, [None] * DEPTH)[l] = v.reshape(p[k].shape[1:])
        big_grad("w_in", l, _hosted(_mm_tn, grad_rider(l, "dw_in"), land_grads, sv["xs"], dh, name=n + "dw_in"))
        dx = _hosted(_mm, grad_rider(l, "dxs"), land_grads, dh, big_w["w_in"][l], trans_b=True, res=dr1,
                     res_scale=ALPHA, name=n + "dxs")

    keys, rd = grad_rider(0, "ln_in_bwd")
    if rd is None:
        grad_x, dgi, dbi, _ = _ln_bwd(x, dx, _row(p["ln_in_g"]), name="ln_in_bwd")
    else:
        (grad_x, dgi, dbi, _), routs = _ln_bwd(x, dx, _row(p["ln_in_g"]), name="ln_in_bwd", rider=rd)
        land_grads(keys, routs)
    out = {k: jnp.stack(v, axis=0) for k, v in small.items()}
    out["ln_in_g"], out["ln_in_b"] = dgi.reshape(-1), dbi.reshape(-1)
    return loss, grad_x, out, ((recv, ready) if dist else big_g)


def kernel(x, mem, ln_in_g, ln_in_b, w_in, b_in, s5_lam_re, s5_lam_im, s5_log_dt, s5_b_re, s5_b_im, s5_c_re, s5_c_im, s5_d, s5_w_glu, s5_b_glu, cv_w, cv_b, cv_gn_g, cv_gn_b, cv_w_pw, cv_b_pw, lru_conv_w, lru_conv_b, lru_w_r, lru_b_r, lru_w_i, lru_b_i, lru_lam, attn_w_kv, w_out, b_out, ln1_g, ln1_b, ffn_w_up, ffn_conv_w, ffn_conv_b, ffn_w_down, ln2_g, ln2_b, loss_target, m_ln_in_g, m_ln_in_b, m_w_in, m_b_in, m_s5_lam_re, m_s5_lam_im, m_s5_log_dt, m_s5_b_re, m_s5_b_im, m_s5_c_re, m_s5_c_im, m_s5_d, m_s5_w_glu, m_s5_b_glu, m_cv_w, m_cv_b, m_cv_gn_g, m_cv_gn_b, m_cv_w_pw, m_cv_b_pw, m_lru_conv_w, m_lru_conv_b, m_lru_w_r, m_lru_b_r, m_lru_w_i, m_lru_b_i, m_lru_lam, m_attn_w_kv, m_w_out, m_b_out, m_ln1_g, m_ln1_b, m_ffn_w_up, m_ffn_conv_w, m_ffn_conv_b, m_ffn_w_down, m_ln2_g, m_ln2_b, v_ln_in_g, v_ln_in_b, v_w_in, v_b_in, v_s5_lam_re, v_s5_lam_im, v_s5_log_dt, v_s5_b_re, v_s5_b_im, v_s5_c_re, v_s5_c_im, v_s5_d, v_s5_w_glu, v_s5_b_glu, v_cv_w, v_cv_b, v_cv_gn_g, v_cv_gn_b, v_cv_w_pw, v_cv_b_pw, v_lru_conv_w, v_lru_conv_b, v_lru_w_r, v_lru_b_r, v_lru_w_i, v_lru_b_i, v_lru_lam, v_attn_w_kv, v_w_out, v_b_out, v_ln1_g, v_ln1_b, v_ffn_w_up, v_ffn_conv_w, v_ffn_conv_b, v_ffn_w_down, v_ln2_g, v_ln2_b):
    args = locals()
    w = {n: args[n] for n in WEIGHTS}
    mom = {n: args["m_" + n] for n in WEIGHTS}
    var = {n: args["v_" + n] for n in WEIGHTS}

    shards = {n: w[n].astype(BF16) for n in BIG}
    shards["small_pack"] = [_pack([w[n] for n in SMALL_SHARDED], F32)]
    small_shapes = [w[n].shape for n in SMALL_SHARDED]

    def unpack_small(gathered):
        out = {n: _join_shards(st, SHARDED[n]) for n, st in zip(SMALL_SHARDED, _unpack(gathered, small_shapes, lead=True))}
        for n in ("s5_w_glu", "cv_w_pw"):
            out[n] = out[n].astype(BF16)
        return out

    big_w = {n: [None] * DEPTH for n in BIG}
    p = {n: w[n] for n in REPLICATED}
    p["b_in"] = _perm_in_cols(p["b_in"])

    loss, grad_x, g_small, (recv, ready) = _local_step(x[0], mem[0], loss_target[0], p, big_w, shards, unpack_small)
    loss = lax.psum(loss, ("x", "y", "c"))

    left = list(ready)
    rider = _Rider([ready[k][0] for k in left] + [_pack([g_small["ln_in_g"], g_small["ln_in_b"]], F32)],
                   [ready[k][1] for k in left] + ["all"])
    got = _exchange(rider, name="exchange_grads")
    for k, r in zip(left, got):
        recv[k] = r

    res = [dict(), dict(), dict(), dict()]
    for n in BIG:
        outs = None
        for l in range(DEPTH):
            outs = _adamw_layer(recv[(n, l)], w[n], mom[n], var[n], l, outs, name=f"adamw_{n}_l{l}")
        for kind in range(4):
            res[kind][n] = outs[kind]
    for names, key, tag in ((SMALL_SHARDED, "ssh", "adamw_small_sharded"), (REP_LAYERED, "rep", "adamw_replicated")):
        gstack = jnp.concatenate([recv[(key, l)] for l in range(DEPTH)], axis=1)
        packs = [_pack_layers([t[n] for n in names], F32) for t in (w, mom, var)]
        rows = packs[0].shape[1]
        outs = _adamw(gstack, *[pk.reshape(DEPTH * rows, PACK_COLS) for pk in packs], name=tag)
        for kind in range(4):
            for n, a in zip(names, _unpack_layers(outs[kind].reshape(DEPTH, rows, PACK_COLS), [w[n].shape for n in names])):
                res[kind][n] = a
    ln_names = ("ln_in_g", "ln_in_b")
    outs = _adamw(got[len(left)], _pack([w[n] for n in ln_names], F32), _pack([mom[n] for n in ln_names], F32),
                  _pack([var[n] for n in ln_names], F32), name="adamw_ln_in")
    for kind in range(4):
        for n, a in zip(ln_names, _unpack(outs[kind], [w[n].shape for n in ln_names])):
            res[kind][n] = a
    return (loss, grad_x[None], *[res[0][n] for n in WEIGHTS], *[res[1][n] for n in WEIGHTS],
            *[res[2][n] for n in WEIGHTS], *[res[3][n] for n in WEIGHTS])
```

```python
import math

import jax
import jax.numpy as jnp
from jax import lax
from jax.experimental import pallas as pl
from jax.experimental.pallas import tpu as pltpu

F32 = jnp.float32
BF16 = jnp.bfloat16

D_MODEL = 1024
DEPTH = 2
D_GROUP = 256
N_IN_COLS = 6 * D_GROUP
S5_GROUPS = 16
S5_CH = 16
S5_STATE = 64
S5_LANES = S5_GROUPS * S5_STATE
CONV_WIDTH = 31
GN_GROUPS = 4
LRU_HEADS = 4
LRU_CONV_WIDTH = 4
LRU_C = 8.0
ATTN_HEADS = 4
ATTN_HEAD_DIM = 64
D_FF = 2816
FFN_CONV_WIDTH = 3
ALPHA = (2 * DEPTH) ** 0.25
LN_EPS = 1e-5
ADAM_LR, ADAM_B1, ADAM_B2, ADAM_EPS, ADAM_WD, ADAM_STEP = 0.001, 0.9, 0.999, 1e-08, 0.01, 10

N_DEV = 8
N_PEERS = N_DEV - 1
LANE = 128
SUBLANE = 8
VMEM_LIMIT = 56 * 1024 * 1024
PACK_COLS = 1024
PACK_ROW_BLOCK = 256

SHARDED = {
    "w_in": 2, "s5_w_glu": 1, "cv_w": 2, "cv_w_pw": 1, "lru_conv_w": 2, "attn_w_kv": 1,
    "w_out": 1, "ffn_w_up": 2, "ffn_conv_w": 2, "ffn_w_down": 1,
}
BIG = ("w_in", "attn_w_kv", "w_out", "ffn_w_up", "ffn_w_down")
SMALL_SHARDED = ("s5_w_glu", "cv_w", "cv_w_pw", "lru_conv_w", "ffn_conv_w")
MATMUL_WEIGHTS = ("w_in", "s5_w_glu", "cv_w_pw", "attn_w_kv", "w_out", "ffn_w_up", "ffn_w_down")
WEIGHTS = ['ln_in_g', 'ln_in_b', 'w_in', 'b_in', 's5_lam_re', 's5_lam_im', 's5_log_dt', 's5_b_re', 's5_b_im',
           's5_c_re', 's5_c_im', 's5_d', 's5_w_glu', 's5_b_glu', 'cv_w', 'cv_b', 'cv_gn_g', 'cv_gn_b', 'cv_w_pw',
           'cv_b_pw', 'lru_conv_w', 'lru_conv_b', 'lru_w_r', 'lru_b_r', 'lru_w_i', 'lru_b_i', 'lru_lam',
           'attn_w_kv', 'w_out', 'b_out', 'ln1_g', 'ln1_b', 'ffn_w_up', 'ffn_conv_w', 'ffn_conv_b', 'ffn_w_down',
           'ln2_g', 'ln2_b']
REPLICATED = [n for n in WEIGHTS if n not in SHARDED]
REP_LAYERED = [n for n in REPLICATED if n not in ("ln_in_g", "ln_in_b")]

COL_CV_V, COL_CV_G, COL_LRU_G, COL_LRU_X, COL_S5, COL_Q = range(6)
IN_PERM = (1, 2, 3, 4, 0, 5)
MIX_S5, MIX_CV, MIX_LRU, MIX_ATTN = range(4)


_ANY = pl.BlockSpec(memory_space=pl.ANY)
_MESH = pl.DeviceIdType.MESH


def _cparams(n_axes):
    return pltpu.CompilerParams(dimension_semantics=("arbitrary",) * n_axes, vmem_limit_bytes=VMEM_LIMIT)


def _pick(n, cap):
    if n <= cap:
        return n
    best = None
    for t in range(LANE, cap + 1, LANE):
        if n % t == 0:
            best = t
    assert best is not None, (n, cap)
    return best


def _pick_rows(n, cap):
    best = None
    for t in range(SUBLANE, min(n, cap) + 1, SUBLANE):
        if n % t == 0:
            best = t
    assert best is not None, (n, cap)
    return best


def _full_spec(arr):
    nd = arr.ndim
    return pl.BlockSpec(arr.shape, lambda *_: (0,) * nd)


def _dot(a, b):
    return lax.dot_general(a.astype(BF16), b.astype(BF16), (((1,), (0,)), ((), ())), preferred_element_type=F32)


def _dot_nt(a, b):
    return lax.dot_general(a.astype(BF16), b.astype(BF16), (((1,), (1,)), ((), ())), preferred_element_type=F32)


def _dot_tn(a, b):
    return lax.dot_general(a.astype(BF16), b.astype(BF16), (((0,), (0,)), ((), ())), preferred_element_type=F32)


def _dot_hi(a, b):
    return jnp.dot(a, b, precision=lax.Precision.HIGHEST, preferred_element_type=F32)


def _colsum(x):
    return jnp.sum(x, axis=0, keepdims=True)


def _sigmoid(x):
    return 1.0 / (1.0 + jnp.exp(-x))


_GELU_K = math.sqrt(2.0 / math.pi)
_GELU_C = 0.044715


def _gelu(x):
    t = jnp.tanh(_GELU_K * (x + _GELU_C * x * x * x))
    return 0.5 * x * (1.0 + t)


def _gelu_and_grad(x):
    x2 = x * x
    t = jnp.tanh(_GELU_K * (x + _GELU_C * x2 * x))
    g = 0.5 * x * (1.0 + t)
    dg = 0.5 * (1.0 + t) + 0.5 * x * (1.0 - t * t) * (_GELU_K * (1.0 + 3.0 * _GELU_C * x2))
    return g, dg


def _neg_expm1(x):
    series = x * (1.0 + x * (0.5 + x * (1.0 / 6.0 + x * (1.0 / 24.0 + x * (1.0 / 120.0)))))
    return -jnp.where(jnp.abs(x) < 0.1, series, jnp.exp(x) - 1.0)


def _seq_tile(s, want):
    t = min(s, want)
    assert s % t == 0
    return t


class _Rider:
    def __init__(self, srcs, kinds):
        self.srcs, self.kinds = list(srcs), list(kinds)
        self.n = len(self.srcs)

    def out_shapes(self):
        shapes = []
        for x, kind in zip(self.srcs, self.kinds):
            if kind == "lead":
                shp = x.shape
            elif kind == "rows":
                shp = (N_DEV, x.shape[0] // N_DEV) + x.shape[1:]
            else:
                shp = (N_DEV,) + x.shape
            shapes.append(jax.ShapeDtypeStruct(shp, x.dtype))
        return shapes

    def scratch(self):
        return [pltpu.SemaphoreType.DMA((self.n * N_PEERS,)), pltpu.SemaphoreType.DMA((self.n * N_PEERS,)),
                pltpu.SemaphoreType.DMA((self.n,))]

    def _copies(self, x_refs, out_refs, sems):
        send_sems, recv_sems, local_sems = sems
        mx, my, mc = lax.axis_index("x"), lax.axis_index("y"), lax.axis_index("c")
        my_id = 4 * mx + 2 * my + mc

        def piece(i, dev):
            if self.kinds[i] == "lead":
                return x_refs[i].at[dev]
            if self.kinds[i] == "rows":
                r = x_refs[i].shape[0] // N_DEV
                return x_refs[i].at[pl.ds(pl.multiple_of(dev * r, SUBLANE), r)]
            return x_refs[i]

        mine = [pltpu.make_async_copy(piece(i, my_id), out_refs[i].at[my_id], local_sems.at[i]) for i in range(self.n)]
        copies = []
        for k in range(1, N_DEV):
            px, py, pc = mx ^ ((k >> 2) & 1), my ^ ((k >> 1) & 1), mc ^ (k & 1)
            for i in range(self.n):
                copies.append(pltpu.make_async_remote_copy(
                    src_ref=piece(i, 4 * px + 2 * py + pc), dst_ref=out_refs[i].at[my_id],
                    send_sem=send_sems.at[i * N_PEERS + k - 1], recv_sem=recv_sems.at[i * N_PEERS + k - 1],
                    device_id=(px, py, pc), device_id_type=_MESH))
        return mine, copies

    def start(self, x_refs, out_refs, sems):
        mine, copies = self._copies(x_refs, out_refs, sems)
        for cp in mine + copies:
            cp.start()

    def wait(self, x_refs, out_refs, sems):
        mine, copies = self._copies(x_refs, out_refs, sems)
        for cp in copies:
            cp.wait_recv()
        for cp in copies:
            cp.wait_send()
        for cp in mine:
            cp.wait()


def _call(body, *, grid, ins, in_specs, outs, out_specs, scratch=(), aliases=None, name, rider=None):
    n_axes = len(grid)
    common = dict(grid=grid, input_output_aliases=aliases or {}, compiler_params=_cparams(n_axes), name=name)
    if rider is None:
        res = pl.pallas_call(body, in_specs=list(in_specs), out_specs=list(out_specs), out_shape=list(outs),
                             scratch_shapes=list(scratch), **common)(*ins)
        return list(res), []
    n_in, n_out, n_scr, nr = len(ins), len(outs), len(scratch), rider.n

    def wrapped(*refs):
        pos = [0]

        def take(k):
            part = refs[pos[0]:pos[0] + k]
            pos[0] += k
            return part

        a_in, r_in, a_out, r_out, a_scr, sems = take(n_in), take(nr), take(n_out), take(nr), take(n_scr), take(3)
        first = last = None
        for ax in range(n_axes):
            pid = pl.program_id(ax)
            f, l = pid == 0, pid == grid[ax] - 1
            first = f if first is None else jnp.logical_and(first, f)
            last = l if last is None else jnp.logical_and(last, l)

        @pl.when(first)
        def _():
            rider.start(r_in, r_out, sems)

        body(*a_in, *a_out, *a_scr)

        @pl.when(last)
        def _():
            rider.wait(r_in, r_out, sems)

    res = pl.pallas_call(
        wrapped, in_specs=list(in_specs) + [_ANY] * nr, out_specs=list(out_specs) + [_ANY] * nr,
        out_shape=list(outs) + rider.out_shapes(), scratch_shapes=list(scratch) + rider.scratch(), **common)(*ins, *rider.srcs)
    return list(res[:n_out]), list(res[n_out:])


def _block_mask(n_blocks, block_rows, block_cols):
    r = jnp.arange(n_blocks * block_rows) // block_rows
    c = jnp.arange(n_blocks * block_cols) // block_cols
    return (r[:, None] == c[None, :]).astype(F32)


def _mm(a, b, *, bias=None, res=None, res_scale=1.0, trans_b=False, out_dtype=F32, ln=None, name, rider=None):
    m, kdim = a.shape
    n = b.shape[0] if trans_b else b.shape[1]
    tm = _seq_tile(m, 1024)
    tn = _pick(n, 1408)
    tk = _pick(kdim, 1536)
    nk = kdim // tk
    has_bias, has_res, has_ln = bias is not None, res is not None, ln is not None
    assert not has_ln or tn == n

    def body(*refs):
        a_ref, b_ref = refs[0], refs[1]
        pos = 2
        bias_ref = res_ref = g_ref = beta_ref = x_ref = None
        if has_bias:
            bias_ref = refs[pos]
            pos += 1
        if has_res:
            res_ref = refs[pos]
            pos += 1
        if has_ln:
            g_ref, beta_ref = refs[pos], refs[pos + 1]
            pos += 2
        o_ref = refs[pos]
        pos += 1
        if has_ln:
            x_ref = refs[pos]
            pos += 1
        acc_ref = refs[pos]
        k = pl.program_id(2)

        @pl.when(k == 0)
        def _():
            acc_ref[...] = jnp.zeros_like(acc_ref)

        if trans_b:
            acc_ref[...] += _dot_nt(a_ref[...], b_ref[...])
        else:
            acc_ref[...] += _dot(a_ref[...], b_ref[...])

        @pl.when(k == nk - 1)
        def _():
            r = acc_ref[...]
            if has_bias:
                r = r + bias_ref[...]
            if has_res:
                r = r + res_scale * res_ref[...]
            o_ref[...] = r.astype(out_dtype)
            if has_ln:
                xc = r - jnp.mean(r, axis=1, keepdims=True)
                var = jnp.mean(xc * xc, axis=1, keepdims=True)
                x_ref[...] = xc * lax.rsqrt(var + LN_EPS) * g_ref[...] + beta_ref[...]

    ins = [a, b]
    in_specs = [pl.BlockSpec((tm, tk), lambda i, j, k: (i, k)),
                pl.BlockSpec((tn, tk), lambda i, j, k: (j, k)) if trans_b
                else pl.BlockSpec((tk, tn), lambda i, j, k: (k, j))]
    if has_bias:
        ins.append(bias)
        in_specs.append(pl.BlockSpec((1, tn), lambda i, j, k: (0, j)))
    if has_res:
        ins.append(res)
        in_specs.append(pl.BlockSpec((tm, tn), lambda i, j, k: (i, j)))
    if has_ln:
        ins += list(ln)
        in_specs += [pl.BlockSpec((1, tn), lambda i, j, k: (0, j))] * 2
    tile = pl.BlockSpec((tm, tn), lambda i, j, k: (i, j))
    outs, routs = _call(
        body, grid=(m // tm, n // tn, nk), ins=ins, in_specs=in_specs,
        outs=[jax.ShapeDtypeStruct((m, n), out_dtype)] + ([jax.ShapeDtypeStruct((m, n), F32)] if has_ln else []),
        out_specs=[tile] * (2 if has_ln else 1),
        scratch=[pltpu.VMEM((tm, tn), F32)], name=name, rider=rider)
    out = tuple(outs) if has_ln else outs[0]
    return out if rider is None else (out, routs)


def _mm_tn(a, b, *, colsum=False, name, rider=None):
    s, ka = a.shape
    nb = b.shape[1]
    ts = _seq_tile(s, 512)
    tka = _pick(ka, 1408)
    tnb = _pick(nb, 1408)
    assert not colsum or tka == ka

    def body(a_ref, b_ref, o_ref, *rest):
        @pl.when(pl.program_id(2) == 0)
        def _():
            o_ref[...] = jnp.zeros_like(o_ref)
            if colsum:
                rest[0][...] = jnp.zeros_like(rest[0])

        bv = b_ref[...]
        o_ref[...] += _dot_tn(a_ref[...], bv)
        if colsum:
            rest[0][...] += _colsum(bv.astype(F32))

    outs, routs = _call(
        body, grid=(ka // tka, nb // tnb, s // ts), ins=[a, b],
        in_specs=[pl.BlockSpec((ts, tka), lambda i, j, k: (k, i)), pl.BlockSpec((ts, tnb), lambda i, j, k: (k, j))],
        outs=[jax.ShapeDtypeStruct((ka, nb), F32)] + ([jax.ShapeDtypeStruct((1, nb), F32)] if colsum else []),
        out_specs=[pl.BlockSpec((tka, tnb), lambda i, j, k: (i, j))] +
                  ([pl.BlockSpec((1, tnb), lambda i, j, k: (0, j))] if colsum else []),
        name=name, rider=rider)
    out = tuple(outs) if colsum else outs[0]
    return out if rider is None else (out, routs)


def _ln_fwd(r, g, b, *, name, rider=None):
    s, d = r.shape
    ts = _seq_tile(s, 512)

    def body(r_ref, g_ref, b_ref, o_ref):
        x = r_ref[...]
        mu = jnp.mean(x, axis=1, keepdims=True)
        xc = x - mu
        var = jnp.mean(xc * xc, axis=1, keepdims=True)
        o_ref[...] = xc * lax.rsqrt(var + LN_EPS) * g_ref[...] + b_ref[...]

    (out,), routs = _call(
        body, grid=(s // ts,), ins=[r, g, b],
        in_specs=[pl.BlockSpec((ts, d), lambda i: (i, 0)), _full_spec(g), _full_spec(b)],
        out_specs=[pl.BlockSpec((ts, d), lambda i: (i, 0))], outs=[jax.ShapeDtypeStruct((s, d), F32)],
        name=name, rider=rider)
    return out if rider is None else (out, routs)


def _ln_bwd(r, dy, g, *, name, rider=None):
    s, d = r.shape
    ts = _seq_tile(s, 512)

    def body(r_ref, dy_ref, g_ref, dr_ref, dg_ref, db_ref, ds_ref):
        @pl.when(pl.program_id(0) == 0)
        def _():
            dg_ref[...] = jnp.zeros_like(dg_ref)
            db_ref[...] = jnp.zeros_like(db_ref)
            ds_ref[...] = jnp.zeros_like(ds_ref)

        x = r_ref[...]
        dy = dy_ref[...]
        mu = jnp.mean(x, axis=1, keepdims=True)
        xc = x - mu
        var = jnp.mean(xc * xc, axis=1, keepdims=True)
        rstd = lax.rsqrt(var + LN_EPS)
        xh = xc * rstd
        dxh = dy * g_ref[...]
        m1 = jnp.mean(dxh, axis=1, keepdims=True)
        m2 = jnp.mean(dxh * xh, axis=1, keepdims=True)
        dr = rstd * (dxh - m1 - xh * m2)
        dr_ref[...] = dr
        dg_ref[...] += _colsum(dy * xh)
        db_ref[...] += _colsum(dy)
        ds_ref[...] += _colsum(dr)

    vec = jax.ShapeDtypeStruct((1, d), F32)
    vspec = pl.BlockSpec((1, d), lambda i: (0, 0))
    outs, routs = _call(
        body, grid=(s // ts,), ins=[r, dy, g],
        in_specs=[pl.BlockSpec((ts, d), lambda i: (i, 0)), pl.BlockSpec((ts, d), lambda i: (i, 0)), _full_spec(g)],
        out_specs=[pl.BlockSpec((ts, d), lambda i: (i, 0)), vspec, vspec, vspec],
        outs=[jax.ShapeDtypeStruct((s, d), F32), vec, vec, vec], name=name, rider=rider)
    return outs if rider is None else (outs, routs)


def _loss_ln_bwd(r, g, b, target, *, name):
    s, d = r.shape
    ts = _seq_tile(s, 512)

    def body(r_ref, g_ref, b_ref, t_ref, dr_ref, dg_ref, db_ref, l_ref):
        @pl.when(pl.program_id(0) == 0)
        def _():
            dg_ref[...] = jnp.zeros_like(dg_ref)
            db_ref[...] = jnp.zeros_like(db_ref)
            l_ref[...] = jnp.zeros_like(l_ref)

        x = r_ref[...]
        gam = g_ref[...]
        xc = x - jnp.mean(x, axis=1, keepdims=True)
        var = jnp.mean(xc * xc, axis=1, keepdims=True)
        rstd = lax.rsqrt(var + LN_EPS)
        xh = xc * rstd
        e = xh * gam + b_ref[...] - t_ref[...]
        part = jnp.sum(jnp.sum(e * e, axis=1, keepdims=True), axis=0, keepdims=True) * (0.5 / d)
        l_ref[...] += jnp.broadcast_to(part, l_ref.shape)
        dy = e * (1.0 / d)
        dxh = dy * gam
        m1 = jnp.mean(dxh, axis=1, keepdims=True)
        m2 = jnp.mean(dxh * xh, axis=1, keepdims=True)
        dr_ref[...] = rstd * (dxh - m1 - xh * m2)
        dg_ref[...] += _colsum(dy * xh)
        db_ref[...] += _colsum(dy)

    vec = jax.ShapeDtypeStruct((1, d), F32)
    vspec = pl.BlockSpec((1, d), lambda i: (0, 0))
    tile = pl.BlockSpec((ts, d), lambda i: (i, 0))
    return pl.pallas_call(
        body, grid=(s // ts,), in_specs=[tile, _full_spec(g), _full_spec(b), tile],
        out_specs=[tile, vspec, vspec, pl.BlockSpec((SUBLANE, LANE), lambda i: (0, 0))],
        out_shape=[jax.ShapeDtypeStruct((s, d), F32), vec, vec, jax.ShapeDtypeStruct((SUBLANE, LANE), F32)],
        compiler_params=_cparams(1), name=name)(r, g, b, target)


SCAN_CHUNK = 32


def _cscan_levels(bufs, apow_ref, t, pad, *, reverse):
    half = bufs[0].shape[1] // 2
    ch = min(SCAN_CHUNK, t)
    nlev = t.bit_length() - 1
    assert (1 << nlev) == t
    for k in range(nlev):
        d = 1 << k
        src, dst = bufs[k % 2], bufs[(k + 1) % 2]

        def chunk(c, carry, src=src, dst=dst, d=d, k=k):
            ar = apow_ref[k:k + 1, :half]
            ai = apow_ref[k:k + 1, half:]
            if reverse:
                ai = -ai
            r0 = pl.multiple_of(c * ch, ch)
            cur = src[pl.ds(pad + r0, ch), :]
            if d >= SUBLANE:
                off = pad + d if reverse else pad - d
                sh = src[pl.ds(off + r0, ch), :]
            elif reverse:
                blk = src[pl.ds(pad + r0, ch + SUBLANE), :]
                sh = pltpu.roll(blk, ch + SUBLANE - d, axis=0)[:ch, :]
            else:
                blk = src[pl.ds(pad - SUBLANE + r0, ch + SUBLANE), :]
                sh = pltpu.roll(blk, d, axis=0)[SUBLANE:, :]
            sre, sim = sh[:, :half], sh[:, half:]
            dst[pl.ds(pad + r0, ch), :half] = cur[:, :half] + ar * sre - ai * sim
            dst[pl.ds(pad + r0, ch), half:] = cur[:, half:] + ar * sim + ai * sre
            return carry

        lax.fori_loop(0, t // ch, chunk, 0)
    return nlev % 2


def _rscan_levels(abufs, bbufs, t, pad, *, reverse):
    nlev = t.bit_length() - 1
    assert (1 << nlev) == t
    for k in range(nlev):
        d = 1 << k
        asrc, adst = abufs[k % 2], abufs[(k + 1) % 2]
        bsrc, bdst = bbufs[k % 2], bbufs[(k + 1) % 2]
        off = pad + d if reverse else pad - d
        a = asrc[pad:pad + t, :]
        bdst[pad:pad + t, :] = a * bsrc[off:off + t, :] + bsrc[pad:pad + t, :]
        if k < nlev - 1:
            adst[pad:pad + t, :] = a * asrc[off:off + t, :]
    return nlev % 2


S5_TILE = 256


def _s5_scan_forward(u_bf, wb_ref, apow_ref, state, bufs, t, pad):
    half = S5_LANES
    bufs[0][pad:pad + t, :] = _dot(u_bf, wb_ref[...])
    ar, ai = apow_ref[0:1, :half], apow_ref[0:1, half:]
    sr, si = state[:, :half], state[:, half:]
    bufs[0][pad:pad + 1, :half] += ar * sr - ai * si
    bufs[0][pad:pad + 1, half:] += ar * si + ai * sr
    return _cscan_levels(bufs, apow_ref, t, pad, reverse=False)


def _s5_fwd(h_in, wb, apow, wc, dvec, wglu, bglu, *, name, rider=None):
    s = h_in.shape[0]
    t = _seq_tile(s, S5_TILE)
    pad = t // 2
    nt = s // t
    lanes2 = 2 * S5_LANES

    def body(u_ref, wb_ref, apow_ref, wc_ref, d_ref, wglu_ref, bglu_ref, out_ref, y1_ref, hb_ref, h_ref, buf0, buf1,
             carry):
        bufs = (buf0, buf1)

        @pl.when(pl.program_id(0) == 0)
        def _():
            buf0[0:pad, :] = jnp.zeros((pad, lanes2), F32)
            buf1[0:pad, :] = jnp.zeros((pad, lanes2), F32)
            carry[...] = jnp.zeros_like(carry)

        u = u_ref[...]
        state = carry[0:1, :]
        hb_ref[0] = state
        fin = _s5_scan_forward(u.astype(BF16), wb_ref, apow_ref, state, bufs, t, pad)
        hbuf = bufs[fin]
        carry[0:1, :] = hbuf[pad + t - 1:pad + t, :]
        h_bf = hbuf[pad:pad + t, :].astype(BF16)
        h_ref[...] = h_bf
        y1 = _dot(h_bf, wc_ref[...]) + d_ref[...] * u
        y1_ref[...] = y1
        y2 = _gelu(y1)
        z = _dot(y2, wglu_ref[...]) + bglu_ref[...]
        out_ref[...] = (y2 * _sigmoid(z)).astype(BF16)

    ins = [h_in, wb, apow, wc, dvec, wglu, bglu]
    in_specs = [pl.BlockSpec((t, D_GROUP), lambda i: (i, COL_S5))] + [_full_spec(a) for a in ins[1:]]
    return _call(
        body, grid=(nt,), ins=ins, in_specs=in_specs,
        out_specs=[pl.BlockSpec((t, D_GROUP), lambda i: (i, MIX_S5)), pl.BlockSpec((t, D_GROUP), lambda i: (i, 0)),
                   pl.BlockSpec((1, 1, lanes2), lambda i: (i, 0, 0)), pl.BlockSpec((t, lanes2), lambda i: (i, 0))],
        outs=[jax.ShapeDtypeStruct((s, D_MODEL), BF16), jax.ShapeDtypeStruct((s, D_GROUP), F32),
              jax.ShapeDtypeStruct((nt, 1, lanes2), F32), jax.ShapeDtypeStruct((s, lanes2), BF16)],
        scratch=[pltpu.VMEM((pad + t, lanes2), F32), pltpu.VMEM((pad + t, lanes2), F32),
                 pltpu.VMEM((SUBLANE, lanes2), F32)],
        name=name, rider=rider)


def _s5_bwd(h_in, y1, dmix, hb, h_all, wb, apow, wc, dvec, wglu, bglu, dh_all, *, name, rider=None):
    s = h_in.shape[0]
    t = _seq_tile(s, S5_TILE)
    pad = t // 2
    nt = s // t
    half = S5_LANES
    lanes2 = 2 * half
    rows = t + pad

    def body(u_ref, y1_ref, do_ref, hb_ref, h_ref, wb_ref, apow_ref, wc_ref, d_ref, wglu_ref, bglu_ref, _dh_in,
             du_ref, dwglu_ref, dwc_ref, dwb_ref, dbglu_ref, dd_ref, da_ref, buf2, buf3, hpbuf, carry):
        @pl.when(pl.program_id(0) == 0)
        def _():
            for bf in (buf2, buf3):
                bf[t:rows, :] = jnp.zeros((pad, lanes2), F32)
            carry[...] = jnp.zeros_like(carry)
            for r in (dwglu_ref, dwc_ref, dwb_ref, dbglu_ref, dd_ref, da_ref):
                r[...] = jnp.zeros_like(r)

        u = u_ref[...]
        u_bf = u.astype(BF16)
        h_bf = h_ref[...]

        y1 = y1_ref[...]
        dout = do_ref[...]
        y2, dgelu = _gelu_and_grad(y1)
        sg = _sigmoid(_dot(y2, wglu_ref[...]) + bglu_ref[...])
        dz = dout * y2 * sg * (1.0 - sg)
        dy2 = dout * sg + _dot_nt(dz, wglu_ref[...])
        dwglu_ref[...] += _dot_tn(y2, dz)
        dbglu_ref[...] += _colsum(dz)
        dy1 = dy2 * dgelu
        dd_ref[...] += _colsum(dy1 * u)
        dy1_bf = dy1.astype(BF16)
        dwc_ref[...] += _dot_tn(h_bf, dy1_bf)

        buf2[0:t, :] = _dot_nt(dy1_bf, wc_ref[...])
        ar, ai = apow_ref[0:1, :half], apow_ref[0:1, half:]
        cr, ci = carry[0:1, :half], carry[0:1, half:]
        buf2[t - 1:t, :half] += ar * cr + ai * ci
        buf2[t - 1:t, half:] += ar * ci - ai * cr
        lfin = _cscan_levels((buf2, buf3), apow_ref, t, 0, reverse=True)
        lbuf = (buf2, buf3)[lfin]
        lam = lbuf[0:t, :]
        carry[0:1, :] = lbuf[0:1, :]
        lam_bf = lam.astype(BF16)
        du_ref[...] = dy1 * d_ref[...] + _dot_nt(lam_bf, wb_ref[...])
        dwb_ref[...] += _dot_tn(u_bf, lam_bf)

        hpbuf[SUBLANE - 1:SUBLANE, :] = hb_ref[0]
        hpbuf[SUBLANE:SUBLANE + t, :] = h_bf.astype(F32)
        hp = hpbuf[SUBLANE - 1:SUBLANE - 1 + t, :]
        lre, lim = lam[:, :half], lam[:, half:]
        hre, him = hp[:, :half], hp[:, half:]
        da_ref[:, :half] += _colsum(lre * hre + lim * him)
        da_ref[:, half:] += _colsum(lim * hre - lre * him)

    def rev(col):
        return lambda i: (nt - 1 - i, col)

    ins = [h_in, y1, dmix, hb, h_all, wb, apow, wc, dvec, wglu, bglu, dh_all]
    in_specs = [pl.BlockSpec((t, D_GROUP), rev(COL_S5)), pl.BlockSpec((t, D_GROUP), rev(0)),
                pl.BlockSpec((t, D_GROUP), rev(MIX_S5)), pl.BlockSpec((1, 1, lanes2), lambda i: (nt - 1 - i, 0, 0)),
                pl.BlockSpec((t, lanes2), rev(0))] + [_full_spec(a) for a in ins[5:11]] + [_ANY]
    outs = [jax.ShapeDtypeStruct((s, N_IN_COLS), F32), jax.ShapeDtypeStruct((D_GROUP, D_GROUP), F32),
            jax.ShapeDtypeStruct((lanes2, D_GROUP), F32), jax.ShapeDtypeStruct((D_GROUP, lanes2), F32),
            jax.ShapeDtypeStruct((1, D_GROUP), F32), jax.ShapeDtypeStruct((1, D_GROUP), F32),
            jax.ShapeDtypeStruct((1, lanes2), F32)]
    out_specs = [pl.BlockSpec((t, D_GROUP), rev(COL_S5))] + [_full_spec(o) for o in outs[1:]]
    return _call(
        body, grid=(nt,), ins=ins, in_specs=in_specs, out_specs=out_specs, outs=outs, aliases={11: 0},
        scratch=[pltpu.VMEM((rows, lanes2), F32), pltpu.VMEM((rows, lanes2), F32),
                 pltpu.VMEM((SUBLANE + t, lanes2), F32), pltpu.VMEM((SUBLANE, lanes2), F32)],
        name=name, rider=rider)


def _s5_param_map(lam_re, lam_im, log_dt, b_re, b_im, c_re, c_im):
    dt = jnp.exp(log_dt)[:, None]
    er = jnp.exp(lam_re * dt)
    a_re, a_im = er * jnp.cos(lam_im * dt), er * jnp.sin(lam_im * dt)
    den = lam_re * lam_re + lam_im * lam_im
    n_re = a_re - 1.0
    k_re = (n_re * lam_re + a_im * lam_im) / den
    k_im = (a_im * lam_re - n_re * lam_im) / den
    bb_re = k_re[..., None] * b_re - k_im[..., None] * b_im
    bb_im = k_re[..., None] * b_im + k_im[..., None] * b_re
    mask_in = _block_mask(S5_GROUPS, S5_CH, S5_STATE)
    mask_out = _block_mask(S5_GROUPS, S5_STATE, S5_CH)

    def blockdiag_in(m):
        return jnp.tile(jnp.transpose(m, (0, 2, 1)).reshape(S5_GROUPS * S5_CH, S5_STATE), (1, S5_GROUPS)) * mask_in

    def blockdiag_out(m):
        return jnp.tile(jnp.transpose(m, (0, 2, 1)).reshape(S5_LANES, S5_CH), (1, S5_GROUPS)) * mask_out

    a = jnp.concatenate([a_re.reshape(1, -1), a_im.reshape(1, -1)], axis=1)
    wb = jnp.concatenate([blockdiag_in(bb_re), blockdiag_in(bb_im)], axis=1)
    wc = jnp.concatenate([blockdiag_out(c_re), -blockdiag_out(c_im)], axis=0)
    return a, wb, wc


def _s5_apow(a, nlev):
    half = S5_LANES
    re, im = a[:, :half], a[:, half:]
    rows = []
    for _ in range(nlev):
        rows.append(jnp.concatenate([re, im], axis=1))
        re, im = re * re - im * im, 2.0 * re * im
    n_rows = -(-nlev // SUBLANE) * SUBLANE
    rows += [jnp.zeros_like(rows[0])] * (n_rows - nlev)
    return lax.stop_gradient(jnp.concatenate(rows, axis=0))


CV_TILE = 256
CV_PAD = 32
CV_CHUNK = 64


def _shifted_copies(buf, shifted, rows):
    n = rows - SUBLANE
    for s in range(1, SUBLANE):
        shifted[s - 1, 0:n, :] = buf[s:s + n, :]


def _window(buf, shifted, o, ch):
    q, s = divmod(o, SUBLANE)
    if s == 0:
        return buf[o:o + ch, :]
    return shifted[s - 1, q * SUBLANE:q * SUBLANE + ch, :]


def _gn_stats(c, mavg):
    mu = _dot_hi(c, mavg)
    cen = c - mu
    var = _dot_hi(cen * cen, mavg)
    rstd = lax.rsqrt(var + LN_EPS)
    return cen * rstd, rstd


def _cv_fwd(h_in, cw, cb, gng, gnb, mavg, wpw, bpw, mix, *, name, rider=None):
    s = h_in.shape[0]
    t = _seq_tile(s, CV_TILE)
    ch = min(CV_CHUNK, t)

    def body(v_ref, g_ref, cw_ref, cb_ref, gng_ref, gnb_ref, mavg_ref, wpw_ref, bpw_ref, _mix_in, out_ref, c_ref, xpad,
             shifted):
        @pl.when(pl.program_id(0) == 0)
        def _():
            xpad[0:CV_PAD, :] = jnp.zeros((CV_PAD, D_GROUP), F32)

        xpad[CV_PAD:CV_PAD + t, :] = v_ref[...] * _sigmoid(g_ref[...])
        _shifted_copies(xpad, shifted, t + CV_PAD)
        for r0 in range(0, t, ch):
            acc = jnp.broadcast_to(cb_ref[...], (ch, D_GROUP))
            for k in range(CONV_WIDTH):
                o = CV_PAD - (CONV_WIDTH - 1) + k + r0
                acc = acc + cw_ref[k:k + 1, :] * _window(xpad, shifted, o, ch)
            c_ref[r0:r0 + ch, :] = acc
        xpad[0:CV_PAD, :] = xpad[t:t + CV_PAD, :]
        xn, _ = _gn_stats(c_ref[...], mavg_ref[...])
        gn = xn * gng_ref[...] + gnb_ref[...]
        out_ref[...] = (_dot(gn * _sigmoid(gn), wpw_ref[...]) + bpw_ref[...]).astype(BF16)

    ins = [h_in, h_in, cw, cb, gng, gnb, mavg, wpw, bpw, mix]
    in_specs = [pl.BlockSpec((t, D_GROUP), lambda i: (i, COL_CV_V)), pl.BlockSpec((t, D_GROUP), lambda i: (i, COL_CV_G))] + \
               [_full_spec(a) for a in ins[2:9]] + [_ANY]
    return _call(
        body, grid=(s // t,), ins=ins, in_specs=in_specs,
        out_specs=[pl.BlockSpec((t, D_GROUP), lambda i: (i, MIX_CV)), pl.BlockSpec((t, D_GROUP), lambda i: (i, 0))],
        outs=[jax.ShapeDtypeStruct((s, D_MODEL), BF16), jax.ShapeDtypeStruct((s, D_GROUP), F32)],
        aliases={9: 0},
        scratch=[pltpu.VMEM((CV_PAD + t, D_GROUP), F32), pltpu.VMEM((SUBLANE - 1, CV_PAD + t, D_GROUP), F32)],
        name=name, rider=rider)


def _cv_bwd(h_in, c, dmix, cw, gng, gnb, mavg, wpw, *, name, rider=None):
    s = h_in.shape[0]
    t = _seq_tile(s, CV_TILE)
    nt = s // t
    ch = min(CV_CHUNK, t)

    def body(v_ref, g_ref, c_ref, do_ref, cw_ref, gng_ref, gnb_ref, mavg_ref, wpw_ref,
             dvg_ref, dwpw_ref, dcw_ref, dbpw_ref, dgg_ref, dgb_ref, dcb_ref, dcpad, hgbuf, shifted):
        @pl.when(pl.program_id(0) == 0)
        def _():
            dcpad[t:t + CV_PAD, :] = jnp.zeros((CV_PAD, D_GROUP), F32)
            for r in (dwpw_ref, dcw_ref, dbpw_ref, dgg_ref, dgb_ref, dcb_ref):
                r[...] = jnp.zeros_like(r)

        mavg = mavg_ref[...]
        xn, rstd = _gn_stats(c_ref[...], mavg)
        gg = gng_ref[...]
        gn = xn * gg + gnb_ref[...]
        sg = _sigmoid(gn)
        dout = do_ref[...]
        dwpw_ref[...] += _dot_tn(gn * sg, dout)
        dbpw_ref[...] += _colsum(dout)
        dgn = _dot_nt(dout, wpw_ref[...]) * (sg * (1.0 + gn * (1.0 - sg)))
        dgg_ref[...] += _colsum(dgn * xn)
        dgb_ref[...] += _colsum(dgn)
        dxn = dgn * gg
        dc = rstd * (dxn - _dot_hi(dxn, mavg) - xn * _dot_hi(dxn * xn, mavg))
        dcb_ref[...] += _colsum(dc)
        dcpad[0:t, :] = dc

        v = v_ref[...]
        sgm = _sigmoid(g_ref[...])
        hgbuf[...] = v * sgm
        _shifted_copies(dcpad, shifted, t + CV_PAD)
        for r0 in range(0, t, ch):
            hg = hgbuf[r0:r0 + ch, :]
            acc = jnp.zeros((ch, D_GROUP), F32)
            for k in range(CONV_WIDTH):
                o = (CONV_WIDTH - 1) - k + r0
                sh = _window(dcpad, shifted, o, ch)
                acc = acc + cw_ref[k:k + 1, :] * sh
                dcw_ref[k:k + 1, :] += _colsum(hg * sh)
            hgbuf[r0:r0 + ch, :] = acc
        dcpad[t:t + CV_PAD, :] = dcpad[0:CV_PAD, :]
        dhg = hgbuf[...]
        dvg_ref[:, :D_GROUP] = dhg * sgm
        dvg_ref[:, D_GROUP:] = dhg * v * sgm * (1.0 - sgm)

    def rev(col):
        return lambda i: (nt - 1 - i, col)

    ins = [h_in, h_in, c, dmix, cw, gng, gnb, mavg, wpw]
    in_specs = [pl.BlockSpec((t, D_GROUP), rev(COL_CV_V)), pl.BlockSpec((t, D_GROUP), rev(COL_CV_G)),
                pl.BlockSpec((t, D_GROUP), rev(0)), pl.BlockSpec((t, D_GROUP), rev(MIX_CV))] + [_full_spec(a) for a in ins[4:]]
    vec = jax.ShapeDtypeStruct((1, D_GROUP), F32)
    outs = [jax.ShapeDtypeStruct((s, N_IN_COLS), F32),
            jax.ShapeDtypeStruct((D_GROUP, D_GROUP), F32), jax.ShapeDtypeStruct((CV_PAD, D_GROUP), F32), vec, vec, vec, vec]
    out_specs = [pl.BlockSpec((t, 2 * D_GROUP), rev(COL_CV_V // 2))] + [_full_spec(o) for o in outs[1:]]
    return _call(
        body, grid=(nt,), ins=ins, in_specs=in_specs, out_specs=out_specs, outs=outs,
        scratch=[pltpu.VMEM((t + CV_PAD, D_GROUP), F32), pltpu.VMEM((t, D_GROUP), F32),
                 pltpu.VMEM((SUBLANE - 1, t + CV_PAD, D_GROUP), F32)], name=name, rider=rider)


LRU_TILE = 256


def _lru_gates(xc, wr_ref, br_ref, wi_ref, bi_ref, sp_ref):
    r = _sigmoid(_dot(xc, wr_ref[...]) + br_ref[...])
    i = _sigmoid(_dot(xc, wi_ref[...]) + bi_ref[...])
    log_a = -LRU_C * r * sp_ref[...]
    a = jnp.exp(log_a)
    m = jnp.sqrt(_neg_expm1(2.0 * log_a))
    return r, i, a, m


def _lru_fwd(h_in, lcw, lcb, wr, br, wi, bi, sp, mix, *, name):
    s = h_in.shape[0]
    t = _seq_tile(s, LRU_TILE)
    pad = max(t // 2, SUBLANE)

    def body(xg_ref, xr_ref, lcw_ref, lcb_ref, wr_ref, br_ref, wi_ref, bi_ref, sp_ref, _mix_in,
             out_ref, xc_ref, h_ref, xpad, a0, a1, b0, b1, carry):
        @pl.when(pl.program_id(0) == 0)
        def _():
            xpad[0:SUBLANE, :] = jnp.zeros((SUBLANE, D_GROUP), F32)
            for bf in (a0, a1, b0, b1):
                bf[0:pad, :] = jnp.zeros((pad, D_GROUP), F32)
            carry[...] = jnp.zeros_like(carry)

        xpad[SUBLANE:SUBLANE + t, :] = xr_ref[...]
        xc = jnp.broadcast_to(lcb_ref[...], (t, D_GROUP))
        for k in range(LRU_CONV_WIDTH):
            o = SUBLANE - (LRU_CONV_WIDTH - 1) + k
            xc = xc + lcw_ref[k:k + 1, :] * xpad[o:o + t, :]
        xpad[0:SUBLANE, :] = xpad[t:t + SUBLANE, :]
        xc_ref[...] = xc
        _, i, a, m = _lru_gates(xc, wr_ref, br_ref, wi_ref, bi_ref, sp_ref)
        a0[pad:pad + t, :] = a
        b0[pad:pad + t, :] = m * (i * xc)
        b0[pad:pad + 1, :] += a0[pad:pad + 1, :] * carry[0:1, :]
        fin = _rscan_levels((a0, a1), (b0, b1), t, pad, reverse=False)
        hbuf = (b0, b1)[fin]
        carry[0:1, :] = hbuf[pad + t - 1:pad + t, :]
        h = hbuf[pad:pad + t, :]
        h_ref[...] = h
        out_ref[...] = (h * _gelu(xg_ref[...])).astype(BF16)

    ins = [h_in, h_in, lcw, lcb, wr, br, wi, bi, sp, mix]
    row = pl.BlockSpec((t, D_GROUP), lambda i: (i, 0))
    in_specs = [pl.BlockSpec((t, D_GROUP), lambda i: (i, COL_LRU_G)), pl.BlockSpec((t, D_GROUP), lambda i: (i, COL_LRU_X))] + \
               [_full_spec(a) for a in ins[2:9]] + [_ANY]
    return pl.pallas_call(
        body, grid=(s // t,), in_specs=in_specs,
        out_specs=[pl.BlockSpec((t, D_GROUP), lambda i: (i, MIX_LRU)), row, row],
        out_shape=[jax.ShapeDtypeStruct((s, D_MODEL), BF16)] + [jax.ShapeDtypeStruct((s, D_GROUP), F32)] * 2,
        input_output_aliases={9: 0},
        scratch_shapes=[pltpu.VMEM((SUBLANE + t, D_GROUP), F32)] + [pltpu.VMEM((pad + t, D_GROUP), F32)] * 4 +
                       [pltpu.VMEM((SUBLANE, D_GROUP), F32)],
        compiler_params=_cparams(1), name=name)(*ins)


def _lru_bwd(h_in, xc_all, h_all, dmix, lcw, wr, br, wi, bi, sp, dh_all, *, name):
    s = h_in.shape[0]
    t = _seq_tile(s, LRU_TILE)
    nt = s // t
    pad = max(t // 2, SUBLANE)
    tb = t // SUBLANE

    def body(xg_ref, xr_ref, xc_ref, h_ref, hprev_ref, do_ref, lcw_ref, wr_ref, br_ref, wi_ref, bi_ref, sp_ref, _dh_in,
             dgr_ref, dwr_ref, dwi_ref, dlcw_ref, dbr_ref, dbi_ref, dsp_ref, dlcb_ref,
             a0, a1, b0, b1, hp, dxpad, carry):
        pid = pl.program_id(0)

        @pl.when(pid == 0)
        def _():
            for bf in (a0, a1, b0, b1):
                bf[pad + t:pad + t + pad, :] = jnp.zeros((pad, D_GROUP), F32)
            dxpad[t:t + SUBLANE, :] = jnp.zeros((SUBLANE, D_GROUP), F32)
            carry[...] = jnp.zeros_like(carry)
            for r in (dwr_ref, dwi_ref, dlcw_ref, dbr_ref, dbi_ref, dsp_ref, dlcb_ref):
                r[...] = jnp.zeros_like(r)

        xc = xc_ref[...]
        h = h_ref[...]
        dout = do_ref[...]
        gate, dgate = _gelu_and_grad(xg_ref[...])
        dgr_ref[:, :D_GROUP] = dout * h * dgate
        r, i, a, m = _lru_gates(xc, wr_ref, br_ref, wi_ref, bi_ref, sp_ref)

        a0[pad:pad + t, :] = a
        b0[pad:pad + t, :] = dout * gate
        b0[pad + t - 1:pad + t, :] += carry[0:1, :]
        a1[pad:pad + t, :] = a0[pad + 1:pad + 1 + t, :]
        fin = _rscan_levels((a1, a0), (b0, b1), t, pad, reverse=True)
        lam = (b0, b1)[fin][pad:pad + t, :]
        carry[0:1, :] = a[0:1, :] * lam[0:1, :]

        is_first = pid == nt - 1
        hp[0:SUBLANE, :] = jnp.where(is_first, 0.0, hprev_ref[...])
        hp[SUBLANE:SUBLANE + t, :] = h
        hprev = hp[SUBLANE - 1:SUBLANE - 1 + t, :]

        ix = i * xc
        dmm = lam * ix
        dix = lam * m
        da = lam * hprev - dmm * (a / m)
        dlog_a = da * a
        dr = dlog_a * (-LRU_C * sp_ref[...])
        dsp_ref[...] += _colsum(dlog_a * (-LRU_C * r))
        dpr = dr * r * (1.0 - r)
        dpi = dix * xc * i * (1.0 - i)
        dbr_ref[...] += _colsum(dpr)
        dbi_ref[...] += _colsum(dpi)
        dwr_ref[...] += _dot_tn(xc, dpr)
        dwi_ref[...] += _dot_tn(xc, dpi)
        dxc = dix * i + _dot_nt(dpr, wr_ref[...]) + _dot_nt(dpi, wi_ref[...])
        dlcb_ref[...] += _colsum(dxc)

        dxpad[0:t, :] = dxc
        xr = xr_ref[...]
        dxr = jnp.zeros((t, D_GROUP), F32)
        for k in range(LRU_CONV_WIDTH):
            o = (LRU_CONV_WIDTH - 1) - k
            sh = dxpad[o:o + t, :]
            dxr = dxr + lcw_ref[k:k + 1, :] * sh
            dlcw_ref[k:k + 1, :] += _colsum(xr * sh)
        dxpad[t:t + SUBLANE, :] = dxpad[0:SUBLANE, :]
        dgr_ref[:, D_GROUP:] = dxr

    def rev(col):
        return lambda i: (nt - 1 - i, col)

    ins = [h_in, h_in, xc_all, h_all, h_all, dmix, lcw, wr, br, wi, bi, sp, dh_all]
    in_specs = [pl.BlockSpec((t, D_GROUP), rev(COL_LRU_G)), pl.BlockSpec((t, D_GROUP), rev(COL_LRU_X)),
                pl.BlockSpec((t, D_GROUP), rev(0)), pl.BlockSpec((t, D_GROUP), rev(0)),
                pl.BlockSpec((SUBLANE, D_GROUP), lambda i: (jnp.maximum((nt - 1 - i) * tb - 1, 0), 0)),
                pl.BlockSpec((t, D_GROUP), rev(MIX_LRU))] + [_full_spec(a) for a in ins[6:12]] + [_ANY]
    vec = jax.ShapeDtypeStruct((1, D_GROUP), F32)
    mat = jax.ShapeDtypeStruct((D_GROUP, D_GROUP), F32)
    outs = [jax.ShapeDtypeStruct((s, N_IN_COLS), F32), mat, mat, jax.ShapeDtypeStruct((SUBLANE, D_GROUP), F32),
            vec, vec, vec, vec]
    out_specs = [pl.BlockSpec((t, 2 * D_GROUP), rev(COL_LRU_G // 2))] + [_full_spec(o) for o in outs[1:]]
    return pl.pallas_call(
        body, grid=(nt,), in_specs=in_specs, out_specs=out_specs, out_shape=outs, input_output_aliases={12: 0},
        scratch_shapes=[pltpu.VMEM((pad + t + pad, D_GROUP), F32)] * 4 +
                       [pltpu.VMEM((SUBLANE + t, D_GROUP), F32), pltpu.VMEM((t + SUBLANE, D_GROUP), F32),
                        pltpu.VMEM((SUBLANE, D_GROUP), F32)],
        compiler_params=_cparams(1), name=name)(*ins)


def _blockdiag(w):
    h, d, _ = w.shape
    return jnp.tile(w.reshape(h * d, d), (1, h)) * _block_mask(h, d, d)


ATTN_TILE = 512
ATTN_SCALE = ATTN_HEAD_DIM ** -0.5


def _attn_big(kv):
    m = kv.shape[0]
    kbig = jnp.tile(kv[:, :D_GROUP].T, (1, ATTN_HEADS)) * _block_mask(ATTN_HEADS, ATTN_HEAD_DIM, m)
    vbig = jnp.tile(kv[:, D_GROUP:], (ATTN_HEADS, 1)) * _block_mask(ATTN_HEADS, m, ATTN_HEAD_DIM)
    return kbig, vbig


def _attn_probs(q, kbig_ref, m):
    sc = _dot(q, kbig_ref[...]) * ATTN_SCALE
    ps = []
    for h in range(ATTN_HEADS):
        sh = sc[:, h * m:(h + 1) * m]
        e = jnp.exp(sh - jnp.max(sh, axis=1, keepdims=True))
        ps.append(e / jnp.sum(e, axis=1, keepdims=True))
    return ps


def _attn_fwd(h_in, kbig, vbig, mix, *, name):
    s = h_in.shape[0]
    t = _seq_tile(s, ATTN_TILE)
    m = kbig.shape[1] // ATTN_HEADS

    def body(q_ref, kbig_ref, vbig_ref, _mix_in, o_ref):
        ps = _attn_probs(q_ref[...], kbig_ref, m)
        o_ref[...] = _dot(jnp.concatenate(ps, axis=1), vbig_ref[...]).astype(BF16)

    return pl.pallas_call(
        body, grid=(s // t,),
        in_specs=[pl.BlockSpec((t, D_GROUP), lambda i: (i, COL_Q)), _full_spec(kbig), _full_spec(vbig), _ANY],
        out_specs=pl.BlockSpec((t, D_GROUP), lambda i: (i, MIX_ATTN)),
        out_shape=jax.ShapeDtypeStruct((s, D_MODEL), BF16), input_output_aliases={3: 0},
        compiler_params=_cparams(1), name=name)(h_in, kbig, vbig, mix)


def _attn_bwd(h_in, dmix, kbig, vbig, dh_all, *, name):
    s = h_in.shape[0]
    t = _seq_tile(s, ATTN_TILE)
    m = kbig.shape[1] // ATTN_HEADS

    def body(q_ref, do_ref, kbig_ref, vbig_ref, _dh_in, dq_ref, dk_ref, dv_ref):
        @pl.when(pl.program_id(0) == 0)
        def _():
            dk_ref[...] = jnp.zeros_like(dk_ref)
            dv_ref[...] = jnp.zeros_like(dv_ref)

        q = q_ref[...]
        dout = do_ref[...]
        ps = _attn_probs(q, kbig_ref, m)
        dp = _dot_nt(dout, vbig_ref[...])
        dss = []
        for h in range(ATTN_HEADS):
            dph = dp[:, h * m:(h + 1) * m]
            dss.append(ps[h] * (dph - jnp.sum(dph * ps[h], axis=1, keepdims=True)))
        ds = (jnp.concatenate(dss, axis=1) * ATTN_SCALE).astype(BF16)
        dv_ref[...] += _dot_tn(jnp.concatenate(ps, axis=1), dout)
        dq_ref[...] = _dot_nt(ds, kbig_ref[...])
        dk_ref[...] += _dot_tn(q, ds)

    outs = [jax.ShapeDtypeStruct((s, N_IN_COLS), F32), jax.ShapeDtypeStruct(kbig.shape, F32),
            jax.ShapeDtypeStruct(vbig.shape, F32)]
    return pl.pallas_call(
        body, grid=(s // t,),
        in_specs=[pl.BlockSpec((t, D_GROUP), lambda i: (i, COL_Q)), pl.BlockSpec((t, D_GROUP), lambda i: (i, MIX_ATTN)),
                  _full_spec(kbig), _full_spec(vbig), _ANY],
        out_specs=[pl.BlockSpec((t, D_GROUP), lambda i: (i, COL_Q)), _full_spec(outs[1]), _full_spec(outs[2])],
        out_shape=outs, input_output_aliases={4: 0},
        compiler_params=_cparams(1), name=name)(h_in, dmix, kbig, vbig, dh_all)


FFN_TILE = 128
FFN_COL_CHUNK = 256
FFN_ROW_CHUNK = 64


def _ffn_conv(pad_ref, w_ref, b_ref, r0, ch, c0):
    cc = FFN_COL_CHUNK
    acc = jnp.broadcast_to(b_ref[:, c0:c0 + cc], (ch, cc))
    for k in range(FFN_CONV_WIDTH):
        o = SUBLANE - (FFN_CONV_WIDTH - 1) + k + r0
        acc = acc + w_ref[k:k + 1, c0:c0 + cc] * pad_ref[o:o + ch, c0:c0 + cc]
    return acc


def _ffn_gate_fwd(u, fcw, fcb, *, name, rider=None):
    s = u.shape[0]
    t = _seq_tile(s, FFN_TILE)
    ch = min(FFN_ROW_CHUNK, t)
    cc = FFN_COL_CHUNK

    def body(u_ref, w_ref, b_ref, o_ref, uc_ref, upad):
        @pl.when(pl.program_id(0) == 0)
        def _():
            upad[0:SUBLANE, :] = jnp.zeros((SUBLANE, 2 * D_FF), F32)

        upad[SUBLANE:SUBLANE + t, :] = u_ref[...].astype(F32)
        for c0 in range(0, D_FF, cc):
            for r0 in range(0, t, ch):
                val = _ffn_conv(upad, w_ref, b_ref, r0, ch, c0)
                gt = _ffn_conv(upad, w_ref, b_ref, r0, ch, c0 + D_FF)
                o_ref[r0:r0 + ch, c0:c0 + cc] = (val * _gelu(gt)).astype(BF16)
                uc_ref[r0:r0 + ch, c0:c0 + cc] = val.astype(BF16)
                uc_ref[r0:r0 + ch, c0 + D_FF:c0 + D_FF + cc] = gt.astype(BF16)
        upad[0:SUBLANE, :] = upad[t:t + SUBLANE, :]

    return _call(
        body, grid=(s // t,), ins=[u, fcw, fcb],
        in_specs=[pl.BlockSpec((t, 2 * D_FF), lambda i: (i, 0)), _full_spec(fcw), _full_spec(fcb)],
        out_specs=[pl.BlockSpec((t, D_FF), lambda i: (i, 0)), pl.BlockSpec((t, 2 * D_FF), lambda i: (i, 0))],
        outs=[jax.ShapeDtypeStruct((s, D_FF), BF16), jax.ShapeDtypeStruct((s, 2 * D_FF), BF16)],
        scratch=[pltpu.VMEM((SUBLANE + t, 2 * D_FF), F32)], name=name, rider=rider)


def _ffn_gate_bwd(u, uc, dh, fcw, *, name, rider=None):
    s = u.shape[0]
    t = _seq_tile(s, FFN_TILE)
    nt = s // t
    ch = min(FFN_ROW_CHUNK, t)
    cc = FFN_COL_CHUNK

    def body(u_ref, uc_ref, dh_ref, w_ref, du_ref, dw_ref, db_ref, dpad):
        @pl.when(pl.program_id(0) == 0)
        def _():
            dpad[t:t + SUBLANE, :] = jnp.zeros((SUBLANE, 2 * D_FF), F32)
            dw_ref[...] = jnp.zeros_like(dw_ref)
            db_ref[...] = jnp.zeros_like(db_ref)

        for c0 in range(0, D_FF, cc):
            for r0 in range(0, t, ch):
                val = uc_ref[r0:r0 + ch, c0:c0 + cc].astype(F32)
                gt = uc_ref[r0:r0 + ch, c0 + D_FF:c0 + D_FF + cc].astype(F32)
                gl, dgl = _gelu_and_grad(gt)
                d = dh_ref[r0:r0 + ch, c0:c0 + cc].astype(F32)
                dpad[r0:r0 + ch, c0:c0 + cc] = d * gl
                dpad[r0:r0 + ch, c0 + D_FF:c0 + D_FF + cc] = d * val * dgl
        for c0 in range(0, 2 * D_FF, cc):
            dbs = jnp.zeros((1, cc), F32)
            dws = [jnp.zeros((1, cc), F32) for _ in range(FFN_CONV_WIDTH)]
            for r0 in range(0, t, ch):
                x = u_ref[r0:r0 + ch, c0:c0 + cc].astype(F32)
                acc = jnp.zeros((ch, cc), F32)
                for k in range(FFN_CONV_WIDTH):
                    o = (FFN_CONV_WIDTH - 1) - k + r0
                    sh = dpad[o:o + ch, c0:c0 + cc]
                    acc = acc + w_ref[k:k + 1, c0:c0 + cc] * sh
                    dws[k] = dws[k] + _colsum(x * sh)
                    if k == FFN_CONV_WIDTH - 1:
                        dbs = dbs + _colsum(sh)
                du_ref[r0:r0 + ch, c0:c0 + cc] = acc.astype(BF16)
            db_ref[:, c0:c0 + cc] += dbs
            for k in range(FFN_CONV_WIDTH):
                dw_ref[k:k + 1, c0:c0 + cc] += dws[k]
        dpad[t:t + SUBLANE, :] = dpad[0:SUBLANE, :]

    outs = [jax.ShapeDtypeStruct((s, 2 * D_FF), BF16), jax.ShapeDtypeStruct((SUBLANE, 2 * D_FF), F32),
            jax.ShapeDtypeStruct((1, 2 * D_FF), F32)]
    return _call(
        body, grid=(nt,), ins=[u, uc, dh, fcw],
        in_specs=[pl.BlockSpec((t, 2 * D_FF), lambda i: (nt - 1 - i, 0)),
                  pl.BlockSpec((t, 2 * D_FF), lambda i: (nt - 1 - i, 0)),
                  pl.BlockSpec((t, D_FF), lambda i: (nt - 1 - i, 0)), _full_spec(fcw)],
        out_specs=[pl.BlockSpec((t, 2 * D_FF), lambda i: (nt - 1 - i, 0)), _full_spec(outs[1]), _full_spec(outs[2])],
        outs=outs, scratch=[pltpu.VMEM((t + SUBLANE, 2 * D_FF), F32)], name=name, rider=rider)


def _adamw_body(g_ref, w_ref, m_ref, v_ref, go_ref, d_ref, mo_ref, vo_ref):
    inv_b1 = 1.0 - ADAM_B1 ** ADAM_STEP
    inv_b2 = 1.0 - ADAM_B2 ** ADAM_STEP
    g = g_ref[0]
    for dev in range(1, N_DEV):
        g = g + g_ref[dev]
    go_ref[...] = g
    mn = ADAM_B1 * m_ref[...] + (1.0 - ADAM_B1) * g
    vn = ADAM_B2 * v_ref[...] + (1.0 - ADAM_B2) * (g * g)
    mo_ref[...] = mn
    vo_ref[...] = vn
    d_ref[...] = -ADAM_LR * ((mn / inv_b1) / (jnp.sqrt(vn / inv_b2) + ADAM_EPS) + ADAM_WD * w_ref[...])


def _adamw(gstack, w, m, v, *, name):
    _, r, c = gstack.shape
    tr = _pick_rows(r, PACK_ROW_BLOCK)

    def body(*refs):
        _adamw_body(*refs)

    blk = pl.BlockSpec((tr, c), lambda i: (i, 0))
    sh = jax.ShapeDtypeStruct((r, c), F32)
    return pl.pallas_call(
        body, grid=(r // tr,),
        in_specs=[pl.BlockSpec((N_DEV, tr, c), lambda i: (0, i, 0)), blk, blk, blk],
        out_specs=[blk] * 4, out_shape=[sh] * 4,
        compiler_params=_cparams(1), name=name)(gstack, w, m, v)


def _adamw_layer(gstack, w, m, v, layer, into, *, name):
    n_layers, r, c = w.shape
    tr = _pick_rows(r, PACK_ROW_BLOCK)

    def body(g_ref, w_ref, m_ref, v_ref, *rest):
        _adamw_body(g_ref, w_ref, m_ref, v_ref, *rest[-4:])

    blk = pl.BlockSpec((None, tr, c), lambda i: (layer, i, 0))
    sh = jax.ShapeDtypeStruct((n_layers, r, c), F32)
    into = list(into or [])
    return pl.pallas_call(
        body, grid=(r // tr,),
        in_specs=[pl.BlockSpec((N_DEV, tr, c), lambda i: (0, i, 0)), blk, blk, blk] + [_ANY] * len(into),
        out_specs=[blk] * 4, out_shape=[sh] * 4, input_output_aliases={4 + k: k for k in range(len(into))},
        compiler_params=_cparams(1), name=name)(gstack, w, m, v, *into)


def _exchange(rider, *, name):
    n = rider.n

    def body(*refs):
        x_refs, out_refs, sems = refs[:n], refs[n:2 * n], refs[2 * n:]
        rider.start(x_refs, out_refs, sems)
        rider.wait(x_refs, out_refs, sems)

    return pl.pallas_call(
        body, in_specs=[_ANY] * n, out_specs=[_ANY] * n, out_shape=rider.out_shapes(),
        scratch_shapes=rider.scratch(), name=name)(*rider.srcs)


def _pack_rows(n):
    rows = -(-n // PACK_COLS)
    return -(-rows // SUBLANE) * SUBLANE


def _pack(arrs, dtype):
    flat = jnp.concatenate([a.reshape(-1).astype(dtype) for a in arrs])
    rows = _pack_rows(flat.shape[0])
    flat = jnp.pad(flat, (0, rows * PACK_COLS - flat.shape[0]))
    return flat.reshape(rows, PACK_COLS)


def _pack_lead(arrs, dtype):
    flat = jnp.concatenate([a.reshape(N_DEV, -1).astype(dtype) for a in arrs], axis=1)
    rows = _pack_rows(flat.shape[1])
    flat = jnp.pad(flat, ((0, 0), (0, rows * PACK_COLS - flat.shape[1])))
    return flat.reshape(N_DEV, rows, PACK_COLS)


def _pack_layers(arrs, dtype):
    n_layers = arrs[0].shape[0]
    flat = jnp.concatenate([a.reshape(n_layers, -1).astype(dtype) for a in arrs], axis=1)
    rows = _pack_rows(flat.shape[1])
    flat = jnp.pad(flat, ((0, 0), (0, rows * PACK_COLS - flat.shape[1])))
    return flat.reshape(n_layers, rows, PACK_COLS)


def _unpack_layers(packed, shapes):
    flat = packed.reshape(packed.shape[0], -1)
    out, pos = [], 0
    for sh in shapes:
        n = math.prod(sh[1:])
        out.append(flat[:, pos:pos + n].reshape(sh))
        pos += n
    return out


def _unpack(packed, shapes, lead=False):
    flat = packed.reshape(N_DEV, -1) if lead else packed.reshape(-1)
    out, pos = [], 0
    for sh in shapes:
        n = math.prod(sh)
        out.append(flat[:, pos:pos + n].reshape((N_DEV,) + tuple(sh)) if lead else flat[pos:pos + n].reshape(sh))
        pos += n
    return out


def _join_shards(stacked, axis):
    return jnp.concatenate([stacked[d] for d in range(N_DEV)], axis=axis)


def _split_shards(full, axis):
    return jnp.stack(jnp.split(full, N_DEV, axis=axis), axis=0)


def _perm_in_cols(a, inverse=False):
    blocks = jnp.split(a, 6, axis=-1)
    if inverse:
        order = [IN_PERM.index(j) for j in range(6)]
    else:
        order = list(IN_PERM)
    return jnp.concatenate([blocks[j] for j in order], axis=-1)


def _row(v):
    return v.reshape(1, -1)


def _pad_rows(w, rows):
    return jnp.pad(w, ((0, rows - w.shape[0]), (0, 0)))


def _gn_avg_matrix():
    return _block_mask(GN_GROUPS, D_GROUP // GN_GROUPS, D_GROUP // GN_GROUPS) / (D_GROUP // GN_GROUPS)


def _layer_params(p, l):
    q = {}
    (a, wb, wc), q["s5_vjp"] = jax.vjp(_s5_param_map, p["s5_lam_re"][l], p["s5_lam_im"][l], p["s5_log_dt"][l],
                                       p["s5_b_re"][l], p["s5_b_im"][l], p["s5_c_re"][l], p["s5_c_im"][l])
    q["wb"], q["wc"] = wb.astype(BF16), wc.astype(BF16)
    q["apow"] = _s5_apow(a, max(S5_TILE.bit_length() - 1, 1))
    (q["wr"], q["wi"]), q["lru_w_vjp"] = jax.vjp(lambda r, i: (_blockdiag(r), _blockdiag(i)), p["lru_w_r"][l], p["lru_w_i"][l])
    q["wr"], q["wi"] = q["wr"].astype(BF16), q["wi"].astype(BF16)
    q["sp"], q["sp_vjp"] = jax.vjp(lambda lam: _row(jax.nn.softplus(-lam)), p["lru_lam"][l])
    return q


WEIGHT_RIDES = {(0, "ln_in_fwd"): [("w_in", 0), ("small_pack", 0)],
                (0, "inproj"): [("attn_w_kv", 0), ("w_out", 0)], (0, "s5_fwd"): [("ffn_w_up", 0)],
                (0, "cv_fwd"): [("ffn_w_down", 0)],
                (0, "ffn_up"): [("w_in", 1), ("attn_w_kv", 1), ("w_out", 1), ("ffn_w_down", 1)],
                (0, "ffn_gate_fwd"): [("ffn_w_up", 1)]}
GRAD_RIDES = {(1, "ffn_gate_bwd"): [("ffn_w_down", 1)],
              (0, "dw_down"): [("w_out", 1), ("attn_w_kv", 1), ("w_in", 1)],
              (0, "dhff"): [("rep", 1), ("ssh", 1)],
              (0, "dw_in"): [("attn_w_kv", 0), ("ssh", 0)],
              (0, "ln_in_bwd"): [("rep", 0)],
              (0, "ffn_gate_bwd"): [("ffn_w_up", 1)],
              (0, "dx1"): [("ffn_w_down", 0)],
              (0, "cv_bwd"): [("w_out", 0)],
              (0, "s5_bwd"): [("ffn_w_up", 0)],
              (0, "dxs"): [("w_in", 0)]}


def _assemble_weight(n, gathered):
    if SHARDED[n] == 2:
        full = jnp.transpose(gathered, (1, 0, 2)).reshape(gathered.shape[1], -1)
        return _perm_in_cols(full) if n == "w_in" else full
    return gathered.reshape(-1, gathered.shape[-1])


def _grad_source(n, g):
    if SHARDED[n] == 2:
        if n == "w_in":
            g = _perm_in_cols(g, inverse=True)
        k, nn = g.shape
        return jnp.transpose(g.reshape(k, N_DEV, nn // N_DEV), (1, 0, 2)), "lead"
    return g, "rows"


def _hosted(fn, keys_rider, land, *args, **kw):
    keys, rider = keys_rider
    if rider is None:
        return fn(*args, **kw)
    out, routs = fn(*args, rider=rider, **kw)
    land(keys, routs)
    return out


def _local_step(x, mem, target, p, big_w, shards=None, unpack_small=None):
    dist = shards is not None
    small, saved = {}, []
    big_g, ready, recv = {}, {}, {}
    mavg = _gn_avg_matrix()

    def weight_rider(l, host):
        keys = WEIGHT_RIDES.get((l, host), []) if dist else []
        return keys, (_Rider([shards[n][ll] for n, ll in keys], ["all"] * len(keys)) if keys else None)

    def land_weights(keys, routs):
        for (n, ll), r in zip(keys, routs):
            if n == "small_pack":
                p.update(unpack_small(r))
            else:
                big_w[n][ll] = _assemble_weight(n, r)

    def grad_rider(l, host):
        keys = [k for k in GRAD_RIDES.get((l, host), []) if k in ready] if dist else []
        return keys, (_Rider([ready[k][0] for k in keys], [ready[k][1] for k in keys]) if keys else None)

    def land_grads(keys, routs):
        for k, r in zip(keys, routs):
            recv[k] = r
            del ready[k]

    def big_grad(n, l, g):
        if dist:
            ready[(n, l)] = _grad_source(n, g)
        else:
            big_g[(n, l)] = g

    xs = _hosted(_ln_fwd, weight_rider(0, "ln_in_fwd"), land_weights, x, _row(p["ln_in_g"]), _row(p["ln_in_b"]),
                 name="ln_in_fwd")
    for l in range(DEPTH):
        q = _layer_params(p, l)
        n = f"l{l}_"
        hin = _hosted(_mm, weight_rider(l, "inproj"), land_weights, xs, big_w["w_in"][l], bias=_row(p["b_in"][l]),
                      name=n + "inproj")
        keys, rd = weight_rider(l, "s5_fwd")
        (mix, s5_y1, s5_hb, s5_h), routs = _s5_fwd(hin, q["wb"], q["apow"], q["wc"], _row(p["s5_d"][l]),
                                                   p["s5_w_glu"][l], _row(p["s5_b_glu"][l]), name=n + "s5_fwd", rider=rd)
        land_weights(keys, routs)
        cvw = _pad_rows(p["cv_w"][l], CV_PAD)
        keys, rd = weight_rider(l, "cv_fwd")
        (mix, cv_c), routs = _cv_fwd(hin, cvw, _row(p["cv_b"][l]), _row(p["cv_gn_g"][l]), _row(p["cv_gn_b"][l]), mavg,
                                     p["cv_w_pw"][l], _row(p["cv_b_pw"][l]), mix, name=n + "cv_fwd", rider=rd)
        land_weights(keys, routs)
        lcw = _pad_rows(p["lru_conv_w"][l], SUBLANE)
        mix, lru_xc, lru_h = _lru_fwd(hin, lcw, _row(p["lru_conv_b"][l]), q["wr"], _row(p["lru_b_r"][l]), q["wi"],
                                      _row(p["lru_b_i"][l]), q["sp"], mix, name=n + "lru_fwd")
        kv = _mm(mem, big_w["attn_w_kv"][l], name=n + "kv")
        (kbig, vbig), kv_vjp = jax.vjp(_attn_big, kv)
        kbig, vbig = kbig.astype(BF16), vbig.astype(BF16)
        mix = _attn_fwd(hin, kbig, vbig, mix, name=n + "attn_fwd")
        r1, x1 = _mm(mix, big_w["w_out"][l], bias=_row(p["b_out"][l]), res=xs, res_scale=ALPHA,
                     ln=(_row(p["ln1_g"][l]), _row(p["ln1_b"][l])), name=n + "outproj")
        u = _hosted(_mm, weight_rider(l, "ffn_up"), land_weights, x1, big_w["ffn_w_up"][l], out_dtype=BF16,
                    name=n + "ffn_up")
        fcw = _pad_rows(p["ffn_conv_w"][l], SUBLANE)
        fcb = _row(p["ffn_conv_b"][l])
        keys, rd = weight_rider(l, "ffn_gate_fwd")
        (hff, uc), routs = _ffn_gate_fwd(u, fcw, fcb, name=n + "ffn_gate_fwd", rider=rd)
        land_weights(keys, routs)
        if l < DEPTH - 1:
            r2, x2 = _mm(hff, big_w["ffn_w_down"][l], res=x1, res_scale=ALPHA,
                         ln=(_row(p["ln2_g"][l]), _row(p["ln2_b"][l])), name=n + "ffn_down")
        else:
            r2, x2 = _mm(hff, big_w["ffn_w_down"][l], res=x1, res_scale=ALPHA, name=n + "ffn_down"), None
        saved.append(dict(q=q, xs=xs, hin=hin, s5_y1=s5_y1, s5_hb=s5_hb, s5_h=s5_h, cvw=cvw, cv_c=cv_c, lcw=lcw, lru_xc=lru_xc,
                          lru_h=lru_h, kbig=kbig, vbig=vbig, kv_vjp=kv_vjp, mix=mix, r1=r1, x1=x1, u=u, uc=uc, fcw=fcw,
                          hff=hff, r2=r2))
        xs = x2

    top = DEPTH - 1
    dr_top, dg_top, db_top, loss_blk = _loss_ln_bwd(saved[top]["r2"], _row(p["ln2_g"][top]), _row(p["ln2_b"][top]), target,
                                                     name="loss_ln_bwd")
    loss = loss_blk[0, 0]
    dx = None

    for l in reversed(range(DEPTH)):
        sv = saved[l]
        q = sv["q"]
        n = f"l{l}_"
        g = {}
        if l == top:
            dr2, g["ln2_g"], g["ln2_b"] = dr_top, dg_top, db_top
        else:
            dr2, g["ln2_g"], g["ln2_b"], _ = _ln_bwd(sv["r2"], dx, _row(p["ln2_g"][l]), name=n + "ln2_bwd")
        big_grad("ffn_w_down", l, _hosted(_mm_tn, grad_rider(l, "dw_down"), land_grads, sv["hff"], dr2, name=n + "dw_down"))
        dhff = _hosted(_mm, grad_rider(l, "dhff"), land_grads, dr2, big_w["ffn_w_down"][l], trans_b=True,
                       out_dtype=BF16, name=n + "dhff")
        keys, rd = grad_rider(l, "ffn_gate_bwd")
        (du, dfw, g["ffn_conv_b"]), routs = _ffn_gate_bwd(sv["u"], sv["uc"], dhff, sv["fcw"], name=n + "ffn_gate_bwd",
                                                          rider=rd)
        land_grads(keys, routs)
        g["ffn_conv_w"] = dfw[:FFN_CONV_WIDTH]
        big_grad("ffn_w_up", l, _mm_tn(sv["x1"], du, name=n + "dw_up"))
        dx1 = _hosted(_mm, grad_rider(l, "dx1"), land_grads, du, big_w["ffn_w_up"][l], trans_b=True, res=dr2,
                      res_scale=ALPHA, name=n + "dx1")
        dr1, g["ln1_g"], g["ln1_b"], g["b_out"] = _ln_bwd(sv["r1"], dx1, _row(p["ln1_g"][l]), name=n + "ln1_bwd")
        big_grad("w_out", l, _mm_tn(sv["mix"], dr1, name=n + "dw_out"))
        dmix = _mm(dr1, big_w["w_out"][l], trans_b=True, name=n + "dmix")

        hin = sv["hin"]
        keys, rd = grad_rider(l, "cv_bwd")
        (dh, g["cv_w_pw"], dcw, g["cv_b_pw"], g["cv_gn_g"], g["cv_gn_b"], g["cv_b"]), routs = _cv_bwd(
            hin, sv["cv_c"], dmix, sv["cvw"], _row(p["cv_gn_g"][l]), _row(p["cv_gn_b"][l]), mavg, p["cv_w_pw"][l],
            name=n + "cv_bwd", rider=rd)
        land_grads(keys, routs)
        g["cv_w"] = dcw[:CONV_WIDTH]
        dh, dwr, dwi, dlcw, g["lru_b_r"], g["lru_b_i"], dsp, g["lru_conv_b"] = _lru_bwd(
            hin, sv["lru_xc"], sv["lru_h"], dmix, sv["lcw"], q["wr"], _row(p["lru_b_r"][l]), q["wi"],
            _row(p["lru_b_i"][l]), q["sp"], dh, name=n + "lru_bwd")
        g["lru_conv_w"] = dlcw[:LRU_CONV_WIDTH]
        g["lru_w_r"], g["lru_w_i"] = q["lru_w_vjp"]((dwr, dwi))
        (g["lru_lam"],) = q["sp_vjp"](dsp)
        keys, rd = grad_rider(l, "s5_bwd")
        (dh, g["s5_w_glu"], dwc, dwb, g["s5_b_glu"], g["s5_d"], da), routs = _s5_bwd(
            hin, sv["s5_y1"], dmix, sv["s5_hb"], sv["s5_h"], q["wb"], q["apow"], q["wc"], _row(p["s5_d"][l]), p["s5_w_glu"][l],
            _row(p["s5_b_glu"][l]), dh, name=n + "s5_bwd", rider=rd)
        land_grads(keys, routs)
        (g["s5_lam_re"], g["s5_lam_im"], g["s5_log_dt"], g["s5_b_re"], g["s5_b_im"], g["s5_c_re"],
         g["s5_c_im"]) = q["s5_vjp"]((da, dwb, dwc))
        dh, dkbig, dvbig = _attn_bwd(hin, dmix, sv["kbig"], sv["vbig"], dh, name=n + "attn_bwd")
        (dkv,) = sv["kv_vjp"]((dkbig, dvbig))
        big_grad("attn_w_kv", l, _mm_tn(mem, dkv, name=n + "dw_kv"))

        if dist:
            ready[("ssh", l)] = (_pack_lead([_split_shards(g[k], SHARDED[k] - 1) for k in SMALL_SHARDED], F32), "lead")
        gw_in, g["b_in"] = _hosted(_mm_tn, grad_rider(l, "dw_in"), land_grads, sv["xs"], dh, colsum=True, name=n + "dw_in")
        big_grad("w_in", l, gw_in)
        if dist:
            g["b_in"] = _perm_in_cols(g["b_in"], inverse=True)
            ready[("rep", l)] = (_pack([g[k] for k in REP_LAYERED], F32), "all")
        else:
            for k, v in g.items():
                small.setdefault(k, [None] * DEPTH)[l] = v.reshape(p[k].shape[1:])
        dx = _hosted(_mm, grad_rider(l, "dxs"), land_grads, dh, big_w["w_in"][l], trans_b=True, res=dr1,
                     res_scale=ALPHA, name=n + "dxs")

    keys, rd = grad_rider(0, "ln_in_bwd")
    if rd is None:
        grad_x, dgi, dbi, _ = _ln_bwd(x, dx, _row(p["ln_in_g"]), name="ln_in_bwd")
    else:
        (grad_x, dgi, dbi, _), routs = _ln_bwd(x, dx, _row(p["ln_in_g"]), name="ln_in_bwd", rider=rd)
        land_grads(keys, routs)
    out = {k: jnp.stack(v, axis=0) for k, v in small.items()}
    out["ln_in_g"], out["ln_in_b"] = dgi.reshape(-1), dbi.reshape(-1)
    return loss, grad_x, out, ((recv, ready) if dist else big_g)


def kernel(x, mem, ln_in_g, ln_in_b, w_in, b_in, s5_lam_re, s5_lam_im, s5_log_dt, s5_b_re, s5_b_im, s5_c_re, s5_c_im, s5_d, s5_w_glu, s5_b_glu, cv_w, cv_b, cv_gn_g, cv_gn_b, cv_w_pw, cv_b_pw, lru_conv_w, lru_conv_b, lru_w_r, lru_b_r, lru_w_i, lru_b_i, lru_lam, attn_w_kv, w_out, b_out, ln1_g, ln1_b, ffn_w_up, ffn_conv_w, ffn_conv_b, ffn_w_down, ln2_g, ln2_b, loss_target, m_ln_in_g, m_ln_in_b, m_w_in, m_b_in, m_s5_lam_re, m_s5_lam_im, m_s5_log_dt, m_s5_b_re, m_s5_b_im, m_s5_c_re, m_s5_c_im, m_s5_d, m_s5_w_glu, m_s5_b_glu, m_cv_w, m_cv_b, m_cv_gn_g, m_cv_gn_b, m_cv_w_pw, m_cv_b_pw, m_lru_conv_w, m_lru_conv_b, m_lru_w_r, m_lru_b_r, m_lru_w_i, m_lru_b_i, m_lru_lam, m_attn_w_kv, m_w_out, m_b_out, m_ln1_g, m_ln1_b, m_ffn_w_up, m_ffn_conv_w, m_ffn_conv_b, m_ffn_w_down, m_ln2_g, m_ln2_b, v_ln_in_g, v_ln_in_b, v_w_in, v_b_in, v_s5_lam_re, v_s5_lam_im, v_s5_log_dt, v_s5_b_re, v_s5_b_im, v_s5_c_re, v_s5_c_im, v_s5_d, v_s5_w_glu, v_s5_b_glu, v_cv_w, v_cv_b, v_cv_gn_g, v_cv_gn_b, v_cv_w_pw, v_cv_b_pw, v_lru_conv_w, v_lru_conv_b, v_lru_w_r, v_lru_b_r, v_lru_w_i, v_lru_b_i, v_lru_lam, v_attn_w_kv, v_w_out, v_b_out, v_ln1_g, v_ln1_b, v_ffn_w_up, v_ffn_conv_w, v_ffn_conv_b, v_ffn_w_down, v_ln2_g, v_ln2_b):
    args = locals()
    w = {n: args[n] for n in WEIGHTS}
    mom = {n: args["m_" + n] for n in WEIGHTS}
    var = {n: args["v_" + n] for n in WEIGHTS}

    shards = {n: w[n].astype(BF16) for n in BIG}
    shards["small_pack"] = [_pack([w[n] for n in SMALL_SHARDED], F32)]
    small_shapes = [w[n].shape for n in SMALL_SHARDED]

    def unpack_small(gathered):
        out = {n: _join_shards(st, SHARDED[n]) for n, st in zip(SMALL_SHARDED, _unpack(gathered, small_shapes, lead=True))}
        for n in ("s5_w_glu", "cv_w_pw"):
            out[n] = out[n].astype(BF16)
        return out

    big_w = {n: [None] * DEPTH for n in BIG}
    p = {n: w[n] for n in REPLICATED}
    p["b_in"] = _perm_in_cols(p["b_in"])

    loss, grad_x, g_small, (recv, ready) = _local_step(x[0], mem[0], loss_target[0], p, big_w, shards, unpack_small)
    loss = lax.psum(loss, ("x", "y", "c"))

    left = list(ready)
    rider = _Rider([ready[k][0] for k in left] + [_pack([g_small["ln_in_g"], g_small["ln_in_b"]], F32)],
                   [ready[k][1] for k in left] + ["all"])
    got = _exchange(rider, name="exchange_grads")
    for k, r in zip(left, got):
        recv[k] = r

    res = [dict(), dict(), dict(), dict()]
    for n in BIG:
        outs = None
        for l in range(DEPTH):
            outs = _adamw_layer(recv[(n, l)], w[n], mom[n], var[n], l, outs, name=f"adamw_{n}_l{l}")
        for kind in range(4):
            res[kind][n] = outs[kind]
    for names, key, tag in ((SMALL_SHARDED, "ssh", "adamw_small_sharded"), (REP_LAYERED, "rep", "adamw_replicated")):
        gstack = jnp.concatenate([recv[(key, l)] for l in range(DEPTH)], axis=1)
        packs = [_pack_layers([t[n] for n in names], F32) for t in (w, mom, var)]
        rows = packs[0].shape[1]
        outs = _adamw(gstack, *[pk.reshape(DEPTH * rows, PACK_COLS) for pk in packs], name=tag)
        for kind in range(4):
            for n, a in zip(names, _unpack_layers(outs[kind].reshape(DEPTH, rows, PACK_COLS), [w[n].shape for n in names])):
                res[kind][n] = a
    ln_names = ("ln_in_g", "ln_in_b")
    outs = _adamw(got[len(left)], _pack([w[n] for n in ln_names], F32), _pack([mom[n] for n in ln_names], F32),
                  _pack([var[n] for n in ln_names], F32), name="adamw_ln_in")
    for kind in range(4):
        for n, a in zip(ln_names, _unpack(outs[kind], [w[n].shape for n in ln_names])):
            res[kind][n] = a
    return (loss, grad_x[None], *[res[0][n] for n in WEIGHTS], *[res[1][n] for n in WEIGHTS],
            *[res[2][n] for n in WEIGHTS], *[res[3][n] for n in WEIGHTS])
```

```python
import math

import jax
import jax.numpy as jnp
from jax import lax
from jax.experimental import pallas as pl
from jax.experimental.pallas import tpu as pltpu

F32 = jnp.float32
BF16 = jnp.bfloat16

D_MODEL = 1024
DEPTH = 2
D_GROUP = 256
N_IN_COLS = 6 * D_GROUP
S5_GROUPS = 16
S5_CH = 16
S5_STATE = 64
S5_LANES = S5_GROUPS * S5_STATE
CONV_WIDTH = 31
GN_GROUPS = 4
LRU_HEADS = 4
LRU_CONV_WIDTH = 4
LRU_C = 8.0
ATTN_HEADS = 4
ATTN_HEAD_DIM = 64
D_FF = 2816
FFN_CONV_WIDTH = 3
ALPHA = (2 * DEPTH) ** 0.25
LN_EPS = 1e-5
ADAM_LR, ADAM_B1, ADAM_B2, ADAM_EPS, ADAM_WD, ADAM_STEP = 0.001, 0.9, 0.999, 1e-08, 0.01, 10

N_DEV = 8
N_PEERS = N_DEV - 1
LANE = 128
SUBLANE = 8
VMEM_LIMIT = 56 * 1024 * 1024
PACK_COLS = 1024
PACK_ROW_BLOCK = 256

SHARDED = {
    "w_in": 2, "s5_w_glu": 1, "cv_w": 2, "cv_w_pw": 1, "lru_conv_w": 2, "attn_w_kv": 1,
    "w_out": 1, "ffn_w_up": 2, "ffn_conv_w": 2, "ffn_w_down": 1,
}
BIG = ("w_in", "attn_w_kv", "w_out", "ffn_w_up", "ffn_w_down")
SMALL_SHARDED = ("s5_w_glu", "cv_w", "cv_w_pw", "lru_conv_w", "ffn_conv_w")
MATMUL_WEIGHTS = ("w_in", "s5_w_glu", "cv_w_pw", "attn_w_kv", "w_out", "ffn_w_up", "ffn_w_down")
WEIGHTS = ['ln_in_g', 'ln_in_b', 'w_in', 'b_in', 's5_lam_re', 's5_lam_im', 's5_log_dt', 's5_b_re', 's5_b_im',
           's5_c_re', 's5_c_im', 's5_d', 's5_w_glu', 's5_b_glu', 'cv_w', 'cv_b', 'cv_gn_g', 'cv_gn_b', 'cv_w_pw',
           'cv_b_pw', 'lru_conv_w', 'lru_conv_b', 'lru_w_r', 'lru_b_r', 'lru_w_i', 'lru_b_i', 'lru_lam',
           'attn_w_kv', 'w_out', 'b_out', 'ln1_g', 'ln1_b', 'ffn_w_up', 'ffn_conv_w', 'ffn_conv_b', 'ffn_w_down',
           'ln2_g', 'ln2_b']
REPLICATED = [n for n in WEIGHTS if n not in SHARDED]
REP_LAYERED = [n for n in REPLICATED if n not in ("ln_in_g", "ln_in_b")]

COL_CV_V, COL_CV_G, COL_LRU_G, COL_LRU_X, COL_S5, COL_Q = range(6)
IN_PERM = (1, 2, 3, 4, 0, 5)
MIX_S5, MIX_CV, MIX_LRU, MIX_ATTN = range(4)


_ANY = pl.BlockSpec(memory_space=pl.ANY)
_MESH = pl.DeviceIdType.MESH


def _cparams(n_axes):
    return pltpu.CompilerParams(dimension_semantics=("arbitrary",) * n_axes, vmem_limit_bytes=VMEM_LIMIT)


def _pick(n, cap):
    if n <= cap:
        return n
    best = None
    for t in range(LANE, cap + 1, LANE):
        if n % t == 0:
            best = t
    assert best is not None, (n, cap)
    return best


def _pick_rows(n, cap):
    best = None
    for t in range(SUBLANE, min(n, cap) + 1, SUBLANE):
        if n % t == 0:
            best = t
    assert best is not None, (n, cap)
    return best


def _full_spec(arr):
    nd = arr.ndim
    return pl.BlockSpec(arr.shape, lambda *_: (0,) * nd)


def _dot(a, b):
    return lax.dot_general(a.astype(BF16), b.astype(BF16), (((1,), (0,)), ((), ())), preferred_element_type=F32)


def _dot_nt(a, b):
    return lax.dot_general(a.astype(BF16), b.astype(BF16), (((1,), (1,)), ((), ())), preferred_element_type=F32)


def _dot_tn(a, b):
    return lax.dot_general(a.astype(BF16), b.astype(BF16), (((0,), (0,)), ((), ())), preferred_element_type=F32)


def _dot_hi(a, b):
    return jnp.dot(a, b, precision=lax.Precision.HIGHEST, preferred_element_type=F32)


def _colsum(x):
    return jnp.sum(x, axis=0, keepdims=True)


def _sigmoid(x):
    return 1.0 / (1.0 + jnp.exp(-x))


_GELU_K = math.sqrt(2.0 / math.pi)
_GELU_C = 0.044715


def _gelu(x):
    t = jnp.tanh(_GELU_K * (x + _GELU_C * x * x * x))
    return 0.5 * x * (1.0 + t)


def _gelu_and_grad(x):
    x2 = x * x
    t = jnp.tanh(_GELU_K * (x + _GELU_C * x2 * x))
    g = 0.5 * x * (1.0 + t)
    dg = 0.5 * (1.0 + t) + 0.5 * x * (1.0 - t * t) * (_GELU_K * (1.0 + 3.0 * _GELU_C * x2))
    return g, dg


def _neg_expm1(x):
    series = x * (1.0 + x * (0.5 + x * (1.0 / 6.0 + x * (1.0 / 24.0 + x * (1.0 / 120.0)))))
    return -jnp.where(jnp.abs(x) < 0.1, series, jnp.exp(x) - 1.0)


def _seq_tile(s, want):
    t = min(s, want)
    assert s % t == 0
    return t


class _Rider:
    def __init__(self, srcs, kinds):
        self.srcs, self.kinds = list(srcs), list(kinds)
        self.n = len(self.srcs)

    def out_shapes(self):
        shapes = []
        for x, kind in zip(self.srcs, self.kinds):
            if kind == "lead":
                shp = x.shape
            elif kind == "rows":
                shp = (N_DEV, x.shape[0] // N_DEV) + x.shape[1:]
            else:
                shp = (N_DEV,) + x.shape
            shapes.append(jax.ShapeDtypeStruct(shp, x.dtype))
        return shapes

    def scratch(self):
        return [pltpu.SemaphoreType.DMA((self.n * N_PEERS,)), pltpu.SemaphoreType.DMA((self.n * N_PEERS,)),
                pltpu.SemaphoreType.DMA((self.n,))]

    def _copies(self, x_refs, out_refs, sems):
        send_sems, recv_sems, local_sems = sems
        mx, my, mc = lax.axis_index("x"), lax.axis_index("y"), lax.axis_index("c")
        my_id = 4 * mx + 2 * my + mc

        def piece(i, dev):
            if self.kinds[i] == "lead":
                return x_refs[i].at[dev]
            if self.kinds[i] == "rows":
                r = x_refs[i].shape[0] // N_DEV
                return x_refs[i].at[pl.ds(pl.multiple_of(dev * r, SUBLANE), r)]
            return x_refs[i]

        mine = [pltpu.make_async_copy(piece(i, my_id), out_refs[i].at[my_id], local_sems.at[i]) for i in range(self.n)]
        copies = []
        for k in range(1, N_DEV):
            px, py, pc = mx ^ ((k >> 2) & 1), my ^ ((k >> 1) & 1), mc ^ (k & 1)
            for i in range(self.n):
                copies.append(pltpu.make_async_remote_copy(
                    src_ref=piece(i, 4 * px + 2 * py + pc), dst_ref=out_refs[i].at[my_id],
                    send_sem=send_sems.at[i * N_PEERS + k - 1], recv_sem=recv_sems.at[i * N_PEERS + k - 1],
                    device_id=(px, py, pc), device_id_type=_MESH))
        return mine, copies

    def start(self, x_refs, out_refs, sems):
        mine, copies = self._copies(x_refs, out_refs, sems)
        for cp in mine + copies:
            cp.start()

    def wait(self, x_refs, out_refs, sems):
        mine, copies = self._copies(x_refs, out_refs, sems)
        for cp in copies:
            cp.wait_recv()
        for cp in copies:
            cp.wait_send()
        for cp in mine:
            cp.wait()


def _call(body, *, grid, ins, in_specs, outs, out_specs, scratch=(), aliases=None, name, rider=None):
    n_axes = len(grid)
    common = dict(grid=grid, input_output_aliases=aliases or {}, compiler_params=_cparams(n_axes), name=name)
    if rider is None:
        res = pl.pallas_call(body, in_specs=list(in_specs), out_specs=list(out_specs), out_shape=list(outs),
                             scratch_shapes=list(scratch), **common)(*ins)
        return list(res), []
    n_in, n_out, n_scr, nr = len(ins), len(outs), len(scratch), rider.n

    def wrapped(*refs):
        pos = [0]

        def take(k):
            part = refs[pos[0]:pos[0] + k]
            pos[0] += k
            return part

        a_in, r_in, a_out, r_out, a_scr, sems = take(n_in), take(nr), take(n_out), take(nr), take(n_scr), take(3)
        first = last = None
        for ax in range(n_axes):
            pid = pl.program_id(ax)
            f, l = pid == 0, pid == grid[ax] - 1
            first = f if first is None else jnp.logical_and(first, f)
            last = l if last is None else jnp.logical_and(last, l)

        @pl.when(first)
        def _():
            rider.start(r_in, r_out, sems)

        body(*a_in, *a_out, *a_scr)

        @pl.when(last)
        def _():
            rider.wait(r_in, r_out, sems)

    res = pl.pallas_call(
        wrapped, in_specs=list(in_specs) + [_ANY] * nr, out_specs=list(out_specs) + [_ANY] * nr,
        out_shape=list(outs) + rider.out_shapes(), scratch_shapes=list(scratch) + rider.scratch(), **common)(*ins, *rider.srcs)
    return list(res[:n_out]), list(res[n_out:])


def _block_mask(n_blocks, block_rows, block_cols):
    r = jnp.arange(n_blocks * block_rows) // block_rows
    c = jnp.arange(n_blocks * block_cols) // block_cols
    return (r[:, None] == c[None, :]).astype(F32)


def _mm(a, b, *, bias=None, res=None, res_scale=1.0, trans_b=False, out_dtype=F32, ln=None, name, rider=None):
    m, kdim = a.shape
    n = b.shape[0] if trans_b else b.shape[1]
    tm = _seq_tile(m, 1024)
    tn = _pick(n, 1408)
    tk = _pick(kdim, 1536)
    nk = kdim // tk
    has_bias, has_res, has_ln = bias is not None, res is not None, ln is not None
    assert not has_ln or tn == n

    def body(*refs):
        a_ref, b_ref = refs[0], refs[1]
        pos = 2
        bias_ref = res_ref = g_ref = beta_ref = x_ref = None
        if has_bias:
            bias_ref = refs[pos]
            pos += 1
        if has_res:
            res_ref = refs[pos]
            pos += 1
        if has_ln:
            g_ref, beta_ref = refs[pos], refs[pos + 1]
            pos += 2
        o_ref = refs[pos]
        pos += 1
        if has_ln:
            x_ref = refs[pos]
            pos += 1
        acc_ref = refs[pos]
        k = pl.program_id(2)

        @pl.when(k == 0)
        def _():
            acc_ref[...] = jnp.zeros_like(acc_ref)

        if trans_b:
            acc_ref[...] += _dot_nt(a_ref[...], b_ref[...])
        else:
            acc_ref[...] += _dot(a_ref[...], b_ref[...])

        @pl.when(k == nk - 1)
        def _():
            r = acc_ref[...]
            if has_bias:
                r = r + bias_ref[...]
            if has_res:
                r = r + res_scale * res_ref[...]
            o_ref[...] = r.astype(out_dtype)
            if has_ln:
                xc = r - jnp.mean(r, axis=1, keepdims=True)
                var = jnp.mean(xc * xc, axis=1, keepdims=True)
                x_ref[...] = xc * lax.rsqrt(var + LN_EPS) * g_ref[...] + beta_ref[...]

    ins = [a, b]
    in_specs = [pl.BlockSpec((tm, tk), lambda i, j, k: (i, k)),
                pl.BlockSpec((tn, tk), lambda i, j, k: (j, k)) if trans_b
                else pl.BlockSpec((tk, tn), lambda i, j, k: (k, j))]
    if has_bias:
        ins.append(bias)
        in_specs.append(pl.BlockSpec((1, tn), lambda i, j, k: (0, j)))
    if has_res:
        ins.append(res)
        in_specs.append(pl.BlockSpec((tm, tn), lambda i, j, k: (i, j)))
    if has_ln:
        ins += list(ln)
        in_specs += [pl.BlockSpec((1, tn), lambda i, j, k: (0, j))] * 2
    tile = pl.BlockSpec((tm, tn), lambda i, j, k: (i, j))
    outs, routs = _call(
        body, grid=(m // tm, n // tn, nk), ins=ins, in_specs=in_specs,
        outs=[jax.ShapeDtypeStruct((m, n), out_dtype)] + ([jax.ShapeDtypeStruct((m, n), F32)] if has_ln else []),
        out_specs=[tile] * (2 if has_ln else 1),
        scratch=[pltpu.VMEM((tm, tn), F32)], name=name, rider=rider)
    out = tuple(outs) if has_ln else outs[0]
    return out if rider is None else (out, routs)


def _mm_tn(a, b, *, colsum=False, name, rider=None):
    s, ka = a.shape
    nb = b.shape[1]
    ts = _seq_tile(s, 512)
    tka = _pick(ka, 1408)
    tnb = _pick(nb, 1408)
    assert not colsum or tka == ka

    def body(a_ref, b_ref, o_ref, *rest):
        @pl.when(pl.program_id(2) == 0)
        def _():
            o_ref[...] = jnp.zeros_like(o_ref)
            if colsum:
                rest[0][...] = jnp.zeros_like(rest[0])

        bv = b_ref[...]
        o_ref[...] += _dot_tn(a_ref[...], bv)
        if colsum:
            rest[0][...] += _colsum(bv.astype(F32))

    outs, routs = _call(
        body, grid=(ka // tka, nb // tnb, s // ts), ins=[a, b],
        in_specs=[pl.BlockSpec((ts, tka), lambda i, j, k: (k, i)), pl.BlockSpec((ts, tnb), lambda i, j, k: (k, j))],
        outs=[jax.ShapeDtypeStruct((ka, nb), F32)] + ([jax.ShapeDtypeStruct((1, nb), F32)] if colsum else []),
        out_specs=[pl.BlockSpec((tka, tnb), lambda i, j, k: (i, j))] +
                  ([pl.BlockSpec((1, tnb), lambda i, j, k: (0, j))] if colsum else []),
        name=name, rider=rider)
    out = tuple(outs) if colsum else outs[0]
    return out if rider is None else (out, routs)


def _ln_fwd(r, g, b, *, name, rider=None):
    s, d = r.shape
    ts = _seq_tile(s, 512)

    def body(r_ref, g_ref, b_ref, o_ref):
        x = r_ref[...]
        mu = jnp.mean(x, axis=1, keepdims=True)
        xc = x - mu
        var = jnp.mean(xc * xc, axis=1, keepdims=True)
        o_ref[...] = xc * lax.rsqrt(var + LN_EPS) * g_ref[...] + b_ref[...]

    (out,), routs = _call(
        body, grid=(s // ts,), ins=[r, g, b],
        in_specs=[pl.BlockSpec((ts, d), lambda i: (i, 0)), _full_spec(g), _full_spec(b)],
        out_specs=[pl.BlockSpec((ts, d), lambda i: (i, 0))], outs=[jax.ShapeDtypeStruct((s, d), F32)],
        name=name, rider=rider)
    return out if rider is None else (out, routs)


def _ln_bwd(r, dy, g, *, name, rider=None):
    s, d = r.shape
    ts = _seq_tile(s, 512)

    def body(r_ref, dy_ref, g_ref, dr_ref, dg_ref, db_ref, ds_ref):
        @pl.when(pl.program_id(0) == 0)
        def _():
            dg_ref[...] = jnp.zeros_like(dg_ref)
            db_ref[...] = jnp.zeros_like(db_ref)
            ds_ref[...] = jnp.zeros_like(ds_ref)

        x = r_ref[...]
        dy = dy_ref[...]
        mu = jnp.mean(x, axis=1, keepdims=True)
        xc = x - mu
        var = jnp.mean(xc * xc, axis=1, keepdims=True)
        rstd = lax.rsqrt(var + LN_EPS)
        xh = xc * rstd
        dxh = dy * g_ref[...]
        m1 = jnp.mean(dxh, axis=1, keepdims=True)
        m2 = jnp.mean(dxh * xh, axis=1, keepdims=True)
        dr = rstd * (dxh - m1 - xh * m2)
        dr_ref[...] = dr
        dg_ref[...] += _colsum(dy * xh)
        db_ref[...] += _colsum(dy)
        ds_ref[...] += _colsum(dr)

    vec = jax.ShapeDtypeStruct((1, d), F32)
    vspec = pl.BlockSpec((1, d), lambda i: (0, 0))
    outs, routs = _call(
        body, grid=(s // ts,), ins=[r, dy, g],
        in_specs=[pl.BlockSpec((ts, d), lambda i: (i, 0)), pl.BlockSpec((ts, d), lambda i: (i, 0)), _full_spec(g)],
        out_specs=[pl.BlockSpec((ts, d), lambda i: (i, 0)), vspec, vspec, vspec],
        outs=[jax.ShapeDtypeStruct((s, d), F32), vec, vec, vec], name=name, rider=rider)
    return outs if rider is None else (outs, routs)


def _loss_ln_bwd(r, g, b, target, *, name):
    s, d = r.shape
    ts = _seq_tile(s, 512)

    def body(r_ref, g_ref, b_ref, t_ref, dr_ref, dg_ref, db_ref, l_ref):
        @pl.when(pl.program_id(0) == 0)
        def _():
            dg_ref[...] = jnp.zeros_like(dg_ref)
            db_ref[...] = jnp.zeros_like(db_ref)
            l_ref[...] = jnp.zeros_like(l_ref)

        x = r_ref[...]
        gam = g_ref[...]
        xc = x - jnp.mean(x, axis=1, keepdims=True)
        var = jnp.mean(xc * xc, axis=1, keepdims=True)
        rstd = lax.rsqrt(var + LN_EPS)
        xh = xc * rstd
        e = xh * gam + b_ref[...] - t_ref[...]
        part = jnp.sum(jnp.sum(e * e, axis=1, keepdims=True), axis=0, keepdims=True) * (0.5 / d)
        l_ref[...] += jnp.broadcast_to(part, l_ref.shape)
        dy = e * (1.0 / d)
        dxh = dy * gam
        m1 = jnp.mean(dxh, axis=1, keepdims=True)
        m2 = jnp.mean(dxh * xh, axis=1, keepdims=True)
        dr_ref[...] = rstd * (dxh - m1 - xh * m2)
        dg_ref[...] += _colsum(dy * xh)
        db_ref[...] += _colsum(dy)

    vec = jax.ShapeDtypeStruct((1, d), F32)
    vspec = pl.BlockSpec((1, d), lambda i: (0, 0))
    tile = pl.BlockSpec((ts, d), lambda i: (i, 0))
    return pl.pallas_call(
        body, grid=(s // ts,), in_specs=[tile, _full_spec(g), _full_spec(b), tile],
        out_specs=[tile, vspec, vspec, pl.BlockSpec((SUBLANE, LANE), lambda i: (0, 0))],
        out_shape=[jax.ShapeDtypeStruct((s, d), F32), vec, vec, jax.ShapeDtypeStruct((SUBLANE, LANE), F32)],
        compiler_params=_cparams(1), name=name)(r, g, b, target)


SCAN_CHUNK = 32


def _cscan_levels(bufs, apow_ref, t, pad, *, reverse):
    half = bufs[0].shape[1] // 2
    ch = min(SCAN_CHUNK, t)
    nlev = t.bit_length() - 1
    assert (1 << nlev) == t
    for k in range(nlev):
        d = 1 << k
        src, dst = bufs[k % 2], bufs[(k + 1) % 2]

        def chunk(c, carry, src=src, dst=dst, d=d, k=k):
            ar = apow_ref[k:k + 1, :half]
            ai = apow_ref[k:k + 1, half:]
            if reverse:
                ai = -ai
            r0 = pl.multiple_of(c * ch, ch)
            cur = src[pl.ds(pad + r0, ch), :]
            if d >= SUBLANE:
                off = pad + d if reverse else pad - d
                sh = src[pl.ds(off + r0, ch), :]
            elif reverse:
                blk = src[pl.ds(pad + r0, ch + SUBLANE), :]
                sh = pltpu.roll(blk, ch + SUBLANE - d, axis=0)[:ch, :]
            else:
                blk = src[pl.ds(pad - SUBLANE + r0, ch + SUBLANE), :]
                sh = pltpu.roll(blk, d, axis=0)[SUBLANE:, :]
            sre, sim = sh[:, :half], sh[:, half:]
            dst[pl.ds(pad + r0, ch), :half] = cur[:, :half] + ar * sre - ai * sim
            dst[pl.ds(pad + r0, ch), half:] = cur[:, half:] + ar * sim + ai * sre
            return carry

        lax.fori_loop(0, t // ch, chunk, 0)
    return nlev % 2


def _rscan_levels(abufs, bbufs, t, pad, *, reverse):
    nlev = t.bit_length() - 1
    assert (1 << nlev) == t
    for k in range(nlev):
        d = 1 << k
        asrc, adst = abufs[k % 2], abufs[(k + 1) % 2]
        bsrc, bdst = bbufs[k % 2], bbufs[(k + 1) % 2]
        off = pad + d if reverse else pad - d
        a = asrc[pad:pad + t, :]
        bdst[pad:pad + t, :] = a * bsrc[off:off + t, :] + bsrc[pad:pad + t, :]
        if k < nlev - 1:
            adst[pad:pad + t, :] = a * asrc[off:off + t, :]
    return nlev % 2


S5_CHUNK = 16
S5_SG = S5_GROUPS // 2
S5_SG_IN = 2 * S5_CHUNK * S5_CH
S5_SG_ST = 2 * S5_STATE


def _s5_to_chunks(u, dtype):
    s = u.shape[0]
    x = u.reshape(s // S5_CHUNK, S5_CHUNK, S5_SG, 2, S5_CH)
    return jnp.transpose(x, (2, 0, 3, 1, 4)).reshape(S5_SG, s // S5_CHUNK, S5_SG_IN).astype(dtype)


def _s5_from_chunks(y):
    nb = y.shape[1]
    x = y.reshape(S5_SG, nb, 2, S5_CHUNK, S5_CH)
    return jnp.transpose(x, (1, 3, 0, 2, 4)).reshape(nb * S5_CHUNK, D_GROUP)


def _s5_core_fwd(u2, m2, pre, pim, qre, qim, a16, *, name):
    sg, nb, nin = u2.shape
    st2 = 2 * S5_SG_ST
    pad = nb // 2

    def body(u_ref, m_ref, pre_ref, pim_ref, qre_ref, qim_ref, a_ref, y_ref, x_ref, buf0, buf1):
        @pl.when(pl.program_id(0) == 0)
        def _():
            buf0[0:pad, :] = jnp.zeros((pad, st2), F32)
            buf1[0:pad, :] = jnp.zeros((pad, st2), F32)

        u = u_ref[...]
        buf0[pad:pad + nb, :S5_SG_ST] = _dot(u, pre_ref[...])
        buf0[pad:pad + nb, S5_SG_ST:] = _dot(u, pim_ref[...])
        xbuf = (buf0, buf1)[_cscan_levels((buf0, buf1), a_ref, nb, pad, reverse=False)]
        x_ref[...] = xbuf[pad:pad + nb, :]
        xprev = xbuf[pad - 1:pad - 1 + nb, :]
        y_ref[...] = _dot(u, m_ref[...]) + _dot(xprev[:, :S5_SG_ST], qre_ref[...]) + _dot(xprev[:, S5_SG_ST:], qim_ref[...])

    ins = [u2, m2, pre, pim, qre, qim, a16]
    return pl.pallas_call(
        body, grid=(sg,), in_specs=[pl.BlockSpec((None,) + a.shape[1:], lambda i: (i, 0, 0)) for a in ins],
        out_specs=[pl.BlockSpec((None, nb, nin), lambda i: (i, 0, 0)), pl.BlockSpec((None, nb, st2), lambda i: (i, 0, 0))],
        out_shape=[jax.ShapeDtypeStruct((sg, nb, nin), F32), jax.ShapeDtypeStruct((sg, nb, st2), F32)],
        scratch_shapes=[pltpu.VMEM((pad + nb, st2), F32), pltpu.VMEM((pad + nb, st2), F32)],
        compiler_params=_cparams(1), name=name)(*ins)


def _s5_core_bwd(u2, dy2, x_all, m2, pre, pim, qre, qim, a16, *, name):
    sg, nb, nin = u2.shape
    half = S5_SG_ST
    st2 = 2 * half
    pad = nb // 2

    def body(u_ref, dy_ref, x_ref, m_ref, pre_ref, pim_ref, qre_ref, qim_ref, a_ref,
             du_ref, dm_ref, dpre_ref, dpim_ref, dqre_ref, dqim_ref, da_ref, buf2, buf3, xp):
        @pl.when(pl.program_id(0) == 0)
        def _():
            buf2[nb:nb + pad, :] = jnp.zeros((pad, st2), F32)
            buf3[nb:nb + pad, :] = jnp.zeros((pad, st2), F32)
            xp[0:SUBLANE, :] = jnp.zeros((SUBLANE, st2), F32)

        u = u_ref[...]
        dy = dy_ref[...]
        dm_ref[...] = _dot_tn(u, dy)
        xp[SUBLANE:SUBLANE + nb, :] = x_ref[...]
        xprev = xp[SUBLANE - 1:SUBLANE - 1 + nb, :]
        xre, xim = xprev[:, :half], xprev[:, half:]
        dqre_ref[...] = _dot_tn(xre, dy)
        dqim_ref[...] = _dot_tn(xim, dy)
        buf2[0:nb, :half] = _dot_nt(dy, qre_ref[...])
        buf2[0:nb, half:] = _dot_nt(dy, qim_ref[...])
        mbuf = (buf2, buf3)[_cscan_levels((buf2, buf3), a_ref, nb, 0, reverse=True)]
        lam = mbuf[1:1 + nb, :]
        lre, lim = lam[:, :half], lam[:, half:]
        dpre_ref[...] = _dot_tn(u, lre)
        dpim_ref[...] = _dot_tn(u, lim)
        du_ref[...] = _dot_nt(dy, m_ref[...]) + _dot_nt(lre, pre_ref[...]) + _dot_nt(lim, pim_ref[...])
        da_ref[:, :half] = _colsum(lre * xre + lim * xim)
        da_ref[:, half:] = _colsum(lim * xre - lre * xim)

    ins = [u2, dy2, x_all, m2, pre, pim, qre, qim, a16]
    outs = [jax.ShapeDtypeStruct((sg, nb, nin), F32)] + [jax.ShapeDtypeStruct(a.shape, F32) for a in (m2, pre, pim, qre, qim)] + \
           [jax.ShapeDtypeStruct((sg, 1, st2), F32)]
    return pl.pallas_call(
        body, grid=(sg,), in_specs=[pl.BlockSpec((None,) + a.shape[1:], lambda i: (i, 0, 0)) for a in ins],
        out_specs=[pl.BlockSpec((None,) + o.shape[1:], lambda i: (i, 0, 0)) for o in outs], out_shape=outs,
        scratch_shapes=[pltpu.VMEM((nb + pad, st2), F32), pltpu.VMEM((nb + pad, st2), F32),
                        pltpu.VMEM((SUBLANE + nb, st2), F32)],
        compiler_params=_cparams(1), name=name)(*ins)


def _s5_glu_fwd(y1, wglu, bglu, *, name, rider=None):
    s = y1.shape[0]
    t = _seq_tile(s, 512)

    def body(y1_ref, wglu_ref, bglu_ref, out_ref):
        y2 = _gelu(y1_ref[...])
        out_ref[...] = (y2 * _sigmoid(_dot(y2, wglu_ref[...]) + bglu_ref[...])).astype(BF16)

    return _call(
        body, grid=(s // t,), ins=[y1, wglu, bglu],
        in_specs=[pl.BlockSpec((t, D_GROUP), lambda i: (i, 0)), _full_spec(wglu), _full_spec(bglu)],
        out_specs=[pl.BlockSpec((t, D_GROUP), lambda i: (i, MIX_S5))], outs=[jax.ShapeDtypeStruct((s, D_MODEL), BF16)],
        name=name, rider=rider)


def _s5_glu_bwd(y1, dmix, wglu, bglu, *, name):
    s = y1.shape[0]
    t = _seq_tile(s, 512)

    def body(y1_ref, do_ref, wglu_ref, bglu_ref, dy1_ref, dwglu_ref, dbglu_ref):
        @pl.when(pl.program_id(0) == 0)
        def _():
            dwglu_ref[...] = jnp.zeros_like(dwglu_ref)
            dbglu_ref[...] = jnp.zeros_like(dbglu_ref)

        dout = do_ref[...]
        y2, dgelu = _gelu_and_grad(y1_ref[...])
        sg = _sigmoid(_dot(y2, wglu_ref[...]) + bglu_ref[...])
        dz = dout * y2 * sg * (1.0 - sg)
        dwglu_ref[...] += _dot_tn(y2, dz)
        dbglu_ref[...] += _colsum(dz)
        dy1_ref[...] = (dout * sg + _dot_nt(dz, wglu_ref[...])) * dgelu

    outs = [jax.ShapeDtypeStruct((s, D_GROUP), F32), jax.ShapeDtypeStruct((D_GROUP, D_GROUP), F32),
            jax.ShapeDtypeStruct((1, D_GROUP), F32)]
    return pl.pallas_call(
        body, grid=(s // t,),
        in_specs=[pl.BlockSpec((t, D_GROUP), lambda i: (i, 0)), pl.BlockSpec((t, D_GROUP), lambda i: (i, MIX_S5)),
                  _full_spec(wglu), _full_spec(bglu)],
        out_specs=[pl.BlockSpec((t, D_GROUP), lambda i: (i, 0)), _full_spec(outs[1]), _full_spec(outs[2])],
        out_shape=outs, compiler_params=_cparams(1), name=name)(y1, dmix, wglu, bglu)


def _pair_blockdiag(x):
    g, r, c = x.shape
    x = x.reshape(g // 2, 2, r, c)
    z = jnp.zeros_like(x[:, 0])
    return jnp.concatenate([jnp.concatenate([x[:, 0], z], axis=2), jnp.concatenate([z, x[:, 1]], axis=2)], axis=1)


def _s5_chunk_map(lam_re, lam_im, log_dt, b_re, b_im, c_re, c_im, d_skip):
    g, n, c, lc = S5_GROUPS, S5_STATE, S5_CH, S5_CHUNK
    dt = jnp.exp(log_dt)[:, None]
    mag, ang = lam_re * dt, lam_im * dt
    j = jnp.arange(lc + 1, dtype=F32)[:, None, None]
    pw_mag = jnp.exp(j * mag)
    pw_re, pw_im = pw_mag * jnp.cos(j * ang), pw_mag * jnp.sin(j * ang)
    a_re, a_im = pw_re[1], pw_im[1]
    den = lam_re * lam_re + lam_im * lam_im
    n_re = a_re - 1.0
    k_re = (n_re * lam_re + a_im * lam_im) / den
    k_im = (a_im * lam_re - n_re * lam_im) / den
    bb_re = k_re[..., None] * b_re - k_im[..., None] * b_im
    bb_im = k_re[..., None] * b_im + k_im[..., None] * b_re
    e_re = pw_re[:lc, :, :, None] * bb_re - pw_im[:lc, :, :, None] * bb_im
    e_im = pw_re[:lc, :, :, None] * bb_im + pw_im[:lc, :, :, None] * bb_re
    kern = jnp.einsum("gdn,jgnc->jgdc", c_re, e_re) - jnp.einsum("gdn,jgnc->jgdc", c_im, e_im)
    steps = jnp.arange(lc)
    lag = (steps[None, :, None] + steps[:, None, None] == steps[None, None, :]).astype(F32)
    m = jnp.einsum("jst,jgdc->gsctd", lag, kern).reshape(g, lc * c, lc * c)
    skip = jnp.tile(d_skip.reshape(g, 1, c), (1, lc, 1)).reshape(g, lc * c)
    m = m + jnp.eye(lc * c, dtype=F32)[None] * skip[:, None, :]
    p_re = jnp.transpose(e_re[::-1], (1, 0, 3, 2)).reshape(g, lc * c, n)
    p_im = jnp.transpose(e_im[::-1], (1, 0, 3, 2)).reshape(g, lc * c, n)
    f_re = c_re[None] * pw_re[1:, :, None, :] - c_im[None] * pw_im[1:, :, None, :]
    f_im = c_re[None] * pw_im[1:, :, None, :] + c_im[None] * pw_re[1:, :, None, :]
    q_re = jnp.transpose(f_re, (1, 3, 0, 2)).reshape(g, n, lc * c)
    q_im = -jnp.transpose(f_im, (1, 3, 0, 2)).reshape(g, n, lc * c)
    a16 = jnp.concatenate([pw_re[lc].reshape(S5_SG, 1, S5_SG_ST), pw_im[lc].reshape(S5_SG, 1, S5_SG_ST)], axis=2)
    return (_pair_blockdiag(m), _pair_blockdiag(p_re), _pair_blockdiag(p_im), _pair_blockdiag(q_re),
            _pair_blockdiag(q_im), a16)


def _s5_a16_powers(a16, nlev):
    half = S5_SG_ST
    re, im = a16[:, :, :half], a16[:, :, half:]
    rows = []
    for _ in range(nlev):
        rows.append(jnp.concatenate([re, im], axis=2))
        re, im = re * re - im * im, 2.0 * re * im
    n_rows = -(-nlev // SUBLANE) * SUBLANE
    rows += [jnp.zeros_like(rows[0])] * (n_rows - nlev)
    return lax.stop_gradient(jnp.concatenate(rows, axis=1))


CV_TILE = 256
CV_PAD = 32
CV_CHUNK = 64


def _shifted_copies(buf, shifted, rows):
    n = rows - SUBLANE
    for s in range(1, SUBLANE):
        shifted[s - 1, 0:n, :] = buf[s:s + n, :]


def _window(buf, shifted, o, ch):
    q, s = divmod(o, SUBLANE)
    if s == 0:
        return buf[o:o + ch, :]
    return shifted[s - 1, q * SUBLANE:q * SUBLANE + ch, :]


def _gn_stats(c, mavg):
    mu = _dot_hi(c, mavg)
    cen = c - mu
    var = _dot_hi(cen * cen, mavg)
    rstd = lax.rsqrt(var + LN_EPS)
    return cen * rstd, rstd


def _cv_fwd(h_in, cw, cb, gng, gnb, mavg, wpw, bpw, mix, *, name, rider=None):
    s = h_in.shape[0]
    t = _seq_tile(s, CV_TILE)
    ch = min(CV_CHUNK, t)

    def body(v_ref, g_ref, cw_ref, cb_ref, gng_ref, gnb_ref, mavg_ref, wpw_ref, bpw_ref, _mix_in, out_ref, c_ref, xpad,
             shifted):
        @pl.when(pl.program_id(0) == 0)
        def _():
            xpad[0:CV_PAD, :] = jnp.zeros((CV_PAD, D_GROUP), F32)

        xpad[CV_PAD:CV_PAD + t, :] = v_ref[...] * _sigmoid(g_ref[...])
        _shifted_copies(xpad, shifted, t + CV_PAD)
        for r0 in range(0, t, ch):
            acc = jnp.broadcast_to(cb_ref[...], (ch, D_GROUP))
            for k in range(CONV_WIDTH):
                o = CV_PAD - (CONV_WIDTH - 1) + k + r0
                acc = acc + cw_ref[k:k + 1, :] * _window(xpad, shifted, o, ch)
            c_ref[r0:r0 + ch, :] = acc
        xpad[0:CV_PAD, :] = xpad[t:t + CV_PAD, :]
        xn, _ = _gn_stats(c_ref[...], mavg_ref[...])
        gn = xn * gng_ref[...] + gnb_ref[...]
        out_ref[...] = (_dot(gn * _sigmoid(gn), wpw_ref[...]) + bpw_ref[...]).astype(BF16)

    ins = [h_in, h_in, cw, cb, gng, gnb, mavg, wpw, bpw, mix]
    in_specs = [pl.BlockSpec((t, D_GROUP), lambda i: (i, COL_CV_V)), pl.BlockSpec((t, D_GROUP), lambda i: (i, COL_CV_G))] + \
               [_full_spec(a) for a in ins[2:9]] + [_ANY]
    return _call(
        body, grid=(s // t,), ins=ins, in_specs=in_specs,
        out_specs=[pl.BlockSpec((t, D_GROUP), lambda i: (i, MIX_CV)), pl.BlockSpec((t, D_GROUP), lambda i: (i, 0))],
        outs=[jax.ShapeDtypeStruct((s, D_MODEL), BF16), jax.ShapeDtypeStruct((s, D_GROUP), F32)],
        aliases={9: 0},
        scratch=[pltpu.VMEM((CV_PAD + t, D_GROUP), F32), pltpu.VMEM((SUBLANE - 1, CV_PAD + t, D_GROUP), F32)],
        name=name, rider=rider)


def _cv_bwd(h_in, c, dmix, cw, gng, gnb, mavg, wpw, *, name, rider=None):
    s = h_in.shape[0]
    t = _seq_tile(s, CV_TILE)
    nt = s // t
    ch = min(CV_CHUNK, t)

    def body(v_ref, g_ref, c_ref, do_ref, cw_ref, gng_ref, gnb_ref, mavg_ref, wpw_ref,
             dvg_ref, dwpw_ref, dcw_ref, dbpw_ref, dgg_ref, dgb_ref, dcb_ref, dcpad, hgbuf, shifted):
        @pl.when(pl.program_id(0) == 0)
        def _():
            dcpad[t:t + CV_PAD, :] = jnp.zeros((CV_PAD, D_GROUP), F32)
            for r in (dwpw_ref, dcw_ref, dbpw_ref, dgg_ref, dgb_ref, dcb_ref):
                r[...] = jnp.zeros_like(r)

        mavg = mavg_ref[...]
        xn, rstd = _gn_stats(c_ref[...], mavg)
        gg = gng_ref[...]
        gn = xn * gg + gnb_ref[...]
        sg = _sigmoid(gn)
        dout = do_ref[...]
        dwpw_ref[...] += _dot_tn(gn * sg, dout)
        dbpw_ref[...] += _colsum(dout)
        dgn = _dot_nt(dout, wpw_ref[...]) * (sg * (1.0 + gn * (1.0 - sg)))
        dgg_ref[...] += _colsum(dgn * xn)
        dgb_ref[...] += _colsum(dgn)
        dxn = dgn * gg
        dc = rstd * (dxn - _dot_hi(dxn, mavg) - xn * _dot_hi(dxn * xn, mavg))
        dcb_ref[...] += _colsum(dc)
        dcpad[0:t, :] = dc

        v = v_ref[...]
        sgm = _sigmoid(g_ref[...])
        hgbuf[...] = v * sgm
        _shifted_copies(dcpad, shifted, t + CV_PAD)
        for r0 in range(0, t, ch):
            hg = hgbuf[r0:r0 + ch, :]
            acc = jnp.zeros((ch, D_GROUP), F32)
            for k in range(CONV_WIDTH):
                o = (CONV_WIDTH - 1) - k + r0
                sh = _window(dcpad, shifted, o, ch)
                acc = acc + cw_ref[k:k + 1, :] * sh
                dcw_ref[k:k + 1, :] += _colsum(hg * sh)
            hgbuf[r0:r0 + ch, :] = acc
        dcpad[t:t + CV_PAD, :] = dcpad[0:CV_PAD, :]
        dhg = hgbuf[...]
        dvg_ref[:, :D_GROUP] = dhg * sgm
        dvg_ref[:, D_GROUP:] = dhg * v * sgm * (1.0 - sgm)

    def rev(col):
        return lambda i: (nt - 1 - i, col)

    ins = [h_in, h_in, c, dmix, cw, gng, gnb, mavg, wpw]
    in_specs = [pl.BlockSpec((t, D_GROUP), rev(COL_CV_V)), pl.BlockSpec((t, D_GROUP), rev(COL_CV_G)),
                pl.BlockSpec((t, D_GROUP), rev(0)), pl.BlockSpec((t, D_GROUP), rev(MIX_CV))] + [_full_spec(a) for a in ins[4:]]
    vec = jax.ShapeDtypeStruct((1, D_GROUP), F32)
    outs = [jax.ShapeDtypeStruct((s, N_IN_COLS), F32),
            jax.ShapeDtypeStruct((D_GROUP, D_GROUP), F32), jax.ShapeDtypeStruct((CV_PAD, D_GROUP), F32), vec, vec, vec, vec]
    out_specs = [pl.BlockSpec((t, 2 * D_GROUP), rev(COL_CV_V // 2))] + [_full_spec(o) for o in outs[1:]]
    return _call(
        body, grid=(nt,), ins=ins, in_specs=in_specs, out_specs=out_specs, outs=outs,
        scratch=[pltpu.VMEM((t + CV_PAD, D_GROUP), F32), pltpu.VMEM((t, D_GROUP), F32),
                 pltpu.VMEM((SUBLANE - 1, t + CV_PAD, D_GROUP), F32)], name=name, rider=rider)


LRU_TILE = 256


def _lru_gates(xc, wr_ref, br_ref, wi_ref, bi_ref, sp_ref):
    r = _sigmoid(_dot(xc, wr_ref[...]) + br_ref[...])
    i = _sigmoid(_dot(xc, wi_ref[...]) + bi_ref[...])
    log_a = -LRU_C * r * sp_ref[...]
    a = jnp.exp(log_a)
    m = jnp.sqrt(_neg_expm1(2.0 * log_a))
    return r, i, a, m


def _lru_fwd(h_in, lcw, lcb, wr, br, wi, bi, sp, mix, *, name):
    s = h_in.shape[0]
    t = _seq_tile(s, LRU_TILE)
    pad = max(t // 2, SUBLANE)

    def body(xg_ref, xr_ref, lcw_ref, lcb_ref, wr_ref, br_ref, wi_ref, bi_ref, sp_ref, _mix_in,
             out_ref, xc_ref, h_ref, xpad, a0, a1, b0, b1, carry):
        @pl.when(pl.program_id(0) == 0)
        def _():
            xpad[0:SUBLANE, :] = jnp.zeros((SUBLANE, D_GROUP), F32)
            for bf in (a0, a1, b0, b1):
                bf[0:pad, :] = jnp.zeros((pad, D_GROUP), F32)
            carry[...] = jnp.zeros_like(carry)

        xpad[SUBLANE:SUBLANE + t, :] = xr_ref[...]
        xc = jnp.broadcast_to(lcb_ref[...], (t, D_GROUP))
        for k in range(LRU_CONV_WIDTH):
            o = SUBLANE - (LRU_CONV_WIDTH - 1) + k
            xc = xc + lcw_ref[k:k + 1, :] * xpad[o:o + t, :]
        xpad[0:SUBLANE, :] = xpad[t:t + SUBLANE, :]
        xc_ref[...] = xc
        _, i, a, m = _lru_gates(xc, wr_ref, br_ref, wi_ref, bi_ref, sp_ref)
        a0[pad:pad + t, :] = a
        b0[pad:pad + t, :] = m * (i * xc)
        b0[pad:pad + 1, :] += a0[pad:pad + 1, :] * carry[0:1, :]
        fin = _rscan_levels((a0, a1), (b0, b1), t, pad, reverse=False)
        hbuf = (b0, b1)[fin]
        carry[0:1, :] = hbuf[pad + t - 1:pad + t, :]
        h = hbuf[pad:pad + t, :]
        h_ref[...] = h
        out_ref[...] = (h * _gelu(xg_ref[...])).astype(BF16)

    ins = [h_in, h_in, lcw, lcb, wr, br, wi, bi, sp, mix]
    row = pl.BlockSpec((t, D_GROUP), lambda i: (i, 0))
    in_specs = [pl.BlockSpec((t, D_GROUP), lambda i: (i, COL_LRU_G)), pl.BlockSpec((t, D_GROUP), lambda i: (i, COL_LRU_X))] + \
               [_full_spec(a) for a in ins[2:9]] + [_ANY]
    return pl.pallas_call(
        body, grid=(s // t,), in_specs=in_specs,
        out_specs=[pl.BlockSpec((t, D_GROUP), lambda i: (i, MIX_LRU)), row, row],
        out_shape=[jax.ShapeDtypeStruct((s, D_MODEL), BF16)] + [jax.ShapeDtypeStruct((s, D_GROUP), F32)] * 2,
        input_output_aliases={9: 0},
        scratch_shapes=[pltpu.VMEM((SUBLANE + t, D_GROUP), F32)] + [pltpu.VMEM((pad + t, D_GROUP), F32)] * 4 +
                       [pltpu.VMEM((SUBLANE, D_GROUP), F32)],
        compiler_params=_cparams(1), name=name)(*ins)


def _lru_bwd(h_in, xc_all, h_all, dmix, lcw, wr, br, wi, bi, sp, dh_all, *, name):
    s = h_in.shape[0]
    t = _seq_tile(s, LRU_TILE)
    nt = s // t
    pad = max(t // 2, SUBLANE)
    tb = t // SUBLANE

    def body(xg_ref, xr_ref, xc_ref, h_ref, hprev_ref, do_ref, lcw_ref, wr_ref, br_ref, wi_ref, bi_ref, sp_ref, _dh_in,
             dgr_ref, dwr_ref, dwi_ref, dlcw_ref, dbr_ref, dbi_ref, dsp_ref, dlcb_ref,
             a0, a1, b0, b1, hp, dxpad, carry):
        pid = pl.program_id(0)

        @pl.when(pid == 0)
        def _():
            for bf in (a0, a1, b0, b1):
                bf[pad + t:pad + t + pad, :] = jnp.zeros((pad, D_GROUP), F32)
            dxpad[t:t + SUBLANE, :] = jnp.zeros((SUBLANE, D_GROUP), F32)
            carry[...] = jnp.zeros_like(carry)
            for r in (dwr_ref, dwi_ref, dlcw_ref, dbr_ref, dbi_ref, dsp_ref, dlcb_ref):
                r[...] = jnp.zeros_like(r)

        xc = xc_ref[...]
        h = h_ref[...]
        dout = do_ref[...]
        gate, dgate = _gelu_and_grad(xg_ref[...])
        dgr_ref[:, :D_GROUP] = dout * h * dgate
        r, i, a, m = _lru_gates(xc, wr_ref, br_ref, wi_ref, bi_ref, sp_ref)

        a0[pad:pad + t, :] = a
        b0[pad:pad + t, :] = dout * gate
        b0[pad + t - 1:pad + t, :] += carry[0:1, :]
        a1[pad:pad + t, :] = a0[pad + 1:pad + 1 + t, :]
        fin = _rscan_levels((a1, a0), (b0, b1), t, pad, reverse=True)
        lam = (b0, b1)[fin][pad:pad + t, :]
        carry[0:1, :] = a[0:1, :] * lam[0:1, :]

        is_first = pid == nt - 1
        hp[0:SUBLANE, :] = jnp.where(is_first, 0.0, hprev_ref[...])
        hp[SUBLANE:SUBLANE + t, :] = h
        hprev = hp[SUBLANE - 1:SUBLANE - 1 + t, :]

        ix = i * xc
        dmm = lam * ix
        dix = lam * m
        da = lam * hprev - dmm * (a / m)
        dlog_a = da * a
        dr = dlog_a * (-LRU_C * sp_ref[...])
        dsp_ref[...] += _colsum(dlog_a * (-LRU_C * r))
        dpr = dr * r * (1.0 - r)
        dpi = dix * xc * i * (1.0 - i)
        dbr_ref[...] += _colsum(dpr)
        dbi_ref[...] += _colsum(dpi)
        dwr_ref[...] += _dot_tn(xc, dpr)
        dwi_ref[...] += _dot_tn(xc, dpi)
        dxc = dix * i + _dot_nt(dpr, wr_ref[...]) + _dot_nt(dpi, wi_ref[...])
        dlcb_ref[...] += _colsum(dxc)

        dxpad[0:t, :] = dxc
        xr = xr_ref[...]
        dxr = jnp.zeros((t, D_GROUP), F32)
        for k in range(LRU_CONV_WIDTH):
            o = (LRU_CONV_WIDTH - 1) - k
            sh = dxpad[o:o + t, :]
            dxr = dxr + lcw_ref[k:k + 1, :] * sh
            dlcw_ref[k:k + 1, :] += _colsum(xr * sh)
        dxpad[t:t + SUBLANE, :] = dxpad[0:SUBLANE, :]
        dgr_ref[:, D_GROUP:] = dxr

    def rev(col):
        return lambda i: (nt - 1 - i, col)

    ins = [h_in, h_in, xc_all, h_all, h_all, dmix, lcw, wr, br, wi, bi, sp, dh_all]
    in_specs = [pl.BlockSpec((t, D_GROUP), rev(COL_LRU_G)), pl.BlockSpec((t, D_GROUP), rev(COL_LRU_X)),
                pl.BlockSpec((t, D_GROUP), rev(0)), pl.BlockSpec((t, D_GROUP), rev(0)),
                pl.BlockSpec((SUBLANE, D_GROUP), lambda i: (jnp.maximum((nt - 1 - i) * tb - 1, 0), 0)),
                pl.BlockSpec((t, D_GROUP), rev(MIX_LRU))] + [_full_spec(a) for a in ins[6:12]] + [_ANY]
    vec = jax.ShapeDtypeStruct((1, D_GROUP), F32)
    mat = jax.ShapeDtypeStruct((D_GROUP, D_GROUP), F32)
    outs = [jax.ShapeDtypeStruct((s, N_IN_COLS), F32), mat, mat, jax.ShapeDtypeStruct((SUBLANE, D_GROUP), F32),
            vec, vec, vec, vec]
    out_specs = [pl.BlockSpec((t, 2 * D_GROUP), rev(COL_LRU_G // 2))] + [_full_spec(o) for o in outs[1:]]
    return pl.pallas_call(
        body, grid=(nt,), in_specs=in_specs, out_specs=out_specs, out_shape=outs, input_output_aliases={12: 0},
        scratch_shapes=[pltpu.VMEM((pad + t + pad, D_GROUP), F32)] * 4 +
                       [pltpu.VMEM((SUBLANE + t, D_GROUP), F32), pltpu.VMEM((t + SUBLANE, D_GROUP), F32),
                        pltpu.VMEM((SUBLANE, D_GROUP), F32)],
        compiler_params=_cparams(1), name=name)(*ins)


def _blockdiag(w):
    h, d, _ = w.shape
    return jnp.tile(w.reshape(h * d, d), (1, h)) * _block_mask(h, d, d)


ATTN_TILE = 512
ATTN_SCALE = ATTN_HEAD_DIM ** -0.5


def _attn_big(kv):
    m = kv.shape[0]
    kbig = jnp.tile(kv[:, :D_GROUP].T, (1, ATTN_HEADS)) * _block_mask(ATTN_HEADS, ATTN_HEAD_DIM, m)
    vbig = jnp.tile(kv[:, D_GROUP:], (ATTN_HEADS, 1)) * _block_mask(ATTN_HEADS, m, ATTN_HEAD_DIM)
    return kbig, vbig


def _attn_probs(q, kbig_ref, m):
    sc = _dot(q, kbig_ref[...]) * ATTN_SCALE
    ps = []
    for h in range(ATTN_HEADS):
        sh = sc[:, h * m:(h + 1) * m]
        e = jnp.exp(sh - jnp.max(sh, axis=1, keepdims=True))
        ps.append(e / jnp.sum(e, axis=1, keepdims=True))
    return ps


def _attn_fwd(h_in, kbig, vbig, mix, *, name):
    s = h_in.shape[0]
    t = _seq_tile(s, ATTN_TILE)
    m = kbig.shape[1] // ATTN_HEADS

    def body(q_ref, kbig_ref, vbig_ref, _mix_in, o_ref):
        ps = _attn_probs(q_ref[...], kbig_ref, m)
        o_ref[...] = _dot(jnp.concatenate(ps, axis=1), vbig_ref[...]).astype(BF16)

    return pl.pallas_call(
        body, grid=(s // t,),
        in_specs=[pl.BlockSpec((t, D_GROUP), lambda i: (i, COL_Q)), _full_spec(kbig), _full_spec(vbig), _ANY],
        out_specs=pl.BlockSpec((t, D_GROUP), lambda i: (i, MIX_ATTN)),
        out_shape=jax.ShapeDtypeStruct((s, D_MODEL), BF16), input_output_aliases={3: 0},
        compiler_params=_cparams(1), name=name)(h_in, kbig, vbig, mix)


def _attn_bwd(h_in, dmix, kbig, vbig, du_s5, dh_all, *, name):
    s = h_in.shape[0]
    t = _seq_tile(s, ATTN_TILE)
    m = kbig.shape[1] // ATTN_HEADS

    def body(q_ref, do_ref, kbig_ref, vbig_ref, dus5_ref, _dh_in, dpair_ref, dk_ref, dv_ref):
        @pl.when(pl.program_id(0) == 0)
        def _():
            dk_ref[...] = jnp.zeros_like(dk_ref)
            dv_ref[...] = jnp.zeros_like(dv_ref)

        q = q_ref[...]
        dout = do_ref[...]
        ps = _attn_probs(q, kbig_ref, m)
        dp = _dot_nt(dout, vbig_ref[...])
        dss = []
        for h in range(ATTN_HEADS):
            dph = dp[:, h * m:(h + 1) * m]
            dss.append(ps[h] * (dph - jnp.sum(dph * ps[h], axis=1, keepdims=True)))
        ds = (jnp.concatenate(dss, axis=1) * ATTN_SCALE).astype(BF16)
        dv_ref[...] += _dot_tn(jnp.concatenate(ps, axis=1), dout)
        dpair_ref[:, :D_GROUP] = dus5_ref[...]
        dpair_ref[:, D_GROUP:] = _dot_nt(ds, kbig_ref[...])
        dk_ref[...] += _dot_tn(q, ds)

    assert (COL_S5, COL_Q) == (4, 5)
    outs = [jax.ShapeDtypeStruct((s, N_IN_COLS), F32), jax.ShapeDtypeStruct(kbig.shape, F32),
            jax.ShapeDtypeStruct(vbig.shape, F32)]
    return pl.pallas_call(
        body, grid=(s // t,),
        in_specs=[pl.BlockSpec((t, D_GROUP), lambda i: (i, COL_Q)), pl.BlockSpec((t, D_GROUP), lambda i: (i, MIX_ATTN)),
                  _full_spec(kbig), _full_spec(vbig), pl.BlockSpec((t, D_GROUP), lambda i: (i, 0)), _ANY],
        out_specs=[pl.BlockSpec((t, 2 * D_GROUP), lambda i: (i, COL_S5 // 2)), _full_spec(outs[1]), _full_spec(outs[2])],
        out_shape=outs, input_output_aliases={5: 0},
        compiler_params=_cparams(1), name=name)(h_in, dmix, kbig, vbig, du_s5, dh_all)


FFN_TILE = 128
FFN_COL_CHUNK = 256
FFN_ROW_CHUNK = 64


def _ffn_conv(pad_ref, w_ref, b_ref, r0, ch, c0):
    cc = FFN_COL_CHUNK
    acc = jnp.broadcast_to(b_ref[:, c0:c0 + cc], (ch, cc))
    for k in range(FFN_CONV_WIDTH):
        o = SUBLANE - (FFN_CONV_WIDTH - 1) + k + r0
        acc = acc + w_ref[k:k + 1, c0:c0 + cc] * pad_ref[o:o + ch, c0:c0 + cc]
    return acc


def _ffn_gate_fwd(u, fcw, fcb, *, name, rider=None):
    s = u.shape[0]
    t = _seq_tile(s, FFN_TILE)
    ch = min(FFN_ROW_CHUNK, t)
    cc = FFN_COL_CHUNK

    def body(u_ref, w_ref, b_ref, o_ref, uc_ref, upad):
        @pl.when(pl.program_id(0) == 0)
        def _():
            upad[0:SUBLANE, :] = jnp.zeros((SUBLANE, 2 * D_FF), F32)

        upad[SUBLANE:SUBLANE + t, :] = u_ref[...].astype(F32)
        for c0 in range(0, D_FF, cc):
            for r0 in range(0, t, ch):
                val = _ffn_conv(upad, w_ref, b_ref, r0, ch, c0)
                gt = _ffn_conv(upad, w_ref, b_ref, r0, ch, c0 + D_FF)
                o_ref[r0:r0 + ch, c0:c0 + cc] = (val * _gelu(gt)).astype(BF16)
                uc_ref[r0:r0 + ch, c0:c0 + cc] = val.astype(BF16)
                uc_ref[r0:r0 + ch, c0 + D_FF:c0 + D_FF + cc] = gt.astype(BF16)
        upad[0:SUBLANE, :] = upad[t:t + SUBLANE, :]

    return _call(
        body, grid=(s // t,), ins=[u, fcw, fcb],
        in_specs=[pl.BlockSpec((t, 2 * D_FF), lambda i: (i, 0)), _full_spec(fcw), _full_spec(fcb)],
        out_specs=[pl.BlockSpec((t, D_FF), lambda i: (i, 0)), pl.BlockSpec((t, 2 * D_FF), lambda i: (i, 0))],
        outs=[jax.ShapeDtypeStruct((s, D_FF), BF16), jax.ShapeDtypeStruct((s, 2 * D_FF), BF16)],
        scratch=[pltpu.VMEM((SUBLANE + t, 2 * D_FF), F32)], name=name, rider=rider)


def _ffn_gate_bwd(u, uc, dh, fcw, *, name, rider=None):
    s = u.shape[0]
    t = _seq_tile(s, FFN_TILE)
    nt = s // t
    ch = min(FFN_ROW_CHUNK, t)
    cc = FFN_COL_CHUNK

    def body(u_ref, uc_ref, dh_ref, w_ref, du_ref, dw_ref, db_ref, dpad):
        @pl.when(pl.program_id(0) == 0)
        def _():
            dpad[t:t + SUBLANE, :] = jnp.zeros((SUBLANE, 2 * D_FF), F32)
            dw_ref[...] = jnp.zeros_like(dw_ref)
            db_ref[...] = jnp.zeros_like(db_ref)

        for c0 in range(0, D_FF, cc):
            for r0 in range(0, t, ch):
                val = uc_ref[r0:r0 + ch, c0:c0 + cc].astype(F32)
                gt = uc_ref[r0:r0 + ch, c0 + D_FF:c0 + D_FF + cc].astype(F32)
                gl, dgl = _gelu_and_grad(gt)
                d = dh_ref[r0:r0 + ch, c0:c0 + cc].astype(F32)
                dpad[r0:r0 + ch, c0:c0 + cc] = d * gl
                dpad[r0:r0 + ch, c0 + D_FF:c0 + D_FF + cc] = d * val * dgl
        for c0 in range(0, 2 * D_FF, cc):
            dbs = jnp.zeros((1, cc), F32)
            dws = [jnp.zeros((1, cc), F32) for _ in range(FFN_CONV_WIDTH)]
            for r0 in range(0, t, ch):
                x = u_ref[r0:r0 + ch, c0:c0 + cc].astype(F32)
                acc = jnp.zeros((ch, cc), F32)
                for k in range(FFN_CONV_WIDTH):
                    o = (FFN_CONV_WIDTH - 1) - k + r0
                    sh = dpad[o:o + ch, c0:c0 + cc]
                    acc = acc + w_ref[k:k + 1, c0:c0 + cc] * sh
                    dws[k] = dws[k] + _colsum(x * sh)
                    if k == FFN_CONV_WIDTH - 1:
                        dbs = dbs + _colsum(sh)
                du_ref[r0:r0 + ch, c0:c0 + cc] = acc.astype(BF16)
            db_ref[:, c0:c0 + cc] += dbs
            for k in range(FFN_CONV_WIDTH):
                dw_ref[k:k + 1, c0:c0 + cc] += dws[k]
        dpad[t:t + SUBLANE, :] = dpad[0:SUBLANE, :]

    outs = [jax.ShapeDtypeStruct((s, 2 * D_FF), BF16), jax.ShapeDtypeStruct((SUBLANE, 2 * D_FF), F32),
            jax.ShapeDtypeStruct((1, 2 * D_FF), F32)]
    return _call(
        body, grid=(nt,), ins=[u, uc, dh, fcw],
        in_specs=[pl.BlockSpec((t, 2 * D_FF), lambda i: (nt - 1 - i, 0)),
                  pl.BlockSpec((t, 2 * D_FF), lambda i: (nt - 1 - i, 0)),
                  pl.BlockSpec((t, D_FF), lambda i: (nt - 1 - i, 0)), _full_spec(fcw)],
        out_specs=[pl.BlockSpec((t, 2 * D_FF), lambda i: (nt - 1 - i, 0)), _full_spec(outs[1]), _full_spec(outs[2])],
        outs=outs, scratch=[pltpu.VMEM((t + SUBLANE, 2 * D_FF), F32)], name=name, rider=rider)


def _adamw_body(g_ref, w_ref, m_ref, v_ref, go_ref, d_ref, mo_ref, vo_ref):
    inv_b1 = 1.0 - ADAM_B1 ** ADAM_STEP
    inv_b2 = 1.0 - ADAM_B2 ** ADAM_STEP
    g = g_ref[0].astype(F32)
    for dev in range(1, N_DEV):
        g = g + g_ref[dev].astype(F32)
    go_ref[...] = g
    mn = ADAM_B1 * m_ref[...] + (1.0 - ADAM_B1) * g
    vn = ADAM_B2 * v_ref[...] + (1.0 - ADAM_B2) * (g * g)
    mo_ref[...] = mn
    vo_ref[...] = vn
    d_ref[...] = -ADAM_LR * ((mn / inv_b1) / (jnp.sqrt(vn / inv_b2) + ADAM_EPS) + ADAM_WD * w_ref[...])


def _adamw(gstack, w, m, v, *, name):
    _, r, c = gstack.shape
    tr = _pick_rows(r, PACK_ROW_BLOCK)

    def body(*refs):
        _adamw_body(*refs)

    blk = pl.BlockSpec((tr, c), lambda i: (i, 0))
    sh = jax.ShapeDtypeStruct((r, c), F32)
    return pl.pallas_call(
        body, grid=(r // tr,),
        in_specs=[pl.BlockSpec((N_DEV, tr, c), lambda i: (0, i, 0)), blk, blk, blk],
        out_specs=[blk] * 4, out_shape=[sh] * 4,
        compiler_params=_cparams(1), name=name)(gstack, w, m, v)


def _adamw_layer(gstack, w, m, v, layer, into, *, name):
    n_layers, r, c = w.shape
    tr = _pick_rows(r, PACK_ROW_BLOCK)

    def body(g_ref, w_ref, m_ref, v_ref, *rest):
        _adamw_body(g_ref, w_ref, m_ref, v_ref, *rest[-4:])

    blk = pl.BlockSpec((None, tr, c), lambda i: (layer, i, 0))
    sh = jax.ShapeDtypeStruct((n_layers, r, c), F32)
    into = list(into or [])
    return pl.pallas_call(
        body, grid=(r // tr,),
        in_specs=[pl.BlockSpec((N_DEV, tr, c), lambda i: (0, i, 0)), blk, blk, blk] + [_ANY] * len(into),
        out_specs=[blk] * 4, out_shape=[sh] * 4, input_output_aliases={4 + k: k for k in range(len(into))},
        compiler_params=_cparams(1), name=name)(gstack, w, m, v, *into)


def _exchange(rider, *, name):
    n = rider.n

    def body(*refs):
        x_refs, out_refs, sems = refs[:n], refs[n:2 * n], refs[2 * n:]
        rider.start(x_refs, out_refs, sems)
        rider.wait(x_refs, out_refs, sems)

    return pl.pallas_call(
        body, in_specs=[_ANY] * n, out_specs=[_ANY] * n, out_shape=rider.out_shapes(),
        scratch_shapes=rider.scratch(), name=name)(*rider.srcs)


def _pack_rows(n):
    rows = -(-n // PACK_COLS)
    return -(-rows // SUBLANE) * SUBLANE


def _pack(arrs, dtype):
    flat = jnp.concatenate([a.reshape(-1).astype(dtype) for a in arrs])
    rows = _pack_rows(flat.shape[0])
    flat = jnp.pad(flat, (0, rows * PACK_COLS - flat.shape[0]))
    return flat.reshape(rows, PACK_COLS)


def _pack_lead(arrs, dtype):
    flat = jnp.concatenate([a.reshape(N_DEV, -1).astype(dtype) for a in arrs], axis=1)
    rows = _pack_rows(flat.shape[1])
    flat = jnp.pad(flat, ((0, 0), (0, rows * PACK_COLS - flat.shape[1])))
    return flat.reshape(N_DEV, rows, PACK_COLS)


def _pack_layers(arrs, dtype):
    n_layers = arrs[0].shape[0]
    flat = jnp.concatenate([a.reshape(n_layers, -1).astype(dtype) for a in arrs], axis=1)
    rows = _pack_rows(flat.shape[1])
    flat = jnp.pad(flat, ((0, 0), (0, rows * PACK_COLS - flat.shape[1])))
    return flat.reshape(n_layers, rows, PACK_COLS)


def _unpack_layers(packed, shapes):
    flat = packed.reshape(packed.shape[0], -1)
    out, pos = [], 0
    for sh in shapes:
        n = math.prod(sh[1:])
        out.append(flat[:, pos:pos + n].reshape(sh))
        pos += n
    return out


def _unpack(packed, shapes, lead=False):
    flat = packed.reshape(N_DEV, -1) if lead else packed.reshape(-1)
    out, pos = [], 0
    for sh in shapes:
        n = math.prod(sh)
        out.append(flat[:, pos:pos + n].reshape((N_DEV,) + tuple(sh)) if lead else flat[pos:pos + n].reshape(sh))
        pos += n
    return out


def _join_shards(stacked, axis):
    return jnp.concatenate([stacked[d] for d in range(N_DEV)], axis=axis)


def _split_shards(full, axis):
    return jnp.stack(jnp.split(full, N_DEV, axis=axis), axis=0)


def _perm_in_cols(a, inverse=False):
    blocks = jnp.split(a, 6, axis=-1)
    if inverse:
        order = [IN_PERM.index(j) for j in range(6)]
    else:
        order = list(IN_PERM)
    return jnp.concatenate([blocks[j] for j in order], axis=-1)


def _row(v):
    return v.reshape(1, -1)


def _pad_rows(w, rows):
    return jnp.pad(w, ((0, rows - w.shape[0]), (0, 0)))


def _gn_avg_matrix():
    return _block_mask(GN_GROUPS, D_GROUP // GN_GROUPS, D_GROUP // GN_GROUPS) / (D_GROUP // GN_GROUPS)


def _layer_params(p, l):
    q = {}
    s5_mats, q["s5_vjp"] = jax.vjp(_s5_chunk_map, p["s5_lam_re"][l], p["s5_lam_im"][l], p["s5_log_dt"][l],
                                   p["s5_b_re"][l], p["s5_b_im"][l], p["s5_c_re"][l], p["s5_c_im"][l], p["s5_d"][l])
    q["s5_mats"] = [m.astype(BF16) for m in s5_mats[:5]]
    q["s5_a16"] = s5_mats[5]
    (q["wr"], q["wi"]), q["lru_w_vjp"] = jax.vjp(lambda r, i: (_blockdiag(r), _blockdiag(i)), p["lru_w_r"][l], p["lru_w_i"][l])
    q["wr"], q["wi"] = q["wr"].astype(BF16), q["wi"].astype(BF16)
    q["sp"], q["sp_vjp"] = jax.vjp(lambda lam: _row(jax.nn.softplus(-lam)), p["lru_lam"][l])
    return q


WEIGHT_RIDES = {(0, "ln_in_fwd"): [("w_in", 0)],
                (0, "inproj"): [("attn_w_kv", 0), ("w_out", 0), ("small_pack", 0)],
                (0, "cv_fwd"): [("ffn_w_up#a", 0)],
                (0, "outproj"): [("ffn_w_up#b", 0)],
                (0, "ffn_up"): [("ffn_w_down", 0), ("w_in", 1), ("attn_w_kv", 1), ("w_out", 1)],
                (0, "ffn_gate_fwd"): [("ffn_w_up", 1)],
                (0, "ffn_down"): [("ffn_w_down", 1)]}
GRAD_RIDES = {(1, "ffn_gate_bwd"): [("ffn_w_down", 1)],
              (0, "dw_down"): [("w_out", 1), ("attn_w_kv", 1), ("w_in", 1)],
              (0, "dhff"): [("rep", 1), ("ssh", 1)],
              (0, "ffn_gate_bwd"): [("ffn_w_up", 1)],
              (0, "dw_up"): [("ffn_w_down", 0)],
              (0, "dx1"): [("ffn_w_up", 0)],
              (0, "cv_bwd"): [("w_out", 0)],
              (0, "dw_in"): [("attn_w_kv", 0), ("ssh", 0)],
              (0, "dxs"): [("w_in", 0)],
              (0, "ln_in_bwd"): [("rep", 0)]}


def _assemble_weight(n, gathered):
    if SHARDED[n] == 2:
        full = jnp.transpose(gathered, (1, 0, 2)).reshape(gathered.shape[1], -1)
        return _perm_in_cols(full) if n == "w_in" else full
    return gathered.reshape(-1, gathered.shape[-1])


def _grad_source(n, g):
    g = g.astype(BF16)
    if SHARDED[n] == 2:
        if n == "w_in":
            g = _perm_in_cols(g, inverse=True)
        k, nn = g.shape
        return jnp.transpose(g.reshape(k, N_DEV, nn // N_DEV), (1, 0, 2)), "lead"
    return g, "rows"


def _hosted(fn, keys_rider, land, *args, **kw):
    keys, rider = keys_rider
    if rider is None:
        return fn(*args, **kw)
    out, routs = fn(*args, rider=rider, **kw)
    land(keys, routs)
    return out


def _local_step(x, mem, target, p, big_w, shards=None, unpack_small=None):
    dist = shards is not None
    small, saved = {}, []
    big_g, ready, recv = {}, {}, {}
    mavg = _gn_avg_matrix()

    def weight_rider(l, host):
        keys = WEIGHT_RIDES.get((l, host), []) if dist else []
        return keys, (_Rider([shards[n][ll] for n, ll in keys], ["all"] * len(keys)) if keys else None)

    halves = {}

    def land_weights(keys, routs):
        for (n, ll), r in zip(keys, routs):
            if n == "small_pack":
                p.update(unpack_small(r))
            elif "#" in n:
                base = n.split("#")[0]
                halves[(n, ll)] = r
                if (base + "#a", ll) in halves and (base + "#b", ll) in halves:
                    whole = jnp.concatenate([halves[(base + "#a", ll)], halves[(base + "#b", ll)]], axis=1)
                    big_w[base][ll] = _assemble_weight(base, whole)
            else:
                big_w[n][ll] = _assemble_weight(n, r)

    def grad_rider(l, host):
        keys = [k for k in GRAD_RIDES.get((l, host), []) if k in ready] if dist else []
        return keys, (_Rider([ready[k][0] for k in keys], [ready[k][1] for k in keys]) if keys else None)

    def land_grads(keys, routs):
        for k, r in zip(keys, routs):
            recv[k] = r
            del ready[k]

    def big_grad(n, l, g):
        if dist:
            ready[(n, l)] = _grad_source(n, g)
        else:
            big_g[(n, l)] = g

    xs = _hosted(_ln_fwd, weight_rider(0, "ln_in_fwd"), land_weights, x, _row(p["ln_in_g"]), _row(p["ln_in_b"]),
                 name="ln_in_fwd")
    for l in range(DEPTH):
        q = _layer_params(p, l)
        n = f"l{l}_"
        hin = _hosted(_mm, weight_rider(l, "inproj"), land_weights, xs, big_w["w_in"][l], bias=_row(p["b_in"][l]),
                      name=n + "inproj")
        nb = hin.shape[0] // S5_CHUNK
        s5_pows = _s5_a16_powers(q["s5_a16"], nb.bit_length() - 1)
        s5_u2 = _s5_to_chunks(hin[:, COL_S5 * D_GROUP:(COL_S5 + 1) * D_GROUP], BF16)
        s5_y2, s5_x = _s5_core_fwd(s5_u2, *q["s5_mats"], s5_pows, name=n + "s5_core_fwd")
        s5_y1 = _s5_from_chunks(s5_y2)
        (mix,), _ = _s5_glu_fwd(s5_y1, p["s5_w_glu"][l], _row(p["s5_b_glu"][l]), name=n + "s5_glu_fwd")
        cvw = _pad_rows(p["cv_w"][l], CV_PAD)
        keys, rd = weight_rider(l, "cv_fwd")
        (mix, cv_c), routs = _cv_fwd(hin, cvw, _row(p["cv_b"][l]), _row(p["cv_gn_g"][l]), _row(p["cv_gn_b"][l]), mavg,
                                     p["cv_w_pw"][l], _row(p["cv_b_pw"][l]), mix, name=n + "cv_fwd", rider=rd)
        land_weights(keys, routs)
        lcw = _pad_rows(p["lru_conv_w"][l], SUBLANE)
        mix, lru_xc, lru_h = _lru_fwd(hin, lcw, _row(p["lru_conv_b"][l]), q["wr"], _row(p["lru_b_r"][l]), q["wi"],
                                      _row(p["lru_b_i"][l]), q["sp"], mix, name=n + "lru_fwd")
        kv = _mm(mem, big_w["attn_w_kv"][l], name=n + "kv")
        (kbig, vbig), kv_vjp = jax.vjp(_attn_big, kv)
        kbig, vbig = kbig.astype(BF16), vbig.astype(BF16)
        mix = _attn_fwd(hin, kbig, vbig, mix, name=n + "attn_fwd")
        r1, x1 = _hosted(_mm, weight_rider(l, "outproj"), land_weights, mix, big_w["w_out"][l], bias=_row(p["b_out"][l]),
                         res=xs, res_scale=ALPHA, ln=(_row(p["ln1_g"][l]), _row(p["ln1_b"][l])), name=n + "outproj")
        u = _hosted(_mm, weight_rider(l, "ffn_up"), land_weights, x1, big_w["ffn_w_up"][l], out_dtype=BF16,
                    name=n + "ffn_up")
        fcw = _pad_rows(p["ffn_conv_w"][l], SUBLANE)
        fcb = _row(p["ffn_conv_b"][l])
        keys, rd = weight_rider(l, "ffn_gate_fwd")
        (hff, uc), routs = _ffn_gate_fwd(u, fcw, fcb, name=n + "ffn_gate_fwd", rider=rd)
        land_weights(keys, routs)
        if l < DEPTH - 1:
            r2, x2 = _hosted(_mm, weight_rider(l, "ffn_down"), land_weights, hff, big_w["ffn_w_down"][l], res=x1,
                             res_scale=ALPHA, ln=(_row(p["ln2_g"][l]), _row(p["ln2_b"][l])), name=n + "ffn_down")
        else:
            r2, x2 = _mm(hff, big_w["ffn_w_down"][l], res=x1, res_scale=ALPHA, name=n + "ffn_down"), None
        saved.append(dict(q=q, xs=xs, hin=hin, s5_y1=s5_y1, s5_u2=s5_u2, s5_x=s5_x, s5_pows=s5_pows, cvw=cvw, cv_c=cv_c, lcw=lcw, lru_xc=lru_xc,
                          lru_h=lru_h, kbig=kbig, vbig=vbig, kv_vjp=kv_vjp, mix=mix, r1=r1, x1=x1, u=u, uc=uc, fcw=fcw,
                          hff=hff, r2=r2))
        xs = x2

    top = DEPTH - 1
    dr_top, dg_top, db_top, loss_blk = _loss_ln_bwd(saved[top]["r2"], _row(p["ln2_g"][top]), _row(p["ln2_b"][top]), target,
                                                     name="loss_ln_bwd")
    loss = loss_blk[0, 0]
    dx = None

    for l in reversed(range(DEPTH)):
        sv = saved[l]
        q = sv["q"]
        n = f"l{l}_"
        g = {}
        if l == top:
            dr2, g["ln2_g"], g["ln2_b"] = dr_top, dg_top, db_top
        else:
            dr2, g["ln2_g"], g["ln2_b"], _ = _ln_bwd(sv["r2"], dx, _row(p["ln2_g"][l]), name=n + "ln2_bwd")
        big_grad("ffn_w_down", l, _hosted(_mm_tn, grad_rider(l, "dw_down"), land_grads, sv["hff"], dr2, name=n + "dw_down"))
        dhff = _hosted(_mm, grad_rider(l, "dhff"), land_grads, dr2, big_w["ffn_w_down"][l], trans_b=True,
                       out_dtype=BF16, name=n + "dhff")
        keys, rd = grad_rider(l, "ffn_gate_bwd")
        (du, dfw, g["ffn_conv_b"]), routs = _ffn_gate_bwd(sv["u"], sv["uc"], dhff, sv["fcw"], name=n + "ffn_gate_bwd",
                                                          rider=rd)
        land_grads(keys, routs)
        g["ffn_conv_w"] = dfw[:FFN_CONV_WIDTH]
        big_grad("ffn_w_up", l, _hosted(_mm_tn, grad_rider(l, "dw_up"), land_grads, sv["x1"], du, name=n + "dw_up"))
        dx1 = _hosted(_mm, grad_rider(l, "dx1"), land_grads, du, big_w["ffn_w_up"][l], trans_b=True, res=dr2,
                      res_scale=ALPHA, name=n + "dx1")
        dr1, g["ln1_g"], g["ln1_b"], g["b_out"] = _ln_bwd(sv["r1"], dx1, _row(p["ln1_g"][l]), name=n + "ln1_bwd")
        big_grad("w_out", l, _mm_tn(sv["mix"], dr1, name=n + "dw_out"))
        dmix = _mm(dr1, big_w["w_out"][l], trans_b=True, name=n + "dmix")

        hin = sv["hin"]
        keys, rd = grad_rider(l, "cv_bwd")
        (dh, g["cv_w_pw"], dcw, g["cv_b_pw"], g["cv_gn_g"], g["cv_gn_b"], g["cv_b"]), routs = _cv_bwd(
            hin, sv["cv_c"], dmix, sv["cvw"], _row(p["cv_gn_g"][l]), _row(p["cv_gn_b"][l]), mavg, p["cv_w_pw"][l],
            name=n + "cv_bwd", rider=rd)
        land_grads(keys, routs)
        g["cv_w"] = dcw[:CONV_WIDTH]
        dh, dwr, dwi, dlcw, g["lru_b_r"], g["lru_b_i"], dsp, g["lru_conv_b"] = _lru_bwd(
            hin, sv["lru_xc"], sv["lru_h"], dmix, sv["lcw"], q["wr"], _row(p["lru_b_r"][l]), q["wi"],
            _row(p["lru_b_i"][l]), q["sp"], dh, name=n + "lru_bwd")
        g["lru_conv_w"] = dlcw[:LRU_CONV_WIDTH]
        g["lru_w_r"], g["lru_w_i"] = q["lru_w_vjp"]((dwr, dwi))
        (g["lru_lam"],) = q["sp_vjp"](dsp)
        dy1, g["s5_w_glu"], g["s5_b_glu"] = _s5_glu_bwd(sv["s5_y1"], dmix, p["s5_w_glu"][l], _row(p["s5_b_glu"][l]),
                                                        name=n + "s5_glu_bwd")
        s5_du2, *s5_dmats = _s5_core_bwd(sv["s5_u2"], _s5_to_chunks(dy1, BF16), sv["s5_x"], *q["s5_mats"], sv["s5_pows"],
                                         name=n + "s5_core_bwd")
        (g["s5_lam_re"], g["s5_lam_im"], g["s5_log_dt"], g["s5_b_re"], g["s5_b_im"], g["s5_c_re"], g["s5_c_im"],
         g["s5_d"]) = q["s5_vjp"](tuple(s5_dmats))
        dh, dkbig, dvbig = _attn_bwd(hin, dmix, sv["kbig"], sv["vbig"], _s5_from_chunks(s5_du2), dh, name=n + "attn_bwd")
        (dkv,) = sv["kv_vjp"]((dkbig, dvbig))
        big_grad("attn_w_kv", l, _mm_tn(mem, dkv, name=n + "dw_kv"))

        if dist:
            ready[("ssh", l)] = (_pack_lead([_split_shards(g[k], SHARDED[k] - 1) for k in SMALL_SHARDED], F32), "lead")
        gw_in, g["b_in"] = _hosted(_mm_tn, grad_rider(l, "dw_in"), land_grads, sv["xs"], dh, colsum=True, name=n + "dw_in")
        big_grad("w_in", l, gw_in)
        if dist:
            g["b_in"] = _perm_in_cols(g["b_in"], inverse=True)
            ready[("rep", l)] = (_pack([g[k] for k in REP_LAYERED], F32), "all")
        else:
            for k, v in g.items():
                small.setdefault(k, [None] * DEPTH)[l] = v.reshape(p[k].shape[1:])
        dx = _hosted(_mm, grad_rider(l, "dxs"), land_grads, dh, big_w["w_in"][l], trans_b=True, res=dr1,
                     res_scale=ALPHA, name=n + "dxs")

    keys, rd = grad_rider(0, "ln_in_bwd")
    if rd is None:
        grad_x, dgi, dbi, _ = _ln_bwd(x, dx, _row(p["ln_in_g"]), name="ln_in_bwd")
    else:
        (grad_x, dgi, dbi, _), routs = _ln_bwd(x, dx, _row(p["ln_in_g"]), name="ln_in_bwd", rider=rd)
        land_grads(keys, routs)
    out = {k: jnp.stack(v, axis=0) for k, v in small.items()}
    out["ln_in_g"], out["ln_in_b"] = dgi.reshape(-1), dbi.reshape(-1)
    return loss, grad_x, out, ((recv, ready) if dist else big_g)


def kernel(x, mem, ln_in_g, ln_in_b, w_in, b_in, s5_lam_re, s5_lam_im, s5_log_dt, s5_b_re, s5_b_im, s5_c_re, s5_c_im, s5_d, s5_w_glu, s5_b_glu, cv_w, cv_b, cv_gn_g, cv_gn_b, cv_w_pw, cv_b_pw, lru_conv_w, lru_conv_b, lru_w_r, lru_b_r, lru_w_i, lru_b_i, lru_lam, attn_w_kv, w_out, b_out, ln1_g, ln1_b, ffn_w_up, ffn_conv_w, ffn_conv_b, ffn_w_down, ln2_g, ln2_b, loss_target, m_ln_in_g, m_ln_in_b, m_w_in, m_b_in, m_s5_lam_re, m_s5_lam_im, m_s5_log_dt, m_s5_b_re, m_s5_b_im, m_s5_c_re, m_s5_c_im, m_s5_d, m_s5_w_glu, m_s5_b_glu, m_cv_w, m_cv_b, m_cv_gn_g, m_cv_gn_b, m_cv_w_pw, m_cv_b_pw, m_lru_conv_w, m_lru_conv_b, m_lru_w_r, m_lru_b_r, m_lru_w_i, m_lru_b_i, m_lru_lam, m_attn_w_kv, m_w_out, m_b_out, m_ln1_g, m_ln1_b, m_ffn_w_up, m_ffn_conv_w, m_ffn_conv_b, m_ffn_w_down, m_ln2_g, m_ln2_b, v_ln_in_g, v_ln_in_b, v_w_in, v_b_in, v_s5_lam_re, v_s5_lam_im, v_s5_log_dt, v_s5_b_re, v_s5_b_im, v_s5_c_re, v_s5_c_im, v_s5_d, v_s5_w_glu, v_s5_b_glu, v_cv_w, v_cv_b, v_cv_gn_g, v_cv_gn_b, v_cv_w_pw, v_cv_b_pw, v_lru_conv_w, v_lru_conv_b, v_lru_w_r, v_lru_b_r, v_lru_w_i, v_lru_b_i, v_lru_lam, v_attn_w_kv, v_w_out, v_b_out, v_ln1_g, v_ln1_b, v_ffn_w_up, v_ffn_conv_w, v_ffn_conv_b, v_ffn_w_down, v_ln2_g, v_ln2_b):
    args = locals()
    w = {n: args[n] for n in WEIGHTS}
    mom = {n: args["m_" + n] for n in WEIGHTS}
    var = {n: args["v_" + n] for n in WEIGHTS}

    shards = {n: w[n].astype(BF16) for n in BIG}
    half_rows = shards["ffn_w_up"].shape[1] // 2
    shards["ffn_w_up#a"] = [shards["ffn_w_up"][0, :half_rows]]
    shards["ffn_w_up#b"] = [shards["ffn_w_up"][0, half_rows:]]
    shards["small_pack"] = [_pack([w[n] for n in SMALL_SHARDED], F32)]
    small_shapes = [w[n].shape for n in SMALL_SHARDED]

    def unpack_small(gathered):
        out = {n: _join_shards(st, SHARDED[n]) for n, st in zip(SMALL_SHARDED, _unpack(gathered, small_shapes, lead=True))}
        for n in ("s5_w_glu", "cv_w_pw"):
            out[n] = out[n].astype(BF16)
        return out

    big_w = {n: [None] * DEPTH for n in BIG}
    p = {n: w[n] for n in REPLICATED}
    p["b_in"] = _perm_in_cols(p["b_in"])

    loss, grad_x, g_small, (recv, ready) = _local_step(x[0], mem[0], loss_target[0], p, big_w, shards, unpack_small)
    loss = lax.psum(loss, ("x", "y", "c"))

    left = list(ready)
    rider = _Rider([ready[k][0] for k in left] + [_pack([g_small["ln_in_g"], g_small["ln_in_b"]], F32)],
                   [ready[k][1] for k in left] + ["all"])
    got = _exchange(rider, name="exchange_grads")
    for k, r in zip(left, got):
        recv[k] = r

    res = [dict(), dict(), dict(), dict()]
    for n in BIG:
        outs = None
        for l in range(DEPTH):
            outs = _adamw_layer(recv[(n, l)], w[n], mom[n], var[n], l, outs, name=f"adamw_{n}_l{l}")
        for kind in range(4):
            res[kind][n] = outs[kind]
    for names, key, tag in ((SMALL_SHARDED, "ssh", "adamw_small_sharded"), (REP_LAYERED, "rep", "adamw_replicated")):
        gstack = jnp.concatenate([recv[(key, l)] for l in range(DEPTH)], axis=1)
        packs = [_pack_layers([t[n] for n in names], F32) for t in (w, mom, var)]
        rows = packs[0].shape[1]
        outs = _adamw(gstack, *[pk.reshape(DEPTH * rows, PACK_COLS) for pk in packs], name=tag)
        for kind in range(4):
            for n, a in zip(names, _unpack_layers(outs[kind].reshape(DEPTH, rows, PACK_COLS), [w[n].shape for n in names])):
                res[kind][n] = a
    ln_names = ("ln_in_g", "ln_in_b")
    outs = _adamw(got[len(left)], _pack([w[n] for n in ln_names], F32), _pack([mom[n] for n in ln_names], F32),
                  _pack([var[n] for n in ln_names], F32), name="adamw_ln_in")
    for kind in range(4):
        for n, a in zip(ln_names, _unpack(outs[kind], [w[n].shape for n in ln_names])):
            res[kind][n] = a
    return (loss, grad_x[None], *[res[0][n] for n in WEIGHTS], *[res[1][n] for n in WEIGHTS],
            *[res[2][n] for n in WEIGHTS], *[res[3][n] for n in WEIGHTS])
```

```python
import math

import jax
import jax.numpy as jnp
from jax import lax
from jax.experimental import pallas as pl
from jax.experimental.pallas import tpu as pltpu

F32 = jnp.float32
BF16 = jnp.bfloat16

D_MODEL = 1024
DEPTH = 2
D_GROUP = 256
N_IN_COLS = 6 * D_GROUP
S5_GROUPS = 16
S5_CH = 16
S5_STATE = 64
S5_LANES = S5_GROUPS * S5_STATE
CONV_WIDTH = 31
GN_GROUPS = 4
LRU_HEADS = 4
LRU_CONV_WIDTH = 4
LRU_C = 8.0
ATTN_HEADS = 4
ATTN_HEAD_DIM = 64
D_FF = 2816
FFN_CONV_WIDTH = 3
ALPHA = (2 * DEPTH) ** 0.25
LN_EPS = 1e-5
ADAM_LR, ADAM_B1, ADAM_B2, ADAM_EPS, ADAM_WD, ADAM_STEP = 0.001, 0.9, 0.999, 1e-08, 0.01, 10

N_DEV = 8
N_PEERS = N_DEV - 1
LANE = 128
SUBLANE = 8
VMEM_LIMIT = 56 * 1024 * 1024
PACK_COLS = 1024
PACK_ROW_BLOCK = 256

SHARDED = {
    "w_in": 2, "s5_w_glu": 1, "cv_w": 2, "cv_w_pw": 1, "lru_conv_w": 2, "attn_w_kv": 1,
    "w_out": 1, "ffn_w_up": 2, "ffn_conv_w": 2, "ffn_w_down": 1,
}
BIG = ("w_in", "attn_w_kv", "w_out", "ffn_w_up", "ffn_w_down")
SMALL_SHARDED = ("s5_w_glu", "cv_w", "cv_w_pw", "lru_conv_w", "ffn_conv_w")
MATMUL_WEIGHTS = ("w_in", "s5_w_glu", "cv_w_pw", "attn_w_kv", "w_out", "ffn_w_up", "ffn_w_down")
WEIGHTS = ['ln_in_g', 'ln_in_b', 'w_in', 'b_in', 's5_lam_re', 's5_lam_im', 's5_log_dt', 's5_b_re', 's5_b_im',
           's5_c_re', 's5_c_im', 's5_d', 's5_w_glu', 's5_b_glu', 'cv_w', 'cv_b', 'cv_gn_g', 'cv_gn_b', 'cv_w_pw',
           'cv_b_pw', 'lru_conv_w', 'lru_conv_b', 'lru_w_r', 'lru_b_r', 'lru_w_i', 'lru_b_i', 'lru_lam',
           'attn_w_kv', 'w_out', 'b_out', 'ln1_g', 'ln1_b', 'ffn_w_up', 'ffn_conv_w', 'ffn_conv_b', 'ffn_w_down',
           'ln2_g', 'ln2_b']
REPLICATED = [n for n in WEIGHTS if n not in SHARDED]
REP_LAYERED = [n for n in REPLICATED if n not in ("ln_in_g", "ln_in_b")]

COL_CV_V, COL_CV_G, COL_LRU_G, COL_LRU_X, COL_S5, COL_Q = range(6)
IN_PERM = (1, 2, 3, 4, 0, 5)
MIX_S5, MIX_CV, MIX_LRU, MIX_ATTN = range(4)


_ANY = pl.BlockSpec(memory_space=pl.ANY)
_MESH = pl.DeviceIdType.MESH


def _cparams(n_axes):
    return pltpu.CompilerParams(dimension_semantics=("arbitrary",) * n_axes, vmem_limit_bytes=VMEM_LIMIT)


def _pick(n, cap):
    if n <= cap:
        return n
    best = None
    for t in range(LANE, cap + 1, LANE):
        if n % t == 0:
            best = t
    assert best is not None, (n, cap)
    return best


def _pick_rows(n, cap):
    best = None
    for t in range(SUBLANE, min(n, cap) + 1, SUBLANE):
        if n % t == 0:
            best = t
    assert best is not None, (n, cap)
    return best


def _full_spec(arr):
    nd = arr.ndim
    return pl.BlockSpec(arr.shape, lambda *_: (0,) * nd)


def _dot(a, b):
    return lax.dot_general(a.astype(BF16), b.astype(BF16), (((1,), (0,)), ((), ())), preferred_element_type=F32)


def _dot_nt(a, b):
    return lax.dot_general(a.astype(BF16), b.astype(BF16), (((1,), (1,)), ((), ())), preferred_element_type=F32)


def _dot_tn(a, b):
    return lax.dot_general(a.astype(BF16), b.astype(BF16), (((0,), (0,)), ((), ())), preferred_element_type=F32)


def _dot_hi(a, b):
    return jnp.dot(a, b, precision=lax.Precision.HIGHEST, preferred_element_type=F32)


def _colsum(x):
    return jnp.sum(x, axis=0, keepdims=True)


def _sigmoid(x):
    return 1.0 / (1.0 + jnp.exp(-x))


_GELU_K = math.sqrt(2.0 / math.pi)
_GELU_C = 0.044715


def _gelu(x):
    t = jnp.tanh(_GELU_K * (x + _GELU_C * x * x * x))
    return 0.5 * x * (1.0 + t)


def _gelu_and_grad(x):
    x2 = x * x
    t = jnp.tanh(_GELU_K * (x + _GELU_C * x2 * x))
    g = 0.5 * x * (1.0 + t)
    dg = 0.5 * (1.0 + t) + 0.5 * x * (1.0 - t * t) * (_GELU_K * (1.0 + 3.0 * _GELU_C * x2))
    return g, dg


def _neg_expm1(x):
    series = x * (1.0 + x * (0.5 + x * (1.0 / 6.0 + x * (1.0 / 24.0 + x * (1.0 / 120.0)))))
    return -jnp.where(jnp.abs(x) < 0.1, series, jnp.exp(x) - 1.0)


def _seq_tile(s, want):
    t = min(s, want)
    assert s % t == 0
    return t


class _Rider:
    def __init__(self, srcs, kinds):
        self.srcs, self.kinds = list(srcs), list(kinds)
        self.n = len(self.srcs)

    def out_shapes(self):
        shapes = []
        for x, kind in zip(self.srcs, self.kinds):
            if kind == "lead":
                shp = x.shape
            elif kind == "rows":
                shp = (N_DEV, x.shape[0] // N_DEV) + x.shape[1:]
            else:
                shp = (N_DEV,) + x.shape
            shapes.append(jax.ShapeDtypeStruct(shp, x.dtype))
        return shapes

    def scratch(self):
        return [pltpu.SemaphoreType.DMA((self.n * N_PEERS,)), pltpu.SemaphoreType.DMA((self.n * N_PEERS,)),
                pltpu.SemaphoreType.DMA((self.n,))]

    def _copies(self, x_refs, out_refs, sems):
        send_sems, recv_sems, local_sems = sems
        mx, my, mc = lax.axis_index("x"), lax.axis_index("y"), lax.axis_index("c")
        my_id = 4 * mx + 2 * my + mc

        def piece(i, dev):
            if self.kinds[i] == "lead":
                return x_refs[i].at[dev]
            if self.kinds[i] == "rows":
                r = x_refs[i].shape[0] // N_DEV
                return x_refs[i].at[pl.ds(pl.multiple_of(dev * r, SUBLANE), r)]
            return x_refs[i]

        mine = [pltpu.make_async_copy(piece(i, my_id), out_refs[i].at[my_id], local_sems.at[i]) for i in range(self.n)]
        copies = []
        for k in range(1, N_DEV):
            px, py, pc = mx ^ ((k >> 2) & 1), my ^ ((k >> 1) & 1), mc ^ (k & 1)
            for i in range(self.n):
                copies.append(pltpu.make_async_remote_copy(
                    src_ref=piece(i, 4 * px + 2 * py + pc), dst_ref=out_refs[i].at[my_id],
                    send_sem=send_sems.at[i * N_PEERS + k - 1], recv_sem=recv_sems.at[i * N_PEERS + k - 1],
                    device_id=(px, py, pc), device_id_type=_MESH))
        return mine, copies

    def start(self, x_refs, out_refs, sems):
        mine, copies = self._copies(x_refs, out_refs, sems)
        for cp in mine + copies:
            cp.start()

    def wait(self, x_refs, out_refs, sems):
        mine, copies = self._copies(x_refs, out_refs, sems)
        for cp in copies:
            cp.wait_recv()
        for cp in copies:
            cp.wait_send()
        for cp in mine:
            cp.wait()


def _call(body, *, grid, ins, in_specs, outs, out_specs, scratch=(), aliases=None, name, rider=None):
    n_axes = len(grid)
    common = dict(grid=grid, input_output_aliases=aliases or {}, compiler_params=_cparams(n_axes), name=name)
    if rider is None:
        res = pl.pallas_call(body, in_specs=list(in_specs), out_specs=list(out_specs), out_shape=list(outs),
                             scratch_shapes=list(scratch), **common)(*ins)
        return list(res), []
    n_in, n_out, n_scr, nr = len(ins), len(outs), len(scratch), rider.n

    def wrapped(*refs):
        pos = [0]

        def take(k):
            part = refs[pos[0]:pos[0] + k]
            pos[0] += k
            return part

        a_in, r_in, a_out, r_out, a_scr, sems = take(n_in), take(nr), take(n_out), take(nr), take(n_scr), take(3)
        first = last = None
        for ax in range(n_axes):
            pid = pl.program_id(ax)
            f, l = pid == 0, pid == grid[ax] - 1
            first = f if first is None else jnp.logical_and(first, f)
            last = l if last is None else jnp.logical_and(last, l)

        @pl.when(first)
        def _():
            rider.start(r_in, r_out, sems)

        body(*a_in, *a_out, *a_scr)

        @pl.when(last)
        def _():
            rider.wait(r_in, r_out, sems)

    res = pl.pallas_call(
        wrapped, in_specs=list(in_specs) + [_ANY] * nr, out_specs=list(out_specs) + [_ANY] * nr,
        out_shape=list(outs) + rider.out_shapes(), scratch_shapes=list(scratch) + rider.scratch(), **common)(*ins, *rider.srcs)
    return list(res[:n_out]), list(res[n_out:])


def _block_mask(n_blocks, block_rows, block_cols):
    r = jnp.arange(n_blocks * block_rows) // block_rows
    c = jnp.arange(n_blocks * block_cols) // block_cols
    return (r[:, None] == c[None, :]).astype(F32)


def _mm(a, b, *, bias=None, res=None, res_scale=1.0, trans_b=False, out_dtype=F32, ln=None, name, rider=None):
    m, kdim = a.shape
    n = b.shape[0] if trans_b else b.shape[1]
    tm = _seq_tile(m, 1024)
    tn = _pick(n, 1408)
    tk = _pick(kdim, 1536)
    nk = kdim // tk
    has_bias, has_res, has_ln = bias is not None, res is not None, ln is not None
    assert not has_ln or tn == n

    def body(*refs):
        a_ref, b_ref = refs[0], refs[1]
        pos = 2
        bias_ref = res_ref = g_ref = beta_ref = x_ref = None
        if has_bias:
            bias_ref = refs[pos]
            pos += 1
        if has_res:
            res_ref = refs[pos]
            pos += 1
        if has_ln:
            g_ref, beta_ref = refs[pos], refs[pos + 1]
            pos += 2
        o_ref = refs[pos]
        pos += 1
        if has_ln:
            x_ref = refs[pos]
            pos += 1
        acc_ref = refs[pos]
        k = pl.program_id(2)

        @pl.when(k == 0)
        def _():
            acc_ref[...] = jnp.zeros_like(acc_ref)

        if trans_b:
            acc_ref[...] += _dot_nt(a_ref[...], b_ref[...])
        else:
            acc_ref[...] += _dot(a_ref[...], b_ref[...])

        @pl.when(k == nk - 1)
        def _():
            r = acc_ref[...]
            if has_bias:
                r = r + bias_ref[...]
            if has_res:
                r = r + res_scale * res_ref[...]
            o_ref[...] = r.astype(out_dtype)
            if has_ln:
                xc = r - jnp.mean(r, axis=1, keepdims=True)
                var = jnp.mean(xc * xc, axis=1, keepdims=True)
                x_ref[...] = xc * lax.rsqrt(var + LN_EPS) * g_ref[...] + beta_ref[...]

    ins = [a, b]
    in_specs = [pl.BlockSpec((tm, tk), lambda i, j, k: (i, k)),
                pl.BlockSpec((tn, tk), lambda i, j, k: (j, k)) if trans_b
                else pl.BlockSpec((tk, tn), lambda i, j, k: (k, j))]
    if has_bias:
        ins.append(bias)
        in_specs.append(pl.BlockSpec((1, tn), lambda i, j, k: (0, j)))
    if has_res:
        ins.append(res)
        in_specs.append(pl.BlockSpec((tm, tn), lambda i, j, k: (i, j)))
    if has_ln:
        ins += list(ln)
        in_specs += [pl.BlockSpec((1, tn), lambda i, j, k: (0, j))] * 2
    tile = pl.BlockSpec((tm, tn), lambda i, j, k: (i, j))
    outs, routs = _call(
        body, grid=(m // tm, n // tn, nk), ins=ins, in_specs=in_specs,
        outs=[jax.ShapeDtypeStruct((m, n), out_dtype)] + ([jax.ShapeDtypeStruct((m, n), F32)] if has_ln else []),
        out_specs=[tile] * (2 if has_ln else 1),
        scratch=[pltpu.VMEM((tm, tn), F32)], name=name, rider=rider)
    out = tuple(outs) if has_ln else outs[0]
    return out if rider is None else (out, routs)


def _mm_tn(a, b, *, colsum=False, out_dtype=F32, name, rider=None):
    s, ka = a.shape
    nb = b.shape[1]
    ts = _seq_tile(s, 512)
    tka = _pick(ka, 1408)
    tnb = _pick(nb, 1408)
    nk = s // ts
    assert not colsum or tka == ka

    def body(a_ref, b_ref, o_ref, *rest):
        cs_ref = rest[0] if colsum else None
        acc_ref = rest[-1]
        k = pl.program_id(2)

        @pl.when(k == 0)
        def _():
            acc_ref[...] = jnp.zeros_like(acc_ref)
            if colsum:
                cs_ref[...] = jnp.zeros_like(cs_ref)

        bv = b_ref[...]
        acc_ref[...] += _dot_tn(a_ref[...], bv)
        if colsum:
            cs_ref[...] += _colsum(bv.astype(F32))

        @pl.when(k == nk - 1)
        def _():
            o_ref[...] = acc_ref[...].astype(out_dtype)

    outs, routs = _call(
        body, grid=(ka // tka, nb // tnb, nk), ins=[a, b],
        in_specs=[pl.BlockSpec((ts, tka), lambda i, j, k: (k, i)), pl.BlockSpec((ts, tnb), lambda i, j, k: (k, j))],
        outs=[jax.ShapeDtypeStruct((ka, nb), out_dtype)] + ([jax.ShapeDtypeStruct((1, nb), F32)] if colsum else []),
        out_specs=[pl.BlockSpec((tka, tnb), lambda i, j, k: (i, j))] +
                  ([pl.BlockSpec((1, tnb), lambda i, j, k: (0, j))] if colsum else []),
        scratch=[pltpu.VMEM((tka, tnb), F32)], name=name, rider=rider)
    out = tuple(outs) if colsum else outs[0]
    return out if rider is None else (out, routs)


def _ln_fwd(r, g, b, *, name, rider=None):
    s, d = r.shape
    ts = _seq_tile(s, 512)

    def body(r_ref, g_ref, b_ref, o_ref):
        x = r_ref[...]
        mu = jnp.mean(x, axis=1, keepdims=True)
        xc = x - mu
        var = jnp.mean(xc * xc, axis=1, keepdims=True)
        o_ref[...] = xc * lax.rsqrt(var + LN_EPS) * g_ref[...] + b_ref[...]

    (out,), routs = _call(
        body, grid=(s // ts,), ins=[r, g, b],
        in_specs=[pl.BlockSpec((ts, d), lambda i: (i, 0)), _full_spec(g), _full_spec(b)],
        out_specs=[pl.BlockSpec((ts, d), lambda i: (i, 0))], outs=[jax.ShapeDtypeStruct((s, d), F32)],
        name=name, rider=rider)
    return out if rider is None else (out, routs)


def _ln_bwd(r, dy, g, *, name, rider=None):
    s, d = r.shape
    ts = _seq_tile(s, 512)

    def body(r_ref, dy_ref, g_ref, dr_ref, dg_ref, db_ref, ds_ref):
        @pl.when(pl.program_id(0) == 0)
        def _():
            dg_ref[...] = jnp.zeros_like(dg_ref)
            db_ref[...] = jnp.zeros_like(db_ref)
            ds_ref[...] = jnp.zeros_like(ds_ref)

        x = r_ref[...]
        dy = dy_ref[...]
        mu = jnp.mean(x, axis=1, keepdims=True)
        xc = x - mu
        var = jnp.mean(xc * xc, axis=1, keepdims=True)
        rstd = lax.rsqrt(var + LN_EPS)
        xh = xc * rstd
        dxh = dy * g_ref[...]
        m1 = jnp.mean(dxh, axis=1, keepdims=True)
        m2 = jnp.mean(dxh * xh, axis=1, keepdims=True)
        dr = rstd * (dxh - m1 - xh * m2)
        dr_ref[...] = dr
        dg_ref[...] += _colsum(dy * xh)
        db_ref[...] += _colsum(dy)
        ds_ref[...] += _colsum(dr)

    vec = jax.ShapeDtypeStruct((1, d), F32)
    vspec = pl.BlockSpec((1, d), lambda i: (0, 0))
    outs, routs = _call(
        body, grid=(s // ts,), ins=[r, dy, g],
        in_specs=[pl.BlockSpec((ts, d), lambda i: (i, 0)), pl.BlockSpec((ts, d), lambda i: (i, 0)), _full_spec(g)],
        out_specs=[pl.BlockSpec((ts, d), lambda i: (i, 0)), vspec, vspec, vspec],
        outs=[jax.ShapeDtypeStruct((s, d), F32), vec, vec, vec], name=name, rider=rider)
    return outs if rider is None else (outs, routs)


def _loss_ln_bwd(r, g, b, target, *, name):
    s, d = r.shape
    ts = _seq_tile(s, 512)

    def body(r_ref, g_ref, b_ref, t_ref, dr_ref, dg_ref, db_ref, l_ref):
        @pl.when(pl.program_id(0) == 0)
        def _():
            dg_ref[...] = jnp.zeros_like(dg_ref)
            db_ref[...] = jnp.zeros_like(db_ref)
            l_ref[...] = jnp.zeros_like(l_ref)

        x = r_ref[...]
        gam = g_ref[...]
        xc = x - jnp.mean(x, axis=1, keepdims=True)
        var = jnp.mean(xc * xc, axis=1, keepdims=True)
        rstd = lax.rsqrt(var + LN_EPS)
        xh = xc * rstd
        e = xh * gam + b_ref[...] - t_ref[...]
        part = jnp.sum(jnp.sum(e * e, axis=1, keepdims=True), axis=0, keepdims=True) * (0.5 / d)
        l_ref[...] += jnp.broadcast_to(part, l_ref.shape)
        dy = e * (1.0 / d)
        dxh = dy * gam
        m1 = jnp.mean(dxh, axis=1, keepdims=True)
        m2 = jnp.mean(dxh * xh, axis=1, keepdims=True)
        dr_ref[...] = rstd * (dxh - m1 - xh * m2)
        dg_ref[...] += _colsum(dy * xh)
        db_ref[...] += _colsum(dy)

    vec = jax.ShapeDtypeStruct((1, d), F32)
    vspec = pl.BlockSpec((1, d), lambda i: (0, 0))
    tile = pl.BlockSpec((ts, d), lambda i: (i, 0))
    return pl.pallas_call(
        body, grid=(s // ts,), in_specs=[tile, _full_spec(g), _full_spec(b), tile],
        out_specs=[tile, vspec, vspec, pl.BlockSpec((SUBLANE, LANE), lambda i: (0, 0))],
        out_shape=[jax.ShapeDtypeStruct((s, d), F32), vec, vec, jax.ShapeDtypeStruct((SUBLANE, LANE), F32)],
        compiler_params=_cparams(1), name=name)(r, g, b, target)


SCAN_CHUNK = 32


def _cscan_levels(bufs, apow_ref, t, pad, *, reverse):
    half = bufs[0].shape[1] // 2
    ch = min(SCAN_CHUNK, t)
    nlev = t.bit_length() - 1
    assert (1 << nlev) == t
    for k in range(nlev):
        d = 1 << k
        src, dst = bufs[k % 2], bufs[(k + 1) % 2]

        def chunk(c, carry, src=src, dst=dst, d=d, k=k):
            ar = apow_ref[k:k + 1, :half]
            ai = apow_ref[k:k + 1, half:]
            if reverse:
                ai = -ai
            r0 = pl.multiple_of(c * ch, ch)
            cur = src[pl.ds(pad + r0, ch), :]
            if d >= SUBLANE:
                off = pad + d if reverse else pad - d
                sh = src[pl.ds(off + r0, ch), :]
            elif reverse:
                blk = src[pl.ds(pad + r0, ch + SUBLANE), :]
                sh = pltpu.roll(blk, ch + SUBLANE - d, axis=0)[:ch, :]
            else:
                blk = src[pl.ds(pad - SUBLANE + r0, ch + SUBLANE), :]
                sh = pltpu.roll(blk, d, axis=0)[SUBLANE:, :]
            sre, sim = sh[:, :half], sh[:, half:]
            dst[pl.ds(pad + r0, ch), :half] = cur[:, :half] + ar * sre - ai * sim
            dst[pl.ds(pad + r0, ch), half:] = cur[:, half:] + ar * sim + ai * sre
            return carry

        lax.fori_loop(0, t // ch, chunk, 0)
    return nlev % 2


def _rscan_levels(abufs, bbufs, t, pad, *, reverse):
    nlev = t.bit_length() - 1
    assert (1 << nlev) == t
    for k in range(nlev):
        d = 1 << k
        asrc, adst = abufs[k % 2], abufs[(k + 1) % 2]
        bsrc, bdst = bbufs[k % 2], bbufs[(k + 1) % 2]
        off = pad + d if reverse else pad - d
        a = asrc[pad:pad + t, :]
        bdst[pad:pad + t, :] = a * bsrc[off:off + t, :] + bsrc[pad:pad + t, :]
        if k < nlev - 1:
            adst[pad:pad + t, :] = a * asrc[off:off + t, :]
    return nlev % 2


S5_CHUNK = 16
S5_SG = S5_GROUPS // 2
S5_SG_IN = 2 * S5_CHUNK * S5_CH
S5_SG_ST = 2 * S5_STATE


S5_HALF_SGS = S5_SG // 2
S5_HALF_IN = S5_HALF_SGS * S5_SG_IN


def _s5_perm():
    idx = jnp.arange(S5_HALF_IN)
    step, grp, chan = idx // LANE, (idx % LANE) // S5_CH, idx % S5_CH
    col = (grp // 2) * S5_SG_IN + (grp % 2) * (S5_CHUNK * S5_CH) + step * S5_CH + chan
    return (col[:, None] == idx[None, :]).astype(BF16)


def _s5_to_chunks(x, col_block, perm, *, name):
    s, width = x.shape
    nb = s // S5_CHUNK
    x3 = x.reshape(nb, S5_CHUNK, width)

    def body(x_ref, perm_ref, o_ref):
        tok = jnp.concatenate([x_ref[:, t, :].astype(BF16) for t in range(S5_CHUNK)], axis=1)
        grouped = _dot(tok, perm_ref[...]).astype(BF16)
        for k in range(S5_HALF_SGS):
            o_ref[k] = grouped[:, k * S5_SG_IN:(k + 1) * S5_SG_IN]

    return pl.pallas_call(
        body, grid=(2,),
        in_specs=[pl.BlockSpec((nb, S5_CHUNK, LANE), lambda h: (0, 0, col_block + h)), _full_spec(perm)],
        out_specs=pl.BlockSpec((S5_HALF_SGS, nb, S5_SG_IN), lambda h: (h, 0, 0)),
        out_shape=jax.ShapeDtypeStruct((S5_SG, nb, S5_SG_IN), BF16),
        compiler_params=_cparams(1), name=name)(x3, perm)


def _s5_from_chunks(y, perm, *, name):
    _, nb, _ = y.shape

    def body(y_ref, perm_ref, o_ref):
        grouped = jnp.concatenate([y_ref[k] for k in range(S5_HALF_SGS)], axis=1)
        hi = grouped.astype(BF16)
        lo = (grouped - hi.astype(F32)).astype(BF16)
        tok = _dot_nt(hi, perm_ref[...]) + _dot_nt(lo, perm_ref[...])
        for t in range(S5_CHUNK):
            o_ref[:, t, :] = tok[:, t * LANE:(t + 1) * LANE]

    out = pl.pallas_call(
        body, grid=(2,),
        in_specs=[pl.BlockSpec((S5_HALF_SGS, nb, S5_SG_IN), lambda h: (h, 0, 0)), _full_spec(perm)],
        out_specs=pl.BlockSpec((nb, S5_CHUNK, LANE), lambda h: (0, 0, h)),
        out_shape=jax.ShapeDtypeStruct((nb, S5_CHUNK, D_GROUP), F32),
        compiler_params=_cparams(1), name=name)(y, perm)
    return out.reshape(nb * S5_CHUNK, D_GROUP)


def _s5_core_fwd(u2, m2, pre, pim, qre, qim, a16, *, name):
    sg, nb, nin = u2.shape
    st2 = 2 * S5_SG_ST
    pad = nb // 2

    def body(u_ref, m_ref, pre_ref, pim_ref, qre_ref, qim_ref, a_ref, y_ref, x_ref, buf0, buf1):
        @pl.when(pl.program_id(0) == 0)
        def _():
            buf0[0:pad, :] = jnp.zeros((pad, st2), F32)
            buf1[0:pad, :] = jnp.zeros((pad, st2), F32)

        u = u_ref[...]
        buf0[pad:pad + nb, :S5_SG_ST] = _dot(u, pre_ref[...])
        buf0[pad:pad + nb, S5_SG_ST:] = _dot(u, pim_ref[...])
        xbuf = (buf0, buf1)[_cscan_levels((buf0, buf1), a_ref, nb, pad, reverse=False)]
        x_ref[...] = xbuf[pad:pad + nb, :]
        xprev = xbuf[pad - 1:pad - 1 + nb, :]
        y_ref[...] = _dot(u, m_ref[...]) + _dot(xprev[:, :S5_SG_ST], qre_ref[...]) + _dot(xprev[:, S5_SG_ST:], qim_ref[...])

    ins = [u2, m2, pre, pim, qre, qim, a16]
    return pl.pallas_call(
        body, grid=(sg,), in_specs=[pl.BlockSpec((None,) + a.shape[1:], lambda i: (i, 0, 0)) for a in ins],
        out_specs=[pl.BlockSpec((None, nb, nin), lambda i: (i, 0, 0)), pl.BlockSpec((None, nb, st2), lambda i: (i, 0, 0))],
        out_shape=[jax.ShapeDtypeStruct((sg, nb, nin), F32), jax.ShapeDtypeStruct((sg, nb, st2), F32)],
        scratch_shapes=[pltpu.VMEM((pad + nb, st2), F32), pltpu.VMEM((pad + nb, st2), F32)],
        compiler_params=_cparams(1), name=name)(*ins)


def _s5_core_bwd(u2, dy2, x_all, m2, pre, pim, qre, qim, a16, *, name):
    sg, nb, nin = u2.shape
    half = S5_SG_ST
    st2 = 2 * half
    pad = nb // 2

    def body(u_ref, dy_ref, x_ref, m_ref, pre_ref, pim_ref, qre_ref, qim_ref, a_ref,
             du_ref, dm_ref, dpre_ref, dpim_ref, dqre_ref, dqim_ref, da_ref, buf2, buf3, xp):
        @pl.when(pl.program_id(0) == 0)
        def _():
            buf2[nb:nb + pad, :] = jnp.zeros((pad, st2), F32)
            buf3[nb:nb + pad, :] = jnp.zeros((pad, st2), F32)
            xp[0:SUBLANE, :] = jnp.zeros((SUBLANE, st2), F32)

        u = u_ref[...]
        dy = dy_ref[...]
        dm_ref[...] = _dot_tn(u, dy)
        xp[SUBLANE:SUBLANE + nb, :] = x_ref[...]
        xprev = xp[SUBLANE - 1:SUBLANE - 1 + nb, :]
        xre, xim = xprev[:, :half], xprev[:, half:]
        dqre_ref[...] = _dot_tn(xre, dy)
        dqim_ref[...] = _dot_tn(xim, dy)
        buf2[0:nb, :half] = _dot_nt(dy, qre_ref[...])
        buf2[0:nb, half:] = _dot_nt(dy, qim_ref[...])
        mbuf = (buf2, buf3)[_cscan_levels((buf2, buf3), a_ref, nb, 0, reverse=True)]
        lam = mbuf[1:1 + nb, :]
        lre, lim = lam[:, :half], lam[:, half:]
        dpre_ref[...] = _dot_tn(u, lre)
        dpim_ref[...] = _dot_tn(u, lim)
        du_ref[...] = _dot_nt(dy, m_ref[...]) + _dot_nt(lre, pre_ref[...]) + _dot_nt(lim, pim_ref[...])
        da_ref[:, :half] = _colsum(lre * xre + lim * xim)
        da_ref[:, half:] = _colsum(lim * xre - lre * xim)

    ins = [u2, dy2, x_all, m2, pre, pim, qre, qim, a16]
    outs = [jax.ShapeDtypeStruct((sg, nb, nin), F32)] + [jax.ShapeDtypeStruct(a.shape, F32) for a in (m2, pre, pim, qre, qim)] + \
           [jax.ShapeDtypeStruct((sg, 1, st2), F32)]
    return pl.pallas_call(
        body, grid=(sg,), in_specs=[pl.BlockSpec((None,) + a.shape[1:], lambda i: (i, 0, 0)) for a in ins],
        out_specs=[pl.BlockSpec((None,) + o.shape[1:], lambda i: (i, 0, 0)) for o in outs], out_shape=outs,
        scratch_shapes=[pltpu.VMEM((nb + pad, st2), F32), pltpu.VMEM((nb + pad, st2), F32),
                        pltpu.VMEM((SUBLANE + nb, st2), F32)],
        compiler_params=_cparams(1), name=name)(*ins)


def _s5_glu_fwd(y1, wglu, bglu, *, name, rider=None):
    s = y1.shape[0]
    t = _seq_tile(s, 512)

    def body(y1_ref, wglu_ref, bglu_ref, out_ref):
        y2 = _gelu(y1_ref[...])
        out_ref[...] = (y2 * _sigmoid(_dot(y2, wglu_ref[...]) + bglu_ref[...])).astype(BF16)

    return _call(
        body, grid=(s // t,), ins=[y1, wglu, bglu],
        in_specs=[pl.BlockSpec((t, D_GROUP), lambda i: (i, 0)), _full_spec(wglu), _full_spec(bglu)],
        out_specs=[pl.BlockSpec((t, D_GROUP), lambda i: (i, MIX_S5))], outs=[jax.ShapeDtypeStruct((s, D_MODEL), BF16)],
        name=name, rider=rider)


def _s5_glu_bwd(y1, dmix, wglu, bglu, *, name):
    s = y1.shape[0]
    t = _seq_tile(s, 512)

    def body(y1_ref, do_ref, wglu_ref, bglu_ref, dy1_ref, dwglu_ref, dbglu_ref):
        @pl.when(pl.program_id(0) == 0)
        def _():
            dwglu_ref[...] = jnp.zeros_like(dwglu_ref)
            dbglu_ref[...] = jnp.zeros_like(dbglu_ref)

        dout = do_ref[...]
        y2, dgelu = _gelu_and_grad(y1_ref[...])
        sg = _sigmoid(_dot(y2, wglu_ref[...]) + bglu_ref[...])
        dz = dout * y2 * sg * (1.0 - sg)
        dwglu_ref[...] += _dot_tn(y2, dz)
        dbglu_ref[...] += _colsum(dz)
        dy1_ref[...] = (dout * sg + _dot_nt(dz, wglu_ref[...])) * dgelu

    outs = [jax.ShapeDtypeStruct((s, D_GROUP), F32), jax.ShapeDtypeStruct((D_GROUP, D_GROUP), F32),
            jax.ShapeDtypeStruct((1, D_GROUP), F32)]
    return pl.pallas_call(
        body, grid=(s // t,),
        in_specs=[pl.BlockSpec((t, D_GROUP), lambda i: (i, 0)), pl.BlockSpec((t, D_GROUP), lambda i: (i, MIX_S5)),
                  _full_spec(wglu), _full_spec(bglu)],
        out_specs=[pl.BlockSpec((t, D_GROUP), lambda i: (i, 0)), _full_spec(outs[1]), _full_spec(outs[2])],
        out_shape=outs, compiler_params=_cparams(1), name=name)(y1, dmix, wglu, bglu)


def _pair_blockdiag(x):
    g, r, c = x.shape
    x = x.reshape(g // 2, 2, r, c)
    z = jnp.zeros_like(x[:, 0])
    return jnp.concatenate([jnp.concatenate([x[:, 0], z], axis=2), jnp.concatenate([z, x[:, 1]], axis=2)], axis=1)


def _s5_chunk_map(lam_re, lam_im, log_dt, b_re, b_im, c_re, c_im, d_skip):
    g, n, c, lc = S5_GROUPS, S5_STATE, S5_CH, S5_CHUNK
    dt = jnp.exp(log_dt)[:, None]
    mag, ang = lam_re * dt, lam_im * dt
    j = jnp.arange(lc + 1, dtype=F32)[:, None, None]
    pw_mag = jnp.exp(j * mag)
    pw_re, pw_im = pw_mag * jnp.cos(j * ang), pw_mag * jnp.sin(j * ang)
    a_re, a_im = pw_re[1], pw_im[1]
    den = lam_re * lam_re + lam_im * lam_im
    n_re = a_re - 1.0
    k_re = (n_re * lam_re + a_im * lam_im) / den
    k_im = (a_im * lam_re - n_re * lam_im) / den
    bb_re = k_re[..., None] * b_re - k_im[..., None] * b_im
    bb_im = k_re[..., None] * b_im + k_im[..., None] * b_re
    e_re = pw_re[:lc, :, :, None] * bb_re - pw_im[:lc, :, :, None] * bb_im
    e_im = pw_re[:lc, :, :, None] * bb_im + pw_im[:lc, :, :, None] * bb_re
    kern = jnp.einsum("gdn,jgnc->jgdc", c_re, e_re) - jnp.einsum("gdn,jgnc->jgdc", c_im, e_im)
    steps = jnp.arange(lc)
    lag = (steps[None, :, None] + steps[:, None, None] == steps[None, None, :]).astype(F32)
    m = jnp.einsum("jst,jgdc->gsctd", lag, kern).reshape(g, lc * c, lc * c)
    skip = jnp.tile(d_skip.reshape(g, 1, c), (1, lc, 1)).reshape(g, lc * c)
    m = m + jnp.eye(lc * c, dtype=F32)[None] * skip[:, None, :]
    p_re = jnp.transpose(e_re[::-1], (1, 0, 3, 2)).reshape(g, lc * c, n)
    p_im = jnp.transpose(e_im[::-1], (1, 0, 3, 2)).reshape(g, lc * c, n)
    f_re = c_re[None] * pw_re[1:, :, None, :] - c_im[None] * pw_im[1:, :, None, :]
    f_im = c_re[None] * pw_im[1:, :, None, :] + c_im[None] * pw_re[1:, :, None, :]
    q_re = jnp.transpose(f_re, (1, 3, 0, 2)).reshape(g, n, lc * c)
    q_im = -jnp.transpose(f_im, (1, 3, 0, 2)).reshape(g, n, lc * c)
    a16 = jnp.concatenate([pw_re[lc].reshape(S5_SG, 1, S5_SG_ST), pw_im[lc].reshape(S5_SG, 1, S5_SG_ST)], axis=2)
    return (_pair_blockdiag(m), _pair_blockdiag(p_re), _pair_blockdiag(p_im), _pair_blockdiag(q_re),
            _pair_blockdiag(q_im), a16)


def _s5_a16_powers(a16, nlev):
    half = S5_SG_ST
    re, im = a16[:, :, :half], a16[:, :, half:]
    rows = []
    for _ in range(nlev):
        rows.append(jnp.concatenate([re, im], axis=2))
        re, im = re * re - im * im, 2.0 * re * im
    n_rows = -(-nlev // SUBLANE) * SUBLANE
    rows += [jnp.zeros_like(rows[0])] * (n_rows - nlev)
    return lax.stop_gradient(jnp.concatenate(rows, axis=1))


CV_TILE = 256
CV_PAD = 32
CV_CHUNK = 64


def _shifted_copies(buf, shifted, rows):
    n = rows - SUBLANE
    for s in range(1, SUBLANE):
        shifted[s - 1, 0:n, :] = buf[s:s + n, :]


def _window(buf, shifted, o, ch):
    q, s = divmod(o, SUBLANE)
    if s == 0:
        return buf[o:o + ch, :]
    return shifted[s - 1, q * SUBLANE:q * SUBLANE + ch, :]


def _gn_stats(c, mavg):
    mu = _dot_hi(c, mavg)
    cen = c - mu
    var = _dot_hi(cen * cen, mavg)
    rstd = lax.rsqrt(var + LN_EPS)
    return cen * rstd, rstd


def _cv_fwd(h_in, cw, cb, gng, gnb, mavg, wpw, bpw, mix, *, name, rider=None):
    s = h_in.shape[0]
    t = _seq_tile(s, CV_TILE)
    ch = min(CV_CHUNK, t)

    def body(v_ref, g_ref, cw_ref, cb_ref, gng_ref, gnb_ref, mavg_ref, wpw_ref, bpw_ref, _mix_in, out_ref, c_ref, xpad,
             shifted):
        @pl.when(pl.program_id(0) == 0)
        def _():
            xpad[0:CV_PAD, :] = jnp.zeros((CV_PAD, D_GROUP), F32)

        xpad[CV_PAD:CV_PAD + t, :] = v_ref[...] * _sigmoid(g_ref[...])
        _shifted_copies(xpad, shifted, t + CV_PAD)
        for r0 in range(0, t, ch):
            acc = jnp.broadcast_to(cb_ref[...], (ch, D_GROUP))
            for k in range(CONV_WIDTH):
                o = CV_PAD - (CONV_WIDTH - 1) + k + r0
                acc = acc + cw_ref[k:k + 1, :] * _window(xpad, shifted, o, ch)
            c_ref[r0:r0 + ch, :] = acc
        xpad[0:CV_PAD, :] = xpad[t:t + CV_PAD, :]
        xn, _ = _gn_stats(c_ref[...], mavg_ref[...])
        gn = xn * gng_ref[...] + gnb_ref[...]
        out_ref[...] = (_dot(gn * _sigmoid(gn), wpw_ref[...]) + bpw_ref[...]).astype(BF16)

    ins = [h_in, h_in, cw, cb, gng, gnb, mavg, wpw, bpw, mix]
    in_specs = [pl.BlockSpec((t, D_GROUP), lambda i: (i, COL_CV_V)), pl.BlockSpec((t, D_GROUP), lambda i: (i, COL_CV_G))] + \
               [_full_spec(a) for a in ins[2:9]] + [_ANY]
    return _call(
        body, grid=(s // t,), ins=ins, in_specs=in_specs,
        out_specs=[pl.BlockSpec((t, D_GROUP), lambda i: (i, MIX_CV)), pl.BlockSpec((t, D_GROUP), lambda i: (i, 0))],
        outs=[jax.ShapeDtypeStruct((s, D_MODEL), BF16), jax.ShapeDtypeStruct((s, D_GROUP), F32)],
        aliases={9: 0},
        scratch=[pltpu.VMEM((CV_PAD + t, D_GROUP), F32), pltpu.VMEM((SUBLANE - 1, CV_PAD + t, D_GROUP), F32)],
        name=name, rider=rider)


def _cv_bwd(h_in, c, dmix, cw, gng, gnb, mavg, wpw, *, name, rider=None):
    s = h_in.shape[0]
    t = _seq_tile(s, CV_TILE)
    nt = s // t
    ch = min(CV_CHUNK, t)

    def body(v_ref, g_ref, c_ref, do_ref, cw_ref, gng_ref, gnb_ref, mavg_ref, wpw_ref,
             dvg_ref, dwpw_ref, dcw_ref, dbpw_ref, dgg_ref, dgb_ref, dcb_ref, dcpad, hgbuf, shifted):
        @pl.when(pl.program_id(0) == 0)
        def _():
            dcpad[t:t + CV_PAD, :] = jnp.zeros((CV_PAD, D_GROUP), F32)
            for r in (dwpw_ref, dcw_ref, dbpw_ref, dgg_ref, dgb_ref, dcb_ref):
                r[...] = jnp.zeros_like(r)

        mavg = mavg_ref[...]
        xn, rstd = _gn_stats(c_ref[...], mavg)
        gg = gng_ref[...]
        gn = xn * gg + gnb_ref[...]
        sg = _sigmoid(gn)
        dout = do_ref[...]
        dwpw_ref[...] += _dot_tn(gn * sg, dout)
        dbpw_ref[...] += _colsum(dout)
        dgn = _dot_nt(dout, wpw_ref[...]) * (sg * (1.0 + gn * (1.0 - sg)))
        dgg_ref[...] += _colsum(dgn * xn)
        dgb_ref[...] += _colsum(dgn)
        dxn = dgn * gg
        dc = rstd * (dxn - _dot_hi(dxn, mavg) - xn * _dot_hi(dxn * xn, mavg))
        dcb_ref[...] += _colsum(dc)
        dcpad[0:t, :] = dc

        v = v_ref[...]
        sgm = _sigmoid(g_ref[...])
        hgbuf[...] = v * sgm
        _shifted_copies(dcpad, shifted, t + CV_PAD)
        for r0 in range(0, t, ch):
            hg = hgbuf[r0:r0 + ch, :]
            acc = jnp.zeros((ch, D_GROUP), F32)
            for k in range(CONV_WIDTH):
                o = (CONV_WIDTH - 1) - k + r0
                sh = _window(dcpad, shifted, o, ch)
                acc = acc + cw_ref[k:k + 1, :] * sh
                dcw_ref[k:k + 1, :] += _colsum(hg * sh)
            hgbuf[r0:r0 + ch, :] = acc
        dcpad[t:t + CV_PAD, :] = dcpad[0:CV_PAD, :]
        dhg = hgbuf[...]
        dvg_ref[:, :D_GROUP] = dhg * sgm
        dvg_ref[:, D_GROUP:] = dhg * v * sgm * (1.0 - sgm)

    def rev(col):
        return lambda i: (nt - 1 - i, col)

    ins = [h_in, h_in, c, dmix, cw, gng, gnb, mavg, wpw]
    in_specs = [pl.BlockSpec((t, D_GROUP), rev(COL_CV_V)), pl.BlockSpec((t, D_GROUP), rev(COL_CV_G)),
                pl.BlockSpec((t, D_GROUP), rev(0)), pl.BlockSpec((t, D_GROUP), rev(MIX_CV))] + [_full_spec(a) for a in ins[4:]]
    vec = jax.ShapeDtypeStruct((1, D_GROUP), F32)
    outs = [jax.ShapeDtypeStruct((s, N_IN_COLS), F32),
            jax.ShapeDtypeStruct((D_GROUP, D_GROUP), F32), jax.ShapeDtypeStruct((CV_PAD, D_GROUP), F32), vec, vec, vec, vec]
    out_specs = [pl.BlockSpec((t, 2 * D_GROUP), rev(COL_CV_V // 2))] + [_full_spec(o) for o in outs[1:]]
    return _call(
        body, grid=(nt,), ins=ins, in_specs=in_specs, out_specs=out_specs, outs=outs,
        scratch=[pltpu.VMEM((t + CV_PAD, D_GROUP), F32), pltpu.VMEM((t, D_GROUP), F32),
                 pltpu.VMEM((SUBLANE - 1, t + CV_PAD, D_GROUP), F32)], name=name, rider=rider)


LRU_TILE = 256


def _lru_gates(xc, wr_ref, br_ref, wi_ref, bi_ref, sp_ref):
    r = _sigmoid(_dot(xc, wr_ref[...]) + br_ref[...])
    i = _sigmoid(_dot(xc, wi_ref[...]) + bi_ref[...])
    log_a = -LRU_C * r * sp_ref[...]
    a = jnp.exp(log_a)
    m = jnp.sqrt(_neg_expm1(2.0 * log_a))
    return r, i, a, m


def _lru_fwd(h_in, lcw, lcb, wr, br, wi, bi, sp, mix, *, name):
    s = h_in.shape[0]
    t = _seq_tile(s, LRU_TILE)
    pad = max(t // 2, SUBLANE)

    def body(xg_ref, xr_ref, lcw_ref, lcb_ref, wr_ref, br_ref, wi_ref, bi_ref, sp_ref, _mix_in,
             out_ref, xc_ref, h_ref, xpad, a0, a1, b0, b1, carry):
        @pl.when(pl.program_id(0) == 0)
        def _():
            xpad[0:SUBLANE, :] = jnp.zeros((SUBLANE, D_GROUP), F32)
            for bf in (a0, a1, b0, b1):
                bf[0:pad, :] = jnp.zeros((pad, D_GROUP), F32)
            carry[...] = jnp.zeros_like(carry)

        xpad[SUBLANE:SUBLANE + t, :] = xr_ref[...]
        xc = jnp.broadcast_to(lcb_ref[...], (t, D_GROUP))
        for k in range(LRU_CONV_WIDTH):
            o = SUBLANE - (LRU_CONV_WIDTH - 1) + k
            xc = xc + lcw_ref[k:k + 1, :] * xpad[o:o + t, :]
        xpad[0:SUBLANE, :] = xpad[t:t + SUBLANE, :]
        xc_ref[...] = xc
        _, i, a, m = _lru_gates(xc, wr_ref, br_ref, wi_ref, bi_ref, sp_ref)
        a0[pad:pad + t, :] = a
        b0[pad:pad + t, :] = m * (i * xc)
        b0[pad:pad + 1, :] += a0[pad:pad + 1, :] * carry[0:1, :]
        fin = _rscan_levels((a0, a1), (b0, b1), t, pad, reverse=False)
        hbuf = (b0, b1)[fin]
        carry[0:1, :] = hbuf[pad + t - 1:pad + t, :]
        h = hbuf[pad:pad + t, :]
        h_ref[...] = h
        out_ref[...] = (h * _gelu(xg_ref[...])).astype(BF16)

    ins = [h_in, h_in, lcw, lcb, wr, br, wi, bi, sp, mix]
    row = pl.BlockSpec((t, D_GROUP), lambda i: (i, 0))
    in_specs = [pl.BlockSpec((t, D_GROUP), lambda i: (i, COL_LRU_G)), pl.BlockSpec((t, D_GROUP), lambda i: (i, COL_LRU_X))] + \
               [_full_spec(a) for a in ins[2:9]] + [_ANY]
    return pl.pallas_call(
        body, grid=(s // t,), in_specs=in_specs,
        out_specs=[pl.BlockSpec((t, D_GROUP), lambda i: (i, MIX_LRU)), row, row],
        out_shape=[jax.ShapeDtypeStruct((s, D_MODEL), BF16)] + [jax.ShapeDtypeStruct((s, D_GROUP), F32)] * 2,
        input_output_aliases={9: 0},
        scratch_shapes=[pltpu.VMEM((SUBLANE + t, D_GROUP), F32)] + [pltpu.VMEM((pad + t, D_GROUP), F32)] * 4 +
                       [pltpu.VMEM((SUBLANE, D_GROUP), F32)],
        compiler_params=_cparams(1), name=name)(*ins)


def _lru_bwd(h_in, xc_all, h_all, dmix, lcw, wr, br, wi, bi, sp, dh_all, *, name):
    s = h_in.shape[0]
    t = _seq_tile(s, LRU_TILE)
    nt = s // t
    pad = max(t // 2, SUBLANE)
    tb = t // SUBLANE

    def body(xg_ref, xr_ref, xc_ref, h_ref, hprev_ref, do_ref, lcw_ref, wr_ref, br_ref, wi_ref, bi_ref, sp_ref, _dh_in,
             dgr_ref, dwr_ref, dwi_ref, dlcw_ref, dbr_ref, dbi_ref, dsp_ref, dlcb_ref,
             a0, a1, b0, b1, hp, dxpad, carry):
        pid = pl.program_id(0)

        @pl.when(pid == 0)
        def _():
            for bf in (a0, a1, b0, b1):
                bf[pad + t:pad + t + pad, :] = jnp.zeros((pad, D_GROUP), F32)
            dxpad[t:t + SUBLANE, :] = jnp.zeros((SUBLANE, D_GROUP), F32)
            carry[...] = jnp.zeros_like(carry)
            for r in (dwr_ref, dwi_ref, dlcw_ref, dbr_ref, dbi_ref, dsp_ref, dlcb_ref):
                r[...] = jnp.zeros_like(r)

        xc = xc_ref[...]
        h = h_ref[...]
        dout = do_ref[...]
        gate, dgate = _gelu_and_grad(xg_ref[...])
        dgr_ref[:, :D_GROUP] = dout * h * dgate
        r, i, a, m = _lru_gates(xc, wr_ref, br_ref, wi_ref, bi_ref, sp_ref)

        a0[pad:pad + t, :] = a
        b0[pad:pad + t, :] = dout * gate
        b0[pad + t - 1:pad + t, :] += carry[0:1, :]
        a1[pad:pad + t, :] = a0[pad + 1:pad + 1 + t, :]
        fin = _rscan_levels((a1, a0), (b0, b1), t, pad, reverse=True)
        lam = (b0, b1)[fin][pad:pad + t, :]
        carry[0:1, :] = a[0:1, :] * lam[0:1, :]

        is_first = pid == nt - 1
        hp[0:SUBLANE, :] = jnp.where(is_first, 0.0, hprev_ref[...])
        hp[SUBLANE:SUBLANE + t, :] = h
        hprev = hp[SUBLANE - 1:SUBLANE - 1 + t, :]

        ix = i * xc
        dmm = lam * ix
        dix = lam * m
        da = lam * hprev - dmm * (a / m)
        dlog_a = da * a
        dr = dlog_a * (-LRU_C * sp_ref[...])
        dsp_ref[...] += _colsum(dlog_a * (-LRU_C * r))
        dpr = dr * r * (1.0 - r)
        dpi = dix * xc * i * (1.0 - i)
        dbr_ref[...] += _colsum(dpr)
        dbi_ref[...] += _colsum(dpi)
        dwr_ref[...] += _dot_tn(xc, dpr)
        dwi_ref[...] += _dot_tn(xc, dpi)
        dxc = dix * i + _dot_nt(dpr, wr_ref[...]) + _dot_nt(dpi, wi_ref[...])
        dlcb_ref[...] += _colsum(dxc)

        dxpad[0:t, :] = dxc
        xr = xr_ref[...]
        dxr = jnp.zeros((t, D_GROUP), F32)
        for k in range(LRU_CONV_WIDTH):
            o = (LRU_CONV_WIDTH - 1) - k
            sh = dxpad[o:o + t, :]
            dxr = dxr + lcw_ref[k:k + 1, :] * sh
            dlcw_ref[k:k + 1, :] += _colsum(xr * sh)
        dxpad[t:t + SUBLANE, :] = dxpad[0:SUBLANE, :]
        dgr_ref[:, D_GROUP:] = dxr

    def rev(col):
        return lambda i: (nt - 1 - i, col)

    ins = [h_in, h_in, xc_all, h_all, h_all, dmix, lcw, wr, br, wi, bi, sp, dh_all]
    in_specs = [pl.BlockSpec((t, D_GROUP), rev(COL_LRU_G)), pl.BlockSpec((t, D_GROUP), rev(COL_LRU_X)),
                pl.BlockSpec((t, D_GROUP), rev(0)), pl.BlockSpec((t, D_GROUP), rev(0)),
                pl.BlockSpec((SUBLANE, D_GROUP), lambda i: (jnp.maximum((nt - 1 - i) * tb - 1, 0), 0)),
                pl.BlockSpec((t, D_GROUP), rev(MIX_LRU))] + [_full_spec(a) for a in ins[6:12]] + [_ANY]
    vec = jax.ShapeDtypeStruct((1, D_GROUP), F32)
    mat = jax.ShapeDtypeStruct((D_GROUP, D_GROUP), F32)
    outs = [jax.ShapeDtypeStruct((s, N_IN_COLS), F32), mat, mat, jax.ShapeDtypeStruct((SUBLANE, D_GROUP), F32),
            vec, vec, vec, vec]
    out_specs = [pl.BlockSpec((t, 2 * D_GROUP), rev(COL_LRU_G // 2))] + [_full_spec(o) for o in outs[1:]]
    return pl.pallas_call(
        body, grid=(nt,), in_specs=in_specs, out_specs=out_specs, out_shape=outs, input_output_aliases={12: 0},
        scratch_shapes=[pltpu.VMEM((pad + t + pad, D_GROUP), F32)] * 4 +
                       [pltpu.VMEM((SUBLANE + t, D_GROUP), F32), pltpu.VMEM((t + SUBLANE, D_GROUP), F32),
                        pltpu.VMEM((SUBLANE, D_GROUP), F32)],
        compiler_params=_cparams(1), name=name)(*ins)


def _blockdiag(w):
    h, d, _ = w.shape
    return jnp.tile(w.reshape(h * d, d), (1, h)) * _block_mask(h, d, d)


ATTN_TILE = 512
ATTN_SCALE = ATTN_HEAD_DIM ** -0.5


def _attn_big(kv):
    m = kv.shape[0]
    kbig = jnp.tile(kv[:, :D_GROUP].T, (1, ATTN_HEADS)) * _block_mask(ATTN_HEADS, ATTN_HEAD_DIM, m)
    vbig = jnp.tile(kv[:, D_GROUP:], (ATTN_HEADS, 1)) * _block_mask(ATTN_HEADS, m, ATTN_HEAD_DIM)
    return kbig, vbig


def _attn_probs(q, kbig_ref, m):
    sc = _dot(q, kbig_ref[...]) * ATTN_SCALE
    ps = []
    for h in range(ATTN_HEADS):
        sh = sc[:, h * m:(h + 1) * m]
        e = jnp.exp(sh - jnp.max(sh, axis=1, keepdims=True))
        ps.append(e / jnp.sum(e, axis=1, keepdims=True))
    return ps


def _attn_fwd(h_in, kbig, vbig, mix, *, name):
    s = h_in.shape[0]
    t = _seq_tile(s, ATTN_TILE)
    m = kbig.shape[1] // ATTN_HEADS

    def body(q_ref, kbig_ref, vbig_ref, _mix_in, o_ref):
        ps = _attn_probs(q_ref[...], kbig_ref, m)
        o_ref[...] = _dot(jnp.concatenate(ps, axis=1), vbig_ref[...]).astype(BF16)

    return pl.pallas_call(
        body, grid=(s // t,),
        in_specs=[pl.BlockSpec((t, D_GROUP), lambda i: (i, COL_Q)), _full_spec(kbig), _full_spec(vbig), _ANY],
        out_specs=pl.BlockSpec((t, D_GROUP), lambda i: (i, MIX_ATTN)),
        out_shape=jax.ShapeDtypeStruct((s, D_MODEL), BF16), input_output_aliases={3: 0},
        compiler_params=_cparams(1), name=name)(h_in, kbig, vbig, mix)


def _attn_bwd(h_in, dmix, kbig, vbig, du_s5, dh_all, *, name):
    s = h_in.shape[0]
    t = _seq_tile(s, ATTN_TILE)
    m = kbig.shape[1] // ATTN_HEADS

    def body(q_ref, do_ref, kbig_ref, vbig_ref, dus5_ref, _dh_in, dpair_ref, dk_ref, dv_ref):
        @pl.when(pl.program_id(0) == 0)
        def _():
            dk_ref[...] = jnp.zeros_like(dk_ref)
            dv_ref[...] = jnp.zeros_like(dv_ref)

        q = q_ref[...]
        dout = do_ref[...]
        ps = _attn_probs(q, kbig_ref, m)
        dp = _dot_nt(dout, vbig_ref[...])
        dss = []
        for h in range(ATTN_HEADS):
            dph = dp[:, h * m:(h + 1) * m]
            dss.append(ps[h] * (dph - jnp.sum(dph * ps[h], axis=1, keepdims=True)))
        ds = (jnp.concatenate(dss, axis=1) * ATTN_SCALE).astype(BF16)
        dv_ref[...] += _dot_tn(jnp.concatenate(ps, axis=1), dout)
        dpair_ref[:, :D_GROUP] = dus5_ref[...]
        dpair_ref[:, D_GROUP:] = _dot_nt(ds, kbig_ref[...])
        dk_ref[...] += _dot_tn(q, ds)

    assert (COL_S5, COL_Q) == (4, 5)
    outs = [jax.ShapeDtypeStruct((s, N_IN_COLS), F32), jax.ShapeDtypeStruct(kbig.shape, F32),
            jax.ShapeDtypeStruct(vbig.shape, F32)]
    return pl.pallas_call(
        body, grid=(s // t,),
        in_specs=[pl.BlockSpec((t, D_GROUP), lambda i: (i, COL_Q)), pl.BlockSpec((t, D_GROUP), lambda i: (i, MIX_ATTN)),
                  _full_spec(kbig), _full_spec(vbig), pl.BlockSpec((t, D_GROUP), lambda i: (i, 0)), _ANY],
        out_specs=[pl.BlockSpec((t, 2 * D_GROUP), lambda i: (i, COL_S5 // 2)), _full_spec(outs[1]), _full_spec(outs[2])],
        out_shape=outs, input_output_aliases={5: 0},
        compiler_params=_cparams(1), name=name)(h_in, dmix, kbig, vbig, du_s5, dh_all)


FFN_TILE = 128
FFN_COL_CHUNK = 256
FFN_ROW_CHUNK = 64


def _ffn_conv(pad_ref, w_ref, b_ref, r0, ch, c0):
    cc = FFN_COL_CHUNK
    acc = jnp.broadcast_to(b_ref[:, c0:c0 + cc], (ch, cc))
    for k in range(FFN_CONV_WIDTH):
        o = SUBLANE - (FFN_CONV_WIDTH - 1) + k + r0
        acc = acc + w_ref[k:k + 1, c0:c0 + cc] * pad_ref[o:o + ch, c0:c0 + cc]
    return acc


def _ffn_gate_fwd(u, fcw, fcb, *, name, rider=None):
    s = u.shape[0]
    t = _seq_tile(s, FFN_TILE)
    ch = min(FFN_ROW_CHUNK, t)
    cc = FFN_COL_CHUNK

    def body(u_ref, w_ref, b_ref, o_ref, uc_ref, upad):
        @pl.when(pl.program_id(0) == 0)
        def _():
            upad[0:SUBLANE, :] = jnp.zeros((SUBLANE, 2 * D_FF), F32)

        upad[SUBLANE:SUBLANE + t, :] = u_ref[...].astype(F32)
        for c0 in range(0, D_FF, cc):
            for r0 in range(0, t, ch):
                val = _ffn_conv(upad, w_ref, b_ref, r0, ch, c0)
                gt = _ffn_conv(upad, w_ref, b_ref, r0, ch, c0 + D_FF)
                o_ref[r0:r0 + ch, c0:c0 + cc] = (val * _gelu(gt)).astype(BF16)
                uc_ref[r0:r0 + ch, c0:c0 + cc] = val.astype(BF16)
                uc_ref[r0:r0 + ch, c0 + D_FF:c0 + D_FF + cc] = gt.astype(BF16)
        upad[0:SUBLANE, :] = upad[t:t + SUBLANE, :]

    return _call(
        body, grid=(s // t,), ins=[u, fcw, fcb],
        in_specs=[pl.BlockSpec((t, 2 * D_FF), lambda i: (i, 0)), _full_spec(fcw), _full_spec(fcb)],
        out_specs=[pl.BlockSpec((t, D_FF), lambda i: (i, 0)), pl.BlockSpec((t, 2 * D_FF), lambda i: (i, 0))],
        outs=[jax.ShapeDtypeStruct((s, D_FF), BF16), jax.ShapeDtypeStruct((s, 2 * D_FF), BF16)],
        scratch=[pltpu.VMEM((SUBLANE + t, 2 * D_FF), F32)], name=name, rider=rider)


def _ffn_gate_bwd(u, uc, dh, fcw, *, name, rider=None):
    s = u.shape[0]
    t = _seq_tile(s, FFN_TILE)
    nt = s // t
    ch = min(FFN_ROW_CHUNK, t)
    cc = FFN_COL_CHUNK

    def body(u_ref, uc_ref, dh_ref, w_ref, du_ref, dw_ref, db_ref, dpad):
        @pl.when(pl.program_id(0) == 0)
        def _():
            dpad[t:t + SUBLANE, :] = jnp.zeros((SUBLANE, 2 * D_FF), F32)
            dw_ref[...] = jnp.zeros_like(dw_ref)
            db_ref[...] = jnp.zeros_like(db_ref)

        for c0 in range(0, D_FF, cc):
            for r0 in range(0, t, ch):
                val = uc_ref[r0:r0 + ch, c0:c0 + cc].astype(F32)
                gt = uc_ref[r0:r0 + ch, c0 + D_FF:c0 + D_FF + cc].astype(F32)
                gl, dgl = _gelu_and_grad(gt)
                d = dh_ref[r0:r0 + ch, c0:c0 + cc].astype(F32)
                dpad[r0:r0 + ch, c0:c0 + cc] = d * gl
                dpad[r0:r0 + ch, c0 + D_FF:c0 + D_FF + cc] = d * val * dgl
        for c0 in range(0, 2 * D_FF, cc):
            dbs = jnp.zeros((1, cc), F32)
            dws = [jnp.zeros((1, cc), F32) for _ in range(FFN_CONV_WIDTH)]
            for r0 in range(0, t, ch):
                x = u_ref[r0:r0 + ch, c0:c0 + cc].astype(F32)
                acc = jnp.zeros((ch, cc), F32)
                for k in range(FFN_CONV_WIDTH):
                    o = (FFN_CONV_WIDTH - 1) - k + r0
                    sh = dpad[o:o + ch, c0:c0 + cc]
                    acc = acc + w_ref[k:k + 1, c0:c0 + cc] * sh
                    dws[k] = dws[k] + _colsum(x * sh)
                    if k == FFN_CONV_WIDTH - 1:
                        dbs = dbs + _colsum(sh)
                du_ref[r0:r0 + ch, c0:c0 + cc] = acc.astype(BF16)
            db_ref[:, c0:c0 + cc] += dbs
            for k in range(FFN_CONV_WIDTH):
                dw_ref[k:k + 1, c0:c0 + cc] += dws[k]
        dpad[t:t + SUBLANE, :] = dpad[0:SUBLANE, :]

    outs = [jax.ShapeDtypeStruct((s, 2 * D_FF), BF16), jax.ShapeDtypeStruct((SUBLANE, 2 * D_FF), F32),
            jax.ShapeDtypeStruct((1, 2 * D_FF), F32)]
    return _call(
        body, grid=(nt,), ins=[u, uc, dh, fcw],
        in_specs=[pl.BlockSpec((t, 2 * D_FF), lambda i: (nt - 1 - i, 0)),
                  pl.BlockSpec((t, 2 * D_FF), lambda i: (nt - 1 - i, 0)),
                  pl.BlockSpec((t, D_FF), lambda i: (nt - 1 - i, 0)), _full_spec(fcw)],
        out_specs=[pl.BlockSpec((t, 2 * D_FF), lambda i: (nt - 1 - i, 0)), _full_spec(outs[1]), _full_spec(outs[2])],
        outs=outs, scratch=[pltpu.VMEM((t + SUBLANE, 2 * D_FF), F32)], name=name, rider=rider)


def _adamw_body(g_ref, w_ref, m_ref, v_ref, go_ref, d_ref, mo_ref, vo_ref):
    inv_b1 = 1.0 - ADAM_B1 ** ADAM_STEP
    inv_b2 = 1.0 - ADAM_B2 ** ADAM_STEP
    g = g_ref[0].astype(F32)
    for dev in range(1, N_DEV):
        g = g + g_ref[dev].astype(F32)
    go_ref[...] = g
    mn = ADAM_B1 * m_ref[...] + (1.0 - ADAM_B1) * g
    vn = ADAM_B2 * v_ref[...] + (1.0 - ADAM_B2) * (g * g)
    mo_ref[...] = mn
    vo_ref[...] = vn
    d_ref[...] = -ADAM_LR * ((mn / inv_b1) / (jnp.sqrt(vn / inv_b2) + ADAM_EPS) + ADAM_WD * w_ref[...])


def _adamw(gstack, w, m, v, *, name):
    _, r, c = gstack.shape
    tr = _pick_rows(r, PACK_ROW_BLOCK)

    def body(*refs):
        _adamw_body(*refs)

    blk = pl.BlockSpec((tr, c), lambda i: (i, 0))
    sh = jax.ShapeDtypeStruct((r, c), F32)
    return pl.pallas_call(
        body, grid=(r // tr,),
        in_specs=[pl.BlockSpec((N_DEV, tr, c), lambda i: (0, i, 0)), blk, blk, blk],
        out_specs=[blk] * 4, out_shape=[sh] * 4,
        compiler_params=_cparams(1), name=name)(gstack, w, m, v)


def _adamw_layer(gstack, w, m, v, layer, into, *, name):
    n_layers, r, c = w.shape
    tr = _pick_rows(r, PACK_ROW_BLOCK)

    def body(g_ref, w_ref, m_ref, v_ref, *rest):
        _adamw_body(g_ref, w_ref, m_ref, v_ref, *rest[-4:])

    blk = pl.BlockSpec((None, tr, c), lambda i: (layer, i, 0))
    sh = jax.ShapeDtypeStruct((n_layers, r, c), F32)
    into = list(into or [])
    return pl.pallas_call(
        body, grid=(r // tr,),
        in_specs=[pl.BlockSpec((N_DEV, tr, c), lambda i: (0, i, 0)), blk, blk, blk] + [_ANY] * len(into),
        out_specs=[blk] * 4, out_shape=[sh] * 4, input_output_aliases={4 + k: k for k in range(len(into))},
        compiler_params=_cparams(1), name=name)(gstack, w, m, v, *into)


def _exchange(rider, *, name):
    n = rider.n

    def body(*refs):
        x_refs, out_refs, sems = refs[:n], refs[n:2 * n], refs[2 * n:]
        rider.start(x_refs, out_refs, sems)
        rider.wait(x_refs, out_refs, sems)

    return pl.pallas_call(
        body, in_specs=[_ANY] * n, out_specs=[_ANY] * n, out_shape=rider.out_shapes(),
        scratch_shapes=rider.scratch(), name=name)(*rider.srcs)


def _pack_rows(n):
    rows = -(-n // PACK_COLS)
    return -(-rows // SUBLANE) * SUBLANE


def _pack(arrs, dtype):
    flat = jnp.concatenate([a.reshape(-1).astype(dtype) for a in arrs])
    rows = _pack_rows(flat.shape[0])
    flat = jnp.pad(flat, (0, rows * PACK_COLS - flat.shape[0]))
    return flat.reshape(rows, PACK_COLS)


def _pack_lead(arrs, dtype):
    flat = jnp.concatenate([a.reshape(N_DEV, -1).astype(dtype) for a in arrs], axis=1)
    rows = _pack_rows(flat.shape[1])
    flat = jnp.pad(flat, ((0, 0), (0, rows * PACK_COLS - flat.shape[1])))
    return flat.reshape(N_DEV, rows, PACK_COLS)


def _pack_layers(arrs, dtype):
    n_layers = arrs[0].shape[0]
    flat = jnp.concatenate([a.reshape(n_layers, -1).astype(dtype) for a in arrs], axis=1)
    rows = _pack_rows(flat.shape[1])
    flat = jnp.pad(flat, ((0, 0), (0, rows * PACK_COLS - flat.shape[1])))
    return flat.reshape(n_layers, rows, PACK_COLS)


def _unpack_layers(packed, shapes):
    flat = packed.reshape(packed.shape[0], -1)
    out, pos = [], 0
    for sh in shapes:
        n = math.prod(sh[1:])
        out.append(flat[:, pos:pos + n].reshape(sh))
        pos += n
    return out


def _unpack(packed, shapes, lead=False):
    flat = packed.reshape(N_DEV, -1) if lead else packed.reshape(-1)
    out, pos = [], 0
    for sh in shapes:
        n = math.prod(sh)
        out.append(flat[:, pos:pos + n].reshape((N_DEV,) + tuple(sh)) if lead else flat[pos:pos + n].reshape(sh))
        pos += n
    return out


def _join_shards(stacked, axis):
    return jnp.concatenate([stacked[d] for d in range(N_DEV)], axis=axis)


def _split_shards(full, axis):
    return jnp.stack(jnp.split(full, N_DEV, axis=axis), axis=0)


def _perm_in_cols(a, inverse=False):
    blocks = jnp.split(a, 6, axis=-1)
    if inverse:
        order = [IN_PERM.index(j) for j in range(6)]
    else:
        order = list(IN_PERM)
    return jnp.concatenate([blocks[j] for j in order], axis=-1)


def _row(v):
    return v.reshape(1, -1)


def _pad_rows(w, rows):
    return jnp.pad(w, ((0, rows - w.shape[0]), (0, 0)))


def _gn_avg_matrix():
    return _block_mask(GN_GROUPS, D_GROUP // GN_GROUPS, D_GROUP // GN_GROUPS) / (D_GROUP // GN_GROUPS)


def _layer_params(p, l):
    q = {}
    s5_mats, q["s5_vjp"] = jax.vjp(_s5_chunk_map, p["s5_lam_re"][l], p["s5_lam_im"][l], p["s5_log_dt"][l],
                                   p["s5_b_re"][l], p["s5_b_im"][l], p["s5_c_re"][l], p["s5_c_im"][l], p["s5_d"][l])
    q["s5_mats"] = [m.astype(BF16) for m in s5_mats[:5]]
    q["s5_a16"] = s5_mats[5]
    (q["wr"], q["wi"]), q["lru_w_vjp"] = jax.vjp(lambda r, i: (_blockdiag(r), _blockdiag(i)), p["lru_w_r"][l], p["lru_w_i"][l])
    q["wr"], q["wi"] = q["wr"].astype(BF16), q["wi"].astype(BF16)
    q["sp"], q["sp_vjp"] = jax.vjp(lambda lam: _row(jax.nn.softplus(-lam)), p["lru_lam"][l])
    return q


WEIGHT_RIDES = {(0, "ln_in_fwd"): [("w_in", 0)],
                (0, "inproj"): [("attn_w_kv", 0), ("w_out", 0), ("small_pack", 0)],
                (0, "cv_fwd"): [("ffn_w_up#a", 0)],
                (0, "outproj"): [("ffn_w_up#b", 0)],
                (0, "ffn_up"): [("ffn_w_down", 0), ("w_in", 1), ("attn_w_kv", 1), ("w_out", 1)],
                (0, "ffn_gate_fwd"): [("ffn_w_up", 1)],
                (0, "ffn_down"): [("ffn_w_down", 1)]}
GRAD_RIDES = {(1, "ffn_gate_bwd"): [("ffn_w_down", 1)],
              (0, "dw_down"): [("w_out", 1), ("attn_w_kv", 1), ("w_in", 1)],
              (0, "dhff"): [("rep", 1), ("ssh", 1)],
              (0, "ffn_gate_bwd"): [("ffn_w_up", 1)],
              (0, "dw_up"): [("ffn_w_down", 0)],
              (0, "dx1"): [("ffn_w_up", 0)],
              (0, "cv_bwd"): [("w_out", 0)],
              (0, "dw_in"): [("attn_w_kv", 0), ("ssh", 0)],
              (0, "dxs"): [("w_in", 0)],
              (0, "ln_in_bwd"): [("rep", 0)]}


def _assemble_weight(n, gathered):
    if SHARDED[n] == 2:
        full = jnp.transpose(gathered, (1, 0, 2)).reshape(gathered.shape[1], -1)
        return _perm_in_cols(full) if n == "w_in" else full
    return gathered.reshape(-1, gathered.shape[-1])


def _grad_source(n, g):
    g = g.astype(BF16)
    if SHARDED[n] == 2:
        if n == "w_in":
            g = _perm_in_cols(g, inverse=True)
        k, nn = g.shape
        return jnp.transpose(g.reshape(k, N_DEV, nn // N_DEV), (1, 0, 2)), "lead"
    return g, "rows"


def _hosted(fn, keys_rider, land, *args, **kw):
    keys, rider = keys_rider
    if rider is None:
        return fn(*args, **kw)
    out, routs = fn(*args, rider=rider, **kw)
    land(keys, routs)
    return out


def _local_step(x, mem, target, p, big_w, shards=None, unpack_small=None):
    dist = shards is not None
    gdt = BF16 if dist else F32
    small, saved = {}, []
    big_g, ready, recv = {}, {}, {}
    mavg = _gn_avg_matrix()
    s5_perm = _s5_perm()

    def weight_rider(l, host):
        keys = WEIGHT_RIDES.get((l, host), []) if dist else []
        return keys, (_Rider([shards[n][ll] for n, ll in keys], ["all"] * len(keys)) if keys else None)

    halves = {}

    def land_weights(keys, routs):
        for (n, ll), r in zip(keys, routs):
            if n == "small_pack":
                p.update(unpack_small(r))
            elif "#" in n:
                base = n.split("#")[0]
                halves[(n, ll)] = r
                if (base + "#a", ll) in halves and (base + "#b", ll) in halves:
                    whole = jnp.concatenate([halves[(base + "#a", ll)], halves[(base + "#b", ll)]], axis=1)
                    big_w[base][ll] = _assemble_weight(base, whole)
            else:
                big_w[n][ll] = _assemble_weight(n, r)

    def grad_rider(l, host):
        keys = [k for k in GRAD_RIDES.get((l, host), []) if k in ready] if dist else []
        return keys, (_Rider([ready[k][0] for k in keys], [ready[k][1] for k in keys]) if keys else None)

    def land_grads(keys, routs):
        for k, r in zip(keys, routs):
            recv[k] = r
            del ready[k]

    def big_grad(n, l, g):
        if dist:
            ready[(n, l)] = _grad_source(n, g)
        else:
            big_g[(n, l)] = g

    xs = _hosted(_ln_fwd, weight_rider(0, "ln_in_fwd"), land_weights, x, _row(p["ln_in_g"]), _row(p["ln_in_b"]),
                 name="ln_in_fwd")
    for l in range(DEPTH):
        q = _layer_params(p, l)
        n = f"l{l}_"
        hin = _hosted(_mm, weight_rider(l, "inproj"), land_weights, xs, big_w["w_in"][l], bias=_row(p["b_in"][l]),
                      name=n + "inproj")
        nb = hin.shape[0] // S5_CHUNK
        s5_pows = _s5_a16_powers(q["s5_a16"], nb.bit_length() - 1)
        s5_u2 = _s5_to_chunks(hin, COL_S5 * (D_GROUP // LANE), s5_perm, name=n + "s5_in")
        s5_y2, s5_x = _s5_core_fwd(s5_u2, *q["s5_mats"], s5_pows, name=n + "s5_core_fwd")
        s5_y1 = _s5_from_chunks(s5_y2, s5_perm, name=n + "s5_out")
        (mix,), _ = _s5_glu_fwd(s5_y1, p["s5_w_glu"][l], _row(p["s5_b_glu"][l]), name=n + "s5_glu_fwd")
        cvw = _pad_rows(p["cv_w"][l], CV_PAD)
        keys, rd = weight_rider(l, "cv_fwd")
        (mix, cv_c), routs = _cv_fwd(hin, cvw, _row(p["cv_b"][l]), _row(p["cv_gn_g"][l]), _row(p["cv_gn_b"][l]), mavg,
                                     p["cv_w_pw"][l], _row(p["cv_b_pw"][l]), mix, name=n + "cv_fwd", rider=rd)
        land_weights(keys, routs)
        lcw = _pad_rows(p["lru_conv_w"][l], SUBLANE)
        mix, lru_xc, lru_h = _lru_fwd(hin, lcw, _row(p["lru_conv_b"][l]), q["wr"], _row(p["lru_b_r"][l]), q["wi"],
                                      _row(p["lru_b_i"][l]), q["sp"], mix, name=n + "lru_fwd")
        kv = _mm(mem, big_w["attn_w_kv"][l], name=n + "kv")
        (kbig, vbig), kv_vjp = jax.vjp(_attn_big, kv)
        kbig, vbig = kbig.astype(BF16), vbig.astype(BF16)
        mix = _attn_fwd(hin, kbig, vbig, mix, name=n + "attn_fwd")
        r1, x1 = _hosted(_mm, weight_rider(l, "outproj"), land_weights, mix, big_w["w_out"][l], bias=_row(p["b_out"][l]),
                         res=xs, res_scale=ALPHA, ln=(_row(p["ln1_g"][l]), _row(p["ln1_b"][l])), name=n + "outproj")
        u = _hosted(_mm, weight_rider(l, "ffn_up"), land_weights, x1, big_w["ffn_w_up"][l], out_dtype=BF16,
                    name=n + "ffn_up")
        fcw = _pad_rows(p["ffn_conv_w"][l], SUBLANE)
        fcb = _row(p["ffn_conv_b"][l])
        keys, rd = weight_rider(l, "ffn_gate_fwd")
        (hff, uc), routs = _ffn_gate_fwd(u, fcw, fcb, name=n + "ffn_gate_fwd", rider=rd)
        land_weights(keys, routs)
        if l < DEPTH - 1:
            r2, x2 = _hosted(_mm, weight_rider(l, "ffn_down"), land_weights, hff, big_w["ffn_w_down"][l], res=x1,
                             res_scale=ALPHA, ln=(_row(p["ln2_g"][l]), _row(p["ln2_b"][l])), name=n + "ffn_down")
        else:
            r2, x2 = _mm(hff, big_w["ffn_w_down"][l], res=x1, res_scale=ALPHA, name=n + "ffn_down"), None
        saved.append(dict(q=q, xs=xs, hin=hin, s5_y1=s5_y1, s5_u2=s5_u2, s5_x=s5_x, s5_pows=s5_pows, cvw=cvw, cv_c=cv_c, lcw=lcw, lru_xc=lru_xc,
                          lru_h=lru_h, kbig=kbig, vbig=vbig, kv_vjp=kv_vjp, mix=mix, r1=r1, x1=x1, u=u, uc=uc, fcw=fcw,
                          hff=hff, r2=r2))
        xs = x2

    top = DEPTH - 1
    dr_top, dg_top, db_top, loss_blk = _loss_ln_bwd(saved[top]["r2"], _row(p["ln2_g"][top]), _row(p["ln2_b"][top]), target,
                                                     name="loss_ln_bwd")
    loss = loss_blk[0, 0]
    dx = None

    for l in reversed(range(DEPTH)):
        sv = saved[l]
        q = sv["q"]
        n = f"l{l}_"
        g = {}
        if l == top:
            dr2, g["ln2_g"], g["ln2_b"] = dr_top, dg_top, db_top
        else:
            dr2, g["ln2_g"], g["ln2_b"], _ = _ln_bwd(sv["r2"], dx, _row(p["ln2_g"][l]), name=n + "ln2_bwd")
        big_grad("ffn_w_down", l, _hosted(_mm_tn, grad_rider(l, "dw_down"), land_grads, sv["hff"], dr2, out_dtype=gdt,
                                          name=n + "dw_down"))
        dhff = _hosted(_mm, grad_rider(l, "dhff"), land_grads, dr2, big_w["ffn_w_down"][l], trans_b=True,
                       out_dtype=BF16, name=n + "dhff")
        keys, rd = grad_rider(l, "ffn_gate_bwd")
        (du, dfw, g["ffn_conv_b"]), routs = _ffn_gate_bwd(sv["u"], sv["uc"], dhff, sv["fcw"], name=n + "ffn_gate_bwd",
                                                          rider=rd)
        land_grads(keys, routs)
        g["ffn_conv_w"] = dfw[:FFN_CONV_WIDTH]
        big_grad("ffn_w_up", l, _hosted(_mm_tn, grad_rider(l, "dw_up"), land_grads, sv["x1"], du, out_dtype=gdt,
                                        name=n + "dw_up"))
        dx1 = _hosted(_mm, grad_rider(l, "dx1"), land_grads, du, big_w["ffn_w_up"][l], trans_b=True, res=dr2,
                      res_scale=ALPHA, name=n + "dx1")
        dr1, g["ln1_g"], g["ln1_b"], g["b_out"] = _ln_bwd(sv["r1"], dx1, _row(p["ln1_g"][l]), name=n + "ln1_bwd")
        big_grad("w_out", l, _mm_tn(sv["mix"], dr1, out_dtype=gdt, name=n + "dw_out"))
        dmix = _mm(dr1, big_w["w_out"][l], trans_b=True, name=n + "dmix")

        hin = sv["hin"]
        keys, rd = grad_rider(l, "cv_bwd")
        (dh, g["cv_w_pw"], dcw, g["cv_b_pw"], g["cv_gn_g"], g["cv_gn_b"], g["cv_b"]), routs = _cv_bwd(
            hin, sv["cv_c"], dmix, sv["cvw"], _row(p["cv_gn_g"][l]), _row(p["cv_gn_b"][l]), mavg, p["cv_w_pw"][l],
            name=n + "cv_bwd", rider=rd)
        land_grads(keys, routs)
        g["cv_w"] = dcw[:CONV_WIDTH]
        dh, dwr, dwi, dlcw, g["lru_b_r"], g["lru_b_i"], dsp, g["lru_conv_b"] = _lru_bwd(
            hin, sv["lru_xc"], sv["lru_h"], dmix, sv["lcw"], q["wr"], _row(p["lru_b_r"][l]), q["wi"],
            _row(p["lru_b_i"][l]), q["sp"], dh, name=n + "lru_bwd")
        g["lru_conv_w"] = dlcw[:LRU_CONV_WIDTH]
        g["lru_w_r"], g["lru_w_i"] = q["lru_w_vjp"]((dwr, dwi))
        (g["lru_lam"],) = q["sp_vjp"](dsp)
        dy1, g["s5_w_glu"], g["s5_b_glu"] = _s5_glu_bwd(sv["s5_y1"], dmix, p["s5_w_glu"][l], _row(p["s5_b_glu"][l]),
                                                        name=n + "s5_glu_bwd")
        s5_du2, *s5_dmats = _s5_core_bwd(sv["s5_u2"], _s5_to_chunks(dy1, 0, s5_perm, name=n + "s5_din"), sv["s5_x"],
                                         *q["s5_mats"], sv["s5_pows"], name=n + "s5_core_bwd")
        (g["s5_lam_re"], g["s5_lam_im"], g["s5_log_dt"], g["s5_b_re"], g["s5_b_im"], g["s5_c_re"], g["s5_c_im"],
         g["s5_d"]) = q["s5_vjp"](tuple(s5_dmats))
        dh, dkbig, dvbig = _attn_bwd(hin, dmix, sv["kbig"], sv["vbig"],
                                     _s5_from_chunks(s5_du2, s5_perm, name=n + "s5_dout"), dh, name=n + "attn_bwd")
        (dkv,) = sv["kv_vjp"]((dkbig, dvbig))
        big_grad("attn_w_kv", l, _mm_tn(mem, dkv, out_dtype=gdt, name=n + "dw_kv"))

        if dist:
            ready[("ssh", l)] = (_pack_lead([_split_shards(g[k], SHARDED[k] - 1) for k in SMALL_SHARDED], F32), "lead")
        gw_in, g["b_in"] = _hosted(_mm_tn, grad_rider(l, "dw_in"), land_grads, sv["xs"], dh, colsum=True, out_dtype=gdt,
                                   name=n + "dw_in")
        big_grad("w_in", l, gw_in)
        if dist:
            g["b_in"] = _perm_in_cols(g["b_in"], inverse=True)
            ready[("rep", l)] = (_pack([g[k] for k in REP_LAYERED], F32), "all")
        else:
            for k, v in g.items():
                small.setdefault(k, [None] * DEPTH)[l] = v.reshape(p[k].shape[1:])
        dx = _hosted(_mm, grad_rider(l, "dxs"), land_grads, dh, big_w["w_in"][l], trans_b=True, res=dr1,
                     res_scale=ALPHA, name=n + "dxs")

    keys, rd = grad_rider(0, "ln_in_bwd")
    if rd is None:
        grad_x, dgi, dbi, _ = _ln_bwd(x, dx, _row(p["ln_in_g"]), name="ln_in_bwd")
    else:
        (grad_x, dgi, dbi, _), routs = _ln_bwd(x, dx, _row(p["ln_in_g"]), name="ln_in_bwd", rider=rd)
        land_grads(keys, routs)
    out = {k: jnp.stack(v, axis=0) for k, v in small.items()}
    out["ln_in_g"], out["ln_in_b"] = dgi.reshape(-1), dbi.reshape(-1)
    return loss, grad_x, out, ((recv, ready) if dist else big_g)


def kernel(x, mem, ln_in_g, ln_in_b, w_in, b_in, s5_lam_re, s5_lam_im, s5_log_dt, s5_b_re, s5_b_im, s5_c_re, s5_c_im, s5_d, s5_w_glu, s5_b_glu, cv_w, cv_b, cv_gn_g, cv_gn_b, cv_w_pw, cv_b_pw, lru_conv_w, lru_conv_b, lru_w_r, lru_b_r, lru_w_i, lru_b_i, lru_lam, attn_w_kv, w_out, b_out, ln1_g, ln1_b, ffn_w_up, ffn_conv_w, ffn_conv_b, ffn_w_down, ln2_g, ln2_b, loss_target, m_ln_in_g, m_ln_in_b, m_w_in, m_b_in, m_s5_lam_re, m_s5_lam_im, m_s5_log_dt, m_s5_b_re, m_s5_b_im, m_s5_c_re, m_s5_c_im, m_s5_d, m_s5_w_glu, m_s5_b_glu, m_cv_w, m_cv_b, m_cv_gn_g, m_cv_gn_b, m_cv_w_pw, m_cv_b_pw, m_lru_conv_w, m_lru_conv_b, m_lru_w_r, m_lru_b_r, m_lru_w_i, m_lru_b_i, m_lru_lam, m_attn_w_kv, m_w_out, m_b_out, m_ln1_g, m_ln1_b, m_ffn_w_up, m_ffn_conv_w, m_ffn_conv_b, m_ffn_w_down, m_ln2_g, m_ln2_b, v_ln_in_g, v_ln_in_b, v_w_in, v_b_in, v_s5_lam_re, v_s5_lam_im, v_s5_log_dt, v_s5_b_re, v_s5_b_im, v_s5_c_re, v_s5_c_im, v_s5_d, v_s5_w_glu, v_s5_b_glu, v_cv_w, v_cv_b, v_cv_gn_g, v_cv_gn_b, v_cv_w_pw, v_cv_b_pw, v_lru_conv_w, v_lru_conv_b, v_lru_w_r, v_lru_b_r, v_lru_w_i, v_lru_b_i, v_lru_lam, v_attn_w_kv, v_w_out, v_b_out, v_ln1_g, v_ln1_b, v_ffn_w_up, v_ffn_conv_w, v_ffn_conv_b, v_ffn_w_down, v_ln2_g, v_ln2_b):
    args = locals()
    w = {n: args[n] for n in WEIGHTS}
    mom = {n: args["m_" + n] for n in WEIGHTS}
    var = {n: args["v_" + n] for n in WEIGHTS}

    shards = {n: w[n].astype(BF16) for n in BIG}
    half_rows = shards["ffn_w_up"].shape[1] // 2
    shards["ffn_w_up#a"] = [shards["ffn_w_up"][0, :half_rows]]
    shards["ffn_w_up#b"] = [shards["ffn_w_up"][0, half_rows:]]
    shards["small_pack"] = [_pack([w[n] for n in SMALL_SHARDED], F32)]
    small_shapes = [w[n].shape for n in SMALL_SHARDED]

    def unpack_small(gathered):
        out = {n: _join_shards(st, SHARDED[n]) for n, st in zip(SMALL_SHARDED, _unpack(gathered, small_shapes, lead=True))}
        for n in ("s5_w_glu", "cv_w_pw"):
            out[n] = out[n].astype(BF16)
        return out

    big_w = {n: [None] * DEPTH for n in BIG}
    p = {n: w[n] for n in REPLICATED}
    p["b_in"] = _perm_in_cols(p["b_in"])

    loss, grad_x, g_small, (recv, ready) = _local_step(x[0], mem[0], loss_target[0], p, big_w, shards, unpack_small)
    loss = lax.psum(loss, ("x", "y", "c"))

    left = list(ready)
    rider = _Rider([ready[k][0] for k in left] + [_pack([g_small["ln_in_g"], g_small["ln_in_b"]], F32)],
                   [ready[k][1] for k in left] + ["all"])
    got = _exchange(rider, name="exchange_grads")
    for k, r in zip(left, got):
        recv[k] = r

    res = [dict(), dict(), dict(), dict()]
    for n in BIG:
        outs = None
        for l in range(DEPTH):
            outs = _adamw_layer(recv[(n, l)], w[n], mom[n], var[n], l, outs, name=f"adamw_{n}_l{l}")
        for kind in range(4):
            res[kind][n] = outs[kind]
    for names, key, tag in ((SMALL_SHARDED, "ssh", "adamw_small_sharded"), (REP_LAYERED, "rep", "adamw_replicated")):
        gstack = jnp.concatenate([recv[(key, l)] for l in range(DEPTH)], axis=1)
        packs = [_pack_layers([t[n] for n in names], F32) for t in (w, mom, var)]
        rows = packs[0].shape[1]
        outs = _adamw(gstack, *[pk.reshape(DEPTH * rows, PACK_COLS) for pk in packs], name=tag)
        for kind in range(4):
            for n, a in zip(names, _unpack_layers(outs[kind].reshape(DEPTH, rows, PACK_COLS), [w[n].shape for n in names])):
                res[kind][n] = a
    ln_names = ("ln_in_g", "ln_in_b")
    outs = _adamw(got[len(left)], _pack([w[n] for n in ln_names], F32), _pack([mom[n] for n in ln_names], F32),
                  _pack([var[n] for n in ln_names], F32), name="adamw_ln_in")
    for kind in range(4):
        for n, a in zip(ln_names, _unpack(outs[kind], [w[n].shape for n in ln_names])):
            res[kind][n] = a
    return (loss, grad_x[None], *[res[0][n] for n in WEIGHTS], *[res[1][n] for n in WEIGHTS],
            *[res[2][n] for n in WEIGHTS], *[res[3][n] for n in WEIGHTS])
```

```python
import math

import jax
import jax.numpy as jnp
from jax import lax
from jax.experimental import pallas as pl
from jax.experimental.pallas import tpu as pltpu

F32 = jnp.float32
BF16 = jnp.bfloat16

D_MODEL = 1024
DEPTH = 2
D_GROUP = 256
N_IN_COLS = 6 * D_GROUP
S5_GROUPS = 16
S5_CH = 16
S5_STATE = 64
S5_LANES = S5_GROUPS * S5_STATE
CONV_WIDTH = 31
GN_GROUPS = 4
LRU_HEADS = 4
LRU_CONV_WIDTH = 4
LRU_C = 8.0
ATTN_HEADS = 4
ATTN_HEAD_DIM = 64
D_FF = 2816
FFN_CONV_WIDTH = 3
ALPHA = (2 * DEPTH) ** 0.25
LN_EPS = 1e-5
ADAM_LR, ADAM_B1, ADAM_B2, ADAM_EPS, ADAM_WD, ADAM_STEP = 0.001, 0.9, 0.999, 1e-08, 0.01, 10

N_DEV = 8
N_PEERS = N_DEV - 1
LANE = 128
SUBLANE = 8
VMEM_LIMIT = 56 * 1024 * 1024
PACK_COLS = 1024
PACK_ROW_BLOCK = 256

SHARDED = {
    "w_in": 2, "s5_w_glu": 1, "cv_w": 2, "cv_w_pw": 1, "lru_conv_w": 2, "attn_w_kv": 1,
    "w_out": 1, "ffn_w_up": 2, "ffn_conv_w": 2, "ffn_w_down": 1,
}
BIG = ("w_in", "attn_w_kv", "w_out", "ffn_w_up", "ffn_w_down")
SMALL_SHARDED = ("s5_w_glu", "cv_w", "cv_w_pw", "lru_conv_w", "ffn_conv_w")
MATMUL_WEIGHTS = ("w_in", "s5_w_glu", "cv_w_pw", "attn_w_kv", "w_out", "ffn_w_up", "ffn_w_down")
WEIGHTS = ['ln_in_g', 'ln_in_b', 'w_in', 'b_in', 's5_lam_re', 's5_lam_im', 's5_log_dt', 's5_b_re', 's5_b_im',
           's5_c_re', 's5_c_im', 's5_d', 's5_w_glu', 's5_b_glu', 'cv_w', 'cv_b', 'cv_gn_g', 'cv_gn_b', 'cv_w_pw',
           'cv_b_pw', 'lru_conv_w', 'lru_conv_b', 'lru_w_r', 'lru_b_r', 'lru_w_i', 'lru_b_i', 'lru_lam',
           'attn_w_kv', 'w_out', 'b_out', 'ln1_g', 'ln1_b', 'ffn_w_up', 'ffn_conv_w', 'ffn_conv_b', 'ffn_w_down',
           'ln2_g', 'ln2_b']
REPLICATED = [n for n in WEIGHTS if n not in SHARDED]
REP_LAYERED = [n for n in REPLICATED if n not in ("ln_in_g", "ln_in_b")]

COL_CV_V, COL_CV_G, COL_LRU_G, COL_LRU_X, COL_S5, COL_Q = range(6)
IN_PERM = (1, 2, 3, 4, 0, 5)
MIX_S5, MIX_CV, MIX_LRU, MIX_ATTN = range(4)


_ANY = pl.BlockSpec(memory_space=pl.ANY)
_MESH = pl.DeviceIdType.MESH


def _cparams(n_axes):
    return pltpu.CompilerParams(dimension_semantics=("arbitrary",) * n_axes, vmem_limit_bytes=VMEM_LIMIT)


def _pick(n, cap):
    if n <= cap:
        return n
    best = None
    for t in range(LANE, cap + 1, LANE):
        if n % t == 0:
            best = t
    assert best is not None, (n, cap)
    return best


def _pick_rows(n, cap):
    best = None
    for t in range(SUBLANE, min(n, cap) + 1, SUBLANE):
        if n % t == 0:
            best = t
    assert best is not None, (n, cap)
    return best


def _full_spec(arr):
    nd = arr.ndim
    return pl.BlockSpec(arr.shape, lambda *_: (0,) * nd)


def _dot(a, b):
    return lax.dot_general(a.astype(BF16), b.astype(BF16), (((1,), (0,)), ((), ())), preferred_element_type=F32)


def _dot_nt(a, b):
    return lax.dot_general(a.astype(BF16), b.astype(BF16), (((1,), (1,)), ((), ())), preferred_element_type=F32)


def _dot_tn(a, b):
    return lax.dot_general(a.astype(BF16), b.astype(BF16), (((0,), (0,)), ((), ())), preferred_element_type=F32)


def _dot_hi(a, b):
    return jnp.dot(a, b, precision=lax.Precision.HIGHEST, preferred_element_type=F32)


def _colsum(x):
    return jnp.sum(x, axis=0, keepdims=True)


def _sigmoid(x):
    return 1.0 / (1.0 + jnp.exp(-x))


_GELU_K = math.sqrt(2.0 / math.pi)
_GELU_C = 0.044715


def _gelu(x):
    t = jnp.tanh(_GELU_K * (x + _GELU_C * x * x * x))
    return 0.5 * x * (1.0 + t)


def _gelu_and_grad(x):
    x2 = x * x
    t = jnp.tanh(_GELU_K * (x + _GELU_C * x2 * x))
    g = 0.5 * x * (1.0 + t)
    dg = 0.5 * (1.0 + t) + 0.5 * x * (1.0 - t * t) * (_GELU_K * (1.0 + 3.0 * _GELU_C * x2))
    return g, dg


def _neg_expm1(x):
    series = x * (1.0 + x * (0.5 + x * (1.0 / 6.0 + x * (1.0 / 24.0 + x * (1.0 / 120.0)))))
    return -jnp.where(jnp.abs(x) < 0.1, series, jnp.exp(x) - 1.0)


def _seq_tile(s, want):
    t = min(s, want)
    assert s % t == 0
    return t


class _Rider:
    def __init__(self, srcs, kinds):
        self.srcs, self.kinds = list(srcs), list(kinds)
        self.n = len(self.srcs)

    def out_shapes(self):
        shapes = []
        for x, kind in zip(self.srcs, self.kinds):
            if kind == "lead":
                shp = x.shape
            elif kind == "rows":
                shp = (N_DEV, x.shape[0] // N_DEV) + x.shape[1:]
            else:
                shp = (N_DEV,) + x.shape
            shapes.append(jax.ShapeDtypeStruct(shp, x.dtype))
        return shapes

    def scratch(self):
        return [pltpu.SemaphoreType.DMA((self.n * N_PEERS,)), pltpu.SemaphoreType.DMA((self.n * N_PEERS,)),
                pltpu.SemaphoreType.DMA((self.n,))]

    def _copies(self, x_refs, out_refs, sems):
        send_sems, recv_sems, local_sems = sems
        mx, my, mc = lax.axis_index("x"), lax.axis_index("y"), lax.axis_index("c")
        my_id = 4 * mx + 2 * my + mc

        def piece(i, dev):
            if self.kinds[i] == "lead":
                return x_refs[i].at[dev]
            if self.kinds[i] == "rows":
                r = x_refs[i].shape[0] // N_DEV
                return x_refs[i].at[pl.ds(pl.multiple_of(dev * r, SUBLANE), r)]
            return x_refs[i]

        mine = [pltpu.make_async_copy(piece(i, my_id), out_refs[i].at[my_id], local_sems.at[i]) for i in range(self.n)]
        copies = []
        for k in range(1, N_DEV):
            px, py, pc = mx ^ ((k >> 2) & 1), my ^ ((k >> 1) & 1), mc ^ (k & 1)
            for i in range(self.n):
                copies.append(pltpu.make_async_remote_copy(
                    src_ref=piece(i, 4 * px + 2 * py + pc), dst_ref=out_refs[i].at[my_id],
                    send_sem=send_sems.at[i * N_PEERS + k - 1], recv_sem=recv_sems.at[i * N_PEERS + k - 1],
                    device_id=(px, py, pc), device_id_type=_MESH))
        return mine, copies

    def start(self, x_refs, out_refs, sems):
        mine, copies = self._copies(x_refs, out_refs, sems)
        for cp in mine + copies:
            cp.start()

    def wait(self, x_refs, out_refs, sems):
        mine, copies = self._copies(x_refs, out_refs, sems)
        for cp in copies:
            cp.wait_recv()
        for cp in copies:
            cp.wait_send()
        for cp in mine:
            cp.wait()


def _call(body, *, grid, ins, in_specs, outs, out_specs, scratch=(), aliases=None, name, rider=None):
    n_axes = len(grid)
    common = dict(grid=grid, input_output_aliases=aliases or {}, compiler_params=_cparams(n_axes), name=name)
    if rider is None:
        res = pl.pallas_call(body, in_specs=list(in_specs), out_specs=list(out_specs), out_shape=list(outs),
                             scratch_shapes=list(scratch), **common)(*ins)
        return list(res), []
    n_in, n_out, n_scr, nr = len(ins), len(outs), len(scratch), rider.n

    def wrapped(*refs):
        pos = [0]

        def take(k):
            part = refs[pos[0]:pos[0] + k]
            pos[0] += k
            return part

        a_in, r_in, a_out, r_out, a_scr, sems = take(n_in), take(nr), take(n_out), take(nr), take(n_scr), take(3)
        first = last = None
        for ax in range(n_axes):
            pid = pl.program_id(ax)
            f, l = pid == 0, pid == grid[ax] - 1
            first = f if first is None else jnp.logical_and(first, f)
            last = l if last is None else jnp.logical_and(last, l)

        @pl.when(first)
        def _():
            rider.start(r_in, r_out, sems)

        body(*a_in, *a_out, *a_scr)

        @pl.when(last)
        def _():
            rider.wait(r_in, r_out, sems)

    res = pl.pallas_call(
        wrapped, in_specs=list(in_specs) + [_ANY] * nr, out_specs=list(out_specs) + [_ANY] * nr,
        out_shape=list(outs) + rider.out_shapes(), scratch_shapes=list(scratch) + rider.scratch(), **common)(*ins, *rider.srcs)
    return list(res[:n_out]), list(res[n_out:])


def _block_mask(n_blocks, block_rows, block_cols):
    r = jnp.arange(n_blocks * block_rows) // block_rows
    c = jnp.arange(n_blocks * block_cols) // block_cols
    return (r[:, None] == c[None, :]).astype(F32)


def _mm(a, b, *, bias=None, res=None, res_scale=1.0, trans_b=False, out_dtype=F32, ln=None, name, rider=None):
    m, kdim = a.shape
    n = b.shape[0] if trans_b else b.shape[1]
    tm = _seq_tile(m, 1024)
    tn = _pick(n, 1408)
    tk = _pick(kdim, 1536)
    nk = kdim // tk
    has_bias, has_res, has_ln = bias is not None, res is not None, ln is not None
    assert not has_ln or tn == n

    def body(*refs):
        a_ref, b_ref = refs[0], refs[1]
        pos = 2
        bias_ref = res_ref = g_ref = beta_ref = x_ref = None
        if has_bias:
            bias_ref = refs[pos]
            pos += 1
        if has_res:
            res_ref = refs[pos]
            pos += 1
        if has_ln:
            g_ref, beta_ref = refs[pos], refs[pos + 1]
            pos += 2
        o_ref = refs[pos]
        pos += 1
        if has_ln:
            x_ref = refs[pos]
            pos += 1
        acc_ref = refs[pos]
        k = pl.program_id(2)

        @pl.when(k == 0)
        def _():
            acc_ref[...] = jnp.zeros_like(acc_ref)

        if trans_b:
            acc_ref[...] += _dot_nt(a_ref[...], b_ref[...])
        else:
            acc_ref[...] += _dot(a_ref[...], b_ref[...])

        @pl.when(k == nk - 1)
        def _():
            r = acc_ref[...]
            if has_bias:
                r = r + bias_ref[...]
            if has_res:
                r = r + res_scale * res_ref[...]
            o_ref[...] = r.astype(out_dtype)
            if has_ln:
                xc = r - jnp.mean(r, axis=1, keepdims=True)
                var = jnp.mean(xc * xc, axis=1, keepdims=True)
                x_ref[...] = xc * lax.rsqrt(var + LN_EPS) * g_ref[...] + beta_ref[...]

    ins = [a, b]
    in_specs = [pl.BlockSpec((tm, tk), lambda i, j, k: (i, k)),
                pl.BlockSpec((tn, tk), lambda i, j, k: (j, k)) if trans_b
                else pl.BlockSpec((tk, tn), lambda i, j, k: (k, j))]
    if has_bias:
        ins.append(bias)
        in_specs.append(pl.BlockSpec((1, tn), lambda i, j, k: (0, j)))
    if has_res:
        ins.append(res)
        in_specs.append(pl.BlockSpec((tm, tn), lambda i, j, k: (i, j)))
    if has_ln:
        ins += list(ln)
        in_specs += [pl.BlockSpec((1, tn), lambda i, j, k: (0, j))] * 2
    tile = pl.BlockSpec((tm, tn), lambda i, j, k: (i, j))
    outs, routs = _call(
        body, grid=(m // tm, n // tn, nk), ins=ins, in_specs=in_specs,
        outs=[jax.ShapeDtypeStruct((m, n), out_dtype)] + ([jax.ShapeDtypeStruct((m, n), F32)] if has_ln else []),
        out_specs=[tile] * (2 if has_ln else 1),
        scratch=[pltpu.VMEM((tm, tn), F32)], name=name, rider=rider)
    out = tuple(outs) if has_ln else outs[0]
    return out if rider is None else (out, routs)


def _mm_tn(a, b, *, colsum=False, out_dtype=F32, dev_cols=None, name, rider=None):
    s, ka = a.shape
    nb = b.shape[1]
    ts = _seq_tile(s, 512)
    tka = _pick(ka, 1408)
    tnb = _pick(nb, 1408)
    nk = s // ts
    assert not colsum or tka == ka
    per_tile = 1 if dev_cols is None else tnb // dev_cols
    assert dev_cols is None or tnb == per_tile * dev_cols

    def body(a_ref, b_ref, o_ref, *rest):
        cs_ref = rest[0] if colsum else None
        acc_ref = rest[-1]
        k = pl.program_id(2)

        @pl.when(k == 0)
        def _():
            acc_ref[...] = jnp.zeros_like(acc_ref)
            if colsum:
                cs_ref[...] = jnp.zeros_like(cs_ref)

        bv = b_ref[...]
        acc_ref[...] += _dot_tn(a_ref[...], bv)
        if colsum:
            cs_ref[...] += _colsum(bv.astype(F32))

        @pl.when(k == nk - 1)
        def _():
            if dev_cols is None:
                o_ref[...] = acc_ref[...].astype(out_dtype)
            else:
                for d in range(per_tile):
                    o_ref[d] = acc_ref[:, d * dev_cols:(d + 1) * dev_cols].astype(out_dtype)

    if dev_cols is None:
        main_shape, main_spec = (ka, nb), pl.BlockSpec((tka, tnb), lambda i, j, k: (i, j))
    else:
        main_shape = (nb // dev_cols, ka, dev_cols)
        main_spec = pl.BlockSpec((per_tile, tka, dev_cols), lambda i, j, k: (j, i, 0))
    outs, routs = _call(
        body, grid=(ka // tka, nb // tnb, nk), ins=[a, b],
        in_specs=[pl.BlockSpec((ts, tka), lambda i, j, k: (k, i)), pl.BlockSpec((ts, tnb), lambda i, j, k: (k, j))],
        outs=[jax.ShapeDtypeStruct(main_shape, out_dtype)] + ([jax.ShapeDtypeStruct((1, nb), F32)] if colsum else []),
        out_specs=[main_spec] + ([pl.BlockSpec((1, tnb), lambda i, j, k: (0, j))] if colsum else []),
        scratch=[pltpu.VMEM((tka, tnb), F32)], name=name, rider=rider)
    out = tuple(outs) if colsum else outs[0]
    return out if rider is None else (out, routs)


def _ln_fwd(r, g, b, *, name, rider=None):
    s, d = r.shape
    ts = _seq_tile(s, 512)

    def body(r_ref, g_ref, b_ref, o_ref):
        x = r_ref[...]
        mu = jnp.mean(x, axis=1, keepdims=True)
        xc = x - mu
        var = jnp.mean(xc * xc, axis=1, keepdims=True)
        o_ref[...] = xc * lax.rsqrt(var + LN_EPS) * g_ref[...] + b_ref[...]

    (out,), routs = _call(
        body, grid=(s // ts,), ins=[r, g, b],
        in_specs=[pl.BlockSpec((ts, d), lambda i: (i, 0)), _full_spec(g), _full_spec(b)],
        out_specs=[pl.BlockSpec((ts, d), lambda i: (i, 0))], outs=[jax.ShapeDtypeStruct((s, d), F32)],
        name=name, rider=rider)
    return out if rider is None else (out, routs)


def _ln_bwd(r, dy, g, *, name, rider=None):
    s, d = r.shape
    ts = _seq_tile(s, 512)

    def body(r_ref, dy_ref, g_ref, dr_ref, dg_ref, db_ref, ds_ref):
        @pl.when(pl.program_id(0) == 0)
        def _():
            dg_ref[...] = jnp.zeros_like(dg_ref)
            db_ref[...] = jnp.zeros_like(db_ref)
            ds_ref[...] = jnp.zeros_like(ds_ref)

        x = r_ref[...]
        dy = dy_ref[...]
        mu = jnp.mean(x, axis=1, keepdims=True)
        xc = x - mu
        var = jnp.mean(xc * xc, axis=1, keepdims=True)
        rstd = lax.rsqrt(var + LN_EPS)
        xh = xc * rstd
        dxh = dy * g_ref[...]
        m1 = jnp.mean(dxh, axis=1, keepdims=True)
        m2 = jnp.mean(dxh * xh, axis=1, keepdims=True)
        dr = rstd * (dxh - m1 - xh * m2)
        dr_ref[...] = dr
        dg_ref[...] += _colsum(dy * xh)
        db_ref[...] += _colsum(dy)
        ds_ref[...] += _colsum(dr)

    vec = jax.ShapeDtypeStruct((1, d), F32)
    vspec = pl.BlockSpec((1, d), lambda i: (0, 0))
    outs, routs = _call(
        body, grid=(s // ts,), ins=[r, dy, g],
        in_specs=[pl.BlockSpec((ts, d), lambda i: (i, 0)), pl.BlockSpec((ts, d), lambda i: (i, 0)), _full_spec(g)],
        out_specs=[pl.BlockSpec((ts, d), lambda i: (i, 0)), vspec, vspec, vspec],
        outs=[jax.ShapeDtypeStruct((s, d), F32), vec, vec, vec], name=name, rider=rider)
    return outs if rider is None else (outs, routs)


def _loss_ln_bwd(r, g, b, target, *, name):
    s, d = r.shape
    ts = _seq_tile(s, 512)

    def body(r_ref, g_ref, b_ref, t_ref, dr_ref, dg_ref, db_ref, l_ref):
        @pl.when(pl.program_id(0) == 0)
        def _():
            dg_ref[...] = jnp.zeros_like(dg_ref)
            db_ref[...] = jnp.zeros_like(db_ref)
            l_ref[...] = jnp.zeros_like(l_ref)

        x = r_ref[...]
        gam = g_ref[...]
        xc = x - jnp.mean(x, axis=1, keepdims=True)
        var = jnp.mean(xc * xc, axis=1, keepdims=True)
        rstd = lax.rsqrt(var + LN_EPS)
        xh = xc * rstd
        e = xh * gam + b_ref[...] - t_ref[...]
        part = jnp.sum(jnp.sum(e * e, axis=1, keepdims=True), axis=0, keepdims=True) * (0.5 / d)
        l_ref[...] += jnp.broadcast_to(part, l_ref.shape)
        dy = e * (1.0 / d)
        dxh = dy * gam
        m1 = jnp.mean(dxh, axis=1, keepdims=True)
        m2 = jnp.mean(dxh * xh, axis=1, keepdims=True)
        dr_ref[...] = rstd * (dxh - m1 - xh * m2)
        dg_ref[...] += _colsum(dy * xh)
        db_ref[...] += _colsum(dy)

    vec = jax.ShapeDtypeStruct((1, d), F32)
    vspec = pl.BlockSpec((1, d), lambda i: (0, 0))
    tile = pl.BlockSpec((ts, d), lambda i: (i, 0))
    return pl.pallas_call(
        body, grid=(s // ts,), in_specs=[tile, _full_spec(g), _full_spec(b), tile],
        out_specs=[tile, vspec, vspec, pl.BlockSpec((SUBLANE, LANE), lambda i: (0, 0))],
        out_shape=[jax.ShapeDtypeStruct((s, d), F32), vec, vec, jax.ShapeDtypeStruct((SUBLANE, LANE), F32)],
        compiler_params=_cparams(1), name=name)(r, g, b, target)


SCAN_CHUNK = 32


def _cscan_levels(bufs, apow_ref, t, pad, *, reverse):
    half = bufs[0].shape[1] // 2
    ch = min(SCAN_CHUNK, t)
    nlev = t.bit_length() - 1
    assert (1 << nlev) == t
    for k in range(nlev):
        d = 1 << k
        src, dst = bufs[k % 2], bufs[(k + 1) % 2]

        def chunk(c, carry, src=src, dst=dst, d=d, k=k):
            ar = apow_ref[k:k + 1, :half]
            ai = apow_ref[k:k + 1, half:]
            if reverse:
                ai = -ai
            r0 = pl.multiple_of(c * ch, ch)
            cur = src[pl.ds(pad + r0, ch), :]
            if d >= SUBLANE:
                off = pad + d if reverse else pad - d
                sh = src[pl.ds(off + r0, ch), :]
            elif reverse:
                blk = src[pl.ds(pad + r0, ch + SUBLANE), :]
                sh = pltpu.roll(blk, ch + SUBLANE - d, axis=0)[:ch, :]
            else:
                blk = src[pl.ds(pad - SUBLANE + r0, ch + SUBLANE), :]
                sh = pltpu.roll(blk, d, axis=0)[SUBLANE:, :]
            sre, sim = sh[:, :half], sh[:, half:]
            dst[pl.ds(pad + r0, ch), :half] = cur[:, :half] + ar * sre - ai * sim
            dst[pl.ds(pad + r0, ch), half:] = cur[:, half:] + ar * sim + ai * sre
            return carry

        lax.fori_loop(0, t // ch, chunk, 0)
    return nlev % 2


def _rscan_levels(abufs, bbufs, t, pad, *, reverse):
    nlev = t.bit_length() - 1
    assert (1 << nlev) == t
    for k in range(nlev):
        d = 1 << k
        asrc, adst = abufs[k % 2], abufs[(k + 1) % 2]
        bsrc, bdst = bbufs[k % 2], bbufs[(k + 1) % 2]
        off = pad + d if reverse else pad - d
        a = asrc[pad:pad + t, :]
        bdst[pad:pad + t, :] = a * bsrc[off:off + t, :] + bsrc[pad:pad + t, :]
        if k < nlev - 1:
            adst[pad:pad + t, :] = a * asrc[off:off + t, :]
    return nlev % 2


S5_CHUNK = 16
S5_SG = S5_GROUPS // 2
S5_SG_IN = 2 * S5_CHUNK * S5_CH
S5_SG_ST = 2 * S5_STATE


S5_HALF_SGS = S5_SG // 2
S5_HALF_IN = S5_HALF_SGS * S5_SG_IN


def _s5_perm():
    idx = jnp.arange(S5_HALF_IN)
    step, grp, chan = idx // LANE, (idx % LANE) // S5_CH, idx % S5_CH
    col = (grp // 2) * S5_SG_IN + (grp % 2) * (S5_CHUNK * S5_CH) + step * S5_CH + chan
    return (col[:, None] == idx[None, :]).astype(BF16)


def _s5_to_chunks(x, col_block, perm, *, name):
    s = x.shape[0]
    nb = s // S5_CHUNK

    def body(x_ref, perm_ref, o_ref):
        tok = jnp.concatenate([x_ref[pl.ds(t, nb, stride=S5_CHUNK), :].astype(BF16) for t in range(S5_CHUNK)], axis=1)
        grouped = _dot(tok, perm_ref[...]).astype(BF16)
        for k in range(S5_HALF_SGS):
            o_ref[k] = grouped[:, k * S5_SG_IN:(k + 1) * S5_SG_IN]

    return pl.pallas_call(
        body, grid=(2,),
        in_specs=[pl.BlockSpec((s, LANE), lambda h: (0, col_block + h)), _full_spec(perm)],
        out_specs=pl.BlockSpec((S5_HALF_SGS, nb, S5_SG_IN), lambda h: (h, 0, 0)),
        out_shape=jax.ShapeDtypeStruct((S5_SG, nb, S5_SG_IN), BF16),
        compiler_params=_cparams(1), name=name)(x, perm)


def _s5_from_chunks(y, perm, *, name):
    _, nb, _ = y.shape

    def body(y_ref, perm_ref, o_ref):
        grouped = jnp.concatenate([y_ref[k] for k in range(S5_HALF_SGS)], axis=1)
        hi = grouped.astype(BF16)
        lo = (grouped - hi.astype(F32)).astype(BF16)
        tok = _dot_nt(hi, perm_ref[...]) + _dot_nt(lo, perm_ref[...])
        for t in range(S5_CHUNK):
            o_ref[pl.ds(t, nb, stride=S5_CHUNK), :] = tok[:, t * LANE:(t + 1) * LANE]

    return pl.pallas_call(
        body, grid=(2,),
        in_specs=[pl.BlockSpec((S5_HALF_SGS, nb, S5_SG_IN), lambda h: (h, 0, 0)), _full_spec(perm)],
        out_specs=pl.BlockSpec((nb * S5_CHUNK, LANE), lambda h: (0, h)),
        out_shape=jax.ShapeDtypeStruct((nb * S5_CHUNK, D_GROUP), F32),
        compiler_params=_cparams(1), name=name)(y, perm)


def _s5_core_fwd(u2, m2, pre, pim, qre, qim, a16, *, name):
    sg, nb, nin = u2.shape
    st2 = 2 * S5_SG_ST
    pad = nb // 2

    def body(u_ref, m_ref, pre_ref, pim_ref, qre_ref, qim_ref, a_ref, y_ref, x_ref, buf0, buf1):
        @pl.when(pl.program_id(0) == 0)
        def _():
            buf0[0:pad, :] = jnp.zeros((pad, st2), F32)
            buf1[0:pad, :] = jnp.zeros((pad, st2), F32)

        u = u_ref[...]
        buf0[pad:pad + nb, :S5_SG_ST] = _dot(u, pre_ref[...])
        buf0[pad:pad + nb, S5_SG_ST:] = _dot(u, pim_ref[...])
        xbuf = (buf0, buf1)[_cscan_levels((buf0, buf1), a_ref, nb, pad, reverse=False)]
        x_ref[...] = xbuf[pad:pad + nb, :]
        xprev = xbuf[pad - 1:pad - 1 + nb, :]
        y_ref[...] = _dot(u, m_ref[...]) + _dot(xprev[:, :S5_SG_ST], qre_ref[...]) + _dot(xprev[:, S5_SG_ST:], qim_ref[...])

    ins = [u2, m2, pre, pim, qre, qim, a16]
    return pl.pallas_call(
        body, grid=(sg,), in_specs=[pl.BlockSpec((None,) + a.shape[1:], lambda i: (i, 0, 0)) for a in ins],
        out_specs=[pl.BlockSpec((None, nb, nin), lambda i: (i, 0, 0)), pl.BlockSpec((None, nb, st2), lambda i: (i, 0, 0))],
        out_shape=[jax.ShapeDtypeStruct((sg, nb, nin), F32), jax.ShapeDtypeStruct((sg, nb, st2), F32)],
        scratch_shapes=[pltpu.VMEM((pad + nb, st2), F32), pltpu.VMEM((pad + nb, st2), F32)],
        compiler_params=_cparams(1), name=name)(*ins)


def _s5_core_bwd(u2, dy2, x_all, m2, pre, pim, qre, qim, a16, *, name):
    sg, nb, nin = u2.shape
    half = S5_SG_ST
    st2 = 2 * half
    pad = nb // 2

    def body(u_ref, dy_ref, x_ref, m_ref, pre_ref, pim_ref, qre_ref, qim_ref, a_ref,
             du_ref, dm_ref, dpre_ref, dpim_ref, dqre_ref, dqim_ref, da_ref, buf2, buf3, xp):
        @pl.when(pl.program_id(0) == 0)
        def _():
            buf2[nb:nb + pad, :] = jnp.zeros((pad, st2), F32)
            buf3[nb:nb + pad, :] = jnp.zeros((pad, st2), F32)
            xp[0:SUBLANE, :] = jnp.zeros((SUBLANE, st2), F32)

        u = u_ref[...]
        dy = dy_ref[...]
        dm_ref[...] = _dot_tn(u, dy)
        xp[SUBLANE:SUBLANE + nb, :] = x_ref[...]
        xprev = xp[SUBLANE - 1:SUBLANE - 1 + nb, :]
        xre, xim = xprev[:, :half], xprev[:, half:]
        dqre_ref[...] = _dot_tn(xre, dy)
        dqim_ref[...] = _dot_tn(xim, dy)
        buf2[0:nb, :half] = _dot_nt(dy, qre_ref[...])
        buf2[0:nb, half:] = _dot_nt(dy, qim_ref[...])
        mbuf = (buf2, buf3)[_cscan_levels((buf2, buf3), a_ref, nb, 0, reverse=True)]
        lam = mbuf[1:1 + nb, :]
        lre, lim = lam[:, :half], lam[:, half:]
        dpre_ref[...] = _dot_tn(u, lre)
        dpim_ref[...] = _dot_tn(u, lim)
        du_ref[...] = _dot_nt(dy, m_ref[...]) + _dot_nt(lre, pre_ref[...]) + _dot_nt(lim, pim_ref[...])
        da_ref[:, :half] = _colsum(lre * xre + lim * xim)
        da_ref[:, half:] = _colsum(lim * xre - lre * xim)

    ins = [u2, dy2, x_all, m2, pre, pim, qre, qim, a16]
    outs = [jax.ShapeDtypeStruct((sg, nb, nin), F32)] + [jax.ShapeDtypeStruct(a.shape, F32) for a in (m2, pre, pim, qre, qim)] + \
           [jax.ShapeDtypeStruct((sg, 1, st2), F32)]
    return pl.pallas_call(
        body, grid=(sg,), in_specs=[pl.BlockSpec((None,) + a.shape[1:], lambda i: (i, 0, 0)) for a in ins],
        out_specs=[pl.BlockSpec((None,) + o.shape[1:], lambda i: (i, 0, 0)) for o in outs], out_shape=outs,
        scratch_shapes=[pltpu.VMEM((nb + pad, st2), F32), pltpu.VMEM((nb + pad, st2), F32),
                        pltpu.VMEM((SUBLANE + nb, st2), F32)],
        compiler_params=_cparams(1), name=name)(*ins)


def _s5_glu_fwd(y1, wglu, bglu, *, name, rider=None):
    s = y1.shape[0]
    t = _seq_tile(s, 512)

    def body(y1_ref, wglu_ref, bglu_ref, out_ref):
        y2 = _gelu(y1_ref[...])
        out_ref[...] = (y2 * _sigmoid(_dot(y2, wglu_ref[...]) + bglu_ref[...])).astype(BF16)

    return _call(
        body, grid=(s // t,), ins=[y1, wglu, bglu],
        in_specs=[pl.BlockSpec((t, D_GROUP), lambda i: (i, 0)), _full_spec(wglu), _full_spec(bglu)],
        out_specs=[pl.BlockSpec((t, D_GROUP), lambda i: (i, MIX_S5))], outs=[jax.ShapeDtypeStruct((s, D_MODEL), BF16)],
        name=name, rider=rider)


def _s5_glu_bwd(y1, dmix, wglu, bglu, *, name):
    s = y1.shape[0]
    t = _seq_tile(s, 512)

    def body(y1_ref, do_ref, wglu_ref, bglu_ref, dy1_ref, dwglu_ref, dbglu_ref):
        @pl.when(pl.program_id(0) == 0)
        def _():
            dwglu_ref[...] = jnp.zeros_like(dwglu_ref)
            dbglu_ref[...] = jnp.zeros_like(dbglu_ref)

        dout = do_ref[...]
        y2, dgelu = _gelu_and_grad(y1_ref[...])
        sg = _sigmoid(_dot(y2, wglu_ref[...]) + bglu_ref[...])
        dz = dout * y2 * sg * (1.0 - sg)
        dwglu_ref[...] += _dot_tn(y2, dz)
        dbglu_ref[...] += _colsum(dz)
        dy1_ref[...] = (dout * sg + _dot_nt(dz, wglu_ref[...])) * dgelu

    outs = [jax.ShapeDtypeStruct((s, D_GROUP), F32), jax.ShapeDtypeStruct((D_GROUP, D_GROUP), F32),
            jax.ShapeDtypeStruct((1, D_GROUP), F32)]
    return pl.pallas_call(
        body, grid=(s // t,),
        in_specs=[pl.BlockSpec((t, D_GROUP), lambda i: (i, 0)), pl.BlockSpec((t, D_GROUP), lambda i: (i, MIX_S5)),
                  _full_spec(wglu), _full_spec(bglu)],
        out_specs=[pl.BlockSpec((t, D_GROUP), lambda i: (i, 0)), _full_spec(outs[1]), _full_spec(outs[2])],
        out_shape=outs, compiler_params=_cparams(1), name=name)(y1, dmix, wglu, bglu)


def _pair_blockdiag(x):
    g, r, c = x.shape
    x = x.reshape(g // 2, 2, r, c)
    z = jnp.zeros_like(x[:, 0])
    return jnp.concatenate([jnp.concatenate([x[:, 0], z], axis=2), jnp.concatenate([z, x[:, 1]], axis=2)], axis=1)


def _s5_chunk_map(lam_re, lam_im, log_dt, b_re, b_im, c_re, c_im, d_skip):
    g, n, c, lc = S5_GROUPS, S5_STATE, S5_CH, S5_CHUNK
    dt = jnp.exp(log_dt)[:, None]
    mag, ang = lam_re * dt, lam_im * dt
    j = jnp.arange(lc + 1, dtype=F32)[:, None, None]
    pw_mag = jnp.exp(j * mag)
    pw_re, pw_im = pw_mag * jnp.cos(j * ang), pw_mag * jnp.sin(j * ang)
    a_re, a_im = pw_re[1], pw_im[1]
    den = lam_re * lam_re + lam_im * lam_im
    n_re = a_re - 1.0
    k_re = (n_re * lam_re + a_im * lam_im) / den
    k_im = (a_im * lam_re - n_re * lam_im) / den
    bb_re = k_re[..., None] * b_re - k_im[..., None] * b_im
    bb_im = k_re[..., None] * b_im + k_im[..., None] * b_re
    e_re = pw_re[:lc, :, :, None] * bb_re - pw_im[:lc, :, :, None] * bb_im
    e_im = pw_re[:lc, :, :, None] * bb_im + pw_im[:lc, :, :, None] * bb_re
    kern = jnp.einsum("gdn,jgnc->jgdc", c_re, e_re) - jnp.einsum("gdn,jgnc->jgdc", c_im, e_im)
    steps = jnp.arange(lc)
    lag = (steps[None, :, None] + steps[:, None, None] == steps[None, None, :]).astype(F32)
    m = jnp.einsum("jst,jgdc->gsctd", lag, kern).reshape(g, lc * c, lc * c)
    skip = jnp.tile(d_skip.reshape(g, 1, c), (1, lc, 1)).reshape(g, lc * c)
    m = m + jnp.eye(lc * c, dtype=F32)[None] * skip[:, None, :]
    p_re = jnp.transpose(e_re[::-1], (1, 0, 3, 2)).reshape(g, lc * c, n)
    p_im = jnp.transpose(e_im[::-1], (1, 0, 3, 2)).reshape(g, lc * c, n)
    f_re = c_re[None] * pw_re[1:, :, None, :] - c_im[None] * pw_im[1:, :, None, :]
    f_im = c_re[None] * pw_im[1:, :, None, :] + c_im[None] * pw_re[1:, :, None, :]
    q_re = jnp.transpose(f_re, (1, 3, 0, 2)).reshape(g, n, lc * c)
    q_im = -jnp.transpose(f_im, (1, 3, 0, 2)).reshape(g, n, lc * c)
    a16 = jnp.concatenate([pw_re[lc].reshape(S5_SG, 1, S5_SG_ST), pw_im[lc].reshape(S5_SG, 1, S5_SG_ST)], axis=2)
    return (_pair_blockdiag(m), _pair_blockdiag(p_re), _pair_blockdiag(p_im), _pair_blockdiag(q_re),
            _pair_blockdiag(q_im), a16)


def _s5_a16_powers(a16, nlev):
    half = S5_SG_ST
    re, im = a16[:, :, :half], a16[:, :, half:]
    rows = []
    for _ in range(nlev):
        rows.append(jnp.concatenate([re, im], axis=2))
        re, im = re * re - im * im, 2.0 * re * im
    n_rows = -(-nlev // SUBLANE) * SUBLANE
    rows += [jnp.zeros_like(rows[0])] * (n_rows - nlev)
    return lax.stop_gradient(jnp.concatenate(rows, axis=1))


CV_TILE = 256
CV_PAD = 32
CV_CHUNK = 64


def _shifted_copies(buf, shifted, rows):
    n = rows - SUBLANE
    for s in range(1, SUBLANE):
        shifted[s - 1, 0:n, :] = buf[s:s + n, :]


def _window(buf, shifted, o, ch):
    q, s = divmod(o, SUBLANE)
    if s == 0:
        return buf[o:o + ch, :]
    return shifted[s - 1, q * SUBLANE:q * SUBLANE + ch, :]


def _gn_stats(c, mavg):
    mu = _dot_hi(c, mavg)
    cen = c - mu
    var = _dot_hi(cen * cen, mavg)
    rstd = lax.rsqrt(var + LN_EPS)
    return cen * rstd, rstd


def _cv_fwd(h_in, cw, cb, gng, gnb, mavg, wpw, bpw, mix, *, name, rider=None):
    s = h_in.shape[0]
    t = _seq_tile(s, CV_TILE)
    ch = min(CV_CHUNK, t)

    def body(v_ref, g_ref, cw_ref, cb_ref, gng_ref, gnb_ref, mavg_ref, wpw_ref, bpw_ref, _mix_in, out_ref, c_ref, xpad,
             shifted):
        @pl.when(pl.program_id(0) == 0)
        def _():
            xpad[0:CV_PAD, :] = jnp.zeros((CV_PAD, D_GROUP), F32)

        xpad[CV_PAD:CV_PAD + t, :] = v_ref[...] * _sigmoid(g_ref[...])
        _shifted_copies(xpad, shifted, t + CV_PAD)
        for r0 in range(0, t, ch):
            acc = jnp.broadcast_to(cb_ref[...], (ch, D_GROUP))
            for k in range(CONV_WIDTH):
                o = CV_PAD - (CONV_WIDTH - 1) + k + r0
                acc = acc + cw_ref[k:k + 1, :] * _window(xpad, shifted, o, ch)
            c_ref[r0:r0 + ch, :] = acc
        xpad[0:CV_PAD, :] = xpad[t:t + CV_PAD, :]
        xn, _ = _gn_stats(c_ref[...], mavg_ref[...])
        gn = xn * gng_ref[...] + gnb_ref[...]
        out_ref[...] = (_dot(gn * _sigmoid(gn), wpw_ref[...]) + bpw_ref[...]).astype(BF16)

    ins = [h_in, h_in, cw, cb, gng, gnb, mavg, wpw, bpw, mix]
    in_specs = [pl.BlockSpec((t, D_GROUP), lambda i: (i, COL_CV_V)), pl.BlockSpec((t, D_GROUP), lambda i: (i, COL_CV_G))] + \
               [_full_spec(a) for a in ins[2:9]] + [_ANY]
    return _call(
        body, grid=(s // t,), ins=ins, in_specs=in_specs,
        out_specs=[pl.BlockSpec((t, D_GROUP), lambda i: (i, MIX_CV)), pl.BlockSpec((t, D_GROUP), lambda i: (i, 0))],
        outs=[jax.ShapeDtypeStruct((s, D_MODEL), BF16), jax.ShapeDtypeStruct((s, D_GROUP), F32)],
        aliases={9: 0},
        scratch=[pltpu.VMEM((CV_PAD + t, D_GROUP), F32), pltpu.VMEM((SUBLANE - 1, CV_PAD + t, D_GROUP), F32)],
        name=name, rider=rider)


def _cv_bwd(h_in, c, dmix, cw, gng, gnb, mavg, wpw, *, name, rider=None):
    s = h_in.shape[0]
    t = _seq_tile(s, CV_TILE)
    nt = s // t
    ch = min(CV_CHUNK, t)

    def body(v_ref, g_ref, c_ref, do_ref, cw_ref, gng_ref, gnb_ref, mavg_ref, wpw_ref,
             dvg_ref, dwpw_ref, dcw_ref, dbpw_ref, dgg_ref, dgb_ref, dcb_ref, dcpad, hgbuf, shifted):
        @pl.when(pl.program_id(0) == 0)
        def _():
            dcpad[t:t + CV_PAD, :] = jnp.zeros((CV_PAD, D_GROUP), F32)
            for r in (dwpw_ref, dcw_ref, dbpw_ref, dgg_ref, dgb_ref, dcb_ref):
                r[...] = jnp.zeros_like(r)

        mavg = mavg_ref[...]
        xn, rstd = _gn_stats(c_ref[...], mavg)
        gg = gng_ref[...]
        gn = xn * gg + gnb_ref[...]
        sg = _sigmoid(gn)
        dout = do_ref[...]
        dwpw_ref[...] += _dot_tn(gn * sg, dout)
        dbpw_ref[...] += _colsum(dout)
        dgn = _dot_nt(dout, wpw_ref[...]) * (sg * (1.0 + gn * (1.0 - sg)))
        dgg_ref[...] += _colsum(dgn * xn)
        dgb_ref[...] += _colsum(dgn)
        dxn = dgn * gg
        dc = rstd * (dxn - _dot_hi(dxn, mavg) - xn * _dot_hi(dxn * xn, mavg))
        dcb_ref[...] += _colsum(dc)
        dcpad[0:t, :] = dc

        v = v_ref[...]
        sgm = _sigmoid(g_ref[...])
        hgbuf[...] = v * sgm
        _shifted_copies(dcpad, shifted, t + CV_PAD)
        for r0 in range(0, t, ch):
            hg = hgbuf[r0:r0 + ch, :]
            acc = jnp.zeros((ch, D_GROUP), F32)
            for k in range(CONV_WIDTH):
                o = (CONV_WIDTH - 1) - k + r0
                sh = _window(dcpad, shifted, o, ch)
                acc = acc + cw_ref[k:k + 1, :] * sh
                dcw_ref[k:k + 1, :] += _colsum(hg * sh)
            hgbuf[r0:r0 + ch, :] = acc
        dcpad[t:t + CV_PAD, :] = dcpad[0:CV_PAD, :]
        dhg = hgbuf[...]
        dvg_ref[:, :D_GROUP] = dhg * sgm
        dvg_ref[:, D_GROUP:] = dhg * v * sgm * (1.0 - sgm)

    def rev(col):
        return lambda i: (nt - 1 - i, col)

    ins = [h_in, h_in, c, dmix, cw, gng, gnb, mavg, wpw]
    in_specs = [pl.BlockSpec((t, D_GROUP), rev(COL_CV_V)), pl.BlockSpec((t, D_GROUP), rev(COL_CV_G)),
                pl.BlockSpec((t, D_GROUP), rev(0)), pl.BlockSpec((t, D_GROUP), rev(MIX_CV))] + [_full_spec(a) for a in ins[4:]]
    vec = jax.ShapeDtypeStruct((1, D_GROUP), F32)
    outs = [jax.ShapeDtypeStruct((s, N_IN_COLS), F32),
            jax.ShapeDtypeStruct((D_GROUP, D_GROUP), F32), jax.ShapeDtypeStruct((CV_PAD, D_GROUP), F32), vec, vec, vec, vec]
    out_specs = [pl.BlockSpec((t, 2 * D_GROUP), rev(COL_CV_V // 2))] + [_full_spec(o) for o in outs[1:]]
    return _call(
        body, grid=(nt,), ins=ins, in_specs=in_specs, out_specs=out_specs, outs=outs,
        scratch=[pltpu.VMEM((t + CV_PAD, D_GROUP), F32), pltpu.VMEM((t, D_GROUP), F32),
                 pltpu.VMEM((SUBLANE - 1, t + CV_PAD, D_GROUP), F32)], name=name, rider=rider)


LRU_TILE = 256


def _lru_gates(xc, wr_ref, br_ref, wi_ref, bi_ref, sp_ref):
    r = _sigmoid(_dot(xc, wr_ref[...]) + br_ref[...])
    i = _sigmoid(_dot(xc, wi_ref[...]) + bi_ref[...])
    log_a = -LRU_C * r * sp_ref[...]
    a = jnp.exp(log_a)
    m = jnp.sqrt(_neg_expm1(2.0 * log_a))
    return r, i, a, m


def _lru_fwd(h_in, lcw, lcb, wr, br, wi, bi, sp, mix, *, name):
    s = h_in.shape[0]
    t = _seq_tile(s, LRU_TILE)
    pad = max(t // 2, SUBLANE)

    def body(xg_ref, xr_ref, lcw_ref, lcb_ref, wr_ref, br_ref, wi_ref, bi_ref, sp_ref, _mix_in,
             out_ref, xc_ref, h_ref, xpad, a0, a1, b0, b1, carry):
        @pl.when(pl.program_id(0) == 0)
        def _():
            xpad[0:SUBLANE, :] = jnp.zeros((SUBLANE, D_GROUP), F32)
            for bf in (a0, a1, b0, b1):
                bf[0:pad, :] = jnp.zeros((pad, D_GROUP), F32)
            carry[...] = jnp.zeros_like(carry)

        xpad[SUBLANE:SUBLANE + t, :] = xr_ref[...]
        xc = jnp.broadcast_to(lcb_ref[...], (t, D_GROUP))
        for k in range(LRU_CONV_WIDTH):
            o = SUBLANE - (LRU_CONV_WIDTH - 1) + k
            xc = xc + lcw_ref[k:k + 1, :] * xpad[o:o + t, :]
        xpad[0:SUBLANE, :] = xpad[t:t + SUBLANE, :]
        xc_ref[...] = xc
        _, i, a, m = _lru_gates(xc, wr_ref, br_ref, wi_ref, bi_ref, sp_ref)
        a0[pad:pad + t, :] = a
        b0[pad:pad + t, :] = m * (i * xc)
        b0[pad:pad + 1, :] += a0[pad:pad + 1, :] * carry[0:1, :]
        fin = _rscan_levels((a0, a1), (b0, b1), t, pad, reverse=False)
        hbuf = (b0, b1)[fin]
        carry[0:1, :] = hbuf[pad + t - 1:pad + t, :]
        h = hbuf[pad:pad + t, :]
        h_ref[...] = h
        out_ref[...] = (h * _gelu(xg_ref[...])).astype(BF16)

    ins = [h_in, h_in, lcw, lcb, wr, br, wi, bi, sp, mix]
    row = pl.BlockSpec((t, D_GROUP), lambda i: (i, 0))
    in_specs = [pl.BlockSpec((t, D_GROUP), lambda i: (i, COL_LRU_G)), pl.BlockSpec((t, D_GROUP), lambda i: (i, COL_LRU_X))] + \
               [_full_spec(a) for a in ins[2:9]] + [_ANY]
    return pl.pallas_call(
        body, grid=(s // t,), in_specs=in_specs,
        out_specs=[pl.BlockSpec((t, D_GROUP), lambda i: (i, MIX_LRU)), row, row],
        out_shape=[jax.ShapeDtypeStruct((s, D_MODEL), BF16)] + [jax.ShapeDtypeStruct((s, D_GROUP), F32)] * 2,
        input_output_aliases={9: 0},
        scratch_shapes=[pltpu.VMEM((SUBLANE + t, D_GROUP), F32)] + [pltpu.VMEM((pad + t, D_GROUP), F32)] * 4 +
                       [pltpu.VMEM((SUBLANE, D_GROUP), F32)],
        compiler_params=_cparams(1), name=name)(*ins)


def _lru_bwd(h_in, xc_all, h_all, dmix, lcw, wr, br, wi, bi, sp, dh_all, *, name):
    s = h_in.shape[0]
    t = _seq_tile(s, LRU_TILE)
    nt = s // t
    pad = max(t // 2, SUBLANE)
    tb = t // SUBLANE

    def body(xg_ref, xr_ref, xc_ref, h_ref, hprev_ref, do_ref, lcw_ref, wr_ref, br_ref, wi_ref, bi_ref, sp_ref, _dh_in,
             dgr_ref, dwr_ref, dwi_ref, dlcw_ref, dbr_ref, dbi_ref, dsp_ref, dlcb_ref,
             a0, a1, b0, b1, hp, dxpad, carry):
        pid = pl.program_id(0)

        @pl.when(pid == 0)
        def _():
            for bf in (a0, a1, b0, b1):
                bf[pad + t:pad + t + pad, :] = jnp.zeros((pad, D_GROUP), F32)
            dxpad[t:t + SUBLANE, :] = jnp.zeros((SUBLANE, D_GROUP), F32)
            carry[...] = jnp.zeros_like(carry)
            for r in (dwr_ref, dwi_ref, dlcw_ref, dbr_ref, dbi_ref, dsp_ref, dlcb_ref):
                r[...] = jnp.zeros_like(r)

        xc = xc_ref[...]
        h = h_ref[...]
        dout = do_ref[...]
        gate, dgate = _gelu_and_grad(xg_ref[...])
        dgr_ref[:, :D_GROUP] = dout * h * dgate
        r, i, a, m = _lru_gates(xc, wr_ref, br_ref, wi_ref, bi_ref, sp_ref)

        a0[pad:pad + t, :] = a
        b0[pad:pad + t, :] = dout * gate
        b0[pad + t - 1:pad + t, :] += carry[0:1, :]
        a1[pad:pad + t, :] = a0[pad + 1:pad + 1 + t, :]
        fin = _rscan_levels((a1, a0), (b0, b1), t, pad, reverse=True)
        lam = (b0, b1)[fin][pad:pad + t, :]
        carry[0:1, :] = a[0:1, :] * lam[0:1, :]

        is_first = pid == nt - 1
        hp[0:SUBLANE, :] = jnp.where(is_first, 0.0, hprev_ref[...])
        hp[SUBLANE:SUBLANE + t, :] = h
        hprev = hp[SUBLANE - 1:SUBLANE - 1 + t, :]

        ix = i * xc
        dmm = lam * ix
        dix = lam * m
        da = lam * hprev - dmm * (a / m)
        dlog_a = da * a
        dr = dlog_a * (-LRU_C * sp_ref[...])
        dsp_ref[...] += _colsum(dlog_a * (-LRU_C * r))
        dpr = dr * r * (1.0 - r)
        dpi = dix * xc * i * (1.0 - i)
        dbr_ref[...] += _colsum(dpr)
        dbi_ref[...] += _colsum(dpi)
        dwr_ref[...] += _dot_tn(xc, dpr)
        dwi_ref[...] += _dot_tn(xc, dpi)
        dxc = dix * i + _dot_nt(dpr, wr_ref[...]) + _dot_nt(dpi, wi_ref[...])
        dlcb_ref[...] += _colsum(dxc)

        dxpad[0:t, :] = dxc
        xr = xr_ref[...]
        dxr = jnp.zeros((t, D_GROUP), F32)
        for k in range(LRU_CONV_WIDTH):
            o = (LRU_CONV_WIDTH - 1) - k
            sh = dxpad[o:o + t, :]
            dxr = dxr + lcw_ref[k:k + 1, :] * sh
            dlcw_ref[k:k + 1, :] += _colsum(xr * sh)
        dxpad[t:t + SUBLANE, :] = dxpad[0:SUBLANE, :]
        dgr_ref[:, D_GROUP:] = dxr

    def rev(col):
        return lambda i: (nt - 1 - i, col)

    ins = [h_in, h_in, xc_all, h_all, h_all, dmix, lcw, wr, br, wi, bi, sp, dh_all]
    in_specs = [pl.BlockSpec((t, D_GROUP), rev(COL_LRU_G)), pl.BlockSpec((t, D_GROUP), rev(COL_LRU_X)),
                pl.BlockSpec((t, D_GROUP), rev(0)), pl.BlockSpec((t, D_GROUP), rev(0)),
                pl.BlockSpec((SUBLANE, D_GROUP), lambda i: (jnp.maximum((nt - 1 - i) * tb - 1, 0), 0)),
                pl.BlockSpec((t, D_GROUP), rev(MIX_LRU))] + [_full_spec(a) for a in ins[6:12]] + [_ANY]
    vec = jax.ShapeDtypeStruct((1, D_GROUP), F32)
    mat = jax.ShapeDtypeStruct((D_GROUP, D_GROUP), F32)
    outs = [jax.ShapeDtypeStruct((s, N_IN_COLS), F32), mat, mat, jax.ShapeDtypeStruct((SUBLANE, D_GROUP), F32),
            vec, vec, vec, vec]
    out_specs = [pl.BlockSpec((t, 2 * D_GROUP), rev(COL_LRU_G // 2))] + [_full_spec(o) for o in outs[1:]]
    return pl.pallas_call(
        body, grid=(nt,), in_specs=in_specs, out_specs=out_specs, out_shape=outs, input_output_aliases={12: 0},
        scratch_shapes=[pltpu.VMEM((pad + t + pad, D_GROUP), F32)] * 4 +
                       [pltpu.VMEM((SUBLANE + t, D_GROUP), F32), pltpu.VMEM((t + SUBLANE, D_GROUP), F32),
                        pltpu.VMEM((SUBLANE, D_GROUP), F32)],
        compiler_params=_cparams(1), name=name)(*ins)


def _blockdiag(w):
    h, d, _ = w.shape
    return jnp.tile(w.reshape(h * d, d), (1, h)) * _block_mask(h, d, d)


ATTN_TILE = 512
ATTN_SCALE = ATTN_HEAD_DIM ** -0.5


def _attn_big(kv):
    m = kv.shape[0]
    kbig = jnp.tile(kv[:, :D_GROUP].T, (1, ATTN_HEADS)) * _block_mask(ATTN_HEADS, ATTN_HEAD_DIM, m)
    vbig = jnp.tile(kv[:, D_GROUP:], (ATTN_HEADS, 1)) * _block_mask(ATTN_HEADS, m, ATTN_HEAD_DIM)
    return kbig, vbig


def _attn_probs(q, kbig_ref, m):
    sc = _dot(q, kbig_ref[...]) * ATTN_SCALE
    ps = []
    for h in range(ATTN_HEADS):
        sh = sc[:, h * m:(h + 1) * m]
        e = jnp.exp(sh - jnp.max(sh, axis=1, keepdims=True))
        ps.append(e / jnp.sum(e, axis=1, keepdims=True))
    return ps


def _attn_fwd(h_in, kbig, vbig, mix, *, name):
    s = h_in.shape[0]
    t = _seq_tile(s, ATTN_TILE)
    m = kbig.shape[1] // ATTN_HEADS

    def body(q_ref, kbig_ref, vbig_ref, _mix_in, o_ref):
        ps = _attn_probs(q_ref[...], kbig_ref, m)
        o_ref[...] = _dot(jnp.concatenate(ps, axis=1), vbig_ref[...]).astype(BF16)

    return pl.pallas_call(
        body, grid=(s // t,),
        in_specs=[pl.BlockSpec((t, D_GROUP), lambda i: (i, COL_Q)), _full_spec(kbig), _full_spec(vbig), _ANY],
        out_specs=pl.BlockSpec((t, D_GROUP), lambda i: (i, MIX_ATTN)),
        out_shape=jax.ShapeDtypeStruct((s, D_MODEL), BF16), input_output_aliases={3: 0},
        compiler_params=_cparams(1), name=name)(h_in, kbig, vbig, mix)


def _attn_bwd(h_in, dmix, kbig, vbig, du_s5, dh_all, *, name):
    s = h_in.shape[0]
    t = _seq_tile(s, ATTN_TILE)
    m = kbig.shape[1] // ATTN_HEADS

    def body(q_ref, do_ref, kbig_ref, vbig_ref, dus5_ref, _dh_in, dpair_ref, dk_ref, dv_ref):
        @pl.when(pl.program_id(0) == 0)
        def _():
            dk_ref[...] = jnp.zeros_like(dk_ref)
            dv_ref[...] = jnp.zeros_like(dv_ref)

        q = q_ref[...]
        dout = do_ref[...]
        ps = _attn_probs(q, kbig_ref, m)
        dp = _dot_nt(dout, vbig_ref[...])
        dss = []
        for h in range(ATTN_HEADS):
            dph = dp[:, h * m:(h + 1) * m]
            dss.append(ps[h] * (dph - jnp.sum(dph * ps[h], axis=1, keepdims=True)))
        ds = (jnp.concatenate(dss, axis=1) * ATTN_SCALE).astype(BF16)
        dv_ref[...] += _dot_tn(jnp.concatenate(ps, axis=1), dout)
        dpair_ref[:, :D_GROUP] = dus5_ref[...]
        dpair_ref[:, D_GROUP:] = _dot_nt(ds, kbig_ref[...])
        dk_ref[...] += _dot_tn(q, ds)

    assert (COL_S5, COL_Q) == (4, 5)
    outs = [jax.ShapeDtypeStruct((s, N_IN_COLS), F32), jax.ShapeDtypeStruct(kbig.shape, F32),
            jax.ShapeDtypeStruct(vbig.shape, F32)]
    return pl.pallas_call(
        body, grid=(s // t,),
        in_specs=[pl.BlockSpec((t, D_GROUP), lambda i: (i, COL_Q)), pl.BlockSpec((t, D_GROUP), lambda i: (i, MIX_ATTN)),
                  _full_spec(kbig), _full_spec(vbig), pl.BlockSpec((t, D_GROUP), lambda i: (i, 0)), _ANY],
        out_specs=[pl.BlockSpec((t, 2 * D_GROUP), lambda i: (i, COL_S5 // 2)), _full_spec(outs[1]), _full_spec(outs[2])],
        out_shape=outs, input_output_aliases={5: 0},
        compiler_params=_cparams(1), name=name)(h_in, dmix, kbig, vbig, du_s5, dh_all)


FFN_TILE = 128
FFN_COL_CHUNK = 256
FFN_ROW_CHUNK = 64


def _ffn_conv(pad_ref, w_ref, b_ref, r0, ch, c0):
    cc = FFN_COL_CHUNK
    acc = jnp.broadcast_to(b_ref[:, c0:c0 + cc], (ch, cc))
    for k in range(FFN_CONV_WIDTH):
        o = SUBLANE - (FFN_CONV_WIDTH - 1) + k + r0
        acc = acc + w_ref[k:k + 1, c0:c0 + cc] * pad_ref[o:o + ch, c0:c0 + cc]
    return acc


def _ffn_gate_fwd(u, fcw, fcb, *, name, rider=None):
    s = u.shape[0]
    t = _seq_tile(s, FFN_TILE)
    ch = min(FFN_ROW_CHUNK, t)
    cc = FFN_COL_CHUNK

    def body(u_ref, w_ref, b_ref, o_ref, uc_ref, upad):
        @pl.when(pl.program_id(0) == 0)
        def _():
            upad[0:SUBLANE, :] = jnp.zeros((SUBLANE, 2 * D_FF), F32)

        upad[SUBLANE:SUBLANE + t, :] = u_ref[...].astype(F32)
        for c0 in range(0, D_FF, cc):
            for r0 in range(0, t, ch):
                val = _ffn_conv(upad, w_ref, b_ref, r0, ch, c0)
                gt = _ffn_conv(upad, w_ref, b_ref, r0, ch, c0 + D_FF)
                o_ref[r0:r0 + ch, c0:c0 + cc] = (val * _gelu(gt)).astype(BF16)
                uc_ref[r0:r0 + ch, c0:c0 + cc] = val.astype(BF16)
                uc_ref[r0:r0 + ch, c0 + D_FF:c0 + D_FF + cc] = gt.astype(BF16)
        upad[0:SUBLANE, :] = upad[t:t + SUBLANE, :]

    return _call(
        body, grid=(s // t,), ins=[u, fcw, fcb],
        in_specs=[pl.BlockSpec((t, 2 * D_FF), lambda i: (i, 0)), _full_spec(fcw), _full_spec(fcb)],
        out_specs=[pl.BlockSpec((t, D_FF), lambda i: (i, 0)), pl.BlockSpec((t, 2 * D_FF), lambda i: (i, 0))],
        outs=[jax.ShapeDtypeStruct((s, D_FF), BF16), jax.ShapeDtypeStruct((s, 2 * D_FF), BF16)],
        scratch=[pltpu.VMEM((SUBLANE + t, 2 * D_FF), F32)], name=name, rider=rider)


def _ffn_gate_bwd(u, uc, dh, fcw, *, name, rider=None):
    s = u.shape[0]
    t = _seq_tile(s, FFN_TILE)
    nt = s // t
    ch = min(FFN_ROW_CHUNK, t)
    cc = FFN_COL_CHUNK

    def body(u_ref, uc_ref, dh_ref, w_ref, du_ref, dw_ref, db_ref, dpad):
        @pl.when(pl.program_id(0) == 0)
        def _():
            dpad[t:t + SUBLANE, :] = jnp.zeros((SUBLANE, 2 * D_FF), F32)
            dw_ref[...] = jnp.zeros_like(dw_ref)
            db_ref[...] = jnp.zeros_like(db_ref)

        for c0 in range(0, D_FF, cc):
            for r0 in range(0, t, ch):
                val = uc_ref[r0:r0 + ch, c0:c0 + cc].astype(F32)
                gt = uc_ref[r0:r0 + ch, c0 + D_FF:c0 + D_FF + cc].astype(F32)
                gl, dgl = _gelu_and_grad(gt)
                d = dh_ref[r0:r0 + ch, c0:c0 + cc].astype(F32)
                dpad[r0:r0 + ch, c0:c0 + cc] = d * gl
                dpad[r0:r0 + ch, c0 + D_FF:c0 + D_FF + cc] = d * val * dgl
        for c0 in range(0, 2 * D_FF, cc):
            dbs = jnp.zeros((1, cc), F32)
            dws = [jnp.zeros((1, cc), F32) for _ in range(FFN_CONV_WIDTH)]
            for r0 in range(0, t, ch):
                x = u_ref[r0:r0 + ch, c0:c0 + cc].astype(F32)
                acc = jnp.zeros((ch, cc), F32)
                for k in range(FFN_CONV_WIDTH):
                    o = (FFN_CONV_WIDTH - 1) - k + r0
                    sh = dpad[o:o + ch, c0:c0 + cc]
                    acc = acc + w_ref[k:k + 1, c0:c0 + cc] * sh
                    dws[k] = dws[k] + _colsum(x * sh)
                    if k == FFN_CONV_WIDTH - 1:
                        dbs = dbs + _colsum(sh)
                du_ref[r0:r0 + ch, c0:c0 + cc] = acc.astype(BF16)
            db_ref[:, c0:c0 + cc] += dbs
            for k in range(FFN_CONV_WIDTH):
                dw_ref[k:k + 1, c0:c0 + cc] += dws[k]
        dpad[t:t + SUBLANE, :] = dpad[0:SUBLANE, :]

    outs = [jax.ShapeDtypeStruct((s, 2 * D_FF), BF16), jax.ShapeDtypeStruct((SUBLANE, 2 * D_FF), F32),
            jax.ShapeDtypeStruct((1, 2 * D_FF), F32)]
    return _call(
        body, grid=(nt,), ins=[u, uc, dh, fcw],
        in_specs=[pl.BlockSpec((t, 2 * D_FF), lambda i: (nt - 1 - i, 0)),
                  pl.BlockSpec((t, 2 * D_FF), lambda i: (nt - 1 - i, 0)),
                  pl.BlockSpec((t, D_FF), lambda i: (nt - 1 - i, 0)), _full_spec(fcw)],
        out_specs=[pl.BlockSpec((t, 2 * D_FF), lambda i: (nt - 1 - i, 0)), _full_spec(outs[1]), _full_spec(outs[2])],
        outs=outs, scratch=[pltpu.VMEM((t + SUBLANE, 2 * D_FF), F32)], name=name, rider=rider)


def _adamw_body(g_ref, w_ref, m_ref, v_ref, go_ref, d_ref, mo_ref, vo_ref):
    inv_b1 = 1.0 - ADAM_B1 ** ADAM_STEP
    inv_b2 = 1.0 - ADAM_B2 ** ADAM_STEP
    g = g_ref[0].astype(F32)
    for dev in range(1, N_DEV):
        g = g + g_ref[dev].astype(F32)
    go_ref[...] = g
    mn = ADAM_B1 * m_ref[...] + (1.0 - ADAM_B1) * g
    vn = ADAM_B2 * v_ref[...] + (1.0 - ADAM_B2) * (g * g)
    mo_ref[...] = mn
    vo_ref[...] = vn
    d_ref[...] = -ADAM_LR * ((mn / inv_b1) / (jnp.sqrt(vn / inv_b2) + ADAM_EPS) + ADAM_WD * w_ref[...])


def _adamw(gstack, w, m, v, *, name):
    _, r, c = gstack.shape
    tr = _pick_rows(r, PACK_ROW_BLOCK)

    def body(*refs):
        _adamw_body(*refs)

    blk = pl.BlockSpec((tr, c), lambda i: (i, 0))
    sh = jax.ShapeDtypeStruct((r, c), F32)
    return pl.pallas_call(
        body, grid=(r // tr,),
        in_specs=[pl.BlockSpec((N_DEV, tr, c), lambda i: (0, i, 0)), blk, blk, blk],
        out_specs=[blk] * 4, out_shape=[sh] * 4,
        compiler_params=_cparams(1), name=name)(gstack, w, m, v)


def _adamw_layer(gstack, w, m, v, layer, into, *, name):
    n_layers, r, c = w.shape
    tr = _pick_rows(r, PACK_ROW_BLOCK)

    def body(g_ref, w_ref, m_ref, v_ref, *rest):
        _adamw_body(g_ref, w_ref, m_ref, v_ref, *rest[-4:])

    blk = pl.BlockSpec((None, tr, c), lambda i: (layer, i, 0))
    sh = jax.ShapeDtypeStruct((n_layers, r, c), F32)
    into = list(into or [])
    return pl.pallas_call(
        body, grid=(r // tr,),
        in_specs=[pl.BlockSpec((N_DEV, tr, c), lambda i: (0, i, 0)), blk, blk, blk] + [_ANY] * len(into),
        out_specs=[blk] * 4, out_shape=[sh] * 4, input_output_aliases={4 + k: k for k in range(len(into))},
        compiler_params=_cparams(1), name=name)(gstack, w, m, v, *into)


def _exchange(rider, *, name):
    n = rider.n

    def body(*refs):
        x_refs, out_refs, sems = refs[:n], refs[n:2 * n], refs[2 * n:]
        rider.start(x_refs, out_refs, sems)
        rider.wait(x_refs, out_refs, sems)

    return pl.pallas_call(
        body, in_specs=[_ANY] * n, out_specs=[_ANY] * n, out_shape=rider.out_shapes(),
        scratch_shapes=rider.scratch(), name=name)(*rider.srcs)


def _pack_rows(n):
    rows = -(-n // PACK_COLS)
    return -(-rows // SUBLANE) * SUBLANE


def _pack(arrs, dtype):
    flat = jnp.concatenate([a.reshape(-1).astype(dtype) for a in arrs])
    rows = _pack_rows(flat.shape[0])
    flat = jnp.pad(flat, (0, rows * PACK_COLS - flat.shape[0]))
    return flat.reshape(rows, PACK_COLS)


def _pack_lead(arrs, dtype):
    flat = jnp.concatenate([a.reshape(N_DEV, -1).astype(dtype) for a in arrs], axis=1)
    rows = _pack_rows(flat.shape[1])
    flat = jnp.pad(flat, ((0, 0), (0, rows * PACK_COLS - flat.shape[1])))
    return flat.reshape(N_DEV, rows, PACK_COLS)


def _pack_layers(arrs, dtype):
    n_layers = arrs[0].shape[0]
    flat = jnp.concatenate([a.reshape(n_layers, -1).astype(dtype) for a in arrs], axis=1)
    rows = _pack_rows(flat.shape[1])
    flat = jnp.pad(flat, ((0, 0), (0, rows * PACK_COLS - flat.shape[1])))
    return flat.reshape(n_layers, rows, PACK_COLS)


def _unpack_layers(packed, shapes):
    flat = packed.reshape(packed.shape[0], -1)
    out, pos = [], 0
    for sh in shapes:
        n = math.prod(sh[1:])
        out.append(flat[:, pos:pos + n].reshape(sh))
        pos += n
    return out


def _unpack(packed, shapes, lead=False):
    flat = packed.reshape(N_DEV, -1) if lead else packed.reshape(-1)
    out, pos = [], 0
    for sh in shapes:
        n = math.prod(sh)
        out.append(flat[:, pos:pos + n].reshape((N_DEV,) + tuple(sh)) if lead else flat[pos:pos + n].reshape(sh))
        pos += n
    return out


def _join_shards(stacked, axis):
    return jnp.concatenate([stacked[d] for d in range(N_DEV)], axis=axis)


def _split_shards(full, axis):
    return jnp.stack(jnp.split(full, N_DEV, axis=axis), axis=0)


def _perm_in_cols(a, inverse=False):
    blocks = jnp.split(a, 6, axis=-1)
    if inverse:
        order = [IN_PERM.index(j) for j in range(6)]
    else:
        order = list(IN_PERM)
    return jnp.concatenate([blocks[j] for j in order], axis=-1)


def _row(v):
    return v.reshape(1, -1)


def _pad_rows(w, rows):
    return jnp.pad(w, ((0, rows - w.shape[0]), (0, 0)))


def _gn_avg_matrix():
    return _block_mask(GN_GROUPS, D_GROUP // GN_GROUPS, D_GROUP // GN_GROUPS) / (D_GROUP // GN_GROUPS)


def _layer_params(p, l):
    q = {}
    s5_mats, q["s5_vjp"] = jax.vjp(_s5_chunk_map, p["s5_lam_re"][l], p["s5_lam_im"][l], p["s5_log_dt"][l],
                                   p["s5_b_re"][l], p["s5_b_im"][l], p["s5_c_re"][l], p["s5_c_im"][l], p["s5_d"][l])
    q["s5_mats"] = [m.astype(BF16) for m in s5_mats[:5]]
    q["s5_a16"] = s5_mats[5]
    (q["wr"], q["wi"]), q["lru_w_vjp"] = jax.vjp(lambda r, i: (_blockdiag(r), _blockdiag(i)), p["lru_w_r"][l], p["lru_w_i"][l])
    q["wr"], q["wi"] = q["wr"].astype(BF16), q["wi"].astype(BF16)
    q["sp"], q["sp_vjp"] = jax.vjp(lambda lam: _row(jax.nn.softplus(-lam)), p["lru_lam"][l])
    return q


WEIGHT_RIDES = {(0, "ln_in_fwd"): [("w_in", 0)],
                (0, "inproj"): [("attn_w_kv", 0), ("w_out", 0), ("small_pack", 0)],
                (0, "cv_fwd"): [("ffn_w_up#a", 0)],
                (0, "outproj"): [("ffn_w_up#b", 0)],
                (0, "ffn_up"): [("ffn_w_down", 0), ("w_in", 1), ("attn_w_kv", 1), ("w_out", 1)],
                (0, "ffn_gate_fwd"): [("ffn_w_up", 1)],
                (0, "ffn_down"): [("ffn_w_down", 1)]}
GRAD_RIDES = {(1, "ffn_gate_bwd"): [("ffn_w_down", 1)],
              (0, "dw_down"): [("w_out", 1), ("attn_w_kv", 1), ("w_in", 1)],
              (0, "dhff"): [("rep", 1), ("ssh", 1)],
              (0, "ffn_gate_bwd"): [("ffn_w_up", 1)],
              (0, "dw_up"): [("ffn_w_down", 0)],
              (0, "dx1"): [("ffn_w_up", 0)],
              (0, "cv_bwd"): [("w_out", 0)],
              (0, "dw_in"): [("attn_w_kv", 0), ("ssh", 0)],
              (0, "dxs"): [("w_in", 0)],
              (0, "ln_in_bwd"): [("rep", 0)]}


def _join_cols(gathered, *, name):
    n_dev, k, c = gathered.shape
    assert (2 * c) % LANE == 0

    def body(g_ref, o_ref):
        o_ref[...] = jnp.concatenate([g_ref[0], g_ref[1]], axis=1)

    return pl.pallas_call(
        body, grid=(n_dev // 2,), in_specs=[pl.BlockSpec((2, k, c), lambda j: (j, 0, 0))],
        out_specs=pl.BlockSpec((k, 2 * c), lambda j: (0, j)), out_shape=jax.ShapeDtypeStruct((k, n_dev * c), gathered.dtype),
        compiler_params=_cparams(1), name=name)(gathered)


def _assemble_weight(n, gathered, layer=0):
    if n == "ffn_w_up":
        return _join_cols(gathered, name=f"l{layer}_join_w_up")
    if SHARDED[n] == 2:
        full = jnp.transpose(gathered, (1, 0, 2)).reshape(gathered.shape[1], -1)
        return _perm_in_cols(full) if n == "w_in" else full
    return gathered.reshape(-1, gathered.shape[-1])


def _grad_source(n, g):
    g = g.astype(BF16)
    if SHARDED[n] == 2:
        if n == "w_in":
            g = _perm_in_cols(g, inverse=True)
        k, nn = g.shape
        return jnp.transpose(g.reshape(k, N_DEV, nn // N_DEV), (1, 0, 2)), "lead"
    return g, "rows"


def _hosted(fn, keys_rider, land, *args, **kw):
    keys, rider = keys_rider
    if rider is None:
        return fn(*args, **kw)
    out, routs = fn(*args, rider=rider, **kw)
    land(keys, routs)
    return out


def _local_step(x, mem, target, p, big_w, shards=None, unpack_small=None):
    dist = shards is not None
    gdt = BF16 if dist else F32
    small, saved = {}, []
    big_g, ready, recv = {}, {}, {}
    mavg = _gn_avg_matrix()
    s5_perm = _s5_perm()

    def weight_rider(l, host):
        keys = WEIGHT_RIDES.get((l, host), []) if dist else []
        return keys, (_Rider([shards[n][ll] for n, ll in keys], ["all"] * len(keys)) if keys else None)

    halves = {}

    def land_weights(keys, routs):
        for (n, ll), r in zip(keys, routs):
            if n == "small_pack":
                p.update(unpack_small(r))
            elif "#" in n:
                base = n.split("#")[0]
                halves[(n, ll)] = r
                if (base + "#a", ll) in halves and (base + "#b", ll) in halves:
                    whole = jnp.concatenate([halves[(base + "#a", ll)], halves[(base + "#b", ll)]], axis=1)
                    big_w[base][ll] = _assemble_weight(base, whole, ll)
            else:
                big_w[n][ll] = _assemble_weight(n, r, ll)

    def grad_rider(l, host):
        keys = [k for k in GRAD_RIDES.get((l, host), []) if k in ready] if dist else []
        return keys, (_Rider([ready[k][0] for k in keys], [ready[k][1] for k in keys]) if keys else None)

    def land_grads(keys, routs):
        for k, r in zip(keys, routs):
            recv[k] = r
            del ready[k]

    def big_grad(n, l, g):
        if dist:
            ready[(n, l)] = _grad_source(n, g)
        else:
            big_g[(n, l)] = g

    xs = _hosted(_ln_fwd, weight_rider(0, "ln_in_fwd"), land_weights, x, _row(p["ln_in_g"]), _row(p["ln_in_b"]),
                 name="ln_in_fwd")
    for l in range(DEPTH):
        q = _layer_params(p, l)
        n = f"l{l}_"
        hin = _hosted(_mm, weight_rider(l, "inproj"), land_weights, xs, big_w["w_in"][l], bias=_row(p["b_in"][l]),
                      name=n + "inproj")
        nb = hin.shape[0] // S5_CHUNK
        s5_pows = _s5_a16_powers(q["s5_a16"], nb.bit_length() - 1)
        s5_u2 = _s5_to_chunks(hin, COL_S5 * (D_GROUP // LANE), s5_perm, name=n + "s5_in")
        s5_y2, s5_x = _s5_core_fwd(s5_u2, *q["s5_mats"], s5_pows, name=n + "s5_core_fwd")
        s5_y1 = _s5_from_chunks(s5_y2, s5_perm, name=n + "s5_out")
        (mix,), _ = _s5_glu_fwd(s5_y1, p["s5_w_glu"][l], _row(p["s5_b_glu"][l]), name=n + "s5_glu_fwd")
        cvw = _pad_rows(p["cv_w"][l], CV_PAD)
        keys, rd = weight_rider(l, "cv_fwd")
        (mix, cv_c), routs = _cv_fwd(hin, cvw, _row(p["cv_b"][l]), _row(p["cv_gn_g"][l]), _row(p["cv_gn_b"][l]), mavg,
                                     p["cv_w_pw"][l], _row(p["cv_b_pw"][l]), mix, name=n + "cv_fwd", rider=rd)
        land_weights(keys, routs)
        lcw = _pad_rows(p["lru_conv_w"][l], SUBLANE)
        mix, lru_xc, lru_h = _lru_fwd(hin, lcw, _row(p["lru_conv_b"][l]), q["wr"], _row(p["lru_b_r"][l]), q["wi"],
                                      _row(p["lru_b_i"][l]), q["sp"], mix, name=n + "lru_fwd")
        kv = _mm(mem, big_w["attn_w_kv"][l], name=n + "kv")
        (kbig, vbig), kv_vjp = jax.vjp(_attn_big, kv)
        kbig, vbig = kbig.astype(BF16), vbig.astype(BF16)
        mix = _attn_fwd(hin, kbig, vbig, mix, name=n + "attn_fwd")
        r1, x1 = _hosted(_mm, weight_rider(l, "outproj"), land_weights, mix, big_w["w_out"][l], bias=_row(p["b_out"][l]),
                         res=xs, res_scale=ALPHA, ln=(_row(p["ln1_g"][l]), _row(p["ln1_b"][l])), name=n + "outproj")
        u = _hosted(_mm, weight_rider(l, "ffn_up"), land_weights, x1, big_w["ffn_w_up"][l], out_dtype=BF16,
                    name=n + "ffn_up")
        fcw = _pad_rows(p["ffn_conv_w"][l], SUBLANE)
        fcb = _row(p["ffn_conv_b"][l])
        keys, rd = weight_rider(l, "ffn_gate_fwd")
        (hff, uc), routs = _ffn_gate_fwd(u, fcw, fcb, name=n + "ffn_gate_fwd", rider=rd)
        land_weights(keys, routs)
        if l < DEPTH - 1:
            r2, x2 = _hosted(_mm, weight_rider(l, "ffn_down"), land_weights, hff, big_w["ffn_w_down"][l], res=x1,
                             res_scale=ALPHA, ln=(_row(p["ln2_g"][l]), _row(p["ln2_b"][l])), name=n + "ffn_down")
        else:
            r2, x2 = _mm(hff, big_w["ffn_w_down"][l], res=x1, res_scale=ALPHA, name=n + "ffn_down"), None
        saved.append(dict(q=q, xs=xs, hin=hin, s5_y1=s5_y1, s5_u2=s5_u2, s5_x=s5_x, s5_pows=s5_pows, cvw=cvw, cv_c=cv_c, lcw=lcw, lru_xc=lru_xc,
                          lru_h=lru_h, kbig=kbig, vbig=vbig, kv_vjp=kv_vjp, mix=mix, r1=r1, x1=x1, u=u, uc=uc, fcw=fcw,
                          hff=hff, r2=r2))
        xs = x2

    top = DEPTH - 1
    dr_top, dg_top, db_top, loss_blk = _loss_ln_bwd(saved[top]["r2"], _row(p["ln2_g"][top]), _row(p["ln2_b"][top]), target,
                                                     name="loss_ln_bwd")
    loss = loss_blk[0, 0]
    dx = None

    for l in reversed(range(DEPTH)):
        sv = saved[l]
        q = sv["q"]
        n = f"l{l}_"
        g = {}
        if l == top:
            dr2, g["ln2_g"], g["ln2_b"] = dr_top, dg_top, db_top
        else:
            dr2, g["ln2_g"], g["ln2_b"], _ = _ln_bwd(sv["r2"], dx, _row(p["ln2_g"][l]), name=n + "ln2_bwd")
        big_grad("ffn_w_down", l, _hosted(_mm_tn, grad_rider(l, "dw_down"), land_grads, sv["hff"], dr2, out_dtype=gdt,
                                          name=n + "dw_down"))
        dhff = _hosted(_mm, grad_rider(l, "dhff"), land_grads, dr2, big_w["ffn_w_down"][l], trans_b=True,
                       out_dtype=BF16, name=n + "dhff")
        keys, rd = grad_rider(l, "ffn_gate_bwd")
        (du, dfw, g["ffn_conv_b"]), routs = _ffn_gate_bwd(sv["u"], sv["uc"], dhff, sv["fcw"], name=n + "ffn_gate_bwd",
                                                          rider=rd)
        land_grads(keys, routs)
        g["ffn_conv_w"] = dfw[:FFN_CONV_WIDTH]
        if dist:
            ready[("ffn_w_up", l)] = (_hosted(_mm_tn, grad_rider(l, "dw_up"), land_grads, sv["x1"], du, out_dtype=gdt,
                                              dev_cols=du.shape[1] // N_DEV, name=n + "dw_up"), "lead")
        else:
            big_grad("ffn_w_up", l, _mm_tn(sv["x1"], du, name=n + "dw_up"))
        dx1 = _hosted(_mm, grad_rider(l, "dx1"), land_grads, du, big_w["ffn_w_up"][l], trans_b=True, res=dr2,
                      res_scale=ALPHA, name=n + "dx1")
        dr1, g["ln1_g"], g["ln1_b"], g["b_out"] = _ln_bwd(sv["r1"], dx1, _row(p["ln1_g"][l]), name=n + "ln1_bwd")
        big_grad("w_out", l, _mm_tn(sv["mix"], dr1, out_dtype=gdt, name=n + "dw_out"))
        dmix = _mm(dr1, big_w["w_out"][l], trans_b=True, name=n + "dmix")

        hin = sv["hin"]
        keys, rd = grad_rider(l, "cv_bwd")
        (dh, g["cv_w_pw"], dcw, g["cv_b_pw"], g["cv_gn_g"], g["cv_gn_b"], g["cv_b"]), routs = _cv_bwd(
            hin, sv["cv_c"], dmix, sv["cvw"], _row(p["cv_gn_g"][l]), _row(p["cv_gn_b"][l]), mavg, p["cv_w_pw"][l],
            name=n + "cv_bwd", rider=rd)
        land_grads(keys, routs)
        g["cv_w"] = dcw[:CONV_WIDTH]
        dh, dwr, dwi, dlcw, g["lru_b_r"], g["lru_b_i"], dsp, g["lru_conv_b"] = _lru_bwd(
            hin, sv["lru_xc"], sv["lru_h"], dmix, sv["lcw"], q["wr"], _row(p["lru_b_r"][l]), q["wi"],
            _row(p["lru_b_i"][l]), q["sp"], dh, name=n + "lru_bwd")
        g["lru_conv_w"] = dlcw[:LRU_CONV_WIDTH]
        g["lru_w_r"], g["lru_w_i"] = q["lru_w_vjp"]((dwr, dwi))
        (g["lru_lam"],) = q["sp_vjp"](dsp)
        dy1, g["s5_w_glu"], g["s5_b_glu"] = _s5_glu_bwd(sv["s5_y1"], dmix, p["s5_w_glu"][l], _row(p["s5_b_glu"][l]),
                                                        name=n + "s5_glu_bwd")
        s5_du2, *s5_dmats = _s5_core_bwd(sv["s5_u2"], _s5_to_chunks(dy1, 0, s5_perm, name=n + "s5_din"), sv["s5_x"],
                                         *q["s5_mats"], sv["s5_pows"], name=n + "s5_core_bwd")
        (g["s5_lam_re"], g["s5_lam_im"], g["s5_log_dt"], g["s5_b_re"], g["s5_b_im"], g["s5_c_re"], g["s5_c_im"],
         g["s5_d"]) = q["s5_vjp"](tuple(s5_dmats))
        dh, dkbig, dvbig = _attn_bwd(hin, dmix, sv["kbig"], sv["vbig"],
                                     _s5_from_chunks(s5_du2, s5_perm, name=n + "s5_dout"), dh, name=n + "attn_bwd")
        (dkv,) = sv["kv_vjp"]((dkbig, dvbig))
        big_grad("attn_w_kv", l, _mm_tn(mem, dkv, out_dtype=gdt, name=n + "dw_kv"))

        if dist:
            ready[("ssh", l)] = (_pack_lead([_split_shards(g[k], SHARDED[k] - 1) for k in SMALL_SHARDED], F32), "lead")
        gw_in, g["b_in"] = _hosted(_mm_tn, grad_rider(l, "dw_in"), land_grads, sv["xs"], dh, colsum=True, out_dtype=gdt,
                                   name=n + "dw_in")
        big_grad("w_in", l, gw_in)
        if dist:
            g["b_in"] = _perm_in_cols(g["b_in"], inverse=True)
            ready[("rep", l)] = (_pack([g[k] for k in REP_LAYERED], F32), "all")
        else:
            for k, v in g.items():
                small.setdefault(k, [None] * DEPTH)[l] = v.reshape(p[k].shape[1:])
        dx = _hosted(_mm, grad_rider(l, "dxs"), land_grads, dh, big_w["w_in"][l], trans_b=True, res=dr1,
                     res_scale=ALPHA, name=n + "dxs")

    keys, rd = grad_rider(0, "ln_in_bwd")
    if rd is None:
        grad_x, dgi, dbi, _ = _ln_bwd(x, dx, _row(p["ln_in_g"]), name="ln_in_bwd")
    else:
        (grad_x, dgi, dbi, _), routs = _ln_bwd(x, dx, _row(p["ln_in_g"]), name="ln_in_bwd", rider=rd)
        land_grads(keys, routs)
    out = {k: jnp.stack(v, axis=0) for k, v in small.items()}
    out["ln_in_g"], out["ln_in_b"] = dgi.reshape(-1), dbi.reshape(-1)
    return loss, grad_x, out, ((recv, ready) if dist else big_g)


def kernel(x, mem, ln_in_g, ln_in_b, w_in, b_in, s5_lam_re, s5_lam_im, s5_log_dt, s5_b_re, s5_b_im, s5_c_re, s5_c_im, s5_d, s5_w_glu, s5_b_glu, cv_w, cv_b, cv_gn_g, cv_gn_b, cv_w_pw, cv_b_pw, lru_conv_w, lru_conv_b, lru_w_r, lru_b_r, lru_w_i, lru_b_i, lru_lam, attn_w_kv, w_out, b_out, ln1_g, ln1_b, ffn_w_up, ffn_conv_w, ffn_conv_b, ffn_w_down, ln2_g, ln2_b, loss_target, m_ln_in_g, m_ln_in_b, m_w_in, m_b_in, m_s5_lam_re, m_s5_lam_im, m_s5_log_dt, m_s5_b_re, m_s5_b_im, m_s5_c_re, m_s5_c_im, m_s5_d, m_s5_w_glu, m_s5_b_glu, m_cv_w, m_cv_b, m_cv_gn_g, m_cv_gn_b, m_cv_w_pw, m_cv_b_pw, m_lru_conv_w, m_lru_conv_b, m_lru_w_r, m_lru_b_r, m_lru_w_i, m_lru_b_i, m_lru_lam, m_attn_w_kv, m_w_out, m_b_out, m_ln1_g, m_ln1_b, m_ffn_w_up, m_ffn_conv_w, m_ffn_conv_b, m_ffn_w_down, m_ln2_g, m_ln2_b, v_ln_in_g, v_ln_in_b, v_w_in, v_b_in, v_s5_lam_re, v_s5_lam_im, v_s5_log_dt, v_s5_b_re, v_s5_b_im, v_s5_c_re, v_s5_c_im, v_s5_d, v_s5_w_glu, v_s5_b_glu, v_cv_w, v_cv_b, v_cv_gn_g, v_cv_gn_b, v_cv_w_pw, v_cv_b_pw, v_lru_conv_w, v_lru_conv_b, v_lru_w_r, v_lru_b_r, v_lru_w_i, v_lru_b_i, v_lru_lam, v_attn_w_kv, v_w_out, v_b_out, v_ln1_g, v_ln1_b, v_ffn_w_up, v_ffn_conv_w, v_ffn_conv_b, v_ffn_w_down, v_ln2_g, v_ln2_b):
    args = locals()
    w = {n: args[n] for n in WEIGHTS}
    mom = {n: args["m_" + n] for n in WEIGHTS}
    var = {n: args["v_" + n] for n in WEIGHTS}

    shards = {n: w[n].astype(BF16) for n in BIG}
    half_rows = shards["ffn_w_up"].shape[1] // 2
    shards["ffn_w_up#a"] = [shards["ffn_w_up"][0, :half_rows]]
    shards["ffn_w_up#b"] = [shards["ffn_w_up"][0, half_rows:]]
    shards["small_pack"] = [_pack([w[n] for n in SMALL_SHARDED], F32)]
    small_shapes = [w[n].shape for n in SMALL_SHARDED]

    def unpack_small(gathered):
        out = {n: _join_shards(st, SHARDED[n]) for n, st in zip(SMALL_SHARDED, _unpack(gathered, small_shapes, lead=True))}
        for n in ("s5_w_glu", "cv_w_pw"):
            out[n] = out[n].astype(BF16)
        return out

    big_w = {n: [None] * DEPTH for n in BIG}
    p = {n: w[n] for n in REPLICATED}
    p["b_in"] = _perm_in_cols(p["b_in"])

    loss, grad_x, g_small, (recv, ready) = _local_step(x[0], mem[0], loss_target[0], p, big_w, shards, unpack_small)
    loss = lax.psum(loss, ("x", "y", "c"))

    left = list(ready)
    rider = _Rider([ready[k][0] for k in left] + [_pack([g_small["ln_in_g"], g_small["ln_in_b"]], F32)],
                   [ready[k][1] for k in left] + ["all"])
    got = _exchange(rider, name="exchange_grads")
    for k, r in zip(left, got):
        recv[k] = r

    res = [dict(), dict(), dict(), dict()]
    for n in BIG:
        outs = None
        for l in range(DEPTH):
            outs = _adamw_layer(recv[(n, l)], w[n], mom[n], var[n], l, outs, name=f"adamw_{n}_l{l}")
        for kind in range(4):
            res[kind][n] = outs[kind]
    for names, key, tag in ((SMALL_SHARDED, "ssh", "adamw_small_sharded"), (REP_LAYERED, "rep", "adamw_replicated")):
        gstack = jnp.concatenate([recv[(key, l)] for l in range(DEPTH)], axis=1)
        packs = [_pack_layers([t[n] for n in names], F32) for t in (w, mom, var)]
        rows = packs[0].shape[1]
        outs = _adamw(gstack, *[pk.reshape(DEPTH * rows, PACK_COLS) for pk in packs], name=tag)
        for kind in range(4):
            for n, a in zip(names, _unpack_layers(outs[kind].reshape(DEPTH, rows, PACK_COLS), [w[n].shape for n in names])):
                res[kind][n] = a
    ln_names = ("ln_in_g", "ln_in_b")
    outs = _adamw(got[len(left)], _pack([w[n] for n in ln_names], F32), _pack([mom[n] for n in ln_names], F32),
                  _pack([var[n] for n in ln_names], F32), name="adamw_ln_in")
    for kind in range(4):
        for n, a in zip(ln_names, _unpack(outs[kind], [w[n].shape for n in ln_names])):
            res[kind][n] = a
    return (loss, grad_x[None], *[res[0][n] for n in WEIGHTS], *[res[1][n] for n in WEIGHTS],
            *[res[2][n] for n in WEIGHTS], *[res[3][n] for n in WEIGHTS])
```

```python
import math

import jax
import jax.numpy as jnp
from jax import lax
from jax.experimental import pallas as pl
from jax.experimental.pallas import tpu as pltpu

F32 = jnp.float32
BF16 = jnp.bfloat16

D_MODEL = 1024
DEPTH = 2
D_GROUP = 256
N_IN_COLS = 6 * D_GROUP
S5_GROUPS = 16
S5_CH = 16
S5_STATE = 64
S5_LANES = S5_GROUPS * S5_STATE
CONV_WIDTH = 31
GN_GROUPS = 4
LRU_HEADS = 4
LRU_CONV_WIDTH = 4
LRU_C = 8.0
ATTN_HEADS = 4
ATTN_HEAD_DIM = 64
D_FF = 2816
FFN_CONV_WIDTH = 3
ALPHA = (2 * DEPTH) ** 0.25
LN_EPS = 1e-5
ADAM_LR, ADAM_B1, ADAM_B2, ADAM_EPS, ADAM_WD, ADAM_STEP = 0.001, 0.9, 0.999, 1e-08, 0.01, 10

N_DEV = 8
N_PEERS = N_DEV - 1
LANE = 128
SUBLANE = 8
VMEM_LIMIT = 56 * 1024 * 1024
PACK_COLS = 1024
PACK_ROW_BLOCK = 256

SHARDED = {
    "w_in": 2, "s5_w_glu": 1, "cv_w": 2, "cv_w_pw": 1, "lru_conv_w": 2, "attn_w_kv": 1,
    "w_out": 1, "ffn_w_up": 2, "ffn_conv_w": 2, "ffn_w_down": 1,
}
BIG = ("w_in", "attn_w_kv", "w_out", "ffn_w_up", "ffn_w_down")
SMALL_SHARDED = ("s5_w_glu", "cv_w", "cv_w_pw", "lru_conv_w", "ffn_conv_w")
MATMUL_WEIGHTS = ("w_in", "s5_w_glu", "cv_w_pw", "attn_w_kv", "w_out", "ffn_w_up", "ffn_w_down")
WEIGHTS = ['ln_in_g', 'ln_in_b', 'w_in', 'b_in', 's5_lam_re', 's5_lam_im', 's5_log_dt', 's5_b_re', 's5_b_im',
           's5_c_re', 's5_c_im', 's5_d', 's5_w_glu', 's5_b_glu', 'cv_w', 'cv_b', 'cv_gn_g', 'cv_gn_b', 'cv_w_pw',
           'cv_b_pw', 'lru_conv_w', 'lru_conv_b', 'lru_w_r', 'lru_b_r', 'lru_w_i', 'lru_b_i', 'lru_lam',
           'attn_w_kv', 'w_out', 'b_out', 'ln1_g', 'ln1_b', 'ffn_w_up', 'ffn_conv_w', 'ffn_conv_b', 'ffn_w_down',
           'ln2_g', 'ln2_b']
REPLICATED = [n for n in WEIGHTS if n not in SHARDED]
REP_LAST = ("ln_in_g", "ln_in_b", "b_in")
REP_LAYERED = [n for n in REPLICATED if n not in REP_LAST]

COL_CV_V, COL_CV_G, COL_LRU_G, COL_LRU_X, COL_S5, COL_Q = range(6)
IN_PERM = (1, 2, 3, 4, 0, 5)
MIX_S5, MIX_CV, MIX_LRU, MIX_ATTN = range(4)


_ANY = pl.BlockSpec(memory_space=pl.ANY)
_MESH = pl.DeviceIdType.MESH


def _cparams(n_axes):
    return pltpu.CompilerParams(dimension_semantics=("arbitrary",) * n_axes, vmem_limit_bytes=VMEM_LIMIT)


def _pick(n, cap):
    if n <= cap:
        return n
    best = None
    for t in range(LANE, cap + 1, LANE):
        if n % t == 0:
            best = t
    assert best is not None, (n, cap)
    return best


def _pick_rows(n, cap):
    best = None
    for t in range(SUBLANE, min(n, cap) + 1, SUBLANE):
        if n % t == 0:
            best = t
    assert best is not None, (n, cap)
    return best


def _full_spec(arr):
    nd = arr.ndim
    return pl.BlockSpec(arr.shape, lambda *_: (0,) * nd)


def _dot(a, b):
    return lax.dot_general(a.astype(BF16), b.astype(BF16), (((1,), (0,)), ((), ())), preferred_element_type=F32)


def _dot_nt(a, b):
    return lax.dot_general(a.astype(BF16), b.astype(BF16), (((1,), (1,)), ((), ())), preferred_element_type=F32)


def _dot_tn(a, b):
    return lax.dot_general(a.astype(BF16), b.astype(BF16), (((0,), (0,)), ((), ())), preferred_element_type=F32)


def _dot_hi(a, b):
    return jnp.dot(a, b, precision=lax.Precision.HIGHEST, preferred_element_type=F32)


def _colsum(x):
    return jnp.sum(x, axis=0, keepdims=True)


def _sigmoid(x):
    return 1.0 / (1.0 + jnp.exp(-x))


_GELU_K = math.sqrt(2.0 / math.pi)
_GELU_C = 0.044715


def _gelu(x):
    t = jnp.tanh(_GELU_K * (x + _GELU_C * x * x * x))
    return 0.5 * x * (1.0 + t)


def _gelu_and_grad(x):
    x2 = x * x
    t = jnp.tanh(_GELU_K * (x + _GELU_C * x2 * x))
    g = 0.5 * x * (1.0 + t)
    dg = 0.5 * (1.0 + t) + 0.5 * x * (1.0 - t * t) * (_GELU_K * (1.0 + 3.0 * _GELU_C * x2))
    return g, dg


def _neg_expm1(x):
    series = x * (1.0 + x * (0.5 + x * (1.0 / 6.0 + x * (1.0 / 24.0 + x * (1.0 / 120.0)))))
    return -jnp.where(jnp.abs(x) < 0.1, series, jnp.exp(x) - 1.0)


def _seq_tile(s, want):
    t = min(s, want)
    assert s % t == 0
    return t


class _Rider:
    def __init__(self, srcs, kinds):
        self.srcs, self.kinds = list(srcs), list(kinds)
        self.n = len(self.srcs)

    def out_shapes(self):
        shapes = []
        for x, kind in zip(self.srcs, self.kinds):
            if kind == "lead":
                shp = x.shape
            elif kind == "rows":
                shp = (N_DEV, x.shape[0] // N_DEV) + x.shape[1:]
            else:
                shp = (N_DEV,) + x.shape
            shapes.append(jax.ShapeDtypeStruct(shp, x.dtype))
        return shapes

    def scratch(self):
        return [pltpu.SemaphoreType.DMA((self.n * N_PEERS,)), pltpu.SemaphoreType.DMA((self.n * N_PEERS,)),
                pltpu.SemaphoreType.DMA((self.n,))]

    def _copies(self, x_refs, out_refs, sems):
        send_sems, recv_sems, local_sems = sems
        mx, my, mc = lax.axis_index("x"), lax.axis_index("y"), lax.axis_index("c")
        my_id = 4 * mx + 2 * my + mc

        def piece(i, dev):
            if self.kinds[i] == "lead":
                return x_refs[i].at[dev]
            if self.kinds[i] == "rows":
                r = x_refs[i].shape[0] // N_DEV
                return x_refs[i].at[pl.ds(pl.multiple_of(dev * r, SUBLANE), r)]
            return x_refs[i]

        mine = [pltpu.make_async_copy(piece(i, my_id), out_refs[i].at[my_id], local_sems.at[i]) for i in range(self.n)]
        copies = []
        for k in range(1, N_DEV):
            px, py, pc = mx ^ ((k >> 2) & 1), my ^ ((k >> 1) & 1), mc ^ (k & 1)
            for i in range(self.n):
                copies.append(pltpu.make_async_remote_copy(
                    src_ref=piece(i, 4 * px + 2 * py + pc), dst_ref=out_refs[i].at[my_id],
                    send_sem=send_sems.at[i * N_PEERS + k - 1], recv_sem=recv_sems.at[i * N_PEERS + k - 1],
                    device_id=(px, py, pc), device_id_type=_MESH))
        return mine, copies

    def start(self, x_refs, out_refs, sems):
        mine, copies = self._copies(x_refs, out_refs, sems)
        for cp in mine + copies:
            cp.start()

    def wait(self, x_refs, out_refs, sems):
        mine, copies = self._copies(x_refs, out_refs, sems)
        for cp in copies:
            cp.wait_recv()
        for cp in copies:
            cp.wait_send()
        for cp in mine:
            cp.wait()


def _call(body, *, grid, ins, in_specs, outs, out_specs, scratch=(), aliases=None, name, rider=None):
    n_axes = len(grid)
    common = dict(grid=grid, input_output_aliases=aliases or {}, compiler_params=_cparams(n_axes), name=name)
    if rider is None:
        res = pl.pallas_call(body, in_specs=list(in_specs), out_specs=list(out_specs), out_shape=list(outs),
                             scratch_shapes=list(scratch), **common)(*ins)
        return list(res), []
    n_in, n_out, n_scr, nr = len(ins), len(outs), len(scratch), rider.n

    def wrapped(*refs):
        pos = [0]

        def take(k):
            part = refs[pos[0]:pos[0] + k]
            pos[0] += k
            return part

        a_in, r_in, a_out, r_out, a_scr, sems = take(n_in), take(nr), take(n_out), take(nr), take(n_scr), take(3)
        first = last = None
        for ax in range(n_axes):
            pid = pl.program_id(ax)
            f, l = pid == 0, pid == grid[ax] - 1
            first = f if first is None else jnp.logical_and(first, f)
            last = l if last is None else jnp.logical_and(last, l)

        @pl.when(first)
        def _():
            rider.start(r_in, r_out, sems)

        body(*a_in, *a_out, *a_scr)

        @pl.when(last)
        def _():
            rider.wait(r_in, r_out, sems)

    res = pl.pallas_call(
        wrapped, in_specs=list(in_specs) + [_ANY] * nr, out_specs=list(out_specs) + [_ANY] * nr,
        out_shape=list(outs) + rider.out_shapes(), scratch_shapes=list(scratch) + rider.scratch(), **common)(*ins, *rider.srcs)
    return list(res[:n_out]), list(res[n_out:])


def _block_mask(n_blocks, block_rows, block_cols):
    r = jnp.arange(n_blocks * block_rows) // block_rows
    c = jnp.arange(n_blocks * block_cols) // block_cols
    return (r[:, None] == c[None, :]).astype(F32)


def _mm(a, b, *, bias=None, res=None, res_scale=1.0, trans_b=False, out_dtype=F32, ln=None, name, rider=None):
    m, kdim = a.shape
    n = b.shape[0] if trans_b else b.shape[1]
    tm = _seq_tile(m, 1024)
    tn = _pick(n, 1408)
    tk = _pick(kdim, 1536)
    nk = kdim // tk
    has_bias, has_res, has_ln = bias is not None, res is not None, ln is not None
    assert not has_ln or tn == n

    def body(*refs):
        a_ref, b_ref = refs[0], refs[1]
        pos = 2
        bias_ref = res_ref = g_ref = beta_ref = x_ref = None
        if has_bias:
            bias_ref = refs[pos]
            pos += 1
        if has_res:
            res_ref = refs[pos]
            pos += 1
        if has_ln:
            g_ref, beta_ref = refs[pos], refs[pos + 1]
            pos += 2
        o_ref = refs[pos]
        pos += 1
        if has_ln:
            x_ref = refs[pos]
            pos += 1
        acc_ref = refs[pos]
        k = pl.program_id(2)

        @pl.when(k == 0)
        def _():
            acc_ref[...] = jnp.zeros_like(acc_ref)

        if trans_b:
            acc_ref[...] += _dot_nt(a_ref[...], b_ref[...])
        else:
            acc_ref[...] += _dot(a_ref[...], b_ref[...])

        @pl.when(k == nk - 1)
        def _():
            r = acc_ref[...]
            if has_bias:
                r = r + bias_ref[...]
            if has_res:
                r = r + res_scale * res_ref[...]
            o_ref[...] = r.astype(out_dtype)
            if has_ln:
                xc = r - jnp.mean(r, axis=1, keepdims=True)
                var = jnp.mean(xc * xc, axis=1, keepdims=True)
                x_ref[...] = xc * lax.rsqrt(var + LN_EPS) * g_ref[...] + beta_ref[...]

    ins = [a, b]
    in_specs = [pl.BlockSpec((tm, tk), lambda i, j, k: (i, k)),
                pl.BlockSpec((tn, tk), lambda i, j, k: (j, k)) if trans_b
                else pl.BlockSpec((tk, tn), lambda i, j, k: (k, j))]
    if has_bias:
        ins.append(bias)
        in_specs.append(pl.BlockSpec((1, tn), lambda i, j, k: (0, j)))
    if has_res:
        ins.append(res)
        in_specs.append(pl.BlockSpec((tm, tn), lambda i, j, k: (i, j)))
    if has_ln:
        ins += list(ln)
        in_specs += [pl.BlockSpec((1, tn), lambda i, j, k: (0, j))] * 2
    tile = pl.BlockSpec((tm, tn), lambda i, j, k: (i, j))
    outs, routs = _call(
        body, grid=(m // tm, n // tn, nk), ins=ins, in_specs=in_specs,
        outs=[jax.ShapeDtypeStruct((m, n), out_dtype)] + ([jax.ShapeDtypeStruct((m, n), F32)] if has_ln else []),
        out_specs=[tile] * (2 if has_ln else 1),
        scratch=[pltpu.VMEM((tm, tn), F32)], name=name, rider=rider)
    out = tuple(outs) if has_ln else outs[0]
    return out if rider is None else (out, routs)


def _mm_tn(a, b, *, colsum=False, out_dtype=F32, dev_cols=None, name, rider=None):
    s, ka = a.shape
    nb = b.shape[1]
    ts = _seq_tile(s, 512)
    tka = _pick(ka, 1408)
    tnb = _pick(nb, 1408)
    nk = s // ts
    assert not colsum or tka == ka
    per_tile = 1 if dev_cols is None else tnb // dev_cols
    assert dev_cols is None or tnb == per_tile * dev_cols

    def body(a_ref, b_ref, o_ref, *rest):
        cs_ref = rest[0] if colsum else None
        acc_ref = rest[-1]
        k = pl.program_id(2)

        @pl.when(k == 0)
        def _():
            acc_ref[...] = jnp.zeros_like(acc_ref)
            if colsum:
                cs_ref[...] = jnp.zeros_like(cs_ref)

        bv = b_ref[...]
        acc_ref[...] += _dot_tn(a_ref[...], bv)
        if colsum:
            cs_ref[...] += _colsum(bv.astype(F32))

        @pl.when(k == nk - 1)
        def _():
            if dev_cols is None:
                o_ref[...] = acc_ref[...].astype(out_dtype)
            else:
                for d in range(per_tile):
                    o_ref[d] = acc_ref[:, d * dev_cols:(d + 1) * dev_cols].astype(out_dtype)

    if dev_cols is None:
        main_shape, main_spec = (ka, nb), pl.BlockSpec((tka, tnb), lambda i, j, k: (i, j))
    else:
        main_shape = (nb // dev_cols, ka, dev_cols)
        main_spec = pl.BlockSpec((per_tile, tka, dev_cols), lambda i, j, k: (j, i, 0))
    outs, routs = _call(
        body, grid=(ka // tka, nb // tnb, nk), ins=[a, b],
        in_specs=[pl.BlockSpec((ts, tka), lambda i, j, k: (k, i)), pl.BlockSpec((ts, tnb), lambda i, j, k: (k, j))],
        outs=[jax.ShapeDtypeStruct(main_shape, out_dtype)] + ([jax.ShapeDtypeStruct((1, nb), F32)] if colsum else []),
        out_specs=[main_spec] + ([pl.BlockSpec((1, tnb), lambda i, j, k: (0, j))] if colsum else []),
        scratch=[pltpu.VMEM((tka, tnb), F32)], name=name, rider=rider)
    out = tuple(outs) if colsum else outs[0]
    return out if rider is None else (out, routs)


def _ln_fwd(r, g, b, *, name, rider=None):
    s, d = r.shape
    ts = _seq_tile(s, 512)

    def body(r_ref, g_ref, b_ref, o_ref):
        x = r_ref[...]
        mu = jnp.mean(x, axis=1, keepdims=True)
        xc = x - mu
        var = jnp.mean(xc * xc, axis=1, keepdims=True)
        o_ref[...] = xc * lax.rsqrt(var + LN_EPS) * g_ref[...] + b_ref[...]

    (out,), routs = _call(
        body, grid=(s // ts,), ins=[r, g, b],
        in_specs=[pl.BlockSpec((ts, d), lambda i: (i, 0)), _full_spec(g), _full_spec(b)],
        out_specs=[pl.BlockSpec((ts, d), lambda i: (i, 0))], outs=[jax.ShapeDtypeStruct((s, d), F32)],
        name=name, rider=rider)
    return out if rider is None else (out, routs)


def _ln_bwd(r, dy, g, *, name, rider=None):
    s, d = r.shape
    ts = _seq_tile(s, 512)

    def body(r_ref, dy_ref, g_ref, dr_ref, dg_ref, db_ref, ds_ref):
        @pl.when(pl.program_id(0) == 0)
        def _():
            dg_ref[...] = jnp.zeros_like(dg_ref)
            db_ref[...] = jnp.zeros_like(db_ref)
            ds_ref[...] = jnp.zeros_like(ds_ref)

        x = r_ref[...]
        dy = dy_ref[...]
        mu = jnp.mean(x, axis=1, keepdims=True)
        xc = x - mu
        var = jnp.mean(xc * xc, axis=1, keepdims=True)
        rstd = lax.rsqrt(var + LN_EPS)
        xh = xc * rstd
        dxh = dy * g_ref[...]
        m1 = jnp.mean(dxh, axis=1, keepdims=True)
        m2 = jnp.mean(dxh * xh, axis=1, keepdims=True)
        dr = rstd * (dxh - m1 - xh * m2)
        dr_ref[...] = dr
        dg_ref[...] += _colsum(dy * xh)
        db_ref[...] += _colsum(dy)
        ds_ref[...] += _colsum(dr)

    vec = jax.ShapeDtypeStruct((1, d), F32)
    vspec = pl.BlockSpec((1, d), lambda i: (0, 0))
    outs, routs = _call(
        body, grid=(s // ts,), ins=[r, dy, g],
        in_specs=[pl.BlockSpec((ts, d), lambda i: (i, 0)), pl.BlockSpec((ts, d), lambda i: (i, 0)), _full_spec(g)],
        out_specs=[pl.BlockSpec((ts, d), lambda i: (i, 0)), vspec, vspec, vspec],
        outs=[jax.ShapeDtypeStruct((s, d), F32), vec, vec, vec], name=name, rider=rider)
    return outs if rider is None else (outs, routs)


def _loss_ln_bwd(r, g, b, target, *, name):
    s, d = r.shape
    ts = _seq_tile(s, 512)

    def body(r_ref, g_ref, b_ref, t_ref, dr_ref, dg_ref, db_ref, l_ref):
        @pl.when(pl.program_id(0) == 0)
        def _():
            dg_ref[...] = jnp.zeros_like(dg_ref)
            db_ref[...] = jnp.zeros_like(db_ref)
            l_ref[...] = jnp.zeros_like(l_ref)

        x = r_ref[...]
        gam = g_ref[...]
        xc = x - jnp.mean(x, axis=1, keepdims=True)
        var = jnp.mean(xc * xc, axis=1, keepdims=True)
        rstd = lax.rsqrt(var + LN_EPS)
        xh = xc * rstd
        e = xh * gam + b_ref[...] - t_ref[...]
        part = jnp.sum(jnp.sum(e * e, axis=1, keepdims=True), axis=0, keepdims=True) * (0.5 / d)
        l_ref[...] += jnp.broadcast_to(part, l_ref.shape)
        dy = e * (1.0 / d)
        dxh = dy * gam
        m1 = jnp.mean(dxh, axis=1, keepdims=True)
        m2 = jnp.mean(dxh * xh, axis=1, keepdims=True)
        dr_ref[...] = rstd * (dxh - m1 - xh * m2)
        dg_ref[...] += _colsum(dy * xh)
        db_ref[...] += _colsum(dy)

    vec = jax.ShapeDtypeStruct((1, d), F32)
    vspec = pl.BlockSpec((1, d), lambda i: (0, 0))
    tile = pl.BlockSpec((ts, d), lambda i: (i, 0))
    return pl.pallas_call(
        body, grid=(s // ts,), in_specs=[tile, _full_spec(g), _full_spec(b), tile],
        out_specs=[tile, vspec, vspec, pl.BlockSpec((SUBLANE, LANE), lambda i: (0, 0))],
        out_shape=[jax.ShapeDtypeStruct((s, d), F32), vec, vec, jax.ShapeDtypeStruct((SUBLANE, LANE), F32)],
        compiler_params=_cparams(1), name=name)(r, g, b, target)


SCAN_CHUNK = 32


def _cscan_levels(bufs, apow_ref, t, pad, *, reverse):
    half = bufs[0].shape[1] // 2
    ch = min(SCAN_CHUNK, t)
    nlev = t.bit_length() - 1
    assert (1 << nlev) == t
    for k in range(nlev):
        d = 1 << k
        src, dst = bufs[k % 2], bufs[(k + 1) % 2]

        def chunk(c, carry, src=src, dst=dst, d=d, k=k):
            ar = apow_ref[k:k + 1, :half]
            ai = apow_ref[k:k + 1, half:]
            if reverse:
                ai = -ai
            r0 = pl.multiple_of(c * ch, ch)
            cur = src[pl.ds(pad + r0, ch), :]
            if d >= SUBLANE:
                off = pad + d if reverse else pad - d
                sh = src[pl.ds(off + r0, ch), :]
            elif reverse:
                blk = src[pl.ds(pad + r0, ch + SUBLANE), :]
                sh = pltpu.roll(blk, ch + SUBLANE - d, axis=0)[:ch, :]
            else:
                blk = src[pl.ds(pad - SUBLANE + r0, ch + SUBLANE), :]
                sh = pltpu.roll(blk, d, axis=0)[SUBLANE:, :]
            sre, sim = sh[:, :half], sh[:, half:]
            dst[pl.ds(pad + r0, ch), :half] = cur[:, :half] + ar * sre - ai * sim
            dst[pl.ds(pad + r0, ch), half:] = cur[:, half:] + ar * sim + ai * sre
            return carry

        lax.fori_loop(0, t // ch, chunk, 0)
    return nlev % 2


def _rscan_levels(abufs, bbufs, t, pad, *, reverse):
    nlev = t.bit_length() - 1
    assert (1 << nlev) == t
    for k in range(nlev):
        d = 1 << k
        asrc, adst = abufs[k % 2], abufs[(k + 1) % 2]
        bsrc, bdst = bbufs[k % 2], bbufs[(k + 1) % 2]
        off = pad + d if reverse else pad - d
        a = asrc[pad:pad + t, :]
        bdst[pad:pad + t, :] = a * bsrc[off:off + t, :] + bsrc[pad:pad + t, :]
        if k < nlev - 1:
            adst[pad:pad + t, :] = a * asrc[off:off + t, :]
    return nlev % 2


S5_CHUNK = 16
S5_SG = S5_GROUPS // 2
S5_SG_IN = 2 * S5_CHUNK * S5_CH
S5_SG_ST = 2 * S5_STATE


S5_HALF_SGS = S5_SG // 2
S5_HALF_IN = S5_HALF_SGS * S5_SG_IN


def _s5_perm():
    idx = jnp.arange(S5_HALF_IN)
    step, grp, chan = idx // LANE, (idx % LANE) // S5_CH, idx % S5_CH
    col = (grp // 2) * S5_SG_IN + (grp % 2) * (S5_CHUNK * S5_CH) + step * S5_CH + chan
    return (col[:, None] == idx[None, :]).astype(BF16)


def _s5_to_chunks(x, col_block, perm, *, name):
    s = x.shape[0]
    nb = s // S5_CHUNK

    def body(x_ref, perm_ref, o_ref):
        tok = jnp.concatenate([x_ref[pl.ds(t, nb, stride=S5_CHUNK), :].astype(BF16) for t in range(S5_CHUNK)], axis=1)
        grouped = _dot(tok, perm_ref[...]).astype(BF16)
        for k in range(S5_HALF_SGS):
            o_ref[k] = grouped[:, k * S5_SG_IN:(k + 1) * S5_SG_IN]

    return pl.pallas_call(
        body, grid=(2,),
        in_specs=[pl.BlockSpec((s, LANE), lambda h: (0, col_block + h)), _full_spec(perm)],
        out_specs=pl.BlockSpec((S5_HALF_SGS, nb, S5_SG_IN), lambda h: (h, 0, 0)),
        out_shape=jax.ShapeDtypeStruct((S5_SG, nb, S5_SG_IN), BF16),
        compiler_params=_cparams(1), name=name)(x, perm)


def _s5_from_chunks(y, perm, *, name):
    _, nb, _ = y.shape

    def body(y_ref, perm_ref, o_ref):
        grouped = jnp.concatenate([y_ref[k] for k in range(S5_HALF_SGS)], axis=1)
        hi = grouped.astype(BF16)
        lo = (grouped - hi.astype(F32)).astype(BF16)
        tok = _dot_nt(hi, perm_ref[...]) + _dot_nt(lo, perm_ref[...])
        for t in range(S5_CHUNK):
            o_ref[pl.ds(t, nb, stride=S5_CHUNK), :] = tok[:, t * LANE:(t + 1) * LANE]

    return pl.pallas_call(
        body, grid=(2,),
        in_specs=[pl.BlockSpec((S5_HALF_SGS, nb, S5_SG_IN), lambda h: (h, 0, 0)), _full_spec(perm)],
        out_specs=pl.BlockSpec((nb * S5_CHUNK, LANE), lambda h: (0, h)),
        out_shape=jax.ShapeDtypeStruct((nb * S5_CHUNK, D_GROUP), F32),
        compiler_params=_cparams(1), name=name)(y, perm)


def _s5_core_fwd(u2, m2, pre, pim, qre, qim, a16, *, name):
    sg, nb, nin = u2.shape
    st2 = 2 * S5_SG_ST
    pad = nb // 2

    def body(u_ref, m_ref, pre_ref, pim_ref, qre_ref, qim_ref, a_ref, y_ref, x_ref, buf0, buf1):
        @pl.when(pl.program_id(0) == 0)
        def _():
            buf0[0:pad, :] = jnp.zeros((pad, st2), F32)
            buf1[0:pad, :] = jnp.zeros((pad, st2), F32)

        u = u_ref[...]
        buf0[pad:pad + nb, :S5_SG_ST] = _dot(u, pre_ref[...])
        buf0[pad:pad + nb, S5_SG_ST:] = _dot(u, pim_ref[...])
        xbuf = (buf0, buf1)[_cscan_levels((buf0, buf1), a_ref, nb, pad, reverse=False)]
        x_ref[...] = xbuf[pad:pad + nb, :]
        xprev = xbuf[pad - 1:pad - 1 + nb, :]
        y_ref[...] = _dot(u, m_ref[...]) + _dot(xprev[:, :S5_SG_ST], qre_ref[...]) + _dot(xprev[:, S5_SG_ST:], qim_ref[...])

    ins = [u2, m2, pre, pim, qre, qim, a16]
    return pl.pallas_call(
        body, grid=(sg,), in_specs=[pl.BlockSpec((None,) + a.shape[1:], lambda i: (i, 0, 0)) for a in ins],
        out_specs=[pl.BlockSpec((None, nb, nin), lambda i: (i, 0, 0)), pl.BlockSpec((None, nb, st2), lambda i: (i, 0, 0))],
        out_shape=[jax.ShapeDtypeStruct((sg, nb, nin), F32), jax.ShapeDtypeStruct((sg, nb, st2), F32)],
        scratch_shapes=[pltpu.VMEM((pad + nb, st2), F32), pltpu.VMEM((pad + nb, st2), F32)],
        compiler_params=_cparams(1), name=name)(*ins)


def _s5_core_bwd(u2, dy2, x_all, m2, pre, pim, qre, qim, a16, *, name):
    sg, nb, nin = u2.shape
    half = S5_SG_ST
    st2 = 2 * half
    pad = nb // 2

    def body(u_ref, dy_ref, x_ref, m_ref, pre_ref, pim_ref, qre_ref, qim_ref, a_ref,
             du_ref, dm_ref, dpre_ref, dpim_ref, dqre_ref, dqim_ref, da_ref, buf2, buf3, xp):
        @pl.when(pl.program_id(0) == 0)
        def _():
            buf2[nb:nb + pad, :] = jnp.zeros((pad, st2), F32)
            buf3[nb:nb + pad, :] = jnp.zeros((pad, st2), F32)
            xp[0:SUBLANE, :] = jnp.zeros((SUBLANE, st2), F32)

        u = u_ref[...]
        dy = dy_ref[...]
        dm_ref[...] = _dot_tn(u, dy)
        xp[SUBLANE:SUBLANE + nb, :] = x_ref[...]
        xprev = xp[SUBLANE - 1:SUBLANE - 1 + nb, :]
        xre, xim = xprev[:, :half], xprev[:, half:]
        dqre_ref[...] = _dot_tn(xre, dy)
        dqim_ref[...] = _dot_tn(xim, dy)
        buf2[0:nb, :half] = _dot_nt(dy, qre_ref[...])
        buf2[0:nb, half:] = _dot_nt(dy, qim_ref[...])
        mbuf = (buf2, buf3)[_cscan_levels((buf2, buf3), a_ref, nb, 0, reverse=True)]
        lam = mbuf[1:1 + nb, :]
        lre, lim = lam[:, :half], lam[:, half:]
        dpre_ref[...] = _dot_tn(u, lre)
        dpim_ref[...] = _dot_tn(u, lim)
        du_ref[...] = _dot_nt(dy, m_ref[...]) + _dot_nt(lre, pre_ref[...]) + _dot_nt(lim, pim_ref[...])
        da_ref[:, :half] = _colsum(lre * xre + lim * xim)
        da_ref[:, half:] = _colsum(lim * xre - lre * xim)

    ins = [u2, dy2, x_all, m2, pre, pim, qre, qim, a16]
    outs = [jax.ShapeDtypeStruct((sg, nb, nin), F32)] + [jax.ShapeDtypeStruct(a.shape, F32) for a in (m2, pre, pim, qre, qim)] + \
           [jax.ShapeDtypeStruct((sg, 1, st2), F32)]
    return pl.pallas_call(
        body, grid=(sg,), in_specs=[pl.BlockSpec((None,) + a.shape[1:], lambda i: (i, 0, 0)) for a in ins],
        out_specs=[pl.BlockSpec((None,) + o.shape[1:], lambda i: (i, 0, 0)) for o in outs], out_shape=outs,
        scratch_shapes=[pltpu.VMEM((nb + pad, st2), F32), pltpu.VMEM((nb + pad, st2), F32),
                        pltpu.VMEM((SUBLANE + nb, st2), F32)],
        compiler_params=_cparams(1), name=name)(*ins)


def _s5_glu_fwd(y1, wglu, bglu, *, name, rider=None):
    s = y1.shape[0]
    t = _seq_tile(s, 512)

    def body(y1_ref, wglu_ref, bglu_ref, out_ref):
        y2 = _gelu(y1_ref[...])
        out_ref[...] = (y2 * _sigmoid(_dot(y2, wglu_ref[...]) + bglu_ref[...])).astype(BF16)

    return _call(
        body, grid=(s // t,), ins=[y1, wglu, bglu],
        in_specs=[pl.BlockSpec((t, D_GROUP), lambda i: (i, 0)), _full_spec(wglu), _full_spec(bglu)],
        out_specs=[pl.BlockSpec((t, D_GROUP), lambda i: (i, MIX_S5))], outs=[jax.ShapeDtypeStruct((s, D_MODEL), BF16)],
        name=name, rider=rider)


def _s5_glu_bwd(y1, dmix, wglu, bglu, *, name):
    s = y1.shape[0]
    t = _seq_tile(s, 512)

    def body(y1_ref, do_ref, wglu_ref, bglu_ref, dy1_ref, dwglu_ref, dbglu_ref):
        @pl.when(pl.program_id(0) == 0)
        def _():
            dwglu_ref[...] = jnp.zeros_like(dwglu_ref)
            dbglu_ref[...] = jnp.zeros_like(dbglu_ref)

        dout = do_ref[...]
        y2, dgelu = _gelu_and_grad(y1_ref[...])
        sg = _sigmoid(_dot(y2, wglu_ref[...]) + bglu_ref[...])
        dz = dout * y2 * sg * (1.0 - sg)
        dwglu_ref[...] += _dot_tn(y2, dz)
        dbglu_ref[...] += _colsum(dz)
        dy1_ref[...] = (dout * sg + _dot_nt(dz, wglu_ref[...])) * dgelu

    outs = [jax.ShapeDtypeStruct((s, D_GROUP), F32), jax.ShapeDtypeStruct((D_GROUP, D_GROUP), F32),
            jax.ShapeDtypeStruct((1, D_GROUP), F32)]
    return pl.pallas_call(
        body, grid=(s // t,),
        in_specs=[pl.BlockSpec((t, D_GROUP), lambda i: (i, 0)), pl.BlockSpec((t, D_GROUP), lambda i: (i, MIX_S5)),
                  _full_spec(wglu), _full_spec(bglu)],
        out_specs=[pl.BlockSpec((t, D_GROUP), lambda i: (i, 0)), _full_spec(outs[1]), _full_spec(outs[2])],
        out_shape=outs, compiler_params=_cparams(1), name=name)(y1, dmix, wglu, bglu)


def _pair_blockdiag(x):
    g, r, c = x.shape
    x = x.reshape(g // 2, 2, r, c)
    z = jnp.zeros_like(x[:, 0])
    return jnp.concatenate([jnp.concatenate([x[:, 0], z], axis=2), jnp.concatenate([z, x[:, 1]], axis=2)], axis=1)


def _s5_chunk_map(lam_re, lam_im, log_dt, b_re, b_im, c_re, c_im, d_skip):
    g, n, c, lc = S5_GROUPS, S5_STATE, S5_CH, S5_CHUNK
    dt = jnp.exp(log_dt)[:, None]
    mag, ang = lam_re * dt, lam_im * dt
    j = jnp.arange(lc + 1, dtype=F32)[:, None, None]
    pw_mag = jnp.exp(j * mag)
    pw_re, pw_im = pw_mag * jnp.cos(j * ang), pw_mag * jnp.sin(j * ang)
    a_re, a_im = pw_re[1], pw_im[1]
    den = lam_re * lam_re + lam_im * lam_im
    n_re = a_re - 1.0
    k_re = (n_re * lam_re + a_im * lam_im) / den
    k_im = (a_im * lam_re - n_re * lam_im) / den
    bb_re = k_re[..., None] * b_re - k_im[..., None] * b_im
    bb_im = k_re[..., None] * b_im + k_im[..., None] * b_re
    e_re = pw_re[:lc, :, :, None] * bb_re - pw_im[:lc, :, :, None] * bb_im
    e_im = pw_re[:lc, :, :, None] * bb_im + pw_im[:lc, :, :, None] * bb_re
    kern = jnp.einsum("gdn,jgnc->jgdc", c_re, e_re) - jnp.einsum("gdn,jgnc->jgdc", c_im, e_im)
    steps = jnp.arange(lc)
    lag = (steps[None, :, None] + steps[:, None, None] == steps[None, None, :]).astype(F32)
    m = jnp.einsum("jst,jgdc->gsctd", lag, kern).reshape(g, lc * c, lc * c)
    skip = jnp.tile(d_skip.reshape(g, 1, c), (1, lc, 1)).reshape(g, lc * c)
    m = m + jnp.eye(lc * c, dtype=F32)[None] * skip[:, None, :]
    p_re = jnp.transpose(e_re[::-1], (1, 0, 3, 2)).reshape(g, lc * c, n)
    p_im = jnp.transpose(e_im[::-1], (1, 0, 3, 2)).reshape(g, lc * c, n)
    f_re = c_re[None] * pw_re[1:, :, None, :] - c_im[None] * pw_im[1:, :, None, :]
    f_im = c_re[None] * pw_im[1:, :, None, :] + c_im[None] * pw_re[1:, :, None, :]
    q_re = jnp.transpose(f_re, (1, 3, 0, 2)).reshape(g, n, lc * c)
    q_im = -jnp.transpose(f_im, (1, 3, 0, 2)).reshape(g, n, lc * c)
    a16 = jnp.concatenate([pw_re[lc].reshape(S5_SG, 1, S5_SG_ST), pw_im[lc].reshape(S5_SG, 1, S5_SG_ST)], axis=2)
    return (_pair_blockdiag(m), _pair_blockdiag(p_re), _pair_blockdiag(p_im), _pair_blockdiag(q_re),
            _pair_blockdiag(q_im), a16)


def _s5_a16_powers(a16, nlev):
    half = S5_SG_ST
    re, im = a16[:, :, :half], a16[:, :, half:]
    rows = []
    for _ in range(nlev):
        rows.append(jnp.concatenate([re, im], axis=2))
        re, im = re * re - im * im, 2.0 * re * im
    n_rows = -(-nlev // SUBLANE) * SUBLANE
    rows += [jnp.zeros_like(rows[0])] * (n_rows - nlev)
    return lax.stop_gradient(jnp.concatenate(rows, axis=1))


CV_TILE = 256
CV_PAD = 32
CV_CHUNK = 64


def _shifted_copies(buf, shifted, rows):
    n = rows - SUBLANE
    for s in range(1, SUBLANE):
        shifted[s - 1, 0:n, :] = buf[s:s + n, :]


def _window(buf, shifted, o, ch):
    q, s = divmod(o, SUBLANE)
    if s == 0:
        return buf[o:o + ch, :]
    return shifted[s - 1, q * SUBLANE:q * SUBLANE + ch, :]


def _gn_stats(c, mavg):
    mu = _dot_hi(c, mavg)
    cen = c - mu
    var = _dot_hi(cen * cen, mavg)
    rstd = lax.rsqrt(var + LN_EPS)
    return cen * rstd, rstd


def _cv_fwd(h_in, cw, cb, gng, gnb, mavg, wpw, bpw, mix, *, name, rider=None):
    s = h_in.shape[0]
    t = _seq_tile(s, CV_TILE)
    ch = min(CV_CHUNK, t)

    def body(v_ref, g_ref, cw_ref, cb_ref, gng_ref, gnb_ref, mavg_ref, wpw_ref, bpw_ref, _mix_in, out_ref, c_ref, xpad,
             shifted):
        @pl.when(pl.program_id(0) == 0)
        def _():
            xpad[0:CV_PAD, :] = jnp.zeros((CV_PAD, D_GROUP), F32)

        xpad[CV_PAD:CV_PAD + t, :] = v_ref[...] * _sigmoid(g_ref[...])
        _shifted_copies(xpad, shifted, t + CV_PAD)
        for r0 in range(0, t, ch):
            acc = jnp.broadcast_to(cb_ref[...], (ch, D_GROUP))
            for k in range(CONV_WIDTH):
                o = CV_PAD - (CONV_WIDTH - 1) + k + r0
                acc = acc + cw_ref[k:k + 1, :] * _window(xpad, shifted, o, ch)
            c_ref[r0:r0 + ch, :] = acc
        xpad[0:CV_PAD, :] = xpad[t:t + CV_PAD, :]
        xn, _ = _gn_stats(c_ref[...], mavg_ref[...])
        gn = xn * gng_ref[...] + gnb_ref[...]
        out_ref[...] = (_dot(gn * _sigmoid(gn), wpw_ref[...]) + bpw_ref[...]).astype(BF16)

    ins = [h_in, h_in, cw, cb, gng, gnb, mavg, wpw, bpw, mix]
    in_specs = [pl.BlockSpec((t, D_GROUP), lambda i: (i, COL_CV_V)), pl.BlockSpec((t, D_GROUP), lambda i: (i, COL_CV_G))] + \
               [_full_spec(a) for a in ins[2:9]] + [_ANY]
    return _call(
        body, grid=(s // t,), ins=ins, in_specs=in_specs,
        out_specs=[pl.BlockSpec((t, D_GROUP), lambda i: (i, MIX_CV)), pl.BlockSpec((t, D_GROUP), lambda i: (i, 0))],
        outs=[jax.ShapeDtypeStruct((s, D_MODEL), BF16), jax.ShapeDtypeStruct((s, D_GROUP), F32)],
        aliases={9: 0},
        scratch=[pltpu.VMEM((CV_PAD + t, D_GROUP), F32), pltpu.VMEM((SUBLANE - 1, CV_PAD + t, D_GROUP), F32)],
        name=name, rider=rider)


def _cv_bwd(h_in, c, dmix, cw, gng, gnb, mavg, wpw, *, name, rider=None):
    s = h_in.shape[0]
    t = _seq_tile(s, CV_TILE)
    nt = s // t
    ch = min(CV_CHUNK, t)

    def body(v_ref, g_ref, c_ref, do_ref, cw_ref, gng_ref, gnb_ref, mavg_ref, wpw_ref,
             dvg_ref, dwpw_ref, dcw_ref, dbpw_ref, dgg_ref, dgb_ref, dcb_ref, dcpad, hgbuf, shifted):
        @pl.when(pl.program_id(0) == 0)
        def _():
            dcpad[t:t + CV_PAD, :] = jnp.zeros((CV_PAD, D_GROUP), F32)
            for r in (dwpw_ref, dcw_ref, dbpw_ref, dgg_ref, dgb_ref, dcb_ref):
                r[...] = jnp.zeros_like(r)

        mavg = mavg_ref[...]
        xn, rstd = _gn_stats(c_ref[...], mavg)
        gg = gng_ref[...]
        gn = xn * gg + gnb_ref[...]
        sg = _sigmoid(gn)
        dout = do_ref[...]
        dwpw_ref[...] += _dot_tn(gn * sg, dout)
        dbpw_ref[...] += _colsum(dout)
        dgn = _dot_nt(dout, wpw_ref[...]) * (sg * (1.0 + gn * (1.0 - sg)))
        dgg_ref[...] += _colsum(dgn * xn)
        dgb_ref[...] += _colsum(dgn)
        dxn = dgn * gg
        dc = rstd * (dxn - _dot_hi(dxn, mavg) - xn * _dot_hi(dxn * xn, mavg))
        dcb_ref[...] += _colsum(dc)
        dcpad[0:t, :] = dc

        v = v_ref[...]
        sgm = _sigmoid(g_ref[...])
        hgbuf[...] = v * sgm
        _shifted_copies(dcpad, shifted, t + CV_PAD)
        for r0 in range(0, t, ch):
            hg = hgbuf[r0:r0 + ch, :]
            acc = jnp.zeros((ch, D_GROUP), F32)
            for k in range(CONV_WIDTH):
                o = (CONV_WIDTH - 1) - k + r0
                sh = _window(dcpad, shifted, o, ch)
                acc = acc + cw_ref[k:k + 1, :] * sh
                dcw_ref[k:k + 1, :] += _colsum(hg * sh)
            hgbuf[r0:r0 + ch, :] = acc
        dcpad[t:t + CV_PAD, :] = dcpad[0:CV_PAD, :]
        dhg = hgbuf[...]
        dvg_ref[:, :D_GROUP] = dhg * sgm
        dvg_ref[:, D_GROUP:] = dhg * v * sgm * (1.0 - sgm)

    def rev(col):
        return lambda i: (nt - 1 - i, col)

    ins = [h_in, h_in, c, dmix, cw, gng, gnb, mavg, wpw]
    in_specs = [pl.BlockSpec((t, D_GROUP), rev(COL_CV_V)), pl.BlockSpec((t, D_GROUP), rev(COL_CV_G)),
                pl.BlockSpec((t, D_GROUP), rev(0)), pl.BlockSpec((t, D_GROUP), rev(MIX_CV))] + [_full_spec(a) for a in ins[4:]]
    vec = jax.ShapeDtypeStruct((1, D_GROUP), F32)
    outs = [jax.ShapeDtypeStruct((s, N_IN_COLS), F32),
            jax.ShapeDtypeStruct((D_GROUP, D_GROUP), F32), jax.ShapeDtypeStruct((CV_PAD, D_GROUP), F32), vec, vec, vec, vec]
    out_specs = [pl.BlockSpec((t, 2 * D_GROUP), rev(COL_CV_V // 2))] + [_full_spec(o) for o in outs[1:]]
    return _call(
        body, grid=(nt,), ins=ins, in_specs=in_specs, out_specs=out_specs, outs=outs,
        scratch=[pltpu.VMEM((t + CV_PAD, D_GROUP), F32), pltpu.VMEM((t, D_GROUP), F32),
                 pltpu.VMEM((SUBLANE - 1, t + CV_PAD, D_GROUP), F32)], name=name, rider=rider)


LRU_TILE = 256


def _lru_gates(xc, wr_ref, br_ref, wi_ref, bi_ref, sp_ref):
    r = _sigmoid(_dot(xc, wr_ref[...]) + br_ref[...])
    i = _sigmoid(_dot(xc, wi_ref[...]) + bi_ref[...])
    log_a = -LRU_C * r * sp_ref[...]
    a = jnp.exp(log_a)
    m = jnp.sqrt(_neg_expm1(2.0 * log_a))
    return r, i, a, m


def _lru_fwd(h_in, lcw, lcb, wr, br, wi, bi, sp, mix, *, name):
    s = h_in.shape[0]
    t = _seq_tile(s, LRU_TILE)
    pad = max(t // 2, SUBLANE)

    def body(xg_ref, xr_ref, lcw_ref, lcb_ref, wr_ref, br_ref, wi_ref, bi_ref, sp_ref, _mix_in,
             out_ref, xc_ref, h_ref, xpad, a0, a1, b0, b1, carry):
        @pl.when(pl.program_id(0) == 0)
        def _():
            xpad[0:SUBLANE, :] = jnp.zeros((SUBLANE, D_GROUP), F32)
            for bf in (a0, a1, b0, b1):
                bf[0:pad, :] = jnp.zeros((pad, D_GROUP), F32)
            carry[...] = jnp.zeros_like(carry)

        xpad[SUBLANE:SUBLANE + t, :] = xr_ref[...]
        xc = jnp.broadcast_to(lcb_ref[...], (t, D_GROUP))
        for k in range(LRU_CONV_WIDTH):
            o = SUBLANE - (LRU_CONV_WIDTH - 1) + k
            xc = xc + lcw_ref[k:k + 1, :] * xpad[o:o + t, :]
        xpad[0:SUBLANE, :] = xpad[t:t + SUBLANE, :]
        xc_ref[...] = xc
        _, i, a, m = _lru_gates(xc, wr_ref, br_ref, wi_ref, bi_ref, sp_ref)
        a0[pad:pad + t, :] = a
        b0[pad:pad + t, :] = m * (i * xc)
        b0[pad:pad + 1, :] += a0[pad:pad + 1, :] * carry[0:1, :]
        fin = _rscan_levels((a0, a1), (b0, b1), t, pad, reverse=False)
        hbuf = (b0, b1)[fin]
        carry[0:1, :] = hbuf[pad + t - 1:pad + t, :]
        h = hbuf[pad:pad + t, :]
        h_ref[...] = h
        out_ref[...] = (h * _gelu(xg_ref[...])).astype(BF16)

    ins = [h_in, h_in, lcw, lcb, wr, br, wi, bi, sp, mix]
    row = pl.BlockSpec((t, D_GROUP), lambda i: (i, 0))
    in_specs = [pl.BlockSpec((t, D_GROUP), lambda i: (i, COL_LRU_G)), pl.BlockSpec((t, D_GROUP), lambda i: (i, COL_LRU_X))] + \
               [_full_spec(a) for a in ins[2:9]] + [_ANY]
    return pl.pallas_call(
        body, grid=(s // t,), in_specs=in_specs,
        out_specs=[pl.BlockSpec((t, D_GROUP), lambda i: (i, MIX_LRU)), row, row],
        out_shape=[jax.ShapeDtypeStruct((s, D_MODEL), BF16)] + [jax.ShapeDtypeStruct((s, D_GROUP), F32)] * 2,
        input_output_aliases={9: 0},
        scratch_shapes=[pltpu.VMEM((SUBLANE + t, D_GROUP), F32)] + [pltpu.VMEM((pad + t, D_GROUP), F32)] * 4 +
                       [pltpu.VMEM((SUBLANE, D_GROUP), F32)],
        compiler_params=_cparams(1), name=name)(*ins)


def _lru_bwd(h_in, xc_all, h_all, dmix, lcw, wr, br, wi, bi, sp, dh_all, *, name):
    s = h_in.shape[0]
    t = _seq_tile(s, LRU_TILE)
    nt = s // t
    pad = max(t // 2, SUBLANE)
    tb = t // SUBLANE

    def body(xg_ref, xr_ref, xc_ref, h_ref, hprev_ref, do_ref, lcw_ref, wr_ref, br_ref, wi_ref, bi_ref, sp_ref, _dh_in,
             dgr_ref, dwr_ref, dwi_ref, dlcw_ref, dbr_ref, dbi_ref, dsp_ref, dlcb_ref,
             a0, a1, b0, b1, hp, dxpad, carry):
        pid = pl.program_id(0)

        @pl.when(pid == 0)
        def _():
            for bf in (a0, a1, b0, b1):
                bf[pad + t:pad + t + pad, :] = jnp.zeros((pad, D_GROUP), F32)
            dxpad[t:t + SUBLANE, :] = jnp.zeros((SUBLANE, D_GROUP), F32)
            carry[...] = jnp.zeros_like(carry)
            for r in (dwr_ref, dwi_ref, dlcw_ref, dbr_ref, dbi_ref, dsp_ref, dlcb_ref):
                r[...] = jnp.zeros_like(r)

        xc = xc_ref[...]
        h = h_ref[...]
        dout = do_ref[...]
        gate, dgate = _gelu_and_grad(xg_ref[...])
        dgr_ref[:, :D_GROUP] = dout * h * dgate
        r, i, a, m = _lru_gates(xc, wr_ref, br_ref, wi_ref, bi_ref, sp_ref)

        a0[pad:pad + t, :] = a
        b0[pad:pad + t, :] = dout * gate
        b0[pad + t - 1:pad + t, :] += carry[0:1, :]
        a1[pad:pad + t, :] = a0[pad + 1:pad + 1 + t, :]
        fin = _rscan_levels((a1, a0), (b0, b1), t, pad, reverse=True)
        lam = (b0, b1)[fin][pad:pad + t, :]
        carry[0:1, :] = a[0:1, :] * lam[0:1, :]

        is_first = pid == nt - 1
        hp[0:SUBLANE, :] = jnp.where(is_first, 0.0, hprev_ref[...])
        hp[SUBLANE:SUBLANE + t, :] = h
        hprev = hp[SUBLANE - 1:SUBLANE - 1 + t, :]

        ix = i * xc
        dmm = lam * ix
        dix = lam * m
        da = lam * hprev - dmm * (a / m)
        dlog_a = da * a
        dr = dlog_a * (-LRU_C * sp_ref[...])
        dsp_ref[...] += _colsum(dlog_a * (-LRU_C * r))
        dpr = dr * r * (1.0 - r)
        dpi = dix * xc * i * (1.0 - i)
        dbr_ref[...] += _colsum(dpr)
        dbi_ref[...] += _colsum(dpi)
        dwr_ref[...] += _dot_tn(xc, dpr)
        dwi_ref[...] += _dot_tn(xc, dpi)
        dxc = dix * i + _dot_nt(dpr, wr_ref[...]) + _dot_nt(dpi, wi_ref[...])
        dlcb_ref[...] += _colsum(dxc)

        dxpad[0:t, :] = dxc
        xr = xr_ref[...]
        dxr = jnp.zeros((t, D_GROUP), F32)
        for k in range(LRU_CONV_WIDTH):
            o = (LRU_CONV_WIDTH - 1) - k
            sh = dxpad[o:o + t, :]
            dxr = dxr + lcw_ref[k:k + 1, :] * sh
            dlcw_ref[k:k + 1, :] += _colsum(xr * sh)
        dxpad[t:t + SUBLANE, :] = dxpad[0:SUBLANE, :]
        dgr_ref[:, D_GROUP:] = dxr

    def rev(col):
        return lambda i: (nt - 1 - i, col)

    ins = [h_in, h_in, xc_all, h_all, h_all, dmix, lcw, wr, br, wi, bi, sp, dh_all]
    in_specs = [pl.BlockSpec((t, D_GROUP), rev(COL_LRU_G)), pl.BlockSpec((t, D_GROUP), rev(COL_LRU_X)),
                pl.BlockSpec((t, D_GROUP), rev(0)), pl.BlockSpec((t, D_GROUP), rev(0)),
                pl.BlockSpec((SUBLANE, D_GROUP), lambda i: (jnp.maximum((nt - 1 - i) * tb - 1, 0), 0)),
                pl.BlockSpec((t, D_GROUP), rev(MIX_LRU))] + [_full_spec(a) for a in ins[6:12]] + [_ANY]
    vec = jax.ShapeDtypeStruct((1, D_GROUP), F32)
    mat = jax.ShapeDtypeStruct((D_GROUP, D_GROUP), F32)
    outs = [jax.ShapeDtypeStruct((s, N_IN_COLS), F32), mat, mat, jax.ShapeDtypeStruct((SUBLANE, D_GROUP), F32),
            vec, vec, vec, vec]
    out_specs = [pl.BlockSpec((t, 2 * D_GROUP), rev(COL_LRU_G // 2))] + [_full_spec(o) for o in outs[1:]]
    return pl.pallas_call(
        body, grid=(nt,), in_specs=in_specs, out_specs=out_specs, out_shape=outs, input_output_aliases={12: 0},
        scratch_shapes=[pltpu.VMEM((pad + t + pad, D_GROUP), F32)] * 4 +
                       [pltpu.VMEM((SUBLANE + t, D_GROUP), F32), pltpu.VMEM((t + SUBLANE, D_GROUP), F32),
                        pltpu.VMEM((SUBLANE, D_GROUP), F32)],
        compiler_params=_cparams(1), name=name)(*ins)


def _blockdiag(w):
    h, d, _ = w.shape
    return jnp.tile(w.reshape(h * d, d), (1, h)) * _block_mask(h, d, d)


ATTN_TILE = 512
ATTN_SCALE = ATTN_HEAD_DIM ** -0.5


def _attn_big(kv):
    m = kv.shape[0]
    kbig = jnp.tile(kv[:, :D_GROUP].T, (1, ATTN_HEADS)) * _block_mask(ATTN_HEADS, ATTN_HEAD_DIM, m)
    vbig = jnp.tile(kv[:, D_GROUP:], (ATTN_HEADS, 1)) * _block_mask(ATTN_HEADS, m, ATTN_HEAD_DIM)
    return kbig, vbig


def _attn_probs(q, kbig_ref, m):
    sc = _dot(q, kbig_ref[...]) * ATTN_SCALE
    ps = []
    for h in range(ATTN_HEADS):
        sh = sc[:, h * m:(h + 1) * m]
        e = jnp.exp(sh - jnp.max(sh, axis=1, keepdims=True))
        ps.append(e / jnp.sum(e, axis=1, keepdims=True))
    return ps


def _attn_fwd(h_in, kbig, vbig, mix, *, name):
    s = h_in.shape[0]
    t = _seq_tile(s, ATTN_TILE)
    m = kbig.shape[1] // ATTN_HEADS

    def body(q_ref, kbig_ref, vbig_ref, _mix_in, o_ref):
        ps = _attn_probs(q_ref[...], kbig_ref, m)
        o_ref[...] = _dot(jnp.concatenate(ps, axis=1), vbig_ref[...]).astype(BF16)

    return pl.pallas_call(
        body, grid=(s // t,),
        in_specs=[pl.BlockSpec((t, D_GROUP), lambda i: (i, COL_Q)), _full_spec(kbig), _full_spec(vbig), _ANY],
        out_specs=pl.BlockSpec((t, D_GROUP), lambda i: (i, MIX_ATTN)),
        out_shape=jax.ShapeDtypeStruct((s, D_MODEL), BF16), input_output_aliases={3: 0},
        compiler_params=_cparams(1), name=name)(h_in, kbig, vbig, mix)


def _attn_bwd(h_in, dmix, kbig, vbig, du_s5, dh_all, *, name):
    s = h_in.shape[0]
    t = _seq_tile(s, ATTN_TILE)
    m = kbig.shape[1] // ATTN_HEADS

    def body(q_ref, do_ref, kbig_ref, vbig_ref, dus5_ref, _dh_in, dpair_ref, dk_ref, dv_ref):
        @pl.when(pl.program_id(0) == 0)
        def _():
            dk_ref[...] = jnp.zeros_like(dk_ref)
            dv_ref[...] = jnp.zeros_like(dv_ref)

        q = q_ref[...]
        dout = do_ref[...]
        ps = _attn_probs(q, kbig_ref, m)
        dp = _dot_nt(dout, vbig_ref[...])
        dss = []
        for h in range(ATTN_HEADS):
            dph = dp[:, h * m:(h + 1) * m]
            dss.append(ps[h] * (dph - jnp.sum(dph * ps[h], axis=1, keepdims=True)))
        ds = (jnp.concatenate(dss, axis=1) * ATTN_SCALE).astype(BF16)
        dv_ref[...] += _dot_tn(jnp.concatenate(ps, axis=1), dout)
        dpair_ref[:, :D_GROUP] = dus5_ref[...]
        dpair_ref[:, D_GROUP:] = _dot_nt(ds, kbig_ref[...])
        dk_ref[...] += _dot_tn(q, ds)

    assert (COL_S5, COL_Q) == (4, 5)
    outs = [jax.ShapeDtypeStruct((s, N_IN_COLS), F32), jax.ShapeDtypeStruct(kbig.shape, F32),
            jax.ShapeDtypeStruct(vbig.shape, F32)]
    return pl.pallas_call(
        body, grid=(s // t,),
        in_specs=[pl.BlockSpec((t, D_GROUP), lambda i: (i, COL_Q)), pl.BlockSpec((t, D_GROUP), lambda i: (i, MIX_ATTN)),
                  _full_spec(kbig), _full_spec(vbig), pl.BlockSpec((t, D_GROUP), lambda i: (i, 0)), _ANY],
        out_specs=[pl.BlockSpec((t, 2 * D_GROUP), lambda i: (i, COL_S5 // 2)), _full_spec(outs[1]), _full_spec(outs[2])],
        out_shape=outs, input_output_aliases={5: 0},
        compiler_params=_cparams(1), name=name)(h_in, dmix, kbig, vbig, du_s5, dh_all)


FFN_TILE = 128
FFN_COL_CHUNK = 256
FFN_ROW_CHUNK = 64


def _ffn_conv(pad_ref, w_ref, b_ref, r0, ch, c0):
    cc = FFN_COL_CHUNK
    acc = jnp.broadcast_to(b_ref[:, c0:c0 + cc], (ch, cc))
    for k in range(FFN_CONV_WIDTH):
        o = SUBLANE - (FFN_CONV_WIDTH - 1) + k + r0
        acc = acc + w_ref[k:k + 1, c0:c0 + cc] * pad_ref[o:o + ch, c0:c0 + cc]
    return acc


def _ffn_gate_fwd(u, fcw, fcb, *, name, rider=None):
    s = u.shape[0]
    t = _seq_tile(s, FFN_TILE)
    ch = min(FFN_ROW_CHUNK, t)
    cc = FFN_COL_CHUNK

    def body(u_ref, w_ref, b_ref, o_ref, uc_ref, upad):
        @pl.when(pl.program_id(0) == 0)
        def _():
            upad[0:SUBLANE, :] = jnp.zeros((SUBLANE, 2 * D_FF), F32)

        upad[SUBLANE:SUBLANE + t, :] = u_ref[...].astype(F32)
        for c0 in range(0, D_FF, cc):
            for r0 in range(0, t, ch):
                val = _ffn_conv(upad, w_ref, b_ref, r0, ch, c0)
                gt = _ffn_conv(upad, w_ref, b_ref, r0, ch, c0 + D_FF)
                o_ref[r0:r0 + ch, c0:c0 + cc] = (val * _gelu(gt)).astype(BF16)
                uc_ref[r0:r0 + ch, c0:c0 + cc] = val.astype(BF16)
                uc_ref[r0:r0 + ch, c0 + D_FF:c0 + D_FF + cc] = gt.astype(BF16)
        upad[0:SUBLANE, :] = upad[t:t + SUBLANE, :]

    return _call(
        body, grid=(s // t,), ins=[u, fcw, fcb],
        in_specs=[pl.BlockSpec((t, 2 * D_FF), lambda i: (i, 0)), _full_spec(fcw), _full_spec(fcb)],
        out_specs=[pl.BlockSpec((t, D_FF), lambda i: (i, 0)), pl.BlockSpec((t, 2 * D_FF), lambda i: (i, 0))],
        outs=[jax.ShapeDtypeStruct((s, D_FF), BF16), jax.ShapeDtypeStruct((s, 2 * D_FF), BF16)],
        scratch=[pltpu.VMEM((SUBLANE + t, 2 * D_FF), F32)], name=name, rider=rider)


def _ffn_gate_bwd(u, uc, dh, fcw, *, name, rider=None):
    s = u.shape[0]
    t = _seq_tile(s, FFN_TILE)
    nt = s // t
    ch = min(FFN_ROW_CHUNK, t)
    cc = FFN_COL_CHUNK

    def body(u_ref, uc_ref, dh_ref, w_ref, du_ref, dw_ref, db_ref, dpad):
        @pl.when(pl.program_id(0) == 0)
        def _():
            dpad[t:t + SUBLANE, :] = jnp.zeros((SUBLANE, 2 * D_FF), F32)
            dw_ref[...] = jnp.zeros_like(dw_ref)
            db_ref[...] = jnp.zeros_like(db_ref)

        for c0 in range(0, D_FF, cc):
            for r0 in range(0, t, ch):
                val = uc_ref[r0:r0 + ch, c0:c0 + cc].astype(F32)
                gt = uc_ref[r0:r0 + ch, c0 + D_FF:c0 + D_FF + cc].astype(F32)
                gl, dgl = _gelu_and_grad(gt)
                d = dh_ref[r0:r0 + ch, c0:c0 + cc].astype(F32)
                dpad[r0:r0 + ch, c0:c0 + cc] = d * gl
                dpad[r0:r0 + ch, c0 + D_FF:c0 + D_FF + cc] = d * val * dgl
        for c0 in range(0, 2 * D_FF, cc):
            dbs = jnp.zeros((1, cc), F32)
            dws = [jnp.zeros((1, cc), F32) for _ in range(FFN_CONV_WIDTH)]
            for r0 in range(0, t, ch):
                x = u_ref[r0:r0 + ch, c0:c0 + cc].astype(F32)
                acc = jnp.zeros((ch, cc), F32)
                for k in range(FFN_CONV_WIDTH):
                    o = (FFN_CONV_WIDTH - 1) - k + r0
                    sh = dpad[o:o + ch, c0:c0 + cc]
                    acc = acc + w_ref[k:k + 1, c0:c0 + cc] * sh
                    dws[k] = dws[k] + _colsum(x * sh)
                    if k == FFN_CONV_WIDTH - 1:
                        dbs = dbs + _colsum(sh)
                du_ref[r0:r0 + ch, c0:c0 + cc] = acc.astype(BF16)
            db_ref[:, c0:c0 + cc] += dbs
            for k in range(FFN_CONV_WIDTH):
                dw_ref[k:k + 1, c0:c0 + cc] += dws[k]
        dpad[t:t + SUBLANE, :] = dpad[0:SUBLANE, :]

    outs = [jax.ShapeDtypeStruct((s, 2 * D_FF), BF16), jax.ShapeDtypeStruct((SUBLANE, 2 * D_FF), F32),
            jax.ShapeDtypeStruct((1, 2 * D_FF), F32)]
    return _call(
        body, grid=(nt,), ins=[u, uc, dh, fcw],
        in_specs=[pl.BlockSpec((t, 2 * D_FF), lambda i: (nt - 1 - i, 0)),
                  pl.BlockSpec((t, 2 * D_FF), lambda i: (nt - 1 - i, 0)),
                  pl.BlockSpec((t, D_FF), lambda i: (nt - 1 - i, 0)), _full_spec(fcw)],
        out_specs=[pl.BlockSpec((t, 2 * D_FF), lambda i: (nt - 1 - i, 0)), _full_spec(outs[1]), _full_spec(outs[2])],
        outs=outs, scratch=[pltpu.VMEM((t + SUBLANE, 2 * D_FF), F32)], name=name, rider=rider)


def _adamw_body(g_ref, w_ref, m_ref, v_ref, go_ref, d_ref, mo_ref, vo_ref):
    inv_b1 = 1.0 - ADAM_B1 ** ADAM_STEP
    inv_b2 = 1.0 - ADAM_B2 ** ADAM_STEP
    g = g_ref[0].astype(F32)
    for dev in range(1, N_DEV):
        g = g + g_ref[dev].astype(F32)
    go_ref[...] = g
    mn = ADAM_B1 * m_ref[...] + (1.0 - ADAM_B1) * g
    vn = ADAM_B2 * v_ref[...] + (1.0 - ADAM_B2) * (g * g)
    mo_ref[...] = mn
    vo_ref[...] = vn
    d_ref[...] = -ADAM_LR * ((mn / inv_b1) / (jnp.sqrt(vn / inv_b2) + ADAM_EPS) + ADAM_WD * w_ref[...])


def _adamw(gstack, w, m, v, *, name):
    _, r, c = gstack.shape
    tr = _pick_rows(r, PACK_ROW_BLOCK)

    def body(*refs):
        _adamw_body(*refs)

    blk = pl.BlockSpec((tr, c), lambda i: (i, 0))
    sh = jax.ShapeDtypeStruct((r, c), F32)
    return pl.pallas_call(
        body, grid=(r // tr,),
        in_specs=[pl.BlockSpec((N_DEV, tr, c), lambda i: (0, i, 0)), blk, blk, blk],
        out_specs=[blk] * 4, out_shape=[sh] * 4,
        compiler_params=_cparams(1), name=name)(gstack, w, m, v)


def _adamw_layer(gstack, w, m, v, layer, into, *, name):
    n_layers, r, c = w.shape
    tr = _pick_rows(r, PACK_ROW_BLOCK)

    def body(g_ref, w_ref, m_ref, v_ref, *rest):
        _adamw_body(g_ref, w_ref, m_ref, v_ref, *rest[-4:])

    blk = pl.BlockSpec((None, tr, c), lambda i: (layer, i, 0))
    sh = jax.ShapeDtypeStruct((n_layers, r, c), F32)
    into = list(into or [])
    return pl.pallas_call(
        body, grid=(r // tr,),
        in_specs=[pl.BlockSpec((N_DEV, tr, c), lambda i: (0, i, 0)), blk, blk, blk] + [_ANY] * len(into),
        out_specs=[blk] * 4, out_shape=[sh] * 4, input_output_aliases={4 + k: k for k in range(len(into))},
        compiler_params=_cparams(1), name=name)(gstack, w, m, v, *into)


def _exchange(rider, *, name):
    n = rider.n

    def body(*refs):
        x_refs, out_refs, sems = refs[:n], refs[n:2 * n], refs[2 * n:]
        rider.start(x_refs, out_refs, sems)
        rider.wait(x_refs, out_refs, sems)

    return pl.pallas_call(
        body, in_specs=[_ANY] * n, out_specs=[_ANY] * n, out_shape=rider.out_shapes(),
        scratch_shapes=rider.scratch(), name=name)(*rider.srcs)


def _pack_rows(n):
    rows = -(-n // PACK_COLS)
    return -(-rows // SUBLANE) * SUBLANE


def _pack(arrs, dtype):
    flat = jnp.concatenate([a.reshape(-1).astype(dtype) for a in arrs])
    rows = _pack_rows(flat.shape[0])
    flat = jnp.pad(flat, (0, rows * PACK_COLS - flat.shape[0]))
    return flat.reshape(rows, PACK_COLS)


def _pack_lead(arrs, dtype):
    flat = jnp.concatenate([a.reshape(N_DEV, -1).astype(dtype) for a in arrs], axis=1)
    rows = _pack_rows(flat.shape[1])
    flat = jnp.pad(flat, ((0, 0), (0, rows * PACK_COLS - flat.shape[1])))
    return flat.reshape(N_DEV, rows, PACK_COLS)


def _pack_layers(arrs, dtype):
    n_layers = arrs[0].shape[0]
    flat = jnp.concatenate([a.reshape(n_layers, -1).astype(dtype) for a in arrs], axis=1)
    rows = _pack_rows(flat.shape[1])
    flat = jnp.pad(flat, ((0, 0), (0, rows * PACK_COLS - flat.shape[1])))
    return flat.reshape(n_layers, rows, PACK_COLS)


def _unpack_layers(packed, shapes):
    flat = packed.reshape(packed.shape[0], -1)
    out, pos = [], 0
    for sh in shapes:
        n = math.prod(sh[1:])
        out.append(flat[:, pos:pos + n].reshape(sh))
        pos += n
    return out


def _unpack(packed, shapes, lead=False):
    flat = packed.reshape(N_DEV, -1) if lead else packed.reshape(-1)
    out, pos = [], 0
    for sh in shapes:
        n = math.prod(sh)
        out.append(flat[:, pos:pos + n].reshape((N_DEV,) + tuple(sh)) if lead else flat[pos:pos + n].reshape(sh))
        pos += n
    return out


def _join_shards(stacked, axis):
    return jnp.concatenate([stacked[d] for d in range(N_DEV)], axis=axis)


def _split_shards(full, axis):
    return jnp.stack(jnp.split(full, N_DEV, axis=axis), axis=0)


def _perm_in_cols(a, inverse=False):
    blocks = jnp.split(a, 6, axis=-1)
    if inverse:
        order = [IN_PERM.index(j) for j in range(6)]
    else:
        order = list(IN_PERM)
    return jnp.concatenate([blocks[j] for j in order], axis=-1)


def _row(v):
    return v.reshape(1, -1)


def _pad_rows(w, rows):
    return jnp.pad(w, ((0, rows - w.shape[0]), (0, 0)))


def _gn_avg_matrix():
    return _block_mask(GN_GROUPS, D_GROUP // GN_GROUPS, D_GROUP // GN_GROUPS) / (D_GROUP // GN_GROUPS)


def _layer_params(p, l):
    q = {}
    s5_mats, q["s5_vjp"] = jax.vjp(_s5_chunk_map, p["s5_lam_re"][l], p["s5_lam_im"][l], p["s5_log_dt"][l],
                                   p["s5_b_re"][l], p["s5_b_im"][l], p["s5_c_re"][l], p["s5_c_im"][l], p["s5_d"][l])
    q["s5_mats"] = [m.astype(BF16) for m in s5_mats[:5]]
    q["s5_a16"] = s5_mats[5]
    (q["wr"], q["wi"]), q["lru_w_vjp"] = jax.vjp(lambda r, i: (_blockdiag(r), _blockdiag(i)), p["lru_w_r"][l], p["lru_w_i"][l])
    q["wr"], q["wi"] = q["wr"].astype(BF16), q["wi"].astype(BF16)
    q["sp"], q["sp_vjp"] = jax.vjp(lambda lam: _row(jax.nn.softplus(-lam)), p["lru_lam"][l])
    return q


WEIGHT_RIDES = {(0, "ln_in_fwd"): [("w_in", 0)],
                (0, "inproj"): [("attn_w_kv", 0), ("w_out", 0), ("small_pack", 0)],
                (0, "cv_fwd"): [("ffn_w_up#a", 0)],
                (0, "outproj"): [("ffn_w_up#b", 0)],
                (0, "ffn_up"): [("ffn_w_down", 0), ("w_in", 1), ("attn_w_kv", 1), ("w_out", 1)],
                (0, "ffn_gate_fwd"): [("ffn_w_up", 1)],
                (0, "ffn_down"): [("ffn_w_down", 1)]}
GRAD_RIDES = {(1, "ffn_gate_bwd"): [("ffn_w_down", 1)],
              (0, "dw_down"): [("w_out", 1), ("attn_w_kv", 1), ("w_in", 1)],
              (0, "dhff"): [("rep", 1), ("ssh", 1)],
              (0, "ffn_gate_bwd"): [("ffn_w_up", 1)],
              (0, "dw_up"): [("ffn_w_down", 0)],
              (0, "dx1"): [("ffn_w_up", 0)],
              (0, "cv_bwd"): [("w_out", 0)],
              (0, "dw_in"): [("attn_w_kv", 0), ("ssh", 0), ("rep", 0)],
              (0, "dxs"): [("w_in", 0)]}


def _join_cols(pieces, *, name):
    n_dev, k, c = pieces[0].shape
    assert (2 * c) % LANE == 0 and all(p.shape == pieces[0].shape for p in pieces)
    n_p = len(pieces)

    def body(*refs):
        o_ref = refs[n_p]
        for i in range(n_p):
            @pl.when(pl.program_id(0) == i)
            def _(i=i):
                o_ref[...] = jnp.concatenate([refs[i][0], refs[i][1]], axis=1)

    return pl.pallas_call(
        body, grid=(n_p, n_dev // 2), in_specs=[pl.BlockSpec((2, k, c), lambda i, j: (j, 0, 0))] * n_p,
        out_specs=pl.BlockSpec((k, 2 * c), lambda i, j: (i, j)),
        out_shape=jax.ShapeDtypeStruct((n_p * k, n_dev * c), pieces[0].dtype),
        compiler_params=_cparams(2), name=name)(*pieces)


def _split_cols(full, *, name):
    k, n = full.shape
    c = n // N_DEV
    assert (2 * c) % LANE == 0

    def body(x_ref, o_ref):
        o_ref[0] = x_ref[:, :c]
        o_ref[1] = x_ref[:, c:]

    return pl.pallas_call(
        body, grid=(N_DEV // 2,), in_specs=[pl.BlockSpec((k, 2 * c), lambda j: (0, j))],
        out_specs=pl.BlockSpec((2, k, c), lambda j: (j, 0, 0)), out_shape=jax.ShapeDtypeStruct((N_DEV, k, c), full.dtype),
        compiler_params=_cparams(1), name=name)(full)


def _assemble_weight(n, pieces, layer=0):
    if SHARDED[n] == 2:
        full = _join_cols(pieces, name=f"l{layer}_join_{n}")
        return _perm_in_cols(full) if n == "w_in" else full
    (gathered,) = pieces
    return gathered.reshape(-1, gathered.shape[-1])


def _grad_source(n, g, layer=0):
    g = g.astype(BF16)
    if SHARDED[n] == 2:
        if n == "w_in":
            g = _perm_in_cols(g, inverse=True)
        return _split_cols(g, name=f"l{layer}_split_d{n}"), "lead"
    return g, "rows"


def _hosted(fn, keys_rider, land, *args, **kw):
    keys, rider = keys_rider
    if rider is None:
        return fn(*args, **kw)
    out, routs = fn(*args, rider=rider, **kw)
    land(keys, routs)
    return out


def _local_step(x, mem, target, p, big_w, shards=None, unpack_small=None):
    dist = shards is not None
    gdt = BF16 if dist else F32
    small, saved = {}, []
    big_g, ready, recv = {}, {}, {}
    mavg = _gn_avg_matrix()
    s5_perm = _s5_perm()

    def weight_rider(l, host):
        keys = WEIGHT_RIDES.get((l, host), []) if dist else []
        return keys, (_Rider([shards[n][ll] for n, ll in keys], ["all"] * len(keys)) if keys else None)

    halves = {}

    def land_weights(keys, routs):
        for (n, ll), r in zip(keys, routs):
            if n == "small_pack":
                p.update(unpack_small(r))
            elif "#" in n:
                base = n.split("#")[0]
                halves[(n, ll)] = r
                if (base + "#a", ll) in halves and (base + "#b", ll) in halves:
                    big_w[base][ll] = _assemble_weight(base, [halves[(base + "#a", ll)], halves[(base + "#b", ll)]], ll)
            else:
                big_w[n][ll] = _assemble_weight(n, [r], ll)

    def grad_rider(l, host):
        keys = [k for k in GRAD_RIDES.get((l, host), []) if k in ready] if dist else []
        return keys, (_Rider([ready[k][0] for k in keys], [ready[k][1] for k in keys]) if keys else None)

    def land_grads(keys, routs):
        for k, r in zip(keys, routs):
            recv[k] = r
            del ready[k]

    def big_grad(n, l, g):
        if dist:
            ready[(n, l)] = _grad_source(n, g, l)
        else:
            big_g[(n, l)] = g

    xs = _hosted(_ln_fwd, weight_rider(0, "ln_in_fwd"), land_weights, x, _row(p["ln_in_g"]), _row(p["ln_in_b"]),
                 name="ln_in_fwd")
    for l in range(DEPTH):
        q = _layer_params(p, l)
        n = f"l{l}_"
        hin = _hosted(_mm, weight_rider(l, "inproj"), land_weights, xs, big_w["w_in"][l], bias=_row(p["b_in"][l]),
                      name=n + "inproj")
        nb = hin.shape[0] // S5_CHUNK
        s5_pows = _s5_a16_powers(q["s5_a16"], nb.bit_length() - 1)
        s5_u2 = _s5_to_chunks(hin, COL_S5 * (D_GROUP // LANE), s5_perm, name=n + "s5_in")
        s5_y2, s5_x = _s5_core_fwd(s5_u2, *q["s5_mats"], s5_pows, name=n + "s5_core_fwd")
        s5_y1 = _s5_from_chunks(s5_y2, s5_perm, name=n + "s5_out")
        (mix,), _ = _s5_glu_fwd(s5_y1, p["s5_w_glu"][l], _row(p["s5_b_glu"][l]), name=n + "s5_glu_fwd")
        cvw = _pad_rows(p["cv_w"][l], CV_PAD)
        keys, rd = weight_rider(l, "cv_fwd")
        (mix, cv_c), routs = _cv_fwd(hin, cvw, _row(p["cv_b"][l]), _row(p["cv_gn_g"][l]), _row(p["cv_gn_b"][l]), mavg,
                                     p["cv_w_pw"][l], _row(p["cv_b_pw"][l]), mix, name=n + "cv_fwd", rider=rd)
        land_weights(keys, routs)
        lcw = _pad_rows(p["lru_conv_w"][l], SUBLANE)
        mix, lru_xc, lru_h = _lru_fwd(hin, lcw, _row(p["lru_conv_b"][l]), q["wr"], _row(p["lru_b_r"][l]), q["wi"],
                                      _row(p["lru_b_i"][l]), q["sp"], mix, name=n + "lru_fwd")
        kv = _mm(mem, big_w["attn_w_kv"][l], name=n + "kv")
        (kbig, vbig), kv_vjp = jax.vjp(_attn_big, kv)
        kbig, vbig = kbig.astype(BF16), vbig.astype(BF16)
        mix = _attn_fwd(hin, kbig, vbig, mix, name=n + "attn_fwd")
        r1, x1 = _hosted(_mm, weight_rider(l, "outproj"), land_weights, mix, big_w["w_out"][l], bias=_row(p["b_out"][l]),
                         res=xs, res_scale=ALPHA, ln=(_row(p["ln1_g"][l]), _row(p["ln1_b"][l])), name=n + "outproj")
        u = _hosted(_mm, weight_rider(l, "ffn_up"), land_weights, x1, big_w["ffn_w_up"][l], out_dtype=BF16,
                    name=n + "ffn_up")
        fcw = _pad_rows(p["ffn_conv_w"][l], SUBLANE)
        fcb = _row(p["ffn_conv_b"][l])
        keys, rd = weight_rider(l, "ffn_gate_fwd")
        (hff, uc), routs = _ffn_gate_fwd(u, fcw, fcb, name=n + "ffn_gate_fwd", rider=rd)
        land_weights(keys, routs)
        if l < DEPTH - 1:
            r2, x2 = _hosted(_mm, weight_rider(l, "ffn_down"), land_weights, hff, big_w["ffn_w_down"][l], res=x1,
                             res_scale=ALPHA, ln=(_row(p["ln2_g"][l]), _row(p["ln2_b"][l])), name=n + "ffn_down")
        else:
            r2, x2 = _mm(hff, big_w["ffn_w_down"][l], res=x1, res_scale=ALPHA, name=n + "ffn_down"), None
        saved.append(dict(q=q, xs=xs, hin=hin, s5_y1=s5_y1, s5_u2=s5_u2, s5_x=s5_x, s5_pows=s5_pows, cvw=cvw, cv_c=cv_c, lcw=lcw, lru_xc=lru_xc,
                          lru_h=lru_h, kbig=kbig, vbig=vbig, kv_vjp=kv_vjp, mix=mix, r1=r1, x1=x1, u=u, uc=uc, fcw=fcw,
                          hff=hff, r2=r2))
        xs = x2

    top = DEPTH - 1
    dr_top, dg_top, db_top, loss_blk = _loss_ln_bwd(saved[top]["r2"], _row(p["ln2_g"][top]), _row(p["ln2_b"][top]), target,
                                                     name="loss_ln_bwd")
    loss = loss_blk[0, 0]
    dx = None

    for l in reversed(range(DEPTH)):
        sv = saved[l]
        q = sv["q"]
        n = f"l{l}_"
        g = {}
        if l == top:
            dr2, g["ln2_g"], g["ln2_b"] = dr_top, dg_top, db_top
        else:
            dr2, g["ln2_g"], g["ln2_b"], _ = _ln_bwd(sv["r2"], dx, _row(p["ln2_g"][l]), name=n + "ln2_bwd")
        big_grad("ffn_w_down", l, _hosted(_mm_tn, grad_rider(l, "dw_down"), land_grads, sv["hff"], dr2, out_dtype=gdt,
                                          name=n + "dw_down"))
        dhff = _hosted(_mm, grad_rider(l, "dhff"), land_grads, dr2, big_w["ffn_w_down"][l], trans_b=True,
                       out_dtype=BF16, name=n + "dhff")
        keys, rd = grad_rider(l, "ffn_gate_bwd")
        (du, dfw, g["ffn_conv_b"]), routs = _ffn_gate_bwd(sv["u"], sv["uc"], dhff, sv["fcw"], name=n + "ffn_gate_bwd",
                                                          rider=rd)
        land_grads(keys, routs)
        g["ffn_conv_w"] = dfw[:FFN_CONV_WIDTH]
        if dist:
            ready[("ffn_w_up", l)] = (_hosted(_mm_tn, grad_rider(l, "dw_up"), land_grads, sv["x1"], du, out_dtype=gdt,
                                              dev_cols=du.shape[1] // N_DEV, name=n + "dw_up"), "lead")
        else:
            big_grad("ffn_w_up", l, _mm_tn(sv["x1"], du, name=n + "dw_up"))
        dx1 = _hosted(_mm, grad_rider(l, "dx1"), land_grads, du, big_w["ffn_w_up"][l], trans_b=True, res=dr2,
                      res_scale=ALPHA, name=n + "dx1")
        dr1, g["ln1_g"], g["ln1_b"], g["b_out"] = _ln_bwd(sv["r1"], dx1, _row(p["ln1_g"][l]), name=n + "ln1_bwd")
        big_grad("w_out", l, _mm_tn(sv["mix"], dr1, out_dtype=gdt, name=n + "dw_out"))
        dmix = _mm(dr1, big_w["w_out"][l], trans_b=True, name=n + "dmix")

        hin = sv["hin"]
        keys, rd = grad_rider(l, "cv_bwd")
        (dh, g["cv_w_pw"], dcw, g["cv_b_pw"], g["cv_gn_g"], g["cv_gn_b"], g["cv_b"]), routs = _cv_bwd(
            hin, sv["cv_c"], dmix, sv["cvw"], _row(p["cv_gn_g"][l]), _row(p["cv_gn_b"][l]), mavg, p["cv_w_pw"][l],
            name=n + "cv_bwd", rider=rd)
        land_grads(keys, routs)
        g["cv_w"] = dcw[:CONV_WIDTH]
        dh, dwr, dwi, dlcw, g["lru_b_r"], g["lru_b_i"], dsp, g["lru_conv_b"] = _lru_bwd(
            hin, sv["lru_xc"], sv["lru_h"], dmix, sv["lcw"], q["wr"], _row(p["lru_b_r"][l]), q["wi"],
            _row(p["lru_b_i"][l]), q["sp"], dh, name=n + "lru_bwd")
        g["lru_conv_w"] = dlcw[:LRU_CONV_WIDTH]
        g["lru_w_r"], g["lru_w_i"] = q["lru_w_vjp"]((dwr, dwi))
        (g["lru_lam"],) = q["sp_vjp"](dsp)
        dy1, g["s5_w_glu"], g["s5_b_glu"] = _s5_glu_bwd(sv["s5_y1"], dmix, p["s5_w_glu"][l], _row(p["s5_b_glu"][l]),
                                                        name=n + "s5_glu_bwd")
        s5_du2, *s5_dmats = _s5_core_bwd(sv["s5_u2"], _s5_to_chunks(dy1, 0, s5_perm, name=n + "s5_din"), sv["s5_x"],
                                         *q["s5_mats"], sv["s5_pows"], name=n + "s5_core_bwd")
        (g["s5_lam_re"], g["s5_lam_im"], g["s5_log_dt"], g["s5_b_re"], g["s5_b_im"], g["s5_c_re"], g["s5_c_im"],
         g["s5_d"]) = q["s5_vjp"](tuple(s5_dmats))
        dh, dkbig, dvbig = _attn_bwd(hin, dmix, sv["kbig"], sv["vbig"],
                                     _s5_from_chunks(s5_du2, s5_perm, name=n + "s5_dout"), dh, name=n + "attn_bwd")
        (dkv,) = sv["kv_vjp"]((dkbig, dvbig))
        big_grad("attn_w_kv", l, _mm_tn(mem, dkv, out_dtype=gdt, name=n + "dw_kv"))

        if dist:
            ready[("ssh", l)] = (_pack_lead([_split_shards(g[k], SHARDED[k] - 1) for k in SMALL_SHARDED], F32), "lead")
            ready[("rep", l)] = (_pack([g[k] for k in REP_LAYERED], F32), "all")
        gw_in, g["b_in"] = _hosted(_mm_tn, grad_rider(l, "dw_in"), land_grads, sv["xs"], dh, colsum=True, out_dtype=gdt,
                                   name=n + "dw_in")
        big_grad("w_in", l, gw_in)
        if dist:
            small.setdefault("b_in", [None] * DEPTH)[l] = g["b_in"].reshape(-1)
        else:
            for k, v in g.items():
                small.setdefault(k, [None] * DEPTH)[l] = v.reshape(p[k].shape[1:])
        dx = _hosted(_mm, grad_rider(l, "dxs"), land_grads, dh, big_w["w_in"][l], trans_b=True, res=dr1,
                     res_scale=ALPHA, name=n + "dxs")

    keys, rd = grad_rider(0, "ln_in_bwd")
    if rd is None:
        grad_x, dgi, dbi, _ = _ln_bwd(x, dx, _row(p["ln_in_g"]), name="ln_in_bwd")
    else:
        (grad_x, dgi, dbi, _), routs = _ln_bwd(x, dx, _row(p["ln_in_g"]), name="ln_in_bwd", rider=rd)
        land_grads(keys, routs)
    out = {k: jnp.stack(v, axis=0) for k, v in small.items()}
    out["ln_in_g"], out["ln_in_b"] = dgi.reshape(-1), dbi.reshape(-1)
    return loss, grad_x, out, ((recv, ready) if dist else big_g)


def kernel(x, mem, ln_in_g, ln_in_b, w_in, b_in, s5_lam_re, s5_lam_im, s5_log_dt, s5_b_re, s5_b_im, s5_c_re, s5_c_im, s5_d, s5_w_glu, s5_b_glu, cv_w, cv_b, cv_gn_g, cv_gn_b, cv_w_pw, cv_b_pw, lru_conv_w, lru_conv_b, lru_w_r, lru_b_r, lru_w_i, lru_b_i, lru_lam, attn_w_kv, w_out, b_out, ln1_g, ln1_b, ffn_w_up, ffn_conv_w, ffn_conv_b, ffn_w_down, ln2_g, ln2_b, loss_target, m_ln_in_g, m_ln_in_b, m_w_in, m_b_in, m_s5_lam_re, m_s5_lam_im, m_s5_log_dt, m_s5_b_re, m_s5_b_im, m_s5_c_re, m_s5_c_im, m_s5_d, m_s5_w_glu, m_s5_b_glu, m_cv_w, m_cv_b, m_cv_gn_g, m_cv_gn_b, m_cv_w_pw, m_cv_b_pw, m_lru_conv_w, m_lru_conv_b, m_lru_w_r, m_lru_b_r, m_lru_w_i, m_lru_b_i, m_lru_lam, m_attn_w_kv, m_w_out, m_b_out, m_ln1_g, m_ln1_b, m_ffn_w_up, m_ffn_conv_w, m_ffn_conv_b, m_ffn_w_down, m_ln2_g, m_ln2_b, v_ln_in_g, v_ln_in_b, v_w_in, v_b_in, v_s5_lam_re, v_s5_lam_im, v_s5_log_dt, v_s5_b_re, v_s5_b_im, v_s5_c_re, v_s5_c_im, v_s5_d, v_s5_w_glu, v_s5_b_glu, v_cv_w, v_cv_b, v_cv_gn_g, v_cv_gn_b, v_cv_w_pw, v_cv_b_pw, v_lru_conv_w, v_lru_conv_b, v_lru_w_r, v_lru_b_r, v_lru_w_i, v_lru_b_i, v_lru_lam, v_attn_w_kv, v_w_out, v_b_out, v_ln1_g, v_ln1_b, v_ffn_w_up, v_ffn_conv_w, v_ffn_conv_b, v_ffn_w_down, v_ln2_g, v_ln2_b):
    args = locals()
    w = {n: args[n] for n in WEIGHTS}
    mom = {n: args["m_" + n] for n in WEIGHTS}
    var = {n: args["v_" + n] for n in WEIGHTS}

    shards = {n: w[n].astype(BF16) for n in BIG}
    half_rows = shards["ffn_w_up"].shape[1] // 2
    shards["ffn_w_up#a"] = [shards["ffn_w_up"][0, :half_rows]]
    shards["ffn_w_up#b"] = [shards["ffn_w_up"][0, half_rows:]]
    shards["small_pack"] = [_pack([w[n] for n in SMALL_SHARDED], F32)]
    small_shapes = [w[n].shape for n in SMALL_SHARDED]

    def unpack_small(gathered):
        out = {n: _join_shards(st, SHARDED[n]) for n, st in zip(SMALL_SHARDED, _unpack(gathered, small_shapes, lead=True))}
        for n in ("s5_w_glu", "cv_w_pw"):
            out[n] = out[n].astype(BF16)
        return out

    big_w = {n: [None] * DEPTH for n in BIG}
    p = {n: w[n] for n in REPLICATED}
    p["b_in"] = _perm_in_cols(p["b_in"])

    loss, grad_x, g_small, (recv, ready) = _local_step(x[0], mem[0], loss_target[0], p, big_w, shards, unpack_small)
    loss = lax.psum(loss, ("x", "y", "c"))

    g_small["b_in"] = _perm_in_cols(g_small["b_in"], inverse=True)
    left = list(ready)
    rider = _Rider([ready[k][0] for k in left] + [_pack([g_small[n] for n in REP_LAST], F32)],
                   [ready[k][1] for k in left] + ["all"])
    got = _exchange(rider, name="exchange_grads")
    for k, r in zip(left, got):
        recv[k] = r

    res = [dict(), dict(), dict(), dict()]
    for n in BIG:
        outs = None
        for l in range(DEPTH):
            outs = _adamw_layer(recv[(n, l)], w[n], mom[n], var[n], l, outs, name=f"adamw_{n}_l{l}")
        for kind in range(4):
            res[kind][n] = outs[kind]
    for names, key, tag in ((SMALL_SHARDED, "ssh", "adamw_small_sharded"), (REP_LAYERED, "rep", "adamw_replicated")):
        gstack = jnp.concatenate([recv[(key, l)] for l in range(DEPTH)], axis=1)
        packs = [_pack_layers([t[n] for n in names], F32) for t in (w, mom, var)]
        rows = packs[0].shape[1]
        outs = _adamw(gstack, *[pk.reshape(DEPTH * rows, PACK_COLS) for pk in packs], name=tag)
        for kind in range(4):
            for n, a in zip(names, _unpack_layers(outs[kind].reshape(DEPTH, rows, PACK_COLS), [w[n].shape for n in names])):
                res[kind][n] = a
    outs = _adamw(got[len(left)], _pack([w[n] for n in REP_LAST], F32), _pack([mom[n] for n in REP_LAST], F32),
                  _pack([var[n] for n in REP_LAST], F32), name="adamw_last")
    for kind in range(4):
        for n, a in zip(REP_LAST, _unpack(outs[kind], [w[n].shape for n in REP_LAST])):
            res[kind][n] = a
    return (loss, grad_x[None], *[res[0][n] for n in WEIGHTS], *[res[1][n] for n in WEIGHTS],
            *[res[2][n] for n in WEIGHTS], *[res[3][n] for n in WEIGHTS])
```

```python
import math

import jax
import jax.numpy as jnp
from jax import lax
from jax.experimental import pallas as pl
from jax.experimental.pallas import tpu as pltpu

F32 = jnp.float32
BF16 = jnp.bfloat16

D_MODEL = 1024
DEPTH = 2
D_GROUP = 256
N_IN_COLS = 6 * D_GROUP
S5_GROUPS = 16
S5_CH = 16
S5_STATE = 64
S5_LANES = S5_GROUPS * S5_STATE
CONV_WIDTH = 31
GN_GROUPS = 4
LRU_HEADS = 4
LRU_CONV_WIDTH = 4
LRU_C = 8.0
ATTN_HEADS = 4
ATTN_HEAD_DIM = 64
D_FF = 2816
FFN_CONV_WIDTH = 3
ALPHA = (2 * DEPTH) ** 0.25
LN_EPS = 1e-5
ADAM_LR, ADAM_B1, ADAM_B2, ADAM_EPS, ADAM_WD, ADAM_STEP = 0.001, 0.9, 0.999, 1e-08, 0.01, 10

N_DEV = 8
N_PEERS = N_DEV - 1
LANE = 128
SUBLANE = 8
VMEM_LIMIT = 56 * 1024 * 1024
PACK_COLS = 1024
PACK_ROW_BLOCK = 256

SHARDED = {
    "w_in": 2, "s5_w_glu": 1, "cv_w": 2, "cv_w_pw": 1, "lru_conv_w": 2, "attn_w_kv": 1,
    "w_out": 1, "ffn_w_up": 2, "ffn_conv_w": 2, "ffn_w_down": 1,
}
BIG = ("w_in", "attn_w_kv", "w_out", "ffn_w_up", "ffn_w_down")
SMALL_SHARDED = ("s5_w_glu", "cv_w", "cv_w_pw", "lru_conv_w", "ffn_conv_w")
MATMUL_WEIGHTS = ("w_in", "s5_w_glu", "cv_w_pw", "attn_w_kv", "w_out", "ffn_w_up", "ffn_w_down")
WEIGHTS = ['ln_in_g', 'ln_in_b', 'w_in', 'b_in', 's5_lam_re', 's5_lam_im', 's5_log_dt', 's5_b_re', 's5_b_im',
           's5_c_re', 's5_c_im', 's5_d', 's5_w_glu', 's5_b_glu', 'cv_w', 'cv_b', 'cv_gn_g', 'cv_gn_b', 'cv_w_pw',
           'cv_b_pw', 'lru_conv_w', 'lru_conv_b', 'lru_w_r', 'lru_b_r', 'lru_w_i', 'lru_b_i', 'lru_lam',
           'attn_w_kv', 'w_out', 'b_out', 'ln1_g', 'ln1_b', 'ffn_w_up', 'ffn_conv_w', 'ffn_conv_b', 'ffn_w_down',
           'ln2_g', 'ln2_b']
REPLICATED = [n for n in WEIGHTS if n not in SHARDED]
REP_LAST = ("ln_in_g", "ln_in_b", "b_in")
REP_LAYERED = [n for n in REPLICATED if n not in REP_LAST]

COL_CV_V, COL_CV_G, COL_LRU_G, COL_LRU_X, COL_S5, COL_Q = range(6)
IN_PERM = (1, 2, 3, 4, 0, 5)
MIX_S5, MIX_CV, MIX_LRU, MIX_ATTN = range(4)


_ANY = pl.BlockSpec(memory_space=pl.ANY)
_MESH = pl.DeviceIdType.MESH


def _cparams(n_axes):
    return pltpu.CompilerParams(dimension_semantics=("arbitrary",) * n_axes, vmem_limit_bytes=VMEM_LIMIT)


def _pick(n, cap):
    if n <= cap:
        return n
    best = None
    for t in range(LANE, cap + 1, LANE):
        if n % t == 0:
            best = t
    assert best is not None, (n, cap)
    return best


def _pick_rows(n, cap):
    best = None
    for t in range(SUBLANE, min(n, cap) + 1, SUBLANE):
        if n % t == 0:
            best = t
    assert best is not None, (n, cap)
    return best


def _full_spec(arr):
    nd = arr.ndim
    return pl.BlockSpec(arr.shape, lambda *_: (0,) * nd)


def _dot(a, b):
    return lax.dot_general(a.astype(BF16), b.astype(BF16), (((1,), (0,)), ((), ())), preferred_element_type=F32)


def _dot_nt(a, b):
    return lax.dot_general(a.astype(BF16), b.astype(BF16), (((1,), (1,)), ((), ())), preferred_element_type=F32)


def _dot_tn(a, b):
    return lax.dot_general(a.astype(BF16), b.astype(BF16), (((0,), (0,)), ((), ())), preferred_element_type=F32)


def _dot_hi(a, b):
    return jnp.dot(a, b, precision=lax.Precision.HIGHEST, preferred_element_type=F32)


def _colsum(x):
    return jnp.sum(x, axis=0, keepdims=True)


def _sigmoid(x):
    return 1.0 / (1.0 + jnp.exp(-x))


_GELU_K = math.sqrt(2.0 / math.pi)
_GELU_C = 0.044715


def _gelu(x):
    t = jnp.tanh(_GELU_K * (x + _GELU_C * x * x * x))
    return 0.5 * x * (1.0 + t)


def _gelu_and_grad(x):
    x2 = x * x
    t = jnp.tanh(_GELU_K * (x + _GELU_C * x2 * x))
    g = 0.5 * x * (1.0 + t)
    dg = 0.5 * (1.0 + t) + 0.5 * x * (1.0 - t * t) * (_GELU_K * (1.0 + 3.0 * _GELU_C * x2))
    return g, dg


def _neg_expm1(x):
    series = x * (1.0 + x * (0.5 + x * (1.0 / 6.0 + x * (1.0 / 24.0 + x * (1.0 / 120.0)))))
    return -jnp.where(jnp.abs(x) < 0.1, series, jnp.exp(x) - 1.0)


def _seq_tile(s, want):
    t = min(s, want)
    assert s % t == 0
    return t


class _Rider:
    def __init__(self, srcs, kinds):
        self.srcs, self.kinds = list(srcs), list(kinds)
        self.n = len(self.srcs)

    def out_shapes(self):
        shapes = []
        for x, kind in zip(self.srcs, self.kinds):
            if kind == "lead":
                shp = x.shape
            elif kind == "rows":
                shp = (N_DEV, x.shape[0] // N_DEV) + x.shape[1:]
            else:
                shp = (N_DEV,) + x.shape
            shapes.append(jax.ShapeDtypeStruct(shp, x.dtype))
        return shapes

    def scratch(self):
        return [pltpu.SemaphoreType.DMA((self.n * N_PEERS,)), pltpu.SemaphoreType.DMA((self.n * N_PEERS,)),
                pltpu.SemaphoreType.DMA((self.n,))]

    def _copies(self, x_refs, out_refs, sems):
        send_sems, recv_sems, local_sems = sems
        mx, my, mc = lax.axis_index("x"), lax.axis_index("y"), lax.axis_index("c")
        my_id = 4 * mx + 2 * my + mc

        def piece(i, dev):
            if self.kinds[i] == "lead":
                return x_refs[i].at[dev]
            if self.kinds[i] == "rows":
                r = x_refs[i].shape[0] // N_DEV
                return x_refs[i].at[pl.ds(pl.multiple_of(dev * r, SUBLANE), r)]
            return x_refs[i]

        mine = [pltpu.make_async_copy(piece(i, my_id), out_refs[i].at[my_id], local_sems.at[i]) for i in range(self.n)]
        copies = []
        for k in range(1, N_DEV):
            px, py, pc = mx ^ ((k >> 2) & 1), my ^ ((k >> 1) & 1), mc ^ (k & 1)
            for i in range(self.n):
                copies.append(pltpu.make_async_remote_copy(
                    src_ref=piece(i, 4 * px + 2 * py + pc), dst_ref=out_refs[i].at[my_id],
                    send_sem=send_sems.at[i * N_PEERS + k - 1], recv_sem=recv_sems.at[i * N_PEERS + k - 1],
                    device_id=(px, py, pc), device_id_type=_MESH))
        return mine, copies

    def start(self, x_refs, out_refs, sems):
        mine, copies = self._copies(x_refs, out_refs, sems)
        for cp in mine + copies:
            cp.start()

    def wait(self, x_refs, out_refs, sems):
        mine, copies = self._copies(x_refs, out_refs, sems)
        for cp in copies:
            cp.wait_recv()
        for cp in copies:
            cp.wait_send()
        for cp in mine:
            cp.wait()


def _call(body, *, grid, ins, in_specs, outs, out_specs, scratch=(), aliases=None, name, rider=None):
    n_axes = len(grid)
    common = dict(grid=grid, input_output_aliases=aliases or {}, compiler_params=_cparams(n_axes), name=name)
    if rider is None:
        res = pl.pallas_call(body, in_specs=list(in_specs), out_specs=list(out_specs), out_shape=list(outs),
                             scratch_shapes=list(scratch), **common)(*ins)
        return list(res), []
    n_in, n_out, n_scr, nr = len(ins), len(outs), len(scratch), rider.n

    def wrapped(*refs):
        pos = [0]

        def take(k):
            part = refs[pos[0]:pos[0] + k]
            pos[0] += k
            return part

        a_in, r_in, a_out, r_out, a_scr, sems = take(n_in), take(nr), take(n_out), take(nr), take(n_scr), take(3)
        first = last = None
        for ax in range(n_axes):
            pid = pl.program_id(ax)
            f, l = pid == 0, pid == grid[ax] - 1
            first = f if first is None else jnp.logical_and(first, f)
            last = l if last is None else jnp.logical_and(last, l)

        @pl.when(first)
        def _():
            rider.start(r_in, r_out, sems)

        body(*a_in, *a_out, *a_scr)

        @pl.when(last)
        def _():
            rider.wait(r_in, r_out, sems)

    res = pl.pallas_call(
        wrapped, in_specs=list(in_specs) + [_ANY] * nr, out_specs=list(out_specs) + [_ANY] * nr,
        out_shape=list(outs) + rider.out_shapes(), scratch_shapes=list(scratch) + rider.scratch(), **common)(*ins, *rider.srcs)
    return list(res[:n_out]), list(res[n_out:])


def _block_mask(n_blocks, block_rows, block_cols):
    r = jnp.arange(n_blocks * block_rows) // block_rows
    c = jnp.arange(n_blocks * block_cols) // block_cols
    return (r[:, None] == c[None, :]).astype(F32)


def _mm(a, b, *, bias=None, res=None, res_scale=1.0, trans_b=False, out_dtype=F32, ln=None, name, rider=None):
    m, kdim = a.shape
    n = b.shape[0] if trans_b else b.shape[1]
    tm = _seq_tile(m, 1024)
    tn = _pick(n, 1408)
    tk = _pick(kdim, 1536)
    nk = kdim // tk
    has_bias, has_res, has_ln = bias is not None, res is not None, ln is not None
    assert not has_ln or tn == n

    def body(*refs):
        a_ref, b_ref = refs[0], refs[1]
        pos = 2
        bias_ref = res_ref = g_ref = beta_ref = x_ref = None
        if has_bias:
            bias_ref = refs[pos]
            pos += 1
        if has_res:
            res_ref = refs[pos]
            pos += 1
        if has_ln:
            g_ref, beta_ref = refs[pos], refs[pos + 1]
            pos += 2
        o_ref = refs[pos]
        pos += 1
        if has_ln:
            x_ref = refs[pos]
            pos += 1
        acc_ref = refs[pos]
        k = pl.program_id(2)

        @pl.when(k == 0)
        def _():
            acc_ref[...] = jnp.zeros_like(acc_ref)

        if trans_b:
            acc_ref[...] += _dot_nt(a_ref[...], b_ref[...])
        else:
            acc_ref[...] += _dot(a_ref[...], b_ref[...])

        @pl.when(k == nk - 1)
        def _():
            r = acc_ref[...]
            if has_bias:
                r = r + bias_ref[...]
            if has_res:
                r = r + res_scale * res_ref[...]
            o_ref[...] = r.astype(out_dtype)
            if has_ln:
                xc = r - jnp.mean(r, axis=1, keepdims=True)
                var = jnp.mean(xc * xc, axis=1, keepdims=True)
                x_ref[...] = xc * lax.rsqrt(var + LN_EPS) * g_ref[...] + beta_ref[...]

    ins = [a, b]
    in_specs = [pl.BlockSpec((tm, tk), lambda i, j, k: (i, k)),
                pl.BlockSpec((tn, tk), lambda i, j, k: (j, k)) if trans_b
                else pl.BlockSpec((tk, tn), lambda i, j, k: (k, j))]
    if has_bias:
        ins.append(bias)
        in_specs.append(pl.BlockSpec((1, tn), lambda i, j, k: (0, j)))
    if has_res:
        ins.append(res)
        in_specs.append(pl.BlockSpec((tm, tn), lambda i, j, k: (i, j)))
    if has_ln:
        ins += list(ln)
        in_specs += [pl.BlockSpec((1, tn), lambda i, j, k: (0, j))] * 2
    tile = pl.BlockSpec((tm, tn), lambda i, j, k: (i, j))
    outs, routs = _call(
        body, grid=(m // tm, n // tn, nk), ins=ins, in_specs=in_specs,
        outs=[jax.ShapeDtypeStruct((m, n), out_dtype)] + ([jax.ShapeDtypeStruct((m, n), F32)] if has_ln else []),
        out_specs=[tile] * (2 if has_ln else 1),
        scratch=[pltpu.VMEM((tm, tn), F32)], name=name, rider=rider)
    out = tuple(outs) if has_ln else outs[0]
    return out if rider is None else (out, routs)


def _mm_tn(a, b, *, colsum=False, out_dtype=F32, dev_cols=None, name, rider=None):
    s, ka = a.shape
    nb = b.shape[1]
    ts = _seq_tile(s, 512)
    tka = _pick(ka, 1408)
    tnb = _pick(nb, 1408)
    nk = s // ts
    assert not colsum or tka == ka
    per_tile = 1 if dev_cols is None else tnb // dev_cols
    assert dev_cols is None or tnb == per_tile * dev_cols

    def body(a_ref, b_ref, o_ref, *rest):
        cs_ref = rest[0] if colsum else None
        acc_ref = rest[-1]
        k = pl.program_id(2)

        @pl.when(k == 0)
        def _():
            acc_ref[...] = jnp.zeros_like(acc_ref)
            if colsum:
                cs_ref[...] = jnp.zeros_like(cs_ref)

        bv = b_ref[...]
        acc_ref[...] += _dot_tn(a_ref[...], bv)
        if colsum:
            cs_ref[...] += _colsum(bv.astype(F32))

        @pl.when(k == nk - 1)
        def _():
            if dev_cols is None:
                o_ref[...] = acc_ref[...].astype(out_dtype)
            else:
                for d in range(per_tile):
                    o_ref[d] = acc_ref[:, d * dev_cols:(d + 1) * dev_cols].astype(out_dtype)

    if dev_cols is None:
        main_shape, main_spec = (ka, nb), pl.BlockSpec((tka, tnb), lambda i, j, k: (i, j))
    else:
        main_shape = (nb // dev_cols, ka, dev_cols)
        main_spec = pl.BlockSpec((per_tile, tka, dev_cols), lambda i, j, k: (j, i, 0))
    outs, routs = _call(
        body, grid=(ka // tka, nb // tnb, nk), ins=[a, b],
        in_specs=[pl.BlockSpec((ts, tka), lambda i, j, k: (k, i)), pl.BlockSpec((ts, tnb), lambda i, j, k: (k, j))],
        outs=[jax.ShapeDtypeStruct(main_shape, out_dtype)] + ([jax.ShapeDtypeStruct((1, nb), F32)] if colsum else []),
        out_specs=[main_spec] + ([pl.BlockSpec((1, tnb), lambda i, j, k: (0, j))] if colsum else []),
        scratch=[pltpu.VMEM((tka, tnb), F32)], name=name, rider=rider)
    out = tuple(outs) if colsum else outs[0]
    return out if rider is None else (out, routs)


def _ln_fwd(r, g, b, *, name, rider=None):
    s, d = r.shape
    ts = _seq_tile(s, 512)

    def body(r_ref, g_ref, b_ref, o_ref):
        x = r_ref[...]
        mu = jnp.mean(x, axis=1, keepdims=True)
        xc = x - mu
        var = jnp.mean(xc * xc, axis=1, keepdims=True)
        o_ref[...] = xc * lax.rsqrt(var + LN_EPS) * g_ref[...] + b_ref[...]

    (out,), routs = _call(
        body, grid=(s // ts,), ins=[r, g, b],
        in_specs=[pl.BlockSpec((ts, d), lambda i: (i, 0)), _full_spec(g), _full_spec(b)],
        out_specs=[pl.BlockSpec((ts, d), lambda i: (i, 0))], outs=[jax.ShapeDtypeStruct((s, d), F32)],
        name=name, rider=rider)
    return out if rider is None else (out, routs)


def _ln_bwd(r, dy, g, *, name, rider=None):
    s, d = r.shape
    ts = _seq_tile(s, 512)

    def body(r_ref, dy_ref, g_ref, dr_ref, dg_ref, db_ref, ds_ref):
        @pl.when(pl.program_id(0) == 0)
        def _():
            dg_ref[...] = jnp.zeros_like(dg_ref)
            db_ref[...] = jnp.zeros_like(db_ref)
            ds_ref[...] = jnp.zeros_like(ds_ref)

        x = r_ref[...]
        dy = dy_ref[...]
        mu = jnp.mean(x, axis=1, keepdims=True)
        xc = x - mu
        var = jnp.mean(xc * xc, axis=1, keepdims=True)
        rstd = lax.rsqrt(var + LN_EPS)
        xh = xc * rstd
        dxh = dy * g_ref[...]
        m1 = jnp.mean(dxh, axis=1, keepdims=True)
        m2 = jnp.mean(dxh * xh, axis=1, keepdims=True)
        dr = rstd * (dxh - m1 - xh * m2)
        dr_ref[...] = dr
        dg_ref[...] += _colsum(dy * xh)
        db_ref[...] += _colsum(dy)
        ds_ref[...] += _colsum(dr)

    vec = jax.ShapeDtypeStruct((1, d), F32)
    vspec = pl.BlockSpec((1, d), lambda i: (0, 0))
    outs, routs = _call(
        body, grid=(s // ts,), ins=[r, dy, g],
        in_specs=[pl.BlockSpec((ts, d), lambda i: (i, 0)), pl.BlockSpec((ts, d), lambda i: (i, 0)), _full_spec(g)],
        out_specs=[pl.BlockSpec((ts, d), lambda i: (i, 0)), vspec, vspec, vspec],
        outs=[jax.ShapeDtypeStruct((s, d), F32), vec, vec, vec], name=name, rider=rider)
    return outs if rider is None else (outs, routs)


def _loss_ln_bwd(r, g, b, target, *, name):
    s, d = r.shape
    ts = _seq_tile(s, 512)

    def body(r_ref, g_ref, b_ref, t_ref, dr_ref, dg_ref, db_ref, l_ref):
        @pl.when(pl.program_id(0) == 0)
        def _():
            dg_ref[...] = jnp.zeros_like(dg_ref)
            db_ref[...] = jnp.zeros_like(db_ref)
            l_ref[...] = jnp.zeros_like(l_ref)

        x = r_ref[...]
        gam = g_ref[...]
        xc = x - jnp.mean(x, axis=1, keepdims=True)
        var = jnp.mean(xc * xc, axis=1, keepdims=True)
        rstd = lax.rsqrt(var + LN_EPS)
        xh = xc * rstd
        e = xh * gam + b_ref[...] - t_ref[...]
        part = jnp.sum(jnp.sum(e * e, axis=1, keepdims=True), axis=0, keepdims=True) * (0.5 / d)
        l_ref[...] += jnp.broadcast_to(part, l_ref.shape)
        dy = e * (1.0 / d)
        dxh = dy * gam
        m1 = jnp.mean(dxh, axis=1, keepdims=True)
        m2 = jnp.mean(dxh * xh, axis=1, keepdims=True)
        dr_ref[...] = rstd * (dxh - m1 - xh * m2)
        dg_ref[...] += _colsum(dy * xh)
        db_ref[...] += _colsum(dy)

    vec = jax.ShapeDtypeStruct((1, d), F32)
    vspec = pl.BlockSpec((1, d), lambda i: (0, 0))
    tile = pl.BlockSpec((ts, d), lambda i: (i, 0))
    return pl.pallas_call(
        body, grid=(s // ts,), in_specs=[tile, _full_spec(g), _full_spec(b), tile],
        out_specs=[tile, vspec, vspec, pl.BlockSpec((SUBLANE, LANE), lambda i: (0, 0))],
        out_shape=[jax.ShapeDtypeStruct((s, d), F32), vec, vec, jax.ShapeDtypeStruct((SUBLANE, LANE), F32)],
        compiler_params=_cparams(1), name=name)(r, g, b, target)


SCAN_CHUNK = 32


def _cscan_levels(bufs, apow_ref, t, pad, *, reverse):
    half = bufs[0].shape[1] // 2
    ch = min(SCAN_CHUNK, t)
    nlev = t.bit_length() - 1
    assert (1 << nlev) == t
    for k in range(nlev):
        d = 1 << k
        src, dst = bufs[k % 2], bufs[(k + 1) % 2]

        def chunk(c, carry, src=src, dst=dst, d=d, k=k):
            ar = apow_ref[k:k + 1, :half]
            ai = apow_ref[k:k + 1, half:]
            if reverse:
                ai = -ai
            r0 = pl.multiple_of(c * ch, ch)
            cur = src[pl.ds(pad + r0, ch), :]
            if d >= SUBLANE:
                off = pad + d if reverse else pad - d
                sh = src[pl.ds(off + r0, ch), :]
            elif reverse:
                blk = src[pl.ds(pad + r0, ch + SUBLANE), :]
                sh = pltpu.roll(blk, ch + SUBLANE - d, axis=0)[:ch, :]
            else:
                blk = src[pl.ds(pad - SUBLANE + r0, ch + SUBLANE), :]
                sh = pltpu.roll(blk, d, axis=0)[SUBLANE:, :]
            sre, sim = sh[:, :half], sh[:, half:]
            dst[pl.ds(pad + r0, ch), :half] = cur[:, :half] + ar * sre - ai * sim
            dst[pl.ds(pad + r0, ch), half:] = cur[:, half:] + ar * sim + ai * sre
            return carry

        lax.fori_loop(0, t // ch, chunk, 0)
    return nlev % 2


def _rscan_levels(abufs, bbufs, t, pad, *, reverse):
    nlev = t.bit_length() - 1
    assert (1 << nlev) == t
    for k in range(nlev):
        d = 1 << k
        asrc, adst = abufs[k % 2], abufs[(k + 1) % 2]
        bsrc, bdst = bbufs[k % 2], bbufs[(k + 1) % 2]
        off = pad + d if reverse else pad - d
        a = asrc[pad:pad + t, :]
        bdst[pad:pad + t, :] = a * bsrc[off:off + t, :] + bsrc[pad:pad + t, :]
        if k < nlev - 1:
            adst[pad:pad + t, :] = a * asrc[off:off + t, :]
    return nlev % 2


S5_CHUNK = 16
S5_SG = S5_GROUPS // 2
S5_SG_IN = 2 * S5_CHUNK * S5_CH
S5_SG_ST = 2 * S5_STATE


S5_HALF_SGS = S5_SG // 2
S5_HALF_IN = S5_HALF_SGS * S5_SG_IN


def _s5_perm():
    idx = jnp.arange(S5_HALF_IN)
    step, grp, chan = idx // LANE, (idx % LANE) // S5_CH, idx % S5_CH
    col = (grp // 2) * S5_SG_IN + (grp % 2) * (S5_CHUNK * S5_CH) + step * S5_CH + chan
    return (col[:, None] == idx[None, :]).astype(BF16)


def _s5_to_chunks(x, col_block, perm, *, name):
    s = x.shape[0]
    nb = s // S5_CHUNK

    def body(x_ref, perm_ref, o_ref):
        tok = jnp.concatenate([x_ref[pl.ds(t, nb, stride=S5_CHUNK), :].astype(BF16) for t in range(S5_CHUNK)], axis=1)
        grouped = _dot(tok, perm_ref[...]).astype(BF16)
        for k in range(S5_HALF_SGS):
            o_ref[k] = grouped[:, k * S5_SG_IN:(k + 1) * S5_SG_IN]

    return pl.pallas_call(
        body, grid=(2,),
        in_specs=[pl.BlockSpec((s, LANE), lambda h: (0, col_block + h)), _full_spec(perm)],
        out_specs=pl.BlockSpec((S5_HALF_SGS, nb, S5_SG_IN), lambda h: (h, 0, 0)),
        out_shape=jax.ShapeDtypeStruct((S5_SG, nb, S5_SG_IN), BF16),
        compiler_params=_cparams(1), name=name)(x, perm)


def _s5_from_chunks(y, perm, *, name):
    _, nb, _ = y.shape

    def body(y_ref, perm_ref, o_ref):
        grouped = jnp.concatenate([y_ref[k] for k in range(S5_HALF_SGS)], axis=1)
        hi = grouped.astype(BF16)
        lo = (grouped - hi.astype(F32)).astype(BF16)
        tok = _dot_nt(hi, perm_ref[...]) + _dot_nt(lo, perm_ref[...])
        for t in range(S5_CHUNK):
            o_ref[pl.ds(t, nb, stride=S5_CHUNK), :] = tok[:, t * LANE:(t + 1) * LANE]

    return pl.pallas_call(
        body, grid=(2,),
        in_specs=[pl.BlockSpec((S5_HALF_SGS, nb, S5_SG_IN), lambda h: (h, 0, 0)), _full_spec(perm)],
        out_specs=pl.BlockSpec((nb * S5_CHUNK, LANE), lambda h: (0, h)),
        out_shape=jax.ShapeDtypeStruct((nb * S5_CHUNK, D_GROUP), F32),
        compiler_params=_cparams(1), name=name)(y, perm)


def _s5_core_fwd(u2, m2, pre, pim, qre, qim, a16, *, name):
    sg, nb, nin = u2.shape
    st2 = 2 * S5_SG_ST
    pad = nb // 2

    def body(u_ref, m_ref, pre_ref, pim_ref, qre_ref, qim_ref, a_ref, y_ref, x_ref, buf0, buf1):
        @pl.when(pl.program_id(0) == 0)
        def _():
            buf0[0:pad, :] = jnp.zeros((pad, st2), F32)
            buf1[0:pad, :] = jnp.zeros((pad, st2), F32)

        u = u_ref[...]
        buf0[pad:pad + nb, :S5_SG_ST] = _dot(u, pre_ref[...])
        buf0[pad:pad + nb, S5_SG_ST:] = _dot(u, pim_ref[...])
        xbuf = (buf0, buf1)[_cscan_levels((buf0, buf1), a_ref, nb, pad, reverse=False)]
        x_ref[...] = xbuf[pad:pad + nb, :]
        xprev = xbuf[pad - 1:pad - 1 + nb, :]
        y_ref[...] = _dot(u, m_ref[...]) + _dot(xprev[:, :S5_SG_ST], qre_ref[...]) + _dot(xprev[:, S5_SG_ST:], qim_ref[...])

    ins = [u2, m2, pre, pim, qre, qim, a16]
    return pl.pallas_call(
        body, grid=(sg,), in_specs=[pl.BlockSpec((None,) + a.shape[1:], lambda i: (i, 0, 0)) for a in ins],
        out_specs=[pl.BlockSpec((None, nb, nin), lambda i: (i, 0, 0)), pl.BlockSpec((None, nb, st2), lambda i: (i, 0, 0))],
        out_shape=[jax.ShapeDtypeStruct((sg, nb, nin), F32), jax.ShapeDtypeStruct((sg, nb, st2), F32)],
        scratch_shapes=[pltpu.VMEM((pad + nb, st2), F32), pltpu.VMEM((pad + nb, st2), F32)],
        compiler_params=_cparams(1), name=name)(*ins)


def _s5_core_bwd(u2, dy2, x_all, m2, pre, pim, qre, qim, a16, *, name):
    sg, nb, nin = u2.shape
    half = S5_SG_ST
    st2 = 2 * half
    pad = nb // 2

    def body(u_ref, dy_ref, x_ref, m_ref, pre_ref, pim_ref, qre_ref, qim_ref, a_ref,
             du_ref, dm_ref, dpre_ref, dpim_ref, dqre_ref, dqim_ref, da_ref, buf2, buf3, xp):
        @pl.when(pl.program_id(0) == 0)
        def _():
            buf2[nb:nb + pad, :] = jnp.zeros((pad, st2), F32)
            buf3[nb:nb + pad, :] = jnp.zeros((pad, st2), F32)
            xp[0:SUBLANE, :] = jnp.zeros((SUBLANE, st2), F32)

        u = u_ref[...]
        dy = dy_ref[...]
        dm_ref[...] = _dot_tn(u, dy)
        xp[SUBLANE:SUBLANE + nb, :] = x_ref[...]
        xprev = xp[SUBLANE - 1:SUBLANE - 1 + nb, :]
        xre, xim = xprev[:, :half], xprev[:, half:]
        dqre_ref[...] = _dot_tn(xre, dy)
        dqim_ref[...] = _dot_tn(xim, dy)
        buf2[0:nb, :half] = _dot_nt(dy, qre_ref[...])
        buf2[0:nb, half:] = _dot_nt(dy, qim_ref[...])
        mbuf = (buf2, buf3)[_cscan_levels((buf2, buf3), a_ref, nb, 0, reverse=True)]
        lam = mbuf[1:1 + nb, :]
        lre, lim = lam[:, :half], lam[:, half:]
        dpre_ref[...] = _dot_tn(u, lre)
        dpim_ref[...] = _dot_tn(u, lim)
        du_ref[...] = _dot_nt(dy, m_ref[...]) + _dot_nt(lre, pre_ref[...]) + _dot_nt(lim, pim_ref[...])
        da_ref[:, :half] = _colsum(lre * xre + lim * xim)
        da_ref[:, half:] = _colsum(lim * xre - lre * xim)

    ins = [u2, dy2, x_all, m2, pre, pim, qre, qim, a16]
    outs = [jax.ShapeDtypeStruct((sg, nb, nin), F32)] + [jax.ShapeDtypeStruct(a.shape, F32) for a in (m2, pre, pim, qre, qim)] + \
           [jax.ShapeDtypeStruct((sg, 1, st2), F32)]
    return pl.pallas_call(
        body, grid=(sg,), in_specs=[pl.BlockSpec((None,) + a.shape[1:], lambda i: (i, 0, 0)) for a in ins],
        out_specs=[pl.BlockSpec((None,) + o.shape[1:], lambda i: (i, 0, 0)) for o in outs], out_shape=outs,
        scratch_shapes=[pltpu.VMEM((nb + pad, st2), F32), pltpu.VMEM((nb + pad, st2), F32),
                        pltpu.VMEM((SUBLANE + nb, st2), F32)],
        compiler_params=_cparams(1), name=name)(*ins)


def _s5_glu_fwd(y1, wglu, bglu, *, name, rider=None):
    s = y1.shape[0]
    t = _seq_tile(s, 512)

    def body(y1_ref, wglu_ref, bglu_ref, out_ref):
        y2 = _gelu(y1_ref[...])
        out_ref[...] = (y2 * _sigmoid(_dot(y2, wglu_ref[...]) + bglu_ref[...])).astype(BF16)

    return _call(
        body, grid=(s // t,), ins=[y1, wglu, bglu],
        in_specs=[pl.BlockSpec((t, D_GROUP), lambda i: (i, 0)), _full_spec(wglu), _full_spec(bglu)],
        out_specs=[pl.BlockSpec((t, D_GROUP), lambda i: (i, MIX_S5))], outs=[jax.ShapeDtypeStruct((s, D_MODEL), BF16)],
        name=name, rider=rider)


def _s5_glu_bwd(y1, dmix, wglu, bglu, *, name):
    s = y1.shape[0]
    t = _seq_tile(s, 512)

    def body(y1_ref, do_ref, wglu_ref, bglu_ref, dy1_ref, dwglu_ref, dbglu_ref):
        @pl.when(pl.program_id(0) == 0)
        def _():
            dwglu_ref[...] = jnp.zeros_like(dwglu_ref)
            dbglu_ref[...] = jnp.zeros_like(dbglu_ref)

        dout = do_ref[...]
        y2, dgelu = _gelu_and_grad(y1_ref[...])
        sg = _sigmoid(_dot(y2, wglu_ref[...]) + bglu_ref[...])
        dz = dout * y2 * sg * (1.0 - sg)
        dwglu_ref[...] += _dot_tn(y2, dz)
        dbglu_ref[...] += _colsum(dz)
        dy1_ref[...] = (dout * sg + _dot_nt(dz, wglu_ref[...])) * dgelu

    outs = [jax.ShapeDtypeStruct((s, D_GROUP), F32), jax.ShapeDtypeStruct((D_GROUP, D_GROUP), F32),
            jax.ShapeDtypeStruct((1, D_GROUP), F32)]
    return pl.pallas_call(
        body, grid=(s // t,),
        in_specs=[pl.BlockSpec((t, D_GROUP), lambda i: (i, 0)), pl.BlockSpec((t, D_GROUP), lambda i: (i, MIX_S5)),
                  _full_spec(wglu), _full_spec(bglu)],
        out_specs=[pl.BlockSpec((t, D_GROUP), lambda i: (i, 0)), _full_spec(outs[1]), _full_spec(outs[2])],
        out_shape=outs, compiler_params=_cparams(1), name=name)(y1, dmix, wglu, bglu)


def _pair_blockdiag(x):
    g, r, c = x.shape
    x = x.reshape(g // 2, 2, r, c)
    z = jnp.zeros_like(x[:, 0])
    return jnp.concatenate([jnp.concatenate([x[:, 0], z], axis=2), jnp.concatenate([z, x[:, 1]], axis=2)], axis=1)


def _s5_chunk_map(lam_re, lam_im, log_dt, b_re, b_im, c_re, c_im, d_skip):
    g, n, c, lc = S5_GROUPS, S5_STATE, S5_CH, S5_CHUNK
    dt = jnp.exp(log_dt)[:, None]
    mag, ang = lam_re * dt, lam_im * dt
    j = jnp.arange(lc + 1, dtype=F32)[:, None, None]
    pw_mag = jnp.exp(j * mag)
    pw_re, pw_im = pw_mag * jnp.cos(j * ang), pw_mag * jnp.sin(j * ang)
    a_re, a_im = pw_re[1], pw_im[1]
    den = lam_re * lam_re + lam_im * lam_im
    n_re = a_re - 1.0
    k_re = (n_re * lam_re + a_im * lam_im) / den
    k_im = (a_im * lam_re - n_re * lam_im) / den
    bb_re = k_re[..., None] * b_re - k_im[..., None] * b_im
    bb_im = k_re[..., None] * b_im + k_im[..., None] * b_re
    e_re = pw_re[:lc, :, :, None] * bb_re - pw_im[:lc, :, :, None] * bb_im
    e_im = pw_re[:lc, :, :, None] * bb_im + pw_im[:lc, :, :, None] * bb_re
    kern = jnp.einsum("gdn,jgnc->jgdc", c_re, e_re) - jnp.einsum("gdn,jgnc->jgdc", c_im, e_im)
    lags = jnp.pad(jnp.transpose(kern, (1, 3, 0, 2)), ((0, 0), (0, 0), (lc - 1, 0), (0, 0)))
    lags = lags.reshape(g, c, (2 * lc - 1) * c)
    m = jnp.stack([lags[:, :, (lc - 1 - s) * c:(2 * lc - 1 - s) * c] for s in range(lc)], axis=1).reshape(g, lc * c, lc * c)
    skip = jnp.tile(d_skip.reshape(g, 1, c), (1, lc, 1)).reshape(g, lc * c)
    m = m + jnp.eye(lc * c, dtype=F32)[None] * skip[:, None, :]
    p_re = jnp.transpose(e_re[::-1], (1, 0, 3, 2)).reshape(g, lc * c, n)
    p_im = jnp.transpose(e_im[::-1], (1, 0, 3, 2)).reshape(g, lc * c, n)
    f_re = c_re[None] * pw_re[1:, :, None, :] - c_im[None] * pw_im[1:, :, None, :]
    f_im = c_re[None] * pw_im[1:, :, None, :] + c_im[None] * pw_re[1:, :, None, :]
    q_re = jnp.transpose(f_re, (1, 3, 0, 2)).reshape(g, n, lc * c)
    q_im = -jnp.transpose(f_im, (1, 3, 0, 2)).reshape(g, n, lc * c)
    a16 = jnp.concatenate([pw_re[lc].reshape(S5_SG, 1, S5_SG_ST), pw_im[lc].reshape(S5_SG, 1, S5_SG_ST)], axis=2)
    return (_pair_blockdiag(m), _pair_blockdiag(p_re), _pair_blockdiag(p_im), _pair_blockdiag(q_re),
            _pair_blockdiag(q_im), a16)


def _s5_a16_powers(a16, nlev):
    half = S5_SG_ST
    re, im = a16[:, :, :half], a16[:, :, half:]
    rows = []
    for _ in range(nlev):
        rows.append(jnp.concatenate([re, im], axis=2))
        re, im = re * re - im * im, 2.0 * re * im
    n_rows = -(-nlev // SUBLANE) * SUBLANE
    rows += [jnp.zeros_like(rows[0])] * (n_rows - nlev)
    return lax.stop_gradient(jnp.concatenate(rows, axis=1))


CV_TILE = 256
CV_PAD = 32
CV_CHUNK = 64


def _shifted_copies(buf, shifted, rows):
    n = rows - SUBLANE
    for s in range(1, SUBLANE):
        shifted[s - 1, 0:n, :] = buf[s:s + n, :]


def _window(buf, shifted, o, ch):
    q, s = divmod(o, SUBLANE)
    if s == 0:
        return buf[o:o + ch, :]
    return shifted[s - 1, q * SUBLANE:q * SUBLANE + ch, :]


def _gn_stats(c, mavg):
    mu = _dot_hi(c, mavg)
    cen = c - mu
    var = _dot_hi(cen * cen, mavg)
    rstd = lax.rsqrt(var + LN_EPS)
    return cen * rstd, rstd


def _cv_fwd(h_in, cw, cb, gng, gnb, mavg, wpw, bpw, mix, *, name, rider=None):
    s = h_in.shape[0]
    t = _seq_tile(s, CV_TILE)
    ch = min(CV_CHUNK, t)

    def body(v_ref, g_ref, cw_ref, cb_ref, gng_ref, gnb_ref, mavg_ref, wpw_ref, bpw_ref, _mix_in, out_ref, c_ref, xpad,
             shifted):
        @pl.when(pl.program_id(0) == 0)
        def _():
            xpad[0:CV_PAD, :] = jnp.zeros((CV_PAD, D_GROUP), F32)

        xpad[CV_PAD:CV_PAD + t, :] = v_ref[...] * _sigmoid(g_ref[...])
        _shifted_copies(xpad, shifted, t + CV_PAD)
        for r0 in range(0, t, ch):
            acc = jnp.broadcast_to(cb_ref[...], (ch, D_GROUP))
            for k in range(CONV_WIDTH):
                o = CV_PAD - (CONV_WIDTH - 1) + k + r0
                acc = acc + cw_ref[k:k + 1, :] * _window(xpad, shifted, o, ch)
            c_ref[r0:r0 + ch, :] = acc
        xpad[0:CV_PAD, :] = xpad[t:t + CV_PAD, :]
        xn, _ = _gn_stats(c_ref[...], mavg_ref[...])
        gn = xn * gng_ref[...] + gnb_ref[...]
        out_ref[...] = (_dot(gn * _sigmoid(gn), wpw_ref[...]) + bpw_ref[...]).astype(BF16)

    ins = [h_in, h_in, cw, cb, gng, gnb, mavg, wpw, bpw, mix]
    in_specs = [pl.BlockSpec((t, D_GROUP), lambda i: (i, COL_CV_V)), pl.BlockSpec((t, D_GROUP), lambda i: (i, COL_CV_G))] + \
               [_full_spec(a) for a in ins[2:9]] + [_ANY]
    return _call(
        body, grid=(s // t,), ins=ins, in_specs=in_specs,
        out_specs=[pl.BlockSpec((t, D_GROUP), lambda i: (i, MIX_CV)), pl.BlockSpec((t, D_GROUP), lambda i: (i, 0))],
        outs=[jax.ShapeDtypeStruct((s, D_MODEL), BF16), jax.ShapeDtypeStruct((s, D_GROUP), F32)],
        aliases={9: 0},
        scratch=[pltpu.VMEM((CV_PAD + t, D_GROUP), F32), pltpu.VMEM((SUBLANE - 1, CV_PAD + t, D_GROUP), F32)],
        name=name, rider=rider)


def _cv_bwd(h_in, c, dmix, cw, gng, gnb, mavg, wpw, *, name, rider=None):
    s = h_in.shape[0]
    t = _seq_tile(s, CV_TILE)
    nt = s // t
    ch = min(CV_CHUNK, t)

    def body(v_ref, g_ref, c_ref, do_ref, cw_ref, gng_ref, gnb_ref, mavg_ref, wpw_ref,
             dvg_ref, dwpw_ref, dcw_ref, dbpw_ref, dgg_ref, dgb_ref, dcb_ref, dcpad, hgbuf, shifted):
        @pl.when(pl.program_id(0) == 0)
        def _():
            dcpad[t:t + CV_PAD, :] = jnp.zeros((CV_PAD, D_GROUP), F32)
            for r in (dwpw_ref, dcw_ref, dbpw_ref, dgg_ref, dgb_ref, dcb_ref):
                r[...] = jnp.zeros_like(r)

        mavg = mavg_ref[...]
        xn, rstd = _gn_stats(c_ref[...], mavg)
        gg = gng_ref[...]
        gn = xn * gg + gnb_ref[...]
        sg = _sigmoid(gn)
        dout = do_ref[...]
        dwpw_ref[...] += _dot_tn(gn * sg, dout)
        dbpw_ref[...] += _colsum(dout)
        dgn = _dot_nt(dout, wpw_ref[...]) * (sg * (1.0 + gn * (1.0 - sg)))
        dgg_ref[...] += _colsum(dgn * xn)
        dgb_ref[...] += _colsum(dgn)
        dxn = dgn * gg
        dc = rstd * (dxn - _dot_hi(dxn, mavg) - xn * _dot_hi(dxn * xn, mavg))
        dcb_ref[...] += _colsum(dc)
        dcpad[0:t, :] = dc

        v = v_ref[...]
        sgm = _sigmoid(g_ref[...])
        hgbuf[...] = v * sgm
        _shifted_copies(dcpad, shifted, t + CV_PAD)
        for r0 in range(0, t, ch):
            hg = hgbuf[r0:r0 + ch, :]
            acc = jnp.zeros((ch, D_GROUP), F32)
            for k in range(CONV_WIDTH):
                o = (CONV_WIDTH - 1) - k + r0
                sh = _window(dcpad, shifted, o, ch)
                acc = acc + cw_ref[k:k + 1, :] * sh
                dcw_ref[k:k + 1, :] += _colsum(hg * sh)
            hgbuf[r0:r0 + ch, :] = acc
        dcpad[t:t + CV_PAD, :] = dcpad[0:CV_PAD, :]
        dhg = hgbuf[...]
        dvg_ref[:, :D_GROUP] = dhg * sgm
        dvg_ref[:, D_GROUP:] = dhg * v * sgm * (1.0 - sgm)

    def rev(col):
        return lambda i: (nt - 1 - i, col)

    ins = [h_in, h_in, c, dmix, cw, gng, gnb, mavg, wpw]
    in_specs = [pl.BlockSpec((t, D_GROUP), rev(COL_CV_V)), pl.BlockSpec((t, D_GROUP), rev(COL_CV_G)),
                pl.BlockSpec((t, D_GROUP), rev(0)), pl.BlockSpec((t, D_GROUP), rev(MIX_CV))] + [_full_spec(a) for a in ins[4:]]
    vec = jax.ShapeDtypeStruct((1, D_GROUP), F32)
    outs = [jax.ShapeDtypeStruct((s, N_IN_COLS), F32),
            jax.ShapeDtypeStruct((D_GROUP, D_GROUP), F32), jax.ShapeDtypeStruct((CV_PAD, D_GROUP), F32), vec, vec, vec, vec]
    out_specs = [pl.BlockSpec((t, 2 * D_GROUP), rev(COL_CV_V // 2))] + [_full_spec(o) for o in outs[1:]]
    return _call(
        body, grid=(nt,), ins=ins, in_specs=in_specs, out_specs=out_specs, outs=outs,
        scratch=[pltpu.VMEM((t + CV_PAD, D_GROUP), F32), pltpu.VMEM((t, D_GROUP), F32),
                 pltpu.VMEM((SUBLANE - 1, t + CV_PAD, D_GROUP), F32)], name=name, rider=rider)


LRU_TILE = 256


def _lru_gates(xc, wr_ref, br_ref, wi_ref, bi_ref, sp_ref):
    r = _sigmoid(_dot(xc, wr_ref[...]) + br_ref[...])
    i = _sigmoid(_dot(xc, wi_ref[...]) + bi_ref[...])
    log_a = -LRU_C * r * sp_ref[...]
    a = jnp.exp(log_a)
    m = jnp.sqrt(_neg_expm1(2.0 * log_a))
    return r, i, a, m


def _lru_fwd(h_in, lcw, lcb, wr, br, wi, bi, sp, mix, *, name):
    s = h_in.shape[0]
    t = _seq_tile(s, LRU_TILE)
    pad = max(t // 2, SUBLANE)

    def body(xg_ref, xr_ref, lcw_ref, lcb_ref, wr_ref, br_ref, wi_ref, bi_ref, sp_ref, _mix_in,
             out_ref, xc_ref, h_ref, xpad, a0, a1, b0, b1, carry):
        @pl.when(pl.program_id(0) == 0)
        def _():
            xpad[0:SUBLANE, :] = jnp.zeros((SUBLANE, D_GROUP), F32)
            for bf in (a0, a1, b0, b1):
                bf[0:pad, :] = jnp.zeros((pad, D_GROUP), F32)
            carry[...] = jnp.zeros_like(carry)

        xpad[SUBLANE:SUBLANE + t, :] = xr_ref[...]
        xc = jnp.broadcast_to(lcb_ref[...], (t, D_GROUP))
        for k in range(LRU_CONV_WIDTH):
            o = SUBLANE - (LRU_CONV_WIDTH - 1) + k
            xc = xc + lcw_ref[k:k + 1, :] * xpad[o:o + t, :]
        xpad[0:SUBLANE, :] = xpad[t:t + SUBLANE, :]
        xc_ref[...] = xc
        _, i, a, m = _lru_gates(xc, wr_ref, br_ref, wi_ref, bi_ref, sp_ref)
        a0[pad:pad + t, :] = a
        b0[pad:pad + t, :] = m * (i * xc)
        b0[pad:pad + 1, :] += a0[pad:pad + 1, :] * carry[0:1, :]
        fin = _rscan_levels((a0, a1), (b0, b1), t, pad, reverse=False)
        hbuf = (b0, b1)[fin]
        carry[0:1, :] = hbuf[pad + t - 1:pad + t, :]
        h = hbuf[pad:pad + t, :]
        h_ref[...] = h
        out_ref[...] = (h * _gelu(xg_ref[...])).astype(BF16)

    ins = [h_in, h_in, lcw, lcb, wr, br, wi, bi, sp, mix]
    row = pl.BlockSpec((t, D_GROUP), lambda i: (i, 0))
    in_specs = [pl.BlockSpec((t, D_GROUP), lambda i: (i, COL_LRU_G)), pl.BlockSpec((t, D_GROUP), lambda i: (i, COL_LRU_X))] + \
               [_full_spec(a) for a in ins[2:9]] + [_ANY]
    return pl.pallas_call(
        body, grid=(s // t,), in_specs=in_specs,
        out_specs=[pl.BlockSpec((t, D_GROUP), lambda i: (i, MIX_LRU)), row, row],
        out_shape=[jax.ShapeDtypeStruct((s, D_MODEL), BF16)] + [jax.ShapeDtypeStruct((s, D_GROUP), F32)] * 2,
        input_output_aliases={9: 0},
        scratch_shapes=[pltpu.VMEM((SUBLANE + t, D_GROUP), F32)] + [pltpu.VMEM((pad + t, D_GROUP), F32)] * 4 +
                       [pltpu.VMEM((SUBLANE, D_GROUP), F32)],
        compiler_params=_cparams(1), name=name)(*ins)


def _lru_bwd(h_in, xc_all, h_all, dmix, lcw, wr, br, wi, bi, sp, dh_all, *, name):
    s = h_in.shape[0]
    t = _seq_tile(s, LRU_TILE)
    nt = s // t
    pad = max(t // 2, SUBLANE)
    tb = t // SUBLANE

    def body(xg_ref, xr_ref, xc_ref, h_ref, hprev_ref, do_ref, lcw_ref, wr_ref, br_ref, wi_ref, bi_ref, sp_ref, _dh_in,
             dgr_ref, dwr_ref, dwi_ref, dlcw_ref, dbr_ref, dbi_ref, dsp_ref, dlcb_ref,
             a0, a1, b0, b1, hp, dxpad, carry):
        pid = pl.program_id(0)

        @pl.when(pid == 0)
        def _():
            for bf in (a0, a1, b0, b1):
                bf[pad + t:pad + t + pad, :] = jnp.zeros((pad, D_GROUP), F32)
            dxpad[t:t + SUBLANE, :] = jnp.zeros((SUBLANE, D_GROUP), F32)
            carry[...] = jnp.zeros_like(carry)
            for r in (dwr_ref, dwi_ref, dlcw_ref, dbr_ref, dbi_ref, dsp_ref, dlcb_ref):
                r[...] = jnp.zeros_like(r)

        xc = xc_ref[...]
        h = h_ref[...]
        dout = do_ref[...]
        gate, dgate = _gelu_and_grad(xg_ref[...])
        dgr_ref[:, :D_GROUP] = dout * h * dgate
        r, i, a, m = _lru_gates(xc, wr_ref, br_ref, wi_ref, bi_ref, sp_ref)

        a0[pad:pad + t, :] = a
        b0[pad:pad + t, :] = dout * gate
        b0[pad + t - 1:pad + t, :] += carry[0:1, :]
        a1[pad:pad + t, :] = a0[pad + 1:pad + 1 + t, :]
        fin = _rscan_levels((a1, a0), (b0, b1), t, pad, reverse=True)
        lam = (b0, b1)[fin][pad:pad + t, :]
        carry[0:1, :] = a[0:1, :] * lam[0:1, :]

        is_first = pid == nt - 1
        hp[0:SUBLANE, :] = jnp.where(is_first, 0.0, hprev_ref[...])
        hp[SUBLANE:SUBLANE + t, :] = h
        hprev = hp[SUBLANE - 1:SUBLANE - 1 + t, :]

        ix = i * xc
        dmm = lam * ix
        dix = lam * m
        da = lam * hprev - dmm * (a / m)
        dlog_a = da * a
        dr = dlog_a * (-LRU_C * sp_ref[...])
        dsp_ref[...] += _colsum(dlog_a * (-LRU_C * r))
        dpr = dr * r * (1.0 - r)
        dpi = dix * xc * i * (1.0 - i)
        dbr_ref[...] += _colsum(dpr)
        dbi_ref[...] += _colsum(dpi)
        dwr_ref[...] += _dot_tn(xc, dpr)
        dwi_ref[...] += _dot_tn(xc, dpi)
        dxc = dix * i + _dot_nt(dpr, wr_ref[...]) + _dot_nt(dpi, wi_ref[...])
        dlcb_ref[...] += _colsum(dxc)

        dxpad[0:t, :] = dxc
        xr = xr_ref[...]
        dxr = jnp.zeros((t, D_GROUP), F32)
        for k in range(LRU_CONV_WIDTH):
            o = (LRU_CONV_WIDTH - 1) - k
            sh = dxpad[o:o + t, :]
            dxr = dxr + lcw_ref[k:k + 1, :] * sh
            dlcw_ref[k:k + 1, :] += _colsum(xr * sh)
        dxpad[t:t + SUBLANE, :] = dxpad[0:SUBLANE, :]
        dgr_ref[:, D_GROUP:] = dxr

    def rev(col):
        return lambda i: (nt - 1 - i, col)

    ins = [h_in, h_in, xc_all, h_all, h_all, dmix, lcw, wr, br, wi, bi, sp, dh_all]
    in_specs = [pl.BlockSpec((t, D_GROUP), rev(COL_LRU_G)), pl.BlockSpec((t, D_GROUP), rev(COL_LRU_X)),
                pl.BlockSpec((t, D_GROUP), rev(0)), pl.BlockSpec((t, D_GROUP), rev(0)),
                pl.BlockSpec((SUBLANE, D_GROUP), lambda i: (jnp.maximum((nt - 1 - i) * tb - 1, 0), 0)),
                pl.BlockSpec((t, D_GROUP), rev(MIX_LRU))] + [_full_spec(a) for a in ins[6:12]] + [_ANY]
    vec = jax.ShapeDtypeStruct((1, D_GROUP), F32)
    mat = jax.ShapeDtypeStruct((D_GROUP, D_GROUP), F32)
    outs = [jax.ShapeDtypeStruct((s, N_IN_COLS), F32), mat, mat, jax.ShapeDtypeStruct((SUBLANE, D_GROUP), F32),
            vec, vec, vec, vec]
    out_specs = [pl.BlockSpec((t, 2 * D_GROUP), rev(COL_LRU_G // 2))] + [_full_spec(o) for o in outs[1:]]
    return pl.pallas_call(
        body, grid=(nt,), in_specs=in_specs, out_specs=out_specs, out_shape=outs, input_output_aliases={12: 0},
        scratch_shapes=[pltpu.VMEM((pad + t + pad, D_GROUP), F32)] * 4 +
                       [pltpu.VMEM((SUBLANE + t, D_GROUP), F32), pltpu.VMEM((t + SUBLANE, D_GROUP), F32),
                        pltpu.VMEM((SUBLANE, D_GROUP), F32)],
        compiler_params=_cparams(1), name=name)(*ins)


def _blockdiag(w):
    h, d, _ = w.shape
    return jnp.tile(w.reshape(h * d, d), (1, h)) * _block_mask(h, d, d)


ATTN_TILE = 512
ATTN_SCALE = ATTN_HEAD_DIM ** -0.5


def _attn_big(kv):
    m = kv.shape[0]
    kbig = jnp.tile(kv[:, :D_GROUP].T, (1, ATTN_HEADS)) * _block_mask(ATTN_HEADS, ATTN_HEAD_DIM, m)
    vbig = jnp.tile(kv[:, D_GROUP:], (ATTN_HEADS, 1)) * _block_mask(ATTN_HEADS, m, ATTN_HEAD_DIM)
    return kbig, vbig


def _attn_probs(q, kbig_ref, m):
    sc = _dot(q, kbig_ref[...]) * ATTN_SCALE
    ps = []
    for h in range(ATTN_HEADS):
        sh = sc[:, h * m:(h + 1) * m]
        e = jnp.exp(sh - jnp.max(sh, axis=1, keepdims=True))
        ps.append(e / jnp.sum(e, axis=1, keepdims=True))
    return ps


def _attn_fwd(h_in, kbig, vbig, mix, *, name):
    s = h_in.shape[0]
    t = _seq_tile(s, ATTN_TILE)
    m = kbig.shape[1] // ATTN_HEADS

    def body(q_ref, kbig_ref, vbig_ref, _mix_in, o_ref):
        ps = _attn_probs(q_ref[...], kbig_ref, m)
        o_ref[...] = _dot(jnp.concatenate(ps, axis=1), vbig_ref[...]).astype(BF16)

    return pl.pallas_call(
        body, grid=(s // t,),
        in_specs=[pl.BlockSpec((t, D_GROUP), lambda i: (i, COL_Q)), _full_spec(kbig), _full_spec(vbig), _ANY],
        out_specs=pl.BlockSpec((t, D_GROUP), lambda i: (i, MIX_ATTN)),
        out_shape=jax.ShapeDtypeStruct((s, D_MODEL), BF16), input_output_aliases={3: 0},
        compiler_params=_cparams(1), name=name)(h_in, kbig, vbig, mix)


def _attn_bwd(h_in, dmix, kbig, vbig, du_s5, dh_all, *, name):
    s = h_in.shape[0]
    t = _seq_tile(s, ATTN_TILE)
    m = kbig.shape[1] // ATTN_HEADS

    def body(q_ref, do_ref, kbig_ref, vbig_ref, dus5_ref, _dh_in, dpair_ref, dk_ref, dv_ref):
        @pl.when(pl.program_id(0) == 0)
        def _():
            dk_ref[...] = jnp.zeros_like(dk_ref)
            dv_ref[...] = jnp.zeros_like(dv_ref)

        q = q_ref[...]
        dout = do_ref[...]
        ps = _attn_probs(q, kbig_ref, m)
        dp = _dot_nt(dout, vbig_ref[...])
        dss = []
        for h in range(ATTN_HEADS):
            dph = dp[:, h * m:(h + 1) * m]
            dss.append(ps[h] * (dph - jnp.sum(dph * ps[h], axis=1, keepdims=True)))
        ds = (jnp.concatenate(dss, axis=1) * ATTN_SCALE).astype(BF16)
        dv_ref[...] += _dot_tn(jnp.concatenate(ps, axis=1), dout)
        dpair_ref[:, :D_GROUP] = dus5_ref[...]
        dpair_ref[:, D_GROUP:] = _dot_nt(ds, kbig_ref[...])
        dk_ref[...] += _dot_tn(q, ds)

    assert (COL_S5, COL_Q) == (4, 5)
    outs = [jax.ShapeDtypeStruct((s, N_IN_COLS), F32), jax.ShapeDtypeStruct(kbig.shape, F32),
            jax.ShapeDtypeStruct(vbig.shape, F32)]
    return pl.pallas_call(
        body, grid=(s // t,),
        in_specs=[pl.BlockSpec((t, D_GROUP), lambda i: (i, COL_Q)), pl.BlockSpec((t, D_GROUP), lambda i: (i, MIX_ATTN)),
                  _full_spec(kbig), _full_spec(vbig), pl.BlockSpec((t, D_GROUP), lambda i: (i, 0)), _ANY],
        out_specs=[pl.BlockSpec((t, 2 * D_GROUP), lambda i: (i, COL_S5 // 2)), _full_spec(outs[1]), _full_spec(outs[2])],
        out_shape=outs, input_output_aliases={5: 0},
        compiler_params=_cparams(1), name=name)(h_in, dmix, kbig, vbig, du_s5, dh_all)


FFN_TILE = 128
FFN_COL_CHUNK = 256
FFN_ROW_CHUNK = 64


def _ffn_conv(pad_ref, w_ref, b_ref, r0, ch, c0):
    cc = FFN_COL_CHUNK
    acc = jnp.broadcast_to(b_ref[:, c0:c0 + cc], (ch, cc))
    for k in range(FFN_CONV_WIDTH):
        o = SUBLANE - (FFN_CONV_WIDTH - 1) + k + r0
        acc = acc + w_ref[k:k + 1, c0:c0 + cc] * pad_ref[o:o + ch, c0:c0 + cc]
    return acc


def _ffn_gate_fwd(u, fcw, fcb, *, name, rider=None):
    s = u.shape[0]
    t = _seq_tile(s, FFN_TILE)
    ch = min(FFN_ROW_CHUNK, t)
    cc = FFN_COL_CHUNK

    def body(u_ref, w_ref, b_ref, o_ref, uc_ref, upad):
        @pl.when(pl.program_id(0) == 0)
        def _():
            upad[0:SUBLANE, :] = jnp.zeros((SUBLANE, 2 * D_FF), F32)

        upad[SUBLANE:SUBLANE + t, :] = u_ref[...].astype(F32)
        for c0 in range(0, D_FF, cc):
            for r0 in range(0, t, ch):
                val = _ffn_conv(upad, w_ref, b_ref, r0, ch, c0)
                gt = _ffn_conv(upad, w_ref, b_ref, r0, ch, c0 + D_FF)
                o_ref[r0:r0 + ch, c0:c0 + cc] = (val * _gelu(gt)).astype(BF16)
                uc_ref[r0:r0 + ch, c0:c0 + cc] = val.astype(BF16)
                uc_ref[r0:r0 + ch, c0 + D_FF:c0 + D_FF + cc] = gt.astype(BF16)
        upad[0:SUBLANE, :] = upad[t:t + SUBLANE, :]

    return _call(
        body, grid=(s // t,), ins=[u, fcw, fcb],
        in_specs=[pl.BlockSpec((t, 2 * D_FF), lambda i: (i, 0)), _full_spec(fcw), _full_spec(fcb)],
        out_specs=[pl.BlockSpec((t, D_FF), lambda i: (i, 0)), pl.BlockSpec((t, 2 * D_FF), lambda i: (i, 0))],
        outs=[jax.ShapeDtypeStruct((s, D_FF), BF16), jax.ShapeDtypeStruct((s, 2 * D_FF), BF16)],
        scratch=[pltpu.VMEM((SUBLANE + t, 2 * D_FF), F32)], name=name, rider=rider)


def _ffn_gate_bwd(u, uc, dh, fcw, *, name, rider=None):
    s = u.shape[0]
    t = _seq_tile(s, FFN_TILE)
    nt = s // t
    ch = min(FFN_ROW_CHUNK, t)
    cc = FFN_COL_CHUNK

    def body(u_ref, uc_ref, dh_ref, w_ref, du_ref, dw_ref, db_ref, dpad):
        @pl.when(pl.program_id(0) == 0)
        def _():
            dpad[t:t + SUBLANE, :] = jnp.zeros((SUBLANE, 2 * D_FF), F32)
            dw_ref[...] = jnp.zeros_like(dw_ref)
            db_ref[...] = jnp.zeros_like(db_ref)

        for c0 in range(0, D_FF, cc):
            for r0 in range(0, t, ch):
                val = uc_ref[r0:r0 + ch, c0:c0 + cc].astype(F32)
                gt = uc_ref[r0:r0 + ch, c0 + D_FF:c0 + D_FF + cc].astype(F32)
                gl, dgl = _gelu_and_grad(gt)
                d = dh_ref[r0:r0 + ch, c0:c0 + cc].astype(F32)
                dpad[r0:r0 + ch, c0:c0 + cc] = d * gl
                dpad[r0:r0 + ch, c0 + D_FF:c0 + D_FF + cc] = d * val * dgl
        for c0 in range(0, 2 * D_FF, cc):
            dbs = jnp.zeros((1, cc), F32)
            dws = [jnp.zeros((1, cc), F32) for _ in range(FFN_CONV_WIDTH)]
            for r0 in range(0, t, ch):
                x = u_ref[r0:r0 + ch, c0:c0 + cc].astype(F32)
                acc = jnp.zeros((ch, cc), F32)
                for k in range(FFN_CONV_WIDTH):
                    o = (FFN_CONV_WIDTH - 1) - k + r0
                    sh = dpad[o:o + ch, c0:c0 + cc]
                    acc = acc + w_ref[k:k + 1, c0:c0 + cc] * sh
                    dws[k] = dws[k] + _colsum(x * sh)
                    if k == FFN_CONV_WIDTH - 1:
                        dbs = dbs + _colsum(sh)
                du_ref[r0:r0 + ch, c0:c0 + cc] = acc.astype(BF16)
            db_ref[:, c0:c0 + cc] += dbs
            for k in range(FFN_CONV_WIDTH):
                dw_ref[k:k + 1, c0:c0 + cc] += dws[k]
        dpad[t:t + SUBLANE, :] = dpad[0:SUBLANE, :]

    outs = [jax.ShapeDtypeStruct((s, 2 * D_FF), BF16), jax.ShapeDtypeStruct((SUBLANE, 2 * D_FF), F32),
            jax.ShapeDtypeStruct((1, 2 * D_FF), F32)]
    return _call(
        body, grid=(nt,), ins=[u, uc, dh, fcw],
        in_specs=[pl.BlockSpec((t, 2 * D_FF), lambda i: (nt - 1 - i, 0)),
                  pl.BlockSpec((t, 2 * D_FF), lambda i: (nt - 1 - i, 0)),
                  pl.BlockSpec((t, D_FF), lambda i: (nt - 1 - i, 0)), _full_spec(fcw)],
        out_specs=[pl.BlockSpec((t, 2 * D_FF), lambda i: (nt - 1 - i, 0)), _full_spec(outs[1]), _full_spec(outs[2])],
        outs=outs, scratch=[pltpu.VMEM((t + SUBLANE, 2 * D_FF), F32)], name=name, rider=rider)


def _adamw_body(g_ref, w_ref, m_ref, v_ref, go_ref, d_ref, mo_ref, vo_ref):
    inv_b1 = 1.0 - ADAM_B1 ** ADAM_STEP
    inv_b2 = 1.0 - ADAM_B2 ** ADAM_STEP
    g = g_ref[0].astype(F32)
    for dev in range(1, N_DEV):
        g = g + g_ref[dev].astype(F32)
    go_ref[...] = g
    mn = ADAM_B1 * m_ref[...] + (1.0 - ADAM_B1) * g
    vn = ADAM_B2 * v_ref[...] + (1.0 - ADAM_B2) * (g * g)
    mo_ref[...] = mn
    vo_ref[...] = vn
    d_ref[...] = -ADAM_LR * ((mn / inv_b1) / (jnp.sqrt(vn / inv_b2) + ADAM_EPS) + ADAM_WD * w_ref[...])


def _adamw(gstack, w, m, v, *, name):
    _, r, c = gstack.shape
    tr = _pick_rows(r, PACK_ROW_BLOCK)

    def body(*refs):
        _adamw_body(*refs)

    blk = pl.BlockSpec((tr, c), lambda i: (i, 0))
    sh = jax.ShapeDtypeStruct((r, c), F32)
    return pl.pallas_call(
        body, grid=(r // tr,),
        in_specs=[pl.BlockSpec((N_DEV, tr, c), lambda i: (0, i, 0)), blk, blk, blk],
        out_specs=[blk] * 4, out_shape=[sh] * 4,
        compiler_params=_cparams(1), name=name)(gstack, w, m, v)


def _adamw_layer(gstack, w, m, v, layer, into, *, name):
    n_layers, r, c = w.shape
    tr = _pick_rows(r, PACK_ROW_BLOCK)

    def body(g_ref, w_ref, m_ref, v_ref, *rest):
        _adamw_body(g_ref, w_ref, m_ref, v_ref, *rest[-4:])

    blk = pl.BlockSpec((None, tr, c), lambda i: (layer, i, 0))
    sh = jax.ShapeDtypeStruct((n_layers, r, c), F32)
    into = list(into or [])
    return pl.pallas_call(
        body, grid=(r // tr,),
        in_specs=[pl.BlockSpec((N_DEV, tr, c), lambda i: (0, i, 0)), blk, blk, blk] + [_ANY] * len(into),
        out_specs=[blk] * 4, out_shape=[sh] * 4, input_output_aliases={4 + k: k for k in range(len(into))},
        compiler_params=_cparams(1), name=name)(gstack, w, m, v, *into)


def _exchange(rider, *, name):
    n = rider.n

    def body(*refs):
        x_refs, out_refs, sems = refs[:n], refs[n:2 * n], refs[2 * n:]
        rider.start(x_refs, out_refs, sems)
        rider.wait(x_refs, out_refs, sems)

    return pl.pallas_call(
        body, in_specs=[_ANY] * n, out_specs=[_ANY] * n, out_shape=rider.out_shapes(),
        scratch_shapes=rider.scratch(), name=name)(*rider.srcs)


def _pack_rows(n):
    rows = -(-n // PACK_COLS)
    return -(-rows // SUBLANE) * SUBLANE


def _pack(arrs, dtype):
    flat = jnp.concatenate([a.reshape(-1).astype(dtype) for a in arrs])
    rows = _pack_rows(flat.shape[0])
    flat = jnp.pad(flat, (0, rows * PACK_COLS - flat.shape[0]))
    return flat.reshape(rows, PACK_COLS)


def _pack_lead(arrs, dtype):
    flat = jnp.concatenate([a.reshape(N_DEV, -1).astype(dtype) for a in arrs], axis=1)
    rows = _pack_rows(flat.shape[1])
    flat = jnp.pad(flat, ((0, 0), (0, rows * PACK_COLS - flat.shape[1])))
    return flat.reshape(N_DEV, rows, PACK_COLS)


def _pack_layers(arrs, dtype):
    n_layers = arrs[0].shape[0]
    flat = jnp.concatenate([a.reshape(n_layers, -1).astype(dtype) for a in arrs], axis=1)
    rows = _pack_rows(flat.shape[1])
    flat = jnp.pad(flat, ((0, 0), (0, rows * PACK_COLS - flat.shape[1])))
    return flat.reshape(n_layers, rows, PACK_COLS)


def _unpack_layers(packed, shapes):
    flat = packed.reshape(packed.shape[0], -1)
    out, pos = [], 0
    for sh in shapes:
        n = math.prod(sh[1:])
        out.append(flat[:, pos:pos + n].reshape(sh))
        pos += n
    return out


def _unpack(packed, shapes, lead=False):
    flat = packed.reshape(N_DEV, -1) if lead else packed.reshape(-1)
    out, pos = [], 0
    for sh in shapes:
        n = math.prod(sh)
        out.append(flat[:, pos:pos + n].reshape((N_DEV,) + tuple(sh)) if lead else flat[pos:pos + n].reshape(sh))
        pos += n
    return out


def _join_shards(stacked, axis):
    return jnp.concatenate([stacked[d] for d in range(N_DEV)], axis=axis)


def _split_shards(full, axis):
    return jnp.stack(jnp.split(full, N_DEV, axis=axis), axis=0)


def _perm_in_cols(a, inverse=False):
    blocks = jnp.split(a, 6, axis=-1)
    if inverse:
        order = [IN_PERM.index(j) for j in range(6)]
    else:
        order = list(IN_PERM)
    return jnp.concatenate([blocks[j] for j in order], axis=-1)


def _row(v):
    return v.reshape(1, -1)


def _pad_rows(w, rows):
    return jnp.pad(w, ((0, rows - w.shape[0]), (0, 0)))


def _gn_avg_matrix():
    return _block_mask(GN_GROUPS, D_GROUP // GN_GROUPS, D_GROUP // GN_GROUPS) / (D_GROUP // GN_GROUPS)


def _layer_params(p, l):
    q = {}
    s5_mats, q["s5_vjp"] = jax.vjp(_s5_chunk_map, p["s5_lam_re"][l], p["s5_lam_im"][l], p["s5_log_dt"][l],
                                   p["s5_b_re"][l], p["s5_b_im"][l], p["s5_c_re"][l], p["s5_c_im"][l], p["s5_d"][l])
    q["s5_mats"] = [m.astype(BF16) for m in s5_mats[:5]]
    q["s5_a16"] = s5_mats[5]
    (q["wr"], q["wi"]), q["lru_w_vjp"] = jax.vjp(lambda r, i: (_blockdiag(r), _blockdiag(i)), p["lru_w_r"][l], p["lru_w_i"][l])
    q["wr"], q["wi"] = q["wr"].astype(BF16), q["wi"].astype(BF16)
    q["sp"], q["sp_vjp"] = jax.vjp(lambda lam: _row(jax.nn.softplus(-lam)), p["lru_lam"][l])
    return q


WEIGHT_RIDES = {(0, "ln_in_fwd"): [("w_in", 0)],
                (0, "inproj"): [("attn_w_kv", 0), ("w_out", 0), ("small_pack", 0)],
                (0, "cv_fwd"): [("ffn_w_up#a", 0)],
                (0, "outproj"): [("ffn_w_up#b", 0)],
                (0, "ffn_up"): [("ffn_w_down", 0), ("w_in", 1), ("attn_w_kv", 1), ("w_out", 1)],
                (0, "ffn_gate_fwd"): [("ffn_w_up", 1)],
                (0, "ffn_down"): [("ffn_w_down", 1)]}
GRAD_RIDES = {(1, "ffn_gate_bwd"): [("ffn_w_down", 1)],
              (0, "dw_down"): [("w_out", 1), ("attn_w_kv", 1), ("w_in", 1)],
              (0, "dhff"): [("rep", 1), ("ssh", 1)],
              (0, "ffn_gate_bwd"): [("ffn_w_up", 1)],
              (0, "dw_up"): [("ffn_w_down", 0)],
              (0, "dx1"): [("ffn_w_up", 0)],
              (0, "cv_bwd"): [("w_out", 0)],
              (0, "dw_in"): [("attn_w_kv", 0), ("ssh", 0), ("rep", 0)],
              (0, "dxs"): [("w_in", 0)]}


def _join_cols(pieces, *, name):
    n_dev, k, c = pieces[0].shape
    assert (2 * c) % LANE == 0 and all(p.shape == pieces[0].shape for p in pieces)
    n_p = len(pieces)

    def body(*refs):
        o_ref = refs[n_p]
        for i in range(n_p):
            @pl.when(pl.program_id(0) == i)
            def _(i=i):
                o_ref[...] = jnp.concatenate([refs[i][0], refs[i][1]], axis=1)

    return pl.pallas_call(
        body, grid=(n_p, n_dev // 2), in_specs=[pl.BlockSpec((2, k, c), lambda i, j: (j, 0, 0))] * n_p,
        out_specs=pl.BlockSpec((k, 2 * c), lambda i, j: (i, j)),
        out_shape=jax.ShapeDtypeStruct((n_p * k, n_dev * c), pieces[0].dtype),
        compiler_params=_cparams(2), name=name)(*pieces)


def _split_cols(full, *, name):
    k, n = full.shape
    c = n // N_DEV
    assert (2 * c) % LANE == 0

    def body(x_ref, o_ref):
        o_ref[0] = x_ref[:, :c]
        o_ref[1] = x_ref[:, c:]

    return pl.pallas_call(
        body, grid=(N_DEV // 2,), in_specs=[pl.BlockSpec((k, 2 * c), lambda j: (0, j))],
        out_specs=pl.BlockSpec((2, k, c), lambda j: (j, 0, 0)), out_shape=jax.ShapeDtypeStruct((N_DEV, k, c), full.dtype),
        compiler_params=_cparams(1), name=name)(full)


def _assemble_weight(n, pieces, layer=0):
    if SHARDED[n] == 2:
        full = _join_cols(pieces, name=f"l{layer}_join_{n}")
        return _perm_in_cols(full) if n == "w_in" else full
    (gathered,) = pieces
    return gathered.reshape(-1, gathered.shape[-1])


def _grad_source(n, g, layer=0):
    g = g.astype(BF16)
    if SHARDED[n] == 2:
        if n == "w_in":
            g = _perm_in_cols(g, inverse=True)
        return _split_cols(g, name=f"l{layer}_split_d{n}"), "lead"
    return g, "rows"


def _hosted(fn, keys_rider, land, *args, **kw):
    keys, rider = keys_rider
    if rider is None:
        return fn(*args, **kw)
    out, routs = fn(*args, rider=rider, **kw)
    land(keys, routs)
    return out


def _local_step(x, mem, target, p, big_w, shards=None, unpack_small=None):
    dist = shards is not None
    gdt = BF16 if dist else F32
    small, saved = {}, []
    big_g, ready, recv = {}, {}, {}
    mavg = _gn_avg_matrix()
    s5_perm = _s5_perm()

    def weight_rider(l, host):
        keys = WEIGHT_RIDES.get((l, host), []) if dist else []
        return keys, (_Rider([shards[n][ll] for n, ll in keys], ["all"] * len(keys)) if keys else None)

    halves = {}

    def land_weights(keys, routs):
        for (n, ll), r in zip(keys, routs):
            if n == "small_pack":
                p.update(unpack_small(r))
            elif "#" in n:
                base = n.split("#")[0]
                halves[(n, ll)] = r
                if (base + "#a", ll) in halves and (base + "#b", ll) in halves:
                    big_w[base][ll] = _assemble_weight(base, [halves[(base + "#a", ll)], halves[(base + "#b", ll)]], ll)
            else:
                big_w[n][ll] = _assemble_weight(n, [r], ll)

    def grad_rider(l, host):
        keys = [k for k in GRAD_RIDES.get((l, host), []) if k in ready] if dist else []
        return keys, (_Rider([ready[k][0] for k in keys], [ready[k][1] for k in keys]) if keys else None)

    def land_grads(keys, routs):
        for k, r in zip(keys, routs):
            recv[k] = r
            del ready[k]

    def big_grad(n, l, g):
        if dist:
            ready[(n, l)] = _grad_source(n, g, l)
        else:
            big_g[(n, l)] = g

    xs = _hosted(_ln_fwd, weight_rider(0, "ln_in_fwd"), land_weights, x, _row(p["ln_in_g"]), _row(p["ln_in_b"]),
                 name="ln_in_fwd")
    for l in range(DEPTH):
        q = _layer_params(p, l)
        n = f"l{l}_"
        hin = _hosted(_mm, weight_rider(l, "inproj"), land_weights, xs, big_w["w_in"][l], bias=_row(p["b_in"][l]),
                      name=n + "inproj")
        nb = hin.shape[0] // S5_CHUNK
        s5_pows = _s5_a16_powers(q["s5_a16"], nb.bit_length() - 1)
        s5_u2 = _s5_to_chunks(hin, COL_S5 * (D_GROUP // LANE), s5_perm, name=n + "s5_in")
        s5_y2, s5_x = _s5_core_fwd(s5_u2, *q["s5_mats"], s5_pows, name=n + "s5_core_fwd")
        s5_y1 = _s5_from_chunks(s5_y2, s5_perm, name=n + "s5_out")
        (mix,), _ = _s5_glu_fwd(s5_y1, p["s5_w_glu"][l], _row(p["s5_b_glu"][l]), name=n + "s5_glu_fwd")
        cvw = _pad_rows(p["cv_w"][l], CV_PAD)
        keys, rd = weight_rider(l, "cv_fwd")
        (mix, cv_c), routs = _cv_fwd(hin, cvw, _row(p["cv_b"][l]), _row(p["cv_gn_g"][l]), _row(p["cv_gn_b"][l]), mavg,
                                     p["cv_w_pw"][l], _row(p["cv_b_pw"][l]), mix, name=n + "cv_fwd", rider=rd)
        land_weights(keys, routs)
        lcw = _pad_rows(p["lru_conv_w"][l], SUBLANE)
        mix, lru_xc, lru_h = _lru_fwd(hin, lcw, _row(p["lru_conv_b"][l]), q["wr"], _row(p["lru_b_r"][l]), q["wi"],
                                      _row(p["lru_b_i"][l]), q["sp"], mix, name=n + "lru_fwd")
        kv = _mm(mem, big_w["attn_w_kv"][l], name=n + "kv")
        (kbig, vbig), kv_vjp = jax.vjp(_attn_big, kv)
        kbig, vbig = kbig.astype(BF16), vbig.astype(BF16)
        mix = _attn_fwd(hin, kbig, vbig, mix, name=n + "attn_fwd")
        r1, x1 = _hosted(_mm, weight_rider(l, "outproj"), land_weights, mix, big_w["w_out"][l], bias=_row(p["b_out"][l]),
                         res=xs, res_scale=ALPHA, ln=(_row(p["ln1_g"][l]), _row(p["ln1_b"][l])), name=n + "outproj")
        u = _hosted(_mm, weight_rider(l, "ffn_up"), land_weights, x1, big_w["ffn_w_up"][l], out_dtype=BF16,
                    name=n + "ffn_up")
        fcw = _pad_rows(p["ffn_conv_w"][l], SUBLANE)
        fcb = _row(p["ffn_conv_b"][l])
        keys, rd = weight_rider(l, "ffn_gate_fwd")
        (hff, uc), routs = _ffn_gate_fwd(u, fcw, fcb, name=n + "ffn_gate_fwd", rider=rd)
        land_weights(keys, routs)
        if l < DEPTH - 1:
            r2, x2 = _hosted(_mm, weight_rider(l, "ffn_down"), land_weights, hff, big_w["ffn_w_down"][l], res=x1,
                             res_scale=ALPHA, ln=(_row(p["ln2_g"][l]), _row(p["ln2_b"][l])), name=n + "ffn_down")
        else:
            r2, x2 = _mm(hff, big_w["ffn_w_down"][l], res=x1, res_scale=ALPHA, name=n + "ffn_down"), None
        saved.append(dict(q=q, xs=xs, hin=hin, s5_y1=s5_y1, s5_u2=s5_u2, s5_x=s5_x, s5_pows=s5_pows, cvw=cvw, cv_c=cv_c, lcw=lcw, lru_xc=lru_xc,
                          lru_h=lru_h, kbig=kbig, vbig=vbig, kv_vjp=kv_vjp, mix=mix, r1=r1, x1=x1, u=u, uc=uc, fcw=fcw,
                          hff=hff, r2=r2))
        xs = x2

    top = DEPTH - 1
    dr_top, dg_top, db_top, loss_blk = _loss_ln_bwd(saved[top]["r2"], _row(p["ln2_g"][top]), _row(p["ln2_b"][top]), target,
                                                     name="loss_ln_bwd")
    loss = loss_blk[0, 0]
    dx = None

    for l in reversed(range(DEPTH)):
        sv = saved[l]
        q = sv["q"]
        n = f"l{l}_"
        g = {}
        if l == top:
            dr2, g["ln2_g"], g["ln2_b"] = dr_top, dg_top, db_top
        else:
            dr2, g["ln2_g"], g["ln2_b"], _ = _ln_bwd(sv["r2"], dx, _row(p["ln2_g"][l]), name=n + "ln2_bwd")
        big_grad("ffn_w_down", l, _hosted(_mm_tn, grad_rider(l, "dw_down"), land_grads, sv["hff"], dr2, out_dtype=gdt,
                                          name=n + "dw_down"))
        dhff = _hosted(_mm, grad_rider(l, "dhff"), land_grads, dr2, big_w["ffn_w_down"][l], trans_b=True,
                       out_dtype=BF16, name=n + "dhff")
        keys, rd = grad_rider(l, "ffn_gate_bwd")
        (du, dfw, g["ffn_conv_b"]), routs = _ffn_gate_bwd(sv["u"], sv["uc"], dhff, sv["fcw"], name=n + "ffn_gate_bwd",
                                                          rider=rd)
        land_grads(keys, routs)
        g["ffn_conv_w"] = dfw[:FFN_CONV_WIDTH]
        if dist:
            ready[("ffn_w_up", l)] = (_hosted(_mm_tn, grad_rider(l, "dw_up"), land_grads, sv["x1"], du, out_dtype=gdt,
                                              dev_cols=du.shape[1] // N_DEV, name=n + "dw_up"), "lead")
        else:
            big_grad("ffn_w_up", l, _mm_tn(sv["x1"], du, name=n + "dw_up"))
        dx1 = _hosted(_mm, grad_rider(l, "dx1"), land_grads, du, big_w["ffn_w_up"][l], trans_b=True, res=dr2,
                      res_scale=ALPHA, name=n + "dx1")
        dr1, g["ln1_g"], g["ln1_b"], g["b_out"] = _ln_bwd(sv["r1"], dx1, _row(p["ln1_g"][l]), name=n + "ln1_bwd")
        big_grad("w_out", l, _mm_tn(sv["mix"], dr1, out_dtype=gdt, name=n + "dw_out"))
        dmix = _mm(dr1, big_w["w_out"][l], trans_b=True, name=n + "dmix")

        hin = sv["hin"]
        keys, rd = grad_rider(l, "cv_bwd")
        (dh, g["cv_w_pw"], dcw, g["cv_b_pw"], g["cv_gn_g"], g["cv_gn_b"], g["cv_b"]), routs = _cv_bwd(
            hin, sv["cv_c"], dmix, sv["cvw"], _row(p["cv_gn_g"][l]), _row(p["cv_gn_b"][l]), mavg, p["cv_w_pw"][l],
            name=n + "cv_bwd", rider=rd)
        land_grads(keys, routs)
        g["cv_w"] = dcw[:CONV_WIDTH]
        dh, dwr, dwi, dlcw, g["lru_b_r"], g["lru_b_i"], dsp, g["lru_conv_b"] = _lru_bwd(
            hin, sv["lru_xc"], sv["lru_h"], dmix, sv["lcw"], q["wr"], _row(p["lru_b_r"][l]), q["wi"],
            _row(p["lru_b_i"][l]), q["sp"], dh, name=n + "lru_bwd")
        g["lru_conv_w"] = dlcw[:LRU_CONV_WIDTH]
        g["lru_w_r"], g["lru_w_i"] = q["lru_w_vjp"]((dwr, dwi))
        (g["lru_lam"],) = q["sp_vjp"](dsp)
        dy1, g["s5_w_glu"], g["s5_b_glu"] = _s5_glu_bwd(sv["s5_y1"], dmix, p["s5_w_glu"][l], _row(p["s5_b_glu"][l]),
                                                        name=n + "s5_glu_bwd")
        s5_du2, *s5_dmats = _s5_core_bwd(sv["s5_u2"], _s5_to_chunks(dy1, 0, s5_perm, name=n + "s5_din"), sv["s5_x"],
                                         *q["s5_mats"], sv["s5_pows"], name=n + "s5_core_bwd")
        (g["s5_lam_re"], g["s5_lam_im"], g["s5_log_dt"], g["s5_b_re"], g["s5_b_im"], g["s5_c_re"], g["s5_c_im"],
         g["s5_d"]) = q["s5_vjp"](tuple(s5_dmats))
        dh, dkbig, dvbig = _attn_bwd(hin, dmix, sv["kbig"], sv["vbig"],
                                     _s5_from_chunks(s5_du2, s5_perm, name=n + "s5_dout"), dh, name=n + "attn_bwd")
        (dkv,) = sv["kv_vjp"]((dkbig, dvbig))
        big_grad("attn_w_kv", l, _mm_tn(mem, dkv, out_dtype=gdt, name=n + "dw_kv"))

        if dist:
            ready[("ssh", l)] = (_pack_lead([_split_shards(g[k], SHARDED[k] - 1) for k in SMALL_SHARDED], F32), "lead")
            ready[("rep", l)] = (_pack([g[k] for k in REP_LAYERED], F32), "all")
        gw_in, g["b_in"] = _hosted(_mm_tn, grad_rider(l, "dw_in"), land_grads, sv["xs"], dh, colsum=True, out_dtype=gdt,
                                   name=n + "dw_in")
        big_grad("w_in", l, gw_in)
        if dist:
            small.setdefault("b_in", [None] * DEPTH)[l] = g["b_in"].reshape(-1)
        else:
            for k, v in g.items():
                small.setdefault(k, [None] * DEPTH)[l] = v.reshape(p[k].shape[1:])
        dx = _hosted(_mm, grad_rider(l, "dxs"), land_grads, dh, big_w["w_in"][l], trans_b=True, res=dr1,
                     res_scale=ALPHA, name=n + "dxs")

    keys, rd = grad_rider(0, "ln_in_bwd")
    if rd is None:
        grad_x, dgi, dbi, _ = _ln_bwd(x, dx, _row(p["ln_in_g"]), name="ln_in_bwd")
    else:
        (grad_x, dgi, dbi, _), routs = _ln_bwd(x, dx, _row(p["ln_in_g"]), name="ln_in_bwd", rider=rd)
        land_grads(keys, routs)
    out = {k: jnp.stack(v, axis=0) for k, v in small.items()}
    out["ln_in_g"], out["ln_in_b"] = dgi.reshape(-1), dbi.reshape(-1)
    return loss, grad_x, out, ((recv, ready) if dist else big_g)


def kernel(x, mem, ln_in_g, ln_in_b, w_in, b_in, s5_lam_re, s5_lam_im, s5_log_dt, s5_b_re, s5_b_im, s5_c_re, s5_c_im, s5_d, s5_w_glu, s5_b_glu, cv_w, cv_b, cv_gn_g, cv_gn_b, cv_w_pw, cv_b_pw, lru_conv_w, lru_conv_b, lru_w_r, lru_b_r, lru_w_i, lru_b_i, lru_lam, attn_w_kv, w_out, b_out, ln1_g, ln1_b, ffn_w_up, ffn_conv_w, ffn_conv_b, ffn_w_down, ln2_g, ln2_b, loss_target, m_ln_in_g, m_ln_in_b, m_w_in, m_b_in, m_s5_lam_re, m_s5_lam_im, m_s5_log_dt, m_s5_b_re, m_s5_b_im, m_s5_c_re, m_s5_c_im, m_s5_d, m_s5_w_glu, m_s5_b_glu, m_cv_w, m_cv_b, m_cv_gn_g, m_cv_gn_b, m_cv_w_pw, m_cv_b_pw, m_lru_conv_w, m_lru_conv_b, m_lru_w_r, m_lru_b_r, m_lru_w_i, m_lru_b_i, m_lru_lam, m_attn_w_kv, m_w_out, m_b_out, m_ln1_g, m_ln1_b, m_ffn_w_up, m_ffn_conv_w, m_ffn_conv_b, m_ffn_w_down, m_ln2_g, m_ln2_b, v_ln_in_g, v_ln_in_b, v_w_in, v_b_in, v_s5_lam_re, v_s5_lam_im, v_s5_log_dt, v_s5_b_re, v_s5_b_im, v_s5_c_re, v_s5_c_im, v_s5_d, v_s5_w_glu, v_s5_b_glu, v_cv_w, v_cv_b, v_cv_gn_g, v_cv_gn_b, v_cv_w_pw, v_cv_b_pw, v_lru_conv_w, v_lru_conv_b, v_lru_w_r, v_lru_b_r, v_lru_w_i, v_lru_b_i, v_lru_lam, v_attn_w_kv, v_w_out, v_b_out, v_ln1_g, v_ln1_b, v_ffn_w_up, v_ffn_conv_w, v_ffn_conv_b, v_ffn_w_down, v_ln2_g, v_ln2_b):
    args = locals()
    w = {n: args[n] for n in WEIGHTS}
    mom = {n: args["m_" + n] for n in WEIGHTS}
    var = {n: args["v_" + n] for n in WEIGHTS}

    shards = {n: w[n].astype(BF16) for n in BIG}
    half_rows = shards["ffn_w_up"].shape[1] // 2
    shards["ffn_w_up#a"] = [shards["ffn_w_up"][0, :half_rows]]
    shards["ffn_w_up#b"] = [shards["ffn_w_up"][0, half_rows:]]
    shards["small_pack"] = [_pack([w[n] for n in SMALL_SHARDED], F32)]
    small_shapes = [w[n].shape for n in SMALL_SHARDED]

    def unpack_small(gathered):
        out = {n: _join_shards(st, SHARDED[n]) for n, st in zip(SMALL_SHARDED, _unpack(gathered, small_shapes, lead=True))}
        for n in ("s5_w_glu", "cv_w_pw"):
            out[n] = out[n].astype(BF16)
        return out

    big_w = {n: [None] * DEPTH for n in BIG}
    p = {n: w[n] for n in REPLICATED}
    p["b_in"] = _perm_in_cols(p["b_in"])

    loss, grad_x, g_small, (recv, ready) = _local_step(x[0], mem[0], loss_target[0], p, big_w, shards, unpack_small)
    loss = lax.psum(loss, ("x", "y", "c"))

    g_small["b_in"] = _perm_in_cols(g_small["b_in"], inverse=True)
    left = list(ready)
    rider = _Rider([ready[k][0] for k in left] + [_pack([g_small[n] for n in REP_LAST], F32)],
                   [ready[k][1] for k in left] + ["all"])
    got = _exchange(rider, name="exchange_grads")
    for k, r in zip(left, got):
        recv[k] = r

    res = [dict(), dict(), dict(), dict()]
    for n in BIG:
        outs = None
        for l in range(DEPTH):
            outs = _adamw_layer(recv[(n, l)], w[n], mom[n], var[n], l, outs, name=f"adamw_{n}_l{l}")
        for kind in range(4):
            res[kind][n] = outs[kind]
    for names, key, tag in ((SMALL_SHARDED, "ssh", "adamw_small_sharded"), (REP_LAYERED, "rep", "adamw_replicated")):
        gstack = jnp.concatenate([recv[(key, l)] for l in range(DEPTH)], axis=1)
        packs = [_pack_layers([t[n] for n in names], F32) for t in (w, mom, var)]
        rows = packs[0].shape[1]
        outs = _adamw(gstack, *[pk.reshape(DEPTH * rows, PACK_COLS) for pk in packs], name=tag)
        for kind in range(4):
            for n, a in zip(names, _unpack_layers(outs[kind].reshape(DEPTH, rows, PACK_COLS), [w[n].shape for n in names])):
                res[kind][n] = a
    outs = _adamw(got[len(left)], _pack([w[n] for n in REP_LAST], F32), _pack([mom[n] for n in REP_LAST], F32),
                  _pack([var[n] for n in REP_LAST], F32), name="adamw_last")
    for kind in range(4):
        for n, a in zip(REP_LAST, _unpack(outs[kind], [w[n].shape for n in REP_LAST])):
            res[kind][n] = a
    return (loss, grad_x[None], *[res[0][n] for n in WEIGHTS], *[res[1][n] for n in WEIGHTS],
            *[res[2][n] for n in WEIGHTS], *[res[3][n] for n in WEIGHTS])
```

```python
import math

import jax
import jax.numpy as jnp
from jax import lax
from jax.experimental import pallas as pl
from jax.experimental.pallas import tpu as pltpu

F32 = jnp.float32
BF16 = jnp.bfloat16

D_MODEL = 1024
DEPTH = 2
D_GROUP = 256
N_IN_COLS = 6 * D_GROUP
S5_GROUPS = 16
S5_CH = 16
S5_STATE = 64
CONV_WIDTH = 31
GN_GROUPS = 4
LRU_HEADS = 4
LRU_CONV_WIDTH = 4
LRU_C = 8.0
ATTN_HEADS = 4
ATTN_HEAD_DIM = 64
D_FF = 2816
FFN_CONV_WIDTH = 3
ALPHA = (2 * DEPTH) ** 0.25
LN_EPS = 1e-5
ADAM_LR, ADAM_B1, ADAM_B2, ADAM_EPS, ADAM_WD, ADAM_STEP = 0.001, 0.9, 0.999, 1e-08, 0.01, 10

N_DEV = 8
N_PEERS = N_DEV - 1
LANE = 128
SUBLANE = 8
VMEM_LIMIT = 56 * 1024 * 1024
PACK_COLS = 1024
PACK_ROW_BLOCK = 256
MM_ROW_TILE = 1024
MM_COL_CAP = 1408
MM_K_CAP = 1536
SEQ_TILE = 512

SHARDED = {
    "w_in": 2, "s5_w_glu": 1, "cv_w": 2, "cv_w_pw": 1, "lru_conv_w": 2, "attn_w_kv": 1,
    "w_out": 1, "ffn_w_up": 2, "ffn_conv_w": 2, "ffn_w_down": 1,
}
BIG = ("w_in", "attn_w_kv", "w_out", "ffn_w_up", "ffn_w_down")
SMALL_SHARDED = ("s5_w_glu", "cv_w", "cv_w_pw", "lru_conv_w", "ffn_conv_w")
WEIGHTS = ['ln_in_g', 'ln_in_b', 'w_in', 'b_in', 's5_lam_re', 's5_lam_im', 's5_log_dt', 's5_b_re', 's5_b_im',
           's5_c_re', 's5_c_im', 's5_d', 's5_w_glu', 's5_b_glu', 'cv_w', 'cv_b', 'cv_gn_g', 'cv_gn_b', 'cv_w_pw',
           'cv_b_pw', 'lru_conv_w', 'lru_conv_b', 'lru_w_r', 'lru_b_r', 'lru_w_i', 'lru_b_i', 'lru_lam',
           'attn_w_kv', 'w_out', 'b_out', 'ln1_g', 'ln1_b', 'ffn_w_up', 'ffn_conv_w', 'ffn_conv_b', 'ffn_w_down',
           'ln2_g', 'ln2_b']
REPLICATED = [n for n in WEIGHTS if n not in SHARDED]
REP_LAST = ("ln_in_g", "ln_in_b", "b_in")
REP_LAYERED = [n for n in REPLICATED if n not in REP_LAST]

COL_CV_V, COL_CV_G, COL_LRU_G, COL_LRU_X, COL_S5, COL_Q = range(6)
IN_PERM = (1, 2, 3, 4, 0, 5)
MIX_S5, MIX_CV, MIX_LRU, MIX_ATTN = range(4)


_ANY = pl.BlockSpec(memory_space=pl.ANY)
_MESH = pl.DeviceIdType.MESH


def _cparams(n_axes):
    return pltpu.CompilerParams(dimension_semantics=("arbitrary",) * n_axes, vmem_limit_bytes=VMEM_LIMIT)


def _pick(n, cap):
    if n <= cap:
        return n
    best = None
    for t in range(LANE, cap + 1, LANE):
        if n % t == 0:
            best = t
    assert best is not None, (n, cap)
    return best


def _pick_rows(n, cap):
    best = None
    for t in range(SUBLANE, min(n, cap) + 1, SUBLANE):
        if n % t == 0:
            best = t
    assert best is not None, (n, cap)
    return best


def _full_spec(arr):
    nd = arr.ndim
    return pl.BlockSpec(arr.shape, lambda *_: (0,) * nd)


def _dot(a, b):
    return lax.dot_general(a.astype(BF16), b.astype(BF16), (((1,), (0,)), ((), ())), preferred_element_type=F32)


def _dot_nt(a, b):
    return lax.dot_general(a.astype(BF16), b.astype(BF16), (((1,), (1,)), ((), ())), preferred_element_type=F32)


def _dot_tn(a, b):
    return lax.dot_general(a.astype(BF16), b.astype(BF16), (((0,), (0,)), ((), ())), preferred_element_type=F32)


def _dot_hi(a, b):
    b = b.astype(BF16)
    a1 = a.astype(BF16)
    r1 = a - a1.astype(F32)
    a2 = r1.astype(BF16)
    a3 = (r1 - a2.astype(F32)).astype(BF16)
    return _dot(a1, b) + _dot(a2, b) + _dot(a3, b)


def _colsum(x):
    return jnp.sum(x, axis=0, keepdims=True)


def _sigmoid(x):
    return 1.0 / (1.0 + jnp.exp(-x))


_GELU_K = math.sqrt(2.0 / math.pi)
_GELU_C = 0.044715


def _gelu(x):
    t = jnp.tanh(_GELU_K * (x + _GELU_C * x * x * x))
    return 0.5 * x * (1.0 + t)


def _gelu_and_grad(x):
    x2 = x * x
    t = jnp.tanh(_GELU_K * (x + _GELU_C * x2 * x))
    g = 0.5 * x * (1.0 + t)
    dg = 0.5 * (1.0 + t) + 0.5 * x * (1.0 - t * t) * (_GELU_K * (1.0 + 3.0 * _GELU_C * x2))
    return g, dg


def _neg_expm1(x):
    series = x * (1.0 + x * (0.5 + x * (1.0 / 6.0 + x * (1.0 / 24.0 + x * (1.0 / 120.0)))))
    return -jnp.where(jnp.abs(x) < 0.1, series, jnp.exp(x) - 1.0)


def _seq_tile(s, want):
    t = min(s, want)
    assert s % t == 0
    return t


class _Rider:
    def __init__(self, srcs, kinds):
        self.srcs, self.kinds = list(srcs), list(kinds)
        self.n = len(self.srcs)

    def out_shapes(self):
        shapes = []
        for x, kind in zip(self.srcs, self.kinds):
            if kind == "lead":
                shp = x.shape
            elif kind == "rows":
                shp = (N_DEV, x.shape[0] // N_DEV) + x.shape[1:]
            else:
                shp = (N_DEV,) + x.shape
            shapes.append(jax.ShapeDtypeStruct(shp, x.dtype))
        return shapes

    def scratch(self):
        return [pltpu.SemaphoreType.DMA((self.n * N_PEERS,)), pltpu.SemaphoreType.DMA((self.n * N_PEERS,)),
                pltpu.SemaphoreType.DMA((self.n,))]

    def _copies(self, x_refs, out_refs, sems):
        send_sems, recv_sems, local_sems = sems
        mx, my, mc = lax.axis_index("x"), lax.axis_index("y"), lax.axis_index("c")
        my_id = 4 * mx + 2 * my + mc

        def piece(i, dev):
            if self.kinds[i] == "lead":
                return x_refs[i].at[dev]
            if self.kinds[i] == "rows":
                r = x_refs[i].shape[0] // N_DEV
                return x_refs[i].at[pl.ds(pl.multiple_of(dev * r, SUBLANE), r)]
            return x_refs[i]

        mine = [pltpu.make_async_copy(piece(i, my_id), out_refs[i].at[my_id], local_sems.at[i]) for i in range(self.n)]
        copies = []
        for k in range(1, N_DEV):
            px, py, pc = mx ^ ((k >> 2) & 1), my ^ ((k >> 1) & 1), mc ^ (k & 1)
            for i in range(self.n):
                copies.append(pltpu.make_async_remote_copy(
                    src_ref=piece(i, 4 * px + 2 * py + pc), dst_ref=out_refs[i].at[my_id],
                    send_sem=send_sems.at[i * N_PEERS + k - 1], recv_sem=recv_sems.at[i * N_PEERS + k - 1],
                    device_id=(px, py, pc), device_id_type=_MESH))
        return mine, copies

    def start(self, x_refs, out_refs, sems):
        mine, copies = self._copies(x_refs, out_refs, sems)
        for cp in mine + copies:
            cp.start()

    def wait(self, x_refs, out_refs, sems):
        mine, copies = self._copies(x_refs, out_refs, sems)
        for cp in copies:
            cp.wait_recv()
        for cp in copies:
            cp.wait_send()
        for cp in mine:
            cp.wait()


def _call(body, *, grid, ins, in_specs, outs, out_specs, scratch=(), aliases=None, name, rider=None):
    n_axes = len(grid)
    common = dict(grid=grid, input_output_aliases=aliases or {}, compiler_params=_cparams(n_axes), name=name)
    if rider is None:
        res = pl.pallas_call(body, in_specs=list(in_specs), out_specs=list(out_specs), out_shape=list(outs),
                             scratch_shapes=list(scratch), **common)(*ins)
        return list(res), []
    n_in, n_out, n_scr, nr = len(ins), len(outs), len(scratch), rider.n

    def wrapped(*refs):
        pos = [0]

        def take(k):
            part = refs[pos[0]:pos[0] + k]
            pos[0] += k
            return part

        a_in, r_in, a_out, r_out, a_scr, sems = take(n_in), take(nr), take(n_out), take(nr), take(n_scr), take(3)
        first = last = None
        for ax in range(n_axes):
            pid = pl.program_id(ax)
            f, l = pid == 0, pid == grid[ax] - 1
            first = f if first is None else jnp.logical_and(first, f)
            last = l if last is None else jnp.logical_and(last, l)

        @pl.when(first)
        def _():
            rider.start(r_in, r_out, sems)

        body(*a_in, *a_out, *a_scr)

        @pl.when(last)
        def _():
            rider.wait(r_in, r_out, sems)

    res = pl.pallas_call(
        wrapped, in_specs=list(in_specs) + [_ANY] * nr, out_specs=list(out_specs) + [_ANY] * nr,
        out_shape=list(outs) + rider.out_shapes(), scratch_shapes=list(scratch) + rider.scratch(), **common)(*ins, *rider.srcs)
    return list(res[:n_out]), list(res[n_out:])


def _block_mask(n_blocks, block_rows, block_cols):
    r = jnp.arange(n_blocks * block_rows) // block_rows
    c = jnp.arange(n_blocks * block_cols) // block_cols
    return (r[:, None] == c[None, :]).astype(F32)


def _mm(a, b, *, bias=None, res=None, res_scale=1.0, trans_b=False, out_dtype=F32, ln=None, name, rider=None):
    m, kdim = a.shape
    n = b.shape[0] if trans_b else b.shape[1]
    tm = _seq_tile(m, MM_ROW_TILE)
    tn = _pick(n, MM_COL_CAP)
    tk = _pick(kdim, MM_K_CAP)
    nk = kdim // tk
    has_bias, has_res, has_ln = bias is not None, res is not None, ln is not None
    assert not has_ln or tn == n

    def body(*refs):
        a_ref, b_ref = refs[0], refs[1]
        pos = 2
        bias_ref = res_ref = g_ref = beta_ref = x_ref = None
        if has_bias:
            bias_ref = refs[pos]
            pos += 1
        if has_res:
            res_ref = refs[pos]
            pos += 1
        if has_ln:
            g_ref, beta_ref = refs[pos], refs[pos + 1]
            pos += 2
        o_ref = refs[pos]
        pos += 1
        if has_ln:
            x_ref = refs[pos]
            pos += 1
        acc_ref = refs[pos]
        k = pl.program_id(2)

        @pl.when(k == 0)
        def _():
            acc_ref[...] = jnp.zeros_like(acc_ref)

        if trans_b:
            acc_ref[...] += _dot_nt(a_ref[...], b_ref[...])
        else:
            acc_ref[...] += _dot(a_ref[...], b_ref[...])

        @pl.when(k == nk - 1)
        def _():
            r = acc_ref[...]
            if has_bias:
                r = r + bias_ref[...]
            if has_res:
                r = r + res_scale * res_ref[...]
            o_ref[...] = r.astype(out_dtype)
            if has_ln:
                xc = r - jnp.mean(r, axis=1, keepdims=True)
                var = jnp.mean(xc * xc, axis=1, keepdims=True)
                x_ref[...] = xc * lax.rsqrt(var + LN_EPS) * g_ref[...] + beta_ref[...]

    ins = [a, b]
    in_specs = [pl.BlockSpec((tm, tk), lambda i, j, k: (i, k)),
                pl.BlockSpec((tn, tk), lambda i, j, k: (j, k)) if trans_b
                else pl.BlockSpec((tk, tn), lambda i, j, k: (k, j))]
    if has_bias:
        ins.append(bias)
        in_specs.append(pl.BlockSpec((1, tn), lambda i, j, k: (0, j)))
    if has_res:
        ins.append(res)
        in_specs.append(pl.BlockSpec((tm, tn), lambda i, j, k: (i, j)))
    if has_ln:
        ins += list(ln)
        in_specs += [pl.BlockSpec((1, tn), lambda i, j, k: (0, j))] * 2
    tile = pl.BlockSpec((tm, tn), lambda i, j, k: (i, j))
    outs, routs = _call(
        body, grid=(m // tm, n // tn, nk), ins=ins, in_specs=in_specs,
        outs=[jax.ShapeDtypeStruct((m, n), out_dtype)] + ([jax.ShapeDtypeStruct((m, n), F32)] if has_ln else []),
        out_specs=[tile] * (2 if has_ln else 1),
        scratch=[pltpu.VMEM((tm, tn), F32)], name=name, rider=rider)
    out = tuple(outs) if has_ln else outs[0]
    return out if rider is None else (out, routs)


def _mm_tn(a, b, *, colsum=False, out_dtype=F32, dev_cols=None, name, rider=None):
    s, ka = a.shape
    nb = b.shape[1]
    ts = _seq_tile(s, SEQ_TILE)
    tka = _pick(ka, MM_COL_CAP)
    tnb = _pick(nb, MM_COL_CAP)
    nk = s // ts
    assert not colsum or tka == ka
    per_tile = 1 if dev_cols is None else tnb // dev_cols
    assert dev_cols is None or tnb == per_tile * dev_cols

    def body(a_ref, b_ref, o_ref, *rest):
        cs_ref = rest[0] if colsum else None
        acc_ref = rest[-1]
        k = pl.program_id(2)

        @pl.when(k == 0)
        def _():
            acc_ref[...] = jnp.zeros_like(acc_ref)
            if colsum:
                cs_ref[...] = jnp.zeros_like(cs_ref)

        bv = b_ref[...]
        acc_ref[...] += _dot_tn(a_ref[...], bv)
        if colsum:
            cs_ref[...] += _colsum(bv.astype(F32))

        @pl.when(k == nk - 1)
        def _():
            if dev_cols is None:
                o_ref[...] = acc_ref[...].astype(out_dtype)
            else:
                for d in range(per_tile):
                    o_ref[d] = acc_ref[:, d * dev_cols:(d + 1) * dev_cols].astype(out_dtype)

    if dev_cols is None:
        main_shape, main_spec = (ka, nb), pl.BlockSpec((tka, tnb), lambda i, j, k: (i, j))
    else:
        main_shape = (nb // dev_cols, ka, dev_cols)
        main_spec = pl.BlockSpec((per_tile, tka, dev_cols), lambda i, j, k: (j, i, 0))
    outs, routs = _call(
        body, grid=(ka // tka, nb // tnb, nk), ins=[a, b],
        in_specs=[pl.BlockSpec((ts, tka), lambda i, j, k: (k, i)), pl.BlockSpec((ts, tnb), lambda i, j, k: (k, j))],
        outs=[jax.ShapeDtypeStruct(main_shape, out_dtype)] + ([jax.ShapeDtypeStruct((1, nb), F32)] if colsum else []),
        out_specs=[main_spec] + ([pl.BlockSpec((1, tnb), lambda i, j, k: (0, j))] if colsum else []),
        scratch=[pltpu.VMEM((tka, tnb), F32)], name=name, rider=rider)
    out = tuple(outs) if colsum else outs[0]
    return out if rider is None else (out, routs)


def _ln_fwd(r, g, b, *, name, rider=None):
    s, d = r.shape
    ts = _seq_tile(s, SEQ_TILE)

    def body(r_ref, g_ref, b_ref, o_ref):
        x = r_ref[...]
        mu = jnp.mean(x, axis=1, keepdims=True)
        xc = x - mu
        var = jnp.mean(xc * xc, axis=1, keepdims=True)
        o_ref[...] = xc * lax.rsqrt(var + LN_EPS) * g_ref[...] + b_ref[...]

    (out,), routs = _call(
        body, grid=(s // ts,), ins=[r, g, b],
        in_specs=[pl.BlockSpec((ts, d), lambda i: (i, 0)), _full_spec(g), _full_spec(b)],
        out_specs=[pl.BlockSpec((ts, d), lambda i: (i, 0))], outs=[jax.ShapeDtypeStruct((s, d), F32)],
        name=name, rider=rider)
    return out if rider is None else (out, routs)


def _ln_bwd(r, dy, g, *, name, rider=None):
    s, d = r.shape
    ts = _seq_tile(s, SEQ_TILE)

    def body(r_ref, dy_ref, g_ref, dr_ref, dg_ref, db_ref, ds_ref):
        @pl.when(pl.program_id(0) == 0)
        def _():
            dg_ref[...] = jnp.zeros_like(dg_ref)
            db_ref[...] = jnp.zeros_like(db_ref)
            ds_ref[...] = jnp.zeros_like(ds_ref)

        x = r_ref[...]
        dy = dy_ref[...]
        mu = jnp.mean(x, axis=1, keepdims=True)
        xc = x - mu
        var = jnp.mean(xc * xc, axis=1, keepdims=True)
        rstd = lax.rsqrt(var + LN_EPS)
        xh = xc * rstd
        dxh = dy * g_ref[...]
        m1 = jnp.mean(dxh, axis=1, keepdims=True)
        m2 = jnp.mean(dxh * xh, axis=1, keepdims=True)
        dr = rstd * (dxh - m1 - xh * m2)
        dr_ref[...] = dr
        dg_ref[...] += _colsum(dy * xh)
        db_ref[...] += _colsum(dy)
        ds_ref[...] += _colsum(dr)

    vec = jax.ShapeDtypeStruct((1, d), F32)
    vspec = pl.BlockSpec((1, d), lambda i: (0, 0))
    outs, routs = _call(
        body, grid=(s // ts,), ins=[r, dy, g],
        in_specs=[pl.BlockSpec((ts, d), lambda i: (i, 0)), pl.BlockSpec((ts, d), lambda i: (i, 0)), _full_spec(g)],
        out_specs=[pl.BlockSpec((ts, d), lambda i: (i, 0)), vspec, vspec, vspec],
        outs=[jax.ShapeDtypeStruct((s, d), F32), vec, vec, vec], name=name, rider=rider)
    return outs if rider is None else (outs, routs)


def _loss_ln_bwd(r, g, b, target, *, name):
    s, d = r.shape
    ts = _seq_tile(s, SEQ_TILE)

    def body(r_ref, g_ref, b_ref, t_ref, dr_ref, dg_ref, db_ref, l_ref):
        @pl.when(pl.program_id(0) == 0)
        def _():
            dg_ref[...] = jnp.zeros_like(dg_ref)
            db_ref[...] = jnp.zeros_like(db_ref)
            l_ref[...] = jnp.zeros_like(l_ref)

        x = r_ref[...]
        gam = g_ref[...]
        xc = x - jnp.mean(x, axis=1, keepdims=True)
        var = jnp.mean(xc * xc, axis=1, keepdims=True)
        rstd = lax.rsqrt(var + LN_EPS)
        xh = xc * rstd
        e = xh * gam + b_ref[...] - t_ref[...]
        part = jnp.sum(jnp.sum(e * e, axis=1, keepdims=True), axis=0, keepdims=True) * (0.5 / d)
        l_ref[...] += jnp.broadcast_to(part, l_ref.shape)
        dy = e * (1.0 / d)
        dxh = dy * gam
        m1 = jnp.mean(dxh, axis=1, keepdims=True)
        m2 = jnp.mean(dxh * xh, axis=1, keepdims=True)
        dr_ref[...] = rstd * (dxh - m1 - xh * m2)
        dg_ref[...] += _colsum(dy * xh)
        db_ref[...] += _colsum(dy)

    vec = jax.ShapeDtypeStruct((1, d), F32)
    vspec = pl.BlockSpec((1, d), lambda i: (0, 0))
    tile = pl.BlockSpec((ts, d), lambda i: (i, 0))
    return pl.pallas_call(
        body, grid=(s // ts,), in_specs=[tile, _full_spec(g), _full_spec(b), tile],
        out_specs=[tile, vspec, vspec, pl.BlockSpec((SUBLANE, LANE), lambda i: (0, 0))],
        out_shape=[jax.ShapeDtypeStruct((s, d), F32), vec, vec, jax.ShapeDtypeStruct((SUBLANE, LANE), F32)],
        compiler_params=_cparams(1), name=name)(r, g, b, target)


SCAN_CHUNK = 32


def _cscan_levels(bufs, apow_ref, t, pad, *, reverse):
    half = bufs[0].shape[1] // 2
    ch = min(SCAN_CHUNK, t)
    nlev = t.bit_length() - 1
    assert (1 << nlev) == t
    for k in range(nlev):
        d = 1 << k
        src, dst = bufs[k % 2], bufs[(k + 1) % 2]

        def chunk(c, carry, src=src, dst=dst, d=d, k=k):
            ar = apow_ref[k:k + 1, :half]
            ai = apow_ref[k:k + 1, half:]
            if reverse:
                ai = -ai
            r0 = pl.multiple_of(c * ch, ch)
            cur = src[pl.ds(pad + r0, ch), :]
            if d >= SUBLANE:
                off = pad + d if reverse else pad - d
                sh = src[pl.ds(off + r0, ch), :]
            elif reverse:
                blk = src[pl.ds(pad + r0, ch + SUBLANE), :]
                sh = pltpu.roll(blk, ch + SUBLANE - d, axis=0)[:ch, :]
            else:
                blk = src[pl.ds(pad - SUBLANE + r0, ch + SUBLANE), :]
                sh = pltpu.roll(blk, d, axis=0)[SUBLANE:, :]
            sre, sim = sh[:, :half], sh[:, half:]
            dst[pl.ds(pad + r0, ch), :half] = cur[:, :half] + ar * sre - ai * sim
            dst[pl.ds(pad + r0, ch), half:] = cur[:, half:] + ar * sim + ai * sre
            return carry

        lax.fori_loop(0, t // ch, chunk, 0)
    return nlev % 2


def _rscan_levels(abufs, bbufs, t, pad, *, reverse):
    nlev = t.bit_length() - 1
    assert (1 << nlev) == t
    for k in range(nlev):
        d = 1 << k
        asrc, adst = abufs[k % 2], abufs[(k + 1) % 2]
        bsrc, bdst = bbufs[k % 2], bbufs[(k + 1) % 2]
        off = pad + d if reverse else pad - d
        a = asrc[pad:pad + t, :]
        bdst[pad:pad + t, :] = a * bsrc[off:off + t, :] + bsrc[pad:pad + t, :]
        if k < nlev - 1:
            adst[pad:pad + t, :] = a * asrc[off:off + t, :]
    return nlev % 2


S5_CHUNK = 16
S5_SG = S5_GROUPS // 2
S5_SG_IN = 2 * S5_CHUNK * S5_CH
S5_SG_ST = 2 * S5_STATE


S5_HALF_SGS = S5_SG // 2
S5_HALF_IN = S5_HALF_SGS * S5_SG_IN


def _s5_perm():
    idx = jnp.arange(S5_HALF_IN)
    step, grp, chan = idx // LANE, (idx % LANE) // S5_CH, idx % S5_CH
    col = (grp // 2) * S5_SG_IN + (grp % 2) * (S5_CHUNK * S5_CH) + step * S5_CH + chan
    return (col[:, None] == idx[None, :]).astype(BF16)


def _s5_to_chunks(x, col_block, perm, *, name):
    s = x.shape[0]
    nb = s // S5_CHUNK

    def body(x_ref, perm_ref, o_ref):
        tok = jnp.concatenate([x_ref[pl.ds(t, nb, stride=S5_CHUNK), :].astype(BF16) for t in range(S5_CHUNK)], axis=1)
        grouped = _dot(tok, perm_ref[...]).astype(BF16)
        for k in range(S5_HALF_SGS):
            o_ref[k] = grouped[:, k * S5_SG_IN:(k + 1) * S5_SG_IN]

    return pl.pallas_call(
        body, grid=(2,),
        in_specs=[pl.BlockSpec((s, LANE), lambda h: (0, col_block + h)), _full_spec(perm)],
        out_specs=pl.BlockSpec((S5_HALF_SGS, nb, S5_SG_IN), lambda h: (h, 0, 0)),
        out_shape=jax.ShapeDtypeStruct((S5_SG, nb, S5_SG_IN), BF16),
        compiler_params=_cparams(1), name=name)(x, perm)


def _s5_from_chunks(y, perm, *, name):
    _, nb, _ = y.shape

    def body(y_ref, perm_ref, o_ref):
        grouped = jnp.concatenate([y_ref[k] for k in range(S5_HALF_SGS)], axis=1)
        hi = grouped.astype(BF16)
        lo = (grouped - hi.astype(F32)).astype(BF16)
        tok = _dot_nt(hi, perm_ref[...]) + _dot_nt(lo, perm_ref[...])
        for t in range(S5_CHUNK):
            o_ref[pl.ds(t, nb, stride=S5_CHUNK), :] = tok[:, t * LANE:(t + 1) * LANE]

    return pl.pallas_call(
        body, grid=(2,),
        in_specs=[pl.BlockSpec((S5_HALF_SGS, nb, S5_SG_IN), lambda h: (h, 0, 0)), _full_spec(perm)],
        out_specs=pl.BlockSpec((nb * S5_CHUNK, LANE), lambda h: (0, h)),
        out_shape=jax.ShapeDtypeStruct((nb * S5_CHUNK, D_GROUP), F32),
        compiler_params=_cparams(1), name=name)(y, perm)


def _s5_core_fwd(u2, m2, pre, pim, qre, qim, a16, *, name):
    sg, nb, nin = u2.shape
    st2 = 2 * S5_SG_ST
    pad = nb // 2

    def body(u_ref, m_ref, pre_ref, pim_ref, qre_ref, qim_ref, a_ref, y_ref, x_ref, buf0, buf1):
        @pl.when(pl.program_id(0) == 0)
        def _():
            buf0[0:pad, :] = jnp.zeros((pad, st2), F32)
            buf1[0:pad, :] = jnp.zeros((pad, st2), F32)

        u = u_ref[...]
        buf0[pad:pad + nb, :S5_SG_ST] = _dot(u, pre_ref[...])
        buf0[pad:pad + nb, S5_SG_ST:] = _dot(u, pim_ref[...])
        xbuf = (buf0, buf1)[_cscan_levels((buf0, buf1), a_ref, nb, pad, reverse=False)]
        x_ref[...] = xbuf[pad:pad + nb, :]
        xprev = xbuf[pad - 1:pad - 1 + nb, :]
        y_ref[...] = _dot(u, m_ref[...]) + _dot(xprev[:, :S5_SG_ST], qre_ref[...]) + _dot(xprev[:, S5_SG_ST:], qim_ref[...])

    ins = [u2, m2, pre, pim, qre, qim, a16]
    return pl.pallas_call(
        body, grid=(sg,), in_specs=[pl.BlockSpec((None,) + a.shape[1:], lambda i: (i, 0, 0)) for a in ins],
        out_specs=[pl.BlockSpec((None, nb, nin), lambda i: (i, 0, 0)), pl.BlockSpec((None, nb, st2), lambda i: (i, 0, 0))],
        out_shape=[jax.ShapeDtypeStruct((sg, nb, nin), F32), jax.ShapeDtypeStruct((sg, nb, st2), F32)],
        scratch_shapes=[pltpu.VMEM((pad + nb, st2), F32), pltpu.VMEM((pad + nb, st2), F32)],
        compiler_params=_cparams(1), name=name)(*ins)


def _s5_core_bwd(u2, dy2, x_all, m2, pre, pim, qre, qim, a16, *, name):
    sg, nb, nin = u2.shape
    half = S5_SG_ST
    st2 = 2 * half
    pad = nb // 2

    def body(u_ref, dy_ref, x_ref, m_ref, pre_ref, pim_ref, qre_ref, qim_ref, a_ref,
             du_ref, dm_ref, dpre_ref, dpim_ref, dqre_ref, dqim_ref, da_ref, buf2, buf3, xp):
        @pl.when(pl.program_id(0) == 0)
        def _():
            buf2[nb:nb + pad, :] = jnp.zeros((pad, st2), F32)
            buf3[nb:nb + pad, :] = jnp.zeros((pad, st2), F32)
            xp[0:SUBLANE, :] = jnp.zeros((SUBLANE, st2), F32)

        u = u_ref[...]
        dy = dy_ref[...]
        dm_ref[...] = _dot_tn(u, dy)
        xp[SUBLANE:SUBLANE + nb, :] = x_ref[...]
        xprev = xp[SUBLANE - 1:SUBLANE - 1 + nb, :]
        xre, xim = xprev[:, :half], xprev[:, half:]
        dqre_ref[...] = _dot_tn(xre, dy)
        dqim_ref[...] = _dot_tn(xim, dy)
        buf2[0:nb, :half] = _dot_nt(dy, qre_ref[...])
        buf2[0:nb, half:] = _dot_nt(dy, qim_ref[...])
        mbuf = (buf2, buf3)[_cscan_levels((buf2, buf3), a_ref, nb, 0, reverse=True)]
        lam = mbuf[1:1 + nb, :]
        lre, lim = lam[:, :half], lam[:, half:]
        dpre_ref[...] = _dot_tn(u, lre)
        dpim_ref[...] = _dot_tn(u, lim)
        du_ref[...] = _dot_nt(dy, m_ref[...]) + _dot_nt(lre, pre_ref[...]) + _dot_nt(lim, pim_ref[...])
        da_ref[:, :half] = _colsum(lre * xre + lim * xim)
        da_ref[:, half:] = _colsum(lim * xre - lre * xim)

    ins = [u2, dy2, x_all, m2, pre, pim, qre, qim, a16]
    outs = [jax.ShapeDtypeStruct((sg, nb, nin), F32)] + [jax.ShapeDtypeStruct(a.shape, F32) for a in (m2, pre, pim, qre, qim)] + \
           [jax.ShapeDtypeStruct((sg, 1, st2), F32)]
    return pl.pallas_call(
        body, grid=(sg,), in_specs=[pl.BlockSpec((None,) + a.shape[1:], lambda i: (i, 0, 0)) for a in ins],
        out_specs=[pl.BlockSpec((None,) + o.shape[1:], lambda i: (i, 0, 0)) for o in outs], out_shape=outs,
        scratch_shapes=[pltpu.VMEM((nb + pad, st2), F32), pltpu.VMEM((nb + pad, st2), F32),
                        pltpu.VMEM((SUBLANE + nb, st2), F32)],
        compiler_params=_cparams(1), name=name)(*ins)


def _s5_glu_fwd(y1, wglu, bglu, *, name, rider=None):
    s = y1.shape[0]
    t = _seq_tile(s, SEQ_TILE)

    def body(y1_ref, wglu_ref, bglu_ref, out_ref):
        y2 = _gelu(y1_ref[...])
        out_ref[...] = (y2 * _sigmoid(_dot(y2, wglu_ref[...]) + bglu_ref[...])).astype(BF16)

    return _call(
        body, grid=(s // t,), ins=[y1, wglu, bglu],
        in_specs=[pl.BlockSpec((t, D_GROUP), lambda i: (i, 0)), _full_spec(wglu), _full_spec(bglu)],
        out_specs=[pl.BlockSpec((t, D_GROUP), lambda i: (i, MIX_S5))], outs=[jax.ShapeDtypeStruct((s, D_MODEL), BF16)],
        name=name, rider=rider)


def _s5_glu_bwd(y1, dmix, wglu, bglu, *, name):
    s = y1.shape[0]
    t = _seq_tile(s, SEQ_TILE)

    def body(y1_ref, do_ref, wglu_ref, bglu_ref, dy1_ref, dwglu_ref, dbglu_ref):
        @pl.when(pl.program_id(0) == 0)
        def _():
            dwglu_ref[...] = jnp.zeros_like(dwglu_ref)
            dbglu_ref[...] = jnp.zeros_like(dbglu_ref)

        dout = do_ref[...]
        y2, dgelu = _gelu_and_grad(y1_ref[...])
        sg = _sigmoid(_dot(y2, wglu_ref[...]) + bglu_ref[...])
        dz = dout * y2 * sg * (1.0 - sg)
        dwglu_ref[...] += _dot_tn(y2, dz)
        dbglu_ref[...] += _colsum(dz)
        dy1_ref[...] = (dout * sg + _dot_nt(dz, wglu_ref[...])) * dgelu

    outs = [jax.ShapeDtypeStruct((s, D_GROUP), F32), jax.ShapeDtypeStruct((D_GROUP, D_GROUP), F32),
            jax.ShapeDtypeStruct((1, D_GROUP), F32)]
    return pl.pallas_call(
        body, grid=(s // t,),
        in_specs=[pl.BlockSpec((t, D_GROUP), lambda i: (i, 0)), pl.BlockSpec((t, D_GROUP), lambda i: (i, MIX_S5)),
                  _full_spec(wglu), _full_spec(bglu)],
        out_specs=[pl.BlockSpec((t, D_GROUP), lambda i: (i, 0)), _full_spec(outs[1]), _full_spec(outs[2])],
        out_shape=outs, compiler_params=_cparams(1), name=name)(y1, dmix, wglu, bglu)


def _pair_blockdiag(x):
    g, r, c = x.shape
    x = x.reshape(g // 2, 2, r, c)
    z = jnp.zeros_like(x[:, 0])
    return jnp.concatenate([jnp.concatenate([x[:, 0], z], axis=2), jnp.concatenate([z, x[:, 1]], axis=2)], axis=1)


def _s5_chunk_map(lam_re, lam_im, log_dt, b_re, b_im, c_re, c_im, d_skip):
    g, n, c, lc = S5_GROUPS, S5_STATE, S5_CH, S5_CHUNK
    dt = jnp.exp(log_dt)[:, None]
    mag, ang = lam_re * dt, lam_im * dt
    j = jnp.arange(lc + 1, dtype=F32)[:, None, None]
    pw_mag = jnp.exp(j * mag)
    pw_re, pw_im = pw_mag * jnp.cos(j * ang), pw_mag * jnp.sin(j * ang)
    a_re, a_im = pw_re[1], pw_im[1]
    den = lam_re * lam_re + lam_im * lam_im
    n_re = a_re - 1.0
    k_re = (n_re * lam_re + a_im * lam_im) / den
    k_im = (a_im * lam_re - n_re * lam_im) / den
    bb_re = k_re[..., None] * b_re - k_im[..., None] * b_im
    bb_im = k_re[..., None] * b_im + k_im[..., None] * b_re
    e_re = pw_re[:lc, :, :, None] * bb_re - pw_im[:lc, :, :, None] * bb_im
    e_im = pw_re[:lc, :, :, None] * bb_im + pw_im[:lc, :, :, None] * bb_re
    kern = jnp.einsum("gdn,jgnc->jgdc", c_re, e_re) - jnp.einsum("gdn,jgnc->jgdc", c_im, e_im)
    lags = jnp.pad(jnp.transpose(kern, (1, 3, 0, 2)), ((0, 0), (0, 0), (lc - 1, 0), (0, 0)))
    lags = lags.reshape(g, c, (2 * lc - 1) * c)
    m = jnp.stack([lags[:, :, (lc - 1 - s) * c:(2 * lc - 1 - s) * c] for s in range(lc)], axis=1).reshape(g, lc * c, lc * c)
    skip = jnp.tile(d_skip.reshape(g, 1, c), (1, lc, 1)).reshape(g, lc * c)
    m = m + jnp.eye(lc * c, dtype=F32)[None] * skip[:, None, :]
    p_re = jnp.transpose(e_re[::-1], (1, 0, 3, 2)).reshape(g, lc * c, n)
    p_im = jnp.transpose(e_im[::-1], (1, 0, 3, 2)).reshape(g, lc * c, n)
    f_re = c_re[None] * pw_re[1:, :, None, :] - c_im[None] * pw_im[1:, :, None, :]
    f_im = c_re[None] * pw_im[1:, :, None, :] + c_im[None] * pw_re[1:, :, None, :]
    q_re = jnp.transpose(f_re, (1, 3, 0, 2)).reshape(g, n, lc * c)
    q_im = -jnp.transpose(f_im, (1, 3, 0, 2)).reshape(g, n, lc * c)
    a16 = jnp.concatenate([pw_re[lc].reshape(S5_SG, 1, S5_SG_ST), pw_im[lc].reshape(S5_SG, 1, S5_SG_ST)], axis=2)
    return (_pair_blockdiag(m), _pair_blockdiag(p_re), _pair_blockdiag(p_im), _pair_blockdiag(q_re),
            _pair_blockdiag(q_im), a16)


def _s5_a16_powers(a16, nlev):
    half = S5_SG_ST
    re, im = a16[:, :, :half], a16[:, :, half:]
    rows = []
    for _ in range(nlev):
        rows.append(jnp.concatenate([re, im], axis=2))
        re, im = re * re - im * im, 2.0 * re * im
    n_rows = -(-nlev // SUBLANE) * SUBLANE
    rows += [jnp.zeros_like(rows[0])] * (n_rows - nlev)
    return lax.stop_gradient(jnp.concatenate(rows, axis=1))


CV_TILE = 256
CV_PAD = 32
CV_CHUNK = 64


def _shifted_copies(buf, shifted, rows):
    n = rows - SUBLANE
    for s in range(1, SUBLANE):
        shifted[s - 1, 0:n, :] = buf[s:s + n, :]


def _window(buf, shifted, o, ch):
    q, s = divmod(o, SUBLANE)
    if s == 0:
        return buf[o:o + ch, :]
    return shifted[s - 1, q * SUBLANE:q * SUBLANE + ch, :]


def _gn_stats(c, mavg):
    mu = _dot_hi(c, mavg)
    cen = c - mu
    var = _dot_hi(cen * cen, mavg)
    rstd = lax.rsqrt(var + LN_EPS)
    return cen * rstd, rstd


def _cv_fwd(h_in, cw, cb, gng, gnb, mavg, wpw, bpw, mix, *, name, rider=None):
    s = h_in.shape[0]
    t = _seq_tile(s, CV_TILE)
    ch = min(CV_CHUNK, t)

    def body(v_ref, g_ref, cw_ref, cb_ref, gng_ref, gnb_ref, mavg_ref, wpw_ref, bpw_ref, _mix_in, out_ref, c_ref, xpad,
             shifted):
        @pl.when(pl.program_id(0) == 0)
        def _():
            xpad[0:CV_PAD, :] = jnp.zeros((CV_PAD, D_GROUP), F32)

        xpad[CV_PAD:CV_PAD + t, :] = v_ref[...] * _sigmoid(g_ref[...])
        _shifted_copies(xpad, shifted, t + CV_PAD)
        for r0 in range(0, t, ch):
            acc = jnp.broadcast_to(cb_ref[...], (ch, D_GROUP))
            for k in range(CONV_WIDTH):
                o = CV_PAD - (CONV_WIDTH - 1) + k + r0
                acc = acc + cw_ref[k:k + 1, :] * _window(xpad, shifted, o, ch)
            c_ref[r0:r0 + ch, :] = acc
        xpad[0:CV_PAD, :] = xpad[t:t + CV_PAD, :]
        xn, _ = _gn_stats(c_ref[...], mavg_ref[...])
        gn = xn * gng_ref[...] + gnb_ref[...]
        out_ref[...] = (_dot(gn * _sigmoid(gn), wpw_ref[...]) + bpw_ref[...]).astype(BF16)

    ins = [h_in, h_in, cw, cb, gng, gnb, mavg, wpw, bpw, mix]
    in_specs = [pl.BlockSpec((t, D_GROUP), lambda i: (i, COL_CV_V)), pl.BlockSpec((t, D_GROUP), lambda i: (i, COL_CV_G))] + \
               [_full_spec(a) for a in ins[2:9]] + [_ANY]
    return _call(
        body, grid=(s // t,), ins=ins, in_specs=in_specs,
        out_specs=[pl.BlockSpec((t, D_GROUP), lambda i: (i, MIX_CV)), pl.BlockSpec((t, D_GROUP), lambda i: (i, 0))],
        outs=[jax.ShapeDtypeStruct((s, D_MODEL), BF16), jax.ShapeDtypeStruct((s, D_GROUP), F32)],
        aliases={9: 0},
        scratch=[pltpu.VMEM((CV_PAD + t, D_GROUP), F32), pltpu.VMEM((SUBLANE - 1, CV_PAD + t, D_GROUP), F32)],
        name=name, rider=rider)


def _cv_bwd(h_in, c, dmix, cw, gng, gnb, mavg, wpw, *, name, rider=None):
    s = h_in.shape[0]
    t = _seq_tile(s, CV_TILE)
    nt = s // t
    ch = min(CV_CHUNK, t)

    def body(v_ref, g_ref, c_ref, do_ref, cw_ref, gng_ref, gnb_ref, mavg_ref, wpw_ref,
             dvg_ref, dwpw_ref, dcw_ref, dbpw_ref, dgg_ref, dgb_ref, dcb_ref, dcpad, hgbuf, shifted):
        @pl.when(pl.program_id(0) == 0)
        def _():
            dcpad[t:t + CV_PAD, :] = jnp.zeros((CV_PAD, D_GROUP), F32)
            for r in (dwpw_ref, dcw_ref, dbpw_ref, dgg_ref, dgb_ref, dcb_ref):
                r[...] = jnp.zeros_like(r)

        mavg = mavg_ref[...]
        xn, rstd = _gn_stats(c_ref[...], mavg)
        gg = gng_ref[...]
        gn = xn * gg + gnb_ref[...]
        sg = _sigmoid(gn)
        dout = do_ref[...]
        dwpw_ref[...] += _dot_tn(gn * sg, dout)
        dbpw_ref[...] += _colsum(dout)
        dgn = _dot_nt(dout, wpw_ref[...]) * (sg * (1.0 + gn * (1.0 - sg)))
        dgg_ref[...] += _colsum(dgn * xn)
        dgb_ref[...] += _colsum(dgn)
        dxn = dgn * gg
        dc = rstd * (dxn - _dot_hi(dxn, mavg) - xn * _dot_hi(dxn * xn, mavg))
        dcb_ref[...] += _colsum(dc)
        dcpad[0:t, :] = dc

        v = v_ref[...]
        sgm = _sigmoid(g_ref[...])
        hgbuf[...] = v * sgm
        _shifted_copies(dcpad, shifted, t + CV_PAD)
        for r0 in range(0, t, ch):
            hg = hgbuf[r0:r0 + ch, :]
            acc = jnp.zeros((ch, D_GROUP), F32)
            for k in range(CONV_WIDTH):
                o = (CONV_WIDTH - 1) - k + r0
                sh = _window(dcpad, shifted, o, ch)
                acc = acc + cw_ref[k:k + 1, :] * sh
                dcw_ref[k:k + 1, :] += _colsum(hg * sh)
            hgbuf[r0:r0 + ch, :] = acc
        dcpad[t:t + CV_PAD, :] = dcpad[0:CV_PAD, :]
        dhg = hgbuf[...]
        dvg_ref[:, :D_GROUP] = dhg * sgm
        dvg_ref[:, D_GROUP:] = dhg * v * sgm * (1.0 - sgm)

    def rev(col):
        return lambda i: (nt - 1 - i, col)

    ins = [h_in, h_in, c, dmix, cw, gng, gnb, mavg, wpw]
    in_specs = [pl.BlockSpec((t, D_GROUP), rev(COL_CV_V)), pl.BlockSpec((t, D_GROUP), rev(COL_CV_G)),
                pl.BlockSpec((t, D_GROUP), rev(0)), pl.BlockSpec((t, D_GROUP), rev(MIX_CV))] + [_full_spec(a) for a in ins[4:]]
    vec = jax.ShapeDtypeStruct((1, D_GROUP), F32)
    outs = [jax.ShapeDtypeStruct((s, N_IN_COLS), F32),
            jax.ShapeDtypeStruct((D_GROUP, D_GROUP), F32), jax.ShapeDtypeStruct((CV_PAD, D_GROUP), F32), vec, vec, vec, vec]
    out_specs = [pl.BlockSpec((t, 2 * D_GROUP), rev(COL_CV_V // 2))] + [_full_spec(o) for o in outs[1:]]
    return _call(
        body, grid=(nt,), ins=ins, in_specs=in_specs, out_specs=out_specs, outs=outs,
        scratch=[pltpu.VMEM((t + CV_PAD, D_GROUP), F32), pltpu.VMEM((t, D_GROUP), F32),
                 pltpu.VMEM((SUBLANE - 1, t + CV_PAD, D_GROUP), F32)], name=name, rider=rider)


LRU_TILE = 256


def _lru_gates(xc, wr_ref, br_ref, wi_ref, bi_ref, sp_ref):
    r = _sigmoid(_dot(xc, wr_ref[...]) + br_ref[...])
    i = _sigmoid(_dot(xc, wi_ref[...]) + bi_ref[...])
    log_a = -LRU_C * r * sp_ref[...]
    a = jnp.exp(log_a)
    m = jnp.sqrt(_neg_expm1(2.0 * log_a))
    return r, i, a, m


def _lru_fwd(h_in, lcw, lcb, wr, br, wi, bi, sp, mix, *, name):
    s = h_in.shape[0]
    t = _seq_tile(s, LRU_TILE)
    pad = max(t // 2, SUBLANE)

    def body(xg_ref, xr_ref, lcw_ref, lcb_ref, wr_ref, br_ref, wi_ref, bi_ref, sp_ref, _mix_in,
             out_ref, xc_ref, h_ref, xpad, a0, a1, b0, b1, carry):
        @pl.when(pl.program_id(0) == 0)
        def _():
            xpad[0:SUBLANE, :] = jnp.zeros((SUBLANE, D_GROUP), F32)
            for bf in (a0, a1, b0, b1):
                bf[0:pad, :] = jnp.zeros((pad, D_GROUP), F32)
            carry[...] = jnp.zeros_like(carry)

        xpad[SUBLANE:SUBLANE + t, :] = xr_ref[...]
        xc = jnp.broadcast_to(lcb_ref[...], (t, D_GROUP))
        for k in range(LRU_CONV_WIDTH):
            o = SUBLANE - (LRU_CONV_WIDTH - 1) + k
            xc = xc + lcw_ref[k:k + 1, :] * xpad[o:o + t, :]
        xpad[0:SUBLANE, :] = xpad[t:t + SUBLANE, :]
        xc_ref[...] = xc
        _, i, a, m = _lru_gates(xc, wr_ref, br_ref, wi_ref, bi_ref, sp_ref)
        a0[pad:pad + t, :] = a
        b0[pad:pad + t, :] = m * (i * xc)
        b0[pad:pad + 1, :] += a0[pad:pad + 1, :] * carry[0:1, :]
        fin = _rscan_levels((a0, a1), (b0, b1), t, pad, reverse=False)
        hbuf = (b0, b1)[fin]
        carry[0:1, :] = hbuf[pad + t - 1:pad + t, :]
        h = hbuf[pad:pad + t, :]
        h_ref[...] = h
        out_ref[...] = (h * _gelu(xg_ref[...])).astype(BF16)

    ins = [h_in, h_in, lcw, lcb, wr, br, wi, bi, sp, mix]
    row = pl.BlockSpec((t, D_GROUP), lambda i: (i, 0))
    in_specs = [pl.BlockSpec((t, D_GROUP), lambda i: (i, COL_LRU_G)), pl.BlockSpec((t, D_GROUP), lambda i: (i, COL_LRU_X))] + \
               [_full_spec(a) for a in ins[2:9]] + [_ANY]
    return pl.pallas_call(
        body, grid=(s // t,), in_specs=in_specs,
        out_specs=[pl.BlockSpec((t, D_GROUP), lambda i: (i, MIX_LRU)), row, row],
        out_shape=[jax.ShapeDtypeStruct((s, D_MODEL), BF16)] + [jax.ShapeDtypeStruct((s, D_GROUP), F32)] * 2,
        input_output_aliases={9: 0},
        scratch_shapes=[pltpu.VMEM((SUBLANE + t, D_GROUP), F32)] + [pltpu.VMEM((pad + t, D_GROUP), F32)] * 4 +
                       [pltpu.VMEM((SUBLANE, D_GROUP), F32)],
        compiler_params=_cparams(1), name=name)(*ins)


def _lru_bwd(h_in, xc_all, h_all, dmix, lcw, wr, br, wi, bi, sp, dh_all, *, name):
    s = h_in.shape[0]
    t = _seq_tile(s, LRU_TILE)
    nt = s // t
    pad = max(t // 2, SUBLANE)
    tb = t // SUBLANE

    def body(xg_ref, xr_ref, xc_ref, h_ref, hprev_ref, do_ref, lcw_ref, wr_ref, br_ref, wi_ref, bi_ref, sp_ref, _dh_in,
             dgr_ref, dwr_ref, dwi_ref, dlcw_ref, dbr_ref, dbi_ref, dsp_ref, dlcb_ref,
             a0, a1, b0, b1, hp, dxpad, carry):
        pid = pl.program_id(0)

        @pl.when(pid == 0)
        def _():
            for bf in (a0, a1, b0, b1):
                bf[pad + t:pad + t + pad, :] = jnp.zeros((pad, D_GROUP), F32)
            dxpad[t:t + SUBLANE, :] = jnp.zeros((SUBLANE, D_GROUP), F32)
            carry[...] = jnp.zeros_like(carry)
            for r in (dwr_ref, dwi_ref, dlcw_ref, dbr_ref, dbi_ref, dsp_ref, dlcb_ref):
                r[...] = jnp.zeros_like(r)

        xc = xc_ref[...]
        h = h_ref[...]
        dout = do_ref[...]
        gate, dgate = _gelu_and_grad(xg_ref[...])
        dgr_ref[:, :D_GROUP] = dout * h * dgate
        r, i, a, m = _lru_gates(xc, wr_ref, br_ref, wi_ref, bi_ref, sp_ref)

        a0[pad:pad + t, :] = a
        b0[pad:pad + t, :] = dout * gate
        b0[pad + t - 1:pad + t, :] += carry[0:1, :]
        a1[pad:pad + t, :] = a0[pad + 1:pad + 1 + t, :]
        fin = _rscan_levels((a1, a0), (b0, b1), t, pad, reverse=True)
        lam = (b0, b1)[fin][pad:pad + t, :]
        carry[0:1, :] = a[0:1, :] * lam[0:1, :]

        is_first = pid == nt - 1
        hp[0:SUBLANE, :] = jnp.where(is_first, 0.0, hprev_ref[...])
        hp[SUBLANE:SUBLANE + t, :] = h
        hprev = hp[SUBLANE - 1:SUBLANE - 1 + t, :]

        ix = i * xc
        dmm = lam * ix
        dix = lam * m
        da = lam * hprev - dmm * (a / m)
        dlog_a = da * a
        dr = dlog_a * (-LRU_C * sp_ref[...])
        dsp_ref[...] += _colsum(dlog_a * (-LRU_C * r))
        dpr = dr * r * (1.0 - r)
        dpi = dix * xc * i * (1.0 - i)
        dbr_ref[...] += _colsum(dpr)
        dbi_ref[...] += _colsum(dpi)
        dwr_ref[...] += _dot_tn(xc, dpr)
        dwi_ref[...] += _dot_tn(xc, dpi)
        dxc = dix * i + _dot_nt(dpr, wr_ref[...]) + _dot_nt(dpi, wi_ref[...])
        dlcb_ref[...] += _colsum(dxc)

        dxpad[0:t, :] = dxc
        xr = xr_ref[...]
        dxr = jnp.zeros((t, D_GROUP), F32)
        for k in range(LRU_CONV_WIDTH):
            o = (LRU_CONV_WIDTH - 1) - k
            sh = dxpad[o:o + t, :]
            dxr = dxr + lcw_ref[k:k + 1, :] * sh
            dlcw_ref[k:k + 1, :] += _colsum(xr * sh)
        dxpad[t:t + SUBLANE, :] = dxpad[0:SUBLANE, :]
        dgr_ref[:, D_GROUP:] = dxr

    def rev(col):
        return lambda i: (nt - 1 - i, col)

    ins = [h_in, h_in, xc_all, h_all, h_all, dmix, lcw, wr, br, wi, bi, sp, dh_all]
    in_specs = [pl.BlockSpec((t, D_GROUP), rev(COL_LRU_G)), pl.BlockSpec((t, D_GROUP), rev(COL_LRU_X)),
                pl.BlockSpec((t, D_GROUP), rev(0)), pl.BlockSpec((t, D_GROUP), rev(0)),
                pl.BlockSpec((SUBLANE, D_GROUP), lambda i: (jnp.maximum((nt - 1 - i) * tb - 1, 0), 0)),
                pl.BlockSpec((t, D_GROUP), rev(MIX_LRU))] + [_full_spec(a) for a in ins[6:12]] + [_ANY]
    vec = jax.ShapeDtypeStruct((1, D_GROUP), F32)
    mat = jax.ShapeDtypeStruct((D_GROUP, D_GROUP), F32)
    outs = [jax.ShapeDtypeStruct((s, N_IN_COLS), F32), mat, mat, jax.ShapeDtypeStruct((SUBLANE, D_GROUP), F32),
            vec, vec, vec, vec]
    out_specs = [pl.BlockSpec((t, 2 * D_GROUP), rev(COL_LRU_G // 2))] + [_full_spec(o) for o in outs[1:]]
    return pl.pallas_call(
        body, grid=(nt,), in_specs=in_specs, out_specs=out_specs, out_shape=outs, input_output_aliases={12: 0},
        scratch_shapes=[pltpu.VMEM((pad + t + pad, D_GROUP), F32)] * 4 +
                       [pltpu.VMEM((SUBLANE + t, D_GROUP), F32), pltpu.VMEM((t + SUBLANE, D_GROUP), F32),
                        pltpu.VMEM((SUBLANE, D_GROUP), F32)],
        compiler_params=_cparams(1), name=name)(*ins)


def _blockdiag(w):
    h, d, _ = w.shape
    return jnp.tile(w.reshape(h * d, d), (1, h)) * _block_mask(h, d, d)


ATTN_TILE = 512
ATTN_SCALE = ATTN_HEAD_DIM ** -0.5


def _attn_big(kv):
    m = kv.shape[0]
    kbig = jnp.tile(kv[:, :D_GROUP].T, (1, ATTN_HEADS)) * _block_mask(ATTN_HEADS, ATTN_HEAD_DIM, m)
    vbig = jnp.tile(kv[:, D_GROUP:], (ATTN_HEADS, 1)) * _block_mask(ATTN_HEADS, m, ATTN_HEAD_DIM)
    return kbig, vbig


def _attn_probs(q, kbig_ref, m):
    sc = _dot(q, kbig_ref[...]) * ATTN_SCALE
    ps = []
    for h in range(ATTN_HEADS):
        sh = sc[:, h * m:(h + 1) * m]
        e = jnp.exp(sh - jnp.max(sh, axis=1, keepdims=True))
        ps.append(e / jnp.sum(e, axis=1, keepdims=True))
    return ps


def _attn_fwd(h_in, kbig, vbig, mix, *, name):
    s = h_in.shape[0]
    t = _seq_tile(s, ATTN_TILE)
    m = kbig.shape[1] // ATTN_HEADS

    def body(q_ref, kbig_ref, vbig_ref, _mix_in, o_ref):
        ps = _attn_probs(q_ref[...], kbig_ref, m)
        o_ref[...] = _dot(jnp.concatenate(ps, axis=1), vbig_ref[...]).astype(BF16)

    return pl.pallas_call(
        body, grid=(s // t,),
        in_specs=[pl.BlockSpec((t, D_GROUP), lambda i: (i, COL_Q)), _full_spec(kbig), _full_spec(vbig), _ANY],
        out_specs=pl.BlockSpec((t, D_GROUP), lambda i: (i, MIX_ATTN)),
        out_shape=jax.ShapeDtypeStruct((s, D_MODEL), BF16), input_output_aliases={3: 0},
        compiler_params=_cparams(1), name=name)(h_in, kbig, vbig, mix)


def _attn_bwd(h_in, dmix, kbig, vbig, du_s5, dh_all, *, name):
    s = h_in.shape[0]
    t = _seq_tile(s, ATTN_TILE)
    m = kbig.shape[1] // ATTN_HEADS

    def body(q_ref, do_ref, kbig_ref, vbig_ref, dus5_ref, _dh_in, dpair_ref, dk_ref, dv_ref):
        @pl.when(pl.program_id(0) == 0)
        def _():
            dk_ref[...] = jnp.zeros_like(dk_ref)
            dv_ref[...] = jnp.zeros_like(dv_ref)

        q = q_ref[...]
        dout = do_ref[...]
        ps = _attn_probs(q, kbig_ref, m)
        dp = _dot_nt(dout, vbig_ref[...])
        dss = []
        for h in range(ATTN_HEADS):
            dph = dp[:, h * m:(h + 1) * m]
            dss.append(ps[h] * (dph - jnp.sum(dph * ps[h], axis=1, keepdims=True)))
        ds = (jnp.concatenate(dss, axis=1) * ATTN_SCALE).astype(BF16)
        dv_ref[...] += _dot_tn(jnp.concatenate(ps, axis=1), dout)
        dpair_ref[:, :D_GROUP] = dus5_ref[...]
        dpair_ref[:, D_GROUP:] = _dot_nt(ds, kbig_ref[...])
        dk_ref[...] += _dot_tn(q, ds)

    assert (COL_S5, COL_Q) == (4, 5)
    outs = [jax.ShapeDtypeStruct((s, N_IN_COLS), F32), jax.ShapeDtypeStruct(kbig.shape, F32),
            jax.ShapeDtypeStruct(vbig.shape, F32)]
    return pl.pallas_call(
        body, grid=(s // t,),
        in_specs=[pl.BlockSpec((t, D_GROUP), lambda i: (i, COL_Q)), pl.BlockSpec((t, D_GROUP), lambda i: (i, MIX_ATTN)),
                  _full_spec(kbig), _full_spec(vbig), pl.BlockSpec((t, D_GROUP), lambda i: (i, 0)), _ANY],
        out_specs=[pl.BlockSpec((t, 2 * D_GROUP), lambda i: (i, COL_S5 // 2)), _full_spec(outs[1]), _full_spec(outs[2])],
        out_shape=outs, input_output_aliases={5: 0},
        compiler_params=_cparams(1), name=name)(h_in, dmix, kbig, vbig, du_s5, dh_all)


FFN_TILE = 128
FFN_COL_CHUNK = 256
FFN_ROW_CHUNK = 64


def _ffn_conv(pad_ref, w_ref, b_ref, r0, ch, c0):
    cc = FFN_COL_CHUNK
    acc = jnp.broadcast_to(b_ref[:, c0:c0 + cc], (ch, cc))
    for k in range(FFN_CONV_WIDTH):
        o = SUBLANE - (FFN_CONV_WIDTH - 1) + k + r0
        acc = acc + w_ref[k:k + 1, c0:c0 + cc] * pad_ref[o:o + ch, c0:c0 + cc]
    return acc


def _ffn_gate_fwd(u, fcw, fcb, *, name, rider=None):
    s = u.shape[0]
    t = _seq_tile(s, FFN_TILE)
    ch = min(FFN_ROW_CHUNK, t)
    cc = FFN_COL_CHUNK

    def body(u_ref, w_ref, b_ref, o_ref, uc_ref, upad):
        @pl.when(pl.program_id(0) == 0)
        def _():
            upad[0:SUBLANE, :] = jnp.zeros((SUBLANE, 2 * D_FF), F32)

        upad[SUBLANE:SUBLANE + t, :] = u_ref[...].astype(F32)
        for c0 in range(0, D_FF, cc):
            for r0 in range(0, t, ch):
                val = _ffn_conv(upad, w_ref, b_ref, r0, ch, c0)
                gt = _ffn_conv(upad, w_ref, b_ref, r0, ch, c0 + D_FF)
                o_ref[r0:r0 + ch, c0:c0 + cc] = (val * _gelu(gt)).astype(BF16)
                uc_ref[r0:r0 + ch, c0:c0 + cc] = val.astype(BF16)
                uc_ref[r0:r0 + ch, c0 + D_FF:c0 + D_FF + cc] = gt.astype(BF16)
        upad[0:SUBLANE, :] = upad[t:t + SUBLANE, :]

    return _call(
        body, grid=(s // t,), ins=[u, fcw, fcb],
        in_specs=[pl.BlockSpec((t, 2 * D_FF), lambda i: (i, 0)), _full_spec(fcw), _full_spec(fcb)],
        out_specs=[pl.BlockSpec((t, D_FF), lambda i: (i, 0)), pl.BlockSpec((t, 2 * D_FF), lambda i: (i, 0))],
        outs=[jax.ShapeDtypeStruct((s, D_FF), BF16), jax.ShapeDtypeStruct((s, 2 * D_FF), BF16)],
        scratch=[pltpu.VMEM((SUBLANE + t, 2 * D_FF), F32)], name=name, rider=rider)


def _ffn_gate_bwd(u, uc, dh, fcw, *, name, rider=None):
    s = u.shape[0]
    t = _seq_tile(s, FFN_TILE)
    nt = s // t
    ch = min(FFN_ROW_CHUNK, t)
    cc = FFN_COL_CHUNK

    def body(u_ref, uc_ref, dh_ref, w_ref, du_ref, dw_ref, db_ref, dpad):
        @pl.when(pl.program_id(0) == 0)
        def _():
            dpad[t:t + SUBLANE, :] = jnp.zeros((SUBLANE, 2 * D_FF), F32)
            dw_ref[...] = jnp.zeros_like(dw_ref)
            db_ref[...] = jnp.zeros_like(db_ref)

        for c0 in range(0, D_FF, cc):
            for r0 in range(0, t, ch):
                val = uc_ref[r0:r0 + ch, c0:c0 + cc].astype(F32)
                gt = uc_ref[r0:r0 + ch, c0 + D_FF:c0 + D_FF + cc].astype(F32)
                gl, dgl = _gelu_and_grad(gt)
                d = dh_ref[r0:r0 + ch, c0:c0 + cc].astype(F32)
                dpad[r0:r0 + ch, c0:c0 + cc] = d * gl
                dpad[r0:r0 + ch, c0 + D_FF:c0 + D_FF + cc] = d * val * dgl
        for c0 in range(0, 2 * D_FF, cc):
            dbs = jnp.zeros((1, cc), F32)
            dws = [jnp.zeros((1, cc), F32) for _ in range(FFN_CONV_WIDTH)]
            for r0 in range(0, t, ch):
                x = u_ref[r0:r0 + ch, c0:c0 + cc].astype(F32)
                acc = jnp.zeros((ch, cc), F32)
                for k in range(FFN_CONV_WIDTH):
                    o = (FFN_CONV_WIDTH - 1) - k + r0
                    sh = dpad[o:o + ch, c0:c0 + cc]
                    acc = acc + w_ref[k:k + 1, c0:c0 + cc] * sh
                    dws[k] = dws[k] + _colsum(x * sh)
                    if k == FFN_CONV_WIDTH - 1:
                        dbs = dbs + _colsum(sh)
                du_ref[r0:r0 + ch, c0:c0 + cc] = acc.astype(BF16)
            db_ref[:, c0:c0 + cc] += dbs
            for k in range(FFN_CONV_WIDTH):
                dw_ref[k:k + 1, c0:c0 + cc] += dws[k]
        dpad[t:t + SUBLANE, :] = dpad[0:SUBLANE, :]

    outs = [jax.ShapeDtypeStruct((s, 2 * D_FF), BF16), jax.ShapeDtypeStruct((SUBLANE, 2 * D_FF), F32),
            jax.ShapeDtypeStruct((1, 2 * D_FF), F32)]
    return _call(
        body, grid=(nt,), ins=[u, uc, dh, fcw],
        in_specs=[pl.BlockSpec((t, 2 * D_FF), lambda i: (nt - 1 - i, 0)),
                  pl.BlockSpec((t, 2 * D_FF), lambda i: (nt - 1 - i, 0)),
                  pl.BlockSpec((t, D_FF), lambda i: (nt - 1 - i, 0)), _full_spec(fcw)],
        out_specs=[pl.BlockSpec((t, 2 * D_FF), lambda i: (nt - 1 - i, 0)), _full_spec(outs[1]), _full_spec(outs[2])],
        outs=outs, scratch=[pltpu.VMEM((t + SUBLANE, 2 * D_FF), F32)], name=name, rider=rider)


def _adamw_body(g_ref, w_ref, m_ref, v_ref, go_ref, d_ref, mo_ref, vo_ref):
    inv_b1 = 1.0 - ADAM_B1 ** ADAM_STEP
    inv_b2 = 1.0 - ADAM_B2 ** ADAM_STEP
    g = g_ref[0].astype(F32)
    for dev in range(1, N_DEV):
        g = g + g_ref[dev].astype(F32)
    go_ref[...] = g
    mn = ADAM_B1 * m_ref[...] + (1.0 - ADAM_B1) * g
    vn = ADAM_B2 * v_ref[...] + (1.0 - ADAM_B2) * (g * g)
    mo_ref[...] = mn
    vo_ref[...] = vn
    d_ref[...] = -ADAM_LR * ((mn / inv_b1) / (jnp.sqrt(vn / inv_b2) + ADAM_EPS) + ADAM_WD * w_ref[...])


def _adamw(gstack, w, m, v, *, name):
    _, r, c = gstack.shape
    tr = _pick_rows(r, PACK_ROW_BLOCK)

    def body(*refs):
        _adamw_body(*refs)

    blk = pl.BlockSpec((tr, c), lambda i: (i, 0))
    sh = jax.ShapeDtypeStruct((r, c), F32)
    return pl.pallas_call(
        body, grid=(r // tr,),
        in_specs=[pl.BlockSpec((N_DEV, tr, c), lambda i: (0, i, 0)), blk, blk, blk],
        out_specs=[blk] * 4, out_shape=[sh] * 4,
        compiler_params=_cparams(1), name=name)(gstack, w, m, v)


def _adamw_layer(gstack, w, m, v, layer, into, *, name):
    n_layers, r, c = w.shape
    tr = _pick_rows(r, PACK_ROW_BLOCK)

    def body(g_ref, w_ref, m_ref, v_ref, *rest):
        _adamw_body(g_ref, w_ref, m_ref, v_ref, *rest[-4:])

    blk = pl.BlockSpec((None, tr, c), lambda i: (layer, i, 0))
    sh = jax.ShapeDtypeStruct((n_layers, r, c), F32)
    into = list(into or [])
    return pl.pallas_call(
        body, grid=(r // tr,),
        in_specs=[pl.BlockSpec((N_DEV, tr, c), lambda i: (0, i, 0)), blk, blk, blk] + [_ANY] * len(into),
        out_specs=[blk] * 4, out_shape=[sh] * 4, input_output_aliases={4 + k: k for k in range(len(into))},
        compiler_params=_cparams(1), name=name)(gstack, w, m, v, *into)


def _exchange(rider, *, name):
    n = rider.n

    def body(*refs):
        x_refs, out_refs, sems = refs[:n], refs[n:2 * n], refs[2 * n:]
        rider.start(x_refs, out_refs, sems)
        rider.wait(x_refs, out_refs, sems)

    return pl.pallas_call(
        body, in_specs=[_ANY] * n, out_specs=[_ANY] * n, out_shape=rider.out_shapes(),
        scratch_shapes=rider.scratch(), name=name)(*rider.srcs)


def _pack_rows(n):
    rows = -(-n // PACK_COLS)
    return -(-rows // SUBLANE) * SUBLANE


def _pack(arrs, dtype):
    flat = jnp.concatenate([a.reshape(-1).astype(dtype) for a in arrs])
    rows = _pack_rows(flat.shape[0])
    flat = jnp.pad(flat, (0, rows * PACK_COLS - flat.shape[0]))
    return flat.reshape(rows, PACK_COLS)


def _pack_lead(arrs, dtype):
    flat = jnp.concatenate([a.reshape(N_DEV, -1).astype(dtype) for a in arrs], axis=1)
    rows = _pack_rows(flat.shape[1])
    flat = jnp.pad(flat, ((0, 0), (0, rows * PACK_COLS - flat.shape[1])))
    return flat.reshape(N_DEV, rows, PACK_COLS)


def _pack_layers(arrs, dtype):
    n_layers = arrs[0].shape[0]
    flat = jnp.concatenate([a.reshape(n_layers, -1).astype(dtype) for a in arrs], axis=1)
    rows = _pack_rows(flat.shape[1])
    flat = jnp.pad(flat, ((0, 0), (0, rows * PACK_COLS - flat.shape[1])))
    return flat.reshape(n_layers, rows, PACK_COLS)


def _unpack_layers(packed, shapes):
    flat = packed.reshape(packed.shape[0], -1)
    out, pos = [], 0
    for sh in shapes:
        n = math.prod(sh[1:])
        out.append(flat[:, pos:pos + n].reshape(sh))
        pos += n
    return out


def _unpack(packed, shapes, lead=False):
    flat = packed.reshape(N_DEV, -1) if lead else packed.reshape(-1)
    out, pos = [], 0
    for sh in shapes:
        n = math.prod(sh)
        out.append(flat[:, pos:pos + n].reshape((N_DEV,) + tuple(sh)) if lead else flat[pos:pos + n].reshape(sh))
        pos += n
    return out


def _join_shards(stacked, axis):
    return jnp.concatenate([stacked[d] for d in range(N_DEV)], axis=axis)


def _split_shards(full, axis):
    return jnp.stack(jnp.split(full, N_DEV, axis=axis), axis=0)


def _perm_in_cols(a, inverse=False):
    blocks = jnp.split(a, 6, axis=-1)
    if inverse:
        order = [IN_PERM.index(j) for j in range(6)]
    else:
        order = list(IN_PERM)
    return jnp.concatenate([blocks[j] for j in order], axis=-1)


def _row(v):
    return v.reshape(1, -1)


def _pad_rows(w, rows):
    return jnp.pad(w, ((0, rows - w.shape[0]), (0, 0)))


def _gn_avg_matrix():
    return _block_mask(GN_GROUPS, D_GROUP // GN_GROUPS, D_GROUP // GN_GROUPS) / (D_GROUP // GN_GROUPS)


def _layer_params(p, l):
    q = {}
    s5_mats, q["s5_vjp"] = jax.vjp(_s5_chunk_map, p["s5_lam_re"][l], p["s5_lam_im"][l], p["s5_log_dt"][l],
                                   p["s5_b_re"][l], p["s5_b_im"][l], p["s5_c_re"][l], p["s5_c_im"][l], p["s5_d"][l])
    q["s5_mats"] = [m.astype(BF16) for m in s5_mats[:5]]
    q["s5_a16"] = s5_mats[5]
    (q["wr"], q["wi"]), q["lru_w_vjp"] = jax.vjp(lambda r, i: (_blockdiag(r), _blockdiag(i)), p["lru_w_r"][l], p["lru_w_i"][l])
    q["wr"], q["wi"] = q["wr"].astype(BF16), q["wi"].astype(BF16)
    q["sp"], q["sp_vjp"] = jax.vjp(lambda lam: _row(jax.nn.softplus(-lam)), p["lru_lam"][l])
    return q


WEIGHT_RIDES = {(0, "ln_in_fwd"): [("w_in", 0)],
                (0, "inproj"): [("attn_w_kv", 0), ("w_out", 0), ("small_pack", 0)],
                (0, "cv_fwd"): [("ffn_w_up#a", 0)],
                (0, "outproj"): [("ffn_w_up#b", 0)],
                (0, "ffn_up"): [("ffn_w_down", 0), ("w_in", 1), ("attn_w_kv", 1), ("w_out", 1)],
                (0, "ffn_gate_fwd"): [("ffn_w_up", 1)],
                (0, "ffn_down"): [("ffn_w_down", 1)]}
GRAD_RIDES = {(1, "ffn_gate_bwd"): [("ffn_w_down", 1)],
              (0, "dw_down"): [("w_out", 1), ("attn_w_kv", 1), ("w_in", 1)],
              (0, "dhff"): [("rep", 1), ("ssh", 1)],
              (0, "ffn_gate_bwd"): [("ffn_w_up", 1)],
              (0, "dw_up"): [("ffn_w_down", 0)],
              (0, "dx1"): [("ffn_w_up", 0)],
              (0, "cv_bwd"): [("w_out", 0)],
              (0, "dw_in"): [("attn_w_kv", 0), ("ssh", 0), ("rep", 0)],
              (0, "dxs"): [("w_in", 0)]}


def _join_cols(pieces, *, name):
    n_dev, k, c = pieces[0].shape
    assert (2 * c) % LANE == 0 and all(p.shape == pieces[0].shape for p in pieces)
    n_p = len(pieces)

    def body(*refs):
        o_ref = refs[n_p]
        for i in range(n_p):
            @pl.when(pl.program_id(0) == i)
            def _(i=i):
                o_ref[...] = jnp.concatenate([refs[i][0], refs[i][1]], axis=1)

    return pl.pallas_call(
        body, grid=(n_p, n_dev // 2), in_specs=[pl.BlockSpec((2, k, c), lambda i, j: (j, 0, 0))] * n_p,
        out_specs=pl.BlockSpec((k, 2 * c), lambda i, j: (i, j)),
        out_shape=jax.ShapeDtypeStruct((n_p * k, n_dev * c), pieces[0].dtype),
        compiler_params=_cparams(2), name=name)(*pieces)


def _split_cols(full, *, name):
    k, n = full.shape
    c = n // N_DEV
    assert (2 * c) % LANE == 0

    def body(x_ref, o_ref):
        o_ref[0] = x_ref[:, :c]
        o_ref[1] = x_ref[:, c:]

    return pl.pallas_call(
        body, grid=(N_DEV // 2,), in_specs=[pl.BlockSpec((k, 2 * c), lambda j: (0, j))],
        out_specs=pl.BlockSpec((2, k, c), lambda j: (j, 0, 0)), out_shape=jax.ShapeDtypeStruct((N_DEV, k, c), full.dtype),
        compiler_params=_cparams(1), name=name)(full)


def _assemble_weight(n, pieces, layer=0):
    if SHARDED[n] == 2:
        full = _join_cols(pieces, name=f"l{layer}_join_{n}")
        return _perm_in_cols(full) if n == "w_in" else full
    (gathered,) = pieces
    return gathered.reshape(-1, gathered.shape[-1])


def _grad_source(n, g, layer=0):
    g = g.astype(BF16)
    if SHARDED[n] == 2:
        if n == "w_in":
            g = _perm_in_cols(g, inverse=True)
        return _split_cols(g, name=f"l{layer}_split_d{n}"), "lead"
    return g, "rows"


def _hosted(fn, keys_rider, land, *args, **kw):
    keys, rider = keys_rider
    if rider is None:
        return fn(*args, **kw)
    out, routs = fn(*args, rider=rider, **kw)
    land(keys, routs)
    return out


def _local_step(x, mem, target, p, big_w, shards=None, unpack_small=None):
    dist = shards is not None
    gdt = BF16 if dist else F32
    small, saved = {}, []
    big_g, ready, recv = {}, {}, {}
    mavg = _gn_avg_matrix()
    s5_perm = _s5_perm()

    def weight_rider(l, host):
        keys = WEIGHT_RIDES.get((l, host), []) if dist else []
        return keys, (_Rider([shards[n][ll] for n, ll in keys], ["all"] * len(keys)) if keys else None)

    halves = {}

    def land_weights(keys, routs):
        for (n, ll), r in zip(keys, routs):
            if n == "small_pack":
                p.update(unpack_small(r))
            elif "#" in n:
                base = n.split("#")[0]
                halves[(n, ll)] = r
                if (base + "#a", ll) in halves and (base + "#b", ll) in halves:
                    big_w[base][ll] = _assemble_weight(base, [halves[(base + "#a", ll)], halves[(base + "#b", ll)]], ll)
            else:
                big_w[n][ll] = _assemble_weight(n, [r], ll)

    def grad_rider(l, host):
        keys = [k for k in GRAD_RIDES.get((l, host), []) if k in ready] if dist else []
        return keys, (_Rider([ready[k][0] for k in keys], [ready[k][1] for k in keys]) if keys else None)

    def land_grads(keys, routs):
        for k, r in zip(keys, routs):
            recv[k] = r
            del ready[k]

    def big_grad(n, l, g):
        if dist:
            ready[(n, l)] = _grad_source(n, g, l)
        else:
            big_g[(n, l)] = g

    xs = _hosted(_ln_fwd, weight_rider(0, "ln_in_fwd"), land_weights, x, _row(p["ln_in_g"]), _row(p["ln_in_b"]),
                 name="ln_in_fwd")
    for l in range(DEPTH):
        q = _layer_params(p, l)
        n = f"l{l}_"
        hin = _hosted(_mm, weight_rider(l, "inproj"), land_weights, xs, big_w["w_in"][l], bias=_row(p["b_in"][l]),
                      name=n + "inproj")
        nb = hin.shape[0] // S5_CHUNK
        s5_pows = _s5_a16_powers(q["s5_a16"], nb.bit_length() - 1)
        s5_u2 = _s5_to_chunks(hin, COL_S5 * (D_GROUP // LANE), s5_perm, name=n + "s5_in")
        s5_y2, s5_x = _s5_core_fwd(s5_u2, *q["s5_mats"], s5_pows, name=n + "s5_core_fwd")
        s5_y1 = _s5_from_chunks(s5_y2, s5_perm, name=n + "s5_out")
        (mix,), _ = _s5_glu_fwd(s5_y1, p["s5_w_glu"][l], _row(p["s5_b_glu"][l]), name=n + "s5_glu_fwd")
        cvw = _pad_rows(p["cv_w"][l], CV_PAD)
        keys, rd = weight_rider(l, "cv_fwd")
        (mix, cv_c), routs = _cv_fwd(hin, cvw, _row(p["cv_b"][l]), _row(p["cv_gn_g"][l]), _row(p["cv_gn_b"][l]), mavg,
                                     p["cv_w_pw"][l], _row(p["cv_b_pw"][l]), mix, name=n + "cv_fwd", rider=rd)
        land_weights(keys, routs)
        lcw = _pad_rows(p["lru_conv_w"][l], SUBLANE)
        mix, lru_xc, lru_h = _lru_fwd(hin, lcw, _row(p["lru_conv_b"][l]), q["wr"], _row(p["lru_b_r"][l]), q["wi"],
                                      _row(p["lru_b_i"][l]), q["sp"], mix, name=n + "lru_fwd")
        kv = _mm(mem, big_w["attn_w_kv"][l], name=n + "kv")
        (kbig, vbig), kv_vjp = jax.vjp(_attn_big, kv)
        kbig, vbig = kbig.astype(BF16), vbig.astype(BF16)
        mix = _attn_fwd(hin, kbig, vbig, mix, name=n + "attn_fwd")
        r1, x1 = _hosted(_mm, weight_rider(l, "outproj"), land_weights, mix, big_w["w_out"][l], bias=_row(p["b_out"][l]),
                         res=xs, res_scale=ALPHA, ln=(_row(p["ln1_g"][l]), _row(p["ln1_b"][l])), name=n + "outproj")
        u = _hosted(_mm, weight_rider(l, "ffn_up"), land_weights, x1, big_w["ffn_w_up"][l], out_dtype=BF16,
                    name=n + "ffn_up")
        fcw = _pad_rows(p["ffn_conv_w"][l], SUBLANE)
        fcb = _row(p["ffn_conv_b"][l])
        keys, rd = weight_rider(l, "ffn_gate_fwd")
        (hff, uc), routs = _ffn_gate_fwd(u, fcw, fcb, name=n + "ffn_gate_fwd", rider=rd)
        land_weights(keys, routs)
        if l < DEPTH - 1:
            r2, x2 = _hosted(_mm, weight_rider(l, "ffn_down"), land_weights, hff, big_w["ffn_w_down"][l], res=x1,
                             res_scale=ALPHA, ln=(_row(p["ln2_g"][l]), _row(p["ln2_b"][l])), name=n + "ffn_down")
        else:
            r2, x2 = _mm(hff, big_w["ffn_w_down"][l], res=x1, res_scale=ALPHA, name=n + "ffn_down"), None
        saved.append(dict(q=q, xs=xs, hin=hin, s5_y1=s5_y1, s5_u2=s5_u2, s5_x=s5_x, s5_pows=s5_pows, cvw=cvw, cv_c=cv_c, lcw=lcw, lru_xc=lru_xc,
                          lru_h=lru_h, kbig=kbig, vbig=vbig, kv_vjp=kv_vjp, mix=mix, r1=r1, x1=x1, u=u, uc=uc, fcw=fcw,
                          hff=hff, r2=r2))
        xs = x2

    top = DEPTH - 1
    dr_top, dg_top, db_top, loss_blk = _loss_ln_bwd(saved[top]["r2"], _row(p["ln2_g"][top]), _row(p["ln2_b"][top]), target,
                                                     name="loss_ln_bwd")
    loss = loss_blk[0, 0]
    dx = None

    for l in reversed(range(DEPTH)):
        sv = saved[l]
        q = sv["q"]
        n = f"l{l}_"
        g = {}
        if l == top:
            dr2, g["ln2_g"], g["ln2_b"] = dr_top, dg_top, db_top
        else:
            dr2, g["ln2_g"], g["ln2_b"], _ = _ln_bwd(sv["r2"], dx, _row(p["ln2_g"][l]), name=n + "ln2_bwd")
        big_grad("ffn_w_down", l, _hosted(_mm_tn, grad_rider(l, "dw_down"), land_grads, sv["hff"], dr2, out_dtype=gdt,
                                          name=n + "dw_down"))
        dhff = _hosted(_mm, grad_rider(l, "dhff"), land_grads, dr2, big_w["ffn_w_down"][l], trans_b=True,
                       out_dtype=BF16, name=n + "dhff")
        keys, rd = grad_rider(l, "ffn_gate_bwd")
        (du, dfw, g["ffn_conv_b"]), routs = _ffn_gate_bwd(sv["u"], sv["uc"], dhff, sv["fcw"], name=n + "ffn_gate_bwd",
                                                          rider=rd)
        land_grads(keys, routs)
        g["ffn_conv_w"] = dfw[:FFN_CONV_WIDTH]
        if dist:
            ready[("ffn_w_up", l)] = (_hosted(_mm_tn, grad_rider(l, "dw_up"), land_grads, sv["x1"], du, out_dtype=gdt,
                                              dev_cols=du.shape[1] // N_DEV, name=n + "dw_up"), "lead")
        else:
            big_grad("ffn_w_up", l, _mm_tn(sv["x1"], du, name=n + "dw_up"))
        dx1 = _hosted(_mm, grad_rider(l, "dx1"), land_grads, du, big_w["ffn_w_up"][l], trans_b=True, res=dr2,
                      res_scale=ALPHA, name=n + "dx1")
        dr1, g["ln1_g"], g["ln1_b"], g["b_out"] = _ln_bwd(sv["r1"], dx1, _row(p["ln1_g"][l]), name=n + "ln1_bwd")
        big_grad("w_out", l, _mm_tn(sv["mix"], dr1, out_dtype=gdt, name=n + "dw_out"))
        dmix = _mm(dr1, big_w["w_out"][l], trans_b=True, name=n + "dmix")

        hin = sv["hin"]
        keys, rd = grad_rider(l, "cv_bwd")
        (dh, g["cv_w_pw"], dcw, g["cv_b_pw"], g["cv_gn_g"], g["cv_gn_b"], g["cv_b"]), routs = _cv_bwd(
            hin, sv["cv_c"], dmix, sv["cvw"], _row(p["cv_gn_g"][l]), _row(p["cv_gn_b"][l]), mavg, p["cv_w_pw"][l],
            name=n + "cv_bwd", rider=rd)
        land_grads(keys, routs)
        g["cv_w"] = dcw[:CONV_WIDTH]
        dh, dwr, dwi, dlcw, g["lru_b_r"], g["lru_b_i"], dsp, g["lru_conv_b"] = _lru_bwd(
            hin, sv["lru_xc"], sv["lru_h"], dmix, sv["lcw"], q["wr"], _row(p["lru_b_r"][l]), q["wi"],
            _row(p["lru_b_i"][l]), q["sp"], dh, name=n + "lru_bwd")
        g["lru_conv_w"] = dlcw[:LRU_CONV_WIDTH]
        g["lru_w_r"], g["lru_w_i"] = q["lru_w_vjp"]((dwr, dwi))
        (g["lru_lam"],) = q["sp_vjp"](dsp)
        dy1, g["s5_w_glu"], g["s5_b_glu"] = _s5_glu_bwd(sv["s5_y1"], dmix, p["s5_w_glu"][l], _row(p["s5_b_glu"][l]),
                                                        name=n + "s5_glu_bwd")
        s5_du2, *s5_dmats = _s5_core_bwd(sv["s5_u2"], _s5_to_chunks(dy1, 0, s5_perm, name=n + "s5_din"), sv["s5_x"],
                                         *q["s5_mats"], sv["s5_pows"], name=n + "s5_core_bwd")
        (g["s5_lam_re"], g["s5_lam_im"], g["s5_log_dt"], g["s5_b_re"], g["s5_b_im"], g["s5_c_re"], g["s5_c_im"],
         g["s5_d"]) = q["s5_vjp"](tuple(s5_dmats))
        dh, dkbig, dvbig = _attn_bwd(hin, dmix, sv["kbig"], sv["vbig"],
                                     _s5_from_chunks(s5_du2, s5_perm, name=n + "s5_dout"), dh, name=n + "attn_bwd")
        (dkv,) = sv["kv_vjp"]((dkbig, dvbig))
        big_grad("attn_w_kv", l, _mm_tn(mem, dkv, out_dtype=gdt, name=n + "dw_kv"))

        if dist:
            ready[("ssh", l)] = (_pack_lead([_split_shards(g[k], SHARDED[k] - 1) for k in SMALL_SHARDED], F32), "lead")
            ready[("rep", l)] = (_pack([g[k] for k in REP_LAYERED], F32), "all")
        gw_in, g["b_in"] = _hosted(_mm_tn, grad_rider(l, "dw_in"), land_grads, sv["xs"], dh, colsum=True, out_dtype=gdt,
                                   name=n + "dw_in")
        big_grad("w_in", l, gw_in)
        if dist:
            small.setdefault("b_in", [None] * DEPTH)[l] = g["b_in"].reshape(-1)
        else:
            for k, v in g.items():
                small.setdefault(k, [None] * DEPTH)[l] = v.reshape(p[k].shape[1:])
        dx = _hosted(_mm, grad_rider(l, "dxs"), land_grads, dh, big_w["w_in"][l], trans_b=True, res=dr1,
                     res_scale=ALPHA, name=n + "dxs")

    keys, rd = grad_rider(0, "ln_in_bwd")
    if rd is None:
        grad_x, dgi, dbi, _ = _ln_bwd(x, dx, _row(p["ln_in_g"]), name="ln_in_bwd")
    else:
        (grad_x, dgi, dbi, _), routs = _ln_bwd(x, dx, _row(p["ln_in_g"]), name="ln_in_bwd", rider=rd)
        land_grads(keys, routs)
    out = {k: jnp.stack(v, axis=0) for k, v in small.items()}
    out["ln_in_g"], out["ln_in_b"] = dgi.reshape(-1), dbi.reshape(-1)
    return loss, grad_x, out, ((recv, ready) if dist else big_g)


def kernel(x, mem, ln_in_g, ln_in_b, w_in, b_in, s5_lam_re, s5_lam_im, s5_log_dt, s5_b_re, s5_b_im, s5_c_re, s5_c_im, s5_d, s5_w_glu, s5_b_glu, cv_w, cv_b, cv_gn_g, cv_gn_b, cv_w_pw, cv_b_pw, lru_conv_w, lru_conv_b, lru_w_r, lru_b_r, lru_w_i, lru_b_i, lru_lam, attn_w_kv, w_out, b_out, ln1_g, ln1_b, ffn_w_up, ffn_conv_w, ffn_conv_b, ffn_w_down, ln2_g, ln2_b, loss_target, m_ln_in_g, m_ln_in_b, m_w_in, m_b_in, m_s5_lam_re, m_s5_lam_im, m_s5_log_dt, m_s5_b_re, m_s5_b_im, m_s5_c_re, m_s5_c_im, m_s5_d, m_s5_w_glu, m_s5_b_glu, m_cv_w, m_cv_b, m_cv_gn_g, m_cv_gn_b, m_cv_w_pw, m_cv_b_pw, m_lru_conv_w, m_lru_conv_b, m_lru_w_r, m_lru_b_r, m_lru_w_i, m_lru_b_i, m_lru_lam, m_attn_w_kv, m_w_out, m_b_out, m_ln1_g, m_ln1_b, m_ffn_w_up, m_ffn_conv_w, m_ffn_conv_b, m_ffn_w_down, m_ln2_g, m_ln2_b, v_ln_in_g, v_ln_in_b, v_w_in, v_b_in, v_s5_lam_re, v_s5_lam_im, v_s5_log_dt, v_s5_b_re, v_s5_b_im, v_s5_c_re, v_s5_c_im, v_s5_d, v_s5_w_glu, v_s5_b_glu, v_cv_w, v_cv_b, v_cv_gn_g, v_cv_gn_b, v_cv_w_pw, v_cv_b_pw, v_lru_conv_w, v_lru_conv_b, v_lru_w_r, v_lru_b_r, v_lru_w_i, v_lru_b_i, v_lru_lam, v_attn_w_kv, v_w_out, v_b_out, v_ln1_g, v_ln1_b, v_ffn_w_up, v_ffn_conv_w, v_ffn_conv_b, v_ffn_w_down, v_ln2_g, v_ln2_b):
    args = locals()
    w = {n: args[n] for n in WEIGHTS}
    mom = {n: args["m_" + n] for n in WEIGHTS}
    var = {n: args["v_" + n] for n in WEIGHTS}

    shards = {n: w[n].astype(BF16) for n in BIG}
    half_rows = shards["ffn_w_up"].shape[1] // 2
    shards["ffn_w_up#a"] = [shards["ffn_w_up"][0, :half_rows]]
    shards["ffn_w_up#b"] = [shards["ffn_w_up"][0, half_rows:]]
    shards["small_pack"] = [_pack([w[n] for n in SMALL_SHARDED], F32)]
    small_shapes = [w[n].shape for n in SMALL_SHARDED]

    def unpack_small(gathered):
        out = {n: _join_shards(st, SHARDED[n]) for n, st in zip(SMALL_SHARDED, _unpack(gathered, small_shapes, lead=True))}
        for n in ("s5_w_glu", "cv_w_pw"):
            out[n] = out[n].astype(BF16)
        return out

    big_w = {n: [None] * DEPTH for n in BIG}
    p = {n: w[n] for n in REPLICATED}
    p["b_in"] = _perm_in_cols(p["b_in"])

    loss, grad_x, g_small, (recv, ready) = _local_step(x[0], mem[0], loss_target[0], p, big_w, shards, unpack_small)
    loss = lax.psum(loss, ("x", "y", "c"))

    g_small["b_in"] = _perm_in_cols(g_small["b_in"], inverse=True)
    left = list(ready)
    rider = _Rider([ready[k][0] for k in left] + [_pack([g_small[n] for n in REP_LAST], F32)],
                   [ready[k][1] for k in left] + ["all"])
    got = _exchange(rider, name="exchange_grads")
    for k, r in zip(left, got):
        recv[k] = r

    res = [dict(), dict(), dict(), dict()]
    for n in BIG:
        outs = None
        for l in range(DEPTH):
            outs = _adamw_layer(recv[(n, l)], w[n], mom[n], var[n], l, outs, name=f"adamw_{n}_l{l}")
        for kind in range(4):
            res[kind][n] = outs[kind]
    for names, key, tag in ((SMALL_SHARDED, "ssh", "adamw_small_sharded"), (REP_LAYERED, "rep", "adamw_replicated")):
        gstack = jnp.concatenate([recv[(key, l)] for l in range(DEPTH)], axis=1)
        packs = [_pack_layers([t[n] for n in names], F32) for t in (w, mom, var)]
        rows = packs[0].shape[1]
        outs = _adamw(gstack, *[pk.reshape(DEPTH * rows, PACK_COLS) for pk in packs], name=tag)
        for kind in range(4):
            for n, a in zip(names, _unpack_layers(outs[kind].reshape(DEPTH, rows, PACK_COLS), [w[n].shape for n in names])):
                res[kind][n] = a
    outs = _adamw(got[len(left)], _pack([w[n] for n in REP_LAST], F32), _pack([mom[n] for n in REP_LAST], F32),
                  _pack([var[n] for n in REP_LAST], F32), name="adamw_last")
    for kind in range(4):
        for n, a in zip(REP_LAST, _unpack(outs[kind], [w[n].shape for n in REP_LAST])):
            res[kind][n] = a
    return (loss, grad_x[None], *[res[0][n] for n in WEIGHTS], *[res[1][n] for n in WEIGHTS],
            *[res[2][n] for n in WEIGHTS], *[res[3][n] for n in WEIGHTS])
```

```python
import math

import jax
import jax.numpy as jnp
from jax import lax
from jax.experimental import pallas as pl
from jax.experimental.pallas import tpu as pltpu

F32 = jnp.float32
BF16 = jnp.bfloat16

D_MODEL = 1024
DEPTH = 2
D_GROUP = 256
N_IN_COLS = 6 * D_GROUP
S5_GROUPS = 16
S5_CH = 16
S5_STATE = 64
CONV_WIDTH = 31
GN_GROUPS = 4
LRU_HEADS = 4
LRU_CONV_WIDTH = 4
LRU_C = 8.0
ATTN_HEADS = 4
ATTN_HEAD_DIM = 64
D_FF = 2816
FFN_CONV_WIDTH = 3
ALPHA = (2 * DEPTH) ** 0.25
LN_EPS = 1e-5
ADAM_LR, ADAM_B1, ADAM_B2, ADAM_EPS, ADAM_WD, ADAM_STEP = 0.001, 0.9, 0.999, 1e-08, 0.01, 10

N_DEV = 8
N_PEERS = N_DEV - 1
LANE = 128
SUBLANE = 8
VMEM_LIMIT = 56 * 1024 * 1024
PACK_COLS = 1024
PACK_ROW_BLOCK = 256
MM_ROW_TILE = 1024
MM_COL_CAP = 1408
MM_K_CAP = 1536
SEQ_TILE = 512

SHARDED = {
    "w_in": 2, "s5_w_glu": 1, "cv_w": 2, "cv_w_pw": 1, "lru_conv_w": 2, "attn_w_kv": 1,
    "w_out": 1, "ffn_w_up": 2, "ffn_conv_w": 2, "ffn_w_down": 1,
}
BIG = ("w_in", "attn_w_kv", "w_out", "ffn_w_up", "ffn_w_down")
SMALL_SHARDED = ("s5_w_glu", "cv_w", "cv_w_pw", "lru_conv_w", "ffn_conv_w")
WEIGHTS = ['ln_in_g', 'ln_in_b', 'w_in', 'b_in', 's5_lam_re', 's5_lam_im', 's5_log_dt', 's5_b_re', 's5_b_im',
           's5_c_re', 's5_c_im', 's5_d', 's5_w_glu', 's5_b_glu', 'cv_w', 'cv_b', 'cv_gn_g', 'cv_gn_b', 'cv_w_pw',
           'cv_b_pw', 'lru_conv_w', 'lru_conv_b', 'lru_w_r', 'lru_b_r', 'lru_w_i', 'lru_b_i', 'lru_lam',
           'attn_w_kv', 'w_out', 'b_out', 'ln1_g', 'ln1_b', 'ffn_w_up', 'ffn_conv_w', 'ffn_conv_b', 'ffn_w_down',
           'ln2_g', 'ln2_b']
REPLICATED = [n for n in WEIGHTS if n not in SHARDED]
REP_LAST = ("ln_in_g", "ln_in_b", "b_in")
REP_LAYERED = [n for n in REPLICATED if n not in REP_LAST]

COL_CV_V, COL_CV_G, COL_LRU_G, COL_LRU_X, COL_S5, COL_Q = range(6)
IN_PERM = (1, 2, 3, 4, 0, 5)
MIX_S5, MIX_CV, MIX_LRU, MIX_ATTN = range(4)


_ANY = pl.BlockSpec(memory_space=pl.ANY)
_MESH = pl.DeviceIdType.MESH


def _cparams(n_axes):
    return pltpu.CompilerParams(dimension_semantics=("arbitrary",) * n_axes, vmem_limit_bytes=VMEM_LIMIT)


def _pick(n, cap):
    if n <= cap:
        return n
    best = None
    for t in range(LANE, cap + 1, LANE):
        if n % t == 0:
            best = t
    assert best is not None, (n, cap)
    return best


def _pick_rows(n, cap):
    best = None
    for t in range(SUBLANE, min(n, cap) + 1, SUBLANE):
        if n % t == 0:
            best = t
    assert best is not None, (n, cap)
    return best


def _full_spec(arr):
    nd = arr.ndim
    return pl.BlockSpec(arr.shape, lambda *_: (0,) * nd)


def _dot(a, b):
    return lax.dot_general(a.astype(BF16), b.astype(BF16), (((1,), (0,)), ((), ())), preferred_element_type=F32)


def _dot_nt(a, b):
    return lax.dot_general(a.astype(BF16), b.astype(BF16), (((1,), (1,)), ((), ())), preferred_element_type=F32)


def _dot_tn(a, b):
    return lax.dot_general(a.astype(BF16), b.astype(BF16), (((0,), (0,)), ((), ())), preferred_element_type=F32)


def _dot_hi(a, b):
    b = b.astype(BF16)
    a1 = a.astype(BF16)
    r1 = a - a1.astype(F32)
    a2 = r1.astype(BF16)
    a3 = (r1 - a2.astype(F32)).astype(BF16)
    return _dot(a1, b) + _dot(a2, b) + _dot(a3, b)


def _colsum(x):
    return jnp.sum(x, axis=0, keepdims=True)


def _sigmoid(x):
    return 1.0 / (1.0 + jnp.exp(-x))


_GELU_K = math.sqrt(2.0 / math.pi)
_GELU_C = 0.044715


def _gelu(x):
    t = jnp.tanh(_GELU_K * (x + _GELU_C * x * x * x))
    return 0.5 * x * (1.0 + t)


def _gelu_and_grad(x):
    x2 = x * x
    t = jnp.tanh(_GELU_K * (x + _GELU_C * x2 * x))
    g = 0.5 * x * (1.0 + t)
    dg = 0.5 * (1.0 + t) + 0.5 * x * (1.0 - t * t) * (_GELU_K * (1.0 + 3.0 * _GELU_C * x2))
    return g, dg


def _neg_expm1(x):
    series = x * (1.0 + x * (0.5 + x * (1.0 / 6.0 + x * (1.0 / 24.0 + x * (1.0 / 120.0)))))
    return -jnp.where(jnp.abs(x) < 0.1, series, jnp.exp(x) - 1.0)


def _seq_tile(s, want):
    t = min(s, want)
    assert s % t == 0
    return t


class _Rider:
    def __init__(self, srcs, kinds):
        self.srcs, self.kinds = list(srcs), list(kinds)
        self.n = len(self.srcs)

    def out_shapes(self):
        shapes = []
        for x, kind in zip(self.srcs, self.kinds):
            if kind == "lead":
                shp = x.shape
            elif kind == "rows":
                shp = (N_DEV, x.shape[0] // N_DEV) + x.shape[1:]
            else:
                shp = (N_DEV,) + x.shape
            shapes.append(jax.ShapeDtypeStruct(shp, x.dtype))
        return shapes

    def scratch(self):
        return [pltpu.SemaphoreType.DMA((self.n * N_PEERS,)), pltpu.SemaphoreType.DMA((self.n * N_PEERS,)),
                pltpu.SemaphoreType.DMA((self.n,))]

    def _copies(self, x_refs, out_refs, sems):
        send_sems, recv_sems, local_sems = sems
        mx, my, mc = lax.axis_index("x"), lax.axis_index("y"), lax.axis_index("c")
        my_id = 4 * mx + 2 * my + mc

        def piece(i, dev):
            if self.kinds[i] == "lead":
                return x_refs[i].at[dev]
            if self.kinds[i] == "rows":
                r = x_refs[i].shape[0] // N_DEV
                return x_refs[i].at[pl.ds(pl.multiple_of(dev * r, SUBLANE), r)]
            return x_refs[i]

        mine = [pltpu.make_async_copy(piece(i, my_id), out_refs[i].at[my_id], local_sems.at[i]) for i in range(self.n)]
        copies = []
        for k in range(1, N_DEV):
            px, py, pc = mx ^ ((k >> 2) & 1), my ^ ((k >> 1) & 1), mc ^ (k & 1)
            for i in range(self.n):
                copies.append(pltpu.make_async_remote_copy(
                    src_ref=piece(i, 4 * px + 2 * py + pc), dst_ref=out_refs[i].at[my_id],
                    send_sem=send_sems.at[i * N_PEERS + k - 1], recv_sem=recv_sems.at[i * N_PEERS + k - 1],
                    device_id=(px, py, pc), device_id_type=_MESH))
        return mine, copies

    def start(self, x_refs, out_refs, sems):
        mine, copies = self._copies(x_refs, out_refs, sems)
        for cp in mine + copies:
            cp.start()

    def wait(self, x_refs, out_refs, sems):
        mine, copies = self._copies(x_refs, out_refs, sems)
        for cp in copies:
            cp.wait_recv()
        for cp in copies:
            cp.wait_send()
        for cp in mine:
            cp.wait()


def _call(body, *, grid, ins, in_specs, outs, out_specs, scratch=(), aliases=None, name, rider=None):
    n_axes = len(grid)
    common = dict(grid=grid, input_output_aliases=aliases or {}, compiler_params=_cparams(n_axes), name=name)
    if rider is None:
        res = pl.pallas_call(body, in_specs=list(in_specs), out_specs=list(out_specs), out_shape=list(outs),
                             scratch_shapes=list(scratch), **common)(*ins)
        return list(res), []
    n_in, n_out, n_scr, nr = len(ins), len(outs), len(scratch), rider.n

    def wrapped(*refs):
        pos = [0]

        def take(k):
            part = refs[pos[0]:pos[0] + k]
            pos[0] += k
            return part

        a_in, r_in, a_out, r_out, a_scr, sems = take(n_in), take(nr), take(n_out), take(nr), take(n_scr), take(3)
        first = last = None
        for ax in range(n_axes):
            pid = pl.program_id(ax)
            f, l = pid == 0, pid == grid[ax] - 1
            first = f if first is None else jnp.logical_and(first, f)
            last = l if last is None else jnp.logical_and(last, l)

        @pl.when(first)
        def _():
            rider.start(r_in, r_out, sems)

        body(*a_in, *a_out, *a_scr)

        @pl.when(last)
        def _():
            rider.wait(r_in, r_out, sems)

    res = pl.pallas_call(
        wrapped, in_specs=list(in_specs) + [_ANY] * nr, out_specs=list(out_specs) + [_ANY] * nr,
        out_shape=list(outs) + rider.out_shapes(), scratch_shapes=list(scratch) + rider.scratch(), **common)(*ins, *rider.srcs)
    return list(res[:n_out]), list(res[n_out:])


def _block_mask(n_blocks, block_rows, block_cols):
    r = jnp.arange(n_blocks * block_rows) // block_rows
    c = jnp.arange(n_blocks * block_cols) // block_cols
    return (r[:, None] == c[None, :]).astype(F32)


def _mm(a, b, *, bias=None, res=None, res_scale=1.0, trans_b=False, out_dtype=F32, ln=None, ln_bwd=None, name,
        rider=None):
    m, kdim = a.shape
    n = b.shape[0] if trans_b else b.shape[1]
    tm = _seq_tile(m, MM_ROW_TILE)
    tn = _pick(n, MM_COL_CAP)
    tk = _pick(kdim, MM_K_CAP)
    nk = kdim // tk
    has_bias, has_res, has_ln, has_lnb = bias is not None, res is not None, ln is not None, ln_bwd is not None
    assert not (has_ln or has_lnb) or tn == n
    assert not (has_ln and has_lnb)

    def body(*refs):
        a_ref, b_ref = refs[0], refs[1]
        pos = 2
        bias_ref = res_ref = g_ref = beta_ref = x_ref = None
        if has_bias:
            bias_ref = refs[pos]
            pos += 1
        if has_res:
            res_ref = refs[pos]
            pos += 1
        if has_ln:
            g_ref, beta_ref = refs[pos], refs[pos + 1]
            pos += 2
        if has_lnb:
            x_ref, g_ref = refs[pos], refs[pos + 1]
            pos += 2
        o_ref = refs[pos]
        pos += 1
        if has_ln:
            x_ref = refs[pos]
            pos += 1
        if has_lnb:
            dg_ref, db_ref, ds_ref = refs[pos:pos + 3]
            pos += 3
        acc_ref = refs[pos]
        k = pl.program_id(2)

        @pl.when(k == 0)
        def _():
            acc_ref[...] = jnp.zeros_like(acc_ref)

        if has_lnb:
            @pl.when(jnp.logical_and(pl.program_id(0) == 0, k == 0))
            def _():
                dg_ref[...] = jnp.zeros_like(dg_ref)
                db_ref[...] = jnp.zeros_like(db_ref)
                ds_ref[...] = jnp.zeros_like(ds_ref)

        if trans_b:
            acc_ref[...] += _dot_nt(a_ref[...], b_ref[...])
        else:
            acc_ref[...] += _dot(a_ref[...], b_ref[...])

        @pl.when(k == nk - 1)
        def _():
            r = acc_ref[...]
            if has_bias:
                r = r + bias_ref[...]
            if has_res:
                r = r + res_scale * res_ref[...]
            if has_lnb:
                x = x_ref[...]
                xc = x - jnp.mean(x, axis=1, keepdims=True)
                rstd = lax.rsqrt(jnp.mean(xc * xc, axis=1, keepdims=True) + LN_EPS)
                xh = xc * rstd
                dxh = r * g_ref[...]
                dx = rstd * (dxh - jnp.mean(dxh, axis=1, keepdims=True) - xh * jnp.mean(dxh * xh, axis=1, keepdims=True))
                o_ref[...] = dx
                dg_ref[...] += _colsum(r * xh)
                db_ref[...] += _colsum(r)
                ds_ref[...] += _colsum(dx)
            else:
                o_ref[...] = r.astype(out_dtype)
            if has_ln:
                xc = r - jnp.mean(r, axis=1, keepdims=True)
                var = jnp.mean(xc * xc, axis=1, keepdims=True)
                x_ref[...] = xc * lax.rsqrt(var + LN_EPS) * g_ref[...] + beta_ref[...]

    ins = [a, b]
    in_specs = [pl.BlockSpec((tm, tk), lambda i, j, k: (i, k)),
                pl.BlockSpec((tn, tk), lambda i, j, k: (j, k)) if trans_b
                else pl.BlockSpec((tk, tn), lambda i, j, k: (k, j))]
    if has_bias:
        ins.append(bias)
        in_specs.append(pl.BlockSpec((1, tn), lambda i, j, k: (0, j)))
    if has_res:
        ins.append(res)
        in_specs.append(pl.BlockSpec((tm, tn), lambda i, j, k: (i, j)))
    tile = pl.BlockSpec((tm, tn), lambda i, j, k: (i, j))
    vec = pl.BlockSpec((1, tn), lambda i, j, k: (0, j))
    out_shapes, out_specs = [jax.ShapeDtypeStruct((m, n), out_dtype)], [tile]
    if has_ln:
        ins += list(ln)
        in_specs += [vec] * 2
        out_shapes.append(jax.ShapeDtypeStruct((m, n), F32))
        out_specs.append(tile)
    if has_lnb:
        ins += list(ln_bwd)
        in_specs += [tile, vec]
        out_shapes += [jax.ShapeDtypeStruct((1, n), F32)] * 3
        out_specs += [vec] * 3
    outs, routs = _call(
        body, grid=(m // tm, n // tn, nk), ins=ins, in_specs=in_specs, outs=out_shapes, out_specs=out_specs,
        scratch=[pltpu.VMEM((tm, tn), F32)], name=name, rider=rider)
    out = tuple(outs) if (has_ln or has_lnb) else outs[0]
    return out if rider is None else (out, routs)


def _mm_tn(a, b, *, colsum=False, out_dtype=F32, dev_cols=None, name, rider=None):
    s, ka = a.shape
    nb = b.shape[1]
    ts = _seq_tile(s, SEQ_TILE)
    tka = _pick(ka, MM_COL_CAP)
    tnb = _pick(nb, MM_COL_CAP)
    nk = s // ts
    assert not colsum or tka == ka
    per_tile = 1 if dev_cols is None else tnb // dev_cols
    assert dev_cols is None or tnb == per_tile * dev_cols

    def body(a_ref, b_ref, o_ref, *rest):
        cs_ref = rest[0] if colsum else None
        acc_ref = rest[-1]
        k = pl.program_id(2)

        @pl.when(k == 0)
        def _():
            acc_ref[...] = jnp.zeros_like(acc_ref)
            if colsum:
                cs_ref[...] = jnp.zeros_like(cs_ref)

        bv = b_ref[...]
        acc_ref[...] += _dot_tn(a_ref[...], bv)
        if colsum:
            cs_ref[...] += _colsum(bv.astype(F32))

        @pl.when(k == nk - 1)
        def _():
            if dev_cols is None:
                o_ref[...] = acc_ref[...].astype(out_dtype)
            else:
                for d in range(per_tile):
                    o_ref[d] = acc_ref[:, d * dev_cols:(d + 1) * dev_cols].astype(out_dtype)

    if dev_cols is None:
        main_shape, main_spec = (ka, nb), pl.BlockSpec((tka, tnb), lambda i, j, k: (i, j))
    else:
        main_shape = (nb // dev_cols, ka, dev_cols)
        main_spec = pl.BlockSpec((per_tile, tka, dev_cols), lambda i, j, k: (j, i, 0))
    outs, routs = _call(
        body, grid=(ka // tka, nb // tnb, nk), ins=[a, b],
        in_specs=[pl.BlockSpec((ts, tka), lambda i, j, k: (k, i)), pl.BlockSpec((ts, tnb), lambda i, j, k: (k, j))],
        outs=[jax.ShapeDtypeStruct(main_shape, out_dtype)] + ([jax.ShapeDtypeStruct((1, nb), F32)] if colsum else []),
        out_specs=[main_spec] + ([pl.BlockSpec((1, tnb), lambda i, j, k: (0, j))] if colsum else []),
        scratch=[pltpu.VMEM((tka, tnb), F32)], name=name, rider=rider)
    out = tuple(outs) if colsum else outs[0]
    return out if rider is None else (out, routs)


def _ln_fwd(r, g, b, *, name, rider=None):
    s, d = r.shape
    ts = _seq_tile(s, SEQ_TILE)

    def body(r_ref, g_ref, b_ref, o_ref):
        x = r_ref[...]
        mu = jnp.mean(x, axis=1, keepdims=True)
        xc = x - mu
        var = jnp.mean(xc * xc, axis=1, keepdims=True)
        o_ref[...] = xc * lax.rsqrt(var + LN_EPS) * g_ref[...] + b_ref[...]

    (out,), routs = _call(
        body, grid=(s // ts,), ins=[r, g, b],
        in_specs=[pl.BlockSpec((ts, d), lambda i: (i, 0)), _full_spec(g), _full_spec(b)],
        out_specs=[pl.BlockSpec((ts, d), lambda i: (i, 0))], outs=[jax.ShapeDtypeStruct((s, d), F32)],
        name=name, rider=rider)
    return out if rider is None else (out, routs)


def _ln_bwd(r, dy, g, *, name, rider=None):
    s, d = r.shape
    ts = _seq_tile(s, SEQ_TILE)

    def body(r_ref, dy_ref, g_ref, dr_ref, dg_ref, db_ref, ds_ref):
        @pl.when(pl.program_id(0) == 0)
        def _():
            dg_ref[...] = jnp.zeros_like(dg_ref)
            db_ref[...] = jnp.zeros_like(db_ref)
            ds_ref[...] = jnp.zeros_like(ds_ref)

        x = r_ref[...]
        dy = dy_ref[...]
        mu = jnp.mean(x, axis=1, keepdims=True)
        xc = x - mu
        var = jnp.mean(xc * xc, axis=1, keepdims=True)
        rstd = lax.rsqrt(var + LN_EPS)
        xh = xc * rstd
        dxh = dy * g_ref[...]
        m1 = jnp.mean(dxh, axis=1, keepdims=True)
        m2 = jnp.mean(dxh * xh, axis=1, keepdims=True)
        dr = rstd * (dxh - m1 - xh * m2)
        dr_ref[...] = dr
        dg_ref[...] += _colsum(dy * xh)
        db_ref[...] += _colsum(dy)
        ds_ref[...] += _colsum(dr)

    vec = jax.ShapeDtypeStruct((1, d), F32)
    vspec = pl.BlockSpec((1, d), lambda i: (0, 0))
    outs, routs = _call(
        body, grid=(s // ts,), ins=[r, dy, g],
        in_specs=[pl.BlockSpec((ts, d), lambda i: (i, 0)), pl.BlockSpec((ts, d), lambda i: (i, 0)), _full_spec(g)],
        out_specs=[pl.BlockSpec((ts, d), lambda i: (i, 0)), vspec, vspec, vspec],
        outs=[jax.ShapeDtypeStruct((s, d), F32), vec, vec, vec], name=name, rider=rider)
    return outs if rider is None else (outs, routs)


def _loss_ln_bwd(r, g, b, target, *, name):
    s, d = r.shape
    ts = _seq_tile(s, SEQ_TILE)

    def body(r_ref, g_ref, b_ref, t_ref, dr_ref, dg_ref, db_ref, l_ref):
        @pl.when(pl.program_id(0) == 0)
        def _():
            dg_ref[...] = jnp.zeros_like(dg_ref)
            db_ref[...] = jnp.zeros_like(db_ref)
            l_ref[...] = jnp.zeros_like(l_ref)

        x = r_ref[...]
        gam = g_ref[...]
        xc = x - jnp.mean(x, axis=1, keepdims=True)
        var = jnp.mean(xc * xc, axis=1, keepdims=True)
        rstd = lax.rsqrt(var + LN_EPS)
        xh = xc * rstd
        e = xh * gam + b_ref[...] - t_ref[...]
        part = jnp.sum(jnp.sum(e * e, axis=1, keepdims=True), axis=0, keepdims=True) * (0.5 / d)
        l_ref[...] += jnp.broadcast_to(part, l_ref.shape)
        dy = e * (1.0 / d)
        dxh = dy * gam
        m1 = jnp.mean(dxh, axis=1, keepdims=True)
        m2 = jnp.mean(dxh * xh, axis=1, keepdims=True)
        dr_ref[...] = rstd * (dxh - m1 - xh * m2)
        dg_ref[...] += _colsum(dy * xh)
        db_ref[...] += _colsum(dy)

    vec = jax.ShapeDtypeStruct((1, d), F32)
    vspec = pl.BlockSpec((1, d), lambda i: (0, 0))
    tile = pl.BlockSpec((ts, d), lambda i: (i, 0))
    return pl.pallas_call(
        body, grid=(s // ts,), in_specs=[tile, _full_spec(g), _full_spec(b), tile],
        out_specs=[tile, vspec, vspec, pl.BlockSpec((SUBLANE, LANE), lambda i: (0, 0))],
        out_shape=[jax.ShapeDtypeStruct((s, d), F32), vec, vec, jax.ShapeDtypeStruct((SUBLANE, LANE), F32)],
        compiler_params=_cparams(1), name=name)(r, g, b, target)


SCAN_CHUNK = 32


def _cscan_levels(bufs, apow_ref, t, pad, *, reverse):
    half = bufs[0].shape[1] // 2
    ch = min(SCAN_CHUNK, t)
    nlev = t.bit_length() - 1
    assert (1 << nlev) == t
    for k in range(nlev):
        d = 1 << k
        src, dst = bufs[k % 2], bufs[(k + 1) % 2]

        def chunk(c, carry, src=src, dst=dst, d=d, k=k):
            ar = apow_ref[k:k + 1, :half]
            ai = apow_ref[k:k + 1, half:]
            if reverse:
                ai = -ai
            r0 = pl.multiple_of(c * ch, ch)
            cur = src[pl.ds(pad + r0, ch), :]
            if d >= SUBLANE:
                off = pad + d if reverse else pad - d
                sh = src[pl.ds(off + r0, ch), :]
            elif reverse:
                blk = src[pl.ds(pad + r0, ch + SUBLANE), :]
                sh = pltpu.roll(blk, ch + SUBLANE - d, axis=0)[:ch, :]
            else:
                blk = src[pl.ds(pad - SUBLANE + r0, ch + SUBLANE), :]
                sh = pltpu.roll(blk, d, axis=0)[SUBLANE:, :]
            sre, sim = sh[:, :half], sh[:, half:]
            dst[pl.ds(pad + r0, ch), :half] = cur[:, :half] + ar * sre - ai * sim
            dst[pl.ds(pad + r0, ch), half:] = cur[:, half:] + ar * sim + ai * sre
            return carry

        lax.fori_loop(0, t // ch, chunk, 0)
    return nlev % 2


def _rscan_levels(abufs, bbufs, t, pad, *, reverse):
    nlev = t.bit_length() - 1
    assert (1 << nlev) == t
    for k in range(nlev):
        d = 1 << k
        asrc, adst = abufs[k % 2], abufs[(k + 1) % 2]
        bsrc, bdst = bbufs[k % 2], bbufs[(k + 1) % 2]
        off = pad + d if reverse else pad - d
        a = asrc[pad:pad + t, :]
        bdst[pad:pad + t, :] = a * bsrc[off:off + t, :] + bsrc[pad:pad + t, :]
        if k < nlev - 1:
            adst[pad:pad + t, :] = a * asrc[off:off + t, :]
    return nlev % 2


S5_CHUNK = 16
S5_SG = S5_GROUPS // 2
S5_SG_IN = 2 * S5_CHUNK * S5_CH
S5_SG_ST = 2 * S5_STATE


S5_HALF_SGS = S5_SG // 2
S5_HALF_IN = S5_HALF_SGS * S5_SG_IN


def _s5_perm():
    idx = jnp.arange(S5_HALF_IN)
    step, grp, chan = idx // LANE, (idx % LANE) // S5_CH, idx % S5_CH
    col = (grp // 2) * S5_SG_IN + (grp % 2) * (S5_CHUNK * S5_CH) + step * S5_CH + chan
    return (col[:, None] == idx[None, :]).astype(BF16)


def _s5_to_chunks(x, col_block, perm, *, name):
    s = x.shape[0]
    nb = s // S5_CHUNK

    def body(x_ref, perm_ref, o_ref):
        tok = jnp.concatenate([x_ref[pl.ds(t, nb, stride=S5_CHUNK), :].astype(BF16) for t in range(S5_CHUNK)], axis=1)
        grouped = _dot(tok, perm_ref[...]).astype(BF16)
        for k in range(S5_HALF_SGS):
            o_ref[k] = grouped[:, k * S5_SG_IN:(k + 1) * S5_SG_IN]

    return pl.pallas_call(
        body, grid=(2,),
        in_specs=[pl.BlockSpec((s, LANE), lambda h: (0, col_block + h)), _full_spec(perm)],
        out_specs=pl.BlockSpec((S5_HALF_SGS, nb, S5_SG_IN), lambda h: (h, 0, 0)),
        out_shape=jax.ShapeDtypeStruct((S5_SG, nb, S5_SG_IN), BF16),
        compiler_params=_cparams(1), name=name)(x, perm)


def _s5_from_chunks(y, perm, *, name):
    _, nb, _ = y.shape

    def body(y_ref, perm_ref, o_ref):
        grouped = jnp.concatenate([y_ref[k] for k in range(S5_HALF_SGS)], axis=1)
        hi = grouped.astype(BF16)
        lo = (grouped - hi.astype(F32)).astype(BF16)
        tok = _dot_nt(hi, perm_ref[...]) + _dot_nt(lo, perm_ref[...])
        for t in range(S5_CHUNK):
            o_ref[pl.ds(t, nb, stride=S5_CHUNK), :] = tok[:, t * LANE:(t + 1) * LANE]

    return pl.pallas_call(
        body, grid=(2,),
        in_specs=[pl.BlockSpec((S5_HALF_SGS, nb, S5_SG_IN), lambda h: (h, 0, 0)), _full_spec(perm)],
        out_specs=pl.BlockSpec((nb * S5_CHUNK, LANE), lambda h: (0, h)),
        out_shape=jax.ShapeDtypeStruct((nb * S5_CHUNK, D_GROUP), F32),
        compiler_params=_cparams(1), name=name)(y, perm)


def _s5_core_fwd(u2, m2, pre, pim, qre, qim, a16, *, name):
    sg, nb, nin = u2.shape
    st2 = 2 * S5_SG_ST
    pad = nb // 2

    def body(u_ref, m_ref, pre_ref, pim_ref, qre_ref, qim_ref, a_ref, y_ref, x_ref, buf0, buf1):
        @pl.when(pl.program_id(0) == 0)
        def _():
            buf0[0:pad, :] = jnp.zeros((pad, st2), F32)
            buf1[0:pad, :] = jnp.zeros((pad, st2), F32)

        u = u_ref[...]
        buf0[pad:pad + nb, :S5_SG_ST] = _dot(u, pre_ref[...])
        buf0[pad:pad + nb, S5_SG_ST:] = _dot(u, pim_ref[...])
        xbuf = (buf0, buf1)[_cscan_levels((buf0, buf1), a_ref, nb, pad, reverse=False)]
        x_ref[...] = xbuf[pad:pad + nb, :]
        xprev = xbuf[pad - 1:pad - 1 + nb, :]
        y_ref[...] = _dot(u, m_ref[...]) + _dot(xprev[:, :S5_SG_ST], qre_ref[...]) + _dot(xprev[:, S5_SG_ST:], qim_ref[...])

    ins = [u2, m2, pre, pim, qre, qim, a16]
    return pl.pallas_call(
        body, grid=(sg,), in_specs=[pl.BlockSpec((None,) + a.shape[1:], lambda i: (i, 0, 0)) for a in ins],
        out_specs=[pl.BlockSpec((None, nb, nin), lambda i: (i, 0, 0)), pl.BlockSpec((None, nb, st2), lambda i: (i, 0, 0))],
        out_shape=[jax.ShapeDtypeStruct((sg, nb, nin), F32), jax.ShapeDtypeStruct((sg, nb, st2), F32)],
        scratch_shapes=[pltpu.VMEM((pad + nb, st2), F32), pltpu.VMEM((pad + nb, st2), F32)],
        compiler_params=_cparams(1), name=name)(*ins)


def _s5_core_bwd(u2, dy2, x_all, m2, pre, pim, qre, qim, a16, *, name):
    sg, nb, nin = u2.shape
    half = S5_SG_ST
    st2 = 2 * half
    pad = nb // 2

    def body(u_ref, dy_ref, x_ref, m_ref, pre_ref, pim_ref, qre_ref, qim_ref, a_ref,
             du_ref, dm_ref, dpre_ref, dpim_ref, dqre_ref, dqim_ref, da_ref, buf2, buf3, xp):
        @pl.when(pl.program_id(0) == 0)
        def _():
            buf2[nb:nb + pad, :] = jnp.zeros((pad, st2), F32)
            buf3[nb:nb + pad, :] = jnp.zeros((pad, st2), F32)
            xp[0:SUBLANE, :] = jnp.zeros((SUBLANE, st2), F32)

        u = u_ref[...]
        dy = dy_ref[...]
        dm_ref[...] = _dot_tn(u, dy)
        xp[SUBLANE:SUBLANE + nb, :] = x_ref[...]
        xprev = xp[SUBLANE - 1:SUBLANE - 1 + nb, :]
        xre, xim = xprev[:, :half], xprev[:, half:]
        dqre_ref[...] = _dot_tn(xre, dy)
        dqim_ref[...] = _dot_tn(xim, dy)
        buf2[0:nb, :half] = _dot_nt(dy, qre_ref[...])
        buf2[0:nb, half:] = _dot_nt(dy, qim_ref[...])
        mbuf = (buf2, buf3)[_cscan_levels((buf2, buf3), a_ref, nb, 0, reverse=True)]
        lam = mbuf[1:1 + nb, :]
        lre, lim = lam[:, :half], lam[:, half:]
        dpre_ref[...] = _dot_tn(u, lre)
        dpim_ref[...] = _dot_tn(u, lim)
        du_ref[...] = _dot_nt(dy, m_ref[...]) + _dot_nt(lre, pre_ref[...]) + _dot_nt(lim, pim_ref[...])
        da_ref[:, :half] = _colsum(lre * xre + lim * xim)
        da_ref[:, half:] = _colsum(lim * xre - lre * xim)

    ins = [u2, dy2, x_all, m2, pre, pim, qre, qim, a16]
    outs = [jax.ShapeDtypeStruct((sg, nb, nin), F32)] + [jax.ShapeDtypeStruct(a.shape, F32) for a in (m2, pre, pim, qre, qim)] + \
           [jax.ShapeDtypeStruct((sg, 1, st2), F32)]
    return pl.pallas_call(
        body, grid=(sg,), in_specs=[pl.BlockSpec((None,) + a.shape[1:], lambda i: (i, 0, 0)) for a in ins],
        out_specs=[pl.BlockSpec((None,) + o.shape[1:], lambda i: (i, 0, 0)) for o in outs], out_shape=outs,
        scratch_shapes=[pltpu.VMEM((nb + pad, st2), F32), pltpu.VMEM((nb + pad, st2), F32),
                        pltpu.VMEM((SUBLANE + nb, st2), F32)],
        compiler_params=_cparams(1), name=name)(*ins)


def _s5_glu_fwd(y1, wglu, bglu, *, name, rider=None):
    s = y1.shape[0]
    t = _seq_tile(s, SEQ_TILE)

    def body(y1_ref, wglu_ref, bglu_ref, out_ref):
        y2 = _gelu(y1_ref[...])
        out_ref[...] = (y2 * _sigmoid(_dot(y2, wglu_ref[...]) + bglu_ref[...])).astype(BF16)

    return _call(
        body, grid=(s // t,), ins=[y1, wglu, bglu],
        in_specs=[pl.BlockSpec((t, D_GROUP), lambda i: (i, 0)), _full_spec(wglu), _full_spec(bglu)],
        out_specs=[pl.BlockSpec((t, D_GROUP), lambda i: (i, MIX_S5))], outs=[jax.ShapeDtypeStruct((s, D_MODEL), BF16)],
        name=name, rider=rider)


def _s5_glu_bwd(y1, dmix, wglu, bglu, *, name):
    s = y1.shape[0]
    t = _seq_tile(s, SEQ_TILE)

    def body(y1_ref, do_ref, wglu_ref, bglu_ref, dy1_ref, dwglu_ref, dbglu_ref):
        @pl.when(pl.program_id(0) == 0)
        def _():
            dwglu_ref[...] = jnp.zeros_like(dwglu_ref)
            dbglu_ref[...] = jnp.zeros_like(dbglu_ref)

        dout = do_ref[...]
        y2, dgelu = _gelu_and_grad(y1_ref[...])
        sg = _sigmoid(_dot(y2, wglu_ref[...]) + bglu_ref[...])
        dz = dout * y2 * sg * (1.0 - sg)
        dwglu_ref[...] += _dot_tn(y2, dz)
        dbglu_ref[...] += _colsum(dz)
        dy1_ref[...] = (dout * sg + _dot_nt(dz, wglu_ref[...])) * dgelu

    outs = [jax.ShapeDtypeStruct((s, D_GROUP), F32), jax.ShapeDtypeStruct((D_GROUP, D_GROUP), F32),
            jax.ShapeDtypeStruct((1, D_GROUP), F32)]
    return pl.pallas_call(
        body, grid=(s // t,),
        in_specs=[pl.BlockSpec((t, D_GROUP), lambda i: (i, 0)), pl.BlockSpec((t, D_GROUP), lambda i: (i, MIX_S5)),
                  _full_spec(wglu), _full_spec(bglu)],
        out_specs=[pl.BlockSpec((t, D_GROUP), lambda i: (i, 0)), _full_spec(outs[1]), _full_spec(outs[2])],
        out_shape=outs, compiler_params=_cparams(1), name=name)(y1, dmix, wglu, bglu)


def _pair_blockdiag(x):
    g, r, c = x.shape
    x = x.reshape(g // 2, 2, r, c)
    z = jnp.zeros_like(x[:, 0])
    return jnp.concatenate([jnp.concatenate([x[:, 0], z], axis=2), jnp.concatenate([z, x[:, 1]], axis=2)], axis=1)


def _s5_chunk_map(lam_re, lam_im, log_dt, b_re, b_im, c_re, c_im, d_skip):
    g, n, c, lc = S5_GROUPS, S5_STATE, S5_CH, S5_CHUNK
    dt = jnp.exp(log_dt)[:, None]
    mag, ang = lam_re * dt, lam_im * dt
    j = jnp.arange(lc + 1, dtype=F32)[:, None, None]
    pw_mag = jnp.exp(j * mag)
    pw_re, pw_im = pw_mag * jnp.cos(j * ang), pw_mag * jnp.sin(j * ang)
    a_re, a_im = pw_re[1], pw_im[1]
    den = lam_re * lam_re + lam_im * lam_im
    n_re = a_re - 1.0
    k_re = (n_re * lam_re + a_im * lam_im) / den
    k_im = (a_im * lam_re - n_re * lam_im) / den
    bb_re = k_re[..., None] * b_re - k_im[..., None] * b_im
    bb_im = k_re[..., None] * b_im + k_im[..., None] * b_re
    e_re = pw_re[:lc, :, :, None] * bb_re - pw_im[:lc, :, :, None] * bb_im
    e_im = pw_re[:lc, :, :, None] * bb_im + pw_im[:lc, :, :, None] * bb_re
    kern = jnp.einsum("gdn,jgnc->jgdc", c_re, e_re) - jnp.einsum("gdn,jgnc->jgdc", c_im, e_im)
    lags = jnp.pad(jnp.transpose(kern, (1, 3, 0, 2)), ((0, 0), (0, 0), (lc - 1, 0), (0, 0)))
    lags = lags.reshape(g, c, (2 * lc - 1) * c)
    m = jnp.stack([lags[:, :, (lc - 1 - s) * c:(2 * lc - 1 - s) * c] for s in range(lc)], axis=1).reshape(g, lc * c, lc * c)
    skip = jnp.tile(d_skip.reshape(g, 1, c), (1, lc, 1)).reshape(g, lc * c)
    m = m + jnp.eye(lc * c, dtype=F32)[None] * skip[:, None, :]
    p_re = jnp.transpose(e_re[::-1], (1, 0, 3, 2)).reshape(g, lc * c, n)
    p_im = jnp.transpose(e_im[::-1], (1, 0, 3, 2)).reshape(g, lc * c, n)
    f_re = c_re[None] * pw_re[1:, :, None, :] - c_im[None] * pw_im[1:, :, None, :]
    f_im = c_re[None] * pw_im[1:, :, None, :] + c_im[None] * pw_re[1:, :, None, :]
    q_re = jnp.transpose(f_re, (1, 3, 0, 2)).reshape(g, n, lc * c)
    q_im = -jnp.transpose(f_im, (1, 3, 0, 2)).reshape(g, n, lc * c)
    a16 = jnp.concatenate([pw_re[lc].reshape(S5_SG, 1, S5_SG_ST), pw_im[lc].reshape(S5_SG, 1, S5_SG_ST)], axis=2)
    return (_pair_blockdiag(m), _pair_blockdiag(p_re), _pair_blockdiag(p_im), _pair_blockdiag(q_re),
            _pair_blockdiag(q_im), a16)


def _s5_a16_powers(a16, nlev):
    half = S5_SG_ST
    re, im = a16[:, :, :half], a16[:, :, half:]
    rows = []
    for _ in range(nlev):
        rows.append(jnp.concatenate([re, im], axis=2))
        re, im = re * re - im * im, 2.0 * re * im
    n_rows = -(-nlev // SUBLANE) * SUBLANE
    rows += [jnp.zeros_like(rows[0])] * (n_rows - nlev)
    return lax.stop_gradient(jnp.concatenate(rows, axis=1))


CV_TILE = 256
CV_PAD = 32
CV_CHUNK = 64


def _shifted_copies(buf, shifted, rows):
    n = rows - SUBLANE
    for s in range(1, SUBLANE):
        shifted[s - 1, 0:n, :] = buf[s:s + n, :]


def _window(buf, shifted, o, ch):
    q, s = divmod(o, SUBLANE)
    if s == 0:
        return buf[o:o + ch, :]
    return shifted[s - 1, q * SUBLANE:q * SUBLANE + ch, :]


def _gn_stats(c, mavg):
    mu = _dot_hi(c, mavg)
    cen = c - mu
    var = _dot_hi(cen * cen, mavg)
    rstd = lax.rsqrt(var + LN_EPS)
    return cen * rstd, rstd


def _cv_fwd(h_in, cw, cb, gng, gnb, mavg, wpw, bpw, mix, *, name, rider=None):
    s = h_in.shape[0]
    t = _seq_tile(s, CV_TILE)
    ch = min(CV_CHUNK, t)

    def body(v_ref, g_ref, cw_ref, cb_ref, gng_ref, gnb_ref, mavg_ref, wpw_ref, bpw_ref, _mix_in, out_ref, c_ref, xpad,
             shifted):
        @pl.when(pl.program_id(0) == 0)
        def _():
            xpad[0:CV_PAD, :] = jnp.zeros((CV_PAD, D_GROUP), F32)

        xpad[CV_PAD:CV_PAD + t, :] = v_ref[...] * _sigmoid(g_ref[...])
        _shifted_copies(xpad, shifted, t + CV_PAD)
        for r0 in range(0, t, ch):
            acc = jnp.broadcast_to(cb_ref[...], (ch, D_GROUP))
            for k in range(CONV_WIDTH):
                o = CV_PAD - (CONV_WIDTH - 1) + k + r0
                acc = acc + cw_ref[k:k + 1, :] * _window(xpad, shifted, o, ch)
            c_ref[r0:r0 + ch, :] = acc
        xpad[0:CV_PAD, :] = xpad[t:t + CV_PAD, :]
        xn, _ = _gn_stats(c_ref[...], mavg_ref[...])
        gn = xn * gng_ref[...] + gnb_ref[...]
        out_ref[...] = (_dot(gn * _sigmoid(gn), wpw_ref[...]) + bpw_ref[...]).astype(BF16)

    ins = [h_in, h_in, cw, cb, gng, gnb, mavg, wpw, bpw, mix]
    in_specs = [pl.BlockSpec((t, D_GROUP), lambda i: (i, COL_CV_V)), pl.BlockSpec((t, D_GROUP), lambda i: (i, COL_CV_G))] + \
               [_full_spec(a) for a in ins[2:9]] + [_ANY]
    return _call(
        body, grid=(s // t,), ins=ins, in_specs=in_specs,
        out_specs=[pl.BlockSpec((t, D_GROUP), lambda i: (i, MIX_CV)), pl.BlockSpec((t, D_GROUP), lambda i: (i, 0))],
        outs=[jax.ShapeDtypeStruct((s, D_MODEL), BF16), jax.ShapeDtypeStruct((s, D_GROUP), F32)],
        aliases={9: 0},
        scratch=[pltpu.VMEM((CV_PAD + t, D_GROUP), F32), pltpu.VMEM((SUBLANE - 1, CV_PAD + t, D_GROUP), F32)],
        name=name, rider=rider)


def _cv_bwd(h_in, c, dmix, cw, gng, gnb, mavg, wpw, *, name, rider=None):
    s = h_in.shape[0]
    t = _seq_tile(s, CV_TILE)
    nt = s // t
    ch = min(CV_CHUNK, t)

    def body(v_ref, g_ref, c_ref, do_ref, cw_ref, gng_ref, gnb_ref, mavg_ref, wpw_ref,
             dvg_ref, dwpw_ref, dcw_ref, dbpw_ref, dgg_ref, dgb_ref, dcb_ref, dcpad, hgbuf, shifted):
        @pl.when(pl.program_id(0) == 0)
        def _():
            dcpad[t:t + CV_PAD, :] = jnp.zeros((CV_PAD, D_GROUP), F32)
            for r in (dwpw_ref, dcw_ref, dbpw_ref, dgg_ref, dgb_ref, dcb_ref):
                r[...] = jnp.zeros_like(r)

        mavg = mavg_ref[...]
        xn, rstd = _gn_stats(c_ref[...], mavg)
        gg = gng_ref[...]
        gn = xn * gg + gnb_ref[...]
        sg = _sigmoid(gn)
        dout = do_ref[...]
        dwpw_ref[...] += _dot_tn(gn * sg, dout)
        dbpw_ref[...] += _colsum(dout)
        dgn = _dot_nt(dout, wpw_ref[...]) * (sg * (1.0 + gn * (1.0 - sg)))
        dgg_ref[...] += _colsum(dgn * xn)
        dgb_ref[...] += _colsum(dgn)
        dxn = dgn * gg
        dc = rstd * (dxn - _dot_hi(dxn, mavg) - xn * _dot_hi(dxn * xn, mavg))
        dcb_ref[...] += _colsum(dc)
        dcpad[0:t, :] = dc

        v = v_ref[...]
        sgm = _sigmoid(g_ref[...])
        hgbuf[...] = v * sgm
        _shifted_copies(dcpad, shifted, t + CV_PAD)
        for r0 in range(0, t, ch):
            hg = hgbuf[r0:r0 + ch, :]
            acc = jnp.zeros((ch, D_GROUP), F32)
            for k in range(CONV_WIDTH):
                o = (CONV_WIDTH - 1) - k + r0
                sh = _window(dcpad, shifted, o, ch)
                acc = acc + cw_ref[k:k + 1, :] * sh
                dcw_ref[k:k + 1, :] += _colsum(hg * sh)
            hgbuf[r0:r0 + ch, :] = acc
        dcpad[t:t + CV_PAD, :] = dcpad[0:CV_PAD, :]
        dhg = hgbuf[...]
        dvg_ref[:, :D_GROUP] = dhg * sgm
        dvg_ref[:, D_GROUP:] = dhg * v * sgm * (1.0 - sgm)

    def rev(col):
        return lambda i: (nt - 1 - i, col)

    ins = [h_in, h_in, c, dmix, cw, gng, gnb, mavg, wpw]
    in_specs = [pl.BlockSpec((t, D_GROUP), rev(COL_CV_V)), pl.BlockSpec((t, D_GROUP), rev(COL_CV_G)),
                pl.BlockSpec((t, D_GROUP), rev(0)), pl.BlockSpec((t, D_GROUP), rev(MIX_CV))] + [_full_spec(a) for a in ins[4:]]
    vec = jax.ShapeDtypeStruct((1, D_GROUP), F32)
    outs = [jax.ShapeDtypeStruct((s, N_IN_COLS), F32),
            jax.ShapeDtypeStruct((D_GROUP, D_GROUP), F32), jax.ShapeDtypeStruct((CV_PAD, D_GROUP), F32), vec, vec, vec, vec]
    out_specs = [pl.BlockSpec((t, 2 * D_GROUP), rev(COL_CV_V // 2))] + [_full_spec(o) for o in outs[1:]]
    return _call(
        body, grid=(nt,), ins=ins, in_specs=in_specs, out_specs=out_specs, outs=outs,
        scratch=[pltpu.VMEM((t + CV_PAD, D_GROUP), F32), pltpu.VMEM((t, D_GROUP), F32),
                 pltpu.VMEM((SUBLANE - 1, t + CV_PAD, D_GROUP), F32)], name=name, rider=rider)


LRU_TILE = 256


def _lru_gates(xc, wr_ref, br_ref, wi_ref, bi_ref, sp_ref):
    r = _sigmoid(_dot(xc, wr_ref[...]) + br_ref[...])
    i = _sigmoid(_dot(xc, wi_ref[...]) + bi_ref[...])
    log_a = -LRU_C * r * sp_ref[...]
    a = jnp.exp(log_a)
    m = jnp.sqrt(_neg_expm1(2.0 * log_a))
    return r, i, a, m


def _lru_fwd(h_in, lcw, lcb, wr, br, wi, bi, sp, mix, *, name):
    s = h_in.shape[0]
    t = _seq_tile(s, LRU_TILE)
    pad = max(t // 2, SUBLANE)

    def body(xg_ref, xr_ref, lcw_ref, lcb_ref, wr_ref, br_ref, wi_ref, bi_ref, sp_ref, _mix_in,
             out_ref, xc_ref, h_ref, xpad, a0, a1, b0, b1, carry):
        @pl.when(pl.program_id(0) == 0)
        def _():
            xpad[0:SUBLANE, :] = jnp.zeros((SUBLANE, D_GROUP), F32)
            for bf in (a0, a1, b0, b1):
                bf[0:pad, :] = jnp.zeros((pad, D_GROUP), F32)
            carry[...] = jnp.zeros_like(carry)

        xpad[SUBLANE:SUBLANE + t, :] = xr_ref[...]
        xc = jnp.broadcast_to(lcb_ref[...], (t, D_GROUP))
        for k in range(LRU_CONV_WIDTH):
            o = SUBLANE - (LRU_CONV_WIDTH - 1) + k
            xc = xc + lcw_ref[k:k + 1, :] * xpad[o:o + t, :]
        xpad[0:SUBLANE, :] = xpad[t:t + SUBLANE, :]
        xc_ref[...] = xc
        _, i, a, m = _lru_gates(xc, wr_ref, br_ref, wi_ref, bi_ref, sp_ref)
        a0[pad:pad + t, :] = a
        b0[pad:pad + t, :] = m * (i * xc)
        b0[pad:pad + 1, :] += a0[pad:pad + 1, :] * carry[0:1, :]
        fin = _rscan_levels((a0, a1), (b0, b1), t, pad, reverse=False)
        hbuf = (b0, b1)[fin]
        carry[0:1, :] = hbuf[pad + t - 1:pad + t, :]
        h = hbuf[pad:pad + t, :]
        h_ref[...] = h
        out_ref[...] = (h * _gelu(xg_ref[...])).astype(BF16)

    ins = [h_in, h_in, lcw, lcb, wr, br, wi, bi, sp, mix]
    row = pl.BlockSpec((t, D_GROUP), lambda i: (i, 0))
    in_specs = [pl.BlockSpec((t, D_GROUP), lambda i: (i, COL_LRU_G)), pl.BlockSpec((t, D_GROUP), lambda i: (i, COL_LRU_X))] + \
               [_full_spec(a) for a in ins[2:9]] + [_ANY]
    return pl.pallas_call(
        body, grid=(s // t,), in_specs=in_specs,
        out_specs=[pl.BlockSpec((t, D_GROUP), lambda i: (i, MIX_LRU)), row, row],
        out_shape=[jax.ShapeDtypeStruct((s, D_MODEL), BF16)] + [jax.ShapeDtypeStruct((s, D_GROUP), F32)] * 2,
        input_output_aliases={9: 0},
        scratch_shapes=[pltpu.VMEM((SUBLANE + t, D_GROUP), F32)] + [pltpu.VMEM((pad + t, D_GROUP), F32)] * 4 +
                       [pltpu.VMEM((SUBLANE, D_GROUP), F32)],
        compiler_params=_cparams(1), name=name)(*ins)


def _lru_bwd(h_in, xc_all, h_all, dmix, lcw, wr, br, wi, bi, sp, dh_all, *, name):
    s = h_in.shape[0]
    t = _seq_tile(s, LRU_TILE)
    nt = s // t
    pad = max(t // 2, SUBLANE)
    tb = t // SUBLANE

    def body(xg_ref, xr_ref, xc_ref, h_ref, hprev_ref, do_ref, lcw_ref, wr_ref, br_ref, wi_ref, bi_ref, sp_ref, _dh_in,
             dgr_ref, dwr_ref, dwi_ref, dlcw_ref, dbr_ref, dbi_ref, dsp_ref, dlcb_ref,
             a0, a1, b0, b1, hp, dxpad, carry):
        pid = pl.program_id(0)

        @pl.when(pid == 0)
        def _():
            for bf in (a0, a1, b0, b1):
                bf[pad + t:pad + t + pad, :] = jnp.zeros((pad, D_GROUP), F32)
            dxpad[t:t + SUBLANE, :] = jnp.zeros((SUBLANE, D_GROUP), F32)
            carry[...] = jnp.zeros_like(carry)
            for r in (dwr_ref, dwi_ref, dlcw_ref, dbr_ref, dbi_ref, dsp_ref, dlcb_ref):
                r[...] = jnp.zeros_like(r)

        xc = xc_ref[...]
        h = h_ref[...]
        dout = do_ref[...]
        gate, dgate = _gelu_and_grad(xg_ref[...])
        dgr_ref[:, :D_GROUP] = dout * h * dgate
        r, i, a, m = _lru_gates(xc, wr_ref, br_ref, wi_ref, bi_ref, sp_ref)

        a0[pad:pad + t, :] = a
        b0[pad:pad + t, :] = dout * gate
        b0[pad + t - 1:pad + t, :] += carry[0:1, :]
        a1[pad:pad + t, :] = a0[pad + 1:pad + 1 + t, :]
        fin = _rscan_levels((a1, a0), (b0, b1), t, pad, reverse=True)
        lam = (b0, b1)[fin][pad:pad + t, :]
        carry[0:1, :] = a[0:1, :] * lam[0:1, :]

        is_first = pid == nt - 1
        hp[0:SUBLANE, :] = jnp.where(is_first, 0.0, hprev_ref[...])
        hp[SUBLANE:SUBLANE + t, :] = h
        hprev = hp[SUBLANE - 1:SUBLANE - 1 + t, :]

        ix = i * xc
        dmm = lam * ix
        dix = lam * m
        da = lam * hprev - dmm * (a / m)
        dlog_a = da * a
        dr = dlog_a * (-LRU_C * sp_ref[...])
        dsp_ref[...] += _colsum(dlog_a * (-LRU_C * r))
        dpr = dr * r * (1.0 - r)
        dpi = dix * xc * i * (1.0 - i)
        dbr_ref[...] += _colsum(dpr)
        dbi_ref[...] += _colsum(dpi)
        dwr_ref[...] += _dot_tn(xc, dpr)
        dwi_ref[...] += _dot_tn(xc, dpi)
        dxc = dix * i + _dot_nt(dpr, wr_ref[...]) + _dot_nt(dpi, wi_ref[...])
        dlcb_ref[...] += _colsum(dxc)

        dxpad[0:t, :] = dxc
        xr = xr_ref[...]
        dxr = jnp.zeros((t, D_GROUP), F32)
        for k in range(LRU_CONV_WIDTH):
            o = (LRU_CONV_WIDTH - 1) - k
            sh = dxpad[o:o + t, :]
            dxr = dxr + lcw_ref[k:k + 1, :] * sh
            dlcw_ref[k:k + 1, :] += _colsum(xr * sh)
        dxpad[t:t + SUBLANE, :] = dxpad[0:SUBLANE, :]
        dgr_ref[:, D_GROUP:] = dxr

    def rev(col):
        return lambda i: (nt - 1 - i, col)

    ins = [h_in, h_in, xc_all, h_all, h_all, dmix, lcw, wr, br, wi, bi, sp, dh_all]
    in_specs = [pl.BlockSpec((t, D_GROUP), rev(COL_LRU_G)), pl.BlockSpec((t, D_GROUP), rev(COL_LRU_X)),
                pl.BlockSpec((t, D_GROUP), rev(0)), pl.BlockSpec((t, D_GROUP), rev(0)),
                pl.BlockSpec((SUBLANE, D_GROUP), lambda i: (jnp.maximum((nt - 1 - i) * tb - 1, 0), 0)),
                pl.BlockSpec((t, D_GROUP), rev(MIX_LRU))] + [_full_spec(a) for a in ins[6:12]] + [_ANY]
    vec = jax.ShapeDtypeStruct((1, D_GROUP), F32)
    mat = jax.ShapeDtypeStruct((D_GROUP, D_GROUP), F32)
    outs = [jax.ShapeDtypeStruct((s, N_IN_COLS), F32), mat, mat, jax.ShapeDtypeStruct((SUBLANE, D_GROUP), F32),
            vec, vec, vec, vec]
    out_specs = [pl.BlockSpec((t, 2 * D_GROUP), rev(COL_LRU_G // 2))] + [_full_spec(o) for o in outs[1:]]
    return pl.pallas_call(
        body, grid=(nt,), in_specs=in_specs, out_specs=out_specs, out_shape=outs, input_output_aliases={12: 0},
        scratch_shapes=[pltpu.VMEM((pad + t + pad, D_GROUP), F32)] * 4 +
                       [pltpu.VMEM((SUBLANE + t, D_GROUP), F32), pltpu.VMEM((t + SUBLANE, D_GROUP), F32),
                        pltpu.VMEM((SUBLANE, D_GROUP), F32)],
        compiler_params=_cparams(1), name=name)(*ins)


def _blockdiag(w):
    h, d, _ = w.shape
    return jnp.tile(w.reshape(h * d, d), (1, h)) * _block_mask(h, d, d)


ATTN_TILE = 512
ATTN_SCALE = ATTN_HEAD_DIM ** -0.5


def _attn_big(kv):
    m = kv.shape[0]
    kbig = jnp.tile(kv[:, :D_GROUP].T, (1, ATTN_HEADS)) * _block_mask(ATTN_HEADS, ATTN_HEAD_DIM, m)
    vbig = jnp.tile(kv[:, D_GROUP:], (ATTN_HEADS, 1)) * _block_mask(ATTN_HEADS, m, ATTN_HEAD_DIM)
    return kbig, vbig


def _attn_probs(q, kbig_ref, m):
    sc = _dot(q, kbig_ref[...]) * ATTN_SCALE
    ps = []
    for h in range(ATTN_HEADS):
        sh = sc[:, h * m:(h + 1) * m]
        e = jnp.exp(sh - jnp.max(sh, axis=1, keepdims=True))
        ps.append(e / jnp.sum(e, axis=1, keepdims=True))
    return ps


def _attn_fwd(h_in, kbig, vbig, mix, *, name):
    s = h_in.shape[0]
    t = _seq_tile(s, ATTN_TILE)
    m = kbig.shape[1] // ATTN_HEADS

    def body(q_ref, kbig_ref, vbig_ref, _mix_in, o_ref):
        ps = _attn_probs(q_ref[...], kbig_ref, m)
        o_ref[...] = _dot(jnp.concatenate(ps, axis=1), vbig_ref[...]).astype(BF16)

    return pl.pallas_call(
        body, grid=(s // t,),
        in_specs=[pl.BlockSpec((t, D_GROUP), lambda i: (i, COL_Q)), _full_spec(kbig), _full_spec(vbig), _ANY],
        out_specs=pl.BlockSpec((t, D_GROUP), lambda i: (i, MIX_ATTN)),
        out_shape=jax.ShapeDtypeStruct((s, D_MODEL), BF16), input_output_aliases={3: 0},
        compiler_params=_cparams(1), name=name)(h_in, kbig, vbig, mix)


def _attn_bwd(h_in, dmix, kbig, vbig, du_s5, dh_all, *, name):
    s = h_in.shape[0]
    t = _seq_tile(s, ATTN_TILE)
    m = kbig.shape[1] // ATTN_HEADS

    def body(q_ref, do_ref, kbig_ref, vbig_ref, dus5_ref, _dh_in, dpair_ref, dk_ref, dv_ref):
        @pl.when(pl.program_id(0) == 0)
        def _():
            dk_ref[...] = jnp.zeros_like(dk_ref)
            dv_ref[...] = jnp.zeros_like(dv_ref)

        q = q_ref[...]
        dout = do_ref[...]
        ps = _attn_probs(q, kbig_ref, m)
        dp = _dot_nt(dout, vbig_ref[...])
        dss = []
        for h in range(ATTN_HEADS):
            dph = dp[:, h * m:(h + 1) * m]
            dss.append(ps[h] * (dph - jnp.sum(dph * ps[h], axis=1, keepdims=True)))
        ds = (jnp.concatenate(dss, axis=1) * ATTN_SCALE).astype(BF16)
        dv_ref[...] += _dot_tn(jnp.concatenate(ps, axis=1), dout)
        dpair_ref[:, :D_GROUP] = dus5_ref[...]
        dpair_ref[:, D_GROUP:] = _dot_nt(ds, kbig_ref[...])
        dk_ref[...] += _dot_tn(q, ds)

    assert (COL_S5, COL_Q) == (4, 5)
    outs = [jax.ShapeDtypeStruct((s, N_IN_COLS), F32), jax.ShapeDtypeStruct(kbig.shape, F32),
            jax.ShapeDtypeStruct(vbig.shape, F32)]
    return pl.pallas_call(
        body, grid=(s // t,),
        in_specs=[pl.BlockSpec((t, D_GROUP), lambda i: (i, COL_Q)), pl.BlockSpec((t, D_GROUP), lambda i: (i, MIX_ATTN)),
                  _full_spec(kbig), _full_spec(vbig), pl.BlockSpec((t, D_GROUP), lambda i: (i, 0)), _ANY],
        out_specs=[pl.BlockSpec((t, 2 * D_GROUP), lambda i: (i, COL_S5 // 2)), _full_spec(outs[1]), _full_spec(outs[2])],
        out_shape=outs, input_output_aliases={5: 0},
        compiler_params=_cparams(1), name=name)(h_in, dmix, kbig, vbig, du_s5, dh_all)


FFN_TILE = 128
FFN_COL_CHUNK = 256
FFN_ROW_CHUNK = 64


def _ffn_conv(pad_ref, w_ref, b_ref, r0, ch, c0):
    cc = FFN_COL_CHUNK
    acc = jnp.broadcast_to(b_ref[:, c0:c0 + cc], (ch, cc))
    for k in range(FFN_CONV_WIDTH):
        o = SUBLANE - (FFN_CONV_WIDTH - 1) + k + r0
        acc = acc + w_ref[k:k + 1, c0:c0 + cc] * pad_ref[o:o + ch, c0:c0 + cc]
    return acc


def _ffn_gate_fwd(u, fcw, fcb, *, name, rider=None):
    s = u.shape[0]
    t = _seq_tile(s, FFN_TILE)
    ch = min(FFN_ROW_CHUNK, t)
    cc = FFN_COL_CHUNK

    def body(u_ref, w_ref, b_ref, o_ref, uc_ref, upad):
        @pl.when(pl.program_id(0) == 0)
        def _():
            upad[0:SUBLANE, :] = jnp.zeros((SUBLANE, 2 * D_FF), F32)

        upad[SUBLANE:SUBLANE + t, :] = u_ref[...].astype(F32)
        for c0 in range(0, D_FF, cc):
            for r0 in range(0, t, ch):
                val = _ffn_conv(upad, w_ref, b_ref, r0, ch, c0)
                gt = _ffn_conv(upad, w_ref, b_ref, r0, ch, c0 + D_FF)
                o_ref[r0:r0 + ch, c0:c0 + cc] = (val * _gelu(gt)).astype(BF16)
                uc_ref[r0:r0 + ch, c0:c0 + cc] = val.astype(BF16)
                uc_ref[r0:r0 + ch, c0 + D_FF:c0 + D_FF + cc] = gt.astype(BF16)
        upad[0:SUBLANE, :] = upad[t:t + SUBLANE, :]

    return _call(
        body, grid=(s // t,), ins=[u, fcw, fcb],
        in_specs=[pl.BlockSpec((t, 2 * D_FF), lambda i: (i, 0)), _full_spec(fcw), _full_spec(fcb)],
        out_specs=[pl.BlockSpec((t, D_FF), lambda i: (i, 0)), pl.BlockSpec((t, 2 * D_FF), lambda i: (i, 0))],
        outs=[jax.ShapeDtypeStruct((s, D_FF), BF16), jax.ShapeDtypeStruct((s, 2 * D_FF), BF16)],
        scratch=[pltpu.VMEM((SUBLANE + t, 2 * D_FF), F32)], name=name, rider=rider)


def _ffn_gate_bwd(u, uc, dh, fcw, *, name, rider=None):
    s = u.shape[0]
    t = _seq_tile(s, FFN_TILE)
    nt = s // t
    ch = min(FFN_ROW_CHUNK, t)
    cc = FFN_COL_CHUNK

    def body(u_ref, uc_ref, dh_ref, w_ref, du_ref, dw_ref, db_ref, dpad):
        @pl.when(pl.program_id(0) == 0)
        def _():
            dpad[t:t + SUBLANE, :] = jnp.zeros((SUBLANE, 2 * D_FF), F32)
            dw_ref[...] = jnp.zeros_like(dw_ref)
            db_ref[...] = jnp.zeros_like(db_ref)

        for c0 in range(0, D_FF, cc):
            for r0 in range(0, t, ch):
                val = uc_ref[r0:r0 + ch, c0:c0 + cc].astype(F32)
                gt = uc_ref[r0:r0 + ch, c0 + D_FF:c0 + D_FF + cc].astype(F32)
                gl, dgl = _gelu_and_grad(gt)
                d = dh_ref[r0:r0 + ch, c0:c0 + cc].astype(F32)
                dpad[r0:r0 + ch, c0:c0 + cc] = d * gl
                dpad[r0:r0 + ch, c0 + D_FF:c0 + D_FF + cc] = d * val * dgl
        for c0 in range(0, 2 * D_FF, cc):
            dbs = jnp.zeros((1, cc), F32)
            dws = [jnp.zeros((1, cc), F32) for _ in range(FFN_CONV_WIDTH)]
            for r0 in range(0, t, ch):
                x = u_ref[r0:r0 + ch, c0:c0 + cc].astype(F32)
                acc = jnp.zeros((ch, cc), F32)
                for k in range(FFN_CONV_WIDTH):
                    o = (FFN_CONV_WIDTH - 1) - k + r0
                    sh = dpad[o:o + ch, c0:c0 + cc]
                    acc = acc + w_ref[k:k + 1, c0:c0 + cc] * sh
                    dws[k] = dws[k] + _colsum(x * sh)
                    if k == FFN_CONV_WIDTH - 1:
                        dbs = dbs + _colsum(sh)
                du_ref[r0:r0 + ch, c0:c0 + cc] = acc.astype(BF16)
            db_ref[:, c0:c0 + cc] += dbs
            for k in range(FFN_CONV_WIDTH):
                dw_ref[k:k + 1, c0:c0 + cc] += dws[k]
        dpad[t:t + SUBLANE, :] = dpad[0:SUBLANE, :]

    outs = [jax.ShapeDtypeStruct((s, 2 * D_FF), BF16), jax.ShapeDtypeStruct((SUBLANE, 2 * D_FF), F32),
            jax.ShapeDtypeStruct((1, 2 * D_FF), F32)]
    return _call(
        body, grid=(nt,), ins=[u, uc, dh, fcw],
        in_specs=[pl.BlockSpec((t, 2 * D_FF), lambda i: (nt - 1 - i, 0)),
                  pl.BlockSpec((t, 2 * D_FF), lambda i: (nt - 1 - i, 0)),
                  pl.BlockSpec((t, D_FF), lambda i: (nt - 1 - i, 0)), _full_spec(fcw)],
        out_specs=[pl.BlockSpec((t, 2 * D_FF), lambda i: (nt - 1 - i, 0)), _full_spec(outs[1]), _full_spec(outs[2])],
        outs=outs, scratch=[pltpu.VMEM((t + SUBLANE, 2 * D_FF), F32)], name=name, rider=rider)


def _adamw_body(g_ref, w_ref, m_ref, v_ref, go_ref, d_ref, mo_ref, vo_ref):
    inv_b1 = 1.0 - ADAM_B1 ** ADAM_STEP
    inv_b2 = 1.0 - ADAM_B2 ** ADAM_STEP
    g = g_ref[0].astype(F32)
    for dev in range(1, N_DEV):
        g = g + g_ref[dev].astype(F32)
    go_ref[...] = g
    mn = ADAM_B1 * m_ref[...] + (1.0 - ADAM_B1) * g
    vn = ADAM_B2 * v_ref[...] + (1.0 - ADAM_B2) * (g * g)
    mo_ref[...] = mn
    vo_ref[...] = vn
    d_ref[...] = -ADAM_LR * ((mn / inv_b1) / (jnp.sqrt(vn / inv_b2) + ADAM_EPS) + ADAM_WD * w_ref[...])


def _adamw(gstack, w, m, v, *, name):
    _, r, c = gstack.shape
    tr = _pick_rows(r, PACK_ROW_BLOCK)

    def body(*refs):
        _adamw_body(*refs)

    blk = pl.BlockSpec((tr, c), lambda i: (i, 0))
    sh = jax.ShapeDtypeStruct((r, c), F32)
    return pl.pallas_call(
        body, grid=(r // tr,),
        in_specs=[pl.BlockSpec((N_DEV, tr, c), lambda i: (0, i, 0)), blk, blk, blk],
        out_specs=[blk] * 4, out_shape=[sh] * 4,
        compiler_params=_cparams(1), name=name)(gstack, w, m, v)


def _adamw_layer(gstack, w, m, v, layer, into, *, name):
    n_layers, r, c = w.shape
    tr = _pick_rows(r, PACK_ROW_BLOCK)

    def body(g_ref, w_ref, m_ref, v_ref, *rest):
        _adamw_body(g_ref, w_ref, m_ref, v_ref, *rest[-4:])

    blk = pl.BlockSpec((None, tr, c), lambda i: (layer, i, 0))
    sh = jax.ShapeDtypeStruct((n_layers, r, c), F32)
    into = list(into or [])
    return pl.pallas_call(
        body, grid=(r // tr,),
        in_specs=[pl.BlockSpec((N_DEV, tr, c), lambda i: (0, i, 0)), blk, blk, blk] + [_ANY] * len(into),
        out_specs=[blk] * 4, out_shape=[sh] * 4, input_output_aliases={4 + k: k for k in range(len(into))},
        compiler_params=_cparams(1), name=name)(gstack, w, m, v, *into)


def _exchange(rider, *, name):
    n = rider.n

    def body(*refs):
        x_refs, out_refs, sems = refs[:n], refs[n:2 * n], refs[2 * n:]
        rider.start(x_refs, out_refs, sems)
        rider.wait(x_refs, out_refs, sems)

    return pl.pallas_call(
        body, in_specs=[_ANY] * n, out_specs=[_ANY] * n, out_shape=rider.out_shapes(),
        scratch_shapes=rider.scratch(), name=name)(*rider.srcs)


def _pack_rows(n):
    rows = -(-n // PACK_COLS)
    return -(-rows // SUBLANE) * SUBLANE


def _pack(arrs, dtype):
    flat = jnp.concatenate([a.reshape(-1).astype(dtype) for a in arrs])
    rows = _pack_rows(flat.shape[0])
    flat = jnp.pad(flat, (0, rows * PACK_COLS - flat.shape[0]))
    return flat.reshape(rows, PACK_COLS)


def _pack_lead(arrs, dtype):
    flat = jnp.concatenate([a.reshape(N_DEV, -1).astype(dtype) for a in arrs], axis=1)
    rows = _pack_rows(flat.shape[1])
    flat = jnp.pad(flat, ((0, 0), (0, rows * PACK_COLS - flat.shape[1])))
    return flat.reshape(N_DEV, rows, PACK_COLS)


def _pack_layers(arrs, dtype):
    n_layers = arrs[0].shape[0]
    flat = jnp.concatenate([a.reshape(n_layers, -1).astype(dtype) for a in arrs], axis=1)
    rows = _pack_rows(flat.shape[1])
    flat = jnp.pad(flat, ((0, 0), (0, rows * PACK_COLS - flat.shape[1])))
    return flat.reshape(n_layers, rows, PACK_COLS)


def _unpack_layers(packed, shapes):
    flat = packed.reshape(packed.shape[0], -1)
    out, pos = [], 0
    for sh in shapes:
        n = math.prod(sh[1:])
        out.append(flat[:, pos:pos + n].reshape(sh))
        pos += n
    return out


def _unpack(packed, shapes, lead=False):
    flat = packed.reshape(N_DEV, -1) if lead else packed.reshape(-1)
    out, pos = [], 0
    for sh in shapes:
        n = math.prod(sh)
        out.append(flat[:, pos:pos + n].reshape((N_DEV,) + tuple(sh)) if lead else flat[pos:pos + n].reshape(sh))
        pos += n
    return out


def _join_shards(stacked, axis):
    return jnp.concatenate([stacked[d] for d in range(N_DEV)], axis=axis)


def _split_shards(full, axis):
    return jnp.stack(jnp.split(full, N_DEV, axis=axis), axis=0)


def _perm_in_cols(a, inverse=False):
    blocks = jnp.split(a, 6, axis=-1)
    if inverse:
        order = [IN_PERM.index(j) for j in range(6)]
    else:
        order = list(IN_PERM)
    return jnp.concatenate([blocks[j] for j in order], axis=-1)


def _row(v):
    return v.reshape(1, -1)


def _pad_rows(w, rows):
    return jnp.pad(w, ((0, rows - w.shape[0]), (0, 0)))


def _gn_avg_matrix():
    return _block_mask(GN_GROUPS, D_GROUP // GN_GROUPS, D_GROUP // GN_GROUPS) / (D_GROUP // GN_GROUPS)


def _layer_params(p, l):
    q = {}
    s5_mats, q["s5_vjp"] = jax.vjp(_s5_chunk_map, p["s5_lam_re"][l], p["s5_lam_im"][l], p["s5_log_dt"][l],
                                   p["s5_b_re"][l], p["s5_b_im"][l], p["s5_c_re"][l], p["s5_c_im"][l], p["s5_d"][l])
    q["s5_mats"] = [m.astype(BF16) for m in s5_mats[:5]]
    q["s5_a16"] = s5_mats[5]
    (q["wr"], q["wi"]), q["lru_w_vjp"] = jax.vjp(lambda r, i: (_blockdiag(r), _blockdiag(i)), p["lru_w_r"][l], p["lru_w_i"][l])
    q["wr"], q["wi"] = q["wr"].astype(BF16), q["wi"].astype(BF16)
    q["sp"], q["sp_vjp"] = jax.vjp(lambda lam: _row(jax.nn.softplus(-lam)), p["lru_lam"][l])
    return q


WEIGHT_RIDES = {(0, "ln_in_fwd"): [("w_in", 0)],
                (0, "inproj"): [("attn_w_kv", 0), ("w_out", 0), ("small_pack", 0)],
                (0, "cv_fwd"): [("ffn_w_up#a", 0)],
                (0, "outproj"): [("ffn_w_up#b", 0)],
                (0, "ffn_up"): [("ffn_w_down", 0), ("w_in", 1), ("attn_w_kv", 1), ("w_out", 1)],
                (0, "ffn_gate_fwd"): [("ffn_w_up", 1)],
                (0, "ffn_down"): [("ffn_w_down", 1)]}
GRAD_RIDES = {(1, "ffn_gate_bwd"): [("ffn_w_down", 1)],
              (0, "dw_down"): [("w_out", 1), ("attn_w_kv", 1), ("w_in", 1)],
              (0, "dhff"): [("rep", 1), ("ssh", 1)],
              (0, "ffn_gate_bwd"): [("ffn_w_up", 1)],
              (0, "dw_up"): [("ffn_w_down", 0)],
              (0, "dx1"): [("ffn_w_up", 0)],
              (0, "cv_bwd"): [("w_out", 0)],
              (0, "dw_in"): [("attn_w_kv", 0), ("ssh", 0), ("rep", 0)],
              (0, "dxs"): [("w_in", 0)]}


def _join_cols(pieces, *, name):
    n_dev, k, c = pieces[0].shape
    assert (2 * c) % LANE == 0 and all(p.shape == pieces[0].shape for p in pieces)
    n_p = len(pieces)

    def body(*refs):
        o_ref = refs[n_p]
        for i in range(n_p):
            @pl.when(pl.program_id(0) == i)
            def _(i=i):
                o_ref[...] = jnp.concatenate([refs[i][0], refs[i][1]], axis=1)

    return pl.pallas_call(
        body, grid=(n_p, n_dev // 2), in_specs=[pl.BlockSpec((2, k, c), lambda i, j: (j, 0, 0))] * n_p,
        out_specs=pl.BlockSpec((k, 2 * c), lambda i, j: (i, j)),
        out_shape=jax.ShapeDtypeStruct((n_p * k, n_dev * c), pieces[0].dtype),
        compiler_params=_cparams(2), name=name)(*pieces)


def _split_cols(full, *, name):
    k, n = full.shape
    c = n // N_DEV
    assert (2 * c) % LANE == 0

    def body(x_ref, o_ref):
        o_ref[0] = x_ref[:, :c]
        o_ref[1] = x_ref[:, c:]

    return pl.pallas_call(
        body, grid=(N_DEV // 2,), in_specs=[pl.BlockSpec((k, 2 * c), lambda j: (0, j))],
        out_specs=pl.BlockSpec((2, k, c), lambda j: (j, 0, 0)), out_shape=jax.ShapeDtypeStruct((N_DEV, k, c), full.dtype),
        compiler_params=_cparams(1), name=name)(full)


def _assemble_weight(n, pieces, layer=0):
    if SHARDED[n] == 2:
        full = _join_cols(pieces, name=f"l{layer}_join_{n}")
        return _perm_in_cols(full) if n == "w_in" else full
    (gathered,) = pieces
    return gathered.reshape(-1, gathered.shape[-1])


def _grad_source(n, g, layer=0):
    g = g.astype(BF16)
    if SHARDED[n] == 2:
        if n == "w_in":
            g = _perm_in_cols(g, inverse=True)
        return _split_cols(g, name=f"l{layer}_split_d{n}"), "lead"
    return g, "rows"


def _hosted(fn, keys_rider, land, *args, **kw):
    keys, rider = keys_rider
    if rider is None:
        return fn(*args, **kw)
    out, routs = fn(*args, rider=rider, **kw)
    land(keys, routs)
    return out


def _local_step(x, mem, target, p, big_w, shards=None, unpack_small=None):
    dist = shards is not None
    gdt = BF16 if dist else F32
    small, saved = {}, []
    big_g, ready, recv = {}, {}, {}
    mavg = _gn_avg_matrix()
    s5_perm = _s5_perm()

    def weight_rider(l, host):
        keys = WEIGHT_RIDES.get((l, host), []) if dist else []
        return keys, (_Rider([shards[n][ll] for n, ll in keys], ["all"] * len(keys)) if keys else None)

    halves = {}

    def land_weights(keys, routs):
        for (n, ll), r in zip(keys, routs):
            if n == "small_pack":
                p.update(unpack_small(r))
            elif "#" in n:
                base = n.split("#")[0]
                halves[(n, ll)] = r
                if (base + "#a", ll) in halves and (base + "#b", ll) in halves:
                    big_w[base][ll] = _assemble_weight(base, [halves[(base + "#a", ll)], halves[(base + "#b", ll)]], ll)
            else:
                big_w[n][ll] = _assemble_weight(n, [r], ll)

    def grad_rider(l, host):
        keys = [k for k in GRAD_RIDES.get((l, host), []) if k in ready] if dist else []
        return keys, (_Rider([ready[k][0] for k in keys], [ready[k][1] for k in keys]) if keys else None)

    def land_grads(keys, routs):
        for k, r in zip(keys, routs):
            recv[k] = r
            del ready[k]

    def big_grad(n, l, g):
        if dist:
            ready[(n, l)] = _grad_source(n, g, l)
        else:
            big_g[(n, l)] = g

    xs = _hosted(_ln_fwd, weight_rider(0, "ln_in_fwd"), land_weights, x, _row(p["ln_in_g"]), _row(p["ln_in_b"]),
                 name="ln_in_fwd")
    for l in range(DEPTH):
        q = _layer_params(p, l)
        n = f"l{l}_"
        hin = _hosted(_mm, weight_rider(l, "inproj"), land_weights, xs, big_w["w_in"][l], bias=_row(p["b_in"][l]),
                      name=n + "inproj")
        nb = hin.shape[0] // S5_CHUNK
        s5_pows = _s5_a16_powers(q["s5_a16"], nb.bit_length() - 1)
        s5_u2 = _s5_to_chunks(hin, COL_S5 * (D_GROUP // LANE), s5_perm, name=n + "s5_in")
        s5_y2, s5_x = _s5_core_fwd(s5_u2, *q["s5_mats"], s5_pows, name=n + "s5_core_fwd")
        s5_y1 = _s5_from_chunks(s5_y2, s5_perm, name=n + "s5_out")
        (mix,), _ = _s5_glu_fwd(s5_y1, p["s5_w_glu"][l], _row(p["s5_b_glu"][l]), name=n + "s5_glu_fwd")
        cvw = _pad_rows(p["cv_w"][l], CV_PAD)
        keys, rd = weight_rider(l, "cv_fwd")
        (mix, cv_c), routs = _cv_fwd(hin, cvw, _row(p["cv_b"][l]), _row(p["cv_gn_g"][l]), _row(p["cv_gn_b"][l]), mavg,
                                     p["cv_w_pw"][l], _row(p["cv_b_pw"][l]), mix, name=n + "cv_fwd", rider=rd)
        land_weights(keys, routs)
        lcw = _pad_rows(p["lru_conv_w"][l], SUBLANE)
        mix, lru_xc, lru_h = _lru_fwd(hin, lcw, _row(p["lru_conv_b"][l]), q["wr"], _row(p["lru_b_r"][l]), q["wi"],
                                      _row(p["lru_b_i"][l]), q["sp"], mix, name=n + "lru_fwd")
        kv = _mm(mem, big_w["attn_w_kv"][l], name=n + "kv")
        (kbig, vbig), kv_vjp = jax.vjp(_attn_big, kv)
        kbig, vbig = kbig.astype(BF16), vbig.astype(BF16)
        mix = _attn_fwd(hin, kbig, vbig, mix, name=n + "attn_fwd")
        r1, x1 = _hosted(_mm, weight_rider(l, "outproj"), land_weights, mix, big_w["w_out"][l], bias=_row(p["b_out"][l]),
                         res=xs, res_scale=ALPHA, ln=(_row(p["ln1_g"][l]), _row(p["ln1_b"][l])), name=n + "outproj")
        u = _hosted(_mm, weight_rider(l, "ffn_up"), land_weights, x1, big_w["ffn_w_up"][l], out_dtype=BF16,
                    name=n + "ffn_up")
        fcw = _pad_rows(p["ffn_conv_w"][l], SUBLANE)
        fcb = _row(p["ffn_conv_b"][l])
        keys, rd = weight_rider(l, "ffn_gate_fwd")
        (hff, uc), routs = _ffn_gate_fwd(u, fcw, fcb, name=n + "ffn_gate_fwd", rider=rd)
        land_weights(keys, routs)
        if l < DEPTH - 1:
            r2, x2 = _hosted(_mm, weight_rider(l, "ffn_down"), land_weights, hff, big_w["ffn_w_down"][l], res=x1,
                             res_scale=ALPHA, ln=(_row(p["ln2_g"][l]), _row(p["ln2_b"][l])), name=n + "ffn_down")
        else:
            r2, x2 = _mm(hff, big_w["ffn_w_down"][l], res=x1, res_scale=ALPHA, name=n + "ffn_down"), None
        saved.append(dict(q=q, xs=xs, hin=hin, s5_y1=s5_y1, s5_u2=s5_u2, s5_x=s5_x, s5_pows=s5_pows, cvw=cvw, cv_c=cv_c, lcw=lcw, lru_xc=lru_xc,
                          lru_h=lru_h, kbig=kbig, vbig=vbig, kv_vjp=kv_vjp, mix=mix, r1=r1, x1=x1, u=u, uc=uc, fcw=fcw,
                          hff=hff, r2=r2))
        xs = x2

    top = DEPTH - 1
    dr_top, dg_top, db_top, loss_blk = _loss_ln_bwd(saved[top]["r2"], _row(p["ln2_g"][top]), _row(p["ln2_b"][top]), target,
                                                     name="loss_ln_bwd")
    loss = loss_blk[0, 0]
    dx = None

    for l in reversed(range(DEPTH)):
        sv = saved[l]
        q = sv["q"]
        n = f"l{l}_"
        g = {}
        if l == top:
            dr2, g["ln2_g"], g["ln2_b"] = dr_top, dg_top, db_top
        else:
            dr2, g["ln2_g"], g["ln2_b"] = from_above
        big_grad("ffn_w_down", l, _hosted(_mm_tn, grad_rider(l, "dw_down"), land_grads, sv["hff"], dr2, out_dtype=gdt,
                                          name=n + "dw_down"))
        dhff = _hosted(_mm, grad_rider(l, "dhff"), land_grads, dr2, big_w["ffn_w_down"][l], trans_b=True,
                       out_dtype=BF16, name=n + "dhff")
        keys, rd = grad_rider(l, "ffn_gate_bwd")
        (du, dfw, g["ffn_conv_b"]), routs = _ffn_gate_bwd(sv["u"], sv["uc"], dhff, sv["fcw"], name=n + "ffn_gate_bwd",
                                                          rider=rd)
        land_grads(keys, routs)
        g["ffn_conv_w"] = dfw[:FFN_CONV_WIDTH]
        if dist:
            ready[("ffn_w_up", l)] = (_hosted(_mm_tn, grad_rider(l, "dw_up"), land_grads, sv["x1"], du, out_dtype=gdt,
                                              dev_cols=du.shape[1] // N_DEV, name=n + "dw_up"), "lead")
        else:
            big_grad("ffn_w_up", l, _mm_tn(sv["x1"], du, name=n + "dw_up"))
        dr1, g["ln1_g"], g["ln1_b"], g["b_out"] = _hosted(
            _mm, grad_rider(l, "dx1"), land_grads, du, big_w["ffn_w_up"][l], trans_b=True, res=dr2, res_scale=ALPHA,
            ln_bwd=(sv["r1"], _row(p["ln1_g"][l])), name=n + "dx1")
        big_grad("w_out", l, _mm_tn(sv["mix"], dr1, out_dtype=gdt, name=n + "dw_out"))
        dmix = _mm(dr1, big_w["w_out"][l], trans_b=True, name=n + "dmix")

        hin = sv["hin"]
        keys, rd = grad_rider(l, "cv_bwd")
        (dh, g["cv_w_pw"], dcw, g["cv_b_pw"], g["cv_gn_g"], g["cv_gn_b"], g["cv_b"]), routs = _cv_bwd(
            hin, sv["cv_c"], dmix, sv["cvw"], _row(p["cv_gn_g"][l]), _row(p["cv_gn_b"][l]), mavg, p["cv_w_pw"][l],
            name=n + "cv_bwd", rider=rd)
        land_grads(keys, routs)
        g["cv_w"] = dcw[:CONV_WIDTH]
        dh, dwr, dwi, dlcw, g["lru_b_r"], g["lru_b_i"], dsp, g["lru_conv_b"] = _lru_bwd(
            hin, sv["lru_xc"], sv["lru_h"], dmix, sv["lcw"], q["wr"], _row(p["lru_b_r"][l]), q["wi"],
            _row(p["lru_b_i"][l]), q["sp"], dh, name=n + "lru_bwd")
        g["lru_conv_w"] = dlcw[:LRU_CONV_WIDTH]
        g["lru_w_r"], g["lru_w_i"] = q["lru_w_vjp"]((dwr, dwi))
        (g["lru_lam"],) = q["sp_vjp"](dsp)
        dy1, g["s5_w_glu"], g["s5_b_glu"] = _s5_glu_bwd(sv["s5_y1"], dmix, p["s5_w_glu"][l], _row(p["s5_b_glu"][l]),
                                                        name=n + "s5_glu_bwd")
        s5_du2, *s5_dmats = _s5_core_bwd(sv["s5_u2"], _s5_to_chunks(dy1, 0, s5_perm, name=n + "s5_din"), sv["s5_x"],
                                         *q["s5_mats"], sv["s5_pows"], name=n + "s5_core_bwd")
        (g["s5_lam_re"], g["s5_lam_im"], g["s5_log_dt"], g["s5_b_re"], g["s5_b_im"], g["s5_c_re"], g["s5_c_im"],
         g["s5_d"]) = q["s5_vjp"](tuple(s5_dmats))
        dh, dkbig, dvbig = _attn_bwd(hin, dmix, sv["kbig"], sv["vbig"],
                                     _s5_from_chunks(s5_du2, s5_perm, name=n + "s5_dout"), dh, name=n + "attn_bwd")
        (dkv,) = sv["kv_vjp"]((dkbig, dvbig))
        big_grad("attn_w_kv", l, _mm_tn(mem, dkv, out_dtype=gdt, name=n + "dw_kv"))

        if dist:
            ready[("ssh", l)] = (_pack_lead([_split_shards(g[k], SHARDED[k] - 1) for k in SMALL_SHARDED], F32), "lead")
            ready[("rep", l)] = (_pack([g[k] for k in REP_LAYERED], F32), "all")
        gw_in, g["b_in"] = _hosted(_mm_tn, grad_rider(l, "dw_in"), land_grads, sv["xs"], dh, colsum=True, out_dtype=gdt,
                                   name=n + "dw_in")
        big_grad("w_in", l, gw_in)
        if dist:
            small.setdefault("b_in", [None] * DEPTH)[l] = g["b_in"].reshape(-1)
        else:
            for k, v in g.items():
                small.setdefault(k, [None] * DEPTH)[l] = v.reshape(p[k].shape[1:])
        if l > 0:
            dr2_below, dg_below, db_below, _ = _hosted(
                _mm, grad_rider(l, "dxs"), land_grads, dh, big_w["w_in"][l], trans_b=True, res=dr1, res_scale=ALPHA,
                ln_bwd=(saved[l - 1]["r2"], _row(p["ln2_g"][l - 1])), name=n + "dxs")
            from_above = (dr2_below, dg_below, db_below)
        else:
            dx = _hosted(_mm, grad_rider(l, "dxs"), land_grads, dh, big_w["w_in"][l], trans_b=True, res=dr1,
                         res_scale=ALPHA, name=n + "dxs")

    keys, rd = grad_rider(0, "ln_in_bwd")
    if rd is None:
        grad_x, dgi, dbi, _ = _ln_bwd(x, dx, _row(p["ln_in_g"]), name="ln_in_bwd")
    else:
        (grad_x, dgi, dbi, _), routs = _ln_bwd(x, dx, _row(p["ln_in_g"]), name="ln_in_bwd", rider=rd)
        land_grads(keys, routs)
    out = {k: jnp.stack(v, axis=0) for k, v in small.items()}
    out["ln_in_g"], out["ln_in_b"] = dgi.reshape(-1), dbi.reshape(-1)
    return loss, grad_x, out, ((recv, ready) if dist else big_g)


def kernel(x, mem, ln_in_g, ln_in_b, w_in, b_in, s5_lam_re, s5_lam_im, s5_log_dt, s5_b_re, s5_b_im, s5_c_re, s5_c_im, s5_d, s5_w_glu, s5_b_glu, cv_w, cv_b, cv_gn_g, cv_gn_b, cv_w_pw, cv_b_pw, lru_conv_w, lru_conv_b, lru_w_r, lru_b_r, lru_w_i, lru_b_i, lru_lam, attn_w_kv, w_out, b_out, ln1_g, ln1_b, ffn_w_up, ffn_conv_w, ffn_conv_b, ffn_w_down, ln2_g, ln2_b, loss_target, m_ln_in_g, m_ln_in_b, m_w_in, m_b_in, m_s5_lam_re, m_s5_lam_im, m_s5_log_dt, m_s5_b_re, m_s5_b_im, m_s5_c_re, m_s5_c_im, m_s5_d, m_s5_w_glu, m_s5_b_glu, m_cv_w, m_cv_b, m_cv_gn_g, m_cv_gn_b, m_cv_w_pw, m_cv_b_pw, m_lru_conv_w, m_lru_conv_b, m_lru_w_r, m_lru_b_r, m_lru_w_i, m_lru_b_i, m_lru_lam, m_attn_w_kv, m_w_out, m_b_out, m_ln1_g, m_ln1_b, m_ffn_w_up, m_ffn_conv_w, m_ffn_conv_b, m_ffn_w_down, m_ln2_g, m_ln2_b, v_ln_in_g, v_ln_in_b, v_w_in, v_b_in, v_s5_lam_re, v_s5_lam_im, v_s5_log_dt, v_s5_b_re, v_s5_b_im, v_s5_c_re, v_s5_c_im, v_s5_d, v_s5_w_glu, v_s5_b_glu, v_cv_w, v_cv_b, v_cv_gn_g, v_cv_gn_b, v_cv_w_pw, v_cv_b_pw, v_lru_conv_w, v_lru_conv_b, v_lru_w_r, v_lru_b_r, v_lru_w_i, v_lru_b_i, v_lru_lam, v_attn_w_kv, v_w_out, v_b_out, v_ln1_g, v_ln1_b, v_ffn_w_up, v_ffn_conv_w, v_ffn_conv_b, v_ffn_w_down, v_ln2_g, v_ln2_b):
    args = locals()
    w = {n: args[n] for n in WEIGHTS}
    mom = {n: args["m_" + n] for n in WEIGHTS}
    var = {n: args["v_" + n] for n in WEIGHTS}

    shards = {n: w[n].astype(BF16) for n in BIG}
    half_rows = shards["ffn_w_up"].shape[1] // 2
    shards["ffn_w_up#a"] = [shards["ffn_w_up"][0, :half_rows]]
    shards["ffn_w_up#b"] = [shards["ffn_w_up"][0, half_rows:]]
    shards["small_pack"] = [_pack([w[n] for n in SMALL_SHARDED], F32)]
    small_shapes = [w[n].shape for n in SMALL_SHARDED]

    def unpack_small(gathered):
        out = {n: _join_shards(st, SHARDED[n]) for n, st in zip(SMALL_SHARDED, _unpack(gathered, small_shapes, lead=True))}
        for n in ("s5_w_glu", "cv_w_pw"):
            out[n] = out[n].astype(BF16)
        return out

    big_w = {n: [None] * DEPTH for n in BIG}
    p = {n: w[n] for n in REPLICATED}
    p["b_in"] = _perm_in_cols(p["b_in"])

    loss, grad_x, g_small, (recv, ready) = _local_step(x[0], mem[0], loss_target[0], p, big_w, shards, unpack_small)
    loss = lax.psum(loss, ("x", "y", "c"))

    g_small["b_in"] = _perm_in_cols(g_small["b_in"], inverse=True)
    left = list(ready)
    rider = _Rider([ready[k][0] for k in left] + [_pack([g_small[n] for n in REP_LAST], F32)],
                   [ready[k][1] for k in left] + ["all"])
    got = _exchange(rider, name="exchange_grads")
    for k, r in zip(left, got):
        recv[k] = r

    res = [dict(), dict(), dict(), dict()]
    for n in BIG:
        outs = None
        for l in range(DEPTH):
            outs = _adamw_layer(recv[(n, l)], w[n], mom[n], var[n], l, outs, name=f"adamw_{n}_l{l}")
        for kind in range(4):
            res[kind][n] = outs[kind]
    for names, key, tag in ((SMALL_SHARDED, "ssh", "adamw_small_sharded"), (REP_LAYERED, "rep", "adamw_replicated")):
        gstack = jnp.concatenate([recv[(key, l)] for l in range(DEPTH)], axis=1)
        packs = [_pack_layers([t[n] for n in names], F32) for t in (w, mom, var)]
        rows = packs[0].shape[1]
        outs = _adamw(gstack, *[pk.reshape(DEPTH * rows, PACK_COLS) for pk in packs], name=tag)
        for kind in range(4):
            for n, a in zip(names, _unpack_layers(outs[kind].reshape(DEPTH, rows, PACK_COLS), [w[n].shape for n in names])):
                res[kind][n] = a
    outs = _adamw(got[len(left)], _pack([w[n] for n in REP_LAST], F32), _pack([mom[n] for n in REP_LAST], F32),
                  _pack([var[n] for n in REP_LAST], F32), name="adamw_last")
    for kind in range(4):
        for n, a in zip(REP_LAST, _unpack(outs[kind], [w[n].shape for n in REP_LAST])):
            res[kind][n] = a
    return (loss, grad_x[None], *[res[0][n] for n in WEIGHTS], *[res[1][n] for n in WEIGHTS],
            *[res[2][n] for n in WEIGHTS], *[res[3][n] for n in WEIGHTS])
```

```python
import math

import jax
import jax.numpy as jnp
from jax import lax
from jax.experimental import pallas as pl
from jax.experimental.pallas import tpu as pltpu

F32 = jnp.float32
BF16 = jnp.bfloat16

D_MODEL = 1024
DEPTH = 2
D_GROUP = 256
N_IN_COLS = 6 * D_GROUP
S5_GROUPS = 16
S5_CH = 16
S5_STATE = 64
CONV_WIDTH = 31
GN_GROUPS = 4
LRU_HEADS = 4
LRU_CONV_WIDTH = 4
LRU_C = 8.0
ATTN_HEADS = 4
ATTN_HEAD_DIM = 64
D_FF = 2816
FFN_CONV_WIDTH = 3
ALPHA = (2 * DEPTH) ** 0.25
LN_EPS = 1e-5
ADAM_LR, ADAM_B1, ADAM_B2, ADAM_EPS, ADAM_WD, ADAM_STEP = 0.001, 0.9, 0.999, 1e-08, 0.01, 10

N_DEV = 8
N_PEERS = N_DEV - 1
LANE = 128
SUBLANE = 8
VMEM_LIMIT = 56 * 1024 * 1024
PACK_COLS = 1024
PACK_ROW_BLOCK = 256
MM_ROW_TILE = 1024
MM_COL_CAP = 1408
MM_K_CAP = 1536
SEQ_TILE = 512

SHARDED = {
    "w_in": 2, "s5_w_glu": 1, "cv_w": 2, "cv_w_pw": 1, "lru_conv_w": 2, "attn_w_kv": 1,
    "w_out": 1, "ffn_w_up": 2, "ffn_conv_w": 2, "ffn_w_down": 1,
}
BIG = ("w_in", "attn_w_kv", "w_out", "ffn_w_up", "ffn_w_down")
SMALL_SHARDED = ("s5_w_glu", "cv_w", "cv_w_pw", "lru_conv_w", "ffn_conv_w")
WEIGHTS = ['ln_in_g', 'ln_in_b', 'w_in', 'b_in', 's5_lam_re', 's5_lam_im', 's5_log_dt', 's5_b_re', 's5_b_im',
           's5_c_re', 's5_c_im', 's5_d', 's5_w_glu', 's5_b_glu', 'cv_w', 'cv_b', 'cv_gn_g', 'cv_gn_b', 'cv_w_pw',
           'cv_b_pw', 'lru_conv_w', 'lru_conv_b', 'lru_w_r', 'lru_b_r', 'lru_w_i', 'lru_b_i', 'lru_lam',
           'attn_w_kv', 'w_out', 'b_out', 'ln1_g', 'ln1_b', 'ffn_w_up', 'ffn_conv_w', 'ffn_conv_b', 'ffn_w_down',
           'ln2_g', 'ln2_b']
REPLICATED = [n for n in WEIGHTS if n not in SHARDED]
REP_LAST = ("ln_in_g", "ln_in_b", "b_in")
REP_LAYERED = [n for n in REPLICATED if n not in REP_LAST]

COL_CV_V, COL_CV_G, COL_LRU_G, COL_LRU_X, COL_S5, COL_Q = range(6)
IN_PERM = (1, 2, 3, 4, 0, 5)
MIX_S5, MIX_CV, MIX_LRU, MIX_ATTN = range(4)


_ANY = pl.BlockSpec(memory_space=pl.ANY)
_MESH = pl.DeviceIdType.MESH


def _cparams(n_axes):
    return pltpu.CompilerParams(dimension_semantics=("arbitrary",) * n_axes, vmem_limit_bytes=VMEM_LIMIT)


def _pick(n, cap):
    if n <= cap:
        return n
    best = None
    for t in range(LANE, cap + 1, LANE):
        if n % t == 0:
            best = t
    assert best is not None, (n, cap)
    return best


def _pick_rows(n, cap):
    best = None
    for t in range(SUBLANE, min(n, cap) + 1, SUBLANE):
        if n % t == 0:
            best = t
    assert best is not None, (n, cap)
    return best


def _full_spec(arr):
    nd = arr.ndim
    return pl.BlockSpec(arr.shape, lambda *_: (0,) * nd)


def _dot(a, b):
    return lax.dot_general(a.astype(BF16), b.astype(BF16), (((1,), (0,)), ((), ())), preferred_element_type=F32)


def _dot_nt(a, b):
    return lax.dot_general(a.astype(BF16), b.astype(BF16), (((1,), (1,)), ((), ())), preferred_element_type=F32)


def _dot_tn(a, b):
    return lax.dot_general(a.astype(BF16), b.astype(BF16), (((0,), (0,)), ((), ())), preferred_element_type=F32)


def _dot_hi(a, b):
    b = b.astype(BF16)
    a1 = a.astype(BF16)
    r1 = a - a1.astype(F32)
    a2 = r1.astype(BF16)
    a3 = (r1 - a2.astype(F32)).astype(BF16)
    return _dot(a1, b) + _dot(a2, b) + _dot(a3, b)


def _colsum(x):
    return jnp.sum(x, axis=0, keepdims=True)


def _sigmoid(x):
    return 1.0 / (1.0 + jnp.exp(-x))


_GELU_K = math.sqrt(2.0 / math.pi)
_GELU_C = 0.044715


def _gelu(x):
    t = jnp.tanh(_GELU_K * (x + _GELU_C * x * x * x))
    return 0.5 * x * (1.0 + t)


def _gelu_and_grad(x):
    x2 = x * x
    t = jnp.tanh(_GELU_K * (x + _GELU_C * x2 * x))
    g = 0.5 * x * (1.0 + t)
    dg = 0.5 * (1.0 + t) + 0.5 * x * (1.0 - t * t) * (_GELU_K * (1.0 + 3.0 * _GELU_C * x2))
    return g, dg


def _neg_expm1(x):
    series = x * (1.0 + x * (0.5 + x * (1.0 / 6.0 + x * (1.0 / 24.0 + x * (1.0 / 120.0)))))
    return -jnp.where(jnp.abs(x) < 0.1, series, jnp.exp(x) - 1.0)


def _seq_tile(s, want):
    t = min(s, want)
    assert s % t == 0
    return t


class _Rider:
    def __init__(self, srcs, kinds):
        self.srcs, self.kinds = list(srcs), list(kinds)
        self.n = len(self.srcs)

    def out_shapes(self):
        shapes = []
        for x, kind in zip(self.srcs, self.kinds):
            if kind == "lead":
                shp = x.shape
            elif kind == "rows":
                shp = (N_DEV, x.shape[0] // N_DEV) + x.shape[1:]
            else:
                shp = (N_DEV,) + x.shape
            shapes.append(jax.ShapeDtypeStruct(shp, x.dtype))
        return shapes

    def scratch(self):
        return [pltpu.SemaphoreType.DMA((self.n * N_PEERS,)), pltpu.SemaphoreType.DMA((self.n * N_PEERS,)),
                pltpu.SemaphoreType.DMA((self.n,))]

    def _copies(self, x_refs, out_refs, sems):
        send_sems, recv_sems, local_sems = sems
        mx, my, mc = lax.axis_index("x"), lax.axis_index("y"), lax.axis_index("c")
        my_id = 4 * mx + 2 * my + mc

        def piece(i, dev):
            if self.kinds[i] == "lead":
                return x_refs[i].at[dev]
            if self.kinds[i] == "rows":
                r = x_refs[i].shape[0] // N_DEV
                return x_refs[i].at[pl.ds(pl.multiple_of(dev * r, SUBLANE), r)]
            return x_refs[i]

        mine = [pltpu.make_async_copy(piece(i, my_id), out_refs[i].at[my_id], local_sems.at[i]) for i in range(self.n)]
        copies = []
        for k in range(1, N_DEV):
            px, py, pc = mx ^ ((k >> 2) & 1), my ^ ((k >> 1) & 1), mc ^ (k & 1)
            for i in range(self.n):
                copies.append(pltpu.make_async_remote_copy(
                    src_ref=piece(i, 4 * px + 2 * py + pc), dst_ref=out_refs[i].at[my_id],
                    send_sem=send_sems.at[i * N_PEERS + k - 1], recv_sem=recv_sems.at[i * N_PEERS + k - 1],
                    device_id=(px, py, pc), device_id_type=_MESH))
        return mine, copies

    def start(self, x_refs, out_refs, sems):
        mine, copies = self._copies(x_refs, out_refs, sems)
        for cp in mine + copies:
            cp.start()

    def wait(self, x_refs, out_refs, sems):
        mine, copies = self._copies(x_refs, out_refs, sems)
        for cp in copies:
            cp.wait_recv()
        for cp in copies:
            cp.wait_send()
        for cp in mine:
            cp.wait()


def _call(body, *, grid, ins, in_specs, outs, out_specs, scratch=(), aliases=None, name, rider=None):
    n_axes = len(grid)
    common = dict(grid=grid, input_output_aliases=aliases or {}, compiler_params=_cparams(n_axes), name=name)
    if rider is None:
        res = pl.pallas_call(body, in_specs=list(in_specs), out_specs=list(out_specs), out_shape=list(outs),
                             scratch_shapes=list(scratch), **common)(*ins)
        return list(res), []
    n_in, n_out, n_scr, nr = len(ins), len(outs), len(scratch), rider.n

    def wrapped(*refs):
        pos = [0]

        def take(k):
            part = refs[pos[0]:pos[0] + k]
            pos[0] += k
            return part

        a_in, r_in, a_out, r_out, a_scr, sems = take(n_in), take(nr), take(n_out), take(nr), take(n_scr), take(3)
        first = last = None
        for ax in range(n_axes):
            pid = pl.program_id(ax)
            f, l = pid == 0, pid == grid[ax] - 1
            first = f if first is None else jnp.logical_and(first, f)
            last = l if last is None else jnp.logical_and(last, l)

        @pl.when(first)
        def _():
            rider.start(r_in, r_out, sems)

        body(*a_in, *a_out, *a_scr)

        @pl.when(last)
        def _():
            rider.wait(r_in, r_out, sems)

    res = pl.pallas_call(
        wrapped, in_specs=list(in_specs) + [_ANY] * nr, out_specs=list(out_specs) + [_ANY] * nr,
        out_shape=list(outs) + rider.out_shapes(), scratch_shapes=list(scratch) + rider.scratch(), **common)(*ins, *rider.srcs)
    return list(res[:n_out]), list(res[n_out:])


def _block_mask(n_blocks, block_rows, block_cols):
    r = jnp.arange(n_blocks * block_rows) // block_rows
    c = jnp.arange(n_blocks * block_cols) // block_cols
    return (r[:, None] == c[None, :]).astype(F32)


def _mm(a, b, *, bias=None, res=None, res_scale=1.0, trans_b=False, out_dtype=F32, ln=None, ln_bwd=None, name,
        rider=None):
    m, kdim = a.shape
    n = b.shape[0] if trans_b else b.shape[1]
    tm = _seq_tile(m, MM_ROW_TILE)
    tn = _pick(n, MM_COL_CAP)
    tk = _pick(kdim, MM_K_CAP)
    nk = kdim // tk
    has_bias, has_res, has_ln, has_lnb = bias is not None, res is not None, ln is not None, ln_bwd is not None
    assert not (has_ln or has_lnb) or tn == n
    assert not (has_ln and has_lnb)

    def body(*refs):
        a_ref, b_ref = refs[0], refs[1]
        pos = 2
        bias_ref = res_ref = g_ref = beta_ref = x_ref = None
        if has_bias:
            bias_ref = refs[pos]
            pos += 1
        if has_res:
            res_ref = refs[pos]
            pos += 1
        if has_ln:
            g_ref, beta_ref = refs[pos], refs[pos + 1]
            pos += 2
        if has_lnb:
            x_ref, g_ref = refs[pos], refs[pos + 1]
            pos += 2
        o_ref = refs[pos]
        pos += 1
        if has_ln:
            x_ref = refs[pos]
            pos += 1
        if has_lnb:
            dg_ref, db_ref, ds_ref = refs[pos:pos + 3]
            pos += 3
        acc_ref = refs[pos]
        k = pl.program_id(2)

        @pl.when(k == 0)
        def _():
            acc_ref[...] = jnp.zeros_like(acc_ref)

        if has_lnb:
            @pl.when(jnp.logical_and(pl.program_id(0) == 0, k == 0))
            def _():
                dg_ref[...] = jnp.zeros_like(dg_ref)
                db_ref[...] = jnp.zeros_like(db_ref)
                ds_ref[...] = jnp.zeros_like(ds_ref)

        if trans_b:
            acc_ref[...] += _dot_nt(a_ref[...], b_ref[...])
        else:
            acc_ref[...] += _dot(a_ref[...], b_ref[...])

        @pl.when(k == nk - 1)
        def _():
            r = acc_ref[...]
            if has_bias:
                r = r + bias_ref[...]
            if has_res:
                r = r + res_scale * res_ref[...]
            if has_lnb:
                x = x_ref[...]
                xc = x - jnp.mean(x, axis=1, keepdims=True)
                rstd = lax.rsqrt(jnp.mean(xc * xc, axis=1, keepdims=True) + LN_EPS)
                xh = xc * rstd
                dxh = r * g_ref[...]
                dx = rstd * (dxh - jnp.mean(dxh, axis=1, keepdims=True) - xh * jnp.mean(dxh * xh, axis=1, keepdims=True))
                o_ref[...] = dx
                dg_ref[...] += _colsum(r * xh)
                db_ref[...] += _colsum(r)
                ds_ref[...] += _colsum(dx)
            else:
                o_ref[...] = r.astype(out_dtype)
            if has_ln:
                xc = r - jnp.mean(r, axis=1, keepdims=True)
                var = jnp.mean(xc * xc, axis=1, keepdims=True)
                x_ref[...] = xc * lax.rsqrt(var + LN_EPS) * g_ref[...] + beta_ref[...]

    ins = [a, b]
    in_specs = [pl.BlockSpec((tm, tk), lambda i, j, k: (i, k)),
                pl.BlockSpec((tn, tk), lambda i, j, k: (j, k)) if trans_b
                else pl.BlockSpec((tk, tn), lambda i, j, k: (k, j))]
    if has_bias:
        ins.append(bias)
        in_specs.append(pl.BlockSpec((1, tn), lambda i, j, k: (0, j)))
    if has_res:
        ins.append(res)
        in_specs.append(pl.BlockSpec((tm, tn), lambda i, j, k: (i, j)))
    tile = pl.BlockSpec((tm, tn), lambda i, j, k: (i, j))
    vec = pl.BlockSpec((1, tn), lambda i, j, k: (0, j))
    out_shapes, out_specs = [jax.ShapeDtypeStruct((m, n), out_dtype)], [tile]
    if has_ln:
        ins += list(ln)
        in_specs += [vec] * 2
        out_shapes.append(jax.ShapeDtypeStruct((m, n), F32))
        out_specs.append(tile)
    if has_lnb:
        ins += list(ln_bwd)
        in_specs += [tile, vec]
        out_shapes += [jax.ShapeDtypeStruct((1, n), F32)] * 3
        out_specs += [vec] * 3
    outs, routs = _call(
        body, grid=(m // tm, n // tn, nk), ins=ins, in_specs=in_specs, outs=out_shapes, out_specs=out_specs,
        scratch=[pltpu.VMEM((tm, tn), F32)], name=name, rider=rider)
    out = tuple(outs) if (has_ln or has_lnb) else outs[0]
    return out if rider is None else (out, routs)


def _mm_tn(a, b, *, colsum=False, out_dtype=F32, dev_cols=None, name, rider=None):
    s, ka = a.shape
    nb = b.shape[1]
    ts = _seq_tile(s, SEQ_TILE)
    tka = _pick(ka, MM_COL_CAP)
    tnb = _pick(nb, MM_COL_CAP)
    nk = s // ts
    assert not colsum or tka == ka
    per_tile = 1 if dev_cols is None else tnb // dev_cols
    assert dev_cols is None or tnb == per_tile * dev_cols

    def body(a_ref, b_ref, o_ref, *rest):
        cs_ref = rest[0] if colsum else None
        acc_ref = rest[-1]
        k = pl.program_id(2)

        @pl.when(k == 0)
        def _():
            acc_ref[...] = jnp.zeros_like(acc_ref)
            if colsum:
                cs_ref[...] = jnp.zeros_like(cs_ref)

        bv = b_ref[...]
        acc_ref[...] += _dot_tn(a_ref[...], bv)
        if colsum:
            cs_ref[...] += _colsum(bv.astype(F32))

        @pl.when(k == nk - 1)
        def _():
            if dev_cols is None:
                o_ref[...] = acc_ref[...].astype(out_dtype)
            else:
                for d in range(per_tile):
                    o_ref[d] = acc_ref[:, d * dev_cols:(d + 1) * dev_cols].astype(out_dtype)

    if dev_cols is None:
        main_shape, main_spec = (ka, nb), pl.BlockSpec((tka, tnb), lambda i, j, k: (i, j))
    else:
        main_shape = (nb // dev_cols, ka, dev_cols)
        main_spec = pl.BlockSpec((per_tile, tka, dev_cols), lambda i, j, k: (j, i, 0))
    outs, routs = _call(
        body, grid=(ka // tka, nb // tnb, nk), ins=[a, b],
        in_specs=[pl.BlockSpec((ts, tka), lambda i, j, k: (k, i)), pl.BlockSpec((ts, tnb), lambda i, j, k: (k, j))],
        outs=[jax.ShapeDtypeStruct(main_shape, out_dtype)] + ([jax.ShapeDtypeStruct((1, nb), F32)] if colsum else []),
        out_specs=[main_spec] + ([pl.BlockSpec((1, tnb), lambda i, j, k: (0, j))] if colsum else []),
        scratch=[pltpu.VMEM((tka, tnb), F32)], name=name, rider=rider)
    out = tuple(outs) if colsum else outs[0]
    return out if rider is None else (out, routs)


def _ln_fwd(r, g, b, *, name, rider=None):
    s, d = r.shape
    ts = _seq_tile(s, SEQ_TILE)

    def body(r_ref, g_ref, b_ref, o_ref):
        x = r_ref[...]
        mu = jnp.mean(x, axis=1, keepdims=True)
        xc = x - mu
        var = jnp.mean(xc * xc, axis=1, keepdims=True)
        o_ref[...] = xc * lax.rsqrt(var + LN_EPS) * g_ref[...] + b_ref[...]

    (out,), routs = _call(
        body, grid=(s // ts,), ins=[r, g, b],
        in_specs=[pl.BlockSpec((ts, d), lambda i: (i, 0)), _full_spec(g), _full_spec(b)],
        out_specs=[pl.BlockSpec((ts, d), lambda i: (i, 0))], outs=[jax.ShapeDtypeStruct((s, d), F32)],
        name=name, rider=rider)
    return out if rider is None else (out, routs)


def _ln_bwd(r, dy, g, *, name, rider=None):
    s, d = r.shape
    ts = _seq_tile(s, SEQ_TILE)

    def body(r_ref, dy_ref, g_ref, dr_ref, dg_ref, db_ref, ds_ref):
        @pl.when(pl.program_id(0) == 0)
        def _():
            dg_ref[...] = jnp.zeros_like(dg_ref)
            db_ref[...] = jnp.zeros_like(db_ref)
            ds_ref[...] = jnp.zeros_like(ds_ref)

        x = r_ref[...]
        dy = dy_ref[...]
        mu = jnp.mean(x, axis=1, keepdims=True)
        xc = x - mu
        var = jnp.mean(xc * xc, axis=1, keepdims=True)
        rstd = lax.rsqrt(var + LN_EPS)
        xh = xc * rstd
        dxh = dy * g_ref[...]
        m1 = jnp.mean(dxh, axis=1, keepdims=True)
        m2 = jnp.mean(dxh * xh, axis=1, keepdims=True)
        dr = rstd * (dxh - m1 - xh * m2)
        dr_ref[...] = dr
        dg_ref[...] += _colsum(dy * xh)
        db_ref[...] += _colsum(dy)
        ds_ref[...] += _colsum(dr)

    vec = jax.ShapeDtypeStruct((1, d), F32)
    vspec = pl.BlockSpec((1, d), lambda i: (0, 0))
    outs, routs = _call(
        body, grid=(s // ts,), ins=[r, dy, g],
        in_specs=[pl.BlockSpec((ts, d), lambda i: (i, 0)), pl.BlockSpec((ts, d), lambda i: (i, 0)), _full_spec(g)],
        out_specs=[pl.BlockSpec((ts, d), lambda i: (i, 0)), vspec, vspec, vspec],
        outs=[jax.ShapeDtypeStruct((s, d), F32), vec, vec, vec], name=name, rider=rider)
    return outs if rider is None else (outs, routs)


def _loss_ln_bwd(r, g, b, target, *, name):
    s, d = r.shape
    ts = _seq_tile(s, SEQ_TILE)

    def body(r_ref, g_ref, b_ref, t_ref, dr_ref, dg_ref, db_ref, l_ref):
        @pl.when(pl.program_id(0) == 0)
        def _():
            dg_ref[...] = jnp.zeros_like(dg_ref)
            db_ref[...] = jnp.zeros_like(db_ref)
            l_ref[...] = jnp.zeros_like(l_ref)

        x = r_ref[...]
        gam = g_ref[...]
        xc = x - jnp.mean(x, axis=1, keepdims=True)
        var = jnp.mean(xc * xc, axis=1, keepdims=True)
        rstd = lax.rsqrt(var + LN_EPS)
        xh = xc * rstd
        e = xh * gam + b_ref[...] - t_ref[...]
        part = jnp.sum(jnp.sum(e * e, axis=1, keepdims=True), axis=0, keepdims=True) * (0.5 / d)
        l_ref[...] += jnp.broadcast_to(part, l_ref.shape)
        dy = e * (1.0 / d)
        dxh = dy * gam
        m1 = jnp.mean(dxh, axis=1, keepdims=True)
        m2 = jnp.mean(dxh * xh, axis=1, keepdims=True)
        dr_ref[...] = rstd * (dxh - m1 - xh * m2)
        dg_ref[...] += _colsum(dy * xh)
        db_ref[...] += _colsum(dy)

    vec = jax.ShapeDtypeStruct((1, d), F32)
    vspec = pl.BlockSpec((1, d), lambda i: (0, 0))
    tile = pl.BlockSpec((ts, d), lambda i: (i, 0))
    return pl.pallas_call(
        body, grid=(s // ts,), in_specs=[tile, _full_spec(g), _full_spec(b), tile],
        out_specs=[tile, vspec, vspec, pl.BlockSpec((SUBLANE, LANE), lambda i: (0, 0))],
        out_shape=[jax.ShapeDtypeStruct((s, d), F32), vec, vec, jax.ShapeDtypeStruct((SUBLANE, LANE), F32)],
        compiler_params=_cparams(1), name=name)(r, g, b, target)


SCAN_CHUNK = 32


def _cscan_levels(bufs, apow_ref, t, pad, *, reverse):
    half = bufs[0].shape[1] // 2
    ch = min(SCAN_CHUNK, t)
    nlev = t.bit_length() - 1
    assert (1 << nlev) == t
    for k in range(nlev):
        d = 1 << k
        src, dst = bufs[k % 2], bufs[(k + 1) % 2]

        def chunk(c, carry, src=src, dst=dst, d=d, k=k):
            ar = apow_ref[k:k + 1, :half]
            ai = apow_ref[k:k + 1, half:]
            if reverse:
                ai = -ai
            r0 = pl.multiple_of(c * ch, ch)
            cur = src[pl.ds(pad + r0, ch), :]
            if d >= SUBLANE:
                off = pad + d if reverse else pad - d
                sh = src[pl.ds(off + r0, ch), :]
            elif reverse:
                blk = src[pl.ds(pad + r0, ch + SUBLANE), :]
                sh = pltpu.roll(blk, ch + SUBLANE - d, axis=0)[:ch, :]
            else:
                blk = src[pl.ds(pad - SUBLANE + r0, ch + SUBLANE), :]
                sh = pltpu.roll(blk, d, axis=0)[SUBLANE:, :]
            sre, sim = sh[:, :half], sh[:, half:]
            dst[pl.ds(pad + r0, ch), :half] = cur[:, :half] + ar * sre - ai * sim
            dst[pl.ds(pad + r0, ch), half:] = cur[:, half:] + ar * sim + ai * sre
            return carry

        lax.fori_loop(0, t // ch, chunk, 0)
    return nlev % 2


def _rscan_levels(abufs, bbufs, t, pad, *, reverse):
    nlev = t.bit_length() - 1
    assert (1 << nlev) == t
    for k in range(nlev):
        d = 1 << k
        asrc, adst = abufs[k % 2], abufs[(k + 1) % 2]
        bsrc, bdst = bbufs[k % 2], bbufs[(k + 1) % 2]
        off = pad + d if reverse else pad - d
        a = asrc[pad:pad + t, :]
        bdst[pad:pad + t, :] = a * bsrc[off:off + t, :] + bsrc[pad:pad + t, :]
        if k < nlev - 1:
            adst[pad:pad + t, :] = a * asrc[off:off + t, :]
    return nlev % 2


S5_CHUNK = 16
S5_SG = S5_GROUPS // 2
S5_SG_IN = 2 * S5_CHUNK * S5_CH
S5_SG_ST = 2 * S5_STATE


S5_HALF_SGS = S5_SG // 2
S5_HALF_IN = S5_HALF_SGS * S5_SG_IN


def _s5_perm():
    idx = jnp.arange(S5_HALF_IN)
    step, grp, chan = idx // LANE, (idx % LANE) // S5_CH, idx % S5_CH
    col = (grp // 2) * S5_SG_IN + (grp % 2) * (S5_CHUNK * S5_CH) + step * S5_CH + chan
    return (col[:, None] == idx[None, :]).astype(BF16)


def _s5_to_chunks(x, col_block, perm, *, name):
    s = x.shape[0]
    nb = s // S5_CHUNK

    def body(x_ref, perm_ref, o_ref):
        tok = jnp.concatenate([x_ref[pl.ds(t, nb, stride=S5_CHUNK), :].astype(BF16) for t in range(S5_CHUNK)], axis=1)
        grouped = _dot(tok, perm_ref[...]).astype(BF16)
        for k in range(S5_HALF_SGS):
            o_ref[k] = grouped[:, k * S5_SG_IN:(k + 1) * S5_SG_IN]

    return pl.pallas_call(
        body, grid=(2,),
        in_specs=[pl.BlockSpec((s, LANE), lambda h: (0, col_block + h)), _full_spec(perm)],
        out_specs=pl.BlockSpec((S5_HALF_SGS, nb, S5_SG_IN), lambda h: (h, 0, 0)),
        out_shape=jax.ShapeDtypeStruct((S5_SG, nb, S5_SG_IN), BF16),
        compiler_params=_cparams(1), name=name)(x, perm)


def _s5_from_chunks(y, perm, *, name):
    _, nb, _ = y.shape

    def body(y_ref, perm_ref, o_ref):
        grouped = jnp.concatenate([y_ref[k] for k in range(S5_HALF_SGS)], axis=1)
        hi = grouped.astype(BF16)
        lo = (grouped - hi.astype(F32)).astype(BF16)
        tok = _dot_nt(hi, perm_ref[...]) + _dot_nt(lo, perm_ref[...])
        for t in range(S5_CHUNK):
            o_ref[pl.ds(t, nb, stride=S5_CHUNK), :] = tok[:, t * LANE:(t + 1) * LANE]

    return pl.pallas_call(
        body, grid=(2,),
        in_specs=[pl.BlockSpec((S5_HALF_SGS, nb, S5_SG_IN), lambda h: (h, 0, 0)), _full_spec(perm)],
        out_specs=pl.BlockSpec((nb * S5_CHUNK, LANE), lambda h: (0, h)),
        out_shape=jax.ShapeDtypeStruct((nb * S5_CHUNK, D_GROUP), F32),
        compiler_params=_cparams(1), name=name)(y, perm)


def _s5_core_fwd(u2, m2, pre, pim, qre, qim, a16, *, name):
    sg, nb, nin = u2.shape
    st2 = 2 * S5_SG_ST
    pad = nb // 2

    def body(u_ref, m_ref, pre_ref, pim_ref, qre_ref, qim_ref, a_ref, y_ref, x_ref, buf0, buf1):
        @pl.when(pl.program_id(0) == 0)
        def _():
            buf0[0:pad, :] = jnp.zeros((pad, st2), F32)
            buf1[0:pad, :] = jnp.zeros((pad, st2), F32)

        u = u_ref[...]
        buf0[pad:pad + nb, :S5_SG_ST] = _dot(u, pre_ref[...])
        buf0[pad:pad + nb, S5_SG_ST:] = _dot(u, pim_ref[...])
        xbuf = (buf0, buf1)[_cscan_levels((buf0, buf1), a_ref, nb, pad, reverse=False)]
        x_ref[...] = xbuf[pad:pad + nb, :]
        xprev = xbuf[pad - 1:pad - 1 + nb, :]
        y_ref[...] = _dot(u, m_ref[...]) + _dot(xprev[:, :S5_SG_ST], qre_ref[...]) + _dot(xprev[:, S5_SG_ST:], qim_ref[...])

    ins = [u2, m2, pre, pim, qre, qim, a16]
    return pl.pallas_call(
        body, grid=(sg,), in_specs=[pl.BlockSpec((None,) + a.shape[1:], lambda i: (i, 0, 0)) for a in ins],
        out_specs=[pl.BlockSpec((None, nb, nin), lambda i: (i, 0, 0)), pl.BlockSpec((None, nb, st2), lambda i: (i, 0, 0))],
        out_shape=[jax.ShapeDtypeStruct((sg, nb, nin), F32), jax.ShapeDtypeStruct((sg, nb, st2), F32)],
        scratch_shapes=[pltpu.VMEM((pad + nb, st2), F32), pltpu.VMEM((pad + nb, st2), F32)],
        compiler_params=_cparams(1), name=name)(*ins)


def _s5_core_bwd(u2, dy2, x_all, m2, pre, pim, qre, qim, a16, *, name):
    sg, nb, nin = u2.shape
    half = S5_SG_ST
    st2 = 2 * half
    pad = nb // 2

    def body(u_ref, dy_ref, x_ref, m_ref, pre_ref, pim_ref, qre_ref, qim_ref, a_ref,
             du_ref, dm_ref, dpre_ref, dpim_ref, dqre_ref, dqim_ref, da_ref, buf2, buf3, xp):
        @pl.when(pl.program_id(0) == 0)
        def _():
            buf2[nb:nb + pad, :] = jnp.zeros((pad, st2), F32)
            buf3[nb:nb + pad, :] = jnp.zeros((pad, st2), F32)
            xp[0:SUBLANE, :] = jnp.zeros((SUBLANE, st2), F32)

        u = u_ref[...]
        dy = dy_ref[...]
        dm_ref[...] = _dot_tn(u, dy)
        xp[SUBLANE:SUBLANE + nb, :] = x_ref[...]
        xprev = xp[SUBLANE - 1:SUBLANE - 1 + nb, :]
        xre, xim = xprev[:, :half], xprev[:, half:]
        dqre_ref[...] = _dot_tn(xre, dy)
        dqim_ref[...] = _dot_tn(xim, dy)
        buf2[0:nb, :half] = _dot_nt(dy, qre_ref[...])
        buf2[0:nb, half:] = _dot_nt(dy, qim_ref[...])
        mbuf = (buf2, buf3)[_cscan_levels((buf2, buf3), a_ref, nb, 0, reverse=True)]
        lam = mbuf[1:1 + nb, :]
        lre, lim = lam[:, :half], lam[:, half:]
        dpre_ref[...] = _dot_tn(u, lre)
        dpim_ref[...] = _dot_tn(u, lim)
        du_ref[...] = _dot_nt(dy, m_ref[...]) + _dot_nt(lre, pre_ref[...]) + _dot_nt(lim, pim_ref[...])
        da_ref[:, :half] = _colsum(lre * xre + lim * xim)
        da_ref[:, half:] = _colsum(lim * xre - lre * xim)

    ins = [u2, dy2, x_all, m2, pre, pim, qre, qim, a16]
    outs = [jax.ShapeDtypeStruct((sg, nb, nin), F32)] + [jax.ShapeDtypeStruct(a.shape, F32) for a in (m2, pre, pim, qre, qim)] + \
           [jax.ShapeDtypeStruct((sg, 1, st2), F32)]
    return pl.pallas_call(
        body, grid=(sg,), in_specs=[pl.BlockSpec((None,) + a.shape[1:], lambda i: (i, 0, 0)) for a in ins],
        out_specs=[pl.BlockSpec((None,) + o.shape[1:], lambda i: (i, 0, 0)) for o in outs], out_shape=outs,
        scratch_shapes=[pltpu.VMEM((nb + pad, st2), F32), pltpu.VMEM((nb + pad, st2), F32),
                        pltpu.VMEM((SUBLANE + nb, st2), F32)],
        compiler_params=_cparams(1), name=name)(*ins)


def _s5_glu_fwd(y1, wglu, bglu, *, name, rider=None):
    s = y1.shape[0]
    t = _seq_tile(s, SEQ_TILE)

    def body(y1_ref, wglu_ref, bglu_ref, out_ref):
        y2 = _gelu(y1_ref[...])
        out_ref[...] = (y2 * _sigmoid(_dot(y2, wglu_ref[...]) + bglu_ref[...])).astype(BF16)

    return _call(
        body, grid=(s // t,), ins=[y1, wglu, bglu],
        in_specs=[pl.BlockSpec((t, D_GROUP), lambda i: (i, 0)), _full_spec(wglu), _full_spec(bglu)],
        out_specs=[pl.BlockSpec((t, D_GROUP), lambda i: (i, MIX_S5))], outs=[jax.ShapeDtypeStruct((s, D_MODEL), BF16)],
        name=name, rider=rider)


def _s5_glu_bwd(y1, dmix, wglu, bglu, *, name):
    s = y1.shape[0]
    t = _seq_tile(s, SEQ_TILE)

    def body(y1_ref, do_ref, wglu_ref, bglu_ref, dy1_ref, dwglu_ref, dbglu_ref):
        @pl.when(pl.program_id(0) == 0)
        def _():
            dwglu_ref[...] = jnp.zeros_like(dwglu_ref)
            dbglu_ref[...] = jnp.zeros_like(dbglu_ref)

        dout = do_ref[...]
        y2, dgelu = _gelu_and_grad(y1_ref[...])
        sg = _sigmoid(_dot(y2, wglu_ref[...]) + bglu_ref[...])
        dz = dout * y2 * sg * (1.0 - sg)
        dwglu_ref[...] += _dot_tn(y2, dz)
        dbglu_ref[...] += _colsum(dz)
        dy1_ref[...] = (dout * sg + _dot_nt(dz, wglu_ref[...])) * dgelu

    outs = [jax.ShapeDtypeStruct((s, D_GROUP), F32), jax.ShapeDtypeStruct((D_GROUP, D_GROUP), F32),
            jax.ShapeDtypeStruct((1, D_GROUP), F32)]
    return pl.pallas_call(
        body, grid=(s // t,),
        in_specs=[pl.BlockSpec((t, D_GROUP), lambda i: (i, 0)), pl.BlockSpec((t, D_GROUP), lambda i: (i, MIX_S5)),
                  _full_spec(wglu), _full_spec(bglu)],
        out_specs=[pl.BlockSpec((t, D_GROUP), lambda i: (i, 0)), _full_spec(outs[1]), _full_spec(outs[2])],
        out_shape=outs, compiler_params=_cparams(1), name=name)(y1, dmix, wglu, bglu)


def _pair_blockdiag(x):
    g, r, c = x.shape
    return jnp.tile(x.reshape(g // 2, 2 * r, c), (1, 1, 2)) * _block_mask(2, r, c)


def _s5_chunk_map(lam_re, lam_im, log_dt, b_re, b_im, c_re, c_im, d_skip):
    g, n, c, lc = S5_GROUPS, S5_STATE, S5_CH, S5_CHUNK
    dt = jnp.exp(log_dt)[:, None]
    mag, ang = lam_re * dt, lam_im * dt
    j = jnp.arange(lc + 1, dtype=F32)[:, None, None]
    pw_mag = jnp.exp(j * mag)
    pw_re, pw_im = pw_mag * jnp.cos(j * ang), pw_mag * jnp.sin(j * ang)
    a_re, a_im = pw_re[1], pw_im[1]
    den = lam_re * lam_re + lam_im * lam_im
    n_re = a_re - 1.0
    k_re = (n_re * lam_re + a_im * lam_im) / den
    k_im = (a_im * lam_re - n_re * lam_im) / den
    bb_re = k_re[..., None] * b_re - k_im[..., None] * b_im
    bb_im = k_re[..., None] * b_im + k_im[..., None] * b_re
    e_re = pw_re[:lc, :, :, None] * bb_re - pw_im[:lc, :, :, None] * bb_im
    e_im = pw_re[:lc, :, :, None] * bb_im + pw_im[:lc, :, :, None] * bb_re
    kern = jnp.einsum("gdn,jgnc->jgdc", c_re, e_re) - jnp.einsum("gdn,jgnc->jgdc", c_im, e_im)
    lags = jnp.pad(jnp.transpose(kern, (1, 3, 0, 2)), ((0, 0), (0, 0), (lc - 1, 0), (0, 0)))
    lags = lags.reshape(g, c, (2 * lc - 1) * c)
    m = jnp.stack([lags[:, :, (lc - 1 - s) * c:(2 * lc - 1 - s) * c] for s in range(lc)], axis=1).reshape(g, lc * c, lc * c)
    skip = jnp.tile(d_skip.reshape(g, 1, c), (1, lc, 1)).reshape(g, lc * c)
    m = m + jnp.eye(lc * c, dtype=F32)[None] * skip[:, None, :]
    p_re = jnp.transpose(e_re[::-1], (1, 0, 3, 2)).reshape(g, lc * c, n)
    p_im = jnp.transpose(e_im[::-1], (1, 0, 3, 2)).reshape(g, lc * c, n)
    f_re = c_re[None] * pw_re[1:, :, None, :] - c_im[None] * pw_im[1:, :, None, :]
    f_im = c_re[None] * pw_im[1:, :, None, :] + c_im[None] * pw_re[1:, :, None, :]
    q_re = jnp.transpose(f_re, (1, 3, 0, 2)).reshape(g, n, lc * c)
    q_im = -jnp.transpose(f_im, (1, 3, 0, 2)).reshape(g, n, lc * c)
    a16 = jnp.concatenate([pw_re[lc].reshape(S5_SG, 1, S5_SG_ST), pw_im[lc].reshape(S5_SG, 1, S5_SG_ST)], axis=2)
    return (_pair_blockdiag(m), _pair_blockdiag(p_re), _pair_blockdiag(p_im), _pair_blockdiag(q_re),
            _pair_blockdiag(q_im), a16)


def _s5_a16_powers(a16, nlev):
    half = S5_SG_ST
    re, im = a16[:, :, :half], a16[:, :, half:]
    rows = []
    for _ in range(nlev):
        rows.append(jnp.concatenate([re, im], axis=2))
        re, im = re * re - im * im, 2.0 * re * im
    n_rows = -(-nlev // SUBLANE) * SUBLANE
    rows += [jnp.zeros_like(rows[0])] * (n_rows - nlev)
    return lax.stop_gradient(jnp.concatenate(rows, axis=1))


CV_TILE = 256
CV_PAD = 32
CV_CHUNK = 64


def _shifted_copies(buf, shifted, rows):
    n = rows - SUBLANE
    for s in range(1, SUBLANE):
        shifted[s - 1, 0:n, :] = buf[s:s + n, :]


def _window(buf, shifted, o, ch):
    q, s = divmod(o, SUBLANE)
    if s == 0:
        return buf[o:o + ch, :]
    return shifted[s - 1, q * SUBLANE:q * SUBLANE + ch, :]


def _gn_stats(c, mavg):
    mu = _dot_hi(c, mavg)
    cen = c - mu
    var = _dot_hi(cen * cen, mavg)
    rstd = lax.rsqrt(var + LN_EPS)
    return cen * rstd, rstd


def _cv_fwd(h_in, cw, cb, gng, gnb, mavg, wpw, bpw, mix, *, name, rider=None):
    s = h_in.shape[0]
    t = _seq_tile(s, CV_TILE)
    ch = min(CV_CHUNK, t)

    def body(v_ref, g_ref, cw_ref, cb_ref, gng_ref, gnb_ref, mavg_ref, wpw_ref, bpw_ref, _mix_in, out_ref, c_ref, xpad,
             shifted):
        @pl.when(pl.program_id(0) == 0)
        def _():
            xpad[0:CV_PAD, :] = jnp.zeros((CV_PAD, D_GROUP), F32)

        xpad[CV_PAD:CV_PAD + t, :] = v_ref[...] * _sigmoid(g_ref[...])
        _shifted_copies(xpad, shifted, t + CV_PAD)
        for r0 in range(0, t, ch):
            acc = jnp.broadcast_to(cb_ref[...], (ch, D_GROUP))
            for k in range(CONV_WIDTH):
                o = CV_PAD - (CONV_WIDTH - 1) + k + r0
                acc = acc + cw_ref[k:k + 1, :] * _window(xpad, shifted, o, ch)
            c_ref[r0:r0 + ch, :] = acc
        xpad[0:CV_PAD, :] = xpad[t:t + CV_PAD, :]
        xn, _ = _gn_stats(c_ref[...], mavg_ref[...])
        gn = xn * gng_ref[...] + gnb_ref[...]
        out_ref[...] = (_dot(gn * _sigmoid(gn), wpw_ref[...]) + bpw_ref[...]).astype(BF16)

    ins = [h_in, h_in, cw, cb, gng, gnb, mavg, wpw, bpw, mix]
    in_specs = [pl.BlockSpec((t, D_GROUP), lambda i: (i, COL_CV_V)), pl.BlockSpec((t, D_GROUP), lambda i: (i, COL_CV_G))] + \
               [_full_spec(a) for a in ins[2:9]] + [_ANY]
    return _call(
        body, grid=(s // t,), ins=ins, in_specs=in_specs,
        out_specs=[pl.BlockSpec((t, D_GROUP), lambda i: (i, MIX_CV)), pl.BlockSpec((t, D_GROUP), lambda i: (i, 0))],
        outs=[jax.ShapeDtypeStruct((s, D_MODEL), BF16), jax.ShapeDtypeStruct((s, D_GROUP), F32)],
        aliases={9: 0},
        scratch=[pltpu.VMEM((CV_PAD + t, D_GROUP), F32), pltpu.VMEM((SUBLANE - 1, CV_PAD + t, D_GROUP), F32)],
        name=name, rider=rider)


def _cv_bwd(h_in, c, dmix, cw, gng, gnb, mavg, wpw, *, name, rider=None):
    s = h_in.shape[0]
    t = _seq_tile(s, CV_TILE)
    nt = s // t
    ch = min(CV_CHUNK, t)

    def body(v_ref, g_ref, c_ref, do_ref, cw_ref, gng_ref, gnb_ref, mavg_ref, wpw_ref,
             dvg_ref, dwpw_ref, dcw_ref, dbpw_ref, dgg_ref, dgb_ref, dcb_ref, dcpad, hgbuf, shifted):
        @pl.when(pl.program_id(0) == 0)
        def _():
            dcpad[t:t + CV_PAD, :] = jnp.zeros((CV_PAD, D_GROUP), F32)
            for r in (dwpw_ref, dcw_ref, dbpw_ref, dgg_ref, dgb_ref, dcb_ref):
                r[...] = jnp.zeros_like(r)

        mavg = mavg_ref[...]
        xn, rstd = _gn_stats(c_ref[...], mavg)
        gg = gng_ref[...]
        gn = xn * gg + gnb_ref[...]
        sg = _sigmoid(gn)
        dout = do_ref[...]
        dwpw_ref[...] += _dot_tn(gn * sg, dout)
        dbpw_ref[...] += _colsum(dout)
        dgn = _dot_nt(dout, wpw_ref[...]) * (sg * (1.0 + gn * (1.0 - sg)))
        dgg_ref[...] += _colsum(dgn * xn)
        dgb_ref[...] += _colsum(dgn)
        dxn = dgn * gg
        dc = rstd * (dxn - _dot_hi(dxn, mavg) - xn * _dot_hi(dxn * xn, mavg))
        dcb_ref[...] += _colsum(dc)
        dcpad[0:t, :] = dc

        v = v_ref[...]
        sgm = _sigmoid(g_ref[...])
        hgbuf[...] = v * sgm
        _shifted_copies(dcpad, shifted, t + CV_PAD)
        for r0 in range(0, t, ch):
            hg = hgbuf[r0:r0 + ch, :]
            acc = jnp.zeros((ch, D_GROUP), F32)
            for k in range(CONV_WIDTH):
                o = (CONV_WIDTH - 1) - k + r0
                sh = _window(dcpad, shifted, o, ch)
                acc = acc + cw_ref[k:k + 1, :] * sh
                dcw_ref[k:k + 1, :] += _colsum(hg * sh)
            hgbuf[r0:r0 + ch, :] = acc
        dcpad[t:t + CV_PAD, :] = dcpad[0:CV_PAD, :]
        dhg = hgbuf[...]
        dvg_ref[:, :D_GROUP] = dhg * sgm
        dvg_ref[:, D_GROUP:] = dhg * v * sgm * (1.0 - sgm)

    def rev(col):
        return lambda i: (nt - 1 - i, col)

    ins = [h_in, h_in, c, dmix, cw, gng, gnb, mavg, wpw]
    in_specs = [pl.BlockSpec((t, D_GROUP), rev(COL_CV_V)), pl.BlockSpec((t, D_GROUP), rev(COL_CV_G)),
                pl.BlockSpec((t, D_GROUP), rev(0)), pl.BlockSpec((t, D_GROUP), rev(MIX_CV))] + [_full_spec(a) for a in ins[4:]]
    vec = jax.ShapeDtypeStruct((1, D_GROUP), F32)
    outs = [jax.ShapeDtypeStruct((s, N_IN_COLS), F32),
            jax.ShapeDtypeStruct((D_GROUP, D_GROUP), F32), jax.ShapeDtypeStruct((CV_PAD, D_GROUP), F32), vec, vec, vec, vec]
    out_specs = [pl.BlockSpec((t, 2 * D_GROUP), rev(COL_CV_V // 2))] + [_full_spec(o) for o in outs[1:]]
    return _call(
        body, grid=(nt,), ins=ins, in_specs=in_specs, out_specs=out_specs, outs=outs,
        scratch=[pltpu.VMEM((t + CV_PAD, D_GROUP), F32), pltpu.VMEM((t, D_GROUP), F32),
                 pltpu.VMEM((SUBLANE - 1, t + CV_PAD, D_GROUP), F32)], name=name, rider=rider)


LRU_TILE = 256


def _lru_gates(xc, wr_ref, br_ref, wi_ref, bi_ref, sp_ref):
    r = _sigmoid(_dot(xc, wr_ref[...]) + br_ref[...])
    i = _sigmoid(_dot(xc, wi_ref[...]) + bi_ref[...])
    log_a = -LRU_C * r * sp_ref[...]
    a = jnp.exp(log_a)
    m = jnp.sqrt(_neg_expm1(2.0 * log_a))
    return r, i, a, m


def _lru_fwd(h_in, lcw, lcb, wr, br, wi, bi, sp, mix, *, name, rider=None):
    s = h_in.shape[0]
    t = _seq_tile(s, LRU_TILE)
    pad = max(t // 2, SUBLANE)

    def body(xg_ref, xr_ref, lcw_ref, lcb_ref, wr_ref, br_ref, wi_ref, bi_ref, sp_ref, _mix_in,
             out_ref, xc_ref, h_ref, xpad, a0, a1, b0, b1, carry):
        @pl.when(pl.program_id(0) == 0)
        def _():
            xpad[0:SUBLANE, :] = jnp.zeros((SUBLANE, D_GROUP), F32)
            for bf in (a0, a1, b0, b1):
                bf[0:pad, :] = jnp.zeros((pad, D_GROUP), F32)
            carry[...] = jnp.zeros_like(carry)

        xpad[SUBLANE:SUBLANE + t, :] = xr_ref[...]
        xc = jnp.broadcast_to(lcb_ref[...], (t, D_GROUP))
        for k in range(LRU_CONV_WIDTH):
            o = SUBLANE - (LRU_CONV_WIDTH - 1) + k
            xc = xc + lcw_ref[k:k + 1, :] * xpad[o:o + t, :]
        xpad[0:SUBLANE, :] = xpad[t:t + SUBLANE, :]
        xc_ref[...] = xc
        _, i, a, m = _lru_gates(xc, wr_ref, br_ref, wi_ref, bi_ref, sp_ref)
        a0[pad:pad + t, :] = a
        b0[pad:pad + t, :] = m * (i * xc)
        b0[pad:pad + 1, :] += a0[pad:pad + 1, :] * carry[0:1, :]
        fin = _rscan_levels((a0, a1), (b0, b1), t, pad, reverse=False)
        hbuf = (b0, b1)[fin]
        carry[0:1, :] = hbuf[pad + t - 1:pad + t, :]
        h = hbuf[pad:pad + t, :]
        h_ref[...] = h
        out_ref[...] = (h * _gelu(xg_ref[...])).astype(BF16)

    ins = [h_in, h_in, lcw, lcb, wr, br, wi, bi, sp, mix]
    row = pl.BlockSpec((t, D_GROUP), lambda i: (i, 0))
    in_specs = [pl.BlockSpec((t, D_GROUP), lambda i: (i, COL_LRU_G)), pl.BlockSpec((t, D_GROUP), lambda i: (i, COL_LRU_X))] + \
               [_full_spec(a) for a in ins[2:9]] + [_ANY]
    return _call(
        body, grid=(s // t,), ins=ins, in_specs=in_specs,
        out_specs=[pl.BlockSpec((t, D_GROUP), lambda i: (i, MIX_LRU)), row, row],
        outs=[jax.ShapeDtypeStruct((s, D_MODEL), BF16)] + [jax.ShapeDtypeStruct((s, D_GROUP), F32)] * 2,
        aliases={9: 0},
        scratch=[pltpu.VMEM((SUBLANE + t, D_GROUP), F32)] + [pltpu.VMEM((pad + t, D_GROUP), F32)] * 4 +
                [pltpu.VMEM((SUBLANE, D_GROUP), F32)],
        name=name, rider=rider)


def _lru_bwd(h_in, xc_all, h_all, dmix, lcw, wr, br, wi, bi, sp, dh_all, *, name):
    s = h_in.shape[0]
    t = _seq_tile(s, LRU_TILE)
    nt = s // t
    pad = max(t // 2, SUBLANE)
    tb = t // SUBLANE

    def body(xg_ref, xr_ref, xc_ref, h_ref, hprev_ref, do_ref, lcw_ref, wr_ref, br_ref, wi_ref, bi_ref, sp_ref, _dh_in,
             dgr_ref, dwr_ref, dwi_ref, dlcw_ref, dbr_ref, dbi_ref, dsp_ref, dlcb_ref,
             a0, a1, b0, b1, hp, dxpad, carry):
        pid = pl.program_id(0)

        @pl.when(pid == 0)
        def _():
            for bf in (a0, a1, b0, b1):
                bf[pad + t:pad + t + pad, :] = jnp.zeros((pad, D_GROUP), F32)
            dxpad[t:t + SUBLANE, :] = jnp.zeros((SUBLANE, D_GROUP), F32)
            carry[...] = jnp.zeros_like(carry)
            for r in (dwr_ref, dwi_ref, dlcw_ref, dbr_ref, dbi_ref, dsp_ref, dlcb_ref):
                r[...] = jnp.zeros_like(r)

        xc = xc_ref[...]
        h = h_ref[...]
        dout = do_ref[...]
        gate, dgate = _gelu_and_grad(xg_ref[...])
        dgr_ref[:, :D_GROUP] = dout * h * dgate
        r, i, a, m = _lru_gates(xc, wr_ref, br_ref, wi_ref, bi_ref, sp_ref)

        a0[pad:pad + t, :] = a
        b0[pad:pad + t, :] = dout * gate
        b0[pad + t - 1:pad + t, :] += carry[0:1, :]
        a1[pad:pad + t, :] = a0[pad + 1:pad + 1 + t, :]
        fin = _rscan_levels((a1, a0), (b0, b1), t, pad, reverse=True)
        lam = (b0, b1)[fin][pad:pad + t, :]
        carry[0:1, :] = a[0:1, :] * lam[0:1, :]

        is_first = pid == nt - 1
        hp[0:SUBLANE, :] = jnp.where(is_first, 0.0, hprev_ref[...])
        hp[SUBLANE:SUBLANE + t, :] = h
        hprev = hp[SUBLANE - 1:SUBLANE - 1 + t, :]

        ix = i * xc
        dmm = lam * ix
        dix = lam * m
        da = lam * hprev - dmm * (a / m)
        dlog_a = da * a
        dr = dlog_a * (-LRU_C * sp_ref[...])
        dsp_ref[...] += _colsum(dlog_a * (-LRU_C * r))
        dpr = dr * r * (1.0 - r)
        dpi = dix * xc * i * (1.0 - i)
        dbr_ref[...] += _colsum(dpr)
        dbi_ref[...] += _colsum(dpi)
        dwr_ref[...] += _dot_tn(xc, dpr)
        dwi_ref[...] += _dot_tn(xc, dpi)
        dxc = dix * i + _dot_nt(dpr, wr_ref[...]) + _dot_nt(dpi, wi_ref[...])
        dlcb_ref[...] += _colsum(dxc)

        dxpad[0:t, :] = dxc
        xr = xr_ref[...]
        dxr = jnp.zeros((t, D_GROUP), F32)
        for k in range(LRU_CONV_WIDTH):
            o = (LRU_CONV_WIDTH - 1) - k
            sh = dxpad[o:o + t, :]
            dxr = dxr + lcw_ref[k:k + 1, :] * sh
            dlcw_ref[k:k + 1, :] += _colsum(xr * sh)
        dxpad[t:t + SUBLANE, :] = dxpad[0:SUBLANE, :]
        dgr_ref[:, D_GROUP:] = dxr

    def rev(col):
        return lambda i: (nt - 1 - i, col)

    ins = [h_in, h_in, xc_all, h_all, h_all, dmix, lcw, wr, br, wi, bi, sp, dh_all]
    in_specs = [pl.BlockSpec((t, D_GROUP), rev(COL_LRU_G)), pl.BlockSpec((t, D_GROUP), rev(COL_LRU_X)),
                pl.BlockSpec((t, D_GROUP), rev(0)), pl.BlockSpec((t, D_GROUP), rev(0)),
                pl.BlockSpec((SUBLANE, D_GROUP), lambda i: (jnp.maximum((nt - 1 - i) * tb - 1, 0), 0)),
                pl.BlockSpec((t, D_GROUP), rev(MIX_LRU))] + [_full_spec(a) for a in ins[6:12]] + [_ANY]
    vec = jax.ShapeDtypeStruct((1, D_GROUP), F32)
    mat = jax.ShapeDtypeStruct((D_GROUP, D_GROUP), F32)
    outs = [jax.ShapeDtypeStruct((s, N_IN_COLS), F32), mat, mat, jax.ShapeDtypeStruct((SUBLANE, D_GROUP), F32),
            vec, vec, vec, vec]
    out_specs = [pl.BlockSpec((t, 2 * D_GROUP), rev(COL_LRU_G // 2))] + [_full_spec(o) for o in outs[1:]]
    return pl.pallas_call(
        body, grid=(nt,), in_specs=in_specs, out_specs=out_specs, out_shape=outs, input_output_aliases={12: 0},
        scratch_shapes=[pltpu.VMEM((pad + t + pad, D_GROUP), F32)] * 4 +
                       [pltpu.VMEM((SUBLANE + t, D_GROUP), F32), pltpu.VMEM((t + SUBLANE, D_GROUP), F32),
                        pltpu.VMEM((SUBLANE, D_GROUP), F32)],
        compiler_params=_cparams(1), name=name)(*ins)


def _blockdiag(w):
    h, d, _ = w.shape
    return jnp.tile(w.reshape(h * d, d), (1, h)) * _block_mask(h, d, d)


ATTN_TILE = 512
ATTN_SCALE = ATTN_HEAD_DIM ** -0.5


def _attn_big(kv):
    m = kv.shape[0]
    kbig = jnp.tile(kv[:, :D_GROUP].T, (1, ATTN_HEADS)) * _block_mask(ATTN_HEADS, ATTN_HEAD_DIM, m)
    vbig = jnp.tile(kv[:, D_GROUP:], (ATTN_HEADS, 1)) * _block_mask(ATTN_HEADS, m, ATTN_HEAD_DIM)
    return kbig, vbig


def _attn_probs(q, kbig_ref, m):
    sc = _dot(q, kbig_ref[...]) * ATTN_SCALE
    ps = []
    for h in range(ATTN_HEADS):
        sh = sc[:, h * m:(h + 1) * m]
        e = jnp.exp(sh - jnp.max(sh, axis=1, keepdims=True))
        ps.append(e / jnp.sum(e, axis=1, keepdims=True))
    return ps


def _attn_fwd(h_in, kbig, vbig, mix, *, name):
    s = h_in.shape[0]
    t = _seq_tile(s, ATTN_TILE)
    m = kbig.shape[1] // ATTN_HEADS

    def body(q_ref, kbig_ref, vbig_ref, _mix_in, o_ref):
        ps = _attn_probs(q_ref[...], kbig_ref, m)
        o_ref[...] = _dot(jnp.concatenate(ps, axis=1), vbig_ref[...]).astype(BF16)

    return pl.pallas_call(
        body, grid=(s // t,),
        in_specs=[pl.BlockSpec((t, D_GROUP), lambda i: (i, COL_Q)), _full_spec(kbig), _full_spec(vbig), _ANY],
        out_specs=pl.BlockSpec((t, D_GROUP), lambda i: (i, MIX_ATTN)),
        out_shape=jax.ShapeDtypeStruct((s, D_MODEL), BF16), input_output_aliases={3: 0},
        compiler_params=_cparams(1), name=name)(h_in, kbig, vbig, mix)


def _attn_bwd(h_in, dmix, kbig, vbig, du_s5, dh_all, *, name):
    s = h_in.shape[0]
    t = _seq_tile(s, ATTN_TILE)
    m = kbig.shape[1] // ATTN_HEADS

    def body(q_ref, do_ref, kbig_ref, vbig_ref, dus5_ref, _dh_in, dpair_ref, dk_ref, dv_ref):
        @pl.when(pl.program_id(0) == 0)
        def _():
            dk_ref[...] = jnp.zeros_like(dk_ref)
            dv_ref[...] = jnp.zeros_like(dv_ref)

        q = q_ref[...]
        dout = do_ref[...]
        ps = _attn_probs(q, kbig_ref, m)
        dp = _dot_nt(dout, vbig_ref[...])
        dss = []
        for h in range(ATTN_HEADS):
            dph = dp[:, h * m:(h + 1) * m]
            dss.append(ps[h] * (dph - jnp.sum(dph * ps[h], axis=1, keepdims=True)))
        ds = (jnp.concatenate(dss, axis=1) * ATTN_SCALE).astype(BF16)
        dv_ref[...] += _dot_tn(jnp.concatenate(ps, axis=1), dout)
        dpair_ref[:, :D_GROUP] = dus5_ref[...]
        dpair_ref[:, D_GROUP:] = _dot_nt(ds, kbig_ref[...])
        dk_ref[...] += _dot_tn(q, ds)

    assert (COL_S5, COL_Q) == (4, 5)
    outs = [jax.ShapeDtypeStruct((s, N_IN_COLS), F32), jax.ShapeDtypeStruct(kbig.shape, F32),
            jax.ShapeDtypeStruct(vbig.shape, F32)]
    return pl.pallas_call(
        body, grid=(s // t,),
        in_specs=[pl.BlockSpec((t, D_GROUP), lambda i: (i, COL_Q)), pl.BlockSpec((t, D_GROUP), lambda i: (i, MIX_ATTN)),
                  _full_spec(kbig), _full_spec(vbig), pl.BlockSpec((t, D_GROUP), lambda i: (i, 0)), _ANY],
        out_specs=[pl.BlockSpec((t, 2 * D_GROUP), lambda i: (i, COL_S5 // 2)), _full_spec(outs[1]), _full_spec(outs[2])],
        out_shape=outs, input_output_aliases={5: 0},
        compiler_params=_cparams(1), name=name)(h_in, dmix, kbig, vbig, du_s5, dh_all)


FFN_TILE = 128
FFN_COL_CHUNK = 256
FFN_ROW_CHUNK = 64


def _ffn_conv(pad_ref, w_ref, b_ref, r0, ch, c0):
    cc = FFN_COL_CHUNK
    acc = jnp.broadcast_to(b_ref[:, c0:c0 + cc], (ch, cc))
    for k in range(FFN_CONV_WIDTH):
        o = SUBLANE - (FFN_CONV_WIDTH - 1) + k + r0
        acc = acc + w_ref[k:k + 1, c0:c0 + cc] * pad_ref[o:o + ch, c0:c0 + cc]
    return acc


def _ffn_gate_fwd(u, fcw, fcb, *, name, rider=None):
    s = u.shape[0]
    t = _seq_tile(s, FFN_TILE)
    ch = min(FFN_ROW_CHUNK, t)
    cc = FFN_COL_CHUNK

    def body(u_ref, w_ref, b_ref, o_ref, uc_ref, upad):
        @pl.when(pl.program_id(0) == 0)
        def _():
            upad[0:SUBLANE, :] = jnp.zeros((SUBLANE, 2 * D_FF), F32)

        upad[SUBLANE:SUBLANE + t, :] = u_ref[...].astype(F32)
        for c0 in range(0, D_FF, cc):
            for r0 in range(0, t, ch):
                val = _ffn_conv(upad, w_ref, b_ref, r0, ch, c0)
                gt = _ffn_conv(upad, w_ref, b_ref, r0, ch, c0 + D_FF)
                o_ref[r0:r0 + ch, c0:c0 + cc] = (val * _gelu(gt)).astype(BF16)
                uc_ref[r0:r0 + ch, c0:c0 + cc] = val.astype(BF16)
                uc_ref[r0:r0 + ch, c0 + D_FF:c0 + D_FF + cc] = gt.astype(BF16)
        upad[0:SUBLANE, :] = upad[t:t + SUBLANE, :]

    return _call(
        body, grid=(s // t,), ins=[u, fcw, fcb],
        in_specs=[pl.BlockSpec((t, 2 * D_FF), lambda i: (i, 0)), _full_spec(fcw), _full_spec(fcb)],
        out_specs=[pl.BlockSpec((t, D_FF), lambda i: (i, 0)), pl.BlockSpec((t, 2 * D_FF), lambda i: (i, 0))],
        outs=[jax.ShapeDtypeStruct((s, D_FF), BF16), jax.ShapeDtypeStruct((s, 2 * D_FF), BF16)],
        scratch=[pltpu.VMEM((SUBLANE + t, 2 * D_FF), F32)], name=name, rider=rider)


def _ffn_gate_bwd(u, uc, dh, fcw, *, name, rider=None):
    s = u.shape[0]
    t = _seq_tile(s, FFN_TILE)
    nt = s // t
    ch = min(FFN_ROW_CHUNK, t)
    cc = FFN_COL_CHUNK

    def body(u_ref, uc_ref, dh_ref, w_ref, du_ref, dw_ref, db_ref, dpad):
        @pl.when(pl.program_id(0) == 0)
        def _():
            dpad[t:t + SUBLANE, :] = jnp.zeros((SUBLANE, 2 * D_FF), F32)
            dw_ref[...] = jnp.zeros_like(dw_ref)
            db_ref[...] = jnp.zeros_like(db_ref)

        for c0 in range(0, D_FF, cc):
            for r0 in range(0, t, ch):
                val = uc_ref[r0:r0 + ch, c0:c0 + cc].astype(F32)
                gt = uc_ref[r0:r0 + ch, c0 + D_FF:c0 + D_FF + cc].astype(F32)
                gl, dgl = _gelu_and_grad(gt)
                d = dh_ref[r0:r0 + ch, c0:c0 + cc].astype(F32)
                dpad[r0:r0 + ch, c0:c0 + cc] = d * gl
                dpad[r0:r0 + ch, c0 + D_FF:c0 + D_FF + cc] = d * val * dgl
        for c0 in range(0, 2 * D_FF, cc):
            dbs = jnp.zeros((1, cc), F32)
            dws = [jnp.zeros((1, cc), F32) for _ in range(FFN_CONV_WIDTH)]
            for r0 in range(0, t, ch):
                x = u_ref[r0:r0 + ch, c0:c0 + cc].astype(F32)
                acc = jnp.zeros((ch, cc), F32)
                for k in range(FFN_CONV_WIDTH):
                    o = (FFN_CONV_WIDTH - 1) - k + r0
                    sh = dpad[o:o + ch, c0:c0 + cc]
                    acc = acc + w_ref[k:k + 1, c0:c0 + cc] * sh
                    dws[k] = dws[k] + _colsum(x * sh)
                    if k == FFN_CONV_WIDTH - 1:
                        dbs = dbs + _colsum(sh)
                du_ref[r0:r0 + ch, c0:c0 + cc] = acc.astype(BF16)
            db_ref[:, c0:c0 + cc] += dbs
            for k in range(FFN_CONV_WIDTH):
                dw_ref[k:k + 1, c0:c0 + cc] += dws[k]
        dpad[t:t + SUBLANE, :] = dpad[0:SUBLANE, :]

    outs = [jax.ShapeDtypeStruct((s, 2 * D_FF), BF16), jax.ShapeDtypeStruct((SUBLANE, 2 * D_FF), F32),
            jax.ShapeDtypeStruct((1, 2 * D_FF), F32)]
    return _call(
        body, grid=(nt,), ins=[u, uc, dh, fcw],
        in_specs=[pl.BlockSpec((t, 2 * D_FF), lambda i: (nt - 1 - i, 0)),
                  pl.BlockSpec((t, 2 * D_FF), lambda i: (nt - 1 - i, 0)),
                  pl.BlockSpec((t, D_FF), lambda i: (nt - 1 - i, 0)), _full_spec(fcw)],
        out_specs=[pl.BlockSpec((t, 2 * D_FF), lambda i: (nt - 1 - i, 0)), _full_spec(outs[1]), _full_spec(outs[2])],
        outs=outs, scratch=[pltpu.VMEM((t + SUBLANE, 2 * D_FF), F32)], name=name, rider=rider)


def _adamw_body(g_ref, w_ref, m_ref, v_ref, go_ref, d_ref, mo_ref, vo_ref):
    inv_b1 = 1.0 - ADAM_B1 ** ADAM_STEP
    inv_b2 = 1.0 - ADAM_B2 ** ADAM_STEP
    g = g_ref[0].astype(F32)
    for dev in range(1, N_DEV):
        g = g + g_ref[dev].astype(F32)
    go_ref[...] = g
    mn = ADAM_B1 * m_ref[...] + (1.0 - ADAM_B1) * g
    vn = ADAM_B2 * v_ref[...] + (1.0 - ADAM_B2) * (g * g)
    mo_ref[...] = mn
    vo_ref[...] = vn
    d_ref[...] = -ADAM_LR * ((mn / inv_b1) / (jnp.sqrt(vn / inv_b2) + ADAM_EPS) + ADAM_WD * w_ref[...])


def _adamw(gstack, w, m, v, *, name):
    _, r, c = gstack.shape
    tr = _pick_rows(r, PACK_ROW_BLOCK)

    def body(*refs):
        _adamw_body(*refs)

    blk = pl.BlockSpec((tr, c), lambda i: (i, 0))
    sh = jax.ShapeDtypeStruct((r, c), F32)
    return pl.pallas_call(
        body, grid=(r // tr,),
        in_specs=[pl.BlockSpec((N_DEV, tr, c), lambda i: (0, i, 0)), blk, blk, blk],
        out_specs=[blk] * 4, out_shape=[sh] * 4,
        compiler_params=_cparams(1), name=name)(gstack, w, m, v)


def _adamw_layer(gstack, w, m, v, layer, into, *, name):
    n_layers, r, c = w.shape
    tr = _pick_rows(r, PACK_ROW_BLOCK)

    def body(g_ref, w_ref, m_ref, v_ref, *rest):
        _adamw_body(g_ref, w_ref, m_ref, v_ref, *rest[-4:])

    blk = pl.BlockSpec((None, tr, c), lambda i: (layer, i, 0))
    sh = jax.ShapeDtypeStruct((n_layers, r, c), F32)
    into = list(into or [])
    return pl.pallas_call(
        body, grid=(r // tr,),
        in_specs=[pl.BlockSpec((N_DEV, tr, c), lambda i: (0, i, 0)), blk, blk, blk] + [_ANY] * len(into),
        out_specs=[blk] * 4, out_shape=[sh] * 4, input_output_aliases={4 + k: k for k in range(len(into))},
        compiler_params=_cparams(1), name=name)(gstack, w, m, v, *into)


def _exchange(rider, *, name):
    n = rider.n

    def body(*refs):
        x_refs, out_refs, sems = refs[:n], refs[n:2 * n], refs[2 * n:]
        rider.start(x_refs, out_refs, sems)
        rider.wait(x_refs, out_refs, sems)

    return pl.pallas_call(
        body, in_specs=[_ANY] * n, out_specs=[_ANY] * n, out_shape=rider.out_shapes(),
        scratch_shapes=rider.scratch(), name=name)(*rider.srcs)


def _pack_rows(n):
    rows = -(-n // PACK_COLS)
    return -(-rows // SUBLANE) * SUBLANE


def _pack(arrs, dtype):
    flat = jnp.concatenate([a.reshape(-1).astype(dtype) for a in arrs])
    rows = _pack_rows(flat.shape[0])
    flat = jnp.pad(flat, (0, rows * PACK_COLS - flat.shape[0]))
    return flat.reshape(rows, PACK_COLS)


def _pack_lead(arrs, dtype):
    flat = jnp.concatenate([a.reshape(N_DEV, -1).astype(dtype) for a in arrs], axis=1)
    rows = _pack_rows(flat.shape[1])
    flat = jnp.pad(flat, ((0, 0), (0, rows * PACK_COLS - flat.shape[1])))
    return flat.reshape(N_DEV, rows, PACK_COLS)


def _pack_layers(arrs, dtype):
    n_layers = arrs[0].shape[0]
    flat = jnp.concatenate([a.reshape(n_layers, -1).astype(dtype) for a in arrs], axis=1)
    rows = _pack_rows(flat.shape[1])
    flat = jnp.pad(flat, ((0, 0), (0, rows * PACK_COLS - flat.shape[1])))
    return flat.reshape(n_layers, rows, PACK_COLS)


def _unpack_layers(packed, shapes):
    flat = packed.reshape(packed.shape[0], -1)
    out, pos = [], 0
    for sh in shapes:
        n = math.prod(sh[1:])
        out.append(flat[:, pos:pos + n].reshape(sh))
        pos += n
    return out


def _unpack(packed, shapes, lead=False):
    flat = packed.reshape(N_DEV, -1) if lead else packed.reshape(-1)
    out, pos = [], 0
    for sh in shapes:
        n = math.prod(sh)
        out.append(flat[:, pos:pos + n].reshape((N_DEV,) + tuple(sh)) if lead else flat[pos:pos + n].reshape(sh))
        pos += n
    return out


def _join_shards(stacked, axis):
    return jnp.concatenate([stacked[d] for d in range(N_DEV)], axis=axis)


def _split_shards(full, axis):
    return jnp.stack(jnp.split(full, N_DEV, axis=axis), axis=0)


def _perm_in_cols(a, inverse=False):
    blocks = jnp.split(a, 6, axis=-1)
    if inverse:
        order = [IN_PERM.index(j) for j in range(6)]
    else:
        order = list(IN_PERM)
    return jnp.concatenate([blocks[j] for j in order], axis=-1)


def _row(v):
    return v.reshape(1, -1)


def _pad_rows(w, rows):
    return jnp.pad(w, ((0, rows - w.shape[0]), (0, 0)))


def _gn_avg_matrix():
    return _block_mask(GN_GROUPS, D_GROUP // GN_GROUPS, D_GROUP // GN_GROUPS) / (D_GROUP // GN_GROUPS)


def _layer_params(p, l):
    q = {}
    s5_mats, q["s5_vjp"] = jax.vjp(_s5_chunk_map, p["s5_lam_re"][l], p["s5_lam_im"][l], p["s5_log_dt"][l],
                                   p["s5_b_re"][l], p["s5_b_im"][l], p["s5_c_re"][l], p["s5_c_im"][l], p["s5_d"][l])
    q["s5_mats"] = [m.astype(BF16) for m in s5_mats[:5]]
    q["s5_a16"] = s5_mats[5]
    (q["wr"], q["wi"]), q["lru_w_vjp"] = jax.vjp(lambda r, i: (_blockdiag(r), _blockdiag(i)), p["lru_w_r"][l], p["lru_w_i"][l])
    q["wr"], q["wi"] = q["wr"].astype(BF16), q["wi"].astype(BF16)
    q["sp"], q["sp_vjp"] = jax.vjp(lambda lam: _row(jax.nn.softplus(-lam)), p["lru_lam"][l])
    return q


ROW_PARTS = ("a", "b", "c", "d")
WEIGHT_RIDES = {(0, "ln_in_fwd"): [("w_in", 0)],
                (0, "inproj"): [("attn_w_kv", 0), ("w_out", 0), ("small_pack", 0)],
                (0, "cv_fwd"): [("ffn_w_up#a", 0), ("ffn_w_up#b", 0)],
                (0, "lru_fwd"): [("ffn_w_up#c", 0)],
                (0, "outproj"): [("ffn_w_up#d", 0)],
                (0, "ffn_up"): [("ffn_w_down", 0), ("w_in", 1), ("attn_w_kv", 1), ("w_out", 1)],
                (0, "ffn_gate_fwd"): [("ffn_w_up", 1)],
                (0, "ffn_down"): [("ffn_w_down", 1)]}
GRAD_RIDES = {(1, "ffn_gate_bwd"): [("ffn_w_down", 1)],
              (0, "dw_down"): [("w_out", 1), ("attn_w_kv", 1), ("w_in", 1)],
              (0, "dhff"): [("rep", 1), ("ssh", 1)],
              (0, "ffn_gate_bwd"): [("ffn_w_up", 1)],
              (0, "dw_up"): [("ffn_w_down", 0)],
              (0, "dx1"): [("ffn_w_up", 0)],
              (0, "cv_bwd"): [("w_out", 0)],
              (0, "dw_in"): [("attn_w_kv", 0), ("ssh", 0), ("rep", 0)],
              (0, "dxs"): [("w_in", 0)]}


def _join_cols(pieces, *, name):
    n_dev, k, c = pieces[0].shape
    assert (2 * c) % LANE == 0 and all(p.shape == pieces[0].shape for p in pieces)
    n_p = len(pieces)

    def body(*refs):
        o_ref = refs[n_p]
        for i in range(n_p):
            @pl.when(pl.program_id(0) == i)
            def _(i=i):
                o_ref[...] = jnp.concatenate([refs[i][0], refs[i][1]], axis=1)

    return pl.pallas_call(
        body, grid=(n_p, n_dev // 2),
        in_specs=[pl.BlockSpec((2, k, c), lambda i, j, p=p: (jnp.where(i == p, j, 0), 0, 0)) for p in range(n_p)],
        out_specs=pl.BlockSpec((k, 2 * c), lambda i, j: (i, j)),
        out_shape=jax.ShapeDtypeStruct((n_p * k, n_dev * c), pieces[0].dtype),
        compiler_params=_cparams(2), name=name)(*pieces)


def _split_cols(full, *, name):
    k, n = full.shape
    c = n // N_DEV
    assert (2 * c) % LANE == 0

    def body(x_ref, o_ref):
        o_ref[0] = x_ref[:, :c]
        o_ref[1] = x_ref[:, c:]

    return pl.pallas_call(
        body, grid=(N_DEV // 2,), in_specs=[pl.BlockSpec((k, 2 * c), lambda j: (0, j))],
        out_specs=pl.BlockSpec((2, k, c), lambda j: (j, 0, 0)), out_shape=jax.ShapeDtypeStruct((N_DEV, k, c), full.dtype),
        compiler_params=_cparams(1), name=name)(full)


def _assemble_weight(n, pieces, layer=0):
    if SHARDED[n] == 2:
        full = _join_cols(pieces, name=f"l{layer}_join_{n}")
        return _perm_in_cols(full) if n == "w_in" else full
    (gathered,) = pieces
    return gathered.reshape(-1, gathered.shape[-1])


def _grad_source(n, g, layer=0):
    g = g.astype(BF16)
    if SHARDED[n] == 2:
        if n == "w_in":
            g = _perm_in_cols(g, inverse=True)
        return _split_cols(g, name=f"l{layer}_split_d{n}"), "lead"
    return g, "rows"


def _hosted(fn, keys_rider, land, *args, **kw):
    keys, rider = keys_rider
    if rider is None:
        return fn(*args, **kw)
    out, routs = fn(*args, rider=rider, **kw)
    land(keys, routs)
    return out


def _local_step(x, mem, target, p, big_w, shards=None, unpack_small=None):
    dist = shards is not None
    gdt = BF16 if dist else F32
    small, saved = {}, []
    big_g, ready, recv = {}, {}, {}
    mavg = _gn_avg_matrix()
    s5_perm = _s5_perm()

    def weight_rider(l, host):
        keys = WEIGHT_RIDES.get((l, host), []) if dist else []
        return keys, (_Rider([shards[n][ll] for n, ll in keys], ["all"] * len(keys)) if keys else None)

    halves = {}

    def land_weights(keys, routs):
        for (n, ll), r in zip(keys, routs):
            if n == "small_pack":
                p.update(unpack_small(r))
            elif "#" in n:
                base = n.split("#")[0]
                halves[(n, ll)] = r
                parts = [halves.get((base + "#" + tag, ll)) for tag in ROW_PARTS]
                if all(part is not None for part in parts):
                    big_w[base][ll] = _assemble_weight(base, parts, ll)
            else:
                big_w[n][ll] = _assemble_weight(n, [r], ll)

    def grad_rider(l, host):
        keys = [k for k in GRAD_RIDES.get((l, host), []) if k in ready] if dist else []
        return keys, (_Rider([ready[k][0] for k in keys], [ready[k][1] for k in keys]) if keys else None)

    def land_grads(keys, routs):
        for k, r in zip(keys, routs):
            recv[k] = r
            del ready[k]

    def big_grad(n, l, g):
        if dist:
            ready[(n, l)] = _grad_source(n, g, l)
        else:
            big_g[(n, l)] = g

    xs = _hosted(_ln_fwd, weight_rider(0, "ln_in_fwd"), land_weights, x, _row(p["ln_in_g"]), _row(p["ln_in_b"]),
                 name="ln_in_fwd")
    for l in range(DEPTH):
        q = _layer_params(p, l)
        n = f"l{l}_"
        hin = _hosted(_mm, weight_rider(l, "inproj"), land_weights, xs, big_w["w_in"][l], bias=_row(p["b_in"][l]),
                      name=n + "inproj")
        nb = hin.shape[0] // S5_CHUNK
        s5_pows = _s5_a16_powers(q["s5_a16"], nb.bit_length() - 1)
        s5_u2 = _s5_to_chunks(hin, COL_S5 * (D_GROUP // LANE), s5_perm, name=n + "s5_in")
        s5_y2, s5_x = _s5_core_fwd(s5_u2, *q["s5_mats"], s5_pows, name=n + "s5_core_fwd")
        s5_y1 = _s5_from_chunks(s5_y2, s5_perm, name=n + "s5_out")
        (mix,), _ = _s5_glu_fwd(s5_y1, p["s5_w_glu"][l], _row(p["s5_b_glu"][l]), name=n + "s5_glu_fwd")
        cvw = _pad_rows(p["cv_w"][l], CV_PAD)
        keys, rd = weight_rider(l, "cv_fwd")
        (mix, cv_c), routs = _cv_fwd(hin, cvw, _row(p["cv_b"][l]), _row(p["cv_gn_g"][l]), _row(p["cv_gn_b"][l]), mavg,
                                     p["cv_w_pw"][l], _row(p["cv_b_pw"][l]), mix, name=n + "cv_fwd", rider=rd)
        land_weights(keys, routs)
        lcw = _pad_rows(p["lru_conv_w"][l], SUBLANE)
        keys, rd = weight_rider(l, "lru_fwd")
        (mix, lru_xc, lru_h), routs = _lru_fwd(hin, lcw, _row(p["lru_conv_b"][l]), q["wr"], _row(p["lru_b_r"][l]), q["wi"],
                                               _row(p["lru_b_i"][l]), q["sp"], mix, name=n + "lru_fwd", rider=rd)
        land_weights(keys, routs)
        kv = _mm(mem, big_w["attn_w_kv"][l], name=n + "kv")
        (kbig, vbig), kv_vjp = jax.vjp(_attn_big, kv)
        kbig, vbig = kbig.astype(BF16), vbig.astype(BF16)
        mix = _attn_fwd(hin, kbig, vbig, mix, name=n + "attn_fwd")
        r1, x1 = _hosted(_mm, weight_rider(l, "outproj"), land_weights, mix, big_w["w_out"][l], bias=_row(p["b_out"][l]),
                         res=xs, res_scale=ALPHA, ln=(_row(p["ln1_g"][l]), _row(p["ln1_b"][l])), name=n + "outproj")
        u = _hosted(_mm, weight_rider(l, "ffn_up"), land_weights, x1, big_w["ffn_w_up"][l], out_dtype=BF16,
                    name=n + "ffn_up")
        fcw = _pad_rows(p["ffn_conv_w"][l], SUBLANE)
        fcb = _row(p["ffn_conv_b"][l])
        keys, rd = weight_rider(l, "ffn_gate_fwd")
        (hff, uc), routs = _ffn_gate_fwd(u, fcw, fcb, name=n + "ffn_gate_fwd", rider=rd)
        land_weights(keys, routs)
        if l < DEPTH - 1:
            r2, x2 = _hosted(_mm, weight_rider(l, "ffn_down"), land_weights, hff, big_w["ffn_w_down"][l], res=x1,
                             res_scale=ALPHA, ln=(_row(p["ln2_g"][l]), _row(p["ln2_b"][l])), name=n + "ffn_down")
        else:
            r2, x2 = _mm(hff, big_w["ffn_w_down"][l], res=x1, res_scale=ALPHA, name=n + "ffn_down"), None
        saved.append(dict(q=q, xs=xs, hin=hin, s5_y1=s5_y1, s5_u2=s5_u2, s5_x=s5_x, s5_pows=s5_pows, cvw=cvw, cv_c=cv_c, lcw=lcw, lru_xc=lru_xc,
                          lru_h=lru_h, kbig=kbig, vbig=vbig, kv_vjp=kv_vjp, mix=mix, r1=r1, x1=x1, u=u, uc=uc, fcw=fcw,
                          hff=hff, r2=r2))
        xs = x2

    top = DEPTH - 1
    dr_top, dg_top, db_top, loss_blk = _loss_ln_bwd(saved[top]["r2"], _row(p["ln2_g"][top]), _row(p["ln2_b"][top]), target,
                                                     name="loss_ln_bwd")
    loss = loss_blk[0, 0]
    dx = None

    for l in reversed(range(DEPTH)):
        sv = saved[l]
        q = sv["q"]
        n = f"l{l}_"
        g = {}
        if l == top:
            dr2, g["ln2_g"], g["ln2_b"] = dr_top, dg_top, db_top
        else:
            dr2, g["ln2_g"], g["ln2_b"] = from_above
        big_grad("ffn_w_down", l, _hosted(_mm_tn, grad_rider(l, "dw_down"), land_grads, sv["hff"], dr2, out_dtype=gdt,
                                          name=n + "dw_down"))
        dhff = _hosted(_mm, grad_rider(l, "dhff"), land_grads, dr2, big_w["ffn_w_down"][l], trans_b=True,
                       out_dtype=BF16, name=n + "dhff")
        keys, rd = grad_rider(l, "ffn_gate_bwd")
        (du, dfw, g["ffn_conv_b"]), routs = _ffn_gate_bwd(sv["u"], sv["uc"], dhff, sv["fcw"], name=n + "ffn_gate_bwd",
                                                          rider=rd)
        land_grads(keys, routs)
        g["ffn_conv_w"] = dfw[:FFN_CONV_WIDTH]
        if dist:
            ready[("ffn_w_up", l)] = (_hosted(_mm_tn, grad_rider(l, "dw_up"), land_grads, sv["x1"], du, out_dtype=gdt,
                                              dev_cols=du.shape[1] // N_DEV, name=n + "dw_up"), "lead")
        else:
            big_grad("ffn_w_up", l, _mm_tn(sv["x1"], du, name=n + "dw_up"))
        dr1, g["ln1_g"], g["ln1_b"], g["b_out"] = _hosted(
            _mm, grad_rider(l, "dx1"), land_grads, du, big_w["ffn_w_up"][l], trans_b=True, res=dr2, res_scale=ALPHA,
            ln_bwd=(sv["r1"], _row(p["ln1_g"][l])), name=n + "dx1")
        big_grad("w_out", l, _mm_tn(sv["mix"], dr1, out_dtype=gdt, name=n + "dw_out"))
        dmix = _mm(dr1, big_w["w_out"][l], trans_b=True, name=n + "dmix")

        hin = sv["hin"]
        keys, rd = grad_rider(l, "cv_bwd")
        (dh, g["cv_w_pw"], dcw, g["cv_b_pw"], g["cv_gn_g"], g["cv_gn_b"], g["cv_b"]), routs = _cv_bwd(
            hin, sv["cv_c"], dmix, sv["cvw"], _row(p["cv_gn_g"][l]), _row(p["cv_gn_b"][l]), mavg, p["cv_w_pw"][l],
            name=n + "cv_bwd", rider=rd)
        land_grads(keys, routs)
        g["cv_w"] = dcw[:CONV_WIDTH]
        dh, dwr, dwi, dlcw, g["lru_b_r"], g["lru_b_i"], dsp, g["lru_conv_b"] = _lru_bwd(
            hin, sv["lru_xc"], sv["lru_h"], dmix, sv["lcw"], q["wr"], _row(p["lru_b_r"][l]), q["wi"],
            _row(p["lru_b_i"][l]), q["sp"], dh, name=n + "lru_bwd")
        g["lru_conv_w"] = dlcw[:LRU_CONV_WIDTH]
        g["lru_w_r"], g["lru_w_i"] = q["lru_w_vjp"]((dwr, dwi))
        (g["lru_lam"],) = q["sp_vjp"](dsp)
        dy1, g["s5_w_glu"], g["s5_b_glu"] = _s5_glu_bwd(sv["s5_y1"], dmix, p["s5_w_glu"][l], _row(p["s5_b_glu"][l]),
                                                        name=n + "s5_glu_bwd")
        s5_du2, *s5_dmats = _s5_core_bwd(sv["s5_u2"], _s5_to_chunks(dy1, 0, s5_perm, name=n + "s5_din"), sv["s5_x"],
                                         *q["s5_mats"], sv["s5_pows"], name=n + "s5_core_bwd")
        (g["s5_lam_re"], g["s5_lam_im"], g["s5_log_dt"], g["s5_b_re"], g["s5_b_im"], g["s5_c_re"], g["s5_c_im"],
         g["s5_d"]) = q["s5_vjp"](tuple(s5_dmats))
        dh, dkbig, dvbig = _attn_bwd(hin, dmix, sv["kbig"], sv["vbig"],
                                     _s5_from_chunks(s5_du2, s5_perm, name=n + "s5_dout"), dh, name=n + "attn_bwd")
        (dkv,) = sv["kv_vjp"]((dkbig, dvbig))
        big_grad("attn_w_kv", l, _mm_tn(mem, dkv, out_dtype=gdt, name=n + "dw_kv"))

        if dist:
            ready[("ssh", l)] = (_pack_lead([_split_shards(g[k], SHARDED[k] - 1) for k in SMALL_SHARDED], F32), "lead")
            ready[("rep", l)] = (_pack([g[k] for k in REP_LAYERED], F32), "all")
        gw_in, g["b_in"] = _hosted(_mm_tn, grad_rider(l, "dw_in"), land_grads, sv["xs"], dh, colsum=True, out_dtype=gdt,
                                   name=n + "dw_in")
        big_grad("w_in", l, gw_in)
        if dist:
            small.setdefault("b_in", [None] * DEPTH)[l] = g["b_in"].reshape(-1)
        else:
            for k, v in g.items():
                small.setdefault(k, [None] * DEPTH)[l] = v.reshape(p[k].shape[1:])
        if l > 0:
            dr2_below, dg_below, db_below, _ = _hosted(
                _mm, grad_rider(l, "dxs"), land_grads, dh, big_w["w_in"][l], trans_b=True, res=dr1, res_scale=ALPHA,
                ln_bwd=(saved[l - 1]["r2"], _row(p["ln2_g"][l - 1])), name=n + "dxs")
            from_above = (dr2_below, dg_below, db_below)
        else:
            dx = _hosted(_mm, grad_rider(l, "dxs"), land_grads, dh, big_w["w_in"][l], trans_b=True, res=dr1,
                         res_scale=ALPHA, name=n + "dxs")

    keys, rd = grad_rider(0, "ln_in_bwd")
    if rd is None:
        grad_x, dgi, dbi, _ = _ln_bwd(x, dx, _row(p["ln_in_g"]), name="ln_in_bwd")
    else:
        (grad_x, dgi, dbi, _), routs = _ln_bwd(x, dx, _row(p["ln_in_g"]), name="ln_in_bwd", rider=rd)
        land_grads(keys, routs)
    out = {k: jnp.stack(v, axis=0) for k, v in small.items()}
    out["ln_in_g"], out["ln_in_b"] = dgi.reshape(-1), dbi.reshape(-1)
    return loss, grad_x, out, ((recv, ready) if dist else big_g)


def kernel(x, mem, ln_in_g, ln_in_b, w_in, b_in, s5_lam_re, s5_lam_im, s5_log_dt, s5_b_re, s5_b_im, s5_c_re, s5_c_im, s5_d, s5_w_glu, s5_b_glu, cv_w, cv_b, cv_gn_g, cv_gn_b, cv_w_pw, cv_b_pw, lru_conv_w, lru_conv_b, lru_w_r, lru_b_r, lru_w_i, lru_b_i, lru_lam, attn_w_kv, w_out, b_out, ln1_g, ln1_b, ffn_w_up, ffn_conv_w, ffn_conv_b, ffn_w_down, ln2_g, ln2_b, loss_target, m_ln_in_g, m_ln_in_b, m_w_in, m_b_in, m_s5_lam_re, m_s5_lam_im, m_s5_log_dt, m_s5_b_re, m_s5_b_im, m_s5_c_re, m_s5_c_im, m_s5_d, m_s5_w_glu, m_s5_b_glu, m_cv_w, m_cv_b, m_cv_gn_g, m_cv_gn_b, m_cv_w_pw, m_cv_b_pw, m_lru_conv_w, m_lru_conv_b, m_lru_w_r, m_lru_b_r, m_lru_w_i, m_lru_b_i, m_lru_lam, m_attn_w_kv, m_w_out, m_b_out, m_ln1_g, m_ln1_b, m_ffn_w_up, m_ffn_conv_w, m_ffn_conv_b, m_ffn_w_down, m_ln2_g, m_ln2_b, v_ln_in_g, v_ln_in_b, v_w_in, v_b_in, v_s5_lam_re, v_s5_lam_im, v_s5_log_dt, v_s5_b_re, v_s5_b_im, v_s5_c_re, v_s5_c_im, v_s5_d, v_s5_w_glu, v_s5_b_glu, v_cv_w, v_cv_b, v_cv_gn_g, v_cv_gn_b, v_cv_w_pw, v_cv_b_pw, v_lru_conv_w, v_lru_conv_b, v_lru_w_r, v_lru_b_r, v_lru_w_i, v_lru_b_i, v_lru_lam, v_attn_w_kv, v_w_out, v_b_out, v_ln1_g, v_ln1_b, v_ffn_w_up, v_ffn_conv_w, v_ffn_conv_b, v_ffn_w_down, v_ln2_g, v_ln2_b):
    args = locals()
    w = {n: args[n] for n in WEIGHTS}
    mom = {n: args["m_" + n] for n in WEIGHTS}
    var = {n: args["v_" + n] for n in WEIGHTS}

    shards = {n: w[n].astype(BF16) for n in BIG}
    part_rows = shards["ffn_w_up"].shape[1] // len(ROW_PARTS)
    for i, tag in enumerate(ROW_PARTS):
        shards["ffn_w_up#" + tag] = [shards["ffn_w_up"][0, i * part_rows:(i + 1) * part_rows]]
    shards["small_pack"] = [_pack([w[n] for n in SMALL_SHARDED], F32)]
    small_shapes = [w[n].shape for n in SMALL_SHARDED]

    def unpack_small(gathered):
        out = {n: _join_shards(st, SHARDED[n]) for n, st in zip(SMALL_SHARDED, _unpack(gathered, small_shapes, lead=True))}
        for n in ("s5_w_glu", "cv_w_pw"):
            out[n] = out[n].astype(BF16)
        return out

    big_w = {n: [None] * DEPTH for n in BIG}
    p = {n: w[n] for n in REPLICATED}
    p["b_in"] = _perm_in_cols(p["b_in"])

    loss, grad_x, g_small, (recv, ready) = _local_step(x[0], mem[0], loss_target[0], p, big_w, shards, unpack_small)
    loss = lax.psum(loss, ("x", "y", "c"))

    g_small["b_in"] = _perm_in_cols(g_small["b_in"], inverse=True)
    left = list(ready)
    rider = _Rider([ready[k][0] for k in left] + [_pack([g_small[n] for n in REP_LAST], F32)],
                   [ready[k][1] for k in left] + ["all"])
    got = _exchange(rider, name="exchange_grads")
    for k, r in zip(left, got):
        recv[k] = r

    res = [dict(), dict(), dict(), dict()]
    for n in BIG:
        outs = None
        for l in range(DEPTH):
            outs = _adamw_layer(recv[(n, l)], w[n], mom[n], var[n], l, outs, name=f"adamw_{n}_l{l}")
        for kind in range(4):
            res[kind][n] = outs[kind]
    for names, key, tag in ((SMALL_SHARDED, "ssh", "adamw_small_sharded"), (REP_LAYERED, "rep", "adamw_replicated")):
        gstack = jnp.concatenate([recv[(key, l)] for l in range(DEPTH)], axis=1)
        packs = [_pack_layers([t[n] for n in names], F32) for t in (w, mom, var)]
        rows = packs[0].shape[1]
        outs = _adamw(gstack, *[pk.reshape(DEPTH * rows, PACK_COLS) for pk in packs], name=tag)
        for kind in range(4):
            for n, a in zip(names, _unpack_layers(outs[kind].reshape(DEPTH, rows, PACK_COLS), [w[n].shape for n in names])):
                res[kind][n] = a
    outs = _adamw(got[len(left)], _pack([w[n] for n in REP_LAST], F32), _pack([mom[n] for n in REP_LAST], F32),
                  _pack([var[n] for n in REP_LAST], F32), name="adamw_last")
    for kind in range(4):
        for n, a in zip(REP_LAST, _unpack(outs[kind], [w[n].shape for n in REP_LAST])):
            res[kind][n] = a
    return (loss, grad_x[None], *[res[0][n] for n in WEIGHTS], *[res[1][n] for n in WEIGHTS],
            *[res[2][n] for n in WEIGHTS], *[res[3][n] for n in WEIGHTS])
```

```python
import math

import jax
import jax.numpy as jnp
from jax import lax
from jax.experimental import pallas as pl
from jax.experimental.pallas import tpu as pltpu

F32 = jnp.float32
BF16 = jnp.bfloat16

D_MODEL = 1024
DEPTH = 2
D_GROUP = 256
N_IN_COLS = 6 * D_GROUP
S5_GROUPS = 16
S5_CH = 16
S5_STATE = 64
CONV_WIDTH = 31
GN_GROUPS = 4
LRU_HEADS = 4
LRU_CONV_WIDTH = 4
LRU_C = 8.0
ATTN_HEADS = 4
ATTN_HEAD_DIM = 64
D_FF = 2816
FFN_CONV_WIDTH = 3
ALPHA = (2 * DEPTH) ** 0.25
LN_EPS = 1e-5
ADAM_LR, ADAM_B1, ADAM_B2, ADAM_EPS, ADAM_WD, ADAM_STEP = 0.001, 0.9, 0.999, 1e-08, 0.01, 10

N_DEV = 8
N_PEERS = N_DEV - 1
LANE = 128
SUBLANE = 8
VMEM_LIMIT = 56 * 1024 * 1024
PACK_COLS = 1024
PACK_ROW_BLOCK = 256
MM_ROW_TILE = 1024
MM_COL_CAP = 1408
MM_K_CAP = 1536
SEQ_TILE = 512

SHARDED = {
    "w_in": 2, "s5_w_glu": 1, "cv_w": 2, "cv_w_pw": 1, "lru_conv_w": 2, "attn_w_kv": 1,
    "w_out": 1, "ffn_w_up": 2, "ffn_conv_w": 2, "ffn_w_down": 1,
}
BIG = ("w_in", "attn_w_kv", "w_out", "ffn_w_up", "ffn_w_down")
SMALL_SHARDED = ("s5_w_glu", "cv_w", "cv_w_pw", "lru_conv_w", "ffn_conv_w")
WEIGHTS = ['ln_in_g', 'ln_in_b', 'w_in', 'b_in', 's5_lam_re', 's5_lam_im', 's5_log_dt', 's5_b_re', 's5_b_im',
           's5_c_re', 's5_c_im', 's5_d', 's5_w_glu', 's5_b_glu', 'cv_w', 'cv_b', 'cv_gn_g', 'cv_gn_b', 'cv_w_pw',
           'cv_b_pw', 'lru_conv_w', 'lru_conv_b', 'lru_w_r', 'lru_b_r', 'lru_w_i', 'lru_b_i', 'lru_lam',
           'attn_w_kv', 'w_out', 'b_out', 'ln1_g', 'ln1_b', 'ffn_w_up', 'ffn_conv_w', 'ffn_conv_b', 'ffn_w_down',
           'ln2_g', 'ln2_b']
REPLICATED = [n for n in WEIGHTS if n not in SHARDED]
REP_LAST = ("ln_in_g", "ln_in_b", "b_in")
REP_LAYERED = [n for n in REPLICATED if n not in REP_LAST]

COL_CV_V, COL_CV_G, COL_LRU_G, COL_LRU_X, COL_S5, COL_Q = range(6)
IN_PERM = (1, 2, 3, 4, 0, 5)
MIX_S5, MIX_CV, MIX_LRU, MIX_ATTN = range(4)


_ANY = pl.BlockSpec(memory_space=pl.ANY)
_MESH = pl.DeviceIdType.MESH


def _cparams(n_axes):
    return pltpu.CompilerParams(dimension_semantics=("arbitrary",) * n_axes, vmem_limit_bytes=VMEM_LIMIT)


def _pick(n, cap):
    if n <= cap:
        return n
    best = None
    for t in range(LANE, cap + 1, LANE):
        if n % t == 0:
            best = t
    assert best is not None, (n, cap)
    return best


def _pick_rows(n, cap):
    best = None
    for t in range(SUBLANE, min(n, cap) + 1, SUBLANE):
        if n % t == 0:
            best = t
    assert best is not None, (n, cap)
    return best


def _full_spec(arr):
    nd = arr.ndim
    return pl.BlockSpec(arr.shape, lambda *_: (0,) * nd)


def _dot(a, b):
    return lax.dot_general(a.astype(BF16), b.astype(BF16), (((1,), (0,)), ((), ())), preferred_element_type=F32)


def _dot_nt(a, b):
    return lax.dot_general(a.astype(BF16), b.astype(BF16), (((1,), (1,)), ((), ())), preferred_element_type=F32)


def _dot_tn(a, b):
    return lax.dot_general(a.astype(BF16), b.astype(BF16), (((0,), (0,)), ((), ())), preferred_element_type=F32)


def _dot_hi(a, b):
    b = b.astype(BF16)
    a1 = a.astype(BF16)
    r1 = a - a1.astype(F32)
    a2 = r1.astype(BF16)
    a3 = (r1 - a2.astype(F32)).astype(BF16)
    return _dot(a1, b) + _dot(a2, b) + _dot(a3, b)


def _colsum(x):
    return jnp.sum(x, axis=0, keepdims=True)


def _sigmoid(x):
    return 1.0 / (1.0 + jnp.exp(-x))


_GELU_K = math.sqrt(2.0 / math.pi)
_GELU_C = 0.044715


def _gelu(x):
    t = jnp.tanh(_GELU_K * (x + _GELU_C * x * x * x))
    return 0.5 * x * (1.0 + t)


def _gelu_and_grad(x):
    x2 = x * x
    t = jnp.tanh(_GELU_K * (x + _GELU_C * x2 * x))
    g = 0.5 * x * (1.0 + t)
    dg = 0.5 * (1.0 + t) + 0.5 * x * (1.0 - t * t) * (_GELU_K * (1.0 + 3.0 * _GELU_C * x2))
    return g, dg


def _neg_expm1(x):
    series = x * (1.0 + x * (0.5 + x * (1.0 / 6.0 + x * (1.0 / 24.0 + x * (1.0 / 120.0)))))
    return -jnp.where(jnp.abs(x) < 0.1, series, jnp.exp(x) - 1.0)


def _seq_tile(s, want):
    t = min(s, want)
    assert s % t == 0
    return t


class _Rider:
    def __init__(self, srcs, kinds):
        self.srcs, self.kinds = list(srcs), list(kinds)
        self.n = len(self.srcs)

    def out_shapes(self):
        shapes = []
        for x, kind in zip(self.srcs, self.kinds):
            if kind == "lead":
                shp = x.shape
            elif kind == "rows":
                shp = (N_DEV, x.shape[0] // N_DEV) + x.shape[1:]
            else:
                shp = (N_DEV,) + x.shape
            shapes.append(jax.ShapeDtypeStruct(shp, x.dtype))
        return shapes

    def scratch(self):
        return [pltpu.SemaphoreType.DMA((self.n * N_PEERS,)), pltpu.SemaphoreType.DMA((self.n * N_PEERS,)),
                pltpu.SemaphoreType.DMA((self.n,))]

    def _copies(self, x_refs, out_refs, sems):
        send_sems, recv_sems, local_sems = sems
        mx, my, mc = lax.axis_index("x"), lax.axis_index("y"), lax.axis_index("c")
        my_id = 4 * mx + 2 * my + mc

        def piece(i, dev):
            if self.kinds[i] == "lead":
                return x_refs[i].at[dev]
            if self.kinds[i] == "rows":
                r = x_refs[i].shape[0] // N_DEV
                return x_refs[i].at[pl.ds(pl.multiple_of(dev * r, SUBLANE), r)]
            return x_refs[i]

        mine = [pltpu.make_async_copy(piece(i, my_id), out_refs[i].at[my_id], local_sems.at[i]) for i in range(self.n)]
        copies = []
        for k in range(1, N_DEV):
            px, py, pc = mx ^ ((k >> 2) & 1), my ^ ((k >> 1) & 1), mc ^ (k & 1)
            for i in range(self.n):
                copies.append(pltpu.make_async_remote_copy(
                    src_ref=piece(i, 4 * px + 2 * py + pc), dst_ref=out_refs[i].at[my_id],
                    send_sem=send_sems.at[i * N_PEERS + k - 1], recv_sem=recv_sems.at[i * N_PEERS + k - 1],
                    device_id=(px, py, pc), device_id_type=_MESH))
        return mine, copies

    def start(self, x_refs, out_refs, sems):
        mine, copies = self._copies(x_refs, out_refs, sems)
        for cp in mine + copies:
            cp.start()

    def wait(self, x_refs, out_refs, sems):
        mine, copies = self._copies(x_refs, out_refs, sems)
        for cp in copies:
            cp.wait_recv()
        for cp in copies:
            cp.wait_send()
        for cp in mine:
            cp.wait()


def _call(body, *, grid, ins, in_specs, outs, out_specs, scratch=(), aliases=None, name, rider=None):
    n_axes = len(grid)
    common = dict(grid=grid, input_output_aliases=aliases or {}, compiler_params=_cparams(n_axes), name=name)
    if rider is None:
        res = pl.pallas_call(body, in_specs=list(in_specs), out_specs=list(out_specs), out_shape=list(outs),
                             scratch_shapes=list(scratch), **common)(*ins)
        return list(res), []
    n_in, n_out, n_scr, nr = len(ins), len(outs), len(scratch), rider.n

    def wrapped(*refs):
        pos = [0]

        def take(k):
            part = refs[pos[0]:pos[0] + k]
            pos[0] += k
            return part

        a_in, r_in, a_out, r_out, a_scr, sems = take(n_in), take(nr), take(n_out), take(nr), take(n_scr), take(3)
        first = last = None
        for ax in range(n_axes):
            pid = pl.program_id(ax)
            f, l = pid == 0, pid == grid[ax] - 1
            first = f if first is None else jnp.logical_and(first, f)
            last = l if last is None else jnp.logical_and(last, l)

        @pl.when(first)
        def _():
            rider.start(r_in, r_out, sems)

        body(*a_in, *a_out, *a_scr)

        @pl.when(last)
        def _():
            rider.wait(r_in, r_out, sems)

    res = pl.pallas_call(
        wrapped, in_specs=list(in_specs) + [_ANY] * nr, out_specs=list(out_specs) + [_ANY] * nr,
        out_shape=list(outs) + rider.out_shapes(), scratch_shapes=list(scratch) + rider.scratch(), **common)(*ins, *rider.srcs)
    return list(res[:n_out]), list(res[n_out:])


def _block_mask(n_blocks, block_rows, block_cols):
    r = jnp.arange(n_blocks * block_rows) // block_rows
    c = jnp.arange(n_blocks * block_cols) // block_cols
    return (r[:, None] == c[None, :]).astype(F32)


def _mm(a, b, *, bias=None, res=None, res_scale=1.0, trans_b=False, out_dtype=F32, ln=None, ln_bwd=None, name,
        rider=None):
    m, kdim = a.shape
    n = b.shape[0] if trans_b else b.shape[1]
    tm = _seq_tile(m, MM_ROW_TILE)
    tn = _pick(n, MM_COL_CAP)
    tk = _pick(kdim, MM_K_CAP)
    nk = kdim // tk
    has_bias, has_res, has_ln, has_lnb = bias is not None, res is not None, ln is not None, ln_bwd is not None
    assert not (has_ln or has_lnb) or tn == n
    assert not (has_ln and has_lnb)

    def body(*refs):
        a_ref, b_ref = refs[0], refs[1]
        pos = 2
        bias_ref = res_ref = g_ref = beta_ref = x_ref = None
        if has_bias:
            bias_ref = refs[pos]
            pos += 1
        if has_res:
            res_ref = refs[pos]
            pos += 1
        if has_ln:
            g_ref, beta_ref = refs[pos], refs[pos + 1]
            pos += 2
        if has_lnb:
            x_ref, g_ref = refs[pos], refs[pos + 1]
            pos += 2
        o_ref = refs[pos]
        pos += 1
        if has_ln:
            x_ref = refs[pos]
            pos += 1
        if has_lnb:
            dg_ref, db_ref, ds_ref = refs[pos:pos + 3]
            pos += 3
        acc_ref = refs[pos]
        k = pl.program_id(2)

        @pl.when(k == 0)
        def _():
            acc_ref[...] = jnp.zeros_like(acc_ref)

        if has_lnb:
            @pl.when(jnp.logical_and(pl.program_id(0) == 0, k == 0))
            def _():
                dg_ref[...] = jnp.zeros_like(dg_ref)
                db_ref[...] = jnp.zeros_like(db_ref)
                ds_ref[...] = jnp.zeros_like(ds_ref)

        if trans_b:
            acc_ref[...] += _dot_nt(a_ref[...], b_ref[...])
        else:
            acc_ref[...] += _dot(a_ref[...], b_ref[...])

        @pl.when(k == nk - 1)
        def _():
            r = acc_ref[...]
            if has_bias:
                r = r + bias_ref[...]
            if has_res:
                r = r + res_scale * res_ref[...]
            if has_lnb:
                x = x_ref[...]
                xc = x - jnp.mean(x, axis=1, keepdims=True)
                rstd = lax.rsqrt(jnp.mean(xc * xc, axis=1, keepdims=True) + LN_EPS)
                xh = xc * rstd
                dxh = r * g_ref[...]
                dx = rstd * (dxh - jnp.mean(dxh, axis=1, keepdims=True) - xh * jnp.mean(dxh * xh, axis=1, keepdims=True))
                o_ref[...] = dx
                dg_ref[...] += _colsum(r * xh)
                db_ref[...] += _colsum(r)
                ds_ref[...] += _colsum(dx)
            else:
                o_ref[...] = r.astype(out_dtype)
            if has_ln:
                xc = r - jnp.mean(r, axis=1, keepdims=True)
                var = jnp.mean(xc * xc, axis=1, keepdims=True)
                x_ref[...] = xc * lax.rsqrt(var + LN_EPS) * g_ref[...] + beta_ref[...]

    ins = [a, b]
    in_specs = [pl.BlockSpec((tm, tk), lambda i, j, k: (i, k)),
                pl.BlockSpec((tn, tk), lambda i, j, k: (j, k)) if trans_b
                else pl.BlockSpec((tk, tn), lambda i, j, k: (k, j))]
    if has_bias:
        ins.append(bias)
        in_specs.append(pl.BlockSpec((1, tn), lambda i, j, k: (0, j)))
    if has_res:
        ins.append(res)
        in_specs.append(pl.BlockSpec((tm, tn), lambda i, j, k: (i, j)))
    tile = pl.BlockSpec((tm, tn), lambda i, j, k: (i, j))
    vec = pl.BlockSpec((1, tn), lambda i, j, k: (0, j))
    out_shapes, out_specs = [jax.ShapeDtypeStruct((m, n), out_dtype)], [tile]
    if has_ln:
        ins += list(ln)
        in_specs += [vec] * 2
        out_shapes.append(jax.ShapeDtypeStruct((m, n), F32))
        out_specs.append(tile)
    if has_lnb:
        ins += list(ln_bwd)
        in_specs += [tile, vec]
        out_shapes += [jax.ShapeDtypeStruct((1, n), F32)] * 3
        out_specs += [vec] * 3
    outs, routs = _call(
        body, grid=(m // tm, n // tn, nk), ins=ins, in_specs=in_specs, outs=out_shapes, out_specs=out_specs,
        scratch=[pltpu.VMEM((tm, tn), F32)], name=name, rider=rider)
    out = tuple(outs) if (has_ln or has_lnb) else outs[0]
    return out if rider is None else (out, routs)


def _mm_tn(a, b, *, colsum=False, out_dtype=F32, dev_cols=None, name, rider=None):
    s, ka = a.shape
    nb = b.shape[1]
    ts = _seq_tile(s, SEQ_TILE)
    tka = _pick(ka, MM_COL_CAP)
    tnb = _pick(nb, MM_COL_CAP)
    nk = s // ts
    assert not colsum or tka == ka
    per_tile = 1 if dev_cols is None else tnb // dev_cols
    assert dev_cols is None or tnb == per_tile * dev_cols

    def body(a_ref, b_ref, o_ref, *rest):
        cs_ref = rest[0] if colsum else None
        acc_ref = rest[-1]
        k = pl.program_id(2)

        @pl.when(k == 0)
        def _():
            acc_ref[...] = jnp.zeros_like(acc_ref)
            if colsum:
                cs_ref[...] = jnp.zeros_like(cs_ref)

        bv = b_ref[...]
        acc_ref[...] += _dot_tn(a_ref[...], bv)
        if colsum:
            cs_ref[...] += _colsum(bv.astype(F32))

        @pl.when(k == nk - 1)
        def _():
            if dev_cols is None:
                o_ref[...] = acc_ref[...].astype(out_dtype)
            else:
                for d in range(per_tile):
                    o_ref[d] = acc_ref[:, d * dev_cols:(d + 1) * dev_cols].astype(out_dtype)

    if dev_cols is None:
        main_shape, main_spec = (ka, nb), pl.BlockSpec((tka, tnb), lambda i, j, k: (i, j))
    else:
        main_shape = (nb // dev_cols, ka, dev_cols)
        main_spec = pl.BlockSpec((per_tile, tka, dev_cols), lambda i, j, k: (j, i, 0))
    outs, routs = _call(
        body, grid=(ka // tka, nb // tnb, nk), ins=[a, b],
        in_specs=[pl.BlockSpec((ts, tka), lambda i, j, k: (k, i)), pl.BlockSpec((ts, tnb), lambda i, j, k: (k, j))],
        outs=[jax.ShapeDtypeStruct(main_shape, out_dtype)] + ([jax.ShapeDtypeStruct((1, nb), F32)] if colsum else []),
        out_specs=[main_spec] + ([pl.BlockSpec((1, tnb), lambda i, j, k: (0, j))] if colsum else []),
        scratch=[pltpu.VMEM((tka, tnb), F32)], name=name, rider=rider)
    out = tuple(outs) if colsum else outs[0]
    return out if rider is None else (out, routs)


def _ln_fwd(r, g, b, *, name, rider=None):
    s, d = r.shape
    ts = _seq_tile(s, SEQ_TILE)

    def body(r_ref, g_ref, b_ref, o_ref):
        x = r_ref[...]
        mu = jnp.mean(x, axis=1, keepdims=True)
        xc = x - mu
        var = jnp.mean(xc * xc, axis=1, keepdims=True)
        o_ref[...] = xc * lax.rsqrt(var + LN_EPS) * g_ref[...] + b_ref[...]

    (out,), routs = _call(
        body, grid=(s // ts,), ins=[r, g, b],
        in_specs=[pl.BlockSpec((ts, d), lambda i: (i, 0)), _full_spec(g), _full_spec(b)],
        out_specs=[pl.BlockSpec((ts, d), lambda i: (i, 0))], outs=[jax.ShapeDtypeStruct((s, d), F32)],
        name=name, rider=rider)
    return out if rider is None else (out, routs)


def _ln_bwd(r, dy, g, *, name, rider=None):
    s, d = r.shape
    ts = _seq_tile(s, SEQ_TILE)

    def body(r_ref, dy_ref, g_ref, dr_ref, dg_ref, db_ref, ds_ref):
        @pl.when(pl.program_id(0) == 0)
        def _():
            dg_ref[...] = jnp.zeros_like(dg_ref)
            db_ref[...] = jnp.zeros_like(db_ref)
            ds_ref[...] = jnp.zeros_like(ds_ref)

        x = r_ref[...]
        dy = dy_ref[...]
        mu = jnp.mean(x, axis=1, keepdims=True)
        xc = x - mu
        var = jnp.mean(xc * xc, axis=1, keepdims=True)
        rstd = lax.rsqrt(var + LN_EPS)
        xh = xc * rstd
        dxh = dy * g_ref[...]
        m1 = jnp.mean(dxh, axis=1, keepdims=True)
        m2 = jnp.mean(dxh * xh, axis=1, keepdims=True)
        dr = rstd * (dxh - m1 - xh * m2)
        dr_ref[...] = dr
        dg_ref[...] += _colsum(dy * xh)
        db_ref[...] += _colsum(dy)
        ds_ref[...] += _colsum(dr)

    vec = jax.ShapeDtypeStruct((1, d), F32)
    vspec = pl.BlockSpec((1, d), lambda i: (0, 0))
    outs, routs = _call(
        body, grid=(s // ts,), ins=[r, dy, g],
        in_specs=[pl.BlockSpec((ts, d), lambda i: (i, 0)), pl.BlockSpec((ts, d), lambda i: (i, 0)), _full_spec(g)],
        out_specs=[pl.BlockSpec((ts, d), lambda i: (i, 0)), vspec, vspec, vspec],
        outs=[jax.ShapeDtypeStruct((s, d), F32), vec, vec, vec], name=name, rider=rider)
    return outs if rider is None else (outs, routs)


def _loss_ln_bwd(r, g, b, target, *, name):
    s, d = r.shape
    ts = _seq_tile(s, SEQ_TILE)

    def body(r_ref, g_ref, b_ref, t_ref, dr_ref, dg_ref, db_ref, l_ref):
        @pl.when(pl.program_id(0) == 0)
        def _():
            dg_ref[...] = jnp.zeros_like(dg_ref)
            db_ref[...] = jnp.zeros_like(db_ref)
            l_ref[...] = jnp.zeros_like(l_ref)

        x = r_ref[...]
        gam = g_ref[...]
        xc = x - jnp.mean(x, axis=1, keepdims=True)
        var = jnp.mean(xc * xc, axis=1, keepdims=True)
        rstd = lax.rsqrt(var + LN_EPS)
        xh = xc * rstd
        e = xh * gam + b_ref[...] - t_ref[...]
        part = jnp.sum(jnp.sum(e * e, axis=1, keepdims=True), axis=0, keepdims=True) * (0.5 / d)
        l_ref[...] += jnp.broadcast_to(part, l_ref.shape)
        dy = e * (1.0 / d)
        dxh = dy * gam
        m1 = jnp.mean(dxh, axis=1, keepdims=True)
        m2 = jnp.mean(dxh * xh, axis=1, keepdims=True)
        dr_ref[...] = rstd * (dxh - m1 - xh * m2)
        dg_ref[...] += _colsum(dy * xh)
        db_ref[...] += _colsum(dy)

    vec = jax.ShapeDtypeStruct((1, d), F32)
    vspec = pl.BlockSpec((1, d), lambda i: (0, 0))
    tile = pl.BlockSpec((ts, d), lambda i: (i, 0))
    return pl.pallas_call(
        body, grid=(s // ts,), in_specs=[tile, _full_spec(g), _full_spec(b), tile],
        out_specs=[tile, vspec, vspec, pl.BlockSpec((SUBLANE, LANE), lambda i: (0, 0))],
        out_shape=[jax.ShapeDtypeStruct((s, d), F32), vec, vec, jax.ShapeDtypeStruct((SUBLANE, LANE), F32)],
        compiler_params=_cparams(1), name=name)(r, g, b, target)


SCAN_CHUNK = 32


def _cscan_levels(bufs, apow_ref, t, pad, *, reverse):
    half = bufs[0].shape[1] // 2
    ch = min(SCAN_CHUNK, t)
    nlev = t.bit_length() - 1
    assert (1 << nlev) == t
    for k in range(nlev):
        d = 1 << k
        src, dst = bufs[k % 2], bufs[(k + 1) % 2]

        def chunk(c, carry, src=src, dst=dst, d=d, k=k):
            ar = apow_ref[k:k + 1, :half]
            ai = apow_ref[k:k + 1, half:]
            if reverse:
                ai = -ai
            r0 = pl.multiple_of(c * ch, ch)
            cur = src[pl.ds(pad + r0, ch), :]
            if d >= SUBLANE:
                off = pad + d if reverse else pad - d
                sh = src[pl.ds(off + r0, ch), :]
            elif reverse:
                blk = src[pl.ds(pad + r0, ch + SUBLANE), :]
                sh = pltpu.roll(blk, ch + SUBLANE - d, axis=0)[:ch, :]
            else:
                blk = src[pl.ds(pad - SUBLANE + r0, ch + SUBLANE), :]
                sh = pltpu.roll(blk, d, axis=0)[SUBLANE:, :]
            sre, sim = sh[:, :half], sh[:, half:]
            dst[pl.ds(pad + r0, ch), :half] = cur[:, :half] + ar * sre - ai * sim
            dst[pl.ds(pad + r0, ch), half:] = cur[:, half:] + ar * sim + ai * sre
            return carry

        lax.fori_loop(0, t // ch, chunk, 0)
    return nlev % 2


def _rscan_levels(abufs, bbufs, t, pad, *, reverse):
    nlev = t.bit_length() - 1
    assert (1 << nlev) == t
    for k in range(nlev):
        d = 1 << k
        asrc, adst = abufs[k % 2], abufs[(k + 1) % 2]
        bsrc, bdst = bbufs[k % 2], bbufs[(k + 1) % 2]
        off = pad + d if reverse else pad - d
        a = asrc[pad:pad + t, :]
        bdst[pad:pad + t, :] = a * bsrc[off:off + t, :] + bsrc[pad:pad + t, :]
        if k < nlev - 1:
            adst[pad:pad + t, :] = a * asrc[off:off + t, :]
    return nlev % 2


S5_CHUNK = 16
S5_SG = S5_GROUPS // 2
S5_SG_IN = 2 * S5_CHUNK * S5_CH
S5_SG_ST = 2 * S5_STATE


S5_HALF_SGS = S5_SG // 2
S5_HALF_IN = S5_HALF_SGS * S5_SG_IN


def _s5_perm():
    idx = jnp.arange(S5_HALF_IN)
    step, grp, chan = idx // LANE, (idx % LANE) // S5_CH, idx % S5_CH
    col = (grp // 2) * S5_SG_IN + (grp % 2) * (S5_CHUNK * S5_CH) + step * S5_CH + chan
    return (col[:, None] == idx[None, :]).astype(BF16)


def _s5_to_chunks(x, col_block, perm, *, name):
    s = x.shape[0]
    nb = s // S5_CHUNK

    def body(x_ref, perm_ref, o_ref):
        tok = jnp.concatenate([x_ref[pl.ds(t, nb, stride=S5_CHUNK), :].astype(BF16) for t in range(S5_CHUNK)], axis=1)
        grouped = _dot(tok, perm_ref[...]).astype(BF16)
        for k in range(S5_HALF_SGS):
            o_ref[k] = grouped[:, k * S5_SG_IN:(k + 1) * S5_SG_IN]

    return pl.pallas_call(
        body, grid=(2,),
        in_specs=[pl.BlockSpec((s, LANE), lambda h: (0, col_block + h)), _full_spec(perm)],
        out_specs=pl.BlockSpec((S5_HALF_SGS, nb, S5_SG_IN), lambda h: (h, 0, 0)),
        out_shape=jax.ShapeDtypeStruct((S5_SG, nb, S5_SG_IN), BF16),
        compiler_params=_cparams(1), name=name)(x, perm)


def _s5_from_chunks(y, perm, *, name):
    _, nb, _ = y.shape

    def body(y_ref, perm_ref, o_ref):
        grouped = jnp.concatenate([y_ref[k] for k in range(S5_HALF_SGS)], axis=1)
        hi = grouped.astype(BF16)
        lo = (grouped - hi.astype(F32)).astype(BF16)
        tok = _dot_nt(hi, perm_ref[...]) + _dot_nt(lo, perm_ref[...])
        for t in range(S5_CHUNK):
            o_ref[pl.ds(t, nb, stride=S5_CHUNK), :] = tok[:, t * LANE:(t + 1) * LANE]

    return pl.pallas_call(
        body, grid=(2,),
        in_specs=[pl.BlockSpec((S5_HALF_SGS, nb, S5_SG_IN), lambda h: (h, 0, 0)), _full_spec(perm)],
        out_specs=pl.BlockSpec((nb * S5_CHUNK, LANE), lambda h: (0, h)),
        out_shape=jax.ShapeDtypeStruct((nb * S5_CHUNK, D_GROUP), F32),
        compiler_params=_cparams(1), name=name)(y, perm)


def _s5_core_fwd(u2, m2, pre, pim, qre, qim, a16, *, name):
    sg, nb, nin = u2.shape
    st2 = 2 * S5_SG_ST
    pad = nb // 2

    def body(u_ref, m_ref, pre_ref, pim_ref, qre_ref, qim_ref, a_ref, y_ref, x_ref, buf0, buf1):
        @pl.when(pl.program_id(0) == 0)
        def _():
            buf0[0:pad, :] = jnp.zeros((pad, st2), F32)
            buf1[0:pad, :] = jnp.zeros((pad, st2), F32)

        u = u_ref[...]
        buf0[pad:pad + nb, :S5_SG_ST] = _dot(u, pre_ref[...])
        buf0[pad:pad + nb, S5_SG_ST:] = _dot(u, pim_ref[...])
        xbuf = (buf0, buf1)[_cscan_levels((buf0, buf1), a_ref, nb, pad, reverse=False)]
        x_ref[...] = xbuf[pad:pad + nb, :]
        xprev = xbuf[pad - 1:pad - 1 + nb, :]
        y_ref[...] = _dot(u, m_ref[...]) + _dot(xprev[:, :S5_SG_ST], qre_ref[...]) + _dot(xprev[:, S5_SG_ST:], qim_ref[...])

    ins = [u2, m2, pre, pim, qre, qim, a16]
    return pl.pallas_call(
        body, grid=(sg,), in_specs=[pl.BlockSpec((None,) + a.shape[1:], lambda i: (i, 0, 0)) for a in ins],
        out_specs=[pl.BlockSpec((None, nb, nin), lambda i: (i, 0, 0)), pl.BlockSpec((None, nb, st2), lambda i: (i, 0, 0))],
        out_shape=[jax.ShapeDtypeStruct((sg, nb, nin), F32), jax.ShapeDtypeStruct((sg, nb, st2), F32)],
        scratch_shapes=[pltpu.VMEM((pad + nb, st2), F32), pltpu.VMEM((pad + nb, st2), F32)],
        compiler_params=_cparams(1), name=name)(*ins)


def _s5_core_bwd(u2, dy2, x_all, m2, pre, pim, qre, qim, a16, *, name):
    sg, nb, nin = u2.shape
    half = S5_SG_ST
    st2 = 2 * half
    pad = nb // 2

    def body(u_ref, dy_ref, x_ref, m_ref, pre_ref, pim_ref, qre_ref, qim_ref, a_ref,
             du_ref, dm_ref, dpre_ref, dpim_ref, dqre_ref, dqim_ref, da_ref, buf2, buf3, xp):
        @pl.when(pl.program_id(0) == 0)
        def _():
            buf2[nb:nb + pad, :] = jnp.zeros((pad, st2), F32)
            buf3[nb:nb + pad, :] = jnp.zeros((pad, st2), F32)
            xp[0:SUBLANE, :] = jnp.zeros((SUBLANE, st2), F32)

        u = u_ref[...]
        dy = dy_ref[...]
        dm_ref[...] = _dot_tn(u, dy)
        xp[SUBLANE:SUBLANE + nb, :] = x_ref[...]
        xprev = xp[SUBLANE - 1:SUBLANE - 1 + nb, :]
        xre, xim = xprev[:, :half], xprev[:, half:]
        dqre_ref[...] = _dot_tn(xre, dy)
        dqim_ref[...] = _dot_tn(xim, dy)
        buf2[0:nb, :half] = _dot_nt(dy, qre_ref[...])
        buf2[0:nb, half:] = _dot_nt(dy, qim_ref[...])
        mbuf = (buf2, buf3)[_cscan_levels((buf2, buf3), a_ref, nb, 0, reverse=True)]
        lam = mbuf[1:1 + nb, :]
        lre, lim = lam[:, :half], lam[:, half:]
        dpre_ref[...] = _dot_tn(u, lre)
        dpim_ref[...] = _dot_tn(u, lim)
        du_ref[...] = _dot_nt(dy, m_ref[...]) + _dot_nt(lre, pre_ref[...]) + _dot_nt(lim, pim_ref[...])
        da_ref[:, :half] = _colsum(lre * xre + lim * xim)
        da_ref[:, half:] = _colsum(lim * xre - lre * xim)

    ins = [u2, dy2, x_all, m2, pre, pim, qre, qim, a16]
    outs = [jax.ShapeDtypeStruct((sg, nb, nin), F32)] + [jax.ShapeDtypeStruct(a.shape, F32) for a in (m2, pre, pim, qre, qim)] + \
           [jax.ShapeDtypeStruct((sg, 1, st2), F32)]
    return pl.pallas_call(
        body, grid=(sg,), in_specs=[pl.BlockSpec((None,) + a.shape[1:], lambda i: (i, 0, 0)) for a in ins],
        out_specs=[pl.BlockSpec((None,) + o.shape[1:], lambda i: (i, 0, 0)) for o in outs], out_shape=outs,
        scratch_shapes=[pltpu.VMEM((nb + pad, st2), F32), pltpu.VMEM((nb + pad, st2), F32),
                        pltpu.VMEM((SUBLANE + nb, st2), F32)],
        compiler_params=_cparams(1), name=name)(*ins)


def _s5_glu_fwd(y1, wglu, bglu, *, name, rider=None):
    s = y1.shape[0]
    t = _seq_tile(s, SEQ_TILE)

    def body(y1_ref, wglu_ref, bglu_ref, out_ref):
        y2 = _gelu(y1_ref[...])
        out_ref[...] = (y2 * _sigmoid(_dot(y2, wglu_ref[...]) + bglu_ref[...])).astype(BF16)

    return _call(
        body, grid=(s // t,), ins=[y1, wglu, bglu],
        in_specs=[pl.BlockSpec((t, D_GROUP), lambda i: (i, 0)), _full_spec(wglu), _full_spec(bglu)],
        out_specs=[pl.BlockSpec((t, D_GROUP), lambda i: (i, MIX_S5))], outs=[jax.ShapeDtypeStruct((s, D_MODEL), BF16)],
        name=name, rider=rider)


def _s5_glu_bwd(y1, dmix, wglu, bglu, *, name):
    s = y1.shape[0]
    t = _seq_tile(s, SEQ_TILE)

    def body(y1_ref, do_ref, wglu_ref, bglu_ref, dy1_ref, dwglu_ref, dbglu_ref):
        @pl.when(pl.program_id(0) == 0)
        def _():
            dwglu_ref[...] = jnp.zeros_like(dwglu_ref)
            dbglu_ref[...] = jnp.zeros_like(dbglu_ref)

        dout = do_ref[...]
        y2, dgelu = _gelu_and_grad(y1_ref[...])
        sg = _sigmoid(_dot(y2, wglu_ref[...]) + bglu_ref[...])
        dz = dout * y2 * sg * (1.0 - sg)
        dwglu_ref[...] += _dot_tn(y2, dz)
        dbglu_ref[...] += _colsum(dz)
        dy1_ref[...] = (dout * sg + _dot_nt(dz, wglu_ref[...])) * dgelu

    outs = [jax.ShapeDtypeStruct((s, D_GROUP), F32), jax.ShapeDtypeStruct((D_GROUP, D_GROUP), F32),
            jax.ShapeDtypeStruct((1, D_GROUP), F32)]
    return pl.pallas_call(
        body, grid=(s // t,),
        in_specs=[pl.BlockSpec((t, D_GROUP), lambda i: (i, 0)), pl.BlockSpec((t, D_GROUP), lambda i: (i, MIX_S5)),
                  _full_spec(wglu), _full_spec(bglu)],
        out_specs=[pl.BlockSpec((t, D_GROUP), lambda i: (i, 0)), _full_spec(outs[1]), _full_spec(outs[2])],
        out_shape=outs, compiler_params=_cparams(1), name=name)(y1, dmix, wglu, bglu)


def _pair_blockdiag(x):
    g, r, c = x.shape
    x = x.reshape(g // 2, 2, r, c)
    z = jnp.zeros_like(x[:, 0])
    return jnp.concatenate([jnp.concatenate([x[:, 0], z], axis=2), jnp.concatenate([z, x[:, 1]], axis=2)], axis=1)


def _s5_chunk_map(lam_re, lam_im, log_dt, b_re, b_im, c_re, c_im, d_skip):
    g, n, c, lc = S5_GROUPS, S5_STATE, S5_CH, S5_CHUNK
    dt = jnp.exp(log_dt)[:, None]
    mag, ang = lam_re * dt, lam_im * dt
    j = jnp.arange(lc + 1, dtype=F32)[:, None, None]
    pw_mag = jnp.exp(j * mag)
    pw_re, pw_im = pw_mag * jnp.cos(j * ang), pw_mag * jnp.sin(j * ang)
    a_re, a_im = pw_re[1], pw_im[1]
    den = lam_re * lam_re + lam_im * lam_im
    n_re = a_re - 1.0
    k_re = (n_re * lam_re + a_im * lam_im) / den
    k_im = (a_im * lam_re - n_re * lam_im) / den
    bb_re = k_re[..., None] * b_re - k_im[..., None] * b_im
    bb_im = k_re[..., None] * b_im + k_im[..., None] * b_re
    e_re = pw_re[:lc, :, :, None] * bb_re - pw_im[:lc, :, :, None] * bb_im
    e_im = pw_re[:lc, :, :, None] * bb_im + pw_im[:lc, :, :, None] * bb_re
    kern = jnp.einsum("gdn,jgnc->jgdc", c_re, e_re) - jnp.einsum("gdn,jgnc->jgdc", c_im, e_im)
    lags = jnp.pad(jnp.transpose(kern, (1, 3, 0, 2)), ((0, 0), (0, 0), (lc - 1, 0), (0, 0)))
    lags = lags.reshape(g, c, (2 * lc - 1) * c)
    m = jnp.stack([lags[:, :, (lc - 1 - s) * c:(2 * lc - 1 - s) * c] for s in range(lc)], axis=1).reshape(g, lc * c, lc * c)
    skip = jnp.tile(d_skip.reshape(g, 1, c), (1, lc, 1)).reshape(g, lc * c)
    m = m + jnp.eye(lc * c, dtype=F32)[None] * skip[:, None, :]
    p_re = jnp.transpose(e_re[::-1], (1, 0, 3, 2)).reshape(g, lc * c, n)
    p_im = jnp.transpose(e_im[::-1], (1, 0, 3, 2)).reshape(g, lc * c, n)
    f_re = c_re[None] * pw_re[1:, :, None, :] - c_im[None] * pw_im[1:, :, None, :]
    f_im = c_re[None] * pw_im[1:, :, None, :] + c_im[None] * pw_re[1:, :, None, :]
    q_re = jnp.transpose(f_re, (1, 3, 0, 2)).reshape(g, n, lc * c)
    q_im = -jnp.transpose(f_im, (1, 3, 0, 2)).reshape(g, n, lc * c)
    a16 = jnp.concatenate([pw_re[lc].reshape(S5_SG, 1, S5_SG_ST), pw_im[lc].reshape(S5_SG, 1, S5_SG_ST)], axis=2)
    return (_pair_blockdiag(m), _pair_blockdiag(p_re), _pair_blockdiag(p_im), _pair_blockdiag(q_re),
            _pair_blockdiag(q_im), a16)


def _s5_a16_powers(a16, nlev):
    half = S5_SG_ST
    re, im = a16[:, :, :half], a16[:, :, half:]
    rows = []
    for _ in range(nlev):
        rows.append(jnp.concatenate([re, im], axis=2))
        re, im = re * re - im * im, 2.0 * re * im
    n_rows = -(-nlev // SUBLANE) * SUBLANE
    rows += [jnp.zeros_like(rows[0])] * (n_rows - nlev)
    return lax.stop_gradient(jnp.concatenate(rows, axis=1))


CV_TILE = 256
CV_PAD = 32
CV_CHUNK = 64


def _shifted_copies(buf, shifted, rows):
    n = rows - SUBLANE
    for s in range(1, SUBLANE):
        shifted[s - 1, 0:n, :] = buf[s:s + n, :]


def _window(buf, shifted, o, ch):
    q, s = divmod(o, SUBLANE)
    if s == 0:
        return buf[o:o + ch, :]
    return shifted[s - 1, q * SUBLANE:q * SUBLANE + ch, :]


def _gn_stats(c, mavg):
    mu = _dot_hi(c, mavg)
    cen = c - mu
    var = _dot_hi(cen * cen, mavg)
    rstd = lax.rsqrt(var + LN_EPS)
    return cen * rstd, rstd


def _cv_fwd(h_in, cw, cb, gng, gnb, mavg, wpw, bpw, mix, *, name, rider=None):
    s = h_in.shape[0]
    t = _seq_tile(s, CV_TILE)
    ch = min(CV_CHUNK, t)

    def body(v_ref, g_ref, cw_ref, cb_ref, gng_ref, gnb_ref, mavg_ref, wpw_ref, bpw_ref, _mix_in, out_ref, c_ref, xpad,
             shifted):
        @pl.when(pl.program_id(0) == 0)
        def _():
            xpad[0:CV_PAD, :] = jnp.zeros((CV_PAD, D_GROUP), F32)

        xpad[CV_PAD:CV_PAD + t, :] = v_ref[...] * _sigmoid(g_ref[...])
        _shifted_copies(xpad, shifted, t + CV_PAD)
        for r0 in range(0, t, ch):
            acc = jnp.broadcast_to(cb_ref[...], (ch, D_GROUP))
            for k in range(CONV_WIDTH):
                o = CV_PAD - (CONV_WIDTH - 1) + k + r0
                acc = acc + cw_ref[k:k + 1, :] * _window(xpad, shifted, o, ch)
            c_ref[r0:r0 + ch, :] = acc
        xpad[0:CV_PAD, :] = xpad[t:t + CV_PAD, :]
        xn, _ = _gn_stats(c_ref[...], mavg_ref[...])
        gn = xn * gng_ref[...] + gnb_ref[...]
        out_ref[...] = (_dot(gn * _sigmoid(gn), wpw_ref[...]) + bpw_ref[...]).astype(BF16)

    ins = [h_in, h_in, cw, cb, gng, gnb, mavg, wpw, bpw, mix]
    in_specs = [pl.BlockSpec((t, D_GROUP), lambda i: (i, COL_CV_V)), pl.BlockSpec((t, D_GROUP), lambda i: (i, COL_CV_G))] + \
               [_full_spec(a) for a in ins[2:9]] + [_ANY]
    return _call(
        body, grid=(s // t,), ins=ins, in_specs=in_specs,
        out_specs=[pl.BlockSpec((t, D_GROUP), lambda i: (i, MIX_CV)), pl.BlockSpec((t, D_GROUP), lambda i: (i, 0))],
        outs=[jax.ShapeDtypeStruct((s, D_MODEL), BF16), jax.ShapeDtypeStruct((s, D_GROUP), F32)],
        aliases={9: 0},
        scratch=[pltpu.VMEM((CV_PAD + t, D_GROUP), F32), pltpu.VMEM((SUBLANE - 1, CV_PAD + t, D_GROUP), F32)],
        name=name, rider=rider)


def _cv_bwd(h_in, c, dmix, cw, gng, gnb, mavg, wpw, *, name, rider=None):
    s = h_in.shape[0]
    t = _seq_tile(s, CV_TILE)
    nt = s // t
    ch = min(CV_CHUNK, t)

    def body(v_ref, g_ref, c_ref, do_ref, cw_ref, gng_ref, gnb_ref, mavg_ref, wpw_ref,
             dvg_ref, dwpw_ref, dcw_ref, dbpw_ref, dgg_ref, dgb_ref, dcb_ref, dcpad, hgbuf, shifted):
        @pl.when(pl.program_id(0) == 0)
        def _():
            dcpad[t:t + CV_PAD, :] = jnp.zeros((CV_PAD, D_GROUP), F32)
            for r in (dwpw_ref, dcw_ref, dbpw_ref, dgg_ref, dgb_ref, dcb_ref):
                r[...] = jnp.zeros_like(r)

        mavg = mavg_ref[...]
        xn, rstd = _gn_stats(c_ref[...], mavg)
        gg = gng_ref[...]
        gn = xn * gg + gnb_ref[...]
        sg = _sigmoid(gn)
        dout = do_ref[...]
        dwpw_ref[...] += _dot_tn(gn * sg, dout)
        dbpw_ref[...] += _colsum(dout)
        dgn = _dot_nt(dout, wpw_ref[...]) * (sg * (1.0 + gn * (1.0 - sg)))
        dgg_ref[...] += _colsum(dgn * xn)
        dgb_ref[...] += _colsum(dgn)
        dxn = dgn * gg
        dc = rstd * (dxn - _dot_hi(dxn, mavg) - xn * _dot_hi(dxn * xn, mavg))
        dcb_ref[...] += _colsum(dc)
        dcpad[0:t, :] = dc

        v = v_ref[...]
        sgm = _sigmoid(g_ref[...])
        hgbuf[...] = v * sgm
        _shifted_copies(dcpad, shifted, t + CV_PAD)
        for r0 in range(0, t, ch):
            hg = hgbuf[r0:r0 + ch, :]
            acc = jnp.zeros((ch, D_GROUP), F32)
            for k in range(CONV_WIDTH):
                o = (CONV_WIDTH - 1) - k + r0
                sh = _window(dcpad, shifted, o, ch)
                acc = acc + cw_ref[k:k + 1, :] * sh
                dcw_ref[k:k + 1, :] += _colsum(hg * sh)
            hgbuf[r0:r0 + ch, :] = acc
        dcpad[t:t + CV_PAD, :] = dcpad[0:CV_PAD, :]
        dhg = hgbuf[...]
        dvg_ref[:, :D_GROUP] = dhg * sgm
        dvg_ref[:, D_GROUP:] = dhg * v * sgm * (1.0 - sgm)

    def rev(col):
        return lambda i: (nt - 1 - i, col)

    ins = [h_in, h_in, c, dmix, cw, gng, gnb, mavg, wpw]
    in_specs = [pl.BlockSpec((t, D_GROUP), rev(COL_CV_V)), pl.BlockSpec((t, D_GROUP), rev(COL_CV_G)),
                pl.BlockSpec((t, D_GROUP), rev(0)), pl.BlockSpec((t, D_GROUP), rev(MIX_CV))] + [_full_spec(a) for a in ins[4:]]
    vec = jax.ShapeDtypeStruct((1, D_GROUP), F32)
    outs = [jax.ShapeDtypeStruct((s, N_IN_COLS), F32),
            jax.ShapeDtypeStruct((D_GROUP, D_GROUP), F32), jax.ShapeDtypeStruct((CV_PAD, D_GROUP), F32), vec, vec, vec, vec]
    out_specs = [pl.BlockSpec((t, 2 * D_GROUP), rev(COL_CV_V // 2))] + [_full_spec(o) for o in outs[1:]]
    return _call(
        body, grid=(nt,), ins=ins, in_specs=in_specs, out_specs=out_specs, outs=outs,
        scratch=[pltpu.VMEM((t + CV_PAD, D_GROUP), F32), pltpu.VMEM((t, D_GROUP), F32),
                 pltpu.VMEM((SUBLANE - 1, t + CV_PAD, D_GROUP), F32)], name=name, rider=rider)


LRU_TILE = 256


def _lru_gates(xc, wr_ref, br_ref, wi_ref, bi_ref, sp_ref):
    r = _sigmoid(_dot(xc, wr_ref[...]) + br_ref[...])
    i = _sigmoid(_dot(xc, wi_ref[...]) + bi_ref[...])
    log_a = -LRU_C * r * sp_ref[...]
    a = jnp.exp(log_a)
    m = jnp.sqrt(_neg_expm1(2.0 * log_a))
    return r, i, a, m


def _lru_fwd(h_in, lcw, lcb, wr, br, wi, bi, sp, mix, *, name, rider=None):
    s = h_in.shape[0]
    t = _seq_tile(s, LRU_TILE)
    pad = max(t // 2, SUBLANE)

    def body(xg_ref, xr_ref, lcw_ref, lcb_ref, wr_ref, br_ref, wi_ref, bi_ref, sp_ref, _mix_in,
             out_ref, xc_ref, h_ref, xpad, a0, a1, b0, b1, carry):
        @pl.when(pl.program_id(0) == 0)
        def _():
            xpad[0:SUBLANE, :] = jnp.zeros((SUBLANE, D_GROUP), F32)
            for bf in (a0, a1, b0, b1):
                bf[0:pad, :] = jnp.zeros((pad, D_GROUP), F32)
            carry[...] = jnp.zeros_like(carry)

        xpad[SUBLANE:SUBLANE + t, :] = xr_ref[...]
        xc = jnp.broadcast_to(lcb_ref[...], (t, D_GROUP))
        for k in range(LRU_CONV_WIDTH):
            o = SUBLANE - (LRU_CONV_WIDTH - 1) + k
            xc = xc + lcw_ref[k:k + 1, :] * xpad[o:o + t, :]
        xpad[0:SUBLANE, :] = xpad[t:t + SUBLANE, :]
        xc_ref[...] = xc
        _, i, a, m = _lru_gates(xc, wr_ref, br_ref, wi_ref, bi_ref, sp_ref)
        a0[pad:pad + t, :] = a
        b0[pad:pad + t, :] = m * (i * xc)
        b0[pad:pad + 1, :] += a0[pad:pad + 1, :] * carry[0:1, :]
        fin = _rscan_levels((a0, a1), (b0, b1), t, pad, reverse=False)
        hbuf = (b0, b1)[fin]
        carry[0:1, :] = hbuf[pad + t - 1:pad + t, :]
        h = hbuf[pad:pad + t, :]
        h_ref[...] = h
        out_ref[...] = (h * _gelu(xg_ref[...])).astype(BF16)

    ins = [h_in, h_in, lcw, lcb, wr, br, wi, bi, sp, mix]
    row = pl.BlockSpec((t, D_GROUP), lambda i: (i, 0))
    in_specs = [pl.BlockSpec((t, D_GROUP), lambda i: (i, COL_LRU_G)), pl.BlockSpec((t, D_GROUP), lambda i: (i, COL_LRU_X))] + \
               [_full_spec(a) for a in ins[2:9]] + [_ANY]
    return _call(
        body, grid=(s // t,), ins=ins, in_specs=in_specs,
        out_specs=[pl.BlockSpec((t, D_GROUP), lambda i: (i, MIX_LRU)), row, row],
        outs=[jax.ShapeDtypeStruct((s, D_MODEL), BF16)] + [jax.ShapeDtypeStruct((s, D_GROUP), F32)] * 2,
        aliases={9: 0},
        scratch=[pltpu.VMEM((SUBLANE + t, D_GROUP), F32)] + [pltpu.VMEM((pad + t, D_GROUP), F32)] * 4 +
                [pltpu.VMEM((SUBLANE, D_GROUP), F32)],
        name=name, rider=rider)


def _lru_bwd(h_in, xc_all, h_all, dmix, lcw, wr, br, wi, bi, sp, dh_all, *, name):
    s = h_in.shape[0]
    t = _seq_tile(s, LRU_TILE)
    nt = s // t
    pad = max(t // 2, SUBLANE)
    tb = t // SUBLANE

    def body(xg_ref, xr_ref, xc_ref, h_ref, hprev_ref, do_ref, lcw_ref, wr_ref, br_ref, wi_ref, bi_ref, sp_ref, _dh_in,
             dgr_ref, dwr_ref, dwi_ref, dlcw_ref, dbr_ref, dbi_ref, dsp_ref, dlcb_ref,
             a0, a1, b0, b1, hp, dxpad, carry):
        pid = pl.program_id(0)

        @pl.when(pid == 0)
        def _():
            for bf in (a0, a1, b0, b1):
                bf[pad + t:pad + t + pad, :] = jnp.zeros((pad, D_GROUP), F32)
            dxpad[t:t + SUBLANE, :] = jnp.zeros((SUBLANE, D_GROUP), F32)
            carry[...] = jnp.zeros_like(carry)
            for r in (dwr_ref, dwi_ref, dlcw_ref, dbr_ref, dbi_ref, dsp_ref, dlcb_ref):
                r[...] = jnp.zeros_like(r)

        xc = xc_ref[...]
        h = h_ref[...]
        dout = do_ref[...]
        gate, dgate = _gelu_and_grad(xg_ref[...])
        dgr_ref[:, :D_GROUP] = dout * h * dgate
        r, i, a, m = _lru_gates(xc, wr_ref, br_ref, wi_ref, bi_ref, sp_ref)

        a0[pad:pad + t, :] = a
        b0[pad:pad + t, :] = dout * gate
        b0[pad + t - 1:pad + t, :] += carry[0:1, :]
        a1[pad:pad + t, :] = a0[pad + 1:pad + 1 + t, :]
        fin = _rscan_levels((a1, a0), (b0, b1), t, pad, reverse=True)
        lam = (b0, b1)[fin][pad:pad + t, :]
        carry[0:1, :] = a[0:1, :] * lam[0:1, :]

        is_first = pid == nt - 1
        hp[0:SUBLANE, :] = jnp.where(is_first, 0.0, hprev_ref[...])
        hp[SUBLANE:SUBLANE + t, :] = h
        hprev = hp[SUBLANE - 1:SUBLANE - 1 + t, :]

        ix = i * xc
        dmm = lam * ix
        dix = lam * m
        da = lam * hprev - dmm * (a / m)
        dlog_a = da * a
        dr = dlog_a * (-LRU_C * sp_ref[...])
        dsp_ref[...] += _colsum(dlog_a * (-LRU_C * r))
        dpr = dr * r * (1.0 - r)
        dpi = dix * xc * i * (1.0 - i)
        dbr_ref[...] += _colsum(dpr)
        dbi_ref[...] += _colsum(dpi)
        dwr_ref[...] += _dot_tn(xc, dpr)
        dwi_ref[...] += _dot_tn(xc, dpi)
        dxc = dix * i + _dot_nt(dpr, wr_ref[...]) + _dot_nt(dpi, wi_ref[...])
        dlcb_ref[...] += _colsum(dxc)

        dxpad[0:t, :] = dxc
        xr = xr_ref[...]
        dxr = jnp.zeros((t, D_GROUP), F32)
        for k in range(LRU_CONV_WIDTH):
            o = (LRU_CONV_WIDTH - 1) - k
            sh = dxpad[o:o + t, :]
            dxr = dxr + lcw_ref[k:k + 1, :] * sh
            dlcw_ref[k:k + 1, :] += _colsum(xr * sh)
        dxpad[t:t + SUBLANE, :] = dxpad[0:SUBLANE, :]
        dgr_ref[:, D_GROUP:] = dxr

    def rev(col):
        return lambda i: (nt - 1 - i, col)

    ins = [h_in, h_in, xc_all, h_all, h_all, dmix, lcw, wr, br, wi, bi, sp, dh_all]
    in_specs = [pl.BlockSpec((t, D_GROUP), rev(COL_LRU_G)), pl.BlockSpec((t, D_GROUP), rev(COL_LRU_X)),
                pl.BlockSpec((t, D_GROUP), rev(0)), pl.BlockSpec((t, D_GROUP), rev(0)),
                pl.BlockSpec((SUBLANE, D_GROUP), lambda i: (jnp.maximum((nt - 1 - i) * tb - 1, 0), 0)),
                pl.BlockSpec((t, D_GROUP), rev(MIX_LRU))] + [_full_spec(a) for a in ins[6:12]] + [_ANY]
    vec = jax.ShapeDtypeStruct((1, D_GROUP), F32)
    mat = jax.ShapeDtypeStruct((D_GROUP, D_GROUP), F32)
    outs = [jax.ShapeDtypeStruct((s, N_IN_COLS), F32), mat, mat, jax.ShapeDtypeStruct((SUBLANE, D_GROUP), F32),
            vec, vec, vec, vec]
    out_specs = [pl.BlockSpec((t, 2 * D_GROUP), rev(COL_LRU_G // 2))] + [_full_spec(o) for o in outs[1:]]
    return pl.pallas_call(
        body, grid=(nt,), in_specs=in_specs, out_specs=out_specs, out_shape=outs, input_output_aliases={12: 0},
        scratch_shapes=[pltpu.VMEM((pad + t + pad, D_GROUP), F32)] * 4 +
                       [pltpu.VMEM((SUBLANE + t, D_GROUP), F32), pltpu.VMEM((t + SUBLANE, D_GROUP), F32),
                        pltpu.VMEM((SUBLANE, D_GROUP), F32)],
        compiler_params=_cparams(1), name=name)(*ins)


def _blockdiag(w):
    h, d, _ = w.shape
    return jnp.tile(w.reshape(h * d, d), (1, h)) * _block_mask(h, d, d)


ATTN_TILE = 512
ATTN_SCALE = ATTN_HEAD_DIM ** -0.5


def _attn_big(kv):
    m = kv.shape[0]
    kbig = jnp.tile(kv[:, :D_GROUP].T, (1, ATTN_HEADS)) * _block_mask(ATTN_HEADS, ATTN_HEAD_DIM, m)
    vbig = jnp.tile(kv[:, D_GROUP:], (ATTN_HEADS, 1)) * _block_mask(ATTN_HEADS, m, ATTN_HEAD_DIM)
    return kbig, vbig


def _attn_probs(q, kbig_ref, m):
    sc = _dot(q, kbig_ref[...]) * ATTN_SCALE
    ps = []
    for h in range(ATTN_HEADS):
        sh = sc[:, h * m:(h + 1) * m]
        e = jnp.exp(sh - jnp.max(sh, axis=1, keepdims=True))
        ps.append(e / jnp.sum(e, axis=1, keepdims=True))
    return ps


def _attn_fwd(h_in, kbig, vbig, mix, *, name):
    s = h_in.shape[0]
    t = _seq_tile(s, ATTN_TILE)
    m = kbig.shape[1] // ATTN_HEADS

    def body(q_ref, kbig_ref, vbig_ref, _mix_in, o_ref):
        ps = _attn_probs(q_ref[...], kbig_ref, m)
        o_ref[...] = _dot(jnp.concatenate(ps, axis=1), vbig_ref[...]).astype(BF16)

    return pl.pallas_call(
        body, grid=(s // t,),
        in_specs=[pl.BlockSpec((t, D_GROUP), lambda i: (i, COL_Q)), _full_spec(kbig), _full_spec(vbig), _ANY],
        out_specs=pl.BlockSpec((t, D_GROUP), lambda i: (i, MIX_ATTN)),
        out_shape=jax.ShapeDtypeStruct((s, D_MODEL), BF16), input_output_aliases={3: 0},
        compiler_params=_cparams(1), name=name)(h_in, kbig, vbig, mix)


def _attn_bwd(h_in, dmix, kbig, vbig, du_s5, dh_all, *, name):
    s = h_in.shape[0]
    t = _seq_tile(s, ATTN_TILE)
    m = kbig.shape[1] // ATTN_HEADS

    def body(q_ref, do_ref, kbig_ref, vbig_ref, dus5_ref, _dh_in, dpair_ref, dk_ref, dv_ref):
        @pl.when(pl.program_id(0) == 0)
        def _():
            dk_ref[...] = jnp.zeros_like(dk_ref)
            dv_ref[...] = jnp.zeros_like(dv_ref)

        q = q_ref[...]
        dout = do_ref[...]
        ps = _attn_probs(q, kbig_ref, m)
        dp = _dot_nt(dout, vbig_ref[...])
        dss = []
        for h in range(ATTN_HEADS):
            dph = dp[:, h * m:(h + 1) * m]
            dss.append(ps[h] * (dph - jnp.sum(dph * ps[h], axis=1, keepdims=True)))
        ds = (jnp.concatenate(dss, axis=1) * ATTN_SCALE).astype(BF16)
        dv_ref[...] += _dot_tn(jnp.concatenate(ps, axis=1), dout)
        dpair_ref[:, :D_GROUP] = dus5_ref[...]
        dpair_ref[:, D_GROUP:] = _dot_nt(ds, kbig_ref[...])
        dk_ref[...] += _dot_tn(q, ds)

    assert (COL_S5, COL_Q) == (4, 5)
    outs = [jax.ShapeDtypeStruct((s, N_IN_COLS), F32), jax.ShapeDtypeStruct(kbig.shape, F32),
            jax.ShapeDtypeStruct(vbig.shape, F32)]
    return pl.pallas_call(
        body, grid=(s // t,),
        in_specs=[pl.BlockSpec((t, D_GROUP), lambda i: (i, COL_Q)), pl.BlockSpec((t, D_GROUP), lambda i: (i, MIX_ATTN)),
                  _full_spec(kbig), _full_spec(vbig), pl.BlockSpec((t, D_GROUP), lambda i: (i, 0)), _ANY],
        out_specs=[pl.BlockSpec((t, 2 * D_GROUP), lambda i: (i, COL_S5 // 2)), _full_spec(outs[1]), _full_spec(outs[2])],
        out_shape=outs, input_output_aliases={5: 0},
        compiler_params=_cparams(1), name=name)(h_in, dmix, kbig, vbig, du_s5, dh_all)


FFN_TILE = 128
FFN_COL_CHUNK = 256
FFN_ROW_CHUNK = 64


def _ffn_conv(pad_ref, w_ref, b_ref, r0, ch, c0):
    cc = FFN_COL_CHUNK
    acc = jnp.broadcast_to(b_ref[:, c0:c0 + cc], (ch, cc))
    for k in range(FFN_CONV_WIDTH):
        o = SUBLANE - (FFN_CONV_WIDTH - 1) + k + r0
        acc = acc + w_ref[k:k + 1, c0:c0 + cc] * pad_ref[o:o + ch, c0:c0 + cc]
    return acc


def _ffn_gate_fwd(u, fcw, fcb, *, name, rider=None):
    s = u.shape[0]
    t = _seq_tile(s, FFN_TILE)
    ch = min(FFN_ROW_CHUNK, t)
    cc = FFN_COL_CHUNK

    def body(u_ref, w_ref, b_ref, o_ref, uc_ref, upad):
        @pl.when(pl.program_id(0) == 0)
        def _():
            upad[0:SUBLANE, :] = jnp.zeros((SUBLANE, 2 * D_FF), F32)

        upad[SUBLANE:SUBLANE + t, :] = u_ref[...].astype(F32)
        for c0 in range(0, D_FF, cc):
            for r0 in range(0, t, ch):
                val = _ffn_conv(upad, w_ref, b_ref, r0, ch, c0)
                gt = _ffn_conv(upad, w_ref, b_ref, r0, ch, c0 + D_FF)
                o_ref[r0:r0 + ch, c0:c0 + cc] = (val * _gelu(gt)).astype(BF16)
                uc_ref[r0:r0 + ch, c0:c0 + cc] = val.astype(BF16)
                uc_ref[r0:r0 + ch, c0 + D_FF:c0 + D_FF + cc] = gt.astype(BF16)
        upad[0:SUBLANE, :] = upad[t:t + SUBLANE, :]

    return _call(
        body, grid=(s // t,), ins=[u, fcw, fcb],
        in_specs=[pl.BlockSpec((t, 2 * D_FF), lambda i: (i, 0)), _full_spec(fcw), _full_spec(fcb)],
        out_specs=[pl.BlockSpec((t, D_FF), lambda i: (i, 0)), pl.BlockSpec((t, 2 * D_FF), lambda i: (i, 0))],
        outs=[jax.ShapeDtypeStruct((s, D_FF), BF16), jax.ShapeDtypeStruct((s, 2 * D_FF), BF16)],
        scratch=[pltpu.VMEM((SUBLANE + t, 2 * D_FF), F32)], name=name, rider=rider)


def _ffn_gate_bwd(u, uc, dh, fcw, *, name, rider=None):
    s = u.shape[0]
    t = _seq_tile(s, FFN_TILE)
    nt = s // t
    ch = min(FFN_ROW_CHUNK, t)
    cc = FFN_COL_CHUNK

    def body(u_ref, uc_ref, dh_ref, w_ref, du_ref, dw_ref, db_ref, dpad):
        @pl.when(pl.program_id(0) == 0)
        def _():
            dpad[t:t + SUBLANE, :] = jnp.zeros((SUBLANE, 2 * D_FF), F32)
            dw_ref[...] = jnp.zeros_like(dw_ref)
            db_ref[...] = jnp.zeros_like(db_ref)

        for c0 in range(0, D_FF, cc):
            for r0 in range(0, t, ch):
                val = uc_ref[r0:r0 + ch, c0:c0 + cc].astype(F32)
                gt = uc_ref[r0:r0 + ch, c0 + D_FF:c0 + D_FF + cc].astype(F32)
                gl, dgl = _gelu_and_grad(gt)
                d = dh_ref[r0:r0 + ch, c0:c0 + cc].astype(F32)
                dpad[r0:r0 + ch, c0:c0 + cc] = d * gl
                dpad[r0:r0 + ch, c0 + D_FF:c0 + D_FF + cc] = d * val * dgl
        for c0 in range(0, 2 * D_FF, cc):
            dbs = jnp.zeros((1, cc), F32)
            dws = [jnp.zeros((1, cc), F32) for _ in range(FFN_CONV_WIDTH)]
            for r0 in range(0, t, ch):
                x = u_ref[r0:r0 + ch, c0:c0 + cc].astype(F32)
                acc = jnp.zeros((ch, cc), F32)
                for k in range(FFN_CONV_WIDTH):
                    o = (FFN_CONV_WIDTH - 1) - k + r0
                    sh = dpad[o:o + ch, c0:c0 + cc]
                    acc = acc + w_ref[k:k + 1, c0:c0 + cc] * sh
                    dws[k] = dws[k] + _colsum(x * sh)
                    if k == FFN_CONV_WIDTH - 1:
                        dbs = dbs + _colsum(sh)
                du_ref[r0:r0 + ch, c0:c0 + cc] = acc.astype(BF16)
            db_ref[:, c0:c0 + cc] += dbs
            for k in range(FFN_CONV_WIDTH):
                dw_ref[k:k + 1, c0:c0 + cc] += dws[k]
        dpad[t:t + SUBLANE, :] = dpad[0:SUBLANE, :]

    outs = [jax.ShapeDtypeStruct((s, 2 * D_FF), BF16), jax.ShapeDtypeStruct((SUBLANE, 2 * D_FF), F32),
            jax.ShapeDtypeStruct((1, 2 * D_FF), F32)]
    return _call(
        body, grid=(nt,), ins=[u, uc, dh, fcw],
        in_specs=[pl.BlockSpec((t, 2 * D_FF), lambda i: (nt - 1 - i, 0)),
                  pl.BlockSpec((t, 2 * D_FF), lambda i: (nt - 1 - i, 0)),
                  pl.BlockSpec((t, D_FF), lambda i: (nt - 1 - i, 0)), _full_spec(fcw)],
        out_specs=[pl.BlockSpec((t, 2 * D_FF), lambda i: (nt - 1 - i, 0)), _full_spec(outs[1]), _full_spec(outs[2])],
        outs=outs, scratch=[pltpu.VMEM((t + SUBLANE, 2 * D_FF), F32)], name=name, rider=rider)


def _adamw_body(g_ref, w_ref, m_ref, v_ref, go_ref, d_ref, mo_ref, vo_ref):
    inv_b1 = 1.0 - ADAM_B1 ** ADAM_STEP
    inv_b2 = 1.0 - ADAM_B2 ** ADAM_STEP
    g = g_ref[0].astype(F32)
    for dev in range(1, N_DEV):
        g = g + g_ref[dev].astype(F32)
    go_ref[...] = g
    mn = ADAM_B1 * m_ref[...] + (1.0 - ADAM_B1) * g
    vn = ADAM_B2 * v_ref[...] + (1.0 - ADAM_B2) * (g * g)
    mo_ref[...] = mn
    vo_ref[...] = vn
    d_ref[...] = -ADAM_LR * ((mn / inv_b1) / (jnp.sqrt(vn / inv_b2) + ADAM_EPS) + ADAM_WD * w_ref[...])


def _adamw(gstack, w, m, v, *, name):
    _, r, c = gstack.shape
    tr = _pick_rows(r, PACK_ROW_BLOCK)

    def body(*refs):
        _adamw_body(*refs)

    blk = pl.BlockSpec((tr, c), lambda i: (i, 0))
    sh = jax.ShapeDtypeStruct((r, c), F32)
    return pl.pallas_call(
        body, grid=(r // tr,),
        in_specs=[pl.BlockSpec((N_DEV, tr, c), lambda i: (0, i, 0)), blk, blk, blk],
        out_specs=[blk] * 4, out_shape=[sh] * 4,
        compiler_params=_cparams(1), name=name)(gstack, w, m, v)


def _adamw_layer(gstack, w, m, v, layer, into, *, name):
    n_layers, r, c = w.shape
    tr = _pick_rows(r, PACK_ROW_BLOCK)

    def body(g_ref, w_ref, m_ref, v_ref, *rest):
        _adamw_body(g_ref, w_ref, m_ref, v_ref, *rest[-4:])

    blk = pl.BlockSpec((None, tr, c), lambda i: (layer, i, 0))
    sh = jax.ShapeDtypeStruct((n_layers, r, c), F32)
    into = list(into or [])
    return pl.pallas_call(
        body, grid=(r // tr,),
        in_specs=[pl.BlockSpec((N_DEV, tr, c), lambda i: (0, i, 0)), blk, blk, blk] + [_ANY] * len(into),
        out_specs=[blk] * 4, out_shape=[sh] * 4, input_output_aliases={4 + k: k for k in range(len(into))},
        compiler_params=_cparams(1), name=name)(gstack, w, m, v, *into)


def _exchange(rider, *, name):
    n = rider.n

    def body(*refs):
        x_refs, out_refs, sems = refs[:n], refs[n:2 * n], refs[2 * n:]
        rider.start(x_refs, out_refs, sems)
        rider.wait(x_refs, out_refs, sems)

    return pl.pallas_call(
        body, in_specs=[_ANY] * n, out_specs=[_ANY] * n, out_shape=rider.out_shapes(),
        scratch_shapes=rider.scratch(), name=name)(*rider.srcs)


def _pack_rows(n):
    rows = -(-n // PACK_COLS)
    return -(-rows // SUBLANE) * SUBLANE


def _pack(arrs, dtype):
    flat = jnp.concatenate([a.reshape(-1).astype(dtype) for a in arrs])
    rows = _pack_rows(flat.shape[0])
    flat = jnp.pad(flat, (0, rows * PACK_COLS - flat.shape[0]))
    return flat.reshape(rows, PACK_COLS)


def _pack_lead(arrs, dtype):
    flat = jnp.concatenate([a.reshape(N_DEV, -1).astype(dtype) for a in arrs], axis=1)
    rows = _pack_rows(flat.shape[1])
    flat = jnp.pad(flat, ((0, 0), (0, rows * PACK_COLS - flat.shape[1])))
    return flat.reshape(N_DEV, rows, PACK_COLS)


def _pack_layers(arrs, dtype):
    n_layers = arrs[0].shape[0]
    flat = jnp.concatenate([a.reshape(n_layers, -1).astype(dtype) for a in arrs], axis=1)
    rows = _pack_rows(flat.shape[1])
    flat = jnp.pad(flat, ((0, 0), (0, rows * PACK_COLS - flat.shape[1])))
    return flat.reshape(n_layers, rows, PACK_COLS)


def _unpack_layers(packed, shapes):
    flat = packed.reshape(packed.shape[0], -1)
    out, pos = [], 0
    for sh in shapes:
        n = math.prod(sh[1:])
        out.append(flat[:, pos:pos + n].reshape(sh))
        pos += n
    return out


def _unpack(packed, shapes, lead=False):
    flat = packed.reshape(N_DEV, -1) if lead else packed.reshape(-1)
    out, pos = [], 0
    for sh in shapes:
        n = math.prod(sh)
        out.append(flat[:, pos:pos + n].reshape((N_DEV,) + tuple(sh)) if lead else flat[pos:pos + n].reshape(sh))
        pos += n
    return out


def _join_shards(stacked, axis):
    return jnp.concatenate([stacked[d] for d in range(N_DEV)], axis=axis)


def _split_shards(full, axis):
    return jnp.stack(jnp.split(full, N_DEV, axis=axis), axis=0)


def _perm_in_cols(a, inverse=False):
    blocks = jnp.split(a, 6, axis=-1)
    if inverse:
        order = [IN_PERM.index(j) for j in range(6)]
    else:
        order = list(IN_PERM)
    return jnp.concatenate([blocks[j] for j in order], axis=-1)


def _row(v):
    return v.reshape(1, -1)


def _pad_rows(w, rows):
    return jnp.pad(w, ((0, rows - w.shape[0]), (0, 0)))


def _gn_avg_matrix():
    return _block_mask(GN_GROUPS, D_GROUP // GN_GROUPS, D_GROUP // GN_GROUPS) / (D_GROUP // GN_GROUPS)


def _layer_params(p, l):
    q = {}
    s5_mats, q["s5_vjp"] = jax.vjp(_s5_chunk_map, p["s5_lam_re"][l], p["s5_lam_im"][l], p["s5_log_dt"][l],
                                   p["s5_b_re"][l], p["s5_b_im"][l], p["s5_c_re"][l], p["s5_c_im"][l], p["s5_d"][l])
    q["s5_mats"] = [m.astype(BF16) for m in s5_mats[:5]]
    q["s5_a16"] = s5_mats[5]
    (q["wr"], q["wi"]), q["lru_w_vjp"] = jax.vjp(lambda r, i: (_blockdiag(r), _blockdiag(i)), p["lru_w_r"][l], p["lru_w_i"][l])
    q["wr"], q["wi"] = q["wr"].astype(BF16), q["wi"].astype(BF16)
    q["sp"], q["sp_vjp"] = jax.vjp(lambda lam: _row(jax.nn.softplus(-lam)), p["lru_lam"][l])
    return q


ROW_PARTS = ("a", "b", "c", "d")
WEIGHT_RIDES = {(0, "ln_in_fwd"): [("w_in", 0)],
                (0, "inproj"): [("attn_w_kv", 0), ("w_out", 0), ("small_pack", 0)],
                (0, "cv_fwd"): [("ffn_w_up#a", 0), ("ffn_w_up#b", 0)],
                (0, "lru_fwd"): [("ffn_w_up#c", 0)],
                (0, "outproj"): [("ffn_w_up#d", 0)],
                (0, "ffn_up"): [("ffn_w_down", 0), ("w_in", 1), ("attn_w_kv", 1), ("w_out", 1)],
                (0, "ffn_gate_fwd"): [("ffn_w_up", 1)],
                (0, "ffn_down"): [("ffn_w_down", 1)]}
GRAD_RIDES = {(1, "ffn_gate_bwd"): [("ffn_w_down", 1)],
              (0, "dw_down"): [("w_out", 1), ("attn_w_kv", 1), ("w_in", 1)],
              (0, "dhff"): [("rep", 1), ("ssh", 1)],
              (0, "ffn_gate_bwd"): [("ffn_w_up", 1)],
              (0, "dw_up"): [("ffn_w_down", 0)],
              (0, "dx1"): [("ffn_w_up", 0)],
              (0, "cv_bwd"): [("w_out", 0)],
              (0, "dw_in"): [("attn_w_kv", 0), ("ssh", 0), ("rep", 0)],
              (0, "dxs"): [("w_in", 0)]}


def _join_cols(pieces, *, name):
    n_dev, k, c = pieces[0].shape
    assert (2 * c) % LANE == 0 and all(p.shape == pieces[0].shape for p in pieces)
    n_p = len(pieces)

    def body(*refs):
        o_ref = refs[n_p]
        for i in range(n_p):
            @pl.when(pl.program_id(0) == i)
            def _(i=i):
                o_ref[...] = jnp.concatenate([refs[i][0], refs[i][1]], axis=1)

    return pl.pallas_call(
        body, grid=(n_p, n_dev // 2),
        in_specs=[pl.BlockSpec((2, k, c), lambda i, j, p=p: (jnp.where(i == p, j, 0), 0, 0)) for p in range(n_p)],
        out_specs=pl.BlockSpec((k, 2 * c), lambda i, j: (i, j)),
        out_shape=jax.ShapeDtypeStruct((n_p * k, n_dev * c), pieces[0].dtype),
        compiler_params=_cparams(2), name=name)(*pieces)


def _split_cols(full, *, name):
    k, n = full.shape
    c = n // N_DEV
    assert (2 * c) % LANE == 0

    def body(x_ref, o_ref):
        o_ref[0] = x_ref[:, :c]
        o_ref[1] = x_ref[:, c:]

    return pl.pallas_call(
        body, grid=(N_DEV // 2,), in_specs=[pl.BlockSpec((k, 2 * c), lambda j: (0, j))],
        out_specs=pl.BlockSpec((2, k, c), lambda j: (j, 0, 0)), out_shape=jax.ShapeDtypeStruct((N_DEV, k, c), full.dtype),
        compiler_params=_cparams(1), name=name)(full)


def _assemble_weight(n, pieces, layer=0):
    if SHARDED[n] == 2:
        full = _join_cols(pieces, name=f"l{layer}_join_{n}")
        return _perm_in_cols(full) if n == "w_in" else full
    (gathered,) = pieces
    return gathered.reshape(-1, gathered.shape[-1])


def _grad_source(n, g, layer=0):
    g = g.astype(BF16)
    if SHARDED[n] == 2:
        if n == "w_in":
            g = _perm_in_cols(g, inverse=True)
        return _split_cols(g, name=f"l{layer}_split_d{n}"), "lead"
    return g, "rows"


def _hosted(fn, keys_rider, land, *args, **kw):
    keys, rider = keys_rider
    if rider is None:
        return fn(*args, **kw)
    out, routs = fn(*args, rider=rider, **kw)
    land(keys, routs)
    return out


def _local_step(x, mem, target, p, big_w, shards=None, unpack_small=None):
    dist = shards is not None
    gdt = BF16 if dist else F32
    small, saved = {}, []
    big_g, ready, recv = {}, {}, {}
    mavg = _gn_avg_matrix()
    s5_perm = _s5_perm()

    def weight_rider(l, host):
        keys = WEIGHT_RIDES.get((l, host), []) if dist else []
        return keys, (_Rider([shards[n][ll] for n, ll in keys], ["all"] * len(keys)) if keys else None)

    halves = {}

    def land_weights(keys, routs):
        for (n, ll), r in zip(keys, routs):
            if n == "small_pack":
                p.update(unpack_small(r))
            elif "#" in n:
                base = n.split("#")[0]
                halves[(n, ll)] = r
                parts = [halves.get((base + "#" + tag, ll)) for tag in ROW_PARTS]
                if all(part is not None for part in parts):
                    big_w[base][ll] = _assemble_weight(base, parts, ll)
            else:
                big_w[n][ll] = _assemble_weight(n, [r], ll)

    def grad_rider(l, host):
        keys = [k for k in GRAD_RIDES.get((l, host), []) if k in ready] if dist else []
        return keys, (_Rider([ready[k][0] for k in keys], [ready[k][1] for k in keys]) if keys else None)

    def land_grads(keys, routs):
        for k, r in zip(keys, routs):
            recv[k] = r
            del ready[k]

    def big_grad(n, l, g):
        if dist:
            ready[(n, l)] = _grad_source(n, g, l)
        else:
            big_g[(n, l)] = g

    xs = _hosted(_ln_fwd, weight_rider(0, "ln_in_fwd"), land_weights, x, _row(p["ln_in_g"]), _row(p["ln_in_b"]),
                 name="ln_in_fwd")
    for l in range(DEPTH):
        q = _layer_params(p, l)
        n = f"l{l}_"
        hin = _hosted(_mm, weight_rider(l, "inproj"), land_weights, xs, big_w["w_in"][l], bias=_row(p["b_in"][l]),
                      name=n + "inproj")
        nb = hin.shape[0] // S5_CHUNK
        s5_pows = _s5_a16_powers(q["s5_a16"], nb.bit_length() - 1)
        s5_u2 = _s5_to_chunks(hin, COL_S5 * (D_GROUP // LANE), s5_perm, name=n + "s5_in")
        s5_y2, s5_x = _s5_core_fwd(s5_u2, *q["s5_mats"], s5_pows, name=n + "s5_core_fwd")
        s5_y1 = _s5_from_chunks(s5_y2, s5_perm, name=n + "s5_out")
        (mix,), _ = _s5_glu_fwd(s5_y1, p["s5_w_glu"][l], _row(p["s5_b_glu"][l]), name=n + "s5_glu_fwd")
        cvw = _pad_rows(p["cv_w"][l], CV_PAD)
        keys, rd = weight_rider(l, "cv_fwd")
        (mix, cv_c), routs = _cv_fwd(hin, cvw, _row(p["cv_b"][l]), _row(p["cv_gn_g"][l]), _row(p["cv_gn_b"][l]), mavg,
                                     p["cv_w_pw"][l], _row(p["cv_b_pw"][l]), mix, name=n + "cv_fwd", rider=rd)
        land_weights(keys, routs)
        lcw = _pad_rows(p["lru_conv_w"][l], SUBLANE)
        keys, rd = weight_rider(l, "lru_fwd")
        (mix, lru_xc, lru_h), routs = _lru_fwd(hin, lcw, _row(p["lru_conv_b"][l]), q["wr"], _row(p["lru_b_r"][l]), q["wi"],
                                               _row(p["lru_b_i"][l]), q["sp"], mix, name=n + "lru_fwd", rider=rd)
        land_weights(keys, routs)
        kv = _mm(mem, big_w["attn_w_kv"][l], name=n + "kv")
        (kbig, vbig), kv_vjp = jax.vjp(_attn_big, kv)
        kbig, vbig = kbig.astype(BF16), vbig.astype(BF16)
        mix = _attn_fwd(hin, kbig, vbig, mix, name=n + "attn_fwd")
        r1, x1 = _hosted(_mm, weight_rider(l, "outproj"), land_weights, mix, big_w["w_out"][l], bias=_row(p["b_out"][l]),
                         res=xs, res_scale=ALPHA, ln=(_row(p["ln1_g"][l]), _row(p["ln1_b"][l])), name=n + "outproj")
        u = _hosted(_mm, weight_rider(l, "ffn_up"), land_weights, x1, big_w["ffn_w_up"][l], out_dtype=BF16,
                    name=n + "ffn_up")
        fcw = _pad_rows(p["ffn_conv_w"][l], SUBLANE)
        fcb = _row(p["ffn_conv_b"][l])
        keys, rd = weight_rider(l, "ffn_gate_fwd")
        (hff, uc), routs = _ffn_gate_fwd(u, fcw, fcb, name=n + "ffn_gate_fwd", rider=rd)
        land_weights(keys, routs)
        if l < DEPTH - 1:
            r2, x2 = _hosted(_mm, weight_rider(l, "ffn_down"), land_weights, hff, big_w["ffn_w_down"][l], res=x1,
                             res_scale=ALPHA, ln=(_row(p["ln2_g"][l]), _row(p["ln2_b"][l])), name=n + "ffn_down")
        else:
            r2, x2 = _mm(hff, big_w["ffn_w_down"][l], res=x1, res_scale=ALPHA, name=n + "ffn_down"), None
        saved.append(dict(q=q, xs=xs, hin=hin, s5_y1=s5_y1, s5_u2=s5_u2, s5_x=s5_x, s5_pows=s5_pows, cvw=cvw, cv_c=cv_c, lcw=lcw, lru_xc=lru_xc,
                          lru_h=lru_h, kbig=kbig, vbig=vbig, kv_vjp=kv_vjp, mix=mix, r1=r1, x1=x1, u=u, uc=uc, fcw=fcw,
                          hff=hff, r2=r2))
        xs = x2

    top = DEPTH - 1
    dr_top, dg_top, db_top, loss_blk = _loss_ln_bwd(saved[top]["r2"], _row(p["ln2_g"][top]), _row(p["ln2_b"][top]), target,
                                                     name="loss_ln_bwd")
    loss = loss_blk[0, 0]
    dx = None

    for l in reversed(range(DEPTH)):
        sv = saved[l]
        q = sv["q"]
        n = f"l{l}_"
        g = {}
        if l == top:
            dr2, g["ln2_g"], g["ln2_b"] = dr_top, dg_top, db_top
        else:
            dr2, g["ln2_g"], g["ln2_b"] = from_above
        big_grad("ffn_w_down", l, _hosted(_mm_tn, grad_rider(l, "dw_down"), land_grads, sv["hff"], dr2, out_dtype=gdt,
                                          name=n + "dw_down"))
        dhff = _hosted(_mm, grad_rider(l, "dhff"), land_grads, dr2, big_w["ffn_w_down"][l], trans_b=True,
                       out_dtype=BF16, name=n + "dhff")
        keys, rd = grad_rider(l, "ffn_gate_bwd")
        (du, dfw, g["ffn_conv_b"]), routs = _ffn_gate_bwd(sv["u"], sv["uc"], dhff, sv["fcw"], name=n + "ffn_gate_bwd",
                                                          rider=rd)
        land_grads(keys, routs)
        g["ffn_conv_w"] = dfw[:FFN_CONV_WIDTH]
        if dist:
            ready[("ffn_w_up", l)] = (_hosted(_mm_tn, grad_rider(l, "dw_up"), land_grads, sv["x1"], du, out_dtype=gdt,
                                              dev_cols=du.shape[1] // N_DEV, name=n + "dw_up"), "lead")
        else:
            big_grad("ffn_w_up", l, _mm_tn(sv["x1"], du, name=n + "dw_up"))
        dr1, g["ln1_g"], g["ln1_b"], g["b_out"] = _hosted(
            _mm, grad_rider(l, "dx1"), land_grads, du, big_w["ffn_w_up"][l], trans_b=True, res=dr2, res_scale=ALPHA,
            ln_bwd=(sv["r1"], _row(p["ln1_g"][l])), name=n + "dx1")
        big_grad("w_out", l, _mm_tn(sv["mix"], dr1, out_dtype=gdt, name=n + "dw_out"))
        dmix = _mm(dr1, big_w["w_out"][l], trans_b=True, name=n + "dmix")

        hin = sv["hin"]
        keys, rd = grad_rider(l, "cv_bwd")
        (dh, g["cv_w_pw"], dcw, g["cv_b_pw"], g["cv_gn_g"], g["cv_gn_b"], g["cv_b"]), routs = _cv_bwd(
            hin, sv["cv_c"], dmix, sv["cvw"], _row(p["cv_gn_g"][l]), _row(p["cv_gn_b"][l]), mavg, p["cv_w_pw"][l],
            name=n + "cv_bwd", rider=rd)
        land_grads(keys, routs)
        g["cv_w"] = dcw[:CONV_WIDTH]
        dh, dwr, dwi, dlcw, g["lru_b_r"], g["lru_b_i"], dsp, g["lru_conv_b"] = _lru_bwd(
            hin, sv["lru_xc"], sv["lru_h"], dmix, sv["lcw"], q["wr"], _row(p["lru_b_r"][l]), q["wi"],
            _row(p["lru_b_i"][l]), q["sp"], dh, name=n + "lru_bwd")
        g["lru_conv_w"] = dlcw[:LRU_CONV_WIDTH]
        g["lru_w_r"], g["lru_w_i"] = q["lru_w_vjp"]((dwr, dwi))
        (g["lru_lam"],) = q["sp_vjp"](dsp)
        dy1, g["s5_w_glu"], g["s5_b_glu"] = _s5_glu_bwd(sv["s5_y1"], dmix, p["s5_w_glu"][l], _row(p["s5_b_glu"][l]),
                                                        name=n + "s5_glu_bwd")
        s5_du2, *s5_dmats = _s5_core_bwd(sv["s5_u2"], _s5_to_chunks(dy1, 0, s5_perm, name=n + "s5_din"), sv["s5_x"],
                                         *q["s5_mats"], sv["s5_pows"], name=n + "s5_core_bwd")
        (g["s5_lam_re"], g["s5_lam_im"], g["s5_log_dt"], g["s5_b_re"], g["s5_b_im"], g["s5_c_re"], g["s5_c_im"],
         g["s5_d"]) = q["s5_vjp"](tuple(s5_dmats))
        dh, dkbig, dvbig = _attn_bwd(hin, dmix, sv["kbig"], sv["vbig"],
                                     _s5_from_chunks(s5_du2, s5_perm, name=n + "s5_dout"), dh, name=n + "attn_bwd")
        (dkv,) = sv["kv_vjp"]((dkbig, dvbig))
        big_grad("attn_w_kv", l, _mm_tn(mem, dkv, out_dtype=gdt, name=n + "dw_kv"))

        if dist:
            ready[("ssh", l)] = (_pack_lead([_split_shards(g[k], SHARDED[k] - 1) for k in SMALL_SHARDED], F32), "lead")
            ready[("rep", l)] = (_pack([g[k] for k in REP_LAYERED], F32), "all")
        gw_in, g["b_in"] = _hosted(_mm_tn, grad_rider(l, "dw_in"), land_grads, sv["xs"], dh, colsum=True, out_dtype=gdt,
                                   name=n + "dw_in")
        big_grad("w_in", l, gw_in)
        if dist:
            small.setdefault("b_in", [None] * DEPTH)[l] = g["b_in"].reshape(-1)
        else:
            for k, v in g.items():
                small.setdefault(k, [None] * DEPTH)[l] = v.reshape(p[k].shape[1:])
        if l > 0:
            dr2_below, dg_below, db_below, _ = _hosted(
                _mm, grad_rider(l, "dxs"), land_grads, dh, big_w["w_in"][l], trans_b=True, res=dr1, res_scale=ALPHA,
                ln_bwd=(saved[l - 1]["r2"], _row(p["ln2_g"][l - 1])), name=n + "dxs")
            from_above = (dr2_below, dg_below, db_below)
        else:
            dx = _hosted(_mm, grad_rider(l, "dxs"), land_grads, dh, big_w["w_in"][l], trans_b=True, res=dr1,
                         res_scale=ALPHA, name=n + "dxs")

    keys, rd = grad_rider(0, "ln_in_bwd")
    if rd is None:
        grad_x, dgi, dbi, _ = _ln_bwd(x, dx, _row(p["ln_in_g"]), name="ln_in_bwd")
    else:
        (grad_x, dgi, dbi, _), routs = _ln_bwd(x, dx, _row(p["ln_in_g"]), name="ln_in_bwd", rider=rd)
        land_grads(keys, routs)
    out = {k: jnp.stack(v, axis=0) for k, v in small.items()}
    out["ln_in_g"], out["ln_in_b"] = dgi.reshape(-1), dbi.reshape(-1)
    return loss, grad_x, out, ((recv, ready) if dist else big_g)


def kernel(x, mem, ln_in_g, ln_in_b, w_in, b_in, s5_lam_re, s5_lam_im, s5_log_dt, s5_b_re, s5_b_im, s5_c_re, s5_c_im, s5_d, s5_w_glu, s5_b_glu, cv_w, cv_b, cv_gn_g, cv_gn_b, cv_w_pw, cv_b_pw, lru_conv_w, lru_conv_b, lru_w_r, lru_b_r, lru_w_i, lru_b_i, lru_lam, attn_w_kv, w_out, b_out, ln1_g, ln1_b, ffn_w_up, ffn_conv_w, ffn_conv_b, ffn_w_down, ln2_g, ln2_b, loss_target, m_ln_in_g, m_ln_in_b, m_w_in, m_b_in, m_s5_lam_re, m_s5_lam_im, m_s5_log_dt, m_s5_b_re, m_s5_b_im, m_s5_c_re, m_s5_c_im, m_s5_d, m_s5_w_glu, m_s5_b_glu, m_cv_w, m_cv_b, m_cv_gn_g, m_cv_gn_b, m_cv_w_pw, m_cv_b_pw, m_lru_conv_w, m_lru_conv_b, m_lru_w_r, m_lru_b_r, m_lru_w_i, m_lru_b_i, m_lru_lam, m_attn_w_kv, m_w_out, m_b_out, m_ln1_g, m_ln1_b, m_ffn_w_up, m_ffn_conv_w, m_ffn_conv_b, m_ffn_w_down, m_ln2_g, m_ln2_b, v_ln_in_g, v_ln_in_b, v_w_in, v_b_in, v_s5_lam_re, v_s5_lam_im, v_s5_log_dt, v_s5_b_re, v_s5_b_im, v_s5_c_re, v_s5_c_im, v_s5_d, v_s5_w_glu, v_s5_b_glu, v_cv_w, v_cv_b, v_cv_gn_g, v_cv_gn_b, v_cv_w_pw, v_cv_b_pw, v_lru_conv_w, v_lru_conv_b, v_lru_w_r, v_lru_b_r, v_lru_w_i, v_lru_b_i, v_lru_lam, v_attn_w_kv, v_w_out, v_b_out, v_ln1_g, v_ln1_b, v_ffn_w_up, v_ffn_conv_w, v_ffn_conv_b, v_ffn_w_down, v_ln2_g, v_ln2_b):
    args = locals()
    w = {n: args[n] for n in WEIGHTS}
    mom = {n: args["m_" + n] for n in WEIGHTS}
    var = {n: args["v_" + n] for n in WEIGHTS}

    shards = {n: w[n].astype(BF16) for n in BIG}
    part_rows = shards["ffn_w_up"].shape[1] // len(ROW_PARTS)
    for i, tag in enumerate(ROW_PARTS):
        shards["ffn_w_up#" + tag] = [shards["ffn_w_up"][0, i * part_rows:(i + 1) * part_rows]]
    shards["small_pack"] = [_pack([w[n] for n in SMALL_SHARDED], F32)]
    small_shapes = [w[n].shape for n in SMALL_SHARDED]

    def unpack_small(gathered):
        out = {n: _join_shards(st, SHARDED[n]) for n, st in zip(SMALL_SHARDED, _unpack(gathered, small_shapes, lead=True))}
        for n in ("s5_w_glu", "cv_w_pw"):
            out[n] = out[n].astype(BF16)
        return out

    big_w = {n: [None] * DEPTH for n in BIG}
    p = {n: w[n] for n in REPLICATED}
    p["b_in"] = _perm_in_cols(p["b_in"])

    loss, grad_x, g_small, (recv, ready) = _local_step(x[0], mem[0], loss_target[0], p, big_w, shards, unpack_small)
    loss = lax.psum(loss, ("x", "y", "c"))

    g_small["b_in"] = _perm_in_cols(g_small["b_in"], inverse=True)
    left = list(ready)
    rider = _Rider([ready[k][0] for k in left] + [_pack([g_small[n] for n in REP_LAST], F32)],
                   [ready[k][1] for k in left] + ["all"])
    got = _exchange(rider, name="exchange_grads")
    for k, r in zip(left, got):
        recv[k] = r

    res = [dict(), dict(), dict(), dict()]
    for n in BIG:
        outs = None
        for l in range(DEPTH):
            outs = _adamw_layer(recv[(n, l)], w[n], mom[n], var[n], l, outs, name=f"adamw_{n}_l{l}")
        for kind in range(4):
            res[kind][n] = outs[kind]
    for names, key, tag in ((SMALL_SHARDED, "ssh", "adamw_small_sharded"), (REP_LAYERED, "rep", "adamw_replicated")):
        gstack = jnp.concatenate([recv[(key, l)] for l in range(DEPTH)], axis=1)
        packs = [_pack_layers([t[n] for n in names], F32) for t in (w, mom, var)]
        rows = packs[0].shape[1]
        outs = _adamw(gstack, *[pk.reshape(DEPTH * rows, PACK_COLS) for pk in packs], name=tag)
        for kind in range(4):
            for n, a in zip(names, _unpack_layers(outs[kind].reshape(DEPTH, rows, PACK_COLS), [w[n].shape for n in names])):
                res[kind][n] = a
    outs = _adamw(got[len(left)], _pack([w[n] for n in REP_LAST], F32), _pack([mom[n] for n in REP_LAST], F32),
                  _pack([var[n] for n in REP_LAST], F32), name="adamw_last")
    for kind in range(4):
        for n, a in zip(REP_LAST, _unpack(outs[kind], [w[n].shape for n in REP_LAST])):
            res[kind][n] = a
    return (loss, grad_x[None], *[res[0][n] for n in WEIGHTS], *[res[1][n] for n in WEIGHTS],
            *[res[2][n] for n in WEIGHTS], *[res[3][n] for n in WEIGHTS])
```

```python
import math

import jax
import jax.numpy as jnp
from jax import lax
from jax.experimental import pallas as pl
from jax.experimental.pallas import tpu as pltpu

F32 = jnp.float32
BF16 = jnp.bfloat16

D_MODEL = 1024
DEPTH = 2
D_GROUP = 256
N_IN_COLS = 6 * D_GROUP
S5_GROUPS = 16
S5_CH = 16
S5_STATE = 64
CONV_WIDTH = 31
GN_GROUPS = 4
LRU_HEADS = 4
LRU_CONV_WIDTH = 4
LRU_C = 8.0
ATTN_HEADS = 4
ATTN_HEAD_DIM = 64
D_FF = 2816
FFN_CONV_WIDTH = 3
ALPHA = (2 * DEPTH) ** 0.25
LN_EPS = 1e-5
ADAM_LR, ADAM_B1, ADAM_B2, ADAM_EPS, ADAM_WD, ADAM_STEP = 0.001, 0.9, 0.999, 1e-08, 0.01, 10

N_DEV = 8
N_PEERS = N_DEV - 1
LANE = 128
SUBLANE = 8
VMEM_LIMIT = 56 * 1024 * 1024
PACK_COLS = 1024
PACK_ROW_BLOCK = 256
MM_ROW_TILE = 1024
MM_COL_CAP = 1408
MM_K_CAP = 1536
SEQ_TILE = 512

SHARDED = {
    "w_in": 2, "s5_w_glu": 1, "cv_w": 2, "cv_w_pw": 1, "lru_conv_w": 2, "attn_w_kv": 1,
    "w_out": 1, "ffn_w_up": 2, "ffn_conv_w": 2, "ffn_w_down": 1,
}
BIG = ("w_in", "attn_w_kv", "w_out", "ffn_w_up", "ffn_w_down")
SMALL_SHARDED = ("s5_w_glu", "cv_w", "cv_w_pw", "lru_conv_w", "ffn_conv_w")
WEIGHTS = ['ln_in_g', 'ln_in_b', 'w_in', 'b_in', 's5_lam_re', 's5_lam_im', 's5_log_dt', 's5_b_re', 's5_b_im',
           's5_c_re', 's5_c_im', 's5_d', 's5_w_glu', 's5_b_glu', 'cv_w', 'cv_b', 'cv_gn_g', 'cv_gn_b', 'cv_w_pw',
           'cv_b_pw', 'lru_conv_w', 'lru_conv_b', 'lru_w_r', 'lru_b_r', 'lru_w_i', 'lru_b_i', 'lru_lam',
           'attn_w_kv', 'w_out', 'b_out', 'ln1_g', 'ln1_b', 'ffn_w_up', 'ffn_conv_w', 'ffn_conv_b', 'ffn_w_down',
           'ln2_g', 'ln2_b']
REPLICATED = [n for n in WEIGHTS if n not in SHARDED]
REP_LAST = ("ln_in_g", "ln_in_b", "b_in")
REP_LAYERED = [n for n in REPLICATED if n not in REP_LAST]

COL_CV_V, COL_CV_G, COL_LRU_G, COL_LRU_X, COL_S5, COL_Q = range(6)
IN_PERM = (1, 2, 3, 4, 0, 5)
MIX_S5, MIX_CV, MIX_LRU, MIX_ATTN = range(4)


_ANY = pl.BlockSpec(memory_space=pl.ANY)
_MESH = pl.DeviceIdType.MESH


def _cparams(n_axes):
    return pltpu.CompilerParams(dimension_semantics=("arbitrary",) * n_axes, vmem_limit_bytes=VMEM_LIMIT)


def _pick(n, cap):
    if n <= cap:
        return n
    best = None
    for t in range(LANE, cap + 1, LANE):
        if n % t == 0:
            best = t
    assert best is not None, (n, cap)
    return best


def _pick_rows(n, cap):
    best = None
    for t in range(SUBLANE, min(n, cap) + 1, SUBLANE):
        if n % t == 0:
            best = t
    assert best is not None, (n, cap)
    return best


def _full_spec(arr):
    nd = arr.ndim
    return pl.BlockSpec(arr.shape, lambda *_: (0,) * nd)


def _dot(a, b):
    return lax.dot_general(a.astype(BF16), b.astype(BF16), (((1,), (0,)), ((), ())), preferred_element_type=F32)


def _dot_nt(a, b):
    return lax.dot_general(a.astype(BF16), b.astype(BF16), (((1,), (1,)), ((), ())), preferred_element_type=F32)


def _dot_tn(a, b):
    return lax.dot_general(a.astype(BF16), b.astype(BF16), (((0,), (0,)), ((), ())), preferred_element_type=F32)


def _dot_hi(a, b):
    b = b.astype(BF16)
    a1 = a.astype(BF16)
    r1 = a - a1.astype(F32)
    a2 = r1.astype(BF16)
    a3 = (r1 - a2.astype(F32)).astype(BF16)
    return _dot(a1, b) + _dot(a2, b) + _dot(a3, b)


def _colsum(x):
    return jnp.sum(x, axis=0, keepdims=True)


def _sigmoid(x):
    return 1.0 / (1.0 + jnp.exp(-x))


_GELU_K = math.sqrt(2.0 / math.pi)
_GELU_C = 0.044715


def _gelu(x):
    t = jnp.tanh(_GELU_K * (x + _GELU_C * x * x * x))
    return 0.5 * x * (1.0 + t)


def _gelu_and_grad(x):
    x2 = x * x
    t = jnp.tanh(_GELU_K * (x + _GELU_C * x2 * x))
    g = 0.5 * x * (1.0 + t)
    dg = 0.5 * (1.0 + t) + 0.5 * x * (1.0 - t * t) * (_GELU_K * (1.0 + 3.0 * _GELU_C * x2))
    return g, dg


def _neg_expm1(x):
    series = x * (1.0 + x * (0.5 + x * (1.0 / 6.0 + x * (1.0 / 24.0 + x * (1.0 / 120.0)))))
    return -jnp.where(jnp.abs(x) < 0.1, series, jnp.exp(x) - 1.0)


def _seq_tile(s, want):
    t = min(s, want)
    assert s % t == 0
    return t


class _Rider:
    def __init__(self, srcs, kinds):
        self.srcs, self.kinds = list(srcs), list(kinds)
        self.n = len(self.srcs)

    def out_shapes(self):
        shapes = []
        for x, kind in zip(self.srcs, self.kinds):
            if kind == "lead":
                shp = x.shape
            elif kind == "rows":
                shp = (N_DEV, x.shape[0] // N_DEV) + x.shape[1:]
            else:
                shp = (N_DEV,) + x.shape
            shapes.append(jax.ShapeDtypeStruct(shp, x.dtype))
        return shapes

    def scratch(self):
        return [pltpu.SemaphoreType.DMA((self.n * N_PEERS,)), pltpu.SemaphoreType.DMA((self.n * N_PEERS,)),
                pltpu.SemaphoreType.DMA((self.n,))]

    def _copies(self, x_refs, out_refs, sems):
        send_sems, recv_sems, local_sems = sems
        mx, my, mc = lax.axis_index("x"), lax.axis_index("y"), lax.axis_index("c")
        my_id = 4 * mx + 2 * my + mc

        def piece(i, dev):
            if self.kinds[i] == "lead":
                return x_refs[i].at[dev]
            if self.kinds[i] == "rows":
                r = x_refs[i].shape[0] // N_DEV
                return x_refs[i].at[pl.ds(pl.multiple_of(dev * r, SUBLANE), r)]
            return x_refs[i]

        mine = [pltpu.make_async_copy(piece(i, my_id), out_refs[i].at[my_id], local_sems.at[i]) for i in range(self.n)]
        copies = []
        for k in range(1, N_DEV):
            px, py, pc = mx ^ ((k >> 2) & 1), my ^ ((k >> 1) & 1), mc ^ (k & 1)
            for i in range(self.n):
                copies.append(pltpu.make_async_remote_copy(
                    src_ref=piece(i, 4 * px + 2 * py + pc), dst_ref=out_refs[i].at[my_id],
                    send_sem=send_sems.at[i * N_PEERS + k - 1], recv_sem=recv_sems.at[i * N_PEERS + k - 1],
                    device_id=(px, py, pc), device_id_type=_MESH))
        return mine, copies

    def start(self, x_refs, out_refs, sems):
        mine, copies = self._copies(x_refs, out_refs, sems)
        for cp in mine + copies:
            cp.start()

    def wait(self, x_refs, out_refs, sems):
        mine, copies = self._copies(x_refs, out_refs, sems)
        for cp in copies:
            cp.wait_recv()
        for cp in copies:
            cp.wait_send()
        for cp in mine:
            cp.wait()


def _call(body, *, grid, ins, in_specs, outs, out_specs, scratch=(), aliases=None, name, rider=None):
    n_axes = len(grid)
    common = dict(grid=grid, input_output_aliases=aliases or {}, compiler_params=_cparams(n_axes), name=name)
    if rider is None:
        res = pl.pallas_call(body, in_specs=list(in_specs), out_specs=list(out_specs), out_shape=list(outs),
                             scratch_shapes=list(scratch), **common)(*ins)
        return list(res), []
    n_in, n_out, n_scr, nr = len(ins), len(outs), len(scratch), rider.n

    def wrapped(*refs):
        pos = [0]

        def take(k):
            part = refs[pos[0]:pos[0] + k]
            pos[0] += k
            return part

        a_in, r_in, a_out, r_out, a_scr, sems = take(n_in), take(nr), take(n_out), take(nr), take(n_scr), take(3)
        first = last = None
        for ax in range(n_axes):
            pid = pl.program_id(ax)
            f, l = pid == 0, pid == grid[ax] - 1
            first = f if first is None else jnp.logical_and(first, f)
            last = l if last is None else jnp.logical_and(last, l)

        @pl.when(first)
        def _():
            rider.start(r_in, r_out, sems)

        body(*a_in, *a_out, *a_scr)

        @pl.when(last)
        def _():
            rider.wait(r_in, r_out, sems)

    res = pl.pallas_call(
        wrapped, in_specs=list(in_specs) + [_ANY] * nr, out_specs=list(out_specs) + [_ANY] * nr,
        out_shape=list(outs) + rider.out_shapes(), scratch_shapes=list(scratch) + rider.scratch(), **common)(*ins, *rider.srcs)
    return list(res[:n_out]), list(res[n_out:])


def _block_mask(n_blocks, block_rows, block_cols):
    r = jnp.arange(n_blocks * block_rows) // block_rows
    c = jnp.arange(n_blocks * block_cols) // block_cols
    return (r[:, None] == c[None, :]).astype(F32)


def _mm(a, b, *, bias=None, res=None, res_scale=1.0, trans_b=False, out_dtype=F32, ln=None, ln_bwd=None, loss=None,
        name, rider=None):
    m, kdim = a.shape
    n = b.shape[0] if trans_b else b.shape[1]
    tm = _seq_tile(m, MM_ROW_TILE)
    tn = _pick(n, MM_COL_CAP)
    tk = _pick(kdim, MM_K_CAP)
    nk = kdim // tk
    has_bias, has_res, has_ln, has_lnb = bias is not None, res is not None, ln is not None, ln_bwd is not None
    has_loss = loss is not None
    assert not (has_ln or has_lnb or has_loss) or tn == n
    assert has_ln + has_lnb + has_loss <= 1

    def body(*refs):
        a_ref, b_ref = refs[0], refs[1]
        pos = 2
        bias_ref = res_ref = g_ref = beta_ref = x_ref = None
        if has_bias:
            bias_ref = refs[pos]
            pos += 1
        if has_res:
            res_ref = refs[pos]
            pos += 1
        if has_ln:
            g_ref, beta_ref = refs[pos], refs[pos + 1]
            pos += 2
        if has_lnb:
            x_ref, g_ref = refs[pos], refs[pos + 1]
            pos += 2
        if has_loss:
            g_ref, beta_ref, t_ref = refs[pos:pos + 3]
            pos += 3
        o_ref = refs[pos]
        pos += 1
        if has_ln:
            x_ref = refs[pos]
            pos += 1
        if has_lnb or has_loss:
            dg_ref, db_ref, ds_ref = refs[pos:pos + 3]
            pos += 3
        acc_ref = refs[pos]
        k = pl.program_id(2)

        @pl.when(k == 0)
        def _():
            acc_ref[...] = jnp.zeros_like(acc_ref)

        if has_lnb or has_loss:
            @pl.when(jnp.logical_and(pl.program_id(0) == 0, k == 0))
            def _():
                dg_ref[...] = jnp.zeros_like(dg_ref)
                db_ref[...] = jnp.zeros_like(db_ref)
                ds_ref[...] = jnp.zeros_like(ds_ref)

        if trans_b:
            acc_ref[...] += _dot_nt(a_ref[...], b_ref[...])
        else:
            acc_ref[...] += _dot(a_ref[...], b_ref[...])

        @pl.when(k == nk - 1)
        def _():
            r = acc_ref[...]
            if has_bias:
                r = r + bias_ref[...]
            if has_res:
                r = r + res_scale * res_ref[...]
            if has_lnb:
                x = x_ref[...]
                xc = x - jnp.mean(x, axis=1, keepdims=True)
                rstd = lax.rsqrt(jnp.mean(xc * xc, axis=1, keepdims=True) + LN_EPS)
                xh = xc * rstd
                dxh = r * g_ref[...]
                dx = rstd * (dxh - jnp.mean(dxh, axis=1, keepdims=True) - xh * jnp.mean(dxh * xh, axis=1, keepdims=True))
                o_ref[...] = dx
                dg_ref[...] += _colsum(r * xh)
                db_ref[...] += _colsum(r)
                ds_ref[...] += _colsum(dx)
            elif has_loss:
                gam = g_ref[...]
                xc = r - jnp.mean(r, axis=1, keepdims=True)
                rstd = lax.rsqrt(jnp.mean(xc * xc, axis=1, keepdims=True) + LN_EPS)
                xh = xc * rstd
                e = xh * gam + beta_ref[...] - t_ref[...]
                part = jnp.sum(jnp.sum(e * e, axis=1, keepdims=True), axis=0, keepdims=True) * (0.5 / n)
                ds_ref[...] += jnp.broadcast_to(part, ds_ref.shape)
                dy = e * (1.0 / n)
                dxh = dy * gam
                o_ref[...] = rstd * (dxh - jnp.mean(dxh, axis=1, keepdims=True) - xh * jnp.mean(dxh * xh, axis=1, keepdims=True))
                dg_ref[...] += _colsum(dy * xh)
                db_ref[...] += _colsum(dy)
            else:
                o_ref[...] = r.astype(out_dtype)
            if has_ln:
                xc = r - jnp.mean(r, axis=1, keepdims=True)
                var = jnp.mean(xc * xc, axis=1, keepdims=True)
                x_ref[...] = xc * lax.rsqrt(var + LN_EPS) * g_ref[...] + beta_ref[...]

    ins = [a, b]
    in_specs = [pl.BlockSpec((tm, tk), lambda i, j, k: (i, k)),
                pl.BlockSpec((tn, tk), lambda i, j, k: (j, k)) if trans_b
                else pl.BlockSpec((tk, tn), lambda i, j, k: (k, j))]
    if has_bias:
        ins.append(bias)
        in_specs.append(pl.BlockSpec((1, tn), lambda i, j, k: (0, j)))
    if has_res:
        ins.append(res)
        in_specs.append(pl.BlockSpec((tm, tn), lambda i, j, k: (i, j)))
    tile = pl.BlockSpec((tm, tn), lambda i, j, k: (i, j))
    vec = pl.BlockSpec((1, tn), lambda i, j, k: (0, j))
    out_shapes, out_specs = [jax.ShapeDtypeStruct((m, n), out_dtype)], [tile]
    if has_ln:
        ins += list(ln)
        in_specs += [vec] * 2
        out_shapes.append(jax.ShapeDtypeStruct((m, n), F32))
        out_specs.append(tile)
    if has_lnb:
        ins += list(ln_bwd)
        in_specs += [tile, vec]
        out_shapes += [jax.ShapeDtypeStruct((1, n), F32)] * 3
        out_specs += [vec] * 3
    if has_loss:
        ins += list(loss)
        in_specs += [vec, vec, tile]
        out_shapes += [jax.ShapeDtypeStruct((1, n), F32)] * 2 + [jax.ShapeDtypeStruct((SUBLANE, LANE), F32)]
        out_specs += [vec, vec, pl.BlockSpec((SUBLANE, LANE), lambda i, j, k: (0, 0))]
    outs, routs = _call(
        body, grid=(m // tm, n // tn, nk), ins=ins, in_specs=in_specs, outs=out_shapes, out_specs=out_specs,
        scratch=[pltpu.VMEM((tm, tn), F32)], name=name, rider=rider)
    out = tuple(outs) if (has_ln or has_lnb or has_loss) else outs[0]
    return out if rider is None else (out, routs)


def _mm_tn(a, b, *, colsum=False, out_dtype=F32, dev_cols=None, name, rider=None):
    s, ka = a.shape
    nb = b.shape[1]
    ts = _seq_tile(s, SEQ_TILE)
    tka = _pick(ka, MM_COL_CAP)
    tnb = _pick(nb, MM_COL_CAP)
    nk = s // ts
    assert not colsum or tka == ka
    per_tile = 1 if dev_cols is None else tnb // dev_cols
    assert dev_cols is None or tnb == per_tile * dev_cols

    def body(a_ref, b_ref, o_ref, *rest):
        cs_ref = rest[0] if colsum else None
        acc_ref = rest[-1]
        k = pl.program_id(2)

        @pl.when(k == 0)
        def _():
            acc_ref[...] = jnp.zeros_like(acc_ref)
            if colsum:
                cs_ref[...] = jnp.zeros_like(cs_ref)

        bv = b_ref[...]
        acc_ref[...] += _dot_tn(a_ref[...], bv)
        if colsum:
            cs_ref[...] += _colsum(bv.astype(F32))

        @pl.when(k == nk - 1)
        def _():
            if dev_cols is None:
                o_ref[...] = acc_ref[...].astype(out_dtype)
            else:
                for d in range(per_tile):
                    o_ref[d] = acc_ref[:, d * dev_cols:(d + 1) * dev_cols].astype(out_dtype)

    if dev_cols is None:
        main_shape, main_spec = (ka, nb), pl.BlockSpec((tka, tnb), lambda i, j, k: (i, j))
    else:
        main_shape = (nb // dev_cols, ka, dev_cols)
        main_spec = pl.BlockSpec((per_tile, tka, dev_cols), lambda i, j, k: (j, i, 0))
    outs, routs = _call(
        body, grid=(ka // tka, nb // tnb, nk), ins=[a, b],
        in_specs=[pl.BlockSpec((ts, tka), lambda i, j, k: (k, i)), pl.BlockSpec((ts, tnb), lambda i, j, k: (k, j))],
        outs=[jax.ShapeDtypeStruct(main_shape, out_dtype)] + ([jax.ShapeDtypeStruct((1, nb), F32)] if colsum else []),
        out_specs=[main_spec] + ([pl.BlockSpec((1, tnb), lambda i, j, k: (0, j))] if colsum else []),
        scratch=[pltpu.VMEM((tka, tnb), F32)], name=name, rider=rider)
    out = tuple(outs) if colsum else outs[0]
    return out if rider is None else (out, routs)


def _ln_fwd(r, g, b, *, name, rider=None):
    s, d = r.shape
    ts = _seq_tile(s, SEQ_TILE)

    def body(r_ref, g_ref, b_ref, o_ref):
        x = r_ref[...]
        mu = jnp.mean(x, axis=1, keepdims=True)
        xc = x - mu
        var = jnp.mean(xc * xc, axis=1, keepdims=True)
        o_ref[...] = xc * lax.rsqrt(var + LN_EPS) * g_ref[...] + b_ref[...]

    (out,), routs = _call(
        body, grid=(s // ts,), ins=[r, g, b],
        in_specs=[pl.BlockSpec((ts, d), lambda i: (i, 0)), _full_spec(g), _full_spec(b)],
        out_specs=[pl.BlockSpec((ts, d), lambda i: (i, 0))], outs=[jax.ShapeDtypeStruct((s, d), F32)],
        name=name, rider=rider)
    return out if rider is None else (out, routs)


def _ln_bwd(r, dy, g, *, name, rider=None):
    s, d = r.shape
    ts = _seq_tile(s, SEQ_TILE)

    def body(r_ref, dy_ref, g_ref, dr_ref, dg_ref, db_ref, ds_ref):
        @pl.when(pl.program_id(0) == 0)
        def _():
            dg_ref[...] = jnp.zeros_like(dg_ref)
            db_ref[...] = jnp.zeros_like(db_ref)
            ds_ref[...] = jnp.zeros_like(ds_ref)

        x = r_ref[...]
        dy = dy_ref[...]
        mu = jnp.mean(x, axis=1, keepdims=True)
        xc = x - mu
        var = jnp.mean(xc * xc, axis=1, keepdims=True)
        rstd = lax.rsqrt(var + LN_EPS)
        xh = xc * rstd
        dxh = dy * g_ref[...]
        m1 = jnp.mean(dxh, axis=1, keepdims=True)
        m2 = jnp.mean(dxh * xh, axis=1, keepdims=True)
        dr = rstd * (dxh - m1 - xh * m2)
        dr_ref[...] = dr
        dg_ref[...] += _colsum(dy * xh)
        db_ref[...] += _colsum(dy)
        ds_ref[...] += _colsum(dr)

    vec = jax.ShapeDtypeStruct((1, d), F32)
    vspec = pl.BlockSpec((1, d), lambda i: (0, 0))
    outs, routs = _call(
        body, grid=(s // ts,), ins=[r, dy, g],
        in_specs=[pl.BlockSpec((ts, d), lambda i: (i, 0)), pl.BlockSpec((ts, d), lambda i: (i, 0)), _full_spec(g)],
        out_specs=[pl.BlockSpec((ts, d), lambda i: (i, 0)), vspec, vspec, vspec],
        outs=[jax.ShapeDtypeStruct((s, d), F32), vec, vec, vec], name=name, rider=rider)
    return outs if rider is None else (outs, routs)


SCAN_CHUNK = 32


def _cscan_levels(bufs, apow_ref, t, pad, *, reverse):
    half = bufs[0].shape[1] // 2
    ch = min(SCAN_CHUNK, t)
    nlev = t.bit_length() - 1
    assert (1 << nlev) == t
    for k in range(nlev):
        d = 1 << k
        src, dst = bufs[k % 2], bufs[(k + 1) % 2]

        def chunk(c, carry, src=src, dst=dst, d=d, k=k):
            ar = apow_ref[k:k + 1, :half]
            ai = apow_ref[k:k + 1, half:]
            if reverse:
                ai = -ai
            r0 = pl.multiple_of(c * ch, ch)
            cur = src[pl.ds(pad + r0, ch), :]
            if d >= SUBLANE:
                off = pad + d if reverse else pad - d
                sh = src[pl.ds(off + r0, ch), :]
            elif reverse:
                blk = src[pl.ds(pad + r0, ch + SUBLANE), :]
                sh = pltpu.roll(blk, ch + SUBLANE - d, axis=0)[:ch, :]
            else:
                blk = src[pl.ds(pad - SUBLANE + r0, ch + SUBLANE), :]
                sh = pltpu.roll(blk, d, axis=0)[SUBLANE:, :]
            sre, sim = sh[:, :half], sh[:, half:]
            dst[pl.ds(pad + r0, ch), :half] = cur[:, :half] + ar * sre - ai * sim
            dst[pl.ds(pad + r0, ch), half:] = cur[:, half:] + ar * sim + ai * sre
            return carry

        lax.fori_loop(0, t // ch, chunk, 0)
    return nlev % 2


def _rscan_levels(abufs, bbufs, t, pad, *, reverse):
    nlev = t.bit_length() - 1
    assert (1 << nlev) == t
    for k in range(nlev):
        d = 1 << k
        asrc, adst = abufs[k % 2], abufs[(k + 1) % 2]
        bsrc, bdst = bbufs[k % 2], bbufs[(k + 1) % 2]
        off = pad + d if reverse else pad - d
        a = asrc[pad:pad + t, :]
        bdst[pad:pad + t, :] = a * bsrc[off:off + t, :] + bsrc[pad:pad + t, :]
        if k < nlev - 1:
            adst[pad:pad + t, :] = a * asrc[off:off + t, :]
    return nlev % 2


S5_CHUNK = 16
S5_SG = S5_GROUPS // 2
S5_SG_IN = 2 * S5_CHUNK * S5_CH
S5_SG_ST = 2 * S5_STATE


S5_HALF_SGS = S5_SG // 2
S5_HALF_IN = S5_HALF_SGS * S5_SG_IN


def _s5_perm():
    idx = jnp.arange(S5_HALF_IN)
    step, grp, chan = idx // LANE, (idx % LANE) // S5_CH, idx % S5_CH
    col = (grp // 2) * S5_SG_IN + (grp % 2) * (S5_CHUNK * S5_CH) + step * S5_CH + chan
    return (col[:, None] == idx[None, :]).astype(BF16)


def _s5_to_chunks(x, col_block, perm, *, name):
    s = x.shape[0]
    nb = s // S5_CHUNK

    def body(x_ref, perm_ref, o_ref):
        tok = jnp.concatenate([x_ref[pl.ds(t, nb, stride=S5_CHUNK), :].astype(BF16) for t in range(S5_CHUNK)], axis=1)
        grouped = _dot(tok, perm_ref[...]).astype(BF16)
        for k in range(S5_HALF_SGS):
            o_ref[k] = grouped[:, k * S5_SG_IN:(k + 1) * S5_SG_IN]

    return pl.pallas_call(
        body, grid=(2,),
        in_specs=[pl.BlockSpec((s, LANE), lambda h: (0, col_block + h)), _full_spec(perm)],
        out_specs=pl.BlockSpec((S5_HALF_SGS, nb, S5_SG_IN), lambda h: (h, 0, 0)),
        out_shape=jax.ShapeDtypeStruct((S5_SG, nb, S5_SG_IN), BF16),
        compiler_params=_cparams(1), name=name)(x, perm)


def _s5_from_chunks(y, perm, *, name):
    _, nb, _ = y.shape

    def body(y_ref, perm_ref, o_ref):
        grouped = jnp.concatenate([y_ref[k] for k in range(S5_HALF_SGS)], axis=1)
        hi = grouped.astype(BF16)
        lo = (grouped - hi.astype(F32)).astype(BF16)
        tok = _dot_nt(hi, perm_ref[...]) + _dot_nt(lo, perm_ref[...])
        for t in range(S5_CHUNK):
            o_ref[pl.ds(t, nb, stride=S5_CHUNK), :] = tok[:, t * LANE:(t + 1) * LANE]

    return pl.pallas_call(
        body, grid=(2,),
        in_specs=[pl.BlockSpec((S5_HALF_SGS, nb, S5_SG_IN), lambda h: (h, 0, 0)), _full_spec(perm)],
        out_specs=pl.BlockSpec((nb * S5_CHUNK, LANE), lambda h: (0, h)),
        out_shape=jax.ShapeDtypeStruct((nb * S5_CHUNK, D_GROUP), F32),
        compiler_params=_cparams(1), name=name)(y, perm)


def _s5_core_fwd(u2, m2, pre, pim, qre, qim, a16, *, name):
    sg, nb, nin = u2.shape
    st2 = 2 * S5_SG_ST
    pad = nb // 2

    def body(u_ref, m_ref, pre_ref, pim_ref, qre_ref, qim_ref, a_ref, y_ref, x_ref, buf0, buf1):
        @pl.when(pl.program_id(0) == 0)
        def _():
            buf0[0:pad, :] = jnp.zeros((pad, st2), F32)
            buf1[0:pad, :] = jnp.zeros((pad, st2), F32)

        u = u_ref[...]
        buf0[pad:pad + nb, :S5_SG_ST] = _dot(u, pre_ref[...])
        buf0[pad:pad + nb, S5_SG_ST:] = _dot(u, pim_ref[...])
        xbuf = (buf0, buf1)[_cscan_levels((buf0, buf1), a_ref, nb, pad, reverse=False)]
        x_ref[...] = xbuf[pad:pad + nb, :]
        xprev = xbuf[pad - 1:pad - 1 + nb, :]
        y_ref[...] = _dot(u, m_ref[...]) + _dot(xprev[:, :S5_SG_ST], qre_ref[...]) + _dot(xprev[:, S5_SG_ST:], qim_ref[...])

    ins = [u2, m2, pre, pim, qre, qim, a16]
    return pl.pallas_call(
        body, grid=(sg,), in_specs=[pl.BlockSpec((None,) + a.shape[1:], lambda i: (i, 0, 0)) for a in ins],
        out_specs=[pl.BlockSpec((None, nb, nin), lambda i: (i, 0, 0)), pl.BlockSpec((None, nb, st2), lambda i: (i, 0, 0))],
        out_shape=[jax.ShapeDtypeStruct((sg, nb, nin), F32), jax.ShapeDtypeStruct((sg, nb, st2), F32)],
        scratch_shapes=[pltpu.VMEM((pad + nb, st2), F32), pltpu.VMEM((pad + nb, st2), F32)],
        compiler_params=_cparams(1), name=name)(*ins)


def _s5_core_bwd(u2, dy2, x_all, m2, pre, pim, qre, qim, a16, *, name):
    sg, nb, nin = u2.shape
    half = S5_SG_ST
    st2 = 2 * half
    pad = nb // 2

    def body(u_ref, dy_ref, x_ref, m_ref, pre_ref, pim_ref, qre_ref, qim_ref, a_ref,
             du_ref, dm_ref, dpre_ref, dpim_ref, dqre_ref, dqim_ref, da_ref, buf2, buf3, xp):
        @pl.when(pl.program_id(0) == 0)
        def _():
            buf2[nb:nb + pad, :] = jnp.zeros((pad, st2), F32)
            buf3[nb:nb + pad, :] = jnp.zeros((pad, st2), F32)
            xp[0:SUBLANE, :] = jnp.zeros((SUBLANE, st2), F32)

        u = u_ref[...]
        dy = dy_ref[...]
        dm_ref[...] = _dot_tn(u, dy)
        xp[SUBLANE:SUBLANE + nb, :] = x_ref[...]
        xprev = xp[SUBLANE - 1:SUBLANE - 1 + nb, :]
        xre, xim = xprev[:, :half], xprev[:, half:]
        dqre_ref[...] = _dot_tn(xre, dy)
        dqim_ref[...] = _dot_tn(xim, dy)
        buf2[0:nb, :half] = _dot_nt(dy, qre_ref[...])
        buf2[0:nb, half:] = _dot_nt(dy, qim_ref[...])
        mbuf = (buf2, buf3)[_cscan_levels((buf2, buf3), a_ref, nb, 0, reverse=True)]
        lam = mbuf[1:1 + nb, :]
        lre, lim = lam[:, :half], lam[:, half:]
        dpre_ref[...] = _dot_tn(u, lre)
        dpim_ref[...] = _dot_tn(u, lim)
        du_ref[...] = _dot_nt(dy, m_ref[...]) + _dot_nt(lre, pre_ref[...]) + _dot_nt(lim, pim_ref[...])
        da_ref[:, :half] = _colsum(lre * xre + lim * xim)
        da_ref[:, half:] = _colsum(lim * xre - lre * xim)

    ins = [u2, dy2, x_all, m2, pre, pim, qre, qim, a16]
    outs = [jax.ShapeDtypeStruct((sg, nb, nin), F32)] + [jax.ShapeDtypeStruct(a.shape, F32) for a in (m2, pre, pim, qre, qim)] + \
           [jax.ShapeDtypeStruct((sg, 1, st2), F32)]
    return pl.pallas_call(
        body, grid=(sg,), in_specs=[pl.BlockSpec((None,) + a.shape[1:], lambda i: (i, 0, 0)) for a in ins],
        out_specs=[pl.BlockSpec((None,) + o.shape[1:], lambda i: (i, 0, 0)) for o in outs], out_shape=outs,
        scratch_shapes=[pltpu.VMEM((nb + pad, st2), F32), pltpu.VMEM((nb + pad, st2), F32),
                        pltpu.VMEM((SUBLANE + nb, st2), F32)],
        compiler_params=_cparams(1), name=name)(*ins)


def _s5_glu_fwd(y1, wglu, bglu, *, name, rider=None):
    s = y1.shape[0]
    t = _seq_tile(s, SEQ_TILE)

    def body(y1_ref, wglu_ref, bglu_ref, out_ref):
        y2 = _gelu(y1_ref[...])
        out_ref[...] = (y2 * _sigmoid(_dot(y2, wglu_ref[...]) + bglu_ref[...])).astype(BF16)

    return _call(
        body, grid=(s // t,), ins=[y1, wglu, bglu],
        in_specs=[pl.BlockSpec((t, D_GROUP), lambda i: (i, 0)), _full_spec(wglu), _full_spec(bglu)],
        out_specs=[pl.BlockSpec((t, D_GROUP), lambda i: (i, MIX_S5))], outs=[jax.ShapeDtypeStruct((s, D_MODEL), BF16)],
        name=name, rider=rider)


def _s5_glu_bwd(y1, dmix, wglu, bglu, *, name):
    s = y1.shape[0]
    t = _seq_tile(s, SEQ_TILE)

    def body(y1_ref, do_ref, wglu_ref, bglu_ref, dy1_ref, dwglu_ref, dbglu_ref):
        @pl.when(pl.program_id(0) == 0)
        def _():
            dwglu_ref[...] = jnp.zeros_like(dwglu_ref)
            dbglu_ref[...] = jnp.zeros_like(dbglu_ref)

        dout = do_ref[...]
        y2, dgelu = _gelu_and_grad(y1_ref[...])
        sg = _sigmoid(_dot(y2, wglu_ref[...]) + bglu_ref[...])
        dz = dout * y2 * sg * (1.0 - sg)
        dwglu_ref[...] += _dot_tn(y2, dz)
        dbglu_ref[...] += _colsum(dz)
        dy1_ref[...] = (dout * sg + _dot_nt(dz, wglu_ref[...])) * dgelu

    outs = [jax.ShapeDtypeStruct((s, D_GROUP), F32), jax.ShapeDtypeStruct((D_GROUP, D_GROUP), F32),
            jax.ShapeDtypeStruct((1, D_GROUP), F32)]
    return pl.pallas_call(
        body, grid=(s // t,),
        in_specs=[pl.BlockSpec((t, D_GROUP), lambda i: (i, 0)), pl.BlockSpec((t, D_GROUP), lambda i: (i, MIX_S5)),
                  _full_spec(wglu), _full_spec(bglu)],
        out_specs=[pl.BlockSpec((t, D_GROUP), lambda i: (i, 0)), _full_spec(outs[1]), _full_spec(outs[2])],
        out_shape=outs, compiler_params=_cparams(1), name=name)(y1, dmix, wglu, bglu)


def _pair_blockdiag(x):
    g, r, c = x.shape
    x = x.reshape(g // 2, 2, r, c)
    z = jnp.zeros_like(x[:, 0])
    return jnp.concatenate([jnp.concatenate([x[:, 0], z], axis=2), jnp.concatenate([z, x[:, 1]], axis=2)], axis=1)


def _s5_chunk_map(lam_re, lam_im, log_dt, b_re, b_im, c_re, c_im, d_skip):
    g, n, c, lc = S5_GROUPS, S5_STATE, S5_CH, S5_CHUNK
    dt = jnp.exp(log_dt)[:, None]
    mag, ang = lam_re * dt, lam_im * dt
    j = jnp.arange(lc + 1, dtype=F32)[:, None, None]
    pw_mag = jnp.exp(j * mag)
    pw_re, pw_im = pw_mag * jnp.cos(j * ang), pw_mag * jnp.sin(j * ang)
    a_re, a_im = pw_re[1], pw_im[1]
    den = lam_re * lam_re + lam_im * lam_im
    n_re = a_re - 1.0
    k_re = (n_re * lam_re + a_im * lam_im) / den
    k_im = (a_im * lam_re - n_re * lam_im) / den
    bb_re = k_re[..., None] * b_re - k_im[..., None] * b_im
    bb_im = k_re[..., None] * b_im + k_im[..., None] * b_re
    e_re = pw_re[:lc, :, :, None] * bb_re - pw_im[:lc, :, :, None] * bb_im
    e_im = pw_re[:lc, :, :, None] * bb_im + pw_im[:lc, :, :, None] * bb_re
    kern = jnp.einsum("gdn,jgnc->jgdc", c_re, e_re) - jnp.einsum("gdn,jgnc->jgdc", c_im, e_im)
    lags = jnp.pad(jnp.transpose(kern, (1, 3, 0, 2)), ((0, 0), (0, 0), (lc - 1, 0), (0, 0)))
    lags = lags.reshape(g, c, (2 * lc - 1) * c)
    m = jnp.stack([lags[:, :, (lc - 1 - s) * c:(2 * lc - 1 - s) * c] for s in range(lc)], axis=1).reshape(g, lc * c, lc * c)
    skip = jnp.tile(d_skip.reshape(g, 1, c), (1, lc, 1)).reshape(g, lc * c)
    m = m + jnp.eye(lc * c, dtype=F32)[None] * skip[:, None, :]
    p_re = jnp.transpose(e_re[::-1], (1, 0, 3, 2)).reshape(g, lc * c, n)
    p_im = jnp.transpose(e_im[::-1], (1, 0, 3, 2)).reshape(g, lc * c, n)
    f_re = c_re[None] * pw_re[1:, :, None, :] - c_im[None] * pw_im[1:, :, None, :]
    f_im = c_re[None] * pw_im[1:, :, None, :] + c_im[None] * pw_re[1:, :, None, :]
    q_re = jnp.transpose(f_re, (1, 3, 0, 2)).reshape(g, n, lc * c)
    q_im = -jnp.transpose(f_im, (1, 3, 0, 2)).reshape(g, n, lc * c)
    a16 = jnp.concatenate([pw_re[lc].reshape(S5_SG, 1, S5_SG_ST), pw_im[lc].reshape(S5_SG, 1, S5_SG_ST)], axis=2)
    return (_pair_blockdiag(m), _pair_blockdiag(p_re), _pair_blockdiag(p_im), _pair_blockdiag(q_re),
            _pair_blockdiag(q_im), a16)


def _s5_a16_powers(a16, nlev):
    half = S5_SG_ST
    re, im = a16[:, :, :half], a16[:, :, half:]
    rows = []
    for _ in range(nlev):
        rows.append(jnp.concatenate([re, im], axis=2))
        re, im = re * re - im * im, 2.0 * re * im
    n_rows = -(-nlev // SUBLANE) * SUBLANE
    rows += [jnp.zeros_like(rows[0])] * (n_rows - nlev)
    return lax.stop_gradient(jnp.concatenate(rows, axis=1))


CV_TILE = 256
CV_PAD = 32
CV_CHUNK = 64


def _shifted_copies(buf, shifted, rows):
    n = rows - SUBLANE
    for s in range(1, SUBLANE):
        shifted[s - 1, 0:n, :] = buf[s:s + n, :]


def _window(buf, shifted, o, ch):
    q, s = divmod(o, SUBLANE)
    if s == 0:
        return buf[o:o + ch, :]
    return shifted[s - 1, q * SUBLANE:q * SUBLANE + ch, :]


def _gn_stats(c, mavg):
    mu = _dot_hi(c, mavg)
    cen = c - mu
    var = _dot_hi(cen * cen, mavg)
    rstd = lax.rsqrt(var + LN_EPS)
    return cen * rstd, rstd


def _cv_fwd(h_in, cw, cb, gng, gnb, mavg, wpw, bpw, mix, *, name, rider=None):
    s = h_in.shape[0]
    t = _seq_tile(s, CV_TILE)
    ch = min(CV_CHUNK, t)

    def body(v_ref, g_ref, cw_ref, cb_ref, gng_ref, gnb_ref, mavg_ref, wpw_ref, bpw_ref, _mix_in, out_ref, c_ref, xpad,
             shifted):
        @pl.when(pl.program_id(0) == 0)
        def _():
            xpad[0:CV_PAD, :] = jnp.zeros((CV_PAD, D_GROUP), F32)

        xpad[CV_PAD:CV_PAD + t, :] = v_ref[...] * _sigmoid(g_ref[...])
        _shifted_copies(xpad, shifted, t + CV_PAD)
        for r0 in range(0, t, ch):
            acc = jnp.broadcast_to(cb_ref[...], (ch, D_GROUP))
            for k in range(CONV_WIDTH):
                o = CV_PAD - (CONV_WIDTH - 1) + k + r0
                acc = acc + cw_ref[k:k + 1, :] * _window(xpad, shifted, o, ch)
            c_ref[r0:r0 + ch, :] = acc
        xpad[0:CV_PAD, :] = xpad[t:t + CV_PAD, :]
        xn, _ = _gn_stats(c_ref[...], mavg_ref[...])
        gn = xn * gng_ref[...] + gnb_ref[...]
        out_ref[...] = (_dot(gn * _sigmoid(gn), wpw_ref[...]) + bpw_ref[...]).astype(BF16)

    ins = [h_in, h_in, cw, cb, gng, gnb, mavg, wpw, bpw, mix]
    in_specs = [pl.BlockSpec((t, D_GROUP), lambda i: (i, COL_CV_V)), pl.BlockSpec((t, D_GROUP), lambda i: (i, COL_CV_G))] + \
               [_full_spec(a) for a in ins[2:9]] + [_ANY]
    return _call(
        body, grid=(s // t,), ins=ins, in_specs=in_specs,
        out_specs=[pl.BlockSpec((t, D_GROUP), lambda i: (i, MIX_CV)), pl.BlockSpec((t, D_GROUP), lambda i: (i, 0))],
        outs=[jax.ShapeDtypeStruct((s, D_MODEL), BF16), jax.ShapeDtypeStruct((s, D_GROUP), F32)],
        aliases={9: 0},
        scratch=[pltpu.VMEM((CV_PAD + t, D_GROUP), F32), pltpu.VMEM((SUBLANE - 1, CV_PAD + t, D_GROUP), F32)],
        name=name, rider=rider)


def _cv_bwd(h_in, c, dmix, cw, gng, gnb, mavg, wpw, *, name, rider=None):
    s = h_in.shape[0]
    t = _seq_tile(s, CV_TILE)
    nt = s // t
    ch = min(CV_CHUNK, t)

    def body(v_ref, g_ref, c_ref, do_ref, cw_ref, gng_ref, gnb_ref, mavg_ref, wpw_ref,
             dvg_ref, dwpw_ref, dcw_ref, dbpw_ref, dgg_ref, dgb_ref, dcb_ref, dcpad, hgbuf, shifted):
        @pl.when(pl.program_id(0) == 0)
        def _():
            dcpad[t:t + CV_PAD, :] = jnp.zeros((CV_PAD, D_GROUP), F32)
            for r in (dwpw_ref, dcw_ref, dbpw_ref, dgg_ref, dgb_ref, dcb_ref):
                r[...] = jnp.zeros_like(r)

        mavg = mavg_ref[...]
        xn, rstd = _gn_stats(c_ref[...], mavg)
        gg = gng_ref[...]
        gn = xn * gg + gnb_ref[...]
        sg = _sigmoid(gn)
        dout = do_ref[...]
        dwpw_ref[...] += _dot_tn(gn * sg, dout)
        dbpw_ref[...] += _colsum(dout)
        dgn = _dot_nt(dout, wpw_ref[...]) * (sg * (1.0 + gn * (1.0 - sg)))
        dgg_ref[...] += _colsum(dgn * xn)
        dgb_ref[...] += _colsum(dgn)
        dxn = dgn * gg
        dc = rstd * (dxn - _dot_hi(dxn, mavg) - xn * _dot_hi(dxn * xn, mavg))
        dcb_ref[...] += _colsum(dc)
        dcpad[0:t, :] = dc

        v = v_ref[...]
        sgm = _sigmoid(g_ref[...])
        hgbuf[...] = v * sgm
        _shifted_copies(dcpad, shifted, t + CV_PAD)
        for r0 in range(0, t, ch):
            hg = hgbuf[r0:r0 + ch, :]
            acc = jnp.zeros((ch, D_GROUP), F32)
            for k in range(CONV_WIDTH):
                o = (CONV_WIDTH - 1) - k + r0
                sh = _window(dcpad, shifted, o, ch)
                acc = acc + cw_ref[k:k + 1, :] * sh
                dcw_ref[k:k + 1, :] += _colsum(hg * sh)
            hgbuf[r0:r0 + ch, :] = acc
        dcpad[t:t + CV_PAD, :] = dcpad[0:CV_PAD, :]
        dhg = hgbuf[...]
        dvg_ref[:, :D_GROUP] = dhg * sgm
        dvg_ref[:, D_GROUP:] = dhg * v * sgm * (1.0 - sgm)

    def rev(col):
        return lambda i: (nt - 1 - i, col)

    ins = [h_in, h_in, c, dmix, cw, gng, gnb, mavg, wpw]
    in_specs = [pl.BlockSpec((t, D_GROUP), rev(COL_CV_V)), pl.BlockSpec((t, D_GROUP), rev(COL_CV_G)),
                pl.BlockSpec((t, D_GROUP), rev(0)), pl.BlockSpec((t, D_GROUP), rev(MIX_CV))] + [_full_spec(a) for a in ins[4:]]
    vec = jax.ShapeDtypeStruct((1, D_GROUP), F32)
    outs = [jax.ShapeDtypeStruct((s, N_IN_COLS), F32),
            jax.ShapeDtypeStruct((D_GROUP, D_GROUP), F32), jax.ShapeDtypeStruct((CV_PAD, D_GROUP), F32), vec, vec, vec, vec]
    out_specs = [pl.BlockSpec((t, 2 * D_GROUP), rev(COL_CV_V // 2))] + [_full_spec(o) for o in outs[1:]]
    return _call(
        body, grid=(nt,), ins=ins, in_specs=in_specs, out_specs=out_specs, outs=outs,
        scratch=[pltpu.VMEM((t + CV_PAD, D_GROUP), F32), pltpu.VMEM((t, D_GROUP), F32),
                 pltpu.VMEM((SUBLANE - 1, t + CV_PAD, D_GROUP), F32)], name=name, rider=rider)


LRU_TILE = 256


def _lru_gates(xc, wr_ref, br_ref, wi_ref, bi_ref, sp_ref):
    r = _sigmoid(_dot(xc, wr_ref[...]) + br_ref[...])
    i = _sigmoid(_dot(xc, wi_ref[...]) + bi_ref[...])
    log_a = -LRU_C * r * sp_ref[...]
    a = jnp.exp(log_a)
    m = jnp.sqrt(_neg_expm1(2.0 * log_a))
    return r, i, a, m


def _lru_fwd(h_in, lcw, lcb, wr, br, wi, bi, sp, mix, *, name, rider=None):
    s = h_in.shape[0]
    t = _seq_tile(s, LRU_TILE)
    pad = max(t // 2, SUBLANE)

    def body(xg_ref, xr_ref, lcw_ref, lcb_ref, wr_ref, br_ref, wi_ref, bi_ref, sp_ref, _mix_in,
             out_ref, xc_ref, h_ref, xpad, a0, a1, b0, b1, carry):
        @pl.when(pl.program_id(0) == 0)
        def _():
            xpad[0:SUBLANE, :] = jnp.zeros((SUBLANE, D_GROUP), F32)
            for bf in (a0, a1, b0, b1):
                bf[0:pad, :] = jnp.zeros((pad, D_GROUP), F32)
            carry[...] = jnp.zeros_like(carry)

        xpad[SUBLANE:SUBLANE + t, :] = xr_ref[...]
        xc = jnp.broadcast_to(lcb_ref[...], (t, D_GROUP))
        for k in range(LRU_CONV_WIDTH):
            o = SUBLANE - (LRU_CONV_WIDTH - 1) + k
            xc = xc + lcw_ref[k:k + 1, :] * xpad[o:o + t, :]
        xpad[0:SUBLANE, :] = xpad[t:t + SUBLANE, :]
        xc_ref[...] = xc
        _, i, a, m = _lru_gates(xc, wr_ref, br_ref, wi_ref, bi_ref, sp_ref)
        a0[pad:pad + t, :] = a
        b0[pad:pad + t, :] = m * (i * xc)
        b0[pad:pad + 1, :] += a0[pad:pad + 1, :] * carry[0:1, :]
        fin = _rscan_levels((a0, a1), (b0, b1), t, pad, reverse=False)
        hbuf = (b0, b1)[fin]
        carry[0:1, :] = hbuf[pad + t - 1:pad + t, :]
        h = hbuf[pad:pad + t, :]
        h_ref[...] = h
        out_ref[...] = (h * _gelu(xg_ref[...])).astype(BF16)

    ins = [h_in, h_in, lcw, lcb, wr, br, wi, bi, sp, mix]
    row = pl.BlockSpec((t, D_GROUP), lambda i: (i, 0))
    in_specs = [pl.BlockSpec((t, D_GROUP), lambda i: (i, COL_LRU_G)), pl.BlockSpec((t, D_GROUP), lambda i: (i, COL_LRU_X))] + \
               [_full_spec(a) for a in ins[2:9]] + [_ANY]
    return _call(
        body, grid=(s // t,), ins=ins, in_specs=in_specs,
        out_specs=[pl.BlockSpec((t, D_GROUP), lambda i: (i, MIX_LRU)), row, row],
        outs=[jax.ShapeDtypeStruct((s, D_MODEL), BF16)] + [jax.ShapeDtypeStruct((s, D_GROUP), F32)] * 2,
        aliases={9: 0},
        scratch=[pltpu.VMEM((SUBLANE + t, D_GROUP), F32)] + [pltpu.VMEM((pad + t, D_GROUP), F32)] * 4 +
                [pltpu.VMEM((SUBLANE, D_GROUP), F32)],
        name=name, rider=rider)


def _lru_bwd(h_in, xc_all, h_all, dmix, lcw, wr, br, wi, bi, sp, dh_all, *, name):
    s = h_in.shape[0]
    t = _seq_tile(s, LRU_TILE)
    nt = s // t
    pad = max(t // 2, SUBLANE)
    tb = t // SUBLANE

    def body(xg_ref, xr_ref, xc_ref, h_ref, hprev_ref, do_ref, lcw_ref, wr_ref, br_ref, wi_ref, bi_ref, sp_ref, _dh_in,
             dgr_ref, dwr_ref, dwi_ref, dlcw_ref, dbr_ref, dbi_ref, dsp_ref, dlcb_ref,
             a0, a1, b0, b1, hp, dxpad, carry):
        pid = pl.program_id(0)

        @pl.when(pid == 0)
        def _():
            for bf in (a0, a1, b0, b1):
                bf[pad + t:pad + t + pad, :] = jnp.zeros((pad, D_GROUP), F32)
            dxpad[t:t + SUBLANE, :] = jnp.zeros((SUBLANE, D_GROUP), F32)
            carry[...] = jnp.zeros_like(carry)
            for r in (dwr_ref, dwi_ref, dlcw_ref, dbr_ref, dbi_ref, dsp_ref, dlcb_ref):
                r[...] = jnp.zeros_like(r)

        xc = xc_ref[...]
        h = h_ref[...]
        dout = do_ref[...]
        gate, dgate = _gelu_and_grad(xg_ref[...])
        dgr_ref[:, :D_GROUP] = dout * h * dgate
        r, i, a, m = _lru_gates(xc, wr_ref, br_ref, wi_ref, bi_ref, sp_ref)

        a0[pad:pad + t, :] = a
        b0[pad:pad + t, :] = dout * gate
        b0[pad + t - 1:pad + t, :] += carry[0:1, :]
        a1[pad:pad + t, :] = a0[pad + 1:pad + 1 + t, :]
        fin = _rscan_levels((a1, a0), (b0, b1), t, pad, reverse=True)
        lam = (b0, b1)[fin][pad:pad + t, :]
        carry[0:1, :] = a[0:1, :] * lam[0:1, :]

        is_first = pid == nt - 1
        hp[0:SUBLANE, :] = jnp.where(is_first, 0.0, hprev_ref[...])
        hp[SUBLANE:SUBLANE + t, :] = h
        hprev = hp[SUBLANE - 1:SUBLANE - 1 + t, :]

        ix = i * xc
        dmm = lam * ix
        dix = lam * m
        da = lam * hprev - dmm * (a / m)
        dlog_a = da * a
        dr = dlog_a * (-LRU_C * sp_ref[...])
        dsp_ref[...] += _colsum(dlog_a * (-LRU_C * r))
        dpr = dr * r * (1.0 - r)
        dpi = dix * xc * i * (1.0 - i)
        dbr_ref[...] += _colsum(dpr)
        dbi_ref[...] += _colsum(dpi)
        dwr_ref[...] += _dot_tn(xc, dpr)
        dwi_ref[...] += _dot_tn(xc, dpi)
        dxc = dix * i + _dot_nt(dpr, wr_ref[...]) + _dot_nt(dpi, wi_ref[...])
        dlcb_ref[...] += _colsum(dxc)

        dxpad[0:t, :] = dxc
        xr = xr_ref[...]
        dxr = jnp.zeros((t, D_GROUP), F32)
        for k in range(LRU_CONV_WIDTH):
            o = (LRU_CONV_WIDTH - 1) - k
            sh = dxpad[o:o + t, :]
            dxr = dxr + lcw_ref[k:k + 1, :] * sh
            dlcw_ref[k:k + 1, :] += _colsum(xr * sh)
        dxpad[t:t + SUBLANE, :] = dxpad[0:SUBLANE, :]
        dgr_ref[:, D_GROUP:] = dxr

    def rev(col):
        return lambda i: (nt - 1 - i, col)

    ins = [h_in, h_in, xc_all, h_all, h_all, dmix, lcw, wr, br, wi, bi, sp, dh_all]
    in_specs = [pl.BlockSpec((t, D_GROUP), rev(COL_LRU_G)), pl.BlockSpec((t, D_GROUP), rev(COL_LRU_X)),
                pl.BlockSpec((t, D_GROUP), rev(0)), pl.BlockSpec((t, D_GROUP), rev(0)),
                pl.BlockSpec((SUBLANE, D_GROUP), lambda i: (jnp.maximum((nt - 1 - i) * tb - 1, 0), 0)),
                pl.BlockSpec((t, D_GROUP), rev(MIX_LRU))] + [_full_spec(a) for a in ins[6:12]] + [_ANY]
    vec = jax.ShapeDtypeStruct((1, D_GROUP), F32)
    mat = jax.ShapeDtypeStruct((D_GROUP, D_GROUP), F32)
    outs = [jax.ShapeDtypeStruct((s, N_IN_COLS), F32), mat, mat, jax.ShapeDtypeStruct((SUBLANE, D_GROUP), F32),
            vec, vec, vec, vec]
    out_specs = [pl.BlockSpec((t, 2 * D_GROUP), rev(COL_LRU_G // 2))] + [_full_spec(o) for o in outs[1:]]
    return pl.pallas_call(
        body, grid=(nt,), in_specs=in_specs, out_specs=out_specs, out_shape=outs, input_output_aliases={12: 0},
        scratch_shapes=[pltpu.VMEM((pad + t + pad, D_GROUP), F32)] * 4 +
                       [pltpu.VMEM((SUBLANE + t, D_GROUP), F32), pltpu.VMEM((t + SUBLANE, D_GROUP), F32),
                        pltpu.VMEM((SUBLANE, D_GROUP), F32)],
        compiler_params=_cparams(1), name=name)(*ins)


def _blockdiag(w):
    h, d, _ = w.shape
    return jnp.tile(w.reshape(h * d, d), (1, h)) * _block_mask(h, d, d)


ATTN_TILE = 512
ATTN_SCALE = ATTN_HEAD_DIM ** -0.5


def _attn_big(kv):
    m = kv.shape[0]
    kbig = jnp.tile(kv[:, :D_GROUP].T, (1, ATTN_HEADS)) * _block_mask(ATTN_HEADS, ATTN_HEAD_DIM, m)
    vbig = jnp.tile(kv[:, D_GROUP:], (ATTN_HEADS, 1)) * _block_mask(ATTN_HEADS, m, ATTN_HEAD_DIM)
    return kbig, vbig


def _attn_probs(q, kbig_ref, m):
    sc = _dot(q, kbig_ref[...]) * ATTN_SCALE
    ps = []
    for h in range(ATTN_HEADS):
        sh = sc[:, h * m:(h + 1) * m]
        e = jnp.exp(sh - jnp.max(sh, axis=1, keepdims=True))
        ps.append(e / jnp.sum(e, axis=1, keepdims=True))
    return ps


def _attn_fwd(h_in, kbig, vbig, mix, *, name):
    s = h_in.shape[0]
    t = _seq_tile(s, ATTN_TILE)
    m = kbig.shape[1] // ATTN_HEADS

    def body(q_ref, kbig_ref, vbig_ref, _mix_in, o_ref):
        ps = _attn_probs(q_ref[...], kbig_ref, m)
        o_ref[...] = _dot(jnp.concatenate(ps, axis=1), vbig_ref[...]).astype(BF16)

    return pl.pallas_call(
        body, grid=(s // t,),
        in_specs=[pl.BlockSpec((t, D_GROUP), lambda i: (i, COL_Q)), _full_spec(kbig), _full_spec(vbig), _ANY],
        out_specs=pl.BlockSpec((t, D_GROUP), lambda i: (i, MIX_ATTN)),
        out_shape=jax.ShapeDtypeStruct((s, D_MODEL), BF16), input_output_aliases={3: 0},
        compiler_params=_cparams(1), name=name)(h_in, kbig, vbig, mix)


def _attn_bwd(h_in, dmix, kbig, vbig, du_s5, dh_all, *, name):
    s = h_in.shape[0]
    t = _seq_tile(s, ATTN_TILE)
    m = kbig.shape[1] // ATTN_HEADS

    def body(q_ref, do_ref, kbig_ref, vbig_ref, dus5_ref, _dh_in, dpair_ref, dk_ref, dv_ref):
        @pl.when(pl.program_id(0) == 0)
        def _():
            dk_ref[...] = jnp.zeros_like(dk_ref)
            dv_ref[...] = jnp.zeros_like(dv_ref)

        q = q_ref[...]
        dout = do_ref[...]
        ps = _attn_probs(q, kbig_ref, m)
        dp = _dot_nt(dout, vbig_ref[...])
        dss = []
        for h in range(ATTN_HEADS):
            dph = dp[:, h * m:(h + 1) * m]
            dss.append(ps[h] * (dph - jnp.sum(dph * ps[h], axis=1, keepdims=True)))
        ds = (jnp.concatenate(dss, axis=1) * ATTN_SCALE).astype(BF16)
        dv_ref[...] += _dot_tn(jnp.concatenate(ps, axis=1), dout)
        dpair_ref[:, :D_GROUP] = dus5_ref[...]
        dpair_ref[:, D_GROUP:] = _dot_nt(ds, kbig_ref[...])
        dk_ref[...] += _dot_tn(q, ds)

    assert (COL_S5, COL_Q) == (4, 5)
    outs = [jax.ShapeDtypeStruct((s, N_IN_COLS), F32), jax.ShapeDtypeStruct(kbig.shape, F32),
            jax.ShapeDtypeStruct(vbig.shape, F32)]
    return pl.pallas_call(
        body, grid=(s // t,),
        in_specs=[pl.BlockSpec((t, D_GROUP), lambda i: (i, COL_Q)), pl.BlockSpec((t, D_GROUP), lambda i: (i, MIX_ATTN)),
                  _full_spec(kbig), _full_spec(vbig), pl.BlockSpec((t, D_GROUP), lambda i: (i, 0)), _ANY],
        out_specs=[pl.BlockSpec((t, 2 * D_GROUP), lambda i: (i, COL_S5 // 2)), _full_spec(outs[1]), _full_spec(outs[2])],
        out_shape=outs, input_output_aliases={5: 0},
        compiler_params=_cparams(1), name=name)(h_in, dmix, kbig, vbig, du_s5, dh_all)


FFN_TILE = 128
FFN_COL_CHUNK = 256
FFN_ROW_CHUNK = 64


def _ffn_conv(pad_ref, w_ref, b_ref, r0, ch, c0):
    cc = FFN_COL_CHUNK
    acc = jnp.broadcast_to(b_ref[:, c0:c0 + cc], (ch, cc))
    for k in range(FFN_CONV_WIDTH):
        o = SUBLANE - (FFN_CONV_WIDTH - 1) + k + r0
        acc = acc + w_ref[k:k + 1, c0:c0 + cc] * pad_ref[o:o + ch, c0:c0 + cc]
    return acc


def _ffn_gate_fwd(u, fcw, fcb, *, name, rider=None):
    s = u.shape[0]
    t = _seq_tile(s, FFN_TILE)
    ch = min(FFN_ROW_CHUNK, t)
    cc = FFN_COL_CHUNK

    def body(u_ref, w_ref, b_ref, o_ref, uc_ref, upad):
        @pl.when(pl.program_id(0) == 0)
        def _():
            upad[0:SUBLANE, :] = jnp.zeros((SUBLANE, 2 * D_FF), F32)

        upad[SUBLANE:SUBLANE + t, :] = u_ref[...].astype(F32)
        for c0 in range(0, D_FF, cc):
            for r0 in range(0, t, ch):
                val = _ffn_conv(upad, w_ref, b_ref, r0, ch, c0)
                gt = _ffn_conv(upad, w_ref, b_ref, r0, ch, c0 + D_FF)
                o_ref[r0:r0 + ch, c0:c0 + cc] = (val * _gelu(gt)).astype(BF16)
                uc_ref[r0:r0 + ch, c0:c0 + cc] = val.astype(BF16)
                uc_ref[r0:r0 + ch, c0 + D_FF:c0 + D_FF + cc] = gt.astype(BF16)
        upad[0:SUBLANE, :] = upad[t:t + SUBLANE, :]

    return _call(
        body, grid=(s // t,), ins=[u, fcw, fcb],
        in_specs=[pl.BlockSpec((t, 2 * D_FF), lambda i: (i, 0)), _full_spec(fcw), _full_spec(fcb)],
        out_specs=[pl.BlockSpec((t, D_FF), lambda i: (i, 0)), pl.BlockSpec((t, 2 * D_FF), lambda i: (i, 0))],
        outs=[jax.ShapeDtypeStruct((s, D_FF), BF16), jax.ShapeDtypeStruct((s, 2 * D_FF), BF16)],
        scratch=[pltpu.VMEM((SUBLANE + t, 2 * D_FF), F32)], name=name, rider=rider)


def _ffn_gate_bwd(u, uc, dh, fcw, *, name, rider=None):
    s = u.shape[0]
    t = _seq_tile(s, FFN_TILE)
    nt = s // t
    ch = min(FFN_ROW_CHUNK, t)
    cc = FFN_COL_CHUNK

    def body(u_ref, uc_ref, dh_ref, w_ref, du_ref, dw_ref, db_ref, dpad):
        @pl.when(pl.program_id(0) == 0)
        def _():
            dpad[t:t + SUBLANE, :] = jnp.zeros((SUBLANE, 2 * D_FF), F32)
            dw_ref[...] = jnp.zeros_like(dw_ref)
            db_ref[...] = jnp.zeros_like(db_ref)

        for c0 in range(0, D_FF, cc):
            for r0 in range(0, t, ch):
                val = uc_ref[r0:r0 + ch, c0:c0 + cc].astype(F32)
                gt = uc_ref[r0:r0 + ch, c0 + D_FF:c0 + D_FF + cc].astype(F32)
                gl, dgl = _gelu_and_grad(gt)
                d = dh_ref[r0:r0 + ch, c0:c0 + cc].astype(F32)
                dpad[r0:r0 + ch, c0:c0 + cc] = d * gl
                dpad[r0:r0 + ch, c0 + D_FF:c0 + D_FF + cc] = d * val * dgl
        for c0 in range(0, 2 * D_FF, cc):
            dbs = jnp.zeros((1, cc), F32)
            dws = [jnp.zeros((1, cc), F32) for _ in range(FFN_CONV_WIDTH)]
            for r0 in range(0, t, ch):
                x = u_ref[r0:r0 + ch, c0:c0 + cc].astype(F32)
                acc = jnp.zeros((ch, cc), F32)
                for k in range(FFN_CONV_WIDTH):
                    o = (FFN_CONV_WIDTH - 1) - k + r0
                    sh = dpad[o:o + ch, c0:c0 + cc]
                    acc = acc + w_ref[k:k + 1, c0:c0 + cc] * sh
                    dws[k] = dws[k] + _colsum(x * sh)
                    if k == FFN_CONV_WIDTH - 1:
                        dbs = dbs + _colsum(sh)
                du_ref[r0:r0 + ch, c0:c0 + cc] = acc.astype(BF16)
            db_ref[:, c0:c0 + cc] += dbs
            for k in range(FFN_CONV_WIDTH):
                dw_ref[k:k + 1, c0:c0 + cc] += dws[k]
        dpad[t:t + SUBLANE, :] = dpad[0:SUBLANE, :]

    outs = [jax.ShapeDtypeStruct((s, 2 * D_FF), BF16), jax.ShapeDtypeStruct((SUBLANE, 2 * D_FF), F32),
            jax.ShapeDtypeStruct((1, 2 * D_FF), F32)]
    return _call(
        body, grid=(nt,), ins=[u, uc, dh, fcw],
        in_specs=[pl.BlockSpec((t, 2 * D_FF), lambda i: (nt - 1 - i, 0)),
                  pl.BlockSpec((t, 2 * D_FF), lambda i: (nt - 1 - i, 0)),
                  pl.BlockSpec((t, D_FF), lambda i: (nt - 1 - i, 0)), _full_spec(fcw)],
        out_specs=[pl.BlockSpec((t, 2 * D_FF), lambda i: (nt - 1 - i, 0)), _full_spec(outs[1]), _full_spec(outs[2])],
        outs=outs, scratch=[pltpu.VMEM((t + SUBLANE, 2 * D_FF), F32)], name=name, rider=rider)


def _adamw_body(g_ref, w_ref, m_ref, v_ref, go_ref, d_ref, mo_ref, vo_ref):
    inv_b1 = 1.0 - ADAM_B1 ** ADAM_STEP
    inv_b2 = 1.0 - ADAM_B2 ** ADAM_STEP
    g = g_ref[0].astype(F32)
    for dev in range(1, N_DEV):
        g = g + g_ref[dev].astype(F32)
    go_ref[...] = g
    mn = ADAM_B1 * m_ref[...] + (1.0 - ADAM_B1) * g
    vn = ADAM_B2 * v_ref[...] + (1.0 - ADAM_B2) * (g * g)
    mo_ref[...] = mn
    vo_ref[...] = vn
    d_ref[...] = -ADAM_LR * ((mn / inv_b1) / (jnp.sqrt(vn / inv_b2) + ADAM_EPS) + ADAM_WD * w_ref[...])


def _adamw(gstack, w, m, v, *, name):
    _, r, c = gstack.shape
    tr = _pick_rows(r, PACK_ROW_BLOCK)

    def body(*refs):
        _adamw_body(*refs)

    blk = pl.BlockSpec((tr, c), lambda i: (i, 0))
    sh = jax.ShapeDtypeStruct((r, c), F32)
    return pl.pallas_call(
        body, grid=(r // tr,),
        in_specs=[pl.BlockSpec((N_DEV, tr, c), lambda i: (0, i, 0)), blk, blk, blk],
        out_specs=[blk] * 4, out_shape=[sh] * 4,
        compiler_params=_cparams(1), name=name)(gstack, w, m, v)


def _adamw_layer(gstack, w, m, v, layer, into, *, name):
    n_layers, r, c = w.shape
    tr = _pick_rows(r, PACK_ROW_BLOCK)

    def body(g_ref, w_ref, m_ref, v_ref, *rest):
        _adamw_body(g_ref, w_ref, m_ref, v_ref, *rest[-4:])

    blk = pl.BlockSpec((None, tr, c), lambda i: (layer, i, 0))
    sh = jax.ShapeDtypeStruct((n_layers, r, c), F32)
    into = list(into or [])
    return pl.pallas_call(
        body, grid=(r // tr,),
        in_specs=[pl.BlockSpec((N_DEV, tr, c), lambda i: (0, i, 0)), blk, blk, blk] + [_ANY] * len(into),
        out_specs=[blk] * 4, out_shape=[sh] * 4, input_output_aliases={4 + k: k for k in range(len(into))},
        compiler_params=_cparams(1), name=name)(gstack, w, m, v, *into)


def _exchange(rider, *, name):
    n = rider.n

    def body(*refs):
        x_refs, out_refs, sems = refs[:n], refs[n:2 * n], refs[2 * n:]
        rider.start(x_refs, out_refs, sems)
        rider.wait(x_refs, out_refs, sems)

    return pl.pallas_call(
        body, in_specs=[_ANY] * n, out_specs=[_ANY] * n, out_shape=rider.out_shapes(),
        scratch_shapes=rider.scratch(), name=name)(*rider.srcs)


def _pack_rows(n):
    rows = -(-n // PACK_COLS)
    return -(-rows // SUBLANE) * SUBLANE


def _pack(arrs, dtype):
    flat = jnp.concatenate([a.reshape(-1).astype(dtype) for a in arrs])
    rows = _pack_rows(flat.shape[0])
    flat = jnp.pad(flat, (0, rows * PACK_COLS - flat.shape[0]))
    return flat.reshape(rows, PACK_COLS)


def _pack_lead(arrs, dtype):
    flat = jnp.concatenate([a.reshape(N_DEV, -1).astype(dtype) for a in arrs], axis=1)
    rows = _pack_rows(flat.shape[1])
    flat = jnp.pad(flat, ((0, 0), (0, rows * PACK_COLS - flat.shape[1])))
    return flat.reshape(N_DEV, rows, PACK_COLS)


def _pack_layers(arrs, dtype):
    n_layers = arrs[0].shape[0]
    flat = jnp.concatenate([a.reshape(n_layers, -1).astype(dtype) for a in arrs], axis=1)
    rows = _pack_rows(flat.shape[1])
    flat = jnp.pad(flat, ((0, 0), (0, rows * PACK_COLS - flat.shape[1])))
    return flat.reshape(n_layers, rows, PACK_COLS)


def _unpack_layers(packed, shapes):
    flat = packed.reshape(packed.shape[0], -1)
    out, pos = [], 0
    for sh in shapes:
        n = math.prod(sh[1:])
        out.append(flat[:, pos:pos + n].reshape(sh))
        pos += n
    return out


def _unpack(packed, shapes, lead=False):
    flat = packed.reshape(N_DEV, -1) if lead else packed.reshape(-1)
    out, pos = [], 0
    for sh in shapes:
        n = math.prod(sh)
        out.append(flat[:, pos:pos + n].reshape((N_DEV,) + tuple(sh)) if lead else flat[pos:pos + n].reshape(sh))
        pos += n
    return out


def _join_shards(stacked, axis):
    return jnp.concatenate([stacked[d] for d in range(N_DEV)], axis=axis)


def _split_shards(full, axis):
    return jnp.stack(jnp.split(full, N_DEV, axis=axis), axis=0)


def _perm_in_cols(a, inverse=False):
    blocks = jnp.split(a, 6, axis=-1)
    if inverse:
        order = [IN_PERM.index(j) for j in range(6)]
    else:
        order = list(IN_PERM)
    return jnp.concatenate([blocks[j] for j in order], axis=-1)


def _row(v):
    return v.reshape(1, -1)


def _pad_rows(w, rows):
    return jnp.pad(w, ((0, rows - w.shape[0]), (0, 0)))


def _gn_avg_matrix():
    return _block_mask(GN_GROUPS, D_GROUP // GN_GROUPS, D_GROUP // GN_GROUPS) / (D_GROUP // GN_GROUPS)


def _layer_params(p, l):
    q = {}
    s5_mats, q["s5_vjp"] = jax.vjp(_s5_chunk_map, p["s5_lam_re"][l], p["s5_lam_im"][l], p["s5_log_dt"][l],
                                   p["s5_b_re"][l], p["s5_b_im"][l], p["s5_c_re"][l], p["s5_c_im"][l], p["s5_d"][l])
    q["s5_mats"] = [m.astype(BF16) for m in s5_mats[:5]]
    q["s5_a16"] = s5_mats[5]
    (q["wr"], q["wi"]), q["lru_w_vjp"] = jax.vjp(lambda r, i: (_blockdiag(r), _blockdiag(i)), p["lru_w_r"][l], p["lru_w_i"][l])
    q["wr"], q["wi"] = q["wr"].astype(BF16), q["wi"].astype(BF16)
    q["sp"], q["sp_vjp"] = jax.vjp(lambda lam: _row(jax.nn.softplus(-lam)), p["lru_lam"][l])
    return q


ROW_PARTS = ("a", "b", "c", "d")
WEIGHT_RIDES = {(0, "ln_in_fwd"): [("w_in", 0)],
                (0, "inproj"): [("attn_w_kv", 0), ("w_out", 0), ("small_pack", 0)],
                (0, "cv_fwd"): [("ffn_w_up#a", 0), ("ffn_w_up#b", 0)],
                (0, "lru_fwd"): [("ffn_w_up#c", 0)],
                (0, "outproj"): [("ffn_w_up#d", 0)],
                (0, "ffn_up"): [("ffn_w_down", 0), ("w_in", 1), ("attn_w_kv", 1), ("w_out", 1)],
                (0, "ffn_gate_fwd"): [("ffn_w_up", 1)],
                (0, "ffn_down"): [("ffn_w_down", 1)]}
GRAD_RIDES = {(1, "ffn_gate_bwd"): [("ffn_w_down", 1)],
              (0, "dw_down"): [("w_out", 1), ("attn_w_kv", 1), ("w_in", 1)],
              (0, "dhff"): [("rep", 1), ("ssh", 1)],
              (0, "ffn_gate_bwd"): [("ffn_w_up", 1)],
              (0, "dw_up"): [("ffn_w_down", 0)],
              (0, "dx1"): [("ffn_w_up", 0)],
              (0, "cv_bwd"): [("w_out", 0)],
              (0, "dw_in"): [("attn_w_kv", 0), ("ssh", 0), ("rep", 0)],
              (0, "dxs"): [("w_in", 0)]}


def _join_cols(pieces, *, name):
    n_dev, k, c = pieces[0].shape
    assert (2 * c) % LANE == 0 and all(p.shape == pieces[0].shape for p in pieces)
    n_p = len(pieces)

    def body(*refs):
        o_ref = refs[n_p]
        for i in range(n_p):
            @pl.when(pl.program_id(0) == i)
            def _(i=i):
                o_ref[...] = jnp.concatenate([refs[i][0], refs[i][1]], axis=1)

    return pl.pallas_call(
        body, grid=(n_p, n_dev // 2),
        in_specs=[pl.BlockSpec((2, k, c), lambda i, j, p=p: (jnp.where(i == p, j, 0), 0, 0)) for p in range(n_p)],
        out_specs=pl.BlockSpec((k, 2 * c), lambda i, j: (i, j)),
        out_shape=jax.ShapeDtypeStruct((n_p * k, n_dev * c), pieces[0].dtype),
        compiler_params=_cparams(2), name=name)(*pieces)


def _split_cols(full, *, name):
    k, n = full.shape
    c = n // N_DEV
    assert (2 * c) % LANE == 0

    def body(x_ref, o_ref):
        o_ref[0] = x_ref[:, :c]
        o_ref[1] = x_ref[:, c:]

    return pl.pallas_call(
        body, grid=(N_DEV // 2,), in_specs=[pl.BlockSpec((k, 2 * c), lambda j: (0, j))],
        out_specs=pl.BlockSpec((2, k, c), lambda j: (j, 0, 0)), out_shape=jax.ShapeDtypeStruct((N_DEV, k, c), full.dtype),
        compiler_params=_cparams(1), name=name)(full)


def _assemble_weight(n, pieces, layer=0):
    if SHARDED[n] == 2:
        full = _join_cols(pieces, name=f"l{layer}_join_{n}")
        return _perm_in_cols(full) if n == "w_in" else full
    (gathered,) = pieces
    return gathered.reshape(-1, gathered.shape[-1])


def _grad_source(n, g, layer=0):
    g = g.astype(BF16)
    if SHARDED[n] == 2:
        if n == "w_in":
            g = _perm_in_cols(g, inverse=True)
        return _split_cols(g, name=f"l{layer}_split_d{n}"), "lead"
    return g, "rows"


def _hosted(fn, keys_rider, land, *args, **kw):
    keys, rider = keys_rider
    if rider is None:
        return fn(*args, **kw)
    out, routs = fn(*args, rider=rider, **kw)
    land(keys, routs)
    return out


def _local_step(x, mem, target, p, big_w, shards=None, unpack_small=None):
    dist = shards is not None
    gdt = BF16 if dist else F32
    small, saved = {}, []
    big_g, ready, recv = {}, {}, {}
    mavg = _gn_avg_matrix()
    s5_perm = _s5_perm()

    def weight_rider(l, host):
        keys = WEIGHT_RIDES.get((l, host), []) if dist else []
        return keys, (_Rider([shards[n][ll] for n, ll in keys], ["all"] * len(keys)) if keys else None)

    halves = {}

    def land_weights(keys, routs):
        for (n, ll), r in zip(keys, routs):
            if n == "small_pack":
                p.update(unpack_small(r))
            elif "#" in n:
                base = n.split("#")[0]
                halves[(n, ll)] = r
                parts = [halves.get((base + "#" + tag, ll)) for tag in ROW_PARTS]
                if all(part is not None for part in parts):
                    big_w[base][ll] = _assemble_weight(base, parts, ll)
            else:
                big_w[n][ll] = _assemble_weight(n, [r], ll)

    def grad_rider(l, host):
        keys = [k for k in GRAD_RIDES.get((l, host), []) if k in ready] if dist else []
        return keys, (_Rider([ready[k][0] for k in keys], [ready[k][1] for k in keys]) if keys else None)

    def land_grads(keys, routs):
        for k, r in zip(keys, routs):
            recv[k] = r
            del ready[k]

    def big_grad(n, l, g):
        if dist:
            ready[(n, l)] = _grad_source(n, g, l)
        else:
            big_g[(n, l)] = g

    xs = _hosted(_ln_fwd, weight_rider(0, "ln_in_fwd"), land_weights, x, _row(p["ln_in_g"]), _row(p["ln_in_b"]),
                 name="ln_in_fwd")
    for l in range(DEPTH):
        q = _layer_params(p, l)
        n = f"l{l}_"
        hin = _hosted(_mm, weight_rider(l, "inproj"), land_weights, xs, big_w["w_in"][l], bias=_row(p["b_in"][l]),
                      name=n + "inproj")
        nb = hin.shape[0] // S5_CHUNK
        s5_pows = _s5_a16_powers(q["s5_a16"], nb.bit_length() - 1)
        s5_u2 = _s5_to_chunks(hin, COL_S5 * (D_GROUP // LANE), s5_perm, name=n + "s5_in")
        s5_y2, s5_x = _s5_core_fwd(s5_u2, *q["s5_mats"], s5_pows, name=n + "s5_core_fwd")
        s5_y1 = _s5_from_chunks(s5_y2, s5_perm, name=n + "s5_out")
        (mix,), _ = _s5_glu_fwd(s5_y1, p["s5_w_glu"][l], _row(p["s5_b_glu"][l]), name=n + "s5_glu_fwd")
        cvw = _pad_rows(p["cv_w"][l], CV_PAD)
        keys, rd = weight_rider(l, "cv_fwd")
        (mix, cv_c), routs = _cv_fwd(hin, cvw, _row(p["cv_b"][l]), _row(p["cv_gn_g"][l]), _row(p["cv_gn_b"][l]), mavg,
                                     p["cv_w_pw"][l], _row(p["cv_b_pw"][l]), mix, name=n + "cv_fwd", rider=rd)
        land_weights(keys, routs)
        lcw = _pad_rows(p["lru_conv_w"][l], SUBLANE)
        keys, rd = weight_rider(l, "lru_fwd")
        (mix, lru_xc, lru_h), routs = _lru_fwd(hin, lcw, _row(p["lru_conv_b"][l]), q["wr"], _row(p["lru_b_r"][l]), q["wi"],
                                               _row(p["lru_b_i"][l]), q["sp"], mix, name=n + "lru_fwd", rider=rd)
        land_weights(keys, routs)
        kv = _mm(mem, big_w["attn_w_kv"][l], name=n + "kv")
        (kbig, vbig), kv_vjp = jax.vjp(_attn_big, kv)
        kbig, vbig = kbig.astype(BF16), vbig.astype(BF16)
        mix = _attn_fwd(hin, kbig, vbig, mix, name=n + "attn_fwd")
        r1, x1 = _hosted(_mm, weight_rider(l, "outproj"), land_weights, mix, big_w["w_out"][l], bias=_row(p["b_out"][l]),
                         res=xs, res_scale=ALPHA, ln=(_row(p["ln1_g"][l]), _row(p["ln1_b"][l])), name=n + "outproj")
        u = _hosted(_mm, weight_rider(l, "ffn_up"), land_weights, x1, big_w["ffn_w_up"][l], out_dtype=BF16,
                    name=n + "ffn_up")
        fcw = _pad_rows(p["ffn_conv_w"][l], SUBLANE)
        fcb = _row(p["ffn_conv_b"][l])
        keys, rd = weight_rider(l, "ffn_gate_fwd")
        (hff, uc), routs = _ffn_gate_fwd(u, fcw, fcb, name=n + "ffn_gate_fwd", rider=rd)
        land_weights(keys, routs)
        if l < DEPTH - 1:
            r2, x2 = _hosted(_mm, weight_rider(l, "ffn_down"), land_weights, hff, big_w["ffn_w_down"][l], res=x1,
                             res_scale=ALPHA, ln=(_row(p["ln2_g"][l]), _row(p["ln2_b"][l])), name=n + "ffn_down")
        else:
            r2 = x2 = None
            dr_top, dg_top, db_top, loss_blk = _mm(
                hff, big_w["ffn_w_down"][l], res=x1, res_scale=ALPHA,
                loss=(_row(p["ln2_g"][l]), _row(p["ln2_b"][l]), target), name=n + "ffn_down")
        saved.append(dict(q=q, xs=xs, hin=hin, s5_y1=s5_y1, s5_u2=s5_u2, s5_x=s5_x, s5_pows=s5_pows, cvw=cvw, cv_c=cv_c, lcw=lcw, lru_xc=lru_xc,
                          lru_h=lru_h, kbig=kbig, vbig=vbig, kv_vjp=kv_vjp, mix=mix, r1=r1, x1=x1, u=u, uc=uc, fcw=fcw,
                          hff=hff, r2=r2))
        xs = x2

    top = DEPTH - 1
    loss = loss_blk[0, 0]
    dx = None

    for l in reversed(range(DEPTH)):
        sv = saved[l]
        q = sv["q"]
        n = f"l{l}_"
        g = {}
        if l == top:
            dr2, g["ln2_g"], g["ln2_b"] = dr_top, dg_top, db_top
        else:
            dr2, g["ln2_g"], g["ln2_b"] = from_above
        big_grad("ffn_w_down", l, _hosted(_mm_tn, grad_rider(l, "dw_down"), land_grads, sv["hff"], dr2, out_dtype=gdt,
                                          name=n + "dw_down"))
        dhff = _hosted(_mm, grad_rider(l, "dhff"), land_grads, dr2, big_w["ffn_w_down"][l], trans_b=True,
                       out_dtype=BF16, name=n + "dhff")
        keys, rd = grad_rider(l, "ffn_gate_bwd")
        (du, dfw, g["ffn_conv_b"]), routs = _ffn_gate_bwd(sv["u"], sv["uc"], dhff, sv["fcw"], name=n + "ffn_gate_bwd",
                                                          rider=rd)
        land_grads(keys, routs)
        g["ffn_conv_w"] = dfw[:FFN_CONV_WIDTH]
        if dist:
            ready[("ffn_w_up", l)] = (_hosted(_mm_tn, grad_rider(l, "dw_up"), land_grads, sv["x1"], du, out_dtype=gdt,
                                              dev_cols=du.shape[1] // N_DEV, name=n + "dw_up"), "lead")
        else:
            big_grad("ffn_w_up", l, _mm_tn(sv["x1"], du, name=n + "dw_up"))
        dr1, g["ln1_g"], g["ln1_b"], g["b_out"] = _hosted(
            _mm, grad_rider(l, "dx1"), land_grads, du, big_w["ffn_w_up"][l], trans_b=True, res=dr2, res_scale=ALPHA,
            ln_bwd=(sv["r1"], _row(p["ln1_g"][l])), name=n + "dx1")
        big_grad("w_out", l, _mm_tn(sv["mix"], dr1, out_dtype=gdt, name=n + "dw_out"))
        dmix = _mm(dr1, big_w["w_out"][l], trans_b=True, name=n + "dmix")

        hin = sv["hin"]
        keys, rd = grad_rider(l, "cv_bwd")
        (dh, g["cv_w_pw"], dcw, g["cv_b_pw"], g["cv_gn_g"], g["cv_gn_b"], g["cv_b"]), routs = _cv_bwd(
            hin, sv["cv_c"], dmix, sv["cvw"], _row(p["cv_gn_g"][l]), _row(p["cv_gn_b"][l]), mavg, p["cv_w_pw"][l],
            name=n + "cv_bwd", rider=rd)
        land_grads(keys, routs)
        g["cv_w"] = dcw[:CONV_WIDTH]
        dh, dwr, dwi, dlcw, g["lru_b_r"], g["lru_b_i"], dsp, g["lru_conv_b"] = _lru_bwd(
            hin, sv["lru_xc"], sv["lru_h"], dmix, sv["lcw"], q["wr"], _row(p["lru_b_r"][l]), q["wi"],
            _row(p["lru_b_i"][l]), q["sp"], dh, name=n + "lru_bwd")
        g["lru_conv_w"] = dlcw[:LRU_CONV_WIDTH]
        g["lru_w_r"], g["lru_w_i"] = q["lru_w_vjp"]((dwr, dwi))
        (g["lru_lam"],) = q["sp_vjp"](dsp)
        dy1, g["s5_w_glu"], g["s5_b_glu"] = _s5_glu_bwd(sv["s5_y1"], dmix, p["s5_w_glu"][l], _row(p["s5_b_glu"][l]),
                                                        name=n + "s5_glu_bwd")
        s5_du2, *s5_dmats = _s5_core_bwd(sv["s5_u2"], _s5_to_chunks(dy1, 0, s5_perm, name=n + "s5_din"), sv["s5_x"],
                                         *q["s5_mats"], sv["s5_pows"], name=n + "s5_core_bwd")
        (g["s5_lam_re"], g["s5_lam_im"], g["s5_log_dt"], g["s5_b_re"], g["s5_b_im"], g["s5_c_re"], g["s5_c_im"],
         g["s5_d"]) = q["s5_vjp"](tuple(s5_dmats))
        dh, dkbig, dvbig = _attn_bwd(hin, dmix, sv["kbig"], sv["vbig"],
                                     _s5_from_chunks(s5_du2, s5_perm, name=n + "s5_dout"), dh, name=n + "attn_bwd")
        (dkv,) = sv["kv_vjp"]((dkbig, dvbig))
        big_grad("attn_w_kv", l, _mm_tn(mem, dkv, out_dtype=gdt, name=n + "dw_kv"))

        if dist:
            ready[("ssh", l)] = (_pack_lead([_split_shards(g[k], SHARDED[k] - 1) for k in SMALL_SHARDED], F32), "lead")
            ready[("rep", l)] = (_pack([g[k] for k in REP_LAYERED], F32), "all")
        gw_in, g["b_in"] = _hosted(_mm_tn, grad_rider(l, "dw_in"), land_grads, sv["xs"], dh, colsum=True, out_dtype=gdt,
                                   name=n + "dw_in")
        big_grad("w_in", l, gw_in)
        if dist:
            small.setdefault("b_in", [None] * DEPTH)[l] = g["b_in"].reshape(-1)
        else:
            for k, v in g.items():
                small.setdefault(k, [None] * DEPTH)[l] = v.reshape(p[k].shape[1:])
        if l > 0:
            dr2_below, dg_below, db_below, _ = _hosted(
                _mm, grad_rider(l, "dxs"), land_grads, dh, big_w["w_in"][l], trans_b=True, res=dr1, res_scale=ALPHA,
                ln_bwd=(saved[l - 1]["r2"], _row(p["ln2_g"][l - 1])), name=n + "dxs")
            from_above = (dr2_below, dg_below, db_below)
        else:
            dx = _hosted(_mm, grad_rider(l, "dxs"), land_grads, dh, big_w["w_in"][l], trans_b=True, res=dr1,
                         res_scale=ALPHA, name=n + "dxs")

    keys, rd = grad_rider(0, "ln_in_bwd")
    if rd is None:
        grad_x, dgi, dbi, _ = _ln_bwd(x, dx, _row(p["ln_in_g"]), name="ln_in_bwd")
    else:
        (grad_x, dgi, dbi, _), routs = _ln_bwd(x, dx, _row(p["ln_in_g"]), name="ln_in_bwd", rider=rd)
        land_grads(keys, routs)
    out = {k: jnp.stack(v, axis=0) for k, v in small.items()}
    out["ln_in_g"], out["ln_in_b"] = dgi.reshape(-1), dbi.reshape(-1)
    return loss, grad_x, out, ((recv, ready) if dist else big_g)


def kernel(x, mem, ln_in_g, ln_in_b, w_in, b_in, s5_lam_re, s5_lam_im, s5_log_dt, s5_b_re, s5_b_im, s5_c_re, s5_c_im, s5_d, s5_w_glu, s5_b_glu, cv_w, cv_b, cv_gn_g, cv_gn_b, cv_w_pw, cv_b_pw, lru_conv_w, lru_conv_b, lru_w_r, lru_b_r, lru_w_i, lru_b_i, lru_lam, attn_w_kv, w_out, b_out, ln1_g, ln1_b, ffn_w_up, ffn_conv_w, ffn_conv_b, ffn_w_down, ln2_g, ln2_b, loss_target, m_ln_in_g, m_ln_in_b, m_w_in, m_b_in, m_s5_lam_re, m_s5_lam_im, m_s5_log_dt, m_s5_b_re, m_s5_b_im, m_s5_c_re, m_s5_c_im, m_s5_d, m_s5_w_glu, m_s5_b_glu, m_cv_w, m_cv_b, m_cv_gn_g, m_cv_gn_b, m_cv_w_pw, m_cv_b_pw, m_lru_conv_w, m_lru_conv_b, m_lru_w_r, m_lru_b_r, m_lru_w_i, m_lru_b_i, m_lru_lam, m_attn_w_kv, m_w_out, m_b_out, m_ln1_g, m_ln1_b, m_ffn_w_up, m_ffn_conv_w, m_ffn_conv_b, m_ffn_w_down, m_ln2_g, m_ln2_b, v_ln_in_g, v_ln_in_b, v_w_in, v_b_in, v_s5_lam_re, v_s5_lam_im, v_s5_log_dt, v_s5_b_re, v_s5_b_im, v_s5_c_re, v_s5_c_im, v_s5_d, v_s5_w_glu, v_s5_b_glu, v_cv_w, v_cv_b, v_cv_gn_g, v_cv_gn_b, v_cv_w_pw, v_cv_b_pw, v_lru_conv_w, v_lru_conv_b, v_lru_w_r, v_lru_b_r, v_lru_w_i, v_lru_b_i, v_lru_lam, v_attn_w_kv, v_w_out, v_b_out, v_ln1_g, v_ln1_b, v_ffn_w_up, v_ffn_conv_w, v_ffn_conv_b, v_ffn_w_down, v_ln2_g, v_ln2_b):
    args = locals()
    w = {n: args[n] for n in WEIGHTS}
    mom = {n: args["m_" + n] for n in WEIGHTS}
    var = {n: args["v_" + n] for n in WEIGHTS}

    shards = {n: w[n].astype(BF16) for n in BIG}
    part_rows = shards["ffn_w_up"].shape[1] // len(ROW_PARTS)
    for i, tag in enumerate(ROW_PARTS):
        shards["ffn_w_up#" + tag] = [shards["ffn_w_up"][0, i * part_rows:(i + 1) * part_rows]]
    shards["small_pack"] = [_pack([w[n] for n in SMALL_SHARDED], F32)]
    small_shapes = [w[n].shape for n in SMALL_SHARDED]

    def unpack_small(gathered):
        out = {n: _join_shards(st, SHARDED[n]) for n, st in zip(SMALL_SHARDED, _unpack(gathered, small_shapes, lead=True))}
        for n in ("s5_w_glu", "cv_w_pw"):
            out[n] = out[n].astype(BF16)
        return out

    big_w = {n: [None] * DEPTH for n in BIG}
    p = {n: w[n] for n in REPLICATED}
    p["b_in"] = _perm_in_cols(p["b_in"])

    loss, grad_x, g_small, (recv, ready) = _local_step(x[0], mem[0], loss_target[0], p, big_w, shards, unpack_small)
    loss = lax.psum(loss, ("x", "y", "c"))

    g_small["b_in"] = _perm_in_cols(g_small["b_in"], inverse=True)
    left = list(ready)
    rider = _Rider([ready[k][0] for k in left] + [_pack([g_small[n] for n in REP_LAST], F32)],
                   [ready[k][1] for k in left] + ["all"])
    got = _exchange(rider, name="exchange_grads")
    for k, r in zip(left, got):
        recv[k] = r

    res = [dict(), dict(), dict(), dict()]
    for n in BIG:
        outs = None
        for l in range(DEPTH):
            outs = _adamw_layer(recv[(n, l)], w[n], mom[n], var[n], l, outs, name=f"adamw_{n}_l{l}")
        for kind in range(4):
            res[kind][n] = outs[kind]
    for names, key, tag in ((SMALL_SHARDED, "ssh", "adamw_small_sharded"), (REP_LAYERED, "rep", "adamw_replicated")):
        gstack = jnp.concatenate([recv[(key, l)] for l in range(DEPTH)], axis=1)
        packs = [_pack_layers([t[n] for n in names], F32) for t in (w, mom, var)]
        rows = packs[0].shape[1]
        outs = _adamw(gstack, *[pk.reshape(DEPTH * rows, PACK_COLS) for pk in packs], name=tag)
        for kind in range(4):
            for n, a in zip(names, _unpack_layers(outs[kind].reshape(DEPTH, rows, PACK_COLS), [w[n].shape for n in names])):
                res[kind][n] = a
    outs = _adamw(got[len(left)], _pack([w[n] for n in REP_LAST], F32), _pack([mom[n] for n in REP_LAST], F32),
                  _pack([var[n] for n in REP_LAST], F32), name="adamw_last")
    for kind in range(4):
        for n, a in zip(REP_LAST, _unpack(outs[kind], [w[n].shape for n in REP_LAST])):
            res[kind][n] = a
    return (loss, grad_x[None], *[res[0][n] for n in WEIGHTS], *[res[1][n] for n in WEIGHTS],
            *[res[2][n] for n in WEIGHTS], *[res[3][n] for n in WEIGHTS])
```

```python
import math

import jax
import jax.numpy as jnp
from jax import lax
from jax.experimental import pallas as pl
from jax.experimental.pallas import tpu as pltpu

F32 = jnp.float32
BF16 = jnp.bfloat16

D_MODEL = 1024
DEPTH = 2
D_GROUP = 256
N_IN_COLS = 6 * D_GROUP
S5_GROUPS = 16
S5_CH = 16
S5_STATE = 64
CONV_WIDTH = 31
GN_GROUPS = 4
LRU_HEADS = 4
LRU_CONV_WIDTH = 4
LRU_C = 8.0
ATTN_HEADS = 4
ATTN_HEAD_DIM = 64
D_FF = 2816
FFN_CONV_WIDTH = 3
ALPHA = (2 * DEPTH) ** 0.25
LN_EPS = 1e-5
ADAM_LR, ADAM_B1, ADAM_B2, ADAM_EPS, ADAM_WD, ADAM_STEP = 0.001, 0.9, 0.999, 1e-08, 0.01, 10

N_DEV = 8
N_PEERS = N_DEV - 1
LANE = 128
SUBLANE = 8
VMEM_LIMIT = 56 * 1024 * 1024
PACK_COLS = 1024
PACK_ROW_BLOCK = 256
MM_ROW_TILE = 1024
MM_COL_CAP = 1408
MM_K_CAP = 1536
SEQ_TILE = 512

SHARDED = {
    "w_in": 2, "s5_w_glu": 1, "cv_w": 2, "cv_w_pw": 1, "lru_conv_w": 2, "attn_w_kv": 1,
    "w_out": 1, "ffn_w_up": 2, "ffn_conv_w": 2, "ffn_w_down": 1,
}
BIG = ("w_in", "attn_w_kv", "w_out", "ffn_w_up", "ffn_w_down")
SMALL_SHARDED = ("s5_w_glu", "cv_w", "cv_w_pw", "lru_conv_w", "ffn_conv_w")
WEIGHTS = ['ln_in_g', 'ln_in_b', 'w_in', 'b_in', 's5_lam_re', 's5_lam_im', 's5_log_dt', 's5_b_re', 's5_b_im',
           's5_c_re', 's5_c_im', 's5_d', 's5_w_glu', 's5_b_glu', 'cv_w', 'cv_b', 'cv_gn_g', 'cv_gn_b', 'cv_w_pw',
           'cv_b_pw', 'lru_conv_w', 'lru_conv_b', 'lru_w_r', 'lru_b_r', 'lru_w_i', 'lru_b_i', 'lru_lam',
           'attn_w_kv', 'w_out', 'b_out', 'ln1_g', 'ln1_b', 'ffn_w_up', 'ffn_conv_w', 'ffn_conv_b', 'ffn_w_down',
           'ln2_g', 'ln2_b']
REPLICATED = [n for n in WEIGHTS if n not in SHARDED]
REP_LAST = ("ln_in_g", "ln_in_b", "b_in")
REP_LAYERED = [n for n in REPLICATED if n not in REP_LAST]

COL_CV_V, COL_CV_G, COL_LRU_G, COL_LRU_X, COL_S5, COL_Q = range(6)
IN_PERM = (1, 2, 3, 4, 0, 5)
MIX_S5, MIX_CV, MIX_LRU, MIX_ATTN = range(4)


_ANY = pl.BlockSpec(memory_space=pl.ANY)
_MESH = pl.DeviceIdType.MESH


def _cparams(n_axes):
    return pltpu.CompilerParams(dimension_semantics=("arbitrary",) * n_axes, vmem_limit_bytes=VMEM_LIMIT)


def _pick(n, cap):
    if n <= cap:
        return n
    best = None
    for t in range(LANE, cap + 1, LANE):
        if n % t == 0:
            best = t
    assert best is not None, (n, cap)
    return best


def _pick_rows(n, cap):
    best = None
    for t in range(SUBLANE, min(n, cap) + 1, SUBLANE):
        if n % t == 0:
            best = t
    assert best is not None, (n, cap)
    return best


def _full_spec(arr):
    nd = arr.ndim
    return pl.BlockSpec(arr.shape, lambda *_: (0,) * nd)


def _dot(a, b):
    return lax.dot_general(a.astype(BF16), b.astype(BF16), (((1,), (0,)), ((), ())), preferred_element_type=F32)


def _dot_nt(a, b):
    return lax.dot_general(a.astype(BF16), b.astype(BF16), (((1,), (1,)), ((), ())), preferred_element_type=F32)


def _dot_tn(a, b):
    return lax.dot_general(a.astype(BF16), b.astype(BF16), (((0,), (0,)), ((), ())), preferred_element_type=F32)


def _dot_hi(a, b):
    b = b.astype(BF16)
    a1 = a.astype(BF16)
    r1 = a - a1.astype(F32)
    a2 = r1.astype(BF16)
    a3 = (r1 - a2.astype(F32)).astype(BF16)
    return _dot(a1, b) + _dot(a2, b) + _dot(a3, b)


def _colsum(x):
    return jnp.sum(x, axis=0, keepdims=True)


def _sigmoid(x):
    return 1.0 / (1.0 + jnp.exp(-x))


_GELU_K = math.sqrt(2.0 / math.pi)
_GELU_C = 0.044715


def _gelu(x):
    t = jnp.tanh(_GELU_K * (x + _GELU_C * x * x * x))
    return 0.5 * x * (1.0 + t)


def _gelu_and_grad(x):
    x2 = x * x
    t = jnp.tanh(_GELU_K * (x + _GELU_C * x2 * x))
    g = 0.5 * x * (1.0 + t)
    dg = 0.5 * (1.0 + t) + 0.5 * x * (1.0 - t * t) * (_GELU_K * (1.0 + 3.0 * _GELU_C * x2))
    return g, dg


def _neg_expm1(x):
    series = x * (1.0 + x * (0.5 + x * (1.0 / 6.0 + x * (1.0 / 24.0 + x * (1.0 / 120.0)))))
    return -jnp.where(jnp.abs(x) < 0.1, series, jnp.exp(x) - 1.0)


def _seq_tile(s, want):
    t = min(s, want)
    assert s % t == 0
    return t


class _Rider:
    def __init__(self, srcs, kinds):
        self.srcs, self.kinds = list(srcs), list(kinds)
        self.n = len(self.srcs)

    def out_shapes(self):
        shapes = []
        for x, kind in zip(self.srcs, self.kinds):
            if kind == "lead":
                shp = x.shape
            elif kind == "rows":
                shp = (N_DEV, x.shape[0] // N_DEV) + x.shape[1:]
            else:
                shp = (N_DEV,) + x.shape
            shapes.append(jax.ShapeDtypeStruct(shp, x.dtype))
        return shapes

    def scratch(self):
        return [pltpu.SemaphoreType.DMA((self.n * N_PEERS,)), pltpu.SemaphoreType.DMA((self.n * N_PEERS,)),
                pltpu.SemaphoreType.DMA((self.n,))]

    def _copies(self, x_refs, out_refs, sems):
        send_sems, recv_sems, local_sems = sems
        mx, my, mc = lax.axis_index("x"), lax.axis_index("y"), lax.axis_index("c")
        my_id = 4 * mx + 2 * my + mc

        def piece(i, dev):
            if self.kinds[i] == "lead":
                return x_refs[i].at[dev]
            if self.kinds[i] == "rows":
                r = x_refs[i].shape[0] // N_DEV
                return x_refs[i].at[pl.ds(pl.multiple_of(dev * r, SUBLANE), r)]
            return x_refs[i]

        mine = [pltpu.make_async_copy(piece(i, my_id), out_refs[i].at[my_id], local_sems.at[i]) for i in range(self.n)]
        copies = []
        for k in range(1, N_DEV):
            px, py, pc = mx ^ ((k >> 2) & 1), my ^ ((k >> 1) & 1), mc ^ (k & 1)
            for i in range(self.n):
                copies.append(pltpu.make_async_remote_copy(
                    src_ref=piece(i, 4 * px + 2 * py + pc), dst_ref=out_refs[i].at[my_id],
                    send_sem=send_sems.at[i * N_PEERS + k - 1], recv_sem=recv_sems.at[i * N_PEERS + k - 1],
                    device_id=(px, py, pc), device_id_type=_MESH))
        return mine, copies

    def start(self, x_refs, out_refs, sems):
        mine, copies = self._copies(x_refs, out_refs, sems)
        for cp in mine + copies:
            cp.start()

    def wait(self, x_refs, out_refs, sems):
        mine, copies = self._copies(x_refs, out_refs, sems)
        for cp in copies:
            cp.wait_recv()
        for cp in copies:
            cp.wait_send()
        for cp in mine:
            cp.wait()


class _TwoLevelGather:
    def __init__(self, srcs):
        self.srcs = list(srcs)
        self.n = len(self.srcs)

    def out_shapes(self):
        return [jax.ShapeDtypeStruct((N_DEV,) + x.shape, x.dtype) for x in self.srcs]

    def scratch(self):
        return [pltpu.SemaphoreType.DMA((self.n * N_PEERS,)), pltpu.SemaphoreType.DMA((self.n * N_PEERS,)),
                pltpu.SemaphoreType.DMA((self.n,))]

    def _tools(self, x_refs, out_refs, sems):
        send_sems, recv_sems, local_sems = sems
        mx, my, mc = lax.axis_index("x"), lax.axis_index("y"), lax.axis_index("c")
        me, sibling = (mx, my, mc), (mx, my, 1 - mc)
        chips = [(1 - mx, my), (mx, 1 - my), (1 - mx, 1 - my)]

        def slot(i, px, py, pc):
            return out_refs[i].at[4 * px + 2 * py + pc]

        def copy(i, k, block, to, src=None):
            return pltpu.make_async_remote_copy(
                src_ref=slot(i, *block) if src is None else src, dst_ref=slot(i, *block),
                send_sem=send_sems.at[i * N_PEERS + k], recv_sem=recv_sems.at[i * N_PEERS + k],
                device_id=to, device_id_type=_MESH)

        mine = [pltpu.make_async_copy(x_refs[i], slot(i, *me), local_sems.at[i]) for i in range(self.n)]
        first = []
        for i in range(self.n):
            first.append(copy(i, 0, me, sibling, src=x_refs[i]))
            first += [copy(i, 1 + j, me, (*chip, mc), src=x_refs[i]) for j, chip in enumerate(chips)]
        return me, sibling, chips, copy, mine, first

    def start(self, x_refs, out_refs, sems):
        _, _, _, _, mine, first = self._tools(x_refs, out_refs, sems)
        for cp in mine + first:
            cp.start()

    def wait(self, x_refs, out_refs, sems):
        me, sibling, chips, copy, mine, first = self._tools(x_refs, out_refs, sems)
        mc = me[2]
        passed = []
        for j, chip in enumerate(chips):
            for i in range(self.n):
                copy(i, 1 + j, (*chip, mc), me).wait_recv()
                fwd = copy(i, 4 + j, (*chip, mc), sibling)
                fwd.start()
                passed.append(fwd)
        for i in range(self.n):
            copy(i, 0, sibling, me).wait_recv()
            for j, chip in enumerate(chips):
                copy(i, 4 + j, (*chip, 1 - mc), me).wait_recv()
        for cp in first + passed:
            cp.wait_send()
        for cp in mine:
            cp.wait()


def _call(body, *, grid, ins, in_specs, outs, out_specs, scratch=(), aliases=None, name, rider=None):
    n_axes = len(grid)
    common = dict(grid=grid, input_output_aliases=aliases or {}, compiler_params=_cparams(n_axes), name=name)
    if rider is None:
        res = pl.pallas_call(body, in_specs=list(in_specs), out_specs=list(out_specs), out_shape=list(outs),
                             scratch_shapes=list(scratch), **common)(*ins)
        return list(res), []
    n_in, n_out, n_scr, nr = len(ins), len(outs), len(scratch), rider.n

    def wrapped(*refs):
        pos = [0]

        def take(k):
            part = refs[pos[0]:pos[0] + k]
            pos[0] += k
            return part

        a_in, r_in, a_out, r_out, a_scr, sems = take(n_in), take(nr), take(n_out), take(nr), take(n_scr), take(3)
        first = last = None
        for ax in range(n_axes):
            pid = pl.program_id(ax)
            f, l = pid == 0, pid == grid[ax] - 1
            first = f if first is None else jnp.logical_and(first, f)
            last = l if last is None else jnp.logical_and(last, l)

        @pl.when(first)
        def _():
            rider.start(r_in, r_out, sems)

        body(*a_in, *a_out, *a_scr)

        @pl.when(last)
        def _():
            rider.wait(r_in, r_out, sems)

    res = pl.pallas_call(
        wrapped, in_specs=list(in_specs) + [_ANY] * nr, out_specs=list(out_specs) + [_ANY] * nr,
        out_shape=list(outs) + rider.out_shapes(), scratch_shapes=list(scratch) + rider.scratch(), **common)(*ins, *rider.srcs)
    return list(res[:n_out]), list(res[n_out:])


def _block_mask(n_blocks, block_rows, block_cols):
    r = jnp.arange(n_blocks * block_rows) // block_rows
    c = jnp.arange(n_blocks * block_cols) // block_cols
    return (r[:, None] == c[None, :]).astype(F32)


def _mm(a, b, *, bias=None, res=None, res_scale=1.0, trans_b=False, out_dtype=F32, ln=None, ln_bwd=None, loss=None,
        name, rider=None):
    m, kdim = a.shape
    n = b.shape[0] if trans_b else b.shape[1]
    tm = _seq_tile(m, MM_ROW_TILE)
    tn = _pick(n, MM_COL_CAP)
    tk = _pick(kdim, MM_K_CAP)
    nk = kdim // tk
    has_bias, has_res, has_ln, has_lnb = bias is not None, res is not None, ln is not None, ln_bwd is not None
    has_loss = loss is not None
    assert not (has_ln or has_lnb or has_loss) or tn == n
    assert has_ln + has_lnb + has_loss <= 1

    def body(*refs):
        a_ref, b_ref = refs[0], refs[1]
        pos = 2
        bias_ref = res_ref = g_ref = beta_ref = x_ref = None
        if has_bias:
            bias_ref = refs[pos]
            pos += 1
        if has_res:
            res_ref = refs[pos]
            pos += 1
        if has_ln:
            g_ref, beta_ref = refs[pos], refs[pos + 1]
            pos += 2
        if has_lnb:
            x_ref, g_ref = refs[pos], refs[pos + 1]
            pos += 2
        if has_loss:
            g_ref, beta_ref, t_ref = refs[pos:pos + 3]
            pos += 3
        o_ref = refs[pos]
        pos += 1
        if has_ln:
            x_ref = refs[pos]
            pos += 1
        if has_lnb or has_loss:
            dg_ref, db_ref, ds_ref = refs[pos:pos + 3]
            pos += 3
        acc_ref = refs[pos]
        k = pl.program_id(2)

        @pl.when(k == 0)
        def _():
            acc_ref[...] = jnp.zeros_like(acc_ref)

        if has_lnb or has_loss:
            @pl.when(jnp.logical_and(pl.program_id(0) == 0, k == 0))
            def _():
                dg_ref[...] = jnp.zeros_like(dg_ref)
                db_ref[...] = jnp.zeros_like(db_ref)
                ds_ref[...] = jnp.zeros_like(ds_ref)

        if trans_b:
            acc_ref[...] += _dot_nt(a_ref[...], b_ref[...])
        else:
            acc_ref[...] += _dot(a_ref[...], b_ref[...])

        @pl.when(k == nk - 1)
        def _():
            r = acc_ref[...]
            if has_bias:
                r = r + bias_ref[...]
            if has_res:
                r = r + res_scale * res_ref[...]
            if has_lnb:
                x = x_ref[...]
                xc = x - jnp.mean(x, axis=1, keepdims=True)
                rstd = lax.rsqrt(jnp.mean(xc * xc, axis=1, keepdims=True) + LN_EPS)
                xh = xc * rstd
                dxh = r * g_ref[...]
                dx = rstd * (dxh - jnp.mean(dxh, axis=1, keepdims=True) - xh * jnp.mean(dxh * xh, axis=1, keepdims=True))
                o_ref[...] = dx
                dg_ref[...] += _colsum(r * xh)
                db_ref[...] += _colsum(r)
                ds_ref[...] += _colsum(dx)
            elif has_loss:
                gam = g_ref[...]
                xc = r - jnp.mean(r, axis=1, keepdims=True)
                rstd = lax.rsqrt(jnp.mean(xc * xc, axis=1, keepdims=True) + LN_EPS)
                xh = xc * rstd
                e = xh * gam + beta_ref[...] - t_ref[...]
                part = jnp.sum(jnp.sum(e * e, axis=1, keepdims=True), axis=0, keepdims=True) * (0.5 / n)
                ds_ref[...] += jnp.broadcast_to(part, ds_ref.shape)
                dy = e * (1.0 / n)
                dxh = dy * gam
                o_ref[...] = rstd * (dxh - jnp.mean(dxh, axis=1, keepdims=True) - xh * jnp.mean(dxh * xh, axis=1, keepdims=True))
                dg_ref[...] += _colsum(dy * xh)
                db_ref[...] += _colsum(dy)
            else:
                o_ref[...] = r.astype(out_dtype)
            if has_ln:
                xc = r - jnp.mean(r, axis=1, keepdims=True)
                var = jnp.mean(xc * xc, axis=1, keepdims=True)
                x_ref[...] = xc * lax.rsqrt(var + LN_EPS) * g_ref[...] + beta_ref[...]

    ins = [a, b]
    in_specs = [pl.BlockSpec((tm, tk), lambda i, j, k: (i, k)),
                pl.BlockSpec((tn, tk), lambda i, j, k: (j, k)) if trans_b
                else pl.BlockSpec((tk, tn), lambda i, j, k: (k, j))]
    if has_bias:
        ins.append(bias)
        in_specs.append(pl.BlockSpec((1, tn), lambda i, j, k: (0, j)))
    if has_res:
        ins.append(res)
        in_specs.append(pl.BlockSpec((tm, tn), lambda i, j, k: (i, j)))
    tile = pl.BlockSpec((tm, tn), lambda i, j, k: (i, j))
    vec = pl.BlockSpec((1, tn), lambda i, j, k: (0, j))
    out_shapes, out_specs = [jax.ShapeDtypeStruct((m, n), out_dtype)], [tile]
    if has_ln:
        ins += list(ln)
        in_specs += [vec] * 2
        out_shapes.append(jax.ShapeDtypeStruct((m, n), F32))
        out_specs.append(tile)
    if has_lnb:
        ins += list(ln_bwd)
        in_specs += [tile, vec]
        out_shapes += [jax.ShapeDtypeStruct((1, n), F32)] * 3
        out_specs += [vec] * 3
    if has_loss:
        ins += list(loss)
        in_specs += [vec, vec, tile]
        out_shapes += [jax.ShapeDtypeStruct((1, n), F32)] * 2 + [jax.ShapeDtypeStruct((SUBLANE, LANE), F32)]
        out_specs += [vec, vec, pl.BlockSpec((SUBLANE, LANE), lambda i, j, k: (0, 0))]
    outs, routs = _call(
        body, grid=(m // tm, n // tn, nk), ins=ins, in_specs=in_specs, outs=out_shapes, out_specs=out_specs,
        scratch=[pltpu.VMEM((tm, tn), F32)], name=name, rider=rider)
    out = tuple(outs) if (has_ln or has_lnb or has_loss) else outs[0]
    return out if rider is None else (out, routs)


def _mm_tn(a, b, *, colsum=False, out_dtype=F32, dev_cols=None, name, rider=None):
    s, ka = a.shape
    nb = b.shape[1]
    ts = _seq_tile(s, SEQ_TILE)
    tka = _pick(ka, MM_COL_CAP)
    tnb = _pick(nb, MM_COL_CAP)
    nk = s // ts
    assert not colsum or tka == ka
    per_tile = 1 if dev_cols is None else tnb // dev_cols
    assert dev_cols is None or tnb == per_tile * dev_cols

    def body(a_ref, b_ref, o_ref, *rest):
        cs_ref = rest[0] if colsum else None
        acc_ref = rest[-1]
        k = pl.program_id(2)

        @pl.when(k == 0)
        def _():
            acc_ref[...] = jnp.zeros_like(acc_ref)
            if colsum:
                cs_ref[...] = jnp.zeros_like(cs_ref)

        bv = b_ref[...]
        acc_ref[...] += _dot_tn(a_ref[...], bv)
        if colsum:
            cs_ref[...] += _colsum(bv.astype(F32))

        @pl.when(k == nk - 1)
        def _():
            if dev_cols is None:
                o_ref[...] = acc_ref[...].astype(out_dtype)
            else:
                for d in range(per_tile):
                    o_ref[d] = acc_ref[:, d * dev_cols:(d + 1) * dev_cols].astype(out_dtype)

    if dev_cols is None:
        main_shape, main_spec = (ka, nb), pl.BlockSpec((tka, tnb), lambda i, j, k: (i, j))
    else:
        main_shape = (nb // dev_cols, ka, dev_cols)
        main_spec = pl.BlockSpec((per_tile, tka, dev_cols), lambda i, j, k: (j, i, 0))
    outs, routs = _call(
        body, grid=(ka // tka, nb // tnb, nk), ins=[a, b],
        in_specs=[pl.BlockSpec((ts, tka), lambda i, j, k: (k, i)), pl.BlockSpec((ts, tnb), lambda i, j, k: (k, j))],
        outs=[jax.ShapeDtypeStruct(main_shape, out_dtype)] + ([jax.ShapeDtypeStruct((1, nb), F32)] if colsum else []),
        out_specs=[main_spec] + ([pl.BlockSpec((1, tnb), lambda i, j, k: (0, j))] if colsum else []),
        scratch=[pltpu.VMEM((tka, tnb), F32)], name=name, rider=rider)
    out = tuple(outs) if colsum else outs[0]
    return out if rider is None else (out, routs)


def _ln_fwd(r, g, b, *, name, rider=None):
    s, d = r.shape
    ts = _seq_tile(s, SEQ_TILE)

    def body(r_ref, g_ref, b_ref, o_ref):
        x = r_ref[...]
        mu = jnp.mean(x, axis=1, keepdims=True)
        xc = x - mu
        var = jnp.mean(xc * xc, axis=1, keepdims=True)
        o_ref[...] = xc * lax.rsqrt(var + LN_EPS) * g_ref[...] + b_ref[...]

    (out,), routs = _call(
        body, grid=(s // ts,), ins=[r, g, b],
        in_specs=[pl.BlockSpec((ts, d), lambda i: (i, 0)), _full_spec(g), _full_spec(b)],
        out_specs=[pl.BlockSpec((ts, d), lambda i: (i, 0))], outs=[jax.ShapeDtypeStruct((s, d), F32)],
        name=name, rider=rider)
    return out if rider is None else (out, routs)


def _ln_bwd(r, dy, g, *, name, rider=None):
    s, d = r.shape
    ts = _seq_tile(s, SEQ_TILE)

    def body(r_ref, dy_ref, g_ref, dr_ref, dg_ref, db_ref, ds_ref):
        @pl.when(pl.program_id(0) == 0)
        def _():
            dg_ref[...] = jnp.zeros_like(dg_ref)
            db_ref[...] = jnp.zeros_like(db_ref)
            ds_ref[...] = jnp.zeros_like(ds_ref)

        x = r_ref[...]
        dy = dy_ref[...]
        mu = jnp.mean(x, axis=1, keepdims=True)
        xc = x - mu
        var = jnp.mean(xc * xc, axis=1, keepdims=True)
        rstd = lax.rsqrt(var + LN_EPS)
        xh = xc * rstd
        dxh = dy * g_ref[...]
        m1 = jnp.mean(dxh, axis=1, keepdims=True)
        m2 = jnp.mean(dxh * xh, axis=1, keepdims=True)
        dr = rstd * (dxh - m1 - xh * m2)
        dr_ref[...] = dr
        dg_ref[...] += _colsum(dy * xh)
        db_ref[...] += _colsum(dy)
        ds_ref[...] += _colsum(dr)

    vec = jax.ShapeDtypeStruct((1, d), F32)
    vspec = pl.BlockSpec((1, d), lambda i: (0, 0))
    outs, routs = _call(
        body, grid=(s // ts,), ins=[r, dy, g],
        in_specs=[pl.BlockSpec((ts, d), lambda i: (i, 0)), pl.BlockSpec((ts, d), lambda i: (i, 0)), _full_spec(g)],
        out_specs=[pl.BlockSpec((ts, d), lambda i: (i, 0)), vspec, vspec, vspec],
        outs=[jax.ShapeDtypeStruct((s, d), F32), vec, vec, vec], name=name, rider=rider)
    return outs if rider is None else (outs, routs)


SCAN_CHUNK = 32


def _cscan_levels(bufs, apow_ref, t, pad, *, reverse):
    half = bufs[0].shape[1] // 2
    ch = min(SCAN_CHUNK, t)
    nlev = t.bit_length() - 1
    assert (1 << nlev) == t
    for k in range(nlev):
        d = 1 << k
        src, dst = bufs[k % 2], bufs[(k + 1) % 2]

        def chunk(c, carry, src=src, dst=dst, d=d, k=k):
            ar = apow_ref[k:k + 1, :half]
            ai = apow_ref[k:k + 1, half:]
            if reverse:
                ai = -ai
            r0 = pl.multiple_of(c * ch, ch)
            cur = src[pl.ds(pad + r0, ch), :]
            if d >= SUBLANE:
                off = pad + d if reverse else pad - d
                sh = src[pl.ds(off + r0, ch), :]
            elif reverse:
                blk = src[pl.ds(pad + r0, ch + SUBLANE), :]
                sh = pltpu.roll(blk, ch + SUBLANE - d, axis=0)[:ch, :]
            else:
                blk = src[pl.ds(pad - SUBLANE + r0, ch + SUBLANE), :]
                sh = pltpu.roll(blk, d, axis=0)[SUBLANE:, :]
            sre, sim = sh[:, :half], sh[:, half:]
            dst[pl.ds(pad + r0, ch), :half] = cur[:, :half] + ar * sre - ai * sim
            dst[pl.ds(pad + r0, ch), half:] = cur[:, half:] + ar * sim + ai * sre
            return carry

        lax.fori_loop(0, t // ch, chunk, 0)
    return nlev % 2


def _rscan_levels(abufs, bbufs, t, pad, *, reverse):
    nlev = t.bit_length() - 1
    assert (1 << nlev) == t
    for k in range(nlev):
        d = 1 << k
        asrc, adst = abufs[k % 2], abufs[(k + 1) % 2]
        bsrc, bdst = bbufs[k % 2], bbufs[(k + 1) % 2]
        off = pad + d if reverse else pad - d
        a = asrc[pad:pad + t, :]
        bdst[pad:pad + t, :] = a * bsrc[off:off + t, :] + bsrc[pad:pad + t, :]
        if k < nlev - 1:
            adst[pad:pad + t, :] = a * asrc[off:off + t, :]
    return nlev % 2


S5_CHUNK = 16
S5_SG = S5_GROUPS // 2
S5_SG_IN = 2 * S5_CHUNK * S5_CH
S5_SG_ST = 2 * S5_STATE


S5_HALF_SGS = S5_SG // 2
S5_HALF_IN = S5_HALF_SGS * S5_SG_IN


def _s5_perm():
    idx = jnp.arange(S5_HALF_IN)
    step, grp, chan = idx // LANE, (idx % LANE) // S5_CH, idx % S5_CH
    col = (grp // 2) * S5_SG_IN + (grp % 2) * (S5_CHUNK * S5_CH) + step * S5_CH + chan
    return (col[:, None] == idx[None, :]).astype(BF16)


def _s5_to_chunks(x, col_block, perm, *, name):
    s = x.shape[0]
    nb = s // S5_CHUNK

    def body(x_ref, perm_ref, o_ref):
        tok = jnp.concatenate([x_ref[pl.ds(t, nb, stride=S5_CHUNK), :].astype(BF16) for t in range(S5_CHUNK)], axis=1)
        grouped = _dot(tok, perm_ref[...]).astype(BF16)
        for k in range(S5_HALF_SGS):
            o_ref[k] = grouped[:, k * S5_SG_IN:(k + 1) * S5_SG_IN]

    return pl.pallas_call(
        body, grid=(2,),
        in_specs=[pl.BlockSpec((s, LANE), lambda h: (0, col_block + h)), _full_spec(perm)],
        out_specs=pl.BlockSpec((S5_HALF_SGS, nb, S5_SG_IN), lambda h: (h, 0, 0)),
        out_shape=jax.ShapeDtypeStruct((S5_SG, nb, S5_SG_IN), BF16),
        compiler_params=_cparams(1), name=name)(x, perm)


def _s5_from_chunks(y, perm, *, name):
    _, nb, _ = y.shape

    def body(y_ref, perm_ref, o_ref):
        grouped = jnp.concatenate([y_ref[k] for k in range(S5_HALF_SGS)], axis=1)
        hi = grouped.astype(BF16)
        lo = (grouped - hi.astype(F32)).astype(BF16)
        tok = _dot_nt(hi, perm_ref[...]) + _dot_nt(lo, perm_ref[...])
        for t in range(S5_CHUNK):
            o_ref[pl.ds(t, nb, stride=S5_CHUNK), :] = tok[:, t * LANE:(t + 1) * LANE]

    return pl.pallas_call(
        body, grid=(2,),
        in_specs=[pl.BlockSpec((S5_HALF_SGS, nb, S5_SG_IN), lambda h: (h, 0, 0)), _full_spec(perm)],
        out_specs=pl.BlockSpec((nb * S5_CHUNK, LANE), lambda h: (0, h)),
        out_shape=jax.ShapeDtypeStruct((nb * S5_CHUNK, D_GROUP), F32),
        compiler_params=_cparams(1), name=name)(y, perm)


def _s5_core_fwd(u2, m2, pre, pim, qre, qim, a16, *, name):
    sg, nb, nin = u2.shape
    st2 = 2 * S5_SG_ST
    pad = nb // 2

    def body(u_ref, m_ref, pre_ref, pim_ref, qre_ref, qim_ref, a_ref, y_ref, x_ref, buf0, buf1):
        @pl.when(pl.program_id(0) == 0)
        def _():
            buf0[0:pad, :] = jnp.zeros((pad, st2), F32)
            buf1[0:pad, :] = jnp.zeros((pad, st2), F32)

        u = u_ref[...]
        buf0[pad:pad + nb, :S5_SG_ST] = _dot(u, pre_ref[...])
        buf0[pad:pad + nb, S5_SG_ST:] = _dot(u, pim_ref[...])
        xbuf = (buf0, buf1)[_cscan_levels((buf0, buf1), a_ref, nb, pad, reverse=False)]
        x_ref[...] = xbuf[pad:pad + nb, :]
        xprev = xbuf[pad - 1:pad - 1 + nb, :]
        y_ref[...] = _dot(u, m_ref[...]) + _dot(xprev[:, :S5_SG_ST], qre_ref[...]) + _dot(xprev[:, S5_SG_ST:], qim_ref[...])

    ins = [u2, m2, pre, pim, qre, qim, a16]
    return pl.pallas_call(
        body, grid=(sg,), in_specs=[pl.BlockSpec((None,) + a.shape[1:], lambda i: (i, 0, 0)) for a in ins],
        out_specs=[pl.BlockSpec((None, nb, nin), lambda i: (i, 0, 0)), pl.BlockSpec((None, nb, st2), lambda i: (i, 0, 0))],
        out_shape=[jax.ShapeDtypeStruct((sg, nb, nin), F32), jax.ShapeDtypeStruct((sg, nb, st2), F32)],
        scratch_shapes=[pltpu.VMEM((pad + nb, st2), F32), pltpu.VMEM((pad + nb, st2), F32)],
        compiler_params=_cparams(1), name=name)(*ins)


def _s5_core_bwd(u2, dy2, x_all, m2, pre, pim, qre, qim, a16, *, name):
    sg, nb, nin = u2.shape
    half = S5_SG_ST
    st2 = 2 * half
    pad = nb // 2

    def body(u_ref, dy_ref, x_ref, m_ref, pre_ref, pim_ref, qre_ref, qim_ref, a_ref,
             du_ref, dm_ref, dpre_ref, dpim_ref, dqre_ref, dqim_ref, da_ref, buf2, buf3, xp):
        @pl.when(pl.program_id(0) == 0)
        def _():
            buf2[nb:nb + pad, :] = jnp.zeros((pad, st2), F32)
            buf3[nb:nb + pad, :] = jnp.zeros((pad, st2), F32)
            xp[0:SUBLANE, :] = jnp.zeros((SUBLANE, st2), F32)

        u = u_ref[...]
        dy = dy_ref[...]
        dm_ref[...] = _dot_tn(u, dy)
        xp[SUBLANE:SUBLANE + nb, :] = x_ref[...]
        xprev = xp[SUBLANE - 1:SUBLANE - 1 + nb, :]
        xre, xim = xprev[:, :half], xprev[:, half:]
        dqre_ref[...] = _dot_tn(xre, dy)
        dqim_ref[...] = _dot_tn(xim, dy)
        buf2[0:nb, :half] = _dot_nt(dy, qre_ref[...])
        buf2[0:nb, half:] = _dot_nt(dy, qim_ref[...])
        mbuf = (buf2, buf3)[_cscan_levels((buf2, buf3), a_ref, nb, 0, reverse=True)]
        lam = mbuf[1:1 + nb, :]
        lre, lim = lam[:, :half], lam[:, half:]
        dpre_ref[...] = _dot_tn(u, lre)
        dpim_ref[...] = _dot_tn(u, lim)
        du_ref[...] = _dot_nt(dy, m_ref[...]) + _dot_nt(lre, pre_ref[...]) + _dot_nt(lim, pim_ref[...])
        da_ref[:, :half] = _colsum(lre * xre + lim * xim)
        da_ref[:, half:] = _colsum(lim * xre - lre * xim)

    ins = [u2, dy2, x_all, m2, pre, pim, qre, qim, a16]
    outs = [jax.ShapeDtypeStruct((sg, nb, nin), F32)] + [jax.ShapeDtypeStruct(a.shape, F32) for a in (m2, pre, pim, qre, qim)] + \
           [jax.ShapeDtypeStruct((sg, 1, st2), F32)]
    return pl.pallas_call(
        body, grid=(sg,), in_specs=[pl.BlockSpec((None,) + a.shape[1:], lambda i: (i, 0, 0)) for a in ins],
        out_specs=[pl.BlockSpec((None,) + o.shape[1:], lambda i: (i, 0, 0)) for o in outs], out_shape=outs,
        scratch_shapes=[pltpu.VMEM((nb + pad, st2), F32), pltpu.VMEM((nb + pad, st2), F32),
                        pltpu.VMEM((SUBLANE + nb, st2), F32)],
        compiler_params=_cparams(1), name=name)(*ins)


def _s5_glu_fwd(y1, wglu, bglu, *, name, rider=None):
    s = y1.shape[0]
    t = _seq_tile(s, SEQ_TILE)

    def body(y1_ref, wglu_ref, bglu_ref, out_ref):
        y2 = _gelu(y1_ref[...])
        out_ref[...] = (y2 * _sigmoid(_dot(y2, wglu_ref[...]) + bglu_ref[...])).astype(BF16)

    return _call(
        body, grid=(s // t,), ins=[y1, wglu, bglu],
        in_specs=[pl.BlockSpec((t, D_GROUP), lambda i: (i, 0)), _full_spec(wglu), _full_spec(bglu)],
        out_specs=[pl.BlockSpec((t, D_GROUP), lambda i: (i, MIX_S5))], outs=[jax.ShapeDtypeStruct((s, D_MODEL), BF16)],
        name=name, rider=rider)


def _s5_glu_bwd(y1, dmix, wglu, bglu, *, name):
    s = y1.shape[0]
    t = _seq_tile(s, SEQ_TILE)

    def body(y1_ref, do_ref, wglu_ref, bglu_ref, dy1_ref, dwglu_ref, dbglu_ref):
        @pl.when(pl.program_id(0) == 0)
        def _():
            dwglu_ref[...] = jnp.zeros_like(dwglu_ref)
            dbglu_ref[...] = jnp.zeros_like(dbglu_ref)

        dout = do_ref[...]
        y2, dgelu = _gelu_and_grad(y1_ref[...])
        sg = _sigmoid(_dot(y2, wglu_ref[...]) + bglu_ref[...])
        dz = dout * y2 * sg * (1.0 - sg)
        dwglu_ref[...] += _dot_tn(y2, dz)
        dbglu_ref[...] += _colsum(dz)
        dy1_ref[...] = (dout * sg + _dot_nt(dz, wglu_ref[...])) * dgelu

    outs = [jax.ShapeDtypeStruct((s, D_GROUP), F32), jax.ShapeDtypeStruct((D_GROUP, D_GROUP), F32),
            jax.ShapeDtypeStruct((1, D_GROUP), F32)]
    return pl.pallas_call(
        body, grid=(s // t,),
        in_specs=[pl.BlockSpec((t, D_GROUP), lambda i: (i, 0)), pl.BlockSpec((t, D_GROUP), lambda i: (i, MIX_S5)),
                  _full_spec(wglu), _full_spec(bglu)],
        out_specs=[pl.BlockSpec((t, D_GROUP), lambda i: (i, 0)), _full_spec(outs[1]), _full_spec(outs[2])],
        out_shape=outs, compiler_params=_cparams(1), name=name)(y1, dmix, wglu, bglu)


def _pair_blockdiag(x):
    g, r, c = x.shape
    x = x.reshape(g // 2, 2, r, c)
    z = jnp.zeros_like(x[:, 0])
    return jnp.concatenate([jnp.concatenate([x[:, 0], z], axis=2), jnp.concatenate([z, x[:, 1]], axis=2)], axis=1)


def _s5_chunk_map(lam_re, lam_im, log_dt, b_re, b_im, c_re, c_im, d_skip):
    g, n, c, lc = S5_GROUPS, S5_STATE, S5_CH, S5_CHUNK
    dt = jnp.exp(log_dt)[:, None]
    mag, ang = lam_re * dt, lam_im * dt
    j = jnp.arange(lc + 1, dtype=F32)[:, None, None]
    pw_mag = jnp.exp(j * mag)
    pw_re, pw_im = pw_mag * jnp.cos(j * ang), pw_mag * jnp.sin(j * ang)
    a_re, a_im = pw_re[1], pw_im[1]
    den = lam_re * lam_re + lam_im * lam_im
    n_re = a_re - 1.0
    k_re = (n_re * lam_re + a_im * lam_im) / den
    k_im = (a_im * lam_re - n_re * lam_im) / den
    bb_re = k_re[..., None] * b_re - k_im[..., None] * b_im
    bb_im = k_re[..., None] * b_im + k_im[..., None] * b_re
    e_re = pw_re[:lc, :, :, None] * bb_re - pw_im[:lc, :, :, None] * bb_im
    e_im = pw_re[:lc, :, :, None] * bb_im + pw_im[:lc, :, :, None] * bb_re
    kern = jnp.einsum("gdn,jgnc->jgdc", c_re, e_re) - jnp.einsum("gdn,jgnc->jgdc", c_im, e_im)
    lags = jnp.pad(jnp.transpose(kern, (1, 3, 0, 2)), ((0, 0), (0, 0), (lc - 1, 0), (0, 0)))
    lags = lags.reshape(g, c, (2 * lc - 1) * c)
    m = jnp.stack([lags[:, :, (lc - 1 - s) * c:(2 * lc - 1 - s) * c] for s in range(lc)], axis=1).reshape(g, lc * c, lc * c)
    skip = jnp.tile(d_skip.reshape(g, 1, c), (1, lc, 1)).reshape(g, lc * c)
    m = m + jnp.eye(lc * c, dtype=F32)[None] * skip[:, None, :]
    p_re = jnp.transpose(e_re[::-1], (1, 0, 3, 2)).reshape(g, lc * c, n)
    p_im = jnp.transpose(e_im[::-1], (1, 0, 3, 2)).reshape(g, lc * c, n)
    f_re = c_re[None] * pw_re[1:, :, None, :] - c_im[None] * pw_im[1:, :, None, :]
    f_im = c_re[None] * pw_im[1:, :, None, :] + c_im[None] * pw_re[1:, :, None, :]
    q_re = jnp.transpose(f_re, (1, 3, 0, 2)).reshape(g, n, lc * c)
    q_im = -jnp.transpose(f_im, (1, 3, 0, 2)).reshape(g, n, lc * c)
    a16 = jnp.concatenate([pw_re[lc].reshape(S5_SG, 1, S5_SG_ST), pw_im[lc].reshape(S5_SG, 1, S5_SG_ST)], axis=2)
    return (_pair_blockdiag(m), _pair_blockdiag(p_re), _pair_blockdiag(p_im), _pair_blockdiag(q_re),
            _pair_blockdiag(q_im), a16)


def _s5_a16_powers(a16, nlev):
    half = S5_SG_ST
    re, im = a16[:, :, :half], a16[:, :, half:]
    rows = []
    for _ in range(nlev):
        rows.append(jnp.concatenate([re, im], axis=2))
        re, im = re * re - im * im, 2.0 * re * im
    n_rows = -(-nlev // SUBLANE) * SUBLANE
    rows += [jnp.zeros_like(rows[0])] * (n_rows - nlev)
    return lax.stop_gradient(jnp.concatenate(rows, axis=1))


CV_TILE = 256
CV_PAD = 32
CV_CHUNK = 64


def _shifted_copies(buf, shifted, rows):
    n = rows - SUBLANE
    for s in range(1, SUBLANE):
        shifted[s - 1, 0:n, :] = buf[s:s + n, :]


def _window(buf, shifted, o, ch):
    q, s = divmod(o, SUBLANE)
    if s == 0:
        return buf[o:o + ch, :]
    return shifted[s - 1, q * SUBLANE:q * SUBLANE + ch, :]


def _gn_stats(c, mavg):
    mu = _dot_hi(c, mavg)
    cen = c - mu
    var = _dot_hi(cen * cen, mavg)
    rstd = lax.rsqrt(var + LN_EPS)
    return cen * rstd, rstd


def _cv_fwd(h_in, cw, cb, gng, gnb, mavg, wpw, bpw, mix, *, name, rider=None):
    s = h_in.shape[0]
    t = _seq_tile(s, CV_TILE)
    ch = min(CV_CHUNK, t)

    def body(v_ref, g_ref, cw_ref, cb_ref, gng_ref, gnb_ref, mavg_ref, wpw_ref, bpw_ref, _mix_in, out_ref, c_ref, xpad,
             shifted):
        @pl.when(pl.program_id(0) == 0)
        def _():
            xpad[0:CV_PAD, :] = jnp.zeros((CV_PAD, D_GROUP), F32)

        xpad[CV_PAD:CV_PAD + t, :] = v_ref[...] * _sigmoid(g_ref[...])
        _shifted_copies(xpad, shifted, t + CV_PAD)
        for r0 in range(0, t, ch):
            acc = jnp.broadcast_to(cb_ref[...], (ch, D_GROUP))
            for k in range(CONV_WIDTH):
                o = CV_PAD - (CONV_WIDTH - 1) + k + r0
                acc = acc + cw_ref[k:k + 1, :] * _window(xpad, shifted, o, ch)
            c_ref[r0:r0 + ch, :] = acc
        xpad[0:CV_PAD, :] = xpad[t:t + CV_PAD, :]
        xn, _ = _gn_stats(c_ref[...], mavg_ref[...])
        gn = xn * gng_ref[...] + gnb_ref[...]
        out_ref[...] = (_dot(gn * _sigmoid(gn), wpw_ref[...]) + bpw_ref[...]).astype(BF16)

    ins = [h_in, h_in, cw, cb, gng, gnb, mavg, wpw, bpw, mix]
    in_specs = [pl.BlockSpec((t, D_GROUP), lambda i: (i, COL_CV_V)), pl.BlockSpec((t, D_GROUP), lambda i: (i, COL_CV_G))] + \
               [_full_spec(a) for a in ins[2:9]] + [_ANY]
    return _call(
        body, grid=(s // t,), ins=ins, in_specs=in_specs,
        out_specs=[pl.BlockSpec((t, D_GROUP), lambda i: (i, MIX_CV)), pl.BlockSpec((t, D_GROUP), lambda i: (i, 0))],
        outs=[jax.ShapeDtypeStruct((s, D_MODEL), BF16), jax.ShapeDtypeStruct((s, D_GROUP), F32)],
        aliases={9: 0},
        scratch=[pltpu.VMEM((CV_PAD + t, D_GROUP), F32), pltpu.VMEM((SUBLANE - 1, CV_PAD + t, D_GROUP), F32)],
        name=name, rider=rider)


def _cv_bwd(h_in, c, dmix, cw, gng, gnb, mavg, wpw, *, name, rider=None):
    s = h_in.shape[0]
    t = _seq_tile(s, CV_TILE)
    nt = s // t
    ch = min(CV_CHUNK, t)

    def body(v_ref, g_ref, c_ref, do_ref, cw_ref, gng_ref, gnb_ref, mavg_ref, wpw_ref,
             dvg_ref, dwpw_ref, dcw_ref, dbpw_ref, dgg_ref, dgb_ref, dcb_ref, dcpad, hgbuf, shifted):
        @pl.when(pl.program_id(0) == 0)
        def _():
            dcpad[t:t + CV_PAD, :] = jnp.zeros((CV_PAD, D_GROUP), F32)
            for r in (dwpw_ref, dcw_ref, dbpw_ref, dgg_ref, dgb_ref, dcb_ref):
                r[...] = jnp.zeros_like(r)

        mavg = mavg_ref[...]
        xn, rstd = _gn_stats(c_ref[...], mavg)
        gg = gng_ref[...]
        gn = xn * gg + gnb_ref[...]
        sg = _sigmoid(gn)
        dout = do_ref[...]
        dwpw_ref[...] += _dot_tn(gn * sg, dout)
        dbpw_ref[...] += _colsum(dout)
        dgn = _dot_nt(dout, wpw_ref[...]) * (sg * (1.0 + gn * (1.0 - sg)))
        dgg_ref[...] += _colsum(dgn * xn)
        dgb_ref[...] += _colsum(dgn)
        dxn = dgn * gg
        dc = rstd * (dxn - _dot_hi(dxn, mavg) - xn * _dot_hi(dxn * xn, mavg))
        dcb_ref[...] += _colsum(dc)
        dcpad[0:t, :] = dc

        v = v_ref[...]
        sgm = _sigmoid(g_ref[...])
        hgbuf[...] = v * sgm
        _shifted_copies(dcpad, shifted, t + CV_PAD)
        for r0 in range(0, t, ch):
            hg = hgbuf[r0:r0 + ch, :]
            acc = jnp.zeros((ch, D_GROUP), F32)
            for k in range(CONV_WIDTH):
                o = (CONV_WIDTH - 1) - k + r0
                sh = _window(dcpad, shifted, o, ch)
                acc = acc + cw_ref[k:k + 1, :] * sh
                dcw_ref[k:k + 1, :] += _colsum(hg * sh)
            hgbuf[r0:r0 + ch, :] = acc
        dcpad[t:t + CV_PAD, :] = dcpad[0:CV_PAD, :]
        dhg = hgbuf[...]
        dvg_ref[:, :D_GROUP] = dhg * sgm
        dvg_ref[:, D_GROUP:] = dhg * v * sgm * (1.0 - sgm)

    def rev(col):
        return lambda i: (nt - 1 - i, col)

    ins = [h_in, h_in, c, dmix, cw, gng, gnb, mavg, wpw]
    in_specs = [pl.BlockSpec((t, D_GROUP), rev(COL_CV_V)), pl.BlockSpec((t, D_GROUP), rev(COL_CV_G)),
                pl.BlockSpec((t, D_GROUP), rev(0)), pl.BlockSpec((t, D_GROUP), rev(MIX_CV))] + [_full_spec(a) for a in ins[4:]]
    vec = jax.ShapeDtypeStruct((1, D_GROUP), F32)
    outs = [jax.ShapeDtypeStruct((s, N_IN_COLS), F32),
            jax.ShapeDtypeStruct((D_GROUP, D_GROUP), F32), jax.ShapeDtypeStruct((CV_PAD, D_GROUP), F32), vec, vec, vec, vec]
    out_specs = [pl.BlockSpec((t, 2 * D_GROUP), rev(COL_CV_V // 2))] + [_full_spec(o) for o in outs[1:]]
    return _call(
        body, grid=(nt,), ins=ins, in_specs=in_specs, out_specs=out_specs, outs=outs,
        scratch=[pltpu.VMEM((t + CV_PAD, D_GROUP), F32), pltpu.VMEM((t, D_GROUP), F32),
                 pltpu.VMEM((SUBLANE - 1, t + CV_PAD, D_GROUP), F32)], name=name, rider=rider)


LRU_TILE = 256


def _lru_gates(xc, wr_ref, br_ref, wi_ref, bi_ref, sp_ref):
    r = _sigmoid(_dot(xc, wr_ref[...]) + br_ref[...])
    i = _sigmoid(_dot(xc, wi_ref[...]) + bi_ref[...])
    log_a = -LRU_C * r * sp_ref[...]
    a = jnp.exp(log_a)
    m = jnp.sqrt(_neg_expm1(2.0 * log_a))
    return r, i, a, m


def _lru_fwd(h_in, lcw, lcb, wr, br, wi, bi, sp, mix, *, name, rider=None):
    s = h_in.shape[0]
    t = _seq_tile(s, LRU_TILE)
    pad = max(t // 2, SUBLANE)

    def body(xg_ref, xr_ref, lcw_ref, lcb_ref, wr_ref, br_ref, wi_ref, bi_ref, sp_ref, _mix_in,
             out_ref, xc_ref, h_ref, xpad, a0, a1, b0, b1, carry):
        @pl.when(pl.program_id(0) == 0)
        def _():
            xpad[0:SUBLANE, :] = jnp.zeros((SUBLANE, D_GROUP), F32)
            for bf in (a0, a1, b0, b1):
                bf[0:pad, :] = jnp.zeros((pad, D_GROUP), F32)
            carry[...] = jnp.zeros_like(carry)

        xpad[SUBLANE:SUBLANE + t, :] = xr_ref[...]
        xc = jnp.broadcast_to(lcb_ref[...], (t, D_GROUP))
        for k in range(LRU_CONV_WIDTH):
            o = SUBLANE - (LRU_CONV_WIDTH - 1) + k
            xc = xc + lcw_ref[k:k + 1, :] * xpad[o:o + t, :]
        xpad[0:SUBLANE, :] = xpad[t:t + SUBLANE, :]
        xc_ref[...] = xc
        _, i, a, m = _lru_gates(xc, wr_ref, br_ref, wi_ref, bi_ref, sp_ref)
        a0[pad:pad + t, :] = a
        b0[pad:pad + t, :] = m * (i * xc)
        b0[pad:pad + 1, :] += a0[pad:pad + 1, :] * carry[0:1, :]
        fin = _rscan_levels((a0, a1), (b0, b1), t, pad, reverse=False)
        hbuf = (b0, b1)[fin]
        carry[0:1, :] = hbuf[pad + t - 1:pad + t, :]
        h = hbuf[pad:pad + t, :]
        h_ref[...] = h
        out_ref[...] = (h * _gelu(xg_ref[...])).astype(BF16)

    ins = [h_in, h_in, lcw, lcb, wr, br, wi, bi, sp, mix]
    row = pl.BlockSpec((t, D_GROUP), lambda i: (i, 0))
    in_specs = [pl.BlockSpec((t, D_GROUP), lambda i: (i, COL_LRU_G)), pl.BlockSpec((t, D_GROUP), lambda i: (i, COL_LRU_X))] + \
               [_full_spec(a) for a in ins[2:9]] + [_ANY]
    return _call(
        body, grid=(s // t,), ins=ins, in_specs=in_specs,
        out_specs=[pl.BlockSpec((t, D_GROUP), lambda i: (i, MIX_LRU)), row, row],
        outs=[jax.ShapeDtypeStruct((s, D_MODEL), BF16)] + [jax.ShapeDtypeStruct((s, D_GROUP), F32)] * 2,
        aliases={9: 0},
        scratch=[pltpu.VMEM((SUBLANE + t, D_GROUP), F32)] + [pltpu.VMEM((pad + t, D_GROUP), F32)] * 4 +
                [pltpu.VMEM((SUBLANE, D_GROUP), F32)],
        name=name, rider=rider)


def _lru_bwd(h_in, xc_all, h_all, dmix, lcw, wr, br, wi, bi, sp, dh_all, *, name):
    s = h_in.shape[0]
    t = _seq_tile(s, LRU_TILE)
    nt = s // t
    pad = max(t // 2, SUBLANE)
    tb = t // SUBLANE

    def body(xg_ref, xr_ref, xc_ref, h_ref, hprev_ref, do_ref, lcw_ref, wr_ref, br_ref, wi_ref, bi_ref, sp_ref, _dh_in,
             dgr_ref, dwr_ref, dwi_ref, dlcw_ref, dbr_ref, dbi_ref, dsp_ref, dlcb_ref,
             a0, a1, b0, b1, hp, dxpad, carry):
        pid = pl.program_id(0)

        @pl.when(pid == 0)
        def _():
            for bf in (a0, a1, b0, b1):
                bf[pad + t:pad + t + pad, :] = jnp.zeros((pad, D_GROUP), F32)
            dxpad[t:t + SUBLANE, :] = jnp.zeros((SUBLANE, D_GROUP), F32)
            carry[...] = jnp.zeros_like(carry)
            for r in (dwr_ref, dwi_ref, dlcw_ref, dbr_ref, dbi_ref, dsp_ref, dlcb_ref):
                r[...] = jnp.zeros_like(r)

        xc = xc_ref[...]
        h = h_ref[...]
        dout = do_ref[...]
        gate, dgate = _gelu_and_grad(xg_ref[...])
        dgr_ref[:, :D_GROUP] = dout * h * dgate
        r, i, a, m = _lru_gates(xc, wr_ref, br_ref, wi_ref, bi_ref, sp_ref)

        a0[pad:pad + t, :] = a
        b0[pad:pad + t, :] = dout * gate
        b0[pad + t - 1:pad + t, :] += carry[0:1, :]
        a1[pad:pad + t, :] = a0[pad + 1:pad + 1 + t, :]
        fin = _rscan_levels((a1, a0), (b0, b1), t, pad, reverse=True)
        lam = (b0, b1)[fin][pad:pad + t, :]
        carry[0:1, :] = a[0:1, :] * lam[0:1, :]

        is_first = pid == nt - 1
        hp[0:SUBLANE, :] = jnp.where(is_first, 0.0, hprev_ref[...])
        hp[SUBLANE:SUBLANE + t, :] = h
        hprev = hp[SUBLANE - 1:SUBLANE - 1 + t, :]

        ix = i * xc
        dmm = lam * ix
        dix = lam * m
        da = lam * hprev - dmm * (a / m)
        dlog_a = da * a
        dr = dlog_a * (-LRU_C * sp_ref[...])
        dsp_ref[...] += _colsum(dlog_a * (-LRU_C * r))
        dpr = dr * r * (1.0 - r)
        dpi = dix * xc * i * (1.0 - i)
        dbr_ref[...] += _colsum(dpr)
        dbi_ref[...] += _colsum(dpi)
        dwr_ref[...] += _dot_tn(xc, dpr)
        dwi_ref[...] += _dot_tn(xc, dpi)
        dxc = dix * i + _dot_nt(dpr, wr_ref[...]) + _dot_nt(dpi, wi_ref[...])
        dlcb_ref[...] += _colsum(dxc)

        dxpad[0:t, :] = dxc
        xr = xr_ref[...]
        dxr = jnp.zeros((t, D_GROUP), F32)
        for k in range(LRU_CONV_WIDTH):
            o = (LRU_CONV_WIDTH - 1) - k
            sh = dxpad[o:o + t, :]
            dxr = dxr + lcw_ref[k:k + 1, :] * sh
            dlcw_ref[k:k + 1, :] += _colsum(xr * sh)
        dxpad[t:t + SUBLANE, :] = dxpad[0:SUBLANE, :]
        dgr_ref[:, D_GROUP:] = dxr

    def rev(col):
        return lambda i: (nt - 1 - i, col)

    ins = [h_in, h_in, xc_all, h_all, h_all, dmix, lcw, wr, br, wi, bi, sp, dh_all]
    in_specs = [pl.BlockSpec((t, D_GROUP), rev(COL_LRU_G)), pl.BlockSpec((t, D_GROUP), rev(COL_LRU_X)),
                pl.BlockSpec((t, D_GROUP), rev(0)), pl.BlockSpec((t, D_GROUP), rev(0)),
                pl.BlockSpec((SUBLANE, D_GROUP), lambda i: (jnp.maximum((nt - 1 - i) * tb - 1, 0), 0)),
                pl.BlockSpec((t, D_GROUP), rev(MIX_LRU))] + [_full_spec(a) for a in ins[6:12]] + [_ANY]
    vec = jax.ShapeDtypeStruct((1, D_GROUP), F32)
    mat = jax.ShapeDtypeStruct((D_GROUP, D_GROUP), F32)
    outs = [jax.ShapeDtypeStruct((s, N_IN_COLS), F32), mat, mat, jax.ShapeDtypeStruct((SUBLANE, D_GROUP), F32),
            vec, vec, vec, vec]
    out_specs = [pl.BlockSpec((t, 2 * D_GROUP), rev(COL_LRU_G // 2))] + [_full_spec(o) for o in outs[1:]]
    return pl.pallas_call(
        body, grid=(nt,), in_specs=in_specs, out_specs=out_specs, out_shape=outs, input_output_aliases={12: 0},
        scratch_shapes=[pltpu.VMEM((pad + t + pad, D_GROUP), F32)] * 4 +
                       [pltpu.VMEM((SUBLANE + t, D_GROUP), F32), pltpu.VMEM((t + SUBLANE, D_GROUP), F32),
                        pltpu.VMEM((SUBLANE, D_GROUP), F32)],
        compiler_params=_cparams(1), name=name)(*ins)


def _blockdiag(w):
    h, d, _ = w.shape
    return jnp.tile(w.reshape(h * d, d), (1, h)) * _block_mask(h, d, d)


ATTN_TILE = 512
ATTN_SCALE = ATTN_HEAD_DIM ** -0.5


def _attn_big(kv):
    m = kv.shape[0]
    kbig = jnp.tile(kv[:, :D_GROUP].T, (1, ATTN_HEADS)) * _block_mask(ATTN_HEADS, ATTN_HEAD_DIM, m)
    vbig = jnp.tile(kv[:, D_GROUP:], (ATTN_HEADS, 1)) * _block_mask(ATTN_HEADS, m, ATTN_HEAD_DIM)
    return kbig, vbig


def _attn_probs(q, kbig_ref, m):
    sc = _dot(q, kbig_ref[...]) * ATTN_SCALE
    ps = []
    for h in range(ATTN_HEADS):
        sh = sc[:, h * m:(h + 1) * m]
        e = jnp.exp(sh - jnp.max(sh, axis=1, keepdims=True))
        ps.append(e / jnp.sum(e, axis=1, keepdims=True))
    return ps


def _attn_fwd(h_in, kbig, vbig, mix, *, name):
    s = h_in.shape[0]
    t = _seq_tile(s, ATTN_TILE)
    m = kbig.shape[1] // ATTN_HEADS

    def body(q_ref, kbig_ref, vbig_ref, _mix_in, o_ref):
        ps = _attn_probs(q_ref[...], kbig_ref, m)
        o_ref[...] = _dot(jnp.concatenate(ps, axis=1), vbig_ref[...]).astype(BF16)

    return pl.pallas_call(
        body, grid=(s // t,),
        in_specs=[pl.BlockSpec((t, D_GROUP), lambda i: (i, COL_Q)), _full_spec(kbig), _full_spec(vbig), _ANY],
        out_specs=pl.BlockSpec((t, D_GROUP), lambda i: (i, MIX_ATTN)),
        out_shape=jax.ShapeDtypeStruct((s, D_MODEL), BF16), input_output_aliases={3: 0},
        compiler_params=_cparams(1), name=name)(h_in, kbig, vbig, mix)


def _attn_bwd(h_in, dmix, kbig, vbig, du_s5, dh_all, *, name):
    s = h_in.shape[0]
    t = _seq_tile(s, ATTN_TILE)
    m = kbig.shape[1] // ATTN_HEADS

    def body(q_ref, do_ref, kbig_ref, vbig_ref, dus5_ref, _dh_in, dpair_ref, dk_ref, dv_ref):
        @pl.when(pl.program_id(0) == 0)
        def _():
            dk_ref[...] = jnp.zeros_like(dk_ref)
            dv_ref[...] = jnp.zeros_like(dv_ref)

        q = q_ref[...]
        dout = do_ref[...]
        ps = _attn_probs(q, kbig_ref, m)
        dp = _dot_nt(dout, vbig_ref[...])
        dss = []
        for h in range(ATTN_HEADS):
            dph = dp[:, h * m:(h + 1) * m]
            dss.append(ps[h] * (dph - jnp.sum(dph * ps[h], axis=1, keepdims=True)))
        ds = (jnp.concatenate(dss, axis=1) * ATTN_SCALE).astype(BF16)
        dv_ref[...] += _dot_tn(jnp.concatenate(ps, axis=1), dout)
        dpair_ref[:, :D_GROUP] = dus5_ref[...]
        dpair_ref[:, D_GROUP:] = _dot_nt(ds, kbig_ref[...])
        dk_ref[...] += _dot_tn(q, ds)

    assert (COL_S5, COL_Q) == (4, 5)
    outs = [jax.ShapeDtypeStruct((s, N_IN_COLS), F32), jax.ShapeDtypeStruct(kbig.shape, F32),
            jax.ShapeDtypeStruct(vbig.shape, F32)]
    return pl.pallas_call(
        body, grid=(s // t,),
        in_specs=[pl.BlockSpec((t, D_GROUP), lambda i: (i, COL_Q)), pl.BlockSpec((t, D_GROUP), lambda i: (i, MIX_ATTN)),
                  _full_spec(kbig), _full_spec(vbig), pl.BlockSpec((t, D_GROUP), lambda i: (i, 0)), _ANY],
        out_specs=[pl.BlockSpec((t, 2 * D_GROUP), lambda i: (i, COL_S5 // 2)), _full_spec(outs[1]), _full_spec(outs[2])],
        out_shape=outs, input_output_aliases={5: 0},
        compiler_params=_cparams(1), name=name)(h_in, dmix, kbig, vbig, du_s5, dh_all)


FFN_TILE = 128
FFN_COL_CHUNK = 256
FFN_ROW_CHUNK = 64


def _ffn_conv(pad_ref, w_ref, b_ref, r0, ch, c0):
    cc = FFN_COL_CHUNK
    acc = jnp.broadcast_to(b_ref[:, c0:c0 + cc], (ch, cc))
    for k in range(FFN_CONV_WIDTH):
        o = SUBLANE - (FFN_CONV_WIDTH - 1) + k + r0
        acc = acc + w_ref[k:k + 1, c0:c0 + cc] * pad_ref[o:o + ch, c0:c0 + cc]
    return acc


def _ffn_gate_fwd(u, fcw, fcb, *, name, rider=None):
    s = u.shape[0]
    t = _seq_tile(s, FFN_TILE)
    ch = min(FFN_ROW_CHUNK, t)
    cc = FFN_COL_CHUNK

    def body(u_ref, w_ref, b_ref, o_ref, uc_ref, upad):
        @pl.when(pl.program_id(0) == 0)
        def _():
            upad[0:SUBLANE, :] = jnp.zeros((SUBLANE, 2 * D_FF), F32)

        upad[SUBLANE:SUBLANE + t, :] = u_ref[...].astype(F32)
        for c0 in range(0, D_FF, cc):
            for r0 in range(0, t, ch):
                val = _ffn_conv(upad, w_ref, b_ref, r0, ch, c0)
                gt = _ffn_conv(upad, w_ref, b_ref, r0, ch, c0 + D_FF)
                o_ref[r0:r0 + ch, c0:c0 + cc] = (val * _gelu(gt)).astype(BF16)
                uc_ref[r0:r0 + ch, c0:c0 + cc] = val.astype(BF16)
                uc_ref[r0:r0 + ch, c0 + D_FF:c0 + D_FF + cc] = gt.astype(BF16)
        upad[0:SUBLANE, :] = upad[t:t + SUBLANE, :]

    return _call(
        body, grid=(s // t,), ins=[u, fcw, fcb],
        in_specs=[pl.BlockSpec((t, 2 * D_FF), lambda i: (i, 0)), _full_spec(fcw), _full_spec(fcb)],
        out_specs=[pl.BlockSpec((t, D_FF), lambda i: (i, 0)), pl.BlockSpec((t, 2 * D_FF), lambda i: (i, 0))],
        outs=[jax.ShapeDtypeStruct((s, D_FF), BF16), jax.ShapeDtypeStruct((s, 2 * D_FF), BF16)],
        scratch=[pltpu.VMEM((SUBLANE + t, 2 * D_FF), F32)], name=name, rider=rider)


def _ffn_gate_bwd(u, uc, dh, fcw, *, name, rider=None):
    s = u.shape[0]
    t = _seq_tile(s, FFN_TILE)
    nt = s // t
    ch = min(FFN_ROW_CHUNK, t)
    cc = FFN_COL_CHUNK

    def body(u_ref, uc_ref, dh_ref, w_ref, du_ref, dw_ref, db_ref, dpad):
        @pl.when(pl.program_id(0) == 0)
        def _():
            dpad[t:t + SUBLANE, :] = jnp.zeros((SUBLANE, 2 * D_FF), F32)
            dw_ref[...] = jnp.zeros_like(dw_ref)
            db_ref[...] = jnp.zeros_like(db_ref)

        for c0 in range(0, D_FF, cc):
            for r0 in range(0, t, ch):
                val = uc_ref[r0:r0 + ch, c0:c0 + cc].astype(F32)
                gt = uc_ref[r0:r0 + ch, c0 + D_FF:c0 + D_FF + cc].astype(F32)
                gl, dgl = _gelu_and_grad(gt)
                d = dh_ref[r0:r0 + ch, c0:c0 + cc].astype(F32)
                dpad[r0:r0 + ch, c0:c0 + cc] = d * gl
                dpad[r0:r0 + ch, c0 + D_FF:c0 + D_FF + cc] = d * val * dgl
        for c0 in range(0, 2 * D_FF, cc):
            dbs = jnp.zeros((1, cc), F32)
            dws = [jnp.zeros((1, cc), F32) for _ in range(FFN_CONV_WIDTH)]
            for r0 in range(0, t, ch):
                x = u_ref[r0:r0 + ch, c0:c0 + cc].astype(F32)
                acc = jnp.zeros((ch, cc), F32)
                for k in range(FFN_CONV_WIDTH):
                    o = (FFN_CONV_WIDTH - 1) - k + r0
                    sh = dpad[o:o + ch, c0:c0 + cc]
                    acc = acc + w_ref[k:k + 1, c0:c0 + cc] * sh
                    dws[k] = dws[k] + _colsum(x * sh)
                    if k == FFN_CONV_WIDTH - 1:
                        dbs = dbs + _colsum(sh)
                du_ref[r0:r0 + ch, c0:c0 + cc] = acc.astype(BF16)
            db_ref[:, c0:c0 + cc] += dbs
            for k in range(FFN_CONV_WIDTH):
                dw_ref[k:k + 1, c0:c0 + cc] += dws[k]
        dpad[t:t + SUBLANE, :] = dpad[0:SUBLANE, :]

    outs = [jax.ShapeDtypeStruct((s, 2 * D_FF), BF16), jax.ShapeDtypeStruct((SUBLANE, 2 * D_FF), F32),
            jax.ShapeDtypeStruct((1, 2 * D_FF), F32)]
    return _call(
        body, grid=(nt,), ins=[u, uc, dh, fcw],
        in_specs=[pl.BlockSpec((t, 2 * D_FF), lambda i: (nt - 1 - i, 0)),
                  pl.BlockSpec((t, 2 * D_FF), lambda i: (nt - 1 - i, 0)),
                  pl.BlockSpec((t, D_FF), lambda i: (nt - 1 - i, 0)), _full_spec(fcw)],
        out_specs=[pl.BlockSpec((t, 2 * D_FF), lambda i: (nt - 1 - i, 0)), _full_spec(outs[1]), _full_spec(outs[2])],
        outs=outs, scratch=[pltpu.VMEM((t + SUBLANE, 2 * D_FF), F32)], name=name, rider=rider)


def _adamw_body(g_ref, w_ref, m_ref, v_ref, go_ref, d_ref, mo_ref, vo_ref):
    inv_b1 = 1.0 - ADAM_B1 ** ADAM_STEP
    inv_b2 = 1.0 - ADAM_B2 ** ADAM_STEP
    g = g_ref[0].astype(F32)
    for dev in range(1, N_DEV):
        g = g + g_ref[dev].astype(F32)
    go_ref[...] = g
    mn = ADAM_B1 * m_ref[...] + (1.0 - ADAM_B1) * g
    vn = ADAM_B2 * v_ref[...] + (1.0 - ADAM_B2) * (g * g)
    mo_ref[...] = mn
    vo_ref[...] = vn
    d_ref[...] = -ADAM_LR * ((mn / inv_b1) / (jnp.sqrt(vn / inv_b2) + ADAM_EPS) + ADAM_WD * w_ref[...])


def _adamw(gstack, w, m, v, *, name):
    _, r, c = gstack.shape
    tr = _pick_rows(r, PACK_ROW_BLOCK)

    def body(*refs):
        _adamw_body(*refs)

    blk = pl.BlockSpec((tr, c), lambda i: (i, 0))
    sh = jax.ShapeDtypeStruct((r, c), F32)
    return pl.pallas_call(
        body, grid=(r // tr,),
        in_specs=[pl.BlockSpec((N_DEV, tr, c), lambda i: (0, i, 0)), blk, blk, blk],
        out_specs=[blk] * 4, out_shape=[sh] * 4,
        compiler_params=_cparams(1), name=name)(gstack, w, m, v)


def _adamw_layer(gstack, w, m, v, layer, into, *, name):
    n_layers, r, c = w.shape
    tr = _pick_rows(r, PACK_ROW_BLOCK)

    def body(g_ref, w_ref, m_ref, v_ref, *rest):
        _adamw_body(g_ref, w_ref, m_ref, v_ref, *rest[-4:])

    blk = pl.BlockSpec((None, tr, c), lambda i: (layer, i, 0))
    sh = jax.ShapeDtypeStruct((n_layers, r, c), F32)
    into = list(into or [])
    return pl.pallas_call(
        body, grid=(r // tr,),
        in_specs=[pl.BlockSpec((N_DEV, tr, c), lambda i: (0, i, 0)), blk, blk, blk] + [_ANY] * len(into),
        out_specs=[blk] * 4, out_shape=[sh] * 4, input_output_aliases={4 + k: k for k in range(len(into))},
        compiler_params=_cparams(1), name=name)(gstack, w, m, v, *into)


def _exchange(rider, *, name):
    n = rider.n

    def body(*refs):
        x_refs, out_refs, sems = refs[:n], refs[n:2 * n], refs[2 * n:]
        rider.start(x_refs, out_refs, sems)
        rider.wait(x_refs, out_refs, sems)

    return pl.pallas_call(
        body, in_specs=[_ANY] * n, out_specs=[_ANY] * n, out_shape=rider.out_shapes(),
        scratch_shapes=rider.scratch(), name=name)(*rider.srcs)


def _pack_rows(n):
    rows = -(-n // PACK_COLS)
    return -(-rows // SUBLANE) * SUBLANE


def _pack(arrs, dtype):
    flat = jnp.concatenate([a.reshape(-1).astype(dtype) for a in arrs])
    rows = _pack_rows(flat.shape[0])
    flat = jnp.pad(flat, (0, rows * PACK_COLS - flat.shape[0]))
    return flat.reshape(rows, PACK_COLS)


def _pack_lead(arrs, dtype):
    flat = jnp.concatenate([a.reshape(N_DEV, -1).astype(dtype) for a in arrs], axis=1)
    rows = _pack_rows(flat.shape[1])
    flat = jnp.pad(flat, ((0, 0), (0, rows * PACK_COLS - flat.shape[1])))
    return flat.reshape(N_DEV, rows, PACK_COLS)


def _pack_layers(arrs, dtype):
    n_layers = arrs[0].shape[0]
    flat = jnp.concatenate([a.reshape(n_layers, -1).astype(dtype) for a in arrs], axis=1)
    rows = _pack_rows(flat.shape[1])
    flat = jnp.pad(flat, ((0, 0), (0, rows * PACK_COLS - flat.shape[1])))
    return flat.reshape(n_layers, rows, PACK_COLS)


def _unpack_layers(packed, shapes):
    flat = packed.reshape(packed.shape[0], -1)
    out, pos = [], 0
    for sh in shapes:
        n = math.prod(sh[1:])
        out.append(flat[:, pos:pos + n].reshape(sh))
        pos += n
    return out


def _unpack(packed, shapes, lead=False):
    flat = packed.reshape(N_DEV, -1) if lead else packed.reshape(-1)
    out, pos = [], 0
    for sh in shapes:
        n = math.prod(sh)
        out.append(flat[:, pos:pos + n].reshape((N_DEV,) + tuple(sh)) if lead else flat[pos:pos + n].reshape(sh))
        pos += n
    return out


def _join_shards(stacked, axis):
    return jnp.concatenate([stacked[d] for d in range(N_DEV)], axis=axis)


def _split_shards(full, axis):
    return jnp.stack(jnp.split(full, N_DEV, axis=axis), axis=0)


def _perm_in_cols(a, inverse=False):
    blocks = jnp.split(a, 6, axis=-1)
    if inverse:
        order = [IN_PERM.index(j) for j in range(6)]
    else:
        order = list(IN_PERM)
    return jnp.concatenate([blocks[j] for j in order], axis=-1)


def _row(v):
    return v.reshape(1, -1)


def _pad_rows(w, rows):
    return jnp.pad(w, ((0, rows - w.shape[0]), (0, 0)))


def _gn_avg_matrix():
    return _block_mask(GN_GROUPS, D_GROUP // GN_GROUPS, D_GROUP // GN_GROUPS) / (D_GROUP // GN_GROUPS)


def _layer_params(p, l):
    q = {}
    s5_mats, q["s5_vjp"] = jax.vjp(_s5_chunk_map, p["s5_lam_re"][l], p["s5_lam_im"][l], p["s5_log_dt"][l],
                                   p["s5_b_re"][l], p["s5_b_im"][l], p["s5_c_re"][l], p["s5_c_im"][l], p["s5_d"][l])
    q["s5_mats"] = [m.astype(BF16) for m in s5_mats[:5]]
    q["s5_a16"] = s5_mats[5]
    (q["wr"], q["wi"]), q["lru_w_vjp"] = jax.vjp(lambda r, i: (_blockdiag(r), _blockdiag(i)), p["lru_w_r"][l], p["lru_w_i"][l])
    q["wr"], q["wi"] = q["wr"].astype(BF16), q["wi"].astype(BF16)
    q["sp"], q["sp_vjp"] = jax.vjp(lambda lam: _row(jax.nn.softplus(-lam)), p["lru_lam"][l])
    return q


ROW_PARTS = ("a", "b", "c", "d")
TWO_LEVEL_RIDES = {(0, "ln_in_fwd")}
WEIGHT_RIDES = {(0, "ln_in_fwd"): [("w_in", 0)],
                (0, "inproj"): [("attn_w_kv", 0), ("w_out", 0), ("small_pack", 0)],
                (0, "cv_fwd"): [("ffn_w_up#a", 0), ("ffn_w_up#b", 0)],
                (0, "lru_fwd"): [("ffn_w_up#c", 0)],
                (0, "outproj"): [("ffn_w_up#d", 0)],
                (0, "ffn_up"): [("ffn_w_down", 0), ("w_in", 1), ("attn_w_kv", 1), ("w_out", 1)],
                (0, "ffn_gate_fwd"): [("ffn_w_up", 1)],
                (0, "ffn_down"): [("ffn_w_down", 1)]}
GRAD_RIDES = {(1, "ffn_gate_bwd"): [("ffn_w_down", 1)],
              (0, "dw_down"): [("w_out", 1), ("attn_w_kv", 1), ("w_in", 1)],
              (0, "dhff"): [("rep", 1), ("ssh", 1)],
              (0, "ffn_gate_bwd"): [("ffn_w_up", 1)],
              (0, "dw_up"): [("ffn_w_down", 0)],
              (0, "dx1"): [("ffn_w_up", 0)],
              (0, "cv_bwd"): [("w_out", 0)],
              (0, "dw_in"): [("attn_w_kv", 0), ("ssh", 0), ("rep", 0)],
              (0, "dxs"): [("w_in", 0)]}


def _join_cols(pieces, *, name):
    n_dev, k, c = pieces[0].shape
    assert (2 * c) % LANE == 0 and all(p.shape == pieces[0].shape for p in pieces)
    n_p = len(pieces)

    def body(*refs):
        o_ref = refs[n_p]
        for i in range(n_p):
            @pl.when(pl.program_id(0) == i)
            def _(i=i):
                o_ref[...] = jnp.concatenate([refs[i][0], refs[i][1]], axis=1)

    return pl.pallas_call(
        body, grid=(n_p, n_dev // 2),
        in_specs=[pl.BlockSpec((2, k, c), lambda i, j, p=p: (jnp.where(i == p, j, 0), 0, 0)) for p in range(n_p)],
        out_specs=pl.BlockSpec((k, 2 * c), lambda i, j: (i, j)),
        out_shape=jax.ShapeDtypeStruct((n_p * k, n_dev * c), pieces[0].dtype),
        compiler_params=_cparams(2), name=name)(*pieces)


def _split_cols(full, *, name):
    k, n = full.shape
    c = n // N_DEV
    assert (2 * c) % LANE == 0

    def body(x_ref, o_ref):
        o_ref[0] = x_ref[:, :c]
        o_ref[1] = x_ref[:, c:]

    return pl.pallas_call(
        body, grid=(N_DEV // 2,), in_specs=[pl.BlockSpec((k, 2 * c), lambda j: (0, j))],
        out_specs=pl.BlockSpec((2, k, c), lambda j: (j, 0, 0)), out_shape=jax.ShapeDtypeStruct((N_DEV, k, c), full.dtype),
        compiler_params=_cparams(1), name=name)(full)


def _assemble_weight(n, pieces, layer=0):
    if SHARDED[n] == 2:
        full = _join_cols(pieces, name=f"l{layer}_join_{n}")
        return _perm_in_cols(full) if n == "w_in" else full
    (gathered,) = pieces
    return gathered.reshape(-1, gathered.shape[-1])


def _grad_source(n, g, layer=0):
    g = g.astype(BF16)
    if SHARDED[n] == 2:
        if n == "w_in":
            g = _perm_in_cols(g, inverse=True)
        return _split_cols(g, name=f"l{layer}_split_d{n}"), "lead"
    return g, "rows"


def _hosted(fn, keys_rider, land, *args, **kw):
    keys, rider = keys_rider
    if rider is None:
        return fn(*args, **kw)
    out, routs = fn(*args, rider=rider, **kw)
    land(keys, routs)
    return out


def _local_step(x, mem, target, p, big_w, shards=None, unpack_small=None):
    dist = shards is not None
    gdt = BF16 if dist else F32
    small, saved = {}, []
    big_g, ready, recv = {}, {}, {}
    mavg = _gn_avg_matrix()
    s5_perm = _s5_perm()

    def weight_rider(l, host):
        keys = WEIGHT_RIDES.get((l, host), []) if dist else []
        if not keys:
            return keys, None
        srcs = [shards[n][ll] for n, ll in keys]
        return keys, (_TwoLevelGather(srcs) if (l, host) in TWO_LEVEL_RIDES else _Rider(srcs, ["all"] * len(keys)))

    halves = {}

    def land_weights(keys, routs):
        for (n, ll), r in zip(keys, routs):
            if n == "small_pack":
                p.update(unpack_small(r))
            elif "#" in n:
                base = n.split("#")[0]
                halves[(n, ll)] = r
                parts = [halves.get((base + "#" + tag, ll)) for tag in ROW_PARTS]
                if all(part is not None for part in parts):
                    big_w[base][ll] = _assemble_weight(base, parts, ll)
            else:
                big_w[n][ll] = _assemble_weight(n, [r], ll)

    def grad_rider(l, host):
        keys = [k for k in GRAD_RIDES.get((l, host), []) if k in ready] if dist else []
        return keys, (_Rider([ready[k][0] for k in keys], [ready[k][1] for k in keys]) if keys else None)

    def land_grads(keys, routs):
        for k, r in zip(keys, routs):
            recv[k] = r
            del ready[k]

    def big_grad(n, l, g):
        if dist:
            ready[(n, l)] = _grad_source(n, g, l)
        else:
            big_g[(n, l)] = g

    xs = _hosted(_ln_fwd, weight_rider(0, "ln_in_fwd"), land_weights, x, _row(p["ln_in_g"]), _row(p["ln_in_b"]),
                 name="ln_in_fwd")
    for l in range(DEPTH):
        q = _layer_params(p, l)
        n = f"l{l}_"
        hin = _hosted(_mm, weight_rider(l, "inproj"), land_weights, xs, big_w["w_in"][l], bias=_row(p["b_in"][l]),
                      name=n + "inproj")
        nb = hin.shape[0] // S5_CHUNK
        s5_pows = _s5_a16_powers(q["s5_a16"], nb.bit_length() - 1)
        s5_u2 = _s5_to_chunks(hin, COL_S5 * (D_GROUP // LANE), s5_perm, name=n + "s5_in")
        s5_y2, s5_x = _s5_core_fwd(s5_u2, *q["s5_mats"], s5_pows, name=n + "s5_core_fwd")
        s5_y1 = _s5_from_chunks(s5_y2, s5_perm, name=n + "s5_out")
        (mix,), _ = _s5_glu_fwd(s5_y1, p["s5_w_glu"][l], _row(p["s5_b_glu"][l]), name=n + "s5_glu_fwd")
        cvw = _pad_rows(p["cv_w"][l], CV_PAD)
        keys, rd = weight_rider(l, "cv_fwd")
        (mix, cv_c), routs = _cv_fwd(hin, cvw, _row(p["cv_b"][l]), _row(p["cv_gn_g"][l]), _row(p["cv_gn_b"][l]), mavg,
                                     p["cv_w_pw"][l], _row(p["cv_b_pw"][l]), mix, name=n + "cv_fwd", rider=rd)
        land_weights(keys, routs)
        lcw = _pad_rows(p["lru_conv_w"][l], SUBLANE)
        keys, rd = weight_rider(l, "lru_fwd")
        (mix, lru_xc, lru_h), routs = _lru_fwd(hin, lcw, _row(p["lru_conv_b"][l]), q["wr"], _row(p["lru_b_r"][l]), q["wi"],
                                               _row(p["lru_b_i"][l]), q["sp"], mix, name=n + "lru_fwd", rider=rd)
        land_weights(keys, routs)
        kv = _mm(mem, big_w["attn_w_kv"][l], name=n + "kv")
        (kbig, vbig), kv_vjp = jax.vjp(_attn_big, kv)
        kbig, vbig = kbig.astype(BF16), vbig.astype(BF16)
        mix = _attn_fwd(hin, kbig, vbig, mix, name=n + "attn_fwd")
        r1, x1 = _hosted(_mm, weight_rider(l, "outproj"), land_weights, mix, big_w["w_out"][l], bias=_row(p["b_out"][l]),
                         res=xs, res_scale=ALPHA, ln=(_row(p["ln1_g"][l]), _row(p["ln1_b"][l])), name=n + "outproj")
        u = _hosted(_mm, weight_rider(l, "ffn_up"), land_weights, x1, big_w["ffn_w_up"][l], out_dtype=BF16,
                    name=n + "ffn_up")
        fcw = _pad_rows(p["ffn_conv_w"][l], SUBLANE)
        fcb = _row(p["ffn_conv_b"][l])
        keys, rd = weight_rider(l, "ffn_gate_fwd")
        (hff, uc), routs = _ffn_gate_fwd(u, fcw, fcb, name=n + "ffn_gate_fwd", rider=rd)
        land_weights(keys, routs)
        if l < DEPTH - 1:
            r2, x2 = _hosted(_mm, weight_rider(l, "ffn_down"), land_weights, hff, big_w["ffn_w_down"][l], res=x1,
                             res_scale=ALPHA, ln=(_row(p["ln2_g"][l]), _row(p["ln2_b"][l])), name=n + "ffn_down")
        else:
            r2 = x2 = None
            dr_top, dg_top, db_top, loss_blk = _mm(
                hff, big_w["ffn_w_down"][l], res=x1, res_scale=ALPHA,
                loss=(_row(p["ln2_g"][l]), _row(p["ln2_b"][l]), target), name=n + "ffn_down")
        saved.append(dict(q=q, xs=xs, hin=hin, s5_y1=s5_y1, s5_u2=s5_u2, s5_x=s5_x, s5_pows=s5_pows, cvw=cvw, cv_c=cv_c, lcw=lcw, lru_xc=lru_xc,
                          lru_h=lru_h, kbig=kbig, vbig=vbig, kv_vjp=kv_vjp, mix=mix, r1=r1, x1=x1, u=u, uc=uc, fcw=fcw,
                          hff=hff, r2=r2))
        xs = x2

    top = DEPTH - 1
    loss = loss_blk[0, 0]
    dx = None

    for l in reversed(range(DEPTH)):
        sv = saved[l]
        q = sv["q"]
        n = f"l{l}_"
        g = {}
        if l == top:
            dr2, g["ln2_g"], g["ln2_b"] = dr_top, dg_top, db_top
        else:
            dr2, g["ln2_g"], g["ln2_b"] = from_above
        big_grad("ffn_w_down", l, _hosted(_mm_tn, grad_rider(l, "dw_down"), land_grads, sv["hff"], dr2, out_dtype=gdt,
                                          name=n + "dw_down"))
        dhff = _hosted(_mm, grad_rider(l, "dhff"), land_grads, dr2, big_w["ffn_w_down"][l], trans_b=True,
                       out_dtype=BF16, name=n + "dhff")
        keys, rd = grad_rider(l, "ffn_gate_bwd")
        (du, dfw, g["ffn_conv_b"]), routs = _ffn_gate_bwd(sv["u"], sv["uc"], dhff, sv["fcw"], name=n + "ffn_gate_bwd",
                                                          rider=rd)
        land_grads(keys, routs)
        g["ffn_conv_w"] = dfw[:FFN_CONV_WIDTH]
        if dist:
            ready[("ffn_w_up", l)] = (_hosted(_mm_tn, grad_rider(l, "dw_up"), land_grads, sv["x1"], du, out_dtype=gdt,
                                              dev_cols=du.shape[1] // N_DEV, name=n + "dw_up"), "lead")
        else:
            big_grad("ffn_w_up", l, _mm_tn(sv["x1"], du, name=n + "dw_up"))
        dr1, g["ln1_g"], g["ln1_b"], g["b_out"] = _hosted(
            _mm, grad_rider(l, "dx1"), land_grads, du, big_w["ffn_w_up"][l], trans_b=True, res=dr2, res_scale=ALPHA,
            ln_bwd=(sv["r1"], _row(p["ln1_g"][l])), name=n + "dx1")
        big_grad("w_out", l, _mm_tn(sv["mix"], dr1, out_dtype=gdt, name=n + "dw_out"))
        dmix = _mm(dr1, big_w["w_out"][l], trans_b=True, name=n + "dmix")

        hin = sv["hin"]
        keys, rd = grad_rider(l, "cv_bwd")
        (dh, g["cv_w_pw"], dcw, g["cv_b_pw"], g["cv_gn_g"], g["cv_gn_b"], g["cv_b"]), routs = _cv_bwd(
            hin, sv["cv_c"], dmix, sv["cvw"], _row(p["cv_gn_g"][l]), _row(p["cv_gn_b"][l]), mavg, p["cv_w_pw"][l],
            name=n + "cv_bwd", rider=rd)
        land_grads(keys, routs)
        g["cv_w"] = dcw[:CONV_WIDTH]
        dh, dwr, dwi, dlcw, g["lru_b_r"], g["lru_b_i"], dsp, g["lru_conv_b"] = _lru_bwd(
            hin, sv["lru_xc"], sv["lru_h"], dmix, sv["lcw"], q["wr"], _row(p["lru_b_r"][l]), q["wi"],
            _row(p["lru_b_i"][l]), q["sp"], dh, name=n + "lru_bwd")
        g["lru_conv_w"] = dlcw[:LRU_CONV_WIDTH]
        g["lru_w_r"], g["lru_w_i"] = q["lru_w_vjp"]((dwr, dwi))
        (g["lru_lam"],) = q["sp_vjp"](dsp)
        dy1, g["s5_w_glu"], g["s5_b_glu"] = _s5_glu_bwd(sv["s5_y1"], dmix, p["s5_w_glu"][l], _row(p["s5_b_glu"][l]),
                                                        name=n + "s5_glu_bwd")
        s5_du2, *s5_dmats = _s5_core_bwd(sv["s5_u2"], _s5_to_chunks(dy1, 0, s5_perm, name=n + "s5_din"), sv["s5_x"],
                                         *q["s5_mats"], sv["s5_pows"], name=n + "s5_core_bwd")
        (g["s5_lam_re"], g["s5_lam_im"], g["s5_log_dt"], g["s5_b_re"], g["s5_b_im"], g["s5_c_re"], g["s5_c_im"],
         g["s5_d"]) = q["s5_vjp"](tuple(s5_dmats))
        dh, dkbig, dvbig = _attn_bwd(hin, dmix, sv["kbig"], sv["vbig"],
                                     _s5_from_chunks(s5_du2, s5_perm, name=n + "s5_dout"), dh, name=n + "attn_bwd")
        (dkv,) = sv["kv_vjp"]((dkbig, dvbig))
        big_grad("attn_w_kv", l, _mm_tn(mem, dkv, out_dtype=gdt, name=n + "dw_kv"))

        if dist:
            ready[("ssh", l)] = (_pack_lead([_split_shards(g[k], SHARDED[k] - 1) for k in SMALL_SHARDED], F32), "lead")
            ready[("rep", l)] = (_pack([g[k] for k in REP_LAYERED], F32), "all")
        gw_in, g["b_in"] = _hosted(_mm_tn, grad_rider(l, "dw_in"), land_grads, sv["xs"], dh, colsum=True, out_dtype=gdt,
                                   name=n + "dw_in")
        big_grad("w_in", l, gw_in)
        if dist:
            small.setdefault("b_in", [None] * DEPTH)[l] = g["b_in"].reshape(-1)
        else:
            for k, v in g.items():
                small.setdefault(k, [None] * DEPTH)[l] = v.reshape(p[k].shape[1:])
        if l > 0:
            dr2_below, dg_below, db_below, _ = _hosted(
                _mm, grad_rider(l, "dxs"), land_grads, dh, big_w["w_in"][l], trans_b=True, res=dr1, res_scale=ALPHA,
                ln_bwd=(saved[l - 1]["r2"], _row(p["ln2_g"][l - 1])), name=n + "dxs")
            from_above = (dr2_below, dg_below, db_below)
        else:
            dx = _hosted(_mm, grad_rider(l, "dxs"), land_grads, dh, big_w["w_in"][l], trans_b=True, res=dr1,
                         res_scale=ALPHA, name=n + "dxs")

    keys, rd = grad_rider(0, "ln_in_bwd")
    if rd is None:
        grad_x, dgi, dbi, _ = _ln_bwd(x, dx, _row(p["ln_in_g"]), name="ln_in_bwd")
    else:
        (grad_x, dgi, dbi, _), routs = _ln_bwd(x, dx, _row(p["ln_in_g"]), name="ln_in_bwd", rider=rd)
        land_grads(keys, routs)
    out = {k: jnp.stack(v, axis=0) for k, v in small.items()}
    out["ln_in_g"], out["ln_in_b"] = dgi.reshape(-1), dbi.reshape(-1)
    return loss, grad_x, out, ((recv, ready) if dist else big_g)


def kernel(x, mem, ln_in_g, ln_in_b, w_in, b_in, s5_lam_re, s5_lam_im, s5_log_dt, s5_b_re, s5_b_im, s5_c_re, s5_c_im, s5_d, s5_w_glu, s5_b_glu, cv_w, cv_b, cv_gn_g, cv_gn_b, cv_w_pw, cv_b_pw, lru_conv_w, lru_conv_b, lru_w_r, lru_b_r, lru_w_i, lru_b_i, lru_lam, attn_w_kv, w_out, b_out, ln1_g, ln1_b, ffn_w_up, ffn_conv_w, ffn_conv_b, ffn_w_down, ln2_g, ln2_b, loss_target, m_ln_in_g, m_ln_in_b, m_w_in, m_b_in, m_s5_lam_re, m_s5_lam_im, m_s5_log_dt, m_s5_b_re, m_s5_b_im, m_s5_c_re, m_s5_c_im, m_s5_d, m_s5_w_glu, m_s5_b_glu, m_cv_w, m_cv_b, m_cv_gn_g, m_cv_gn_b, m_cv_w_pw, m_cv_b_pw, m_lru_conv_w, m_lru_conv_b, m_lru_w_r, m_lru_b_r, m_lru_w_i, m_lru_b_i, m_lru_lam, m_attn_w_kv, m_w_out, m_b_out, m_ln1_g, m_ln1_b, m_ffn_w_up, m_ffn_conv_w, m_ffn_conv_b, m_ffn_w_down, m_ln2_g, m_ln2_b, v_ln_in_g, v_ln_in_b, v_w_in, v_b_in, v_s5_lam_re, v_s5_lam_im, v_s5_log_dt, v_s5_b_re, v_s5_b_im, v_s5_c_re, v_s5_c_im, v_s5_d, v_s5_w_glu, v_s5_b_glu, v_cv_w, v_cv_b, v_cv_gn_g, v_cv_gn_b, v_cv_w_pw, v_cv_b_pw, v_lru_conv_w, v_lru_conv_b, v_lru_w_r, v_lru_b_r, v_lru_w_i, v_lru_b_i, v_lru_lam, v_attn_w_kv, v_w_out, v_b_out, v_ln1_g, v_ln1_b, v_ffn_w_up, v_ffn_conv_w, v_ffn_conv_b, v_ffn_w_down, v_ln2_g, v_ln2_b):
    args = locals()
    w = {n: args[n] for n in WEIGHTS}
    mom = {n: args["m_" + n] for n in WEIGHTS}
    var = {n: args["v_" + n] for n in WEIGHTS}

    shards = {n: w[n].astype(BF16) for n in BIG}
    part_rows = shards["ffn_w_up"].shape[1] // len(ROW_PARTS)
    for i, tag in enumerate(ROW_PARTS):
        shards["ffn_w_up#" + tag] = [shards["ffn_w_up"][0, i * part_rows:(i + 1) * part_rows]]
    shards["small_pack"] = [_pack([w[n] for n in SMALL_SHARDED], F32)]
    small_shapes = [w[n].shape for n in SMALL_SHARDED]

    def unpack_small(gathered):
        out = {n: _join_shards(st, SHARDED[n]) for n, st in zip(SMALL_SHARDED, _unpack(gathered, small_shapes, lead=True))}
        for n in ("s5_w_glu", "cv_w_pw"):
            out[n] = out[n].astype(BF16)
        return out

    big_w = {n: [None] * DEPTH for n in BIG}
    p = {n: w[n] for n in REPLICATED}
    p["b_in"] = _perm_in_cols(p["b_in"])

    loss, grad_x, g_small, (recv, ready) = _local_step(x[0], mem[0], loss_target[0], p, big_w, shards, unpack_small)
    loss = lax.psum(loss, ("x", "y", "c"))

    g_small["b_in"] = _perm_in_cols(g_small["b_in"], inverse=True)
    left = list(ready)
    rider = _Rider([ready[k][0] for k in left] + [_pack([g_small[n] for n in REP_LAST], F32)],
                   [ready[k][1] for k in left] + ["all"])
    got = _exchange(rider, name="exchange_grads")
    for k, r in zip(left, got):
        recv[k] = r

    res = [dict(), dict(), dict(), dict()]
    for n in BIG:
        outs = None
        for l in range(DEPTH):
            outs = _adamw_layer(recv[(n, l)], w[n], mom[n], var[n], l, outs, name=f"adamw_{n}_l{l}")
        for kind in range(4):
            res[kind][n] = outs[kind]
    for names, key, tag in ((SMALL_SHARDED, "ssh", "adamw_small_sharded"), (REP_LAYERED, "rep", "adamw_replicated")):
        gstack = jnp.concatenate([recv[(key, l)] for l in range(DEPTH)], axis=1)
        packs = [_pack_layers([t[n] for n in names], F32) for t in (w, mom, var)]
        rows = packs[0].shape[1]
        outs = _adamw(gstack, *[pk.reshape(DEPTH * rows, PACK_COLS) for pk in packs], name=tag)
        for kind in range(4):
            for n, a in zip(names, _unpack_layers(outs[kind].reshape(DEPTH, rows, PACK_COLS), [w[n].shape for n in names])):
                res[kind][n] = a
    outs = _adamw(got[len(left)], _pack([w[n] for n in REP_LAST], F32), _pack([mom[n] for n in REP_LAST], F32),
                  _pack([var[n] for n in REP_LAST], F32), name="adamw_last")
    for kind in range(4):
        for n, a in zip(REP_LAST, _unpack(outs[kind], [w[n].shape for n in REP_LAST])):
            res[kind][n] = a
    return (loss, grad_x[None], *[res[0][n] for n in WEIGHTS], *[res[1][n] for n in WEIGHTS],
            *[res[2][n] for n in WEIGHTS], *[res[3][n] for n in WEIGHTS])
```

```python
import math

import jax
import jax.numpy as jnp
from jax import lax
from jax.experimental import pallas as pl
from jax.experimental.pallas import tpu as pltpu

F32 = jnp.float32
BF16 = jnp.bfloat16

D_MODEL = 1024
DEPTH = 2
D_GROUP = 256
N_IN_COLS = 6 * D_GROUP
S5_GROUPS = 16
S5_CH = 16
S5_STATE = 64
CONV_WIDTH = 31
GN_GROUPS = 4
LRU_HEADS = 4
LRU_CONV_WIDTH = 4
LRU_C = 8.0
ATTN_HEADS = 4
ATTN_HEAD_DIM = 64
D_FF = 2816
FFN_CONV_WIDTH = 3
ALPHA = (2 * DEPTH) ** 0.25
LN_EPS = 1e-5
ADAM_LR, ADAM_B1, ADAM_B2, ADAM_EPS, ADAM_WD, ADAM_STEP = 0.001, 0.9, 0.999, 1e-08, 0.01, 10

N_DEV = 8
N_PEERS = N_DEV - 1
LANE = 128
SUBLANE = 8
VMEM_LIMIT = 56 * 1024 * 1024
PACK_COLS = 1024
PACK_ROW_BLOCK = 256
MM_ROW_TILE = 1024
MM_COL_CAP = 1408
MM_K_CAP = 1536
SEQ_TILE = 512

SHARDED = {
    "w_in": 2, "s5_w_glu": 1, "cv_w": 2, "cv_w_pw": 1, "lru_conv_w": 2, "attn_w_kv": 1,
    "w_out": 1, "ffn_w_up": 2, "ffn_conv_w": 2, "ffn_w_down": 1,
}
BIG = ("w_in", "attn_w_kv", "w_out", "ffn_w_up", "ffn_w_down")
SMALL_SHARDED = ("s5_w_glu", "cv_w", "cv_w_pw", "lru_conv_w", "ffn_conv_w")
WEIGHTS = ['ln_in_g', 'ln_in_b', 'w_in', 'b_in', 's5_lam_re', 's5_lam_im', 's5_log_dt', 's5_b_re', 's5_b_im',
           's5_c_re', 's5_c_im', 's5_d', 's5_w_glu', 's5_b_glu', 'cv_w', 'cv_b', 'cv_gn_g', 'cv_gn_b', 'cv_w_pw',
           'cv_b_pw', 'lru_conv_w', 'lru_conv_b', 'lru_w_r', 'lru_b_r', 'lru_w_i', 'lru_b_i', 'lru_lam',
           'attn_w_kv', 'w_out', 'b_out', 'ln1_g', 'ln1_b', 'ffn_w_up', 'ffn_conv_w', 'ffn_conv_b', 'ffn_w_down',
           'ln2_g', 'ln2_b']
REPLICATED = [n for n in WEIGHTS if n not in SHARDED]
REP_LAST = ("ln_in_g", "ln_in_b", "b_in")
REP_LAYERED = [n for n in REPLICATED if n not in REP_LAST]

COL_CV_V, COL_CV_G, COL_LRU_G, COL_LRU_X, COL_S5, COL_Q = range(6)
IN_PERM = (1, 2, 3, 4, 0, 5)
MIX_S5, MIX_CV, MIX_LRU, MIX_ATTN = range(4)


_ANY = pl.BlockSpec(memory_space=pl.ANY)
_MESH = pl.DeviceIdType.MESH


def _cparams(n_axes):
    return pltpu.CompilerParams(dimension_semantics=("arbitrary",) * n_axes, vmem_limit_bytes=VMEM_LIMIT)


def _pick(n, cap):
    if n <= cap:
        return n
    best = None
    for t in range(LANE, cap + 1, LANE):
        if n % t == 0:
            best = t
    assert best is not None, (n, cap)
    return best


def _pick_rows(n, cap):
    best = None
    for t in range(SUBLANE, min(n, cap) + 1, SUBLANE):
        if n % t == 0:
            best = t
    assert best is not None, (n, cap)
    return best


def _full_spec(arr):
    nd = arr.ndim
    return pl.BlockSpec(arr.shape, lambda *_: (0,) * nd)


def _dot(a, b):
    return lax.dot_general(a.astype(BF16), b.astype(BF16), (((1,), (0,)), ((), ())), preferred_element_type=F32)


def _dot_nt(a, b):
    return lax.dot_general(a.astype(BF16), b.astype(BF16), (((1,), (1,)), ((), ())), preferred_element_type=F32)


def _dot_tn(a, b):
    return lax.dot_general(a.astype(BF16), b.astype(BF16), (((0,), (0,)), ((), ())), preferred_element_type=F32)


def _dot_hi(a, b):
    b = b.astype(BF16)
    a1 = a.astype(BF16)
    r1 = a - a1.astype(F32)
    a2 = r1.astype(BF16)
    a3 = (r1 - a2.astype(F32)).astype(BF16)
    return _dot(a1, b) + _dot(a2, b) + _dot(a3, b)


def _colsum(x):
    return jnp.sum(x, axis=0, keepdims=True)


def _sigmoid(x):
    return 1.0 / (1.0 + jnp.exp(-x))


_GELU_K = math.sqrt(2.0 / math.pi)
_GELU_C = 0.044715


def _gelu(x):
    t = jnp.tanh(_GELU_K * (x + _GELU_C * x * x * x))
    return 0.5 * x * (1.0 + t)


def _gelu_and_grad(x):
    x2 = x * x
    t = jnp.tanh(_GELU_K * (x + _GELU_C * x2 * x))
    g = 0.5 * x * (1.0 + t)
    dg = 0.5 * (1.0 + t) + 0.5 * x * (1.0 - t * t) * (_GELU_K * (1.0 + 3.0 * _GELU_C * x2))
    return g, dg


def _neg_expm1(x):
    series = x * (1.0 + x * (0.5 + x * (1.0 / 6.0 + x * (1.0 / 24.0 + x * (1.0 / 120.0)))))
    return -jnp.where(jnp.abs(x) < 0.1, series, jnp.exp(x) - 1.0)


def _seq_tile(s, want):
    t = min(s, want)
    assert s % t == 0
    return t


class _Rider:
    def __init__(self, srcs, kinds):
        self.srcs, self.kinds = list(srcs), list(kinds)
        self.n = len(self.srcs)

    def out_shapes(self):
        shapes = []
        for x, kind in zip(self.srcs, self.kinds):
            if kind == "lead":
                shp = x.shape
            elif kind == "rows":
                shp = (N_DEV, x.shape[0] // N_DEV) + x.shape[1:]
            else:
                shp = (N_DEV,) + x.shape
            shapes.append(jax.ShapeDtypeStruct(shp, x.dtype))
        return shapes

    def scratch(self):
        return [pltpu.SemaphoreType.DMA((self.n * N_PEERS,)), pltpu.SemaphoreType.DMA((self.n * N_PEERS,)),
                pltpu.SemaphoreType.DMA((self.n,))]

    def _copies(self, x_refs, out_refs, sems):
        send_sems, recv_sems, local_sems = sems
        mx, my, mc = lax.axis_index("x"), lax.axis_index("y"), lax.axis_index("c")
        my_id = 4 * mx + 2 * my + mc

        def piece(i, dev):
            if self.kinds[i] == "lead":
                return x_refs[i].at[dev]
            if self.kinds[i] == "rows":
                r = x_refs[i].shape[0] // N_DEV
                return x_refs[i].at[pl.ds(pl.multiple_of(dev * r, SUBLANE), r)]
            return x_refs[i]

        mine = [pltpu.make_async_copy(piece(i, my_id), out_refs[i].at[my_id], local_sems.at[i]) for i in range(self.n)]
        copies = []
        for k in range(1, N_DEV):
            px, py, pc = mx ^ ((k >> 2) & 1), my ^ ((k >> 1) & 1), mc ^ (k & 1)
            for i in range(self.n):
                copies.append(pltpu.make_async_remote_copy(
                    src_ref=piece(i, 4 * px + 2 * py + pc), dst_ref=out_refs[i].at[my_id],
                    send_sem=send_sems.at[i * N_PEERS + k - 1], recv_sem=recv_sems.at[i * N_PEERS + k - 1],
                    device_id=(px, py, pc), device_id_type=_MESH))
        return mine, copies

    def start(self, x_refs, out_refs, sems):
        mine, copies = self._copies(x_refs, out_refs, sems)
        for cp in mine + copies:
            cp.start()

    def wait(self, x_refs, out_refs, sems):
        mine, copies = self._copies(x_refs, out_refs, sems)
        for cp in copies:
            cp.wait_recv()
        for cp in copies:
            cp.wait_send()
        for cp in mine:
            cp.wait()


class _TwoLevelGather:
    def __init__(self, srcs):
        self.srcs = list(srcs)
        self.n = len(self.srcs)

    def out_shapes(self):
        return [jax.ShapeDtypeStruct((N_DEV,) + x.shape, x.dtype) for x in self.srcs]

    def scratch(self):
        return [pltpu.SemaphoreType.DMA((self.n * N_PEERS,)), pltpu.SemaphoreType.DMA((self.n * N_PEERS,)),
                pltpu.SemaphoreType.DMA((self.n,))]

    def _tools(self, x_refs, out_refs, sems):
        send_sems, recv_sems, local_sems = sems
        mx, my, mc = lax.axis_index("x"), lax.axis_index("y"), lax.axis_index("c")
        me, sibling = (mx, my, mc), (mx, my, 1 - mc)
        chips = [(1 - mx, my), (mx, 1 - my), (1 - mx, 1 - my)]

        def slot(i, px, py, pc):
            return out_refs[i].at[4 * px + 2 * py + pc]

        def copy(i, k, block, to, src=None):
            return pltpu.make_async_remote_copy(
                src_ref=slot(i, *block) if src is None else src, dst_ref=slot(i, *block),
                send_sem=send_sems.at[i * N_PEERS + k], recv_sem=recv_sems.at[i * N_PEERS + k],
                device_id=to, device_id_type=_MESH)

        mine = [pltpu.make_async_copy(x_refs[i], slot(i, *me), local_sems.at[i]) for i in range(self.n)]
        first = []
        for i in range(self.n):
            first.append(copy(i, 0, me, sibling, src=x_refs[i]))
            first += [copy(i, 1 + j, me, (*chip, mc), src=x_refs[i]) for j, chip in enumerate(chips)]
        return me, sibling, chips, copy, mine, first

    def start(self, x_refs, out_refs, sems):
        _, _, _, _, mine, first = self._tools(x_refs, out_refs, sems)
        for cp in mine + first:
            cp.start()

    def wait(self, x_refs, out_refs, sems):
        me, sibling, chips, copy, mine, first = self._tools(x_refs, out_refs, sems)
        mc = me[2]
        passed = []
        for j, chip in enumerate(chips):
            for i in range(self.n):
                copy(i, 1 + j, (*chip, mc), me).wait_recv()
                fwd = copy(i, 4 + j, (*chip, mc), sibling)
                fwd.start()
                passed.append(fwd)
        for i in range(self.n):
            copy(i, 0, sibling, me).wait_recv()
            for j, chip in enumerate(chips):
                copy(i, 4 + j, (*chip, 1 - mc), me).wait_recv()
        for cp in first + passed:
            cp.wait_send()
        for cp in mine:
            cp.wait()


def _call(body, *, grid, ins, in_specs, outs, out_specs, scratch=(), aliases=None, name, rider=None):
    n_axes = len(grid)
    common = dict(grid=grid, input_output_aliases=aliases or {}, compiler_params=_cparams(n_axes), name=name)
    if rider is None:
        res = pl.pallas_call(body, in_specs=list(in_specs), out_specs=list(out_specs), out_shape=list(outs),
                             scratch_shapes=list(scratch), **common)(*ins)
        return list(res), []
    n_in, n_out, n_scr, nr = len(ins), len(outs), len(scratch), rider.n

    def wrapped(*refs):
        pos = [0]

        def take(k):
            part = refs[pos[0]:pos[0] + k]
            pos[0] += k
            return part

        a_in, r_in, a_out, r_out, a_scr, sems = take(n_in), take(nr), take(n_out), take(nr), take(n_scr), take(3)
        first = last = None
        for ax in range(n_axes):
            pid = pl.program_id(ax)
            f, l = pid == 0, pid == grid[ax] - 1
            first = f if first is None else jnp.logical_and(first, f)
            last = l if last is None else jnp.logical_and(last, l)

        @pl.when(first)
        def _():
            rider.start(r_in, r_out, sems)

        body(*a_in, *a_out, *a_scr)

        @pl.when(last)
        def _():
            rider.wait(r_in, r_out, sems)

    res = pl.pallas_call(
        wrapped, in_specs=list(in_specs) + [_ANY] * nr, out_specs=list(out_specs) + [_ANY] * nr,
        out_shape=list(outs) + rider.out_shapes(), scratch_shapes=list(scratch) + rider.scratch(), **common)(*ins, *rider.srcs)
    return list(res[:n_out]), list(res[n_out:])


def _block_mask(n_blocks, block_rows, block_cols):
    r = jnp.arange(n_blocks * block_rows) // block_rows
    c = jnp.arange(n_blocks * block_cols) // block_cols
    return (r[:, None] == c[None, :]).astype(F32)


def _mm(a, b, *, bias=None, res=None, res_scale=1.0, trans_b=False, out_dtype=F32, ln=None, ln_bwd=None, loss=None,
        name, rider=None):
    m, kdim = a.shape
    n = b.shape[0] if trans_b else b.shape[1]
    tm = _seq_tile(m, MM_ROW_TILE)
    tn = _pick(n, MM_COL_CAP)
    tk = _pick(kdim, MM_K_CAP)
    nk = kdim // tk
    has_bias, has_res, has_ln, has_lnb = bias is not None, res is not None, ln is not None, ln_bwd is not None
    has_loss = loss is not None
    assert not (has_ln or has_lnb or has_loss) or tn == n
    assert has_ln + has_lnb + has_loss <= 1

    def body(*refs):
        a_ref, b_ref = refs[0], refs[1]
        pos = 2
        bias_ref = res_ref = g_ref = beta_ref = x_ref = None
        if has_bias:
            bias_ref = refs[pos]
            pos += 1
        if has_res:
            res_ref = refs[pos]
            pos += 1
        if has_ln:
            g_ref, beta_ref = refs[pos], refs[pos + 1]
            pos += 2
        if has_lnb:
            x_ref, g_ref = refs[pos], refs[pos + 1]
            pos += 2
        if has_loss:
            g_ref, beta_ref, t_ref = refs[pos:pos + 3]
            pos += 3
        o_ref = refs[pos]
        pos += 1
        if has_ln:
            x_ref = refs[pos]
            pos += 1
        if has_lnb or has_loss:
            dg_ref, db_ref, ds_ref = refs[pos:pos + 3]
            pos += 3
        acc_ref = refs[pos]
        k = pl.program_id(2)

        @pl.when(k == 0)
        def _():
            acc_ref[...] = jnp.zeros_like(acc_ref)

        if has_lnb or has_loss:
            @pl.when(jnp.logical_and(pl.program_id(0) == 0, k == 0))
            def _():
                dg_ref[...] = jnp.zeros_like(dg_ref)
                db_ref[...] = jnp.zeros_like(db_ref)
                ds_ref[...] = jnp.zeros_like(ds_ref)

        if trans_b:
            acc_ref[...] += _dot_nt(a_ref[...], b_ref[...])
        else:
            acc_ref[...] += _dot(a_ref[...], b_ref[...])

        @pl.when(k == nk - 1)
        def _():
            r = acc_ref[...]
            if has_bias:
                r = r + bias_ref[...]
            if has_res:
                r = r + res_scale * res_ref[...]
            if has_lnb:
                x = x_ref[...]
                xc = x - jnp.mean(x, axis=1, keepdims=True)
                rstd = lax.rsqrt(jnp.mean(xc * xc, axis=1, keepdims=True) + LN_EPS)
                xh = xc * rstd
                dxh = r * g_ref[...]
                dx = rstd * (dxh - jnp.mean(dxh, axis=1, keepdims=True) - xh * jnp.mean(dxh * xh, axis=1, keepdims=True))
                o_ref[...] = dx
                dg_ref[...] += _colsum(r * xh)
                db_ref[...] += _colsum(r)
                ds_ref[...] += _colsum(dx)
            elif has_loss:
                gam = g_ref[...]
                xc = r - jnp.mean(r, axis=1, keepdims=True)
                rstd = lax.rsqrt(jnp.mean(xc * xc, axis=1, keepdims=True) + LN_EPS)
                xh = xc * rstd
                e = xh * gam + beta_ref[...] - t_ref[...]
                part = jnp.sum(jnp.sum(e * e, axis=1, keepdims=True), axis=0, keepdims=True) * (0.5 / n)
                ds_ref[...] += jnp.broadcast_to(part, ds_ref.shape)
                dy = e * (1.0 / n)
                dxh = dy * gam
                o_ref[...] = rstd * (dxh - jnp.mean(dxh, axis=1, keepdims=True) - xh * jnp.mean(dxh * xh, axis=1, keepdims=True))
                dg_ref[...] += _colsum(dy * xh)
                db_ref[...] += _colsum(dy)
            else:
                o_ref[...] = r.astype(out_dtype)
            if has_ln:
                xc = r - jnp.mean(r, axis=1, keepdims=True)
                var = jnp.mean(xc * xc, axis=1, keepdims=True)
                x_ref[...] = xc * lax.rsqrt(var + LN_EPS) * g_ref[...] + beta_ref[...]

    ins = [a, b]
    in_specs = [pl.BlockSpec((tm, tk), lambda i, j, k: (i, k)),
                pl.BlockSpec((tn, tk), lambda i, j, k: (j, k)) if trans_b
                else pl.BlockSpec((tk, tn), lambda i, j, k: (k, j))]
    if has_bias:
        ins.append(bias)
        in_specs.append(pl.BlockSpec((1, tn), lambda i, j, k: (0, j)))
    if has_res:
        ins.append(res)
        in_specs.append(pl.BlockSpec((tm, tn), lambda i, j, k: (i, j)))
    tile = pl.BlockSpec((tm, tn), lambda i, j, k: (i, j))
    vec = pl.BlockSpec((1, tn), lambda i, j, k: (0, j))
    out_shapes, out_specs = [jax.ShapeDtypeStruct((m, n), out_dtype)], [tile]
    if has_ln:
        ins += list(ln)
        in_specs += [vec] * 2
        out_shapes.append(jax.ShapeDtypeStruct((m, n), F32))
        out_specs.append(tile)
    if has_lnb:
        ins += list(ln_bwd)
        in_specs += [tile, vec]
        out_shapes += [jax.ShapeDtypeStruct((1, n), F32)] * 3
        out_specs += [vec] * 3
    if has_loss:
        ins += list(loss)
        in_specs += [vec, vec, tile]
        out_shapes += [jax.ShapeDtypeStruct((1, n), F32)] * 2 + [jax.ShapeDtypeStruct((SUBLANE, LANE), F32)]
        out_specs += [vec, vec, pl.BlockSpec((SUBLANE, LANE), lambda i, j, k: (0, 0))]
    outs, routs = _call(
        body, grid=(m // tm, n // tn, nk), ins=ins, in_specs=in_specs, outs=out_shapes, out_specs=out_specs,
        scratch=[pltpu.VMEM((tm, tn), F32)], name=name, rider=rider)
    out = tuple(outs) if (has_ln or has_lnb or has_loss) else outs[0]
    return out if rider is None else (out, routs)


def _mm_tn(a, b, *, colsum=False, out_dtype=F32, dev_cols=None, name, rider=None):
    s, ka = a.shape
    nb = b.shape[1]
    ts = _seq_tile(s, SEQ_TILE)
    tka = _pick(ka, MM_COL_CAP)
    tnb = _pick(nb, MM_COL_CAP)
    nk = s // ts
    assert not colsum or tka == ka
    per_tile = 1 if dev_cols is None else tnb // dev_cols
    assert dev_cols is None or tnb == per_tile * dev_cols

    def body(a_ref, b_ref, o_ref, *rest):
        cs_ref = rest[0] if colsum else None
        acc_ref = rest[-1]
        k = pl.program_id(2)

        @pl.when(k == 0)
        def _():
            acc_ref[...] = jnp.zeros_like(acc_ref)
            if colsum:
                cs_ref[...] = jnp.zeros_like(cs_ref)

        bv = b_ref[...]
        acc_ref[...] += _dot_tn(a_ref[...], bv)
        if colsum:
            cs_ref[...] += _colsum(bv.astype(F32))

        @pl.when(k == nk - 1)
        def _():
            if dev_cols is None:
                o_ref[...] = acc_ref[...].astype(out_dtype)
            else:
                for d in range(per_tile):
                    o_ref[d] = acc_ref[:, d * dev_cols:(d + 1) * dev_cols].astype(out_dtype)

    if dev_cols is None:
        main_shape, main_spec = (ka, nb), pl.BlockSpec((tka, tnb), lambda i, j, k: (i, j))
    else:
        main_shape = (nb // dev_cols, ka, dev_cols)
        main_spec = pl.BlockSpec((per_tile, tka, dev_cols), lambda i, j, k: (j, i, 0))
    outs, routs = _call(
        body, grid=(ka // tka, nb // tnb, nk), ins=[a, b],
        in_specs=[pl.BlockSpec((ts, tka), lambda i, j, k: (k, i)), pl.BlockSpec((ts, tnb), lambda i, j, k: (k, j))],
        outs=[jax.ShapeDtypeStruct(main_shape, out_dtype)] + ([jax.ShapeDtypeStruct((1, nb), F32)] if colsum else []),
        out_specs=[main_spec] + ([pl.BlockSpec((1, tnb), lambda i, j, k: (0, j))] if colsum else []),
        scratch=[pltpu.VMEM((tka, tnb), F32)], name=name, rider=rider)
    out = tuple(outs) if colsum else outs[0]
    return out if rider is None else (out, routs)


def _ln_fwd(r, g, b, *, name, rider=None):
    s, d = r.shape
    ts = _seq_tile(s, SEQ_TILE)

    def body(r_ref, g_ref, b_ref, o_ref):
        x = r_ref[...]
        mu = jnp.mean(x, axis=1, keepdims=True)
        xc = x - mu
        var = jnp.mean(xc * xc, axis=1, keepdims=True)
        o_ref[...] = xc * lax.rsqrt(var + LN_EPS) * g_ref[...] + b_ref[...]

    (out,), routs = _call(
        body, grid=(s // ts,), ins=[r, g, b],
        in_specs=[pl.BlockSpec((ts, d), lambda i: (i, 0)), _full_spec(g), _full_spec(b)],
        out_specs=[pl.BlockSpec((ts, d), lambda i: (i, 0))], outs=[jax.ShapeDtypeStruct((s, d), F32)],
        name=name, rider=rider)
    return out if rider is None else (out, routs)


def _ln_bwd(r, dy, g, *, name, rider=None):
    s, d = r.shape
    ts = _seq_tile(s, SEQ_TILE)

    def body(r_ref, dy_ref, g_ref, dr_ref, dg_ref, db_ref, ds_ref):
        @pl.when(pl.program_id(0) == 0)
        def _():
            dg_ref[...] = jnp.zeros_like(dg_ref)
            db_ref[...] = jnp.zeros_like(db_ref)
            ds_ref[...] = jnp.zeros_like(ds_ref)

        x = r_ref[...]
        dy = dy_ref[...]
        mu = jnp.mean(x, axis=1, keepdims=True)
        xc = x - mu
        var = jnp.mean(xc * xc, axis=1, keepdims=True)
        rstd = lax.rsqrt(var + LN_EPS)
        xh = xc * rstd
        dxh = dy * g_ref[...]
        m1 = jnp.mean(dxh, axis=1, keepdims=True)
        m2 = jnp.mean(dxh * xh, axis=1, keepdims=True)
        dr = rstd * (dxh - m1 - xh * m2)
        dr_ref[...] = dr
        dg_ref[...] += _colsum(dy * xh)
        db_ref[...] += _colsum(dy)
        ds_ref[...] += _colsum(dr)

    vec = jax.ShapeDtypeStruct((1, d), F32)
    vspec = pl.BlockSpec((1, d), lambda i: (0, 0))
    outs, routs = _call(
        body, grid=(s // ts,), ins=[r, dy, g],
        in_specs=[pl.BlockSpec((ts, d), lambda i: (i, 0)), pl.BlockSpec((ts, d), lambda i: (i, 0)), _full_spec(g)],
        out_specs=[pl.BlockSpec((ts, d), lambda i: (i, 0)), vspec, vspec, vspec],
        outs=[jax.ShapeDtypeStruct((s, d), F32), vec, vec, vec], name=name, rider=rider)
    return outs if rider is None else (outs, routs)


SCAN_CHUNK = 32


def _cscan_levels(bufs, apow_ref, t, pad, *, reverse):
    half = bufs[0].shape[1] // 2
    ch = min(SCAN_CHUNK, t)
    nlev = t.bit_length() - 1
    assert (1 << nlev) == t
    for k in range(nlev):
        d = 1 << k
        src, dst = bufs[k % 2], bufs[(k + 1) % 2]

        def chunk(c, carry, src=src, dst=dst, d=d, k=k):
            ar = apow_ref[k:k + 1, :half]
            ai = apow_ref[k:k + 1, half:]
            if reverse:
                ai = -ai
            r0 = pl.multiple_of(c * ch, ch)
            cur = src[pl.ds(pad + r0, ch), :]
            if d >= SUBLANE:
                off = pad + d if reverse else pad - d
                sh = src[pl.ds(off + r0, ch), :]
            elif reverse:
                blk = src[pl.ds(pad + r0, ch + SUBLANE), :]
                sh = pltpu.roll(blk, ch + SUBLANE - d, axis=0)[:ch, :]
            else:
                blk = src[pl.ds(pad - SUBLANE + r0, ch + SUBLANE), :]
                sh = pltpu.roll(blk, d, axis=0)[SUBLANE:, :]
            sre, sim = sh[:, :half], sh[:, half:]
            dst[pl.ds(pad + r0, ch), :half] = cur[:, :half] + ar * sre - ai * sim
            dst[pl.ds(pad + r0, ch), half:] = cur[:, half:] + ar * sim + ai * sre
            return carry

        lax.fori_loop(0, t // ch, chunk, 0)
    return nlev % 2


def _rscan_levels(abufs, bbufs, t, pad, *, reverse):
    nlev = t.bit_length() - 1
    assert (1 << nlev) == t
    for k in range(nlev):
        d = 1 << k
        asrc, adst = abufs[k % 2], abufs[(k + 1) % 2]
        bsrc, bdst = bbufs[k % 2], bbufs[(k + 1) % 2]
        off = pad + d if reverse else pad - d
        a = asrc[pad:pad + t, :]
        bdst[pad:pad + t, :] = a * bsrc[off:off + t, :] + bsrc[pad:pad + t, :]
        if k < nlev - 1:
            adst[pad:pad + t, :] = a * asrc[off:off + t, :]
    return nlev % 2


S5_CHUNK = 16
S5_SG = S5_GROUPS // 2
S5_SG_IN = 2 * S5_CHUNK * S5_CH
S5_SG_ST = 2 * S5_STATE


S5_HALF_SGS = S5_SG // 2
S5_HALF_IN = S5_HALF_SGS * S5_SG_IN


def _s5_perm():
    idx = jnp.arange(S5_HALF_IN)
    step, grp, chan = idx // LANE, (idx % LANE) // S5_CH, idx % S5_CH
    col = (grp // 2) * S5_SG_IN + (grp % 2) * (S5_CHUNK * S5_CH) + step * S5_CH + chan
    return (col[:, None] == idx[None, :]).astype(BF16)


def _s5_to_chunks(x, col_block, perm, *, name):
    s = x.shape[0]
    nb = s // S5_CHUNK

    def body(x_ref, perm_ref, o_ref):
        tok = jnp.concatenate([x_ref[pl.ds(t, nb, stride=S5_CHUNK), :].astype(BF16) for t in range(S5_CHUNK)], axis=1)
        grouped = _dot(tok, perm_ref[...]).astype(BF16)
        for k in range(S5_HALF_SGS):
            o_ref[k] = grouped[:, k * S5_SG_IN:(k + 1) * S5_SG_IN]

    return pl.pallas_call(
        body, grid=(2,),
        in_specs=[pl.BlockSpec((s, LANE), lambda h: (0, col_block + h)), _full_spec(perm)],
        out_specs=pl.BlockSpec((S5_HALF_SGS, nb, S5_SG_IN), lambda h: (h, 0, 0)),
        out_shape=jax.ShapeDtypeStruct((S5_SG, nb, S5_SG_IN), BF16),
        compiler_params=_cparams(1), name=name)(x, perm)


def _s5_from_chunks(y, perm, *, name):
    _, nb, _ = y.shape

    def body(y_ref, perm_ref, o_ref):
        grouped = jnp.concatenate([y_ref[k] for k in range(S5_HALF_SGS)], axis=1)
        hi = grouped.astype(BF16)
        lo = (grouped - hi.astype(F32)).astype(BF16)
        tok = _dot_nt(hi, perm_ref[...]) + _dot_nt(lo, perm_ref[...])
        for t in range(S5_CHUNK):
            o_ref[pl.ds(t, nb, stride=S5_CHUNK), :] = tok[:, t * LANE:(t + 1) * LANE]

    return pl.pallas_call(
        body, grid=(2,),
        in_specs=[pl.BlockSpec((S5_HALF_SGS, nb, S5_SG_IN), lambda h: (h, 0, 0)), _full_spec(perm)],
        out_specs=pl.BlockSpec((nb * S5_CHUNK, LANE), lambda h: (0, h)),
        out_shape=jax.ShapeDtypeStruct((nb * S5_CHUNK, D_GROUP), F32),
        compiler_params=_cparams(1), name=name)(y, perm)


def _s5_core_fwd(u2, m2, pre, pim, qre, qim, a16, *, name):
    sg, nb, nin = u2.shape
    st2 = 2 * S5_SG_ST
    pad = nb // 2

    def body(u_ref, m_ref, pre_ref, pim_ref, qre_ref, qim_ref, a_ref, y_ref, x_ref, buf0, buf1):
        @pl.when(pl.program_id(0) == 0)
        def _():
            buf0[0:pad, :] = jnp.zeros((pad, st2), F32)
            buf1[0:pad, :] = jnp.zeros((pad, st2), F32)

        u = u_ref[...]
        buf0[pad:pad + nb, :S5_SG_ST] = _dot(u, pre_ref[...])
        buf0[pad:pad + nb, S5_SG_ST:] = _dot(u, pim_ref[...])
        xbuf = (buf0, buf1)[_cscan_levels((buf0, buf1), a_ref, nb, pad, reverse=False)]
        x_ref[...] = xbuf[pad:pad + nb, :]
        xprev = xbuf[pad - 1:pad - 1 + nb, :]
        y_ref[...] = _dot(u, m_ref[...]) + _dot(xprev[:, :S5_SG_ST], qre_ref[...]) + _dot(xprev[:, S5_SG_ST:], qim_ref[...])

    ins = [u2, m2, pre, pim, qre, qim, a16]
    return pl.pallas_call(
        body, grid=(sg,), in_specs=[pl.BlockSpec((None,) + a.shape[1:], lambda i: (i, 0, 0)) for a in ins],
        out_specs=[pl.BlockSpec((None, nb, nin), lambda i: (i, 0, 0)), pl.BlockSpec((None, nb, st2), lambda i: (i, 0, 0))],
        out_shape=[jax.ShapeDtypeStruct((sg, nb, nin), F32), jax.ShapeDtypeStruct((sg, nb, st2), F32)],
        scratch_shapes=[pltpu.VMEM((pad + nb, st2), F32), pltpu.VMEM((pad + nb, st2), F32)],
        compiler_params=_cparams(1), name=name)(*ins)


def _s5_core_bwd(u2, dy2, x_all, m2, pre, pim, qre, qim, a16, *, name):
    sg, nb, nin = u2.shape
    half = S5_SG_ST
    st2 = 2 * half
    pad = nb // 2

    def body(u_ref, dy_ref, x_ref, m_ref, pre_ref, pim_ref, qre_ref, qim_ref, a_ref,
             du_ref, dm_ref, dpre_ref, dpim_ref, dqre_ref, dqim_ref, da_ref, buf2, buf3, xp):
        @pl.when(pl.program_id(0) == 0)
        def _():
            buf2[nb:nb + pad, :] = jnp.zeros((pad, st2), F32)
            buf3[nb:nb + pad, :] = jnp.zeros((pad, st2), F32)
            xp[0:SUBLANE, :] = jnp.zeros((SUBLANE, st2), F32)

        u = u_ref[...]
        dy = dy_ref[...]
        dm_ref[...] = _dot_tn(u, dy)
        xp[SUBLANE:SUBLANE + nb, :] = x_ref[...]
        xprev = xp[SUBLANE - 1:SUBLANE - 1 + nb, :]
        xre, xim = xprev[:, :half], xprev[:, half:]
        dqre_ref[...] = _dot_tn(xre, dy)
        dqim_ref[...] = _dot_tn(xim, dy)
        buf2[0:nb, :half] = _dot_nt(dy, qre_ref[...])
        buf2[0:nb, half:] = _dot_nt(dy, qim_ref[...])
        mbuf = (buf2, buf3)[_cscan_levels((buf2, buf3), a_ref, nb, 0, reverse=True)]
        lam = mbuf[1:1 + nb, :]
        lre, lim = lam[:, :half], lam[:, half:]
        dpre_ref[...] = _dot_tn(u, lre)
        dpim_ref[...] = _dot_tn(u, lim)
        du_ref[...] = _dot_nt(dy, m_ref[...]) + _dot_nt(lre, pre_ref[...]) + _dot_nt(lim, pim_ref[...])
        da_ref[:, :half] = _colsum(lre * xre + lim * xim)
        da_ref[:, half:] = _colsum(lim * xre - lre * xim)

    ins = [u2, dy2, x_all, m2, pre, pim, qre, qim, a16]
    outs = [jax.ShapeDtypeStruct((sg, nb, nin), F32)] + [jax.ShapeDtypeStruct(a.shape, F32) for a in (m2, pre, pim, qre, qim)] + \
           [jax.ShapeDtypeStruct((sg, 1, st2), F32)]
    return pl.pallas_call(
        body, grid=(sg,), in_specs=[pl.BlockSpec((None,) + a.shape[1:], lambda i: (i, 0, 0)) for a in ins],
        out_specs=[pl.BlockSpec((None,) + o.shape[1:], lambda i: (i, 0, 0)) for o in outs], out_shape=outs,
        scratch_shapes=[pltpu.VMEM((nb + pad, st2), F32), pltpu.VMEM((nb + pad, st2), F32),
                        pltpu.VMEM((SUBLANE + nb, st2), F32)],
        compiler_params=_cparams(1), name=name)(*ins)


def _s5_glu_fwd(y1, wglu, bglu, *, name, rider=None):
    s = y1.shape[0]
    t = _seq_tile(s, SEQ_TILE)

    def body(y1_ref, wglu_ref, bglu_ref, out_ref):
        y2 = _gelu(y1_ref[...])
        out_ref[...] = (y2 * _sigmoid(_dot(y2, wglu_ref[...]) + bglu_ref[...])).astype(BF16)

    return _call(
        body, grid=(s // t,), ins=[y1, wglu, bglu],
        in_specs=[pl.BlockSpec((t, D_GROUP), lambda i: (i, 0)), _full_spec(wglu), _full_spec(bglu)],
        out_specs=[pl.BlockSpec((t, D_GROUP), lambda i: (i, MIX_S5))], outs=[jax.ShapeDtypeStruct((s, D_MODEL), BF16)],
        name=name, rider=rider)


def _s5_glu_bwd(y1, dmix, wglu, bglu, *, name):
    s = y1.shape[0]
    t = _seq_tile(s, SEQ_TILE)

    def body(y1_ref, do_ref, wglu_ref, bglu_ref, dy1_ref, dwglu_ref, dbglu_ref):
        @pl.when(pl.program_id(0) == 0)
        def _():
            dwglu_ref[...] = jnp.zeros_like(dwglu_ref)
            dbglu_ref[...] = jnp.zeros_like(dbglu_ref)

        dout = do_ref[...]
        y2, dgelu = _gelu_and_grad(y1_ref[...])
        sg = _sigmoid(_dot(y2, wglu_ref[...]) + bglu_ref[...])
        dz = dout * y2 * sg * (1.0 - sg)
        dwglu_ref[...] += _dot_tn(y2, dz)
        dbglu_ref[...] += _colsum(dz)
        dy1_ref[...] = (dout * sg + _dot_nt(dz, wglu_ref[...])) * dgelu

    outs = [jax.ShapeDtypeStruct((s, D_GROUP), F32), jax.ShapeDtypeStruct((D_GROUP, D_GROUP), F32),
            jax.ShapeDtypeStruct((1, D_GROUP), F32)]
    return pl.pallas_call(
        body, grid=(s // t,),
        in_specs=[pl.BlockSpec((t, D_GROUP), lambda i: (i, 0)), pl.BlockSpec((t, D_GROUP), lambda i: (i, MIX_S5)),
                  _full_spec(wglu), _full_spec(bglu)],
        out_specs=[pl.BlockSpec((t, D_GROUP), lambda i: (i, 0)), _full_spec(outs[1]), _full_spec(outs[2])],
        out_shape=outs, compiler_params=_cparams(1), name=name)(y1, dmix, wglu, bglu)


def _pair_blockdiag(x):
    g, r, c = x.shape
    x = x.reshape(g // 2, 2, r, c)
    z = jnp.zeros_like(x[:, 0])
    return jnp.concatenate([jnp.concatenate([x[:, 0], z], axis=2), jnp.concatenate([z, x[:, 1]], axis=2)], axis=1)


def _s5_chunk_map(lam_re, lam_im, log_dt, b_re, b_im, c_re, c_im, d_skip):
    g, n, c, lc = S5_GROUPS, S5_STATE, S5_CH, S5_CHUNK
    dt = jnp.exp(log_dt)[:, None]
    mag, ang = lam_re * dt, lam_im * dt
    j = jnp.arange(lc + 1, dtype=F32)[:, None, None]
    pw_mag = jnp.exp(j * mag)
    pw_re, pw_im = pw_mag * jnp.cos(j * ang), pw_mag * jnp.sin(j * ang)
    a_re, a_im = pw_re[1], pw_im[1]
    den = lam_re * lam_re + lam_im * lam_im
    n_re = a_re - 1.0
    k_re = (n_re * lam_re + a_im * lam_im) / den
    k_im = (a_im * lam_re - n_re * lam_im) / den
    bb_re = k_re[..., None] * b_re - k_im[..., None] * b_im
    bb_im = k_re[..., None] * b_im + k_im[..., None] * b_re
    e_re = pw_re[:lc, :, :, None] * bb_re - pw_im[:lc, :, :, None] * bb_im
    e_im = pw_re[:lc, :, :, None] * bb_im + pw_im[:lc, :, :, None] * bb_re
    kern = jnp.einsum("gdn,jgnc->jgdc", c_re, e_re) - jnp.einsum("gdn,jgnc->jgdc", c_im, e_im)
    lags = jnp.pad(jnp.transpose(kern, (1, 3, 0, 2)), ((0, 0), (0, 0), (lc - 1, 0), (0, 0)))
    lags = lags.reshape(g, c, (2 * lc - 1) * c)
    m = jnp.stack([lags[:, :, (lc - 1 - s) * c:(2 * lc - 1 - s) * c] for s in range(lc)], axis=1).reshape(g, lc * c, lc * c)
    skip = jnp.tile(d_skip.reshape(g, 1, c), (1, lc, 1)).reshape(g, lc * c)
    m = m + jnp.eye(lc * c, dtype=F32)[None] * skip[:, None, :]
    p_re = jnp.transpose(e_re[::-1], (1, 0, 3, 2)).reshape(g, lc * c, n)
    p_im = jnp.transpose(e_im[::-1], (1, 0, 3, 2)).reshape(g, lc * c, n)
    f_re = c_re[None] * pw_re[1:, :, None, :] - c_im[None] * pw_im[1:, :, None, :]
    f_im = c_re[None] * pw_im[1:, :, None, :] + c_im[None] * pw_re[1:, :, None, :]
    q_re = jnp.transpose(f_re, (1, 3, 0, 2)).reshape(g, n, lc * c)
    q_im = -jnp.transpose(f_im, (1, 3, 0, 2)).reshape(g, n, lc * c)
    a16 = jnp.concatenate([pw_re[lc].reshape(S5_SG, 1, S5_SG_ST), pw_im[lc].reshape(S5_SG, 1, S5_SG_ST)], axis=2)
    return (_pair_blockdiag(m), _pair_blockdiag(p_re), _pair_blockdiag(p_im), _pair_blockdiag(q_re),
            _pair_blockdiag(q_im), a16)


def _s5_a16_powers(a16, nlev):
    half = S5_SG_ST
    re, im = a16[:, :, :half], a16[:, :, half:]
    rows = []
    for _ in range(nlev):
        rows.append(jnp.concatenate([re, im], axis=2))
        re, im = re * re - im * im, 2.0 * re * im
    n_rows = -(-nlev // SUBLANE) * SUBLANE
    rows += [jnp.zeros_like(rows[0])] * (n_rows - nlev)
    return lax.stop_gradient(jnp.concatenate(rows, axis=1))


CV_TILE = 256
CV_PAD = 32
CV_CHUNK = 64


def _shifted_copies(buf, shifted, rows):
    n = rows - SUBLANE
    for s in range(1, SUBLANE):
        shifted[s - 1, 0:n, :] = buf[s:s + n, :]


def _window(buf, shifted, o, ch):
    q, s = divmod(o, SUBLANE)
    if s == 0:
        return buf[o:o + ch, :]
    return shifted[s - 1, q * SUBLANE:q * SUBLANE + ch, :]


def _gn_stats(c, mavg):
    mu = _dot_hi(c, mavg)
    cen = c - mu
    var = _dot_hi(cen * cen, mavg)
    rstd = lax.rsqrt(var + LN_EPS)
    return cen * rstd, rstd


def _cv_fwd(h_in, cw, cb, gng, gnb, mavg, wpw, bpw, mix, *, name, rider=None):
    s = h_in.shape[0]
    t = _seq_tile(s, CV_TILE)
    ch = min(CV_CHUNK, t)

    def body(v_ref, g_ref, cw_ref, cb_ref, gng_ref, gnb_ref, mavg_ref, wpw_ref, bpw_ref, _mix_in, out_ref, c_ref, xpad,
             shifted):
        @pl.when(pl.program_id(0) == 0)
        def _():
            xpad[0:CV_PAD, :] = jnp.zeros((CV_PAD, D_GROUP), F32)

        xpad[CV_PAD:CV_PAD + t, :] = v_ref[...] * _sigmoid(g_ref[...])
        _shifted_copies(xpad, shifted, t + CV_PAD)
        for r0 in range(0, t, ch):
            acc = jnp.broadcast_to(cb_ref[...], (ch, D_GROUP))
            for k in range(CONV_WIDTH):
                o = CV_PAD - (CONV_WIDTH - 1) + k + r0
                acc = acc + cw_ref[k:k + 1, :] * _window(xpad, shifted, o, ch)
            c_ref[r0:r0 + ch, :] = acc
        xpad[0:CV_PAD, :] = xpad[t:t + CV_PAD, :]
        xn, _ = _gn_stats(c_ref[...], mavg_ref[...])
        gn = xn * gng_ref[...] + gnb_ref[...]
        out_ref[...] = (_dot(gn * _sigmoid(gn), wpw_ref[...]) + bpw_ref[...]).astype(BF16)

    ins = [h_in, h_in, cw, cb, gng, gnb, mavg, wpw, bpw, mix]
    in_specs = [pl.BlockSpec((t, D_GROUP), lambda i: (i, COL_CV_V)), pl.BlockSpec((t, D_GROUP), lambda i: (i, COL_CV_G))] + \
               [_full_spec(a) for a in ins[2:9]] + [_ANY]
    return _call(
        body, grid=(s // t,), ins=ins, in_specs=in_specs,
        out_specs=[pl.BlockSpec((t, D_GROUP), lambda i: (i, MIX_CV)), pl.BlockSpec((t, D_GROUP), lambda i: (i, 0))],
        outs=[jax.ShapeDtypeStruct((s, D_MODEL), BF16), jax.ShapeDtypeStruct((s, D_GROUP), F32)],
        aliases={9: 0},
        scratch=[pltpu.VMEM((CV_PAD + t, D_GROUP), F32), pltpu.VMEM((SUBLANE - 1, CV_PAD + t, D_GROUP), F32)],
        name=name, rider=rider)


def _cv_bwd(h_in, c, dmix, cw, gng, gnb, mavg, wpw, *, name, rider=None):
    s = h_in.shape[0]
    t = _seq_tile(s, CV_TILE)
    nt = s // t
    ch = min(CV_CHUNK, t)

    def body(v_ref, g_ref, c_ref, do_ref, cw_ref, gng_ref, gnb_ref, mavg_ref, wpw_ref,
             dvg_ref, dwpw_ref, dcw_ref, dbpw_ref, dgg_ref, dgb_ref, dcb_ref, dcpad, hgbuf, shifted):
        @pl.when(pl.program_id(0) == 0)
        def _():
            dcpad[t:t + CV_PAD, :] = jnp.zeros((CV_PAD, D_GROUP), F32)
            for r in (dwpw_ref, dcw_ref, dbpw_ref, dgg_ref, dgb_ref, dcb_ref):
                r[...] = jnp.zeros_like(r)

        mavg = mavg_ref[...]
        xn, rstd = _gn_stats(c_ref[...], mavg)
        gg = gng_ref[...]
        gn = xn * gg + gnb_ref[...]
        sg = _sigmoid(gn)
        dout = do_ref[...]
        dwpw_ref[...] += _dot_tn(gn * sg, dout)
        dbpw_ref[...] += _colsum(dout)
        dgn = _dot_nt(dout, wpw_ref[...]) * (sg * (1.0 + gn * (1.0 - sg)))
        dgg_ref[...] += _colsum(dgn * xn)
        dgb_ref[...] += _colsum(dgn)
        dxn = dgn * gg
        dc = rstd * (dxn - _dot_hi(dxn, mavg) - xn * _dot_hi(dxn * xn, mavg))
        dcb_ref[...] += _colsum(dc)
        dcpad[0:t, :] = dc

        v = v_ref[...]
        sgm = _sigmoid(g_ref[...])
        hgbuf[...] = v * sgm
        _shifted_copies(dcpad, shifted, t + CV_PAD)
        for r0 in range(0, t, ch):
            hg = hgbuf[r0:r0 + ch, :]
            acc = jnp.zeros((ch, D_GROUP), F32)
            for k in range(CONV_WIDTH):
                o = (CONV_WIDTH - 1) - k + r0
                sh = _window(dcpad, shifted, o, ch)
                acc = acc + cw_ref[k:k + 1, :] * sh
                dcw_ref[k:k + 1, :] += _colsum(hg * sh)
            hgbuf[r0:r0 + ch, :] = acc
        dcpad[t:t + CV_PAD, :] = dcpad[0:CV_PAD, :]
        dhg = hgbuf[...]
        dvg_ref[:, :D_GROUP] = dhg * sgm
        dvg_ref[:, D_GROUP:] = dhg * v * sgm * (1.0 - sgm)

    def rev(col):
        return lambda i: (nt - 1 - i, col)

    ins = [h_in, h_in, c, dmix, cw, gng, gnb, mavg, wpw]
    in_specs = [pl.BlockSpec((t, D_GROUP), rev(COL_CV_V)), pl.BlockSpec((t, D_GROUP), rev(COL_CV_G)),
                pl.BlockSpec((t, D_GROUP), rev(0)), pl.BlockSpec((t, D_GROUP), rev(MIX_CV))] + [_full_spec(a) for a in ins[4:]]
    vec = jax.ShapeDtypeStruct((1, D_GROUP), F32)
    outs = [jax.ShapeDtypeStruct((s, N_IN_COLS), F32),
            jax.ShapeDtypeStruct((D_GROUP, D_GROUP), F32), jax.ShapeDtypeStruct((CV_PAD, D_GROUP), F32), vec, vec, vec, vec]
    out_specs = [pl.BlockSpec((t, 2 * D_GROUP), rev(COL_CV_V // 2))] + [_full_spec(o) for o in outs[1:]]
    return _call(
        body, grid=(nt,), ins=ins, in_specs=in_specs, out_specs=out_specs, outs=outs,
        scratch=[pltpu.VMEM((t + CV_PAD, D_GROUP), F32), pltpu.VMEM((t, D_GROUP), F32),
                 pltpu.VMEM((SUBLANE - 1, t + CV_PAD, D_GROUP), F32)], name=name, rider=rider)


LRU_TILE = 256


def _lru_gates(xc, wr_ref, br_ref, wi_ref, bi_ref, sp_ref):
    r = _sigmoid(_dot(xc, wr_ref[...]) + br_ref[...])
    i = _sigmoid(_dot(xc, wi_ref[...]) + bi_ref[...])
    log_a = -LRU_C * r * sp_ref[...]
    a = jnp.exp(log_a)
    m = jnp.sqrt(_neg_expm1(2.0 * log_a))
    return r, i, a, m


def _lru_fwd(h_in, lcw, lcb, wr, br, wi, bi, sp, mix, *, name, rider=None):
    s = h_in.shape[0]
    t = _seq_tile(s, LRU_TILE)
    pad = max(t // 2, SUBLANE)

    def body(xg_ref, xr_ref, lcw_ref, lcb_ref, wr_ref, br_ref, wi_ref, bi_ref, sp_ref, _mix_in,
             out_ref, xc_ref, h_ref, xpad, a0, a1, b0, b1, carry):
        @pl.when(pl.program_id(0) == 0)
        def _():
            xpad[0:SUBLANE, :] = jnp.zeros((SUBLANE, D_GROUP), F32)
            for bf in (a0, a1, b0, b1):
                bf[0:pad, :] = jnp.zeros((pad, D_GROUP), F32)
            carry[...] = jnp.zeros_like(carry)

        xpad[SUBLANE:SUBLANE + t, :] = xr_ref[...]
        xc = jnp.broadcast_to(lcb_ref[...], (t, D_GROUP))
        for k in range(LRU_CONV_WIDTH):
            o = SUBLANE - (LRU_CONV_WIDTH - 1) + k
            xc = xc + lcw_ref[k:k + 1, :] * xpad[o:o + t, :]
        xpad[0:SUBLANE, :] = xpad[t:t + SUBLANE, :]
        xc_ref[...] = xc
        _, i, a, m = _lru_gates(xc, wr_ref, br_ref, wi_ref, bi_ref, sp_ref)
        a0[pad:pad + t, :] = a
        b0[pad:pad + t, :] = m * (i * xc)
        b0[pad:pad + 1, :] += a0[pad:pad + 1, :] * carry[0:1, :]
        fin = _rscan_levels((a0, a1), (b0, b1), t, pad, reverse=False)
        hbuf = (b0, b1)[fin]
        carry[0:1, :] = hbuf[pad + t - 1:pad + t, :]
        h = hbuf[pad:pad + t, :]
        h_ref[...] = h
        out_ref[...] = (h * _gelu(xg_ref[...])).astype(BF16)

    ins = [h_in, h_in, lcw, lcb, wr, br, wi, bi, sp, mix]
    row = pl.BlockSpec((t, D_GROUP), lambda i: (i, 0))
    in_specs = [pl.BlockSpec((t, D_GROUP), lambda i: (i, COL_LRU_G)), pl.BlockSpec((t, D_GROUP), lambda i: (i, COL_LRU_X))] + \
               [_full_spec(a) for a in ins[2:9]] + [_ANY]
    return _call(
        body, grid=(s // t,), ins=ins, in_specs=in_specs,
        out_specs=[pl.BlockSpec((t, D_GROUP), lambda i: (i, MIX_LRU)), row, row],
        outs=[jax.ShapeDtypeStruct((s, D_MODEL), BF16)] + [jax.ShapeDtypeStruct((s, D_GROUP), F32)] * 2,
        aliases={9: 0},
        scratch=[pltpu.VMEM((SUBLANE + t, D_GROUP), F32)] + [pltpu.VMEM((pad + t, D_GROUP), F32)] * 4 +
                [pltpu.VMEM((SUBLANE, D_GROUP), F32)],
        name=name, rider=rider)


def _lru_bwd(h_in, xc_all, h_all, dmix, lcw, wr, br, wi, bi, sp, dh_all, *, name):
    s = h_in.shape[0]
    t = _seq_tile(s, LRU_TILE)
    nt = s // t
    pad = max(t // 2, SUBLANE)
    tb = t // SUBLANE

    def body(xg_ref, xr_ref, xc_ref, h_ref, hprev_ref, do_ref, lcw_ref, wr_ref, br_ref, wi_ref, bi_ref, sp_ref, _dh_in,
             dgr_ref, dwr_ref, dwi_ref, dlcw_ref, dbr_ref, dbi_ref, dsp_ref, dlcb_ref,
             a0, a1, b0, b1, hp, dxpad, carry):
        pid = pl.program_id(0)

        @pl.when(pid == 0)
        def _():
            for bf in (a0, a1, b0, b1):
                bf[pad + t:pad + t + pad, :] = jnp.zeros((pad, D_GROUP), F32)
            dxpad[t:t + SUBLANE, :] = jnp.zeros((SUBLANE, D_GROUP), F32)
            carry[...] = jnp.zeros_like(carry)
            for r in (dwr_ref, dwi_ref, dlcw_ref, dbr_ref, dbi_ref, dsp_ref, dlcb_ref):
                r[...] = jnp.zeros_like(r)

        xc = xc_ref[...]
        h = h_ref[...]
        dout = do_ref[...]
        gate, dgate = _gelu_and_grad(xg_ref[...])
        dgr_ref[:, :D_GROUP] = dout * h * dgate
        r, i, a, m = _lru_gates(xc, wr_ref, br_ref, wi_ref, bi_ref, sp_ref)

        a0[pad:pad + t, :] = a
        b0[pad:pad + t, :] = dout * gate
        b0[pad + t - 1:pad + t, :] += carry[0:1, :]
        a1[pad:pad + t, :] = a0[pad + 1:pad + 1 + t, :]
        fin = _rscan_levels((a1, a0), (b0, b1), t, pad, reverse=True)
        lam = (b0, b1)[fin][pad:pad + t, :]
        carry[0:1, :] = a[0:1, :] * lam[0:1, :]

        is_first = pid == nt - 1
        hp[0:SUBLANE, :] = jnp.where(is_first, 0.0, hprev_ref[...])
        hp[SUBLANE:SUBLANE + t, :] = h
        hprev = hp[SUBLANE - 1:SUBLANE - 1 + t, :]

        ix = i * xc
        dmm = lam * ix
        dix = lam * m
        da = lam * hprev - dmm * (a / m)
        dlog_a = da * a
        dr = dlog_a * (-LRU_C * sp_ref[...])
        dsp_ref[...] += _colsum(dlog_a * (-LRU_C * r))
        dpr = dr * r * (1.0 - r)
        dpi = dix * xc * i * (1.0 - i)
        dbr_ref[...] += _colsum(dpr)
        dbi_ref[...] += _colsum(dpi)
        dwr_ref[...] += _dot_tn(xc, dpr)
        dwi_ref[...] += _dot_tn(xc, dpi)
        dxc = dix * i + _dot_nt(dpr, wr_ref[...]) + _dot_nt(dpi, wi_ref[...])
        dlcb_ref[...] += _colsum(dxc)

        dxpad[0:t, :] = dxc
        xr = xr_ref[...]
        dxr = jnp.zeros((t, D_GROUP), F32)
        for k in range(LRU_CONV_WIDTH):
            o = (LRU_CONV_WIDTH - 1) - k
            sh = dxpad[o:o + t, :]
            dxr = dxr + lcw_ref[k:k + 1, :] * sh
            dlcw_ref[k:k + 1, :] += _colsum(xr * sh)
        dxpad[t:t + SUBLANE, :] = dxpad[0:SUBLANE, :]
        dgr_ref[:, D_GROUP:] = dxr

    def rev(col):
        return lambda i: (nt - 1 - i, col)

    ins = [h_in, h_in, xc_all, h_all, h_all, dmix, lcw, wr, br, wi, bi, sp, dh_all]
    in_specs = [pl.BlockSpec((t, D_GROUP), rev(COL_LRU_G)), pl.BlockSpec((t, D_GROUP), rev(COL_LRU_X)),
                pl.BlockSpec((t, D_GROUP), rev(0)), pl.BlockSpec((t, D_GROUP), rev(0)),
                pl.BlockSpec((SUBLANE, D_GROUP), lambda i: (jnp.maximum((nt - 1 - i) * tb - 1, 0), 0)),
                pl.BlockSpec((t, D_GROUP), rev(MIX_LRU))] + [_full_spec(a) for a in ins[6:12]] + [_ANY]
    vec = jax.ShapeDtypeStruct((1, D_GROUP), F32)
    mat = jax.ShapeDtypeStruct((D_GROUP, D_GROUP), F32)
    outs = [jax.ShapeDtypeStruct((s, N_IN_COLS), F32), mat, mat, jax.ShapeDtypeStruct((SUBLANE, D_GROUP), F32),
            vec, vec, vec, vec]
    out_specs = [pl.BlockSpec((t, 2 * D_GROUP), rev(COL_LRU_G // 2))] + [_full_spec(o) for o in outs[1:]]
    return pl.pallas_call(
        body, grid=(nt,), in_specs=in_specs, out_specs=out_specs, out_shape=outs, input_output_aliases={12: 0},
        scratch_shapes=[pltpu.VMEM((pad + t + pad, D_GROUP), F32)] * 4 +
                       [pltpu.VMEM((SUBLANE + t, D_GROUP), F32), pltpu.VMEM((t + SUBLANE, D_GROUP), F32),
                        pltpu.VMEM((SUBLANE, D_GROUP), F32)],
        compiler_params=_cparams(1), name=name)(*ins)


def _blockdiag(w):
    h, d, _ = w.shape
    return jnp.tile(w.reshape(h * d, d), (1, h)) * _block_mask(h, d, d)


ATTN_TILE = 512
ATTN_SCALE = ATTN_HEAD_DIM ** -0.5


def _attn_big(kv):
    m = kv.shape[0]
    kbig = jnp.tile(kv[:, :D_GROUP].T, (1, ATTN_HEADS)) * _block_mask(ATTN_HEADS, ATTN_HEAD_DIM, m)
    vbig = jnp.tile(kv[:, D_GROUP:], (ATTN_HEADS, 1)) * _block_mask(ATTN_HEADS, m, ATTN_HEAD_DIM)
    return kbig, vbig


def _attn_probs(q, kbig_ref, m):
    sc = _dot(q, kbig_ref[...]) * ATTN_SCALE
    ps = []
    for h in range(ATTN_HEADS):
        sh = sc[:, h * m:(h + 1) * m]
        e = jnp.exp(sh - jnp.max(sh, axis=1, keepdims=True))
        ps.append(e / jnp.sum(e, axis=1, keepdims=True))
    return ps


def _attn_fwd(h_in, kbig, vbig, mix, *, name):
    s = h_in.shape[0]
    t = _seq_tile(s, ATTN_TILE)
    m = kbig.shape[1] // ATTN_HEADS

    def body(q_ref, kbig_ref, vbig_ref, _mix_in, o_ref):
        ps = _attn_probs(q_ref[...], kbig_ref, m)
        o_ref[...] = _dot(jnp.concatenate(ps, axis=1), vbig_ref[...]).astype(BF16)

    return pl.pallas_call(
        body, grid=(s // t,),
        in_specs=[pl.BlockSpec((t, D_GROUP), lambda i: (i, COL_Q)), _full_spec(kbig), _full_spec(vbig), _ANY],
        out_specs=pl.BlockSpec((t, D_GROUP), lambda i: (i, MIX_ATTN)),
        out_shape=jax.ShapeDtypeStruct((s, D_MODEL), BF16), input_output_aliases={3: 0},
        compiler_params=_cparams(1), name=name)(h_in, kbig, vbig, mix)


def _attn_bwd(h_in, dmix, kbig, vbig, du_s5, dh_all, *, name):
    s = h_in.shape[0]
    t = _seq_tile(s, ATTN_TILE)
    m = kbig.shape[1] // ATTN_HEADS

    def body(q_ref, do_ref, kbig_ref, vbig_ref, dus5_ref, _dh_in, dpair_ref, dk_ref, dv_ref):
        @pl.when(pl.program_id(0) == 0)
        def _():
            dk_ref[...] = jnp.zeros_like(dk_ref)
            dv_ref[...] = jnp.zeros_like(dv_ref)

        q = q_ref[...]
        dout = do_ref[...]
        ps = _attn_probs(q, kbig_ref, m)
        dp = _dot_nt(dout, vbig_ref[...])
        dss = []
        for h in range(ATTN_HEADS):
            dph = dp[:, h * m:(h + 1) * m]
            dss.append(ps[h] * (dph - jnp.sum(dph * ps[h], axis=1, keepdims=True)))
        ds = (jnp.concatenate(dss, axis=1) * ATTN_SCALE).astype(BF16)
        dv_ref[...] += _dot_tn(jnp.concatenate(ps, axis=1), dout)
        dpair_ref[:, :D_GROUP] = dus5_ref[...]
        dpair_ref[:, D_GROUP:] = _dot_nt(ds, kbig_ref[...])
        dk_ref[...] += _dot_tn(q, ds)

    assert (COL_S5, COL_Q) == (4, 5)
    outs = [jax.ShapeDtypeStruct((s, N_IN_COLS), F32), jax.ShapeDtypeStruct(kbig.shape, F32),
            jax.ShapeDtypeStruct(vbig.shape, F32)]
    return pl.pallas_call(
        body, grid=(s // t,),
        in_specs=[pl.BlockSpec((t, D_GROUP), lambda i: (i, COL_Q)), pl.BlockSpec((t, D_GROUP), lambda i: (i, MIX_ATTN)),
                  _full_spec(kbig), _full_spec(vbig), pl.BlockSpec((t, D_GROUP), lambda i: (i, 0)), _ANY],
        out_specs=[pl.BlockSpec((t, 2 * D_GROUP), lambda i: (i, COL_S5 // 2)), _full_spec(outs[1]), _full_spec(outs[2])],
        out_shape=outs, input_output_aliases={5: 0},
        compiler_params=_cparams(1), name=name)(h_in, dmix, kbig, vbig, du_s5, dh_all)


FFN_TILE = 128
FFN_COL_CHUNK = 256
FFN_ROW_CHUNK = 64


def _ffn_conv(pad_ref, w_ref, b_ref, r0, ch, c0):
    cc = FFN_COL_CHUNK
    acc = jnp.broadcast_to(b_ref[:, c0:c0 + cc], (ch, cc))
    for k in range(FFN_CONV_WIDTH):
        o = SUBLANE - (FFN_CONV_WIDTH - 1) + k + r0
        acc = acc + w_ref[k:k + 1, c0:c0 + cc] * pad_ref[o:o + ch, c0:c0 + cc]
    return acc


def _ffn_gate_fwd(u, fcw, fcb, *, name, rider=None):
    s = u.shape[0]
    t = _seq_tile(s, FFN_TILE)
    ch = min(FFN_ROW_CHUNK, t)
    cc = FFN_COL_CHUNK

    def body(u_ref, w_ref, b_ref, o_ref, uc_ref, upad):
        @pl.when(pl.program_id(0) == 0)
        def _():
            upad[0:SUBLANE, :] = jnp.zeros((SUBLANE, 2 * D_FF), F32)

        upad[SUBLANE:SUBLANE + t, :] = u_ref[...].astype(F32)
        for c0 in range(0, D_FF, cc):
            for r0 in range(0, t, ch):
                val = _ffn_conv(upad, w_ref, b_ref, r0, ch, c0)
                gt = _ffn_conv(upad, w_ref, b_ref, r0, ch, c0 + D_FF)
                o_ref[r0:r0 + ch, c0:c0 + cc] = (val * _gelu(gt)).astype(BF16)
                uc_ref[r0:r0 + ch, c0:c0 + cc] = val.astype(BF16)
                uc_ref[r0:r0 + ch, c0 + D_FF:c0 + D_FF + cc] = gt.astype(BF16)
        upad[0:SUBLANE, :] = upad[t:t + SUBLANE, :]

    return _call(
        body, grid=(s // t,), ins=[u, fcw, fcb],
        in_specs=[pl.BlockSpec((t, 2 * D_FF), lambda i: (i, 0)), _full_spec(fcw), _full_spec(fcb)],
        out_specs=[pl.BlockSpec((t, D_FF), lambda i: (i, 0)), pl.BlockSpec((t, 2 * D_FF), lambda i: (i, 0))],
        outs=[jax.ShapeDtypeStruct((s, D_FF), BF16), jax.ShapeDtypeStruct((s, 2 * D_FF), BF16)],
        scratch=[pltpu.VMEM((SUBLANE + t, 2 * D_FF), F32)], name=name, rider=rider)


def _ffn_gate_bwd(u, uc, dh, fcw, *, name, rider=None):
    s = u.shape[0]
    t = _seq_tile(s, FFN_TILE)
    nt = s // t
    ch = min(FFN_ROW_CHUNK, t)
    cc = FFN_COL_CHUNK

    def body(u_ref, uc_ref, dh_ref, w_ref, du_ref, dw_ref, db_ref, dpad):
        @pl.when(pl.program_id(0) == 0)
        def _():
            dpad[t:t + SUBLANE, :] = jnp.zeros((SUBLANE, 2 * D_FF), F32)
            dw_ref[...] = jnp.zeros_like(dw_ref)
            db_ref[...] = jnp.zeros_like(db_ref)

        for c0 in range(0, D_FF, cc):
            for r0 in range(0, t, ch):
                val = uc_ref[r0:r0 + ch, c0:c0 + cc]
                gl, dgl = _gelu_and_grad(uc_ref[r0:r0 + ch, c0 + D_FF:c0 + D_FF + cc])
                d = dh_ref[r0:r0 + ch, c0:c0 + cc]
                dpad[r0:r0 + ch, c0:c0 + cc] = (d * gl).astype(F32)
                dpad[r0:r0 + ch, c0 + D_FF:c0 + D_FF + cc] = (d * val * dgl).astype(F32)
        for c0 in range(0, 2 * D_FF, cc):
            dbs = jnp.zeros((1, cc), F32)
            dws = [jnp.zeros((1, cc), F32) for _ in range(FFN_CONV_WIDTH)]
            for r0 in range(0, t, ch):
                x = u_ref[r0:r0 + ch, c0:c0 + cc].astype(F32)
                acc = jnp.zeros((ch, cc), F32)
                for k in range(FFN_CONV_WIDTH):
                    o = (FFN_CONV_WIDTH - 1) - k + r0
                    sh = dpad[o:o + ch, c0:c0 + cc]
                    acc = acc + w_ref[k:k + 1, c0:c0 + cc] * sh
                    dws[k] = dws[k] + _colsum(x * sh)
                    if k == FFN_CONV_WIDTH - 1:
                        dbs = dbs + _colsum(sh)
                du_ref[r0:r0 + ch, c0:c0 + cc] = acc.astype(BF16)
            db_ref[:, c0:c0 + cc] += dbs
            for k in range(FFN_CONV_WIDTH):
                dw_ref[k:k + 1, c0:c0 + cc] += dws[k]
        dpad[t:t + SUBLANE, :] = dpad[0:SUBLANE, :]

    outs = [jax.ShapeDtypeStruct((s, 2 * D_FF), BF16), jax.ShapeDtypeStruct((SUBLANE, 2 * D_FF), F32),
            jax.ShapeDtypeStruct((1, 2 * D_FF), F32)]
    return _call(
        body, grid=(nt,), ins=[u, uc, dh, fcw],
        in_specs=[pl.BlockSpec((t, 2 * D_FF), lambda i: (nt - 1 - i, 0)),
                  pl.BlockSpec((t, 2 * D_FF), lambda i: (nt - 1 - i, 0)),
                  pl.BlockSpec((t, D_FF), lambda i: (nt - 1 - i, 0)), _full_spec(fcw)],
        out_specs=[pl.BlockSpec((t, 2 * D_FF), lambda i: (nt - 1 - i, 0)), _full_spec(outs[1]), _full_spec(outs[2])],
        outs=outs, scratch=[pltpu.VMEM((t + SUBLANE, 2 * D_FF), F32)], name=name, rider=rider)


def _adamw_body(g_ref, w_ref, m_ref, v_ref, go_ref, d_ref, mo_ref, vo_ref):
    inv_b1 = 1.0 - ADAM_B1 ** ADAM_STEP
    inv_b2 = 1.0 - ADAM_B2 ** ADAM_STEP
    g = g_ref[0].astype(F32)
    for dev in range(1, N_DEV):
        g = g + g_ref[dev].astype(F32)
    go_ref[...] = g
    mn = ADAM_B1 * m_ref[...] + (1.0 - ADAM_B1) * g
    vn = ADAM_B2 * v_ref[...] + (1.0 - ADAM_B2) * (g * g)
    mo_ref[...] = mn
    vo_ref[...] = vn
    d_ref[...] = -ADAM_LR * ((mn / inv_b1) / (jnp.sqrt(vn / inv_b2) + ADAM_EPS) + ADAM_WD * w_ref[...])


def _adamw(gstack, w, m, v, *, name):
    _, r, c = gstack.shape
    tr = _pick_rows(r, PACK_ROW_BLOCK)

    def body(*refs):
        _adamw_body(*refs)

    blk = pl.BlockSpec((tr, c), lambda i: (i, 0))
    sh = jax.ShapeDtypeStruct((r, c), F32)
    return pl.pallas_call(
        body, grid=(r // tr,),
        in_specs=[pl.BlockSpec((N_DEV, tr, c), lambda i: (0, i, 0)), blk, blk, blk],
        out_specs=[blk] * 4, out_shape=[sh] * 4,
        compiler_params=_cparams(1), name=name)(gstack, w, m, v)


def _adamw_layer(gstack, w, m, v, layer, into, *, name):
    n_layers, r, c = w.shape
    tr = _pick_rows(r, PACK_ROW_BLOCK)

    def body(g_ref, w_ref, m_ref, v_ref, *rest):
        _adamw_body(g_ref, w_ref, m_ref, v_ref, *rest[-4:])

    blk = pl.BlockSpec((None, tr, c), lambda i: (layer, i, 0))
    sh = jax.ShapeDtypeStruct((n_layers, r, c), F32)
    into = list(into or [])
    return pl.pallas_call(
        body, grid=(r // tr,),
        in_specs=[pl.BlockSpec((N_DEV, tr, c), lambda i: (0, i, 0)), blk, blk, blk] + [_ANY] * len(into),
        out_specs=[blk] * 4, out_shape=[sh] * 4, input_output_aliases={4 + k: k for k in range(len(into))},
        compiler_params=_cparams(1), name=name)(gstack, w, m, v, *into)


def _exchange(rider, *, name):
    n = rider.n

    def body(*refs):
        x_refs, out_refs, sems = refs[:n], refs[n:2 * n], refs[2 * n:]
        rider.start(x_refs, out_refs, sems)
        rider.wait(x_refs, out_refs, sems)

    return pl.pallas_call(
        body, in_specs=[_ANY] * n, out_specs=[_ANY] * n, out_shape=rider.out_shapes(),
        scratch_shapes=rider.scratch(), name=name)(*rider.srcs)


def _pack_rows(n):
    rows = -(-n // PACK_COLS)
    return -(-rows // SUBLANE) * SUBLANE


def _pack(arrs, dtype):
    flat = jnp.concatenate([a.reshape(-1).astype(dtype) for a in arrs])
    rows = _pack_rows(flat.shape[0])
    flat = jnp.pad(flat, (0, rows * PACK_COLS - flat.shape[0]))
    return flat.reshape(rows, PACK_COLS)


def _pack_lead(arrs, dtype):
    flat = jnp.concatenate([a.reshape(N_DEV, -1).astype(dtype) for a in arrs], axis=1)
    rows = _pack_rows(flat.shape[1])
    flat = jnp.pad(flat, ((0, 0), (0, rows * PACK_COLS - flat.shape[1])))
    return flat.reshape(N_DEV, rows, PACK_COLS)


def _pack_layers(arrs, dtype):
    n_layers = arrs[0].shape[0]
    flat = jnp.concatenate([a.reshape(n_layers, -1).astype(dtype) for a in arrs], axis=1)
    rows = _pack_rows(flat.shape[1])
    flat = jnp.pad(flat, ((0, 0), (0, rows * PACK_COLS - flat.shape[1])))
    return flat.reshape(n_layers, rows, PACK_COLS)


def _unpack_layers(packed, shapes):
    flat = packed.reshape(packed.shape[0], -1)
    out, pos = [], 0
    for sh in shapes:
        n = math.prod(sh[1:])
        out.append(flat[:, pos:pos + n].reshape(sh))
        pos += n
    return out


def _unpack(packed, shapes, lead=False):
    flat = packed.reshape(N_DEV, -1) if lead else packed.reshape(-1)
    out, pos = [], 0
    for sh in shapes:
        n = math.prod(sh)
        out.append(flat[:, pos:pos + n].reshape((N_DEV,) + tuple(sh)) if lead else flat[pos:pos + n].reshape(sh))
        pos += n
    return out


def _join_shards(stacked, axis):
    return jnp.concatenate([stacked[d] for d in range(N_DEV)], axis=axis)


def _split_shards(full, axis):
    return jnp.stack(jnp.split(full, N_DEV, axis=axis), axis=0)


def _perm_in_cols(a, inverse=False):
    blocks = jnp.split(a, 6, axis=-1)
    if inverse:
        order = [IN_PERM.index(j) for j in range(6)]
    else:
        order = list(IN_PERM)
    return jnp.concatenate([blocks[j] for j in order], axis=-1)


def _row(v):
    return v.reshape(1, -1)


def _pad_rows(w, rows):
    return jnp.pad(w, ((0, rows - w.shape[0]), (0, 0)))


def _gn_avg_matrix():
    return _block_mask(GN_GROUPS, D_GROUP // GN_GROUPS, D_GROUP // GN_GROUPS) / (D_GROUP // GN_GROUPS)


def _layer_params(p, l):
    q = {}
    s5_mats, q["s5_vjp"] = jax.vjp(_s5_chunk_map, p["s5_lam_re"][l], p["s5_lam_im"][l], p["s5_log_dt"][l],
                                   p["s5_b_re"][l], p["s5_b_im"][l], p["s5_c_re"][l], p["s5_c_im"][l], p["s5_d"][l])
    q["s5_mats"] = [m.astype(BF16) for m in s5_mats[:5]]
    q["s5_a16"] = s5_mats[5]
    (q["wr"], q["wi"]), q["lru_w_vjp"] = jax.vjp(lambda r, i: (_blockdiag(r), _blockdiag(i)), p["lru_w_r"][l], p["lru_w_i"][l])
    q["wr"], q["wi"] = q["wr"].astype(BF16), q["wi"].astype(BF16)
    q["sp"], q["sp_vjp"] = jax.vjp(lambda lam: _row(jax.nn.softplus(-lam)), p["lru_lam"][l])
    return q


ROW_PARTS = ("a", "b", "c", "d")
TWO_LEVEL_RIDES = {(0, "ln_in_fwd")}
WEIGHT_RIDES = {(0, "ln_in_fwd"): [("w_in", 0)],
                (0, "inproj"): [("attn_w_kv", 0), ("w_out", 0), ("small_pack", 0)],
                (0, "cv_fwd"): [("ffn_w_up#a", 0), ("ffn_w_up#b", 0)],
                (0, "lru_fwd"): [("ffn_w_up#c", 0)],
                (0, "outproj"): [("ffn_w_up#d", 0)],
                (0, "ffn_up"): [("ffn_w_down", 0), ("w_in", 1), ("attn_w_kv", 1), ("w_out", 1)],
                (0, "ffn_gate_fwd"): [("ffn_w_up", 1)],
                (0, "ffn_down"): [("ffn_w_down", 1)]}
GRAD_RIDES = {(1, "ffn_gate_bwd"): [("ffn_w_down", 1)],
              (0, "dw_down"): [("w_out", 1), ("attn_w_kv", 1), ("w_in", 1)],
              (0, "dhff"): [("rep", 1), ("ssh", 1)],
              (0, "ffn_gate_bwd"): [("ffn_w_up", 1)],
              (0, "dw_up"): [("ffn_w_down", 0)],
              (0, "dx1"): [("ffn_w_up", 0)],
              (0, "cv_bwd"): [("w_out", 0)],
              (0, "dw_in"): [("attn_w_kv", 0), ("ssh", 0), ("rep", 0)],
              (0, "dxs"): [("w_in", 0)]}


def _join_cols(pieces, *, name):
    n_dev, k, c = pieces[0].shape
    assert (2 * c) % LANE == 0 and all(p.shape == pieces[0].shape for p in pieces)
    n_p = len(pieces)

    def body(*refs):
        o_ref = refs[n_p]
        for i in range(n_p):
            @pl.when(pl.program_id(0) == i)
            def _(i=i):
                o_ref[...] = jnp.concatenate([refs[i][0], refs[i][1]], axis=1)

    return pl.pallas_call(
        body, grid=(n_p, n_dev // 2),
        in_specs=[pl.BlockSpec((2, k, c), lambda i, j, p=p: (jnp.where(i == p, j, 0), 0, 0)) for p in range(n_p)],
        out_specs=pl.BlockSpec((k, 2 * c), lambda i, j: (i, j)),
        out_shape=jax.ShapeDtypeStruct((n_p * k, n_dev * c), pieces[0].dtype),
        compiler_params=_cparams(2), name=name)(*pieces)


def _split_cols(full, *, name):
    k, n = full.shape
    c = n // N_DEV
    assert (2 * c) % LANE == 0

    def body(x_ref, o_ref):
        o_ref[0] = x_ref[:, :c]
        o_ref[1] = x_ref[:, c:]

    return pl.pallas_call(
        body, grid=(N_DEV // 2,), in_specs=[pl.BlockSpec((k, 2 * c), lambda j: (0, j))],
        out_specs=pl.BlockSpec((2, k, c), lambda j: (j, 0, 0)), out_shape=jax.ShapeDtypeStruct((N_DEV, k, c), full.dtype),
        compiler_params=_cparams(1), name=name)(full)


def _assemble_weight(n, pieces, layer=0):
    if SHARDED[n] == 2:
        full = _join_cols(pieces, name=f"l{layer}_join_{n}")
        return _perm_in_cols(full) if n == "w_in" else full
    (gathered,) = pieces
    return gathered.reshape(-1, gathered.shape[-1])


def _grad_source(n, g, layer=0):
    g = g.astype(BF16)
    if SHARDED[n] == 2:
        if n == "w_in":
            g = _perm_in_cols(g, inverse=True)
        return _split_cols(g, name=f"l{layer}_split_d{n}"), "lead"
    return g, "rows"


def _hosted(fn, keys_rider, land, *args, **kw):
    keys, rider = keys_rider
    if rider is None:
        return fn(*args, **kw)
    out, routs = fn(*args, rider=rider, **kw)
    land(keys, routs)
    return out


def _local_step(x, mem, target, p, big_w, shards=None, unpack_small=None):
    dist = shards is not None
    gdt = BF16 if dist else F32
    small, saved = {}, []
    big_g, ready, recv = {}, {}, {}
    mavg = _gn_avg_matrix()
    s5_perm = _s5_perm()

    def weight_rider(l, host):
        keys = WEIGHT_RIDES.get((l, host), []) if dist else []
        if not keys:
            return keys, None
        srcs = [shards[n][ll] for n, ll in keys]
        return keys, (_TwoLevelGather(srcs) if (l, host) in TWO_LEVEL_RIDES else _Rider(srcs, ["all"] * len(keys)))

    halves = {}

    def land_weights(keys, routs):
        for (n, ll), r in zip(keys, routs):
            if n == "small_pack":
                p.update(unpack_small(r))
            elif "#" in n:
                base = n.split("#")[0]
                halves[(n, ll)] = r
                parts = [halves.get((base + "#" + tag, ll)) for tag in ROW_PARTS]
                if all(part is not None for part in parts):
                    big_w[base][ll] = _assemble_weight(base, parts, ll)
            else:
                big_w[n][ll] = _assemble_weight(n, [r], ll)

    def grad_rider(l, host):
        keys = [k for k in GRAD_RIDES.get((l, host), []) if k in ready] if dist else []
        return keys, (_Rider([ready[k][0] for k in keys], [ready[k][1] for k in keys]) if keys else None)

    def land_grads(keys, routs):
        for k, r in zip(keys, routs):
            recv[k] = r
            del ready[k]

    def big_grad(n, l, g):
        if dist:
            ready[(n, l)] = _grad_source(n, g, l)
        else:
            big_g[(n, l)] = g

    xs = _hosted(_ln_fwd, weight_rider(0, "ln_in_fwd"), land_weights, x, _row(p["ln_in_g"]), _row(p["ln_in_b"]),
                 name="ln_in_fwd")
    for l in range(DEPTH):
        q = _layer_params(p, l)
        n = f"l{l}_"
        hin = _hosted(_mm, weight_rider(l, "inproj"), land_weights, xs, big_w["w_in"][l], bias=_row(p["b_in"][l]),
                      name=n + "inproj")
        nb = hin.shape[0] // S5_CHUNK
        s5_pows = _s5_a16_powers(q["s5_a16"], nb.bit_length() - 1)
        s5_u2 = _s5_to_chunks(hin, COL_S5 * (D_GROUP // LANE), s5_perm, name=n + "s5_in")
        s5_y2, s5_x = _s5_core_fwd(s5_u2, *q["s5_mats"], s5_pows, name=n + "s5_core_fwd")
        s5_y1 = _s5_from_chunks(s5_y2, s5_perm, name=n + "s5_out")
        (mix,), _ = _s5_glu_fwd(s5_y1, p["s5_w_glu"][l], _row(p["s5_b_glu"][l]), name=n + "s5_glu_fwd")
        cvw = _pad_rows(p["cv_w"][l], CV_PAD)
        keys, rd = weight_rider(l, "cv_fwd")
        (mix, cv_c), routs = _cv_fwd(hin, cvw, _row(p["cv_b"][l]), _row(p["cv_gn_g"][l]), _row(p["cv_gn_b"][l]), mavg,
                                     p["cv_w_pw"][l], _row(p["cv_b_pw"][l]), mix, name=n + "cv_fwd", rider=rd)
        land_weights(keys, routs)
        lcw = _pad_rows(p["lru_conv_w"][l], SUBLANE)
        keys, rd = weight_rider(l, "lru_fwd")
        (mix, lru_xc, lru_h), routs = _lru_fwd(hin, lcw, _row(p["lru_conv_b"][l]), q["wr"], _row(p["lru_b_r"][l]), q["wi"],
                                               _row(p["lru_b_i"][l]), q["sp"], mix, name=n + "lru_fwd", rider=rd)
        land_weights(keys, routs)
        kv = _mm(mem, big_w["attn_w_kv"][l], name=n + "kv")
        (kbig, vbig), kv_vjp = jax.vjp(_attn_big, kv)
        kbig, vbig = kbig.astype(BF16), vbig.astype(BF16)
        mix = _attn_fwd(hin, kbig, vbig, mix, name=n + "attn_fwd")
        r1, x1 = _hosted(_mm, weight_rider(l, "outproj"), land_weights, mix, big_w["w_out"][l], bias=_row(p["b_out"][l]),
                         res=xs, res_scale=ALPHA, ln=(_row(p["ln1_g"][l]), _row(p["ln1_b"][l])), name=n + "outproj")
        u = _hosted(_mm, weight_rider(l, "ffn_up"), land_weights, x1, big_w["ffn_w_up"][l], out_dtype=BF16,
                    name=n + "ffn_up")
        fcw = _pad_rows(p["ffn_conv_w"][l], SUBLANE)
        fcb = _row(p["ffn_conv_b"][l])
        keys, rd = weight_rider(l, "ffn_gate_fwd")
        (hff, uc), routs = _ffn_gate_fwd(u, fcw, fcb, name=n + "ffn_gate_fwd", rider=rd)
        land_weights(keys, routs)
        if l < DEPTH - 1:
            r2, x2 = _hosted(_mm, weight_rider(l, "ffn_down"), land_weights, hff, big_w["ffn_w_down"][l], res=x1,
                             res_scale=ALPHA, ln=(_row(p["ln2_g"][l]), _row(p["ln2_b"][l])), name=n + "ffn_down")
        else:
            r2 = x2 = None
            dr_top, dg_top, db_top, loss_blk = _mm(
                hff, big_w["ffn_w_down"][l], res=x1, res_scale=ALPHA,
                loss=(_row(p["ln2_g"][l]), _row(p["ln2_b"][l]), target), name=n + "ffn_down")
        saved.append(dict(q=q, xs=xs, hin=hin, s5_y1=s5_y1, s5_u2=s5_u2, s5_x=s5_x, s5_pows=s5_pows, cvw=cvw, cv_c=cv_c, lcw=lcw, lru_xc=lru_xc,
                          lru_h=lru_h, kbig=kbig, vbig=vbig, kv_vjp=kv_vjp, mix=mix, r1=r1, x1=x1, u=u, uc=uc, fcw=fcw,
                          hff=hff, r2=r2))
        xs = x2

    top = DEPTH - 1
    loss = loss_blk[0, 0]
    dx = None

    for l in reversed(range(DEPTH)):
        sv = saved[l]
        q = sv["q"]
        n = f"l{l}_"
        g = {}
        if l == top:
            dr2, g["ln2_g"], g["ln2_b"] = dr_top, dg_top, db_top
        else:
            dr2, g["ln2_g"], g["ln2_b"] = from_above
        big_grad("ffn_w_down", l, _hosted(_mm_tn, grad_rider(l, "dw_down"), land_grads, sv["hff"], dr2, out_dtype=gdt,
                                          name=n + "dw_down"))
        dhff = _hosted(_mm, grad_rider(l, "dhff"), land_grads, dr2, big_w["ffn_w_down"][l], trans_b=True,
                       out_dtype=BF16, name=n + "dhff")
        keys, rd = grad_rider(l, "ffn_gate_bwd")
        (du, dfw, g["ffn_conv_b"]), routs = _ffn_gate_bwd(sv["u"], sv["uc"], dhff, sv["fcw"], name=n + "ffn_gate_bwd",
                                                          rider=rd)
        land_grads(keys, routs)
        g["ffn_conv_w"] = dfw[:FFN_CONV_WIDTH]
        if dist:
            ready[("ffn_w_up", l)] = (_hosted(_mm_tn, grad_rider(l, "dw_up"), land_grads, sv["x1"], du, out_dtype=gdt,
                                              dev_cols=du.shape[1] // N_DEV, name=n + "dw_up"), "lead")
        else:
            big_grad("ffn_w_up", l, _mm_tn(sv["x1"], du, name=n + "dw_up"))
        dr1, g["ln1_g"], g["ln1_b"], g["b_out"] = _hosted(
            _mm, grad_rider(l, "dx1"), land_grads, du, big_w["ffn_w_up"][l], trans_b=True, res=dr2, res_scale=ALPHA,
            ln_bwd=(sv["r1"], _row(p["ln1_g"][l])), name=n + "dx1")
        big_grad("w_out", l, _mm_tn(sv["mix"], dr1, out_dtype=gdt, name=n + "dw_out"))
        dmix = _mm(dr1, big_w["w_out"][l], trans_b=True, name=n + "dmix")

        hin = sv["hin"]
        keys, rd = grad_rider(l, "cv_bwd")
        (dh, g["cv_w_pw"], dcw, g["cv_b_pw"], g["cv_gn_g"], g["cv_gn_b"], g["cv_b"]), routs = _cv_bwd(
            hin, sv["cv_c"], dmix, sv["cvw"], _row(p["cv_gn_g"][l]), _row(p["cv_gn_b"][l]), mavg, p["cv_w_pw"][l],
            name=n + "cv_bwd", rider=rd)
        land_grads(keys, routs)
        g["cv_w"] = dcw[:CONV_WIDTH]
        dh, dwr, dwi, dlcw, g["lru_b_r"], g["lru_b_i"], dsp, g["lru_conv_b"] = _lru_bwd(
            hin, sv["lru_xc"], sv["lru_h"], dmix, sv["lcw"], q["wr"], _row(p["lru_b_r"][l]), q["wi"],
            _row(p["lru_b_i"][l]), q["sp"], dh, name=n + "lru_bwd")
        g["lru_conv_w"] = dlcw[:LRU_CONV_WIDTH]
        g["lru_w_r"], g["lru_w_i"] = q["lru_w_vjp"]((dwr, dwi))
        (g["lru_lam"],) = q["sp_vjp"](dsp)
        dy1, g["s5_w_glu"], g["s5_b_glu"] = _s5_glu_bwd(sv["s5_y1"], dmix, p["s5_w_glu"][l], _row(p["s5_b_glu"][l]),
                                                        name=n + "s5_glu_bwd")
        s5_du2, *s5_dmats = _s5_core_bwd(sv["s5_u2"], _s5_to_chunks(dy1, 0, s5_perm, name=n + "s5_din"), sv["s5_x"],
                                         *q["s5_mats"], sv["s5_pows"], name=n + "s5_core_bwd")
        (g["s5_lam_re"], g["s5_lam_im"], g["s5_log_dt"], g["s5_b_re"], g["s5_b_im"], g["s5_c_re"], g["s5_c_im"],
         g["s5_d"]) = q["s5_vjp"](tuple(s5_dmats))
        dh, dkbig, dvbig = _attn_bwd(hin, dmix, sv["kbig"], sv["vbig"],
                                     _s5_from_chunks(s5_du2, s5_perm, name=n + "s5_dout"), dh, name=n + "attn_bwd")
        (dkv,) = sv["kv_vjp"]((dkbig, dvbig))
        big_grad("attn_w_kv", l, _mm_tn(mem, dkv, out_dtype=gdt, name=n + "dw_kv"))

        if dist:
            ready[("ssh", l)] = (_pack_lead([_split_shards(g[k], SHARDED[k] - 1) for k in SMALL_SHARDED], F32), "lead")
            ready[("rep", l)] = (_pack([g[k] for k in REP_LAYERED], F32), "all")
        gw_in, g["b_in"] = _hosted(_mm_tn, grad_rider(l, "dw_in"), land_grads, sv["xs"], dh, colsum=True, out_dtype=gdt,
                                   name=n + "dw_in")
        big_grad("w_in", l, gw_in)
        if dist:
            small.setdefault("b_in", [None] * DEPTH)[l] = g["b_in"].reshape(-1)
        else:
            for k, v in g.items():
                small.setdefault(k, [None] * DEPTH)[l] = v.reshape(p[k].shape[1:])
        if l > 0:
            dr2_below, dg_below, db_below, _ = _hosted(
                _mm, grad_rider(l, "dxs"), land_grads, dh, big_w["w_in"][l], trans_b=True, res=dr1, res_scale=ALPHA,
                ln_bwd=(saved[l - 1]["r2"], _row(p["ln2_g"][l - 1])), name=n + "dxs")
            from_above = (dr2_below, dg_below, db_below)
        else:
            dx = _hosted(_mm, grad_rider(l, "dxs"), land_grads, dh, big_w["w_in"][l], trans_b=True, res=dr1,
                         res_scale=ALPHA, name=n + "dxs")

    keys, rd = grad_rider(0, "ln_in_bwd")
    if rd is None:
        grad_x, dgi, dbi, _ = _ln_bwd(x, dx, _row(p["ln_in_g"]), name="ln_in_bwd")
    else:
        (grad_x, dgi, dbi, _), routs = _ln_bwd(x, dx, _row(p["ln_in_g"]), name="ln_in_bwd", rider=rd)
        land_grads(keys, routs)
    out = {k: jnp.stack(v, axis=0) for k, v in small.items()}
    out["ln_in_g"], out["ln_in_b"] = dgi.reshape(-1), dbi.reshape(-1)
    return loss, grad_x, out, ((recv, ready) if dist else big_g)


def kernel(x, mem, ln_in_g, ln_in_b, w_in, b_in, s5_lam_re, s5_lam_im, s5_log_dt, s5_b_re, s5_b_im, s5_c_re, s5_c_im, s5_d, s5_w_glu, s5_b_glu, cv_w, cv_b, cv_gn_g, cv_gn_b, cv_w_pw, cv_b_pw, lru_conv_w, lru_conv_b, lru_w_r, lru_b_r, lru_w_i, lru_b_i, lru_lam, attn_w_kv, w_out, b_out, ln1_g, ln1_b, ffn_w_up, ffn_conv_w, ffn_conv_b, ffn_w_down, ln2_g, ln2_b, loss_target, m_ln_in_g, m_ln_in_b, m_w_in, m_b_in, m_s5_lam_re, m_s5_lam_im, m_s5_log_dt, m_s5_b_re, m_s5_b_im, m_s5_c_re, m_s5_c_im, m_s5_d, m_s5_w_glu, m_s5_b_glu, m_cv_w, m_cv_b, m_cv_gn_g, m_cv_gn_b, m_cv_w_pw, m_cv_b_pw, m_lru_conv_w, m_lru_conv_b, m_lru_w_r, m_lru_b_r, m_lru_w_i, m_lru_b_i, m_lru_lam, m_attn_w_kv, m_w_out, m_b_out, m_ln1_g, m_ln1_b, m_ffn_w_up, m_ffn_conv_w, m_ffn_conv_b, m_ffn_w_down, m_ln2_g, m_ln2_b, v_ln_in_g, v_ln_in_b, v_w_in, v_b_in, v_s5_lam_re, v_s5_lam_im, v_s5_log_dt, v_s5_b_re, v_s5_b_im, v_s5_c_re, v_s5_c_im, v_s5_d, v_s5_w_glu, v_s5_b_glu, v_cv_w, v_cv_b, v_cv_gn_g, v_cv_gn_b, v_cv_w_pw, v_cv_b_pw, v_lru_conv_w, v_lru_conv_b, v_lru_w_r, v_lru_b_r, v_lru_w_i, v_lru_b_i, v_lru_lam, v_attn_w_kv, v_w_out, v_b_out, v_ln1_g, v_ln1_b, v_ffn_w_up, v_ffn_conv_w, v_ffn_conv_b, v_ffn_w_down, v_ln2_g, v_ln2_b):
    args = locals()
    w = {n: args[n] for n in WEIGHTS}
    mom = {n: args["m_" + n] for n in WEIGHTS}
    var = {n: args["v_" + n] for n in WEIGHTS}

    shards = {n: w[n].astype(BF16) for n in BIG}
    part_rows = shards["ffn_w_up"].shape[1] // len(ROW_PARTS)
    for i, tag in enumerate(ROW_PARTS):
        shards["ffn_w_up#" + tag] = [shards["ffn_w_up"][0, i * part_rows:(i + 1) * part_rows]]
    shards["small_pack"] = [_pack([w[n] for n in SMALL_SHARDED], F32)]
    small_shapes = [w[n].shape for n in SMALL_SHARDED]

    def unpack_small(gathered):
        out = {n: _join_shards(st, SHARDED[n]) for n, st in zip(SMALL_SHARDED, _unpack(gathered, small_shapes, lead=True))}
        for n in ("s5_w_glu", "cv_w_pw"):
            out[n] = out[n].astype(BF16)
        return out

    big_w = {n: [None] * DEPTH for n in BIG}
    p = {n: w[n] for n in REPLICATED}
    p["b_in"] = _perm_in_cols(p["b_in"])

    loss, grad_x, g_small, (recv, ready) = _local_step(x[0], mem[0], loss_target[0], p, big_w, shards, unpack_small)
    loss = lax.psum(loss, ("x", "y", "c"))

    g_small["b_in"] = _perm_in_cols(g_small["b_in"], inverse=True)
    left = list(ready)
    rider = _Rider([ready[k][0] for k in left] + [_pack([g_small[n] for n in REP_LAST], F32)],
                   [ready[k][1] for k in left] + ["all"])
    got = _exchange(rider, name="exchange_grads")
    for k, r in zip(left, got):
        recv[k] = r

    res = [dict(), dict(), dict(), dict()]
    for n in BIG:
        outs = None
        for l in range(DEPTH):
            outs = _adamw_layer(recv[(n, l)], w[n], mom[n], var[n], l, outs, name=f"adamw_{n}_l{l}")
        for kind in range(4):
            res[kind][n] = outs[kind]
    for names, key, tag in ((SMALL_SHARDED, "ssh", "adamw_small_sharded"), (REP_LAYERED, "rep", "adamw_replicated")):
        gstack = jnp.concatenate([recv[(key, l)] for l in range(DEPTH)], axis=1)
        packs = [_pack_layers([t[n] for n in names], F32) for t in (w, mom, var)]
        rows = packs[0].shape[1]
        outs = _adamw(gstack, *[pk.reshape(DEPTH * rows, PACK_COLS) for pk in packs], name=tag)
        for kind in range(4):
            for n, a in zip(names, _unpack_layers(outs[kind].reshape(DEPTH, rows, PACK_COLS), [w[n].shape for n in names])):
                res[kind][n] = a
    outs = _adamw(got[len(left)], _pack([w[n] for n in REP_LAST], F32), _pack([mom[n] for n in REP_LAST], F32),
                  _pack([var[n] for n in REP_LAST], F32), name="adamw_last")
    for kind in range(4):
        for n, a in zip(REP_LAST, _unpack(outs[kind], [w[n].shape for n in REP_LAST])):
            res[kind][n] = a
    return (loss, grad_x[None], *[res[0][n] for n in WEIGHTS], *[res[1][n] for n in WEIGHTS],
            *[res[2][n] for n in WEIGHTS], *[res[3][n] for n in WEIGHTS])
```

```python
import math

import jax
import jax.numpy as jnp
from jax import lax
from jax.experimental import pallas as pl
from jax.experimental.pallas import tpu as pltpu

F32 = jnp.float32
BF16 = jnp.bfloat16

D_MODEL = 1024
DEPTH = 2
D_GROUP = 256
N_IN_COLS = 6 * D_GROUP
S5_GROUPS = 16
S5_CH = 16
S5_STATE = 64
CONV_WIDTH = 31
GN_GROUPS = 4
LRU_HEADS = 4
LRU_CONV_WIDTH = 4
LRU_C = 8.0
ATTN_HEADS = 4
ATTN_HEAD_DIM = 64
D_FF = 2816
FFN_CONV_WIDTH = 3
ALPHA = (2 * DEPTH) ** 0.25
LN_EPS = 1e-5
ADAM_LR, ADAM_B1, ADAM_B2, ADAM_EPS, ADAM_WD, ADAM_STEP = 0.001, 0.9, 0.999, 1e-08, 0.01, 10

N_DEV = 8
N_PEERS = N_DEV - 1
LANE = 128
SUBLANE = 8
VMEM_LIMIT = 56 * 1024 * 1024
PACK_COLS = 1024
PACK_ROW_BLOCK = 256
MM_ROW_TILE = 1024
MM_COL_CAP = 1408
MM_K_CAP = 1536
SEQ_TILE = 512

SHARDED = {
    "w_in": 2, "s5_w_glu": 1, "cv_w": 2, "cv_w_pw": 1, "lru_conv_w": 2, "attn_w_kv": 1,
    "w_out": 1, "ffn_w_up": 2, "ffn_conv_w": 2, "ffn_w_down": 1,
}
BIG = ("w_in", "attn_w_kv", "w_out", "ffn_w_up", "ffn_w_down")
SMALL_SHARDED = ("s5_w_glu", "cv_w", "cv_w_pw", "lru_conv_w", "ffn_conv_w")
WEIGHTS = ['ln_in_g', 'ln_in_b', 'w_in', 'b_in', 's5_lam_re', 's5_lam_im', 's5_log_dt', 's5_b_re', 's5_b_im',
           's5_c_re', 's5_c_im', 's5_d', 's5_w_glu', 's5_b_glu', 'cv_w', 'cv_b', 'cv_gn_g', 'cv_gn_b', 'cv_w_pw',
           'cv_b_pw', 'lru_conv_w', 'lru_conv_b', 'lru_w_r', 'lru_b_r', 'lru_w_i', 'lru_b_i', 'lru_lam',
           'attn_w_kv', 'w_out', 'b_out', 'ln1_g', 'ln1_b', 'ffn_w_up', 'ffn_conv_w', 'ffn_conv_b', 'ffn_w_down',
           'ln2_g', 'ln2_b']
REPLICATED = [n for n in WEIGHTS if n not in SHARDED]
REP_LAST = ("ln_in_g", "ln_in_b", "b_in")
REP_LAYERED = [n for n in REPLICATED if n not in REP_LAST]

COL_CV_V, COL_CV_G, COL_LRU_G, COL_LRU_X, COL_S5, COL_Q = range(6)
IN_PERM = (1, 2, 3, 4, 0, 5)
MIX_S5, MIX_CV, MIX_LRU, MIX_ATTN = range(4)


_ANY = pl.BlockSpec(memory_space=pl.ANY)
_MESH = pl.DeviceIdType.MESH


def _cparams(n_axes):
    return pltpu.CompilerParams(dimension_semantics=("arbitrary",) * n_axes, vmem_limit_bytes=VMEM_LIMIT)


def _pick(n, cap):
    if n <= cap:
        return n
    best = None
    for t in range(LANE, cap + 1, LANE):
        if n % t == 0:
            best = t
    assert best is not None, (n, cap)
    return best


def _pick_rows(n, cap):
    best = None
    for t in range(SUBLANE, min(n, cap) + 1, SUBLANE):
        if n % t == 0:
            best = t
    assert best is not None, (n, cap)
    return best


def _full_spec(arr):
    nd = arr.ndim
    return pl.BlockSpec(arr.shape, lambda *_: (0,) * nd)


def _dot(a, b):
    return lax.dot_general(a.astype(BF16), b.astype(BF16), (((1,), (0,)), ((), ())), preferred_element_type=F32)


def _dot_nt(a, b):
    return lax.dot_general(a.astype(BF16), b.astype(BF16), (((1,), (1,)), ((), ())), preferred_element_type=F32)


def _dot_tn(a, b):
    return lax.dot_general(a.astype(BF16), b.astype(BF16), (((0,), (0,)), ((), ())), preferred_element_type=F32)


def _dot_hi(a, b):
    b = b.astype(BF16)
    a1 = a.astype(BF16)
    r1 = a - a1.astype(F32)
    a2 = r1.astype(BF16)
    a3 = (r1 - a2.astype(F32)).astype(BF16)
    return _dot(a1, b) + _dot(a2, b) + _dot(a3, b)


def _colsum(x):
    return jnp.sum(x, axis=0, keepdims=True)


def _sigmoid(x):
    return 1.0 / (1.0 + jnp.exp(-x))


_GELU_K = math.sqrt(2.0 / math.pi)
_GELU_C = 0.044715


def _gelu(x):
    t = jnp.tanh(_GELU_K * (x + _GELU_C * x * x * x))
    return 0.5 * x * (1.0 + t)


def _gelu_and_grad(x):
    x2 = x * x
    t = jnp.tanh(_GELU_K * (x + _GELU_C * x2 * x))
    g = 0.5 * x * (1.0 + t)
    dg = 0.5 * (1.0 + t) + 0.5 * x * (1.0 - t * t) * (_GELU_K * (1.0 + 3.0 * _GELU_C * x2))
    return g, dg


def _neg_expm1(x):
    series = x * (1.0 + x * (0.5 + x * (1.0 / 6.0 + x * (1.0 / 24.0 + x * (1.0 / 120.0)))))
    return -jnp.where(jnp.abs(x) < 0.1, series, jnp.exp(x) - 1.0)


def _seq_tile(s, want):
    t = min(s, want)
    assert s % t == 0
    return t


class _Rider:
    def __init__(self, srcs, kinds):
        self.srcs, self.kinds = list(srcs), list(kinds)
        self.n = len(self.srcs)

    def out_shapes(self):
        shapes = []
        for x, kind in zip(self.srcs, self.kinds):
            if kind == "lead":
                shp = x.shape
            elif kind == "rows":
                shp = (N_DEV, x.shape[0] // N_DEV) + x.shape[1:]
            else:
                shp = (N_DEV,) + x.shape
            shapes.append(jax.ShapeDtypeStruct(shp, x.dtype))
        return shapes

    def scratch(self):
        return [pltpu.SemaphoreType.DMA((self.n * N_PEERS,)), pltpu.SemaphoreType.DMA((self.n * N_PEERS,)),
                pltpu.SemaphoreType.DMA((self.n,))]

    def _copies(self, x_refs, out_refs, sems):
        send_sems, recv_sems, local_sems = sems
        mx, my, mc = lax.axis_index("x"), lax.axis_index("y"), lax.axis_index("c")
        my_id = 4 * mx + 2 * my + mc

        def piece(i, dev):
            if self.kinds[i] == "lead":
                return x_refs[i].at[dev]
            if self.kinds[i] == "rows":
                r = x_refs[i].shape[0] // N_DEV
                return x_refs[i].at[pl.ds(pl.multiple_of(dev * r, SUBLANE), r)]
            return x_refs[i]

        mine = [pltpu.make_async_copy(piece(i, my_id), out_refs[i].at[my_id], local_sems.at[i]) for i in range(self.n)]
        copies = []
        for k in range(1, N_DEV):
            px, py, pc = mx ^ ((k >> 2) & 1), my ^ ((k >> 1) & 1), mc ^ (k & 1)
            for i in range(self.n):
                copies.append(pltpu.make_async_remote_copy(
                    src_ref=piece(i, 4 * px + 2 * py + pc), dst_ref=out_refs[i].at[my_id],
                    send_sem=send_sems.at[i * N_PEERS + k - 1], recv_sem=recv_sems.at[i * N_PEERS + k - 1],
                    device_id=(px, py, pc), device_id_type=_MESH))
        return mine, copies

    def start(self, x_refs, out_refs, sems):
        mine, copies = self._copies(x_refs, out_refs, sems)
        for cp in mine + copies:
            cp.start()

    def wait(self, x_refs, out_refs, sems):
        mine, copies = self._copies(x_refs, out_refs, sems)
        for cp in copies:
            cp.wait_recv()
        for cp in copies:
            cp.wait_send()
        for cp in mine:
            cp.wait()


class _TwoLevelGather:
    def __init__(self, srcs):
        self.srcs = list(srcs)
        self.n = len(self.srcs)

    def out_shapes(self):
        return [jax.ShapeDtypeStruct((N_DEV,) + x.shape, x.dtype) for x in self.srcs]

    def scratch(self):
        return [pltpu.SemaphoreType.DMA((self.n * N_PEERS,)), pltpu.SemaphoreType.DMA((self.n * N_PEERS,)),
                pltpu.SemaphoreType.DMA((self.n,))]

    def _tools(self, x_refs, out_refs, sems):
        send_sems, recv_sems, local_sems = sems
        mx, my, mc = lax.axis_index("x"), lax.axis_index("y"), lax.axis_index("c")
        me, sibling = (mx, my, mc), (mx, my, 1 - mc)
        chips = [(1 - mx, my), (mx, 1 - my), (1 - mx, 1 - my)]

        def slot(i, px, py, pc):
            return out_refs[i].at[4 * px + 2 * py + pc]

        def copy(i, k, block, to, src=None):
            return pltpu.make_async_remote_copy(
                src_ref=slot(i, *block) if src is None else src, dst_ref=slot(i, *block),
                send_sem=send_sems.at[i * N_PEERS + k], recv_sem=recv_sems.at[i * N_PEERS + k],
                device_id=to, device_id_type=_MESH)

        mine = [pltpu.make_async_copy(x_refs[i], slot(i, *me), local_sems.at[i]) for i in range(self.n)]
        first = []
        for i in range(self.n):
            first.append(copy(i, 0, me, sibling, src=x_refs[i]))
            first += [copy(i, 1 + j, me, (*chip, mc), src=x_refs[i]) for j, chip in enumerate(chips)]
        return me, sibling, chips, copy, mine, first

    def start(self, x_refs, out_refs, sems):
        _, _, _, _, mine, first = self._tools(x_refs, out_refs, sems)
        for cp in mine + first:
            cp.start()

    def wait(self, x_refs, out_refs, sems):
        me, sibling, chips, copy, mine, first = self._tools(x_refs, out_refs, sems)
        mc = me[2]
        passed = []
        for j, chip in enumerate(chips):
            for i in range(self.n):
                copy(i, 1 + j, (*chip, mc), me).wait_recv()
                fwd = copy(i, 4 + j, (*chip, mc), sibling)
                fwd.start()
                passed.append(fwd)
        for i in range(self.n):
            copy(i, 0, sibling, me).wait_recv()
            for j, chip in enumerate(chips):
                copy(i, 4 + j, (*chip, 1 - mc), me).wait_recv()
        for cp in first + passed:
            cp.wait_send()
        for cp in mine:
            cp.wait()


def _call(body, *, grid, ins, in_specs, outs, out_specs, scratch=(), aliases=None, name, rider=None):
    n_axes = len(grid)
    common = dict(grid=grid, input_output_aliases=aliases or {}, compiler_params=_cparams(n_axes), name=name)
    if rider is None:
        res = pl.pallas_call(body, in_specs=list(in_specs), out_specs=list(out_specs), out_shape=list(outs),
                             scratch_shapes=list(scratch), **common)(*ins)
        return list(res), []
    n_in, n_out, n_scr, nr = len(ins), len(outs), len(scratch), rider.n

    def wrapped(*refs):
        pos = [0]

        def take(k):
            part = refs[pos[0]:pos[0] + k]
            pos[0] += k
            return part

        a_in, r_in, a_out, r_out, a_scr, sems = take(n_in), take(nr), take(n_out), take(nr), take(n_scr), take(3)
        first = last = None
        for ax in range(n_axes):
            pid = pl.program_id(ax)
            f, l = pid == 0, pid == grid[ax] - 1
            first = f if first is None else jnp.logical_and(first, f)
            last = l if last is None else jnp.logical_and(last, l)

        @pl.when(first)
        def _():
            rider.start(r_in, r_out, sems)

        body(*a_in, *a_out, *a_scr)

        @pl.when(last)
        def _():
            rider.wait(r_in, r_out, sems)

    res = pl.pallas_call(
        wrapped, in_specs=list(in_specs) + [_ANY] * nr, out_specs=list(out_specs) + [_ANY] * nr,
        out_shape=list(outs) + rider.out_shapes(), scratch_shapes=list(scratch) + rider.scratch(), **common)(*ins, *rider.srcs)
    return list(res[:n_out]), list(res[n_out:])


def _block_mask(n_blocks, block_rows, block_cols):
    r = jnp.arange(n_blocks * block_rows) // block_rows
    c = jnp.arange(n_blocks * block_cols) // block_cols
    return (r[:, None] == c[None, :]).astype(F32)


def _mm(a, b, *, bias=None, res=None, res_scale=1.0, trans_b=False, out_dtype=F32, ln=None, ln_bwd=None, loss=None,
        name, rider=None):
    m, kdim = a.shape
    n = b.shape[0] if trans_b else b.shape[1]
    tm = _seq_tile(m, MM_ROW_TILE)
    tn = _pick(n, MM_COL_CAP)
    tk = _pick(kdim, MM_K_CAP)
    nk = kdim // tk
    has_bias, has_res, has_ln, has_lnb = bias is not None, res is not None, ln is not None, ln_bwd is not None
    has_loss = loss is not None
    assert not (has_ln or has_lnb or has_loss) or tn == n
    assert has_ln + has_lnb + has_loss <= 1

    def body(*refs):
        a_ref, b_ref = refs[0], refs[1]
        pos = 2
        bias_ref = res_ref = g_ref = beta_ref = x_ref = None
        if has_bias:
            bias_ref = refs[pos]
            pos += 1
        if has_res:
            res_ref = refs[pos]
            pos += 1
        if has_ln:
            g_ref, beta_ref = refs[pos], refs[pos + 1]
            pos += 2
        if has_lnb:
            x_ref, g_ref = refs[pos], refs[pos + 1]
            pos += 2
        if has_loss:
            g_ref, beta_ref, t_ref = refs[pos:pos + 3]
            pos += 3
        o_ref = refs[pos]
        pos += 1
        if has_ln:
            x_ref = refs[pos]
            pos += 1
        if has_lnb or has_loss:
            dg_ref, db_ref, ds_ref = refs[pos:pos + 3]
            pos += 3
        acc_ref = refs[pos]
        k = pl.program_id(2)

        @pl.when(k == 0)
        def _():
            acc_ref[...] = jnp.zeros_like(acc_ref)

        if has_lnb or has_loss:
            @pl.when(jnp.logical_and(pl.program_id(0) == 0, k == 0))
            def _():
                dg_ref[...] = jnp.zeros_like(dg_ref)
                db_ref[...] = jnp.zeros_like(db_ref)
                ds_ref[...] = jnp.zeros_like(ds_ref)

        if trans_b:
            acc_ref[...] += _dot_nt(a_ref[...], b_ref[...])
        else:
            acc_ref[...] += _dot(a_ref[...], b_ref[...])

        @pl.when(k == nk - 1)
        def _():
            r = acc_ref[...]
            if has_bias:
                r = r + bias_ref[...]
            if has_res:
                r = r + res_scale * res_ref[...]
            if has_lnb:
                x = x_ref[...]
                xc = x - jnp.mean(x, axis=1, keepdims=True)
                rstd = lax.rsqrt(jnp.mean(xc * xc, axis=1, keepdims=True) + LN_EPS)
                xh = xc * rstd
                dxh = r * g_ref[...]
                dx = rstd * (dxh - jnp.mean(dxh, axis=1, keepdims=True) - xh * jnp.mean(dxh * xh, axis=1, keepdims=True))
                o_ref[...] = dx
                dg_ref[...] += _colsum(r * xh)
                db_ref[...] += _colsum(r)
                ds_ref[...] += _colsum(dx)
            elif has_loss:
                gam = g_ref[...]
                xc = r - jnp.mean(r, axis=1, keepdims=True)
                rstd = lax.rsqrt(jnp.mean(xc * xc, axis=1, keepdims=True) + LN_EPS)
                xh = xc * rstd
                e = xh * gam + beta_ref[...] - t_ref[...]
                part = jnp.sum(jnp.sum(e * e, axis=1, keepdims=True), axis=0, keepdims=True) * (0.5 / n)
                ds_ref[...] += jnp.broadcast_to(part, ds_ref.shape)
                dy = e * (1.0 / n)
                dxh = dy * gam
                o_ref[...] = rstd * (dxh - jnp.mean(dxh, axis=1, keepdims=True) - xh * jnp.mean(dxh * xh, axis=1, keepdims=True))
                dg_ref[...] += _colsum(dy * xh)
                db_ref[...] += _colsum(dy)
            else:
                o_ref[...] = r.astype(out_dtype)
            if has_ln:
                xc = r - jnp.mean(r, axis=1, keepdims=True)
                var = jnp.mean(xc * xc, axis=1, keepdims=True)
                x_ref[...] = xc * lax.rsqrt(var + LN_EPS) * g_ref[...] + beta_ref[...]

    ins = [a, b]
    in_specs = [pl.BlockSpec((tm, tk), lambda i, j, k: (i, k)),
                pl.BlockSpec((tn, tk), lambda i, j, k: (j, k)) if trans_b
                else pl.BlockSpec((tk, tn), lambda i, j, k: (k, j))]
    if has_bias:
        ins.append(bias)
        in_specs.append(pl.BlockSpec((1, tn), lambda i, j, k: (0, j)))
    if has_res:
        ins.append(res)
        in_specs.append(pl.BlockSpec((tm, tn), lambda i, j, k: (i, j)))
    tile = pl.BlockSpec((tm, tn), lambda i, j, k: (i, j))
    vec = pl.BlockSpec((1, tn), lambda i, j, k: (0, j))
    out_shapes, out_specs = [jax.ShapeDtypeStruct((m, n), out_dtype)], [tile]
    if has_ln:
        ins += list(ln)
        in_specs += [vec] * 2
        out_shapes.append(jax.ShapeDtypeStruct((m, n), F32))
        out_specs.append(tile)
    if has_lnb:
        ins += list(ln_bwd)
        in_specs += [tile, vec]
        out_shapes += [jax.ShapeDtypeStruct((1, n), F32)] * 3
        out_specs += [vec] * 3
    if has_loss:
        ins += list(loss)
        in_specs += [vec, vec, tile]
        out_shapes += [jax.ShapeDtypeStruct((1, n), F32)] * 2 + [jax.ShapeDtypeStruct((SUBLANE, LANE), F32)]
        out_specs += [vec, vec, pl.BlockSpec((SUBLANE, LANE), lambda i, j, k: (0, 0))]
    outs, routs = _call(
        body, grid=(m // tm, n // tn, nk), ins=ins, in_specs=in_specs, outs=out_shapes, out_specs=out_specs,
        scratch=[pltpu.VMEM((tm, tn), F32)], name=name, rider=rider)
    out = tuple(outs) if (has_ln or has_lnb or has_loss) else outs[0]
    return out if rider is None else (out, routs)


def _mm_tn(a, b, *, colsum=False, out_dtype=F32, dev_cols=None, name, rider=None):
    s, ka = a.shape
    nb = b.shape[1]
    ts = _seq_tile(s, SEQ_TILE)
    tka = _pick(ka, MM_COL_CAP)
    tnb = _pick(nb, MM_COL_CAP)
    nk = s // ts
    assert not colsum or tka == ka
    per_tile = 1 if dev_cols is None else tnb // dev_cols
    assert dev_cols is None or tnb == per_tile * dev_cols

    def body(a_ref, b_ref, o_ref, *rest):
        cs_ref = rest[0] if colsum else None
        acc_ref = rest[-1]
        k = pl.program_id(2)

        @pl.when(k == 0)
        def _():
            acc_ref[...] = jnp.zeros_like(acc_ref)
            if colsum:
                cs_ref[...] = jnp.zeros_like(cs_ref)

        bv = b_ref[...]
        acc_ref[...] += _dot_tn(a_ref[...], bv)
        if colsum:
            cs_ref[...] += _colsum(bv.astype(F32))

        @pl.when(k == nk - 1)
        def _():
            if dev_cols is None:
                o_ref[...] = acc_ref[...].astype(out_dtype)
            else:
                for d in range(per_tile):
                    o_ref[d] = acc_ref[:, d * dev_cols:(d + 1) * dev_cols].astype(out_dtype)

    if dev_cols is None:
        main_shape, main_spec = (ka, nb), pl.BlockSpec((tka, tnb), lambda i, j, k: (i, j))
    else:
        main_shape = (nb // dev_cols, ka, dev_cols)
        main_spec = pl.BlockSpec((per_tile, tka, dev_cols), lambda i, j, k: (j, i, 0))
    outs, routs = _call(
        body, grid=(ka // tka, nb // tnb, nk), ins=[a, b],
        in_specs=[pl.BlockSpec((ts, tka), lambda i, j, k: (k, i)), pl.BlockSpec((ts, tnb), lambda i, j, k: (k, j))],
        outs=[jax.ShapeDtypeStruct(main_shape, out_dtype)] + ([jax.ShapeDtypeStruct((1, nb), F32)] if colsum else []),
        out_specs=[main_spec] + ([pl.BlockSpec((1, tnb), lambda i, j, k: (0, j))] if colsum else []),
        scratch=[pltpu.VMEM((tka, tnb), F32)], name=name, rider=rider)
    out = tuple(outs) if colsum else outs[0]
    return out if rider is None else (out, routs)


def _ln_fwd(r, g, b, *, name, rider=None):
    s, d = r.shape
    ts = _seq_tile(s, SEQ_TILE)

    def body(r_ref, g_ref, b_ref, o_ref):
        x = r_ref[...]
        mu = jnp.mean(x, axis=1, keepdims=True)
        xc = x - mu
        var = jnp.mean(xc * xc, axis=1, keepdims=True)
        o_ref[...] = xc * lax.rsqrt(var + LN_EPS) * g_ref[...] + b_ref[...]

    (out,), routs = _call(
        body, grid=(s // ts,), ins=[r, g, b],
        in_specs=[pl.BlockSpec((ts, d), lambda i: (i, 0)), _full_spec(g), _full_spec(b)],
        out_specs=[pl.BlockSpec((ts, d), lambda i: (i, 0))], outs=[jax.ShapeDtypeStruct((s, d), F32)],
        name=name, rider=rider)
    return out if rider is None else (out, routs)


def _ln_bwd(r, dy, g, *, name, rider=None):
    s, d = r.shape
    ts = _seq_tile(s, SEQ_TILE)

    def body(r_ref, dy_ref, g_ref, dr_ref, dg_ref, db_ref, ds_ref):
        @pl.when(pl.program_id(0) == 0)
        def _():
            dg_ref[...] = jnp.zeros_like(dg_ref)
            db_ref[...] = jnp.zeros_like(db_ref)
            ds_ref[...] = jnp.zeros_like(ds_ref)

        x = r_ref[...]
        dy = dy_ref[...]
        mu = jnp.mean(x, axis=1, keepdims=True)
        xc = x - mu
        var = jnp.mean(xc * xc, axis=1, keepdims=True)
        rstd = lax.rsqrt(var + LN_EPS)
        xh = xc * rstd
        dxh = dy * g_ref[...]
        m1 = jnp.mean(dxh, axis=1, keepdims=True)
        m2 = jnp.mean(dxh * xh, axis=1, keepdims=True)
        dr = rstd * (dxh - m1 - xh * m2)
        dr_ref[...] = dr
        dg_ref[...] += _colsum(dy * xh)
        db_ref[...] += _colsum(dy)
        ds_ref[...] += _colsum(dr)

    vec = jax.ShapeDtypeStruct((1, d), F32)
    vspec = pl.BlockSpec((1, d), lambda i: (0, 0))
    outs, routs = _call(
        body, grid=(s // ts,), ins=[r, dy, g],
        in_specs=[pl.BlockSpec((ts, d), lambda i: (i, 0)), pl.BlockSpec((ts, d), lambda i: (i, 0)), _full_spec(g)],
        out_specs=[pl.BlockSpec((ts, d), lambda i: (i, 0)), vspec, vspec, vspec],
        outs=[jax.ShapeDtypeStruct((s, d), F32), vec, vec, vec], name=name, rider=rider)
    return outs if rider is None else (outs, routs)


SCAN_CHUNK = 32


def _cscan_levels(bufs, apow_ref, t, pad, *, reverse):
    half = bufs[0].shape[1] // 2
    ch = min(SCAN_CHUNK, t)
    nlev = t.bit_length() - 1
    assert (1 << nlev) == t
    for k in range(nlev):
        d = 1 << k
        src, dst = bufs[k % 2], bufs[(k + 1) % 2]

        def chunk(c, carry, src=src, dst=dst, d=d, k=k):
            ar = apow_ref[k:k + 1, :half]
            ai = apow_ref[k:k + 1, half:]
            if reverse:
                ai = -ai
            r0 = pl.multiple_of(c * ch, ch)
            cur = src[pl.ds(pad + r0, ch), :]
            if d >= SUBLANE:
                off = pad + d if reverse else pad - d
                sh = src[pl.ds(off + r0, ch), :]
            elif reverse:
                blk = src[pl.ds(pad + r0, ch + SUBLANE), :]
                sh = pltpu.roll(blk, ch + SUBLANE - d, axis=0)[:ch, :]
            else:
                blk = src[pl.ds(pad - SUBLANE + r0, ch + SUBLANE), :]
                sh = pltpu.roll(blk, d, axis=0)[SUBLANE:, :]
            sre, sim = sh[:, :half], sh[:, half:]
            dst[pl.ds(pad + r0, ch), :half] = cur[:, :half] + ar * sre - ai * sim
            dst[pl.ds(pad + r0, ch), half:] = cur[:, half:] + ar * sim + ai * sre
            return carry

        lax.fori_loop(0, t // ch, chunk, 0)
    return nlev % 2


def _rscan_levels(abufs, bbufs, t, pad, *, reverse):
    nlev = t.bit_length() - 1
    assert (1 << nlev) == t
    for k in range(nlev):
        d = 1 << k
        asrc, adst = abufs[k % 2], abufs[(k + 1) % 2]
        bsrc, bdst = bbufs[k % 2], bbufs[(k + 1) % 2]
        off = pad + d if reverse else pad - d
        a = asrc[pad:pad + t, :]
        bdst[pad:pad + t, :] = a * bsrc[off:off + t, :] + bsrc[pad:pad + t, :]
        if k < nlev - 1:
            adst[pad:pad + t, :] = a * asrc[off:off + t, :]
    return nlev % 2


S5_CHUNK = 16
S5_SG = S5_GROUPS // 2
S5_SG_IN = 2 * S5_CHUNK * S5_CH
S5_SG_ST = 2 * S5_STATE


S5_HALF_SGS = S5_SG // 2
S5_HALF_IN = S5_HALF_SGS * S5_SG_IN


def _s5_perm():
    idx = jnp.arange(S5_HALF_IN)
    step, grp, chan = idx // LANE, (idx % LANE) // S5_CH, idx % S5_CH
    col = (grp // 2) * S5_SG_IN + (grp % 2) * (S5_CHUNK * S5_CH) + step * S5_CH + chan
    return (col[:, None] == idx[None, :]).astype(BF16)


def _s5_to_chunks(x, col_block, perm, *, name):
    s = x.shape[0]
    nb = s // S5_CHUNK

    def body(x_ref, perm_ref, o_ref):
        tok = jnp.concatenate([x_ref[pl.ds(t, nb, stride=S5_CHUNK), :].astype(BF16) for t in range(S5_CHUNK)], axis=1)
        grouped = _dot(tok, perm_ref[...]).astype(BF16)
        for k in range(S5_HALF_SGS):
            o_ref[k] = grouped[:, k * S5_SG_IN:(k + 1) * S5_SG_IN]

    return pl.pallas_call(
        body, grid=(2,),
        in_specs=[pl.BlockSpec((s, LANE), lambda h: (0, col_block + h)), _full_spec(perm)],
        out_specs=pl.BlockSpec((S5_HALF_SGS, nb, S5_SG_IN), lambda h: (h, 0, 0)),
        out_shape=jax.ShapeDtypeStruct((S5_SG, nb, S5_SG_IN), BF16),
        compiler_params=_cparams(1), name=name)(x, perm)


def _s5_from_chunks(y, perm, *, name):
    _, nb, _ = y.shape

    def body(y_ref, perm_ref, o_ref):
        grouped = jnp.concatenate([y_ref[k] for k in range(S5_HALF_SGS)], axis=1)
        hi = grouped.astype(BF16)
        lo = (grouped - hi.astype(F32)).astype(BF16)
        tok = _dot_nt(hi, perm_ref[...]) + _dot_nt(lo, perm_ref[...])
        for t in range(S5_CHUNK):
            o_ref[pl.ds(t, nb, stride=S5_CHUNK), :] = tok[:, t * LANE:(t + 1) * LANE]

    return pl.pallas_call(
        body, grid=(2,),
        in_specs=[pl.BlockSpec((S5_HALF_SGS, nb, S5_SG_IN), lambda h: (h, 0, 0)), _full_spec(perm)],
        out_specs=pl.BlockSpec((nb * S5_CHUNK, LANE), lambda h: (0, h)),
        out_shape=jax.ShapeDtypeStruct((nb * S5_CHUNK, D_GROUP), F32),
        compiler_params=_cparams(1), name=name)(y, perm)


def _s5_core_fwd(u2, m2, pre, pim, qre, qim, a16, *, name):
    sg, nb, nin = u2.shape
    st2 = 2 * S5_SG_ST
    pad = nb // 2

    def body(u_ref, m_ref, pre_ref, pim_ref, qre_ref, qim_ref, a_ref, y_ref, x_ref, buf0, buf1):
        @pl.when(pl.program_id(0) == 0)
        def _():
            buf0[0:pad, :] = jnp.zeros((pad, st2), F32)
            buf1[0:pad, :] = jnp.zeros((pad, st2), F32)

        u = u_ref[...]
        buf0[pad:pad + nb, :S5_SG_ST] = _dot(u, pre_ref[...])
        buf0[pad:pad + nb, S5_SG_ST:] = _dot(u, pim_ref[...])
        xbuf = (buf0, buf1)[_cscan_levels((buf0, buf1), a_ref, nb, pad, reverse=False)]
        x_ref[...] = xbuf[pad:pad + nb, :]
        xprev = xbuf[pad - 1:pad - 1 + nb, :]
        y_ref[...] = _dot(u, m_ref[...]) + _dot(xprev[:, :S5_SG_ST], qre_ref[...]) + _dot(xprev[:, S5_SG_ST:], qim_ref[...])

    ins = [u2, m2, pre, pim, qre, qim, a16]
    return pl.pallas_call(
        body, grid=(sg,), in_specs=[pl.BlockSpec((None,) + a.shape[1:], lambda i: (i, 0, 0)) for a in ins],
        out_specs=[pl.BlockSpec((None, nb, nin), lambda i: (i, 0, 0)), pl.BlockSpec((None, nb, st2), lambda i: (i, 0, 0))],
        out_shape=[jax.ShapeDtypeStruct((sg, nb, nin), F32), jax.ShapeDtypeStruct((sg, nb, st2), F32)],
        scratch_shapes=[pltpu.VMEM((pad + nb, st2), F32), pltpu.VMEM((pad + nb, st2), F32)],
        compiler_params=_cparams(1), name=name)(*ins)


def _s5_core_bwd(u2, dy2, x_all, m2, pre, pim, qre, qim, a16, *, name):
    sg, nb, nin = u2.shape
    half = S5_SG_ST
    st2 = 2 * half
    pad = nb // 2

    def body(u_ref, dy_ref, x_ref, m_ref, pre_ref, pim_ref, qre_ref, qim_ref, a_ref,
             du_ref, dm_ref, dpre_ref, dpim_ref, dqre_ref, dqim_ref, da_ref, buf2, buf3, xp):
        @pl.when(pl.program_id(0) == 0)
        def _():
            buf2[nb:nb + pad, :] = jnp.zeros((pad, st2), F32)
            buf3[nb:nb + pad, :] = jnp.zeros((pad, st2), F32)
            xp[0:SUBLANE, :] = jnp.zeros((SUBLANE, st2), F32)

        u = u_ref[...]
        dy = dy_ref[...]
        dm_ref[...] = _dot_tn(u, dy)
        xp[SUBLANE:SUBLANE + nb, :] = x_ref[...]
        xprev = xp[SUBLANE - 1:SUBLANE - 1 + nb, :]
        xre, xim = xprev[:, :half], xprev[:, half:]
        dqre_ref[...] = _dot_tn(xre, dy)
        dqim_ref[...] = _dot_tn(xim, dy)
        buf2[0:nb, :half] = _dot_nt(dy, qre_ref[...])
        buf2[0:nb, half:] = _dot_nt(dy, qim_ref[...])
        mbuf = (buf2, buf3)[_cscan_levels((buf2, buf3), a_ref, nb, 0, reverse=True)]
        lam = mbuf[1:1 + nb, :]
        lre, lim = lam[:, :half], lam[:, half:]
        dpre_ref[...] = _dot_tn(u, lre)
        dpim_ref[...] = _dot_tn(u, lim)
        du_ref[...] = _dot_nt(dy, m_ref[...]) + _dot_nt(lre, pre_ref[...]) + _dot_nt(lim, pim_ref[...])
        da_ref[:, :half] = _colsum(lre * xre + lim * xim)
        da_ref[:, half:] = _colsum(lim * xre - lre * xim)

    ins = [u2, dy2, x_all, m2, pre, pim, qre, qim, a16]
    outs = [jax.ShapeDtypeStruct((sg, nb, nin), F32)] + [jax.ShapeDtypeStruct(a.shape, F32) for a in (m2, pre, pim, qre, qim)] + \
           [jax.ShapeDtypeStruct((sg, 1, st2), F32)]
    return pl.pallas_call(
        body, grid=(sg,), in_specs=[pl.BlockSpec((None,) + a.shape[1:], lambda i: (i, 0, 0)) for a in ins],
        out_specs=[pl.BlockSpec((None,) + o.shape[1:], lambda i: (i, 0, 0)) for o in outs], out_shape=outs,
        scratch_shapes=[pltpu.VMEM((nb + pad, st2), F32), pltpu.VMEM((nb + pad, st2), F32),
                        pltpu.VMEM((SUBLANE + nb, st2), F32)],
        compiler_params=_cparams(1), name=name)(*ins)


def _s5_glu_fwd(y1, wglu, bglu, *, name, rider=None):
    s = y1.shape[0]
    t = _seq_tile(s, SEQ_TILE)

    def body(y1_ref, wglu_ref, bglu_ref, out_ref):
        y2 = _gelu(y1_ref[...])
        out_ref[...] = (y2 * _sigmoid(_dot(y2, wglu_ref[...]) + bglu_ref[...])).astype(BF16)

    return _call(
        body, grid=(s // t,), ins=[y1, wglu, bglu],
        in_specs=[pl.BlockSpec((t, D_GROUP), lambda i: (i, 0)), _full_spec(wglu), _full_spec(bglu)],
        out_specs=[pl.BlockSpec((t, D_GROUP), lambda i: (i, MIX_S5))], outs=[jax.ShapeDtypeStruct((s, D_MODEL), BF16)],
        name=name, rider=rider)


def _s5_glu_bwd(y1, dmix, wglu, bglu, *, name):
    s = y1.shape[0]
    t = _seq_tile(s, SEQ_TILE)

    def body(y1_ref, do_ref, wglu_ref, bglu_ref, dy1_ref, dwglu_ref, dbglu_ref):
        @pl.when(pl.program_id(0) == 0)
        def _():
            dwglu_ref[...] = jnp.zeros_like(dwglu_ref)
            dbglu_ref[...] = jnp.zeros_like(dbglu_ref)

        dout = do_ref[...]
        y2, dgelu = _gelu_and_grad(y1_ref[...])
        sg = _sigmoid(_dot(y2, wglu_ref[...]) + bglu_ref[...])
        dz = dout * y2 * sg * (1.0 - sg)
        dwglu_ref[...] += _dot_tn(y2, dz)
        dbglu_ref[...] += _colsum(dz)
        dy1_ref[...] = (dout * sg + _dot_nt(dz, wglu_ref[...])) * dgelu

    outs = [jax.ShapeDtypeStruct((s, D_GROUP), F32), jax.ShapeDtypeStruct((D_GROUP, D_GROUP), F32),
            jax.ShapeDtypeStruct((1, D_GROUP), F32)]
    return pl.pallas_call(
        body, grid=(s // t,),
        in_specs=[pl.BlockSpec((t, D_GROUP), lambda i: (i, 0)), pl.BlockSpec((t, D_GROUP), lambda i: (i, MIX_S5)),
                  _full_spec(wglu), _full_spec(bglu)],
        out_specs=[pl.BlockSpec((t, D_GROUP), lambda i: (i, 0)), _full_spec(outs[1]), _full_spec(outs[2])],
        out_shape=outs, compiler_params=_cparams(1), name=name)(y1, dmix, wglu, bglu)


def _pair_blockdiag(x):
    g, r, c = x.shape
    x = x.reshape(g // 2, 2, r, c)
    z = jnp.zeros_like(x[:, 0])
    return jnp.concatenate([jnp.concatenate([x[:, 0], z], axis=2), jnp.concatenate([z, x[:, 1]], axis=2)], axis=1)


def _s5_chunk_map(lam_re, lam_im, log_dt, b_re, b_im, c_re, c_im, d_skip):
    g, n, c, lc = S5_GROUPS, S5_STATE, S5_CH, S5_CHUNK
    dt = jnp.exp(log_dt)[:, None]
    mag, ang = lam_re * dt, lam_im * dt
    j = jnp.arange(lc + 1, dtype=F32)[:, None, None]
    pw_mag = jnp.exp(j * mag)
    pw_re, pw_im = pw_mag * jnp.cos(j * ang), pw_mag * jnp.sin(j * ang)
    a_re, a_im = pw_re[1], pw_im[1]
    den = lam_re * lam_re + lam_im * lam_im
    n_re = a_re - 1.0
    k_re = (n_re * lam_re + a_im * lam_im) / den
    k_im = (a_im * lam_re - n_re * lam_im) / den
    bb_re = k_re[..., None] * b_re - k_im[..., None] * b_im
    bb_im = k_re[..., None] * b_im + k_im[..., None] * b_re
    e_re = pw_re[:lc, :, :, None] * bb_re - pw_im[:lc, :, :, None] * bb_im
    e_im = pw_re[:lc, :, :, None] * bb_im + pw_im[:lc, :, :, None] * bb_re
    kern = jnp.einsum("gdn,jgnc->jgdc", c_re, e_re) - jnp.einsum("gdn,jgnc->jgdc", c_im, e_im)
    lags = jnp.pad(jnp.transpose(kern, (1, 3, 0, 2)), ((0, 0), (0, 0), (lc - 1, 0), (0, 0)))
    lags = lags.reshape(g, c, (2 * lc - 1) * c)
    m = jnp.stack([lags[:, :, (lc - 1 - s) * c:(2 * lc - 1 - s) * c] for s in range(lc)], axis=1).reshape(g, lc * c, lc * c)
    skip = jnp.tile(d_skip.reshape(g, 1, c), (1, lc, 1)).reshape(g, lc * c)
    m = m + jnp.eye(lc * c, dtype=F32)[None] * skip[:, None, :]
    p_re = jnp.transpose(e_re[::-1], (1, 0, 3, 2)).reshape(g, lc * c, n)
    p_im = jnp.transpose(e_im[::-1], (1, 0, 3, 2)).reshape(g, lc * c, n)
    f_re = c_re[None] * pw_re[1:, :, None, :] - c_im[None] * pw_im[1:, :, None, :]
    f_im = c_re[None] * pw_im[1:, :, None, :] + c_im[None] * pw_re[1:, :, None, :]
    q_re = jnp.transpose(f_re, (1, 3, 0, 2)).reshape(g, n, lc * c)
    q_im = -jnp.transpose(f_im, (1, 3, 0, 2)).reshape(g, n, lc * c)
    a16 = jnp.concatenate([pw_re[lc].reshape(S5_SG, 1, S5_SG_ST), pw_im[lc].reshape(S5_SG, 1, S5_SG_ST)], axis=2)
    return (_pair_blockdiag(m), _pair_blockdiag(p_re), _pair_blockdiag(p_im), _pair_blockdiag(q_re),
            _pair_blockdiag(q_im), a16)


def _s5_a16_powers(a16, nlev):
    half = S5_SG_ST
    re, im = a16[:, :, :half], a16[:, :, half:]
    rows = []
    for _ in range(nlev):
        rows.append(jnp.concatenate([re, im], axis=2))
        re, im = re * re - im * im, 2.0 * re * im
    n_rows = -(-nlev // SUBLANE) * SUBLANE
    rows += [jnp.zeros_like(rows[0])] * (n_rows - nlev)
    return lax.stop_gradient(jnp.concatenate(rows, axis=1))


CV_TILE = 256
CV_PAD = 32
CV_CHUNK = 64


def _shifted_copies(buf, shifted, rows):
    n = rows - SUBLANE
    for s in range(1, SUBLANE):
        shifted[s - 1, 0:n, :] = buf[s:s + n, :]


def _window(buf, shifted, o, ch):
    q, s = divmod(o, SUBLANE)
    if s == 0:
        return buf[o:o + ch, :]
    return shifted[s - 1, q * SUBLANE:q * SUBLANE + ch, :]


def _gn_stats(c, mavg):
    mu = _dot_hi(c, mavg)
    cen = c - mu
    var = _dot_hi(cen * cen, mavg)
    rstd = lax.rsqrt(var + LN_EPS)
    return cen * rstd, rstd


def _cv_fwd(h_in, cw, cb, gng, gnb, mavg, wpw, bpw, mix, *, name, rider=None):
    s = h_in.shape[0]
    t = _seq_tile(s, CV_TILE)
    ch = min(CV_CHUNK, t)

    def body(v_ref, g_ref, cw_ref, cb_ref, gng_ref, gnb_ref, mavg_ref, wpw_ref, bpw_ref, _mix_in, out_ref, c_ref, xpad,
             shifted):
        @pl.when(pl.program_id(0) == 0)
        def _():
            xpad[0:CV_PAD, :] = jnp.zeros((CV_PAD, D_GROUP), F32)

        xpad[CV_PAD:CV_PAD + t, :] = v_ref[...] * _sigmoid(g_ref[...])
        _shifted_copies(xpad, shifted, t + CV_PAD)
        for r0 in range(0, t, ch):
            acc = jnp.broadcast_to(cb_ref[...], (ch, D_GROUP))
            for k in range(CONV_WIDTH):
                o = CV_PAD - (CONV_WIDTH - 1) + k + r0
                acc = acc + cw_ref[k:k + 1, :] * _window(xpad, shifted, o, ch)
            c_ref[r0:r0 + ch, :] = acc
        xpad[0:CV_PAD, :] = xpad[t:t + CV_PAD, :]
        xn, _ = _gn_stats(c_ref[...], mavg_ref[...])
        gn = xn * gng_ref[...] + gnb_ref[...]
        out_ref[...] = (_dot(gn * _sigmoid(gn), wpw_ref[...]) + bpw_ref[...]).astype(BF16)

    ins = [h_in, h_in, cw, cb, gng, gnb, mavg, wpw, bpw, mix]
    in_specs = [pl.BlockSpec((t, D_GROUP), lambda i: (i, COL_CV_V)), pl.BlockSpec((t, D_GROUP), lambda i: (i, COL_CV_G))] + \
               [_full_spec(a) for a in ins[2:9]] + [_ANY]
    return _call(
        body, grid=(s // t,), ins=ins, in_specs=in_specs,
        out_specs=[pl.BlockSpec((t, D_GROUP), lambda i: (i, MIX_CV)), pl.BlockSpec((t, D_GROUP), lambda i: (i, 0))],
        outs=[jax.ShapeDtypeStruct((s, D_MODEL), BF16), jax.ShapeDtypeStruct((s, D_GROUP), F32)],
        aliases={9: 0},
        scratch=[pltpu.VMEM((CV_PAD + t, D_GROUP), F32), pltpu.VMEM((SUBLANE - 1, CV_PAD + t, D_GROUP), F32)],
        name=name, rider=rider)


def _cv_bwd(h_in, c, dmix, cw, gng, gnb, mavg, wpw, *, name, rider=None):
    s = h_in.shape[0]
    t = _seq_tile(s, CV_TILE)
    nt = s // t
    ch = min(CV_CHUNK, t)

    def body(v_ref, g_ref, c_ref, do_ref, cw_ref, gng_ref, gnb_ref, mavg_ref, wpw_ref,
             dvg_ref, dwpw_ref, dcw_ref, dbpw_ref, dgg_ref, dgb_ref, dcb_ref, dcpad, hgbuf, shifted):
        @pl.when(pl.program_id(0) == 0)
        def _():
            dcpad[t:t + CV_PAD, :] = jnp.zeros((CV_PAD, D_GROUP), F32)
            for r in (dwpw_ref, dcw_ref, dbpw_ref, dgg_ref, dgb_ref, dcb_ref):
                r[...] = jnp.zeros_like(r)

        mavg = mavg_ref[...]
        xn, rstd = _gn_stats(c_ref[...], mavg)
        gg = gng_ref[...]
        gn = xn * gg + gnb_ref[...]
        sg = _sigmoid(gn)
        dout = do_ref[...]
        dwpw_ref[...] += _dot_tn(gn * sg, dout)
        dbpw_ref[...] += _colsum(dout)
        dgn = _dot_nt(dout, wpw_ref[...]) * (sg * (1.0 + gn * (1.0 - sg)))
        dgg_ref[...] += _colsum(dgn * xn)
        dgb_ref[...] += _colsum(dgn)
        dxn = dgn * gg
        dc = rstd * (dxn - _dot_hi(dxn, mavg) - xn * _dot_hi(dxn * xn, mavg))
        dcb_ref[...] += _colsum(dc)
        dcpad[0:t, :] = dc

        v = v_ref[...]
        sgm = _sigmoid(g_ref[...])
        hgbuf[...] = v * sgm
        _shifted_copies(dcpad, shifted, t + CV_PAD)
        for r0 in range(0, t, ch):
            hg = hgbuf[r0:r0 + ch, :]
            acc = jnp.zeros((ch, D_GROUP), F32)
            for k in range(CONV_WIDTH):
                o = (CONV_WIDTH - 1) - k + r0
                sh = _window(dcpad, shifted, o, ch)
                acc = acc + cw_ref[k:k + 1, :] * sh
                dcw_ref[k:k + 1, :] += _colsum(hg * sh)
            hgbuf[r0:r0 + ch, :] = acc
        dcpad[t:t + CV_PAD, :] = dcpad[0:CV_PAD, :]
        dhg = hgbuf[...]
        dvg_ref[:, :D_GROUP] = dhg * sgm
        dvg_ref[:, D_GROUP:] = dhg * v * sgm * (1.0 - sgm)

    def rev(col):
        return lambda i: (nt - 1 - i, col)

    ins = [h_in, h_in, c, dmix, cw, gng, gnb, mavg, wpw]
    in_specs = [pl.BlockSpec((t, D_GROUP), rev(COL_CV_V)), pl.BlockSpec((t, D_GROUP), rev(COL_CV_G)),
                pl.BlockSpec((t, D_GROUP), rev(0)), pl.BlockSpec((t, D_GROUP), rev(MIX_CV))] + [_full_spec(a) for a in ins[4:]]
    vec = jax.ShapeDtypeStruct((1, D_GROUP), F32)
    outs = [jax.ShapeDtypeStruct((s, N_IN_COLS), F32),
            jax.ShapeDtypeStruct((D_GROUP, D_GROUP), F32), jax.ShapeDtypeStruct((CV_PAD, D_GROUP), F32), vec, vec, vec, vec]
    out_specs = [pl.BlockSpec((t, 2 * D_GROUP), rev(COL_CV_V // 2))] + [_full_spec(o) for o in outs[1:]]
    return _call(
        body, grid=(nt,), ins=ins, in_specs=in_specs, out_specs=out_specs, outs=outs,
        scratch=[pltpu.VMEM((t + CV_PAD, D_GROUP), F32), pltpu.VMEM((t, D_GROUP), F32),
                 pltpu.VMEM((SUBLANE - 1, t + CV_PAD, D_GROUP), F32)], name=name, rider=rider)


LRU_TILE = 256


def _lru_gates(xc, wr_ref, br_ref, wi_ref, bi_ref, sp_ref):
    r = _sigmoid(_dot(xc, wr_ref[...]) + br_ref[...])
    i = _sigmoid(_dot(xc, wi_ref[...]) + bi_ref[...])
    log_a = -LRU_C * r * sp_ref[...]
    a = jnp.exp(log_a)
    m = jnp.sqrt(_neg_expm1(2.0 * log_a))
    return r, i, a, m


def _lru_fwd(h_in, lcw, lcb, wr, br, wi, bi, sp, mix, *, name, rider=None):
    s = h_in.shape[0]
    t = _seq_tile(s, LRU_TILE)
    pad = max(t // 2, SUBLANE)

    def body(xg_ref, xr_ref, lcw_ref, lcb_ref, wr_ref, br_ref, wi_ref, bi_ref, sp_ref, _mix_in,
             out_ref, xc_ref, h_ref, xpad, a0, a1, b0, b1, carry):
        @pl.when(pl.program_id(0) == 0)
        def _():
            xpad[0:SUBLANE, :] = jnp.zeros((SUBLANE, D_GROUP), F32)
            for bf in (a0, a1, b0, b1):
                bf[0:pad, :] = jnp.zeros((pad, D_GROUP), F32)
            carry[...] = jnp.zeros_like(carry)

        xpad[SUBLANE:SUBLANE + t, :] = xr_ref[...]
        xc = jnp.broadcast_to(lcb_ref[...], (t, D_GROUP))
        for k in range(LRU_CONV_WIDTH):
            o = SUBLANE - (LRU_CONV_WIDTH - 1) + k
            xc = xc + lcw_ref[k:k + 1, :] * xpad[o:o + t, :]
        xpad[0:SUBLANE, :] = xpad[t:t + SUBLANE, :]
        xc_ref[...] = xc
        _, i, a, m = _lru_gates(xc, wr_ref, br_ref, wi_ref, bi_ref, sp_ref)
        a0[pad:pad + t, :] = a
        b0[pad:pad + t, :] = m * (i * xc)
        b0[pad:pad + 1, :] += a0[pad:pad + 1, :] * carry[0:1, :]
        fin = _rscan_levels((a0, a1), (b0, b1), t, pad, reverse=False)
        hbuf = (b0, b1)[fin]
        carry[0:1, :] = hbuf[pad + t - 1:pad + t, :]
        h = hbuf[pad:pad + t, :]
        h_ref[...] = h
        out_ref[...] = (h * _gelu(xg_ref[...])).astype(BF16)

    ins = [h_in, h_in, lcw, lcb, wr, br, wi, bi, sp, mix]
    row = pl.BlockSpec((t, D_GROUP), lambda i: (i, 0))
    in_specs = [pl.BlockSpec((t, D_GROUP), lambda i: (i, COL_LRU_G)), pl.BlockSpec((t, D_GROUP), lambda i: (i, COL_LRU_X))] + \
               [_full_spec(a) for a in ins[2:9]] + [_ANY]
    return _call(
        body, grid=(s // t,), ins=ins, in_specs=in_specs,
        out_specs=[pl.BlockSpec((t, D_GROUP), lambda i: (i, MIX_LRU)), row, row],
        outs=[jax.ShapeDtypeStruct((s, D_MODEL), BF16)] + [jax.ShapeDtypeStruct((s, D_GROUP), F32)] * 2,
        aliases={9: 0},
        scratch=[pltpu.VMEM((SUBLANE + t, D_GROUP), F32)] + [pltpu.VMEM((pad + t, D_GROUP), F32)] * 4 +
                [pltpu.VMEM((SUBLANE, D_GROUP), F32)],
        name=name, rider=rider)


def _lru_bwd(h_in, xc_all, h_all, dmix, lcw, wr, br, wi, bi, sp, dh_all, *, name):
    s = h_in.shape[0]
    t = _seq_tile(s, LRU_TILE)
    nt = s // t
    pad = max(t // 2, SUBLANE)
    tb = t // SUBLANE

    def body(xg_ref, xr_ref, xc_ref, h_ref, hprev_ref, do_ref, lcw_ref, wr_ref, br_ref, wi_ref, bi_ref, sp_ref, _dh_in,
             dgr_ref, dwr_ref, dwi_ref, dlcw_ref, dbr_ref, dbi_ref, dsp_ref, dlcb_ref,
             a0, a1, b0, b1, hp, dxpad, carry):
        pid = pl.program_id(0)

        @pl.when(pid == 0)
        def _():
            for bf in (a0, a1, b0, b1):
                bf[pad + t:pad + t + pad, :] = jnp.zeros((pad, D_GROUP), F32)
            dxpad[t:t + SUBLANE, :] = jnp.zeros((SUBLANE, D_GROUP), F32)
            carry[...] = jnp.zeros_like(carry)
            for r in (dwr_ref, dwi_ref, dlcw_ref, dbr_ref, dbi_ref, dsp_ref, dlcb_ref):
                r[...] = jnp.zeros_like(r)

        xc = xc_ref[...]
        h = h_ref[...]
        dout = do_ref[...]
        gate, dgate = _gelu_and_grad(xg_ref[...])
        dgr_ref[:, :D_GROUP] = dout * h * dgate
        r, i, a, m = _lru_gates(xc, wr_ref, br_ref, wi_ref, bi_ref, sp_ref)

        a0[pad:pad + t, :] = a
        b0[pad:pad + t, :] = dout * gate
        b0[pad + t - 1:pad + t, :] += carry[0:1, :]
        a1[pad:pad + t, :] = a0[pad + 1:pad + 1 + t, :]
        fin = _rscan_levels((a1, a0), (b0, b1), t, pad, reverse=True)
        lam = (b0, b1)[fin][pad:pad + t, :]
        carry[0:1, :] = a[0:1, :] * lam[0:1, :]

        is_first = pid == nt - 1
        hp[0:SUBLANE, :] = jnp.where(is_first, 0.0, hprev_ref[...])
        hp[SUBLANE:SUBLANE + t, :] = h
        hprev = hp[SUBLANE - 1:SUBLANE - 1 + t, :]

        ix = i * xc
        dmm = lam * ix
        dix = lam * m
        da = lam * hprev - dmm * (a / m)
        dlog_a = da * a
        dr = dlog_a * (-LRU_C * sp_ref[...])
        dsp_ref[...] += _colsum(dlog_a * (-LRU_C * r))
        dpr = dr * r * (1.0 - r)
        dpi = dix * xc * i * (1.0 - i)
        dbr_ref[...] += _colsum(dpr)
        dbi_ref[...] += _colsum(dpi)
        dwr_ref[...] += _dot_tn(xc, dpr)
        dwi_ref[...] += _dot_tn(xc, dpi)
        dxc = dix * i + _dot_nt(dpr, wr_ref[...]) + _dot_nt(dpi, wi_ref[...])
        dlcb_ref[...] += _colsum(dxc)

        dxpad[0:t, :] = dxc
        xr = xr_ref[...]
        dxr = jnp.zeros((t, D_GROUP), F32)
        for k in range(LRU_CONV_WIDTH):
            o = (LRU_CONV_WIDTH - 1) - k
            sh = dxpad[o:o + t, :]
            dxr = dxr + lcw_ref[k:k + 1, :] * sh
            dlcw_ref[k:k + 1, :] += _colsum(xr * sh)
        dxpad[t:t + SUBLANE, :] = dxpad[0:SUBLANE, :]
        dgr_ref[:, D_GROUP:] = dxr

    def rev(col):
        return lambda i: (nt - 1 - i, col)

    ins = [h_in, h_in, xc_all, h_all, h_all, dmix, lcw, wr, br, wi, bi, sp, dh_all]
    in_specs = [pl.BlockSpec((t, D_GROUP), rev(COL_LRU_G)), pl.BlockSpec((t, D_GROUP), rev(COL_LRU_X)),
                pl.BlockSpec((t, D_GROUP), rev(0)), pl.BlockSpec((t, D_GROUP), rev(0)),
                pl.BlockSpec((SUBLANE, D_GROUP), lambda i: (jnp.maximum((nt - 1 - i) * tb - 1, 0), 0)),
                pl.BlockSpec((t, D_GROUP), rev(MIX_LRU))] + [_full_spec(a) for a in ins[6:12]] + [_ANY]
    vec = jax.ShapeDtypeStruct((1, D_GROUP), F32)
    mat = jax.ShapeDtypeStruct((D_GROUP, D_GROUP), F32)
    outs = [jax.ShapeDtypeStruct((s, N_IN_COLS), F32), mat, mat, jax.ShapeDtypeStruct((SUBLANE, D_GROUP), F32),
            vec, vec, vec, vec]
    out_specs = [pl.BlockSpec((t, 2 * D_GROUP), rev(COL_LRU_G // 2))] + [_full_spec(o) for o in outs[1:]]
    return pl.pallas_call(
        body, grid=(nt,), in_specs=in_specs, out_specs=out_specs, out_shape=outs, input_output_aliases={12: 0},
        scratch_shapes=[pltpu.VMEM((pad + t + pad, D_GROUP), F32)] * 4 +
                       [pltpu.VMEM((SUBLANE + t, D_GROUP), F32), pltpu.VMEM((t + SUBLANE, D_GROUP), F32),
                        pltpu.VMEM((SUBLANE, D_GROUP), F32)],
        compiler_params=_cparams(1), name=name)(*ins)


def _blockdiag(w):
    h, d, _ = w.shape
    return jnp.tile(w.reshape(h * d, d), (1, h)) * _block_mask(h, d, d)


ATTN_TILE = 512
ATTN_SCALE = ATTN_HEAD_DIM ** -0.5


def _attn_big(kv):
    m = kv.shape[0]
    kbig = jnp.tile(kv[:, :D_GROUP].T, (1, ATTN_HEADS)) * _block_mask(ATTN_HEADS, ATTN_HEAD_DIM, m)
    vbig = jnp.tile(kv[:, D_GROUP:], (ATTN_HEADS, 1)) * _block_mask(ATTN_HEADS, m, ATTN_HEAD_DIM)
    return kbig, vbig


def _attn_probs(q, kbig_ref, m):
    sc = _dot(q, kbig_ref[...]) * ATTN_SCALE
    ps = []
    for h in range(ATTN_HEADS):
        sh = sc[:, h * m:(h + 1) * m]
        e = jnp.exp(sh - jnp.max(sh, axis=1, keepdims=True))
        ps.append(e / jnp.sum(e, axis=1, keepdims=True))
    return ps


def _attn_fwd(h_in, kbig, vbig, mix, *, name):
    s = h_in.shape[0]
    t = _seq_tile(s, ATTN_TILE)
    m = kbig.shape[1] // ATTN_HEADS

    def body(q_ref, kbig_ref, vbig_ref, _mix_in, o_ref):
        ps = _attn_probs(q_ref[...], kbig_ref, m)
        o_ref[...] = _dot(jnp.concatenate(ps, axis=1), vbig_ref[...]).astype(BF16)

    return pl.pallas_call(
        body, grid=(s // t,),
        in_specs=[pl.BlockSpec((t, D_GROUP), lambda i: (i, COL_Q)), _full_spec(kbig), _full_spec(vbig), _ANY],
        out_specs=pl.BlockSpec((t, D_GROUP), lambda i: (i, MIX_ATTN)),
        out_shape=jax.ShapeDtypeStruct((s, D_MODEL), BF16), input_output_aliases={3: 0},
        compiler_params=_cparams(1), name=name)(h_in, kbig, vbig, mix)


def _attn_bwd(h_in, dmix, kbig, vbig, du_s5, dh_all, *, name):
    s = h_in.shape[0]
    t = _seq_tile(s, ATTN_TILE)
    m = kbig.shape[1] // ATTN_HEADS

    def body(q_ref, do_ref, kbig_ref, vbig_ref, dus5_ref, _dh_in, dpair_ref, dk_ref, dv_ref):
        @pl.when(pl.program_id(0) == 0)
        def _():
            dk_ref[...] = jnp.zeros_like(dk_ref)
            dv_ref[...] = jnp.zeros_like(dv_ref)

        q = q_ref[...]
        dout = do_ref[...]
        ps = _attn_probs(q, kbig_ref, m)
        dp = _dot_nt(dout, vbig_ref[...])
        dss = []
        for h in range(ATTN_HEADS):
            dph = dp[:, h * m:(h + 1) * m]
            dss.append(ps[h] * (dph - jnp.sum(dph * ps[h], axis=1, keepdims=True)))
        ds = (jnp.concatenate(dss, axis=1) * ATTN_SCALE).astype(BF16)
        dv_ref[...] += _dot_tn(jnp.concatenate(ps, axis=1), dout)
        dpair_ref[:, :D_GROUP] = dus5_ref[...]
        dpair_ref[:, D_GROUP:] = _dot_nt(ds, kbig_ref[...])
        dk_ref[...] += _dot_tn(q, ds)

    assert (COL_S5, COL_Q) == (4, 5)
    outs = [jax.ShapeDtypeStruct((s, N_IN_COLS), F32), jax.ShapeDtypeStruct(kbig.shape, F32),
            jax.ShapeDtypeStruct(vbig.shape, F32)]
    return pl.pallas_call(
        body, grid=(s // t,),
        in_specs=[pl.BlockSpec((t, D_GROUP), lambda i: (i, COL_Q)), pl.BlockSpec((t, D_GROUP), lambda i: (i, MIX_ATTN)),
                  _full_spec(kbig), _full_spec(vbig), pl.BlockSpec((t, D_GROUP), lambda i: (i, 0)), _ANY],
        out_specs=[pl.BlockSpec((t, 2 * D_GROUP), lambda i: (i, COL_S5 // 2)), _full_spec(outs[1]), _full_spec(outs[2])],
        out_shape=outs, input_output_aliases={5: 0},
        compiler_params=_cparams(1), name=name)(h_in, dmix, kbig, vbig, du_s5, dh_all)


FFN_TILE = 128
FFN_COL_CHUNK = 256
FFN_ROW_CHUNK = 64


def _ffn_conv(pad_ref, w_ref, b_ref, r0, ch, c0):
    cc = FFN_COL_CHUNK
    acc = jnp.broadcast_to(b_ref[:, c0:c0 + cc], (ch, cc))
    for k in range(FFN_CONV_WIDTH):
        o = SUBLANE - (FFN_CONV_WIDTH - 1) + k + r0
        acc = acc + w_ref[k:k + 1, c0:c0 + cc] * pad_ref[o:o + ch, c0:c0 + cc]
    return acc


def _ffn_gate_fwd(u, fcw, fcb, *, name, rider=None):
    s = u.shape[0]
    t = _seq_tile(s, FFN_TILE)
    ch = min(FFN_ROW_CHUNK, t)
    cc = FFN_COL_CHUNK

    def body(u_ref, w_ref, b_ref, o_ref, uc_ref, upad):
        @pl.when(pl.program_id(0) == 0)
        def _():
            upad[0:SUBLANE, :] = jnp.zeros((SUBLANE, 2 * D_FF), F32)

        upad[SUBLANE:SUBLANE + t, :] = u_ref[...].astype(F32)
        for c0 in range(0, D_FF, cc):
            for r0 in range(0, t, ch):
                val = _ffn_conv(upad, w_ref, b_ref, r0, ch, c0)
                gt = _ffn_conv(upad, w_ref, b_ref, r0, ch, c0 + D_FF)
                val, gt = val.astype(BF16), gt.astype(BF16)
                o_ref[r0:r0 + ch, c0:c0 + cc] = val * _gelu(gt)
                uc_ref[r0:r0 + ch, c0:c0 + cc] = val
                uc_ref[r0:r0 + ch, c0 + D_FF:c0 + D_FF + cc] = gt
        upad[0:SUBLANE, :] = upad[t:t + SUBLANE, :]

    return _call(
        body, grid=(s // t,), ins=[u, fcw, fcb],
        in_specs=[pl.BlockSpec((t, 2 * D_FF), lambda i: (i, 0)), _full_spec(fcw), _full_spec(fcb)],
        out_specs=[pl.BlockSpec((t, D_FF), lambda i: (i, 0)), pl.BlockSpec((t, 2 * D_FF), lambda i: (i, 0))],
        outs=[jax.ShapeDtypeStruct((s, D_FF), BF16), jax.ShapeDtypeStruct((s, 2 * D_FF), BF16)],
        scratch=[pltpu.VMEM((SUBLANE + t, 2 * D_FF), F32)], name=name, rider=rider)


def _ffn_gate_bwd(u, uc, dh, fcw, *, name, rider=None):
    s = u.shape[0]
    t = _seq_tile(s, FFN_TILE)
    nt = s // t
    ch = min(FFN_ROW_CHUNK, t)
    cc = FFN_COL_CHUNK

    def body(u_ref, uc_ref, dh_ref, w_ref, du_ref, dw_ref, db_ref, dpad):
        @pl.when(pl.program_id(0) == 0)
        def _():
            dpad[t:t + SUBLANE, :] = jnp.zeros((SUBLANE, 2 * D_FF), F32)
            dw_ref[...] = jnp.zeros_like(dw_ref)
            db_ref[...] = jnp.zeros_like(db_ref)

        for c0 in range(0, D_FF, cc):
            for r0 in range(0, t, ch):
                val = uc_ref[r0:r0 + ch, c0:c0 + cc]
                gl, dgl = _gelu_and_grad(uc_ref[r0:r0 + ch, c0 + D_FF:c0 + D_FF + cc])
                d = dh_ref[r0:r0 + ch, c0:c0 + cc]
                dpad[r0:r0 + ch, c0:c0 + cc] = (d * gl).astype(F32)
                dpad[r0:r0 + ch, c0 + D_FF:c0 + D_FF + cc] = (d * val * dgl).astype(F32)
        for c0 in range(0, 2 * D_FF, cc):
            dbs = jnp.zeros((1, cc), F32)
            dws = [jnp.zeros((1, cc), F32) for _ in range(FFN_CONV_WIDTH)]
            for r0 in range(0, t, ch):
                x = u_ref[r0:r0 + ch, c0:c0 + cc].astype(F32)
                acc = jnp.zeros((ch, cc), F32)
                for k in range(FFN_CONV_WIDTH):
                    o = (FFN_CONV_WIDTH - 1) - k + r0
                    sh = dpad[o:o + ch, c0:c0 + cc]
                    acc = acc + w_ref[k:k + 1, c0:c0 + cc] * sh
                    dws[k] = dws[k] + _colsum(x * sh)
                    if k == FFN_CONV_WIDTH - 1:
                        dbs = dbs + _colsum(sh)
                du_ref[r0:r0 + ch, c0:c0 + cc] = acc.astype(BF16)
            db_ref[:, c0:c0 + cc] += dbs
            for k in range(FFN_CONV_WIDTH):
                dw_ref[k:k + 1, c0:c0 + cc] += dws[k]
        dpad[t:t + SUBLANE, :] = dpad[0:SUBLANE, :]

    outs = [jax.ShapeDtypeStruct((s, 2 * D_FF), BF16), jax.ShapeDtypeStruct((SUBLANE, 2 * D_FF), F32),
            jax.ShapeDtypeStruct((1, 2 * D_FF), F32)]
    return _call(
        body, grid=(nt,), ins=[u, uc, dh, fcw],
        in_specs=[pl.BlockSpec((t, 2 * D_FF), lambda i: (nt - 1 - i, 0)),
                  pl.BlockSpec((t, 2 * D_FF), lambda i: (nt - 1 - i, 0)),
                  pl.BlockSpec((t, D_FF), lambda i: (nt - 1 - i, 0)), _full_spec(fcw)],
        out_specs=[pl.BlockSpec((t, 2 * D_FF), lambda i: (nt - 1 - i, 0)), _full_spec(outs[1]), _full_spec(outs[2])],
        outs=outs, scratch=[pltpu.VMEM((t + SUBLANE, 2 * D_FF), F32)], name=name, rider=rider)


def _adamw_body(g_ref, w_ref, m_ref, v_ref, go_ref, d_ref, mo_ref, vo_ref):
    inv_b1 = 1.0 - ADAM_B1 ** ADAM_STEP
    inv_b2 = 1.0 - ADAM_B2 ** ADAM_STEP
    g = g_ref[0].astype(F32)
    for dev in range(1, N_DEV):
        g = g + g_ref[dev].astype(F32)
    go_ref[...] = g
    mn = ADAM_B1 * m_ref[...] + (1.0 - ADAM_B1) * g
    vn = ADAM_B2 * v_ref[...] + (1.0 - ADAM_B2) * (g * g)
    mo_ref[...] = mn
    vo_ref[...] = vn
    d_ref[...] = -ADAM_LR * ((mn / inv_b1) / (jnp.sqrt(vn / inv_b2) + ADAM_EPS) + ADAM_WD * w_ref[...])


def _adamw(gstack, w, m, v, *, name):
    _, r, c = gstack.shape
    tr = _pick_rows(r, PACK_ROW_BLOCK)

    def body(*refs):
        _adamw_body(*refs)

    blk = pl.BlockSpec((tr, c), lambda i: (i, 0))
    sh = jax.ShapeDtypeStruct((r, c), F32)
    return pl.pallas_call(
        body, grid=(r // tr,),
        in_specs=[pl.BlockSpec((N_DEV, tr, c), lambda i: (0, i, 0)), blk, blk, blk],
        out_specs=[blk] * 4, out_shape=[sh] * 4,
        compiler_params=_cparams(1), name=name)(gstack, w, m, v)


def _adamw_layer(gstack, w, m, v, layer, into, *, name):
    n_layers, r, c = w.shape
    tr = _pick_rows(r, PACK_ROW_BLOCK)

    def body(g_ref, w_ref, m_ref, v_ref, *rest):
        _adamw_body(g_ref, w_ref, m_ref, v_ref, *rest[-4:])

    blk = pl.BlockSpec((None, tr, c), lambda i: (layer, i, 0))
    sh = jax.ShapeDtypeStruct((n_layers, r, c), F32)
    into = list(into or [])
    return pl.pallas_call(
        body, grid=(r // tr,),
        in_specs=[pl.BlockSpec((N_DEV, tr, c), lambda i: (0, i, 0)), blk, blk, blk] + [_ANY] * len(into),
        out_specs=[blk] * 4, out_shape=[sh] * 4, input_output_aliases={4 + k: k for k in range(len(into))},
        compiler_params=_cparams(1), name=name)(gstack, w, m, v, *into)


def _exchange(rider, *, name):
    n = rider.n

    def body(*refs):
        x_refs, out_refs, sems = refs[:n], refs[n:2 * n], refs[2 * n:]
        rider.start(x_refs, out_refs, sems)
        rider.wait(x_refs, out_refs, sems)

    return pl.pallas_call(
        body, in_specs=[_ANY] * n, out_specs=[_ANY] * n, out_shape=rider.out_shapes(),
        scratch_shapes=rider.scratch(), name=name)(*rider.srcs)


def _pack_rows(n):
    rows = -(-n // PACK_COLS)
    return -(-rows // SUBLANE) * SUBLANE


def _pack(arrs, dtype):
    flat = jnp.concatenate([a.reshape(-1).astype(dtype) for a in arrs])
    rows = _pack_rows(flat.shape[0])
    flat = jnp.pad(flat, (0, rows * PACK_COLS - flat.shape[0]))
    return flat.reshape(rows, PACK_COLS)


def _pack_lead(arrs, dtype):
    flat = jnp.concatenate([a.reshape(N_DEV, -1).astype(dtype) for a in arrs], axis=1)
    rows = _pack_rows(flat.shape[1])
    flat = jnp.pad(flat, ((0, 0), (0, rows * PACK_COLS - flat.shape[1])))
    return flat.reshape(N_DEV, rows, PACK_COLS)


def _pack_layers(arrs, dtype):
    n_layers = arrs[0].shape[0]
    flat = jnp.concatenate([a.reshape(n_layers, -1).astype(dtype) for a in arrs], axis=1)
    rows = _pack_rows(flat.shape[1])
    flat = jnp.pad(flat, ((0, 0), (0, rows * PACK_COLS - flat.shape[1])))
    return flat.reshape(n_layers, rows, PACK_COLS)


def _unpack_layers(packed, shapes):
    flat = packed.reshape(packed.shape[0], -1)
    out, pos = [], 0
    for sh in shapes:
        n = math.prod(sh[1:])
        out.append(flat[:, pos:pos + n].reshape(sh))
        pos += n
    return out


def _unpack(packed, shapes, lead=False):
    flat = packed.reshape(N_DEV, -1) if lead else packed.reshape(-1)
    out, pos = [], 0
    for sh in shapes:
        n = math.prod(sh)
        out.append(flat[:, pos:pos + n].reshape((N_DEV,) + tuple(sh)) if lead else flat[pos:pos + n].reshape(sh))
        pos += n
    return out


def _join_shards(stacked, axis):
    return jnp.concatenate([stacked[d] for d in range(N_DEV)], axis=axis)


def _split_shards(full, axis):
    return jnp.stack(jnp.split(full, N_DEV, axis=axis), axis=0)


def _perm_in_cols(a, inverse=False):
    blocks = jnp.split(a, 6, axis=-1)
    if inverse:
        order = [IN_PERM.index(j) for j in range(6)]
    else:
        order = list(IN_PERM)
    return jnp.concatenate([blocks[j] for j in order], axis=-1)


def _row(v):
    return v.reshape(1, -1)


def _pad_rows(w, rows):
    return jnp.pad(w, ((0, rows - w.shape[0]), (0, 0)))


def _gn_avg_matrix():
    return _block_mask(GN_GROUPS, D_GROUP // GN_GROUPS, D_GROUP // GN_GROUPS) / (D_GROUP // GN_GROUPS)


def _layer_params(p, l):
    q = {}
    s5_mats, q["s5_vjp"] = jax.vjp(_s5_chunk_map, p["s5_lam_re"][l], p["s5_lam_im"][l], p["s5_log_dt"][l],
                                   p["s5_b_re"][l], p["s5_b_im"][l], p["s5_c_re"][l], p["s5_c_im"][l], p["s5_d"][l])
    q["s5_mats"] = [m.astype(BF16) for m in s5_mats[:5]]
    q["s5_a16"] = s5_mats[5]
    (q["wr"], q["wi"]), q["lru_w_vjp"] = jax.vjp(lambda r, i: (_blockdiag(r), _blockdiag(i)), p["lru_w_r"][l], p["lru_w_i"][l])
    q["wr"], q["wi"] = q["wr"].astype(BF16), q["wi"].astype(BF16)
    q["sp"], q["sp_vjp"] = jax.vjp(lambda lam: _row(jax.nn.softplus(-lam)), p["lru_lam"][l])
    return q


ROW_PARTS = ("a", "b", "c", "d")
TWO_LEVEL_RIDES = {(0, "ln_in_fwd")}
WEIGHT_RIDES = {(0, "ln_in_fwd"): [("w_in", 0)],
                (0, "inproj"): [("attn_w_kv", 0), ("w_out", 0), ("small_pack", 0)],
                (0, "cv_fwd"): [("ffn_w_up#a", 0), ("ffn_w_up#b", 0)],
                (0, "lru_fwd"): [("ffn_w_up#c", 0)],
                (0, "outproj"): [("ffn_w_up#d", 0)],
                (0, "ffn_up"): [("ffn_w_down", 0), ("w_in", 1), ("attn_w_kv", 1), ("w_out", 1)],
                (0, "ffn_gate_fwd"): [("ffn_w_up", 1)],
                (0, "ffn_down"): [("ffn_w_down", 1)]}
GRAD_RIDES = {(1, "ffn_gate_bwd"): [("ffn_w_down", 1)],
              (0, "dw_down"): [("w_out", 1), ("attn_w_kv", 1), ("w_in", 1)],
              (0, "dhff"): [("rep", 1), ("ssh", 1)],
              (0, "ffn_gate_bwd"): [("ffn_w_up", 1)],
              (0, "dw_up"): [("ffn_w_down", 0)],
              (0, "dx1"): [("ffn_w_up", 0)],
              (0, "cv_bwd"): [("w_out", 0)],
              (0, "dw_in"): [("attn_w_kv", 0), ("ssh", 0), ("rep", 0)],
              (0, "dxs"): [("w_in", 0)]}


def _join_cols(pieces, *, name):
    n_dev, k, c = pieces[0].shape
    assert (2 * c) % LANE == 0 and all(p.shape == pieces[0].shape for p in pieces)
    n_p = len(pieces)

    def body(*refs):
        o_ref = refs[n_p]
        for i in range(n_p):
            @pl.when(pl.program_id(0) == i)
            def _(i=i):
                o_ref[...] = jnp.concatenate([refs[i][0], refs[i][1]], axis=1)

    return pl.pallas_call(
        body, grid=(n_p, n_dev // 2),
        in_specs=[pl.BlockSpec((2, k, c), lambda i, j, p=p: (jnp.where(i == p, j, 0), 0, 0)) for p in range(n_p)],
        out_specs=pl.BlockSpec((k, 2 * c), lambda i, j: (i, j)),
        out_shape=jax.ShapeDtypeStruct((n_p * k, n_dev * c), pieces[0].dtype),
        compiler_params=_cparams(2), name=name)(*pieces)


def _split_cols(full, *, name):
    k, n = full.shape
    c = n // N_DEV
    assert (2 * c) % LANE == 0

    def body(x_ref, o_ref):
        o_ref[0] = x_ref[:, :c]
        o_ref[1] = x_ref[:, c:]

    return pl.pallas_call(
        body, grid=(N_DEV // 2,), in_specs=[pl.BlockSpec((k, 2 * c), lambda j: (0, j))],
        out_specs=pl.BlockSpec((2, k, c), lambda j: (j, 0, 0)), out_shape=jax.ShapeDtypeStruct((N_DEV, k, c), full.dtype),
        compiler_params=_cparams(1), name=name)(full)


def _assemble_weight(n, pieces, layer=0):
    if SHARDED[n] == 2:
        full = _join_cols(pieces, name=f"l{layer}_join_{n}")
        return _perm_in_cols(full) if n == "w_in" else full
    (gathered,) = pieces
    return gathered.reshape(-1, gathered.shape[-1])


def _grad_source(n, g, layer=0):
    g = g.astype(BF16)
    if SHARDED[n] == 2:
        if n == "w_in":
            g = _perm_in_cols(g, inverse=True)
        return _split_cols(g, name=f"l{layer}_split_d{n}"), "lead"
    return g, "rows"


def _hosted(fn, keys_rider, land, *args, **kw):
    keys, rider = keys_rider
    if rider is None:
        return fn(*args, **kw)
    out, routs = fn(*args, rider=rider, **kw)
    land(keys, routs)
    return out


def _local_step(x, mem, target, p, big_w, shards=None, unpack_small=None):
    dist = shards is not None
    gdt = BF16 if dist else F32
    small, saved = {}, []
    big_g, ready, recv = {}, {}, {}
    mavg = _gn_avg_matrix()
    s5_perm = _s5_perm()

    def weight_rider(l, host):
        keys = WEIGHT_RIDES.get((l, host), []) if dist else []
        if not keys:
            return keys, None
        srcs = [shards[n][ll] for n, ll in keys]
        return keys, (_TwoLevelGather(srcs) if (l, host) in TWO_LEVEL_RIDES else _Rider(srcs, ["all"] * len(keys)))

    halves = {}

    def land_weights(keys, routs):
        for (n, ll), r in zip(keys, routs):
            if n == "small_pack":
                p.update(unpack_small(r))
            elif "#" in n:
                base = n.split("#")[0]
                halves[(n, ll)] = r
                parts = [halves.get((base + "#" + tag, ll)) for tag in ROW_PARTS]
                if all(part is not None for part in parts):
                    big_w[base][ll] = _assemble_weight(base, parts, ll)
            else:
                big_w[n][ll] = _assemble_weight(n, [r], ll)

    def grad_rider(l, host):
        keys = [k for k in GRAD_RIDES.get((l, host), []) if k in ready] if dist else []
        return keys, (_Rider([ready[k][0] for k in keys], [ready[k][1] for k in keys]) if keys else None)

    def land_grads(keys, routs):
        for k, r in zip(keys, routs):
            recv[k] = r
            del ready[k]

    def big_grad(n, l, g):
        if dist:
            ready[(n, l)] = _grad_source(n, g, l)
        else:
            big_g[(n, l)] = g

    xs = _hosted(_ln_fwd, weight_rider(0, "ln_in_fwd"), land_weights, x, _row(p["ln_in_g"]), _row(p["ln_in_b"]),
                 name="ln_in_fwd")
    for l in range(DEPTH):
        q = _layer_params(p, l)
        n = f"l{l}_"
        hin = _hosted(_mm, weight_rider(l, "inproj"), land_weights, xs, big_w["w_in"][l], bias=_row(p["b_in"][l]),
                      name=n + "inproj")
        nb = hin.shape[0] // S5_CHUNK
        s5_pows = _s5_a16_powers(q["s5_a16"], nb.bit_length() - 1)
        s5_u2 = _s5_to_chunks(hin, COL_S5 * (D_GROUP // LANE), s5_perm, name=n + "s5_in")
        s5_y2, s5_x = _s5_core_fwd(s5_u2, *q["s5_mats"], s5_pows, name=n + "s5_core_fwd")
        s5_y1 = _s5_from_chunks(s5_y2, s5_perm, name=n + "s5_out")
        (mix,), _ = _s5_glu_fwd(s5_y1, p["s5_w_glu"][l], _row(p["s5_b_glu"][l]), name=n + "s5_glu_fwd")
        cvw = _pad_rows(p["cv_w"][l], CV_PAD)
        keys, rd = weight_rider(l, "cv_fwd")
        (mix, cv_c), routs = _cv_fwd(hin, cvw, _row(p["cv_b"][l]), _row(p["cv_gn_g"][l]), _row(p["cv_gn_b"][l]), mavg,
                                     p["cv_w_pw"][l], _row(p["cv_b_pw"][l]), mix, name=n + "cv_fwd", rider=rd)
        land_weights(keys, routs)
        lcw = _pad_rows(p["lru_conv_w"][l], SUBLANE)
        keys, rd = weight_rider(l, "lru_fwd")
        (mix, lru_xc, lru_h), routs = _lru_fwd(hin, lcw, _row(p["lru_conv_b"][l]), q["wr"], _row(p["lru_b_r"][l]), q["wi"],
                                               _row(p["lru_b_i"][l]), q["sp"], mix, name=n + "lru_fwd", rider=rd)
        land_weights(keys, routs)
        kv = _mm(mem, big_w["attn_w_kv"][l], name=n + "kv")
        (kbig, vbig), kv_vjp = jax.vjp(_attn_big, kv)
        kbig, vbig = kbig.astype(BF16), vbig.astype(BF16)
        mix = _attn_fwd(hin, kbig, vbig, mix, name=n + "attn_fwd")
        r1, x1 = _hosted(_mm, weight_rider(l, "outproj"), land_weights, mix, big_w["w_out"][l], bias=_row(p["b_out"][l]),
                         res=xs, res_scale=ALPHA, ln=(_row(p["ln1_g"][l]), _row(p["ln1_b"][l])), name=n + "outproj")
        u = _hosted(_mm, weight_rider(l, "ffn_up"), land_weights, x1, big_w["ffn_w_up"][l], out_dtype=BF16,
                    name=n + "ffn_up")
        fcw = _pad_rows(p["ffn_conv_w"][l], SUBLANE)
        fcb = _row(p["ffn_conv_b"][l])
        keys, rd = weight_rider(l, "ffn_gate_fwd")
        (hff, uc), routs = _ffn_gate_fwd(u, fcw, fcb, name=n + "ffn_gate_fwd", rider=rd)
        land_weights(keys, routs)
        if l < DEPTH - 1:
            r2, x2 = _hosted(_mm, weight_rider(l, "ffn_down"), land_weights, hff, big_w["ffn_w_down"][l], res=x1,
                             res_scale=ALPHA, ln=(_row(p["ln2_g"][l]), _row(p["ln2_b"][l])), name=n + "ffn_down")
        else:
            r2 = x2 = None
            dr_top, dg_top, db_top, loss_blk = _mm(
                hff, big_w["ffn_w_down"][l], res=x1, res_scale=ALPHA,
                loss=(_row(p["ln2_g"][l]), _row(p["ln2_b"][l]), target), name=n + "ffn_down")
        saved.append(dict(q=q, xs=xs, hin=hin, s5_y1=s5_y1, s5_u2=s5_u2, s5_x=s5_x, s5_pows=s5_pows, cvw=cvw, cv_c=cv_c, lcw=lcw, lru_xc=lru_xc,
                          lru_h=lru_h, kbig=kbig, vbig=vbig, kv_vjp=kv_vjp, mix=mix, r1=r1, x1=x1, u=u, uc=uc, fcw=fcw,
                          hff=hff, r2=r2))
        xs = x2

    top = DEPTH - 1
    loss = loss_blk[0, 0]
    dx = None

    for l in reversed(range(DEPTH)):
        sv = saved[l]
        q = sv["q"]
        n = f"l{l}_"
        g = {}
        if l == top:
            dr2, g["ln2_g"], g["ln2_b"] = dr_top, dg_top, db_top
        else:
            dr2, g["ln2_g"], g["ln2_b"] = from_above
        big_grad("ffn_w_down", l, _hosted(_mm_tn, grad_rider(l, "dw_down"), land_grads, sv["hff"], dr2, out_dtype=gdt,
                                          name=n + "dw_down"))
        dhff = _hosted(_mm, grad_rider(l, "dhff"), land_grads, dr2, big_w["ffn_w_down"][l], trans_b=True,
                       out_dtype=BF16, name=n + "dhff")
        keys, rd = grad_rider(l, "ffn_gate_bwd")
        (du, dfw, g["ffn_conv_b"]), routs = _ffn_gate_bwd(sv["u"], sv["uc"], dhff, sv["fcw"], name=n + "ffn_gate_bwd",
                                                          rider=rd)
        land_grads(keys, routs)
        g["ffn_conv_w"] = dfw[:FFN_CONV_WIDTH]
        if dist:
            ready[("ffn_w_up", l)] = (_hosted(_mm_tn, grad_rider(l, "dw_up"), land_grads, sv["x1"], du, out_dtype=gdt,
                                              dev_cols=du.shape[1] // N_DEV, name=n + "dw_up"), "lead")
        else:
            big_grad("ffn_w_up", l, _mm_tn(sv["x1"], du, name=n + "dw_up"))
        dr1, g["ln1_g"], g["ln1_b"], g["b_out"] = _hosted(
            _mm, grad_rider(l, "dx1"), land_grads, du, big_w["ffn_w_up"][l], trans_b=True, res=dr2, res_scale=ALPHA,
            ln_bwd=(sv["r1"], _row(p["ln1_g"][l])), name=n + "dx1")
        big_grad("w_out", l, _mm_tn(sv["mix"], dr1, out_dtype=gdt, name=n + "dw_out"))
        dmix = _mm(dr1, big_w["w_out"][l], trans_b=True, name=n + "dmix")

        hin = sv["hin"]
        keys, rd = grad_rider(l, "cv_bwd")
        (dh, g["cv_w_pw"], dcw, g["cv_b_pw"], g["cv_gn_g"], g["cv_gn_b"], g["cv_b"]), routs = _cv_bwd(
            hin, sv["cv_c"], dmix, sv["cvw"], _row(p["cv_gn_g"][l]), _row(p["cv_gn_b"][l]), mavg, p["cv_w_pw"][l],
            name=n + "cv_bwd", rider=rd)
        land_grads(keys, routs)
        g["cv_w"] = dcw[:CONV_WIDTH]
        dh, dwr, dwi, dlcw, g["lru_b_r"], g["lru_b_i"], dsp, g["lru_conv_b"] = _lru_bwd(
            hin, sv["lru_xc"], sv["lru_h"], dmix, sv["lcw"], q["wr"], _row(p["lru_b_r"][l]), q["wi"],
            _row(p["lru_b_i"][l]), q["sp"], dh, name=n + "lru_bwd")
        g["lru_conv_w"] = dlcw[:LRU_CONV_WIDTH]
        g["lru_w_r"], g["lru_w_i"] = q["lru_w_vjp"]((dwr, dwi))
        (g["lru_lam"],) = q["sp_vjp"](dsp)
        dy1, g["s5_w_glu"], g["s5_b_glu"] = _s5_glu_bwd(sv["s5_y1"], dmix, p["s5_w_glu"][l], _row(p["s5_b_glu"][l]),
                                                        name=n + "s5_glu_bwd")
        s5_du2, *s5_dmats = _s5_core_bwd(sv["s5_u2"], _s5_to_chunks(dy1, 0, s5_perm, name=n + "s5_din"), sv["s5_x"],
                                         *q["s5_mats"], sv["s5_pows"], name=n + "s5_core_bwd")
        (g["s5_lam_re"], g["s5_lam_im"], g["s5_log_dt"], g["s5_b_re"], g["s5_b_im"], g["s5_c_re"], g["s5_c_im"],
         g["s5_d"]) = q["s5_vjp"](tuple(s5_dmats))
        dh, dkbig, dvbig = _attn_bwd(hin, dmix, sv["kbig"], sv["vbig"],
                                     _s5_from_chunks(s5_du2, s5_perm, name=n + "s5_dout"), dh, name=n + "attn_bwd")
        (dkv,) = sv["kv_vjp"]((dkbig, dvbig))
        big_grad("attn_w_kv", l, _mm_tn(mem, dkv, out_dtype=gdt, name=n + "dw_kv"))

        if dist:
            ready[("ssh", l)] = (_pack_lead([_split_shards(g[k], SHARDED[k] - 1) for k in SMALL_SHARDED], F32), "lead")
            ready[("rep", l)] = (_pack([g[k] for k in REP_LAYERED], F32), "all")
        gw_in, g["b_in"] = _hosted(_mm_tn, grad_rider(l, "dw_in"), land_grads, sv["xs"], dh, colsum=True, out_dtype=gdt,
                                   name=n + "dw_in")
        big_grad("w_in", l, gw_in)
        if dist:
            small.setdefault("b_in", [None] * DEPTH)[l] = g["b_in"].reshape(-1)
        else:
            for k, v in g.items():
                small.setdefault(k, [None] * DEPTH)[l] = v.reshape(p[k].shape[1:])
        if l > 0:
            dr2_below, dg_below, db_below, _ = _hosted(
                _mm, grad_rider(l, "dxs"), land_grads, dh, big_w["w_in"][l], trans_b=True, res=dr1, res_scale=ALPHA,
                ln_bwd=(saved[l - 1]["r2"], _row(p["ln2_g"][l - 1])), name=n + "dxs")
            from_above = (dr2_below, dg_below, db_below)
        else:
            dx = _hosted(_mm, grad_rider(l, "dxs"), land_grads, dh, big_w["w_in"][l], trans_b=True, res=dr1,
                         res_scale=ALPHA, name=n + "dxs")

    keys, rd = grad_rider(0, "ln_in_bwd")
    if rd is None:
        grad_x, dgi, dbi, _ = _ln_bwd(x, dx, _row(p["ln_in_g"]), name="ln_in_bwd")
    else:
        (grad_x, dgi, dbi, _), routs = _ln_bwd(x, dx, _row(p["ln_in_g"]), name="ln_in_bwd", rider=rd)
        land_grads(keys, routs)
    out = {k: jnp.stack(v, axis=0) for k, v in small.items()}
    out["ln_in_g"], out["ln_in_b"] = dgi.reshape(-1), dbi.reshape(-1)
    return loss, grad_x, out, ((recv, ready) if dist else big_g)


def kernel(x, mem, ln_in_g, ln_in_b, w_in, b_in, s5_lam_re, s5_lam_im, s5_log_dt, s5_b_re, s5_b_im, s5_c_re, s5_c_im, s5_d, s5_w_glu, s5_b_glu, cv_w, cv_b, cv_gn_g, cv_gn_b, cv_w_pw, cv_b_pw, lru_conv_w, lru_conv_b, lru_w_r, lru_b_r, lru_w_i, lru_b_i, lru_lam, attn_w_kv, w_out, b_out, ln1_g, ln1_b, ffn_w_up, ffn_conv_w, ffn_conv_b, ffn_w_down, ln2_g, ln2_b, loss_target, m_ln_in_g, m_ln_in_b, m_w_in, m_b_in, m_s5_lam_re, m_s5_lam_im, m_s5_log_dt, m_s5_b_re, m_s5_b_im, m_s5_c_re, m_s5_c_im, m_s5_d, m_s5_w_glu, m_s5_b_glu, m_cv_w, m_cv_b, m_cv_gn_g, m_cv_gn_b, m_cv_w_pw, m_cv_b_pw, m_lru_conv_w, m_lru_conv_b, m_lru_w_r, m_lru_b_r, m_lru_w_i, m_lru_b_i, m_lru_lam, m_attn_w_kv, m_w_out, m_b_out, m_ln1_g, m_ln1_b, m_ffn_w_up, m_ffn_conv_w, m_ffn_conv_b, m_ffn_w_down, m_ln2_g, m_ln2_b, v_ln_in_g, v_ln_in_b, v_w_in, v_b_in, v_s5_lam_re, v_s5_lam_im, v_s5_log_dt, v_s5_b_re, v_s5_b_im, v_s5_c_re, v_s5_c_im, v_s5_d, v_s5_w_glu, v_s5_b_glu, v_cv_w, v_cv_b, v_cv_gn_g, v_cv_gn_b, v_cv_w_pw, v_cv_b_pw, v_lru_conv_w, v_lru_conv_b, v_lru_w_r, v_lru_b_r, v_lru_w_i, v_lru_b_i, v_lru_lam, v_attn_w_kv, v_w_out, v_b_out, v_ln1_g, v_ln1_b, v_ffn_w_up, v_ffn_conv_w, v_ffn_conv_b, v_ffn_w_down, v_ln2_g, v_ln2_b):
    args = locals()
    w = {n: args[n] for n in WEIGHTS}
    mom = {n: args["m_" + n] for n in WEIGHTS}
    var = {n: args["v_" + n] for n in WEIGHTS}

    shards = {n: w[n].astype(BF16) for n in BIG}
    part_rows = shards["ffn_w_up"].shape[1] // len(ROW_PARTS)
    for i, tag in enumerate(ROW_PARTS):
        shards["ffn_w_up#" + tag] = [shards["ffn_w_up"][0, i * part_rows:(i + 1) * part_rows]]
    shards["small_pack"] = [_pack([w[n] for n in SMALL_SHARDED], F32)]
    small_shapes = [w[n].shape for n in SMALL_SHARDED]

    def unpack_small(gathered):
        out = {n: _join_shards(st, SHARDED[n]) for n, st in zip(SMALL_SHARDED, _unpack(gathered, small_shapes, lead=True))}
        for n in ("s5_w_glu", "cv_w_pw"):
            out[n] = out[n].astype(BF16)
        return out

    big_w = {n: [None] * DEPTH for n in BIG}
    p = {n: w[n] for n in REPLICATED}
    p["b_in"] = _perm_in_cols(p["b_in"])

    loss, grad_x, g_small, (recv, ready) = _local_step(x[0], mem[0], loss_target[0], p, big_w, shards, unpack_small)
    loss = lax.psum(loss, ("x", "y", "c"))

    g_small["b_in"] = _perm_in_cols(g_small["b_in"], inverse=True)
    left = list(ready)
    rider = _Rider([ready[k][0] for k in left] + [_pack([g_small[n] for n in REP_LAST], F32)],
                   [ready[k][1] for k in left] + ["all"])
    got = _exchange(rider, name="exchange_grads")
    for k, r in zip(left, got):
        recv[k] = r

    res = [dict(), dict(), dict(), dict()]
    for n in BIG:
        outs = None
        for l in range(DEPTH):
            outs = _adamw_layer(recv[(n, l)], w[n], mom[n], var[n], l, outs, name=f"adamw_{n}_l{l}")
        for kind in range(4):
            res[kind][n] = outs[kind]
    for names, key, tag in ((SMALL_SHARDED, "ssh", "adamw_small_sharded"), (REP_LAYERED, "rep", "adamw_replicated")):
        gstack = jnp.concatenate([recv[(key, l)] for l in range(DEPTH)], axis=1)
        packs = [_pack_layers([t[n] for n in names], F32) for t in (w, mom, var)]
        rows = packs[0].shape[1]
        outs = _adamw(gstack, *[pk.reshape(DEPTH * rows, PACK_COLS) for pk in packs], name=tag)
        for kind in range(4):
            for n, a in zip(names, _unpack_layers(outs[kind].reshape(DEPTH, rows, PACK_COLS), [w[n].shape for n in names])):
                res[kind][n] = a
    outs = _adamw(got[len(left)], _pack([w[n] for n in REP_LAST], F32), _pack([mom[n] for n in REP_LAST], F32),
                  _pack([var[n] for n in REP_LAST], F32), name="adamw_last")
    for kind in range(4):
        for n, a in zip(REP_LAST, _unpack(outs[kind], [w[n].shape for n in REP_LAST])):
            res[kind][n] = a
    return (loss, grad_x[None], *[res[0][n] for n in WEIGHTS], *[res[1][n] for n in WEIGHTS],
            *[res[2][n] for n in WEIGHTS], *[res[3][n] for n in WEIGHTS])
```
